```python
import jax, jax.numpy as jnp
from jax import lax
import numpy as np

D_MODEL = 1024
BATCH = 8
SEQ = 2048
DEPTH = 2

SB_HEADS = 8
SB_HEAD_DIM = 64
SB_WIDTH = SB_HEADS * SB_HEAD_DIM
SB_BLOCK = 128
SSD_WIDTH = D_MODEL
SSD_HEAD_DIM = 64
SSD_HEADS = SSD_WIDTH // SSD_HEAD_DIM
SSD_GROUPS = 2
SSD_STATE = 64
SSD_CONV = 4
SSD_CHUNK = 128
SSD_CONV_CH = SSD_WIDTH + 2 * SSD_GROUPS * SSD_STATE
RW_HEAD_DIM = 64
RW_WIDTH = D_MODEL // 2
RW_HEADS = RW_WIDTH // RW_HEAD_DIM
RW_DECAY_RANK = 64
RW_ICLR_RANK = 64
N_BRANCH = 3
SB_COLS = 4 * SB_WIDTH
SSD_COLS = SSD_WIDTH + SSD_CONV_CH + SSD_HEADS
RW_COLS = 4 * RW_WIDTH + RW_DECAY_RANK + RW_ICLR_RANK
GATE_COLS = N_BRANCH * D_MODEL
N_IN = SB_COLS + SSD_COLS + RW_COLS + GATE_COLS
RMS_EPS = 1e-6
GN_EPS = 64e-5

kernel_name = "hybrid_sba_ssd_rwkv7_gated"


def _split(x, sizes):
    idx = np.cumsum(sizes)[:-1].tolist()
    return jnp.split(x, idx, axis=-1)


def rms_norm(x, g):
    xf = x.astype(jnp.float32)
    y = xf * lax.rsqrt(jnp.mean(xf * xf, axis=-1, keepdims=True) + RMS_EPS)
    return (y * g.astype(jnp.float32)).astype(x.dtype)


def stick_breaking_attention(q, k, v):
    b, s, h, dh = q.shape
    scale = dh ** -0.5
    outs = []
    for i in range(s // SB_BLOCK):
        q0 = i * SB_BLOCK
        end = q0 + SB_BLOCK
        qb, kb, vb = q[:, q0:end], k[:, :end], v[:, :end]
        z = jnp.einsum('bqhd,bkhd->bhqk', qb, kb).astype(jnp.float32) * scale
        t_idx = q0 + jnp.arange(SB_BLOCK)
        s_idx = jnp.arange(end)
        mask = s_idx[None, :] < t_idx[:, None]
        log_beta = jax.nn.log_sigmoid(z)
        log_keep = jnp.where(mask, jax.nn.log_sigmoid(-z), 0.0)
        after = lax.cumsum(log_keep, axis=3, reverse=True) - log_keep
        att = jnp.where(mask, jnp.exp(log_beta + after), 0.0)
        outs.append(jnp.einsum('bhqk,bkhd->bqhd', att.astype(v.dtype), vb))
    return jnp.concatenate(outs, axis=1).reshape(b, s, h * dh)


def causal_depthwise_conv(x, w, bias):
    kw, ch = w.shape
    y = lax.conv_general_dilated(x, w[:, None, :], window_strides=(1,), padding=[(kw - 1, 0)],
                                 dimension_numbers=('NWC', 'WIO', 'NWC'), feature_group_count=ch)
    return y + bias


def segsum(a):
    t = a.shape[-1]
    rep = jnp.broadcast_to(a[..., None], a.shape + (t,))
    strict = jnp.tril(jnp.ones((t, t), dtype=bool), -1)
    cs = jnp.cumsum(jnp.where(strict, rep, 0), axis=-2)
    return jnp.where(jnp.tril(jnp.ones((t, t), dtype=bool)), cs, -jnp.inf)


def ssd_mixer(xbc_raw, dt_raw, conv_w, conv_b, dt_bias, a_log, d_skip):
    b, s, _ = xbc_raw.shape
    c, l, g = s // SSD_CHUNK, SSD_CHUNK, SSD_GROUPS
    j, p, n = SSD_HEADS // SSD_GROUPS, SSD_HEAD_DIM, SSD_STATE
    xbc = jax.nn.silu(causal_depthwise_conv(xbc_raw, conv_w, conv_b))
    xs, bm, cm = _split(xbc, [SSD_WIDTH, g * n, g * n])
    xs = xs.reshape(b, c, l, g, j, p)
    bm = bm.reshape(b, c, l, g, n)
    cm = cm.reshape(b, c, l, g, n)
    dt = jax.nn.softplus(dt_raw + dt_bias).reshape(b, c, l, g, j)
    a_head = -jnp.exp(a_log).reshape(g, j)
    da = jnp.moveaxis(dt * a_head, 2, -1)
    x_dt = xs * dt[..., None]
    a_cs = jnp.cumsum(da, axis=-1)
    decay_in = jnp.exp(segsum(da))
    cb = jnp.einsum('bclgn,bcsgn->bcgls', cm, bm)
    y_diag = jnp.einsum('bcgls,bcgjls,bcsgjp->bclgjp', cb, decay_in, x_dt)
    decay_states = jnp.exp(a_cs[..., -1:] - a_cs)
    states = jnp.einsum('bclgn,bcgjl,bclgjp->bcgjpn', bm, decay_states, x_dt)
    last = jnp.pad(jnp.moveaxis(a_cs[..., -1], 1, -1), [(0, 0), (0, 0), (0, 0), (1, 0)])
    decay_chunk = jnp.exp(segsum(last))
    states_p = jnp.concatenate([jnp.zeros_like(states[:, :1]), states], axis=1)
    new_states = jnp.einsum('bgjzc,bcgjpn->bzgjpn', decay_chunk, states_p)
    prev_states = new_states[:, :-1]
    y_off = jnp.einsum('bclgn,bcgjpn,bcgjl->bclgjp', cm, prev_states, jnp.exp(a_cs))
    y = y_diag + y_off + xs * d_skip.reshape(g, j)[:, :, None]
    return y.reshape(b, s, SSD_WIDTH)


def _rwkv7_step(state, inp):
    r_t, w_t, k_t, v_t, kk_t, a_t = inp
    sa = jnp.einsum('bhij,bhj->bhi', state, -kk_t)
    state = (state * w_t[:, :, None, :] + sa[..., None] * (kk_t * a_t)[:, :, None, :]
             + v_t[..., None] * k_t[:, :, None, :])
    y_t = jnp.einsum('bhij,bhj->bhi', state, r_t)
    return state, y_t


def rwkv7_mixer(slab, mu, w0, w_up, a0, a_up, k_k, k_a, r_k, ln_g, ln_b):
    b, s, _ = slab.shape
    hh, nn = RW_HEADS, RW_HEAD_DIM
    prev = jnp.pad(slab[:, :-1], [(0, 0), (1, 0), (0, 0)])
    mixed = slab + (prev - slab) * mu
    r, k, v, gate, w_lo, a_lo = _split(mixed, [RW_WIDTH] * 4 + [RW_DECAY_RANK, RW_ICLR_RANK])
    w = -jax.nn.softplus(-(w0 + jnp.tanh(w_lo) @ w_up)) - 0.5
    decay = jnp.exp(-jnp.exp(w.astype(jnp.float32)))
    a = jax.nn.sigmoid(a0 + a_lo @ a_up)
    kk = (k * k_k).reshape(b, s, hh, nn).astype(jnp.float32)
    kk = kk / jnp.maximum(jnp.sqrt(jnp.sum(kk * kk, axis=-1, keepdims=True)), 1e-12)
    k = k * (1 + (a - 1) * k_a)
    heads = lambda t: t.reshape(b, s, hh, nn).astype(jnp.float32)
    r4, k4, v4, a4, w4 = heads(r), heads(k), heads(v), heads(a), heads(decay)
    seq_first = lambda t: jnp.moveaxis(t, 1, 0)
    state0 = jnp.zeros((b, hh, nn, nn), jnp.float32)
    _, ys = lax.scan(_rwkv7_step, state0,
                     (seq_first(r4), seq_first(w4), seq_first(k4), seq_first(v4), seq_first(kk), seq_first(a4)))
    y = jnp.moveaxis(ys, 0, 1)
    mean = jnp.mean(y, axis=-1, keepdims=True)
    var = jnp.mean(jnp.square(y - mean), axis=-1, keepdims=True)
    y = ((y - mean) * lax.rsqrt(var + GN_EPS)).reshape(b, s, RW_WIDTH) * ln_g + ln_b
    bonus = jnp.sum(r4 * k4 * r_k, axis=-1, keepdims=True) * v4
    y = y + bonus.reshape(b, s, RW_WIDTH)
    return y.astype(slab.dtype), gate


def hybrid_layer(x, norm_g, w_in, conv_w, conv_b, dt_bias, a_log, d_skip, ssd_norm_g,
                 rw_mu, rw_w0, rw_w_up, rw_a0, rw_a_up, rw_k_k, rw_k_a, rw_r_k, rw_ln_g, rw_ln_b,
                 w_out_sb, w_out_ssd, w_out_rw, w_o):
    b, s, _ = x.shape
    h = rms_norm(x, norm_g)
    proj = h @ w_in
    sb_cols, ssd_cols, rw_cols, gate_cols = _split(proj, [SB_COLS, SSD_COLS, RW_COLS, GATE_COLS])
    q, k, v, sb_gate = _split(sb_cols, [SB_WIDTH] * 4)
    shp = (b, s, SB_HEADS, SB_HEAD_DIM)
    y_sb = stick_breaking_attention(q.reshape(shp), k.reshape(shp), v.reshape(shp)) * jax.nn.silu(sb_gate)
    z, xbc, dt_raw = _split(ssd_cols, [SSD_WIDTH, SSD_CONV_CH, SSD_HEADS])
    y_ssd = ssd_mixer(xbc, dt_raw, conv_w, conv_b, dt_bias, a_log, d_skip)
    y_ssd = rms_norm(y_ssd * jax.nn.silu(z), ssd_norm_g)
    y_rw, rw_gate = rwkv7_mixer(rw_cols, rw_mu, rw_w0, rw_w_up, rw_a0, rw_a_up, rw_k_k, rw_k_a,
                                rw_r_k, rw_ln_g, rw_ln_b)
    y_rw = y_rw * jax.nn.silu(rw_gate)
    g_sb, g_ssd, g_rw = _split(jax.nn.sigmoid(gate_cols), [D_MODEL] * N_BRANCH)
    merged = g_sb * (y_sb @ w_out_sb) + g_ssd * (y_ssd @ w_out_ssd) + g_rw * (y_rw @ w_out_rw)
    return x + merged @ w_o


def _fwd_setup_inputs(seed: int = 0) -> dict:
    key = jax.random.key(seed)
    ks = jax.random.split(key, 24)
    f32 = jnp.float32
    nrm = lambda k, shp, sc: jax.random.normal(k, shp, f32) * sc
    dt0 = jnp.exp(jax.random.uniform(ks[5], (DEPTH, SSD_HEADS), f32, np.log(1e-3), np.log(1e-1)))
    return {
        "x": nrm(ks[0], (BATCH, SEQ, D_MODEL), 1.0),
        "norm_g": 1.0 + nrm(ks[1], (DEPTH, D_MODEL), 0.02),
        "w_in": nrm(ks[2], (DEPTH, D_MODEL, N_IN), D_MODEL ** -0.5),
        "conv_w": nrm(ks[3], (DEPTH, SSD_CONV, SSD_CONV_CH), SSD_CONV ** -0.5),
        "conv_b": nrm(ks[4], (DEPTH, SSD_CONV_CH), 0.02),
        "dt_bias": dt0 + jnp.log(-jnp.expm1(-dt0)),
        "a_log": jnp.log(jax.random.uniform(ks[6], (DEPTH, SSD_HEADS), f32, 1.0, 16.0)),
        "d_skip": 1.0 + nrm(ks[7], (DEPTH, SSD_HEADS), 0.02),
        "ssd_norm_g": 1.0 + nrm(ks[8], (DEPTH, SSD_WIDTH), 0.02),
        "rw_mu": jax.random.uniform(ks[9], (DEPTH, RW_COLS), f32, 0.0, 1.0),
        "rw_w0": jax.random.uniform(ks[10], (DEPTH, RW_WIDTH), f32, -6.0, -1.0),
        "rw_w_up": nrm(ks[11], (DEPTH, RW_DECAY_RANK, RW_WIDTH), 0.1),
        "rw_a0": nrm(ks[12], (DEPTH, RW_WIDTH), 0.1),
        "rw_a_up": nrm(ks[13], (DEPTH, RW_ICLR_RANK, RW_WIDTH), 0.1),
        "rw_k_k": 0.85 + nrm(ks[14], (DEPTH, RW_WIDTH), 0.02),
        "rw_k_a": 1.0 + nrm(ks[15], (DEPTH, RW_WIDTH), 0.02),
        "rw_r_k": nrm(ks[16], (DEPTH, RW_HEADS, RW_HEAD_DIM), 0.1),
        "rw_ln_g": 1.0 + nrm(ks[17], (DEPTH, RW_WIDTH), 0.02),
        "rw_ln_b": nrm(ks[18], (DEPTH, RW_WIDTH), 0.02),
        "w_out_sb": nrm(ks[19], (DEPTH, SB_WIDTH, D_MODEL), SB_WIDTH ** -0.5),
        "w_out_ssd": nrm(ks[20], (DEPTH, SSD_WIDTH, D_MODEL), SSD_WIDTH ** -0.5),
        "w_out_rw": nrm(ks[21], (DEPTH, RW_WIDTH, D_MODEL), RW_WIDTH ** -0.5),
        "w_o": nrm(ks[22], (DEPTH, D_MODEL, D_MODEL), D_MODEL ** -0.5),
        "final_g": 1.0 + nrm(ks[23], (D_MODEL,), 0.02),
    }


def _fwd_reference(x, norm_g, w_in, conv_w, conv_b, dt_bias, a_log, d_skip, ssd_norm_g,
              rw_mu, rw_w0, rw_w_up, rw_a0, rw_a_up, rw_k_k, rw_k_a, rw_r_k, rw_ln_g, rw_ln_b,
              w_out_sb, w_out_ssd, w_out_rw, w_o, final_g):
    for i in range(DEPTH):
        x = hybrid_layer(x, norm_g[i], w_in[i], conv_w[i], conv_b[i], dt_bias[i], a_log[i], d_skip[i],
                         ssd_norm_g[i], rw_mu[i], rw_w0[i], rw_w_up[i], rw_a0[i], rw_a_up[i], rw_k_k[i],
                         rw_k_a[i], rw_r_k[i], rw_ln_g[i], rw_ln_b[i], w_out_sb[i], w_out_ssd[i],
                         w_out_rw[i], w_o[i])
    return rms_norm(x, final_g)


import jax as _jax
import jax.numpy as _jnp

TWIN_FORMAT = 'train_step'
FWD_PARAMS = ['x', 'norm_g', 'w_in', 'conv_w', 'conv_b', 'dt_bias', 'a_log', 'd_skip', 'ssd_norm_g', 'rw_mu', 'rw_w0', 'rw_w_up', 'rw_a0', 'rw_a_up', 'rw_k_k', 'rw_k_a', 'rw_r_k', 'rw_ln_g', 'rw_ln_b', 'w_out_sb', 'w_out_ssd', 'w_out_rw', 'w_o', 'final_g']
TWIN_WEIGHTS = ['norm_g', 'w_in', 'conv_w', 'conv_b', 'dt_bias', 'a_log', 'd_skip', 'ssd_norm_g', 'rw_mu', 'rw_w0', 'rw_w_up', 'rw_a0', 'rw_a_up', 'rw_k_k', 'rw_k_a', 'rw_r_k', 'rw_ln_g', 'rw_ln_b', 'w_out_sb', 'w_out_ssd', 'w_out_rw', 'w_o', 'final_g']
TWIN_DIFF_INPUT = 'x'
TWIN_INPUTS = ['x', 'norm_g', 'w_in', 'conv_w', 'conv_b', 'dt_bias', 'a_log', 'd_skip', 'ssd_norm_g', 'rw_mu', 'rw_w0', 'rw_w_up', 'rw_a0', 'rw_a_up', 'rw_k_k', 'rw_k_a', 'rw_r_k', 'rw_ln_g', 'rw_ln_b', 'w_out_sb', 'w_out_ssd', 'w_out_rw', 'w_o', 'final_g', 'loss_target', 'm_norm_g', 'm_w_in', 'm_conv_w', 'm_conv_b', 'm_dt_bias', 'm_a_log', 'm_d_skip', 'm_ssd_norm_g', 'm_rw_mu', 'm_rw_w0', 'm_rw_w_up', 'm_rw_a0', 'm_rw_a_up', 'm_rw_k_k', 'm_rw_k_a', 'm_rw_r_k', 'm_rw_ln_g', 'm_rw_ln_b', 'm_w_out_sb', 'm_w_out_ssd', 'm_w_out_rw', 'm_w_o', 'm_final_g', 'v_norm_g', 'v_w_in', 'v_conv_w', 'v_conv_b', 'v_dt_bias', 'v_a_log', 'v_d_skip', 'v_ssd_norm_g', 'v_rw_mu', 'v_rw_w0', 'v_rw_w_up', 'v_rw_a0', 'v_rw_a_up', 'v_rw_k_k', 'v_rw_k_a', 'v_rw_r_k', 'v_rw_ln_g', 'v_rw_ln_b', 'v_w_out_sb', 'v_w_out_ssd', 'v_w_out_rw', 'v_w_o', 'v_final_g']
TWIN_OUTPUTS = ['loss', 'grad_x', 'grad_norm_g', 'grad_w_in', 'grad_conv_w', 'grad_conv_b', 'grad_dt_bias', 'grad_a_log', 'grad_d_skip', 'grad_ssd_norm_g', 'grad_rw_mu', 'grad_rw_w0', 'grad_rw_w_up', 'grad_rw_a0', 'grad_rw_a_up', 'grad_rw_k_k', 'grad_rw_k_a', 'grad_rw_r_k', 'grad_rw_ln_g', 'grad_rw_ln_b', 'grad_w_out_sb', 'grad_w_out_ssd', 'grad_w_out_rw', 'grad_w_o', 'grad_final_g', 'delta_norm_g', 'delta_w_in', 'delta_conv_w', 'delta_conv_b', 'delta_dt_bias', 'delta_a_log', 'delta_d_skip', 'delta_ssd_norm_g', 'delta_rw_mu', 'delta_rw_w0', 'delta_rw_w_up', 'delta_rw_a0', 'delta_rw_a_up', 'delta_rw_k_k', 'delta_rw_k_a', 'delta_rw_r_k', 'delta_rw_ln_g', 'delta_rw_ln_b', 'delta_w_out_sb', 'delta_w_out_ssd', 'delta_w_out_rw', 'delta_w_o', 'delta_final_g', 'new_m_norm_g', 'new_m_w_in', 'new_m_conv_w', 'new_m_conv_b', 'new_m_dt_bias', 'new_m_a_log', 'new_m_d_skip', 'new_m_ssd_norm_g', 'new_m_rw_mu', 'new_m_rw_w0', 'new_m_rw_w_up', 'new_m_rw_a0', 'new_m_rw_a_up', 'new_m_rw_k_k', 'new_m_rw_k_a', 'new_m_rw_r_k', 'new_m_rw_ln_g', 'new_m_rw_ln_b', 'new_m_w_out_sb', 'new_m_w_out_ssd', 'new_m_w_out_rw', 'new_m_w_o', 'new_m_final_g', 'new_v_norm_g', 'new_v_w_in', 'new_v_conv_w', 'new_v_conv_b', 'new_v_dt_bias', 'new_v_a_log', 'new_v_d_skip', 'new_v_ssd_norm_g', 'new_v_rw_mu', 'new_v_rw_w0', 'new_v_rw_w_up', 'new_v_rw_a0', 'new_v_rw_a_up', 'new_v_rw_k_k', 'new_v_rw_k_a', 'new_v_rw_r_k', 'new_v_rw_ln_g', 'new_v_rw_ln_b', 'new_v_w_out_sb', 'new_v_w_out_ssd', 'new_v_w_out_rw', 'new_v_w_o', 'new_v_final_g']
TWIN_LEAF_KINDS = {'loss': 'loss', 'grad_x': 'grad_x', 'grad_norm_g': 'grad_w', 'grad_w_in': 'grad_w', 'grad_conv_w': 'grad_w', 'grad_conv_b': 'grad_w', 'grad_dt_bias': 'grad_w', 'grad_a_log': 'grad_w', 'grad_d_skip': 'grad_w', 'grad_ssd_norm_g': 'grad_w', 'grad_rw_mu': 'grad_w', 'grad_rw_w0': 'grad_w', 'grad_rw_w_up': 'grad_w', 'grad_rw_a0': 'grad_w', 'grad_rw_a_up': 'grad_w', 'grad_rw_k_k': 'grad_w', 'grad_rw_k_a': 'grad_w', 'grad_rw_r_k': 'grad_w', 'grad_rw_ln_g': 'grad_w', 'grad_rw_ln_b': 'grad_w', 'grad_w_out_sb': 'grad_w', 'grad_w_out_ssd': 'grad_w', 'grad_w_out_rw': 'grad_w', 'grad_w_o': 'grad_w', 'grad_final_g': 'grad_w', 'delta_norm_g': 'delta_w', 'delta_w_in': 'delta_w', 'delta_conv_w': 'delta_w', 'delta_conv_b': 'delta_w', 'delta_dt_bias': 'delta_w', 'delta_a_log': 'delta_w', 'delta_d_skip': 'delta_w', 'delta_ssd_norm_g': 'delta_w', 'delta_rw_mu': 'delta_w', 'delta_rw_w0': 'delta_w', 'delta_rw_w_up': 'delta_w', 'delta_rw_a0': 'delta_w', 'delta_rw_a_up': 'delta_w', 'delta_rw_k_k': 'delta_w', 'delta_rw_k_a': 'delta_w', 'delta_rw_r_k': 'delta_w', 'delta_rw_ln_g': 'delta_w', 'delta_rw_ln_b': 'delta_w', 'delta_w_out_sb': 'delta_w', 'delta_w_out_ssd': 'delta_w', 'delta_w_out_rw': 'delta_w', 'delta_w_o': 'delta_w', 'delta_final_g': 'delta_w', 'new_m_norm_g': 'new_m', 'new_m_w_in': 'new_m', 'new_m_conv_w': 'new_m', 'new_m_conv_b': 'new_m', 'new_m_dt_bias': 'new_m', 'new_m_a_log': 'new_m', 'new_m_d_skip': 'new_m', 'new_m_ssd_norm_g': 'new_m', 'new_m_rw_mu': 'new_m', 'new_m_rw_w0': 'new_m', 'new_m_rw_w_up': 'new_m', 'new_m_rw_a0': 'new_m', 'new_m_rw_a_up': 'new_m', 'new_m_rw_k_k': 'new_m', 'new_m_rw_k_a': 'new_m', 'new_m_rw_r_k': 'new_m', 'new_m_rw_ln_g': 'new_m', 'new_m_rw_ln_b': 'new_m', 'new_m_w_out_sb': 'new_m', 'new_m_w_out_ssd': 'new_m', 'new_m_w_out_rw': 'new_m', 'new_m_w_o': 'new_m', 'new_m_final_g': 'new_m', 'new_v_norm_g': 'new_v', 'new_v_w_in': 'new_v', 'new_v_conv_w': 'new_v', 'new_v_conv_b': 'new_v', 'new_v_dt_bias': 'new_v', 'new_v_a_log': 'new_v', 'new_v_d_skip': 'new_v', 'new_v_ssd_norm_g': 'new_v', 'new_v_rw_mu': 'new_v', 'new_v_rw_w0': 'new_v', 'new_v_rw_w_up': 'new_v', 'new_v_rw_a0': 'new_v', 'new_v_rw_a_up': 'new_v', 'new_v_rw_k_k': 'new_v', 'new_v_rw_k_a': 'new_v', 'new_v_rw_r_k': 'new_v', 'new_v_rw_ln_g': 'new_v', 'new_v_rw_ln_b': 'new_v', 'new_v_w_out_sb': 'new_v', 'new_v_w_out_ssd': 'new_v', 'new_v_w_out_rw': 'new_v', 'new_v_w_o': 'new_v', 'new_v_final_g': 'new_v'}


def _forward(args):
    return _fwd_reference(*[args[k] for k in FWD_PARAMS])


def _output_shape():
    out = _jax.eval_shape(lambda: _forward(_fwd_setup_inputs(0)))
    return out.shape, out.dtype

N_MICROBATCH = 1
ADAM_LR = 0.001
ADAM_B1 = 0.9
ADAM_B2 = 0.999
ADAM_EPS = 1e-08
ADAM_WD = 0.01
ADAM_STEP = 10
PER_EXAMPLE_BATCH_AXIS = {'x': 0, 'loss_target': 0}
SHARED_INPUTS = []
_WEIGHT_DTYPES = {'norm_g': _jnp.float32, 'w_in': _jnp.float32, 'conv_w': _jnp.float32, 'conv_b': _jnp.float32, 'dt_bias': _jnp.float32, 'a_log': _jnp.float32, 'd_skip': _jnp.float32, 'ssd_norm_g': _jnp.float32, 'rw_mu': _jnp.float32, 'rw_w0': _jnp.float32, 'rw_w_up': _jnp.float32, 'rw_a0': _jnp.float32, 'rw_a_up': _jnp.float32, 'rw_k_k': _jnp.float32, 'rw_k_a': _jnp.float32, 'rw_r_k': _jnp.float32, 'rw_ln_g': _jnp.float32, 'rw_ln_b': _jnp.float32, 'w_out_sb': _jnp.float32, 'w_out_ssd': _jnp.float32, 'w_out_rw': _jnp.float32, 'w_o': _jnp.float32, 'final_g': _jnp.float32}
MOMENT_SCALE = {'norm_g': 1.173713e-01, 'w_in': 3.857730e-02, 'conv_w': 5.714563e-02, 'conv_b': 6.794331e-02, 'dt_bias': 1.542134e-01, 'a_log': 1.310886e-01, 'd_skip': 3.336808e-01, 'ssd_norm_g': 6.228349e-02, 'rw_mu': 6.990016e-02, 'rw_w0': 1.473532e-02, 'rw_w_up': 2.004145e-03, 'rw_a0': 1.652431e-02, 'rw_a_up': 1.475752e-02, 'rw_k_k': 5.599003e-02, 'rw_k_a': 4.545656e-02, 'rw_r_k': 9.266547e-02, 'rw_ln_g': 4.113567e-02, 'rw_ln_b': 4.092389e-02, 'w_out_sb': 2.302932e-02, 'w_out_ssd': 5.993171e-02, 'w_out_rw': 2.892662e-02, 'w_o': 7.038415e-02, 'final_g': 1.597736e+01}


def _to_microbatches(a, axis):
    t = _jnp.moveaxis(a, axis, 0)
    t = t.reshape((N_MICROBATCH, t.shape[0] // N_MICROBATCH) + t.shape[1:])
    return _jnp.moveaxis(t, 1, axis + 1)


def setup_inputs(seed: int = 0) -> dict:
    inp = _fwd_setup_inputs(seed)
    key = _jax.random.fold_in(_jax.random.key(seed), 7919)
    shape, _ = _output_shape()
    out = dict(inp)
    out["loss_target"] = _jax.random.normal(_jax.random.fold_in(key, 0), shape, _jnp.float32)
    for i, name in enumerate(TWIN_WEIGHTS):
        w = inp[name].astype(_jnp.float32)
        if MOMENT_SCALE is None:
            s = _jnp.sqrt(_jnp.mean(_jnp.square(w)) + 1e-30)
        else:
            s = MOMENT_SCALE[name]
        km, kv = _jax.random.split(_jax.random.fold_in(key, i + 1))
        out[name] = w
        out["m_" + name] = s * _jax.random.normal(km, w.shape, _jnp.float32)
        out["v_" + name] = (s * s) * _jax.random.uniform(kv, w.shape, _jnp.float32, 0.5, 1.5)
    if N_MICROBATCH > 1:
        for name, axis in PER_EXAMPLE_BATCH_AXIS.items():
            out[name] = _to_microbatches(out[name], axis)
    return {'x': out['x'], 'norm_g': out['norm_g'], 'w_in': out['w_in'], 'conv_w': out['conv_w'], 'conv_b': out['conv_b'], 'dt_bias': out['dt_bias'], 'a_log': out['a_log'], 'd_skip': out['d_skip'], 'ssd_norm_g': out['ssd_norm_g'], 'rw_mu': out['rw_mu'], 'rw_w0': out['rw_w0'], 'rw_w_up': out['rw_w_up'], 'rw_a0': out['rw_a0'], 'rw_a_up': out['rw_a_up'], 'rw_k_k': out['rw_k_k'], 'rw_k_a': out['rw_k_a'], 'rw_r_k': out['rw_r_k'], 'rw_ln_g': out['rw_ln_g'], 'rw_ln_b': out['rw_ln_b'], 'w_out_sb': out['w_out_sb'], 'w_out_ssd': out['w_out_ssd'], 'w_out_rw': out['w_out_rw'], 'w_o': out['w_o'], 'final_g': out['final_g'], 'loss_target': out['loss_target'], 'm_norm_g': out['m_norm_g'], 'm_w_in': out['m_w_in'], 'm_conv_w': out['m_conv_w'], 'm_conv_b': out['m_conv_b'], 'm_dt_bias': out['m_dt_bias'], 'm_a_log': out['m_a_log'], 'm_d_skip': out['m_d_skip'], 'm_ssd_norm_g': out['m_ssd_norm_g'], 'm_rw_mu': out['m_rw_mu'], 'm_rw_w0': out['m_rw_w0'], 'm_rw_w_up': out['m_rw_w_up'], 'm_rw_a0': out['m_rw_a0'], 'm_rw_a_up': out['m_rw_a_up'], 'm_rw_k_k': out['m_rw_k_k'], 'm_rw_k_a': out['m_rw_k_a'], 'm_rw_r_k': out['m_rw_r_k'], 'm_rw_ln_g': out['m_rw_ln_g'], 'm_rw_ln_b': out['m_rw_ln_b'], 'm_w_out_sb': out['m_w_out_sb'], 'm_w_out_ssd': out['m_w_out_ssd'], 'm_w_out_rw': out['m_w_out_rw'], 'm_w_o': out['m_w_o'], 'm_final_g': out['m_final_g'], 'v_norm_g': out['v_norm_g'], 'v_w_in': out['v_w_in'], 'v_conv_w': out['v_conv_w'], 'v_conv_b': out['v_conv_b'], 'v_dt_bias': out['v_dt_bias'], 'v_a_log': out['v_a_log'], 'v_d_skip': out['v_d_skip'], 'v_ssd_norm_g': out['v_ssd_norm_g'], 'v_rw_mu': out['v_rw_mu'], 'v_rw_w0': out['v_rw_w0'], 'v_rw_w_up': out['v_rw_w_up'], 'v_rw_a0': out['v_rw_a0'], 'v_rw_a_up': out['v_rw_a_up'], 'v_rw_k_k': out['v_rw_k_k'], 'v_rw_k_a': out['v_rw_k_a'], 'v_rw_r_k': out['v_rw_r_k'], 'v_rw_ln_g': out['v_rw_ln_g'], 'v_rw_ln_b': out['v_rw_ln_b'], 'v_w_out_sb': out['v_w_out_sb'], 'v_w_out_ssd': out['v_w_out_ssd'], 'v_w_out_rw': out['v_w_out_rw'], 'v_w_o': out['v_w_o'], 'v_final_g': out['v_final_g']}


def _loss(weights, diff, rest, loss_target):
    with _jax.named_scope("forward"):
        args = {**rest, TWIN_DIFF_INPUT: diff, **{k: w.astype(_WEIGHT_DTYPES[k]) for k, w in weights.items()}}
        y = _forward(args)
    with _jax.named_scope("loss_head"):
        err = _jnp.square(y.astype(_jnp.float32) - loss_target)
        return 0.5 * _jnp.sum(_jnp.mean(err, axis=-1)) if err.ndim else 0.5 * err


def _adamw(w, g, m, v):
    m = ADAM_B1 * m + (1.0 - ADAM_B1) * g
    v = ADAM_B2 * v + (1.0 - ADAM_B2) * _jnp.square(g)
    m_hat = m / (1.0 - ADAM_B1 ** ADAM_STEP)
    v_hat = v / (1.0 - ADAM_B2 ** ADAM_STEP)
    delta = -ADAM_LR * (m_hat / (_jnp.sqrt(v_hat) + ADAM_EPS) + ADAM_WD * w)
    return delta, m, v


def reference(x, norm_g, w_in, conv_w, conv_b, dt_bias, a_log, d_skip, ssd_norm_g, rw_mu, rw_w0, rw_w_up, rw_a0, rw_a_up, rw_k_k, rw_k_a, rw_r_k, rw_ln_g, rw_ln_b, w_out_sb, w_out_ssd, w_out_rw, w_o, final_g, loss_target, m_norm_g, m_w_in, m_conv_w, m_conv_b, m_dt_bias, m_a_log, m_d_skip, m_ssd_norm_g, m_rw_mu, m_rw_w0, m_rw_w_up, m_rw_a0, m_rw_a_up, m_rw_k_k, m_rw_k_a, m_rw_r_k, m_rw_ln_g, m_rw_ln_b, m_w_out_sb, m_w_out_ssd, m_w_out_rw, m_w_o, m_final_g, v_norm_g, v_w_in, v_conv_w, v_conv_b, v_dt_bias, v_a_log, v_d_skip, v_ssd_norm_g, v_rw_mu, v_rw_w0, v_rw_w_up, v_rw_a0, v_rw_a_up, v_rw_k_k, v_rw_k_a, v_rw_r_k, v_rw_ln_g, v_rw_ln_b, v_w_out_sb, v_w_out_ssd, v_w_out_rw, v_w_o, v_final_g):
    given = dict(x=x, norm_g=norm_g, w_in=w_in, conv_w=conv_w, conv_b=conv_b, dt_bias=dt_bias, a_log=a_log, d_skip=d_skip, ssd_norm_g=ssd_norm_g, rw_mu=rw_mu, rw_w0=rw_w0, rw_w_up=rw_w_up, rw_a0=rw_a0, rw_a_up=rw_a_up, rw_k_k=rw_k_k, rw_k_a=rw_k_a, rw_r_k=rw_r_k, rw_ln_g=rw_ln_g, rw_ln_b=rw_ln_b, w_out_sb=w_out_sb, w_out_ssd=w_out_ssd, w_out_rw=w_out_rw, w_o=w_o, final_g=final_g, loss_target=loss_target, m_norm_g=m_norm_g, m_w_in=m_w_in, m_conv_w=m_conv_w, m_conv_b=m_conv_b, m_dt_bias=m_dt_bias, m_a_log=m_a_log, m_d_skip=m_d_skip, m_ssd_norm_g=m_ssd_norm_g, m_rw_mu=m_rw_mu, m_rw_w0=m_rw_w0, m_rw_w_up=m_rw_w_up, m_rw_a0=m_rw_a0, m_rw_a_up=m_rw_a_up, m_rw_k_k=m_rw_k_k, m_rw_k_a=m_rw_k_a, m_rw_r_k=m_rw_r_k, m_rw_ln_g=m_rw_ln_g, m_rw_ln_b=m_rw_ln_b, m_w_out_sb=m_w_out_sb, m_w_out_ssd=m_w_out_ssd, m_w_out_rw=m_w_out_rw, m_w_o=m_w_o, m_final_g=m_final_g, v_norm_g=v_norm_g, v_w_in=v_w_in, v_conv_w=v_conv_w, v_conv_b=v_conv_b, v_dt_bias=v_dt_bias, v_a_log=v_a_log, v_d_skip=v_d_skip, v_ssd_norm_g=v_ssd_norm_g, v_rw_mu=v_rw_mu, v_rw_w0=v_rw_w0, v_rw_w_up=v_rw_w_up, v_rw_a0=v_rw_a0, v_rw_a_up=v_rw_a_up, v_rw_k_k=v_rw_k_k, v_rw_k_a=v_rw_k_a, v_rw_r_k=v_rw_r_k, v_rw_ln_g=v_rw_ln_g, v_rw_ln_b=v_rw_ln_b, v_w_out_sb=v_w_out_sb, v_w_out_ssd=v_w_out_ssd, v_w_out_rw=v_w_out_rw, v_w_o=v_w_o, v_final_g=v_final_g)
    weights = {n: given[n] for n in TWIN_WEIGHTS}
    shared = {n: given[n] for n in SHARED_INPUTS}
    per_example = {n: given[n] for n in ['x']}
    grad_fn = _jax.value_and_grad(_loss, argnums=(0, 1))

    def one_microbatch(ex, loss_target):
        ex = dict(ex)
        diff = ex.pop(TWIN_DIFF_INPUT)
        return grad_fn(weights, diff, {**shared, **ex}, loss_target)

    if N_MICROBATCH == 1:
        loss, (grad_w, grad_x) = one_microbatch(per_example, given["loss_target"])
    else:
        def body(carry, xs):
            loss_sum, grad_sum = carry
            l_k, (gw_k, gx_k) = one_microbatch(xs[0], xs[1])
            with _jax.named_scope("update"):
                return (loss_sum + l_k, _jax.tree.map(_jnp.add, grad_sum, gw_k)), gx_k

        init = (_jnp.zeros((), _jnp.float32), _jax.tree.map(_jnp.zeros_like, weights))
        (loss, grad_w), grad_x = _jax.lax.scan(body, init, (per_example, given["loss_target"]))
    with _jax.named_scope("update"):
        delta_w, new_m, new_v = {}, {}, {}
        for n in TWIN_WEIGHTS:
            delta_w[n], new_m[n], new_v[n] = _adamw(weights[n], grad_w[n], given["m_" + n], given["v_" + n])
    return (loss, grad_x, *[grad_w[n] for n in TWIN_WEIGHTS], *[delta_w[n] for n in TWIN_WEIGHTS],
            *[new_m[n] for n in TWIN_WEIGHTS], *[new_v[n] for n in TWIN_WEIGHTS])
```

```python
import functools

import jax
import jax.numpy as jnp
from jax import lax
from jax.experimental import pallas as pl
from jax.experimental.pallas import tpu as pltpu

F32 = jnp.float32
BF16 = jnp.bfloat16

D_MODEL = 1024
DEPTH = 2
HEAD = 64
LANES = 128
CHUNK = 128
RMS_EPS = 1e-6
GN_EPS = 64e-5
VMEM_LIMIT = 56 * 1024 * 1024

N_IN = 9616
N_PAD = 9728
C_SB, C_Z, C_GATES, C_RW, C_LO, C_DT, C_XBC = 0, 2048, 3072, 6144, 8192, 8320, 8448
RW_COLS = 2176
XBC_COLS = 1280

ADAM_LR, ADAM_B1, ADAM_B2, ADAM_EPS, ADAM_WD, ADAM_STEP = 0.001, 0.9, 0.999, 1e-08, 0.01, 10


def _params(sem=None):
    return pltpu.CompilerParams(dimension_semantics=sem, vmem_limit_bytes=VMEM_LIMIT)


@jax.custom_vjp
def _sigmoid(x):
    return 1.0 / (1.0 + jnp.exp(-x))


def _sigmoid_fwd(x):
    s = _sigmoid(x)
    return s, s


def _sigmoid_bwd(s, g):
    return (g * s * (1.0 - s),)


_sigmoid.defvjp(_sigmoid_fwd, _sigmoid_bwd)


@jax.custom_vjp
def _silu(x):
    return x * _sigmoid(x)


def _silu_fwd(x):
    s = _sigmoid(x)
    return x * s, (x, s)


def _silu_bwd(res, g):
    x, s = res
    return (g * (s + x * s * (1.0 - s)),)


_silu.defvjp(_silu_fwd, _silu_bwd)


@jax.custom_vjp
def _softplus(x):
    return jnp.maximum(x, 0.0) + jnp.log(1.0 + jnp.exp(-jnp.abs(x)))


def _softplus_fwd(x):
    return _softplus(x), x


def _softplus_bwd(x, g):
    return (g * _sigmoid(x),)


_softplus.defvjp(_softplus_fwd, _softplus_bwd)


def _dot(a, b, dims):
    return lax.dot_general(a.astype(BF16), b.astype(BF16), (dims, ((), ())), preferred_element_type=F32)


def _dot_nn(a, b):
    return _dot(a, b, ((1,), (0,)))


def _dot_nt(a, b):
    return _dot(a, b, ((1,), (1,)))


def _dot_tn(a, b):
    return _dot(a, b, ((0,), (0,)))


@jax.custom_vjp
def _bdot(a, b):
    return _dot_nn(a, b)


def _bdot_fwd(a, b):
    return _dot_nn(a, b), (a, b)


def _bdot_bwd(res, g):
    a, b = res
    return _dot_nt(g, b), _dot_tn(a, g)


_bdot.defvjp(_bdot_fwd, _bdot_bwd)


def _split2(x):
    hi = x.astype(BF16)
    lo = (x - hi.astype(F32)).astype(BF16)
    return hi, lo


_NT = (((1,), (1,)), ((), ()))
_NN = (((1,), (0,)), ((), ()))
_TN = (((0,), (0,)), ((), ()))


def _dot2(x, m, dn=_NN):
    hi, lo = _split2(x)
    return (lax.dot_general(hi, m, dn, preferred_element_type=F32)
            + lax.dot_general(lo, m, dn, preferred_element_type=F32))


def _dot2_tn(x, m):
    return _dot2(x, m, _TN)


def _seg_matrix(n):
    r = lax.broadcasted_iota(jnp.int32, (n, n), 0) // HEAD
    c = lax.broadcasted_iota(jnp.int32, (n, n), 1) // HEAD
    return (r == c).astype(BF16)


@jax.custom_vjp
def _segsum2(x, seg):
    return _dot2(x, seg)


def _segsum2_fwd(x, seg):
    return _dot2(x, seg), seg


def _segsum2_bwd(seg, g):
    return _dot2(g, seg), jnp.zeros_like(seg)


_segsum2.defvjp(_segsum2_fwd, _segsum2_bwd)


def _make_segsum(seg):
    return lambda x: _segsum2(x, seg)


def _shift_down_raw(x, k):
    row = lax.broadcasted_iota(jnp.int32, x.shape, 0)
    return jnp.where(row >= k, pltpu.roll(x, k, 0), 0.0)


def _shift_up_raw(x, k):
    t = x.shape[0]
    row = lax.broadcasted_iota(jnp.int32, x.shape, 0)
    return jnp.where(row < t - k, pltpu.roll(x, t - k, 0), 0.0)


@functools.partial(jax.custom_vjp, nondiff_argnums=(1,))
def _shift_down(x, k):
    return _shift_down_raw(x, k)


def _shift_down_fwd(x, k):
    return _shift_down_raw(x, k), None


def _shift_down_bwd(k, _, g):
    return (_shift_up_raw(g, k),)


_shift_down.defvjp(_shift_down_fwd, _shift_down_bwd)


def _mm(a, b, *, name, ta=False, tb=False, add=None, out_dtype=F32, tm=512, tn=512, tk=None):
    m, k = (a.shape[1], a.shape[0]) if ta else a.shape
    n = b.shape[0] if tb else b.shape[1]
    tm, tn = min(tm, m), min(tn, n)
    tk = k if tk is None else tk
    nk = k // tk
    assert m % tm == 0 and n % tn == 0 and k % tk == 0
    dims = ((0 if ta else 1,), (1 if tb else 0,))

    def body(a_ref, b_ref, *refs):
        o_ref, acc_ref = refs[-2:]
        p = _dot(a_ref[...], b_ref[...], dims)

        def emit(total):
            if add is not None:
                total = total + refs[0][...]
            o_ref[...] = total.astype(o_ref.dtype)

        if nk == 1:
            emit(p)
        else:
            kk = pl.program_id(2)

            @pl.when(kk == 0)
            def _():
                acc_ref[...] = p

            @pl.when(kk > 0)
            def _():
                acc_ref[...] += p

            @pl.when(kk == nk - 1)
            def _():
                emit(acc_ref[...])

    a_spec = pl.BlockSpec((tk, tm), lambda i, j, kk: (kk, i)) if ta else pl.BlockSpec((tm, tk), lambda i, j, kk: (i, kk))
    b_spec = pl.BlockSpec((tn, tk), lambda i, j, kk: (j, kk)) if tb else pl.BlockSpec((tk, tn), lambda i, j, kk: (kk, j))
    o_spec = pl.BlockSpec((tm, tn), lambda i, j, kk: (i, j))
    return pl.pallas_call(
        body, name=name, grid=(m // tm, n // tn, nk),
        in_specs=[a_spec, b_spec] + ([o_spec] if add is not None else []), out_specs=o_spec,
        out_shape=jax.ShapeDtypeStruct((m, n), out_dtype),
        scratch_shapes=[pltpu.VMEM((tm, tn) if nk > 1 else (8, LANES), F32)],
        compiler_params=_params(("parallel", "parallel", "arbitrary")),
    )(a, b, *([add] if add is not None else []))


def _row_specs(rows, bt):
    return [pl.BlockSpec((bt, w), functools.partial(lambda i, c: (i, c), c=c)) for _, w, c in rows]


def _full_spec(p):
    return pl.BlockSpec(p.shape, functools.partial(lambda i, nd: (0,) * nd, nd=p.ndim))


def _rowwise(f, rows, pars, out_widths, *, bt, name, acc_widths=()):
    t = rows[0][0].shape[0]
    nr, npar, no, na = len(rows), len(pars), len(out_widths), len(acc_widths)

    def body(*refs):
        vals = [r[...] for r in refs[:nr + npar]]
        outs = f(*vals)
        for o_ref, o in zip(refs[nr + npar:nr + npar + no], outs[:no]):
            o_ref[...] = o.astype(o_ref.dtype)
        if na:
            first = pl.program_id(0) == 0
            for a_ref, a in zip(refs[nr + npar + no:], outs[no:]):
                @pl.when(first)
                def _():
                    a_ref[...] = jnp.zeros_like(a_ref)
                a_ref[...] += a

    return pl.pallas_call(
        body, name=name, grid=(t // bt,),
        in_specs=_row_specs(rows, bt) + [_full_spec(p) for p in pars],
        out_specs=[pl.BlockSpec((bt, w), lambda i: (i, 0)) for w in out_widths]
        + [pl.BlockSpec((1, w), lambda i: (0, 0)) for w in acc_widths],
        out_shape=[jax.ShapeDtypeStruct((t, w), F32) for w in out_widths]
        + [jax.ShapeDtypeStruct((1, w), F32) for w in acc_widths],
        compiler_params=_params(("arbitrary",)),
    )(*[r[0] for r in rows], *pars)


def _rowwise_bwd(f, rows, pars, douts, *, bt, name, groups=None):
    t = rows[0][0].shape[0]
    nr, npar, nd = len(rows), len(pars), len(douts)
    groups = [[i] for i in range(nr)] if groups is None else groups
    widths = [r[1] for r in rows]

    def body(*refs):
        vals = [r[...] for r in refs[:nr + npar]]
        cts = tuple(r[...] for r in refs[nr + npar:nr + npar + nd])
        _, vjp = jax.vjp(lambda *a: tuple(f(*a)), *vals)
        grads = vjp(cts)
        out_refs = refs[nr + npar + nd:]
        for g_ref, grp in zip(out_refs[:len(groups)], groups):
            off = 0
            for i in grp:
                g_ref[:, off:off + widths[i]] = grads[i]
                off += widths[i]
        first = pl.program_id(0) == 0
        for p_ref, g in zip(out_refs[len(groups):], grads[nr:]):
            @pl.when(first)
            def _():
                p_ref[...] = jnp.zeros_like(p_ref)
            p_ref[...] += g

    gw = [sum(widths[i] for i in grp) for grp in groups]
    return pl.pallas_call(
        body, name=name, grid=(t // bt,),
        in_specs=_row_specs(rows, bt) + [_full_spec(p) for p in pars] + _row_specs(douts, bt),
        out_specs=[pl.BlockSpec((bt, w), lambda i: (i, 0)) for w in gw] + [_full_spec(p) for p in pars],
        out_shape=[jax.ShapeDtypeStruct((t, w), F32) for w in gw] + [jax.ShapeDtypeStruct(p.shape, F32) for p in pars],
        compiler_params=_params(("arbitrary",)),
    )(*[r[0] for r in rows], *pars, *[d[0] for d in douts])


def _colwise(f, x, c0, ncols, pars, *, bc, name):
    t = x.shape[0]

    def body(x_ref, *refs):
        o_ref = refs[-1]
        o_ref[...] = f(x_ref[...], *[r[...] for r in refs[:-1]])

    return pl.pallas_call(
        body, name=name, grid=(ncols // bc,),
        in_specs=[pl.BlockSpec((t, bc), lambda j: (0, j + c0 // bc))]
        + [pl.BlockSpec((p.shape[0], bc), lambda j: (0, j)) for p in pars],
        out_specs=pl.BlockSpec((t, bc), lambda j: (0, j)),
        out_shape=jax.ShapeDtypeStruct((t, ncols), F32),
        compiler_params=_params(("parallel",)),
    )(x, *pars)


def _colwise_bwd(f, x, c0, ncols, pars, dout, *, bc, name):
    t = x.shape[0]
    npar = len(pars)

    def body(x_ref, *refs):
        vals = [x_ref[...]] + [r[...] for r in refs[:npar]]
        _, vjp = jax.vjp(f, *vals)
        grads = vjp(refs[npar][...])
        for g_ref, g in zip(refs[npar + 1:], grads):
            g_ref[...] = g

    return pl.pallas_call(
        body, name=name, grid=(ncols // bc,),
        in_specs=[pl.BlockSpec((t, bc), lambda j: (0, j + c0 // bc))]
        + [pl.BlockSpec((p.shape[0], bc), lambda j: (0, j)) for p in pars]
        + [pl.BlockSpec((t, bc), lambda j: (0, j))],
        out_specs=[pl.BlockSpec((t, bc), lambda j: (0, j))]
        + [pl.BlockSpec((p.shape[0], bc), lambda j: (0, j)) for p in pars],
        out_shape=[jax.ShapeDtypeStruct((t, ncols), F32)] + [jax.ShapeDtypeStruct(p.shape, F32) for p in pars],
        compiler_params=_params(("parallel",)),
    )(x, *pars, dout)


def _f_rms(x, g):
    return (x * lax.rsqrt(jnp.mean(x * x, axis=-1, keepdims=True) + RMS_EPS) * g,)


def _f_sb_gate(y, gate):
    return (y * _silu(gate),)


def _f_ssd_norm(y, z, g):
    u = y * _silu(z)
    return (u * lax.rsqrt(jnp.mean(u * u, axis=-1, keepdims=True) + RMS_EPS) * g,)


def _f_merge(p_sb, p_ssd, p_rw, g_sb, g_ssd, g_rw):
    return (_sigmoid(g_sb) * p_sb + _sigmoid(g_ssd) * p_ssd + _sigmoid(g_rw) * p_rw,)


def _f_rw_pre(k, lo, w0, w_up, a0, a_up, k_k, k_a):
    segsum = _make_segsum(_seg_matrix(k.shape[1]))
    lane = lax.broadcasted_iota(jnp.int32, lo.shape, 1)
    w_lo = jnp.where(lane < HEAD, jnp.tanh(lo), 0.0)
    a_lo = jnp.where(lane >= HEAD, lo, 0.0)
    w = -_softplus(-(w0 + _bdot(w_lo, w_up))) - 0.5
    decay = jnp.exp(-jnp.exp(w))
    a = _sigmoid(a0 + _bdot(a_lo, a_up))
    kk = k * k_k
    kk = kk / jnp.maximum(jnp.sqrt(segsum(kk * kk)), 1e-12)
    return decay, k * (1.0 + (a - 1.0) * k_a), -kk, kk * a


def _f_rw_post(y, r, k2, v, gate, ln_g, ln_b, r_k):
    segsum = _make_segsum(_seg_matrix(y.shape[1]))
    yc = y - segsum(y) * (1.0 / HEAD)
    var = segsum(yc * yc) * (1.0 / HEAD)
    yn = yc * lax.rsqrt(var + GN_EPS) * ln_g + ln_b
    return ((yn + segsum(r * k2 * r_k) * v) * _silu(gate),)


def _f_rw_mix(slab, mu):
    return slab + (_shift_down(slab, 1) - slab) * mu


def _f_conv(x, w0, w1, w2, w3, b):
    acc = x * w3 + b
    for i, w in enumerate((w0, w1, w2)):
        acc = acc + _shift_down(x, 3 - i) * w
    return _silu(acc)


def _log_sigmoid(z):
    return jnp.minimum(z, 0.0) - jnp.log(1.0 + jnp.exp(-jnp.abs(z)))


def _prefix_matrix(kind):
    j = lax.broadcasted_iota(jnp.int32, (CHUNK, 2 * CHUNK), 0)
    s = lax.broadcasted_iota(jnp.int32, (CHUNK, 2 * CHUNK), 1)
    tri = {"gt": j > s, "le": j <= s, "lt": j < s}[kind]
    return (tri | (s >= CHUNK)).astype(BF16)


def _sb_specs(t):
    q = pl.BlockSpec((CHUNK, LANES), lambda j, i: (i, j))
    k = pl.BlockSpec((t, LANES), lambda j, i: (0, 4 + j))
    v = pl.BlockSpec((t, LANES), lambda j, i: (0, 8 + j))
    return q, k, v


def _sb_fwd(proj, *, name):
    t = proj.shape[0]
    scale = HEAD ** -0.5

    def body(q_ref, k_ref, v_ref, y_ref, lt_ref):
        i = pl.program_id(1)
        lane = lax.broadcasted_iota(jnp.int32, (CHUNK, LANES), 1)
        diff = (lax.broadcasted_iota(jnp.int32, (CHUNK, CHUNK), 1)
                - lax.broadcasted_iota(jnp.int32, (CHUNK, CHUNK), 0))
        m_f = _prefix_matrix("gt")
        q = q_ref[...] * scale
        qh = [jnp.where((lane // HEAD) == h, q, 0.0).astype(BF16) for h in (0, 1)]

        def step(it, carry):
            off = pl.multiple_of((i - it) * CHUNK, CHUNK)
            kblk = k_ref[pl.ds(off, CHUNK), :].astype(BF16)
            vblk = v_ref[pl.ds(off, CHUNK), :].astype(BF16)
            mask = diff < it * CHUNK
            new = []
            for h in (0, 1):
                c, acc = carry[2 * h], carry[2 * h + 1]
                z = lax.dot_general(qh[h], kblk, _NT, preferred_element_type=F32)
                lb = _log_sigmoid(z)
                w2 = _dot2(jnp.where(mask, lb - z, 0.0), m_f)
                att = jnp.where(mask, jnp.exp(lb + c + w2[:, :CHUNK]), 0.0)
                acc = acc + lax.dot_general(att.astype(BF16), vblk, _NN, preferred_element_type=F32)
                new += [c + w2[:, CHUNK:], acc]
            return tuple(new)

        zero = jnp.zeros((CHUNK, LANES), F32)
        c_a, acc_a, c_b, acc_b = lax.fori_loop(0, i + 1, step, (zero, zero, zero, zero))
        y_ref[...] = jnp.where(lane < HEAD, acc_a, acc_b)
        lt_ref[0] = c_a
        lt_ref[1] = c_b

    return pl.pallas_call(
        body, name=name, grid=(4, t // CHUNK),
        in_specs=list(_sb_specs(t)),
        out_specs=[pl.BlockSpec((CHUNK, LANES), lambda j, i: (i, j)),
                   pl.BlockSpec((2, CHUNK, LANES), lambda j, i: (j, i, 0))],
        out_shape=[jax.ShapeDtypeStruct((t, 4 * LANES), F32), jax.ShapeDtypeStruct((8, t, LANES), F32)],
        compiler_params=_params(("parallel", "arbitrary")),
    )(proj, proj, proj)


def _sb_bwd(proj, dy, lt, *, name):
    t = proj.shape[0]
    scale = HEAD ** -0.5

    def body(q_ref, k_ref, v_ref, dy_ref, lt_ref, dq_ref, dk_ref, dv_ref):
        i = pl.program_id(1)

        @pl.when(i == 0)
        def _():
            dk_ref[...] = jnp.zeros_like(dk_ref)
            dv_ref[...] = jnp.zeros_like(dv_ref)

        lane = lax.broadcasted_iota(jnp.int32, (CHUNK, LANES), 1)
        diff = (lax.broadcasted_iota(jnp.int32, (CHUNK, CHUNK), 1)
                - lax.broadcasted_iota(jnp.int32, (CHUNK, CHUNK), 0))
        m_le, m_lt = _prefix_matrix("le"), _prefix_matrix("lt")
        q = q_ref[...] * scale
        dy_blk = dy_ref[...]
        qh = [jnp.where((lane // HEAD) == h, q, 0.0).astype(BF16) for h in (0, 1)]
        doh = [jnp.where((lane // HEAD) == h, dy_blk, 0.0).astype(BF16) for h in (0, 1)]
        lth = [lt_ref[0], lt_ref[1]]

        def step(kb, carry):
            off = pl.multiple_of(kb * CHUNK, CHUNK)
            kblk = k_ref[pl.ds(off, CHUNK), :].astype(BF16)
            vblk = v_ref[pl.ds(off, CHUNK), :].astype(BF16)
            mask = diff < (i - kb) * CHUNK
            new = []
            dk_acc = jnp.zeros((CHUNK, LANES), F32)
            dv_acc = jnp.zeros((CHUNK, LANES), F32)
            for h in (0, 1):
                cp, cg, dq = carry[3 * h:3 * h + 3]
                z = lax.dot_general(qh[h], kblk, _NT, preferred_element_type=F32)
                lb = _log_sigmoid(z)
                w2 = _dot2(jnp.where(mask, lb - z, 0.0), m_le)
                att = jnp.where(mask, jnp.exp(lb + lth[h] - cp - w2[:, :CHUNK]), 0.0)
                d_att = lax.dot_general(doh[h], vblk, _NT, preferred_element_type=F32)
                d_e = d_att * att
                g2 = _dot2(d_e, m_lt)
                sig = jnp.exp(lb)
                dz = jnp.where(mask, d_e * (1.0 - sig) - (cg + g2[:, :CHUNK]) * sig, 0.0).astype(BF16)
                dq = dq + lax.dot_general(dz, kblk, _NN, preferred_element_type=F32)
                dk_acc = dk_acc + lax.dot_general(dz, qh[h], _TN, preferred_element_type=F32)
                dv_acc = dv_acc + lax.dot_general(att.astype(BF16), doh[h], _TN, preferred_element_type=F32)
                new += [cp + w2[:, CHUNK:], cg + g2[:, CHUNK:], dq]
            dk_ref[pl.ds(off, CHUNK), :] += dk_acc
            dv_ref[pl.ds(off, CHUNK), :] += dv_acc
            return tuple(new)

        zero = jnp.zeros((CHUNK, LANES), F32)
        out = lax.fori_loop(0, i + 1, step, (zero,) * 6)
        dq_ref[...] = jnp.where(lane < HEAD, out[2], out[5]) * scale

    q_spec, k_spec, v_spec = _sb_specs(t)
    blk = pl.BlockSpec((CHUNK, LANES), lambda j, i: (i, j))
    col = pl.BlockSpec((t, LANES), lambda j, i: (0, j))
    return pl.pallas_call(
        body, name=name, grid=(4, t // CHUNK),
        in_specs=[q_spec, k_spec, v_spec, blk, pl.BlockSpec((2, CHUNK, LANES), lambda j, i: (j, i, 0))],
        out_specs=[blk, col, col],
        out_shape=[jax.ShapeDtypeStruct((t, 4 * LANES), F32)] * 3,
        compiler_params=_params(("parallel", "arbitrary")),
    )(proj, proj, proj, dy, lt)


SSD_HEADS = 16
SSD_PAIRS = 8


def _split3(x):
    a = x.astype(BF16)
    r = x - a.astype(F32)
    b = r.astype(BF16)
    return a, b, (r - b.astype(F32)).astype(BF16)


def _dot3(x, m, dn=_NN):
    return sum(lax.dot_general(p, m, dn, preferred_element_type=F32) for p in _split3(x))


def _mdot3(m, x):
    return sum(lax.dot_general(m, p, _NN, preferred_element_type=F32) for p in _split3(x))


def _ssd_common(dtr, dtb, alog, acsx_s, acst_s):
    lane = lax.broadcasted_iota(jnp.int32, (CHUNK, LANES), 1)
    lane1 = lax.broadcasted_iota(jnp.int32, (1, LANES), 1)
    arow = jnp.where(lane1 < SSD_HEADS, -jnp.exp(alog), 0.0)
    dt = jnp.where(lane < SSD_HEADS, _softplus(dtr + dtb), 0.0)
    da = dt * arow
    r = lax.broadcasted_iota(jnp.int32, (CHUNK, CHUNK), 0)
    c = lax.broadcasted_iota(jnp.int32, (CHUNK, CHUNK), 1)
    tril = (r >= c).astype(BF16)
    triu = (r <= c).astype(BF16)
    acs = _mdot3(tril, da)
    acst_s[...] = _dot3(da, triu, _TN)
    eh = lax.broadcasted_iota(jnp.int32, (LANES, 8 * LANES), 0)
    e = (eh == lax.broadcasted_iota(jnp.int32, (LANES, 8 * LANES), 1) // HEAD).astype(BF16)
    eh2 = lax.broadcasted_iota(jnp.int32, (LANES, 16 * LANES), 0)
    e2 = (eh2 == lax.broadcasted_iota(jnp.int32, (LANES, 16 * LANES), 1) // LANES).astype(BF16)
    acsx_s[...] = _dot3(acs, e)
    return dt, arow, _dot3(dt, e), _dot3(acs, e2), e, tril, triu


def _ssd_fwd(xc, proj, dtb, alog, dsk, *, name):
    t = xc.shape[0]
    nc = t // CHUNK

    def body(x_ref, b_ref, c_ref, dtr_ref, dtb_ref, alog_ref, dsk_ref, y_ref, hin_ref, acsx_s, acst_s, h_s):
        @pl.when(pl.program_id(0) == 0)
        def _():
            h_s[...] = jnp.zeros_like(h_s)

        dt, arow, dt_x, acs_b, e, tril, _ = _ssd_common(dtr_ref[...], dtb_ref[...], alog_ref[...], acsx_s, acst_s)
        dsk_x = _dot3(jnp.broadcast_to(dsk_ref[...], (CHUNK, LANES)), e)
        lane = lax.broadcasted_iota(jnp.int32, (CHUNK, LANES), 1)
        causal = (lax.broadcasted_iota(jnp.int32, (CHUNK, CHUNK), 0)
                  >= lax.broadcasted_iota(jnp.int32, (CHUNK, CHUNK), 1))
        for j in range(SSD_PAIRS):
            g = j // 4
            sl = slice(j * LANES, (j + 1) * LANES)
            if j % 4 == 0:
                bg = jnp.where(lane // HEAD == g, b_ref[...], 0.0)
                cg = jnp.where(lane // HEAD == g, c_ref[...], 0.0)
                cb = _dot_nt(cg, bg)
            x = x_ref[:, sl]
            a = acsx_s[:, sl]
            at = acsx_s[CHUNK - 1:CHUNK, sl]
            xdt = x * dt_x[:, sl]
            hin = h_s[j]
            hin_ref[0, j] = hin
            y = jnp.exp(a) * _dot_nn(cg, hin) + x * dsk_x[:, sl]
            h_s[j] = jnp.exp(at) * hin + _dot_tn(bg, xdt * jnp.exp(at - a))
            yd = []
            for hh in (0, 1):
                h = 2 * j + hh
                dec = jnp.exp(jnp.minimum(acs_b[:, h * LANES:(h + 1) * LANES] - acst_s[pl.ds(h, 1), :], 0.0))
                yd.append(_dot_nn(jnp.where(causal, cb * dec, 0.0), xdt))
            y_ref[:, sl] = y + jnp.where(lane < HEAD, yd[0], yd[1])

    one = pl.BlockSpec((1, LANES), lambda i: (0, 0))
    return pl.pallas_call(
        body, name=name, grid=(nc,),
        in_specs=[pl.BlockSpec((CHUNK, 8 * LANES), lambda i: (i, 0)),
                  pl.BlockSpec((CHUNK, LANES), lambda i: (i, 8)),
                  pl.BlockSpec((CHUNK, LANES), lambda i: (i, 9)),
                  pl.BlockSpec((CHUNK, LANES), lambda i: (i, C_DT // LANES)), one, one, one],
        out_specs=[pl.BlockSpec((CHUNK, 8 * LANES), lambda i: (i, 0)),
                   pl.BlockSpec((1, SSD_PAIRS, LANES, LANES), lambda i: (i, 0, 0, 0))],
        out_shape=[jax.ShapeDtypeStruct((t, 8 * LANES), F32),
                   jax.ShapeDtypeStruct((nc, SSD_PAIRS, LANES, LANES), F32)],
        scratch_shapes=[pltpu.VMEM((CHUNK, 8 * LANES), F32), pltpu.VMEM((LANES, CHUNK), F32),
                        pltpu.VMEM((SSD_PAIRS, LANES, LANES), F32)],
        compiler_params=_params(("arbitrary",)),
    )(xc, xc, xc, proj, dtb, alog, dsk)


def _ssd_bwd(xc, proj, dtb, alog, dsk, hin_all, dy, *, name):
    t = xc.shape[0]
    nc = t // CHUNK

    def body(x_ref, b_ref, c_ref, dtr_ref, dtb_ref, alog_ref, dsk_ref, hin_ref, dy_ref,
             dxc_ref, ddtr_ref, ddtb_ref, dalog_ref, ddsk_ref, acsx_s, acst_s, dh_s, dax_s, ddx_s):
        @pl.when(pl.program_id(0) == 0)
        def _():
            dh_s[...] = jnp.zeros_like(dh_s)
            ddtb_ref[...] = jnp.zeros_like(ddtb_ref)
            dalog_ref[...] = jnp.zeros_like(dalog_ref)
            ddsk_ref[...] = jnp.zeros_like(ddsk_ref)

        dtr = dtr_ref[...]
        dtb = dtb_ref[...]
        dt, arow, dt_x, acs_b, e, tril, triu = _ssd_common(dtr, dtb, alog_ref[...], acsx_s, acst_s)
        dsk_x = _dot3(jnp.broadcast_to(dsk_ref[...], (CHUNK, LANES)), e)
        lane = lax.broadcasted_iota(jnp.int32, (CHUNK, LANES), 1)
        rowi = lax.broadcasted_iota(jnp.int32, (CHUNK, LANES), 0)
        causal = (lax.broadcasted_iota(jnp.int32, (CHUNK, CHUNK), 0)
                  >= lax.broadcasted_iota(jnp.int32, (CHUNK, CHUNK), 1))
        dacs = jnp.zeros((CHUNK, LANES), F32)
        d_b = jnp.zeros((CHUNK, LANES), F32)
        d_c = jnp.zeros((CHUNK, LANES), F32)
        for j in range(SSD_PAIRS):
            g = j // 4
            sl = slice(j * LANES, (j + 1) * LANES)
            if j % 4 == 0:
                bg = jnp.where(lane // HEAD == g, b_ref[...], 0.0)
                cg = jnp.where(lane // HEAD == g, c_ref[...], 0.0)
                cb = _dot_nt(cg, bg)
                dcb = jnp.zeros((CHUNK, CHUNK), F32)
            x = x_ref[:, sl]
            d = dt_x[:, sl]
            a = acsx_s[:, sl]
            at = acsx_s[CHUNK - 1:CHUNK, sl]
            xdt = x * d
            hin = hin_ref[0, j]
            dhout = dh_s[j]
            dyp = dy_ref[:, sl]
            ea, eat, ed = jnp.exp(a), jnp.exp(at), jnp.exp(at - a)
            da_l = dyp * ea * _dot_nn(cg, hin)
            dm = dyp * ea
            d_c = d_c + _dot_nt(dm, hin)
            dh_s[j] = _dot_tn(cg, dm) + eat * dhout
            dat = jnp.sum(dhout * hin * eat, axis=0, keepdims=True)
            d_b = d_b + _dot_nt(xdt * ed, dhout)
            dw = _dot_nn(bg, dhout)
            dxdt = dw * ed
            ded = dw * xdt * ed
            dat = dat + jnp.sum(ded, axis=0, keepdims=True)
            da_l = da_l - ded
            for hh in (0, 1):
                h = 2 * j + hh
                dec = jnp.exp(jnp.minimum(acs_b[:, h * LANES:(h + 1) * LANES] - acst_s[pl.ds(h, 1), :], 0.0))
                gm = jnp.where(causal, cb * dec, 0.0)
                dyh = jnp.where(lane // HEAD == hh, dyp, 0.0)
                dg = _dot_nt(dyh, xdt)
                dxdt = dxdt + _dot_tn(gm, dyh)
                dcb = dcb + jnp.where(causal, dg * dec, 0.0)
                th = dg * gm
                oh = (lane == h).astype(BF16)
                dacs = dacs + _dot2(th, oh) - _dot2_tn(th, oh)
            if j % 4 == 3:
                d_c = d_c + _dot_nn(dcb, bg)
                d_b = d_b + _dot_tn(dcb, cg)
            dxc_ref[:, sl] = dyp * dsk_x[:, sl] + dxdt * d
            ddx_s[:, sl] = dxdt * x
            dax_s[:, sl] = da_l + jnp.where(rowi == CHUNK - 1, dat, 0.0)
            dskp = jnp.sum(dyp * x, axis=0, keepdims=True)
            ddsk_ref[...] += _dot2(jnp.broadcast_to(dskp, (8, LANES)), e[:, sl], _NT)
        dxc_ref[:, 8 * LANES:9 * LANES] = d_b
        dxc_ref[:, 9 * LANES:10 * LANES] = d_c
        dacs = dacs + _dot2(dax_s[...], e, _NT)
        ddt = _dot2(ddx_s[...], e, _NT)
        dda = _mdot3(triu, dacs)
        ddt = ddt + dda * arow
        dalog_ref[...] += jnp.sum(dda * dt, axis=0, keepdims=True) * arow
        ddtr = jnp.where(lane < SSD_HEADS, ddt * _sigmoid(dtr + dtb), 0.0)
        ddtr_ref[...] = ddtr
        ddtb_ref[...] += jnp.sum(ddtr, axis=0, keepdims=True)

    one = pl.BlockSpec((1, LANES), lambda i: (0, 0))
    rev = lambda c: (lambda i: (nc - 1 - i, c))
    return pl.pallas_call(
        body, name=name, grid=(nc,),
        in_specs=[pl.BlockSpec((CHUNK, 8 * LANES), rev(0)), pl.BlockSpec((CHUNK, LANES), rev(8)),
                  pl.BlockSpec((CHUNK, LANES), rev(9)), pl.BlockSpec((CHUNK, LANES), rev(C_DT // LANES)),
                  one, one, one,
                  pl.BlockSpec((1, SSD_PAIRS, LANES, LANES), lambda i: (nc - 1 - i, 0, 0, 0)),
                  pl.BlockSpec((CHUNK, 8 * LANES), rev(0))],
        out_specs=[pl.BlockSpec((CHUNK, XBC_COLS), rev(0)), pl.BlockSpec((CHUNK, LANES), rev(0)), one, one,
                   pl.BlockSpec((8, LANES), lambda i: (0, 0))],
        out_shape=[jax.ShapeDtypeStruct((t, XBC_COLS), F32), jax.ShapeDtypeStruct((t, LANES), F32)]
        + [jax.ShapeDtypeStruct((1, LANES), F32)] * 2 + [jax.ShapeDtypeStruct((8, LANES), F32)],
        scratch_shapes=[pltpu.VMEM((CHUNK, 8 * LANES), F32), pltpu.VMEM((LANES, CHUNK), F32),
                        pltpu.VMEM((SSD_PAIRS, LANES, LANES), F32),
                        pltpu.VMEM((CHUNK, 8 * LANES), F32), pltpu.VMEM((CHUNK, 8 * LANES), F32)],
        compiler_params=_params(("arbitrary",)),
    )(xc, xc, xc, proj, dtb, alog, dsk, hin_all, dy)


RW_PAIRS = 4
RW_BT = 16


def _rw_consts():
    seg = _seg_matrix(LANES)
    ti = (lax.broadcasted_iota(jnp.int32, (HEAD, LANES), 0)
          == lax.broadcasted_iota(jnp.int32, (HEAD, LANES), 1) % HEAD).astype(F32)
    return seg, ti


def _rw_scan_fwd(mixed, w, k, n, b, *, name):
    t = w.shape[0]

    def body(r_ref, v_ref, w_ref, k_ref, n_ref, b_ref, y_ref, st_ref, s_s):
        @pl.when(pl.program_id(0) == 0)
        def _():
            s_s[...] = jnp.zeros_like(s_s)

        seg, ti = _rw_consts()

        def step(tt, state):
            row = pl.ds(tt, 1)
            new = []
            for p in range(RW_PAIRS):
                sl = pl.ds(p * LANES, LANES)
                s = state[p]
                sa = _dot2(s * n_ref[row, sl], seg)
                vb = _dot2(ti * v_ref[row, sl], seg)
                s = s * w_ref[row, sl] + sa * b_ref[row, sl] + vb * k_ref[row, sl]
                yc = _dot2(s * r_ref[row, sl], seg)
                y_ref[row, sl] = jnp.sum(yc * ti, axis=0, keepdims=True)
                st_ref[tt, p] = s
                new.append(s)
            return tuple(new)

        out = tuple(s_s[p] for p in range(RW_PAIRS))
        for tt in range(RW_BT):
            out = step(tt, out)
        for p in range(RW_PAIRS):
            s_s[p] = out[p]

    blk = lambda c: pl.BlockSpec((RW_BT, 4 * LANES), functools.partial(lambda i, c: (i, c), c=c))
    return pl.pallas_call(
        body, name=name, grid=(t // RW_BT,),
        in_specs=[blk(0), blk(2), blk(0), blk(0), blk(0), blk(0)],
        out_specs=[blk(0), pl.BlockSpec((RW_BT, RW_PAIRS, HEAD, LANES), lambda i: (i, 0, 0, 0))],
        out_shape=[jax.ShapeDtypeStruct((t, 4 * LANES), F32),
                   jax.ShapeDtypeStruct((t, RW_PAIRS, HEAD, LANES), F32)],
        scratch_shapes=[pltpu.VMEM((RW_PAIRS, HEAD, LANES), F32)],
        compiler_params=_params(("arbitrary",)),
    )(mixed, mixed, w, k, n, b)


def _rw_scan_bwd(mixed, w, k, n, b, states, dy, dr0, dk0, dv0, *, name):
    t = w.shape[0]
    nb = t // RW_BT

    def body(r_ref, v_ref, w_ref, k_ref, n_ref, b_ref, st_ref, prev_ref, dy_ref, dr0_ref, dk0_ref, dv0_ref,
             dr_ref, dw_ref, dk_ref, dv_ref, dn_ref, db_ref, ds_s):
        @pl.when(pl.program_id(0) == 0)
        def _():
            ds_s[...] = jnp.zeros_like(ds_s)

        seg, ti = _rw_consts()
        has_prev = (pl.program_id(0) < nb - 1).astype(F32)

        def step(it, carry):
            tt = RW_BT - 1 - it
            row = pl.ds(tt, 1)
            prev_t = max(tt - 1, 0)
            new_ds, new_s = [], []
            for p in range(RW_PAIRS):
                sl = pl.ds(p * LANES, LANES)
                ds, s_t = carry[p], carry[RW_PAIRS + p]
                s_p = st_ref[prev_t, p] if tt > 0 else prev_ref[0, p] * has_prev
                rr, ww, kk, vv, nn, bb = (x[row, sl] for x in (r_ref, w_ref, k_ref, v_ref, n_ref, b_ref))
                dyb = _dot2(ti * dy_ref[row, sl], seg)
                vb = _dot2(ti * vv, seg)
                sa = _dot2(s_p * nn, seg)
                ds = ds + dyb * rr
                dr_ref[row, sl] = jnp.sum(s_t * dyb, axis=0, keepdims=True) + dr0_ref[row, sl]
                dw_ref[row, sl] = jnp.sum(ds * s_p, axis=0, keepdims=True)
                db_ref[row, sl] = jnp.sum(ds * sa, axis=0, keepdims=True)
                dk_ref[row, sl] = jnp.sum(ds * vb, axis=0, keepdims=True) + dk0_ref[row, sl]
                dsa = _dot2(ds * bb, seg)
                dv_ref[row, sl] = jnp.sum(_dot2(ds * kk, seg) * ti, axis=0, keepdims=True) + dv0_ref[row, sl]
                dn_ref[row, sl] = jnp.sum(s_p * dsa, axis=0, keepdims=True)
                new_ds.append(ds * ww + dsa * nn)
                new_s.append(s_p)
            return tuple(new_ds) + tuple(new_s)

        init = tuple(ds_s[p] for p in range(RW_PAIRS)) + tuple(st_ref[RW_BT - 1, p] for p in range(RW_PAIRS))
        out = init
        for it in range(RW_BT):
            out = step(it, out)
        for p in range(RW_PAIRS):
            ds_s[p] = out[p]

    blk = lambda c: pl.BlockSpec((RW_BT, 4 * LANES), functools.partial(lambda i, c: (nb - 1 - i, c), c=c))
    st_spec = pl.BlockSpec((RW_BT, RW_PAIRS, HEAD, LANES), lambda i: (nb - 1 - i, 0, 0, 0))
    prev_spec = pl.BlockSpec((1, RW_PAIRS, HEAD, LANES), lambda i: (jnp.maximum((nb - 1 - i) * RW_BT - 1, 0), 0, 0, 0))
    return pl.pallas_call(
        body, name=name, grid=(nb,),
        in_specs=[blk(0), blk(2), blk(0), blk(0), blk(0), blk(0), st_spec, prev_spec, blk(0), blk(0), blk(0), blk(0)],
        out_specs=[blk(0)] * 6,
        out_shape=[jax.ShapeDtypeStruct((t, 4 * LANES), F32)] * 6,
        scratch_shapes=[pltpu.VMEM((RW_PAIRS, HEAD, LANES), F32)],
        compiler_params=_params(("arbitrary",)),
    )(mixed, mixed, w, k, n, b, states, states, dy, dr0, dk0, dv0)


def _f_rms_res(x, g):
    return _f_rms(x, g)[0], x


def _final(x, g, target, *, bt, name):
    t, d = x.shape

    def body(x_ref, g_ref, t_ref, dx_ref, loss_ref, dg_ref):
        tgt = t_ref[...]

        def f(xv, gv):
            err = _f_rms(xv, gv)[0] - tgt
            return 0.5 * jnp.mean(err * err, axis=-1, keepdims=True)

        row_loss, vjp = jax.vjp(f, x_ref[...], g_ref[...])
        dx, dg = vjp(jnp.ones_like(row_loss))
        dx_ref[...] = dx

        @pl.when(pl.program_id(0) == 0)
        def _():
            loss_ref[...] = jnp.zeros_like(loss_ref)
            dg_ref[...] = jnp.zeros_like(dg_ref)

        loss_ref[...] += jnp.broadcast_to(jnp.sum(row_loss, axis=0, keepdims=True), (1, LANES))
        dg_ref[...] += dg

    blk = pl.BlockSpec((bt, d), lambda i: (i, 0))
    return pl.pallas_call(
        body, name=name, grid=(t // bt,),
        in_specs=[blk, pl.BlockSpec((1, d), lambda i: (0, 0)), blk],
        out_specs=[blk, pl.BlockSpec((1, LANES), lambda i: (0, 0)), pl.BlockSpec((1, d), lambda i: (0, 0))],
        out_shape=[jax.ShapeDtypeStruct((t, d), F32), jax.ShapeDtypeStruct((1, LANES), F32),
                   jax.ShapeDtypeStruct((1, d), F32)],
        compiler_params=_params(("arbitrary",)),
    )(x, g, target)


ADAMW_BLOCK_BYTES = 1 << 20


def _adamw(w, g, m, v, *, name):
    shape = w.shape
    c = shape[-1]
    args = [a.reshape(-1, c) for a in (w, g, m, v)]
    r = args[0].shape[0]
    br = r
    if r * c * 4 > ADAMW_BLOCK_BYTES:
        cands = [b for b in range(8, r, 8) if r % b == 0 and b * c * 4 <= ADAMW_BLOCK_BYTES]
        br = max(cands) if cands else r

    def body(w_ref, g_ref, m_ref, v_ref, d_ref, nm_ref, nv_ref):
        gv = g_ref[...]
        m_new = ADAM_B1 * m_ref[...] + (1.0 - ADAM_B1) * gv
        v_new = ADAM_B2 * v_ref[...] + (1.0 - ADAM_B2) * (gv * gv)
        m_hat = m_new / (1.0 - ADAM_B1 ** ADAM_STEP)
        v_hat = v_new / (1.0 - ADAM_B2 ** ADAM_STEP)
        d_ref[...] = -ADAM_LR * (m_hat / (jnp.sqrt(v_hat) + ADAM_EPS) + ADAM_WD * w_ref[...])
        nm_ref[...] = m_new
        nv_ref[...] = v_new

    blk = pl.BlockSpec((br, c), lambda i: (i, 0))
    outs = pl.pallas_call(
        body, name=name, grid=(r // br,), in_specs=[blk] * 4, out_specs=[blk] * 3,
        out_shape=[jax.ShapeDtypeStruct((r, c), F32)] * 3,
        compiler_params=_params(("parallel",)),
    )(*args)
    return tuple(o.reshape(shape) for o in outs)


BT = 256
BC = 128


def _layer_rows(x, proj, s):
    s = {k: s.get(k) for k in ("y_sb_raw", "y_ssd_raw", "mixed", "ys", "k2", "p_sb", "p_ssd", "p_rw")}
    return dict(
        rms=[(x, D_MODEL, 0)],
        sb_gate=[(s["y_sb_raw"], 512, 0), (proj, 512, 3)],
        ssd_norm=[(s["y_ssd_raw"], 1024, 0), (proj, 1024, C_Z // 1024)],
        rw_pre=[(s["mixed"], 512, 1), (s["mixed"], LANES, 16)],
        rw_post=[(s["ys"], 512, 0), (s["mixed"], 512, 0), (s["k2"], 512, 0), (s["mixed"], 512, 2), (s["mixed"], 512, 3)],
        merge=[(s["p_sb"], 1024, 0), (s["p_ssd"], 1024, 0), (s["p_rw"], 1024, 0),
               (proj, 1024, 3), (proj, 1024, 4), (proj, 1024, 5)],
    )


def _layer_fwd(x, p, nm):
    s = {}
    (s["h"],) = _rowwise(_f_rms, [(x, D_MODEL, 0)], [p["norm_g"]], [D_MODEL], bt=BT, name=nm + "rms")
    proj = s["proj"] = _mm(s["h"], p["w_in"], name=nm + "proj")
    s["y_sb_raw"], s["lt"] = _sb_fwd(proj, name=nm + "sb")
    s["xc"] = _colwise(_f_conv, proj, C_XBC, XBC_COLS, p["conv"], bc=BC, name=nm + "conv")
    s["y_ssd_raw"], s["hin"] = _ssd_fwd(s["xc"], proj, p["dt_bias"], p["a_log"], p["d_skip"], name=nm + "ssd")
    s["mixed"] = _colwise(_f_rw_mix, proj, C_RW, RW_COLS, [p["rw_mu"]], bc=BC, name=nm + "mix")
    s["w"], s["k2"], s["n"], s["b"] = _rowwise(_f_rw_pre, [(s["mixed"], 512, 1), (s["mixed"], LANES, 16)], p["rw_pre"],
                                               [512] * 4, bt=BT, name=nm + "rwpre")
    s["ys"], s["st"] = _rw_scan_fwd(s["mixed"], s["w"], s["k2"], s["n"], s["b"], name=nm + "scan")
    rows = _layer_rows(x, proj, s)
    (s["y_sb"],) = _rowwise(_f_sb_gate, rows["sb_gate"], [], [512], bt=BT, name=nm + "sbgate")
    (s["y_ssd"],) = _rowwise(_f_ssd_norm, rows["ssd_norm"], [p["ssd_norm_g"]], [1024], bt=BT, name=nm + "ssdnorm")
    (s["y_rw"],) = _rowwise(_f_rw_post, rows["rw_post"], p["rw_post"], [512], bt=BT, name=nm + "rwpost")
    s["p_sb"] = _mm(s["y_sb"], p["w_out_sb"], name=nm + "osb")
    s["p_ssd"] = _mm(s["y_ssd"], p["w_out_ssd"], name=nm + "ossd")
    s["p_rw"] = _mm(s["y_rw"], p["w_out_rw"], name=nm + "orw")
    (s["merged"],) = _rowwise(_f_merge, _layer_rows(x, proj, s)["merge"], [], [1024], bt=BT, name=nm + "merge")
    return _mm(s["merged"], p["w_o"], add=x, name=nm + "wo"), s


def _layer_bwd(x, dx_out, p, s, nm):
    g = {}
    proj = s["proj"]
    rows = _layer_rows(x, proj, s)
    g["w_o"] = _mm(s["merged"], dx_out, ta=True, name=nm + "g_wo")
    d_merged = _mm(dx_out, p["w_o"], tb=True, name=nm + "d_merged")
    dp_sb, dp_ssd, dp_rw, d_gates = _rowwise_bwd(_f_merge, rows["merge"], [], [(d_merged, 1024, 0)], bt=BT,
                                                 name=nm + "merge_b", groups=[[0], [1], [2], [3, 4, 5]])
    g["w_out_sb"] = _mm(s["y_sb"], dp_sb, ta=True, name=nm + "g_osb")
    g["w_out_ssd"] = _mm(s["y_ssd"], dp_ssd, ta=True, name=nm + "g_ossd")
    g["w_out_rw"] = _mm(s["y_rw"], dp_rw, ta=True, name=nm + "g_orw")
    dy_sb = _mm(dp_sb, p["w_out_sb"], tb=True, name=nm + "d_ysb")
    dy_ssd = _mm(dp_ssd, p["w_out_ssd"], tb=True, name=nm + "d_yssd")
    dy_rw = _mm(dp_rw, p["w_out_rw"], tb=True, name=nm + "d_yrw")
    dy_sb_raw, d_sbgate = _rowwise_bwd(_f_sb_gate, rows["sb_gate"], [], [(dy_sb, 512, 0)], bt=BT, name=nm + "sbgate_b")
    dq, dk, dv = _sb_bwd(proj, dy_sb_raw, s["lt"], name=nm + "sb_b")
    dy_ssd_raw, dz, g["ssd_norm_g"] = _rowwise_bwd(_f_ssd_norm, rows["ssd_norm"], [p["ssd_norm_g"]],
                                                   [(dy_ssd, 1024, 0)], bt=BT, name=nm + "ssdnorm_b")
    dxc, ddtr, g["dt_bias"], g["a_log"], g["d_skip"] = _ssd_bwd(
        s["xc"], proj, p["dt_bias"], p["a_log"], p["d_skip"], s["hin"], dy_ssd_raw, name=nm + "ssd_b")
    conv_out = _colwise_bwd(_f_conv, proj, C_XBC, XBC_COLS, p["conv"], dxc, bc=BC, name=nm + "conv_b")
    dxbc, g["conv"] = conv_out[0], conv_out[1:]
    dys, dr0, dk0, dv0, d_rwgate, g["rw_ln_g"], g["rw_ln_b"], g["rw_r_k"] = _rowwise_bwd(
        _f_rw_post, rows["rw_post"], p["rw_post"], [(dy_rw, 512, 0)], bt=BT, name=nm + "rwpost_b")
    dr, dw, dk2, dvv, dn, db = _rw_scan_bwd(s["mixed"], s["w"], s["k2"], s["n"], s["b"], s["st"], dys, dr0, dk0, dv0,
                                            name=nm + "scan_b")
    pre_out = _rowwise_bwd(_f_rw_pre, rows["rw_pre"], p["rw_pre"],
                           [(dw, 512, 0), (dk2, 512, 0), (dn, 512, 0), (db, 512, 0)], bt=BT, name=nm + "rwpre_b")
    dkm, dlo, g["rw_pre"] = pre_out[0], pre_out[1], pre_out[2:]
    d_mixed = jnp.concatenate([dr, dkm, dvv, d_rwgate, dlo], axis=1)
    d_slab, g["rw_mu"] = _colwise_bwd(_f_rw_mix, proj, C_RW, RW_COLS, [p["rw_mu"]], d_mixed, bc=BC, name=nm + "mix_b")
    d_proj = jnp.concatenate([dq, dk, dv, d_sbgate, dz, d_gates, d_slab, ddtr, dxbc], axis=1)
    g["w_in"] = _mm(s["h"], d_proj, ta=True, name=nm + "g_win")
    dh = _mm(d_proj, p["w_in"], tb=True, tk=512, name=nm + "d_h")
    dx, g["norm_g"] = _rowwise_bwd(_f_rms_res, rows["rms"], [p["norm_g"]], [(dh, D_MODEL, 0), (dx_out, D_MODEL, 0)],
                                   bt=BT, name=nm + "rms_b")
    return dx, g


MESH = pl.DeviceIdType.MESH
N_DEV = 8
_ANY = pl.BlockSpec(memory_space=pl.ANY)
_CHIP_SEMS = [pltpu.SemaphoreType.DMA((3,)), pltpu.SemaphoreType.DMA((3,)), pltpu.SemaphoreType.DMA]


def _here():
    x, y, c = lax.axis_index("x"), lax.axis_index("y"), lax.axis_index("c")
    return x, y, c, [(1 - x, y), (x, 1 - y), (1 - x, 1 - y)]


def _chip_exchange(src, *, per_dest, name):
    shape = src.shape[-2:]

    def body(src_ref, out_ref, send_sems, recv_sems, local_sem):
        x, y, c, chips = _here()
        me = 2 * x + y
        pick = (lambda q: src_ref.at[q]) if per_dest else (lambda q: src_ref)
        own = pltpu.make_async_copy(pick(me), out_ref.at[me], local_sem)
        own.start()
        sends = [pltpu.make_async_remote_copy(pick(2 * px + py), out_ref.at[me], send_sems.at[j], recv_sems.at[j],
                                              device_id=(px, py, c), device_id_type=MESH)
                 for j, (px, py) in enumerate(chips)]
        for cp in sends:
            cp.start()
        for j, (px, py) in enumerate(chips):
            pltpu.make_async_remote_copy(pick(me), out_ref.at[2 * px + py], send_sems.at[j], recv_sems.at[j],
                                         device_id=(px, py, c), device_id_type=MESH).wait_recv()
        for cp in sends:
            cp.wait_send()
        own.wait()

    return pl.pallas_call(
        body, name=name, in_specs=[_ANY], out_specs=_ANY,
        out_shape=jax.ShapeDtypeStruct((4,) + shape, src.dtype), scratch_shapes=_CHIP_SEMS,
    )(src)


def _sibling_send_other_half(src, *, name):
    def body(src_ref, out_ref, send_sem, recv_sem):
        x, y, c, _ = _here()
        cp = pltpu.make_async_remote_copy(src_ref.at[1 - c], out_ref, send_sem, recv_sem,
                                          device_id=(x, y, 1 - c), device_id_type=MESH)
        cp.start()
        cp.wait()

    return pl.pallas_call(
        body, name=name, in_specs=[_ANY], out_specs=_ANY,
        out_shape=jax.ShapeDtypeStruct(src.shape[1:], src.dtype),
        scratch_shapes=[pltpu.SemaphoreType.DMA, pltpu.SemaphoreType.DMA],
    )(src)


def _sibling_join(half, *, name):
    def body(src_ref, out_ref, send_sem, recv_sem, local_sem):
        x, y, c, _ = _here()
        own = pltpu.make_async_copy(src_ref, out_ref.at[c], local_sem)
        own.start()
        cp = pltpu.make_async_remote_copy(src_ref, out_ref.at[c], send_sem, recv_sem,
                                          device_id=(x, y, 1 - c), device_id_type=MESH)
        cp.start()
        pltpu.make_async_remote_copy(src_ref, out_ref.at[1 - c], send_sem, recv_sem,
                                     device_id=(x, y, 1 - c), device_id_type=MESH).wait_recv()
        cp.wait_send()
        own.wait()

    return pl.pallas_call(
        body, name=name, in_specs=[_ANY], out_specs=_ANY,
        out_shape=jax.ShapeDtypeStruct((2,) + half.shape, half.dtype),
        scratch_shapes=[pltpu.SemaphoreType.DMA, pltpu.SemaphoreType.DMA, pltpu.SemaphoreType.DMA],
    )(half)


def _allgather_small(v, *, reduce, name):
    r = v.shape[0]

    def body(v_ref, out_ref, *rest):
        send_sems, recv_sems, local_sem = rest[-3:]
        x, y, c, chips = _here()
        me, sibling = (x, y, c), (x, y, 1 - c)

        def slot(px, py, pc):
            return out_ref.at[4 * px + 2 * py + pc]

        def copy(k, block, to, src=None):
            return pltpu.make_async_remote_copy(
                src_ref=slot(*block) if src is None else src, dst_ref=slot(*block),
                send_sem=send_sems.at[k], recv_sem=recv_sems.at[k], device_id=to, device_id_type=MESH)

        mine = pltpu.make_async_copy(v_ref, slot(*me), local_sem)
        mine.start()
        first = [copy(0, me, sibling, src=v_ref)]
        first += [copy(1 + j, me, (*chip, c), src=v_ref) for j, chip in enumerate(chips)]
        for cp in first:
            cp.start()
        passed = [copy(4 + j, (*chip, c), sibling) for j, chip in enumerate(chips)]
        for j, chip in enumerate(chips):
            copy(1 + j, (*chip, c), me).wait_recv()
            passed[j].start()
        copy(0, sibling, me).wait_recv()
        for j, chip in enumerate(chips):
            copy(4 + j, (*chip, 1 - c), me).wait_recv()
        for cp in first + passed:
            cp.wait_send()
        mine.wait()
        if reduce:
            total = out_ref[0]
            for d in range(1, N_DEV):
                total = total + out_ref[d]
            rest[0][...] = total

    vm = pl.BlockSpec(memory_space=pltpu.VMEM)
    out_shape = [jax.ShapeDtypeStruct((N_DEV, r, LANES), F32)] + ([jax.ShapeDtypeStruct((r, LANES), F32)] if reduce else [])
    return pl.pallas_call(
        body, name=name, in_specs=[vm], out_specs=[vm] * len(out_shape), out_shape=out_shape,
        scratch_shapes=[pltpu.SemaphoreType.DMA((7,)), pltpu.SemaphoreType.DMA((7,)), pltpu.SemaphoreType.DMA],
        compiler_params=pltpu.CompilerParams(vmem_limit_bytes=VMEM_LIMIT),
    )(v)


REDUCE_ROWS = 1952


def _add_halves(mine2, other, c_idx, *, name):
    _, nq, r, _ = mine2.shape

    def body(c_ref, a_ref, b_ref, o_ref):
        o_ref[...] = (a_ref[0] + b_ref[...]).astype(o_ref.dtype)

    blk = pl.BlockSpec((1, REDUCE_ROWS, LANES), lambda q, i, c_ref: (q, i, 0))
    return pl.pallas_call(
        body, name=name,
        grid_spec=pltpu.PrefetchScalarGridSpec(
            num_scalar_prefetch=1, grid=(nq, r // REDUCE_ROWS),
            in_specs=[pl.BlockSpec((1, 1, REDUCE_ROWS, LANES), lambda q, i, c_ref: (c_ref[0], q, i, 0)), blk],
            out_specs=blk),
        out_shape=jax.ShapeDtypeStruct((nq, r, LANES), BF16),
        compiler_params=_params(("parallel", "parallel")),
    )(c_idx, mine2, other)


def _sum_chips(parts, *, name):
    _, r, _ = parts.shape

    def body(p_ref, o_ref):
        total = p_ref[0].astype(F32)
        for q in range(1, 4):
            total = total + p_ref[q].astype(F32)
        o_ref[...] = total

    return pl.pallas_call(
        body, name=name, grid=(r // REDUCE_ROWS,),
        in_specs=[pl.BlockSpec((4, REDUCE_ROWS, LANES), lambda i: (0, i, 0))],
        out_specs=pl.BlockSpec((REDUCE_ROWS, LANES), lambda i: (i, 0)),
        out_shape=jax.ShapeDtypeStruct((r, LANES), F32),
        compiler_params=_params(("parallel",)),
    )(parts)


BIG = ("w_in", "w_out_sb", "w_out_ssd", "w_out_rw", "w_o")
BIG_AXIS = {"w_in": 2, "w_out_sb": 2, "w_out_ssd": 1, "w_out_rw": 2, "w_o": 1}
SMALL_SHARDED = {"conv_w": 320, "rw_w_up": 128, "rw_a_up": 128}
SMALL = ("norm_g", "conv_w", "conv_b", "dt_bias", "a_log", "d_skip", "ssd_norm_g", "rw_mu", "rw_w0", "rw_w_up",
         "rw_a0", "rw_a_up", "rw_k_k", "rw_k_a", "rw_r_k", "rw_ln_g", "rw_ln_b", "final_g")


def _rows_of(a):
    flat = a.reshape(-1)
    pad = (-flat.shape[0]) % LANES
    return jnp.pad(flat, (0, pad)).reshape(-1, LANES)


def _pack_rows(arrays, multiple=8):
    rows = jnp.concatenate([_rows_of(a) for a in arrays], axis=0)
    pad = (-rows.shape[0]) % multiple
    return jnp.pad(rows, ((0, pad), (0, 0)))


def _unpack_rows(rows, shapes):
    out, off = [], 0
    for shp in shapes:
        n = 1
        for d in shp:
            n *= d
        nr = -(-n // LANES)
        out.append(rows[off:off + nr].reshape(-1)[:n].reshape(shp))
        off += nr
    return out


def _pad_cols(w):
    z = jnp.zeros(w.shape[:-1] + (N_PAD - N_IN,), w.dtype)
    return jnp.concatenate([w[..., 0:3072], w[..., 6544:9616], w[..., 4368:6544], w[..., 4352:4368], z,
                            w[..., 3072:4352]], axis=-1)


def _unpad_cols(g):
    return jnp.concatenate([g[..., 0:3072], g[..., 8448:9728], g[..., 8320:8336], g[..., 6144:8320],
                            g[..., 3072:6144]], axis=-1)


def _split_chips(a, axis):
    n = a.shape[axis] // 4
    return jnp.stack([lax.slice_in_dim(a, q * n, (q + 1) * n, axis=axis) for q in range(4)])


def _join_chips(a, axis):
    return jnp.concatenate([a[q] for q in range(4)], axis=axis)


def kernel(x, norm_g, w_in, conv_w, conv_b, dt_bias, a_log, d_skip, ssd_norm_g, rw_mu, rw_w0, rw_w_up, rw_a0, rw_a_up, rw_k_k, rw_k_a, rw_r_k, rw_ln_g, rw_ln_b, w_out_sb, w_out_ssd, w_out_rw, w_o, final_g, loss_target, m_norm_g, m_w_in, m_conv_w, m_conv_b, m_dt_bias, m_a_log, m_d_skip, m_ssd_norm_g, m_rw_mu, m_rw_w0, m_rw_w_up, m_rw_a0, m_rw_a_up, m_rw_k_k, m_rw_k_a, m_rw_r_k, m_rw_ln_g, m_rw_ln_b, m_w_out_sb, m_w_out_ssd, m_w_out_rw, m_w_o, m_final_g, v_norm_g, v_w_in, v_conv_w, v_conv_b, v_dt_bias, v_a_log, v_d_skip, v_ssd_norm_g, v_rw_mu, v_rw_w0, v_rw_w_up, v_rw_a0, v_rw_a_up, v_rw_k_k, v_rw_k_a, v_rw_r_k, v_rw_ln_g, v_rw_ln_b, v_w_out_sb, v_w_out_ssd, v_w_out_rw, v_w_o, v_final_g):
    names = ("norm_g", "w_in", "conv_w", "conv_b", "dt_bias", "a_log", "d_skip", "ssd_norm_g", "rw_mu", "rw_w0",
             "rw_w_up", "rw_a0", "rw_a_up", "rw_k_k", "rw_k_a", "rw_r_k", "rw_ln_g", "rw_ln_b", "w_out_sb",
             "w_out_ssd", "w_out_rw", "w_o", "final_g")
    w_loc = dict(zip(names, (norm_g, w_in, conv_w, conv_b, dt_bias, a_log, d_skip, ssd_norm_g, rw_mu, rw_w0, rw_w_up,
                             rw_a0, rw_a_up, rw_k_k, rw_k_a, rw_r_k, rw_ln_g, rw_ln_b, w_out_sb, w_out_ssd, w_out_rw,
                             w_o, final_g)))
    m_loc = dict(zip(names, (m_norm_g, m_w_in, m_conv_w, m_conv_b, m_dt_bias, m_a_log, m_d_skip, m_ssd_norm_g,
                             m_rw_mu, m_rw_w0, m_rw_w_up, m_rw_a0, m_rw_a_up, m_rw_k_k, m_rw_k_a, m_rw_r_k,
                             m_rw_ln_g, m_rw_ln_b, m_w_out_sb, m_w_out_ssd, m_w_out_rw, m_w_o, m_final_g)))
    v_loc = dict(zip(names, (v_norm_g, v_w_in, v_conv_w, v_conv_b, v_dt_bias, v_a_log, v_d_skip, v_ssd_norm_g,
                             v_rw_mu, v_rw_w0, v_rw_w_up, v_rw_a0, v_rw_a_up, v_rw_k_k, v_rw_k_a, v_rw_r_k,
                             v_rw_ln_g, v_rw_ln_b, v_w_out_sb, v_w_out_ssd, v_w_out_rw, v_w_o, v_final_g)))
    chip = 2 * lax.axis_index("x") + lax.axis_index("y")
    core = lax.axis_index("c")

    big_shapes = [w_loc[n].shape for n in BIG]
    pack = _pack_rows([w_loc[n].astype(BF16) for n in BIG], multiple=16)
    got = _chip_exchange(pack, per_dest=False, name="gather_big")
    full = {}
    per_chip = [_unpack_rows(got[q], big_shapes) for q in range(4)]
    for i, n in enumerate(BIG):
        full[n] = jnp.concatenate([per_chip[q][i] for q in range(4)], axis=BIG_AXIS[n])
    full["w_in"] = _pad_cols(full["w_in"])
    sm_names = tuple(SMALL_SHARDED)
    sm_shapes = [w_loc[n].shape for n in sm_names]
    (got_sm,) = _allgather_small(_pack_rows([w_loc[n] for n in sm_names]), reduce=False, name="gather_small")
    per_chip = [_unpack_rows(got_sm[4 * (q // 2) + 2 * (q % 2)], sm_shapes) for q in range(4)]
    for i, n in enumerate(sm_names):
        full[n] = jnp.concatenate([per_chip[q][i] for q in range(4)], axis=-1)

    def pad16(a):
        return jnp.zeros((1, LANES), F32).at[0, :SSD_HEADS].set(a)

    def layer_params(i):
        row = lambda n: w_loc[n][i].reshape(1, -1)
        cw = full["conv_w"][i]
        return dict(
            norm_g=row("norm_g"), w_in=full["w_in"][i], conv=[cw[k][None] for k in range(4)] + [row("conv_b")],
            dt_bias=pad16(dt_bias[i]), a_log=pad16(a_log[i]), d_skip=pad16(d_skip[i]),
            ssd_norm_g=row("ssd_norm_g"), rw_mu=row("rw_mu"),
            rw_pre=[row("rw_w0"), jnp.zeros((LANES, 512), F32).at[:HEAD].set(full["rw_w_up"][i]), row("rw_a0"),
                    jnp.zeros((LANES, 512), F32).at[HEAD:].set(full["rw_a_up"][i]), row("rw_k_k"), row("rw_k_a")],
            rw_post=[row("rw_ln_g"), row("rw_ln_b"), row("rw_r_k")],
            w_out_sb=full["w_out_sb"][i], w_out_ssd=full["w_out_ssd"][i], w_out_rw=full["w_out_rw"][i],
            w_o=full["w_o"][i])

    params = [layer_params(i) for i in range(DEPTH)]
    xs, saved = [x[0]], []
    for i in range(DEPTH):
        nxt, s = _layer_fwd(xs[-1], params[i], f"l{i}_")
        xs.append(nxt)
        saved.append(s)
    dx, loss_row, g_final = _final(xs[-1], final_g.reshape(1, -1), loss_target[0], bt=BT, name="final")
    grads = [None] * DEPTH
    for i in reversed(range(DEPTH)):
        dx, grads[i] = _layer_bwd(xs[i], dx, params[i], saved[i], f"l{i}_")

    def stacked(fn):
        return jnp.stack([fn(grads[i]) for i in range(DEPTH)])

    g_loc = {
        "norm_g": stacked(lambda g: g["norm_g"][0]),
        "w_in": stacked(lambda g: _unpad_cols(g["w_in"])),
        "conv_w": stacked(lambda g: jnp.concatenate(g["conv"][:4], axis=0)),
        "conv_b": stacked(lambda g: g["conv"][4][0]),
        "dt_bias": stacked(lambda g: g["dt_bias"][0, :SSD_HEADS]),
        "a_log": stacked(lambda g: g["a_log"][0, :SSD_HEADS]),
        "d_skip": stacked(lambda g: g["d_skip"][0, :SSD_HEADS]),
        "ssd_norm_g": stacked(lambda g: g["ssd_norm_g"][0]),
        "rw_mu": stacked(lambda g: g["rw_mu"][0]),
        "rw_w0": stacked(lambda g: g["rw_pre"][0][0]),
        "rw_w_up": stacked(lambda g: g["rw_pre"][1][:HEAD]),
        "rw_a0": stacked(lambda g: g["rw_pre"][2][0]),
        "rw_a_up": stacked(lambda g: g["rw_pre"][3][HEAD:]),
        "rw_k_k": stacked(lambda g: g["rw_pre"][4][0]),
        "rw_k_a": stacked(lambda g: g["rw_pre"][5][0]),
        "rw_r_k": stacked(lambda g: g["rw_r_k"].reshape(8, HEAD)),
        "rw_ln_g": stacked(lambda g: g["rw_ln_g"][0]),
        "rw_ln_b": stacked(lambda g: g["rw_ln_b"][0]),
        "w_out_sb": stacked(lambda g: g["w_out_sb"]),
        "w_out_ssd": stacked(lambda g: g["w_out_ssd"]),
        "w_out_rw": stacked(lambda g: g["w_out_rw"]),
        "w_o": stacked(lambda g: g["w_o"]),
        "final_g": g_final[0],
    }

    send = jnp.stack([_pack_rows([_split_chips(g_loc[n], BIG_AXIS[n])[q] for n in BIG], multiple=16) for q in range(4)])
    half = send.shape[1] // 2
    send = send.reshape(4, 2, half, LANES).transpose(1, 0, 2, 3)
    other = _sibling_send_other_half(send, name="reduce_sibling")
    part = _add_halves(send, other, core.reshape(1).astype(jnp.int32), name="reduce_add")
    parts = _chip_exchange(part, per_dest=True, name="reduce_chips")
    total = _sibling_join(_sum_chips(parts, name="reduce_sum"), name="reduce_join")
    g_out = dict(zip(BIG, _unpack_rows(total.reshape(2 * half, LANES), big_shapes)))

    sm_all = SMALL + ("loss",)
    sm_full_shapes = [g_loc[n].shape for n in SMALL] + [(1,)]
    _, summed = _allgather_small(_pack_rows([g_loc[n] for n in SMALL] + [loss_row[0, :1]]), reduce=True, name="reduce_small")
    sm = dict(zip(sm_all, _unpack_rows(summed, sm_full_shapes)))
    for n in SMALL:
        g_out[n] = sm[n]
    for n, wd in SMALL_SHARDED.items():
        g_out[n] = lax.dynamic_slice_in_dim(sm[n], chip * wd, wd, axis=sm[n].ndim - 1)
    loss = sm["loss"][0]

    upd = {n: _adamw(w_loc[n], g_out[n], m_loc[n], v_loc[n], name="adamw_" + n) for n in names}
    return (loss, dx[None], *[g_out[n] for n in names], *[upd[n][0] for n in names],
            *[upd[n][1] for n in names], *[upd[n][2] for n in names])
```

```python
import functools

import jax
import jax.numpy as jnp
from jax import lax
from jax.experimental import pallas as pl
from jax.experimental.pallas import tpu as pltpu

F32 = jnp.float32
BF16 = jnp.bfloat16

D_MODEL = 1024
DEPTH = 2
HEAD = 64
LANES = 128
CHUNK = 128
RMS_EPS = 1e-6
GN_EPS = 64e-5
VMEM_LIMIT = 56 * 1024 * 1024

N_IN = 9616
N_PAD = 9728
C_SB, C_Z, C_GATES, C_RW, C_LO, C_DT, C_XBC = 0, 2048, 3072, 6144, 8192, 8320, 8448
RW_COLS = 2176
XBC_COLS = 1280

ADAM_LR, ADAM_B1, ADAM_B2, ADAM_EPS, ADAM_WD, ADAM_STEP = 0.001, 0.9, 0.999, 1e-08, 0.01, 10


def _params(sem=None):
    return pltpu.CompilerParams(dimension_semantics=sem, vmem_limit_bytes=VMEM_LIMIT)


@jax.custom_vjp
def _sigmoid(x):
    return 1.0 / (1.0 + jnp.exp(-x))


def _sigmoid_fwd(x):
    s = _sigmoid(x)
    return s, s


def _sigmoid_bwd(s, g):
    return (g * s * (1.0 - s),)


_sigmoid.defvjp(_sigmoid_fwd, _sigmoid_bwd)


@jax.custom_vjp
def _silu(x):
    return x * _sigmoid(x)


def _silu_fwd(x):
    s = _sigmoid(x)
    return x * s, (x, s)


def _silu_bwd(res, g):
    x, s = res
    return (g * (s + x * s * (1.0 - s)),)


_silu.defvjp(_silu_fwd, _silu_bwd)


@jax.custom_vjp
def _softplus(x):
    return jnp.maximum(x, 0.0) + jnp.log(1.0 + jnp.exp(-jnp.abs(x)))


def _softplus_fwd(x):
    return _softplus(x), x


def _softplus_bwd(x, g):
    return (g * _sigmoid(x),)


_softplus.defvjp(_softplus_fwd, _softplus_bwd)


def _dot(a, b, dims):
    return lax.dot_general(a.astype(BF16), b.astype(BF16), (dims, ((), ())), preferred_element_type=F32)


def _dot_nn(a, b):
    return _dot(a, b, ((1,), (0,)))


def _dot_nt(a, b):
    return _dot(a, b, ((1,), (1,)))


def _dot_tn(a, b):
    return _dot(a, b, ((0,), (0,)))


@jax.custom_vjp
def _bdot(a, b):
    return _dot_nn(a, b)


def _bdot_fwd(a, b):
    return _dot_nn(a, b), (a, b)


def _bdot_bwd(res, g):
    a, b = res
    return _dot_nt(g, b), _dot_tn(a, g)


_bdot.defvjp(_bdot_fwd, _bdot_bwd)


def _split2(x):
    hi = x.astype(BF16)
    lo = (x - hi.astype(F32)).astype(BF16)
    return hi, lo


_NT = (((1,), (1,)), ((), ()))
_NN = (((1,), (0,)), ((), ()))
_TN = (((0,), (0,)), ((), ()))


def _dot2(x, m, dn=_NN):
    hi, lo = _split2(x)
    return (lax.dot_general(hi, m, dn, preferred_element_type=F32)
            + lax.dot_general(lo, m, dn, preferred_element_type=F32))


def _dot2_tn(x, m):
    return _dot2(x, m, _TN)


def _seg_matrix(n):
    r = lax.broadcasted_iota(jnp.int32, (n, n), 0) // HEAD
    c = lax.broadcasted_iota(jnp.int32, (n, n), 1) // HEAD
    return (r == c).astype(BF16)


@jax.custom_vjp
def _segsum2(x, seg):
    return _dot2(x, seg)


def _segsum2_fwd(x, seg):
    return _dot2(x, seg), seg


def _segsum2_bwd(seg, g):
    return _dot2(g, seg), jnp.zeros_like(seg)


_segsum2.defvjp(_segsum2_fwd, _segsum2_bwd)


def _make_segsum(seg):
    return lambda x: _segsum2(x, seg)


def _shift_down_raw(x, k):
    row = lax.broadcasted_iota(jnp.int32, x.shape, 0)
    return jnp.where(row >= k, pltpu.roll(x, k, 0), 0.0)


def _shift_up_raw(x, k):
    t = x.shape[0]
    row = lax.broadcasted_iota(jnp.int32, x.shape, 0)
    return jnp.where(row < t - k, pltpu.roll(x, t - k, 0), 0.0)


@functools.partial(jax.custom_vjp, nondiff_argnums=(1,))
def _shift_down(x, k):
    return _shift_down_raw(x, k)


def _shift_down_fwd(x, k):
    return _shift_down_raw(x, k), None


def _shift_down_bwd(k, _, g):
    return (_shift_up_raw(g, k),)


_shift_down.defvjp(_shift_down_fwd, _shift_down_bwd)


def _mm(a, b, *, name, ta=False, tb=False, add=None, out_dtype=F32, tm=512, tn=512, tk=None):
    m, k = (a.shape[1], a.shape[0]) if ta else a.shape
    n = b.shape[0] if tb else b.shape[1]
    tm, tn = min(tm, m), min(tn, n)
    tk = k if tk is None else tk
    nk = k // tk
    assert m % tm == 0 and n % tn == 0 and k % tk == 0
    dims = ((0 if ta else 1,), (1 if tb else 0,))

    def body(a_ref, b_ref, *refs):
        o_ref, acc_ref = refs[-2:]
        p = _dot(a_ref[...], b_ref[...], dims)

        def emit(total):
            if add is not None:
                total = total + refs[0][...]
            o_ref[...] = total.astype(o_ref.dtype)

        if nk == 1:
            emit(p)
        else:
            kk = pl.program_id(2)

            @pl.when(kk == 0)
            def _():
                acc_ref[...] = p

            @pl.when(kk > 0)
            def _():
                acc_ref[...] += p

            @pl.when(kk == nk - 1)
            def _():
                emit(acc_ref[...])

    a_spec = pl.BlockSpec((tk, tm), lambda i, j, kk: (kk, i)) if ta else pl.BlockSpec((tm, tk), lambda i, j, kk: (i, kk))
    b_spec = pl.BlockSpec((tn, tk), lambda i, j, kk: (j, kk)) if tb else pl.BlockSpec((tk, tn), lambda i, j, kk: (kk, j))
    o_spec = pl.BlockSpec((tm, tn), lambda i, j, kk: (i, j))
    return pl.pallas_call(
        body, name=name, grid=(m // tm, n // tn, nk),
        in_specs=[a_spec, b_spec] + ([o_spec] if add is not None else []), out_specs=o_spec,
        out_shape=jax.ShapeDtypeStruct((m, n), out_dtype),
        scratch_shapes=[pltpu.VMEM((tm, tn) if nk > 1 else (8, LANES), F32)],
        compiler_params=_params(("parallel", "parallel", "arbitrary")),
    )(a, b, *([add] if add is not None else []))


def _row_specs(rows, bt):
    return [pl.BlockSpec((bt, w), functools.partial(lambda i, c: (i, c), c=c)) for _, w, c in rows]


def _full_spec(p):
    return pl.BlockSpec(p.shape, functools.partial(lambda i, nd: (0,) * nd, nd=p.ndim))


def _rowwise(f, rows, pars, out_widths, *, bt, name, acc_widths=()):
    t = rows[0][0].shape[0]
    nr, npar, no, na = len(rows), len(pars), len(out_widths), len(acc_widths)

    def body(*refs):
        vals = [r[...] for r in refs[:nr + npar]]
        outs = f(*vals)
        for o_ref, o in zip(refs[nr + npar:nr + npar + no], outs[:no]):
            o_ref[...] = o.astype(o_ref.dtype)
        if na:
            first = pl.program_id(0) == 0
            for a_ref, a in zip(refs[nr + npar + no:], outs[no:]):
                @pl.when(first)
                def _():
                    a_ref[...] = jnp.zeros_like(a_ref)
                a_ref[...] += a

    return pl.pallas_call(
        body, name=name, grid=(t // bt,),
        in_specs=_row_specs(rows, bt) + [_full_spec(p) for p in pars],
        out_specs=[pl.BlockSpec((bt, w), lambda i: (i, 0)) for w in out_widths]
        + [pl.BlockSpec((1, w), lambda i: (0, 0)) for w in acc_widths],
        out_shape=[jax.ShapeDtypeStruct((t, w), F32) for w in out_widths]
        + [jax.ShapeDtypeStruct((1, w), F32) for w in acc_widths],
        compiler_params=_params(("arbitrary",)),
    )(*[r[0] for r in rows], *pars)


def _rowwise_bwd(f, rows, pars, douts, *, bt, name, groups=None):
    t = rows[0][0].shape[0]
    nr, npar, nd = len(rows), len(pars), len(douts)
    groups = [[i] for i in range(nr)] if groups is None else groups
    widths = [r[1] for r in rows]

    def body(*refs):
        vals = [r[...] for r in refs[:nr + npar]]
        cts = tuple(r[...] for r in refs[nr + npar:nr + npar + nd])
        _, vjp = jax.vjp(lambda *a: tuple(f(*a)), *vals)
        grads = vjp(cts)
        out_refs = refs[nr + npar + nd:]
        for g_ref, grp in zip(out_refs[:len(groups)], groups):
            off = 0
            for i in grp:
                g_ref[:, off:off + widths[i]] = grads[i]
                off += widths[i]
        first = pl.program_id(0) == 0
        for p_ref, g in zip(out_refs[len(groups):], grads[nr:]):
            @pl.when(first)
            def _():
                p_ref[...] = jnp.zeros_like(p_ref)
            p_ref[...] += g

    gw = [sum(widths[i] for i in grp) for grp in groups]
    return pl.pallas_call(
        body, name=name, grid=(t // bt,),
        in_specs=_row_specs(rows, bt) + [_full_spec(p) for p in pars] + _row_specs(douts, bt),
        out_specs=[pl.BlockSpec((bt, w), lambda i: (i, 0)) for w in gw] + [_full_spec(p) for p in pars],
        out_shape=[jax.ShapeDtypeStruct((t, w), F32) for w in gw] + [jax.ShapeDtypeStruct(p.shape, F32) for p in pars],
        compiler_params=_params(("arbitrary",)),
    )(*[r[0] for r in rows], *pars, *[d[0] for d in douts])


def _colwise(f, x, c0, ncols, pars, *, bc, name):
    t = x.shape[0]

    def body(x_ref, *refs):
        o_ref = refs[-1]
        o_ref[...] = f(x_ref[...], *[r[...] for r in refs[:-1]])

    return pl.pallas_call(
        body, name=name, grid=(ncols // bc,),
        in_specs=[pl.BlockSpec((t, bc), lambda j: (0, j + c0 // bc))]
        + [pl.BlockSpec((p.shape[0], bc), lambda j: (0, j)) for p in pars],
        out_specs=pl.BlockSpec((t, bc), lambda j: (0, j)),
        out_shape=jax.ShapeDtypeStruct((t, ncols), F32),
        compiler_params=_params(("parallel",)),
    )(x, *pars)


def _colwise_bwd(f, x, c0, ncols, pars, dout, *, bc, name):
    t = x.shape[0]
    npar = len(pars)

    def body(x_ref, *refs):
        vals = [x_ref[...]] + [r[...] for r in refs[:npar]]
        _, vjp = jax.vjp(f, *vals)
        grads = vjp(refs[npar][...])
        for g_ref, g in zip(refs[npar + 1:], grads):
            g_ref[...] = g

    return pl.pallas_call(
        body, name=name, grid=(ncols // bc,),
        in_specs=[pl.BlockSpec((t, bc), lambda j: (0, j + c0 // bc))]
        + [pl.BlockSpec((p.shape[0], bc), lambda j: (0, j)) for p in pars]
        + [pl.BlockSpec((t, bc), lambda j: (0, j))],
        out_specs=[pl.BlockSpec((t, bc), lambda j: (0, j))]
        + [pl.BlockSpec((p.shape[0], bc), lambda j: (0, j)) for p in pars],
        out_shape=[jax.ShapeDtypeStruct((t, ncols), F32)] + [jax.ShapeDtypeStruct(p.shape, F32) for p in pars],
        compiler_params=_params(("parallel",)),
    )(x, *pars, dout)


def _f_rms(x, g):
    return (x * lax.rsqrt(jnp.mean(x * x, axis=-1, keepdims=True) + RMS_EPS) * g,)


def _f_sb_gate(y, gate):
    return (y * _silu(gate),)


def _f_ssd_norm(y, z, g):
    u = y * _silu(z)
    return (u * lax.rsqrt(jnp.mean(u * u, axis=-1, keepdims=True) + RMS_EPS) * g,)


def _f_merge(p_sb, p_ssd, p_rw, g_sb, g_ssd, g_rw):
    return (_sigmoid(g_sb) * p_sb + _sigmoid(g_ssd) * p_ssd + _sigmoid(g_rw) * p_rw,)


def _f_rw_pre(k, lo, w0, w_up, a0, a_up, k_k, k_a):
    segsum = _make_segsum(_seg_matrix(k.shape[1]))
    lane = lax.broadcasted_iota(jnp.int32, lo.shape, 1)
    w_lo = jnp.where(lane < HEAD, jnp.tanh(lo), 0.0)
    a_lo = jnp.where(lane >= HEAD, lo, 0.0)
    w = -_softplus(-(w0 + _bdot(w_lo, w_up))) - 0.5
    decay = jnp.exp(-jnp.exp(w))
    a = _sigmoid(a0 + _bdot(a_lo, a_up))
    kk = k * k_k
    kk = kk / jnp.maximum(jnp.sqrt(segsum(kk * kk)), 1e-12)
    return decay, k * (1.0 + (a - 1.0) * k_a), -kk, kk * a


def _f_rw_post(y, r, k2, v, gate, ln_g, ln_b, r_k):
    segsum = _make_segsum(_seg_matrix(y.shape[1]))
    yc = y - segsum(y) * (1.0 / HEAD)
    var = segsum(yc * yc) * (1.0 / HEAD)
    yn = yc * lax.rsqrt(var + GN_EPS) * ln_g + ln_b
    return ((yn + segsum(r * k2 * r_k) * v) * _silu(gate),)


def _f_rw_mix(slab, mu):
    return slab + (_shift_down(slab, 1) - slab) * mu


def _f_conv(x, w0, w1, w2, w3, b):
    acc = x * w3 + b
    for i, w in enumerate((w0, w1, w2)):
        acc = acc + _shift_down(x, 3 - i) * w
    return _silu(acc)


def _log_sigmoid(z):
    return jnp.minimum(z, 0.0) - jnp.log(1.0 + jnp.exp(-jnp.abs(z)))


def _prefix_matrix(kind):
    j = lax.broadcasted_iota(jnp.int32, (CHUNK, 2 * CHUNK), 0)
    s = lax.broadcasted_iota(jnp.int32, (CHUNK, 2 * CHUNK), 1)
    tri = {"gt": j > s, "le": j <= s, "lt": j < s}[kind]
    return (tri | (s >= CHUNK)).astype(BF16)


def _sb_specs(t):
    q = pl.BlockSpec((CHUNK, LANES), lambda j, i: (i, j))
    k = pl.BlockSpec((t, LANES), lambda j, i: (0, 4 + j))
    v = pl.BlockSpec((t, LANES), lambda j, i: (0, 8 + j))
    return q, k, v


def _sb_fwd(proj, *, name):
    t = proj.shape[0]
    scale = HEAD ** -0.5

    def body(q_ref, k_ref, v_ref, y_ref, lt_ref):
        i = pl.program_id(1)
        lane = lax.broadcasted_iota(jnp.int32, (CHUNK, LANES), 1)
        diff = (lax.broadcasted_iota(jnp.int32, (CHUNK, CHUNK), 1)
                - lax.broadcasted_iota(jnp.int32, (CHUNK, CHUNK), 0))
        m_f = _prefix_matrix("gt")
        q = q_ref[...] * scale
        qh = [jnp.where((lane // HEAD) == h, q, 0.0).astype(BF16) for h in (0, 1)]

        def step(it, carry):
            off = pl.multiple_of((i - it) * CHUNK, CHUNK)
            kblk = k_ref[pl.ds(off, CHUNK), :].astype(BF16)
            vblk = v_ref[pl.ds(off, CHUNK), :].astype(BF16)
            mask = diff < it * CHUNK
            new = []
            for h in (0, 1):
                c, acc = carry[2 * h], carry[2 * h + 1]
                z = lax.dot_general(qh[h], kblk, _NT, preferred_element_type=F32)
                lb = _log_sigmoid(z)
                w2 = _dot2(jnp.where(mask, lb - z, 0.0), m_f)
                att = jnp.where(mask, jnp.exp(lb + c + w2[:, :CHUNK]), 0.0)
                acc = acc + lax.dot_general(att.astype(BF16), vblk, _NN, preferred_element_type=F32)
                new += [c + w2[:, CHUNK:], acc]
            return tuple(new)

        zero = jnp.zeros((CHUNK, LANES), F32)
        c_a, acc_a, c_b, acc_b = lax.fori_loop(0, i + 1, step, (zero, zero, zero, zero))
        y_ref[...] = jnp.where(lane < HEAD, acc_a, acc_b)
        lt_ref[0] = c_a
        lt_ref[1] = c_b

    return pl.pallas_call(
        body, name=name, grid=(4, t // CHUNK),
        in_specs=list(_sb_specs(t)),
        out_specs=[pl.BlockSpec((CHUNK, LANES), lambda j, i: (i, j)),
                   pl.BlockSpec((2, CHUNK, LANES), lambda j, i: (j, i, 0))],
        out_shape=[jax.ShapeDtypeStruct((t, 4 * LANES), F32), jax.ShapeDtypeStruct((8, t, LANES), F32)],
        compiler_params=_params(("parallel", "arbitrary")),
    )(proj, proj, proj)


def _sb_bwd(proj, dy, lt, *, name):
    t = proj.shape[0]
    scale = HEAD ** -0.5

    def body(q_ref, k_ref, v_ref, dy_ref, lt_ref, dq_ref, dk_ref, dv_ref):
        i = pl.program_id(1)

        @pl.when(i == 0)
        def _():
            dk_ref[...] = jnp.zeros_like(dk_ref)
            dv_ref[...] = jnp.zeros_like(dv_ref)

        lane = lax.broadcasted_iota(jnp.int32, (CHUNK, LANES), 1)
        diff = (lax.broadcasted_iota(jnp.int32, (CHUNK, CHUNK), 1)
                - lax.broadcasted_iota(jnp.int32, (CHUNK, CHUNK), 0))
        m_le, m_lt = _prefix_matrix("le"), _prefix_matrix("lt")
        q = q_ref[...] * scale
        dy_blk = dy_ref[...]
        qh = [jnp.where((lane // HEAD) == h, q, 0.0).astype(BF16) for h in (0, 1)]
        doh = [jnp.where((lane // HEAD) == h, dy_blk, 0.0).astype(BF16) for h in (0, 1)]
        lth = [lt_ref[0], lt_ref[1]]

        def step(kb, carry):
            off = pl.multiple_of(kb * CHUNK, CHUNK)
            kblk = k_ref[pl.ds(off, CHUNK), :].astype(BF16)
            vblk = v_ref[pl.ds(off, CHUNK), :].astype(BF16)
            mask = diff < (i - kb) * CHUNK
            new = []
            dk_acc = jnp.zeros((CHUNK, LANES), F32)
            dv_acc = jnp.zeros((CHUNK, LANES), F32)
            for h in (0, 1):
                cp, cg, dq = carry[3 * h:3 * h + 3]
                z = lax.dot_general(qh[h], kblk, _NT, preferred_element_type=F32)
                lb = _log_sigmoid(z)
                w2 = _dot2(jnp.where(mask, lb - z, 0.0), m_le)
                att = jnp.where(mask, jnp.exp(lb + lth[h] - cp - w2[:, :CHUNK]), 0.0)
                d_att = lax.dot_general(doh[h], vblk, _NT, preferred_element_type=F32)
                d_e = d_att * att
                g2 = _dot2(d_e, m_lt)
                sig = jnp.exp(lb)
                dz = jnp.where(mask, d_e * (1.0 - sig) - (cg + g2[:, :CHUNK]) * sig, 0.0).astype(BF16)
                dq = dq + lax.dot_general(dz, kblk, _NN, preferred_element_type=F32)
                dk_acc = dk_acc + lax.dot_general(dz, qh[h], _TN, preferred_element_type=F32)
                dv_acc = dv_acc + lax.dot_general(att.astype(BF16), doh[h], _TN, preferred_element_type=F32)
                new += [cp + w2[:, CHUNK:], cg + g2[:, CHUNK:], dq]
            dk_ref[pl.ds(off, CHUNK), :] += dk_acc
            dv_ref[pl.ds(off, CHUNK), :] += dv_acc
            return tuple(new)

        zero = jnp.zeros((CHUNK, LANES), F32)
        out = lax.fori_loop(0, i + 1, step, (zero,) * 6)
        dq_ref[...] = jnp.where(lane < HEAD, out[2], out[5]) * scale

    q_spec, k_spec, v_spec = _sb_specs(t)
    blk = pl.BlockSpec((CHUNK, LANES), lambda j, i: (i, j))
    col = pl.BlockSpec((t, LANES), lambda j, i: (0, j))
    return pl.pallas_call(
        body, name=name, grid=(4, t // CHUNK),
        in_specs=[q_spec, k_spec, v_spec, blk, pl.BlockSpec((2, CHUNK, LANES), lambda j, i: (j, i, 0))],
        out_specs=[blk, col, col],
        out_shape=[jax.ShapeDtypeStruct((t, 4 * LANES), F32)] * 3,
        compiler_params=_params(("parallel", "arbitrary")),
    )(proj, proj, proj, dy, lt)


SSD_HEADS = 16
SSD_PAIRS = 8


def _split3(x):
    a = x.astype(BF16)
    r = x - a.astype(F32)
    b = r.astype(BF16)
    return a, b, (r - b.astype(F32)).astype(BF16)


def _dot3(x, m, dn=_NN):
    return sum(lax.dot_general(p, m, dn, preferred_element_type=F32) for p in _split3(x))


def _mdot3(m, x):
    return sum(lax.dot_general(m, p, _NN, preferred_element_type=F32) for p in _split3(x))


def _ssd_common(dtr, dtb, alog, acsx_s, acst_s):
    lane = lax.broadcasted_iota(jnp.int32, (CHUNK, LANES), 1)
    lane1 = lax.broadcasted_iota(jnp.int32, (1, LANES), 1)
    arow = jnp.where(lane1 < SSD_HEADS, -jnp.exp(alog), 0.0)
    dt = jnp.where(lane < SSD_HEADS, _softplus(dtr + dtb), 0.0)
    da = dt * arow
    r = lax.broadcasted_iota(jnp.int32, (CHUNK, CHUNK), 0)
    c = lax.broadcasted_iota(jnp.int32, (CHUNK, CHUNK), 1)
    tril = (r >= c).astype(BF16)
    triu = (r <= c).astype(BF16)
    acs = _mdot3(tril, da)
    acst_s[...] = _dot3(da, triu, _TN)
    eh = lax.broadcasted_iota(jnp.int32, (LANES, 8 * LANES), 0)
    e = (eh == lax.broadcasted_iota(jnp.int32, (LANES, 8 * LANES), 1) // HEAD).astype(BF16)
    eh2 = lax.broadcasted_iota(jnp.int32, (LANES, 16 * LANES), 0)
    e2 = (eh2 == lax.broadcasted_iota(jnp.int32, (LANES, 16 * LANES), 1) // LANES).astype(BF16)
    acsx_s[...] = _dot3(acs, e)
    return dt, arow, _dot3(dt, e), _dot3(acs, e2), e, tril, triu


def _ssd_fwd(xc, proj, dtb, alog, dsk, *, name):
    t = xc.shape[0]
    nc = t // CHUNK

    def body(x_ref, b_ref, c_ref, dtr_ref, dtb_ref, alog_ref, dsk_ref, y_ref, hin_ref, acsx_s, acst_s, h_s):
        @pl.when(pl.program_id(0) == 0)
        def _():
            h_s[...] = jnp.zeros_like(h_s)

        dt, arow, dt_x, acs_b, e, tril, _ = _ssd_common(dtr_ref[...], dtb_ref[...], alog_ref[...], acsx_s, acst_s)
        dsk_x = _dot3(jnp.broadcast_to(dsk_ref[...], (CHUNK, LANES)), e)
        lane = lax.broadcasted_iota(jnp.int32, (CHUNK, LANES), 1)
        causal = (lax.broadcasted_iota(jnp.int32, (CHUNK, CHUNK), 0)
                  >= lax.broadcasted_iota(jnp.int32, (CHUNK, CHUNK), 1))
        for j in range(SSD_PAIRS):
            g = j // 4
            sl = slice(j * LANES, (j + 1) * LANES)
            if j % 4 == 0:
                bg = jnp.where(lane // HEAD == g, b_ref[...], 0.0)
                cg = jnp.where(lane // HEAD == g, c_ref[...], 0.0)
                cb = _dot_nt(cg, bg)
            x = x_ref[:, sl]
            a = acsx_s[:, sl]
            at = acsx_s[CHUNK - 1:CHUNK, sl]
            xdt = x * dt_x[:, sl]
            hin = h_s[j]
            hin_ref[0, j] = hin
            y = jnp.exp(a) * _dot_nn(cg, hin) + x * dsk_x[:, sl]
            h_s[j] = jnp.exp(at) * hin + _dot_tn(bg, xdt * jnp.exp(at - a))
            yd = []
            for hh in (0, 1):
                h = 2 * j + hh
                dec = jnp.exp(jnp.minimum(acs_b[:, h * LANES:(h + 1) * LANES] - acst_s[pl.ds(h, 1), :], 0.0))
                yd.append(_dot_nn(jnp.where(causal, cb * dec, 0.0), xdt))
            y_ref[:, sl] = y + jnp.where(lane < HEAD, yd[0], yd[1])

    one = pl.BlockSpec((1, LANES), lambda i: (0, 0))
    return pl.pallas_call(
        body, name=name, grid=(nc,),
        in_specs=[pl.BlockSpec((CHUNK, 8 * LANES), lambda i: (i, 0)),
                  pl.BlockSpec((CHUNK, LANES), lambda i: (i, 8)),
                  pl.BlockSpec((CHUNK, LANES), lambda i: (i, 9)),
                  pl.BlockSpec((CHUNK, LANES), lambda i: (i, C_DT // LANES)), one, one, one],
        out_specs=[pl.BlockSpec((CHUNK, 8 * LANES), lambda i: (i, 0)),
                   pl.BlockSpec((1, SSD_PAIRS, LANES, LANES), lambda i: (i, 0, 0, 0))],
        out_shape=[jax.ShapeDtypeStruct((t, 8 * LANES), F32),
                   jax.ShapeDtypeStruct((nc, SSD_PAIRS, LANES, LANES), F32)],
        scratch_shapes=[pltpu.VMEM((CHUNK, 8 * LANES), F32), pltpu.VMEM((LANES, CHUNK), F32),
                        pltpu.VMEM((SSD_PAIRS, LANES, LANES), F32)],
        compiler_params=_params(("arbitrary",)),
    )(xc, xc, xc, proj, dtb, alog, dsk)


def _ssd_bwd(xc, proj, dtb, alog, dsk, hin_all, dy, *, name):
    t = xc.shape[0]
    nc = t // CHUNK

    def body(x_ref, b_ref, c_ref, dtr_ref, dtb_ref, alog_ref, dsk_ref, hin_ref, dy_ref,
             dxc_ref, ddtr_ref, ddtb_ref, dalog_ref, ddsk_ref, acsx_s, acst_s, dh_s, dax_s, ddx_s):
        @pl.when(pl.program_id(0) == 0)
        def _():
            dh_s[...] = jnp.zeros_like(dh_s)
            ddtb_ref[...] = jnp.zeros_like(ddtb_ref)
            dalog_ref[...] = jnp.zeros_like(dalog_ref)
            ddsk_ref[...] = jnp.zeros_like(ddsk_ref)

        dtr = dtr_ref[...]
        dtb = dtb_ref[...]
        dt, arow, dt_x, acs_b, e, tril, triu = _ssd_common(dtr, dtb, alog_ref[...], acsx_s, acst_s)
        dsk_x = _dot3(jnp.broadcast_to(dsk_ref[...], (CHUNK, LANES)), e)
        lane = lax.broadcasted_iota(jnp.int32, (CHUNK, LANES), 1)
        rowi = lax.broadcasted_iota(jnp.int32, (CHUNK, LANES), 0)
        causal = (lax.broadcasted_iota(jnp.int32, (CHUNK, CHUNK), 0)
                  >= lax.broadcasted_iota(jnp.int32, (CHUNK, CHUNK), 1))
        dacs = jnp.zeros((CHUNK, LANES), F32)
        d_b = jnp.zeros((CHUNK, LANES), F32)
        d_c = jnp.zeros((CHUNK, LANES), F32)
        for j in range(SSD_PAIRS):
            g = j // 4
            sl = slice(j * LANES, (j + 1) * LANES)
            if j % 4 == 0:
                bg = jnp.where(lane // HEAD == g, b_ref[...], 0.0)
                cg = jnp.where(lane // HEAD == g, c_ref[...], 0.0)
                cb = _dot_nt(cg, bg)
                dcb = jnp.zeros((CHUNK, CHUNK), F32)
            x = x_ref[:, sl]
            d = dt_x[:, sl]
            a = acsx_s[:, sl]
            at = acsx_s[CHUNK - 1:CHUNK, sl]
            xdt = x * d
            hin = hin_ref[0, j]
            dhout = dh_s[j]
            dyp = dy_ref[:, sl]
            ea, eat, ed = jnp.exp(a), jnp.exp(at), jnp.exp(at - a)
            da_l = dyp * ea * _dot_nn(cg, hin)
            dm = dyp * ea
            d_c = d_c + _dot_nt(dm, hin)
            dh_s[j] = _dot_tn(cg, dm) + eat * dhout
            dat = jnp.sum(dhout * hin * eat, axis=0, keepdims=True)
            d_b = d_b + _dot_nt(xdt * ed, dhout)
            dw = _dot_nn(bg, dhout)
            dxdt = dw * ed
            ded = dw * xdt * ed
            dat = dat + jnp.sum(ded, axis=0, keepdims=True)
            da_l = da_l - ded
            for hh in (0, 1):
                h = 2 * j + hh
                dec = jnp.exp(jnp.minimum(acs_b[:, h * LANES:(h + 1) * LANES] - acst_s[pl.ds(h, 1), :], 0.0))
                gm = jnp.where(causal, cb * dec, 0.0)
                dyh = jnp.where(lane // HEAD == hh, dyp, 0.0)
                dg = _dot_nt(dyh, xdt)
                dxdt = dxdt + _dot_tn(gm, dyh)
                dcb = dcb + jnp.where(causal, dg * dec, 0.0)
                th = dg * gm
                oh = (lane == h).astype(BF16)
                dacs = dacs + _dot2(th, oh) - _dot2_tn(th, oh)
            if j % 4 == 3:
                d_c = d_c + _dot_nn(dcb, bg)
                d_b = d_b + _dot_tn(dcb, cg)
            dxc_ref[:, sl] = dyp * dsk_x[:, sl] + dxdt * d
            ddx_s[:, sl] = dxdt * x
            dax_s[:, sl] = da_l + jnp.where(rowi == CHUNK - 1, dat, 0.0)
            dskp = jnp.sum(dyp * x, axis=0, keepdims=True)
            ddsk_ref[...] += _dot2(jnp.broadcast_to(dskp, (8, LANES)), e[:, sl], _NT)
        dxc_ref[:, 8 * LANES:9 * LANES] = d_b
        dxc_ref[:, 9 * LANES:10 * LANES] = d_c
        dacs = dacs + _dot2(dax_s[...], e, _NT)
        ddt = _dot2(ddx_s[...], e, _NT)
        dda = _mdot3(triu, dacs)
        ddt = ddt + dda * arow
        dalog_ref[...] += jnp.sum(dda * dt, axis=0, keepdims=True) * arow
        ddtr = jnp.where(lane < SSD_HEADS, ddt * _sigmoid(dtr + dtb), 0.0)
        ddtr_ref[...] = ddtr
        ddtb_ref[...] += jnp.sum(ddtr, axis=0, keepdims=True)

    one = pl.BlockSpec((1, LANES), lambda i: (0, 0))
    rev = lambda c: (lambda i: (nc - 1 - i, c))
    return pl.pallas_call(
        body, name=name, grid=(nc,),
        in_specs=[pl.BlockSpec((CHUNK, 8 * LANES), rev(0)), pl.BlockSpec((CHUNK, LANES), rev(8)),
                  pl.BlockSpec((CHUNK, LANES), rev(9)), pl.BlockSpec((CHUNK, LANES), rev(C_DT // LANES)),
                  one, one, one,
                  pl.BlockSpec((1, SSD_PAIRS, LANES, LANES), lambda i: (nc - 1 - i, 0, 0, 0)),
                  pl.BlockSpec((CHUNK, 8 * LANES), rev(0))],
        out_specs=[pl.BlockSpec((CHUNK, XBC_COLS), rev(0)), pl.BlockSpec((CHUNK, LANES), rev(0)), one, one,
                   pl.BlockSpec((8, LANES), lambda i: (0, 0))],
        out_shape=[jax.ShapeDtypeStruct((t, XBC_COLS), F32), jax.ShapeDtypeStruct((t, LANES), F32)]
        + [jax.ShapeDtypeStruct((1, LANES), F32)] * 2 + [jax.ShapeDtypeStruct((8, LANES), F32)],
        scratch_shapes=[pltpu.VMEM((CHUNK, 8 * LANES), F32), pltpu.VMEM((LANES, CHUNK), F32),
                        pltpu.VMEM((SSD_PAIRS, LANES, LANES), F32),
                        pltpu.VMEM((CHUNK, 8 * LANES), F32), pltpu.VMEM((CHUNK, 8 * LANES), F32)],
        compiler_params=_params(("arbitrary",)),
    )(xc, xc, xc, proj, dtb, alog, dsk, hin_all, dy)


RW_PAIRS = 4
RW_BT = 16


def _rw_consts():
    seg = _seg_matrix(LANES)
    ti = (lax.broadcasted_iota(jnp.int32, (HEAD, LANES), 0)
          == lax.broadcasted_iota(jnp.int32, (HEAD, LANES), 1) % HEAD)
    return seg, ti


def _col_tiles(rows, ti, seg):
    tib = ti.astype(BF16)
    hi = [r.astype(BF16) for r in rows]
    lo = [(r - h.astype(F32)).astype(BF16) for r, h in zip(rows, hi)]
    out = (lax.dot_general(jnp.concatenate([tib * h for h in hi], axis=0), seg, _NN, preferred_element_type=F32)
           + lax.dot_general(jnp.concatenate([tib * l for l in lo], axis=0), seg, _NN, preferred_element_type=F32))
    return [out[i * HEAD:(i + 1) * HEAD] for i in range(len(rows))]


def _head_lane_sums(tiles, ti, seg):
    out = _dot2(jnp.concatenate(tiles, axis=0), seg)
    return [jnp.sum(jnp.where(ti, out[i * HEAD:(i + 1) * HEAD], 0.0), axis=0, keepdims=True) for i in range(len(tiles))]


def _rw_scan_fwd(mixed, w, k, n, b, *, name):
    t = w.shape[0]

    def body(r_ref, v_ref, w_ref, k_ref, n_ref, b_ref, y_ref, st_ref, s_s):
        @pl.when(pl.program_id(0) == 0)
        def _():
            s_s[...] = jnp.zeros_like(s_s)

        seg, ti = _rw_consts()

        def step(tt, state):
            row = pl.ds(tt, 1)
            new = []
            for p in range(RW_PAIRS):
                sl = pl.ds(p * LANES, LANES)
                s = state[p]
                ncol, wcol, bcol, kcol, rcol = _col_tiles(
                    [x[row, sl] for x in (n_ref, w_ref, b_ref, k_ref, r_ref)], ti, seg)
                sa = jnp.sum(s * ncol, axis=0, keepdims=True)
                s = s * wcol + bcol * sa + kcol * v_ref[row, sl]
                y_ref[row, sl] = jnp.sum(s * rcol, axis=0, keepdims=True)
                st_ref[tt, p] = s
                new.append(s)
            return tuple(new)

        out = tuple(s_s[p] for p in range(RW_PAIRS))
        for tt in range(RW_BT):
            out = step(tt, out)
        for p in range(RW_PAIRS):
            s_s[p] = out[p]

    blk = lambda c: pl.BlockSpec((RW_BT, 4 * LANES), functools.partial(lambda i, c: (i, c), c=c))
    return pl.pallas_call(
        body, name=name, grid=(t // RW_BT,),
        in_specs=[blk(0), blk(2), blk(0), blk(0), blk(0), blk(0)],
        out_specs=[blk(0), pl.BlockSpec((RW_BT, RW_PAIRS, HEAD, LANES), lambda i: (i, 0, 0, 0))],
        out_shape=[jax.ShapeDtypeStruct((t, 4 * LANES), F32),
                   jax.ShapeDtypeStruct((t, RW_PAIRS, HEAD, LANES), F32)],
        scratch_shapes=[pltpu.VMEM((RW_PAIRS, HEAD, LANES), F32)],
        compiler_params=_params(("arbitrary",)),
    )(mixed, mixed, w, k, n, b)


def _rw_scan_bwd(mixed, w, k, n, b, states, dy, dr0, dk0, dv0, *, name):
    t = w.shape[0]
    nb = t // RW_BT

    def body(r_ref, v_ref, w_ref, k_ref, n_ref, b_ref, st_ref, prev_ref, dy_ref, dr0_ref, dk0_ref, dv0_ref,
             dr_ref, dw_ref, dk_ref, dv_ref, dn_ref, db_ref, ds_s):
        @pl.when(pl.program_id(0) == 0)
        def _():
            ds_s[...] = jnp.zeros_like(ds_s)

        seg, ti = _rw_consts()
        has_prev = (pl.program_id(0) < nb - 1).astype(F32)

        def step(it, carry):
            tt = RW_BT - 1 - it
            row = pl.ds(tt, 1)
            prev_t = max(tt - 1, 0)
            new_ds, new_s = [], []
            for p in range(RW_PAIRS):
                sl = pl.ds(p * LANES, LANES)
                ds, s_t = carry[p], carry[RW_PAIRS + p]
                s_p = st_ref[prev_t, p] if tt > 0 else prev_ref[0, p] * has_prev
                ncol, wcol, bcol, kcol, rcol = _col_tiles(
                    [x[row, sl] for x in (n_ref, w_ref, b_ref, k_ref, r_ref)], ti, seg)
                vv, dyy = v_ref[row, sl], dy_ref[row, sl]
                sa = jnp.sum(s_p * ncol, axis=0, keepdims=True)
                ds = ds + rcol * dyy
                dsa = jnp.sum(ds * bcol, axis=0, keepdims=True)
                dv_ref[row, sl] = jnp.sum(ds * kcol, axis=0, keepdims=True) + dv0_ref[row, sl]
                dr, dw, db, dk, dn = _head_lane_sums([s_t * dyy, ds * s_p, ds * sa, ds * vv, s_p * dsa], ti, seg)
                dr_ref[row, sl] = dr + dr0_ref[row, sl]
                dw_ref[row, sl] = dw
                db_ref[row, sl] = db
                dk_ref[row, sl] = dk + dk0_ref[row, sl]
                dn_ref[row, sl] = dn
                new_ds.append(ds * wcol + ncol * dsa)
                new_s.append(s_p)
            return tuple(new_ds) + tuple(new_s)

        init = tuple(ds_s[p] for p in range(RW_PAIRS)) + tuple(st_ref[RW_BT - 1, p] for p in range(RW_PAIRS))
        out = init
        for it in range(RW_BT):
            out = step(it, out)
        for p in range(RW_PAIRS):
            ds_s[p] = out[p]

    blk = lambda c: pl.BlockSpec((RW_BT, 4 * LANES), functools.partial(lambda i, c: (nb - 1 - i, c), c=c))
    st_spec = pl.BlockSpec((RW_BT, RW_PAIRS, HEAD, LANES), lambda i: (nb - 1 - i, 0, 0, 0))
    prev_spec = pl.BlockSpec((1, RW_PAIRS, HEAD, LANES), lambda i: (jnp.maximum((nb - 1 - i) * RW_BT - 1, 0), 0, 0, 0))
    return pl.pallas_call(
        body, name=name, grid=(nb,),
        in_specs=[blk(0), blk(2), blk(0), blk(0), blk(0), blk(0), st_spec, prev_spec, blk(0), blk(0), blk(0), blk(0)],
        out_specs=[blk(0)] * 6,
        out_shape=[jax.ShapeDtypeStruct((t, 4 * LANES), F32)] * 6,
        scratch_shapes=[pltpu.VMEM((RW_PAIRS, HEAD, LANES), F32)],
        compiler_params=_params(("arbitrary",)),
    )(mixed, mixed, w, k, n, b, states, states, dy, dr0, dk0, dv0)


def _f_rms_res(x, g):
    return _f_rms(x, g)[0], x


def _final(x, g, target, *, bt, name):
    t, d = x.shape

    def body(x_ref, g_ref, t_ref, dx_ref, loss_ref, dg_ref):
        tgt = t_ref[...]

        def f(xv, gv):
            err = _f_rms(xv, gv)[0] - tgt
            return 0.5 * jnp.mean(err * err, axis=-1, keepdims=True)

        row_loss, vjp = jax.vjp(f, x_ref[...], g_ref[...])
        dx, dg = vjp(jnp.ones_like(row_loss))
        dx_ref[...] = dx

        @pl.when(pl.program_id(0) == 0)
        def _():
            loss_ref[...] = jnp.zeros_like(loss_ref)
            dg_ref[...] = jnp.zeros_like(dg_ref)

        loss_ref[...] += jnp.broadcast_to(jnp.sum(row_loss, axis=0, keepdims=True), (1, LANES))
        dg_ref[...] += dg

    blk = pl.BlockSpec((bt, d), lambda i: (i, 0))
    return pl.pallas_call(
        body, name=name, grid=(t // bt,),
        in_specs=[blk, pl.BlockSpec((1, d), lambda i: (0, 0)), blk],
        out_specs=[blk, pl.BlockSpec((1, LANES), lambda i: (0, 0)), pl.BlockSpec((1, d), lambda i: (0, 0))],
        out_shape=[jax.ShapeDtypeStruct((t, d), F32), jax.ShapeDtypeStruct((1, LANES), F32),
                   jax.ShapeDtypeStruct((1, d), F32)],
        compiler_params=_params(("arbitrary",)),
    )(x, g, target)


ADAMW_BLOCK_BYTES = 1 << 20


def _adamw(w, g, m, v, *, name):
    shape = w.shape
    c = shape[-1]
    args = [a.reshape(-1, c) for a in (w, g, m, v)]
    r = args[0].shape[0]
    br = r
    if r * c * 4 > ADAMW_BLOCK_BYTES:
        cands = [b for b in range(8, r, 8) if r % b == 0 and b * c * 4 <= ADAMW_BLOCK_BYTES]
        br = max(cands) if cands else r

    def body(w_ref, g_ref, m_ref, v_ref, d_ref, nm_ref, nv_ref):
        gv = g_ref[...]
        m_new = ADAM_B1 * m_ref[...] + (1.0 - ADAM_B1) * gv
        v_new = ADAM_B2 * v_ref[...] + (1.0 - ADAM_B2) * (gv * gv)
        m_hat = m_new / (1.0 - ADAM_B1 ** ADAM_STEP)
        v_hat = v_new / (1.0 - ADAM_B2 ** ADAM_STEP)
        d_ref[...] = -ADAM_LR * (m_hat / (jnp.sqrt(v_hat) + ADAM_EPS) + ADAM_WD * w_ref[...])
        nm_ref[...] = m_new
        nv_ref[...] = v_new

    blk = pl.BlockSpec((br, c), lambda i: (i, 0))
    outs = pl.pallas_call(
        body, name=name, grid=(r // br,), in_specs=[blk] * 4, out_specs=[blk] * 3,
        out_shape=[jax.ShapeDtypeStruct((r, c), F32)] * 3,
        compiler_params=_params(("parallel",)),
    )(*args)
    return tuple(o.reshape(shape) for o in outs)


BT = 256
BC = 128


def _layer_rows(x, proj, s):
    s = {k: s.get(k) for k in ("y_sb_raw", "y_ssd_raw", "mixed", "ys", "k2", "p_sb", "p_ssd", "p_rw")}
    return dict(
        rms=[(x, D_MODEL, 0)],
        sb_gate=[(s["y_sb_raw"], 512, 0), (proj, 512, 3)],
        ssd_norm=[(s["y_ssd_raw"], 1024, 0), (proj, 1024, C_Z // 1024)],
        rw_pre=[(s["mixed"], 512, 1), (s["mixed"], LANES, 16)],
        rw_post=[(s["ys"], 512, 0), (s["mixed"], 512, 0), (s["k2"], 512, 0), (s["mixed"], 512, 2), (s["mixed"], 512, 3)],
        merge=[(s["p_sb"], 1024, 0), (s["p_ssd"], 1024, 0), (s["p_rw"], 1024, 0),
               (proj, 1024, 3), (proj, 1024, 4), (proj, 1024, 5)],
    )


def _layer_fwd(x, p, nm):
    s = {}
    (s["h"],) = _rowwise(_f_rms, [(x, D_MODEL, 0)], [p["norm_g"]], [D_MODEL], bt=BT, name=nm + "rms")
    proj = s["proj"] = _mm(s["h"], p["w_in"], name=nm + "proj")
    s["y_sb_raw"], s["lt"] = _sb_fwd(proj, name=nm + "sb")
    s["xc"] = _colwise(_f_conv, proj, C_XBC, XBC_COLS, p["conv"], bc=BC, name=nm + "conv")
    s["y_ssd_raw"], s["hin"] = _ssd_fwd(s["xc"], proj, p["dt_bias"], p["a_log"], p["d_skip"], name=nm + "ssd")
    s["mixed"] = _colwise(_f_rw_mix, proj, C_RW, RW_COLS, [p["rw_mu"]], bc=BC, name=nm + "mix")
    s["w"], s["k2"], s["n"], s["b"] = _rowwise(_f_rw_pre, [(s["mixed"], 512, 1), (s["mixed"], LANES, 16)], p["rw_pre"],
                                               [512] * 4, bt=BT, name=nm + "rwpre")
    s["ys"], s["st"] = _rw_scan_fwd(s["mixed"], s["w"], s["k2"], s["n"], s["b"], name=nm + "scan")
    rows = _layer_rows(x, proj, s)
    (s["y_sb"],) = _rowwise(_f_sb_gate, rows["sb_gate"], [], [512], bt=BT, name=nm + "sbgate")
    (s["y_ssd"],) = _rowwise(_f_ssd_norm, rows["ssd_norm"], [p["ssd_norm_g"]], [1024], bt=BT, name=nm + "ssdnorm")
    (s["y_rw"],) = _rowwise(_f_rw_post, rows["rw_post"], p["rw_post"], [512], bt=BT, name=nm + "rwpost")
    s["p_sb"] = _mm(s["y_sb"], p["w_out_sb"], name=nm + "osb")
    s["p_ssd"] = _mm(s["y_ssd"], p["w_out_ssd"], name=nm + "ossd")
    s["p_rw"] = _mm(s["y_rw"], p["w_out_rw"], name=nm + "orw")
    (s["merged"],) = _rowwise(_f_merge, _layer_rows(x, proj, s)["merge"], [], [1024], bt=BT, name=nm + "merge")
    return _mm(s["merged"], p["w_o"], add=x, name=nm + "wo"), s


def _layer_bwd(x, dx_out, p, s, nm):
    g = {}
    proj = s["proj"]
    rows = _layer_rows(x, proj, s)
    g["w_o"] = _mm(s["merged"], dx_out, ta=True, name=nm + "g_wo")
    d_merged = _mm(dx_out, p["w_o"], tb=True, name=nm + "d_merged")
    dp_sb, dp_ssd, dp_rw, d_gates = _rowwise_bwd(_f_merge, rows["merge"], [], [(d_merged, 1024, 0)], bt=BT,
                                                 name=nm + "merge_b", groups=[[0], [1], [2], [3, 4, 5]])
    g["w_out_sb"] = _mm(s["y_sb"], dp_sb, ta=True, name=nm + "g_osb")
    g["w_out_ssd"] = _mm(s["y_ssd"], dp_ssd, ta=True, name=nm + "g_ossd")
    g["w_out_rw"] = _mm(s["y_rw"], dp_rw, ta=True, name=nm + "g_orw")
    dy_sb = _mm(dp_sb, p["w_out_sb"], tb=True, name=nm + "d_ysb")
    dy_ssd = _mm(dp_ssd, p["w_out_ssd"], tb=True, name=nm + "d_yssd")
    dy_rw = _mm(dp_rw, p["w_out_rw"], tb=True, name=nm + "d_yrw")
    dy_sb_raw, d_sbgate = _rowwise_bwd(_f_sb_gate, rows["sb_gate"], [], [(dy_sb, 512, 0)], bt=BT, name=nm + "sbgate_b")
    dq, dk, dv = _sb_bwd(proj, dy_sb_raw, s["lt"], name=nm + "sb_b")
    dy_ssd_raw, dz, g["ssd_norm_g"] = _rowwise_bwd(_f_ssd_norm, rows["ssd_norm"], [p["ssd_norm_g"]],
                                                   [(dy_ssd, 1024, 0)], bt=BT, name=nm + "ssdnorm_b")
    dxc, ddtr, g["dt_bias"], g["a_log"], g["d_skip"] = _ssd_bwd(
        s["xc"], proj, p["dt_bias"], p["a_log"], p["d_skip"], s["hin"], dy_ssd_raw, name=nm + "ssd_b")
    conv_out = _colwise_bwd(_f_conv, proj, C_XBC, XBC_COLS, p["conv"], dxc, bc=BC, name=nm + "conv_b")
    dxbc, g["conv"] = conv_out[0], conv_out[1:]
    dys, dr0, dk0, dv0, d_rwgate, g["rw_ln_g"], g["rw_ln_b"], g["rw_r_k"] = _rowwise_bwd(
        _f_rw_post, rows["rw_post"], p["rw_post"], [(dy_rw, 512, 0)], bt=BT, name=nm + "rwpost_b")
    dr, dw, dk2, dvv, dn, db = _rw_scan_bwd(s["mixed"], s["w"], s["k2"], s["n"], s["b"], s["st"], dys, dr0, dk0, dv0,
                                            name=nm + "scan_b")
    pre_out = _rowwise_bwd(_f_rw_pre, rows["rw_pre"], p["rw_pre"],
                           [(dw, 512, 0), (dk2, 512, 0), (dn, 512, 0), (db, 512, 0)], bt=BT, name=nm + "rwpre_b")
    dkm, dlo, g["rw_pre"] = pre_out[0], pre_out[1], pre_out[2:]
    d_mixed = jnp.concatenate([dr, dkm, dvv, d_rwgate, dlo], axis=1)
    d_slab, g["rw_mu"] = _colwise_bwd(_f_rw_mix, proj, C_RW, RW_COLS, [p["rw_mu"]], d_mixed, bc=BC, name=nm + "mix_b")
    d_proj = jnp.concatenate([dq, dk, dv, d_sbgate, dz, d_gates, d_slab, ddtr, dxbc], axis=1)
    g["w_in"] = _mm(s["h"], d_proj, ta=True, name=nm + "g_win")
    dh = _mm(d_proj, p["w_in"], tb=True, tk=512, name=nm + "d_h")
    dx, g["norm_g"] = _rowwise_bwd(_f_rms_res, rows["rms"], [p["norm_g"]], [(dh, D_MODEL, 0), (dx_out, D_MODEL, 0)],
                                   bt=BT, name=nm + "rms_b")
    return dx, g


MESH = pl.DeviceIdType.MESH
N_DEV = 8
_ANY = pl.BlockSpec(memory_space=pl.ANY)
_CHIP_SEMS = [pltpu.SemaphoreType.DMA((3,)), pltpu.SemaphoreType.DMA((3,)), pltpu.SemaphoreType.DMA]


def _here():
    x, y, c = lax.axis_index("x"), lax.axis_index("y"), lax.axis_index("c")
    return x, y, c, [(1 - x, y), (x, 1 - y), (1 - x, 1 - y)]


def _chip_exchange(src, *, per_dest, name):
    shape = src.shape[-2:]

    def body(src_ref, out_ref, send_sems, recv_sems, local_sem):
        x, y, c, chips = _here()
        me = 2 * x + y
        pick = (lambda q: src_ref.at[q]) if per_dest else (lambda q: src_ref)
        own = pltpu.make_async_copy(pick(me), out_ref.at[me], local_sem)
        own.start()
        sends = [pltpu.make_async_remote_copy(pick(2 * px + py), out_ref.at[me], send_sems.at[j], recv_sems.at[j],
                                              device_id=(px, py, c), device_id_type=MESH)
                 for j, (px, py) in enumerate(chips)]
        for cp in sends:
            cp.start()
        for j, (px, py) in enumerate(chips):
            pltpu.make_async_remote_copy(pick(me), out_ref.at[2 * px + py], send_sems.at[j], recv_sems.at[j],
                                         device_id=(px, py, c), device_id_type=MESH).wait_recv()
        for cp in sends:
            cp.wait_send()
        own.wait()

    return pl.pallas_call(
        body, name=name, in_specs=[_ANY], out_specs=_ANY,
        out_shape=jax.ShapeDtypeStruct((4,) + shape, src.dtype), scratch_shapes=_CHIP_SEMS,
    )(src)


def _sibling_send_other_half(src, *, name):
    def body(src_ref, out_ref, send_sem, recv_sem):
        x, y, c, _ = _here()
        cp = pltpu.make_async_remote_copy(src_ref.at[1 - c], out_ref, send_sem, recv_sem,
                                          device_id=(x, y, 1 - c), device_id_type=MESH)
        cp.start()
        cp.wait()

    return pl.pallas_call(
        body, name=name, in_specs=[_ANY], out_specs=_ANY,
        out_shape=jax.ShapeDtypeStruct(src.shape[1:], src.dtype),
        scratch_shapes=[pltpu.SemaphoreType.DMA, pltpu.SemaphoreType.DMA],
    )(src)


def _sibling_join(half, *, name):
    def body(src_ref, out_ref, send_sem, recv_sem, local_sem):
        x, y, c, _ = _here()
        own = pltpu.make_async_copy(src_ref, out_ref.at[c], local_sem)
        own.start()
        cp = pltpu.make_async_remote_copy(src_ref, out_ref.at[c], send_sem, recv_sem,
                                          device_id=(x, y, 1 - c), device_id_type=MESH)
        cp.start()
        pltpu.make_async_remote_copy(src_ref, out_ref.at[1 - c], send_sem, recv_sem,
                                     device_id=(x, y, 1 - c), device_id_type=MESH).wait_recv()
        cp.wait_send()
        own.wait()

    return pl.pallas_call(
        body, name=name, in_specs=[_ANY], out_specs=_ANY,
        out_shape=jax.ShapeDtypeStruct((2,) + half.shape, half.dtype),
        scratch_shapes=[pltpu.SemaphoreType.DMA, pltpu.SemaphoreType.DMA, pltpu.SemaphoreType.DMA],
    )(half)


def _allgather_small(v, *, reduce, name):
    r = v.shape[0]

    def body(v_ref, out_ref, *rest):
        send_sems, recv_sems, local_sem = rest[-3:]
        x, y, c, chips = _here()
        me, sibling = (x, y, c), (x, y, 1 - c)

        def slot(px, py, pc):
            return out_ref.at[4 * px + 2 * py + pc]

        def copy(k, block, to, src=None):
            return pltpu.make_async_remote_copy(
                src_ref=slot(*block) if src is None else src, dst_ref=slot(*block),
                send_sem=send_sems.at[k], recv_sem=recv_sems.at[k], device_id=to, device_id_type=MESH)

        mine = pltpu.make_async_copy(v_ref, slot(*me), local_sem)
        mine.start()
        first = [copy(0, me, sibling, src=v_ref)]
        first += [copy(1 + j, me, (*chip, c), src=v_ref) for j, chip in enumerate(chips)]
        for cp in first:
            cp.start()
        passed = [copy(4 + j, (*chip, c), sibling) for j, chip in enumerate(chips)]
        for j, chip in enumerate(chips):
            copy(1 + j, (*chip, c), me).wait_recv()
            passed[j].start()
        copy(0, sibling, me).wait_recv()
        for j, chip in enumerate(chips):
            copy(4 + j, (*chip, 1 - c), me).wait_recv()
        for cp in first + passed:
            cp.wait_send()
        mine.wait()
        if reduce:
            total = out_ref[0]
            for d in range(1, N_DEV):
                total = total + out_ref[d]
            rest[0][...] = total

    vm = pl.BlockSpec(memory_space=pltpu.VMEM)
    out_shape = [jax.ShapeDtypeStruct((N_DEV, r, LANES), F32)] + ([jax.ShapeDtypeStruct((r, LANES), F32)] if reduce else [])
    return pl.pallas_call(
        body, name=name, in_specs=[vm], out_specs=[vm] * len(out_shape), out_shape=out_shape,
        scratch_shapes=[pltpu.SemaphoreType.DMA((7,)), pltpu.SemaphoreType.DMA((7,)), pltpu.SemaphoreType.DMA],
        compiler_params=pltpu.CompilerParams(vmem_limit_bytes=VMEM_LIMIT),
    )(v)


REDUCE_ROWS = 1952


def _add_halves(mine2, other, c_idx, *, name):
    _, nq, r, _ = mine2.shape

    def body(c_ref, a_ref, b_ref, o_ref):
        o_ref[...] = (a_ref[0] + b_ref[...]).astype(o_ref.dtype)

    blk = pl.BlockSpec((1, REDUCE_ROWS, LANES), lambda q, i, c_ref: (q, i, 0))
    return pl.pallas_call(
        body, name=name,
        grid_spec=pltpu.PrefetchScalarGridSpec(
            num_scalar_prefetch=1, grid=(nq, r // REDUCE_ROWS),
            in_specs=[pl.BlockSpec((1, 1, REDUCE_ROWS, LANES), lambda q, i, c_ref: (c_ref[0], q, i, 0)), blk],
            out_specs=blk),
        out_shape=jax.ShapeDtypeStruct((nq, r, LANES), BF16),
        compiler_params=_params(("parallel", "parallel")),
    )(c_idx, mine2, other)


def _sum_chips(parts, *, name):
    _, r, _ = parts.shape

    def body(p_ref, o_ref):
        total = p_ref[0].astype(F32)
        for q in range(1, 4):
            total = total + p_ref[q].astype(F32)
        o_ref[...] = total

    return pl.pallas_call(
        body, name=name, grid=(r // REDUCE_ROWS,),
        in_specs=[pl.BlockSpec((4, REDUCE_ROWS, LANES), lambda i: (0, i, 0))],
        out_specs=pl.BlockSpec((REDUCE_ROWS, LANES), lambda i: (i, 0)),
        out_shape=jax.ShapeDtypeStruct((r, LANES), F32),
        compiler_params=_params(("parallel",)),
    )(parts)


BIG = ("w_in", "w_out_sb", "w_out_ssd", "w_out_rw", "w_o")
BIG_AXIS = {"w_in": 2, "w_out_sb": 2, "w_out_ssd": 1, "w_out_rw": 2, "w_o": 1}
SMALL_SHARDED = {"conv_w": 320, "rw_w_up": 128, "rw_a_up": 128}
SMALL = ("norm_g", "conv_w", "conv_b", "dt_bias", "a_log", "d_skip", "ssd_norm_g", "rw_mu", "rw_w0", "rw_w_up",
         "rw_a0", "rw_a_up", "rw_k_k", "rw_k_a", "rw_r_k", "rw_ln_g", "rw_ln_b", "final_g")


def _rows_of(a):
    flat = a.reshape(-1)
    pad = (-flat.shape[0]) % LANES
    return jnp.pad(flat, (0, pad)).reshape(-1, LANES)


def _pack_rows(arrays, multiple=8):
    rows = jnp.concatenate([_rows_of(a) for a in arrays], axis=0)
    pad = (-rows.shape[0]) % multiple
    return jnp.pad(rows, ((0, pad), (0, 0)))


def _unpack_rows(rows, shapes):
    out, off = [], 0
    for shp in shapes:
        n = 1
        for d in shp:
            n *= d
        nr = -(-n // LANES)
        out.append(rows[off:off + nr].reshape(-1)[:n].reshape(shp))
        off += nr
    return out


def _pad_cols(w):
    z = jnp.zeros(w.shape[:-1] + (N_PAD - N_IN,), w.dtype)
    return jnp.concatenate([w[..., 0:3072], w[..., 6544:9616], w[..., 4368:6544], w[..., 4352:4368], z,
                            w[..., 3072:4352]], axis=-1)


def _unpad_cols(g):
    return jnp.concatenate([g[..., 0:3072], g[..., 8448:9728], g[..., 8320:8336], g[..., 6144:8320],
                            g[..., 3072:6144]], axis=-1)


def _split_chips(a, axis):
    n = a.shape[axis] // 4
    return jnp.stack([lax.slice_in_dim(a, q * n, (q + 1) * n, axis=axis) for q in range(4)])


def _join_chips(a, axis):
    return jnp.concatenate([a[q] for q in range(4)], axis=axis)


def kernel(x, norm_g, w_in, conv_w, conv_b, dt_bias, a_log, d_skip, ssd_norm_g, rw_mu, rw_w0, rw_w_up, rw_a0, rw_a_up, rw_k_k, rw_k_a, rw_r_k, rw_ln_g, rw_ln_b, w_out_sb, w_out_ssd, w_out_rw, w_o, final_g, loss_target, m_norm_g, m_w_in, m_conv_w, m_conv_b, m_dt_bias, m_a_log, m_d_skip, m_ssd_norm_g, m_rw_mu, m_rw_w0, m_rw_w_up, m_rw_a0, m_rw_a_up, m_rw_k_k, m_rw_k_a, m_rw_r_k, m_rw_ln_g, m_rw_ln_b, m_w_out_sb, m_w_out_ssd, m_w_out_rw, m_w_o, m_final_g, v_norm_g, v_w_in, v_conv_w, v_conv_b, v_dt_bias, v_a_log, v_d_skip, v_ssd_norm_g, v_rw_mu, v_rw_w0, v_rw_w_up, v_rw_a0, v_rw_a_up, v_rw_k_k, v_rw_k_a, v_rw_r_k, v_rw_ln_g, v_rw_ln_b, v_w_out_sb, v_w_out_ssd, v_w_out_rw, v_w_o, v_final_g):
    names = ("norm_g", "w_in", "conv_w", "conv_b", "dt_bias", "a_log", "d_skip", "ssd_norm_g", "rw_mu", "rw_w0",
             "rw_w_up", "rw_a0", "rw_a_up", "rw_k_k", "rw_k_a", "rw_r_k", "rw_ln_g", "rw_ln_b", "w_out_sb",
             "w_out_ssd", "w_out_rw", "w_o", "final_g")
    w_loc = dict(zip(names, (norm_g, w_in, conv_w, conv_b, dt_bias, a_log, d_skip, ssd_norm_g, rw_mu, rw_w0, rw_w_up,
                             rw_a0, rw_a_up, rw_k_k, rw_k_a, rw_r_k, rw_ln_g, rw_ln_b, w_out_sb, w_out_ssd, w_out_rw,
                             w_o, final_g)))
    m_loc = dict(zip(names, (m_norm_g, m_w_in, m_conv_w, m_conv_b, m_dt_bias, m_a_log, m_d_skip, m_ssd_norm_g,
                             m_rw_mu, m_rw_w0, m_rw_w_up, m_rw_a0, m_rw_a_up, m_rw_k_k, m_rw_k_a, m_rw_r_k,
                             m_rw_ln_g, m_rw_ln_b, m_w_out_sb, m_w_out_ssd, m_w_out_rw, m_w_o, m_final_g)))
    v_loc = dict(zip(names, (v_norm_g, v_w_in, v_conv_w, v_conv_b, v_dt_bias, v_a_log, v_d_skip, v_ssd_norm_g,
                             v_rw_mu, v_rw_w0, v_rw_w_up, v_rw_a0, v_rw_a_up, v_rw_k_k, v_rw_k_a, v_rw_r_k,
                             v_rw_ln_g, v_rw_ln_b, v_w_out_sb, v_w_out_ssd, v_w_out_rw, v_w_o, v_final_g)))
    chip = 2 * lax.axis_index("x") + lax.axis_index("y")
    core = lax.axis_index("c")

    big_shapes = [w_loc[n].shape for n in BIG]
    pack = _pack_rows([w_loc[n].astype(BF16) for n in BIG], multiple=16)
    got = _chip_exchange(pack, per_dest=False, name="gather_big")
    full = {}
    per_chip = [_unpack_rows(got[q], big_shapes) for q in range(4)]
    for i, n in enumerate(BIG):
        full[n] = jnp.concatenate([per_chip[q][i] for q in range(4)], axis=BIG_AXIS[n])
    full["w_in"] = _pad_cols(full["w_in"])
    sm_names = tuple(SMALL_SHARDED)
    sm_shapes = [w_loc[n].shape for n in sm_names]
    (got_sm,) = _allgather_small(_pack_rows([w_loc[n] for n in sm_names]), reduce=False, name="gather_small")
    per_chip = [_unpack_rows(got_sm[4 * (q // 2) + 2 * (q % 2)], sm_shapes) for q in range(4)]
    for i, n in enumerate(sm_names):
        full[n] = jnp.concatenate([per_chip[q][i] for q in range(4)], axis=-1)

    def pad16(a):
        return jnp.zeros((1, LANES), F32).at[0, :SSD_HEADS].set(a)

    def layer_params(i):
        row = lambda n: w_loc[n][i].reshape(1, -1)
        cw = full["conv_w"][i]
        return dict(
            norm_g=row("norm_g"), w_in=full["w_in"][i], conv=[cw[k][None] for k in range(4)] + [row("conv_b")],
            dt_bias=pad16(dt_bias[i]), a_log=pad16(a_log[i]), d_skip=pad16(d_skip[i]),
            ssd_norm_g=row("ssd_norm_g"), rw_mu=row("rw_mu"),
            rw_pre=[row("rw_w0"), jnp.zeros((LANES, 512), F32).at[:HEAD].set(full["rw_w_up"][i]), row("rw_a0"),
                    jnp.zeros((LANES, 512), F32).at[HEAD:].set(full["rw_a_up"][i]), row("rw_k_k"), row("rw_k_a")],
            rw_post=[row("rw_ln_g"), row("rw_ln_b"), row("rw_r_k")],
            w_out_sb=full["w_out_sb"][i], w_out_ssd=full["w_out_ssd"][i], w_out_rw=full["w_out_rw"][i],
            w_o=full["w_o"][i])

    params = [layer_params(i) for i in range(DEPTH)]
    xs, saved = [x[0]], []
    for i in range(DEPTH):
        nxt, s = _layer_fwd(xs[-1], params[i], f"l{i}_")
        xs.append(nxt)
        saved.append(s)
    dx, loss_row, g_final = _final(xs[-1], final_g.reshape(1, -1), loss_target[0], bt=BT, name="final")
    grads = [None] * DEPTH
    for i in reversed(range(DEPTH)):
        dx, grads[i] = _layer_bwd(xs[i], dx, params[i], saved[i], f"l{i}_")

    def stacked(fn):
        return jnp.stack([fn(grads[i]) for i in range(DEPTH)])

    g_loc = {
        "norm_g": stacked(lambda g: g["norm_g"][0]),
        "w_in": stacked(lambda g: _unpad_cols(g["w_in"])),
        "conv_w": stacked(lambda g: jnp.concatenate(g["conv"][:4], axis=0)),
        "conv_b": stacked(lambda g: g["conv"][4][0]),
        "dt_bias": stacked(lambda g: g["dt_bias"][0, :SSD_HEADS]),
        "a_log": stacked(lambda g: g["a_log"][0, :SSD_HEADS]),
        "d_skip": stacked(lambda g: g["d_skip"][0, :SSD_HEADS]),
        "ssd_norm_g": stacked(lambda g: g["ssd_norm_g"][0]),
        "rw_mu": stacked(lambda g: g["rw_mu"][0]),
        "rw_w0": stacked(lambda g: g["rw_pre"][0][0]),
        "rw_w_up": stacked(lambda g: g["rw_pre"][1][:HEAD]),
        "rw_a0": stacked(lambda g: g["rw_pre"][2][0]),
        "rw_a_up": stacked(lambda g: g["rw_pre"][3][HEAD:]),
        "rw_k_k": stacked(lambda g: g["rw_pre"][4][0]),
        "rw_k_a": stacked(lambda g: g["rw_pre"][5][0]),
        "rw_r_k": stacked(lambda g: g["rw_r_k"].reshape(8, HEAD)),
        "rw_ln_g": stacked(lambda g: g["rw_ln_g"][0]),
        "rw_ln_b": stacked(lambda g: g["rw_ln_b"][0]),
        "w_out_sb": stacked(lambda g: g["w_out_sb"]),
        "w_out_ssd": stacked(lambda g: g["w_out_ssd"]),
        "w_out_rw": stacked(lambda g: g["w_out_rw"]),
        "w_o": stacked(lambda g: g["w_o"]),
        "final_g": g_final[0],
    }

    send = jnp.stack([_pack_rows([_split_chips(g_loc[n], BIG_AXIS[n])[q] for n in BIG], multiple=16) for q in range(4)])
    half = send.shape[1] // 2
    send = send.reshape(4, 2, half, LANES).transpose(1, 0, 2, 3)
    other = _sibling_send_other_half(send, name="reduce_sibling")
    part = _add_halves(send, other, core.reshape(1).astype(jnp.int32), name="reduce_add")
    parts = _chip_exchange(part, per_dest=True, name="reduce_chips")
    total = _sibling_join(_sum_chips(parts, name="reduce_sum"), name="reduce_join")
    g_out = dict(zip(BIG, _unpack_rows(total.reshape(2 * half, LANES), big_shapes)))

    sm_all = SMALL + ("loss",)
    sm_full_shapes = [g_loc[n].shape for n in SMALL] + [(1,)]
    _, summed = _allgather_small(_pack_rows([g_loc[n] for n in SMALL] + [loss_row[0, :1]]), reduce=True, name="reduce_small")
    sm = dict(zip(sm_all, _unpack_rows(summed, sm_full_shapes)))
    for n in SMALL:
        g_out[n] = sm[n]
    for n, wd in SMALL_SHARDED.items():
        g_out[n] = lax.dynamic_slice_in_dim(sm[n], chip * wd, wd, axis=sm[n].ndim - 1)
    loss = sm["loss"][0]

    upd = {n: _adamw(w_loc[n], g_out[n], m_loc[n], v_loc[n], name="adamw_" + n) for n in names}
    return (loss, dx[None], *[g_out[n] for n in names], *[upd[n][0] for n in names],
            *[upd[n][1] for n in names], *[upd[n][2] for n in names])
```

```python
import functools

import jax
import jax.numpy as jnp
from jax import lax
from jax.experimental import pallas as pl
from jax.experimental.pallas import tpu as pltpu

F32 = jnp.float32
BF16 = jnp.bfloat16

D_MODEL = 1024
DEPTH = 2
HEAD = 64
LANES = 128
CHUNK = 128
RMS_EPS = 1e-6
GN_EPS = 64e-5
VMEM_LIMIT = 56 * 1024 * 1024

N_IN = 9616
N_PAD = 9728
C_SB, C_Z, C_GATES, C_RW, C_LO, C_DT, C_XBC = 0, 2048, 3072, 6144, 8192, 8320, 8448
RW_COLS = 2176
XBC_COLS = 1280

ADAM_LR, ADAM_B1, ADAM_B2, ADAM_EPS, ADAM_WD, ADAM_STEP = 0.001, 0.9, 0.999, 1e-08, 0.01, 10


def _params(sem=None):
    return pltpu.CompilerParams(dimension_semantics=sem, vmem_limit_bytes=VMEM_LIMIT)


@jax.custom_vjp
def _sigmoid(x):
    return 1.0 / (1.0 + jnp.exp(-x))


def _sigmoid_fwd(x):
    s = _sigmoid(x)
    return s, s


def _sigmoid_bwd(s, g):
    return (g * s * (1.0 - s),)


_sigmoid.defvjp(_sigmoid_fwd, _sigmoid_bwd)


@jax.custom_vjp
def _silu(x):
    return x * _sigmoid(x)


def _silu_fwd(x):
    s = _sigmoid(x)
    return x * s, (x, s)


def _silu_bwd(res, g):
    x, s = res
    return (g * (s + x * s * (1.0 - s)),)


_silu.defvjp(_silu_fwd, _silu_bwd)


@jax.custom_vjp
def _softplus(x):
    return jnp.maximum(x, 0.0) + jnp.log(1.0 + jnp.exp(-jnp.abs(x)))


def _softplus_fwd(x):
    return _softplus(x), x


def _softplus_bwd(x, g):
    return (g * _sigmoid(x),)


_softplus.defvjp(_softplus_fwd, _softplus_bwd)


def _dot(a, b, dims):
    return lax.dot_general(a.astype(BF16), b.astype(BF16), (dims, ((), ())), preferred_element_type=F32)


def _dot_nn(a, b):
    return _dot(a, b, ((1,), (0,)))


def _dot_nt(a, b):
    return _dot(a, b, ((1,), (1,)))


def _dot_tn(a, b):
    return _dot(a, b, ((0,), (0,)))


@jax.custom_vjp
def _bdot(a, b):
    return _dot_nn(a, b)


def _bdot_fwd(a, b):
    return _dot_nn(a, b), (a, b)


def _bdot_bwd(res, g):
    a, b = res
    return _dot_nt(g, b), _dot_tn(a, g)


_bdot.defvjp(_bdot_fwd, _bdot_bwd)


def _split2(x):
    hi = x.astype(BF16)
    lo = (x - hi.astype(F32)).astype(BF16)
    return hi, lo


_NT = (((1,), (1,)), ((), ()))
_NN = (((1,), (0,)), ((), ()))
_TN = (((0,), (0,)), ((), ()))


def _dot2(x, m, dn=_NN):
    hi, lo = _split2(x)
    return (lax.dot_general(hi, m, dn, preferred_element_type=F32)
            + lax.dot_general(lo, m, dn, preferred_element_type=F32))


def _dot2_tn(x, m):
    return _dot2(x, m, _TN)


def _seg_matrix(n):
    r = lax.broadcasted_iota(jnp.int32, (n, n), 0) // HEAD
    c = lax.broadcasted_iota(jnp.int32, (n, n), 1) // HEAD
    return (r == c).astype(BF16)


@jax.custom_vjp
def _segsum2(x, seg):
    return _dot2(x, seg)


def _segsum2_fwd(x, seg):
    return _dot2(x, seg), seg


def _segsum2_bwd(seg, g):
    return _dot2(g, seg), jnp.zeros_like(seg)


_segsum2.defvjp(_segsum2_fwd, _segsum2_bwd)


def _make_segsum(seg):
    return lambda x: _segsum2(x, seg)


def _shift_down_raw(x, k):
    row = lax.broadcasted_iota(jnp.int32, x.shape, 0)
    return jnp.where(row >= k, pltpu.roll(x, k, 0), 0.0)


def _shift_up_raw(x, k):
    t = x.shape[0]
    row = lax.broadcasted_iota(jnp.int32, x.shape, 0)
    return jnp.where(row < t - k, pltpu.roll(x, t - k, 0), 0.0)


@functools.partial(jax.custom_vjp, nondiff_argnums=(1,))
def _shift_down(x, k):
    return _shift_down_raw(x, k)


def _shift_down_fwd(x, k):
    return _shift_down_raw(x, k), None


def _shift_down_bwd(k, _, g):
    return (_shift_up_raw(g, k),)


_shift_down.defvjp(_shift_down_fwd, _shift_down_bwd)


def _mm(a, b, *, name, ta=False, tb=False, add=None, out_dtype=F32, tm=512, tn=512, tk=None):
    m, k = (a.shape[1], a.shape[0]) if ta else a.shape
    n = b.shape[0] if tb else b.shape[1]
    tm, tn = min(tm, m), min(tn, n)
    tk = k if tk is None else tk
    nk = k // tk
    assert m % tm == 0 and n % tn == 0 and k % tk == 0
    dims = ((0 if ta else 1,), (1 if tb else 0,))

    def body(a_ref, b_ref, *refs):
        o_ref, acc_ref = refs[-2:]
        p = _dot(a_ref[...], b_ref[...], dims)

        def emit(total):
            if add is not None:
                total = total + refs[0][...]
            o_ref[...] = total.astype(o_ref.dtype)

        if nk == 1:
            emit(p)
        else:
            kk = pl.program_id(2)

            @pl.when(kk == 0)
            def _():
                acc_ref[...] = p

            @pl.when(kk > 0)
            def _():
                acc_ref[...] += p

            @pl.when(kk == nk - 1)
            def _():
                emit(acc_ref[...])

    a_spec = pl.BlockSpec((tk, tm), lambda i, j, kk: (kk, i)) if ta else pl.BlockSpec((tm, tk), lambda i, j, kk: (i, kk))
    b_spec = pl.BlockSpec((tn, tk), lambda i, j, kk: (j, kk)) if tb else pl.BlockSpec((tk, tn), lambda i, j, kk: (kk, j))
    o_spec = pl.BlockSpec((tm, tn), lambda i, j, kk: (i, j))
    return pl.pallas_call(
        body, name=name, grid=(m // tm, n // tn, nk),
        in_specs=[a_spec, b_spec] + ([o_spec] if add is not None else []), out_specs=o_spec,
        out_shape=jax.ShapeDtypeStruct((m, n), out_dtype),
        scratch_shapes=[pltpu.VMEM((tm, tn) if nk > 1 else (8, LANES), F32)],
        compiler_params=_params(("parallel", "parallel", "arbitrary")),
    )(a, b, *([add] if add is not None else []))


def _row_specs(rows, bt):
    return [pl.BlockSpec((bt, w), functools.partial(lambda i, c: (i, c), c=c)) for _, w, c in rows]


def _full_spec(p):
    return pl.BlockSpec(p.shape, functools.partial(lambda i, nd: (0,) * nd, nd=p.ndim))


def _rowwise(f, rows, pars, out_widths, *, bt, name, acc_widths=()):
    t = rows[0][0].shape[0]
    nr, npar, no, na = len(rows), len(pars), len(out_widths), len(acc_widths)

    def body(*refs):
        vals = [r[...] for r in refs[:nr + npar]]
        outs = f(*vals)
        for o_ref, o in zip(refs[nr + npar:nr + npar + no], outs[:no]):
            o_ref[...] = o.astype(o_ref.dtype)
        if na:
            first = pl.program_id(0) == 0
            for a_ref, a in zip(refs[nr + npar + no:], outs[no:]):
                @pl.when(first)
                def _():
                    a_ref[...] = jnp.zeros_like(a_ref)
                a_ref[...] += a

    return pl.pallas_call(
        body, name=name, grid=(t // bt,),
        in_specs=_row_specs(rows, bt) + [_full_spec(p) for p in pars],
        out_specs=[pl.BlockSpec((bt, w), lambda i: (i, 0)) for w in out_widths]
        + [pl.BlockSpec((1, w), lambda i: (0, 0)) for w in acc_widths],
        out_shape=[jax.ShapeDtypeStruct((t, w), F32) for w in out_widths]
        + [jax.ShapeDtypeStruct((1, w), F32) for w in acc_widths],
        compiler_params=_params(("arbitrary",)),
    )(*[r[0] for r in rows], *pars)


def _rowwise_bwd(f, rows, pars, douts, *, bt, name, groups=None):
    t = rows[0][0].shape[0]
    nr, npar, nd = len(rows), len(pars), len(douts)
    groups = [[i] for i in range(nr)] if groups is None else groups
    widths = [r[1] for r in rows]

    def body(*refs):
        vals = [r[...] for r in refs[:nr + npar]]
        cts = tuple(r[...] for r in refs[nr + npar:nr + npar + nd])
        _, vjp = jax.vjp(lambda *a: tuple(f(*a)), *vals)
        grads = vjp(cts)
        out_refs = refs[nr + npar + nd:]
        for g_ref, grp in zip(out_refs[:len(groups)], groups):
            off = 0
            for i in grp:
                g_ref[:, off:off + widths[i]] = grads[i]
                off += widths[i]
        first = pl.program_id(0) == 0
        for p_ref, g in zip(out_refs[len(groups):], grads[nr:]):
            @pl.when(first)
            def _():
                p_ref[...] = jnp.zeros_like(p_ref)
            p_ref[...] += g

    gw = [sum(widths[i] for i in grp) for grp in groups]
    return pl.pallas_call(
        body, name=name, grid=(t // bt,),
        in_specs=_row_specs(rows, bt) + [_full_spec(p) for p in pars] + _row_specs(douts, bt),
        out_specs=[pl.BlockSpec((bt, w), lambda i: (i, 0)) for w in gw] + [_full_spec(p) for p in pars],
        out_shape=[jax.ShapeDtypeStruct((t, w), F32) for w in gw] + [jax.ShapeDtypeStruct(p.shape, F32) for p in pars],
        compiler_params=_params(("arbitrary",)),
    )(*[r[0] for r in rows], *pars, *[d[0] for d in douts])


def _colwise(f, x, c0, ncols, pars, *, bc, name):
    t = x.shape[0]

    def body(x_ref, *refs):
        o_ref = refs[-1]
        o_ref[...] = f(x_ref[...], *[r[...] for r in refs[:-1]])

    return pl.pallas_call(
        body, name=name, grid=(ncols // bc,),
        in_specs=[pl.BlockSpec((t, bc), lambda j: (0, j + c0 // bc))]
        + [pl.BlockSpec((p.shape[0], bc), lambda j: (0, j)) for p in pars],
        out_specs=pl.BlockSpec((t, bc), lambda j: (0, j)),
        out_shape=jax.ShapeDtypeStruct((t, ncols), F32),
        compiler_params=_params(("parallel",)),
    )(x, *pars)


def _colwise_bwd(f, x, c0, ncols, pars, dout, *, bc, name):
    t = x.shape[0]
    npar = len(pars)

    def body(x_ref, *refs):
        vals = [x_ref[...]] + [r[...] for r in refs[:npar]]
        _, vjp = jax.vjp(f, *vals)
        grads = vjp(refs[npar][...])
        for g_ref, g in zip(refs[npar + 1:], grads):
            g_ref[...] = g

    return pl.pallas_call(
        body, name=name, grid=(ncols // bc,),
        in_specs=[pl.BlockSpec((t, bc), lambda j: (0, j + c0 // bc))]
        + [pl.BlockSpec((p.shape[0], bc), lambda j: (0, j)) for p in pars]
        + [pl.BlockSpec((t, bc), lambda j: (0, j))],
        out_specs=[pl.BlockSpec((t, bc), lambda j: (0, j))]
        + [pl.BlockSpec((p.shape[0], bc), lambda j: (0, j)) for p in pars],
        out_shape=[jax.ShapeDtypeStruct((t, ncols), F32)] + [jax.ShapeDtypeStruct(p.shape, F32) for p in pars],
        compiler_params=_params(("parallel",)),
    )(x, *pars, dout)


def _f_rms(x, g):
    return (x * lax.rsqrt(jnp.mean(x * x, axis=-1, keepdims=True) + RMS_EPS) * g,)


def _f_sb_gate(y, gate):
    return (y * _silu(gate),)


def _f_ssd_norm(y, z, g):
    u = y * _silu(z)
    return (u * lax.rsqrt(jnp.mean(u * u, axis=-1, keepdims=True) + RMS_EPS) * g,)


def _f_merge(p_sb, p_ssd, p_rw, g_sb, g_ssd, g_rw):
    return (_sigmoid(g_sb) * p_sb + _sigmoid(g_ssd) * p_ssd + _sigmoid(g_rw) * p_rw,)


def _f_rw_pre(k, lo, w0, w_up, a0, a_up, k_k, k_a):
    segsum = _make_segsum(_seg_matrix(k.shape[1]))
    lane = lax.broadcasted_iota(jnp.int32, lo.shape, 1)
    w_lo = jnp.where(lane < HEAD, jnp.tanh(lo), 0.0)
    a_lo = jnp.where(lane >= HEAD, lo, 0.0)
    w = -_softplus(-(w0 + _bdot(w_lo, w_up))) - 0.5
    decay = jnp.exp(-jnp.exp(w))
    a = _sigmoid(a0 + _bdot(a_lo, a_up))
    kk = k * k_k
    kk = kk / jnp.maximum(jnp.sqrt(segsum(kk * kk)), 1e-12)
    return decay, k * (1.0 + (a - 1.0) * k_a), -kk, kk * a


def _f_rw_post(y, r, k2, v, gate, ln_g, ln_b, r_k):
    segsum = _make_segsum(_seg_matrix(y.shape[1]))
    yc = y - segsum(y) * (1.0 / HEAD)
    var = segsum(yc * yc) * (1.0 / HEAD)
    yn = yc * lax.rsqrt(var + GN_EPS) * ln_g + ln_b
    return ((yn + segsum(r * k2 * r_k) * v) * _silu(gate),)


def _f_rw_mix(slab, mu):
    return slab + (_shift_down(slab, 1) - slab) * mu


def _f_conv(x, w0, w1, w2, w3, b):
    acc = x * w3 + b
    for i, w in enumerate((w0, w1, w2)):
        acc = acc + _shift_down(x, 3 - i) * w
    return _silu(acc)


def _log_sigmoid(z):
    return jnp.minimum(z, 0.0) - jnp.log(1.0 + jnp.exp(-jnp.abs(z)))


def _prefix_matrix(kind):
    j = lax.broadcasted_iota(jnp.int32, (CHUNK, 2 * CHUNK), 0)
    s = lax.broadcasted_iota(jnp.int32, (CHUNK, 2 * CHUNK), 1)
    tri = {"gt": j > s, "le": j <= s, "lt": j < s}[kind]
    return (tri | (s >= CHUNK)).astype(BF16)


def _sb_specs(t):
    q = pl.BlockSpec((CHUNK, LANES), lambda j, i: (i, j))
    k = pl.BlockSpec((t, LANES), lambda j, i: (0, 4 + j))
    v = pl.BlockSpec((t, LANES), lambda j, i: (0, 8 + j))
    return q, k, v


def _sb_fwd(proj, *, name):
    t = proj.shape[0]
    scale = HEAD ** -0.5

    def body(q_ref, k_ref, v_ref, y_ref, lt_ref):
        i = pl.program_id(1)
        lane = lax.broadcasted_iota(jnp.int32, (CHUNK, LANES), 1)
        diff = (lax.broadcasted_iota(jnp.int32, (CHUNK, CHUNK), 1)
                - lax.broadcasted_iota(jnp.int32, (CHUNK, CHUNK), 0))
        m_f = _prefix_matrix("gt")
        q = q_ref[...] * scale
        qh = [jnp.where((lane // HEAD) == h, q, 0.0).astype(BF16) for h in (0, 1)]

        def step(it, carry):
            off = pl.multiple_of((i - it) * CHUNK, CHUNK)
            kblk = k_ref[pl.ds(off, CHUNK), :].astype(BF16)
            vblk = v_ref[pl.ds(off, CHUNK), :].astype(BF16)
            mask = diff < it * CHUNK
            new = []
            for h in (0, 1):
                c, acc = carry[2 * h], carry[2 * h + 1]
                z = lax.dot_general(qh[h], kblk, _NT, preferred_element_type=F32)
                lb = _log_sigmoid(z)
                w2 = _dot2(jnp.where(mask, lb - z, 0.0), m_f)
                att = jnp.where(mask, jnp.exp(lb + c + w2[:, :CHUNK]), 0.0)
                acc = acc + lax.dot_general(att.astype(BF16), vblk, _NN, preferred_element_type=F32)
                new += [c + w2[:, CHUNK:], acc]
            return tuple(new)

        zero = jnp.zeros((CHUNK, LANES), F32)
        c_a, acc_a, c_b, acc_b = lax.fori_loop(0, i + 1, step, (zero, zero, zero, zero))
        y_ref[...] = jnp.where(lane < HEAD, acc_a, acc_b)
        lt_ref[0] = c_a
        lt_ref[1] = c_b

    return pl.pallas_call(
        body, name=name, grid=(4, t // CHUNK),
        in_specs=list(_sb_specs(t)),
        out_specs=[pl.BlockSpec((CHUNK, LANES), lambda j, i: (i, j)),
                   pl.BlockSpec((2, CHUNK, LANES), lambda j, i: (j, i, 0))],
        out_shape=[jax.ShapeDtypeStruct((t, 4 * LANES), F32), jax.ShapeDtypeStruct((8, t, LANES), F32)],
        compiler_params=_params(("parallel", "arbitrary")),
    )(proj, proj, proj)


def _sb_bwd(proj, dy, lt, *, name):
    t = proj.shape[0]
    scale = HEAD ** -0.5

    def body(q_ref, k_ref, v_ref, dy_ref, lt_ref, dq_ref, dk_ref, dv_ref):
        i = pl.program_id(1)

        @pl.when(i == 0)
        def _():
            dk_ref[...] = jnp.zeros_like(dk_ref)
            dv_ref[...] = jnp.zeros_like(dv_ref)

        lane = lax.broadcasted_iota(jnp.int32, (CHUNK, LANES), 1)
        diff = (lax.broadcasted_iota(jnp.int32, (CHUNK, CHUNK), 1)
                - lax.broadcasted_iota(jnp.int32, (CHUNK, CHUNK), 0))
        m_le, m_lt = _prefix_matrix("le"), _prefix_matrix("lt")
        q = q_ref[...] * scale
        dy_blk = dy_ref[...]
        qh = [jnp.where((lane // HEAD) == h, q, 0.0).astype(BF16) for h in (0, 1)]
        doh = [jnp.where((lane // HEAD) == h, dy_blk, 0.0).astype(BF16) for h in (0, 1)]
        lth = [lt_ref[0], lt_ref[1]]

        def step(kb, carry):
            off = pl.multiple_of(kb * CHUNK, CHUNK)
            kblk = k_ref[pl.ds(off, CHUNK), :].astype(BF16)
            vblk = v_ref[pl.ds(off, CHUNK), :].astype(BF16)
            mask = diff < (i - kb) * CHUNK
            new = []
            dk_acc = jnp.zeros((CHUNK, LANES), F32)
            dv_acc = jnp.zeros((CHUNK, LANES), F32)
            for h in (0, 1):
                cp, cg, dq = carry[3 * h:3 * h + 3]
                z = lax.dot_general(qh[h], kblk, _NT, preferred_element_type=F32)
                lb = _log_sigmoid(z)
                w2 = _dot2(jnp.where(mask, lb - z, 0.0), m_le)
                att = jnp.where(mask, jnp.exp(lb + lth[h] - cp - w2[:, :CHUNK]), 0.0)
                d_att = lax.dot_general(doh[h], vblk, _NT, preferred_element_type=F32)
                d_e = d_att * att
                g2 = _dot2(d_e, m_lt)
                sig = jnp.exp(lb)
                dz = jnp.where(mask, d_e * (1.0 - sig) - (cg + g2[:, :CHUNK]) * sig, 0.0).astype(BF16)
                dq = dq + lax.dot_general(dz, kblk, _NN, preferred_element_type=F32)
                dk_acc = dk_acc + lax.dot_general(dz, qh[h], _TN, preferred_element_type=F32)
                dv_acc = dv_acc + lax.dot_general(att.astype(BF16), doh[h], _TN, preferred_element_type=F32)
                new += [cp + w2[:, CHUNK:], cg + g2[:, CHUNK:], dq]
            dk_ref[pl.ds(off, CHUNK), :] += dk_acc
            dv_ref[pl.ds(off, CHUNK), :] += dv_acc
            return tuple(new)

        zero = jnp.zeros((CHUNK, LANES), F32)
        out = lax.fori_loop(0, i + 1, step, (zero,) * 6)
        dq_ref[...] = jnp.where(lane < HEAD, out[2], out[5]) * scale

    q_spec, k_spec, v_spec = _sb_specs(t)
    blk = pl.BlockSpec((CHUNK, LANES), lambda j, i: (i, j))
    col = pl.BlockSpec((t, LANES), lambda j, i: (0, j))
    return pl.pallas_call(
        body, name=name, grid=(4, t // CHUNK),
        in_specs=[q_spec, k_spec, v_spec, blk, pl.BlockSpec((2, CHUNK, LANES), lambda j, i: (j, i, 0))],
        out_specs=[blk, col, col],
        out_shape=[jax.ShapeDtypeStruct((t, 4 * LANES), F32)] * 3,
        compiler_params=_params(("parallel", "arbitrary")),
    )(proj, proj, proj, dy, lt)


SB_BQ = 256
SB_BK = 256


def _tri_ones(kind):
    j = lax.broadcasted_iota(jnp.int32, (SB_BK, SB_BK + LANES), 0)
    s = lax.broadcasted_iota(jnp.int32, (SB_BK, SB_BK + LANES), 1)
    tri = {"gt": j > s, "le": j <= s, "lt": j < s}[kind]
    return (tri | (s >= SB_BK)).astype(BF16)


def _sb_common(q_ref):
    lane = lax.broadcasted_iota(jnp.int32, (SB_BQ, LANES), 1)
    q = q_ref[...] * (HEAD ** -0.5)
    q2 = jnp.concatenate([jnp.where(lane < HEAD, q, 0.0), jnp.where(lane >= HEAD, q, 0.0)], axis=0).astype(BF16)
    diff = (lax.broadcasted_iota(jnp.int32, (2 * SB_BQ, SB_BK), 1)
            - (lax.broadcasted_iota(jnp.int32, (2 * SB_BQ, SB_BK), 0) & (SB_BQ - 1)))
    return lane, q2, diff


def _rep(x):
    return jnp.concatenate([x] * (SB_BK // LANES), axis=1)


def _sb2_specs(t):
    q = pl.BlockSpec((SB_BQ, LANES), lambda j, i: (i, j))
    k = pl.BlockSpec((t, LANES), lambda j, i: (0, 4 + j))
    v = pl.BlockSpec((t, LANES), lambda j, i: (0, 8 + j))
    return q, k, v


def _sb2_fwd(proj, *, name):
    t = proj.shape[0]

    def body(q_ref, k_ref, v_ref, y_ref, lt_ref):
        i = pl.program_id(1)
        lane, q2, diff = _sb_common(q_ref)
        m_f = _tri_ones("gt")
        nk = (i + 1) * (SB_BQ // SB_BK)

        def step(it, carry):
            c, acc = carry
            kb = nk - 1 - it
            off = pl.multiple_of(kb * SB_BK, SB_BK)
            kblk = k_ref[pl.ds(off, SB_BK), :].astype(BF16)
            vblk = v_ref[pl.ds(off, SB_BK), :].astype(BF16)
            mask = diff < i * SB_BQ - kb * SB_BK
            z = lax.dot_general(q2, kblk, _NT, preferred_element_type=F32)
            lb = _log_sigmoid(z)
            w2 = _dot2(jnp.where(mask, lb - z, 0.0), m_f)
            att = jnp.where(mask, jnp.exp(lb + _rep(c) + w2[:, :SB_BK]), 0.0)
            acc = acc + lax.dot_general(att.astype(BF16), vblk, _NN, preferred_element_type=F32)
            return c + w2[:, SB_BK:], acc

        zero = jnp.zeros((2 * SB_BQ, LANES), F32)
        c, acc = lax.fori_loop(0, nk, step, (zero, zero))
        y_ref[...] = jnp.where(lane < HEAD, acc[:SB_BQ], acc[SB_BQ:])
        lt_ref[0] = c[:SB_BQ]
        lt_ref[1] = c[SB_BQ:]

    return pl.pallas_call(
        body, name=name, grid=(4, t // SB_BQ),
        in_specs=list(_sb2_specs(t)),
        out_specs=[pl.BlockSpec((SB_BQ, LANES), lambda j, i: (i, j)),
                   pl.BlockSpec((2, SB_BQ, LANES), lambda j, i: (j, i, 0))],
        out_shape=[jax.ShapeDtypeStruct((t, 4 * LANES), F32), jax.ShapeDtypeStruct((8, t, LANES), F32)],
        compiler_params=_params(("parallel", "arbitrary")),
    )(proj, proj, proj)


def _sb2_bwd(proj, dy, lt, *, name):
    t = proj.shape[0]

    def body(q_ref, k_ref, v_ref, dy_ref, lt_ref, dq_ref, dk_ref, dv_ref):
        i = pl.program_id(1)

        @pl.when(i == 0)
        def _():
            dk_ref[...] = jnp.zeros_like(dk_ref)
            dv_ref[...] = jnp.zeros_like(dv_ref)

        lane, q2, diff = _sb_common(q_ref)
        m_le, m_lt = _tri_ones("le"), _tri_ones("lt")
        dy_blk = dy_ref[...]
        do2 = jnp.concatenate([jnp.where(lane < HEAD, dy_blk, 0.0), jnp.where(lane >= HEAD, dy_blk, 0.0)],
                              axis=0).astype(BF16)
        lt2 = jnp.concatenate([lt_ref[0], lt_ref[1]], axis=0)

        def step(kb, carry):
            cp, cg, dq = carry
            off = pl.multiple_of(kb * SB_BK, SB_BK)
            kblk = k_ref[pl.ds(off, SB_BK), :].astype(BF16)
            vblk = v_ref[pl.ds(off, SB_BK), :].astype(BF16)
            mask = diff < i * SB_BQ - kb * SB_BK
            z = lax.dot_general(q2, kblk, _NT, preferred_element_type=F32)
            lb = _log_sigmoid(z)
            w2 = _dot2(jnp.where(mask, lb - z, 0.0), m_le)
            att = jnp.where(mask, jnp.exp(lb + _rep(lt2 - cp) - w2[:, :SB_BK]), 0.0)
            d_e = lax.dot_general(do2, vblk, _NT, preferred_element_type=F32) * att
            g2 = _dot2(d_e, m_lt)
            sig = jnp.exp(lb)
            dz = jnp.where(mask, d_e * (1.0 - sig) - (_rep(cg) + g2[:, :SB_BK]) * sig, 0.0).astype(BF16)
            dq = dq + lax.dot_general(dz, kblk, _NN, preferred_element_type=F32)
            dk_ref[pl.ds(off, SB_BK), :] += lax.dot_general(dz, q2, _TN, preferred_element_type=F32)
            dv_ref[pl.ds(off, SB_BK), :] += lax.dot_general(att.astype(BF16), do2, _TN, preferred_element_type=F32)
            return cp + w2[:, SB_BK:], cg + g2[:, SB_BK:], dq

        zero = jnp.zeros((2 * SB_BQ, LANES), F32)
        _, _, dq = lax.fori_loop(0, (i + 1) * (SB_BQ // SB_BK), step, (zero, zero, zero))
        dq_ref[...] = jnp.where(lane < HEAD, dq[:SB_BQ], dq[SB_BQ:]) * (HEAD ** -0.5)

    q_spec, k_spec, v_spec = _sb2_specs(t)
    blk = pl.BlockSpec((SB_BQ, LANES), lambda j, i: (i, j))
    col = pl.BlockSpec((t, LANES), lambda j, i: (0, j))
    return pl.pallas_call(
        body, name=name, grid=(4, t // SB_BQ),
        in_specs=[q_spec, k_spec, v_spec, blk, pl.BlockSpec((2, SB_BQ, LANES), lambda j, i: (j, i, 0))],
        out_specs=[blk, col, col],
        out_shape=[jax.ShapeDtypeStruct((t, 4 * LANES), F32)] * 3,
        compiler_params=_params(("parallel", "arbitrary")),
    )(proj, proj, proj, dy, lt)


SSD_HEADS = 16
SSD_PAIRS = 8


def _split3(x):
    a = x.astype(BF16)
    r = x - a.astype(F32)
    b = r.astype(BF16)
    return a, b, (r - b.astype(F32)).astype(BF16)


def _dot3(x, m, dn=_NN):
    return sum(lax.dot_general(p, m, dn, preferred_element_type=F32) for p in _split3(x))


def _mdot3(m, x):
    return sum(lax.dot_general(m, p, _NN, preferred_element_type=F32) for p in _split3(x))


def _ssd_common(dtr, dtb, alog, acsx_s, acst_s):
    lane = lax.broadcasted_iota(jnp.int32, (CHUNK, LANES), 1)
    lane1 = lax.broadcasted_iota(jnp.int32, (1, LANES), 1)
    arow = jnp.where(lane1 < SSD_HEADS, -jnp.exp(alog), 0.0)
    dt = jnp.where(lane < SSD_HEADS, _softplus(dtr + dtb), 0.0)
    da = dt * arow
    r = lax.broadcasted_iota(jnp.int32, (CHUNK, CHUNK), 0)
    c = lax.broadcasted_iota(jnp.int32, (CHUNK, CHUNK), 1)
    tril = (r >= c).astype(BF16)
    triu = (r <= c).astype(BF16)
    acs = _mdot3(tril, da)
    acst_s[...] = _dot3(da, triu, _TN)
    eh = lax.broadcasted_iota(jnp.int32, (LANES, 8 * LANES), 0)
    e = (eh == lax.broadcasted_iota(jnp.int32, (LANES, 8 * LANES), 1) // HEAD).astype(BF16)
    eh2 = lax.broadcasted_iota(jnp.int32, (LANES, 16 * LANES), 0)
    e2 = (eh2 == lax.broadcasted_iota(jnp.int32, (LANES, 16 * LANES), 1) // LANES).astype(BF16)
    acsx_s[...] = _dot3(acs, e)
    return dt, arow, _dot3(dt, e), _dot3(acs, e2), e, tril, triu


def _ssd_fwd(xc, proj, dtb, alog, dsk, *, name):
    t = xc.shape[0]
    nc = t // CHUNK

    def body(x_ref, b_ref, c_ref, dtr_ref, dtb_ref, alog_ref, dsk_ref, y_ref, hin_ref, acsx_s, acst_s, h_s):
        @pl.when(pl.program_id(0) == 0)
        def _():
            h_s[...] = jnp.zeros_like(h_s)

        dt, arow, dt_x, acs_b, e, tril, _ = _ssd_common(dtr_ref[...], dtb_ref[...], alog_ref[...], acsx_s, acst_s)
        dsk_x = _dot3(jnp.broadcast_to(dsk_ref[...], (CHUNK, LANES)), e)
        lane = lax.broadcasted_iota(jnp.int32, (CHUNK, LANES), 1)
        causal = (lax.broadcasted_iota(jnp.int32, (CHUNK, CHUNK), 0)
                  >= lax.broadcasted_iota(jnp.int32, (CHUNK, CHUNK), 1))
        for j in range(SSD_PAIRS):
            g = j // 4
            sl = slice(j * LANES, (j + 1) * LANES)
            if j % 4 == 0:
                bg = jnp.where(lane // HEAD == g, b_ref[...], 0.0)
                cg = jnp.where(lane // HEAD == g, c_ref[...], 0.0)
                cb = _dot_nt(cg, bg)
            x = x_ref[:, sl]
            a = acsx_s[:, sl]
            at = acsx_s[CHUNK - 1:CHUNK, sl]
            xdt = x * dt_x[:, sl]
            hin = h_s[j]
            hin_ref[0, j] = hin
            y = jnp.exp(a) * _dot_nn(cg, hin) + x * dsk_x[:, sl]
            h_s[j] = jnp.exp(at) * hin + _dot_tn(bg, xdt * jnp.exp(at - a))
            yd = []
            for hh in (0, 1):
                h = 2 * j + hh
                dec = jnp.exp(jnp.minimum(acs_b[:, h * LANES:(h + 1) * LANES] - acst_s[pl.ds(h, 1), :], 0.0))
                yd.append(_dot_nn(jnp.where(causal, cb * dec, 0.0), xdt))
            y_ref[:, sl] = y + jnp.where(lane < HEAD, yd[0], yd[1])

    one = pl.BlockSpec((1, LANES), lambda i: (0, 0))
    return pl.pallas_call(
        body, name=name, grid=(nc,),
        in_specs=[pl.BlockSpec((CHUNK, 8 * LANES), lambda i: (i, 0)),
                  pl.BlockSpec((CHUNK, LANES), lambda i: (i, 8)),
                  pl.BlockSpec((CHUNK, LANES), lambda i: (i, 9)),
                  pl.BlockSpec((CHUNK, LANES), lambda i: (i, C_DT // LANES)), one, one, one],
        out_specs=[pl.BlockSpec((CHUNK, 8 * LANES), lambda i: (i, 0)),
                   pl.BlockSpec((1, SSD_PAIRS, LANES, LANES), lambda i: (i, 0, 0, 0))],
        out_shape=[jax.ShapeDtypeStruct((t, 8 * LANES), F32),
                   jax.ShapeDtypeStruct((nc, SSD_PAIRS, LANES, LANES), F32)],
        scratch_shapes=[pltpu.VMEM((CHUNK, 8 * LANES), F32), pltpu.VMEM((LANES, CHUNK), F32),
                        pltpu.VMEM((SSD_PAIRS, LANES, LANES), F32)],
        compiler_params=_params(("arbitrary",)),
    )(xc, xc, xc, proj, dtb, alog, dsk)


def _ssd_bwd(xc, proj, dtb, alog, dsk, hin_all, dy, *, name):
    t = xc.shape[0]
    nc = t // CHUNK

    def body(x_ref, b_ref, c_ref, dtr_ref, dtb_ref, alog_ref, dsk_ref, hin_ref, dy_ref,
             dxc_ref, ddtr_ref, ddtb_ref, dalog_ref, ddsk_ref, acsx_s, acst_s, dh_s, dax_s, ddx_s):
        @pl.when(pl.program_id(0) == 0)
        def _():
            dh_s[...] = jnp.zeros_like(dh_s)
            ddtb_ref[...] = jnp.zeros_like(ddtb_ref)
            dalog_ref[...] = jnp.zeros_like(dalog_ref)
            ddsk_ref[...] = jnp.zeros_like(ddsk_ref)

        dtr = dtr_ref[...]
        dtb = dtb_ref[...]
        dt, arow, dt_x, acs_b, e, tril, triu = _ssd_common(dtr, dtb, alog_ref[...], acsx_s, acst_s)
        dsk_x = _dot3(jnp.broadcast_to(dsk_ref[...], (CHUNK, LANES)), e)
        lane = lax.broadcasted_iota(jnp.int32, (CHUNK, LANES), 1)
        rowi = lax.broadcasted_iota(jnp.int32, (CHUNK, LANES), 0)
        causal = (lax.broadcasted_iota(jnp.int32, (CHUNK, CHUNK), 0)
                  >= lax.broadcasted_iota(jnp.int32, (CHUNK, CHUNK), 1))
        dacs = jnp.zeros((CHUNK, LANES), F32)
        d_b = jnp.zeros((CHUNK, LANES), F32)
        d_c = jnp.zeros((CHUNK, LANES), F32)
        for j in range(SSD_PAIRS):
            g = j // 4
            sl = slice(j * LANES, (j + 1) * LANES)
            if j % 4 == 0:
                bg = jnp.where(lane // HEAD == g, b_ref[...], 0.0)
                cg = jnp.where(lane // HEAD == g, c_ref[...], 0.0)
                cb = _dot_nt(cg, bg)
                dcb = jnp.zeros((CHUNK, CHUNK), F32)
            x = x_ref[:, sl]
            d = dt_x[:, sl]
            a = acsx_s[:, sl]
            at = acsx_s[CHUNK - 1:CHUNK, sl]
            xdt = x * d
            hin = hin_ref[0, j]
            dhout = dh_s[j]
            dyp = dy_ref[:, sl]
            ea, eat, ed = jnp.exp(a), jnp.exp(at), jnp.exp(at - a)
            da_l = dyp * ea * _dot_nn(cg, hin)
            dm = dyp * ea
            d_c = d_c + _dot_nt(dm, hin)
            dh_s[j] = _dot_tn(cg, dm) + eat * dhout
            dat = jnp.sum(dhout * hin * eat, axis=0, keepdims=True)
            d_b = d_b + _dot_nt(xdt * ed, dhout)
            dw = _dot_nn(bg, dhout)
            dxdt = dw * ed
            ded = dw * xdt * ed
            dat = dat + jnp.sum(ded, axis=0, keepdims=True)
            da_l = da_l - ded
            for hh in (0, 1):
                h = 2 * j + hh
                dec = jnp.exp(jnp.minimum(acs_b[:, h * LANES:(h + 1) * LANES] - acst_s[pl.ds(h, 1), :], 0.0))
                gm = jnp.where(causal, cb * dec, 0.0)
                dyh = jnp.where(lane // HEAD == hh, dyp, 0.0)
                dg = _dot_nt(dyh, xdt)
                dxdt = dxdt + _dot_tn(gm, dyh)
                dcb = dcb + jnp.where(causal, dg * dec, 0.0)
                th = dg * gm
                oh = (lane == h).astype(BF16)
                dacs = dacs + _dot2(th, oh) - _dot2_tn(th, oh)
            if j % 4 == 3:
                d_c = d_c + _dot_nn(dcb, bg)
                d_b = d_b + _dot_tn(dcb, cg)
            dxc_ref[:, sl] = dyp * dsk_x[:, sl] + dxdt * d
            ddx_s[:, sl] = dxdt * x
            dax_s[:, sl] = da_l + jnp.where(rowi == CHUNK - 1, dat, 0.0)
            dskp = jnp.sum(dyp * x, axis=0, keepdims=True)
            ddsk_ref[...] += _dot2(jnp.broadcast_to(dskp, (8, LANES)), e[:, sl], _NT)
        dxc_ref[:, 8 * LANES:9 * LANES] = d_b
        dxc_ref[:, 9 * LANES:10 * LANES] = d_c
        dacs = dacs + _dot2(dax_s[...], e, _NT)
        ddt = _dot2(ddx_s[...], e, _NT)
        dda = _mdot3(triu, dacs)
        ddt = ddt + dda * arow
        dalog_ref[...] += jnp.sum(dda * dt, axis=0, keepdims=True) * arow
        ddtr = jnp.where(lane < SSD_HEADS, ddt * _sigmoid(dtr + dtb), 0.0)
        ddtr_ref[...] = ddtr
        ddtb_ref[...] += jnp.sum(ddtr, axis=0, keepdims=True)

    one = pl.BlockSpec((1, LANES), lambda i: (0, 0))
    rev = lambda c: (lambda i: (nc - 1 - i, c))
    return pl.pallas_call(
        body, name=name, grid=(nc,),
        in_specs=[pl.BlockSpec((CHUNK, 8 * LANES), rev(0)), pl.BlockSpec((CHUNK, LANES), rev(8)),
                  pl.BlockSpec((CHUNK, LANES), rev(9)), pl.BlockSpec((CHUNK, LANES), rev(C_DT // LANES)),
                  one, one, one,
                  pl.BlockSpec((1, SSD_PAIRS, LANES, LANES), lambda i: (nc - 1 - i, 0, 0, 0)),
                  pl.BlockSpec((CHUNK, 8 * LANES), rev(0))],
        out_specs=[pl.BlockSpec((CHUNK, XBC_COLS), rev(0)), pl.BlockSpec((CHUNK, LANES), rev(0)), one, one,
                   pl.BlockSpec((8, LANES), lambda i: (0, 0))],
        out_shape=[jax.ShapeDtypeStruct((t, XBC_COLS), F32), jax.ShapeDtypeStruct((t, LANES), F32)]
        + [jax.ShapeDtypeStruct((1, LANES), F32)] * 2 + [jax.ShapeDtypeStruct((8, LANES), F32)],
        scratch_shapes=[pltpu.VMEM((CHUNK, 8 * LANES), F32), pltpu.VMEM((LANES, CHUNK), F32),
                        pltpu.VMEM((SSD_PAIRS, LANES, LANES), F32),
                        pltpu.VMEM((CHUNK, 8 * LANES), F32), pltpu.VMEM((CHUNK, 8 * LANES), F32)],
        compiler_params=_params(("arbitrary",)),
    )(xc, xc, xc, proj, dtb, alog, dsk, hin_all, dy)


RW_PAIRS = 4
RW_BT = 16


def _rw_consts():
    seg = _seg_matrix(LANES)
    ti = (lax.broadcasted_iota(jnp.int32, (HEAD, LANES), 0)
          == lax.broadcasted_iota(jnp.int32, (HEAD, LANES), 1) % HEAD)
    return seg, ti


def _col_tiles(rows, ti, seg):
    tib = ti.astype(BF16)
    hi = [r.astype(BF16) for r in rows]
    lo = [(r - h.astype(F32)).astype(BF16) for r, h in zip(rows, hi)]
    out = (lax.dot_general(jnp.concatenate([tib * h for h in hi], axis=0), seg, _NN, preferred_element_type=F32)
           + lax.dot_general(jnp.concatenate([tib * l for l in lo], axis=0), seg, _NN, preferred_element_type=F32))
    return [out[i * HEAD:(i + 1) * HEAD] for i in range(len(rows))]


def _head_lane_sums(tiles, ti, seg):
    out = _dot2(jnp.concatenate(tiles, axis=0), seg)
    return [jnp.sum(jnp.where(ti, out[i * HEAD:(i + 1) * HEAD], 0.0), axis=0, keepdims=True) for i in range(len(tiles))]


def _rw_scan_fwd(mixed, w, k, n, b, *, name):
    t = w.shape[0]

    def body(r_ref, v_ref, w_ref, k_ref, n_ref, b_ref, y_ref, st_ref, s_s):
        @pl.when(pl.program_id(0) == 0)
        def _():
            s_s[...] = jnp.zeros_like(s_s)

        seg, ti = _rw_consts()

        def step(tt, state):
            row = pl.ds(tt, 1)
            new = []
            for p in range(RW_PAIRS):
                sl = pl.ds(p * LANES, LANES)
                s = state[p]
                ncol, wcol, bcol, kcol, rcol = _col_tiles(
                    [x[row, sl] for x in (n_ref, w_ref, b_ref, k_ref, r_ref)], ti, seg)
                sa = jnp.sum(s * ncol, axis=0, keepdims=True)
                s = s * wcol + bcol * sa + kcol * v_ref[row, sl]
                y_ref[row, sl] = jnp.sum(s * rcol, axis=0, keepdims=True)
                st_ref[tt, p] = s
                new.append(s)
            return tuple(new)

        out = tuple(s_s[p] for p in range(RW_PAIRS))
        for tt in range(RW_BT):
            out = step(tt, out)
        for p in range(RW_PAIRS):
            s_s[p] = out[p]

    blk = lambda c: pl.BlockSpec((RW_BT, 4 * LANES), functools.partial(lambda i, c: (i, c), c=c))
    return pl.pallas_call(
        body, name=name, grid=(t // RW_BT,),
        in_specs=[blk(0), blk(2), blk(0), blk(0), blk(0), blk(0)],
        out_specs=[blk(0), pl.BlockSpec((RW_BT, RW_PAIRS, HEAD, LANES), lambda i: (i, 0, 0, 0))],
        out_shape=[jax.ShapeDtypeStruct((t, 4 * LANES), F32),
                   jax.ShapeDtypeStruct((t, RW_PAIRS, HEAD, LANES), F32)],
        scratch_shapes=[pltpu.VMEM((RW_PAIRS, HEAD, LANES), F32)],
        compiler_params=_params(("arbitrary",)),
    )(mixed, mixed, w, k, n, b)


def _rw_scan_bwd(mixed, w, k, n, b, states, dy, dr0, dk0, dv0, *, name):
    t = w.shape[0]
    nb = t // RW_BT

    def body(r_ref, v_ref, w_ref, k_ref, n_ref, b_ref, st_ref, prev_ref, dy_ref, dr0_ref, dk0_ref, dv0_ref,
             dr_ref, dw_ref, dk_ref, dv_ref, dn_ref, db_ref, ds_s):
        @pl.when(pl.program_id(0) == 0)
        def _():
            ds_s[...] = jnp.zeros_like(ds_s)

        seg, ti = _rw_consts()
        has_prev = (pl.program_id(0) < nb - 1).astype(F32)

        def step(it, carry):
            tt = RW_BT - 1 - it
            row = pl.ds(tt, 1)
            prev_t = max(tt - 1, 0)
            new_ds, new_s = [], []
            for p in range(RW_PAIRS):
                sl = pl.ds(p * LANES, LANES)
                ds, s_t = carry[p], carry[RW_PAIRS + p]
                s_p = st_ref[prev_t, p] if tt > 0 else prev_ref[0, p] * has_prev
                ncol, wcol, bcol, kcol, rcol = _col_tiles(
                    [x[row, sl] for x in (n_ref, w_ref, b_ref, k_ref, r_ref)], ti, seg)
                vv, dyy = v_ref[row, sl], dy_ref[row, sl]
                sa = jnp.sum(s_p * ncol, axis=0, keepdims=True)
                ds = ds + rcol * dyy
                dsa = jnp.sum(ds * bcol, axis=0, keepdims=True)
                dv_ref[row, sl] = jnp.sum(ds * kcol, axis=0, keepdims=True) + dv0_ref[row, sl]
                dr, dw, db, dk, dn = _head_lane_sums([s_t * dyy, ds * s_p, ds * sa, ds * vv, s_p * dsa], ti, seg)
                dr_ref[row, sl] = dr + dr0_ref[row, sl]
                dw_ref[row, sl] = dw
                db_ref[row, sl] = db
                dk_ref[row, sl] = dk + dk0_ref[row, sl]
                dn_ref[row, sl] = dn
                new_ds.append(ds * wcol + ncol * dsa)
                new_s.append(s_p)
            return tuple(new_ds) + tuple(new_s)

        init = tuple(ds_s[p] for p in range(RW_PAIRS)) + tuple(st_ref[RW_BT - 1, p] for p in range(RW_PAIRS))
        out = init
        for it in range(RW_BT):
            out = step(it, out)
        for p in range(RW_PAIRS):
            ds_s[p] = out[p]

    blk = lambda c: pl.BlockSpec((RW_BT, 4 * LANES), functools.partial(lambda i, c: (nb - 1 - i, c), c=c))
    st_spec = pl.BlockSpec((RW_BT, RW_PAIRS, HEAD, LANES), lambda i: (nb - 1 - i, 0, 0, 0))
    prev_spec = pl.BlockSpec((1, RW_PAIRS, HEAD, LANES), lambda i: (jnp.maximum((nb - 1 - i) * RW_BT - 1, 0), 0, 0, 0))
    return pl.pallas_call(
        body, name=name, grid=(nb,),
        in_specs=[blk(0), blk(2), blk(0), blk(0), blk(0), blk(0), st_spec, prev_spec, blk(0), blk(0), blk(0), blk(0)],
        out_specs=[blk(0)] * 6,
        out_shape=[jax.ShapeDtypeStruct((t, 4 * LANES), F32)] * 6,
        scratch_shapes=[pltpu.VMEM((RW_PAIRS, HEAD, LANES), F32)],
        compiler_params=_params(("arbitrary",)),
    )(mixed, mixed, w, k, n, b, states, states, dy, dr0, dk0, dv0)


def _f_rms_res(x, g):
    return _f_rms(x, g)[0], x


def _final(x, g, target, *, bt, name):
    t, d = x.shape

    def body(x_ref, g_ref, t_ref, dx_ref, loss_ref, dg_ref):
        tgt = t_ref[...]

        def f(xv, gv):
            err = _f_rms(xv, gv)[0] - tgt
            return 0.5 * jnp.mean(err * err, axis=-1, keepdims=True)

        row_loss, vjp = jax.vjp(f, x_ref[...], g_ref[...])
        dx, dg = vjp(jnp.ones_like(row_loss))
        dx_ref[...] = dx

        @pl.when(pl.program_id(0) == 0)
        def _():
            loss_ref[...] = jnp.zeros_like(loss_ref)
            dg_ref[...] = jnp.zeros_like(dg_ref)

        loss_ref[...] += jnp.broadcast_to(jnp.sum(row_loss, axis=0, keepdims=True), (1, LANES))
        dg_ref[...] += dg

    blk = pl.BlockSpec((bt, d), lambda i: (i, 0))
    return pl.pallas_call(
        body, name=name, grid=(t // bt,),
        in_specs=[blk, pl.BlockSpec((1, d), lambda i: (0, 0)), blk],
        out_specs=[blk, pl.BlockSpec((1, LANES), lambda i: (0, 0)), pl.BlockSpec((1, d), lambda i: (0, 0))],
        out_shape=[jax.ShapeDtypeStruct((t, d), F32), jax.ShapeDtypeStruct((1, LANES), F32),
                   jax.ShapeDtypeStruct((1, d), F32)],
        compiler_params=_params(("arbitrary",)),
    )(x, g, target)


ADAMW_BLOCK_BYTES = 1 << 20


def _adamw(w, g, m, v, *, name):
    shape = w.shape
    c = shape[-1]
    args = [a.reshape(-1, c) for a in (w, g, m, v)]
    r = args[0].shape[0]
    br = r
    if r * c * 4 > ADAMW_BLOCK_BYTES:
        cands = [b for b in range(8, r, 8) if r % b == 0 and b * c * 4 <= ADAMW_BLOCK_BYTES]
        br = max(cands) if cands else r

    def body(w_ref, g_ref, m_ref, v_ref, d_ref, nm_ref, nv_ref):
        gv = g_ref[...]
        m_new = ADAM_B1 * m_ref[...] + (1.0 - ADAM_B1) * gv
        v_new = ADAM_B2 * v_ref[...] + (1.0 - ADAM_B2) * (gv * gv)
        m_hat = m_new / (1.0 - ADAM_B1 ** ADAM_STEP)
        v_hat = v_new / (1.0 - ADAM_B2 ** ADAM_STEP)
        d_ref[...] = -ADAM_LR * (m_hat / (jnp.sqrt(v_hat) + ADAM_EPS) + ADAM_WD * w_ref[...])
        nm_ref[...] = m_new
        nv_ref[...] = v_new

    blk = pl.BlockSpec((br, c), lambda i: (i, 0))
    outs = pl.pallas_call(
        body, name=name, grid=(r // br,), in_specs=[blk] * 4, out_specs=[blk] * 3,
        out_shape=[jax.ShapeDtypeStruct((r, c), F32)] * 3,
        compiler_params=_params(("parallel",)),
    )(*args)
    return tuple(o.reshape(shape) for o in outs)


BT = 256
BC = 128


def _layer_rows(x, proj, s):
    s = {k: s.get(k) for k in ("y_sb_raw", "y_ssd_raw", "mixed", "ys", "k2", "p_sb", "p_ssd", "p_rw")}
    return dict(
        rms=[(x, D_MODEL, 0)],
        sb_gate=[(s["y_sb_raw"], 512, 0), (proj, 512, 3)],
        ssd_norm=[(s["y_ssd_raw"], 1024, 0), (proj, 1024, C_Z // 1024)],
        rw_pre=[(s["mixed"], 512, 1), (s["mixed"], LANES, 16)],
        rw_post=[(s["ys"], 512, 0), (s["mixed"], 512, 0), (s["k2"], 512, 0), (s["mixed"], 512, 2), (s["mixed"], 512, 3)],
        merge=[(s["p_sb"], 1024, 0), (s["p_ssd"], 1024, 0), (s["p_rw"], 1024, 0),
               (proj, 1024, 3), (proj, 1024, 4), (proj, 1024, 5)],
    )


def _layer_fwd(x, p, nm):
    s = {}
    (s["h"],) = _rowwise(_f_rms, [(x, D_MODEL, 0)], [p["norm_g"]], [D_MODEL], bt=BT, name=nm + "rms")
    proj = s["proj"] = _mm(s["h"], p["w_in"], name=nm + "proj")
    s["y_sb_raw"], s["lt"] = _sb2_fwd(proj, name=nm + "sb")
    s["xc"] = _colwise(_f_conv, proj, C_XBC, XBC_COLS, p["conv"], bc=BC, name=nm + "conv")
    s["y_ssd_raw"], s["hin"] = _ssd_fwd(s["xc"], proj, p["dt_bias"], p["a_log"], p["d_skip"], name=nm + "ssd")
    s["mixed"] = _colwise(_f_rw_mix, proj, C_RW, RW_COLS, [p["rw_mu"]], bc=BC, name=nm + "mix")
    s["w"], s["k2"], s["n"], s["b"] = _rowwise(_f_rw_pre, [(s["mixed"], 512, 1), (s["mixed"], LANES, 16)], p["rw_pre"],
                                               [512] * 4, bt=BT, name=nm + "rwpre")
    s["ys"], s["st"] = _rw_scan_fwd(s["mixed"], s["w"], s["k2"], s["n"], s["b"], name=nm + "scan")
    rows = _layer_rows(x, proj, s)
    (s["y_sb"],) = _rowwise(_f_sb_gate, rows["sb_gate"], [], [512], bt=BT, name=nm + "sbgate")
    (s["y_ssd"],) = _rowwise(_f_ssd_norm, rows["ssd_norm"], [p["ssd_norm_g"]], [1024], bt=BT, name=nm + "ssdnorm")
    (s["y_rw"],) = _rowwise(_f_rw_post, rows["rw_post"], p["rw_post"], [512], bt=BT, name=nm + "rwpost")
    s["p_sb"] = _mm(s["y_sb"], p["w_out_sb"], name=nm + "osb")
    s["p_ssd"] = _mm(s["y_ssd"], p["w_out_ssd"], name=nm + "ossd")
    s["p_rw"] = _mm(s["y_rw"], p["w_out_rw"], name=nm + "orw")
    (s["merged"],) = _rowwise(_f_merge, _layer_rows(x, proj, s)["merge"], [], [1024], bt=BT, name=nm + "merge")
    return _mm(s["merged"], p["w_o"], add=x, name=nm + "wo"), s


def _layer_bwd(x, dx_out, p, s, nm):
    g = {}
    proj = s["proj"]
    rows = _layer_rows(x, proj, s)
    g["w_o"] = _mm(s["merged"], dx_out, ta=True, name=nm + "g_wo")
    d_merged = _mm(dx_out, p["w_o"], tb=True, name=nm + "d_merged")
    dp_sb, dp_ssd, dp_rw, d_gates = _rowwise_bwd(_f_merge, rows["merge"], [], [(d_merged, 1024, 0)], bt=BT,
                                                 name=nm + "merge_b", groups=[[0], [1], [2], [3, 4, 5]])
    g["w_out_sb"] = _mm(s["y_sb"], dp_sb, ta=True, name=nm + "g_osb")
    g["w_out_ssd"] = _mm(s["y_ssd"], dp_ssd, ta=True, name=nm + "g_ossd")
    g["w_out_rw"] = _mm(s["y_rw"], dp_rw, ta=True, name=nm + "g_orw")
    dy_sb = _mm(dp_sb, p["w_out_sb"], tb=True, name=nm + "d_ysb")
    dy_ssd = _mm(dp_ssd, p["w_out_ssd"], tb=True, name=nm + "d_yssd")
    dy_rw = _mm(dp_rw, p["w_out_rw"], tb=True, name=nm + "d_yrw")
    dy_sb_raw, d_sbgate = _rowwise_bwd(_f_sb_gate, rows["sb_gate"], [], [(dy_sb, 512, 0)], bt=BT, name=nm + "sbgate_b")
    dq, dk, dv = _sb2_bwd(proj, dy_sb_raw, s["lt"], name=nm + "sb_b")
    dy_ssd_raw, dz, g["ssd_norm_g"] = _rowwise_bwd(_f_ssd_norm, rows["ssd_norm"], [p["ssd_norm_g"]],
                                                   [(dy_ssd, 1024, 0)], bt=BT, name=nm + "ssdnorm_b")
    dxc, ddtr, g["dt_bias"], g["a_log"], g["d_skip"] = _ssd_bwd(
        s["xc"], proj, p["dt_bias"], p["a_log"], p["d_skip"], s["hin"], dy_ssd_raw, name=nm + "ssd_b")
    conv_out = _colwise_bwd(_f_conv, proj, C_XBC, XBC_COLS, p["conv"], dxc, bc=BC, name=nm + "conv_b")
    dxbc, g["conv"] = conv_out[0], conv_out[1:]
    dys, dr0, dk0, dv0, d_rwgate, g["rw_ln_g"], g["rw_ln_b"], g["rw_r_k"] = _rowwise_bwd(
        _f_rw_post, rows["rw_post"], p["rw_post"], [(dy_rw, 512, 0)], bt=BT, name=nm + "rwpost_b")
    dr, dw, dk2, dvv, dn, db = _rw_scan_bwd(s["mixed"], s["w"], s["k2"], s["n"], s["b"], s["st"], dys, dr0, dk0, dv0,
                                            name=nm + "scan_b")
    pre_out = _rowwise_bwd(_f_rw_pre, rows["rw_pre"], p["rw_pre"],
                           [(dw, 512, 0), (dk2, 512, 0), (dn, 512, 0), (db, 512, 0)], bt=BT, name=nm + "rwpre_b")
    dkm, dlo, g["rw_pre"] = pre_out[0], pre_out[1], pre_out[2:]
    d_mixed = jnp.concatenate([dr, dkm, dvv, d_rwgate, dlo], axis=1)
    d_slab, g["rw_mu"] = _colwise_bwd(_f_rw_mix, proj, C_RW, RW_COLS, [p["rw_mu"]], d_mixed, bc=BC, name=nm + "mix_b")
    d_proj = jnp.concatenate([dq, dk, dv, d_sbgate, dz, d_gates, d_slab, ddtr, dxbc], axis=1)
    g["w_in"] = _mm(s["h"], d_proj, ta=True, name=nm + "g_win")
    dh = _mm(d_proj, p["w_in"], tb=True, tk=512, name=nm + "d_h")
    dx, g["norm_g"] = _rowwise_bwd(_f_rms_res, rows["rms"], [p["norm_g"]], [(dh, D_MODEL, 0), (dx_out, D_MODEL, 0)],
                                   bt=BT, name=nm + "rms_b")
    return dx, g


MESH = pl.DeviceIdType.MESH
N_DEV = 8
_ANY = pl.BlockSpec(memory_space=pl.ANY)
_CHIP_SEMS = [pltpu.SemaphoreType.DMA((3,)), pltpu.SemaphoreType.DMA((3,)), pltpu.SemaphoreType.DMA]


def _here():
    x, y, c = lax.axis_index("x"), lax.axis_index("y"), lax.axis_index("c")
    return x, y, c, [(1 - x, y), (x, 1 - y), (1 - x, 1 - y)]


def _chip_exchange(src, *, per_dest, name):
    shape = src.shape[-2:]

    def body(src_ref, out_ref, send_sems, recv_sems, local_sem):
        x, y, c, chips = _here()
        me = 2 * x + y
        pick = (lambda q: src_ref.at[q]) if per_dest else (lambda q: src_ref.at[c])
        own = pltpu.make_async_copy(pick(me), out_ref.at[me], local_sem)
        own.start()
        sends = [pltpu.make_async_remote_copy(pick(2 * px + py), out_ref.at[me], send_sems.at[j], recv_sems.at[j],
                                              device_id=(px, py, c), device_id_type=MESH)
                 for j, (px, py) in enumerate(chips)]
        for cp in sends:
            cp.start()
        for j, (px, py) in enumerate(chips):
            pltpu.make_async_remote_copy(pick(me), out_ref.at[2 * px + py], send_sems.at[j], recv_sems.at[j],
                                         device_id=(px, py, c), device_id_type=MESH).wait_recv()
        for cp in sends:
            cp.wait_send()
        own.wait()

    return pl.pallas_call(
        body, name=name, in_specs=[_ANY], out_specs=_ANY,
        out_shape=jax.ShapeDtypeStruct((4,) + shape, src.dtype), scratch_shapes=_CHIP_SEMS,
    )(src)


def _sibling_send_other_half(src, *, name):
    def body(src_ref, out_ref, send_sem, recv_sem):
        x, y, c, _ = _here()
        cp = pltpu.make_async_remote_copy(src_ref.at[1 - c], out_ref, send_sem, recv_sem,
                                          device_id=(x, y, 1 - c), device_id_type=MESH)
        cp.start()
        cp.wait()

    return pl.pallas_call(
        body, name=name, in_specs=[_ANY], out_specs=_ANY,
        out_shape=jax.ShapeDtypeStruct(src.shape[1:], src.dtype),
        scratch_shapes=[pltpu.SemaphoreType.DMA, pltpu.SemaphoreType.DMA],
    )(src)


def _sibling_swap(src, *, name):
    def body(src_ref, out_ref, send_sem, recv_sem):
        x, y, c, _ = _here()
        cp = pltpu.make_async_remote_copy(src_ref, out_ref, send_sem, recv_sem,
                                          device_id=(x, y, 1 - c), device_id_type=MESH)
        cp.start()
        cp.wait()

    return pl.pallas_call(
        body, name=name, in_specs=[_ANY], out_specs=_ANY,
        out_shape=jax.ShapeDtypeStruct(src.shape, src.dtype),
        scratch_shapes=[pltpu.SemaphoreType.DMA, pltpu.SemaphoreType.DMA],
    )(src)


def _allgather_small(v, *, reduce, name):
    r = v.shape[0]

    def body(v_ref, out_ref, *rest):
        send_sems, recv_sems, local_sem = rest[-3:]
        x, y, c, chips = _here()
        me, sibling = (x, y, c), (x, y, 1 - c)

        def slot(px, py, pc):
            return out_ref.at[4 * px + 2 * py + pc]

        def copy(k, block, to, src=None):
            return pltpu.make_async_remote_copy(
                src_ref=slot(*block) if src is None else src, dst_ref=slot(*block),
                send_sem=send_sems.at[k], recv_sem=recv_sems.at[k], device_id=to, device_id_type=MESH)

        mine = pltpu.make_async_copy(v_ref, slot(*me), local_sem)
        mine.start()
        first = [copy(0, me, sibling, src=v_ref)]
        first += [copy(1 + j, me, (*chip, c), src=v_ref) for j, chip in enumerate(chips)]
        for cp in first:
            cp.start()
        passed = [copy(4 + j, (*chip, c), sibling) for j, chip in enumerate(chips)]
        for j, chip in enumerate(chips):
            copy(1 + j, (*chip, c), me).wait_recv()
            passed[j].start()
        copy(0, sibling, me).wait_recv()
        for j, chip in enumerate(chips):
            copy(4 + j, (*chip, 1 - c), me).wait_recv()
        for cp in first + passed:
            cp.wait_send()
        mine.wait()
        if reduce:
            total = out_ref[0]
            for d in range(1, N_DEV):
                total = total + out_ref[d]
            rest[0][...] = total

    vm = pl.BlockSpec(memory_space=pltpu.VMEM)
    out_shape = [jax.ShapeDtypeStruct((N_DEV, r, LANES), F32)] + ([jax.ShapeDtypeStruct((r, LANES), F32)] if reduce else [])
    return pl.pallas_call(
        body, name=name, in_specs=[vm], out_specs=[vm] * len(out_shape), out_shape=out_shape,
        scratch_shapes=[pltpu.SemaphoreType.DMA((7,)), pltpu.SemaphoreType.DMA((7,)), pltpu.SemaphoreType.DMA],
        compiler_params=pltpu.CompilerParams(vmem_limit_bytes=VMEM_LIMIT),
    )(v)


REDUCE_ROWS = 1952


def _add_halves(mine2, other, c_idx, *, name):
    _, nq, r, _ = mine2.shape

    def body(c_ref, a_ref, b_ref, o_ref):
        o_ref[...] = (a_ref[0] + b_ref[...]).astype(o_ref.dtype)

    blk = pl.BlockSpec((1, REDUCE_ROWS, LANES), lambda q, i, c_ref: (q, i, 0))
    return pl.pallas_call(
        body, name=name,
        grid_spec=pltpu.PrefetchScalarGridSpec(
            num_scalar_prefetch=1, grid=(nq, r // REDUCE_ROWS),
            in_specs=[pl.BlockSpec((1, 1, REDUCE_ROWS, LANES), lambda q, i, c_ref: (c_ref[0], q, i, 0)), blk],
            out_specs=blk),
        out_shape=jax.ShapeDtypeStruct((nq, r, LANES), BF16),
        compiler_params=_params(("parallel", "parallel")),
    )(c_idx, mine2, other)


def _sum_chips(parts, *, name):
    _, r, _ = parts.shape

    def body(p_ref, o_ref):
        total = p_ref[0].astype(F32)
        for q in range(1, 4):
            total = total + p_ref[q].astype(F32)
        o_ref[...] = total

    return pl.pallas_call(
        body, name=name, grid=(r // REDUCE_ROWS,),
        in_specs=[pl.BlockSpec((4, REDUCE_ROWS, LANES), lambda i: (0, i, 0))],
        out_specs=pl.BlockSpec((REDUCE_ROWS, LANES), lambda i: (i, 0)),
        out_shape=jax.ShapeDtypeStruct((r, LANES), F32),
        compiler_params=_params(("parallel",)),
    )(parts)


BIG = ("w_in", "w_out_sb", "w_out_ssd", "w_out_rw", "w_o")
BIG_AXIS = {"w_in": 2, "w_out_sb": 2, "w_out_ssd": 1, "w_out_rw": 2, "w_o": 1}
SMALL_SHARDED = {"conv_w": 320, "rw_w_up": 128, "rw_a_up": 128}
SMALL = ("norm_g", "conv_w", "conv_b", "dt_bias", "a_log", "d_skip", "ssd_norm_g", "rw_mu", "rw_w0", "rw_w_up",
         "rw_a0", "rw_a_up", "rw_k_k", "rw_k_a", "rw_r_k", "rw_ln_g", "rw_ln_b", "final_g")


def _rows_of(a):
    flat = a.reshape(-1)
    pad = (-flat.shape[0]) % LANES
    return jnp.pad(flat, (0, pad)).reshape(-1, LANES)


def _pack_rows(arrays, multiple=8):
    rows = jnp.concatenate([_rows_of(a) for a in arrays], axis=0)
    pad = (-rows.shape[0]) % multiple
    return jnp.pad(rows, ((0, pad), (0, 0)))


def _unpack_rows(rows, shapes):
    out, off = [], 0
    for shp in shapes:
        n = 1
        for d in shp:
            n *= d
        nr = -(-n // LANES)
        out.append(rows[off:off + nr].reshape(-1)[:n].reshape(shp))
        off += nr
    return out


def _pad_cols(w):
    z = jnp.zeros(w.shape[:-1] + (N_PAD - N_IN,), w.dtype)
    return jnp.concatenate([w[..., 0:3072], w[..., 6544:9616], w[..., 4368:6544], w[..., 4352:4368], z,
                            w[..., 3072:4352]], axis=-1)


def _unpad_cols(g):
    return jnp.concatenate([g[..., 0:3072], g[..., 8448:9728], g[..., 8320:8336], g[..., 6144:8320],
                            g[..., 3072:6144]], axis=-1)


def _split_chips(a, axis):
    n = a.shape[axis] // 4
    return jnp.stack([lax.slice_in_dim(a, q * n, (q + 1) * n, axis=axis) for q in range(4)])


def _join_chips(a, axis):
    return jnp.concatenate([a[q] for q in range(4)], axis=axis)


def kernel(x, norm_g, w_in, conv_w, conv_b, dt_bias, a_log, d_skip, ssd_norm_g, rw_mu, rw_w0, rw_w_up, rw_a0, rw_a_up, rw_k_k, rw_k_a, rw_r_k, rw_ln_g, rw_ln_b, w_out_sb, w_out_ssd, w_out_rw, w_o, final_g, loss_target, m_norm_g, m_w_in, m_conv_w, m_conv_b, m_dt_bias, m_a_log, m_d_skip, m_ssd_norm_g, m_rw_mu, m_rw_w0, m_rw_w_up, m_rw_a0, m_rw_a_up, m_rw_k_k, m_rw_k_a, m_rw_r_k, m_rw_ln_g, m_rw_ln_b, m_w_out_sb, m_w_out_ssd, m_w_out_rw, m_w_o, m_final_g, v_norm_g, v_w_in, v_conv_w, v_conv_b, v_dt_bias, v_a_log, v_d_skip, v_ssd_norm_g, v_rw_mu, v_rw_w0, v_rw_w_up, v_rw_a0, v_rw_a_up, v_rw_k_k, v_rw_k_a, v_rw_r_k, v_rw_ln_g, v_rw_ln_b, v_w_out_sb, v_w_out_ssd, v_w_out_rw, v_w_o, v_final_g):
    names = ("norm_g", "w_in", "conv_w", "conv_b", "dt_bias", "a_log", "d_skip", "ssd_norm_g", "rw_mu", "rw_w0",
             "rw_w_up", "rw_a0", "rw_a_up", "rw_k_k", "rw_k_a", "rw_r_k", "rw_ln_g", "rw_ln_b", "w_out_sb",
             "w_out_ssd", "w_out_rw", "w_o", "final_g")
    w_loc = dict(zip(names, (norm_g, w_in, conv_w, conv_b, dt_bias, a_log, d_skip, ssd_norm_g, rw_mu, rw_w0, rw_w_up,
                             rw_a0, rw_a_up, rw_k_k, rw_k_a, rw_r_k, rw_ln_g, rw_ln_b, w_out_sb, w_out_ssd, w_out_rw,
                             w_o, final_g)))
    m_loc = dict(zip(names, (m_norm_g, m_w_in, m_conv_w, m_conv_b, m_dt_bias, m_a_log, m_d_skip, m_ssd_norm_g,
                             m_rw_mu, m_rw_w0, m_rw_w_up, m_rw_a0, m_rw_a_up, m_rw_k_k, m_rw_k_a, m_rw_r_k,
                             m_rw_ln_g, m_rw_ln_b, m_w_out_sb, m_w_out_ssd, m_w_out_rw, m_w_o, m_final_g)))
    v_loc = dict(zip(names, (v_norm_g, v_w_in, v_conv_w, v_conv_b, v_dt_bias, v_a_log, v_d_skip, v_ssd_norm_g,
                             v_rw_mu, v_rw_w0, v_rw_w_up, v_rw_a0, v_rw_a_up, v_rw_k_k, v_rw_k_a, v_rw_r_k,
                             v_rw_ln_g, v_rw_ln_b, v_w_out_sb, v_w_out_ssd, v_w_out_rw, v_w_o, v_final_g)))
    chip = 2 * lax.axis_index("x") + lax.axis_index("y")
    core = lax.axis_index("c")

    big_shapes = [w_loc[n].shape for n in BIG]
    pack = _pack_rows([w_loc[n].astype(BF16) for n in BIG], multiple=32)
    pack_half = pack.shape[0] // 2
    got_mine = _chip_exchange(pack.reshape(2, pack_half, LANES), per_dest=False, name="gather_big")
    got_theirs = _sibling_swap(got_mine, name="gather_join")
    got = jnp.concatenate([jnp.where(core == 0, got_mine, got_theirs), jnp.where(core == 0, got_theirs, got_mine)],
                          axis=1)
    full = {}
    per_chip = [_unpack_rows(got[q], big_shapes) for q in range(4)]
    for i, n in enumerate(BIG):
        full[n] = jnp.concatenate([per_chip[q][i] for q in range(4)], axis=BIG_AXIS[n])
    full["w_in"] = _pad_cols(full["w_in"])
    sm_names = tuple(SMALL_SHARDED)
    sm_shapes = [w_loc[n].shape for n in sm_names]
    (got_sm,) = _allgather_small(_pack_rows([w_loc[n] for n in sm_names]), reduce=False, name="gather_small")
    per_chip = [_unpack_rows(got_sm[4 * (q // 2) + 2 * (q % 2)], sm_shapes) for q in range(4)]
    for i, n in enumerate(sm_names):
        full[n] = jnp.concatenate([per_chip[q][i] for q in range(4)], axis=-1)

    def pad16(a):
        return jnp.zeros((1, LANES), F32).at[0, :SSD_HEADS].set(a)

    def layer_params(i):
        row = lambda n: w_loc[n][i].reshape(1, -1)
        cw = full["conv_w"][i]
        return dict(
            norm_g=row("norm_g"), w_in=full["w_in"][i], conv=[cw[k][None] for k in range(4)] + [row("conv_b")],
            dt_bias=pad16(dt_bias[i]), a_log=pad16(a_log[i]), d_skip=pad16(d_skip[i]),
            ssd_norm_g=row("ssd_norm_g"), rw_mu=row("rw_mu"),
            rw_pre=[row("rw_w0"), jnp.zeros((LANES, 512), F32).at[:HEAD].set(full["rw_w_up"][i]), row("rw_a0"),
                    jnp.zeros((LANES, 512), F32).at[HEAD:].set(full["rw_a_up"][i]), row("rw_k_k"), row("rw_k_a")],
            rw_post=[row("rw_ln_g"), row("rw_ln_b"), row("rw_r_k")],
            w_out_sb=full["w_out_sb"][i], w_out_ssd=full["w_out_ssd"][i], w_out_rw=full["w_out_rw"][i],
            w_o=full["w_o"][i])

    params = [layer_params(i) for i in range(DEPTH)]
    xs, saved = [x[0]], []
    for i in range(DEPTH):
        nxt, s = _layer_fwd(xs[-1], params[i], f"l{i}_")
        xs.append(nxt)
        saved.append(s)
    dx, loss_row, g_final = _final(xs[-1], final_g.reshape(1, -1), loss_target[0], bt=BT, name="final")
    grads = [None] * DEPTH
    for i in reversed(range(DEPTH)):
        dx, grads[i] = _layer_bwd(xs[i], dx, params[i], saved[i], f"l{i}_")

    def stacked(fn):
        return jnp.stack([fn(grads[i]) for i in range(DEPTH)])

    g_loc = {
        "norm_g": stacked(lambda g: g["norm_g"][0]),
        "w_in": stacked(lambda g: _unpad_cols(g["w_in"])),
        "conv_w": stacked(lambda g: jnp.concatenate(g["conv"][:4], axis=0)),
        "conv_b": stacked(lambda g: g["conv"][4][0]),
        "dt_bias": stacked(lambda g: g["dt_bias"][0, :SSD_HEADS]),
        "a_log": stacked(lambda g: g["a_log"][0, :SSD_HEADS]),
        "d_skip": stacked(lambda g: g["d_skip"][0, :SSD_HEADS]),
        "ssd_norm_g": stacked(lambda g: g["ssd_norm_g"][0]),
        "rw_mu": stacked(lambda g: g["rw_mu"][0]),
        "rw_w0": stacked(lambda g: g["rw_pre"][0][0]),
        "rw_w_up": stacked(lambda g: g["rw_pre"][1][:HEAD]),
        "rw_a0": stacked(lambda g: g["rw_pre"][2][0]),
        "rw_a_up": stacked(lambda g: g["rw_pre"][3][HEAD:]),
        "rw_k_k": stacked(lambda g: g["rw_pre"][4][0]),
        "rw_k_a": stacked(lambda g: g["rw_pre"][5][0]),
        "rw_r_k": stacked(lambda g: g["rw_r_k"].reshape(8, HEAD)),
        "rw_ln_g": stacked(lambda g: g["rw_ln_g"][0]),
        "rw_ln_b": stacked(lambda g: g["rw_ln_b"][0]),
        "w_out_sb": stacked(lambda g: g["w_out_sb"]),
        "w_out_ssd": stacked(lambda g: g["w_out_ssd"]),
        "w_out_rw": stacked(lambda g: g["w_out_rw"]),
        "w_o": stacked(lambda g: g["w_o"]),
        "final_g": g_final[0],
    }

    send = jnp.stack([_pack_rows([_split_chips(g_loc[n], BIG_AXIS[n])[q] for n in BIG], multiple=16) for q in range(4)])
    half = send.shape[1] // 2
    send = send.reshape(4, 2, half, LANES).transpose(1, 0, 2, 3)
    other = _sibling_send_other_half(send, name="reduce_sibling")
    part = _add_halves(send, other, core.reshape(1).astype(jnp.int32), name="reduce_add")
    parts = _chip_exchange(part, per_dest=True, name="reduce_chips")
    mine = _sum_chips(parts, name="reduce_sum")
    theirs = _sibling_swap(mine, name="reduce_join")
    total = jnp.concatenate([jnp.where(core == 0, mine, theirs), jnp.where(core == 0, theirs, mine)], axis=0)
    g_out = dict(zip(BIG, _unpack_rows(total, big_shapes)))

    sm_all = SMALL + ("loss",)
    sm_full_shapes = [g_loc[n].shape for n in SMALL] + [(1,)]
    _, summed = _allgather_small(_pack_rows([g_loc[n] for n in SMALL] + [loss_row[0, :1]]), reduce=True, name="reduce_small")
    sm = dict(zip(sm_all, _unpack_rows(summed, sm_full_shapes)))
    for n in SMALL:
        g_out[n] = sm[n]
    for n, wd in SMALL_SHARDED.items():
        g_out[n] = lax.dynamic_slice_in_dim(sm[n], chip * wd, wd, axis=sm[n].ndim - 1)
    loss = sm["loss"][0]

    upd = {n: _adamw(w_loc[n], g_out[n], m_loc[n], v_loc[n], name="adamw_" + n) for n in names}
    return (loss, dx[None], *[g_out[n] for n in names], *[upd[n][0] for n in names],
            *[upd[n][1] for n in names], *[upd[n][2] for n in names])
```

```python
import functools

import jax
import jax.numpy as jnp
from jax import lax
from jax.experimental import pallas as pl
from jax.experimental.pallas import tpu as pltpu

F32 = jnp.float32
BF16 = jnp.bfloat16

D_MODEL = 1024
DEPTH = 2
HEAD = 64
LANES = 128
CHUNK = 128
RMS_EPS = 1e-6
GN_EPS = 64e-5
VMEM_LIMIT = 56 * 1024 * 1024

N_IN = 9616
N_PAD = 9728
C_SB, C_Z, C_GATES, C_RW, C_LO, C_DT, C_XBC = 0, 2048, 3072, 6144, 8192, 8320, 8448
RW_COLS = 2176
XBC_COLS = 1280

ADAM_LR, ADAM_B1, ADAM_B2, ADAM_EPS, ADAM_WD, ADAM_STEP = 0.001, 0.9, 0.999, 1e-08, 0.01, 10


def _params(sem=None):
    return pltpu.CompilerParams(dimension_semantics=sem, vmem_limit_bytes=VMEM_LIMIT)


@jax.custom_vjp
def _sigmoid(x):
    return 1.0 / (1.0 + jnp.exp(-x))


def _sigmoid_fwd(x):
    s = _sigmoid(x)
    return s, s


def _sigmoid_bwd(s, g):
    return (g * s * (1.0 - s),)


_sigmoid.defvjp(_sigmoid_fwd, _sigmoid_bwd)


@jax.custom_vjp
def _silu(x):
    return x * _sigmoid(x)


def _silu_fwd(x):
    s = _sigmoid(x)
    return x * s, (x, s)


def _silu_bwd(res, g):
    x, s = res
    return (g * (s + x * s * (1.0 - s)),)


_silu.defvjp(_silu_fwd, _silu_bwd)


@jax.custom_vjp
def _softplus(x):
    return jnp.maximum(x, 0.0) + jnp.log(1.0 + jnp.exp(-jnp.abs(x)))


def _softplus_fwd(x):
    return _softplus(x), x


def _softplus_bwd(x, g):
    return (g * _sigmoid(x),)


_softplus.defvjp(_softplus_fwd, _softplus_bwd)


def _dot(a, b, dims):
    return lax.dot_general(a.astype(BF16), b.astype(BF16), (dims, ((), ())), preferred_element_type=F32)


def _dot_nn(a, b):
    return _dot(a, b, ((1,), (0,)))


def _dot_nt(a, b):
    return _dot(a, b, ((1,), (1,)))


def _dot_tn(a, b):
    return _dot(a, b, ((0,), (0,)))


@jax.custom_vjp
def _bdot(a, b):
    return _dot_nn(a, b)


def _bdot_fwd(a, b):
    return _dot_nn(a, b), (a, b)


def _bdot_bwd(res, g):
    a, b = res
    return _dot_nt(g, b), _dot_tn(a, g)


_bdot.defvjp(_bdot_fwd, _bdot_bwd)


def _split2(x):
    hi = x.astype(BF16)
    lo = (x - hi.astype(F32)).astype(BF16)
    return hi, lo


_NT = (((1,), (1,)), ((), ()))
_NN = (((1,), (0,)), ((), ()))
_TN = (((0,), (0,)), ((), ()))


def _dot2(x, m, dn=_NN):
    hi, lo = _split2(x)
    return (lax.dot_general(hi, m, dn, preferred_element_type=F32)
            + lax.dot_general(lo, m, dn, preferred_element_type=F32))


def _dot2_tn(x, m):
    return _dot2(x, m, _TN)


def _seg_matrix(n):
    r = lax.broadcasted_iota(jnp.int32, (n, n), 0) // HEAD
    c = lax.broadcasted_iota(jnp.int32, (n, n), 1) // HEAD
    return (r == c).astype(BF16)


@jax.custom_vjp
def _segsum2(x, seg):
    return _dot2(x, seg)


def _segsum2_fwd(x, seg):
    return _dot2(x, seg), seg


def _segsum2_bwd(seg, g):
    return _dot2(g, seg), jnp.zeros_like(seg)


_segsum2.defvjp(_segsum2_fwd, _segsum2_bwd)


def _make_segsum(seg):
    return lambda x: _segsum2(x, seg)


def _shift_down_raw(x, k):
    row = lax.broadcasted_iota(jnp.int32, x.shape, 0)
    return jnp.where(row >= k, pltpu.roll(x, k, 0), 0.0)


def _shift_up_raw(x, k):
    t = x.shape[0]
    row = lax.broadcasted_iota(jnp.int32, x.shape, 0)
    return jnp.where(row < t - k, pltpu.roll(x, t - k, 0), 0.0)


@functools.partial(jax.custom_vjp, nondiff_argnums=(1,))
def _shift_down(x, k):
    return _shift_down_raw(x, k)


def _shift_down_fwd(x, k):
    return _shift_down_raw(x, k), None


def _shift_down_bwd(k, _, g):
    return (_shift_up_raw(g, k),)


_shift_down.defvjp(_shift_down_fwd, _shift_down_bwd)


def _mm(a, b, *, name, ta=False, tb=False, add=None, out_dtype=F32, tm=2048, tn=512, tk=None):
    m, k = (a.shape[1], a.shape[0]) if ta else a.shape
    n = b.shape[0] if tb else b.shape[1]
    tm, tn = min(tm, m), min(tn, n)
    tk = k if tk is None else tk
    nk = k // tk
    assert m % tm == 0 and n % tn == 0 and k % tk == 0
    dims = ((0 if ta else 1,), (1 if tb else 0,))

    def body(a_ref, b_ref, *refs):
        o_ref, acc_ref = refs[-2:]
        p = _dot(a_ref[...], b_ref[...], dims)

        def emit(total):
            if add is not None:
                total = total + refs[0][...]
            o_ref[...] = total.astype(o_ref.dtype)

        if nk == 1:
            emit(p)
        else:
            kk = pl.program_id(2)

            @pl.when(kk == 0)
            def _():
                acc_ref[...] = p

            @pl.when(kk > 0)
            def _():
                acc_ref[...] += p

            @pl.when(kk == nk - 1)
            def _():
                emit(acc_ref[...])

    a_spec = pl.BlockSpec((tk, tm), lambda i, j, kk: (kk, i)) if ta else pl.BlockSpec((tm, tk), lambda i, j, kk: (i, kk))
    b_spec = pl.BlockSpec((tn, tk), lambda i, j, kk: (j, kk)) if tb else pl.BlockSpec((tk, tn), lambda i, j, kk: (kk, j))
    o_spec = pl.BlockSpec((tm, tn), lambda i, j, kk: (i, j))
    return pl.pallas_call(
        body, name=name, grid=(m // tm, n // tn, nk),
        in_specs=[a_spec, b_spec] + ([o_spec] if add is not None else []), out_specs=o_spec,
        out_shape=jax.ShapeDtypeStruct((m, n), out_dtype),
        scratch_shapes=[pltpu.VMEM((tm, tn) if nk > 1 else (8, LANES), F32)],
        compiler_params=_params(("parallel", "parallel", "arbitrary")),
    )(a, b, *([add] if add is not None else []))


def _row_specs(rows, bt):
    return [pl.BlockSpec((bt, w), functools.partial(lambda i, c: (i, c), c=c)) for _, w, c in rows]


def _full_spec(p):
    return pl.BlockSpec(p.shape, functools.partial(lambda i, nd: (0,) * nd, nd=p.ndim))


def _rowwise(f, rows, pars, out_widths, *, bt, name, acc_widths=()):
    t = rows[0][0].shape[0]
    nr, npar, no, na = len(rows), len(pars), len(out_widths), len(acc_widths)

    def body(*refs):
        vals = [r[...] for r in refs[:nr + npar]]
        outs = f(*vals)
        for o_ref, o in zip(refs[nr + npar:nr + npar + no], outs[:no]):
            o_ref[...] = o.astype(o_ref.dtype)
        if na:
            first = pl.program_id(0) == 0
            for a_ref, a in zip(refs[nr + npar + no:], outs[no:]):
                @pl.when(first)
                def _():
                    a_ref[...] = jnp.zeros_like(a_ref)
                a_ref[...] += a

    return pl.pallas_call(
        body, name=name, grid=(t // bt,),
        in_specs=_row_specs(rows, bt) + [_full_spec(p) for p in pars],
        out_specs=[pl.BlockSpec((bt, w), lambda i: (i, 0)) for w in out_widths]
        + [pl.BlockSpec((1, w), lambda i: (0, 0)) for w in acc_widths],
        out_shape=[jax.ShapeDtypeStruct((t, w), F32) for w in out_widths]
        + [jax.ShapeDtypeStruct((1, w), F32) for w in acc_widths],
        compiler_params=_params(("arbitrary",)),
    )(*[r[0] for r in rows], *pars)


def _rowwise_bwd(f, rows, pars, douts, *, bt, name, groups=None):
    t = rows[0][0].shape[0]
    nr, npar, nd = len(rows), len(pars), len(douts)
    groups = [[i] for i in range(nr)] if groups is None else groups
    widths = [r[1] for r in rows]

    def body(*refs):
        vals = [r[...] for r in refs[:nr + npar]]
        cts = tuple(r[...] for r in refs[nr + npar:nr + npar + nd])
        _, vjp = jax.vjp(lambda *a: tuple(f(*a)), *vals)
        grads = vjp(cts)
        out_refs = refs[nr + npar + nd:]
        for g_ref, grp in zip(out_refs[:len(groups)], groups):
            off = 0
            for i in grp:
                g_ref[:, off:off + widths[i]] = grads[i]
                off += widths[i]
        first = pl.program_id(0) == 0
        for p_ref, g in zip(out_refs[len(groups):], grads[nr:]):
            @pl.when(first)
            def _():
                p_ref[...] = jnp.zeros_like(p_ref)
            p_ref[...] += g

    gw = [sum(widths[i] for i in grp) for grp in groups]
    return pl.pallas_call(
        body, name=name, grid=(t // bt,),
        in_specs=_row_specs(rows, bt) + [_full_spec(p) for p in pars] + _row_specs(douts, bt),
        out_specs=[pl.BlockSpec((bt, w), lambda i: (i, 0)) for w in gw] + [_full_spec(p) for p in pars],
        out_shape=[jax.ShapeDtypeStruct((t, w), F32) for w in gw] + [jax.ShapeDtypeStruct(p.shape, F32) for p in pars],
        compiler_params=_params(("arbitrary",)),
    )(*[r[0] for r in rows], *pars, *[d[0] for d in douts])


def _colwise(f, x, c0, ncols, pars, *, bc, name):
    t = x.shape[0]

    def body(x_ref, *refs):
        o_ref = refs[-1]
        o_ref[...] = f(x_ref[...], *[r[...] for r in refs[:-1]])

    return pl.pallas_call(
        body, name=name, grid=(ncols // bc,),
        in_specs=[pl.BlockSpec((t, bc), lambda j: (0, j + c0 // bc))]
        + [pl.BlockSpec((p.shape[0], bc), lambda j: (0, j)) for p in pars],
        out_specs=pl.BlockSpec((t, bc), lambda j: (0, j)),
        out_shape=jax.ShapeDtypeStruct((t, ncols), F32),
        compiler_params=_params(("parallel",)),
    )(x, *pars)


def _colwise_bwd(f, x, c0, ncols, pars, dout, *, bc, name):
    t = x.shape[0]
    npar = len(pars)

    def body(x_ref, *refs):
        vals = [x_ref[...]] + [r[...] for r in refs[:npar]]
        _, vjp = jax.vjp(f, *vals)
        grads = vjp(refs[npar][...])
        for g_ref, g in zip(refs[npar + 1:], grads):
            g_ref[...] = g

    return pl.pallas_call(
        body, name=name, grid=(ncols // bc,),
        in_specs=[pl.BlockSpec((t, bc), lambda j: (0, j + c0 // bc))]
        + [pl.BlockSpec((p.shape[0], bc), lambda j: (0, j)) for p in pars]
        + [pl.BlockSpec((t, bc), lambda j: (0, j))],
        out_specs=[pl.BlockSpec((t, bc), lambda j: (0, j))]
        + [pl.BlockSpec((p.shape[0], bc), lambda j: (0, j)) for p in pars],
        out_shape=[jax.ShapeDtypeStruct((t, ncols), F32)] + [jax.ShapeDtypeStruct(p.shape, F32) for p in pars],
        compiler_params=_params(("parallel",)),
    )(x, *pars, dout)


def _f_rms(x, g):
    return (x * lax.rsqrt(jnp.mean(x * x, axis=-1, keepdims=True) + RMS_EPS) * g,)


def _f_sb_gate(y, gate):
    return (y * _silu(gate),)


def _f_ssd_norm(y, z, g):
    u = y * _silu(z)
    return (u * lax.rsqrt(jnp.mean(u * u, axis=-1, keepdims=True) + RMS_EPS) * g,)


def _f_merge(p_sb, p_ssd, p_rw, g_sb, g_ssd, g_rw):
    return (_sigmoid(g_sb) * p_sb + _sigmoid(g_ssd) * p_ssd + _sigmoid(g_rw) * p_rw,)


def _f_rw_pre(k, lo, w0, w_up, a0, a_up, k_k, k_a):
    segsum = _make_segsum(_seg_matrix(k.shape[1]))
    lane = lax.broadcasted_iota(jnp.int32, lo.shape, 1)
    w_lo = jnp.where(lane < HEAD, jnp.tanh(lo), 0.0)
    a_lo = jnp.where(lane >= HEAD, lo, 0.0)
    w = -_softplus(-(w0 + _bdot(w_lo, w_up))) - 0.5
    decay = jnp.exp(-jnp.exp(w))
    a = _sigmoid(a0 + _bdot(a_lo, a_up))
    kk = k * k_k
    kk = kk / jnp.maximum(jnp.sqrt(segsum(kk * kk)), 1e-12)
    return decay, k * (1.0 + (a - 1.0) * k_a), -kk, kk * a


def _f_rw_post(y, r, k2, v, gate, ln_g, ln_b, r_k):
    segsum = _make_segsum(_seg_matrix(y.shape[1]))
    yc = y - segsum(y) * (1.0 / HEAD)
    var = segsum(yc * yc) * (1.0 / HEAD)
    yn = yc * lax.rsqrt(var + GN_EPS) * ln_g + ln_b
    return ((yn + segsum(r * k2 * r_k) * v) * _silu(gate),)


def _f_rw_mix(slab, mu):
    return slab + (_shift_down(slab, 1) - slab) * mu


def _f_conv(x, w0, w1, w2, w3, b):
    acc = x * w3 + b
    for i, w in enumerate((w0, w1, w2)):
        acc = acc + _shift_down(x, 3 - i) * w
    return _silu(acc)


def _log_sigmoid(z):
    return jnp.minimum(z, 0.0) - jnp.log(1.0 + jnp.exp(-jnp.abs(z)))


def _prefix_matrix(kind):
    j = lax.broadcasted_iota(jnp.int32, (CHUNK, 2 * CHUNK), 0)
    s = lax.broadcasted_iota(jnp.int32, (CHUNK, 2 * CHUNK), 1)
    tri = {"gt": j > s, "le": j <= s, "lt": j < s}[kind]
    return (tri | (s >= CHUNK)).astype(BF16)


def _sb_specs(t):
    q = pl.BlockSpec((CHUNK, LANES), lambda j, i: (i, j))
    k = pl.BlockSpec((t, LANES), lambda j, i: (0, 4 + j))
    v = pl.BlockSpec((t, LANES), lambda j, i: (0, 8 + j))
    return q, k, v


def _sb_fwd(proj, *, name):
    t = proj.shape[0]
    scale = HEAD ** -0.5

    def body(q_ref, k_ref, v_ref, y_ref, lt_ref):
        i = pl.program_id(1)
        lane = lax.broadcasted_iota(jnp.int32, (CHUNK, LANES), 1)
        diff = (lax.broadcasted_iota(jnp.int32, (CHUNK, CHUNK), 1)
                - lax.broadcasted_iota(jnp.int32, (CHUNK, CHUNK), 0))
        m_f = _prefix_matrix("gt")
        q = q_ref[...] * scale
        qh = [jnp.where((lane // HEAD) == h, q, 0.0).astype(BF16) for h in (0, 1)]

        def step(it, carry):
            off = pl.multiple_of((i - it) * CHUNK, CHUNK)
            kblk = k_ref[pl.ds(off, CHUNK), :].astype(BF16)
            vblk = v_ref[pl.ds(off, CHUNK), :].astype(BF16)
            mask = diff < it * CHUNK
            new = []
            for h in (0, 1):
                c, acc = carry[2 * h], carry[2 * h + 1]
                z = lax.dot_general(qh[h], kblk, _NT, preferred_element_type=F32)
                lb = _log_sigmoid(z)
                w2 = _dot2(jnp.where(mask, lb - z, 0.0), m_f)
                att = jnp.where(mask, jnp.exp(lb + c + w2[:, :CHUNK]), 0.0)
                acc = acc + lax.dot_general(att.astype(BF16), vblk, _NN, preferred_element_type=F32)
                new += [c + w2[:, CHUNK:], acc]
            return tuple(new)

        zero = jnp.zeros((CHUNK, LANES), F32)
        c_a, acc_a, c_b, acc_b = lax.fori_loop(0, i + 1, step, (zero, zero, zero, zero))
        y_ref[...] = jnp.where(lane < HEAD, acc_a, acc_b)
        lt_ref[0] = c_a
        lt_ref[1] = c_b

    return pl.pallas_call(
        body, name=name, grid=(4, t // CHUNK),
        in_specs=list(_sb_specs(t)),
        out_specs=[pl.BlockSpec((CHUNK, LANES), lambda j, i: (i, j)),
                   pl.BlockSpec((2, CHUNK, LANES), lambda j, i: (j, i, 0))],
        out_shape=[jax.ShapeDtypeStruct((t, 4 * LANES), F32), jax.ShapeDtypeStruct((8, t, LANES), F32)],
        compiler_params=_params(("parallel", "arbitrary")),
    )(proj, proj, proj)


def _sb_bwd(proj, dy, lt, *, name):
    t = proj.shape[0]
    scale = HEAD ** -0.5

    def body(q_ref, k_ref, v_ref, dy_ref, lt_ref, dq_ref, dk_ref, dv_ref):
        i = pl.program_id(1)

        @pl.when(i == 0)
        def _():
            dk_ref[...] = jnp.zeros_like(dk_ref)
            dv_ref[...] = jnp.zeros_like(dv_ref)

        lane = lax.broadcasted_iota(jnp.int32, (CHUNK, LANES), 1)
        diff = (lax.broadcasted_iota(jnp.int32, (CHUNK, CHUNK), 1)
                - lax.broadcasted_iota(jnp.int32, (CHUNK, CHUNK), 0))
        m_le, m_lt = _prefix_matrix("le"), _prefix_matrix("lt")
        q = q_ref[...] * scale
        dy_blk = dy_ref[...]
        qh = [jnp.where((lane // HEAD) == h, q, 0.0).astype(BF16) for h in (0, 1)]
        doh = [jnp.where((lane // HEAD) == h, dy_blk, 0.0).astype(BF16) for h in (0, 1)]
        lth = [lt_ref[0], lt_ref[1]]

        def step(kb, carry):
            off = pl.multiple_of(kb * CHUNK, CHUNK)
            kblk = k_ref[pl.ds(off, CHUNK), :].astype(BF16)
            vblk = v_ref[pl.ds(off, CHUNK), :].astype(BF16)
            mask = diff < (i - kb) * CHUNK
            new = []
            dk_acc = jnp.zeros((CHUNK, LANES), F32)
            dv_acc = jnp.zeros((CHUNK, LANES), F32)
            for h in (0, 1):
                cp, cg, dq = carry[3 * h:3 * h + 3]
                z = lax.dot_general(qh[h], kblk, _NT, preferred_element_type=F32)
                lb = _log_sigmoid(z)
                w2 = _dot2(jnp.where(mask, lb - z, 0.0), m_le)
                att = jnp.where(mask, jnp.exp(lb + lth[h] - cp - w2[:, :CHUNK]), 0.0)
                d_att = lax.dot_general(doh[h], vblk, _NT, preferred_element_type=F32)
                d_e = d_att * att
                g2 = _dot2(d_e, m_lt)
                sig = jnp.exp(lb)
                dz = jnp.where(mask, d_e * (1.0 - sig) - (cg + g2[:, :CHUNK]) * sig, 0.0).astype(BF16)
                dq = dq + lax.dot_general(dz, kblk, _NN, preferred_element_type=F32)
                dk_acc = dk_acc + lax.dot_general(dz, qh[h], _TN, preferred_element_type=F32)
                dv_acc = dv_acc + lax.dot_general(att.astype(BF16), doh[h], _TN, preferred_element_type=F32)
                new += [cp + w2[:, CHUNK:], cg + g2[:, CHUNK:], dq]
            dk_ref[pl.ds(off, CHUNK), :] += dk_acc
            dv_ref[pl.ds(off, CHUNK), :] += dv_acc
            return tuple(new)

        zero = jnp.zeros((CHUNK, LANES), F32)
        out = lax.fori_loop(0, i + 1, step, (zero,) * 6)
        dq_ref[...] = jnp.where(lane < HEAD, out[2], out[5]) * scale

    q_spec, k_spec, v_spec = _sb_specs(t)
    blk = pl.BlockSpec((CHUNK, LANES), lambda j, i: (i, j))
    col = pl.BlockSpec((t, LANES), lambda j, i: (0, j))
    return pl.pallas_call(
        body, name=name, grid=(4, t // CHUNK),
        in_specs=[q_spec, k_spec, v_spec, blk, pl.BlockSpec((2, CHUNK, LANES), lambda j, i: (j, i, 0))],
        out_specs=[blk, col, col],
        out_shape=[jax.ShapeDtypeStruct((t, 4 * LANES), F32)] * 3,
        compiler_params=_params(("parallel", "arbitrary")),
    )(proj, proj, proj, dy, lt)


SB_BQ = 256
SB_BK = 256


def _tri_ones(kind):
    j = lax.broadcasted_iota(jnp.int32, (SB_BK, SB_BK + LANES), 0)
    s = lax.broadcasted_iota(jnp.int32, (SB_BK, SB_BK + LANES), 1)
    tri = {"gt": j > s, "le": j <= s, "lt": j < s}[kind]
    return (tri | (s >= SB_BK)).astype(BF16)


def _sb_common(q_ref):
    lane = lax.broadcasted_iota(jnp.int32, (SB_BQ, LANES), 1)
    q = q_ref[...] * (HEAD ** -0.5)
    q2 = jnp.concatenate([jnp.where(lane < HEAD, q, 0.0), jnp.where(lane >= HEAD, q, 0.0)], axis=0).astype(BF16)
    diff = (lax.broadcasted_iota(jnp.int32, (2 * SB_BQ, SB_BK), 1)
            - (lax.broadcasted_iota(jnp.int32, (2 * SB_BQ, SB_BK), 0) & (SB_BQ - 1)))
    return lane, q2, diff


def _rep(x):
    return jnp.concatenate([x] * (SB_BK // LANES), axis=1)


def _sb2_specs(t):
    q = pl.BlockSpec((SB_BQ, LANES), lambda j, i: (i, j))
    k = pl.BlockSpec((t, LANES), lambda j, i: (0, 4 + j))
    v = pl.BlockSpec((t, LANES), lambda j, i: (0, 8 + j))
    return q, k, v


def _sb2_fwd(proj, *, name):
    t = proj.shape[0]

    def body(q_ref, k_ref, v_ref, y_ref, lt_ref):
        i = pl.program_id(1)
        lane, q2, diff = _sb_common(q_ref)
        m_f = _tri_ones("gt")
        nk = (i + 1) * (SB_BQ // SB_BK)

        def step(it, carry):
            c, acc = carry
            kb = nk - 1 - it
            off = pl.multiple_of(kb * SB_BK, SB_BK)
            kblk = k_ref[pl.ds(off, SB_BK), :].astype(BF16)
            vblk = v_ref[pl.ds(off, SB_BK), :].astype(BF16)
            mask = diff < i * SB_BQ - kb * SB_BK
            z = lax.dot_general(q2, kblk, _NT, preferred_element_type=F32)
            lb = _log_sigmoid(z)
            w2 = _dot2(jnp.where(mask, lb - z, 0.0), m_f)
            att = jnp.where(mask, jnp.exp(lb + _rep(c) + w2[:, :SB_BK]), 0.0)
            acc = acc + lax.dot_general(att.astype(BF16), vblk, _NN, preferred_element_type=F32)
            return c + w2[:, SB_BK:], acc

        zero = jnp.zeros((2 * SB_BQ, LANES), F32)
        c, acc = lax.fori_loop(0, nk, step, (zero, zero))
        y_ref[...] = jnp.where(lane < HEAD, acc[:SB_BQ], acc[SB_BQ:])
        lt_ref[0] = c[:SB_BQ]
        lt_ref[1] = c[SB_BQ:]

    return pl.pallas_call(
        body, name=name, grid=(4, t // SB_BQ),
        in_specs=list(_sb2_specs(t)),
        out_specs=[pl.BlockSpec((SB_BQ, LANES), lambda j, i: (i, j)),
                   pl.BlockSpec((2, SB_BQ, LANES), lambda j, i: (j, i, 0))],
        out_shape=[jax.ShapeDtypeStruct((t, 4 * LANES), F32), jax.ShapeDtypeStruct((8, t, LANES), F32)],
        compiler_params=_params(("parallel", "arbitrary")),
    )(proj, proj, proj)


def _sb2_bwd(proj, dy, lt, *, name):
    t = proj.shape[0]

    def body(q_ref, k_ref, v_ref, dy_ref, lt_ref, dq_ref, dk_ref, dv_ref):
        i = pl.program_id(1)

        @pl.when(i == 0)
        def _():
            dk_ref[...] = jnp.zeros_like(dk_ref)
            dv_ref[...] = jnp.zeros_like(dv_ref)

        lane, q2, diff = _sb_common(q_ref)
        m_le, m_lt = _tri_ones("le"), _tri_ones("lt")
        dy_blk = dy_ref[...]
        do2 = jnp.concatenate([jnp.where(lane < HEAD, dy_blk, 0.0), jnp.where(lane >= HEAD, dy_blk, 0.0)],
                              axis=0).astype(BF16)
        lt2 = jnp.concatenate([lt_ref[0], lt_ref[1]], axis=0)

        def step(kb, carry):
            cp, cg, dq = carry
            off = pl.multiple_of(kb * SB_BK, SB_BK)
            kblk = k_ref[pl.ds(off, SB_BK), :].astype(BF16)
            vblk = v_ref[pl.ds(off, SB_BK), :].astype(BF16)
            mask = diff < i * SB_BQ - kb * SB_BK
            z = lax.dot_general(q2, kblk, _NT, preferred_element_type=F32)
            lb = _log_sigmoid(z)
            w2 = _dot2(jnp.where(mask, lb - z, 0.0), m_le)
            att = jnp.where(mask, jnp.exp(lb + _rep(lt2 - cp) - w2[:, :SB_BK]), 0.0)
            d_e = lax.dot_general(do2, vblk, _NT, preferred_element_type=F32) * att
            g2 = _dot2(d_e, m_lt)
            sig = jnp.exp(lb)
            dz = jnp.where(mask, d_e * (1.0 - sig) - (_rep(cg) + g2[:, :SB_BK]) * sig, 0.0).astype(BF16)
            dq = dq + lax.dot_general(dz, kblk, _NN, preferred_element_type=F32)
            dk_ref[pl.ds(off, SB_BK), :] += lax.dot_general(dz, q2, _TN, preferred_element_type=F32)
            dv_ref[pl.ds(off, SB_BK), :] += lax.dot_general(att.astype(BF16), do2, _TN, preferred_element_type=F32)
            return cp + w2[:, SB_BK:], cg + g2[:, SB_BK:], dq

        zero = jnp.zeros((2 * SB_BQ, LANES), F32)
        _, _, dq = lax.fori_loop(0, (i + 1) * (SB_BQ // SB_BK), step, (zero, zero, zero))
        dq_ref[...] = jnp.where(lane < HEAD, dq[:SB_BQ], dq[SB_BQ:]) * (HEAD ** -0.5)

    q_spec, k_spec, v_spec = _sb2_specs(t)
    blk = pl.BlockSpec((SB_BQ, LANES), lambda j, i: (i, j))
    col = pl.BlockSpec((t, LANES), lambda j, i: (0, j))
    return pl.pallas_call(
        body, name=name, grid=(4, t // SB_BQ),
        in_specs=[q_spec, k_spec, v_spec, blk, pl.BlockSpec((2, SB_BQ, LANES), lambda j, i: (j, i, 0))],
        out_specs=[blk, col, col],
        out_shape=[jax.ShapeDtypeStruct((t, 4 * LANES), F32)] * 3,
        compiler_params=_params(("parallel", "arbitrary")),
    )(proj, proj, proj, dy, lt)


SSD_HEADS = 16
SSD_PAIRS = 8


def _split3(x):
    a = x.astype(BF16)
    r = x - a.astype(F32)
    b = r.astype(BF16)
    return a, b, (r - b.astype(F32)).astype(BF16)


def _dot3(x, m, dn=_NN):
    return sum(lax.dot_general(p, m, dn, preferred_element_type=F32) for p in _split3(x))


def _mdot3(m, x):
    return sum(lax.dot_general(m, p, _NN, preferred_element_type=F32) for p in _split3(x))


def _ssd_common(dtr, dtb, alog, acsx_s, acst_s):
    lane = lax.broadcasted_iota(jnp.int32, (CHUNK, LANES), 1)
    lane1 = lax.broadcasted_iota(jnp.int32, (1, LANES), 1)
    arow = jnp.where(lane1 < SSD_HEADS, -jnp.exp(alog), 0.0)
    dt = jnp.where(lane < SSD_HEADS, _softplus(dtr + dtb), 0.0)
    da = dt * arow
    r = lax.broadcasted_iota(jnp.int32, (CHUNK, CHUNK), 0)
    c = lax.broadcasted_iota(jnp.int32, (CHUNK, CHUNK), 1)
    tril = (r >= c).astype(BF16)
    triu = (r <= c).astype(BF16)
    acs = _mdot3(tril, da)
    acst_s[...] = _dot3(da, triu, _TN)
    eh = lax.broadcasted_iota(jnp.int32, (LANES, 8 * LANES), 0)
    e = (eh == lax.broadcasted_iota(jnp.int32, (LANES, 8 * LANES), 1) // HEAD).astype(BF16)
    eh2 = lax.broadcasted_iota(jnp.int32, (LANES, 16 * LANES), 0)
    e2 = (eh2 == lax.broadcasted_iota(jnp.int32, (LANES, 16 * LANES), 1) // LANES).astype(BF16)
    acsx_s[...] = _dot3(acs, e)
    return dt, arow, _dot3(dt, e), _dot3(acs, e2), e, tril, triu


def _ssd_fwd(xc, proj, dtb, alog, dsk, *, name):
    t = xc.shape[0]
    nc = t // CHUNK

    def body(x_ref, b_ref, c_ref, dtr_ref, dtb_ref, alog_ref, dsk_ref, y_ref, hin_ref, acsx_s, acst_s, h_s):
        @pl.when(pl.program_id(0) == 0)
        def _():
            h_s[...] = jnp.zeros_like(h_s)

        dt, arow, dt_x, acs_b, e, tril, _ = _ssd_common(dtr_ref[...], dtb_ref[...], alog_ref[...], acsx_s, acst_s)
        dsk_x = _dot3(jnp.broadcast_to(dsk_ref[...], (CHUNK, LANES)), e)
        lane = lax.broadcasted_iota(jnp.int32, (CHUNK, LANES), 1)
        causal = (lax.broadcasted_iota(jnp.int32, (CHUNK, CHUNK), 0)
                  >= lax.broadcasted_iota(jnp.int32, (CHUNK, CHUNK), 1))
        for j in range(SSD_PAIRS):
            g = j // 4
            sl = slice(j * LANES, (j + 1) * LANES)
            if j % 4 == 0:
                bg = jnp.where(lane // HEAD == g, b_ref[...], 0.0)
                cg = jnp.where(lane // HEAD == g, c_ref[...], 0.0)
                cb = _dot_nt(cg, bg)
            x = x_ref[:, sl]
            a = acsx_s[:, sl]
            at = acsx_s[CHUNK - 1:CHUNK, sl]
            xdt = x * dt_x[:, sl]
            hin = h_s[j]
            hin_ref[0, j] = hin
            y = jnp.exp(a) * _dot_nn(cg, hin) + x * dsk_x[:, sl]
            h_s[j] = jnp.exp(at) * hin + _dot_tn(bg, xdt * jnp.exp(at - a))
            yd = []
            for hh in (0, 1):
                h = 2 * j + hh
                dec = jnp.exp(jnp.minimum(acs_b[:, h * LANES:(h + 1) * LANES] - acst_s[pl.ds(h, 1), :], 0.0))
                yd.append(_dot_nn(jnp.where(causal, cb * dec, 0.0), xdt))
            y_ref[:, sl] = y + jnp.where(lane < HEAD, yd[0], yd[1])

    one = pl.BlockSpec((1, LANES), lambda i: (0, 0))
    return pl.pallas_call(
        body, name=name, grid=(nc,),
        in_specs=[pl.BlockSpec((CHUNK, 8 * LANES), lambda i: (i, 0)),
                  pl.BlockSpec((CHUNK, LANES), lambda i: (i, 8)),
                  pl.BlockSpec((CHUNK, LANES), lambda i: (i, 9)),
                  pl.BlockSpec((CHUNK, LANES), lambda i: (i, C_DT // LANES)), one, one, one],
        out_specs=[pl.BlockSpec((CHUNK, 8 * LANES), lambda i: (i, 0)),
                   pl.BlockSpec((1, SSD_PAIRS, LANES, LANES), lambda i: (i, 0, 0, 0))],
        out_shape=[jax.ShapeDtypeStruct((t, 8 * LANES), F32),
                   jax.ShapeDtypeStruct((nc, SSD_PAIRS, LANES, LANES), F32)],
        scratch_shapes=[pltpu.VMEM((CHUNK, 8 * LANES), F32), pltpu.VMEM((LANES, CHUNK), F32),
                        pltpu.VMEM((SSD_PAIRS, LANES, LANES), F32)],
        compiler_params=_params(("arbitrary",)),
    )(xc, xc, xc, proj, dtb, alog, dsk)


def _ssd_bwd(xc, proj, dtb, alog, dsk, hin_all, dy, *, name):
    t = xc.shape[0]
    nc = t // CHUNK

    def body(x_ref, b_ref, c_ref, dtr_ref, dtb_ref, alog_ref, dsk_ref, hin_ref, dy_ref,
             dxc_ref, ddtr_ref, ddtb_ref, dalog_ref, ddsk_ref, acsx_s, acst_s, dh_s, dax_s, ddx_s):
        @pl.when(pl.program_id(0) == 0)
        def _():
            dh_s[...] = jnp.zeros_like(dh_s)
            ddtb_ref[...] = jnp.zeros_like(ddtb_ref)
            dalog_ref[...] = jnp.zeros_like(dalog_ref)
            ddsk_ref[...] = jnp.zeros_like(ddsk_ref)

        dtr = dtr_ref[...]
        dtb = dtb_ref[...]
        dt, arow, dt_x, acs_b, e, tril, triu = _ssd_common(dtr, dtb, alog_ref[...], acsx_s, acst_s)
        dsk_x = _dot3(jnp.broadcast_to(dsk_ref[...], (CHUNK, LANES)), e)
        lane = lax.broadcasted_iota(jnp.int32, (CHUNK, LANES), 1)
        rowi = lax.broadcasted_iota(jnp.int32, (CHUNK, LANES), 0)
        causal = (lax.broadcasted_iota(jnp.int32, (CHUNK, CHUNK), 0)
                  >= lax.broadcasted_iota(jnp.int32, (CHUNK, CHUNK), 1))
        dacs = jnp.zeros((CHUNK, LANES), F32)
        d_b = jnp.zeros((CHUNK, LANES), F32)
        d_c = jnp.zeros((CHUNK, LANES), F32)
        for j in range(SSD_PAIRS):
            g = j // 4
            sl = slice(j * LANES, (j + 1) * LANES)
            if j % 4 == 0:
                bg = jnp.where(lane // HEAD == g, b_ref[...], 0.0)
                cg = jnp.where(lane // HEAD == g, c_ref[...], 0.0)
                cb = _dot_nt(cg, bg)
                dcb = jnp.zeros((CHUNK, CHUNK), F32)
            x = x_ref[:, sl]
            d = dt_x[:, sl]
            a = acsx_s[:, sl]
            at = acsx_s[CHUNK - 1:CHUNK, sl]
            xdt = x * d
            hin = hin_ref[0, j]
            dhout = dh_s[j]
            dyp = dy_ref[:, sl]
            ea, eat, ed = jnp.exp(a), jnp.exp(at), jnp.exp(at - a)
            da_l = dyp * ea * _dot_nn(cg, hin)
            dm = dyp * ea
            d_c = d_c + _dot_nt(dm, hin)
            dh_s[j] = _dot_tn(cg, dm) + eat * dhout
            dat = jnp.sum(dhout * hin * eat, axis=0, keepdims=True)
            d_b = d_b + _dot_nt(xdt * ed, dhout)
            dw = _dot_nn(bg, dhout)
            dxdt = dw * ed
            ded = dw * xdt * ed
            dat = dat + jnp.sum(ded, axis=0, keepdims=True)
            da_l = da_l - ded
            for hh in (0, 1):
                h = 2 * j + hh
                dec = jnp.exp(jnp.minimum(acs_b[:, h * LANES:(h + 1) * LANES] - acst_s[pl.ds(h, 1), :], 0.0))
                gm = jnp.where(causal, cb * dec, 0.0)
                dyh = jnp.where(lane // HEAD == hh, dyp, 0.0)
                dg = _dot_nt(dyh, xdt)
                dxdt = dxdt + _dot_tn(gm, dyh)
                dcb = dcb + jnp.where(causal, dg * dec, 0.0)
                th = dg * gm
                oh = (lane == h).astype(BF16)
                dacs = dacs + _dot2(th, oh) - _dot2_tn(th, oh)
            if j % 4 == 3:
                d_c = d_c + _dot_nn(dcb, bg)
                d_b = d_b + _dot_tn(dcb, cg)
            dxc_ref[:, sl] = dyp * dsk_x[:, sl] + dxdt * d
            ddx_s[:, sl] = dxdt * x
            dax_s[:, sl] = da_l + jnp.where(rowi == CHUNK - 1, dat, 0.0)
            dskp = jnp.sum(dyp * x, axis=0, keepdims=True)
            ddsk_ref[...] += _dot2(jnp.broadcast_to(dskp, (8, LANES)), e[:, sl], _NT)
        dxc_ref[:, 8 * LANES:9 * LANES] = d_b
        dxc_ref[:, 9 * LANES:10 * LANES] = d_c
        dacs = dacs + _dot2(dax_s[...], e, _NT)
        ddt = _dot2(ddx_s[...], e, _NT)
        dda = _mdot3(triu, dacs)
        ddt = ddt + dda * arow
        dalog_ref[...] += jnp.sum(dda * dt, axis=0, keepdims=True) * arow
        ddtr = jnp.where(lane < SSD_HEADS, ddt * _sigmoid(dtr + dtb), 0.0)
        ddtr_ref[...] = ddtr
        ddtb_ref[...] += jnp.sum(ddtr, axis=0, keepdims=True)

    one = pl.BlockSpec((1, LANES), lambda i: (0, 0))
    rev = lambda c: (lambda i: (nc - 1 - i, c))
    return pl.pallas_call(
        body, name=name, grid=(nc,),
        in_specs=[pl.BlockSpec((CHUNK, 8 * LANES), rev(0)), pl.BlockSpec((CHUNK, LANES), rev(8)),
                  pl.BlockSpec((CHUNK, LANES), rev(9)), pl.BlockSpec((CHUNK, LANES), rev(C_DT // LANES)),
                  one, one, one,
                  pl.BlockSpec((1, SSD_PAIRS, LANES, LANES), lambda i: (nc - 1 - i, 0, 0, 0)),
                  pl.BlockSpec((CHUNK, 8 * LANES), rev(0))],
        out_specs=[pl.BlockSpec((CHUNK, XBC_COLS), rev(0)), pl.BlockSpec((CHUNK, LANES), rev(0)), one, one,
                   pl.BlockSpec((8, LANES), lambda i: (0, 0))],
        out_shape=[jax.ShapeDtypeStruct((t, XBC_COLS), F32), jax.ShapeDtypeStruct((t, LANES), F32)]
        + [jax.ShapeDtypeStruct((1, LANES), F32)] * 2 + [jax.ShapeDtypeStruct((8, LANES), F32)],
        scratch_shapes=[pltpu.VMEM((CHUNK, 8 * LANES), F32), pltpu.VMEM((LANES, CHUNK), F32),
                        pltpu.VMEM((SSD_PAIRS, LANES, LANES), F32),
                        pltpu.VMEM((CHUNK, 8 * LANES), F32), pltpu.VMEM((CHUNK, 8 * LANES), F32)],
        compiler_params=_params(("arbitrary",)),
    )(xc, xc, xc, proj, dtb, alog, dsk, hin_all, dy)


RW_PAIRS = 4
RW_BT = 16
RW_DECAY_ROW = 1


def _rw_consts():
    seg = _seg_matrix(LANES)
    ti = (lax.broadcasted_iota(jnp.int32, (HEAD, LANES), 0)
          == lax.broadcasted_iota(jnp.int32, (HEAD, LANES), 1) % HEAD)
    return seg, ti


def _col_tiles(rows, ti, seg):
    tib = ti.astype(BF16)
    hi = [r.astype(BF16) for r in rows]
    out = lax.dot_general(jnp.concatenate([tib * h for h in hi], axis=0), seg, _NN, preferred_element_type=F32)
    tiles = [out[i * HEAD:(i + 1) * HEAD] for i in range(len(rows))]
    w_lo = (rows[RW_DECAY_ROW] - hi[RW_DECAY_ROW].astype(F32)).astype(BF16)
    tiles[RW_DECAY_ROW] = tiles[RW_DECAY_ROW] + lax.dot_general(tib * w_lo, seg, _NN, preferred_element_type=F32)
    return tiles


def _head_lane_sums(tiles, ti, seg):
    out = lax.dot_general(jnp.concatenate(tiles, axis=0).astype(BF16), seg, _NN, preferred_element_type=F32)
    return [jnp.sum(jnp.where(ti, out[i * HEAD:(i + 1) * HEAD], 0.0), axis=0, keepdims=True) for i in range(len(tiles))]


def _rw_scan_fwd(mixed, w, k, n, b, *, name):
    t = w.shape[0]

    def body(r_ref, v_ref, w_ref, k_ref, n_ref, b_ref, y_ref, st_ref, s_s):
        @pl.when(pl.program_id(0) == 0)
        def _():
            s_s[...] = jnp.zeros_like(s_s)

        seg, ti = _rw_consts()

        def step(tt, state):
            row = pl.ds(tt, 1)
            new = []
            for p in range(RW_PAIRS):
                sl = pl.ds(p * LANES, LANES)
                s = state[p]
                ncol, wcol, bcol, kcol, rcol = _col_tiles(
                    [x[row, sl] for x in (n_ref, w_ref, b_ref, k_ref, r_ref)], ti, seg)
                sa = jnp.sum(s * ncol, axis=0, keepdims=True)
                s = s * wcol + bcol * sa + kcol * v_ref[row, sl]
                y_ref[row, sl] = jnp.sum(s * rcol, axis=0, keepdims=True)
                st_ref[tt, p] = s
                new.append(s)
            return tuple(new)

        out = tuple(s_s[p] for p in range(RW_PAIRS))
        for tt in range(RW_BT):
            out = step(tt, out)
        for p in range(RW_PAIRS):
            s_s[p] = out[p]

    blk = lambda c: pl.BlockSpec((RW_BT, 4 * LANES), functools.partial(lambda i, c: (i, c), c=c))
    return pl.pallas_call(
        body, name=name, grid=(t // RW_BT,),
        in_specs=[blk(0), blk(2), blk(0), blk(0), blk(0), blk(0)],
        out_specs=[blk(0), pl.BlockSpec((RW_BT, RW_PAIRS, HEAD, LANES), lambda i: (i, 0, 0, 0))],
        out_shape=[jax.ShapeDtypeStruct((t, 4 * LANES), F32),
                   jax.ShapeDtypeStruct((t, RW_PAIRS, HEAD, LANES), F32)],
        scratch_shapes=[pltpu.VMEM((RW_PAIRS, HEAD, LANES), F32)],
        compiler_params=_params(("arbitrary",)),
    )(mixed, mixed, w, k, n, b)


def _rw_scan_bwd(mixed, w, k, n, b, states, dy, dr0, dk0, dv0, *, name):
    t = w.shape[0]
    nb = t // RW_BT

    def body(r_ref, v_ref, w_ref, k_ref, n_ref, b_ref, st_ref, prev_ref, dy_ref, dr0_ref, dk0_ref, dv0_ref,
             dr_ref, dw_ref, dk_ref, dv_ref, dn_ref, db_ref, ds_s):
        @pl.when(pl.program_id(0) == 0)
        def _():
            ds_s[...] = jnp.zeros_like(ds_s)

        seg, ti = _rw_consts()
        has_prev = (pl.program_id(0) < nb - 1).astype(F32)

        def step(it, carry):
            tt = RW_BT - 1 - it
            row = pl.ds(tt, 1)
            prev_t = max(tt - 1, 0)
            new_ds, new_s = [], []
            for p in range(RW_PAIRS):
                sl = pl.ds(p * LANES, LANES)
                ds, s_t = carry[p], carry[RW_PAIRS + p]
                s_p = st_ref[prev_t, p] if tt > 0 else prev_ref[0, p] * has_prev
                ncol, wcol, bcol, kcol, rcol = _col_tiles(
                    [x[row, sl] for x in (n_ref, w_ref, b_ref, k_ref, r_ref)], ti, seg)
                vv, dyy = v_ref[row, sl], dy_ref[row, sl]
                sa = jnp.sum(s_p * ncol, axis=0, keepdims=True)
                ds = ds + rcol * dyy
                dsa = jnp.sum(ds * bcol, axis=0, keepdims=True)
                dv_ref[row, sl] = jnp.sum(ds * kcol, axis=0, keepdims=True) + dv0_ref[row, sl]
                dr, dw, db, dk, dn = _head_lane_sums([s_t * dyy, ds * s_p, ds * sa, ds * vv, s_p * dsa], ti, seg)
                dr_ref[row, sl] = dr + dr0_ref[row, sl]
                dw_ref[row, sl] = dw
                db_ref[row, sl] = db
                dk_ref[row, sl] = dk + dk0_ref[row, sl]
                dn_ref[row, sl] = dn
                new_ds.append(ds * wcol + ncol * dsa)
                new_s.append(s_p)
            return tuple(new_ds) + tuple(new_s)

        init = tuple(ds_s[p] for p in range(RW_PAIRS)) + tuple(st_ref[RW_BT - 1, p] for p in range(RW_PAIRS))
        out = init
        for it in range(RW_BT):
            out = step(it, out)
        for p in range(RW_PAIRS):
            ds_s[p] = out[p]

    blk = lambda c: pl.BlockSpec((RW_BT, 4 * LANES), functools.partial(lambda i, c: (nb - 1 - i, c), c=c))
    st_spec = pl.BlockSpec((RW_BT, RW_PAIRS, HEAD, LANES), lambda i: (nb - 1 - i, 0, 0, 0))
    prev_spec = pl.BlockSpec((1, RW_PAIRS, HEAD, LANES), lambda i: (jnp.maximum((nb - 1 - i) * RW_BT - 1, 0), 0, 0, 0))
    return pl.pallas_call(
        body, name=name, grid=(nb,),
        in_specs=[blk(0), blk(2), blk(0), blk(0), blk(0), blk(0), st_spec, prev_spec, blk(0), blk(0), blk(0), blk(0)],
        out_specs=[blk(0)] * 6,
        out_shape=[jax.ShapeDtypeStruct((t, 4 * LANES), F32)] * 6,
        scratch_shapes=[pltpu.VMEM((RW_PAIRS, HEAD, LANES), F32)],
        compiler_params=_params(("arbitrary",)),
    )(mixed, mixed, w, k, n, b, states, states, dy, dr0, dk0, dv0)


def _f_rms_res(x, g):
    return _f_rms(x, g)[0], x


def _final(x, g, target, *, bt, name):
    t, d = x.shape

    def body(x_ref, g_ref, t_ref, dx_ref, loss_ref, dg_ref):
        tgt = t_ref[...]

        def f(xv, gv):
            err = _f_rms(xv, gv)[0] - tgt
            return 0.5 * jnp.mean(err * err, axis=-1, keepdims=True)

        row_loss, vjp = jax.vjp(f, x_ref[...], g_ref[...])
        dx, dg = vjp(jnp.ones_like(row_loss))
        dx_ref[...] = dx

        @pl.when(pl.program_id(0) == 0)
        def _():
            loss_ref[...] = jnp.zeros_like(loss_ref)
            dg_ref[...] = jnp.zeros_like(dg_ref)

        loss_ref[...] += jnp.broadcast_to(jnp.sum(row_loss, axis=0, keepdims=True), (1, LANES))
        dg_ref[...] += dg

    blk = pl.BlockSpec((bt, d), lambda i: (i, 0))
    return pl.pallas_call(
        body, name=name, grid=(t // bt,),
        in_specs=[blk, pl.BlockSpec((1, d), lambda i: (0, 0)), blk],
        out_specs=[blk, pl.BlockSpec((1, LANES), lambda i: (0, 0)), pl.BlockSpec((1, d), lambda i: (0, 0))],
        out_shape=[jax.ShapeDtypeStruct((t, d), F32), jax.ShapeDtypeStruct((1, LANES), F32),
                   jax.ShapeDtypeStruct((1, d), F32)],
        compiler_params=_params(("arbitrary",)),
    )(x, g, target)


ADAMW_BLOCK_BYTES = 1 << 20


def _adamw(w, g, m, v, *, name):
    shape = w.shape
    c = shape[-1]
    args = [a.reshape(-1, c) for a in (w, g, m, v)]
    r = args[0].shape[0]
    br = r
    if r * c * 4 > ADAMW_BLOCK_BYTES:
        cands = [b for b in range(8, r, 8) if r % b == 0 and b * c * 4 <= ADAMW_BLOCK_BYTES]
        br = max(cands) if cands else r

    def body(w_ref, g_ref, m_ref, v_ref, d_ref, nm_ref, nv_ref):
        gv = g_ref[...]
        m_new = ADAM_B1 * m_ref[...] + (1.0 - ADAM_B1) * gv
        v_new = ADAM_B2 * v_ref[...] + (1.0 - ADAM_B2) * (gv * gv)
        m_hat = m_new / (1.0 - ADAM_B1 ** ADAM_STEP)
        v_hat = v_new / (1.0 - ADAM_B2 ** ADAM_STEP)
        d_ref[...] = -ADAM_LR * (m_hat / (jnp.sqrt(v_hat) + ADAM_EPS) + ADAM_WD * w_ref[...])
        nm_ref[...] = m_new
        nv_ref[...] = v_new

    blk = pl.BlockSpec((br, c), lambda i: (i, 0))
    outs = pl.pallas_call(
        body, name=name, grid=(r // br,), in_specs=[blk] * 4, out_specs=[blk] * 3,
        out_shape=[jax.ShapeDtypeStruct((r, c), F32)] * 3,
        compiler_params=_params(("parallel",)),
    )(*args)
    return tuple(o.reshape(shape) for o in outs)


BT = 256
BC = 128


def _layer_rows(x, proj, s):
    s = {k: s.get(k) for k in ("y_sb_raw", "y_ssd_raw", "mixed", "ys", "k2", "p_sb", "p_ssd", "p_rw")}
    return dict(
        rms=[(x, D_MODEL, 0)],
        sb_gate=[(s["y_sb_raw"], 512, 0), (proj, 512, 3)],
        ssd_norm=[(s["y_ssd_raw"], 1024, 0), (proj, 1024, C_Z // 1024)],
        rw_pre=[(s["mixed"], 512, 1), (s["mixed"], LANES, 16)],
        rw_post=[(s["ys"], 512, 0), (s["mixed"], 512, 0), (s["k2"], 512, 0), (s["mixed"], 512, 2), (s["mixed"], 512, 3)],
        merge=[(s["p_sb"], 1024, 0), (s["p_ssd"], 1024, 0), (s["p_rw"], 1024, 0),
               (proj, 1024, 3), (proj, 1024, 4), (proj, 1024, 5)],
    )


def _layer_fwd(x, p, nm):
    s = {}
    (s["h"],) = _rowwise(_f_rms, [(x, D_MODEL, 0)], [p["norm_g"]], [D_MODEL], bt=BT, name=nm + "rms")
    proj = s["proj"] = _mm(s["h"], p["w_in"], name=nm + "proj")
    s["y_sb_raw"], s["lt"] = _sb2_fwd(proj, name=nm + "sb")
    s["xc"] = _colwise(_f_conv, proj, C_XBC, XBC_COLS, p["conv"], bc=BC, name=nm + "conv")
    s["y_ssd_raw"], s["hin"] = _ssd_fwd(s["xc"], proj, p["dt_bias"], p["a_log"], p["d_skip"], name=nm + "ssd")
    s["mixed"] = _colwise(_f_rw_mix, proj, C_RW, RW_COLS, [p["rw_mu"]], bc=BC, name=nm + "mix")
    s["w"], s["k2"], s["n"], s["b"] = _rowwise(_f_rw_pre, [(s["mixed"], 512, 1), (s["mixed"], LANES, 16)], p["rw_pre"],
                                               [512] * 4, bt=BT, name=nm + "rwpre")
    s["ys"], s["st"] = _rw_scan_fwd(s["mixed"], s["w"], s["k2"], s["n"], s["b"], name=nm + "scan")
    rows = _layer_rows(x, proj, s)
    (s["y_sb"],) = _rowwise(_f_sb_gate, rows["sb_gate"], [], [512], bt=BT, name=nm + "sbgate")
    (s["y_ssd"],) = _rowwise(_f_ssd_norm, rows["ssd_norm"], [p["ssd_norm_g"]], [1024], bt=BT, name=nm + "ssdnorm")
    (s["y_rw"],) = _rowwise(_f_rw_post, rows["rw_post"], p["rw_post"], [512], bt=BT, name=nm + "rwpost")
    s["p_sb"] = _mm(s["y_sb"], p["w_out_sb"], name=nm + "osb")
    s["p_ssd"] = _mm(s["y_ssd"], p["w_out_ssd"], name=nm + "ossd")
    s["p_rw"] = _mm(s["y_rw"], p["w_out_rw"], name=nm + "orw")
    (s["merged"],) = _rowwise(_f_merge, _layer_rows(x, proj, s)["merge"], [], [1024], bt=BT, name=nm + "merge")
    return _mm(s["merged"], p["w_o"], add=x, name=nm + "wo"), s


def _layer_bwd(x, dx_out, p, s, nm):
    g = {}
    proj = s["proj"]
    rows = _layer_rows(x, proj, s)
    g["w_o"] = _mm(s["merged"], dx_out, ta=True, name=nm + "g_wo")
    d_merged = _mm(dx_out, p["w_o"], tb=True, name=nm + "d_merged")
    dp_sb, dp_ssd, dp_rw, d_gates = _rowwise_bwd(_f_merge, rows["merge"], [], [(d_merged, 1024, 0)], bt=BT,
                                                 name=nm + "merge_b", groups=[[0], [1], [2], [3, 4, 5]])
    g["w_out_sb"] = _mm(s["y_sb"], dp_sb, ta=True, name=nm + "g_osb")
    g["w_out_ssd"] = _mm(s["y_ssd"], dp_ssd, ta=True, name=nm + "g_ossd")
    g["w_out_rw"] = _mm(s["y_rw"], dp_rw, ta=True, name=nm + "g_orw")
    dy_sb = _mm(dp_sb, p["w_out_sb"], tb=True, name=nm + "d_ysb")
    dy_ssd = _mm(dp_ssd, p["w_out_ssd"], tb=True, name=nm + "d_yssd")
    dy_rw = _mm(dp_rw, p["w_out_rw"], tb=True, name=nm + "d_yrw")
    dy_sb_raw, d_sbgate = _rowwise_bwd(_f_sb_gate, rows["sb_gate"], [], [(dy_sb, 512, 0)], bt=BT, name=nm + "sbgate_b")
    dq, dk, dv = _sb2_bwd(proj, dy_sb_raw, s["lt"], name=nm + "sb_b")
    dy_ssd_raw, dz, g["ssd_norm_g"] = _rowwise_bwd(_f_ssd_norm, rows["ssd_norm"], [p["ssd_norm_g"]],
                                                   [(dy_ssd, 1024, 0)], bt=BT, name=nm + "ssdnorm_b")
    dxc, ddtr, g["dt_bias"], g["a_log"], g["d_skip"] = _ssd_bwd(
        s["xc"], proj, p["dt_bias"], p["a_log"], p["d_skip"], s["hin"], dy_ssd_raw, name=nm + "ssd_b")
    conv_out = _colwise_bwd(_f_conv, proj, C_XBC, XBC_COLS, p["conv"], dxc, bc=BC, name=nm + "conv_b")
    dxbc, g["conv"] = conv_out[0], conv_out[1:]
    dys, dr0, dk0, dv0, d_rwgate, g["rw_ln_g"], g["rw_ln_b"], g["rw_r_k"] = _rowwise_bwd(
        _f_rw_post, rows["rw_post"], p["rw_post"], [(dy_rw, 512, 0)], bt=BT, name=nm + "rwpost_b")
    dr, dw, dk2, dvv, dn, db = _rw_scan_bwd(s["mixed"], s["w"], s["k2"], s["n"], s["b"], s["st"], dys, dr0, dk0, dv0,
                                            name=nm + "scan_b")
    pre_out = _rowwise_bwd(_f_rw_pre, rows["rw_pre"], p["rw_pre"],
                           [(dw, 512, 0), (dk2, 512, 0), (dn, 512, 0), (db, 512, 0)], bt=BT, name=nm + "rwpre_b")
    dkm, dlo, g["rw_pre"] = pre_out[0], pre_out[1], pre_out[2:]
    d_mixed = jnp.concatenate([dr, dkm, dvv, d_rwgate, dlo], axis=1)
    d_slab, g["rw_mu"] = _colwise_bwd(_f_rw_mix, proj, C_RW, RW_COLS, [p["rw_mu"]], d_mixed, bc=BC, name=nm + "mix_b")
    d_proj = jnp.concatenate([dq, dk, dv, d_sbgate, dz, d_gates, d_slab, ddtr, dxbc], axis=1)
    g["w_in"] = _mm(s["h"], d_proj, ta=True, name=nm + "g_win")
    dh = _mm(d_proj, p["w_in"], tb=True, tn=1024, tk=512, name=nm + "d_h")
    dx, g["norm_g"] = _rowwise_bwd(_f_rms_res, rows["rms"], [p["norm_g"]], [(dh, D_MODEL, 0), (dx_out, D_MODEL, 0)],
                                   bt=BT, name=nm + "rms_b")
    return dx, g


MESH = pl.DeviceIdType.MESH
N_DEV = 8
_ANY = pl.BlockSpec(memory_space=pl.ANY)
_CHIP_SEMS = [pltpu.SemaphoreType.DMA((3,)), pltpu.SemaphoreType.DMA((3,)), pltpu.SemaphoreType.DMA]


def _here():
    x, y, c = lax.axis_index("x"), lax.axis_index("y"), lax.axis_index("c")
    return x, y, c, [(1 - x, y), (x, 1 - y), (1 - x, 1 - y)]


def _chip_exchange(src, *, per_dest, name):
    shape = src.shape[-2:]

    def body(src_ref, out_ref, send_sems, recv_sems, local_sem):
        x, y, c, chips = _here()
        me = 2 * x + y
        pick = (lambda q: src_ref.at[q]) if per_dest else (lambda q: src_ref.at[c])
        own = pltpu.make_async_copy(pick(me), out_ref.at[me], local_sem)
        own.start()
        sends = [pltpu.make_async_remote_copy(pick(2 * px + py), out_ref.at[me], send_sems.at[j], recv_sems.at[j],
                                              device_id=(px, py, c), device_id_type=MESH)
                 for j, (px, py) in enumerate(chips)]
        for cp in sends:
            cp.start()
        for j, (px, py) in enumerate(chips):
            pltpu.make_async_remote_copy(pick(me), out_ref.at[2 * px + py], send_sems.at[j], recv_sems.at[j],
                                         device_id=(px, py, c), device_id_type=MESH).wait_recv()
        for cp in sends:
            cp.wait_send()
        own.wait()

    return pl.pallas_call(
        body, name=name, in_specs=[_ANY], out_specs=_ANY,
        out_shape=jax.ShapeDtypeStruct((4,) + shape, src.dtype), scratch_shapes=_CHIP_SEMS,
    )(src)


def _sibling_send_other_half(src, *, name):
    def body(src_ref, out_ref, send_sem, recv_sem):
        x, y, c, _ = _here()
        cp = pltpu.make_async_remote_copy(src_ref.at[1 - c], out_ref, send_sem, recv_sem,
                                          device_id=(x, y, 1 - c), device_id_type=MESH)
        cp.start()
        cp.wait()

    return pl.pallas_call(
        body, name=name, in_specs=[_ANY], out_specs=_ANY,
        out_shape=jax.ShapeDtypeStruct(src.shape[1:], src.dtype),
        scratch_shapes=[pltpu.SemaphoreType.DMA, pltpu.SemaphoreType.DMA],
    )(src)


def _sibling_swap(src, *, name):
    def body(src_ref, out_ref, send_sem, recv_sem):
        x, y, c, _ = _here()
        cp = pltpu.make_async_remote_copy(src_ref, out_ref, send_sem, recv_sem,
                                          device_id=(x, y, 1 - c), device_id_type=MESH)
        cp.start()
        cp.wait()

    return pl.pallas_call(
        body, name=name, in_specs=[_ANY], out_specs=_ANY,
        out_shape=jax.ShapeDtypeStruct(src.shape, src.dtype),
        scratch_shapes=[pltpu.SemaphoreType.DMA, pltpu.SemaphoreType.DMA],
    )(src)


def _allgather_small(v, *, reduce, name):
    r = v.shape[0]

    def body(v_ref, out_ref, *rest):
        send_sems, recv_sems, local_sem = rest[-3:]
        x, y, c, chips = _here()
        me, sibling = (x, y, c), (x, y, 1 - c)

        def slot(px, py, pc):
            return out_ref.at[4 * px + 2 * py + pc]

        def copy(k, block, to, src=None):
            return pltpu.make_async_remote_copy(
                src_ref=slot(*block) if src is None else src, dst_ref=slot(*block),
                send_sem=send_sems.at[k], recv_sem=recv_sems.at[k], device_id=to, device_id_type=MESH)

        mine = pltpu.make_async_copy(v_ref, slot(*me), local_sem)
        mine.start()
        first = [copy(0, me, sibling, src=v_ref)]
        first += [copy(1 + j, me, (*chip, c), src=v_ref) for j, chip in enumerate(chips)]
        for cp in first:
            cp.start()
        passed = [copy(4 + j, (*chip, c), sibling) for j, chip in enumerate(chips)]
        for j, chip in enumerate(chips):
            copy(1 + j, (*chip, c), me).wait_recv()
            passed[j].start()
        copy(0, sibling, me).wait_recv()
        for j, chip in enumerate(chips):
            copy(4 + j, (*chip, 1 - c), me).wait_recv()
        for cp in first + passed:
            cp.wait_send()
        mine.wait()
        if reduce:
            total = out_ref[0]
            for d in range(1, N_DEV):
                total = total + out_ref[d]
            rest[0][...] = total

    vm = pl.BlockSpec(memory_space=pltpu.VMEM)
    out_shape = [jax.ShapeDtypeStruct((N_DEV, r, LANES), F32)] + ([jax.ShapeDtypeStruct((r, LANES), F32)] if reduce else [])
    return pl.pallas_call(
        body, name=name, in_specs=[vm], out_specs=[vm] * len(out_shape), out_shape=out_shape,
        scratch_shapes=[pltpu.SemaphoreType.DMA((7,)), pltpu.SemaphoreType.DMA((7,)), pltpu.SemaphoreType.DMA],
        compiler_params=pltpu.CompilerParams(vmem_limit_bytes=VMEM_LIMIT),
    )(v)


REDUCE_ROWS = 1952


def _add_halves(mine2, other, c_idx, *, name):
    _, nq, r, _ = mine2.shape

    def body(c_ref, a_ref, b_ref, o_ref):
        o_ref[...] = (a_ref[0] + b_ref[...]).astype(o_ref.dtype)

    blk = pl.BlockSpec((1, REDUCE_ROWS, LANES), lambda q, i, c_ref: (q, i, 0))
    return pl.pallas_call(
        body, name=name,
        grid_spec=pltpu.PrefetchScalarGridSpec(
            num_scalar_prefetch=1, grid=(nq, r // REDUCE_ROWS),
            in_specs=[pl.BlockSpec((1, 1, REDUCE_ROWS, LANES), lambda q, i, c_ref: (c_ref[0], q, i, 0)), blk],
            out_specs=blk),
        out_shape=jax.ShapeDtypeStruct((nq, r, LANES), BF16),
        compiler_params=_params(("parallel", "parallel")),
    )(c_idx, mine2, other)


def _sum_chips(parts, *, name):
    _, r, _ = parts.shape

    def body(p_ref, o_ref):
        total = p_ref[0].astype(F32)
        for q in range(1, 4):
            total = total + p_ref[q].astype(F32)
        o_ref[...] = total

    return pl.pallas_call(
        body, name=name, grid=(r // REDUCE_ROWS,),
        in_specs=[pl.BlockSpec((4, REDUCE_ROWS, LANES), lambda i: (0, i, 0))],
        out_specs=pl.BlockSpec((REDUCE_ROWS, LANES), lambda i: (i, 0)),
        out_shape=jax.ShapeDtypeStruct((r, LANES), F32),
        compiler_params=_params(("parallel",)),
    )(parts)


BIG = ("w_in", "w_out_sb", "w_out_ssd", "w_out_rw", "w_o")
BIG_AXIS = {"w_in": 2, "w_out_sb": 2, "w_out_ssd": 1, "w_out_rw": 2, "w_o": 1}
SMALL_SHARDED = {"conv_w": 320, "rw_w_up": 128, "rw_a_up": 128}
SMALL = ("norm_g", "conv_w", "conv_b", "dt_bias", "a_log", "d_skip", "ssd_norm_g", "rw_mu", "rw_w0", "rw_w_up",
         "rw_a0", "rw_a_up", "rw_k_k", "rw_k_a", "rw_r_k", "rw_ln_g", "rw_ln_b", "final_g")


def _rows_of(a):
    flat = a.reshape(-1)
    pad = (-flat.shape[0]) % LANES
    return jnp.pad(flat, (0, pad)).reshape(-1, LANES)


def _pack_rows(arrays, multiple=8):
    rows = jnp.concatenate([_rows_of(a) for a in arrays], axis=0)
    pad = (-rows.shape[0]) % multiple
    return jnp.pad(rows, ((0, pad), (0, 0)))


def _unpack_rows(rows, shapes):
    out, off = [], 0
    for shp in shapes:
        n = 1
        for d in shp:
            n *= d
        nr = -(-n // LANES)
        out.append(rows[off:off + nr].reshape(-1)[:n].reshape(shp))
        off += nr
    return out


def _pad_cols(w):
    z = jnp.zeros(w.shape[:-1] + (N_PAD - N_IN,), w.dtype)
    return jnp.concatenate([w[..., 0:3072], w[..., 6544:9616], w[..., 4368:6544], w[..., 4352:4368], z,
                            w[..., 3072:4352]], axis=-1)


def _unpad_cols(g):
    return jnp.concatenate([g[..., 0:3072], g[..., 8448:9728], g[..., 8320:8336], g[..., 6144:8320],
                            g[..., 3072:6144]], axis=-1)


def _split_chips(a, axis):
    n = a.shape[axis] // 4
    return jnp.stack([lax.slice_in_dim(a, q * n, (q + 1) * n, axis=axis) for q in range(4)])


def _join_chips(a, axis):
    return jnp.concatenate([a[q] for q in range(4)], axis=axis)


def kernel(x, norm_g, w_in, conv_w, conv_b, dt_bias, a_log, d_skip, ssd_norm_g, rw_mu, rw_w0, rw_w_up, rw_a0, rw_a_up, rw_k_k, rw_k_a, rw_r_k, rw_ln_g, rw_ln_b, w_out_sb, w_out_ssd, w_out_rw, w_o, final_g, loss_target, m_norm_g, m_w_in, m_conv_w, m_conv_b, m_dt_bias, m_a_log, m_d_skip, m_ssd_norm_g, m_rw_mu, m_rw_w0, m_rw_w_up, m_rw_a0, m_rw_a_up, m_rw_k_k, m_rw_k_a, m_rw_r_k, m_rw_ln_g, m_rw_ln_b, m_w_out_sb, m_w_out_ssd, m_w_out_rw, m_w_o, m_final_g, v_norm_g, v_w_in, v_conv_w, v_conv_b, v_dt_bias, v_a_log, v_d_skip, v_ssd_norm_g, v_rw_mu, v_rw_w0, v_rw_w_up, v_rw_a0, v_rw_a_up, v_rw_k_k, v_rw_k_a, v_rw_r_k, v_rw_ln_g, v_rw_ln_b, v_w_out_sb, v_w_out_ssd, v_w_out_rw, v_w_o, v_final_g):
    names = ("norm_g", "w_in", "conv_w", "conv_b", "dt_bias", "a_log", "d_skip", "ssd_norm_g", "rw_mu", "rw_w0",
             "rw_w_up", "rw_a0", "rw_a_up", "rw_k_k", "rw_k_a", "rw_r_k", "rw_ln_g", "rw_ln_b", "w_out_sb",
             "w_out_ssd", "w_out_rw", "w_o", "final_g")
    w_loc = dict(zip(names, (norm_g, w_in, conv_w, conv_b, dt_bias, a_log, d_skip, ssd_norm_g, rw_mu, rw_w0, rw_w_up,
                             rw_a0, rw_a_up, rw_k_k, rw_k_a, rw_r_k, rw_ln_g, rw_ln_b, w_out_sb, w_out_ssd, w_out_rw,
                             w_o, final_g)))
    m_loc = dict(zip(names, (m_norm_g, m_w_in, m_conv_w, m_conv_b, m_dt_bias, m_a_log, m_d_skip, m_ssd_norm_g,
                             m_rw_mu, m_rw_w0, m_rw_w_up, m_rw_a0, m_rw_a_up, m_rw_k_k, m_rw_k_a, m_rw_r_k,
                             m_rw_ln_g, m_rw_ln_b, m_w_out_sb, m_w_out_ssd, m_w_out_rw, m_w_o, m_final_g)))
    v_loc = dict(zip(names, (v_norm_g, v_w_in, v_conv_w, v_conv_b, v_dt_bias, v_a_log, v_d_skip, v_ssd_norm_g,
                             v_rw_mu, v_rw_w0, v_rw_w_up, v_rw_a0, v_rw_a_up, v_rw_k_k, v_rw_k_a, v_rw_r_k,
                             v_rw_ln_g, v_rw_ln_b, v_w_out_sb, v_w_out_ssd, v_w_out_rw, v_w_o, v_final_g)))
    chip = 2 * lax.axis_index("x") + lax.axis_index("y")
    core = lax.axis_index("c")

    big_shapes = [w_loc[n].shape for n in BIG]
    pack = _pack_rows([w_loc[n].astype(BF16) for n in BIG], multiple=32)
    pack_half = pack.shape[0] // 2
    got_mine = _chip_exchange(pack.reshape(2, pack_half, LANES), per_dest=False, name="gather_big")
    got_theirs = _sibling_swap(got_mine, name="gather_join")
    got = jnp.concatenate([jnp.where(core == 0, got_mine, got_theirs), jnp.where(core == 0, got_theirs, got_mine)],
                          axis=1)
    full = {}
    per_chip = [_unpack_rows(got[q], big_shapes) for q in range(4)]
    for i, n in enumerate(BIG):
        full[n] = jnp.concatenate([per_chip[q][i] for q in range(4)], axis=BIG_AXIS[n])
    full["w_in"] = _pad_cols(full["w_in"])
    sm_names = tuple(SMALL_SHARDED)
    sm_shapes = [w_loc[n].shape for n in sm_names]
    (got_sm,) = _allgather_small(_pack_rows([w_loc[n] for n in sm_names]), reduce=False, name="gather_small")
    per_chip = [_unpack_rows(got_sm[4 * (q // 2) + 2 * (q % 2)], sm_shapes) for q in range(4)]
    for i, n in enumerate(sm_names):
        full[n] = jnp.concatenate([per_chip[q][i] for q in range(4)], axis=-1)

    def pad16(a):
        return jnp.zeros((1, LANES), F32).at[0, :SSD_HEADS].set(a)

    def layer_params(i):
        row = lambda n: w_loc[n][i].reshape(1, -1)
        cw = full["conv_w"][i]
        return dict(
            norm_g=row("norm_g"), w_in=full["w_in"][i], conv=[cw[k][None] for k in range(4)] + [row("conv_b")],
            dt_bias=pad16(dt_bias[i]), a_log=pad16(a_log[i]), d_skip=pad16(d_skip[i]),
            ssd_norm_g=row("ssd_norm_g"), rw_mu=row("rw_mu"),
            rw_pre=[row("rw_w0"), jnp.zeros((LANES, 512), F32).at[:HEAD].set(full["rw_w_up"][i]), row("rw_a0"),
                    jnp.zeros((LANES, 512), F32).at[HEAD:].set(full["rw_a_up"][i]), row("rw_k_k"), row("rw_k_a")],
            rw_post=[row("rw_ln_g"), row("rw_ln_b"), row("rw_r_k")],
            w_out_sb=full["w_out_sb"][i], w_out_ssd=full["w_out_ssd"][i], w_out_rw=full["w_out_rw"][i],
            w_o=full["w_o"][i])

    params = [layer_params(i) for i in range(DEPTH)]
    xs, saved = [x[0]], []
    for i in range(DEPTH):
        nxt, s = _layer_fwd(xs[-1], params[i], f"l{i}_")
        xs.append(nxt)
        saved.append(s)
    dx, loss_row, g_final = _final(xs[-1], final_g.reshape(1, -1), loss_target[0], bt=BT, name="final")
    grads = [None] * DEPTH
    for i in reversed(range(DEPTH)):
        dx, grads[i] = _layer_bwd(xs[i], dx, params[i], saved[i], f"l{i}_")

    def stacked(fn):
        return jnp.stack([fn(grads[i]) for i in range(DEPTH)])

    g_loc = {
        "norm_g": stacked(lambda g: g["norm_g"][0]),
        "w_in": stacked(lambda g: _unpad_cols(g["w_in"])),
        "conv_w": stacked(lambda g: jnp.concatenate(g["conv"][:4], axis=0)),
        "conv_b": stacked(lambda g: g["conv"][4][0]),
        "dt_bias": stacked(lambda g: g["dt_bias"][0, :SSD_HEADS]),
        "a_log": stacked(lambda g: g["a_log"][0, :SSD_HEADS]),
        "d_skip": stacked(lambda g: g["d_skip"][0, :SSD_HEADS]),
        "ssd_norm_g": stacked(lambda g: g["ssd_norm_g"][0]),
        "rw_mu": stacked(lambda g: g["rw_mu"][0]),
        "rw_w0": stacked(lambda g: g["rw_pre"][0][0]),
        "rw_w_up": stacked(lambda g: g["rw_pre"][1][:HEAD]),
        "rw_a0": stacked(lambda g: g["rw_pre"][2][0]),
        "rw_a_up": stacked(lambda g: g["rw_pre"][3][HEAD:]),
        "rw_k_k": stacked(lambda g: g["rw_pre"][4][0]),
        "rw_k_a": stacked(lambda g: g["rw_pre"][5][0]),
        "rw_r_k": stacked(lambda g: g["rw_r_k"].reshape(8, HEAD)),
        "rw_ln_g": stacked(lambda g: g["rw_ln_g"][0]),
        "rw_ln_b": stacked(lambda g: g["rw_ln_b"][0]),
        "w_out_sb": stacked(lambda g: g["w_out_sb"]),
        "w_out_ssd": stacked(lambda g: g["w_out_ssd"]),
        "w_out_rw": stacked(lambda g: g["w_out_rw"]),
        "w_o": stacked(lambda g: g["w_o"]),
        "final_g": g_final[0],
    }

    send = jnp.stack([_pack_rows([_split_chips(g_loc[n], BIG_AXIS[n])[q] for n in BIG], multiple=16) for q in range(4)])
    half = send.shape[1] // 2
    send = send.reshape(4, 2, half, LANES).transpose(1, 0, 2, 3)
    other = _sibling_send_other_half(send, name="reduce_sibling")
    part = _add_halves(send, other, core.reshape(1).astype(jnp.int32), name="reduce_add")
    parts = _chip_exchange(part, per_dest=True, name="reduce_chips")
    mine = _sum_chips(parts, name="reduce_sum")
    theirs = _sibling_swap(mine, name="reduce_join")
    total = jnp.concatenate([jnp.where(core == 0, mine, theirs), jnp.where(core == 0, theirs, mine)], axis=0)
    g_out = dict(zip(BIG, _unpack_rows(total, big_shapes)))

    sm_all = SMALL + ("loss",)
    sm_full_shapes = [g_loc[n].shape for n in SMALL] + [(1,)]
    _, summed = _allgather_small(_pack_rows([g_loc[n] for n in SMALL] + [loss_row[0, :1]]), reduce=True, name="reduce_small")
    sm = dict(zip(sm_all, _unpack_rows(summed, sm_full_shapes)))
    for n in SMALL:
        g_out[n] = sm[n]
    for n, wd in SMALL_SHARDED.items():
        g_out[n] = lax.dynamic_slice_in_dim(sm[n], chip * wd, wd, axis=sm[n].ndim - 1)
    loss = sm["loss"][0]

    upd = {n: _adamw(w_loc[n], g_out[n], m_loc[n], v_loc[n], name="adamw_" + n) for n in names}
    return (loss, dx[None], *[g_out[n] for n in names], *[upd[n][0] for n in names],
            *[upd[n][1] for n in names], *[upd[n][2] for n in names])
```

```python
import functools

import jax
import jax.numpy as jnp
from jax import lax
from jax.experimental import pallas as pl
from jax.experimental.pallas import tpu as pltpu

F32 = jnp.float32
BF16 = jnp.bfloat16

D_MODEL = 1024
DEPTH = 2
HEAD = 64
LANES = 128
CHUNK = 128
RMS_EPS = 1e-6
GN_EPS = 64e-5
VMEM_LIMIT = 56 * 1024 * 1024

N_IN = 9616
N_PAD = 9728
C_SB, C_Z, C_GATES, C_RW, C_LO, C_DT, C_XBC = 0, 2048, 3072, 6144, 8192, 8320, 8448
RW_COLS = 2176
XBC_COLS = 1280

ADAM_LR, ADAM_B1, ADAM_B2, ADAM_EPS, ADAM_WD, ADAM_STEP = 0.001, 0.9, 0.999, 1e-08, 0.01, 10


def _params(sem=None):
    return pltpu.CompilerParams(dimension_semantics=sem, vmem_limit_bytes=VMEM_LIMIT)


@jax.custom_vjp
def _sigmoid(x):
    return 1.0 / (1.0 + jnp.exp(-x))


def _sigmoid_fwd(x):
    s = _sigmoid(x)
    return s, s


def _sigmoid_bwd(s, g):
    return (g * s * (1.0 - s),)


_sigmoid.defvjp(_sigmoid_fwd, _sigmoid_bwd)


@jax.custom_vjp
def _silu(x):
    return x * _sigmoid(x)


def _silu_fwd(x):
    s = _sigmoid(x)
    return x * s, (x, s)


def _silu_bwd(res, g):
    x, s = res
    return (g * (s + x * s * (1.0 - s)),)


_silu.defvjp(_silu_fwd, _silu_bwd)


@jax.custom_vjp
def _softplus(x):
    return jnp.maximum(x, 0.0) + jnp.log(1.0 + jnp.exp(-jnp.abs(x)))


def _softplus_fwd(x):
    return _softplus(x), x


def _softplus_bwd(x, g):
    return (g * _sigmoid(x),)


_softplus.defvjp(_softplus_fwd, _softplus_bwd)


def _dot(a, b, dims):
    return lax.dot_general(a.astype(BF16), b.astype(BF16), (dims, ((), ())), preferred_element_type=F32)


def _dot_nn(a, b):
    return _dot(a, b, ((1,), (0,)))


def _dot_nt(a, b):
    return _dot(a, b, ((1,), (1,)))


def _dot_tn(a, b):
    return _dot(a, b, ((0,), (0,)))


@jax.custom_vjp
def _bdot(a, b):
    return _dot_nn(a, b)


def _bdot_fwd(a, b):
    return _dot_nn(a, b), (a, b)


def _bdot_bwd(res, g):
    a, b = res
    return _dot_nt(g, b), _dot_tn(a, g)


_bdot.defvjp(_bdot_fwd, _bdot_bwd)


def _split2(x):
    hi = x.astype(BF16)
    lo = (x - hi.astype(F32)).astype(BF16)
    return hi, lo


_NT = (((1,), (1,)), ((), ()))
_NN = (((1,), (0,)), ((), ()))
_TN = (((0,), (0,)), ((), ()))


def _dot2(x, m, dn=_NN):
    hi, lo = _split2(x)
    return (lax.dot_general(hi, m, dn, preferred_element_type=F32)
            + lax.dot_general(lo, m, dn, preferred_element_type=F32))


def _dot2_tn(x, m):
    return _dot2(x, m, _TN)


def _seg_matrix(n):
    r = lax.broadcasted_iota(jnp.int32, (n, n), 0) // HEAD
    c = lax.broadcasted_iota(jnp.int32, (n, n), 1) // HEAD
    return (r == c).astype(BF16)


@jax.custom_vjp
def _segsum2(x, seg):
    return _dot2(x, seg)


def _segsum2_fwd(x, seg):
    return _dot2(x, seg), seg


def _segsum2_bwd(seg, g):
    return _dot2(g, seg), jnp.zeros_like(seg)


_segsum2.defvjp(_segsum2_fwd, _segsum2_bwd)


def _make_segsum(seg):
    return lambda x: _segsum2(x, seg)


def _shift_down_raw(x, k):
    row = lax.broadcasted_iota(jnp.int32, x.shape, 0)
    return jnp.where(row >= k, pltpu.roll(x, k, 0), 0.0)


def _shift_up_raw(x, k):
    t = x.shape[0]
    row = lax.broadcasted_iota(jnp.int32, x.shape, 0)
    return jnp.where(row < t - k, pltpu.roll(x, t - k, 0), 0.0)


@functools.partial(jax.custom_vjp, nondiff_argnums=(1,))
def _shift_down(x, k):
    return _shift_down_raw(x, k)


def _shift_down_fwd(x, k):
    return _shift_down_raw(x, k), None


def _shift_down_bwd(k, _, g):
    return (_shift_up_raw(g, k),)


_shift_down.defvjp(_shift_down_fwd, _shift_down_bwd)


def _mm(a, b, *, name, ta=False, tb=False, add=None, out_dtype=F32, tm=2048, tn=512, tk=None):
    m, k = (a.shape[1], a.shape[0]) if ta else a.shape
    n = b.shape[0] if tb else b.shape[1]
    tm, tn = min(tm, m), min(tn, n)
    tk = k if tk is None else tk
    nk = k // tk
    assert m % tm == 0 and n % tn == 0 and k % tk == 0
    dims = ((0 if ta else 1,), (1 if tb else 0,))

    def body(a_ref, b_ref, *refs):
        o_ref, acc_ref = refs[-2:]
        p = _dot(a_ref[...], b_ref[...], dims)

        def emit(total):
            if add is not None:
                total = total + refs[0][...]
            o_ref[...] = total.astype(o_ref.dtype)

        if nk == 1:
            emit(p)
        else:
            kk = pl.program_id(2)

            @pl.when(kk == 0)
            def _():
                acc_ref[...] = p

            @pl.when(kk > 0)
            def _():
                acc_ref[...] += p

            @pl.when(kk == nk - 1)
            def _():
                emit(acc_ref[...])

    a_spec = pl.BlockSpec((tk, tm), lambda i, j, kk: (kk, i)) if ta else pl.BlockSpec((tm, tk), lambda i, j, kk: (i, kk))
    b_spec = pl.BlockSpec((tn, tk), lambda i, j, kk: (j, kk)) if tb else pl.BlockSpec((tk, tn), lambda i, j, kk: (kk, j))
    o_spec = pl.BlockSpec((tm, tn), lambda i, j, kk: (i, j))
    return pl.pallas_call(
        body, name=name, grid=(m // tm, n // tn, nk),
        in_specs=[a_spec, b_spec] + ([o_spec] if add is not None else []), out_specs=o_spec,
        out_shape=jax.ShapeDtypeStruct((m, n), out_dtype),
        scratch_shapes=[pltpu.VMEM((tm, tn) if nk > 1 else (8, LANES), F32)],
        compiler_params=_params(("parallel", "parallel", "arbitrary")),
    )(a, b, *([add] if add is not None else []))


def _row_specs(rows, bt):
    return [pl.BlockSpec((bt, w), functools.partial(lambda i, c: (i, c), c=c)) for _, w, c in rows]


def _full_spec(p):
    return pl.BlockSpec(p.shape, functools.partial(lambda i, nd: (0,) * nd, nd=p.ndim))


def _rowwise(f, rows, pars, out_widths, *, bt, name, acc_widths=()):
    t = rows[0][0].shape[0]
    nr, npar, no, na = len(rows), len(pars), len(out_widths), len(acc_widths)

    def body(*refs):
        vals = [r[...] for r in refs[:nr + npar]]
        outs = f(*vals)
        for o_ref, o in zip(refs[nr + npar:nr + npar + no], outs[:no]):
            o_ref[...] = o.astype(o_ref.dtype)
        if na:
            first = pl.program_id(0) == 0
            for a_ref, a in zip(refs[nr + npar + no:], outs[no:]):
                @pl.when(first)
                def _():
                    a_ref[...] = jnp.zeros_like(a_ref)
                a_ref[...] += a

    return pl.pallas_call(
        body, name=name, grid=(t // bt,),
        in_specs=_row_specs(rows, bt) + [_full_spec(p) for p in pars],
        out_specs=[pl.BlockSpec((bt, w), lambda i: (i, 0)) for w in out_widths]
        + [pl.BlockSpec((1, w), lambda i: (0, 0)) for w in acc_widths],
        out_shape=[jax.ShapeDtypeStruct((t, w), F32) for w in out_widths]
        + [jax.ShapeDtypeStruct((1, w), F32) for w in acc_widths],
        compiler_params=_params(("arbitrary",)),
    )(*[r[0] for r in rows], *pars)


def _rowwise_bwd(f, rows, pars, douts, *, bt, name, groups=None):
    t = rows[0][0].shape[0]
    nr, npar, nd = len(rows), len(pars), len(douts)
    groups = [[i] for i in range(nr)] if groups is None else groups
    widths = [r[1] for r in rows]

    def body(*refs):
        vals = [r[...] for r in refs[:nr + npar]]
        cts = tuple(r[...] for r in refs[nr + npar:nr + npar + nd])
        _, vjp = jax.vjp(lambda *a: tuple(f(*a)), *vals)
        grads = vjp(cts)
        out_refs = refs[nr + npar + nd:]
        for g_ref, grp in zip(out_refs[:len(groups)], groups):
            off = 0
            for i in grp:
                g_ref[:, off:off + widths[i]] = grads[i]
                off += widths[i]
        first = pl.program_id(0) == 0
        for p_ref, g in zip(out_refs[len(groups):], grads[nr:]):
            @pl.when(first)
            def _():
                p_ref[...] = jnp.zeros_like(p_ref)
            p_ref[...] += g

    gw = [sum(widths[i] for i in grp) for grp in groups]
    return pl.pallas_call(
        body, name=name, grid=(t // bt,),
        in_specs=_row_specs(rows, bt) + [_full_spec(p) for p in pars] + _row_specs(douts, bt),
        out_specs=[pl.BlockSpec((bt, w), lambda i: (i, 0)) for w in gw] + [_full_spec(p) for p in pars],
        out_shape=[jax.ShapeDtypeStruct((t, w), F32) for w in gw] + [jax.ShapeDtypeStruct(p.shape, F32) for p in pars],
        compiler_params=_params(("arbitrary",)),
    )(*[r[0] for r in rows], *pars, *[d[0] for d in douts])


def _colwise(f, x, c0, ncols, pars, *, bc, name):
    t = x.shape[0]

    def body(x_ref, *refs):
        o_ref = refs[-1]
        o_ref[...] = f(x_ref[...], *[r[...] for r in refs[:-1]])

    return pl.pallas_call(
        body, name=name, grid=(ncols // bc,),
        in_specs=[pl.BlockSpec((t, bc), lambda j: (0, j + c0 // bc))]
        + [pl.BlockSpec((p.shape[0], bc), lambda j: (0, j)) for p in pars],
        out_specs=pl.BlockSpec((t, bc), lambda j: (0, j)),
        out_shape=jax.ShapeDtypeStruct((t, ncols), F32),
        compiler_params=_params(("parallel",)),
    )(x, *pars)


def _colwise_bwd(f, x, c0, ncols, pars, dout, *, bc, name):
    t = x.shape[0]
    npar = len(pars)

    def body(x_ref, *refs):
        vals = [x_ref[...]] + [r[...] for r in refs[:npar]]
        _, vjp = jax.vjp(f, *vals)
        grads = vjp(refs[npar][...])
        for g_ref, g in zip(refs[npar + 1:], grads):
            g_ref[...] = g

    return pl.pallas_call(
        body, name=name, grid=(ncols // bc,),
        in_specs=[pl.BlockSpec((t, bc), lambda j: (0, j + c0 // bc))]
        + [pl.BlockSpec((p.shape[0], bc), lambda j: (0, j)) for p in pars]
        + [pl.BlockSpec((t, bc), lambda j: (0, j))],
        out_specs=[pl.BlockSpec((t, bc), lambda j: (0, j))]
        + [pl.BlockSpec((p.shape[0], bc), lambda j: (0, j)) for p in pars],
        out_shape=[jax.ShapeDtypeStruct((t, ncols), F32)] + [jax.ShapeDtypeStruct(p.shape, F32) for p in pars],
        compiler_params=_params(("parallel",)),
    )(x, *pars, dout)


def _f_rms(x, g):
    return (x * lax.rsqrt(jnp.mean(x * x, axis=-1, keepdims=True) + RMS_EPS) * g,)


def _f_sb_gate(y, gate):
    return (y * _silu(gate),)


def _f_ssd_norm(y, z, g):
    u = y * _silu(z)
    return (u * lax.rsqrt(jnp.mean(u * u, axis=-1, keepdims=True) + RMS_EPS) * g,)


def _f_merge(p_sb, p_ssd, p_rw, g_sb, g_ssd, g_rw):
    return (_sigmoid(g_sb) * p_sb + _sigmoid(g_ssd) * p_ssd + _sigmoid(g_rw) * p_rw,)


def _f_rw_pre(k, lo, w0, w_up, a0, a_up, k_k, k_a):
    segsum = _make_segsum(_seg_matrix(k.shape[1]))
    lane = lax.broadcasted_iota(jnp.int32, lo.shape, 1)
    w_lo = jnp.where(lane < HEAD, jnp.tanh(lo), 0.0)
    a_lo = jnp.where(lane >= HEAD, lo, 0.0)
    w = -_softplus(-(w0 + _bdot(w_lo, w_up))) - 0.5
    decay = jnp.exp(-jnp.exp(w))
    a = _sigmoid(a0 + _bdot(a_lo, a_up))
    kk = k * k_k
    kk = kk / jnp.maximum(jnp.sqrt(segsum(kk * kk)), 1e-12)
    return decay, k * (1.0 + (a - 1.0) * k_a), -kk, kk * a


def _f_rw_post(y, r, k2, v, gate, ln_g, ln_b, r_k):
    segsum = _make_segsum(_seg_matrix(y.shape[1]))
    yc = y - segsum(y) * (1.0 / HEAD)
    var = segsum(yc * yc) * (1.0 / HEAD)
    yn = yc * lax.rsqrt(var + GN_EPS) * ln_g + ln_b
    return ((yn + segsum(r * k2 * r_k) * v) * _silu(gate),)


def _f_rw_mix(slab, mu):
    return slab + (_shift_down(slab, 1) - slab) * mu


def _f_conv(x, w0, w1, w2, w3, b):
    acc = x * w3 + b
    for i, w in enumerate((w0, w1, w2)):
        acc = acc + _shift_down(x, 3 - i) * w
    return _silu(acc)


def _log_sigmoid(z):
    return jnp.minimum(z, 0.0) - jnp.log(1.0 + jnp.exp(-jnp.abs(z)))


def _prefix_matrix(kind):
    j = lax.broadcasted_iota(jnp.int32, (CHUNK, 2 * CHUNK), 0)
    s = lax.broadcasted_iota(jnp.int32, (CHUNK, 2 * CHUNK), 1)
    tri = {"gt": j > s, "le": j <= s, "lt": j < s}[kind]
    return (tri | (s >= CHUNK)).astype(BF16)


def _sb_specs(t):
    q = pl.BlockSpec((CHUNK, LANES), lambda j, i: (i, j))
    k = pl.BlockSpec((t, LANES), lambda j, i: (0, 4 + j))
    v = pl.BlockSpec((t, LANES), lambda j, i: (0, 8 + j))
    return q, k, v


def _sb_fwd(proj, *, name):
    t = proj.shape[0]
    scale = HEAD ** -0.5

    def body(q_ref, k_ref, v_ref, y_ref, lt_ref):
        i = pl.program_id(1)
        lane = lax.broadcasted_iota(jnp.int32, (CHUNK, LANES), 1)
        diff = (lax.broadcasted_iota(jnp.int32, (CHUNK, CHUNK), 1)
                - lax.broadcasted_iota(jnp.int32, (CHUNK, CHUNK), 0))
        m_f = _prefix_matrix("gt")
        q = q_ref[...] * scale
        qh = [jnp.where((lane // HEAD) == h, q, 0.0).astype(BF16) for h in (0, 1)]

        def step(it, carry):
            off = pl.multiple_of((i - it) * CHUNK, CHUNK)
            kblk = k_ref[pl.ds(off, CHUNK), :].astype(BF16)
            vblk = v_ref[pl.ds(off, CHUNK), :].astype(BF16)
            mask = diff < it * CHUNK
            new = []
            for h in (0, 1):
                c, acc = carry[2 * h], carry[2 * h + 1]
                z = lax.dot_general(qh[h], kblk, _NT, preferred_element_type=F32)
                lb = _log_sigmoid(z)
                w2 = _dot2(jnp.where(mask, lb - z, 0.0), m_f)
                att = jnp.where(mask, jnp.exp(lb + c + w2[:, :CHUNK]), 0.0)
                acc = acc + lax.dot_general(att.astype(BF16), vblk, _NN, preferred_element_type=F32)
                new += [c + w2[:, CHUNK:], acc]
            return tuple(new)

        zero = jnp.zeros((CHUNK, LANES), F32)
        c_a, acc_a, c_b, acc_b = lax.fori_loop(0, i + 1, step, (zero, zero, zero, zero))
        y_ref[...] = jnp.where(lane < HEAD, acc_a, acc_b)
        lt_ref[0] = c_a
        lt_ref[1] = c_b

    return pl.pallas_call(
        body, name=name, grid=(4, t // CHUNK),
        in_specs=list(_sb_specs(t)),
        out_specs=[pl.BlockSpec((CHUNK, LANES), lambda j, i: (i, j)),
                   pl.BlockSpec((2, CHUNK, LANES), lambda j, i: (j, i, 0))],
        out_shape=[jax.ShapeDtypeStruct((t, 4 * LANES), F32), jax.ShapeDtypeStruct((8, t, LANES), F32)],
        compiler_params=_params(("parallel", "arbitrary")),
    )(proj, proj, proj)


def _sb_bwd(proj, dy, lt, *, name):
    t = proj.shape[0]
    scale = HEAD ** -0.5

    def body(q_ref, k_ref, v_ref, dy_ref, lt_ref, dq_ref, dk_ref, dv_ref):
        i = pl.program_id(1)

        @pl.when(i == 0)
        def _():
            dk_ref[...] = jnp.zeros_like(dk_ref)
            dv_ref[...] = jnp.zeros_like(dv_ref)

        lane = lax.broadcasted_iota(jnp.int32, (CHUNK, LANES), 1)
        diff = (lax.broadcasted_iota(jnp.int32, (CHUNK, CHUNK), 1)
                - lax.broadcasted_iota(jnp.int32, (CHUNK, CHUNK), 0))
        m_le, m_lt = _prefix_matrix("le"), _prefix_matrix("lt")
        q = q_ref[...] * scale
        dy_blk = dy_ref[...]
        qh = [jnp.where((lane // HEAD) == h, q, 0.0).astype(BF16) for h in (0, 1)]
        doh = [jnp.where((lane // HEAD) == h, dy_blk, 0.0).astype(BF16) for h in (0, 1)]
        lth = [lt_ref[0], lt_ref[1]]

        def step(kb, carry):
            off = pl.multiple_of(kb * CHUNK, CHUNK)
            kblk = k_ref[pl.ds(off, CHUNK), :].astype(BF16)
            vblk = v_ref[pl.ds(off, CHUNK), :].astype(BF16)
            mask = diff < (i - kb) * CHUNK
            new = []
            dk_acc = jnp.zeros((CHUNK, LANES), F32)
            dv_acc = jnp.zeros((CHUNK, LANES), F32)
            for h in (0, 1):
                cp, cg, dq = carry[3 * h:3 * h + 3]
                z = lax.dot_general(qh[h], kblk, _NT, preferred_element_type=F32)
                lb = _log_sigmoid(z)
                w2 = _dot2(jnp.where(mask, lb - z, 0.0), m_le)
                att = jnp.where(mask, jnp.exp(lb + lth[h] - cp - w2[:, :CHUNK]), 0.0)
                d_att = lax.dot_general(doh[h], vblk, _NT, preferred_element_type=F32)
                d_e = d_att * att
                g2 = _dot2(d_e, m_lt)
                sig = jnp.exp(lb)
                dz = jnp.where(mask, d_e * (1.0 - sig) - (cg + g2[:, :CHUNK]) * sig, 0.0).astype(BF16)
                dq = dq + lax.dot_general(dz, kblk, _NN, preferred_element_type=F32)
                dk_acc = dk_acc + lax.dot_general(dz, qh[h], _TN, preferred_element_type=F32)
                dv_acc = dv_acc + lax.dot_general(att.astype(BF16), doh[h], _TN, preferred_element_type=F32)
                new += [cp + w2[:, CHUNK:], cg + g2[:, CHUNK:], dq]
            dk_ref[pl.ds(off, CHUNK), :] += dk_acc
            dv_ref[pl.ds(off, CHUNK), :] += dv_acc
            return tuple(new)

        zero = jnp.zeros((CHUNK, LANES), F32)
        out = lax.fori_loop(0, i + 1, step, (zero,) * 6)
        dq_ref[...] = jnp.where(lane < HEAD, out[2], out[5]) * scale

    q_spec, k_spec, v_spec = _sb_specs(t)
    blk = pl.BlockSpec((CHUNK, LANES), lambda j, i: (i, j))
    col = pl.BlockSpec((t, LANES), lambda j, i: (0, j))
    return pl.pallas_call(
        body, name=name, grid=(4, t // CHUNK),
        in_specs=[q_spec, k_spec, v_spec, blk, pl.BlockSpec((2, CHUNK, LANES), lambda j, i: (j, i, 0))],
        out_specs=[blk, col, col],
        out_shape=[jax.ShapeDtypeStruct((t, 4 * LANES), F32)] * 3,
        compiler_params=_params(("parallel", "arbitrary")),
    )(proj, proj, proj, dy, lt)


SB_BQ = 256
SB_BK = 256


def _tri_ones(kind):
    j = lax.broadcasted_iota(jnp.int32, (SB_BK, SB_BK + LANES), 0)
    s = lax.broadcasted_iota(jnp.int32, (SB_BK, SB_BK + LANES), 1)
    tri = {"gt": j > s, "le": j <= s, "lt": j < s}[kind]
    return (tri | (s >= SB_BK)).astype(BF16)


def _sb_common(q_ref):
    lane = lax.broadcasted_iota(jnp.int32, (SB_BQ, LANES), 1)
    q = q_ref[...] * (HEAD ** -0.5)
    q2 = jnp.concatenate([jnp.where(lane < HEAD, q, 0.0), jnp.where(lane >= HEAD, q, 0.0)], axis=0).astype(BF16)
    diff = (lax.broadcasted_iota(jnp.int32, (2 * SB_BQ, SB_BK), 1)
            - (lax.broadcasted_iota(jnp.int32, (2 * SB_BQ, SB_BK), 0) & (SB_BQ - 1)))
    return lane, q2, diff


def _rep(x):
    return jnp.concatenate([x] * (SB_BK // LANES), axis=1)


def _sb2_specs(t):
    q = pl.BlockSpec((SB_BQ, LANES), lambda j, i: (i, j))
    k = pl.BlockSpec((t, LANES), lambda j, i: (0, 4 + j))
    v = pl.BlockSpec((t, LANES), lambda j, i: (0, 8 + j))
    return q, k, v


def _sb2_fwd(proj, *, name):
    t = proj.shape[0]

    def body(q_ref, k_ref, v_ref, y_ref, lt_ref):
        i = pl.program_id(1)
        lane, q2, diff = _sb_common(q_ref)
        m_f = _tri_ones("gt")
        nk = (i + 1) * (SB_BQ // SB_BK)

        def step(it, carry):
            c, acc = carry
            kb = nk - 1 - it
            off = pl.multiple_of(kb * SB_BK, SB_BK)
            kblk = k_ref[pl.ds(off, SB_BK), :].astype(BF16)
            vblk = v_ref[pl.ds(off, SB_BK), :].astype(BF16)
            mask = diff < i * SB_BQ - kb * SB_BK
            z = lax.dot_general(q2, kblk, _NT, preferred_element_type=F32)
            lb = _log_sigmoid(z)
            w2 = _dot2(jnp.where(mask, lb - z, 0.0), m_f)
            att = jnp.where(mask, jnp.exp(lb + _rep(c) + w2[:, :SB_BK]), 0.0)
            acc = acc + lax.dot_general(att.astype(BF16), vblk, _NN, preferred_element_type=F32)
            return c + w2[:, SB_BK:], acc

        zero = jnp.zeros((2 * SB_BQ, LANES), F32)
        c, acc = lax.fori_loop(0, nk, step, (zero, zero))
        y_ref[...] = jnp.where(lane < HEAD, acc[:SB_BQ], acc[SB_BQ:])
        lt_ref[0] = c[:SB_BQ]
        lt_ref[1] = c[SB_BQ:]

    return pl.pallas_call(
        body, name=name, grid=(4, t // SB_BQ),
        in_specs=list(_sb2_specs(t)),
        out_specs=[pl.BlockSpec((SB_BQ, LANES), lambda j, i: (i, j)),
                   pl.BlockSpec((2, SB_BQ, LANES), lambda j, i: (j, i, 0))],
        out_shape=[jax.ShapeDtypeStruct((t, 4 * LANES), F32), jax.ShapeDtypeStruct((8, t, LANES), F32)],
        compiler_params=_params(("parallel", "arbitrary")),
    )(proj, proj, proj)


def _sb2_bwd(proj, dy, lt, *, name):
    t = proj.shape[0]

    def body(q_ref, k_ref, v_ref, dy_ref, lt_ref, dq_ref, dk_ref, dv_ref):
        i = pl.program_id(1)

        @pl.when(i == 0)
        def _():
            dk_ref[...] = jnp.zeros_like(dk_ref)
            dv_ref[...] = jnp.zeros_like(dv_ref)

        lane, q2, diff = _sb_common(q_ref)
        m_le, m_lt = _tri_ones("le"), _tri_ones("lt")
        dy_blk = dy_ref[...]
        do2 = jnp.concatenate([jnp.where(lane < HEAD, dy_blk, 0.0), jnp.where(lane >= HEAD, dy_blk, 0.0)],
                              axis=0).astype(BF16)
        lt2 = jnp.concatenate([lt_ref[0], lt_ref[1]], axis=0)

        def step(kb, carry):
            cp, cg, dq = carry
            off = pl.multiple_of(kb * SB_BK, SB_BK)
            kblk = k_ref[pl.ds(off, SB_BK), :].astype(BF16)
            vblk = v_ref[pl.ds(off, SB_BK), :].astype(BF16)
            mask = diff < i * SB_BQ - kb * SB_BK
            z = lax.dot_general(q2, kblk, _NT, preferred_element_type=F32)
            lb = _log_sigmoid(z)
            w2 = _dot2(jnp.where(mask, lb - z, 0.0), m_le)
            att = jnp.where(mask, jnp.exp(lb + _rep(lt2 - cp) - w2[:, :SB_BK]), 0.0)
            d_e = lax.dot_general(do2, vblk, _NT, preferred_element_type=F32) * att
            g2 = _dot2(d_e, m_lt)
            sig = jnp.exp(lb)
            dz = jnp.where(mask, d_e * (1.0 - sig) - (_rep(cg) + g2[:, :SB_BK]) * sig, 0.0).astype(BF16)
            dq = dq + lax.dot_general(dz, kblk, _NN, preferred_element_type=F32)
            dk_ref[pl.ds(off, SB_BK), :] += lax.dot_general(dz, q2, _TN, preferred_element_type=F32)
            dv_ref[pl.ds(off, SB_BK), :] += lax.dot_general(att.astype(BF16), do2, _TN, preferred_element_type=F32)
            return cp + w2[:, SB_BK:], cg + g2[:, SB_BK:], dq

        zero = jnp.zeros((2 * SB_BQ, LANES), F32)
        _, _, dq = lax.fori_loop(0, (i + 1) * (SB_BQ // SB_BK), step, (zero, zero, zero))
        dq_ref[...] = jnp.where(lane < HEAD, dq[:SB_BQ], dq[SB_BQ:]) * (HEAD ** -0.5)

    q_spec, k_spec, v_spec = _sb2_specs(t)
    blk = pl.BlockSpec((SB_BQ, LANES), lambda j, i: (i, j))
    col = pl.BlockSpec((t, LANES), lambda j, i: (0, j))
    return pl.pallas_call(
        body, name=name, grid=(4, t // SB_BQ),
        in_specs=[q_spec, k_spec, v_spec, blk, pl.BlockSpec((2, SB_BQ, LANES), lambda j, i: (j, i, 0))],
        out_specs=[blk, col, col],
        out_shape=[jax.ShapeDtypeStruct((t, 4 * LANES), F32)] * 3,
        compiler_params=_params(("parallel", "arbitrary")),
    )(proj, proj, proj, dy, lt)


SSD_HEADS = 16
SSD_PAIRS = 8


def _split3(x):
    a = x.astype(BF16)
    r = x - a.astype(F32)
    b = r.astype(BF16)
    return a, b, (r - b.astype(F32)).astype(BF16)


def _dot3(x, m, dn=_NN):
    return sum(lax.dot_general(p, m, dn, preferred_element_type=F32) for p in _split3(x))


def _mdot3(m, x):
    return sum(lax.dot_general(m, p, _NN, preferred_element_type=F32) for p in _split3(x))


def _ssd_common(dtr, dtb, alog, acsx_s, acst_s):
    lane = lax.broadcasted_iota(jnp.int32, (CHUNK, LANES), 1)
    lane1 = lax.broadcasted_iota(jnp.int32, (1, LANES), 1)
    arow = jnp.where(lane1 < SSD_HEADS, -jnp.exp(alog), 0.0)
    dt = jnp.where(lane < SSD_HEADS, _softplus(dtr + dtb), 0.0)
    da = dt * arow
    r = lax.broadcasted_iota(jnp.int32, (CHUNK, CHUNK), 0)
    c = lax.broadcasted_iota(jnp.int32, (CHUNK, CHUNK), 1)
    tril = (r >= c).astype(BF16)
    triu = (r <= c).astype(BF16)
    acs = _mdot3(tril, da)
    acst_s[...] = _dot3(da, triu, _TN)
    eh = lax.broadcasted_iota(jnp.int32, (LANES, 8 * LANES), 0)
    e = (eh == lax.broadcasted_iota(jnp.int32, (LANES, 8 * LANES), 1) // HEAD).astype(BF16)
    eh2 = lax.broadcasted_iota(jnp.int32, (LANES, 16 * LANES), 0)
    e2 = (eh2 == lax.broadcasted_iota(jnp.int32, (LANES, 16 * LANES), 1) // LANES).astype(BF16)
    acsx_s[...] = _dot3(acs, e)
    return dt, arow, _dot3(dt, e), _dot3(acs, e2), e, tril, triu


def _ssd_fwd(xc, proj, dtb, alog, dsk, *, name):
    t = xc.shape[0]
    nc = t // CHUNK

    def body(x_ref, b_ref, c_ref, dtr_ref, dtb_ref, alog_ref, dsk_ref, y_ref, hin_ref, acsx_s, acst_s, h_s):
        @pl.when(pl.program_id(0) == 0)
        def _():
            h_s[...] = jnp.zeros_like(h_s)

        dt, arow, dt_x, acs_b, e, tril, _ = _ssd_common(dtr_ref[...], dtb_ref[...], alog_ref[...], acsx_s, acst_s)
        dsk_x = _dot3(jnp.broadcast_to(dsk_ref[...], (CHUNK, LANES)), e)
        lane = lax.broadcasted_iota(jnp.int32, (CHUNK, LANES), 1)
        causal = (lax.broadcasted_iota(jnp.int32, (CHUNK, CHUNK), 0)
                  >= lax.broadcasted_iota(jnp.int32, (CHUNK, CHUNK), 1))
        for j in range(SSD_PAIRS):
            g = j // 4
            sl = slice(j * LANES, (j + 1) * LANES)
            if j % 4 == 0:
                bg = jnp.where(lane // HEAD == g, b_ref[...], 0.0)
                cg = jnp.where(lane // HEAD == g, c_ref[...], 0.0)
                cb = _dot_nt(cg, bg)
            x = x_ref[:, sl]
            a = acsx_s[:, sl]
            at = acsx_s[CHUNK - 1:CHUNK, sl]
            xdt = x * dt_x[:, sl]
            hin = h_s[j]
            hin_ref[0, j] = hin
            y = jnp.exp(a) * _dot_nn(cg, hin) + x * dsk_x[:, sl]
            h_s[j] = jnp.exp(at) * hin + _dot_tn(bg, xdt * jnp.exp(at - a))
            yd = []
            for hh in (0, 1):
                h = 2 * j + hh
                dec = jnp.exp(jnp.minimum(acs_b[:, h * LANES:(h + 1) * LANES] - acst_s[pl.ds(h, 1), :], 0.0))
                yd.append(_dot_nn(jnp.where(causal, cb * dec, 0.0), xdt))
            y_ref[:, sl] = y + jnp.where(lane < HEAD, yd[0], yd[1])

    one = pl.BlockSpec((1, LANES), lambda i: (0, 0))
    return pl.pallas_call(
        body, name=name, grid=(nc,),
        in_specs=[pl.BlockSpec((CHUNK, 8 * LANES), lambda i: (i, 0)),
                  pl.BlockSpec((CHUNK, LANES), lambda i: (i, 8)),
                  pl.BlockSpec((CHUNK, LANES), lambda i: (i, 9)),
                  pl.BlockSpec((CHUNK, LANES), lambda i: (i, C_DT // LANES)), one, one, one],
        out_specs=[pl.BlockSpec((CHUNK, 8 * LANES), lambda i: (i, 0)),
                   pl.BlockSpec((1, SSD_PAIRS, LANES, LANES), lambda i: (i, 0, 0, 0))],
        out_shape=[jax.ShapeDtypeStruct((t, 8 * LANES), F32),
                   jax.ShapeDtypeStruct((nc, SSD_PAIRS, LANES, LANES), F32)],
        scratch_shapes=[pltpu.VMEM((CHUNK, 8 * LANES), F32), pltpu.VMEM((LANES, CHUNK), F32),
                        pltpu.VMEM((SSD_PAIRS, LANES, LANES), F32)],
        compiler_params=_params(("arbitrary",)),
    )(xc, xc, xc, proj, dtb, alog, dsk)


def _ssd_bwd(xc, proj, dtb, alog, dsk, hin_all, dy, *, name):
    t = xc.shape[0]
    nc = t // CHUNK

    def body(x_ref, b_ref, c_ref, dtr_ref, dtb_ref, alog_ref, dsk_ref, hin_ref, dy_ref,
             dxc_ref, ddtr_ref, ddtb_ref, dalog_ref, ddsk_ref, acsx_s, acst_s, dh_s, dax_s, ddx_s):
        @pl.when(pl.program_id(0) == 0)
        def _():
            dh_s[...] = jnp.zeros_like(dh_s)
            ddtb_ref[...] = jnp.zeros_like(ddtb_ref)
            dalog_ref[...] = jnp.zeros_like(dalog_ref)
            ddsk_ref[...] = jnp.zeros_like(ddsk_ref)

        dtr = dtr_ref[...]
        dtb = dtb_ref[...]
        dt, arow, dt_x, acs_b, e, tril, triu = _ssd_common(dtr, dtb, alog_ref[...], acsx_s, acst_s)
        dsk_x = _dot3(jnp.broadcast_to(dsk_ref[...], (CHUNK, LANES)), e)
        lane = lax.broadcasted_iota(jnp.int32, (CHUNK, LANES), 1)
        rowi = lax.broadcasted_iota(jnp.int32, (CHUNK, LANES), 0)
        causal = (lax.broadcasted_iota(jnp.int32, (CHUNK, CHUNK), 0)
                  >= lax.broadcasted_iota(jnp.int32, (CHUNK, CHUNK), 1))
        dacs = jnp.zeros((CHUNK, LANES), F32)
        d_b = jnp.zeros((CHUNK, LANES), F32)
        d_c = jnp.zeros((CHUNK, LANES), F32)
        for j in range(SSD_PAIRS):
            g = j // 4
            sl = slice(j * LANES, (j + 1) * LANES)
            if j % 4 == 0:
                bg = jnp.where(lane // HEAD == g, b_ref[...], 0.0)
                cg = jnp.where(lane // HEAD == g, c_ref[...], 0.0)
                cb = _dot_nt(cg, bg)
                dcb = jnp.zeros((CHUNK, CHUNK), F32)
            x = x_ref[:, sl]
            d = dt_x[:, sl]
            a = acsx_s[:, sl]
            at = acsx_s[CHUNK - 1:CHUNK, sl]
            xdt = x * d
            hin = hin_ref[0, j]
            dhout = dh_s[j]
            dyp = dy_ref[:, sl]
            ea, eat, ed = jnp.exp(a), jnp.exp(at), jnp.exp(at - a)
            da_l = dyp * ea * _dot_nn(cg, hin)
            dm = dyp * ea
            d_c = d_c + _dot_nt(dm, hin)
            dh_s[j] = _dot_tn(cg, dm) + eat * dhout
            dat = jnp.sum(dhout * hin * eat, axis=0, keepdims=True)
            d_b = d_b + _dot_nt(xdt * ed, dhout)
            dw = _dot_nn(bg, dhout)
            dxdt = dw * ed
            ded = dw * xdt * ed
            dat = dat + jnp.sum(ded, axis=0, keepdims=True)
            da_l = da_l - ded
            for hh in (0, 1):
                h = 2 * j + hh
                dec = jnp.exp(jnp.minimum(acs_b[:, h * LANES:(h + 1) * LANES] - acst_s[pl.ds(h, 1), :], 0.0))
                gm = jnp.where(causal, cb * dec, 0.0)
                dyh = jnp.where(lane // HEAD == hh, dyp, 0.0)
                dg = _dot_nt(dyh, xdt)
                dxdt = dxdt + _dot_tn(gm, dyh)
                dcb = dcb + jnp.where(causal, dg * dec, 0.0)
                th = dg * gm
                oh = (lane == h).astype(BF16)
                dacs = dacs + _dot2(th, oh) - _dot2_tn(th, oh)
            if j % 4 == 3:
                d_c = d_c + _dot_nn(dcb, bg)
                d_b = d_b + _dot_tn(dcb, cg)
            dxc_ref[:, sl] = dyp * dsk_x[:, sl] + dxdt * d
            ddx_s[:, sl] = dxdt * x
            dax_s[:, sl] = da_l + jnp.where(rowi == CHUNK - 1, dat, 0.0)
            dskp = jnp.sum(dyp * x, axis=0, keepdims=True)
            ddsk_ref[...] += _dot2(jnp.broadcast_to(dskp, (8, LANES)), e[:, sl], _NT)
        dxc_ref[:, 8 * LANES:9 * LANES] = d_b
        dxc_ref[:, 9 * LANES:10 * LANES] = d_c
        dacs = dacs + _dot2(dax_s[...], e, _NT)
        ddt = _dot2(ddx_s[...], e, _NT)
        dda = _mdot3(triu, dacs)
        ddt = ddt + dda * arow
        dalog_ref[...] += jnp.sum(dda * dt, axis=0, keepdims=True) * arow
        ddtr = jnp.where(lane < SSD_HEADS, ddt * _sigmoid(dtr + dtb), 0.0)
        ddtr_ref[...] = ddtr
        ddtb_ref[...] += jnp.sum(ddtr, axis=0, keepdims=True)

    one = pl.BlockSpec((1, LANES), lambda i: (0, 0))
    rev = lambda c: (lambda i: (nc - 1 - i, c))
    return pl.pallas_call(
        body, name=name, grid=(nc,),
        in_specs=[pl.BlockSpec((CHUNK, 8 * LANES), rev(0)), pl.BlockSpec((CHUNK, LANES), rev(8)),
                  pl.BlockSpec((CHUNK, LANES), rev(9)), pl.BlockSpec((CHUNK, LANES), rev(C_DT // LANES)),
                  one, one, one,
                  pl.BlockSpec((1, SSD_PAIRS, LANES, LANES), lambda i: (nc - 1 - i, 0, 0, 0)),
                  pl.BlockSpec((CHUNK, 8 * LANES), rev(0))],
        out_specs=[pl.BlockSpec((CHUNK, XBC_COLS), rev(0)), pl.BlockSpec((CHUNK, LANES), rev(0)), one, one,
                   pl.BlockSpec((8, LANES), lambda i: (0, 0))],
        out_shape=[jax.ShapeDtypeStruct((t, XBC_COLS), F32), jax.ShapeDtypeStruct((t, LANES), F32)]
        + [jax.ShapeDtypeStruct((1, LANES), F32)] * 2 + [jax.ShapeDtypeStruct((8, LANES), F32)],
        scratch_shapes=[pltpu.VMEM((CHUNK, 8 * LANES), F32), pltpu.VMEM((LANES, CHUNK), F32),
                        pltpu.VMEM((SSD_PAIRS, LANES, LANES), F32),
                        pltpu.VMEM((CHUNK, 8 * LANES), F32), pltpu.VMEM((CHUNK, 8 * LANES), F32)],
        compiler_params=_params(("arbitrary",)),
    )(xc, xc, xc, proj, dtb, alog, dsk, hin_all, dy)


RW_PAIRS = 4
RW_BT = 16
RW_DECAY_ROW = 1
RW_BWD_PAIRS = 4


def _rw_consts():
    seg = _seg_matrix(LANES)
    ti = (lax.broadcasted_iota(jnp.int32, (HEAD, LANES), 0)
          == lax.broadcasted_iota(jnp.int32, (HEAD, LANES), 1) % HEAD)
    return seg, ti


def _col_tiles(rows, ti, seg):
    tib = ti.astype(BF16)
    n = len(rows)
    hi = [r.astype(BF16) for r in rows]
    w_lo = (rows[RW_DECAY_ROW] - hi[RW_DECAY_ROW].astype(F32)).astype(BF16)
    out = lax.dot_general(jnp.concatenate([tib * h for h in hi + [w_lo]], axis=0), seg, _NN, preferred_element_type=F32)
    tiles = [out[i * HEAD:(i + 1) * HEAD] for i in range(n)]
    tiles[RW_DECAY_ROW] = tiles[RW_DECAY_ROW] + out[n * HEAD:(n + 1) * HEAD]
    return tiles


def _col_tiles2(rows, ti, seg):
    tib = ti.astype(BF16)
    hi = [r.astype(BF16) for r in rows]
    lo = [(r - h.astype(F32)).astype(BF16) for r, h in zip(rows, hi)]
    out = (lax.dot_general(jnp.concatenate([tib * h for h in hi], axis=0), seg, _NN, preferred_element_type=F32)
           + lax.dot_general(jnp.concatenate([tib * l for l in lo], axis=0), seg, _NN, preferred_element_type=F32))
    return [out[i * HEAD:(i + 1) * HEAD] for i in range(len(rows))]


def _head_lane_sums(tiles, ti, seg):
    out = _dot2(jnp.concatenate(tiles, axis=0), seg)
    return [jnp.sum(jnp.where(ti, out[i * HEAD:(i + 1) * HEAD], 0.0), axis=0, keepdims=True) for i in range(len(tiles))]


def _rw_scan_fwd(mixed, w, k, n, b, *, name):
    t = w.shape[0]

    def body(r_ref, v_ref, w_ref, k_ref, n_ref, b_ref, y_ref, st_ref, s_s):
        @pl.when(pl.program_id(0) == 0)
        def _():
            s_s[...] = jnp.zeros_like(s_s)

        seg, ti = _rw_consts()

        def step(tt, state):
            row = pl.ds(tt, 1)
            new = []
            for p in range(RW_PAIRS):
                sl = pl.ds(p * LANES, LANES)
                s = state[p]
                ncol, wcol, bcol, kcol, rcol = _col_tiles(
                    [x[row, sl] for x in (n_ref, w_ref, b_ref, k_ref, r_ref)], ti, seg)
                sa = jnp.sum(s * ncol, axis=0, keepdims=True)
                s = s * wcol + bcol * sa + kcol * v_ref[row, sl]
                y_ref[row, sl] = jnp.sum(s * rcol, axis=0, keepdims=True)
                st_ref[tt, p] = s
                new.append(s)
            return tuple(new)

        out = tuple(s_s[p] for p in range(RW_PAIRS))
        for tt in range(RW_BT):
            out = step(tt, out)
        for p in range(RW_PAIRS):
            s_s[p] = out[p]

    blk = lambda c: pl.BlockSpec((RW_BT, 4 * LANES), functools.partial(lambda i, c: (i, c), c=c))
    return pl.pallas_call(
        body, name=name, grid=(t // RW_BT,),
        in_specs=[blk(0), blk(2), blk(0), blk(0), blk(0), blk(0)],
        out_specs=[blk(0), pl.BlockSpec((RW_BT, RW_PAIRS, HEAD, LANES), lambda i: (i, 0, 0, 0))],
        out_shape=[jax.ShapeDtypeStruct((t, 4 * LANES), F32),
                   jax.ShapeDtypeStruct((t, RW_PAIRS, HEAD, LANES), F32)],
        scratch_shapes=[pltpu.VMEM((RW_PAIRS, HEAD, LANES), F32)],
        compiler_params=_params(("arbitrary",)),
    )(mixed, mixed, w, k, n, b)


def _rw_scan_bwd(mixed, w, k, n, b, states, dy, dr0, dk0, dv0, *, name):
    t = w.shape[0]
    nb = t // RW_BT
    ppc = RW_BWD_PAIRS
    ng = RW_PAIRS // ppc

    def body(r_ref, v_ref, w_ref, k_ref, n_ref, b_ref, st_ref, prev_ref, dy_ref, dr0_ref, dk0_ref, dv0_ref,
             dr_ref, dw_ref, dk_ref, dv_ref, dn_ref, db_ref, ds_s):
        @pl.when(pl.program_id(1) == 0)
        def _():
            ds_s[...] = jnp.zeros_like(ds_s)

        seg, ti = _rw_consts()
        has_prev = (pl.program_id(1) < nb - 1).astype(F32)

        def step(it, carry):
            tt = RW_BT - 1 - it
            row = pl.ds(tt, 1)
            prev_t = max(tt - 1, 0)
            new_ds, new_s = [], []
            for p in range(ppc):
                sl = pl.ds(p * LANES, LANES)
                ds, s_t = carry[p], carry[ppc + p]
                s_p = st_ref[prev_t, p] if tt > 0 else prev_ref[0, p] * has_prev
                ncol, wcol, bcol, kcol, rcol = _col_tiles2(
                    [x[row, sl] for x in (n_ref, w_ref, b_ref, k_ref, r_ref)], ti, seg)
                vv, dyy = v_ref[row, sl], dy_ref[row, sl]
                sa = jnp.sum(s_p * ncol, axis=0, keepdims=True)
                ds = ds + rcol * dyy
                dsa = jnp.sum(ds * bcol, axis=0, keepdims=True)
                dv_ref[row, sl] = jnp.sum(ds * kcol, axis=0, keepdims=True) + dv0_ref[row, sl]
                dr, dw, db, dk, dn = _head_lane_sums([s_t * dyy, ds * s_p, ds * sa, ds * vv, s_p * dsa], ti, seg)
                dr_ref[row, sl] = dr + dr0_ref[row, sl]
                dw_ref[row, sl] = dw
                db_ref[row, sl] = db
                dk_ref[row, sl] = dk + dk0_ref[row, sl]
                dn_ref[row, sl] = dn
                new_ds.append(ds * wcol + ncol * dsa)
                new_s.append(s_p)
            return tuple(new_ds) + tuple(new_s)

        init = tuple(ds_s[p] for p in range(ppc)) + tuple(st_ref[RW_BT - 1, p] for p in range(ppc))
        out = init
        for it in range(RW_BT):
            out = step(it, out)
        for p in range(ppc):
            ds_s[p] = out[p]

    blk = lambda c: pl.BlockSpec((RW_BT, ppc * LANES), functools.partial(lambda g, i, c: (nb - 1 - i, c * ng + g), c=c))
    st_spec = pl.BlockSpec((RW_BT, ppc, HEAD, LANES), lambda g, i: (nb - 1 - i, g, 0, 0))
    prev_spec = pl.BlockSpec((1, ppc, HEAD, LANES), lambda g, i: (jnp.maximum((nb - 1 - i) * RW_BT - 1, 0), g, 0, 0))
    return pl.pallas_call(
        body, name=name, grid=(ng, nb),
        in_specs=[blk(0), blk(2), blk(0), blk(0), blk(0), blk(0), st_spec, prev_spec, blk(0), blk(0), blk(0), blk(0)],
        out_specs=[blk(0)] * 6,
        out_shape=[jax.ShapeDtypeStruct((t, 4 * LANES), F32)] * 6,
        scratch_shapes=[pltpu.VMEM((ppc, HEAD, LANES), F32)],
        compiler_params=_params(("parallel", "arbitrary")),
    )(mixed, mixed, w, k, n, b, states, states, dy, dr0, dk0, dv0)


def _f_rms_res(x, g):
    return _f_rms(x, g)[0], x


def _final(x, g, target, *, bt, name):
    t, d = x.shape

    def body(x_ref, g_ref, t_ref, dx_ref, loss_ref, dg_ref):
        tgt = t_ref[...]

        def f(xv, gv):
            err = _f_rms(xv, gv)[0] - tgt
            return 0.5 * jnp.mean(err * err, axis=-1, keepdims=True)

        row_loss, vjp = jax.vjp(f, x_ref[...], g_ref[...])
        dx, dg = vjp(jnp.ones_like(row_loss))
        dx_ref[...] = dx

        @pl.when(pl.program_id(0) == 0)
        def _():
            loss_ref[...] = jnp.zeros_like(loss_ref)
            dg_ref[...] = jnp.zeros_like(dg_ref)

        loss_ref[...] += jnp.broadcast_to(jnp.sum(row_loss, axis=0, keepdims=True), (1, LANES))
        dg_ref[...] += dg

    blk = pl.BlockSpec((bt, d), lambda i: (i, 0))
    return pl.pallas_call(
        body, name=name, grid=(t // bt,),
        in_specs=[blk, pl.BlockSpec((1, d), lambda i: (0, 0)), blk],
        out_specs=[blk, pl.BlockSpec((1, LANES), lambda i: (0, 0)), pl.BlockSpec((1, d), lambda i: (0, 0))],
        out_shape=[jax.ShapeDtypeStruct((t, d), F32), jax.ShapeDtypeStruct((1, LANES), F32),
                   jax.ShapeDtypeStruct((1, d), F32)],
        compiler_params=_params(("arbitrary",)),
    )(x, g, target)


ADAMW_BLOCK_BYTES = 1 << 20


def _adamw(w, g, m, v, *, name):
    shape = w.shape
    c = shape[-1]
    args = [a.reshape(-1, c) for a in (w, g, m, v)]
    r = args[0].shape[0]
    br = r
    if r * c * 4 > ADAMW_BLOCK_BYTES:
        cands = [b for b in range(8, r, 8) if r % b == 0 and b * c * 4 <= ADAMW_BLOCK_BYTES]
        br = max(cands) if cands else r

    def body(w_ref, g_ref, m_ref, v_ref, d_ref, nm_ref, nv_ref):
        gv = g_ref[...]
        m_new = ADAM_B1 * m_ref[...] + (1.0 - ADAM_B1) * gv
        v_new = ADAM_B2 * v_ref[...] + (1.0 - ADAM_B2) * (gv * gv)
        m_hat = m_new / (1.0 - ADAM_B1 ** ADAM_STEP)
        v_hat = v_new / (1.0 - ADAM_B2 ** ADAM_STEP)
        d_ref[...] = -ADAM_LR * (m_hat / (jnp.sqrt(v_hat) + ADAM_EPS) + ADAM_WD * w_ref[...])
        nm_ref[...] = m_new
        nv_ref[...] = v_new

    blk = pl.BlockSpec((br, c), lambda i: (i, 0))
    outs = pl.pallas_call(
        body, name=name, grid=(r // br,), in_specs=[blk] * 4, out_specs=[blk] * 3,
        out_shape=[jax.ShapeDtypeStruct((r, c), F32)] * 3,
        compiler_params=_params(("parallel",)),
    )(*args)
    return tuple(o.reshape(shape) for o in outs)


BT = 256
BC = 128


def _layer_rows(x, proj, s):
    s = {k: s.get(k) for k in ("y_sb_raw", "y_ssd_raw", "mixed", "ys", "k2", "p_sb", "p_ssd", "p_rw")}
    return dict(
        rms=[(x, D_MODEL, 0)],
        sb_gate=[(s["y_sb_raw"], 512, 0), (proj, 512, 3)],
        ssd_norm=[(s["y_ssd_raw"], 1024, 0), (proj, 1024, C_Z // 1024)],
        rw_pre=[(s["mixed"], 512, 1), (s["mixed"], LANES, 16)],
        rw_post=[(s["ys"], 512, 0), (s["mixed"], 512, 0), (s["k2"], 512, 0), (s["mixed"], 512, 2), (s["mixed"], 512, 3)],
        merge=[(s["p_sb"], 1024, 0), (s["p_ssd"], 1024, 0), (s["p_rw"], 1024, 0),
               (proj, 1024, 3), (proj, 1024, 4), (proj, 1024, 5)],
    )


def _layer_fwd(x, p, nm):
    s = {}
    (s["h"],) = _rowwise(_f_rms, [(x, D_MODEL, 0)], [p["norm_g"]], [D_MODEL], bt=BT, name=nm + "rms")
    proj = s["proj"] = _mm(s["h"], p["w_in"], name=nm + "proj")
    s["y_sb_raw"], s["lt"] = _sb2_fwd(proj, name=nm + "sb")
    s["xc"] = _colwise(_f_conv, proj, C_XBC, XBC_COLS, p["conv"], bc=BC, name=nm + "conv")
    s["y_ssd_raw"], s["hin"] = _ssd_fwd(s["xc"], proj, p["dt_bias"], p["a_log"], p["d_skip"], name=nm + "ssd")
    s["mixed"] = _colwise(_f_rw_mix, proj, C_RW, RW_COLS, [p["rw_mu"]], bc=BC, name=nm + "mix")
    s["w"], s["k2"], s["n"], s["b"] = _rowwise(_f_rw_pre, [(s["mixed"], 512, 1), (s["mixed"], LANES, 16)], p["rw_pre"],
                                               [512] * 4, bt=BT, name=nm + "rwpre")
    s["ys"], s["st"] = _rw_scan_fwd(s["mixed"], s["w"], s["k2"], s["n"], s["b"], name=nm + "scan")
    rows = _layer_rows(x, proj, s)
    (s["y_sb"],) = _rowwise(_f_sb_gate, rows["sb_gate"], [], [512], bt=BT, name=nm + "sbgate")
    (s["y_ssd"],) = _rowwise(_f_ssd_norm, rows["ssd_norm"], [p["ssd_norm_g"]], [1024], bt=BT, name=nm + "ssdnorm")
    (s["y_rw"],) = _rowwise(_f_rw_post, rows["rw_post"], p["rw_post"], [512], bt=BT, name=nm + "rwpost")
    s["p_sb"] = _mm(s["y_sb"], p["w_out_sb"], name=nm + "osb")
    s["p_ssd"] = _mm(s["y_ssd"], p["w_out_ssd"], name=nm + "ossd")
    s["p_rw"] = _mm(s["y_rw"], p["w_out_rw"], name=nm + "orw")
    (s["merged"],) = _rowwise(_f_merge, _layer_rows(x, proj, s)["merge"], [], [1024], bt=BT, name=nm + "merge")
    return _mm(s["merged"], p["w_o"], add=x, name=nm + "wo"), s


def _layer_bwd(x, dx_out, p, s, nm):
    g = {}
    proj = s["proj"]
    rows = _layer_rows(x, proj, s)
    g["w_o"] = _mm(s["merged"], dx_out, ta=True, name=nm + "g_wo")
    d_merged = _mm(dx_out, p["w_o"], tb=True, name=nm + "d_merged")
    dp_sb, dp_ssd, dp_rw, d_gates = _rowwise_bwd(_f_merge, rows["merge"], [], [(d_merged, 1024, 0)], bt=BT,
                                                 name=nm + "merge_b", groups=[[0], [1], [2], [3, 4, 5]])
    g["w_out_sb"] = _mm(s["y_sb"], dp_sb, ta=True, name=nm + "g_osb")
    g["w_out_ssd"] = _mm(s["y_ssd"], dp_ssd, ta=True, name=nm + "g_ossd")
    g["w_out_rw"] = _mm(s["y_rw"], dp_rw, ta=True, name=nm + "g_orw")
    dy_sb = _mm(dp_sb, p["w_out_sb"], tb=True, name=nm + "d_ysb")
    dy_ssd = _mm(dp_ssd, p["w_out_ssd"], tb=True, name=nm + "d_yssd")
    dy_rw = _mm(dp_rw, p["w_out_rw"], tb=True, name=nm + "d_yrw")
    dy_sb_raw, d_sbgate = _rowwise_bwd(_f_sb_gate, rows["sb_gate"], [], [(dy_sb, 512, 0)], bt=BT, name=nm + "sbgate_b")
    dq, dk, dv = _sb2_bwd(proj, dy_sb_raw, s["lt"], name=nm + "sb_b")
    dy_ssd_raw, dz, g["ssd_norm_g"] = _rowwise_bwd(_f_ssd_norm, rows["ssd_norm"], [p["ssd_norm_g"]],
                                                   [(dy_ssd, 1024, 0)], bt=BT, name=nm + "ssdnorm_b")
    dxc, ddtr, g["dt_bias"], g["a_log"], g["d_skip"] = _ssd_bwd(
        s["xc"], proj, p["dt_bias"], p["a_log"], p["d_skip"], s["hin"], dy_ssd_raw, name=nm + "ssd_b")
    conv_out = _colwise_bwd(_f_conv, proj, C_XBC, XBC_COLS, p["conv"], dxc, bc=BC, name=nm + "conv_b")
    dxbc, g["conv"] = conv_out[0], conv_out[1:]
    dys, dr0, dk0, dv0, d_rwgate, g["rw_ln_g"], g["rw_ln_b"], g["rw_r_k"] = _rowwise_bwd(
        _f_rw_post, rows["rw_post"], p["rw_post"], [(dy_rw, 512, 0)], bt=BT, name=nm + "rwpost_b")
    dr, dw, dk2, dvv, dn, db = _rw_scan_bwd(s["mixed"], s["w"], s["k2"], s["n"], s["b"], s["st"], dys, dr0, dk0, dv0,
                                            name=nm + "scan_b")
    pre_out = _rowwise_bwd(_f_rw_pre, rows["rw_pre"], p["rw_pre"],
                           [(dw, 512, 0), (dk2, 512, 0), (dn, 512, 0), (db, 512, 0)], bt=BT, name=nm + "rwpre_b")
    dkm, dlo, g["rw_pre"] = pre_out[0], pre_out[1], pre_out[2:]
    d_mixed = jnp.concatenate([dr, dkm, dvv, d_rwgate, dlo], axis=1)
    d_slab, g["rw_mu"] = _colwise_bwd(_f_rw_mix, proj, C_RW, RW_COLS, [p["rw_mu"]], d_mixed, bc=BC, name=nm + "mix_b")
    d_proj = jnp.concatenate([dq, dk, dv, d_sbgate, dz, d_gates, d_slab, ddtr, dxbc], axis=1)
    g["w_in"] = _mm(s["h"], d_proj, ta=True, name=nm + "g_win")
    dh = _mm(d_proj, p["w_in"], tb=True, tn=1024, tk=512, name=nm + "d_h")
    dx, g["norm_g"] = _rowwise_bwd(_f_rms_res, rows["rms"], [p["norm_g"]], [(dh, D_MODEL, 0), (dx_out, D_MODEL, 0)],
                                   bt=BT, name=nm + "rms_b")
    return dx, g


MESH = pl.DeviceIdType.MESH
N_DEV = 8
_ANY = pl.BlockSpec(memory_space=pl.ANY)
_CHIP_SEMS = [pltpu.SemaphoreType.DMA((3,)), pltpu.SemaphoreType.DMA((3,)), pltpu.SemaphoreType.DMA]


def _here():
    x, y, c = lax.axis_index("x"), lax.axis_index("y"), lax.axis_index("c")
    return x, y, c, [(1 - x, y), (x, 1 - y), (1 - x, 1 - y)]


def _chip_exchange(src, *, per_dest, name):
    shape = src.shape[-2:]

    def body(src_ref, out_ref, send_sems, recv_sems, local_sem):
        x, y, c, chips = _here()
        me = 2 * x + y
        pick = (lambda q: src_ref.at[q]) if per_dest else (lambda q: src_ref.at[c])
        own = pltpu.make_async_copy(pick(me), out_ref.at[me], local_sem)
        own.start()
        sends = [pltpu.make_async_remote_copy(pick(2 * px + py), out_ref.at[me], send_sems.at[j], recv_sems.at[j],
                                              device_id=(px, py, c), device_id_type=MESH)
                 for j, (px, py) in enumerate(chips)]
        for cp in sends:
            cp.start()
        for j, (px, py) in enumerate(chips):
            pltpu.make_async_remote_copy(pick(me), out_ref.at[2 * px + py], send_sems.at[j], recv_sems.at[j],
                                         device_id=(px, py, c), device_id_type=MESH).wait_recv()
        for cp in sends:
            cp.wait_send()
        own.wait()

    return pl.pallas_call(
        body, name=name, in_specs=[_ANY], out_specs=_ANY,
        out_shape=jax.ShapeDtypeStruct((4,) + shape, src.dtype), scratch_shapes=_CHIP_SEMS,
    )(src)


def _sibling_send_other_half(src, *, name):
    def body(src_ref, out_ref, send_sem, recv_sem):
        x, y, c, _ = _here()
        cp = pltpu.make_async_remote_copy(src_ref.at[1 - c], out_ref, send_sem, recv_sem,
                                          device_id=(x, y, 1 - c), device_id_type=MESH)
        cp.start()
        cp.wait()

    return pl.pallas_call(
        body, name=name, in_specs=[_ANY], out_specs=_ANY,
        out_shape=jax.ShapeDtypeStruct(src.shape[1:], src.dtype),
        scratch_shapes=[pltpu.SemaphoreType.DMA, pltpu.SemaphoreType.DMA],
    )(src)


def _sibling_swap(src, *, name):
    def body(src_ref, out_ref, send_sem, recv_sem):
        x, y, c, _ = _here()
        cp = pltpu.make_async_remote_copy(src_ref, out_ref, send_sem, recv_sem,
                                          device_id=(x, y, 1 - c), device_id_type=MESH)
        cp.start()
        cp.wait()

    return pl.pallas_call(
        body, name=name, in_specs=[_ANY], out_specs=_ANY,
        out_shape=jax.ShapeDtypeStruct(src.shape, src.dtype),
        scratch_shapes=[pltpu.SemaphoreType.DMA, pltpu.SemaphoreType.DMA],
    )(src)


def _allgather_small(v, *, reduce, name):
    r = v.shape[0]

    def body(v_ref, out_ref, *rest):
        send_sems, recv_sems, local_sem = rest[-3:]
        x, y, c, chips = _here()
        me, sibling = (x, y, c), (x, y, 1 - c)

        def slot(px, py, pc):
            return out_ref.at[4 * px + 2 * py + pc]

        def copy(k, block, to, src=None):
            return pltpu.make_async_remote_copy(
                src_ref=slot(*block) if src is None else src, dst_ref=slot(*block),
                send_sem=send_sems.at[k], recv_sem=recv_sems.at[k], device_id=to, device_id_type=MESH)

        mine = pltpu.make_async_copy(v_ref, slot(*me), local_sem)
        mine.start()
        first = [copy(0, me, sibling, src=v_ref)]
        first += [copy(1 + j, me, (*chip, c), src=v_ref) for j, chip in enumerate(chips)]
        for cp in first:
            cp.start()
        passed = [copy(4 + j, (*chip, c), sibling) for j, chip in enumerate(chips)]
        for j, chip in enumerate(chips):
            copy(1 + j, (*chip, c), me).wait_recv()
            passed[j].start()
        copy(0, sibling, me).wait_recv()
        for j, chip in enumerate(chips):
            copy(4 + j, (*chip, 1 - c), me).wait_recv()
        for cp in first + passed:
            cp.wait_send()
        mine.wait()
        if reduce:
            total = out_ref[0]
            for d in range(1, N_DEV):
                total = total + out_ref[d]
            rest[0][...] = total

    vm = pl.BlockSpec(memory_space=pltpu.VMEM)
    out_shape = [jax.ShapeDtypeStruct((N_DEV, r, LANES), F32)] + ([jax.ShapeDtypeStruct((r, LANES), F32)] if reduce else [])
    return pl.pallas_call(
        body, name=name, in_specs=[vm], out_specs=[vm] * len(out_shape), out_shape=out_shape,
        scratch_shapes=[pltpu.SemaphoreType.DMA((7,)), pltpu.SemaphoreType.DMA((7,)), pltpu.SemaphoreType.DMA],
        compiler_params=pltpu.CompilerParams(vmem_limit_bytes=VMEM_LIMIT),
    )(v)


REDUCE_ROWS = 1952


def _add_halves(mine2, other, c_idx, *, name):
    _, nq, r, _ = mine2.shape

    def body(c_ref, a_ref, b_ref, o_ref):
        o_ref[...] = (a_ref[0] + b_ref[...]).astype(o_ref.dtype)

    blk = pl.BlockSpec((1, REDUCE_ROWS, LANES), lambda q, i, c_ref: (q, i, 0))
    return pl.pallas_call(
        body, name=name,
        grid_spec=pltpu.PrefetchScalarGridSpec(
            num_scalar_prefetch=1, grid=(nq, r // REDUCE_ROWS),
            in_specs=[pl.BlockSpec((1, 1, REDUCE_ROWS, LANES), lambda q, i, c_ref: (c_ref[0], q, i, 0)), blk],
            out_specs=blk),
        out_shape=jax.ShapeDtypeStruct((nq, r, LANES), BF16),
        compiler_params=_params(("parallel", "parallel")),
    )(c_idx, mine2, other)


def _sum_chips(parts, *, name):
    _, r, _ = parts.shape

    def body(p_ref, o_ref):
        total = p_ref[0].astype(F32)
        for q in range(1, 4):
            total = total + p_ref[q].astype(F32)
        o_ref[...] = total

    return pl.pallas_call(
        body, name=name, grid=(r // REDUCE_ROWS,),
        in_specs=[pl.BlockSpec((4, REDUCE_ROWS, LANES), lambda i: (0, i, 0))],
        out_specs=pl.BlockSpec((REDUCE_ROWS, LANES), lambda i: (i, 0)),
        out_shape=jax.ShapeDtypeStruct((r, LANES), F32),
        compiler_params=_params(("parallel",)),
    )(parts)


BIG = ("w_in", "w_out_sb", "w_out_ssd", "w_out_rw", "w_o")
BIG_AXIS = {"w_in": 2, "w_out_sb": 2, "w_out_ssd": 1, "w_out_rw": 2, "w_o": 1}
SMALL_SHARDED = {"conv_w": 320, "rw_w_up": 128, "rw_a_up": 128}
SMALL = ("norm_g", "conv_w", "conv_b", "dt_bias", "a_log", "d_skip", "ssd_norm_g", "rw_mu", "rw_w0", "rw_w_up",
         "rw_a0", "rw_a_up", "rw_k_k", "rw_k_a", "rw_r_k", "rw_ln_g", "rw_ln_b", "final_g")


def _rows_of(a):
    flat = a.reshape(-1)
    pad = (-flat.shape[0]) % LANES
    return jnp.pad(flat, (0, pad)).reshape(-1, LANES)


def _pack_rows(arrays, multiple=8):
    rows = jnp.concatenate([_rows_of(a) for a in arrays], axis=0)
    pad = (-rows.shape[0]) % multiple
    return jnp.pad(rows, ((0, pad), (0, 0)))


def _unpack_rows(rows, shapes):
    out, off = [], 0
    for shp in shapes:
        n = 1
        for d in shp:
            n *= d
        nr = -(-n // LANES)
        out.append(rows[off:off + nr].reshape(-1)[:n].reshape(shp))
        off += nr
    return out


def _pad_cols(w):
    z = jnp.zeros(w.shape[:-1] + (N_PAD - N_IN,), w.dtype)
    return jnp.concatenate([w[..., 0:3072], w[..., 6544:9616], w[..., 4368:6544], w[..., 4352:4368], z,
                            w[..., 3072:4352]], axis=-1)


def _unpad_cols(g):
    return jnp.concatenate([g[..., 0:3072], g[..., 8448:9728], g[..., 8320:8336], g[..., 6144:8320],
                            g[..., 3072:6144]], axis=-1)


def _split_chips(a, axis):
    n = a.shape[axis] // 4
    return jnp.stack([lax.slice_in_dim(a, q * n, (q + 1) * n, axis=axis) for q in range(4)])


def _join_chips(a, axis):
    return jnp.concatenate([a[q] for q in range(4)], axis=axis)


def kernel(x, norm_g, w_in, conv_w, conv_b, dt_bias, a_log, d_skip, ssd_norm_g, rw_mu, rw_w0, rw_w_up, rw_a0, rw_a_up, rw_k_k, rw_k_a, rw_r_k, rw_ln_g, rw_ln_b, w_out_sb, w_out_ssd, w_out_rw, w_o, final_g, loss_target, m_norm_g, m_w_in, m_conv_w, m_conv_b, m_dt_bias, m_a_log, m_d_skip, m_ssd_norm_g, m_rw_mu, m_rw_w0, m_rw_w_up, m_rw_a0, m_rw_a_up, m_rw_k_k, m_rw_k_a, m_rw_r_k, m_rw_ln_g, m_rw_ln_b, m_w_out_sb, m_w_out_ssd, m_w_out_rw, m_w_o, m_final_g, v_norm_g, v_w_in, v_conv_w, v_conv_b, v_dt_bias, v_a_log, v_d_skip, v_ssd_norm_g, v_rw_mu, v_rw_w0, v_rw_w_up, v_rw_a0, v_rw_a_up, v_rw_k_k, v_rw_k_a, v_rw_r_k, v_rw_ln_g, v_rw_ln_b, v_w_out_sb, v_w_out_ssd, v_w_out_rw, v_w_o, v_final_g):
    names = ("norm_g", "w_in", "conv_w", "conv_b", "dt_bias", "a_log", "d_skip", "ssd_norm_g", "rw_mu", "rw_w0",
             "rw_w_up", "rw_a0", "rw_a_up", "rw_k_k", "rw_k_a", "rw_r_k", "rw_ln_g", "rw_ln_b", "w_out_sb",
             "w_out_ssd", "w_out_rw", "w_o", "final_g")
    w_loc = dict(zip(names, (norm_g, w_in, conv_w, conv_b, dt_bias, a_log, d_skip, ssd_norm_g, rw_mu, rw_w0, rw_w_up,
                             rw_a0, rw_a_up, rw_k_k, rw_k_a, rw_r_k, rw_ln_g, rw_ln_b, w_out_sb, w_out_ssd, w_out_rw,
                             w_o, final_g)))
    m_loc = dict(zip(names, (m_norm_g, m_w_in, m_conv_w, m_conv_b, m_dt_bias, m_a_log, m_d_skip, m_ssd_norm_g,
                             m_rw_mu, m_rw_w0, m_rw_w_up, m_rw_a0, m_rw_a_up, m_rw_k_k, m_rw_k_a, m_rw_r_k,
                             m_rw_ln_g, m_rw_ln_b, m_w_out_sb, m_w_out_ssd, m_w_out_rw, m_w_o, m_final_g)))
    v_loc = dict(zip(names, (v_norm_g, v_w_in, v_conv_w, v_conv_b, v_dt_bias, v_a_log, v_d_skip, v_ssd_norm_g,
                             v_rw_mu, v_rw_w0, v_rw_w_up, v_rw_a0, v_rw_a_up, v_rw_k_k, v_rw_k_a, v_rw_r_k,
                             v_rw_ln_g, v_rw_ln_b, v_w_out_sb, v_w_out_ssd, v_w_out_rw, v_w_o, v_final_g)))
    chip = 2 * lax.axis_index("x") + lax.axis_index("y")
    core = lax.axis_index("c")

    big_shapes = [w_loc[n].shape for n in BIG]
    pack = _pack_rows([w_loc[n].astype(BF16) for n in BIG], multiple=32)
    pack_half = pack.shape[0] // 2
    got_mine = _chip_exchange(pack.reshape(2, pack_half, LANES), per_dest=False, name="gather_big")
    got_theirs = _sibling_swap(got_mine, name="gather_join")
    got = jnp.concatenate([jnp.where(core == 0, got_mine, got_theirs), jnp.where(core == 0, got_theirs, got_mine)],
                          axis=1)
    full = {}
    per_chip = [_unpack_rows(got[q], big_shapes) for q in range(4)]
    for i, n in enumerate(BIG):
        full[n] = jnp.concatenate([per_chip[q][i] for q in range(4)], axis=BIG_AXIS[n])
    full["w_in"] = _pad_cols(full["w_in"])
    sm_names = tuple(SMALL_SHARDED)
    sm_shapes = [w_loc[n].shape for n in sm_names]
    (got_sm,) = _allgather_small(_pack_rows([w_loc[n] for n in sm_names]), reduce=False, name="gather_small")
    per_chip = [_unpack_rows(got_sm[4 * (q // 2) + 2 * (q % 2)], sm_shapes) for q in range(4)]
    for i, n in enumerate(sm_names):
        full[n] = jnp.concatenate([per_chip[q][i] for q in range(4)], axis=-1)

    def pad16(a):
        return jnp.zeros((1, LANES), F32).at[0, :SSD_HEADS].set(a)

    def layer_params(i):
        row = lambda n: w_loc[n][i].reshape(1, -1)
        cw = full["conv_w"][i]
        return dict(
            norm_g=row("norm_g"), w_in=full["w_in"][i], conv=[cw[k][None] for k in range(4)] + [row("conv_b")],
            dt_bias=pad16(dt_bias[i]), a_log=pad16(a_log[i]), d_skip=pad16(d_skip[i]),
            ssd_norm_g=row("ssd_norm_g"), rw_mu=row("rw_mu"),
            rw_pre=[row("rw_w0"), jnp.zeros((LANES, 512), F32).at[:HEAD].set(full["rw_w_up"][i]), row("rw_a0"),
                    jnp.zeros((LANES, 512), F32).at[HEAD:].set(full["rw_a_up"][i]), row("rw_k_k"), row("rw_k_a")],
            rw_post=[row("rw_ln_g"), row("rw_ln_b"), row("rw_r_k")],
            w_out_sb=full["w_out_sb"][i], w_out_ssd=full["w_out_ssd"][i], w_out_rw=full["w_out_rw"][i],
            w_o=full["w_o"][i])

    params = [layer_params(i) for i in range(DEPTH)]
    xs, saved = [x[0]], []
    for i in range(DEPTH):
        nxt, s = _layer_fwd(xs[-1], params[i], f"l{i}_")
        xs.append(nxt)
        saved.append(s)
    dx, loss_row, g_final = _final(xs[-1], final_g.reshape(1, -1), loss_target[0], bt=BT, name="final")
    grads = [None] * DEPTH
    for i in reversed(range(DEPTH)):
        dx, grads[i] = _layer_bwd(xs[i], dx, params[i], saved[i], f"l{i}_")

    def stacked(fn):
        return jnp.stack([fn(grads[i]) for i in range(DEPTH)])

    g_loc = {
        "norm_g": stacked(lambda g: g["norm_g"][0]),
        "w_in": stacked(lambda g: _unpad_cols(g["w_in"])),
        "conv_w": stacked(lambda g: jnp.concatenate(g["conv"][:4], axis=0)),
        "conv_b": stacked(lambda g: g["conv"][4][0]),
        "dt_bias": stacked(lambda g: g["dt_bias"][0, :SSD_HEADS]),
        "a_log": stacked(lambda g: g["a_log"][0, :SSD_HEADS]),
        "d_skip": stacked(lambda g: g["d_skip"][0, :SSD_HEADS]),
        "ssd_norm_g": stacked(lambda g: g["ssd_norm_g"][0]),
        "rw_mu": stacked(lambda g: g["rw_mu"][0]),
        "rw_w0": stacked(lambda g: g["rw_pre"][0][0]),
        "rw_w_up": stacked(lambda g: g["rw_pre"][1][:HEAD]),
        "rw_a0": stacked(lambda g: g["rw_pre"][2][0]),
        "rw_a_up": stacked(lambda g: g["rw_pre"][3][HEAD:]),
        "rw_k_k": stacked(lambda g: g["rw_pre"][4][0]),
        "rw_k_a": stacked(lambda g: g["rw_pre"][5][0]),
        "rw_r_k": stacked(lambda g: g["rw_r_k"].reshape(8, HEAD)),
        "rw_ln_g": stacked(lambda g: g["rw_ln_g"][0]),
        "rw_ln_b": stacked(lambda g: g["rw_ln_b"][0]),
        "w_out_sb": stacked(lambda g: g["w_out_sb"]),
        "w_out_ssd": stacked(lambda g: g["w_out_ssd"]),
        "w_out_rw": stacked(lambda g: g["w_out_rw"]),
        "w_o": stacked(lambda g: g["w_o"]),
        "final_g": g_final[0],
    }

    send = jnp.stack([_pack_rows([_split_chips(g_loc[n], BIG_AXIS[n])[q] for n in BIG], multiple=16) for q in range(4)])
    half = send.shape[1] // 2
    send = send.reshape(4, 2, half, LANES).transpose(1, 0, 2, 3)
    other = _sibling_send_other_half(send, name="reduce_sibling")
    part = _add_halves(send, other, core.reshape(1).astype(jnp.int32), name="reduce_add")
    parts = _chip_exchange(part, per_dest=True, name="reduce_chips")
    mine = _sum_chips(parts, name="reduce_sum")
    theirs = _sibling_swap(mine, name="reduce_join")
    total = jnp.concatenate([jnp.where(core == 0, mine, theirs), jnp.where(core == 0, theirs, mine)], axis=0)
    g_out = dict(zip(BIG, _unpack_rows(total, big_shapes)))

    sm_all = SMALL + ("loss",)
    sm_full_shapes = [g_loc[n].shape for n in SMALL] + [(1,)]
    _, summed = _allgather_small(_pack_rows([g_loc[n] for n in SMALL] + [loss_row[0, :1]]), reduce=True, name="reduce_small")
    sm = dict(zip(sm_all, _unpack_rows(summed, sm_full_shapes)))
    for n in SMALL:
        g_out[n] = sm[n]
    for n, wd in SMALL_SHARDED.items():
        g_out[n] = lax.dynamic_slice_in_dim(sm[n], chip * wd, wd, axis=sm[n].ndim - 1)
    loss = sm["loss"][0]

    upd = {n: _adamw(w_loc[n], g_out[n], m_loc[n], v_loc[n], name="adamw_" + n) for n in names}
    return (loss, dx[None], *[g_out[n] for n in names], *[upd[n][0] for n in names],
            *[upd[n][1] for n in names], *[upd[n][2] for n in names])
```

```python
import functools

import jax
import jax.numpy as jnp
from jax import lax
from jax.experimental import pallas as pl
from jax.experimental.pallas import tpu as pltpu

F32 = jnp.float32
BF16 = jnp.bfloat16

D_MODEL = 1024
DEPTH = 2
HEAD = 64
LANES = 128
CHUNK = 128
RMS_EPS = 1e-6
GN_EPS = 64e-5
VMEM_LIMIT = 56 * 1024 * 1024

N_IN = 9616
N_PAD = 9728
C_SB, C_Z, C_GATES, C_RW, C_LO, C_DT, C_XBC = 0, 2048, 3072, 6144, 8192, 8320, 8448
RW_COLS = 2176
XBC_COLS = 1280

ADAM_LR, ADAM_B1, ADAM_B2, ADAM_EPS, ADAM_WD, ADAM_STEP = 0.001, 0.9, 0.999, 1e-08, 0.01, 10


def _params(sem=None):
    return pltpu.CompilerParams(dimension_semantics=sem, vmem_limit_bytes=VMEM_LIMIT)


@jax.custom_vjp
def _sigmoid(x):
    return 1.0 / (1.0 + jnp.exp(-x))


def _sigmoid_fwd(x):
    s = _sigmoid(x)
    return s, s


def _sigmoid_bwd(s, g):
    return (g * s * (1.0 - s),)


_sigmoid.defvjp(_sigmoid_fwd, _sigmoid_bwd)


@jax.custom_vjp
def _silu(x):
    return x * _sigmoid(x)


def _silu_fwd(x):
    s = _sigmoid(x)
    return x * s, (x, s)


def _silu_bwd(res, g):
    x, s = res
    return (g * (s + x * s * (1.0 - s)),)


_silu.defvjp(_silu_fwd, _silu_bwd)


@jax.custom_vjp
def _softplus(x):
    return jnp.maximum(x, 0.0) + jnp.log(1.0 + jnp.exp(-jnp.abs(x)))


def _softplus_fwd(x):
    return _softplus(x), x


def _softplus_bwd(x, g):
    return (g * _sigmoid(x),)


_softplus.defvjp(_softplus_fwd, _softplus_bwd)


def _dot(a, b, dims):
    return lax.dot_general(a.astype(BF16), b.astype(BF16), (dims, ((), ())), preferred_element_type=F32)


def _dot_nn(a, b):
    return _dot(a, b, ((1,), (0,)))


def _dot_nt(a, b):
    return _dot(a, b, ((1,), (1,)))


def _dot_tn(a, b):
    return _dot(a, b, ((0,), (0,)))


@jax.custom_vjp
def _bdot(a, b):
    return _dot_nn(a, b)


def _bdot_fwd(a, b):
    return _dot_nn(a, b), (a, b)


def _bdot_bwd(res, g):
    a, b = res
    return _dot_nt(g, b), _dot_tn(a, g)


_bdot.defvjp(_bdot_fwd, _bdot_bwd)


def _split2(x):
    hi = x.astype(BF16)
    lo = (x - hi.astype(F32)).astype(BF16)
    return hi, lo


_NT = (((1,), (1,)), ((), ()))
_NN = (((1,), (0,)), ((), ()))
_TN = (((0,), (0,)), ((), ()))


def _dot2(x, m, dn=_NN):
    hi, lo = _split2(x)
    return (lax.dot_general(hi, m, dn, preferred_element_type=F32)
            + lax.dot_general(lo, m, dn, preferred_element_type=F32))


def _dot2_tn(x, m):
    return _dot2(x, m, _TN)


def _seg_matrix(n):
    r = lax.broadcasted_iota(jnp.int32, (n, n), 0) // HEAD
    c = lax.broadcasted_iota(jnp.int32, (n, n), 1) // HEAD
    return (r == c).astype(BF16)


@jax.custom_vjp
def _segsum2(x, seg):
    return _dot2(x, seg)


def _segsum2_fwd(x, seg):
    return _dot2(x, seg), seg


def _segsum2_bwd(seg, g):
    return _dot2(g, seg), jnp.zeros_like(seg)


_segsum2.defvjp(_segsum2_fwd, _segsum2_bwd)


def _make_segsum(seg):
    return lambda x: _segsum2(x, seg)


def _shift_down_raw(x, k):
    row = lax.broadcasted_iota(jnp.int32, x.shape, 0)
    return jnp.where(row >= k, pltpu.roll(x, k, 0), 0.0)


def _shift_up_raw(x, k):
    t = x.shape[0]
    row = lax.broadcasted_iota(jnp.int32, x.shape, 0)
    return jnp.where(row < t - k, pltpu.roll(x, t - k, 0), 0.0)


@functools.partial(jax.custom_vjp, nondiff_argnums=(1,))
def _shift_down(x, k):
    return _shift_down_raw(x, k)


def _shift_down_fwd(x, k):
    return _shift_down_raw(x, k), None


def _shift_down_bwd(k, _, g):
    return (_shift_up_raw(g, k),)


_shift_down.defvjp(_shift_down_fwd, _shift_down_bwd)


def _mm(a, b, *, name, ta=False, tb=False, add=None, out_dtype=F32, tm=2048, tn=512, tk=None):
    m, k = (a.shape[1], a.shape[0]) if ta else a.shape
    n = b.shape[0] if tb else b.shape[1]
    tm, tn = min(tm, m), min(tn, n)
    tk = k if tk is None else tk
    nk = k // tk
    assert m % tm == 0 and n % tn == 0 and k % tk == 0
    dims = ((0 if ta else 1,), (1 if tb else 0,))

    def body(a_ref, b_ref, *refs):
        o_ref, acc_ref = refs[-2:]
        p = _dot(a_ref[...], b_ref[...], dims)

        def emit(total):
            if add is not None:
                total = total + refs[0][...]
            o_ref[...] = total.astype(o_ref.dtype)

        if nk == 1:
            emit(p)
        else:
            kk = pl.program_id(2)

            @pl.when(kk == 0)
            def _():
                acc_ref[...] = p

            @pl.when(kk > 0)
            def _():
                acc_ref[...] += p

            @pl.when(kk == nk - 1)
            def _():
                emit(acc_ref[...])

    a_spec = pl.BlockSpec((tk, tm), lambda i, j, kk: (kk, i)) if ta else pl.BlockSpec((tm, tk), lambda i, j, kk: (i, kk))
    b_spec = pl.BlockSpec((tn, tk), lambda i, j, kk: (j, kk)) if tb else pl.BlockSpec((tk, tn), lambda i, j, kk: (kk, j))
    o_spec = pl.BlockSpec((tm, tn), lambda i, j, kk: (i, j))
    return pl.pallas_call(
        body, name=name, grid=(m // tm, n // tn, nk),
        in_specs=[a_spec, b_spec] + ([o_spec] if add is not None else []), out_specs=o_spec,
        out_shape=jax.ShapeDtypeStruct((m, n), out_dtype),
        scratch_shapes=[pltpu.VMEM((tm, tn) if nk > 1 else (8, LANES), F32)],
        compiler_params=_params(("parallel", "parallel", "arbitrary")),
    )(a, b, *([add] if add is not None else []))


def _row_specs(rows, bt):
    return [pl.BlockSpec((bt, w), functools.partial(lambda i, c: (i, c), c=c)) for _, w, c in rows]


def _full_spec(p):
    return pl.BlockSpec(p.shape, functools.partial(lambda i, nd: (0,) * nd, nd=p.ndim))


def _rowwise(f, rows, pars, out_widths, *, bt, name, acc_widths=()):
    t = rows[0][0].shape[0]
    nr, npar, no, na = len(rows), len(pars), len(out_widths), len(acc_widths)

    def body(*refs):
        vals = [r[...] for r in refs[:nr + npar]]
        outs = f(*vals)
        for o_ref, o in zip(refs[nr + npar:nr + npar + no], outs[:no]):
            o_ref[...] = o.astype(o_ref.dtype)
        if na:
            first = pl.program_id(0) == 0
            for a_ref, a in zip(refs[nr + npar + no:], outs[no:]):
                @pl.when(first)
                def _():
                    a_ref[...] = jnp.zeros_like(a_ref)
                a_ref[...] += a

    return pl.pallas_call(
        body, name=name, grid=(t // bt,),
        in_specs=_row_specs(rows, bt) + [_full_spec(p) for p in pars],
        out_specs=[pl.BlockSpec((bt, w), lambda i: (i, 0)) for w in out_widths]
        + [pl.BlockSpec((1, w), lambda i: (0, 0)) for w in acc_widths],
        out_shape=[jax.ShapeDtypeStruct((t, w), F32) for w in out_widths]
        + [jax.ShapeDtypeStruct((1, w), F32) for w in acc_widths],
        compiler_params=_params(("arbitrary",)),
    )(*[r[0] for r in rows], *pars)


def _rowwise_bwd(f, rows, pars, douts, *, bt, name, groups=None):
    t = rows[0][0].shape[0]
    nr, npar, nd = len(rows), len(pars), len(douts)
    groups = [[i] for i in range(nr)] if groups is None else groups
    widths = [r[1] for r in rows]

    def body(*refs):
        vals = [r[...] for r in refs[:nr + npar]]
        cts = tuple(r[...] for r in refs[nr + npar:nr + npar + nd])
        _, vjp = jax.vjp(lambda *a: tuple(f(*a)), *vals)
        grads = vjp(cts)
        out_refs = refs[nr + npar + nd:]
        for g_ref, grp in zip(out_refs[:len(groups)], groups):
            off = 0
            for i in grp:
                g_ref[:, off:off + widths[i]] = grads[i]
                off += widths[i]
        first = pl.program_id(0) == 0
        for p_ref, g in zip(out_refs[len(groups):], grads[nr:]):
            @pl.when(first)
            def _():
                p_ref[...] = jnp.zeros_like(p_ref)
            p_ref[...] += g

    gw = [sum(widths[i] for i in grp) for grp in groups]
    return pl.pallas_call(
        body, name=name, grid=(t // bt,),
        in_specs=_row_specs(rows, bt) + [_full_spec(p) for p in pars] + _row_specs(douts, bt),
        out_specs=[pl.BlockSpec((bt, w), lambda i: (i, 0)) for w in gw] + [_full_spec(p) for p in pars],
        out_shape=[jax.ShapeDtypeStruct((t, w), F32) for w in gw] + [jax.ShapeDtypeStruct(p.shape, F32) for p in pars],
        compiler_params=_params(("arbitrary",)),
    )(*[r[0] for r in rows], *pars, *[d[0] for d in douts])


def _colwise(f, x, c0, ncols, pars, *, bc, name):
    t = x.shape[0]

    def body(x_ref, *refs):
        o_ref = refs[-1]
        o_ref[...] = f(x_ref[...], *[r[...] for r in refs[:-1]])

    return pl.pallas_call(
        body, name=name, grid=(ncols // bc,),
        in_specs=[pl.BlockSpec((t, bc), lambda j: (0, j + c0 // bc))]
        + [pl.BlockSpec((p.shape[0], bc), lambda j: (0, j)) for p in pars],
        out_specs=pl.BlockSpec((t, bc), lambda j: (0, j)),
        out_shape=jax.ShapeDtypeStruct((t, ncols), F32),
        compiler_params=_params(("parallel",)),
    )(x, *pars)


def _colwise_bwd(f, x, c0, ncols, pars, dout, *, bc, name):
    t = x.shape[0]
    npar = len(pars)

    def body(x_ref, *refs):
        vals = [x_ref[...]] + [r[...] for r in refs[:npar]]
        _, vjp = jax.vjp(f, *vals)
        grads = vjp(refs[npar][...])
        for g_ref, g in zip(refs[npar + 1:], grads):
            g_ref[...] = g

    return pl.pallas_call(
        body, name=name, grid=(ncols // bc,),
        in_specs=[pl.BlockSpec((t, bc), lambda j: (0, j + c0 // bc))]
        + [pl.BlockSpec((p.shape[0], bc), lambda j: (0, j)) for p in pars]
        + [pl.BlockSpec((t, bc), lambda j: (0, j))],
        out_specs=[pl.BlockSpec((t, bc), lambda j: (0, j))]
        + [pl.BlockSpec((p.shape[0], bc), lambda j: (0, j)) for p in pars],
        out_shape=[jax.ShapeDtypeStruct((t, ncols), F32)] + [jax.ShapeDtypeStruct(p.shape, F32) for p in pars],
        compiler_params=_params(("parallel",)),
    )(x, *pars, dout)


def _f_rms(x, g):
    return (x * lax.rsqrt(jnp.mean(x * x, axis=-1, keepdims=True) + RMS_EPS) * g,)


def _f_sb_gate(y, gate):
    return (y * _silu(gate),)


def _f_ssd_norm(y, z, g):
    u = y * _silu(z)
    return (u * lax.rsqrt(jnp.mean(u * u, axis=-1, keepdims=True) + RMS_EPS) * g,)


def _f_merge(p_sb, p_ssd, p_rw, g_sb, g_ssd, g_rw):
    return (_sigmoid(g_sb) * p_sb + _sigmoid(g_ssd) * p_ssd + _sigmoid(g_rw) * p_rw,)


def _f_rw_pre(k, lo, w0, w_up, a0, a_up, k_k, k_a):
    segsum = _make_segsum(_seg_matrix(k.shape[1]))
    lane = lax.broadcasted_iota(jnp.int32, lo.shape, 1)
    w_lo = jnp.where(lane < HEAD, jnp.tanh(lo), 0.0)
    a_lo = jnp.where(lane >= HEAD, lo, 0.0)
    w = -_softplus(-(w0 + _bdot(w_lo, w_up))) - 0.5
    decay = jnp.exp(-jnp.exp(w))
    a = _sigmoid(a0 + _bdot(a_lo, a_up))
    kk = k * k_k
    kk = kk / jnp.maximum(jnp.sqrt(segsum(kk * kk)), 1e-12)
    return decay, k * (1.0 + (a - 1.0) * k_a), -kk, kk * a


def _f_rw_post(y, r, k2, v, gate, ln_g, ln_b, r_k):
    segsum = _make_segsum(_seg_matrix(y.shape[1]))
    yc = y - segsum(y) * (1.0 / HEAD)
    var = segsum(yc * yc) * (1.0 / HEAD)
    yn = yc * lax.rsqrt(var + GN_EPS) * ln_g + ln_b
    return ((yn + segsum(r * k2 * r_k) * v) * _silu(gate),)


def _f_rw_mix(slab, mu):
    return slab + (_shift_down(slab, 1) - slab) * mu


def _f_conv(x, w0, w1, w2, w3, b):
    acc = x * w3 + b
    for i, w in enumerate((w0, w1, w2)):
        acc = acc + _shift_down(x, 3 - i) * w
    return _silu(acc)


def _log_sigmoid(z):
    return jnp.minimum(z, 0.0) - jnp.log(1.0 + jnp.exp(-jnp.abs(z)))


def _prefix_matrix(kind):
    j = lax.broadcasted_iota(jnp.int32, (CHUNK, 2 * CHUNK), 0)
    s = lax.broadcasted_iota(jnp.int32, (CHUNK, 2 * CHUNK), 1)
    tri = {"gt": j > s, "le": j <= s, "lt": j < s}[kind]
    return (tri | (s >= CHUNK)).astype(BF16)


def _sb_specs(t):
    q = pl.BlockSpec((CHUNK, LANES), lambda j, i: (i, j))
    k = pl.BlockSpec((t, LANES), lambda j, i: (0, 4 + j))
    v = pl.BlockSpec((t, LANES), lambda j, i: (0, 8 + j))
    return q, k, v


def _sb_fwd(proj, *, name):
    t = proj.shape[0]
    scale = HEAD ** -0.5

    def body(q_ref, k_ref, v_ref, y_ref, lt_ref):
        i = pl.program_id(1)
        lane = lax.broadcasted_iota(jnp.int32, (CHUNK, LANES), 1)
        diff = (lax.broadcasted_iota(jnp.int32, (CHUNK, CHUNK), 1)
                - lax.broadcasted_iota(jnp.int32, (CHUNK, CHUNK), 0))
        m_f = _prefix_matrix("gt")
        q = q_ref[...] * scale
        qh = [jnp.where((lane // HEAD) == h, q, 0.0).astype(BF16) for h in (0, 1)]

        def step(it, carry):
            off = pl.multiple_of((i - it) * CHUNK, CHUNK)
            kblk = k_ref[pl.ds(off, CHUNK), :].astype(BF16)
            vblk = v_ref[pl.ds(off, CHUNK), :].astype(BF16)
            mask = diff < it * CHUNK
            new = []
            for h in (0, 1):
                c, acc = carry[2 * h], carry[2 * h + 1]
                z = lax.dot_general(qh[h], kblk, _NT, preferred_element_type=F32)
                lb = _log_sigmoid(z)
                w2 = _dot2(jnp.where(mask, lb - z, 0.0), m_f)
                att = jnp.where(mask, jnp.exp(lb + c + w2[:, :CHUNK]), 0.0)
                acc = acc + lax.dot_general(att.astype(BF16), vblk, _NN, preferred_element_type=F32)
                new += [c + w2[:, CHUNK:], acc]
            return tuple(new)

        zero = jnp.zeros((CHUNK, LANES), F32)
        c_a, acc_a, c_b, acc_b = lax.fori_loop(0, i + 1, step, (zero, zero, zero, zero))
        y_ref[...] = jnp.where(lane < HEAD, acc_a, acc_b)
        lt_ref[0] = c_a
        lt_ref[1] = c_b

    return pl.pallas_call(
        body, name=name, grid=(4, t // CHUNK),
        in_specs=list(_sb_specs(t)),
        out_specs=[pl.BlockSpec((CHUNK, LANES), lambda j, i: (i, j)),
                   pl.BlockSpec((2, CHUNK, LANES), lambda j, i: (j, i, 0))],
        out_shape=[jax.ShapeDtypeStruct((t, 4 * LANES), F32), jax.ShapeDtypeStruct((8, t, LANES), F32)],
        compiler_params=_params(("parallel", "arbitrary")),
    )(proj, proj, proj)


def _sb_bwd(proj, dy, lt, *, name):
    t = proj.shape[0]
    scale = HEAD ** -0.5

    def body(q_ref, k_ref, v_ref, dy_ref, lt_ref, dq_ref, dk_ref, dv_ref):
        i = pl.program_id(1)

        @pl.when(i == 0)
        def _():
            dk_ref[...] = jnp.zeros_like(dk_ref)
            dv_ref[...] = jnp.zeros_like(dv_ref)

        lane = lax.broadcasted_iota(jnp.int32, (CHUNK, LANES), 1)
        diff = (lax.broadcasted_iota(jnp.int32, (CHUNK, CHUNK), 1)
                - lax.broadcasted_iota(jnp.int32, (CHUNK, CHUNK), 0))
        m_le, m_lt = _prefix_matrix("le"), _prefix_matrix("lt")
        q = q_ref[...] * scale
        dy_blk = dy_ref[...]
        qh = [jnp.where((lane // HEAD) == h, q, 0.0).astype(BF16) for h in (0, 1)]
        doh = [jnp.where((lane // HEAD) == h, dy_blk, 0.0).astype(BF16) for h in (0, 1)]
        lth = [lt_ref[0], lt_ref[1]]

        def step(kb, carry):
            off = pl.multiple_of(kb * CHUNK, CHUNK)
            kblk = k_ref[pl.ds(off, CHUNK), :].astype(BF16)
            vblk = v_ref[pl.ds(off, CHUNK), :].astype(BF16)
            mask = diff < (i - kb) * CHUNK
            new = []
            dk_acc = jnp.zeros((CHUNK, LANES), F32)
            dv_acc = jnp.zeros((CHUNK, LANES), F32)
            for h in (0, 1):
                cp, cg, dq = carry[3 * h:3 * h + 3]
                z = lax.dot_general(qh[h], kblk, _NT, preferred_element_type=F32)
                lb = _log_sigmoid(z)
                w2 = _dot2(jnp.where(mask, lb - z, 0.0), m_le)
                att = jnp.where(mask, jnp.exp(lb + lth[h] - cp - w2[:, :CHUNK]), 0.0)
                d_att = lax.dot_general(doh[h], vblk, _NT, preferred_element_type=F32)
                d_e = d_att * att
                g2 = _dot2(d_e, m_lt)
                sig = jnp.exp(lb)
                dz = jnp.where(mask, d_e * (1.0 - sig) - (cg + g2[:, :CHUNK]) * sig, 0.0).astype(BF16)
                dq = dq + lax.dot_general(dz, kblk, _NN, preferred_element_type=F32)
                dk_acc = dk_acc + lax.dot_general(dz, qh[h], _TN, preferred_element_type=F32)
                dv_acc = dv_acc + lax.dot_general(att.astype(BF16), doh[h], _TN, preferred_element_type=F32)
                new += [cp + w2[:, CHUNK:], cg + g2[:, CHUNK:], dq]
            dk_ref[pl.ds(off, CHUNK), :] += dk_acc
            dv_ref[pl.ds(off, CHUNK), :] += dv_acc
            return tuple(new)

        zero = jnp.zeros((CHUNK, LANES), F32)
        out = lax.fori_loop(0, i + 1, step, (zero,) * 6)
        dq_ref[...] = jnp.where(lane < HEAD, out[2], out[5]) * scale

    q_spec, k_spec, v_spec = _sb_specs(t)
    blk = pl.BlockSpec((CHUNK, LANES), lambda j, i: (i, j))
    col = pl.BlockSpec((t, LANES), lambda j, i: (0, j))
    return pl.pallas_call(
        body, name=name, grid=(4, t // CHUNK),
        in_specs=[q_spec, k_spec, v_spec, blk, pl.BlockSpec((2, CHUNK, LANES), lambda j, i: (j, i, 0))],
        out_specs=[blk, col, col],
        out_shape=[jax.ShapeDtypeStruct((t, 4 * LANES), F32)] * 3,
        compiler_params=_params(("parallel", "arbitrary")),
    )(proj, proj, proj, dy, lt)


SB_BQ = 256
SB_BK = 256


def _tri_ones(kind):
    j = lax.broadcasted_iota(jnp.int32, (SB_BK, SB_BK + LANES), 0)
    s = lax.broadcasted_iota(jnp.int32, (SB_BK, SB_BK + LANES), 1)
    tri = {"gt": j > s, "le": j <= s, "lt": j < s}[kind]
    return (tri | (s >= SB_BK)).astype(BF16)


def _sb_common(q_ref):
    lane = lax.broadcasted_iota(jnp.int32, (SB_BQ, LANES), 1)
    q = q_ref[...] * (HEAD ** -0.5)
    q2 = jnp.concatenate([jnp.where(lane < HEAD, q, 0.0), jnp.where(lane >= HEAD, q, 0.0)], axis=0).astype(BF16)
    diff = (lax.broadcasted_iota(jnp.int32, (2 * SB_BQ, SB_BK), 1)
            - (lax.broadcasted_iota(jnp.int32, (2 * SB_BQ, SB_BK), 0) & (SB_BQ - 1)))
    return lane, q2, diff


def _rep(x):
    return jnp.concatenate([x] * (SB_BK // LANES), axis=1)


def _sb2_specs(t):
    q = pl.BlockSpec((SB_BQ, LANES), lambda j, i: (i, j))
    k = pl.BlockSpec((t, LANES), lambda j, i: (0, 4 + j))
    v = pl.BlockSpec((t, LANES), lambda j, i: (0, 8 + j))
    return q, k, v


def _sb2_fwd(proj, *, name):
    t = proj.shape[0]

    def body(q_ref, k_ref, v_ref, y_ref, lt_ref):
        i = pl.program_id(1)
        lane, q2, diff = _sb_common(q_ref)
        m_f = _tri_ones("gt")
        nk = (i + 1) * (SB_BQ // SB_BK)

        def step(it, carry):
            c, acc = carry
            kb = nk - 1 - it
            off = pl.multiple_of(kb * SB_BK, SB_BK)
            kblk = k_ref[pl.ds(off, SB_BK), :].astype(BF16)
            vblk = v_ref[pl.ds(off, SB_BK), :].astype(BF16)
            mask = diff < i * SB_BQ - kb * SB_BK
            z = lax.dot_general(q2, kblk, _NT, preferred_element_type=F32)
            lb = _log_sigmoid(z)
            w2 = _dot2(jnp.where(mask, lb - z, 0.0), m_f)
            att = jnp.where(mask, jnp.exp(lb + _rep(c) + w2[:, :SB_BK]), 0.0)
            acc = acc + lax.dot_general(att.astype(BF16), vblk, _NN, preferred_element_type=F32)
            return c + w2[:, SB_BK:], acc

        zero = jnp.zeros((2 * SB_BQ, LANES), F32)
        c, acc = lax.fori_loop(0, nk, step, (zero, zero))
        y_ref[...] = jnp.where(lane < HEAD, acc[:SB_BQ], acc[SB_BQ:])
        lt_ref[0] = c[:SB_BQ]
        lt_ref[1] = c[SB_BQ:]

    return pl.pallas_call(
        body, name=name, grid=(4, t // SB_BQ),
        in_specs=list(_sb2_specs(t)),
        out_specs=[pl.BlockSpec((SB_BQ, LANES), lambda j, i: (i, j)),
                   pl.BlockSpec((2, SB_BQ, LANES), lambda j, i: (j, i, 0))],
        out_shape=[jax.ShapeDtypeStruct((t, 4 * LANES), F32), jax.ShapeDtypeStruct((8, t, LANES), F32)],
        compiler_params=_params(("parallel", "arbitrary")),
    )(proj, proj, proj)


def _sb2_bwd(proj, dy, lt, *, name):
    t = proj.shape[0]

    def body(q_ref, k_ref, v_ref, dy_ref, lt_ref, dq_ref, dk_ref, dv_ref):
        i = pl.program_id(1)

        @pl.when(i == 0)
        def _():
            dk_ref[...] = jnp.zeros_like(dk_ref)
            dv_ref[...] = jnp.zeros_like(dv_ref)

        lane, q2, diff = _sb_common(q_ref)
        m_le, m_lt = _tri_ones("le"), _tri_ones("lt")
        dy_blk = dy_ref[...]
        do2 = jnp.concatenate([jnp.where(lane < HEAD, dy_blk, 0.0), jnp.where(lane >= HEAD, dy_blk, 0.0)],
                              axis=0).astype(BF16)
        lt2 = jnp.concatenate([lt_ref[0], lt_ref[1]], axis=0)

        def step(kb, carry):
            cp, cg, dq = carry
            off = pl.multiple_of(kb * SB_BK, SB_BK)
            kblk = k_ref[pl.ds(off, SB_BK), :].astype(BF16)
            vblk = v_ref[pl.ds(off, SB_BK), :].astype(BF16)
            mask = diff < i * SB_BQ - kb * SB_BK
            z = lax.dot_general(q2, kblk, _NT, preferred_element_type=F32)
            lb = _log_sigmoid(z)
            w2 = _dot2(jnp.where(mask, lb - z, 0.0), m_le)
            att = jnp.where(mask, jnp.exp(lb + _rep(lt2 - cp) - w2[:, :SB_BK]), 0.0)
            d_e = lax.dot_general(do2, vblk, _NT, preferred_element_type=F32) * att
            g2 = _dot2(d_e, m_lt)
            sig = jnp.exp(lb)
            dz = jnp.where(mask, d_e * (1.0 - sig) - (_rep(cg) + g2[:, :SB_BK]) * sig, 0.0).astype(BF16)
            dq = dq + lax.dot_general(dz, kblk, _NN, preferred_element_type=F32)
            dk_ref[pl.ds(off, SB_BK), :] += lax.dot_general(dz, q2, _TN, preferred_element_type=F32)
            dv_ref[pl.ds(off, SB_BK), :] += lax.dot_general(att.astype(BF16), do2, _TN, preferred_element_type=F32)
            return cp + w2[:, SB_BK:], cg + g2[:, SB_BK:], dq

        zero = jnp.zeros((2 * SB_BQ, LANES), F32)
        _, _, dq = lax.fori_loop(0, (i + 1) * (SB_BQ // SB_BK), step, (zero, zero, zero))
        dq_ref[...] = jnp.where(lane < HEAD, dq[:SB_BQ], dq[SB_BQ:]) * (HEAD ** -0.5)

    q_spec, k_spec, v_spec = _sb2_specs(t)
    blk = pl.BlockSpec((SB_BQ, LANES), lambda j, i: (i, j))
    col = pl.BlockSpec((t, LANES), lambda j, i: (0, j))
    return pl.pallas_call(
        body, name=name, grid=(4, t // SB_BQ),
        in_specs=[q_spec, k_spec, v_spec, blk, pl.BlockSpec((2, SB_BQ, LANES), lambda j, i: (j, i, 0))],
        out_specs=[blk, col, col],
        out_shape=[jax.ShapeDtypeStruct((t, 4 * LANES), F32)] * 3,
        compiler_params=_params(("parallel", "arbitrary")),
    )(proj, proj, proj, dy, lt)


SSD_HEADS = 16
SSD_PAIRS = 8


def _split3(x):
    a = x.astype(BF16)
    r = x - a.astype(F32)
    b = r.astype(BF16)
    return a, b, (r - b.astype(F32)).astype(BF16)


def _dot3(x, m, dn=_NN):
    return sum(lax.dot_general(p, m, dn, preferred_element_type=F32) for p in _split3(x))


def _mdot3(m, x):
    return sum(lax.dot_general(m, p, _NN, preferred_element_type=F32) for p in _split3(x))


def _ssd_common(dtr, dtb, alog, acsx_s, acst_s):
    lane = lax.broadcasted_iota(jnp.int32, (CHUNK, LANES), 1)
    lane1 = lax.broadcasted_iota(jnp.int32, (1, LANES), 1)
    arow = jnp.where(lane1 < SSD_HEADS, -jnp.exp(alog), 0.0)
    dt = jnp.where(lane < SSD_HEADS, _softplus(dtr + dtb), 0.0)
    da = dt * arow
    r = lax.broadcasted_iota(jnp.int32, (CHUNK, CHUNK), 0)
    c = lax.broadcasted_iota(jnp.int32, (CHUNK, CHUNK), 1)
    tril = (r >= c).astype(BF16)
    triu = (r <= c).astype(BF16)
    acs = _mdot3(tril, da)
    acst_s[...] = _dot3(da, triu, _TN)
    eh = lax.broadcasted_iota(jnp.int32, (LANES, 8 * LANES), 0)
    e = (eh == lax.broadcasted_iota(jnp.int32, (LANES, 8 * LANES), 1) // HEAD).astype(BF16)
    eh2 = lax.broadcasted_iota(jnp.int32, (LANES, 16 * LANES), 0)
    e2 = (eh2 == lax.broadcasted_iota(jnp.int32, (LANES, 16 * LANES), 1) // LANES).astype(BF16)
    acsx_s[...] = _dot3(acs, e)
    return dt, arow, _dot3(dt, e), _dot3(acs, e2), e, tril, triu


def _ssd_fwd(xc, proj, dtb, alog, dsk, *, name):
    t = xc.shape[0]
    nc = t // CHUNK

    def body(x_ref, b_ref, c_ref, dtr_ref, dtb_ref, alog_ref, dsk_ref, y_ref, hin_ref, acsx_s, acst_s, h_s):
        @pl.when(pl.program_id(0) == 0)
        def _():
            h_s[...] = jnp.zeros_like(h_s)

        dt, arow, dt_x, acs_b, e, tril, _ = _ssd_common(dtr_ref[...], dtb_ref[...], alog_ref[...], acsx_s, acst_s)
        dsk_x = _dot3(jnp.broadcast_to(dsk_ref[...], (CHUNK, LANES)), e)
        lane = lax.broadcasted_iota(jnp.int32, (CHUNK, LANES), 1)
        causal = (lax.broadcasted_iota(jnp.int32, (CHUNK, CHUNK), 0)
                  >= lax.broadcasted_iota(jnp.int32, (CHUNK, CHUNK), 1))
        for j in range(SSD_PAIRS):
            g = j // 4
            sl = slice(j * LANES, (j + 1) * LANES)
            if j % 4 == 0:
                bg = jnp.where(lane // HEAD == g, b_ref[...], 0.0)
                cg = jnp.where(lane // HEAD == g, c_ref[...], 0.0)
                cb = _dot_nt(cg, bg)
            x = x_ref[:, sl]
            a = acsx_s[:, sl]
            at = acsx_s[CHUNK - 1:CHUNK, sl]
            xdt = x * dt_x[:, sl]
            hin = h_s[j]
            hin_ref[0, j] = hin
            y = jnp.exp(a) * _dot_nn(cg, hin) + x * dsk_x[:, sl]
            h_s[j] = jnp.exp(at) * hin + _dot_tn(bg, xdt * jnp.exp(at - a))
            yd = []
            for hh in (0, 1):
                h = 2 * j + hh
                dec = jnp.exp(jnp.minimum(acs_b[:, h * LANES:(h + 1) * LANES] - acst_s[pl.ds(h, 1), :], 0.0))
                yd.append(_dot_nn(jnp.where(causal, cb * dec, 0.0), xdt))
            y_ref[:, sl] = y + jnp.where(lane < HEAD, yd[0], yd[1])

    one = pl.BlockSpec((1, LANES), lambda i: (0, 0))
    return pl.pallas_call(
        body, name=name, grid=(nc,),
        in_specs=[pl.BlockSpec((CHUNK, 8 * LANES), lambda i: (i, 0)),
                  pl.BlockSpec((CHUNK, LANES), lambda i: (i, 8)),
                  pl.BlockSpec((CHUNK, LANES), lambda i: (i, 9)),
                  pl.BlockSpec((CHUNK, LANES), lambda i: (i, C_DT // LANES)), one, one, one],
        out_specs=[pl.BlockSpec((CHUNK, 8 * LANES), lambda i: (i, 0)),
                   pl.BlockSpec((1, SSD_PAIRS, LANES, LANES), lambda i: (i, 0, 0, 0))],
        out_shape=[jax.ShapeDtypeStruct((t, 8 * LANES), F32),
                   jax.ShapeDtypeStruct((nc, SSD_PAIRS, LANES, LANES), F32)],
        scratch_shapes=[pltpu.VMEM((CHUNK, 8 * LANES), F32), pltpu.VMEM((LANES, CHUNK), F32),
                        pltpu.VMEM((SSD_PAIRS, LANES, LANES), F32)],
        compiler_params=_params(("arbitrary",)),
    )(xc, xc, xc, proj, dtb, alog, dsk)


def _ssd_bwd(xc, proj, dtb, alog, dsk, hin_all, dy, *, name):
    t = xc.shape[0]
    nc = t // CHUNK

    def body(x_ref, b_ref, c_ref, dtr_ref, dtb_ref, alog_ref, dsk_ref, hin_ref, dy_ref,
             dxc_ref, ddtr_ref, ddtb_ref, dalog_ref, ddsk_ref, acsx_s, acst_s, dh_s, dax_s, ddx_s):
        @pl.when(pl.program_id(0) == 0)
        def _():
            dh_s[...] = jnp.zeros_like(dh_s)
            ddtb_ref[...] = jnp.zeros_like(ddtb_ref)
            dalog_ref[...] = jnp.zeros_like(dalog_ref)
            ddsk_ref[...] = jnp.zeros_like(ddsk_ref)

        dtr = dtr_ref[...]
        dtb = dtb_ref[...]
        dt, arow, dt_x, acs_b, e, tril, triu = _ssd_common(dtr, dtb, alog_ref[...], acsx_s, acst_s)
        dsk_x = _dot3(jnp.broadcast_to(dsk_ref[...], (CHUNK, LANES)), e)
        lane = lax.broadcasted_iota(jnp.int32, (CHUNK, LANES), 1)
        rowi = lax.broadcasted_iota(jnp.int32, (CHUNK, LANES), 0)
        causal = (lax.broadcasted_iota(jnp.int32, (CHUNK, CHUNK), 0)
                  >= lax.broadcasted_iota(jnp.int32, (CHUNK, CHUNK), 1))
        dacs = jnp.zeros((CHUNK, LANES), F32)
        d_b = jnp.zeros((CHUNK, LANES), F32)
        d_c = jnp.zeros((CHUNK, LANES), F32)
        for j in range(SSD_PAIRS):
            g = j // 4
            sl = slice(j * LANES, (j + 1) * LANES)
            if j % 4 == 0:
                bg = jnp.where(lane // HEAD == g, b_ref[...], 0.0)
                cg = jnp.where(lane // HEAD == g, c_ref[...], 0.0)
                cb = _dot_nt(cg, bg)
                dcb = jnp.zeros((CHUNK, CHUNK), F32)
            x = x_ref[:, sl]
            d = dt_x[:, sl]
            a = acsx_s[:, sl]
            at = acsx_s[CHUNK - 1:CHUNK, sl]
            xdt = x * d
            hin = hin_ref[0, j]
            dhout = dh_s[j]
            dyp = dy_ref[:, sl]
            ea, eat, ed = jnp.exp(a), jnp.exp(at), jnp.exp(at - a)
            da_l = dyp * ea * _dot_nn(cg, hin)
            dm = dyp * ea
            d_c = d_c + _dot_nt(dm, hin)
            dh_s[j] = _dot_tn(cg, dm) + eat * dhout
            dat = jnp.sum(dhout * hin * eat, axis=0, keepdims=True)
            d_b = d_b + _dot_nt(xdt * ed, dhout)
            dw = _dot_nn(bg, dhout)
            dxdt = dw * ed
            ded = dw * xdt * ed
            dat = dat + jnp.sum(ded, axis=0, keepdims=True)
            da_l = da_l - ded
            for hh in (0, 1):
                h = 2 * j + hh
                dec = jnp.exp(jnp.minimum(acs_b[:, h * LANES:(h + 1) * LANES] - acst_s[pl.ds(h, 1), :], 0.0))
                gm = jnp.where(causal, cb * dec, 0.0)
                dyh = jnp.where(lane // HEAD == hh, dyp, 0.0)
                dg = _dot_nt(dyh, xdt)
                dxdt = dxdt + _dot_tn(gm, dyh)
                dcb = dcb + jnp.where(causal, dg * dec, 0.0)
                th = dg * gm
                oh = (lane == h).astype(BF16)
                dacs = dacs + _dot2(th, oh) - _dot2_tn(th, oh)
            if j % 4 == 3:
                d_c = d_c + _dot_nn(dcb, bg)
                d_b = d_b + _dot_tn(dcb, cg)
            dxc_ref[:, sl] = dyp * dsk_x[:, sl] + dxdt * d
            ddx_s[:, sl] = dxdt * x
            dax_s[:, sl] = da_l + jnp.where(rowi == CHUNK - 1, dat, 0.0)
            dskp = jnp.sum(dyp * x, axis=0, keepdims=True)
            ddsk_ref[...] += _dot2(jnp.broadcast_to(dskp, (8, LANES)), e[:, sl], _NT)
        dxc_ref[:, 8 * LANES:9 * LANES] = d_b
        dxc_ref[:, 9 * LANES:10 * LANES] = d_c
        dacs = dacs + _dot2(dax_s[...], e, _NT)
        ddt = _dot2(ddx_s[...], e, _NT)
        dda = _mdot3(triu, dacs)
        ddt = ddt + dda * arow
        dalog_ref[...] += jnp.sum(dda * dt, axis=0, keepdims=True) * arow
        ddtr = jnp.where(lane < SSD_HEADS, ddt * _sigmoid(dtr + dtb), 0.0)
        ddtr_ref[...] = ddtr
        ddtb_ref[...] += jnp.sum(ddtr, axis=0, keepdims=True)

    one = pl.BlockSpec((1, LANES), lambda i: (0, 0))
    rev = lambda c: (lambda i: (nc - 1 - i, c))
    return pl.pallas_call(
        body, name=name, grid=(nc,),
        in_specs=[pl.BlockSpec((CHUNK, 8 * LANES), rev(0)), pl.BlockSpec((CHUNK, LANES), rev(8)),
                  pl.BlockSpec((CHUNK, LANES), rev(9)), pl.BlockSpec((CHUNK, LANES), rev(C_DT // LANES)),
                  one, one, one,
                  pl.BlockSpec((1, SSD_PAIRS, LANES, LANES), lambda i: (nc - 1 - i, 0, 0, 0)),
                  pl.BlockSpec((CHUNK, 8 * LANES), rev(0))],
        out_specs=[pl.BlockSpec((CHUNK, XBC_COLS), rev(0)), pl.BlockSpec((CHUNK, LANES), rev(0)), one, one,
                   pl.BlockSpec((8, LANES), lambda i: (0, 0))],
        out_shape=[jax.ShapeDtypeStruct((t, XBC_COLS), F32), jax.ShapeDtypeStruct((t, LANES), F32)]
        + [jax.ShapeDtypeStruct((1, LANES), F32)] * 2 + [jax.ShapeDtypeStruct((8, LANES), F32)],
        scratch_shapes=[pltpu.VMEM((CHUNK, 8 * LANES), F32), pltpu.VMEM((LANES, CHUNK), F32),
                        pltpu.VMEM((SSD_PAIRS, LANES, LANES), F32),
                        pltpu.VMEM((CHUNK, 8 * LANES), F32), pltpu.VMEM((CHUNK, 8 * LANES), F32)],
        compiler_params=_params(("arbitrary",)),
    )(xc, xc, xc, proj, dtb, alog, dsk, hin_all, dy)


RW_LW = 128
RW_PAIRS = 4 * LANES // RW_LW
RW_BT = 16
RW_DECAY_ROW = 1
RW_BWD_PAIRS = 4


def _rw_consts():
    seg = _seg_matrix(RW_LW)
    ti = (lax.broadcasted_iota(jnp.int32, (HEAD, RW_LW), 0)
          == lax.broadcasted_iota(jnp.int32, (HEAD, RW_LW), 1) % HEAD)
    return seg, ti


def _col_tiles(rows, ti, seg):
    tib = ti.astype(BF16)
    n = len(rows)
    hi = [r.astype(BF16) for r in rows]
    w_lo = (rows[RW_DECAY_ROW] - hi[RW_DECAY_ROW].astype(F32)).astype(BF16)
    out = lax.dot_general(jnp.concatenate([tib * h for h in hi + [w_lo]], axis=0), seg, _NN, preferred_element_type=F32)
    tiles = [out[i * HEAD:(i + 1) * HEAD] for i in range(n)]
    tiles[RW_DECAY_ROW] = tiles[RW_DECAY_ROW] + out[n * HEAD:(n + 1) * HEAD]
    return tiles


def _col_tiles2(rows, ti, seg):
    tib = ti.astype(BF16)
    hi = [r.astype(BF16) for r in rows]
    lo = [(r - h.astype(F32)).astype(BF16) for r, h in zip(rows, hi)]
    out = (lax.dot_general(jnp.concatenate([tib * h for h in hi], axis=0), seg, _NN, preferred_element_type=F32)
           + lax.dot_general(jnp.concatenate([tib * l for l in lo], axis=0), seg, _NN, preferred_element_type=F32))
    return [out[i * HEAD:(i + 1) * HEAD] for i in range(len(rows))]


def _head_lane_sums(tiles, ti, seg):
    out = _dot2(jnp.concatenate(tiles, axis=0), seg)
    return [jnp.sum(jnp.where(ti, out[i * HEAD:(i + 1) * HEAD], 0.0), axis=0, keepdims=True) for i in range(len(tiles))]


def _rw_scan_fwd(mixed, w, k, n, b, *, name):
    t = w.shape[0]

    def body(r_ref, v_ref, w_ref, k_ref, n_ref, b_ref, y_ref, st_ref, s_s):
        @pl.when(pl.program_id(0) == 0)
        def _():
            s_s[...] = jnp.zeros_like(s_s)

        seg, ti = _rw_consts()

        def step(tt, state):
            row = pl.ds(tt, 1)
            new = []
            for p in range(RW_PAIRS):
                sl = pl.ds(p * RW_LW, RW_LW)
                s = state[p]
                ncol, wcol, bcol, kcol, rcol = _col_tiles(
                    [x[row, sl] for x in (n_ref, w_ref, b_ref, k_ref, r_ref)], ti, seg)
                sa = jnp.sum(s * ncol, axis=0, keepdims=True)
                s = s * wcol + bcol * sa + kcol * v_ref[row, sl]
                y_ref[row, sl] = jnp.sum(s * rcol, axis=0, keepdims=True)
                st_ref[tt, p] = s
                new.append(s)
            return tuple(new)

        out = tuple(s_s[p] for p in range(RW_PAIRS))
        for tt in range(RW_BT):
            out = step(tt, out)
        for p in range(RW_PAIRS):
            s_s[p] = out[p]

    blk = lambda c: pl.BlockSpec((RW_BT, 4 * LANES), functools.partial(lambda i, c: (i, c), c=c))
    return pl.pallas_call(
        body, name=name, grid=(t // RW_BT,),
        in_specs=[blk(0), blk(2), blk(0), blk(0), blk(0), blk(0)],
        out_specs=[blk(0), pl.BlockSpec((RW_BT, RW_PAIRS, HEAD, RW_LW), lambda i: (i, 0, 0, 0))],
        out_shape=[jax.ShapeDtypeStruct((t, 4 * LANES), F32),
                   jax.ShapeDtypeStruct((t, RW_PAIRS, HEAD, RW_LW), F32)],
        scratch_shapes=[pltpu.VMEM((RW_PAIRS, HEAD, RW_LW), F32)],
        compiler_params=_params(("arbitrary",)),
    )(mixed, mixed, w, k, n, b)


def _rw_scan_bwd(mixed, w, k, n, b, states, dy, dr0, dk0, dv0, *, name):
    t = w.shape[0]
    nb = t // RW_BT
    ppc = RW_BWD_PAIRS
    ng = RW_PAIRS // ppc

    def body(r_ref, v_ref, w_ref, k_ref, n_ref, b_ref, st_ref, prev_ref, dy_ref, dr0_ref, dk0_ref, dv0_ref,
             dr_ref, dw_ref, dk_ref, dv_ref, dn_ref, db_ref, ds_s):
        @pl.when(pl.program_id(1) == 0)
        def _():
            ds_s[...] = jnp.zeros_like(ds_s)

        seg, ti = _rw_consts()
        has_prev = (pl.program_id(1) < nb - 1).astype(F32)

        def step(it, carry):
            tt = RW_BT - 1 - it
            row = pl.ds(tt, 1)
            prev_t = max(tt - 1, 0)
            new_ds, new_s = [], []
            for p in range(ppc):
                sl = pl.ds(p * RW_LW, RW_LW)
                ds, s_t = carry[p], carry[ppc + p]
                s_p = st_ref[prev_t, p] if tt > 0 else prev_ref[0, p] * has_prev
                ncol, wcol, bcol, kcol, rcol = _col_tiles2(
                    [x[row, sl] for x in (n_ref, w_ref, b_ref, k_ref, r_ref)], ti, seg)
                vv, dyy = v_ref[row, sl], dy_ref[row, sl]
                sa = jnp.sum(s_p * ncol, axis=0, keepdims=True)
                ds = ds + rcol * dyy
                dsa = jnp.sum(ds * bcol, axis=0, keepdims=True)
                dv_ref[row, sl] = jnp.sum(ds * kcol, axis=0, keepdims=True) + dv0_ref[row, sl]
                dr, dw, db, dk, dn = _head_lane_sums([s_t * dyy, ds * s_p, ds * sa, ds * vv, s_p * dsa], ti, seg)
                dr_ref[row, sl] = dr + dr0_ref[row, sl]
                dw_ref[row, sl] = dw
                db_ref[row, sl] = db
                dk_ref[row, sl] = dk + dk0_ref[row, sl]
                dn_ref[row, sl] = dn
                new_ds.append(ds * wcol + ncol * dsa)
                new_s.append(s_p)
            return tuple(new_ds) + tuple(new_s)

        init = tuple(ds_s[p] for p in range(ppc)) + tuple(st_ref[RW_BT - 1, p] for p in range(ppc))
        out = init
        for it in range(RW_BT):
            out = step(it, out)
        for p in range(ppc):
            ds_s[p] = out[p]

    blk = lambda c: pl.BlockSpec((RW_BT, ppc * RW_LW), functools.partial(lambda g, i, c: (nb - 1 - i, c * ng + g), c=c))
    st_spec = pl.BlockSpec((RW_BT, ppc, HEAD, RW_LW), lambda g, i: (nb - 1 - i, g, 0, 0))
    prev_spec = pl.BlockSpec((1, ppc, HEAD, RW_LW), lambda g, i: (jnp.maximum((nb - 1 - i) * RW_BT - 1, 0), g, 0, 0))
    return pl.pallas_call(
        body, name=name, grid=(ng, nb),
        in_specs=[blk(0), blk(2), blk(0), blk(0), blk(0), blk(0), st_spec, prev_spec, blk(0), blk(0), blk(0), blk(0)],
        out_specs=[blk(0)] * 6,
        out_shape=[jax.ShapeDtypeStruct((t, 4 * LANES), F32)] * 6,
        scratch_shapes=[pltpu.VMEM((ppc, HEAD, RW_LW), F32)],
        compiler_params=_params(("parallel", "arbitrary")),
    )(mixed, mixed, w, k, n, b, states, states, dy, dr0, dk0, dv0)


def _f_rms_res(x, g):
    return _f_rms(x, g)[0], x


def _final(x, g, target, *, bt, name):
    t, d = x.shape

    def body(x_ref, g_ref, t_ref, dx_ref, loss_ref, dg_ref):
        tgt = t_ref[...]

        def f(xv, gv):
            err = _f_rms(xv, gv)[0] - tgt
            return 0.5 * jnp.mean(err * err, axis=-1, keepdims=True)

        row_loss, vjp = jax.vjp(f, x_ref[...], g_ref[...])
        dx, dg = vjp(jnp.ones_like(row_loss))
        dx_ref[...] = dx

        @pl.when(pl.program_id(0) == 0)
        def _():
            loss_ref[...] = jnp.zeros_like(loss_ref)
            dg_ref[...] = jnp.zeros_like(dg_ref)

        loss_ref[...] += jnp.broadcast_to(jnp.sum(row_loss, axis=0, keepdims=True), (1, LANES))
        dg_ref[...] += dg

    blk = pl.BlockSpec((bt, d), lambda i: (i, 0))
    return pl.pallas_call(
        body, name=name, grid=(t // bt,),
        in_specs=[blk, pl.BlockSpec((1, d), lambda i: (0, 0)), blk],
        out_specs=[blk, pl.BlockSpec((1, LANES), lambda i: (0, 0)), pl.BlockSpec((1, d), lambda i: (0, 0))],
        out_shape=[jax.ShapeDtypeStruct((t, d), F32), jax.ShapeDtypeStruct((1, LANES), F32),
                   jax.ShapeDtypeStruct((1, d), F32)],
        compiler_params=_params(("arbitrary",)),
    )(x, g, target)


ADAMW_BLOCK_BYTES = 1 << 20


def _adamw(w, g, m, v, *, name):
    shape = w.shape
    c = shape[-1]
    args = [a.reshape(-1, c) for a in (w, g, m, v)]
    r = args[0].shape[0]
    br = r
    if r * c * 4 > ADAMW_BLOCK_BYTES:
        cands = [b for b in range(8, r, 8) if r % b == 0 and b * c * 4 <= ADAMW_BLOCK_BYTES]
        br = max(cands) if cands else r

    def body(w_ref, g_ref, m_ref, v_ref, d_ref, nm_ref, nv_ref):
        gv = g_ref[...]
        m_new = ADAM_B1 * m_ref[...] + (1.0 - ADAM_B1) * gv
        v_new = ADAM_B2 * v_ref[...] + (1.0 - ADAM_B2) * (gv * gv)
        m_hat = m_new / (1.0 - ADAM_B1 ** ADAM_STEP)
        v_hat = v_new / (1.0 - ADAM_B2 ** ADAM_STEP)
        d_ref[...] = -ADAM_LR * (m_hat / (jnp.sqrt(v_hat) + ADAM_EPS) + ADAM_WD * w_ref[...])
        nm_ref[...] = m_new
        nv_ref[...] = v_new

    blk = pl.BlockSpec((br, c), lambda i: (i, 0))
    outs = pl.pallas_call(
        body, name=name, grid=(r // br,), in_specs=[blk] * 4, out_specs=[blk] * 3,
        out_shape=[jax.ShapeDtypeStruct((r, c), F32)] * 3,
        compiler_params=_params(("parallel",)),
    )(*args)
    return tuple(o.reshape(shape) for o in outs)


BT = 256
BC = 128


def _layer_rows(x, proj, s):
    s = {k: s.get(k) for k in ("y_sb_raw", "y_ssd_raw", "mixed", "ys", "k2", "p_sb", "p_ssd", "p_rw")}
    return dict(
        rms=[(x, D_MODEL, 0)],
        sb_gate=[(s["y_sb_raw"], 512, 0), (proj, 512, 3)],
        ssd_norm=[(s["y_ssd_raw"], 1024, 0), (proj, 1024, C_Z // 1024)],
        rw_pre=[(s["mixed"], 512, 1), (s["mixed"], LANES, 16)],
        rw_post=[(s["ys"], 512, 0), (s["mixed"], 512, 0), (s["k2"], 512, 0), (s["mixed"], 512, 2), (s["mixed"], 512, 3)],
        merge=[(s["p_sb"], 1024, 0), (s["p_ssd"], 1024, 0), (s["p_rw"], 1024, 0),
               (proj, 1024, 3), (proj, 1024, 4), (proj, 1024, 5)],
    )


def _layer_fwd(x, p, nm):
    s = {}
    (s["h"],) = _rowwise(_f_rms, [(x, D_MODEL, 0)], [p["norm_g"]], [D_MODEL], bt=BT, name=nm + "rms")
    proj = s["proj"] = _mm(s["h"], p["w_in"], name=nm + "proj")
    s["y_sb_raw"], s["lt"] = _sb2_fwd(proj, name=nm + "sb")
    s["xc"] = _colwise(_f_conv, proj, C_XBC, XBC_COLS, p["conv"], bc=BC, name=nm + "conv")
    s["y_ssd_raw"], s["hin"] = _ssd_fwd(s["xc"], proj, p["dt_bias"], p["a_log"], p["d_skip"], name=nm + "ssd")
    s["mixed"] = _colwise(_f_rw_mix, proj, C_RW, RW_COLS, [p["rw_mu"]], bc=BC, name=nm + "mix")
    s["w"], s["k2"], s["n"], s["b"] = _rowwise(_f_rw_pre, [(s["mixed"], 512, 1), (s["mixed"], LANES, 16)], p["rw_pre"],
                                               [512] * 4, bt=BT, name=nm + "rwpre")
    s["ys"], s["st"] = _rw_scan_fwd(s["mixed"], s["w"], s["k2"], s["n"], s["b"], name=nm + "scan")
    rows = _layer_rows(x, proj, s)
    (s["y_sb"],) = _rowwise(_f_sb_gate, rows["sb_gate"], [], [512], bt=BT, name=nm + "sbgate")
    (s["y_ssd"],) = _rowwise(_f_ssd_norm, rows["ssd_norm"], [p["ssd_norm_g"]], [1024], bt=BT, name=nm + "ssdnorm")
    (s["y_rw"],) = _rowwise(_f_rw_post, rows["rw_post"], p["rw_post"], [512], bt=BT, name=nm + "rwpost")
    s["p_sb"] = _mm(s["y_sb"], p["w_out_sb"], name=nm + "osb")
    s["p_ssd"] = _mm(s["y_ssd"], p["w_out_ssd"], name=nm + "ossd")
    s["p_rw"] = _mm(s["y_rw"], p["w_out_rw"], name=nm + "orw")
    (s["merged"],) = _rowwise(_f_merge, _layer_rows(x, proj, s)["merge"], [], [1024], bt=BT, name=nm + "merge")
    return _mm(s["merged"], p["w_o"], add=x, name=nm + "wo"), s


def _layer_bwd(x, dx_out, p, s, nm):
    g = {}
    proj = s["proj"]
    rows = _layer_rows(x, proj, s)
    g["w_o"] = _mm(s["merged"], dx_out, ta=True, name=nm + "g_wo")
    d_merged = _mm(dx_out, p["w_o"], tb=True, name=nm + "d_merged")
    dp_sb, dp_ssd, dp_rw, d_gates = _rowwise_bwd(_f_merge, rows["merge"], [], [(d_merged, 1024, 0)], bt=BT,
                                                 name=nm + "merge_b", groups=[[0], [1], [2], [3, 4, 5]])
    g["w_out_sb"] = _mm(s["y_sb"], dp_sb, ta=True, name=nm + "g_osb")
    g["w_out_ssd"] = _mm(s["y_ssd"], dp_ssd, ta=True, name=nm + "g_ossd")
    g["w_out_rw"] = _mm(s["y_rw"], dp_rw, ta=True, name=nm + "g_orw")
    dy_sb = _mm(dp_sb, p["w_out_sb"], tb=True, name=nm + "d_ysb")
    dy_ssd = _mm(dp_ssd, p["w_out_ssd"], tb=True, name=nm + "d_yssd")
    dy_rw = _mm(dp_rw, p["w_out_rw"], tb=True, name=nm + "d_yrw")
    dy_sb_raw, d_sbgate = _rowwise_bwd(_f_sb_gate, rows["sb_gate"], [], [(dy_sb, 512, 0)], bt=BT, name=nm + "sbgate_b")
    dq, dk, dv = _sb2_bwd(proj, dy_sb_raw, s["lt"], name=nm + "sb_b")
    dy_ssd_raw, dz, g["ssd_norm_g"] = _rowwise_bwd(_f_ssd_norm, rows["ssd_norm"], [p["ssd_norm_g"]],
                                                   [(dy_ssd, 1024, 0)], bt=BT, name=nm + "ssdnorm_b")
    dxc, ddtr, g["dt_bias"], g["a_log"], g["d_skip"] = _ssd_bwd(
        s["xc"], proj, p["dt_bias"], p["a_log"], p["d_skip"], s["hin"], dy_ssd_raw, name=nm + "ssd_b")
    conv_out = _colwise_bwd(_f_conv, proj, C_XBC, XBC_COLS, p["conv"], dxc, bc=BC, name=nm + "conv_b")
    dxbc, g["conv"] = conv_out[0], conv_out[1:]
    dys, dr0, dk0, dv0, d_rwgate, g["rw_ln_g"], g["rw_ln_b"], g["rw_r_k"] = _rowwise_bwd(
        _f_rw_post, rows["rw_post"], p["rw_post"], [(dy_rw, 512, 0)], bt=BT, name=nm + "rwpost_b")
    dr, dw, dk2, dvv, dn, db = _rw_scan_bwd(s["mixed"], s["w"], s["k2"], s["n"], s["b"], s["st"], dys, dr0, dk0, dv0,
                                            name=nm + "scan_b")
    pre_out = _rowwise_bwd(_f_rw_pre, rows["rw_pre"], p["rw_pre"],
                           [(dw, 512, 0), (dk2, 512, 0), (dn, 512, 0), (db, 512, 0)], bt=BT, name=nm + "rwpre_b")
    dkm, dlo, g["rw_pre"] = pre_out[0], pre_out[1], pre_out[2:]
    d_mixed = jnp.concatenate([dr, dkm, dvv, d_rwgate, dlo], axis=1)
    d_slab, g["rw_mu"] = _colwise_bwd(_f_rw_mix, proj, C_RW, RW_COLS, [p["rw_mu"]], d_mixed, bc=BC, name=nm + "mix_b")
    d_proj = jnp.concatenate([dq, dk, dv, d_sbgate, dz, d_gates, d_slab, ddtr, dxbc], axis=1)
    g["w_in"] = _mm(s["h"], d_proj, ta=True, name=nm + "g_win")
    dh = _mm(d_proj, p["w_in"], tb=True, tn=1024, tk=512, name=nm + "d_h")
    dx, g["norm_g"] = _rowwise_bwd(_f_rms_res, rows["rms"], [p["norm_g"]], [(dh, D_MODEL, 0), (dx_out, D_MODEL, 0)],
                                   bt=BT, name=nm + "rms_b")
    return dx, g


MESH = pl.DeviceIdType.MESH
N_DEV = 8
_ANY = pl.BlockSpec(memory_space=pl.ANY)
_CHIP_SEMS = [pltpu.SemaphoreType.DMA((3,)), pltpu.SemaphoreType.DMA((3,)), pltpu.SemaphoreType.DMA]


def _here():
    x, y, c = lax.axis_index("x"), lax.axis_index("y"), lax.axis_index("c")
    return x, y, c, [(1 - x, y), (x, 1 - y), (1 - x, 1 - y)]


def _chip_exchange(srcs, *, per_dest, name):
    n = len(srcs)

    def body(*refs):
        src_refs, out_refs = refs[:n], refs[n:2 * n]
        send_sems, recv_sems, local_sems = refs[2 * n:]
        x, y, c, chips = _here()
        me = 2 * x + y
        sends, owns = [], []
        for a, (src_ref, out_ref) in enumerate(zip(src_refs, out_refs)):
            pick = (lambda q, s=src_ref: s.at[q]) if per_dest else (lambda q, s=src_ref: s.at[c])
            owns.append(pltpu.make_async_copy(pick(me), out_ref.at[me], local_sems.at[a]))
            owns[-1].start()
            for j, (px, py) in enumerate(chips):
                sends.append(pltpu.make_async_remote_copy(
                    pick(2 * px + py), out_ref.at[me], send_sems.at[3 * a + j], recv_sems.at[3 * a + j],
                    device_id=(px, py, c), device_id_type=MESH))
                sends[-1].start()
        for a, (src_ref, out_ref) in enumerate(zip(src_refs, out_refs)):
            for j, (px, py) in enumerate(chips):
                pltpu.make_async_remote_copy(
                    src_ref.at[0], out_ref.at[2 * px + py], send_sems.at[3 * a + j], recv_sems.at[3 * a + j],
                    device_id=(px, py, c), device_id_type=MESH).wait_recv()
        for cp in sends:
            cp.wait_send()
        for cp in owns:
            cp.wait()

    return pl.pallas_call(
        body, name=name, in_specs=[_ANY] * n, out_specs=[_ANY] * n,
        out_shape=[jax.ShapeDtypeStruct((4,) + s.shape[1:], s.dtype) for s in srcs],
        scratch_shapes=[pltpu.SemaphoreType.DMA((3 * n,)), pltpu.SemaphoreType.DMA((3 * n,)),
                        pltpu.SemaphoreType.DMA((n,))],
    )(*srcs)


def _sibling_swap(srcs, *, other_slot, name):
    n = len(srcs)

    def body(*refs):
        src_refs, out_refs, send_sems, recv_sems = refs[:n], refs[n:2 * n], refs[2 * n], refs[2 * n + 1]
        x, y, c, _ = _here()
        copies = [pltpu.make_async_remote_copy(s.at[1 - c] if other_slot else s, o, send_sems.at[a], recv_sems.at[a],
                                               device_id=(x, y, 1 - c), device_id_type=MESH)
                  for a, (s, o) in enumerate(zip(src_refs, out_refs))]
        for cp in copies:
            cp.start()
        for cp in copies:
            cp.wait()

    return pl.pallas_call(
        body, name=name, in_specs=[_ANY] * n, out_specs=[_ANY] * n,
        out_shape=[jax.ShapeDtypeStruct(s.shape[1:] if other_slot else s.shape, s.dtype) for s in srcs],
        scratch_shapes=[pltpu.SemaphoreType.DMA((n,)), pltpu.SemaphoreType.DMA((n,))],
    )(*srcs)


def _allgather_small(v, *, reduce, name):
    r = v.shape[0]

    def body(v_ref, out_ref, *rest):
        send_sems, recv_sems, local_sem = rest[-3:]
        x, y, c, chips = _here()
        me, sibling = (x, y, c), (x, y, 1 - c)

        def slot(px, py, pc):
            return out_ref.at[4 * px + 2 * py + pc]

        def copy(k, block, to, src=None):
            return pltpu.make_async_remote_copy(
                src_ref=slot(*block) if src is None else src, dst_ref=slot(*block),
                send_sem=send_sems.at[k], recv_sem=recv_sems.at[k], device_id=to, device_id_type=MESH)

        mine = pltpu.make_async_copy(v_ref, slot(*me), local_sem)
        mine.start()
        first = [copy(0, me, sibling, src=v_ref)]
        first += [copy(1 + j, me, (*chip, c), src=v_ref) for j, chip in enumerate(chips)]
        for cp in first:
            cp.start()
        passed = [copy(4 + j, (*chip, c), sibling) for j, chip in enumerate(chips)]
        for j, chip in enumerate(chips):
            copy(1 + j, (*chip, c), me).wait_recv()
            passed[j].start()
        copy(0, sibling, me).wait_recv()
        for j, chip in enumerate(chips):
            copy(4 + j, (*chip, 1 - c), me).wait_recv()
        for cp in first + passed:
            cp.wait_send()
        mine.wait()
        if reduce:
            total = out_ref[0]
            for d in range(1, N_DEV):
                total = total + out_ref[d]
            rest[0][...] = total

    vm = pl.BlockSpec(memory_space=pltpu.VMEM)
    out_shape = [jax.ShapeDtypeStruct((N_DEV, r, LANES), F32)] + ([jax.ShapeDtypeStruct((r, LANES), F32)] if reduce else [])
    return pl.pallas_call(
        body, name=name, in_specs=[vm], out_specs=[vm] * len(out_shape), out_shape=out_shape,
        scratch_shapes=[pltpu.SemaphoreType.DMA((7,)), pltpu.SemaphoreType.DMA((7,)), pltpu.SemaphoreType.DMA],
        compiler_params=pltpu.CompilerParams(vmem_limit_bytes=VMEM_LIMIT),
    )(v)


REDUCE_BLOCK_BYTES = 2 << 20


def _reduce_rows(r, c):
    cands = [b for b in range(16, r + 1, 16) if r % b == 0 and b * c * 4 <= REDUCE_BLOCK_BYTES]
    return max(cands)


def _add_halves(mine2, other, c_idx, *, name):
    _, nq, r, c = mine2.shape
    br = _reduce_rows(r, c)

    def body(c_ref, a_ref, b_ref, o_ref):
        o_ref[...] = (a_ref[0] + b_ref[...]).astype(o_ref.dtype)

    blk = pl.BlockSpec((1, br, c), lambda q, i, c_ref: (q, i, 0))
    return pl.pallas_call(
        body, name=name,
        grid_spec=pltpu.PrefetchScalarGridSpec(
            num_scalar_prefetch=1, grid=(nq, r // br),
            in_specs=[pl.BlockSpec((1, 1, br, c), lambda q, i, c_ref: (c_ref[0], q, i, 0)), blk],
            out_specs=blk),
        out_shape=jax.ShapeDtypeStruct((nq, r, c), BF16),
        compiler_params=_params(("parallel", "parallel")),
    )(c_idx, mine2, other)


def _sum_chips(parts, *, name):
    _, r, c = parts.shape
    br = _reduce_rows(r, c)

    def body(p_ref, o_ref):
        total = p_ref[0].astype(F32)
        for q in range(1, 4):
            total = total + p_ref[q].astype(F32)
        o_ref[...] = total

    return pl.pallas_call(
        body, name=name, grid=(r // br,),
        in_specs=[pl.BlockSpec((4, br, c), lambda i: (0, i, 0))],
        out_specs=pl.BlockSpec((br, c), lambda i: (i, 0)),
        out_shape=jax.ShapeDtypeStruct((r, c), F32),
        compiler_params=_params(("parallel",)),
    )(parts)


BIG = ("w_in", "w_out_sb", "w_out_ssd", "w_out_rw", "w_o")
BIG_AXIS = {"w_in": 2, "w_out_sb": 2, "w_out_ssd": 1, "w_out_rw": 2, "w_o": 1}
SMALL_SHARDED = {"conv_w": 320, "rw_w_up": 128, "rw_a_up": 128}
SMALL = ("norm_g", "conv_w", "conv_b", "dt_bias", "a_log", "d_skip", "ssd_norm_g", "rw_mu", "rw_w0", "rw_w_up",
         "rw_a0", "rw_a_up", "rw_k_k", "rw_k_a", "rw_r_k", "rw_ln_g", "rw_ln_b", "final_g")


def _rows_of(a):
    flat = a.reshape(-1)
    pad = (-flat.shape[0]) % LANES
    return jnp.pad(flat, (0, pad)).reshape(-1, LANES)


def _pack_rows(arrays, multiple=8):
    rows = jnp.concatenate([_rows_of(a) for a in arrays], axis=0)
    pad = (-rows.shape[0]) % multiple
    return jnp.pad(rows, ((0, pad), (0, 0)))


def _unpack_rows(rows, shapes):
    out, off = [], 0
    for shp in shapes:
        n = 1
        for d in shp:
            n *= d
        nr = -(-n // LANES)
        out.append(rows[off:off + nr].reshape(-1)[:n].reshape(shp))
        off += nr
    return out


def _pad_cols(w):
    z = jnp.zeros(w.shape[:-1] + (N_PAD - N_IN,), w.dtype)
    return jnp.concatenate([w[..., 0:3072], w[..., 6544:9616], w[..., 4368:6544], w[..., 4352:4368], z,
                            w[..., 3072:4352]], axis=-1)


def _unpad_cols(g):
    return jnp.concatenate([g[..., 0:3072], g[..., 8448:9728], g[..., 8320:8336], g[..., 6144:8320],
                            g[..., 3072:6144]], axis=-1)


def _split_chips(a, axis):
    n = a.shape[axis] // 4
    return jnp.stack([lax.slice_in_dim(a, q * n, (q + 1) * n, axis=axis) for q in range(4)])


def _join_chips(a, axis):
    return jnp.concatenate([a[q] for q in range(4)], axis=axis)


def kernel(x, norm_g, w_in, conv_w, conv_b, dt_bias, a_log, d_skip, ssd_norm_g, rw_mu, rw_w0, rw_w_up, rw_a0, rw_a_up, rw_k_k, rw_k_a, rw_r_k, rw_ln_g, rw_ln_b, w_out_sb, w_out_ssd, w_out_rw, w_o, final_g, loss_target, m_norm_g, m_w_in, m_conv_w, m_conv_b, m_dt_bias, m_a_log, m_d_skip, m_ssd_norm_g, m_rw_mu, m_rw_w0, m_rw_w_up, m_rw_a0, m_rw_a_up, m_rw_k_k, m_rw_k_a, m_rw_r_k, m_rw_ln_g, m_rw_ln_b, m_w_out_sb, m_w_out_ssd, m_w_out_rw, m_w_o, m_final_g, v_norm_g, v_w_in, v_conv_w, v_conv_b, v_dt_bias, v_a_log, v_d_skip, v_ssd_norm_g, v_rw_mu, v_rw_w0, v_rw_w_up, v_rw_a0, v_rw_a_up, v_rw_k_k, v_rw_k_a, v_rw_r_k, v_rw_ln_g, v_rw_ln_b, v_w_out_sb, v_w_out_ssd, v_w_out_rw, v_w_o, v_final_g):
    names = ("norm_g", "w_in", "conv_w", "conv_b", "dt_bias", "a_log", "d_skip", "ssd_norm_g", "rw_mu", "rw_w0",
             "rw_w_up", "rw_a0", "rw_a_up", "rw_k_k", "rw_k_a", "rw_r_k", "rw_ln_g", "rw_ln_b", "w_out_sb",
             "w_out_ssd", "w_out_rw", "w_o", "final_g")
    w_loc = dict(zip(names, (norm_g, w_in, conv_w, conv_b, dt_bias, a_log, d_skip, ssd_norm_g, rw_mu, rw_w0, rw_w_up,
                             rw_a0, rw_a_up, rw_k_k, rw_k_a, rw_r_k, rw_ln_g, rw_ln_b, w_out_sb, w_out_ssd, w_out_rw,
                             w_o, final_g)))
    m_loc = dict(zip(names, (m_norm_g, m_w_in, m_conv_w, m_conv_b, m_dt_bias, m_a_log, m_d_skip, m_ssd_norm_g,
                             m_rw_mu, m_rw_w0, m_rw_w_up, m_rw_a0, m_rw_a_up, m_rw_k_k, m_rw_k_a, m_rw_r_k,
                             m_rw_ln_g, m_rw_ln_b, m_w_out_sb, m_w_out_ssd, m_w_out_rw, m_w_o, m_final_g)))
    v_loc = dict(zip(names, (v_norm_g, v_w_in, v_conv_w, v_conv_b, v_dt_bias, v_a_log, v_d_skip, v_ssd_norm_g,
                             v_rw_mu, v_rw_w0, v_rw_w_up, v_rw_a0, v_rw_a_up, v_rw_k_k, v_rw_k_a, v_rw_r_k,
                             v_rw_ln_g, v_rw_ln_b, v_w_out_sb, v_w_out_ssd, v_w_out_rw, v_w_o, v_final_g)))
    chip = 2 * lax.axis_index("x") + lax.axis_index("y")
    core = lax.axis_index("c")

    got_mine = _chip_exchange([w_loc[n].astype(BF16) for n in BIG], per_dest=False, name="gather_big")
    got_theirs = _sibling_swap(got_mine, other_slot=False, name="gather_join")
    full = {}
    for n, mine, theirs in zip(BIG, got_mine, got_theirs):
        layers = [jnp.where(core == l, mine, theirs) for l in range(DEPTH)]
        full[n] = jnp.stack([jnp.concatenate([lay[q] for q in range(4)], axis=BIG_AXIS[n] - 1) for lay in layers])
    full["w_in"] = _pad_cols(full["w_in"])
    sm_names = tuple(SMALL_SHARDED)
    sm_shapes = [w_loc[n].shape for n in sm_names]
    (got_sm,) = _allgather_small(_pack_rows([w_loc[n] for n in sm_names]), reduce=False, name="gather_small")
    per_chip = [_unpack_rows(got_sm[4 * (q // 2) + 2 * (q % 2)], sm_shapes) for q in range(4)]
    for i, n in enumerate(sm_names):
        full[n] = jnp.concatenate([per_chip[q][i] for q in range(4)], axis=-1)

    def pad16(a):
        return jnp.zeros((1, LANES), F32).at[0, :SSD_HEADS].set(a)

    def layer_params(i):
        row = lambda n: w_loc[n][i].reshape(1, -1)
        cw = full["conv_w"][i]
        return dict(
            norm_g=row("norm_g"), w_in=full["w_in"][i], conv=[cw[k][None] for k in range(4)] + [row("conv_b")],
            dt_bias=pad16(dt_bias[i]), a_log=pad16(a_log[i]), d_skip=pad16(d_skip[i]),
            ssd_norm_g=row("ssd_norm_g"), rw_mu=row("rw_mu"),
            rw_pre=[row("rw_w0"), jnp.zeros((LANES, 512), F32).at[:HEAD].set(full["rw_w_up"][i]), row("rw_a0"),
                    jnp.zeros((LANES, 512), F32).at[HEAD:].set(full["rw_a_up"][i]), row("rw_k_k"), row("rw_k_a")],
            rw_post=[row("rw_ln_g"), row("rw_ln_b"), row("rw_r_k")],
            w_out_sb=full["w_out_sb"][i], w_out_ssd=full["w_out_ssd"][i], w_out_rw=full["w_out_rw"][i],
            w_o=full["w_o"][i])

    params = [layer_params(i) for i in range(DEPTH)]
    xs, saved = [x[0]], []
    for i in range(DEPTH):
        nxt, s = _layer_fwd(xs[-1], params[i], f"l{i}_")
        xs.append(nxt)
        saved.append(s)
    dx, loss_row, g_final = _final(xs[-1], final_g.reshape(1, -1), loss_target[0], bt=BT, name="final")
    grads = [None] * DEPTH
    for i in reversed(range(DEPTH)):
        dx, grads[i] = _layer_bwd(xs[i], dx, params[i], saved[i], f"l{i}_")

    def stacked(fn):
        return jnp.stack([fn(grads[i]) for i in range(DEPTH)])

    g_loc = {
        "norm_g": stacked(lambda g: g["norm_g"][0]),
        "w_in": stacked(lambda g: _unpad_cols(g["w_in"])),
        "conv_w": stacked(lambda g: jnp.concatenate(g["conv"][:4], axis=0)),
        "conv_b": stacked(lambda g: g["conv"][4][0]),
        "dt_bias": stacked(lambda g: g["dt_bias"][0, :SSD_HEADS]),
        "a_log": stacked(lambda g: g["a_log"][0, :SSD_HEADS]),
        "d_skip": stacked(lambda g: g["d_skip"][0, :SSD_HEADS]),
        "ssd_norm_g": stacked(lambda g: g["ssd_norm_g"][0]),
        "rw_mu": stacked(lambda g: g["rw_mu"][0]),
        "rw_w0": stacked(lambda g: g["rw_pre"][0][0]),
        "rw_w_up": stacked(lambda g: g["rw_pre"][1][:HEAD]),
        "rw_a0": stacked(lambda g: g["rw_pre"][2][0]),
        "rw_a_up": stacked(lambda g: g["rw_pre"][3][HEAD:]),
        "rw_k_k": stacked(lambda g: g["rw_pre"][4][0]),
        "rw_k_a": stacked(lambda g: g["rw_pre"][5][0]),
        "rw_r_k": stacked(lambda g: g["rw_r_k"].reshape(8, HEAD)),
        "rw_ln_g": stacked(lambda g: g["rw_ln_g"][0]),
        "rw_ln_b": stacked(lambda g: g["rw_ln_b"][0]),
        "w_out_sb": stacked(lambda g: g["w_out_sb"]),
        "w_out_ssd": stacked(lambda g: g["w_out_ssd"]),
        "w_out_rw": stacked(lambda g: g["w_out_rw"]),
        "w_o": stacked(lambda g: g["w_o"]),
        "final_g": g_final[0],
    }

    sends = [jnp.swapaxes(_split_chips(g_loc[n], BIG_AXIS[n]), 0, 1) for n in BIG]
    others = _sibling_swap(sends, other_slot=True, name="reduce_sibling")
    c_idx = core.reshape(1).astype(jnp.int32)
    parts = [_add_halves(s, o, c_idx, name="reduce_add_" + n) for n, s, o in zip(BIG, sends, others)]
    parts = _chip_exchange(parts, per_dest=True, name="reduce_chips")
    mine = [_sum_chips(p, name="reduce_sum_" + n) for n, p in zip(BIG, parts)]
    theirs = _sibling_swap(mine, other_slot=False, name="reduce_join")
    g_out = {n: jnp.stack([jnp.where(core == 0, a, b), jnp.where(core == 0, b, a)])
             for n, a, b in zip(BIG, mine, theirs)}

    sm_all = SMALL + ("loss",)
    sm_full_shapes = [g_loc[n].shape for n in SMALL] + [(1,)]
    _, summed = _allgather_small(_pack_rows([g_loc[n] for n in SMALL] + [loss_row[0, :1]]), reduce=True, name="reduce_small")
    sm = dict(zip(sm_all, _unpack_rows(summed, sm_full_shapes)))
    for n in SMALL:
        g_out[n] = sm[n]
    for n, wd in SMALL_SHARDED.items():
        g_out[n] = lax.dynamic_slice_in_dim(sm[n], chip * wd, wd, axis=sm[n].ndim - 1)
    loss = sm["loss"][0]

    upd = {n: _adamw(w_loc[n], g_out[n], m_loc[n], v_loc[n], name="adamw_" + n) for n in names}
    return (loss, dx[None], *[g_out[n] for n in names], *[upd[n][0] for n in names],
            *[upd[n][1] for n in names], *[upd[n][2] for n in names])
```

```python
import functools

import jax
import jax.numpy as jnp
from jax import lax
from jax.experimental import pallas as pl
from jax.experimental.pallas import tpu as pltpu

F32 = jnp.float32
BF16 = jnp.bfloat16

D_MODEL = 1024
DEPTH = 2
HEAD = 64
LANES = 128
CHUNK = 128
RMS_EPS = 1e-6
GN_EPS = 64e-5
VMEM_LIMIT = 56 * 1024 * 1024

N_IN = 9616
N_PAD = 9728
C_SB, C_Z, C_GATES, C_RW, C_LO, C_DT, C_XBC = 0, 2048, 3072, 6144, 8192, 8320, 8448
RW_COLS = 2176
XBC_COLS = 1280

ADAM_LR, ADAM_B1, ADAM_B2, ADAM_EPS, ADAM_WD, ADAM_STEP = 0.001, 0.9, 0.999, 1e-08, 0.01, 10


def _params(sem=None):
    return pltpu.CompilerParams(dimension_semantics=sem, vmem_limit_bytes=VMEM_LIMIT)


@jax.custom_vjp
def _sigmoid(x):
    return 1.0 / (1.0 + jnp.exp(-x))


def _sigmoid_fwd(x):
    s = _sigmoid(x)
    return s, s


def _sigmoid_bwd(s, g):
    return (g * s * (1.0 - s),)


_sigmoid.defvjp(_sigmoid_fwd, _sigmoid_bwd)


@jax.custom_vjp
def _silu(x):
    return x * _sigmoid(x)


def _silu_fwd(x):
    s = _sigmoid(x)
    return x * s, (x, s)


def _silu_bwd(res, g):
    x, s = res
    return (g * (s + x * s * (1.0 - s)),)


_silu.defvjp(_silu_fwd, _silu_bwd)


@jax.custom_vjp
def _softplus(x):
    return jnp.maximum(x, 0.0) + jnp.log(1.0 + jnp.exp(-jnp.abs(x)))


def _softplus_fwd(x):
    return _softplus(x), x


def _softplus_bwd(x, g):
    return (g * _sigmoid(x),)


_softplus.defvjp(_softplus_fwd, _softplus_bwd)


def _dot(a, b, dims):
    return lax.dot_general(a.astype(BF16), b.astype(BF16), (dims, ((), ())), preferred_element_type=F32)


def _dot_nn(a, b):
    return _dot(a, b, ((1,), (0,)))


def _dot_nt(a, b):
    return _dot(a, b, ((1,), (1,)))


def _dot_tn(a, b):
    return _dot(a, b, ((0,), (0,)))


@jax.custom_vjp
def _bdot(a, b):
    return _dot_nn(a, b)


def _bdot_fwd(a, b):
    return _dot_nn(a, b), (a, b)


def _bdot_bwd(res, g):
    a, b = res
    return _dot_nt(g, b), _dot_tn(a, g)


_bdot.defvjp(_bdot_fwd, _bdot_bwd)


def _split2(x):
    hi = x.astype(BF16)
    lo = (x - hi.astype(F32)).astype(BF16)
    return hi, lo


_NT = (((1,), (1,)), ((), ()))
_NN = (((1,), (0,)), ((), ()))
_TN = (((0,), (0,)), ((), ()))


def _dot2(x, m, dn=_NN):
    hi, lo = _split2(x)
    return (lax.dot_general(hi, m, dn, preferred_element_type=F32)
            + lax.dot_general(lo, m, dn, preferred_element_type=F32))


def _dot2_tn(x, m):
    return _dot2(x, m, _TN)


def _seg_matrix(n):
    r = lax.broadcasted_iota(jnp.int32, (n, n), 0) // HEAD
    c = lax.broadcasted_iota(jnp.int32, (n, n), 1) // HEAD
    return (r == c).astype(BF16)


@jax.custom_vjp
def _segsum2(x, seg):
    return _dot2(x, seg)


def _segsum2_fwd(x, seg):
    return _dot2(x, seg), seg


def _segsum2_bwd(seg, g):
    return _dot2(g, seg), jnp.zeros_like(seg)


_segsum2.defvjp(_segsum2_fwd, _segsum2_bwd)


def _make_segsum(seg):
    return lambda x: _segsum2(x, seg)


def _shift_down_raw(x, k):
    row = lax.broadcasted_iota(jnp.int32, x.shape, 0)
    return jnp.where(row >= k, pltpu.roll(x, k, 0), 0.0)


def _shift_up_raw(x, k):
    t = x.shape[0]
    row = lax.broadcasted_iota(jnp.int32, x.shape, 0)
    return jnp.where(row < t - k, pltpu.roll(x, t - k, 0), 0.0)


@functools.partial(jax.custom_vjp, nondiff_argnums=(1,))
def _shift_down(x, k):
    return _shift_down_raw(x, k)


def _shift_down_fwd(x, k):
    return _shift_down_raw(x, k), None


def _shift_down_bwd(k, _, g):
    return (_shift_up_raw(g, k),)


_shift_down.defvjp(_shift_down_fwd, _shift_down_bwd)


def _mm(a, b, *, name, ta=False, tb=False, add=None, out_dtype=F32, tm=2048, tn=512, tk=None):
    m, k = (a.shape[1], a.shape[0]) if ta else a.shape
    n = b.shape[0] if tb else b.shape[1]
    tm, tn = min(tm, m), min(tn, n)
    tk = k if tk is None else tk
    nk = k // tk
    assert m % tm == 0 and n % tn == 0 and k % tk == 0
    dims = ((0 if ta else 1,), (1 if tb else 0,))

    def body(a_ref, b_ref, *refs):
        o_ref, acc_ref = refs[-2:]
        p = _dot(a_ref[...], b_ref[...], dims)

        def emit(total):
            if add is not None:
                total = total + refs[0][...]
            o_ref[...] = total.astype(o_ref.dtype)

        if nk == 1:
            emit(p)
        else:
            kk = pl.program_id(2)

            @pl.when(kk == 0)
            def _():
                acc_ref[...] = p

            @pl.when(kk > 0)
            def _():
                acc_ref[...] += p

            @pl.when(kk == nk - 1)
            def _():
                emit(acc_ref[...])

    a_spec = pl.BlockSpec((tk, tm), lambda i, j, kk: (kk, i)) if ta else pl.BlockSpec((tm, tk), lambda i, j, kk: (i, kk))
    b_spec = pl.BlockSpec((tn, tk), lambda i, j, kk: (j, kk)) if tb else pl.BlockSpec((tk, tn), lambda i, j, kk: (kk, j))
    o_spec = pl.BlockSpec((tm, tn), lambda i, j, kk: (i, j))
    return pl.pallas_call(
        body, name=name, grid=(m // tm, n // tn, nk),
        in_specs=[a_spec, b_spec] + ([o_spec] if add is not None else []), out_specs=o_spec,
        out_shape=jax.ShapeDtypeStruct((m, n), out_dtype),
        scratch_shapes=[pltpu.VMEM((tm, tn) if nk > 1 else (8, LANES), F32)],
        compiler_params=_params(("parallel", "parallel", "arbitrary")),
    )(a, b, *([add] if add is not None else []))


def _row_specs(rows, bt):
    return [pl.BlockSpec((bt, w), functools.partial(lambda i, c: (i, c), c=c)) for _, w, c in rows]


def _full_spec(p):
    return pl.BlockSpec(p.shape, functools.partial(lambda i, nd: (0,) * nd, nd=p.ndim))


def _rowwise(f, rows, pars, out_widths, *, bt, name, acc_widths=()):
    t = rows[0][0].shape[0]
    nr, npar, no, na = len(rows), len(pars), len(out_widths), len(acc_widths)

    def body(*refs):
        vals = [r[...] for r in refs[:nr + npar]]
        outs = f(*vals)
        for o_ref, o in zip(refs[nr + npar:nr + npar + no], outs[:no]):
            o_ref[...] = o.astype(o_ref.dtype)
        if na:
            first = pl.program_id(0) == 0
            for a_ref, a in zip(refs[nr + npar + no:], outs[no:]):
                @pl.when(first)
                def _():
                    a_ref[...] = jnp.zeros_like(a_ref)
                a_ref[...] += a

    return pl.pallas_call(
        body, name=name, grid=(t // bt,),
        in_specs=_row_specs(rows, bt) + [_full_spec(p) for p in pars],
        out_specs=[pl.BlockSpec((bt, w), lambda i: (i, 0)) for w in out_widths]
        + [pl.BlockSpec((1, w), lambda i: (0, 0)) for w in acc_widths],
        out_shape=[jax.ShapeDtypeStruct((t, w), F32) for w in out_widths]
        + [jax.ShapeDtypeStruct((1, w), F32) for w in acc_widths],
        compiler_params=_params(("arbitrary",)),
    )(*[r[0] for r in rows], *pars)


def _rowwise_bwd(f, rows, pars, douts, *, bt, name, groups=None):
    t = rows[0][0].shape[0]
    nr, npar, nd = len(rows), len(pars), len(douts)
    groups = [[i] for i in range(nr)] if groups is None else groups
    widths = [r[1] for r in rows]

    def body(*refs):
        vals = [r[...] for r in refs[:nr + npar]]
        cts = tuple(r[...] for r in refs[nr + npar:nr + npar + nd])
        _, vjp = jax.vjp(lambda *a: tuple(f(*a)), *vals)
        grads = vjp(cts)
        out_refs = refs[nr + npar + nd:]
        for g_ref, grp in zip(out_refs[:len(groups)], groups):
            off = 0
            for i in grp:
                g_ref[:, off:off + widths[i]] = grads[i]
                off += widths[i]
        first = pl.program_id(0) == 0
        for p_ref, g in zip(out_refs[len(groups):], grads[nr:]):
            @pl.when(first)
            def _():
                p_ref[...] = jnp.zeros_like(p_ref)
            p_ref[...] += g

    gw = [sum(widths[i] for i in grp) for grp in groups]
    return pl.pallas_call(
        body, name=name, grid=(t // bt,),
        in_specs=_row_specs(rows, bt) + [_full_spec(p) for p in pars] + _row_specs(douts, bt),
        out_specs=[pl.BlockSpec((bt, w), lambda i: (i, 0)) for w in gw] + [_full_spec(p) for p in pars],
        out_shape=[jax.ShapeDtypeStruct((t, w), F32) for w in gw] + [jax.ShapeDtypeStruct(p.shape, F32) for p in pars],
        compiler_params=_params(("arbitrary",)),
    )(*[r[0] for r in rows], *pars, *[d[0] for d in douts])


def _colwise(f, x, c0, ncols, pars, *, bc, name):
    t = x.shape[0]

    def body(x_ref, *refs):
        o_ref = refs[-1]
        o_ref[...] = f(x_ref[...], *[r[...] for r in refs[:-1]])

    return pl.pallas_call(
        body, name=name, grid=(ncols // bc,),
        in_specs=[pl.BlockSpec((t, bc), lambda j: (0, j + c0 // bc))]
        + [pl.BlockSpec((p.shape[0], bc), lambda j: (0, j)) for p in pars],
        out_specs=pl.BlockSpec((t, bc), lambda j: (0, j)),
        out_shape=jax.ShapeDtypeStruct((t, ncols), F32),
        compiler_params=_params(("parallel",)),
    )(x, *pars)


def _colwise_bwd(f, x, c0, ncols, pars, dout, *, bc, name):
    t = x.shape[0]
    npar = len(pars)

    def body(x_ref, *refs):
        vals = [x_ref[...]] + [r[...] for r in refs[:npar]]
        _, vjp = jax.vjp(f, *vals)
        grads = vjp(refs[npar][...])
        for g_ref, g in zip(refs[npar + 1:], grads):
            g_ref[...] = g

    return pl.pallas_call(
        body, name=name, grid=(ncols // bc,),
        in_specs=[pl.BlockSpec((t, bc), lambda j: (0, j + c0 // bc))]
        + [pl.BlockSpec((p.shape[0], bc), lambda j: (0, j)) for p in pars]
        + [pl.BlockSpec((t, bc), lambda j: (0, j))],
        out_specs=[pl.BlockSpec((t, bc), lambda j: (0, j))]
        + [pl.BlockSpec((p.shape[0], bc), lambda j: (0, j)) for p in pars],
        out_shape=[jax.ShapeDtypeStruct((t, ncols), F32)] + [jax.ShapeDtypeStruct(p.shape, F32) for p in pars],
        compiler_params=_params(("parallel",)),
    )(x, *pars, dout)


def _f_rms(x, g):
    return (x * lax.rsqrt(jnp.mean(x * x, axis=-1, keepdims=True) + RMS_EPS) * g,)


def _f_sb_gate(y, gate):
    return (y * _silu(gate),)


def _f_ssd_norm(y, z, g):
    u = y * _silu(z)
    return (u * lax.rsqrt(jnp.mean(u * u, axis=-1, keepdims=True) + RMS_EPS) * g,)


def _f_merge(p_sb, p_ssd, p_rw, g_sb, g_ssd, g_rw):
    return (_sigmoid(g_sb) * p_sb + _sigmoid(g_ssd) * p_ssd + _sigmoid(g_rw) * p_rw,)


def _f_rw_pre(k, lo, w0, w_up, a0, a_up, k_k, k_a):
    segsum = _make_segsum(_seg_matrix(k.shape[1]))
    lane = lax.broadcasted_iota(jnp.int32, lo.shape, 1)
    w_lo = jnp.where(lane < HEAD, jnp.tanh(lo), 0.0)
    a_lo = jnp.where(lane >= HEAD, lo, 0.0)
    w = -_softplus(-(w0 + _bdot(w_lo, w_up))) - 0.5
    log_decay = -jnp.exp(w)
    a = _sigmoid(a0 + _bdot(a_lo, a_up))
    kk = k * k_k
    kk = kk / jnp.maximum(jnp.sqrt(segsum(kk * kk)), 1e-12)
    return log_decay, k * (1.0 + (a - 1.0) * k_a), -kk, kk * a


def _f_rw_post(y, r, k2, v, gate, ln_g, ln_b, r_k):
    segsum = _make_segsum(_seg_matrix(y.shape[1]))
    yc = y - segsum(y) * (1.0 / HEAD)
    var = segsum(yc * yc) * (1.0 / HEAD)
    yn = yc * lax.rsqrt(var + GN_EPS) * ln_g + ln_b
    return ((yn + segsum(r * k2 * r_k) * v) * _silu(gate),)


def _f_rw_mix(slab, mu):
    return slab + (_shift_down(slab, 1) - slab) * mu


def _f_conv(x, w0, w1, w2, w3, b):
    acc = x * w3 + b
    for i, w in enumerate((w0, w1, w2)):
        acc = acc + _shift_down(x, 3 - i) * w
    return _silu(acc)


def _log_sigmoid(z):
    return jnp.minimum(z, 0.0) - jnp.log(1.0 + jnp.exp(-jnp.abs(z)))


def _prefix_matrix(kind):
    j = lax.broadcasted_iota(jnp.int32, (CHUNK, 2 * CHUNK), 0)
    s = lax.broadcasted_iota(jnp.int32, (CHUNK, 2 * CHUNK), 1)
    tri = {"gt": j > s, "le": j <= s, "lt": j < s}[kind]
    return (tri | (s >= CHUNK)).astype(BF16)


def _sb_specs(t):
    q = pl.BlockSpec((CHUNK, LANES), lambda j, i: (i, j))
    k = pl.BlockSpec((t, LANES), lambda j, i: (0, 4 + j))
    v = pl.BlockSpec((t, LANES), lambda j, i: (0, 8 + j))
    return q, k, v


def _sb_fwd(proj, *, name):
    t = proj.shape[0]
    scale = HEAD ** -0.5

    def body(q_ref, k_ref, v_ref, y_ref, lt_ref):
        i = pl.program_id(1)
        lane = lax.broadcasted_iota(jnp.int32, (CHUNK, LANES), 1)
        diff = (lax.broadcasted_iota(jnp.int32, (CHUNK, CHUNK), 1)
                - lax.broadcasted_iota(jnp.int32, (CHUNK, CHUNK), 0))
        m_f = _prefix_matrix("gt")
        q = q_ref[...] * scale
        qh = [jnp.where((lane // HEAD) == h, q, 0.0).astype(BF16) for h in (0, 1)]

        def step(it, carry):
            off = pl.multiple_of((i - it) * CHUNK, CHUNK)
            kblk = k_ref[pl.ds(off, CHUNK), :].astype(BF16)
            vblk = v_ref[pl.ds(off, CHUNK), :].astype(BF16)
            mask = diff < it * CHUNK
            new = []
            for h in (0, 1):
                c, acc = carry[2 * h], carry[2 * h + 1]
                z = lax.dot_general(qh[h], kblk, _NT, preferred_element_type=F32)
                lb = _log_sigmoid(z)
                w2 = _dot2(jnp.where(mask, lb - z, 0.0), m_f)
                att = jnp.where(mask, jnp.exp(lb + c + w2[:, :CHUNK]), 0.0)
                acc = acc + lax.dot_general(att.astype(BF16), vblk, _NN, preferred_element_type=F32)
                new += [c + w2[:, CHUNK:], acc]
            return tuple(new)

        zero = jnp.zeros((CHUNK, LANES), F32)
        c_a, acc_a, c_b, acc_b = lax.fori_loop(0, i + 1, step, (zero, zero, zero, zero))
        y_ref[...] = jnp.where(lane < HEAD, acc_a, acc_b)
        lt_ref[0] = c_a
        lt_ref[1] = c_b

    return pl.pallas_call(
        body, name=name, grid=(4, t // CHUNK),
        in_specs=list(_sb_specs(t)),
        out_specs=[pl.BlockSpec((CHUNK, LANES), lambda j, i: (i, j)),
                   pl.BlockSpec((2, CHUNK, LANES), lambda j, i: (j, i, 0))],
        out_shape=[jax.ShapeDtypeStruct((t, 4 * LANES), F32), jax.ShapeDtypeStruct((8, t, LANES), F32)],
        compiler_params=_params(("parallel", "arbitrary")),
    )(proj, proj, proj)


def _sb_bwd(proj, dy, lt, *, name):
    t = proj.shape[0]
    scale = HEAD ** -0.5

    def body(q_ref, k_ref, v_ref, dy_ref, lt_ref, dq_ref, dk_ref, dv_ref):
        i = pl.program_id(1)

        @pl.when(i == 0)
        def _():
            dk_ref[...] = jnp.zeros_like(dk_ref)
            dv_ref[...] = jnp.zeros_like(dv_ref)

        lane = lax.broadcasted_iota(jnp.int32, (CHUNK, LANES), 1)
        diff = (lax.broadcasted_iota(jnp.int32, (CHUNK, CHUNK), 1)
                - lax.broadcasted_iota(jnp.int32, (CHUNK, CHUNK), 0))
        m_le, m_lt = _prefix_matrix("le"), _prefix_matrix("lt")
        q = q_ref[...] * scale
        dy_blk = dy_ref[...]
        qh = [jnp.where((lane // HEAD) == h, q, 0.0).astype(BF16) for h in (0, 1)]
        doh = [jnp.where((lane // HEAD) == h, dy_blk, 0.0).astype(BF16) for h in (0, 1)]
        lth = [lt_ref[0], lt_ref[1]]

        def step(kb, carry):
            off = pl.multiple_of(kb * CHUNK, CHUNK)
            kblk = k_ref[pl.ds(off, CHUNK), :].astype(BF16)
            vblk = v_ref[pl.ds(off, CHUNK), :].astype(BF16)
            mask = diff < (i - kb) * CHUNK
            new = []
            dk_acc = jnp.zeros((CHUNK, LANES), F32)
            dv_acc = jnp.zeros((CHUNK, LANES), F32)
            for h in (0, 1):
                cp, cg, dq = carry[3 * h:3 * h + 3]
                z = lax.dot_general(qh[h], kblk, _NT, preferred_element_type=F32)
                lb = _log_sigmoid(z)
                w2 = _dot2(jnp.where(mask, lb - z, 0.0), m_le)
                att = jnp.where(mask, jnp.exp(lb + lth[h] - cp - w2[:, :CHUNK]), 0.0)
                d_att = lax.dot_general(doh[h], vblk, _NT, preferred_element_type=F32)
                d_e = d_att * att
                g2 = _dot2(d_e, m_lt)
                sig = jnp.exp(lb)
                dz = jnp.where(mask, d_e * (1.0 - sig) - (cg + g2[:, :CHUNK]) * sig, 0.0).astype(BF16)
                dq = dq + lax.dot_general(dz, kblk, _NN, preferred_element_type=F32)
                dk_acc = dk_acc + lax.dot_general(dz, qh[h], _TN, preferred_element_type=F32)
                dv_acc = dv_acc + lax.dot_general(att.astype(BF16), doh[h], _TN, preferred_element_type=F32)
                new += [cp + w2[:, CHUNK:], cg + g2[:, CHUNK:], dq]
            dk_ref[pl.ds(off, CHUNK), :] += dk_acc
            dv_ref[pl.ds(off, CHUNK), :] += dv_acc
            return tuple(new)

        zero = jnp.zeros((CHUNK, LANES), F32)
        out = lax.fori_loop(0, i + 1, step, (zero,) * 6)
        dq_ref[...] = jnp.where(lane < HEAD, out[2], out[5]) * scale

    q_spec, k_spec, v_spec = _sb_specs(t)
    blk = pl.BlockSpec((CHUNK, LANES), lambda j, i: (i, j))
    col = pl.BlockSpec((t, LANES), lambda j, i: (0, j))
    return pl.pallas_call(
        body, name=name, grid=(4, t // CHUNK),
        in_specs=[q_spec, k_spec, v_spec, blk, pl.BlockSpec((2, CHUNK, LANES), lambda j, i: (j, i, 0))],
        out_specs=[blk, col, col],
        out_shape=[jax.ShapeDtypeStruct((t, 4 * LANES), F32)] * 3,
        compiler_params=_params(("parallel", "arbitrary")),
    )(proj, proj, proj, dy, lt)


SB_BQ = 256
SB_BK = 256


def _tri_ones(kind):
    j = lax.broadcasted_iota(jnp.int32, (SB_BK, SB_BK + LANES), 0)
    s = lax.broadcasted_iota(jnp.int32, (SB_BK, SB_BK + LANES), 1)
    tri = {"gt": j > s, "le": j <= s, "lt": j < s}[kind]
    return (tri | (s >= SB_BK)).astype(BF16)


def _sb_common(q_ref):
    lane = lax.broadcasted_iota(jnp.int32, (SB_BQ, LANES), 1)
    q = q_ref[...] * (HEAD ** -0.5)
    q2 = jnp.concatenate([jnp.where(lane < HEAD, q, 0.0), jnp.where(lane >= HEAD, q, 0.0)], axis=0).astype(BF16)
    diff = (lax.broadcasted_iota(jnp.int32, (2 * SB_BQ, SB_BK), 1)
            - (lax.broadcasted_iota(jnp.int32, (2 * SB_BQ, SB_BK), 0) & (SB_BQ - 1)))
    return lane, q2, diff


def _rep(x):
    return jnp.concatenate([x] * (SB_BK // LANES), axis=1)


def _sb2_specs(t):
    q = pl.BlockSpec((SB_BQ, LANES), lambda j, i: (i, j))
    k = pl.BlockSpec((t, LANES), lambda j, i: (0, 4 + j))
    v = pl.BlockSpec((t, LANES), lambda j, i: (0, 8 + j))
    return q, k, v


def _sb2_fwd(proj, *, name):
    t = proj.shape[0]

    def body(q_ref, k_ref, v_ref, y_ref, lt_ref):
        i = pl.program_id(1)
        lane, q2, diff = _sb_common(q_ref)
        m_f = _tri_ones("gt")
        nk = (i + 1) * (SB_BQ // SB_BK)

        def step(it, carry):
            c, acc = carry
            kb = nk - 1 - it
            off = pl.multiple_of(kb * SB_BK, SB_BK)
            kblk = k_ref[pl.ds(off, SB_BK), :].astype(BF16)
            vblk = v_ref[pl.ds(off, SB_BK), :].astype(BF16)
            mask = diff < i * SB_BQ - kb * SB_BK
            z = lax.dot_general(q2, kblk, _NT, preferred_element_type=F32)
            lb = _log_sigmoid(z)
            w2 = _dot2(jnp.where(mask, lb - z, 0.0), m_f)
            att = jnp.where(mask, jnp.exp(lb + _rep(c) + w2[:, :SB_BK]), 0.0)
            acc = acc + lax.dot_general(att.astype(BF16), vblk, _NN, preferred_element_type=F32)
            return c + w2[:, SB_BK:], acc

        zero = jnp.zeros((2 * SB_BQ, LANES), F32)
        c, acc = lax.fori_loop(0, nk, step, (zero, zero))
        y_ref[...] = jnp.where(lane < HEAD, acc[:SB_BQ], acc[SB_BQ:])
        lt_ref[0] = c[:SB_BQ]
        lt_ref[1] = c[SB_BQ:]

    return pl.pallas_call(
        body, name=name, grid=(4, t // SB_BQ),
        in_specs=list(_sb2_specs(t)),
        out_specs=[pl.BlockSpec((SB_BQ, LANES), lambda j, i: (i, j)),
                   pl.BlockSpec((2, SB_BQ, LANES), lambda j, i: (j, i, 0))],
        out_shape=[jax.ShapeDtypeStruct((t, 4 * LANES), F32), jax.ShapeDtypeStruct((8, t, LANES), F32)],
        compiler_params=_params(("parallel", "arbitrary")),
    )(proj, proj, proj)


def _sb2_bwd(proj, dy, lt, *, name):
    t = proj.shape[0]

    def body(q_ref, k_ref, v_ref, dy_ref, lt_ref, dq_ref, dk_ref, dv_ref):
        i = pl.program_id(1)

        @pl.when(i == 0)
        def _():
            dk_ref[...] = jnp.zeros_like(dk_ref)
            dv_ref[...] = jnp.zeros_like(dv_ref)

        lane, q2, diff = _sb_common(q_ref)
        m_le, m_lt = _tri_ones("le"), _tri_ones("lt")
        dy_blk = dy_ref[...]
        do2 = jnp.concatenate([jnp.where(lane < HEAD, dy_blk, 0.0), jnp.where(lane >= HEAD, dy_blk, 0.0)],
                              axis=0).astype(BF16)
        lt2 = jnp.concatenate([lt_ref[0], lt_ref[1]], axis=0)

        def step(kb, carry):
            cp, cg, dq = carry
            off = pl.multiple_of(kb * SB_BK, SB_BK)
            kblk = k_ref[pl.ds(off, SB_BK), :].astype(BF16)
            vblk = v_ref[pl.ds(off, SB_BK), :].astype(BF16)
            mask = diff < i * SB_BQ - kb * SB_BK
            z = lax.dot_general(q2, kblk, _NT, preferred_element_type=F32)
            lb = _log_sigmoid(z)
            w2 = _dot2(jnp.where(mask, lb - z, 0.0), m_le)
            att = jnp.where(mask, jnp.exp(lb + _rep(lt2 - cp) - w2[:, :SB_BK]), 0.0)
            d_e = lax.dot_general(do2, vblk, _NT, preferred_element_type=F32) * att
            g2 = _dot2(d_e, m_lt)
            sig = jnp.exp(lb)
            dz = jnp.where(mask, d_e * (1.0 - sig) - (_rep(cg) + g2[:, :SB_BK]) * sig, 0.0).astype(BF16)
            dq = dq + lax.dot_general(dz, kblk, _NN, preferred_element_type=F32)
            dk_ref[pl.ds(off, SB_BK), :] += lax.dot_general(dz, q2, _TN, preferred_element_type=F32)
            dv_ref[pl.ds(off, SB_BK), :] += lax.dot_general(att.astype(BF16), do2, _TN, preferred_element_type=F32)
            return cp + w2[:, SB_BK:], cg + g2[:, SB_BK:], dq

        zero = jnp.zeros((2 * SB_BQ, LANES), F32)
        _, _, dq = lax.fori_loop(0, (i + 1) * (SB_BQ // SB_BK), step, (zero, zero, zero))
        dq_ref[...] = jnp.where(lane < HEAD, dq[:SB_BQ], dq[SB_BQ:]) * (HEAD ** -0.5)

    q_spec, k_spec, v_spec = _sb2_specs(t)
    blk = pl.BlockSpec((SB_BQ, LANES), lambda j, i: (i, j))
    col = pl.BlockSpec((t, LANES), lambda j, i: (0, j))
    return pl.pallas_call(
        body, name=name, grid=(4, t // SB_BQ),
        in_specs=[q_spec, k_spec, v_spec, blk, pl.BlockSpec((2, SB_BQ, LANES), lambda j, i: (j, i, 0))],
        out_specs=[blk, col, col],
        out_shape=[jax.ShapeDtypeStruct((t, 4 * LANES), F32)] * 3,
        compiler_params=_params(("parallel", "arbitrary")),
    )(proj, proj, proj, dy, lt)


SSD_HEADS = 16
SSD_PAIRS = 8


def _split3(x):
    a = x.astype(BF16)
    r = x - a.astype(F32)
    b = r.astype(BF16)
    return a, b, (r - b.astype(F32)).astype(BF16)


def _dot3(x, m, dn=_NN):
    return sum(lax.dot_general(p, m, dn, preferred_element_type=F32) for p in _split3(x))


def _mdot3(m, x):
    return sum(lax.dot_general(m, p, _NN, preferred_element_type=F32) for p in _split3(x))


def _ssd_common(dtr, dtb, alog, acsx_s, acst_s):
    lane = lax.broadcasted_iota(jnp.int32, (CHUNK, LANES), 1)
    lane1 = lax.broadcasted_iota(jnp.int32, (1, LANES), 1)
    arow = jnp.where(lane1 < SSD_HEADS, -jnp.exp(alog), 0.0)
    dt = jnp.where(lane < SSD_HEADS, _softplus(dtr + dtb), 0.0)
    da = dt * arow
    r = lax.broadcasted_iota(jnp.int32, (CHUNK, CHUNK), 0)
    c = lax.broadcasted_iota(jnp.int32, (CHUNK, CHUNK), 1)
    tril = (r >= c).astype(BF16)
    triu = (r <= c).astype(BF16)
    acs = _mdot3(tril, da)
    acst_s[...] = _dot3(da, triu, _TN)
    eh = lax.broadcasted_iota(jnp.int32, (LANES, 8 * LANES), 0)
    e = (eh == lax.broadcasted_iota(jnp.int32, (LANES, 8 * LANES), 1) // HEAD).astype(BF16)
    eh2 = lax.broadcasted_iota(jnp.int32, (LANES, 16 * LANES), 0)
    e2 = (eh2 == lax.broadcasted_iota(jnp.int32, (LANES, 16 * LANES), 1) // LANES).astype(BF16)
    acsx_s[...] = _dot3(acs, e)
    return dt, arow, _dot3(dt, e), _dot3(acs, e2), e, tril, triu


def _ssd_fwd(xc, proj, dtb, alog, dsk, *, name):
    t = xc.shape[0]
    nc = t // CHUNK

    def body(x_ref, b_ref, c_ref, dtr_ref, dtb_ref, alog_ref, dsk_ref, y_ref, hin_ref, acsx_s, acst_s, h_s):
        @pl.when(pl.program_id(0) == 0)
        def _():
            h_s[...] = jnp.zeros_like(h_s)

        dt, arow, dt_x, acs_b, e, tril, _ = _ssd_common(dtr_ref[...], dtb_ref[...], alog_ref[...], acsx_s, acst_s)
        dsk_x = _dot3(jnp.broadcast_to(dsk_ref[...], (CHUNK, LANES)), e)
        lane = lax.broadcasted_iota(jnp.int32, (CHUNK, LANES), 1)
        causal = (lax.broadcasted_iota(jnp.int32, (CHUNK, CHUNK), 0)
                  >= lax.broadcasted_iota(jnp.int32, (CHUNK, CHUNK), 1))
        for j in range(SSD_PAIRS):
            g = j // 4
            sl = slice(j * LANES, (j + 1) * LANES)
            if j % 4 == 0:
                bg = jnp.where(lane // HEAD == g, b_ref[...], 0.0)
                cg = jnp.where(lane // HEAD == g, c_ref[...], 0.0)
                cb = _dot_nt(cg, bg)
            x = x_ref[:, sl]
            a = acsx_s[:, sl]
            at = acsx_s[CHUNK - 1:CHUNK, sl]
            xdt = x * dt_x[:, sl]
            hin = h_s[j]
            hin_ref[0, j] = hin
            y = jnp.exp(a) * _dot_nn(cg, hin) + x * dsk_x[:, sl]
            h_s[j] = jnp.exp(at) * hin + _dot_tn(bg, xdt * jnp.exp(at - a))
            yd = []
            for hh in (0, 1):
                h = 2 * j + hh
                dec = jnp.exp(jnp.minimum(acs_b[:, h * LANES:(h + 1) * LANES] - acst_s[pl.ds(h, 1), :], 0.0))
                yd.append(_dot_nn(jnp.where(causal, cb * dec, 0.0), xdt))
            y_ref[:, sl] = y + jnp.where(lane < HEAD, yd[0], yd[1])

    one = pl.BlockSpec((1, LANES), lambda i: (0, 0))
    return pl.pallas_call(
        body, name=name, grid=(nc,),
        in_specs=[pl.BlockSpec((CHUNK, 8 * LANES), lambda i: (i, 0)),
                  pl.BlockSpec((CHUNK, LANES), lambda i: (i, 8)),
                  pl.BlockSpec((CHUNK, LANES), lambda i: (i, 9)),
                  pl.BlockSpec((CHUNK, LANES), lambda i: (i, C_DT // LANES)), one, one, one],
        out_specs=[pl.BlockSpec((CHUNK, 8 * LANES), lambda i: (i, 0)),
                   pl.BlockSpec((1, SSD_PAIRS, LANES, LANES), lambda i: (i, 0, 0, 0))],
        out_shape=[jax.ShapeDtypeStruct((t, 8 * LANES), F32),
                   jax.ShapeDtypeStruct((nc, SSD_PAIRS, LANES, LANES), F32)],
        scratch_shapes=[pltpu.VMEM((CHUNK, 8 * LANES), F32), pltpu.VMEM((LANES, CHUNK), F32),
                        pltpu.VMEM((SSD_PAIRS, LANES, LANES), F32)],
        compiler_params=_params(("arbitrary",)),
    )(xc, xc, xc, proj, dtb, alog, dsk)


def _ssd_bwd(xc, proj, dtb, alog, dsk, hin_all, dy, *, name):
    t = xc.shape[0]
    nc = t // CHUNK

    def body(x_ref, b_ref, c_ref, dtr_ref, dtb_ref, alog_ref, dsk_ref, hin_ref, dy_ref,
             dxc_ref, ddtr_ref, ddtb_ref, dalog_ref, ddsk_ref, acsx_s, acst_s, dh_s, dax_s, ddx_s):
        @pl.when(pl.program_id(0) == 0)
        def _():
            dh_s[...] = jnp.zeros_like(dh_s)
            ddtb_ref[...] = jnp.zeros_like(ddtb_ref)
            dalog_ref[...] = jnp.zeros_like(dalog_ref)
            ddsk_ref[...] = jnp.zeros_like(ddsk_ref)

        dtr = dtr_ref[...]
        dtb = dtb_ref[...]
        dt, arow, dt_x, acs_b, e, tril, triu = _ssd_common(dtr, dtb, alog_ref[...], acsx_s, acst_s)
        dsk_x = _dot3(jnp.broadcast_to(dsk_ref[...], (CHUNK, LANES)), e)
        lane = lax.broadcasted_iota(jnp.int32, (CHUNK, LANES), 1)
        rowi = lax.broadcasted_iota(jnp.int32, (CHUNK, LANES), 0)
        causal = (lax.broadcasted_iota(jnp.int32, (CHUNK, CHUNK), 0)
                  >= lax.broadcasted_iota(jnp.int32, (CHUNK, CHUNK), 1))
        dacs = jnp.zeros((CHUNK, LANES), F32)
        d_b = jnp.zeros((CHUNK, LANES), F32)
        d_c = jnp.zeros((CHUNK, LANES), F32)
        for j in range(SSD_PAIRS):
            g = j // 4
            sl = slice(j * LANES, (j + 1) * LANES)
            if j % 4 == 0:
                bg = jnp.where(lane // HEAD == g, b_ref[...], 0.0)
                cg = jnp.where(lane // HEAD == g, c_ref[...], 0.0)
                cb = _dot_nt(cg, bg)
                dcb = jnp.zeros((CHUNK, CHUNK), F32)
            x = x_ref[:, sl]
            d = dt_x[:, sl]
            a = acsx_s[:, sl]
            at = acsx_s[CHUNK - 1:CHUNK, sl]
            xdt = x * d
            hin = hin_ref[0, j]
            dhout = dh_s[j]
            dyp = dy_ref[:, sl]
            ea, eat, ed = jnp.exp(a), jnp.exp(at), jnp.exp(at - a)
            da_l = dyp * ea * _dot_nn(cg, hin)
            dm = dyp * ea
            d_c = d_c + _dot_nt(dm, hin)
            dh_s[j] = _dot_tn(cg, dm) + eat * dhout
            dat = jnp.sum(dhout * hin * eat, axis=0, keepdims=True)
            d_b = d_b + _dot_nt(xdt * ed, dhout)
            dw = _dot_nn(bg, dhout)
            dxdt = dw * ed
            ded = dw * xdt * ed
            dat = dat + jnp.sum(ded, axis=0, keepdims=True)
            da_l = da_l - ded
            for hh in (0, 1):
                h = 2 * j + hh
                dec = jnp.exp(jnp.minimum(acs_b[:, h * LANES:(h + 1) * LANES] - acst_s[pl.ds(h, 1), :], 0.0))
                gm = jnp.where(causal, cb * dec, 0.0)
                dyh = jnp.where(lane // HEAD == hh, dyp, 0.0)
                dg = _dot_nt(dyh, xdt)
                dxdt = dxdt + _dot_tn(gm, dyh)
                dcb = dcb + jnp.where(causal, dg * dec, 0.0)
                th = dg * gm
                oh = (lane == h).astype(BF16)
                dacs = dacs + _dot2(th, oh) - _dot2_tn(th, oh)
            if j % 4 == 3:
                d_c = d_c + _dot_nn(dcb, bg)
                d_b = d_b + _dot_tn(dcb, cg)
            dxc_ref[:, sl] = dyp * dsk_x[:, sl] + dxdt * d
            ddx_s[:, sl] = dxdt * x
            dax_s[:, sl] = da_l + jnp.where(rowi == CHUNK - 1, dat, 0.0)
            dskp = jnp.sum(dyp * x, axis=0, keepdims=True)
            ddsk_ref[...] += _dot2(jnp.broadcast_to(dskp, (8, LANES)), e[:, sl], _NT)
        dxc_ref[:, 8 * LANES:9 * LANES] = d_b
        dxc_ref[:, 9 * LANES:10 * LANES] = d_c
        dacs = dacs + _dot2(dax_s[...], e, _NT)
        ddt = _dot2(ddx_s[...], e, _NT)
        dda = _mdot3(triu, dacs)
        ddt = ddt + dda * arow
        dalog_ref[...] += jnp.sum(dda * dt, axis=0, keepdims=True) * arow
        ddtr = jnp.where(lane < SSD_HEADS, ddt * _sigmoid(dtr + dtb), 0.0)
        ddtr_ref[...] = ddtr
        ddtb_ref[...] += jnp.sum(ddtr, axis=0, keepdims=True)

    one = pl.BlockSpec((1, LANES), lambda i: (0, 0))
    rev = lambda c: (lambda i: (nc - 1 - i, c))
    return pl.pallas_call(
        body, name=name, grid=(nc,),
        in_specs=[pl.BlockSpec((CHUNK, 8 * LANES), rev(0)), pl.BlockSpec((CHUNK, LANES), rev(8)),
                  pl.BlockSpec((CHUNK, LANES), rev(9)), pl.BlockSpec((CHUNK, LANES), rev(C_DT // LANES)),
                  one, one, one,
                  pl.BlockSpec((1, SSD_PAIRS, LANES, LANES), lambda i: (nc - 1 - i, 0, 0, 0)),
                  pl.BlockSpec((CHUNK, 8 * LANES), rev(0))],
        out_specs=[pl.BlockSpec((CHUNK, XBC_COLS), rev(0)), pl.BlockSpec((CHUNK, LANES), rev(0)), one, one,
                   pl.BlockSpec((8, LANES), lambda i: (0, 0))],
        out_shape=[jax.ShapeDtypeStruct((t, XBC_COLS), F32), jax.ShapeDtypeStruct((t, LANES), F32)]
        + [jax.ShapeDtypeStruct((1, LANES), F32)] * 2 + [jax.ShapeDtypeStruct((8, LANES), F32)],
        scratch_shapes=[pltpu.VMEM((CHUNK, 8 * LANES), F32), pltpu.VMEM((LANES, CHUNK), F32),
                        pltpu.VMEM((SSD_PAIRS, LANES, LANES), F32),
                        pltpu.VMEM((CHUNK, 8 * LANES), F32), pltpu.VMEM((CHUNK, 8 * LANES), F32)],
        compiler_params=_params(("arbitrary",)),
    )(xc, xc, xc, proj, dtb, alog, dsk, hin_all, dy)


RW_LW = 128
RW_PAIRS = 4 * LANES // RW_LW
RW_BT = 16
RW_DECAY_ROW = 1
RW_BWD_PAIRS = 4


def _rw_consts():
    seg = _seg_matrix(RW_LW)
    ti = (lax.broadcasted_iota(jnp.int32, (HEAD, RW_LW), 0)
          == lax.broadcasted_iota(jnp.int32, (HEAD, RW_LW), 1) % HEAD)
    return seg, ti


def _col_tiles(rows, ti, seg):
    tib = ti.astype(BF16)
    n = len(rows)
    hi = [r.astype(BF16) for r in rows]
    w_lo = (rows[RW_DECAY_ROW] - hi[RW_DECAY_ROW].astype(F32)).astype(BF16)
    out = lax.dot_general(jnp.concatenate([tib * h for h in hi + [w_lo]], axis=0), seg, _NN, preferred_element_type=F32)
    tiles = [out[i * HEAD:(i + 1) * HEAD] for i in range(n)]
    tiles[RW_DECAY_ROW] = tiles[RW_DECAY_ROW] + out[n * HEAD:(n + 1) * HEAD]
    return tiles


def _col_tiles2(rows, ti, seg):
    tib = ti.astype(BF16)
    hi = [r.astype(BF16) for r in rows]
    lo = [(r - h.astype(F32)).astype(BF16) for r, h in zip(rows, hi)]
    out = (lax.dot_general(jnp.concatenate([tib * h for h in hi], axis=0), seg, _NN, preferred_element_type=F32)
           + lax.dot_general(jnp.concatenate([tib * l for l in lo], axis=0), seg, _NN, preferred_element_type=F32))
    return [out[i * HEAD:(i + 1) * HEAD] for i in range(len(rows))]


def _head_lane_sums(tiles, ti, seg):
    out = _dot2(jnp.concatenate(tiles, axis=0), seg)
    return [jnp.sum(jnp.where(ti, out[i * HEAD:(i + 1) * HEAD], 0.0), axis=0, keepdims=True) for i in range(len(tiles))]


def _rw_scan_fwd(mixed, w, k, n, b, *, name):
    t = w.shape[0]

    def body(r_ref, v_ref, w_ref, k_ref, n_ref, b_ref, y_ref, st_ref, s_s):
        @pl.when(pl.program_id(0) == 0)
        def _():
            s_s[...] = jnp.zeros_like(s_s)

        seg, ti = _rw_consts()

        def step(tt, state):
            row = pl.ds(tt, 1)
            new = []
            for p in range(RW_PAIRS):
                sl = pl.ds(p * RW_LW, RW_LW)
                s = state[p]
                ncol, wcol, bcol, kcol, rcol = _col_tiles(
                    [x[row, sl] for x in (n_ref, w_ref, b_ref, k_ref, r_ref)], ti, seg)
                sa = jnp.sum(s * ncol, axis=0, keepdims=True)
                s = s * wcol + bcol * sa + kcol * v_ref[row, sl]
                y_ref[row, sl] = jnp.sum(s * rcol, axis=0, keepdims=True)
                st_ref[tt, p] = s
                new.append(s)
            return tuple(new)

        out = tuple(s_s[p] for p in range(RW_PAIRS))
        for tt in range(RW_BT):
            out = step(tt, out)
        for p in range(RW_PAIRS):
            s_s[p] = out[p]

    blk = lambda c: pl.BlockSpec((RW_BT, 4 * LANES), functools.partial(lambda i, c: (i, c), c=c))
    return pl.pallas_call(
        body, name=name, grid=(t // RW_BT,),
        in_specs=[blk(0), blk(2), blk(0), blk(0), blk(0), blk(0)],
        out_specs=[blk(0), pl.BlockSpec((RW_BT, RW_PAIRS, HEAD, RW_LW), lambda i: (i, 0, 0, 0))],
        out_shape=[jax.ShapeDtypeStruct((t, 4 * LANES), F32),
                   jax.ShapeDtypeStruct((t, RW_PAIRS, HEAD, RW_LW), F32)],
        scratch_shapes=[pltpu.VMEM((RW_PAIRS, HEAD, RW_LW), F32)],
        compiler_params=_params(("arbitrary",)),
    )(mixed, mixed, w, k, n, b)


def _rw_scan_bwd(mixed, w, k, n, b, states, dy, dr0, dk0, dv0, *, name):
    t = w.shape[0]
    nb = t // RW_BT
    ppc = RW_BWD_PAIRS
    ng = RW_PAIRS // ppc

    def body(r_ref, v_ref, w_ref, k_ref, n_ref, b_ref, st_ref, prev_ref, dy_ref, dr0_ref, dk0_ref, dv0_ref,
             dr_ref, dw_ref, dk_ref, dv_ref, dn_ref, db_ref, ds_s):
        @pl.when(pl.program_id(1) == 0)
        def _():
            ds_s[...] = jnp.zeros_like(ds_s)

        seg, ti = _rw_consts()
        has_prev = (pl.program_id(1) < nb - 1).astype(F32)

        def step(it, carry):
            tt = RW_BT - 1 - it
            row = pl.ds(tt, 1)
            prev_t = max(tt - 1, 0)
            new_ds, new_s = [], []
            for p in range(ppc):
                sl = pl.ds(p * RW_LW, RW_LW)
                ds, s_t = carry[p], carry[ppc + p]
                s_p = st_ref[prev_t, p] if tt > 0 else prev_ref[0, p] * has_prev
                ncol, wcol, bcol, kcol, rcol = _col_tiles2(
                    [x[row, sl] for x in (n_ref, w_ref, b_ref, k_ref, r_ref)], ti, seg)
                vv, dyy = v_ref[row, sl], dy_ref[row, sl]
                sa = jnp.sum(s_p * ncol, axis=0, keepdims=True)
                ds = ds + rcol * dyy
                dsa = jnp.sum(ds * bcol, axis=0, keepdims=True)
                dv_ref[row, sl] = jnp.sum(ds * kcol, axis=0, keepdims=True) + dv0_ref[row, sl]
                dr, dw, db, dk, dn = _head_lane_sums([s_t * dyy, ds * s_p, ds * sa, ds * vv, s_p * dsa], ti, seg)
                dr_ref[row, sl] = dr + dr0_ref[row, sl]
                dw_ref[row, sl] = dw
                db_ref[row, sl] = db
                dk_ref[row, sl] = dk + dk0_ref[row, sl]
                dn_ref[row, sl] = dn
                new_ds.append(ds * wcol + ncol * dsa)
                new_s.append(s_p)
            return tuple(new_ds) + tuple(new_s)

        init = tuple(ds_s[p] for p in range(ppc)) + tuple(st_ref[RW_BT - 1, p] for p in range(ppc))
        out = init
        for it in range(RW_BT):
            out = step(it, out)
        for p in range(ppc):
            ds_s[p] = out[p]

    blk = lambda c: pl.BlockSpec((RW_BT, ppc * RW_LW), functools.partial(lambda g, i, c: (nb - 1 - i, c * ng + g), c=c))
    st_spec = pl.BlockSpec((RW_BT, ppc, HEAD, RW_LW), lambda g, i: (nb - 1 - i, g, 0, 0))
    prev_spec = pl.BlockSpec((1, ppc, HEAD, RW_LW), lambda g, i: (jnp.maximum((nb - 1 - i) * RW_BT - 1, 0), g, 0, 0))
    return pl.pallas_call(
        body, name=name, grid=(ng, nb),
        in_specs=[blk(0), blk(2), blk(0), blk(0), blk(0), blk(0), st_spec, prev_spec, blk(0), blk(0), blk(0), blk(0)],
        out_specs=[blk(0)] * 6,
        out_shape=[jax.ShapeDtypeStruct((t, 4 * LANES), F32)] * 6,
        scratch_shapes=[pltpu.VMEM((ppc, HEAD, RW_LW), F32)],
        compiler_params=_params(("parallel", "arbitrary")),
    )(mixed, mixed, w, k, n, b, states, states, dy, dr0, dk0, dv0)


RW_C = 64


def _p3(a, b, dn):
    ah, al = _split2(a)
    bh, bl = _split2(b)
    d = lambda x, y: lax.dot_general(x, y, dn, preferred_element_type=F32)
    return d(ah, bh) + d(ah, bl) + d(al, bh)


_BNN = (((2,), (1,)), ((0,), (0,)))
_BNT = (((2,), (2,)), ((0,), (0,)))
_BTN = (((1,), (1,)), ((0,), (0,)))


@jax.custom_vjp
def _pnn(a, b):
    return _p3(a, b, _BNN)


@jax.custom_vjp
def _pnt(a, b):
    return _p3(a, b, _BNT)


@jax.custom_vjp
def _ptn(a, b):
    return _p3(a, b, _BTN)


_pnn.defvjp(lambda a, b: (_p3(a, b, _BNN), (a, b)), lambda res, g: (_p3(g, res[1], _BNT), _p3(res[0], g, _BTN)))
_pnt.defvjp(lambda a, b: (_p3(a, b, _BNT), (a, b)), lambda res, g: (_p3(g, res[1], _BNN), _p3(g, res[0], _BTN)))
_ptn.defvjp(lambda a, b: (_p3(a, b, _BTN), (a, b)), lambda res, g: (_p3(res[1], g, _BNT), _p3(res[0], g, _BNN)))


def _rw_chunk_consts():
    c2 = 2 * RW_C
    row = lax.broadcasted_iota(jnp.int32, (c2, c2), 0)
    col = lax.broadcasted_iota(jnp.int32, (c2, c2), 1)
    same = (row // RW_C) == (col // RW_C)
    strict = (same & (row > col)).astype(F32)
    incl = (same & (row >= col)).astype(F32)
    eye = (row == col).astype(F32)
    tr = lax.broadcasted_iota(jnp.int32, (RW_C, RW_C), 0)
    tc = lax.broadcasted_iota(jnp.int32, (RW_C, RW_C), 1)
    tril = (tr >= tc).astype(F32)
    lane = lax.broadcasted_iota(jnp.int32, (1, LANES), 1)
    hm = [(lane // HEAD == h).astype(F32) for h in (0, 1)]
    return strict, incl, eye, tril, hm


def _rw_chunk(r, lw, k, v, n, b, s2, consts):
    strict, incl, eye, tril, hm = consts
    two = lambda x: jnp.concatenate([x * hm[0], x * hm[1]], axis=1)
    cum = _pnn(jnp.broadcast_to(tril, (4, RW_C, RW_C)), lw)
    grow, shrink = jnp.exp(-cum), jnp.exp(cum)
    n2, r2 = two(n * jnp.exp(cum - lw)), two(r * shrink)
    b2, k2, v2 = two(b * grow), two(k * grow), two(v)
    p = _pnt(n2, b2) * strict
    x2 = _pnt(n2, s2) + _pnn(_pnt(n2, k2) * strict, v2)
    t_inv, a = eye + p, p
    for _ in range(RW_C.bit_length() - 2):
        a = _pnn(a, a)
        t_inv = t_inv + _pnn(t_inv, a)
    u2 = _pnn(t_inv, x2)
    y2 = _pnt(r2, s2) + _pnn(_pnt(r2, b2) * incl, u2) + _pnn(_pnt(r2, k2) * incl, v2)
    s2_new = (s2 + _ptn(u2, b2) + _ptn(v2, k2)) * jnp.exp(jnp.sum(lw, axis=1, keepdims=True))
    return jnp.sum(y2.reshape(4, 2, RW_C, LANES), axis=1), s2_new


def _pairs(ref):
    return jnp.stack([ref[:, p * LANES:(p + 1) * LANES] for p in range(4)])


def _rw_chunk_fwd(mixed, lw, k, n, b, *, name):
    t = lw.shape[0]
    nc = t // RW_C

    def body(r_ref, v_ref, lw_ref, k_ref, n_ref, b_ref, y_ref, sin_ref, s_s):
        @pl.when(pl.program_id(0) == 0)
        def _():
            s_s[...] = jnp.zeros_like(s_s)

        s2 = s_s[...]
        sin_ref[0] = s2
        y, s2 = _rw_chunk(*[_pairs(x) for x in (r_ref, lw_ref, k_ref, v_ref, n_ref, b_ref)], s2, _rw_chunk_consts())
        for p in range(4):
            y_ref[:, p * LANES:(p + 1) * LANES] = y[p]
        s_s[...] = s2

    blk = lambda c: pl.BlockSpec((RW_C, 4 * LANES), functools.partial(lambda i, c: (i, c), c=c))
    return pl.pallas_call(
        body, name=name, grid=(nc,),
        in_specs=[blk(0), blk(2), blk(0), blk(0), blk(0), blk(0)],
        out_specs=[blk(0), pl.BlockSpec((1, 4, LANES, LANES), lambda i: (i, 0, 0, 0))],
        out_shape=[jax.ShapeDtypeStruct((t, 4 * LANES), F32), jax.ShapeDtypeStruct((nc, 4, LANES, LANES), F32)],
        scratch_shapes=[pltpu.VMEM((4, LANES, LANES), F32)],
        compiler_params=_params(("arbitrary",)),
    )(mixed, mixed, lw, k, n, b)


def _rw_chunk_bwd(mixed, lw, k, n, b, s_in, dy, dr0, dk0, dv0, *, name):
    t = lw.shape[0]
    nc = t // RW_C

    def body(r_ref, v_ref, lw_ref, k_ref, n_ref, b_ref, sin_ref, dy_ref, dr0_ref, dk0_ref, dv0_ref,
             dr_ref, dlw_ref, dk_ref, dv_ref, dn_ref, db_ref, ds_s):
        @pl.when(pl.program_id(0) == 0)
        def _():
            ds_s[...] = jnp.zeros_like(ds_s)

        consts = _rw_chunk_consts()
        args = [_pairs(x) for x in (r_ref, lw_ref, k_ref, v_ref, n_ref, b_ref)] + [sin_ref[0]]
        _, vjp = jax.vjp(lambda *a: _rw_chunk(*a, consts), *args)
        dr, dlw, dk, dv, dn, db, ds = vjp((_pairs(dy_ref), ds_s[...]))
        for p in range(4):
            sl = slice(p * LANES, (p + 1) * LANES)
            dr_ref[:, sl] = dr[p] + dr0_ref[:, sl]
            dlw_ref[:, sl] = dlw[p]
            dk_ref[:, sl] = dk[p] + dk0_ref[:, sl]
            dv_ref[:, sl] = dv[p] + dv0_ref[:, sl]
            dn_ref[:, sl] = dn[p]
            db_ref[:, sl] = db[p]
        ds_s[...] = ds

    blk = lambda c: pl.BlockSpec((RW_C, 4 * LANES), functools.partial(lambda i, c: (nc - 1 - i, c), c=c))
    return pl.pallas_call(
        body, name=name, grid=(nc,),
        in_specs=[blk(0), blk(2), blk(0), blk(0), blk(0), blk(0),
                  pl.BlockSpec((1, 4, LANES, LANES), lambda i: (nc - 1 - i, 0, 0, 0)), blk(0), blk(0), blk(0), blk(0)],
        out_specs=[blk(0)] * 6,
        out_shape=[jax.ShapeDtypeStruct((t, 4 * LANES), F32)] * 6,
        scratch_shapes=[pltpu.VMEM((4, LANES, LANES), F32)],
        compiler_params=_params(("arbitrary",)),
    )(mixed, mixed, lw, k, n, b, s_in, dy, dr0, dk0, dv0)


def _f_rms_res(x, g):
    return _f_rms(x, g)[0], x


def _final(x, g, target, *, bt, name):
    t, d = x.shape

    def body(x_ref, g_ref, t_ref, dx_ref, loss_ref, dg_ref):
        tgt = t_ref[...]

        def f(xv, gv):
            err = _f_rms(xv, gv)[0] - tgt
            return 0.5 * jnp.mean(err * err, axis=-1, keepdims=True)

        row_loss, vjp = jax.vjp(f, x_ref[...], g_ref[...])
        dx, dg = vjp(jnp.ones_like(row_loss))
        dx_ref[...] = dx

        @pl.when(pl.program_id(0) == 0)
        def _():
            loss_ref[...] = jnp.zeros_like(loss_ref)
            dg_ref[...] = jnp.zeros_like(dg_ref)

        loss_ref[...] += jnp.broadcast_to(jnp.sum(row_loss, axis=0, keepdims=True), (1, LANES))
        dg_ref[...] += dg

    blk = pl.BlockSpec((bt, d), lambda i: (i, 0))
    return pl.pallas_call(
        body, name=name, grid=(t // bt,),
        in_specs=[blk, pl.BlockSpec((1, d), lambda i: (0, 0)), blk],
        out_specs=[blk, pl.BlockSpec((1, LANES), lambda i: (0, 0)), pl.BlockSpec((1, d), lambda i: (0, 0))],
        out_shape=[jax.ShapeDtypeStruct((t, d), F32), jax.ShapeDtypeStruct((1, LANES), F32),
                   jax.ShapeDtypeStruct((1, d), F32)],
        compiler_params=_params(("arbitrary",)),
    )(x, g, target)


ADAMW_BLOCK_BYTES = 1 << 20


def _adamw(w, g, m, v, *, name):
    shape = w.shape
    c = shape[-1]
    args = [a.reshape(-1, c) for a in (w, g, m, v)]
    r = args[0].shape[0]
    br = r
    if r * c * 4 > ADAMW_BLOCK_BYTES:
        cands = [b for b in range(8, r, 8) if r % b == 0 and b * c * 4 <= ADAMW_BLOCK_BYTES]
        br = max(cands) if cands else r

    def body(w_ref, g_ref, m_ref, v_ref, d_ref, nm_ref, nv_ref):
        gv = g_ref[...]
        m_new = ADAM_B1 * m_ref[...] + (1.0 - ADAM_B1) * gv
        v_new = ADAM_B2 * v_ref[...] + (1.0 - ADAM_B2) * (gv * gv)
        m_hat = m_new / (1.0 - ADAM_B1 ** ADAM_STEP)
        v_hat = v_new / (1.0 - ADAM_B2 ** ADAM_STEP)
        d_ref[...] = -ADAM_LR * (m_hat / (jnp.sqrt(v_hat) + ADAM_EPS) + ADAM_WD * w_ref[...])
        nm_ref[...] = m_new
        nv_ref[...] = v_new

    blk = pl.BlockSpec((br, c), lambda i: (i, 0))
    outs = pl.pallas_call(
        body, name=name, grid=(r // br,), in_specs=[blk] * 4, out_specs=[blk] * 3,
        out_shape=[jax.ShapeDtypeStruct((r, c), F32)] * 3,
        compiler_params=_params(("parallel",)),
    )(*args)
    return tuple(o.reshape(shape) for o in outs)


BT = 256
BC = 128


def _layer_rows(x, proj, s):
    s = {k: s.get(k) for k in ("y_sb_raw", "y_ssd_raw", "mixed", "ys", "k2", "p_sb", "p_ssd", "p_rw")}
    return dict(
        rms=[(x, D_MODEL, 0)],
        sb_gate=[(s["y_sb_raw"], 512, 0), (proj, 512, 3)],
        ssd_norm=[(s["y_ssd_raw"], 1024, 0), (proj, 1024, C_Z // 1024)],
        rw_pre=[(s["mixed"], 512, 1), (s["mixed"], LANES, 16)],
        rw_post=[(s["ys"], 512, 0), (s["mixed"], 512, 0), (s["k2"], 512, 0), (s["mixed"], 512, 2), (s["mixed"], 512, 3)],
        merge=[(s["p_sb"], 1024, 0), (s["p_ssd"], 1024, 0), (s["p_rw"], 1024, 0),
               (proj, 1024, 3), (proj, 1024, 4), (proj, 1024, 5)],
    )


def _layer_fwd(x, p, nm):
    s = {}
    (s["h"],) = _rowwise(_f_rms, [(x, D_MODEL, 0)], [p["norm_g"]], [D_MODEL], bt=BT, name=nm + "rms")
    proj = s["proj"] = _mm(s["h"], p["w_in"], name=nm + "proj")
    s["y_sb_raw"], s["lt"] = _sb2_fwd(proj, name=nm + "sb")
    s["xc"] = _colwise(_f_conv, proj, C_XBC, XBC_COLS, p["conv"], bc=BC, name=nm + "conv")
    s["y_ssd_raw"], s["hin"] = _ssd_fwd(s["xc"], proj, p["dt_bias"], p["a_log"], p["d_skip"], name=nm + "ssd")
    s["mixed"] = _colwise(_f_rw_mix, proj, C_RW, RW_COLS, [p["rw_mu"]], bc=BC, name=nm + "mix")
    s["w"], s["k2"], s["n"], s["b"] = _rowwise(_f_rw_pre, [(s["mixed"], 512, 1), (s["mixed"], LANES, 16)], p["rw_pre"],
                                               [512] * 4, bt=BT, name=nm + "rwpre")
    s["ys"], s["st"] = _rw_chunk_fwd(s["mixed"], s["w"], s["k2"], s["n"], s["b"], name=nm + "scan")
    rows = _layer_rows(x, proj, s)
    (s["y_sb"],) = _rowwise(_f_sb_gate, rows["sb_gate"], [], [512], bt=BT, name=nm + "sbgate")
    (s["y_ssd"],) = _rowwise(_f_ssd_norm, rows["ssd_norm"], [p["ssd_norm_g"]], [1024], bt=BT, name=nm + "ssdnorm")
    (s["y_rw"],) = _rowwise(_f_rw_post, rows["rw_post"], p["rw_post"], [512], bt=BT, name=nm + "rwpost")
    s["p_sb"] = _mm(s["y_sb"], p["w_out_sb"], name=nm + "osb")
    s["p_ssd"] = _mm(s["y_ssd"], p["w_out_ssd"], name=nm + "ossd")
    s["p_rw"] = _mm(s["y_rw"], p["w_out_rw"], name=nm + "orw")
    (s["merged"],) = _rowwise(_f_merge, _layer_rows(x, proj, s)["merge"], [], [1024], bt=BT, name=nm + "merge")
    return _mm(s["merged"], p["w_o"], add=x, name=nm + "wo"), s


def _layer_bwd(x, dx_out, p, s, nm):
    g = {}
    proj = s["proj"]
    rows = _layer_rows(x, proj, s)
    g["w_o"] = _mm(s["merged"], dx_out, ta=True, name=nm + "g_wo")
    d_merged = _mm(dx_out, p["w_o"], tb=True, name=nm + "d_merged")
    dp_sb, dp_ssd, dp_rw, d_gates = _rowwise_bwd(_f_merge, rows["merge"], [], [(d_merged, 1024, 0)], bt=BT,
                                                 name=nm + "merge_b", groups=[[0], [1], [2], [3, 4, 5]])
    g["w_out_sb"] = _mm(s["y_sb"], dp_sb, ta=True, name=nm + "g_osb")
    g["w_out_ssd"] = _mm(s["y_ssd"], dp_ssd, ta=True, name=nm + "g_ossd")
    g["w_out_rw"] = _mm(s["y_rw"], dp_rw, ta=True, name=nm + "g_orw")
    dy_sb = _mm(dp_sb, p["w_out_sb"], tb=True, name=nm + "d_ysb")
    dy_ssd = _mm(dp_ssd, p["w_out_ssd"], tb=True, name=nm + "d_yssd")
    dy_rw = _mm(dp_rw, p["w_out_rw"], tb=True, name=nm + "d_yrw")
    dy_sb_raw, d_sbgate = _rowwise_bwd(_f_sb_gate, rows["sb_gate"], [], [(dy_sb, 512, 0)], bt=BT, name=nm + "sbgate_b")
    dq, dk, dv = _sb2_bwd(proj, dy_sb_raw, s["lt"], name=nm + "sb_b")
    dy_ssd_raw, dz, g["ssd_norm_g"] = _rowwise_bwd(_f_ssd_norm, rows["ssd_norm"], [p["ssd_norm_g"]],
                                                   [(dy_ssd, 1024, 0)], bt=BT, name=nm + "ssdnorm_b")
    dxc, ddtr, g["dt_bias"], g["a_log"], g["d_skip"] = _ssd_bwd(
        s["xc"], proj, p["dt_bias"], p["a_log"], p["d_skip"], s["hin"], dy_ssd_raw, name=nm + "ssd_b")
    conv_out = _colwise_bwd(_f_conv, proj, C_XBC, XBC_COLS, p["conv"], dxc, bc=BC, name=nm + "conv_b")
    dxbc, g["conv"] = conv_out[0], conv_out[1:]
    dys, dr0, dk0, dv0, d_rwgate, g["rw_ln_g"], g["rw_ln_b"], g["rw_r_k"] = _rowwise_bwd(
        _f_rw_post, rows["rw_post"], p["rw_post"], [(dy_rw, 512, 0)], bt=BT, name=nm + "rwpost_b")
    dr, dw, dk2, dvv, dn, db = _rw_chunk_bwd(s["mixed"], s["w"], s["k2"], s["n"], s["b"], s["st"], dys, dr0, dk0, dv0,
                                            name=nm + "scan_b")
    pre_out = _rowwise_bwd(_f_rw_pre, rows["rw_pre"], p["rw_pre"],
                           [(dw, 512, 0), (dk2, 512, 0), (dn, 512, 0), (db, 512, 0)], bt=BT, name=nm + "rwpre_b")
    dkm, dlo, g["rw_pre"] = pre_out[0], pre_out[1], pre_out[2:]
    d_mixed = jnp.concatenate([dr, dkm, dvv, d_rwgate, dlo], axis=1)
    d_slab, g["rw_mu"] = _colwise_bwd(_f_rw_mix, proj, C_RW, RW_COLS, [p["rw_mu"]], d_mixed, bc=BC, name=nm + "mix_b")
    d_proj = jnp.concatenate([dq, dk, dv, d_sbgate, dz, d_gates, d_slab, ddtr, dxbc], axis=1)
    g["w_in"] = _mm(s["h"], d_proj, ta=True, name=nm + "g_win")
    dh = _mm(d_proj, p["w_in"], tb=True, tn=1024, tk=512, name=nm + "d_h")
    dx, g["norm_g"] = _rowwise_bwd(_f_rms_res, rows["rms"], [p["norm_g"]], [(dh, D_MODEL, 0), (dx_out, D_MODEL, 0)],
                                   bt=BT, name=nm + "rms_b")
    return dx, g


MESH = pl.DeviceIdType.MESH
N_DEV = 8
_ANY = pl.BlockSpec(memory_space=pl.ANY)
_CHIP_SEMS = [pltpu.SemaphoreType.DMA((3,)), pltpu.SemaphoreType.DMA((3,)), pltpu.SemaphoreType.DMA]


def _here():
    x, y, c = lax.axis_index("x"), lax.axis_index("y"), lax.axis_index("c")
    return x, y, c, [(1 - x, y), (x, 1 - y), (1 - x, 1 - y)]


def _chip_exchange(srcs, *, per_dest, name):
    n = len(srcs)

    def body(*refs):
        src_refs, out_refs = refs[:n], refs[n:2 * n]
        send_sems, recv_sems, local_sems = refs[2 * n:]
        x, y, c, chips = _here()
        me = 2 * x + y
        sends, owns = [], []
        for a, (src_ref, out_ref) in enumerate(zip(src_refs, out_refs)):
            pick = (lambda q, s=src_ref: s.at[q]) if per_dest else (lambda q, s=src_ref: s.at[c])
            owns.append(pltpu.make_async_copy(pick(me), out_ref.at[me], local_sems.at[a]))
            owns[-1].start()
            for j, (px, py) in enumerate(chips):
                sends.append(pltpu.make_async_remote_copy(
                    pick(2 * px + py), out_ref.at[me], send_sems.at[3 * a + j], recv_sems.at[3 * a + j],
                    device_id=(px, py, c), device_id_type=MESH))
                sends[-1].start()
        for a, (src_ref, out_ref) in enumerate(zip(src_refs, out_refs)):
            for j, (px, py) in enumerate(chips):
                pltpu.make_async_remote_copy(
                    src_ref.at[0], out_ref.at[2 * px + py], send_sems.at[3 * a + j], recv_sems.at[3 * a + j],
                    device_id=(px, py, c), device_id_type=MESH).wait_recv()
        for cp in sends:
            cp.wait_send()
        for cp in owns:
            cp.wait()

    return pl.pallas_call(
        body, name=name, in_specs=[_ANY] * n, out_specs=[_ANY] * n,
        out_shape=[jax.ShapeDtypeStruct((4,) + s.shape[1:], s.dtype) for s in srcs],
        scratch_shapes=[pltpu.SemaphoreType.DMA((3 * n,)), pltpu.SemaphoreType.DMA((3 * n,)),
                        pltpu.SemaphoreType.DMA((n,))],
    )(*srcs)


def _sibling_swap(srcs, *, other_slot, name):
    n = len(srcs)

    def body(*refs):
        src_refs, out_refs, send_sems, recv_sems = refs[:n], refs[n:2 * n], refs[2 * n], refs[2 * n + 1]
        x, y, c, _ = _here()
        copies = [pltpu.make_async_remote_copy(s.at[1 - c] if other_slot else s, o, send_sems.at[a], recv_sems.at[a],
                                               device_id=(x, y, 1 - c), device_id_type=MESH)
                  for a, (s, o) in enumerate(zip(src_refs, out_refs))]
        for cp in copies:
            cp.start()
        for cp in copies:
            cp.wait()

    return pl.pallas_call(
        body, name=name, in_specs=[_ANY] * n, out_specs=[_ANY] * n,
        out_shape=[jax.ShapeDtypeStruct(s.shape[1:] if other_slot else s.shape, s.dtype) for s in srcs],
        scratch_shapes=[pltpu.SemaphoreType.DMA((n,)), pltpu.SemaphoreType.DMA((n,))],
    )(*srcs)


def _allgather_small(v, *, reduce, name):
    r = v.shape[0]

    def body(v_ref, out_ref, *rest):
        send_sems, recv_sems, local_sem = rest[-3:]
        x, y, c, chips = _here()
        me, sibling = (x, y, c), (x, y, 1 - c)

        def slot(px, py, pc):
            return out_ref.at[4 * px + 2 * py + pc]

        def copy(k, block, to, src=None):
            return pltpu.make_async_remote_copy(
                src_ref=slot(*block) if src is None else src, dst_ref=slot(*block),
                send_sem=send_sems.at[k], recv_sem=recv_sems.at[k], device_id=to, device_id_type=MESH)

        mine = pltpu.make_async_copy(v_ref, slot(*me), local_sem)
        mine.start()
        first = [copy(0, me, sibling, src=v_ref)]
        first += [copy(1 + j, me, (*chip, c), src=v_ref) for j, chip in enumerate(chips)]
        for cp in first:
            cp.start()
        passed = [copy(4 + j, (*chip, c), sibling) for j, chip in enumerate(chips)]
        for j, chip in enumerate(chips):
            copy(1 + j, (*chip, c), me).wait_recv()
            passed[j].start()
        copy(0, sibling, me).wait_recv()
        for j, chip in enumerate(chips):
            copy(4 + j, (*chip, 1 - c), me).wait_recv()
        for cp in first + passed:
            cp.wait_send()
        mine.wait()
        if reduce:
            total = out_ref[0]
            for d in range(1, N_DEV):
                total = total + out_ref[d]
            rest[0][...] = total

    vm = pl.BlockSpec(memory_space=pltpu.VMEM)
    out_shape = [jax.ShapeDtypeStruct((N_DEV, r, LANES), F32)] + ([jax.ShapeDtypeStruct((r, LANES), F32)] if reduce else [])
    return pl.pallas_call(
        body, name=name, in_specs=[vm], out_specs=[vm] * len(out_shape), out_shape=out_shape,
        scratch_shapes=[pltpu.SemaphoreType.DMA((7,)), pltpu.SemaphoreType.DMA((7,)), pltpu.SemaphoreType.DMA],
        compiler_params=pltpu.CompilerParams(vmem_limit_bytes=VMEM_LIMIT),
    )(v)


REDUCE_BLOCK_BYTES = 2 << 20


def _reduce_rows(r, c):
    cands = [b for b in range(16, r + 1, 16) if r % b == 0 and b * c * 4 <= REDUCE_BLOCK_BYTES]
    return max(cands)


def _add_halves(mine2, other, c_idx, *, name):
    _, nq, r, c = mine2.shape
    br = _reduce_rows(r, c)

    def body(c_ref, a_ref, b_ref, o_ref):
        o_ref[...] = (a_ref[0] + b_ref[...]).astype(o_ref.dtype)

    blk = pl.BlockSpec((1, br, c), lambda q, i, c_ref: (q, i, 0))
    return pl.pallas_call(
        body, name=name,
        grid_spec=pltpu.PrefetchScalarGridSpec(
            num_scalar_prefetch=1, grid=(nq, r // br),
            in_specs=[pl.BlockSpec((1, 1, br, c), lambda q, i, c_ref: (c_ref[0], q, i, 0)), blk],
            out_specs=blk),
        out_shape=jax.ShapeDtypeStruct((nq, r, c), BF16),
        compiler_params=_params(("parallel", "parallel")),
    )(c_idx, mine2, other)


def _sum_chips(parts, *, name):
    _, r, c = parts.shape
    br = _reduce_rows(r, c)

    def body(p_ref, o_ref):
        total = p_ref[0].astype(F32)
        for q in range(1, 4):
            total = total + p_ref[q].astype(F32)
        o_ref[...] = total

    return pl.pallas_call(
        body, name=name, grid=(r // br,),
        in_specs=[pl.BlockSpec((4, br, c), lambda i: (0, i, 0))],
        out_specs=pl.BlockSpec((br, c), lambda i: (i, 0)),
        out_shape=jax.ShapeDtypeStruct((r, c), F32),
        compiler_params=_params(("parallel",)),
    )(parts)


BIG = ("w_in", "w_out_sb", "w_out_ssd", "w_out_rw", "w_o")
BIG_AXIS = {"w_in": 2, "w_out_sb": 2, "w_out_ssd": 1, "w_out_rw": 2, "w_o": 1}
SMALL_SHARDED = {"conv_w": 320, "rw_w_up": 128, "rw_a_up": 128}
SMALL = ("norm_g", "conv_w", "conv_b", "dt_bias", "a_log", "d_skip", "ssd_norm_g", "rw_mu", "rw_w0", "rw_w_up",
         "rw_a0", "rw_a_up", "rw_k_k", "rw_k_a", "rw_r_k", "rw_ln_g", "rw_ln_b", "final_g")


def _rows_of(a):
    flat = a.reshape(-1)
    pad = (-flat.shape[0]) % LANES
    return jnp.pad(flat, (0, pad)).reshape(-1, LANES)


def _pack_rows(arrays, multiple=8):
    rows = jnp.concatenate([_rows_of(a) for a in arrays], axis=0)
    pad = (-rows.shape[0]) % multiple
    return jnp.pad(rows, ((0, pad), (0, 0)))


def _unpack_rows(rows, shapes):
    out, off = [], 0
    for shp in shapes:
        n = 1
        for d in shp:
            n *= d
        nr = -(-n // LANES)
        out.append(rows[off:off + nr].reshape(-1)[:n].reshape(shp))
        off += nr
    return out


def _pad_cols(w):
    z = jnp.zeros(w.shape[:-1] + (N_PAD - N_IN,), w.dtype)
    return jnp.concatenate([w[..., 0:3072], w[..., 6544:9616], w[..., 4368:6544], w[..., 4352:4368], z,
                            w[..., 3072:4352]], axis=-1)


def _unpad_cols(g):
    return jnp.concatenate([g[..., 0:3072], g[..., 8448:9728], g[..., 8320:8336], g[..., 6144:8320],
                            g[..., 3072:6144]], axis=-1)


def _split_chips(a, axis):
    n = a.shape[axis] // 4
    return jnp.stack([lax.slice_in_dim(a, q * n, (q + 1) * n, axis=axis) for q in range(4)])


def _join_chips(a, axis):
    return jnp.concatenate([a[q] for q in range(4)], axis=axis)


def kernel(x, norm_g, w_in, conv_w, conv_b, dt_bias, a_log, d_skip, ssd_norm_g, rw_mu, rw_w0, rw_w_up, rw_a0, rw_a_up, rw_k_k, rw_k_a, rw_r_k, rw_ln_g, rw_ln_b, w_out_sb, w_out_ssd, w_out_rw, w_o, final_g, loss_target, m_norm_g, m_w_in, m_conv_w, m_conv_b, m_dt_bias, m_a_log, m_d_skip, m_ssd_norm_g, m_rw_mu, m_rw_w0, m_rw_w_up, m_rw_a0, m_rw_a_up, m_rw_k_k, m_rw_k_a, m_rw_r_k, m_rw_ln_g, m_rw_ln_b, m_w_out_sb, m_w_out_ssd, m_w_out_rw, m_w_o, m_final_g, v_norm_g, v_w_in, v_conv_w, v_conv_b, v_dt_bias, v_a_log, v_d_skip, v_ssd_norm_g, v_rw_mu, v_rw_w0, v_rw_w_up, v_rw_a0, v_rw_a_up, v_rw_k_k, v_rw_k_a, v_rw_r_k, v_rw_ln_g, v_rw_ln_b, v_w_out_sb, v_w_out_ssd, v_w_out_rw, v_w_o, v_final_g):
    names = ("norm_g", "w_in", "conv_w", "conv_b", "dt_bias", "a_log", "d_skip", "ssd_norm_g", "rw_mu", "rw_w0",
             "rw_w_up", "rw_a0", "rw_a_up", "rw_k_k", "rw_k_a", "rw_r_k", "rw_ln_g", "rw_ln_b", "w_out_sb",
             "w_out_ssd", "w_out_rw", "w_o", "final_g")
    w_loc = dict(zip(names, (norm_g, w_in, conv_w, conv_b, dt_bias, a_log, d_skip, ssd_norm_g, rw_mu, rw_w0, rw_w_up,
                             rw_a0, rw_a_up, rw_k_k, rw_k_a, rw_r_k, rw_ln_g, rw_ln_b, w_out_sb, w_out_ssd, w_out_rw,
                             w_o, final_g)))
    m_loc = dict(zip(names, (m_norm_g, m_w_in, m_conv_w, m_conv_b, m_dt_bias, m_a_log, m_d_skip, m_ssd_norm_g,
                             m_rw_mu, m_rw_w0, m_rw_w_up, m_rw_a0, m_rw_a_up, m_rw_k_k, m_rw_k_a, m_rw_r_k,
                             m_rw_ln_g, m_rw_ln_b, m_w_out_sb, m_w_out_ssd, m_w_out_rw, m_w_o, m_final_g)))
    v_loc = dict(zip(names, (v_norm_g, v_w_in, v_conv_w, v_conv_b, v_dt_bias, v_a_log, v_d_skip, v_ssd_norm_g,
                             v_rw_mu, v_rw_w0, v_rw_w_up, v_rw_a0, v_rw_a_up, v_rw_k_k, v_rw_k_a, v_rw_r_k,
                             v_rw_ln_g, v_rw_ln_b, v_w_out_sb, v_w_out_ssd, v_w_out_rw, v_w_o, v_final_g)))
    chip = 2 * lax.axis_index("x") + lax.axis_index("y")
    core = lax.axis_index("c")

    got_mine = _chip_exchange([w_loc[n].astype(BF16) for n in BIG], per_dest=False, name="gather_big")
    got_theirs = _sibling_swap(got_mine, other_slot=False, name="gather_join")
    full = {}
    for n, mine, theirs in zip(BIG, got_mine, got_theirs):
        layers = [jnp.where(core == l, mine, theirs) for l in range(DEPTH)]
        full[n] = jnp.stack([jnp.concatenate([lay[q] for q in range(4)], axis=BIG_AXIS[n] - 1) for lay in layers])
    full["w_in"] = _pad_cols(full["w_in"])
    sm_names = tuple(SMALL_SHARDED)
    sm_shapes = [w_loc[n].shape for n in sm_names]
    (got_sm,) = _allgather_small(_pack_rows([w_loc[n] for n in sm_names]), reduce=False, name="gather_small")
    per_chip = [_unpack_rows(got_sm[4 * (q // 2) + 2 * (q % 2)], sm_shapes) for q in range(4)]
    for i, n in enumerate(sm_names):
        full[n] = jnp.concatenate([per_chip[q][i] for q in range(4)], axis=-1)

    def pad16(a):
        return jnp.zeros((1, LANES), F32).at[0, :SSD_HEADS].set(a)

    def layer_params(i):
        row = lambda n: w_loc[n][i].reshape(1, -1)
        cw = full["conv_w"][i]
        return dict(
            norm_g=row("norm_g"), w_in=full["w_in"][i], conv=[cw[k][None] for k in range(4)] + [row("conv_b")],
            dt_bias=pad16(dt_bias[i]), a_log=pad16(a_log[i]), d_skip=pad16(d_skip[i]),
            ssd_norm_g=row("ssd_norm_g"), rw_mu=row("rw_mu"),
            rw_pre=[row("rw_w0"), jnp.zeros((LANES, 512), F32).at[:HEAD].set(full["rw_w_up"][i]), row("rw_a0"),
                    jnp.zeros((LANES, 512), F32).at[HEAD:].set(full["rw_a_up"][i]), row("rw_k_k"), row("rw_k_a")],
            rw_post=[row("rw_ln_g"), row("rw_ln_b"), row("rw_r_k")],
            w_out_sb=full["w_out_sb"][i], w_out_ssd=full["w_out_ssd"][i], w_out_rw=full["w_out_rw"][i],
            w_o=full["w_o"][i])

    params = [layer_params(i) for i in range(DEPTH)]
    xs, saved = [x[0]], []
    for i in range(DEPTH):
        nxt, s = _layer_fwd(xs[-1], params[i], f"l{i}_")
        xs.append(nxt)
        saved.append(s)
    dx, loss_row, g_final = _final(xs[-1], final_g.reshape(1, -1), loss_target[0], bt=BT, name="final")
    grads = [None] * DEPTH
    for i in reversed(range(DEPTH)):
        dx, grads[i] = _layer_bwd(xs[i], dx, params[i], saved[i], f"l{i}_")

    def stacked(fn):
        return jnp.stack([fn(grads[i]) for i in range(DEPTH)])

    g_loc = {
        "norm_g": stacked(lambda g: g["norm_g"][0]),
        "w_in": stacked(lambda g: _unpad_cols(g["w_in"])),
        "conv_w": stacked(lambda g: jnp.concatenate(g["conv"][:4], axis=0)),
        "conv_b": stacked(lambda g: g["conv"][4][0]),
        "dt_bias": stacked(lambda g: g["dt_bias"][0, :SSD_HEADS]),
        "a_log": stacked(lambda g: g["a_log"][0, :SSD_HEADS]),
        "d_skip": stacked(lambda g: g["d_skip"][0, :SSD_HEADS]),
        "ssd_norm_g": stacked(lambda g: g["ssd_norm_g"][0]),
        "rw_mu": stacked(lambda g: g["rw_mu"][0]),
        "rw_w0": stacked(lambda g: g["rw_pre"][0][0]),
        "rw_w_up": stacked(lambda g: g["rw_pre"][1][:HEAD]),
        "rw_a0": stacked(lambda g: g["rw_pre"][2][0]),
        "rw_a_up": stacked(lambda g: g["rw_pre"][3][HEAD:]),
        "rw_k_k": stacked(lambda g: g["rw_pre"][4][0]),
        "rw_k_a": stacked(lambda g: g["rw_pre"][5][0]),
        "rw_r_k": stacked(lambda g: g["rw_r_k"].reshape(8, HEAD)),
        "rw_ln_g": stacked(lambda g: g["rw_ln_g"][0]),
        "rw_ln_b": stacked(lambda g: g["rw_ln_b"][0]),
        "w_out_sb": stacked(lambda g: g["w_out_sb"]),
        "w_out_ssd": stacked(lambda g: g["w_out_ssd"]),
        "w_out_rw": stacked(lambda g: g["w_out_rw"]),
        "w_o": stacked(lambda g: g["w_o"]),
        "final_g": g_final[0],
    }

    sends = [jnp.swapaxes(_split_chips(g_loc[n], BIG_AXIS[n]), 0, 1) for n in BIG]
    others = _sibling_swap(sends, other_slot=True, name="reduce_sibling")
    c_idx = core.reshape(1).astype(jnp.int32)
    parts = [_add_halves(s, o, c_idx, name="reduce_add_" + n) for n, s, o in zip(BIG, sends, others)]
    parts = _chip_exchange(parts, per_dest=True, name="reduce_chips")
    mine = [_sum_chips(p, name="reduce_sum_" + n) for n, p in zip(BIG, parts)]
    theirs = _sibling_swap(mine, other_slot=False, name="reduce_join")
    g_out = {n: jnp.stack([jnp.where(core == 0, a, b), jnp.where(core == 0, b, a)])
             for n, a, b in zip(BIG, mine, theirs)}

    sm_all = SMALL + ("loss",)
    sm_full_shapes = [g_loc[n].shape for n in SMALL] + [(1,)]
    _, summed = _allgather_small(_pack_rows([g_loc[n] for n in SMALL] + [loss_row[0, :1]]), reduce=True, name="reduce_small")
    sm = dict(zip(sm_all, _unpack_rows(summed, sm_full_shapes)))
    for n in SMALL:
        g_out[n] = sm[n]
    for n, wd in SMALL_SHARDED.items():
        g_out[n] = lax.dynamic_slice_in_dim(sm[n], chip * wd, wd, axis=sm[n].ndim - 1)
    loss = sm["loss"][0]

    upd = {n: _adamw(w_loc[n], g_out[n], m_loc[n], v_loc[n], name="adamw_" + n) for n in names}
    return (loss, dx[None], *[g_out[n] for n in names], *[upd[n][0] for n in names],
            *[upd[n][1] for n in names], *[upd[n][2] for n in names])
```

```python
import functools

import jax
import jax.numpy as jnp
from jax import lax
from jax.experimental import pallas as pl
from jax.experimental.pallas import tpu as pltpu

F32 = jnp.float32
BF16 = jnp.bfloat16

D_MODEL = 1024
DEPTH = 2
HEAD = 64
LANES = 128
CHUNK = 128
RMS_EPS = 1e-6
GN_EPS = 64e-5
VMEM_LIMIT = 56 * 1024 * 1024

N_IN = 9616
N_PAD = 9728
C_SB, C_Z, C_GATES, C_RW, C_LO, C_DT, C_XBC = 0, 2048, 3072, 6144, 8192, 8320, 8448
RW_COLS = 2176
XBC_COLS = 1280

ADAM_LR, ADAM_B1, ADAM_B2, ADAM_EPS, ADAM_WD, ADAM_STEP = 0.001, 0.9, 0.999, 1e-08, 0.01, 10


def _params(sem=None):
    return pltpu.CompilerParams(dimension_semantics=sem, vmem_limit_bytes=VMEM_LIMIT)


@jax.custom_vjp
def _sigmoid(x):
    return 1.0 / (1.0 + jnp.exp(-x))


def _sigmoid_fwd(x):
    s = _sigmoid(x)
    return s, s


def _sigmoid_bwd(s, g):
    return (g * s * (1.0 - s),)


_sigmoid.defvjp(_sigmoid_fwd, _sigmoid_bwd)


@jax.custom_vjp
def _silu(x):
    return x * _sigmoid(x)


def _silu_fwd(x):
    s = _sigmoid(x)
    return x * s, (x, s)


def _silu_bwd(res, g):
    x, s = res
    return (g * (s + x * s * (1.0 - s)),)


_silu.defvjp(_silu_fwd, _silu_bwd)


@jax.custom_vjp
def _softplus(x):
    return jnp.maximum(x, 0.0) + jnp.log(1.0 + jnp.exp(-jnp.abs(x)))


def _softplus_fwd(x):
    return _softplus(x), x


def _softplus_bwd(x, g):
    return (g * _sigmoid(x),)


_softplus.defvjp(_softplus_fwd, _softplus_bwd)


def _dot(a, b, dims):
    return lax.dot_general(a.astype(BF16), b.astype(BF16), (dims, ((), ())), preferred_element_type=F32)


def _dot_nn(a, b):
    return _dot(a, b, ((1,), (0,)))


def _dot_nt(a, b):
    return _dot(a, b, ((1,), (1,)))


def _dot_tn(a, b):
    return _dot(a, b, ((0,), (0,)))


@jax.custom_vjp
def _bdot(a, b):
    return _dot_nn(a, b)


def _bdot_fwd(a, b):
    return _dot_nn(a, b), (a, b)


def _bdot_bwd(res, g):
    a, b = res
    return _dot_nt(g, b), _dot_tn(a, g)


_bdot.defvjp(_bdot_fwd, _bdot_bwd)


def _split2(x):
    hi = x.astype(BF16)
    lo = (x - hi.astype(F32)).astype(BF16)
    return hi, lo


_NT = (((1,), (1,)), ((), ()))
_NN = (((1,), (0,)), ((), ()))
_TN = (((0,), (0,)), ((), ()))


def _dot2(x, m, dn=_NN):
    hi, lo = _split2(x)
    return (lax.dot_general(hi, m, dn, preferred_element_type=F32)
            + lax.dot_general(lo, m, dn, preferred_element_type=F32))


def _dot2_tn(x, m):
    return _dot2(x, m, _TN)


def _seg_matrix(n):
    r = lax.broadcasted_iota(jnp.int32, (n, n), 0) // HEAD
    c = lax.broadcasted_iota(jnp.int32, (n, n), 1) // HEAD
    return (r == c).astype(BF16)


@jax.custom_vjp
def _segsum2(x, seg):
    return _dot2(x, seg)


def _segsum2_fwd(x, seg):
    return _dot2(x, seg), seg


def _segsum2_bwd(seg, g):
    return _dot2(g, seg), jnp.zeros_like(seg)


_segsum2.defvjp(_segsum2_fwd, _segsum2_bwd)


def _make_segsum(seg):
    return lambda x: _segsum2(x, seg)


def _shift_down_raw(x, k):
    row = lax.broadcasted_iota(jnp.int32, x.shape, 0)
    return jnp.where(row >= k, pltpu.roll(x, k, 0), 0.0)


def _shift_up_raw(x, k):
    t = x.shape[0]
    row = lax.broadcasted_iota(jnp.int32, x.shape, 0)
    return jnp.where(row < t - k, pltpu.roll(x, t - k, 0), 0.0)


@functools.partial(jax.custom_vjp, nondiff_argnums=(1,))
def _shift_down(x, k):
    return _shift_down_raw(x, k)


def _shift_down_fwd(x, k):
    return _shift_down_raw(x, k), None


def _shift_down_bwd(k, _, g):
    return (_shift_up_raw(g, k),)


_shift_down.defvjp(_shift_down_fwd, _shift_down_bwd)


def _mm(a, b, *, name, ta=False, tb=False, add=None, out_dtype=F32, tm=2048, tn=512, tk=None):
    m, k = (a.shape[1], a.shape[0]) if ta else a.shape
    n = b.shape[0] if tb else b.shape[1]
    tm, tn = min(tm, m), min(tn, n)
    tk = k if tk is None else tk
    nk = k // tk
    assert m % tm == 0 and n % tn == 0 and k % tk == 0
    dims = ((0 if ta else 1,), (1 if tb else 0,))

    def body(a_ref, b_ref, *refs):
        o_ref, acc_ref = refs[-2:]
        p = _dot(a_ref[...], b_ref[...], dims)

        def emit(total):
            if add is not None:
                total = total + refs[0][...]
            o_ref[...] = total.astype(o_ref.dtype)

        if nk == 1:
            emit(p)
        else:
            kk = pl.program_id(2)

            @pl.when(kk == 0)
            def _():
                acc_ref[...] = p

            @pl.when(kk > 0)
            def _():
                acc_ref[...] += p

            @pl.when(kk == nk - 1)
            def _():
                emit(acc_ref[...])

    a_spec = pl.BlockSpec((tk, tm), lambda i, j, kk: (kk, i)) if ta else pl.BlockSpec((tm, tk), lambda i, j, kk: (i, kk))
    b_spec = pl.BlockSpec((tn, tk), lambda i, j, kk: (j, kk)) if tb else pl.BlockSpec((tk, tn), lambda i, j, kk: (kk, j))
    o_spec = pl.BlockSpec((tm, tn), lambda i, j, kk: (i, j))
    return pl.pallas_call(
        body, name=name, grid=(m // tm, n // tn, nk),
        in_specs=[a_spec, b_spec] + ([o_spec] if add is not None else []), out_specs=o_spec,
        out_shape=jax.ShapeDtypeStruct((m, n), out_dtype),
        scratch_shapes=[pltpu.VMEM((tm, tn) if nk > 1 else (8, LANES), F32)],
        compiler_params=_params(("parallel", "parallel", "arbitrary")),
    )(a, b, *([add] if add is not None else []))


def _row_specs(rows, bt):
    return [pl.BlockSpec((bt, w), functools.partial(lambda i, c: (i, c), c=c)) for _, w, c in rows]


def _full_spec(p):
    return pl.BlockSpec(p.shape, functools.partial(lambda i, nd: (0,) * nd, nd=p.ndim))


def _rowwise(f, rows, pars, out_widths, *, bt, name, acc_widths=()):
    t = rows[0][0].shape[0]
    nr, npar, no, na = len(rows), len(pars), len(out_widths), len(acc_widths)

    def body(*refs):
        vals = [r[...] for r in refs[:nr + npar]]
        outs = f(*vals)
        for o_ref, o in zip(refs[nr + npar:nr + npar + no], outs[:no]):
            o_ref[...] = o.astype(o_ref.dtype)
        if na:
            first = pl.program_id(0) == 0
            for a_ref, a in zip(refs[nr + npar + no:], outs[no:]):
                @pl.when(first)
                def _():
                    a_ref[...] = jnp.zeros_like(a_ref)
                a_ref[...] += a

    return pl.pallas_call(
        body, name=name, grid=(t // bt,),
        in_specs=_row_specs(rows, bt) + [_full_spec(p) for p in pars],
        out_specs=[pl.BlockSpec((bt, w), lambda i: (i, 0)) for w in out_widths]
        + [pl.BlockSpec((1, w), lambda i: (0, 0)) for w in acc_widths],
        out_shape=[jax.ShapeDtypeStruct((t, w), F32) for w in out_widths]
        + [jax.ShapeDtypeStruct((1, w), F32) for w in acc_widths],
        compiler_params=_params(("arbitrary",)),
    )(*[r[0] for r in rows], *pars)


def _rowwise_bwd(f, rows, pars, douts, *, bt, name, groups=None):
    t = rows[0][0].shape[0]
    nr, npar, nd = len(rows), len(pars), len(douts)
    groups = [[i] for i in range(nr)] if groups is None else groups
    widths = [r[1] for r in rows]

    def body(*refs):
        vals = [r[...] for r in refs[:nr + npar]]
        cts = tuple(r[...] for r in refs[nr + npar:nr + npar + nd])
        _, vjp = jax.vjp(lambda *a: tuple(f(*a)), *vals)
        grads = vjp(cts)
        out_refs = refs[nr + npar + nd:]
        for g_ref, grp in zip(out_refs[:len(groups)], groups):
            off = 0
            for i in grp:
                g_ref[:, off:off + widths[i]] = grads[i]
                off += widths[i]
        first = pl.program_id(0) == 0
        for p_ref, g in zip(out_refs[len(groups):], grads[nr:]):
            @pl.when(first)
            def _():
                p_ref[...] = jnp.zeros_like(p_ref)
            p_ref[...] += g

    gw = [sum(widths[i] for i in grp) for grp in groups]
    return pl.pallas_call(
        body, name=name, grid=(t // bt,),
        in_specs=_row_specs(rows, bt) + [_full_spec(p) for p in pars] + _row_specs(douts, bt),
        out_specs=[pl.BlockSpec((bt, w), lambda i: (i, 0)) for w in gw] + [_full_spec(p) for p in pars],
        out_shape=[jax.ShapeDtypeStruct((t, w), F32) for w in gw] + [jax.ShapeDtypeStruct(p.shape, F32) for p in pars],
        compiler_params=_params(("arbitrary",)),
    )(*[r[0] for r in rows], *pars, *[d[0] for d in douts])


def _colwise(f, x, c0, ncols, pars, *, bc, name):
    t = x.shape[0]

    def body(x_ref, *refs):
        o_ref = refs[-1]
        o_ref[...] = f(x_ref[...], *[r[...] for r in refs[:-1]])

    return pl.pallas_call(
        body, name=name, grid=(ncols // bc,),
        in_specs=[pl.BlockSpec((t, bc), lambda j: (0, j + c0 // bc))]
        + [pl.BlockSpec((p.shape[0], bc), lambda j: (0, j)) for p in pars],
        out_specs=pl.BlockSpec((t, bc), lambda j: (0, j)),
        out_shape=jax.ShapeDtypeStruct((t, ncols), F32),
        compiler_params=_params(("parallel",)),
    )(x, *pars)


def _colwise_bwd(f, x, c0, ncols, pars, dout, *, bc, name):
    t = x.shape[0]
    npar = len(pars)

    def body(x_ref, *refs):
        vals = [x_ref[...]] + [r[...] for r in refs[:npar]]
        _, vjp = jax.vjp(f, *vals)
        grads = vjp(refs[npar][...])
        for g_ref, g in zip(refs[npar + 1:], grads):
            g_ref[...] = g

    return pl.pallas_call(
        body, name=name, grid=(ncols // bc,),
        in_specs=[pl.BlockSpec((t, bc), lambda j: (0, j + c0 // bc))]
        + [pl.BlockSpec((p.shape[0], bc), lambda j: (0, j)) for p in pars]
        + [pl.BlockSpec((t, bc), lambda j: (0, j))],
        out_specs=[pl.BlockSpec((t, bc), lambda j: (0, j))]
        + [pl.BlockSpec((p.shape[0], bc), lambda j: (0, j)) for p in pars],
        out_shape=[jax.ShapeDtypeStruct((t, ncols), F32)] + [jax.ShapeDtypeStruct(p.shape, F32) for p in pars],
        compiler_params=_params(("parallel",)),
    )(x, *pars, dout)


def _f_rms(x, g):
    return (x * lax.rsqrt(jnp.mean(x * x, axis=-1, keepdims=True) + RMS_EPS) * g,)


def _f_sb_gate(y, gate):
    return (y * _silu(gate),)


def _f_ssd_norm(y, z, g):
    u = y * _silu(z)
    return (u * lax.rsqrt(jnp.mean(u * u, axis=-1, keepdims=True) + RMS_EPS) * g,)


def _f_merge(p_sb, p_ssd, p_rw, g_sb, g_ssd, g_rw):
    return (_sigmoid(g_sb) * p_sb + _sigmoid(g_ssd) * p_ssd + _sigmoid(g_rw) * p_rw,)


def _f_rw_pre(k, lo, w0, w_up, a0, a_up, k_k, k_a):
    segsum = _make_segsum(_seg_matrix(k.shape[1]))
    lane = lax.broadcasted_iota(jnp.int32, lo.shape, 1)
    w_lo = jnp.where(lane < HEAD, jnp.tanh(lo), 0.0)
    a_lo = jnp.where(lane >= HEAD, lo, 0.0)
    w = -_softplus(-(w0 + _bdot(w_lo, w_up))) - 0.5
    log_decay = -jnp.exp(w)
    a = _sigmoid(a0 + _bdot(a_lo, a_up))
    kk = k * k_k
    kk = kk / jnp.maximum(jnp.sqrt(segsum(kk * kk)), 1e-12)
    return log_decay, k * (1.0 + (a - 1.0) * k_a), -kk, kk * a


def _f_rw_post(y, r, k2, v, gate, ln_g, ln_b, r_k):
    segsum = _make_segsum(_seg_matrix(y.shape[1]))
    yc = y - segsum(y) * (1.0 / HEAD)
    var = segsum(yc * yc) * (1.0 / HEAD)
    yn = yc * lax.rsqrt(var + GN_EPS) * ln_g + ln_b
    return ((yn + segsum(r * k2 * r_k) * v) * _silu(gate),)


def _f_rw_mix(slab, mu):
    return slab + (_shift_down(slab, 1) - slab) * mu


def _f_conv(x, w0, w1, w2, w3, b):
    acc = x * w3 + b
    for i, w in enumerate((w0, w1, w2)):
        acc = acc + _shift_down(x, 3 - i) * w
    return _silu(acc)


def _log_sigmoid(z):
    return jnp.minimum(z, 0.0) - jnp.log(1.0 + jnp.exp(-jnp.abs(z)))


def _prefix_matrix(kind):
    j = lax.broadcasted_iota(jnp.int32, (CHUNK, 2 * CHUNK), 0)
    s = lax.broadcasted_iota(jnp.int32, (CHUNK, 2 * CHUNK), 1)
    tri = {"gt": j > s, "le": j <= s, "lt": j < s}[kind]
    return (tri | (s >= CHUNK)).astype(BF16)


def _sb_specs(t):
    q = pl.BlockSpec((CHUNK, LANES), lambda j, i: (i, j))
    k = pl.BlockSpec((t, LANES), lambda j, i: (0, 4 + j))
    v = pl.BlockSpec((t, LANES), lambda j, i: (0, 8 + j))
    return q, k, v


def _sb_fwd(proj, *, name):
    t = proj.shape[0]
    scale = HEAD ** -0.5

    def body(q_ref, k_ref, v_ref, y_ref, lt_ref):
        i = pl.program_id(1)
        lane = lax.broadcasted_iota(jnp.int32, (CHUNK, LANES), 1)
        diff = (lax.broadcasted_iota(jnp.int32, (CHUNK, CHUNK), 1)
                - lax.broadcasted_iota(jnp.int32, (CHUNK, CHUNK), 0))
        m_f = _prefix_matrix("gt")
        q = q_ref[...] * scale
        qh = [jnp.where((lane // HEAD) == h, q, 0.0).astype(BF16) for h in (0, 1)]

        def step(it, carry):
            off = pl.multiple_of((i - it) * CHUNK, CHUNK)
            kblk = k_ref[pl.ds(off, CHUNK), :].astype(BF16)
            vblk = v_ref[pl.ds(off, CHUNK), :].astype(BF16)
            mask = diff < it * CHUNK
            new = []
            for h in (0, 1):
                c, acc = carry[2 * h], carry[2 * h + 1]
                z = lax.dot_general(qh[h], kblk, _NT, preferred_element_type=F32)
                lb = _log_sigmoid(z)
                w2 = _dot2(jnp.where(mask, lb - z, 0.0), m_f)
                att = jnp.where(mask, jnp.exp(lb + c + w2[:, :CHUNK]), 0.0)
                acc = acc + lax.dot_general(att.astype(BF16), vblk, _NN, preferred_element_type=F32)
                new += [c + w2[:, CHUNK:], acc]
            return tuple(new)

        zero = jnp.zeros((CHUNK, LANES), F32)
        c_a, acc_a, c_b, acc_b = lax.fori_loop(0, i + 1, step, (zero, zero, zero, zero))
        y_ref[...] = jnp.where(lane < HEAD, acc_a, acc_b)
        lt_ref[0] = c_a
        lt_ref[1] = c_b

    return pl.pallas_call(
        body, name=name, grid=(4, t // CHUNK),
        in_specs=list(_sb_specs(t)),
        out_specs=[pl.BlockSpec((CHUNK, LANES), lambda j, i: (i, j)),
                   pl.BlockSpec((2, CHUNK, LANES), lambda j, i: (j, i, 0))],
        out_shape=[jax.ShapeDtypeStruct((t, 4 * LANES), F32), jax.ShapeDtypeStruct((8, t, LANES), F32)],
        compiler_params=_params(("parallel", "arbitrary")),
    )(proj, proj, proj)


def _sb_bwd(proj, dy, lt, *, name):
    t = proj.shape[0]
    scale = HEAD ** -0.5

    def body(q_ref, k_ref, v_ref, dy_ref, lt_ref, dq_ref, dk_ref, dv_ref):
        i = pl.program_id(1)

        @pl.when(i == 0)
        def _():
            dk_ref[...] = jnp.zeros_like(dk_ref)
            dv_ref[...] = jnp.zeros_like(dv_ref)

        lane = lax.broadcasted_iota(jnp.int32, (CHUNK, LANES), 1)
        diff = (lax.broadcasted_iota(jnp.int32, (CHUNK, CHUNK), 1)
                - lax.broadcasted_iota(jnp.int32, (CHUNK, CHUNK), 0))
        m_le, m_lt = _prefix_matrix("le"), _prefix_matrix("lt")
        q = q_ref[...] * scale
        dy_blk = dy_ref[...]
        qh = [jnp.where((lane // HEAD) == h, q, 0.0).astype(BF16) for h in (0, 1)]
        doh = [jnp.where((lane // HEAD) == h, dy_blk, 0.0).astype(BF16) for h in (0, 1)]
        lth = [lt_ref[0], lt_ref[1]]

        def step(kb, carry):
            off = pl.multiple_of(kb * CHUNK, CHUNK)
            kblk = k_ref[pl.ds(off, CHUNK), :].astype(BF16)
            vblk = v_ref[pl.ds(off, CHUNK), :].astype(BF16)
            mask = diff < (i - kb) * CHUNK
            new = []
            dk_acc = jnp.zeros((CHUNK, LANES), F32)
            dv_acc = jnp.zeros((CHUNK, LANES), F32)
            for h in (0, 1):
                cp, cg, dq = carry[3 * h:3 * h + 3]
                z = lax.dot_general(qh[h], kblk, _NT, preferred_element_type=F32)
                lb = _log_sigmoid(z)
                w2 = _dot2(jnp.where(mask, lb - z, 0.0), m_le)
                att = jnp.where(mask, jnp.exp(lb + lth[h] - cp - w2[:, :CHUNK]), 0.0)
                d_att = lax.dot_general(doh[h], vblk, _NT, preferred_element_type=F32)
                d_e = d_att * att
                g2 = _dot2(d_e, m_lt)
                sig = jnp.exp(lb)
                dz = jnp.where(mask, d_e * (1.0 - sig) - (cg + g2[:, :CHUNK]) * sig, 0.0).astype(BF16)
                dq = dq + lax.dot_general(dz, kblk, _NN, preferred_element_type=F32)
                dk_acc = dk_acc + lax.dot_general(dz, qh[h], _TN, preferred_element_type=F32)
                dv_acc = dv_acc + lax.dot_general(att.astype(BF16), doh[h], _TN, preferred_element_type=F32)
                new += [cp + w2[:, CHUNK:], cg + g2[:, CHUNK:], dq]
            dk_ref[pl.ds(off, CHUNK), :] += dk_acc
            dv_ref[pl.ds(off, CHUNK), :] += dv_acc
            return tuple(new)

        zero = jnp.zeros((CHUNK, LANES), F32)
        out = lax.fori_loop(0, i + 1, step, (zero,) * 6)
        dq_ref[...] = jnp.where(lane < HEAD, out[2], out[5]) * scale

    q_spec, k_spec, v_spec = _sb_specs(t)
    blk = pl.BlockSpec((CHUNK, LANES), lambda j, i: (i, j))
    col = pl.BlockSpec((t, LANES), lambda j, i: (0, j))
    return pl.pallas_call(
        body, name=name, grid=(4, t // CHUNK),
        in_specs=[q_spec, k_spec, v_spec, blk, pl.BlockSpec((2, CHUNK, LANES), lambda j, i: (j, i, 0))],
        out_specs=[blk, col, col],
        out_shape=[jax.ShapeDtypeStruct((t, 4 * LANES), F32)] * 3,
        compiler_params=_params(("parallel", "arbitrary")),
    )(proj, proj, proj, dy, lt)


SB_BQ = 256
SB_BK = 256


def _tri_ones(kind):
    j = lax.broadcasted_iota(jnp.int32, (SB_BK, SB_BK + LANES), 0)
    s = lax.broadcasted_iota(jnp.int32, (SB_BK, SB_BK + LANES), 1)
    tri = {"gt": j > s, "le": j <= s, "lt": j < s}[kind]
    return (tri | (s >= SB_BK)).astype(BF16)


def _sb_common(q_ref):
    lane = lax.broadcasted_iota(jnp.int32, (SB_BQ, LANES), 1)
    q = q_ref[...] * (HEAD ** -0.5)
    q2 = jnp.concatenate([jnp.where(lane < HEAD, q, 0.0), jnp.where(lane >= HEAD, q, 0.0)], axis=0).astype(BF16)
    diff = (lax.broadcasted_iota(jnp.int32, (2 * SB_BQ, SB_BK), 1)
            - (lax.broadcasted_iota(jnp.int32, (2 * SB_BQ, SB_BK), 0) & (SB_BQ - 1)))
    return lane, q2, diff


def _rep(x):
    return jnp.concatenate([x] * (SB_BK // LANES), axis=1)


def _sb2_specs(t):
    q = pl.BlockSpec((SB_BQ, LANES), lambda j, i: (i, j))
    k = pl.BlockSpec((t, LANES), lambda j, i: (0, 4 + j))
    v = pl.BlockSpec((t, LANES), lambda j, i: (0, 8 + j))
    return q, k, v


def _sb2_fwd(proj, *, name):
    t = proj.shape[0]

    def body(q_ref, k_ref, v_ref, y_ref, lt_ref):
        i = pl.program_id(1)
        lane, q2, diff = _sb_common(q_ref)
        m_f = _tri_ones("gt")
        nk = (i + 1) * (SB_BQ // SB_BK)

        def step(it, carry):
            c, acc = carry
            kb = nk - 1 - it
            off = pl.multiple_of(kb * SB_BK, SB_BK)
            kblk = k_ref[pl.ds(off, SB_BK), :].astype(BF16)
            vblk = v_ref[pl.ds(off, SB_BK), :].astype(BF16)
            mask = diff < i * SB_BQ - kb * SB_BK
            z = lax.dot_general(q2, kblk, _NT, preferred_element_type=F32)
            lb = _log_sigmoid(z)
            w2 = _dot2(jnp.where(mask, lb - z, 0.0), m_f)
            att = jnp.where(mask, jnp.exp(lb + _rep(c) + w2[:, :SB_BK]), 0.0)
            acc = acc + lax.dot_general(att.astype(BF16), vblk, _NN, preferred_element_type=F32)
            return c + w2[:, SB_BK:], acc

        zero = jnp.zeros((2 * SB_BQ, LANES), F32)
        c, acc = lax.fori_loop(0, nk, step, (zero, zero))
        y_ref[...] = jnp.where(lane < HEAD, acc[:SB_BQ], acc[SB_BQ:])
        lt_ref[0] = c[:SB_BQ]
        lt_ref[1] = c[SB_BQ:]

    return pl.pallas_call(
        body, name=name, grid=(4, t // SB_BQ),
        in_specs=list(_sb2_specs(t)),
        out_specs=[pl.BlockSpec((SB_BQ, LANES), lambda j, i: (i, j)),
                   pl.BlockSpec((2, SB_BQ, LANES), lambda j, i: (j, i, 0))],
        out_shape=[jax.ShapeDtypeStruct((t, 4 * LANES), F32), jax.ShapeDtypeStruct((8, t, LANES), F32)],
        compiler_params=_params(("parallel", "arbitrary")),
    )(proj, proj, proj)


def _sb2_bwd(proj, dy, lt, *, name):
    t = proj.shape[0]

    def body(q_ref, k_ref, v_ref, dy_ref, lt_ref, dq_ref, dk_ref, dv_ref):
        i = pl.program_id(1)

        @pl.when(i == 0)
        def _():
            dk_ref[...] = jnp.zeros_like(dk_ref)
            dv_ref[...] = jnp.zeros_like(dv_ref)

        lane, q2, diff = _sb_common(q_ref)
        m_le, m_lt = _tri_ones("le"), _tri_ones("lt")
        dy_blk = dy_ref[...]
        do2 = jnp.concatenate([jnp.where(lane < HEAD, dy_blk, 0.0), jnp.where(lane >= HEAD, dy_blk, 0.0)],
                              axis=0).astype(BF16)
        lt2 = jnp.concatenate([lt_ref[0], lt_ref[1]], axis=0)

        def step(kb, carry):
            cp, cg, dq = carry
            off = pl.multiple_of(kb * SB_BK, SB_BK)
            kblk = k_ref[pl.ds(off, SB_BK), :].astype(BF16)
            vblk = v_ref[pl.ds(off, SB_BK), :].astype(BF16)
            mask = diff < i * SB_BQ - kb * SB_BK
            z = lax.dot_general(q2, kblk, _NT, preferred_element_type=F32)
            lb = _log_sigmoid(z)
            w2 = _dot2(jnp.where(mask, lb - z, 0.0), m_le)
            att = jnp.where(mask, jnp.exp(lb + _rep(lt2 - cp) - w2[:, :SB_BK]), 0.0)
            d_e = lax.dot_general(do2, vblk, _NT, preferred_element_type=F32) * att
            g2 = _dot2(d_e, m_lt)
            sig = jnp.exp(lb)
            dz = jnp.where(mask, d_e * (1.0 - sig) - (_rep(cg) + g2[:, :SB_BK]) * sig, 0.0).astype(BF16)
            dq = dq + lax.dot_general(dz, kblk, _NN, preferred_element_type=F32)
            dk_ref[pl.ds(off, SB_BK), :] += lax.dot_general(dz, q2, _TN, preferred_element_type=F32)
            dv_ref[pl.ds(off, SB_BK), :] += lax.dot_general(att.astype(BF16), do2, _TN, preferred_element_type=F32)
            return cp + w2[:, SB_BK:], cg + g2[:, SB_BK:], dq

        zero = jnp.zeros((2 * SB_BQ, LANES), F32)
        _, _, dq = lax.fori_loop(0, (i + 1) * (SB_BQ // SB_BK), step, (zero, zero, zero))
        dq_ref[...] = jnp.where(lane < HEAD, dq[:SB_BQ], dq[SB_BQ:]) * (HEAD ** -0.5)

    q_spec, k_spec, v_spec = _sb2_specs(t)
    blk = pl.BlockSpec((SB_BQ, LANES), lambda j, i: (i, j))
    col = pl.BlockSpec((t, LANES), lambda j, i: (0, j))
    return pl.pallas_call(
        body, name=name, grid=(4, t // SB_BQ),
        in_specs=[q_spec, k_spec, v_spec, blk, pl.BlockSpec((2, SB_BQ, LANES), lambda j, i: (j, i, 0))],
        out_specs=[blk, col, col],
        out_shape=[jax.ShapeDtypeStruct((t, 4 * LANES), F32)] * 3,
        compiler_params=_params(("parallel", "arbitrary")),
    )(proj, proj, proj, dy, lt)


SSD_HEADS = 16
SSD_PAIRS = 8


def _split3(x):
    a = x.astype(BF16)
    r = x - a.astype(F32)
    b = r.astype(BF16)
    return a, b, (r - b.astype(F32)).astype(BF16)


def _dot3(x, m, dn=_NN):
    return sum(lax.dot_general(p, m, dn, preferred_element_type=F32) for p in _split3(x))


def _mdot3(m, x):
    return sum(lax.dot_general(m, p, _NN, preferred_element_type=F32) for p in _split3(x))


def _ssd_common(dtr, dtb, alog, acsx_s, acst_s):
    lane = lax.broadcasted_iota(jnp.int32, (CHUNK, LANES), 1)
    lane1 = lax.broadcasted_iota(jnp.int32, (1, LANES), 1)
    arow = jnp.where(lane1 < SSD_HEADS, -jnp.exp(alog), 0.0)
    dt = jnp.where(lane < SSD_HEADS, _softplus(dtr + dtb), 0.0)
    da = dt * arow
    r = lax.broadcasted_iota(jnp.int32, (CHUNK, CHUNK), 0)
    c = lax.broadcasted_iota(jnp.int32, (CHUNK, CHUNK), 1)
    tril = (r >= c).astype(BF16)
    triu = (r <= c).astype(BF16)
    acs = _mdot3(tril, da)
    acst_s[...] = _dot3(da, triu, _TN)
    eh = lax.broadcasted_iota(jnp.int32, (LANES, 8 * LANES), 0)
    e = (eh == lax.broadcasted_iota(jnp.int32, (LANES, 8 * LANES), 1) // HEAD).astype(BF16)
    eh2 = lax.broadcasted_iota(jnp.int32, (LANES, 16 * LANES), 0)
    e2 = (eh2 == lax.broadcasted_iota(jnp.int32, (LANES, 16 * LANES), 1) // LANES).astype(BF16)
    acsx_s[...] = _dot3(acs, e)
    return dt, arow, _dot3(dt, e), _dot3(acs, e2), e, tril, triu


def _ssd_fwd(xc, proj, dtb, alog, dsk, *, name):
    t = xc.shape[0]
    nc = t // CHUNK

    def body(x_ref, b_ref, c_ref, dtr_ref, dtb_ref, alog_ref, dsk_ref, y_ref, hin_ref, acsx_s, acst_s, h_s):
        @pl.when(pl.program_id(0) == 0)
        def _():
            h_s[...] = jnp.zeros_like(h_s)

        dt, arow, dt_x, acs_b, e, tril, _ = _ssd_common(dtr_ref[...], dtb_ref[...], alog_ref[...], acsx_s, acst_s)
        dsk_x = _dot3(jnp.broadcast_to(dsk_ref[...], (CHUNK, LANES)), e)
        lane = lax.broadcasted_iota(jnp.int32, (CHUNK, LANES), 1)
        causal = (lax.broadcasted_iota(jnp.int32, (CHUNK, CHUNK), 0)
                  >= lax.broadcasted_iota(jnp.int32, (CHUNK, CHUNK), 1))
        for j in range(SSD_PAIRS):
            g = j // 4
            sl = slice(j * LANES, (j + 1) * LANES)
            if j % 4 == 0:
                bg = jnp.where(lane // HEAD == g, b_ref[...], 0.0)
                cg = jnp.where(lane // HEAD == g, c_ref[...], 0.0)
                cb = _dot_nt(cg, bg)
            x = x_ref[:, sl]
            a = acsx_s[:, sl]
            at = acsx_s[CHUNK - 1:CHUNK, sl]
            xdt = x * dt_x[:, sl]
            hin = h_s[j]
            hin_ref[0, j] = hin
            y = jnp.exp(a) * _dot_nn(cg, hin) + x * dsk_x[:, sl]
            h_s[j] = jnp.exp(at) * hin + _dot_tn(bg, xdt * jnp.exp(at - a))
            yd = []
            for hh in (0, 1):
                h = 2 * j + hh
                dec = jnp.exp(jnp.minimum(acs_b[:, h * LANES:(h + 1) * LANES] - acst_s[pl.ds(h, 1), :], 0.0))
                yd.append(_dot_nn(jnp.where(causal, cb * dec, 0.0), xdt))
            y_ref[:, sl] = y + jnp.where(lane < HEAD, yd[0], yd[1])

    one = pl.BlockSpec((1, LANES), lambda i: (0, 0))
    return pl.pallas_call(
        body, name=name, grid=(nc,),
        in_specs=[pl.BlockSpec((CHUNK, 8 * LANES), lambda i: (i, 0)),
                  pl.BlockSpec((CHUNK, LANES), lambda i: (i, 8)),
                  pl.BlockSpec((CHUNK, LANES), lambda i: (i, 9)),
                  pl.BlockSpec((CHUNK, LANES), lambda i: (i, C_DT // LANES)), one, one, one],
        out_specs=[pl.BlockSpec((CHUNK, 8 * LANES), lambda i: (i, 0)),
                   pl.BlockSpec((1, SSD_PAIRS, LANES, LANES), lambda i: (i, 0, 0, 0))],
        out_shape=[jax.ShapeDtypeStruct((t, 8 * LANES), F32),
                   jax.ShapeDtypeStruct((nc, SSD_PAIRS, LANES, LANES), F32)],
        scratch_shapes=[pltpu.VMEM((CHUNK, 8 * LANES), F32), pltpu.VMEM((LANES, CHUNK), F32),
                        pltpu.VMEM((SSD_PAIRS, LANES, LANES), F32)],
        compiler_params=_params(("arbitrary",)),
    )(xc, xc, xc, proj, dtb, alog, dsk)


def _ssd_bwd(xc, proj, dtb, alog, dsk, hin_all, dy, *, name):
    t = xc.shape[0]
    nc = t // CHUNK

    def body(x_ref, b_ref, c_ref, dtr_ref, dtb_ref, alog_ref, dsk_ref, hin_ref, dy_ref,
             dxc_ref, ddtr_ref, ddtb_ref, dalog_ref, ddsk_ref, acsx_s, acst_s, dh_s, dax_s, ddx_s):
        @pl.when(pl.program_id(0) == 0)
        def _():
            dh_s[...] = jnp.zeros_like(dh_s)
            ddtb_ref[...] = jnp.zeros_like(ddtb_ref)
            dalog_ref[...] = jnp.zeros_like(dalog_ref)
            ddsk_ref[...] = jnp.zeros_like(ddsk_ref)

        dtr = dtr_ref[...]
        dtb = dtb_ref[...]
        dt, arow, dt_x, acs_b, e, tril, triu = _ssd_common(dtr, dtb, alog_ref[...], acsx_s, acst_s)
        dsk_x = _dot3(jnp.broadcast_to(dsk_ref[...], (CHUNK, LANES)), e)
        lane = lax.broadcasted_iota(jnp.int32, (CHUNK, LANES), 1)
        rowi = lax.broadcasted_iota(jnp.int32, (CHUNK, LANES), 0)
        causal = (lax.broadcasted_iota(jnp.int32, (CHUNK, CHUNK), 0)
                  >= lax.broadcasted_iota(jnp.int32, (CHUNK, CHUNK), 1))
        dacs = jnp.zeros((CHUNK, LANES), F32)
        d_b = jnp.zeros((CHUNK, LANES), F32)
        d_c = jnp.zeros((CHUNK, LANES), F32)
        for j in range(SSD_PAIRS):
            g = j // 4
            sl = slice(j * LANES, (j + 1) * LANES)
            if j % 4 == 0:
                bg = jnp.where(lane // HEAD == g, b_ref[...], 0.0)
                cg = jnp.where(lane // HEAD == g, c_ref[...], 0.0)
                cb = _dot_nt(cg, bg)
                dcb = jnp.zeros((CHUNK, CHUNK), F32)
            x = x_ref[:, sl]
            d = dt_x[:, sl]
            a = acsx_s[:, sl]
            at = acsx_s[CHUNK - 1:CHUNK, sl]
            xdt = x * d
            hin = hin_ref[0, j]
            dhout = dh_s[j]
            dyp = dy_ref[:, sl]
            ea, eat, ed = jnp.exp(a), jnp.exp(at), jnp.exp(at - a)
            da_l = dyp * ea * _dot_nn(cg, hin)
            dm = dyp * ea
            d_c = d_c + _dot_nt(dm, hin)
            dh_s[j] = _dot_tn(cg, dm) + eat * dhout
            dat = jnp.sum(dhout * hin * eat, axis=0, keepdims=True)
            d_b = d_b + _dot_nt(xdt * ed, dhout)
            dw = _dot_nn(bg, dhout)
            dxdt = dw * ed
            ded = dw * xdt * ed
            dat = dat + jnp.sum(ded, axis=0, keepdims=True)
            da_l = da_l - ded
            for hh in (0, 1):
                h = 2 * j + hh
                dec = jnp.exp(jnp.minimum(acs_b[:, h * LANES:(h + 1) * LANES] - acst_s[pl.ds(h, 1), :], 0.0))
                gm = jnp.where(causal, cb * dec, 0.0)
                dyh = jnp.where(lane // HEAD == hh, dyp, 0.0)
                dg = _dot_nt(dyh, xdt)
                dxdt = dxdt + _dot_tn(gm, dyh)
                dcb = dcb + jnp.where(causal, dg * dec, 0.0)
                th = dg * gm
                oh = (lane == h).astype(BF16)
                dacs = dacs + _dot2(th, oh) - _dot2_tn(th, oh)
            if j % 4 == 3:
                d_c = d_c + _dot_nn(dcb, bg)
                d_b = d_b + _dot_tn(dcb, cg)
            dxc_ref[:, sl] = dyp * dsk_x[:, sl] + dxdt * d
            ddx_s[:, sl] = dxdt * x
            dax_s[:, sl] = da_l + jnp.where(rowi == CHUNK - 1, dat, 0.0)
            dskp = jnp.sum(dyp * x, axis=0, keepdims=True)
            ddsk_ref[...] += _dot2(jnp.broadcast_to(dskp, (8, LANES)), e[:, sl], _NT)
        dxc_ref[:, 8 * LANES:9 * LANES] = d_b
        dxc_ref[:, 9 * LANES:10 * LANES] = d_c
        dacs = dacs + _dot2(dax_s[...], e, _NT)
        ddt = _dot2(ddx_s[...], e, _NT)
        dda = _mdot3(triu, dacs)
        ddt = ddt + dda * arow
        dalog_ref[...] += jnp.sum(dda * dt, axis=0, keepdims=True) * arow
        ddtr = jnp.where(lane < SSD_HEADS, ddt * _sigmoid(dtr + dtb), 0.0)
        ddtr_ref[...] = ddtr
        ddtb_ref[...] += jnp.sum(ddtr, axis=0, keepdims=True)

    one = pl.BlockSpec((1, LANES), lambda i: (0, 0))
    rev = lambda c: (lambda i: (nc - 1 - i, c))
    return pl.pallas_call(
        body, name=name, grid=(nc,),
        in_specs=[pl.BlockSpec((CHUNK, 8 * LANES), rev(0)), pl.BlockSpec((CHUNK, LANES), rev(8)),
                  pl.BlockSpec((CHUNK, LANES), rev(9)), pl.BlockSpec((CHUNK, LANES), rev(C_DT // LANES)),
                  one, one, one,
                  pl.BlockSpec((1, SSD_PAIRS, LANES, LANES), lambda i: (nc - 1 - i, 0, 0, 0)),
                  pl.BlockSpec((CHUNK, 8 * LANES), rev(0))],
        out_specs=[pl.BlockSpec((CHUNK, XBC_COLS), rev(0)), pl.BlockSpec((CHUNK, LANES), rev(0)), one, one,
                   pl.BlockSpec((8, LANES), lambda i: (0, 0))],
        out_shape=[jax.ShapeDtypeStruct((t, XBC_COLS), F32), jax.ShapeDtypeStruct((t, LANES), F32)]
        + [jax.ShapeDtypeStruct((1, LANES), F32)] * 2 + [jax.ShapeDtypeStruct((8, LANES), F32)],
        scratch_shapes=[pltpu.VMEM((CHUNK, 8 * LANES), F32), pltpu.VMEM((LANES, CHUNK), F32),
                        pltpu.VMEM((SSD_PAIRS, LANES, LANES), F32),
                        pltpu.VMEM((CHUNK, 8 * LANES), F32), pltpu.VMEM((CHUNK, 8 * LANES), F32)],
        compiler_params=_params(("arbitrary",)),
    )(xc, xc, xc, proj, dtb, alog, dsk, hin_all, dy)


RW_LW = 128
RW_PAIRS = 4 * LANES // RW_LW
RW_BT = 16
RW_DECAY_ROW = 1
RW_BWD_PAIRS = 4


def _rw_consts():
    seg = _seg_matrix(RW_LW)
    ti = (lax.broadcasted_iota(jnp.int32, (HEAD, RW_LW), 0)
          == lax.broadcasted_iota(jnp.int32, (HEAD, RW_LW), 1) % HEAD)
    return seg, ti


def _col_tiles(rows, ti, seg):
    tib = ti.astype(BF16)
    n = len(rows)
    hi = [r.astype(BF16) for r in rows]
    w_lo = (rows[RW_DECAY_ROW] - hi[RW_DECAY_ROW].astype(F32)).astype(BF16)
    out = lax.dot_general(jnp.concatenate([tib * h for h in hi + [w_lo]], axis=0), seg, _NN, preferred_element_type=F32)
    tiles = [out[i * HEAD:(i + 1) * HEAD] for i in range(n)]
    tiles[RW_DECAY_ROW] = tiles[RW_DECAY_ROW] + out[n * HEAD:(n + 1) * HEAD]
    return tiles


def _col_tiles2(rows, ti, seg):
    tib = ti.astype(BF16)
    hi = [r.astype(BF16) for r in rows]
    lo = [(r - h.astype(F32)).astype(BF16) for r, h in zip(rows, hi)]
    out = (lax.dot_general(jnp.concatenate([tib * h for h in hi], axis=0), seg, _NN, preferred_element_type=F32)
           + lax.dot_general(jnp.concatenate([tib * l for l in lo], axis=0), seg, _NN, preferred_element_type=F32))
    return [out[i * HEAD:(i + 1) * HEAD] for i in range(len(rows))]


def _head_lane_sums(tiles, ti, seg):
    out = _dot2(jnp.concatenate(tiles, axis=0), seg)
    return [jnp.sum(jnp.where(ti, out[i * HEAD:(i + 1) * HEAD], 0.0), axis=0, keepdims=True) for i in range(len(tiles))]


def _rw_scan_fwd(mixed, w, k, n, b, *, name):
    t = w.shape[0]

    def body(r_ref, v_ref, w_ref, k_ref, n_ref, b_ref, y_ref, st_ref, s_s):
        @pl.when(pl.program_id(0) == 0)
        def _():
            s_s[...] = jnp.zeros_like(s_s)

        seg, ti = _rw_consts()

        def step(tt, state):
            row = pl.ds(tt, 1)
            new = []
            for p in range(RW_PAIRS):
                sl = pl.ds(p * RW_LW, RW_LW)
                s = state[p]
                ncol, wcol, bcol, kcol, rcol = _col_tiles(
                    [x[row, sl] for x in (n_ref, w_ref, b_ref, k_ref, r_ref)], ti, seg)
                sa = jnp.sum(s * ncol, axis=0, keepdims=True)
                s = s * wcol + bcol * sa + kcol * v_ref[row, sl]
                y_ref[row, sl] = jnp.sum(s * rcol, axis=0, keepdims=True)
                st_ref[tt, p] = s
                new.append(s)
            return tuple(new)

        out = tuple(s_s[p] for p in range(RW_PAIRS))
        for tt in range(RW_BT):
            out = step(tt, out)
        for p in range(RW_PAIRS):
            s_s[p] = out[p]

    blk = lambda c: pl.BlockSpec((RW_BT, 4 * LANES), functools.partial(lambda i, c: (i, c), c=c))
    return pl.pallas_call(
        body, name=name, grid=(t // RW_BT,),
        in_specs=[blk(0), blk(2), blk(0), blk(0), blk(0), blk(0)],
        out_specs=[blk(0), pl.BlockSpec((RW_BT, RW_PAIRS, HEAD, RW_LW), lambda i: (i, 0, 0, 0))],
        out_shape=[jax.ShapeDtypeStruct((t, 4 * LANES), F32),
                   jax.ShapeDtypeStruct((t, RW_PAIRS, HEAD, RW_LW), F32)],
        scratch_shapes=[pltpu.VMEM((RW_PAIRS, HEAD, RW_LW), F32)],
        compiler_params=_params(("arbitrary",)),
    )(mixed, mixed, w, k, n, b)


def _rw_scan_bwd(mixed, w, k, n, b, states, dy, dr0, dk0, dv0, *, name):
    t = w.shape[0]
    nb = t // RW_BT
    ppc = RW_BWD_PAIRS
    ng = RW_PAIRS // ppc

    def body(r_ref, v_ref, w_ref, k_ref, n_ref, b_ref, st_ref, prev_ref, dy_ref, dr0_ref, dk0_ref, dv0_ref,
             dr_ref, dw_ref, dk_ref, dv_ref, dn_ref, db_ref, ds_s):
        @pl.when(pl.program_id(1) == 0)
        def _():
            ds_s[...] = jnp.zeros_like(ds_s)

        seg, ti = _rw_consts()
        has_prev = (pl.program_id(1) < nb - 1).astype(F32)

        def step(it, carry):
            tt = RW_BT - 1 - it
            row = pl.ds(tt, 1)
            prev_t = max(tt - 1, 0)
            new_ds, new_s = [], []
            for p in range(ppc):
                sl = pl.ds(p * RW_LW, RW_LW)
                ds, s_t = carry[p], carry[ppc + p]
                s_p = st_ref[prev_t, p] if tt > 0 else prev_ref[0, p] * has_prev
                ncol, wcol, bcol, kcol, rcol = _col_tiles2(
                    [x[row, sl] for x in (n_ref, w_ref, b_ref, k_ref, r_ref)], ti, seg)
                vv, dyy = v_ref[row, sl], dy_ref[row, sl]
                sa = jnp.sum(s_p * ncol, axis=0, keepdims=True)
                ds = ds + rcol * dyy
                dsa = jnp.sum(ds * bcol, axis=0, keepdims=True)
                dv_ref[row, sl] = jnp.sum(ds * kcol, axis=0, keepdims=True) + dv0_ref[row, sl]
                dr, dw, db, dk, dn = _head_lane_sums([s_t * dyy, ds * s_p, ds * sa, ds * vv, s_p * dsa], ti, seg)
                dr_ref[row, sl] = dr + dr0_ref[row, sl]
                dw_ref[row, sl] = dw
                db_ref[row, sl] = db
                dk_ref[row, sl] = dk + dk0_ref[row, sl]
                dn_ref[row, sl] = dn
                new_ds.append(ds * wcol + ncol * dsa)
                new_s.append(s_p)
            return tuple(new_ds) + tuple(new_s)

        init = tuple(ds_s[p] for p in range(ppc)) + tuple(st_ref[RW_BT - 1, p] for p in range(ppc))
        out = init
        for it in range(RW_BT):
            out = step(it, out)
        for p in range(ppc):
            ds_s[p] = out[p]

    blk = lambda c: pl.BlockSpec((RW_BT, ppc * RW_LW), functools.partial(lambda g, i, c: (nb - 1 - i, c * ng + g), c=c))
    st_spec = pl.BlockSpec((RW_BT, ppc, HEAD, RW_LW), lambda g, i: (nb - 1 - i, g, 0, 0))
    prev_spec = pl.BlockSpec((1, ppc, HEAD, RW_LW), lambda g, i: (jnp.maximum((nb - 1 - i) * RW_BT - 1, 0), g, 0, 0))
    return pl.pallas_call(
        body, name=name, grid=(ng, nb),
        in_specs=[blk(0), blk(2), blk(0), blk(0), blk(0), blk(0), st_spec, prev_spec, blk(0), blk(0), blk(0), blk(0)],
        out_specs=[blk(0)] * 6,
        out_shape=[jax.ShapeDtypeStruct((t, 4 * LANES), F32)] * 6,
        scratch_shapes=[pltpu.VMEM((ppc, HEAD, RW_LW), F32)],
        compiler_params=_params(("parallel", "arbitrary")),
    )(mixed, mixed, w, k, n, b, states, states, dy, dr0, dk0, dv0)


RW_C = 64


def _p3(a, b, dn):
    ah, al = _split2(a)
    bh, bl = _split2(b)
    d = lambda x, y: lax.dot_general(x, y, dn, preferred_element_type=F32)
    return d(ah, bh) + d(ah, bl) + d(al, bh)


_BNN = (((2,), (1,)), ((0,), (0,)))
_BNT = (((2,), (2,)), ((0,), (0,)))
_BTN = (((1,), (1,)), ((0,), (0,)))


@jax.custom_vjp
def _pnn(a, b):
    return _p3(a, b, _BNN)


@jax.custom_vjp
def _pnt(a, b):
    return _p3(a, b, _BNT)


@jax.custom_vjp
def _ptn(a, b):
    return _p3(a, b, _BTN)


_pnn.defvjp(lambda a, b: (_p3(a, b, _BNN), (a, b)), lambda res, g: (_p3(g, res[1], _BNT), _p3(res[0], g, _BTN)))
_pnt.defvjp(lambda a, b: (_p3(a, b, _BNT), (a, b)), lambda res, g: (_p3(g, res[1], _BNN), _p3(g, res[0], _BTN)))
_ptn.defvjp(lambda a, b: (_p3(a, b, _BTN), (a, b)), lambda res, g: (_p3(res[1], g, _BNT), _p3(res[0], g, _BNN)))


def _rw_chunk_consts():
    c2 = 2 * RW_C
    row = lax.broadcasted_iota(jnp.int32, (c2, c2), 0)
    col = lax.broadcasted_iota(jnp.int32, (c2, c2), 1)
    same = (row // RW_C) == (col // RW_C)
    strict = (same & (row > col)).astype(F32)
    incl = (same & (row >= col)).astype(F32)
    eye = (row == col).astype(F32)
    tr = lax.broadcasted_iota(jnp.int32, (RW_C, RW_C), 0)
    tc = lax.broadcasted_iota(jnp.int32, (RW_C, RW_C), 1)
    tril = (tr >= tc).astype(F32)
    lane = lax.broadcasted_iota(jnp.int32, (1, LANES), 1)
    hm = [(lane // HEAD == h).astype(F32) for h in (0, 1)]
    return strict, incl, eye, tril, hm


def _rw_chunk(r, lw, k, v, n, b, s2, consts):
    strict, incl, eye, tril, hm = consts
    two = lambda x: jnp.concatenate([x * hm[0], x * hm[1]], axis=1)
    cum = _pnn(jnp.broadcast_to(tril, (4, RW_C, RW_C)), lw)
    grow, shrink = jnp.exp(-cum), jnp.exp(cum)
    n2, r2 = two(n * jnp.exp(cum - lw)), two(r * shrink)
    b2, k2, v2 = two(b * grow), two(k * grow), two(v)
    p = _pnt(n2, b2) * strict
    x2 = _pnt(n2, s2) + _pnn(_pnt(n2, k2) * strict, v2)
    t_inv, a = eye + p, p
    for _ in range(RW_C.bit_length() - 2):
        a = _pnn(a, a)
        t_inv = t_inv + _pnn(t_inv, a)
    u2 = _pnn(t_inv, x2)
    y2 = _pnt(r2, s2) + _pnn(_pnt(r2, b2) * incl, u2) + _pnn(_pnt(r2, k2) * incl, v2)
    s2_new = (s2 + _ptn(u2, b2) + _ptn(v2, k2)) * jnp.exp(jnp.sum(lw, axis=1, keepdims=True))
    return jnp.sum(y2.reshape(4, 2, RW_C, LANES), axis=1), s2_new


def _pairs(ref):
    return jnp.stack([ref[:, p * LANES:(p + 1) * LANES] for p in range(4)])


def _rw_chunk_fwd(mixed, lw, k, n, b, *, name):
    t = lw.shape[0]
    nc = t // RW_C

    def body(r_ref, v_ref, lw_ref, k_ref, n_ref, b_ref, y_ref, sin_ref, s_s):
        @pl.when(pl.program_id(0) == 0)
        def _():
            s_s[...] = jnp.zeros_like(s_s)

        s2 = s_s[...]
        sin_ref[0] = s2
        y, s2 = _rw_chunk(*[_pairs(x) for x in (r_ref, lw_ref, k_ref, v_ref, n_ref, b_ref)], s2, _rw_chunk_consts())
        for p in range(4):
            y_ref[:, p * LANES:(p + 1) * LANES] = y[p]
        s_s[...] = s2

    blk = lambda c: pl.BlockSpec((RW_C, 4 * LANES), functools.partial(lambda i, c: (i, c), c=c))
    return pl.pallas_call(
        body, name=name, grid=(nc,),
        in_specs=[blk(0), blk(2), blk(0), blk(0), blk(0), blk(0)],
        out_specs=[blk(0), pl.BlockSpec((1, 4, LANES, LANES), lambda i: (i, 0, 0, 0))],
        out_shape=[jax.ShapeDtypeStruct((t, 4 * LANES), F32), jax.ShapeDtypeStruct((nc, 4, LANES, LANES), F32)],
        scratch_shapes=[pltpu.VMEM((4, LANES, LANES), F32)],
        compiler_params=_params(("arbitrary",)),
    )(mixed, mixed, lw, k, n, b)


def _rw_chunk_bwd(mixed, lw, k, n, b, s_in, dy, dr0, dk0, dv0, *, name):
    t = lw.shape[0]
    nc = t // RW_C

    def body(r_ref, v_ref, lw_ref, k_ref, n_ref, b_ref, sin_ref, dy_ref, dr0_ref, dk0_ref, dv0_ref,
             dr_ref, dlw_ref, dk_ref, dv_ref, dn_ref, db_ref, ds_s):
        @pl.when(pl.program_id(0) == 0)
        def _():
            ds_s[...] = jnp.zeros_like(ds_s)

        consts = _rw_chunk_consts()
        args = [_pairs(x) for x in (r_ref, lw_ref, k_ref, v_ref, n_ref, b_ref)] + [sin_ref[0]]
        _, vjp = jax.vjp(lambda *a: _rw_chunk(*a, consts), *args)
        dr, dlw, dk, dv, dn, db, ds = vjp((_pairs(dy_ref), ds_s[...]))
        for p in range(4):
            sl = slice(p * LANES, (p + 1) * LANES)
            dr_ref[:, sl] = dr[p] + dr0_ref[:, sl]
            dlw_ref[:, sl] = dlw[p]
            dk_ref[:, sl] = dk[p] + dk0_ref[:, sl]
            dv_ref[:, sl] = dv[p] + dv0_ref[:, sl]
            dn_ref[:, sl] = dn[p]
            db_ref[:, sl] = db[p]
        ds_s[...] = ds

    blk = lambda c: pl.BlockSpec((RW_C, 4 * LANES), functools.partial(lambda i, c: (nc - 1 - i, c), c=c))
    return pl.pallas_call(
        body, name=name, grid=(nc,),
        in_specs=[blk(0), blk(2), blk(0), blk(0), blk(0), blk(0),
                  pl.BlockSpec((1, 4, LANES, LANES), lambda i: (nc - 1 - i, 0, 0, 0)), blk(0), blk(0), blk(0), blk(0)],
        out_specs=[blk(0)] * 6,
        out_shape=[jax.ShapeDtypeStruct((t, 4 * LANES), F32)] * 6,
        scratch_shapes=[pltpu.VMEM((4, LANES, LANES), F32)],
        compiler_params=_params(("arbitrary",)),
    )(mixed, mixed, lw, k, n, b, s_in, dy, dr0, dk0, dv0)


def _f_rms_res(x, g):
    return _f_rms(x, g)[0], x


def _final(x, g, target, *, bt, name):
    t, d = x.shape

    def body(x_ref, g_ref, t_ref, dx_ref, loss_ref, dg_ref):
        tgt = t_ref[...]

        def f(xv, gv):
            err = _f_rms(xv, gv)[0] - tgt
            return 0.5 * jnp.mean(err * err, axis=-1, keepdims=True)

        row_loss, vjp = jax.vjp(f, x_ref[...], g_ref[...])
        dx, dg = vjp(jnp.ones_like(row_loss))
        dx_ref[...] = dx

        @pl.when(pl.program_id(0) == 0)
        def _():
            loss_ref[...] = jnp.zeros_like(loss_ref)
            dg_ref[...] = jnp.zeros_like(dg_ref)

        loss_ref[...] += jnp.broadcast_to(jnp.sum(row_loss, axis=0, keepdims=True), (1, LANES))
        dg_ref[...] += dg

    blk = pl.BlockSpec((bt, d), lambda i: (i, 0))
    return pl.pallas_call(
        body, name=name, grid=(t // bt,),
        in_specs=[blk, pl.BlockSpec((1, d), lambda i: (0, 0)), blk],
        out_specs=[blk, pl.BlockSpec((1, LANES), lambda i: (0, 0)), pl.BlockSpec((1, d), lambda i: (0, 0))],
        out_shape=[jax.ShapeDtypeStruct((t, d), F32), jax.ShapeDtypeStruct((1, LANES), F32),
                   jax.ShapeDtypeStruct((1, d), F32)],
        compiler_params=_params(("arbitrary",)),
    )(x, g, target)


ADAMW_BLOCK_BYTES = 1 << 20


def _adamw(w, g, m, v, *, name):
    shape = w.shape
    c = shape[-1]
    shape3 = (1,) * (3 - len(shape)) + shape if len(shape) <= 3 else (-1,) + shape[-2:]
    args = [a.reshape(shape3) for a in (w, g, m, v)]
    lead, r, _ = args[0].shape
    br = r
    if r * c * 4 > ADAMW_BLOCK_BYTES:
        cands = [b for b in range(8, r, 8) if r % b == 0 and b * c * 4 <= ADAMW_BLOCK_BYTES]
        br = max(cands) if cands else r

    def body(w_ref, g_ref, m_ref, v_ref, d_ref, nm_ref, nv_ref):
        gv = g_ref[...]
        m_new = ADAM_B1 * m_ref[...] + (1.0 - ADAM_B1) * gv
        v_new = ADAM_B2 * v_ref[...] + (1.0 - ADAM_B2) * (gv * gv)
        m_hat = m_new / (1.0 - ADAM_B1 ** ADAM_STEP)
        v_hat = v_new / (1.0 - ADAM_B2 ** ADAM_STEP)
        d_ref[...] = -ADAM_LR * (m_hat / (jnp.sqrt(v_hat) + ADAM_EPS) + ADAM_WD * w_ref[...])
        nm_ref[...] = m_new
        nv_ref[...] = v_new

    blk = pl.BlockSpec((1, br, c), lambda l, i: (l, i, 0))
    outs = pl.pallas_call(
        body, name=name, grid=(lead, r // br), in_specs=[blk] * 4, out_specs=[blk] * 3,
        out_shape=[jax.ShapeDtypeStruct((lead, r, c), F32)] * 3,
        compiler_params=_params(("parallel", "parallel")),
    )(*args)
    return tuple(o.reshape(shape) for o in outs)


BT = 256
BC = 128


def _layer_rows(x, proj, s):
    s = {k: s.get(k) for k in ("y_sb_raw", "y_ssd_raw", "mixed", "ys", "k2", "p_sb", "p_ssd", "p_rw")}
    return dict(
        rms=[(x, D_MODEL, 0)],
        sb_gate=[(s["y_sb_raw"], 512, 0), (proj, 512, 3)],
        ssd_norm=[(s["y_ssd_raw"], 1024, 0), (proj, 1024, C_Z // 1024)],
        rw_pre=[(s["mixed"], 512, 1), (s["mixed"], LANES, 16)],
        rw_post=[(s["ys"], 512, 0), (s["mixed"], 512, 0), (s["k2"], 512, 0), (s["mixed"], 512, 2), (s["mixed"], 512, 3)],
        merge=[(s["p_sb"], 1024, 0), (s["p_ssd"], 1024, 0), (s["p_rw"], 1024, 0),
               (proj, 1024, 3), (proj, 1024, 4), (proj, 1024, 5)],
    )


def _layer_fwd(x, p, nm):
    s = {}
    (s["h"],) = _rowwise(_f_rms, [(x, D_MODEL, 0)], [p["norm_g"]], [D_MODEL], bt=BT, name=nm + "rms")
    proj = s["proj"] = _mm(s["h"], p["w_in"], name=nm + "proj")
    s["y_sb_raw"], s["lt"] = _sb2_fwd(proj, name=nm + "sb")
    s["xc"] = _colwise(_f_conv, proj, C_XBC, XBC_COLS, p["conv"], bc=BC, name=nm + "conv")
    s["y_ssd_raw"], s["hin"] = _ssd_fwd(s["xc"], proj, p["dt_bias"], p["a_log"], p["d_skip"], name=nm + "ssd")
    s["mixed"] = _colwise(_f_rw_mix, proj, C_RW, RW_COLS, [p["rw_mu"]], bc=BC, name=nm + "mix")
    s["w"], s["k2"], s["n"], s["b"] = _rowwise(_f_rw_pre, [(s["mixed"], 512, 1), (s["mixed"], LANES, 16)], p["rw_pre"],
                                               [512] * 4, bt=BT, name=nm + "rwpre")
    s["ys"], s["st"] = _rw_chunk_fwd(s["mixed"], s["w"], s["k2"], s["n"], s["b"], name=nm + "scan")
    rows = _layer_rows(x, proj, s)
    (s["y_sb"],) = _rowwise(_f_sb_gate, rows["sb_gate"], [], [512], bt=BT, name=nm + "sbgate")
    (s["y_ssd"],) = _rowwise(_f_ssd_norm, rows["ssd_norm"], [p["ssd_norm_g"]], [1024], bt=BT, name=nm + "ssdnorm")
    (s["y_rw"],) = _rowwise(_f_rw_post, rows["rw_post"], p["rw_post"], [512], bt=BT, name=nm + "rwpost")
    s["p_sb"] = _mm(s["y_sb"], p["w_out_sb"], name=nm + "osb")
    s["p_ssd"] = _mm(s["y_ssd"], p["w_out_ssd"], name=nm + "ossd")
    s["p_rw"] = _mm(s["y_rw"], p["w_out_rw"], name=nm + "orw")
    (s["merged"],) = _rowwise(_f_merge, _layer_rows(x, proj, s)["merge"], [], [1024], bt=BT, name=nm + "merge")
    return _mm(s["merged"], p["w_o"], add=x, name=nm + "wo"), s


def _layer_bwd(x, dx_out, p, s, nm):
    g = {}
    proj = s["proj"]
    rows = _layer_rows(x, proj, s)
    g["w_o"] = _mm(s["merged"], dx_out, ta=True, name=nm + "g_wo")
    d_merged = _mm(dx_out, p["w_o"], tb=True, name=nm + "d_merged")
    dp_sb, dp_ssd, dp_rw, d_gates = _rowwise_bwd(_f_merge, rows["merge"], [], [(d_merged, 1024, 0)], bt=BT,
                                                 name=nm + "merge_b", groups=[[0], [1], [2], [3, 4, 5]])
    g["w_out_sb"] = _mm(s["y_sb"], dp_sb, ta=True, name=nm + "g_osb")
    g["w_out_ssd"] = _mm(s["y_ssd"], dp_ssd, ta=True, name=nm + "g_ossd")
    g["w_out_rw"] = _mm(s["y_rw"], dp_rw, ta=True, name=nm + "g_orw")
    dy_sb = _mm(dp_sb, p["w_out_sb"], tb=True, name=nm + "d_ysb")
    dy_ssd = _mm(dp_ssd, p["w_out_ssd"], tb=True, name=nm + "d_yssd")
    dy_rw = _mm(dp_rw, p["w_out_rw"], tb=True, name=nm + "d_yrw")
    dy_sb_raw, d_sbgate = _rowwise_bwd(_f_sb_gate, rows["sb_gate"], [], [(dy_sb, 512, 0)], bt=BT, name=nm + "sbgate_b")
    dq, dk, dv = _sb2_bwd(proj, dy_sb_raw, s["lt"], name=nm + "sb_b")
    dy_ssd_raw, dz, g["ssd_norm_g"] = _rowwise_bwd(_f_ssd_norm, rows["ssd_norm"], [p["ssd_norm_g"]],
                                                   [(dy_ssd, 1024, 0)], bt=BT, name=nm + "ssdnorm_b")
    dxc, ddtr, g["dt_bias"], g["a_log"], g["d_skip"] = _ssd_bwd(
        s["xc"], proj, p["dt_bias"], p["a_log"], p["d_skip"], s["hin"], dy_ssd_raw, name=nm + "ssd_b")
    conv_out = _colwise_bwd(_f_conv, proj, C_XBC, XBC_COLS, p["conv"], dxc, bc=BC, name=nm + "conv_b")
    dxbc, g["conv"] = conv_out[0], conv_out[1:]
    dys, dr0, dk0, dv0, d_rwgate, g["rw_ln_g"], g["rw_ln_b"], g["rw_r_k"] = _rowwise_bwd(
        _f_rw_post, rows["rw_post"], p["rw_post"], [(dy_rw, 512, 0)], bt=BT, name=nm + "rwpost_b")
    dr, dw, dk2, dvv, dn, db = _rw_chunk_bwd(s["mixed"], s["w"], s["k2"], s["n"], s["b"], s["st"], dys, dr0, dk0, dv0,
                                            name=nm + "scan_b")
    pre_out = _rowwise_bwd(_f_rw_pre, rows["rw_pre"], p["rw_pre"],
                           [(dw, 512, 0), (dk2, 512, 0), (dn, 512, 0), (db, 512, 0)], bt=BT, name=nm + "rwpre_b")
    dkm, dlo, g["rw_pre"] = pre_out[0], pre_out[1], pre_out[2:]
    d_mixed = jnp.concatenate([dr, dkm, dvv, d_rwgate, dlo], axis=1)
    d_slab, g["rw_mu"] = _colwise_bwd(_f_rw_mix, proj, C_RW, RW_COLS, [p["rw_mu"]], d_mixed, bc=BC, name=nm + "mix_b")
    d_proj = jnp.concatenate([dq, dk, dv, d_sbgate, dz, d_gates, d_slab, ddtr, dxbc], axis=1)
    g["w_in"] = _mm(s["h"], d_proj, ta=True, name=nm + "g_win")
    dh = _mm(d_proj, p["w_in"], tb=True, tn=1024, tk=512, name=nm + "d_h")
    dx, g["norm_g"] = _rowwise_bwd(_f_rms_res, rows["rms"], [p["norm_g"]], [(dh, D_MODEL, 0), (dx_out, D_MODEL, 0)],
                                   bt=BT, name=nm + "rms_b")
    return dx, g


MESH = pl.DeviceIdType.MESH
N_DEV = 8
_ANY = pl.BlockSpec(memory_space=pl.ANY)
_CHIP_SEMS = [pltpu.SemaphoreType.DMA((3,)), pltpu.SemaphoreType.DMA((3,)), pltpu.SemaphoreType.DMA]


def _here():
    x, y, c = lax.axis_index("x"), lax.axis_index("y"), lax.axis_index("c")
    return x, y, c, [(1 - x, y), (x, 1 - y), (1 - x, 1 - y)]


def _chip_exchange(srcs, *, per_dest, name):
    n = len(srcs)

    def body(*refs):
        src_refs, out_refs = refs[:n], refs[n:2 * n]
        send_sems, recv_sems, local_sems = refs[2 * n:]
        x, y, c, chips = _here()
        me = 2 * x + y
        sends, owns = [], []
        for a, (src_ref, out_ref) in enumerate(zip(src_refs, out_refs)):
            pick = (lambda q, s=src_ref: s.at[q]) if per_dest else (lambda q, s=src_ref: s.at[c])
            owns.append(pltpu.make_async_copy(pick(me), out_ref.at[me], local_sems.at[a]))
            owns[-1].start()
            for j, (px, py) in enumerate(chips):
                sends.append(pltpu.make_async_remote_copy(
                    pick(2 * px + py), out_ref.at[me], send_sems.at[3 * a + j], recv_sems.at[3 * a + j],
                    device_id=(px, py, c), device_id_type=MESH))
                sends[-1].start()
        for a, (src_ref, out_ref) in enumerate(zip(src_refs, out_refs)):
            for j, (px, py) in enumerate(chips):
                pltpu.make_async_remote_copy(
                    src_ref.at[0], out_ref.at[2 * px + py], send_sems.at[3 * a + j], recv_sems.at[3 * a + j],
                    device_id=(px, py, c), device_id_type=MESH).wait_recv()
        for cp in sends:
            cp.wait_send()
        for cp in owns:
            cp.wait()

    return pl.pallas_call(
        body, name=name, in_specs=[_ANY] * n, out_specs=[_ANY] * n,
        out_shape=[jax.ShapeDtypeStruct((4,) + s.shape[1:], s.dtype) for s in srcs],
        scratch_shapes=[pltpu.SemaphoreType.DMA((3 * n,)), pltpu.SemaphoreType.DMA((3 * n,)),
                        pltpu.SemaphoreType.DMA((n,))],
    )(*srcs)


def _sibling_swap(srcs, *, other_slot, name):
    n = len(srcs)

    def body(*refs):
        src_refs, out_refs, send_sems, recv_sems = refs[:n], refs[n:2 * n], refs[2 * n], refs[2 * n + 1]
        x, y, c, _ = _here()
        copies = [pltpu.make_async_remote_copy(s.at[1 - c] if other_slot else s, o, send_sems.at[a], recv_sems.at[a],
                                               device_id=(x, y, 1 - c), device_id_type=MESH)
                  for a, (s, o) in enumerate(zip(src_refs, out_refs))]
        for cp in copies:
            cp.start()
        for cp in copies:
            cp.wait()

    return pl.pallas_call(
        body, name=name, in_specs=[_ANY] * n, out_specs=[_ANY] * n,
        out_shape=[jax.ShapeDtypeStruct(s.shape[1:] if other_slot else s.shape, s.dtype) for s in srcs],
        scratch_shapes=[pltpu.SemaphoreType.DMA((n,)), pltpu.SemaphoreType.DMA((n,))],
    )(*srcs)


def _allgather_small(v, *, reduce, name):
    r = v.shape[0]

    def body(v_ref, out_ref, *rest):
        send_sems, recv_sems, local_sem = rest[-3:]
        x, y, c, chips = _here()
        me, sibling = (x, y, c), (x, y, 1 - c)

        def slot(px, py, pc):
            return out_ref.at[4 * px + 2 * py + pc]

        def copy(k, block, to, src=None):
            return pltpu.make_async_remote_copy(
                src_ref=slot(*block) if src is None else src, dst_ref=slot(*block),
                send_sem=send_sems.at[k], recv_sem=recv_sems.at[k], device_id=to, device_id_type=MESH)

        mine = pltpu.make_async_copy(v_ref, slot(*me), local_sem)
        mine.start()
        first = [copy(0, me, sibling, src=v_ref)]
        first += [copy(1 + j, me, (*chip, c), src=v_ref) for j, chip in enumerate(chips)]
        for cp in first:
            cp.start()
        passed = [copy(4 + j, (*chip, c), sibling) for j, chip in enumerate(chips)]
        for j, chip in enumerate(chips):
            copy(1 + j, (*chip, c), me).wait_recv()
            passed[j].start()
        copy(0, sibling, me).wait_recv()
        for j, chip in enumerate(chips):
            copy(4 + j, (*chip, 1 - c), me).wait_recv()
        for cp in first + passed:
            cp.wait_send()
        mine.wait()
        if reduce:
            total = out_ref[0]
            for d in range(1, N_DEV):
                total = total + out_ref[d]
            rest[0][...] = total

    vm = pl.BlockSpec(memory_space=pltpu.VMEM)
    out_shape = [jax.ShapeDtypeStruct((N_DEV, r, LANES), F32)] + ([jax.ShapeDtypeStruct((r, LANES), F32)] if reduce else [])
    return pl.pallas_call(
        body, name=name, in_specs=[vm], out_specs=[vm] * len(out_shape), out_shape=out_shape,
        scratch_shapes=[pltpu.SemaphoreType.DMA((7,)), pltpu.SemaphoreType.DMA((7,)), pltpu.SemaphoreType.DMA],
        compiler_params=pltpu.CompilerParams(vmem_limit_bytes=VMEM_LIMIT),
    )(v)


REDUCE_BLOCK_BYTES = 2 << 20


def _reduce_rows(r, c):
    cands = [b for b in range(16, r + 1, 16) if r % b == 0 and b * c * 4 <= REDUCE_BLOCK_BYTES]
    return max(cands)


def _add_halves(mine2, other, c_idx, *, name):
    _, nq, r, c = mine2.shape
    br = _reduce_rows(r, c)

    def body(c_ref, a_ref, b_ref, o_ref):
        o_ref[...] = (a_ref[0] + b_ref[...]).astype(o_ref.dtype)

    blk = pl.BlockSpec((1, br, c), lambda q, i, c_ref: (q, i, 0))
    return pl.pallas_call(
        body, name=name,
        grid_spec=pltpu.PrefetchScalarGridSpec(
            num_scalar_prefetch=1, grid=(nq, r // br),
            in_specs=[pl.BlockSpec((1, 1, br, c), lambda q, i, c_ref: (c_ref[0], q, i, 0)), blk],
            out_specs=blk),
        out_shape=jax.ShapeDtypeStruct((nq, r, c), BF16),
        compiler_params=_params(("parallel", "parallel")),
    )(c_idx, mine2, other)


def _sum_chips(parts, *, name):
    _, r, c = parts.shape
    br = _reduce_rows(r, c)

    def body(p_ref, o_ref):
        total = p_ref[0].astype(F32)
        for q in range(1, 4):
            total = total + p_ref[q].astype(F32)
        o_ref[...] = total

    return pl.pallas_call(
        body, name=name, grid=(r // br,),
        in_specs=[pl.BlockSpec((4, br, c), lambda i: (0, i, 0))],
        out_specs=pl.BlockSpec((br, c), lambda i: (i, 0)),
        out_shape=jax.ShapeDtypeStruct((r, c), F32),
        compiler_params=_params(("parallel",)),
    )(parts)


BIG = ("w_in", "w_out_sb", "w_out_ssd", "w_out_rw", "w_o")
BIG_AXIS = {"w_in": 2, "w_out_sb": 2, "w_out_ssd": 1, "w_out_rw": 2, "w_o": 1}
SMALL_SHARDED = {"conv_w": 320, "rw_w_up": 128, "rw_a_up": 128}
SMALL = ("norm_g", "conv_w", "conv_b", "dt_bias", "a_log", "d_skip", "ssd_norm_g", "rw_mu", "rw_w0", "rw_w_up",
         "rw_a0", "rw_a_up", "rw_k_k", "rw_k_a", "rw_r_k", "rw_ln_g", "rw_ln_b", "final_g")


def _rows_of(a):
    flat = a.reshape(-1)
    pad = (-flat.shape[0]) % LANES
    return jnp.pad(flat, (0, pad)).reshape(-1, LANES)


def _pack_rows(arrays, multiple=8):
    rows = jnp.concatenate([_rows_of(a) for a in arrays], axis=0)
    pad = (-rows.shape[0]) % multiple
    return jnp.pad(rows, ((0, pad), (0, 0)))


def _unpack_rows(rows, shapes):
    out, off = [], 0
    for shp in shapes:
        n = 1
        for d in shp:
            n *= d
        nr = -(-n // LANES)
        out.append(rows[off:off + nr].reshape(-1)[:n].reshape(shp))
        off += nr
    return out


COL_MAP = ((0, 3072, 0), (3072, 4352, C_XBC), (4352, 4368, C_DT), (4368, 6544, C_RW), (6544, 9616, C_GATES))
SHARD_COLS = N_IN // 4


def _w_in_from_shards(shards):
    pieces = []
    for a, b, dst in sorted(COL_MAP, key=lambda m: m[2]):
        if pieces and dst > pieces[-1][0]:
            pieces.append((dst, jnp.zeros((shards[0].shape[0], dst - pieces[-1][0]), shards[0].dtype)))
        for q in range(4):
            lo, hi = max(a, q * SHARD_COLS), min(b, (q + 1) * SHARD_COLS)
            if lo < hi:
                pieces.append((dst + hi - a, shards[q][:, lo - q * SHARD_COLS:hi - q * SHARD_COLS]))
    return jnp.concatenate([p for _, p in pieces], axis=1)


def _w_in_shard(g, q):
    pieces = []
    for a, b, dst in COL_MAP:
        lo, hi = max(a, q * SHARD_COLS), min(b, (q + 1) * SHARD_COLS)
        if lo < hi:
            pieces.append(g[:, dst + lo - a:dst + hi - a])
    return jnp.concatenate(pieces, axis=1)


def _split_chips(a, axis):
    n = a.shape[axis] // 4
    return jnp.stack([lax.slice_in_dim(a, q * n, (q + 1) * n, axis=axis) for q in range(4)])


def _join_chips(a, axis):
    return jnp.concatenate([a[q] for q in range(4)], axis=axis)


def kernel(x, norm_g, w_in, conv_w, conv_b, dt_bias, a_log, d_skip, ssd_norm_g, rw_mu, rw_w0, rw_w_up, rw_a0, rw_a_up, rw_k_k, rw_k_a, rw_r_k, rw_ln_g, rw_ln_b, w_out_sb, w_out_ssd, w_out_rw, w_o, final_g, loss_target, m_norm_g, m_w_in, m_conv_w, m_conv_b, m_dt_bias, m_a_log, m_d_skip, m_ssd_norm_g, m_rw_mu, m_rw_w0, m_rw_w_up, m_rw_a0, m_rw_a_up, m_rw_k_k, m_rw_k_a, m_rw_r_k, m_rw_ln_g, m_rw_ln_b, m_w_out_sb, m_w_out_ssd, m_w_out_rw, m_w_o, m_final_g, v_norm_g, v_w_in, v_conv_w, v_conv_b, v_dt_bias, v_a_log, v_d_skip, v_ssd_norm_g, v_rw_mu, v_rw_w0, v_rw_w_up, v_rw_a0, v_rw_a_up, v_rw_k_k, v_rw_k_a, v_rw_r_k, v_rw_ln_g, v_rw_ln_b, v_w_out_sb, v_w_out_ssd, v_w_out_rw, v_w_o, v_final_g):
    names = ("norm_g", "w_in", "conv_w", "conv_b", "dt_bias", "a_log", "d_skip", "ssd_norm_g", "rw_mu", "rw_w0",
             "rw_w_up", "rw_a0", "rw_a_up", "rw_k_k", "rw_k_a", "rw_r_k", "rw_ln_g", "rw_ln_b", "w_out_sb",
             "w_out_ssd", "w_out_rw", "w_o", "final_g")
    w_loc = dict(zip(names, (norm_g, w_in, conv_w, conv_b, dt_bias, a_log, d_skip, ssd_norm_g, rw_mu, rw_w0, rw_w_up,
                             rw_a0, rw_a_up, rw_k_k, rw_k_a, rw_r_k, rw_ln_g, rw_ln_b, w_out_sb, w_out_ssd, w_out_rw,
                             w_o, final_g)))
    m_loc = dict(zip(names, (m_norm_g, m_w_in, m_conv_w, m_conv_b, m_dt_bias, m_a_log, m_d_skip, m_ssd_norm_g,
                             m_rw_mu, m_rw_w0, m_rw_w_up, m_rw_a0, m_rw_a_up, m_rw_k_k, m_rw_k_a, m_rw_r_k,
                             m_rw_ln_g, m_rw_ln_b, m_w_out_sb, m_w_out_ssd, m_w_out_rw, m_w_o, m_final_g)))
    v_loc = dict(zip(names, (v_norm_g, v_w_in, v_conv_w, v_conv_b, v_dt_bias, v_a_log, v_d_skip, v_ssd_norm_g,
                             v_rw_mu, v_rw_w0, v_rw_w_up, v_rw_a0, v_rw_a_up, v_rw_k_k, v_rw_k_a, v_rw_r_k,
                             v_rw_ln_g, v_rw_ln_b, v_w_out_sb, v_w_out_ssd, v_w_out_rw, v_w_o, v_final_g)))
    chip = 2 * lax.axis_index("x") + lax.axis_index("y")
    core = lax.axis_index("c")

    got_mine = _chip_exchange([w_loc[n].astype(BF16) for n in BIG], per_dest=False, name="gather_big")
    got_theirs = _sibling_swap(got_mine, other_slot=False, name="gather_join")
    full = {}
    for n, mine, theirs in zip(BIG, got_mine, got_theirs):
        layers = [jnp.where(core == l, mine, theirs) for l in range(DEPTH)]
        if n == "w_in":
            full[n] = jnp.stack([_w_in_from_shards([lay[q] for q in range(4)]) for lay in layers])
        else:
            full[n] = jnp.stack([jnp.concatenate([lay[q] for q in range(4)], axis=BIG_AXIS[n] - 1) for lay in layers])
    sm_names = tuple(SMALL_SHARDED)
    sm_shapes = [w_loc[n].shape for n in sm_names]
    (got_sm,) = _allgather_small(_pack_rows([w_loc[n] for n in sm_names]), reduce=False, name="gather_small")
    per_chip = [_unpack_rows(got_sm[4 * (q // 2) + 2 * (q % 2)], sm_shapes) for q in range(4)]
    for i, n in enumerate(sm_names):
        full[n] = jnp.concatenate([per_chip[q][i] for q in range(4)], axis=-1)

    def pad16(a):
        return jnp.zeros((1, LANES), F32).at[0, :SSD_HEADS].set(a)

    def layer_params(i):
        row = lambda n: w_loc[n][i].reshape(1, -1)
        cw = full["conv_w"][i]
        return dict(
            norm_g=row("norm_g"), w_in=full["w_in"][i], conv=[cw[k][None] for k in range(4)] + [row("conv_b")],
            dt_bias=pad16(dt_bias[i]), a_log=pad16(a_log[i]), d_skip=pad16(d_skip[i]),
            ssd_norm_g=row("ssd_norm_g"), rw_mu=row("rw_mu"),
            rw_pre=[row("rw_w0"), jnp.zeros((LANES, 512), F32).at[:HEAD].set(full["rw_w_up"][i]), row("rw_a0"),
                    jnp.zeros((LANES, 512), F32).at[HEAD:].set(full["rw_a_up"][i]), row("rw_k_k"), row("rw_k_a")],
            rw_post=[row("rw_ln_g"), row("rw_ln_b"), row("rw_r_k")],
            w_out_sb=full["w_out_sb"][i], w_out_ssd=full["w_out_ssd"][i], w_out_rw=full["w_out_rw"][i],
            w_o=full["w_o"][i])

    params = [layer_params(i) for i in range(DEPTH)]
    xs, saved = [x[0]], []
    for i in range(DEPTH):
        nxt, s = _layer_fwd(xs[-1], params[i], f"l{i}_")
        xs.append(nxt)
        saved.append(s)
    dx, loss_row, g_final = _final(xs[-1], final_g.reshape(1, -1), loss_target[0], bt=BT, name="final")
    grads = [None] * DEPTH
    for i in reversed(range(DEPTH)):
        dx, grads[i] = _layer_bwd(xs[i], dx, params[i], saved[i], f"l{i}_")

    def stacked(fn):
        return jnp.stack([fn(grads[i]) for i in range(DEPTH)])

    g_loc = {
        "norm_g": stacked(lambda g: g["norm_g"][0]),
        "w_in": stacked(lambda g: g["w_in"]),
        "conv_w": stacked(lambda g: jnp.concatenate(g["conv"][:4], axis=0)),
        "conv_b": stacked(lambda g: g["conv"][4][0]),
        "dt_bias": stacked(lambda g: g["dt_bias"][0, :SSD_HEADS]),
        "a_log": stacked(lambda g: g["a_log"][0, :SSD_HEADS]),
        "d_skip": stacked(lambda g: g["d_skip"][0, :SSD_HEADS]),
        "ssd_norm_g": stacked(lambda g: g["ssd_norm_g"][0]),
        "rw_mu": stacked(lambda g: g["rw_mu"][0]),
        "rw_w0": stacked(lambda g: g["rw_pre"][0][0]),
        "rw_w_up": stacked(lambda g: g["rw_pre"][1][:HEAD]),
        "rw_a0": stacked(lambda g: g["rw_pre"][2][0]),
        "rw_a_up": stacked(lambda g: g["rw_pre"][3][HEAD:]),
        "rw_k_k": stacked(lambda g: g["rw_pre"][4][0]),
        "rw_k_a": stacked(lambda g: g["rw_pre"][5][0]),
        "rw_r_k": stacked(lambda g: g["rw_r_k"].reshape(8, HEAD)),
        "rw_ln_g": stacked(lambda g: g["rw_ln_g"][0]),
        "rw_ln_b": stacked(lambda g: g["rw_ln_b"][0]),
        "w_out_sb": stacked(lambda g: g["w_out_sb"]),
        "w_out_ssd": stacked(lambda g: g["w_out_ssd"]),
        "w_out_rw": stacked(lambda g: g["w_out_rw"]),
        "w_o": stacked(lambda g: g["w_o"]),
        "final_g": g_final[0],
    }

    sends = [jnp.stack([jnp.stack([_w_in_shard(g_loc[n][l], q) for q in range(4)]) for l in range(DEPTH)])
             if n == "w_in" else jnp.swapaxes(_split_chips(g_loc[n], BIG_AXIS[n]), 0, 1) for n in BIG]
    others = _sibling_swap(sends, other_slot=True, name="reduce_sibling")
    c_idx = core.reshape(1).astype(jnp.int32)
    parts = [_add_halves(s, o, c_idx, name="reduce_add_" + n) for n, s, o in zip(BIG, sends, others)]
    parts = _chip_exchange(parts, per_dest=True, name="reduce_chips")
    mine = [_sum_chips(p, name="reduce_sum_" + n) for n, p in zip(BIG, parts)]
    theirs = _sibling_swap(mine, other_slot=False, name="reduce_join")
    g_out = {n: jnp.stack([jnp.where(core == 0, a, b), jnp.where(core == 0, b, a)])
             for n, a, b in zip(BIG, mine, theirs)}

    sm_all = SMALL + ("loss",)
    sm_full_shapes = [g_loc[n].shape for n in SMALL] + [(1,)]
    _, summed = _allgather_small(_pack_rows([g_loc[n] for n in SMALL] + [loss_row[0, :1]]), reduce=True, name="reduce_small")
    sm = dict(zip(sm_all, _unpack_rows(summed, sm_full_shapes)))
    for n in SMALL:
        g_out[n] = sm[n]
    for n, wd in SMALL_SHARDED.items():
        g_out[n] = lax.dynamic_slice_in_dim(sm[n], chip * wd, wd, axis=sm[n].ndim - 1)
    loss = sm["loss"][0]

    upd = {n: _adamw(w_loc[n], g_out[n], m_loc[n], v_loc[n], name="adamw_" + n) for n in names}
    return (loss, dx[None], *[g_out[n] for n in names], *[upd[n][0] for n in names],
            *[upd[n][1] for n in names], *[upd[n][2] for n in names])
```

```python
import functools

import jax
import jax.numpy as jnp
from jax import lax
from jax.experimental import pallas as pl
from jax.experimental.pallas import tpu as pltpu

F32 = jnp.float32
BF16 = jnp.bfloat16

D_MODEL = 1024
DEPTH = 2
HEAD = 64
LANES = 128
CHUNK = 128
RMS_EPS = 1e-6
GN_EPS = 64e-5
VMEM_LIMIT = 56 * 1024 * 1024

N_IN = 9616
N_PAD = 9728
C_SB, C_Z, C_GATES, C_RW, C_LO, C_DT, C_XBC = 0, 2048, 3072, 6144, 8192, 8320, 8448
RW_COLS = 2176
XBC_COLS = 1280

ADAM_LR, ADAM_B1, ADAM_B2, ADAM_EPS, ADAM_WD, ADAM_STEP = 0.001, 0.9, 0.999, 1e-08, 0.01, 10


def _params(sem=None):
    return pltpu.CompilerParams(dimension_semantics=sem, vmem_limit_bytes=VMEM_LIMIT)


@jax.custom_vjp
def _sigmoid(x):
    return 1.0 / (1.0 + jnp.exp(-x))


def _sigmoid_fwd(x):
    s = _sigmoid(x)
    return s, s


def _sigmoid_bwd(s, g):
    return (g * s * (1.0 - s),)


_sigmoid.defvjp(_sigmoid_fwd, _sigmoid_bwd)


@jax.custom_vjp
def _silu(x):
    return x * _sigmoid(x)


def _silu_fwd(x):
    s = _sigmoid(x)
    return x * s, (x, s)


def _silu_bwd(res, g):
    x, s = res
    return (g * (s + x * s * (1.0 - s)),)


_silu.defvjp(_silu_fwd, _silu_bwd)


@jax.custom_vjp
def _softplus(x):
    return jnp.maximum(x, 0.0) + jnp.log(1.0 + jnp.exp(-jnp.abs(x)))


def _softplus_fwd(x):
    return _softplus(x), x


def _softplus_bwd(x, g):
    return (g * _sigmoid(x),)


_softplus.defvjp(_softplus_fwd, _softplus_bwd)


def _dot(a, b, dims):
    return lax.dot_general(a.astype(BF16), b.astype(BF16), (dims, ((), ())), preferred_element_type=F32)


def _dot_nn(a, b):
    return _dot(a, b, ((1,), (0,)))


def _dot_nt(a, b):
    return _dot(a, b, ((1,), (1,)))


def _dot_tn(a, b):
    return _dot(a, b, ((0,), (0,)))


@jax.custom_vjp
def _bdot(a, b):
    return _dot_nn(a, b)


def _bdot_fwd(a, b):
    return _dot_nn(a, b), (a, b)


def _bdot_bwd(res, g):
    a, b = res
    return _dot_nt(g, b), _dot_tn(a, g)


_bdot.defvjp(_bdot_fwd, _bdot_bwd)


def _split2(x):
    hi = x.astype(BF16)
    lo = (x - hi.astype(F32)).astype(BF16)
    return hi, lo


_NT = (((1,), (1,)), ((), ()))
_NN = (((1,), (0,)), ((), ()))
_TN = (((0,), (0,)), ((), ()))


def _dot2(x, m, dn=_NN):
    hi, lo = _split2(x)
    return (lax.dot_general(hi, m, dn, preferred_element_type=F32)
            + lax.dot_general(lo, m, dn, preferred_element_type=F32))


def _dot2_tn(x, m):
    return _dot2(x, m, _TN)


def _seg_matrix(n):
    r = lax.broadcasted_iota(jnp.int32, (n, n), 0) // HEAD
    c = lax.broadcasted_iota(jnp.int32, (n, n), 1) // HEAD
    return (r == c).astype(BF16)


@jax.custom_vjp
def _segsum2(x, seg):
    return _dot2(x, seg)


def _segsum2_fwd(x, seg):
    return _dot2(x, seg), seg


def _segsum2_bwd(seg, g):
    return _dot2(g, seg), jnp.zeros_like(seg)


_segsum2.defvjp(_segsum2_fwd, _segsum2_bwd)


def _make_segsum(seg):
    return lambda x: _segsum2(x, seg)


def _shift_down_raw(x, k):
    row = lax.broadcasted_iota(jnp.int32, x.shape, 0)
    return jnp.where(row >= k, pltpu.roll(x, k, 0), 0.0)


def _shift_up_raw(x, k):
    t = x.shape[0]
    row = lax.broadcasted_iota(jnp.int32, x.shape, 0)
    return jnp.where(row < t - k, pltpu.roll(x, t - k, 0), 0.0)


@functools.partial(jax.custom_vjp, nondiff_argnums=(1,))
def _shift_down(x, k):
    return _shift_down_raw(x, k)


def _shift_down_fwd(x, k):
    return _shift_down_raw(x, k), None


def _shift_down_bwd(k, _, g):
    return (_shift_up_raw(g, k),)


_shift_down.defvjp(_shift_down_fwd, _shift_down_bwd)


def _mm(a, b, *, name, ta=False, tb=False, add=None, out_dtype=F32, tm=2048, tn=512, tk=None):
    m, k = (a.shape[1], a.shape[0]) if ta else a.shape
    n = b.shape[0] if tb else b.shape[1]
    tm, tn = min(tm, m), min(tn, n)
    tk = k if tk is None else tk
    nk = k // tk
    assert m % tm == 0 and n % tn == 0 and k % tk == 0
    dims = ((0 if ta else 1,), (1 if tb else 0,))

    def body(a_ref, b_ref, *refs):
        o_ref, acc_ref = refs[-2:]
        p = _dot(a_ref[...], b_ref[...], dims)

        def emit(total):
            if add is not None:
                total = total + refs[0][...]
            o_ref[...] = total.astype(o_ref.dtype)

        if nk == 1:
            emit(p)
        else:
            kk = pl.program_id(2)

            @pl.when(kk == 0)
            def _():
                acc_ref[...] = p

            @pl.when(kk > 0)
            def _():
                acc_ref[...] += p

            @pl.when(kk == nk - 1)
            def _():
                emit(acc_ref[...])

    a_spec = pl.BlockSpec((tk, tm), lambda i, j, kk: (kk, i)) if ta else pl.BlockSpec((tm, tk), lambda i, j, kk: (i, kk))
    b_spec = pl.BlockSpec((tn, tk), lambda i, j, kk: (j, kk)) if tb else pl.BlockSpec((tk, tn), lambda i, j, kk: (kk, j))
    o_spec = pl.BlockSpec((tm, tn), lambda i, j, kk: (i, j))
    return pl.pallas_call(
        body, name=name, grid=(m // tm, n // tn, nk),
        in_specs=[a_spec, b_spec] + ([o_spec] if add is not None else []), out_specs=o_spec,
        out_shape=jax.ShapeDtypeStruct((m, n), out_dtype),
        scratch_shapes=[pltpu.VMEM((tm, tn) if nk > 1 else (8, LANES), F32)],
        compiler_params=_params(("parallel", "parallel", "arbitrary")),
    )(a, b, *([add] if add is not None else []))


def _row_specs(rows, bt):
    return [pl.BlockSpec((bt, w), functools.partial(lambda i, c: (i, c), c=c)) for _, w, c in rows]


def _full_spec(p):
    return pl.BlockSpec(p.shape, functools.partial(lambda i, nd: (0,) * nd, nd=p.ndim))


def _rowwise(f, rows, pars, out_widths, *, bt, name, acc_widths=()):
    t = rows[0][0].shape[0]
    nr, npar, no, na = len(rows), len(pars), len(out_widths), len(acc_widths)

    def body(*refs):
        vals = [r[...] for r in refs[:nr + npar]]
        outs = f(*vals)
        for o_ref, o in zip(refs[nr + npar:nr + npar + no], outs[:no]):
            o_ref[...] = o.astype(o_ref.dtype)
        if na:
            first = pl.program_id(0) == 0
            for a_ref, a in zip(refs[nr + npar + no:], outs[no:]):
                @pl.when(first)
                def _():
                    a_ref[...] = jnp.zeros_like(a_ref)
                a_ref[...] += a

    return pl.pallas_call(
        body, name=name, grid=(t // bt,),
        in_specs=_row_specs(rows, bt) + [_full_spec(p) for p in pars],
        out_specs=[pl.BlockSpec((bt, w), lambda i: (i, 0)) for w in out_widths]
        + [pl.BlockSpec((1, w), lambda i: (0, 0)) for w in acc_widths],
        out_shape=[jax.ShapeDtypeStruct((t, w), F32) for w in out_widths]
        + [jax.ShapeDtypeStruct((1, w), F32) for w in acc_widths],
        compiler_params=_params(("arbitrary",)),
    )(*[r[0] for r in rows], *pars)


def _rowwise_bwd(f, rows, pars, douts, *, bt, name, groups=None):
    t = rows[0][0].shape[0]
    nr, npar, nd = len(rows), len(pars), len(douts)
    groups = [[i] for i in range(nr)] if groups is None else groups
    widths = [r[1] for r in rows]

    def body(*refs):
        vals = [r[...] for r in refs[:nr + npar]]
        cts = tuple(r[...] for r in refs[nr + npar:nr + npar + nd])
        _, vjp = jax.vjp(lambda *a: tuple(f(*a)), *vals)
        grads = vjp(cts)
        out_refs = refs[nr + npar + nd:]
        for g_ref, grp in zip(out_refs[:len(groups)], groups):
            off = 0
            for i in grp:
                g_ref[:, off:off + widths[i]] = grads[i]
                off += widths[i]
        first = pl.program_id(0) == 0
        for p_ref, g in zip(out_refs[len(groups):], grads[nr:]):
            @pl.when(first)
            def _():
                p_ref[...] = jnp.zeros_like(p_ref)
            p_ref[...] += g

    gw = [sum(widths[i] for i in grp) for grp in groups]
    return pl.pallas_call(
        body, name=name, grid=(t // bt,),
        in_specs=_row_specs(rows, bt) + [_full_spec(p) for p in pars] + _row_specs(douts, bt),
        out_specs=[pl.BlockSpec((bt, w), lambda i: (i, 0)) for w in gw] + [_full_spec(p) for p in pars],
        out_shape=[jax.ShapeDtypeStruct((t, w), F32) for w in gw] + [jax.ShapeDtypeStruct(p.shape, F32) for p in pars],
        compiler_params=_params(("arbitrary",)),
    )(*[r[0] for r in rows], *pars, *[d[0] for d in douts])


def _colwise(f, x, c0, ncols, pars, *, bc, name):
    t = x.shape[0]

    def body(x_ref, *refs):
        o_ref = refs[-1]
        o_ref[...] = f(x_ref[...], *[r[...] for r in refs[:-1]])

    return pl.pallas_call(
        body, name=name, grid=(ncols // bc,),
        in_specs=[pl.BlockSpec((t, bc), lambda j: (0, j + c0 // bc))]
        + [pl.BlockSpec((p.shape[0], bc), lambda j: (0, j)) for p in pars],
        out_specs=pl.BlockSpec((t, bc), lambda j: (0, j)),
        out_shape=jax.ShapeDtypeStruct((t, ncols), F32),
        compiler_params=_params(("parallel",)),
    )(x, *pars)


def _colwise_bwd(f, x, c0, ncols, pars, dout, *, bc, name):
    t = x.shape[0]
    npar = len(pars)

    def body(x_ref, *refs):
        vals = [x_ref[...]] + [r[...] for r in refs[:npar]]
        _, vjp = jax.vjp(f, *vals)
        grads = vjp(refs[npar][...])
        for g_ref, g in zip(refs[npar + 1:], grads):
            g_ref[...] = g

    return pl.pallas_call(
        body, name=name, grid=(ncols // bc,),
        in_specs=[pl.BlockSpec((t, bc), lambda j: (0, j + c0 // bc))]
        + [pl.BlockSpec((p.shape[0], bc), lambda j: (0, j)) for p in pars]
        + [pl.BlockSpec((t, bc), lambda j: (0, j))],
        out_specs=[pl.BlockSpec((t, bc), lambda j: (0, j))]
        + [pl.BlockSpec((p.shape[0], bc), lambda j: (0, j)) for p in pars],
        out_shape=[jax.ShapeDtypeStruct((t, ncols), F32)] + [jax.ShapeDtypeStruct(p.shape, F32) for p in pars],
        compiler_params=_params(("parallel",)),
    )(x, *pars, dout)


def _f_rms(x, g):
    return (x * lax.rsqrt(jnp.mean(x * x, axis=-1, keepdims=True) + RMS_EPS) * g,)


def _f_sb_gate(y, gate):
    return (y * _silu(gate),)


def _f_ssd_norm(y, z, g):
    u = y * _silu(z)
    return (u * lax.rsqrt(jnp.mean(u * u, axis=-1, keepdims=True) + RMS_EPS) * g,)


def _f_merge(p_sb, p_ssd, p_rw, g_sb, g_ssd, g_rw):
    return (_sigmoid(g_sb) * p_sb + _sigmoid(g_ssd) * p_ssd + _sigmoid(g_rw) * p_rw,)


def _f_rw_pre(k, lo, w0, w_up, a0, a_up, k_k, k_a):
    segsum = _make_segsum(_seg_matrix(k.shape[1]))
    lane = lax.broadcasted_iota(jnp.int32, lo.shape, 1)
    w_lo = jnp.where(lane < HEAD, jnp.tanh(lo), 0.0)
    a_lo = jnp.where(lane >= HEAD, lo, 0.0)
    w = -_softplus(-(w0 + _bdot(w_lo, w_up))) - 0.5
    log_decay = -jnp.exp(w)
    a = _sigmoid(a0 + _bdot(a_lo, a_up))
    kk = k * k_k
    kk = kk / jnp.maximum(jnp.sqrt(segsum(kk * kk)), 1e-12)
    return log_decay, k * (1.0 + (a - 1.0) * k_a), -kk, kk * a


def _f_rw_post(y, r, k2, v, gate, ln_g, ln_b, r_k):
    segsum = _make_segsum(_seg_matrix(y.shape[1]))
    yc = y - segsum(y) * (1.0 / HEAD)
    var = segsum(yc * yc) * (1.0 / HEAD)
    yn = yc * lax.rsqrt(var + GN_EPS) * ln_g + ln_b
    return ((yn + segsum(r * k2 * r_k) * v) * _silu(gate),)


def _f_rw_mix(slab, mu):
    return slab + (_shift_down(slab, 1) - slab) * mu


def _f_conv(x, w0, w1, w2, w3, b):
    acc = x * w3 + b
    for i, w in enumerate((w0, w1, w2)):
        acc = acc + _shift_down(x, 3 - i) * w
    return _silu(acc)


def _log_sigmoid(z):
    return jnp.minimum(z, 0.0) - jnp.log(1.0 + jnp.exp(-jnp.abs(z)))


def _prefix_matrix(kind):
    j = lax.broadcasted_iota(jnp.int32, (CHUNK, 2 * CHUNK), 0)
    s = lax.broadcasted_iota(jnp.int32, (CHUNK, 2 * CHUNK), 1)
    tri = {"gt": j > s, "le": j <= s, "lt": j < s}[kind]
    return (tri | (s >= CHUNK)).astype(BF16)


def _sb_specs(t):
    q = pl.BlockSpec((CHUNK, LANES), lambda j, i: (i, j))
    k = pl.BlockSpec((t, LANES), lambda j, i: (0, 4 + j))
    v = pl.BlockSpec((t, LANES), lambda j, i: (0, 8 + j))
    return q, k, v


def _sb_fwd(proj, *, name):
    t = proj.shape[0]
    scale = HEAD ** -0.5

    def body(q_ref, k_ref, v_ref, y_ref, lt_ref):
        i = pl.program_id(1)
        lane = lax.broadcasted_iota(jnp.int32, (CHUNK, LANES), 1)
        diff = (lax.broadcasted_iota(jnp.int32, (CHUNK, CHUNK), 1)
                - lax.broadcasted_iota(jnp.int32, (CHUNK, CHUNK), 0))
        m_f = _prefix_matrix("gt")
        q = q_ref[...] * scale
        qh = [jnp.where((lane // HEAD) == h, q, 0.0).astype(BF16) for h in (0, 1)]

        def step(it, carry):
            off = pl.multiple_of((i - it) * CHUNK, CHUNK)
            kblk = k_ref[pl.ds(off, CHUNK), :].astype(BF16)
            vblk = v_ref[pl.ds(off, CHUNK), :].astype(BF16)
            mask = diff < it * CHUNK
            new = []
            for h in (0, 1):
                c, acc = carry[2 * h], carry[2 * h + 1]
                z = lax.dot_general(qh[h], kblk, _NT, preferred_element_type=F32)
                lb = _log_sigmoid(z)
                w2 = _dot2(jnp.where(mask, lb - z, 0.0), m_f)
                att = jnp.where(mask, jnp.exp(lb + c + w2[:, :CHUNK]), 0.0)
                acc = acc + lax.dot_general(att.astype(BF16), vblk, _NN, preferred_element_type=F32)
                new += [c + w2[:, CHUNK:], acc]
            return tuple(new)

        zero = jnp.zeros((CHUNK, LANES), F32)
        c_a, acc_a, c_b, acc_b = lax.fori_loop(0, i + 1, step, (zero, zero, zero, zero))
        y_ref[...] = jnp.where(lane < HEAD, acc_a, acc_b)
        lt_ref[0] = c_a
        lt_ref[1] = c_b

    return pl.pallas_call(
        body, name=name, grid=(4, t // CHUNK),
        in_specs=list(_sb_specs(t)),
        out_specs=[pl.BlockSpec((CHUNK, LANES), lambda j, i: (i, j)),
                   pl.BlockSpec((2, CHUNK, LANES), lambda j, i: (j, i, 0))],
        out_shape=[jax.ShapeDtypeStruct((t, 4 * LANES), F32), jax.ShapeDtypeStruct((8, t, LANES), F32)],
        compiler_params=_params(("parallel", "arbitrary")),
    )(proj, proj, proj)


def _sb_bwd(proj, dy, lt, *, name):
    t = proj.shape[0]
    scale = HEAD ** -0.5

    def body(q_ref, k_ref, v_ref, dy_ref, lt_ref, dq_ref, dk_ref, dv_ref):
        i = pl.program_id(1)

        @pl.when(i == 0)
        def _():
            dk_ref[...] = jnp.zeros_like(dk_ref)
            dv_ref[...] = jnp.zeros_like(dv_ref)

        lane = lax.broadcasted_iota(jnp.int32, (CHUNK, LANES), 1)
        diff = (lax.broadcasted_iota(jnp.int32, (CHUNK, CHUNK), 1)
                - lax.broadcasted_iota(jnp.int32, (CHUNK, CHUNK), 0))
        m_le, m_lt = _prefix_matrix("le"), _prefix_matrix("lt")
        q = q_ref[...] * scale
        dy_blk = dy_ref[...]
        qh = [jnp.where((lane // HEAD) == h, q, 0.0).astype(BF16) for h in (0, 1)]
        doh = [jnp.where((lane // HEAD) == h, dy_blk, 0.0).astype(BF16) for h in (0, 1)]
        lth = [lt_ref[0], lt_ref[1]]

        def step(kb, carry):
            off = pl.multiple_of(kb * CHUNK, CHUNK)
            kblk = k_ref[pl.ds(off, CHUNK), :].astype(BF16)
            vblk = v_ref[pl.ds(off, CHUNK), :].astype(BF16)
            mask = diff < (i - kb) * CHUNK
            new = []
            dk_acc = jnp.zeros((CHUNK, LANES), F32)
            dv_acc = jnp.zeros((CHUNK, LANES), F32)
            for h in (0, 1):
                cp, cg, dq = carry[3 * h:3 * h + 3]
                z = lax.dot_general(qh[h], kblk, _NT, preferred_element_type=F32)
                lb = _log_sigmoid(z)
                w2 = _dot2(jnp.where(mask, lb - z, 0.0), m_le)
                att = jnp.where(mask, jnp.exp(lb + lth[h] - cp - w2[:, :CHUNK]), 0.0)
                d_att = lax.dot_general(doh[h], vblk, _NT, preferred_element_type=F32)
                d_e = d_att * att
                g2 = _dot2(d_e, m_lt)
                sig = jnp.exp(lb)
                dz = jnp.where(mask, d_e * (1.0 - sig) - (cg + g2[:, :CHUNK]) * sig, 0.0).astype(BF16)
                dq = dq + lax.dot_general(dz, kblk, _NN, preferred_element_type=F32)
                dk_acc = dk_acc + lax.dot_general(dz, qh[h], _TN, preferred_element_type=F32)
                dv_acc = dv_acc + lax.dot_general(att.astype(BF16), doh[h], _TN, preferred_element_type=F32)
                new += [cp + w2[:, CHUNK:], cg + g2[:, CHUNK:], dq]
            dk_ref[pl.ds(off, CHUNK), :] += dk_acc
            dv_ref[pl.ds(off, CHUNK), :] += dv_acc
            return tuple(new)

        zero = jnp.zeros((CHUNK, LANES), F32)
        out = lax.fori_loop(0, i + 1, step, (zero,) * 6)
        dq_ref[...] = jnp.where(lane < HEAD, out[2], out[5]) * scale

    q_spec, k_spec, v_spec = _sb_specs(t)
    blk = pl.BlockSpec((CHUNK, LANES), lambda j, i: (i, j))
    col = pl.BlockSpec((t, LANES), lambda j, i: (0, j))
    return pl.pallas_call(
        body, name=name, grid=(4, t // CHUNK),
        in_specs=[q_spec, k_spec, v_spec, blk, pl.BlockSpec((2, CHUNK, LANES), lambda j, i: (j, i, 0))],
        out_specs=[blk, col, col],
        out_shape=[jax.ShapeDtypeStruct((t, 4 * LANES), F32)] * 3,
        compiler_params=_params(("parallel", "arbitrary")),
    )(proj, proj, proj, dy, lt)


SB_BQ = 256
SB_BK = 256
assert SB_BQ == SB_BK


def _tri_ones(kind):
    j = lax.broadcasted_iota(jnp.int32, (SB_BK, SB_BK + LANES), 0)
    s = lax.broadcasted_iota(jnp.int32, (SB_BK, SB_BK + LANES), 1)
    tri = {"gt": j > s, "le": j <= s, "lt": j < s}[kind]
    return (tri | (s >= SB_BK)).astype(BF16)


def _sb_common(q_ref):
    lane = lax.broadcasted_iota(jnp.int32, (SB_BQ, LANES), 1)
    q = q_ref[...] * (HEAD ** -0.5)
    q2 = jnp.concatenate([jnp.where(lane < HEAD, q, 0.0), jnp.where(lane >= HEAD, q, 0.0)], axis=0).astype(BF16)
    diff = (lax.broadcasted_iota(jnp.int32, (2 * SB_BQ, SB_BK), 1)
            - (lax.broadcasted_iota(jnp.int32, (2 * SB_BQ, SB_BK), 0) & (SB_BQ - 1)))
    return lane, q2, diff


def _rep(x):
    return jnp.concatenate([x] * (SB_BK // LANES), axis=1)


def _sb2_specs(t):
    q = pl.BlockSpec((SB_BQ, LANES), lambda j, i: (i, j))
    k = pl.BlockSpec((t, LANES), lambda j, i: (0, 4 + j))
    v = pl.BlockSpec((t, LANES), lambda j, i: (0, 8 + j))
    return q, k, v


def _sb2_fwd(proj, *, name):
    t = proj.shape[0]

    def body(q_ref, k_ref, v_ref, y_ref, lt_ref):
        i = pl.program_id(1)
        lane, q2, diff = _sb_common(q_ref)
        m_f = _tri_ones("gt")

        def step(kb, carry, diagonal):
            c, acc = carry
            off = pl.multiple_of(kb * SB_BK, SB_BK)
            kblk = k_ref[pl.ds(off, SB_BK), :].astype(BF16)
            vblk = v_ref[pl.ds(off, SB_BK), :].astype(BF16)
            z = lax.dot_general(q2, kblk, _NT, preferred_element_type=F32)
            lb = _log_sigmoid(z)
            lk = jnp.where(diff < 0, lb - z, 0.0) if diagonal else lb - z
            w2 = _dot2(lk, m_f)
            att = jnp.exp(lb + _rep(c) + w2[:, :SB_BK])
            if diagonal:
                att = jnp.where(diff < 0, att, 0.0)
            acc = acc + lax.dot_general(att.astype(BF16), vblk, _NN, preferred_element_type=F32)
            return c + w2[:, SB_BK:], acc

        zero = jnp.zeros((2 * SB_BQ, LANES), F32)
        c, acc = lax.fori_loop(0, i, lambda it, carry: step(i - 1 - it, carry, False), step(i, (zero, zero), True))
        y_ref[...] = jnp.where(lane < HEAD, acc[:SB_BQ], acc[SB_BQ:])
        lt_ref[0] = c[:SB_BQ]
        lt_ref[1] = c[SB_BQ:]

    return pl.pallas_call(
        body, name=name, grid=(4, t // SB_BQ),
        in_specs=list(_sb2_specs(t)),
        out_specs=[pl.BlockSpec((SB_BQ, LANES), lambda j, i: (i, j)),
                   pl.BlockSpec((2, SB_BQ, LANES), lambda j, i: (j, i, 0))],
        out_shape=[jax.ShapeDtypeStruct((t, 4 * LANES), F32), jax.ShapeDtypeStruct((8, t, LANES), F32)],
        compiler_params=_params(("parallel", "arbitrary")),
    )(proj, proj, proj)


def _sb2_bwd(proj, dy, lt, *, name):
    t = proj.shape[0]

    def body(q_ref, k_ref, v_ref, dy_ref, lt_ref, dq_ref, dk_ref, dv_ref):
        i = pl.program_id(1)

        @pl.when(i == 0)
        def _():
            dk_ref[...] = jnp.zeros_like(dk_ref)
            dv_ref[...] = jnp.zeros_like(dv_ref)

        lane, q2, diff = _sb_common(q_ref)
        m_le, m_lt = _tri_ones("le"), _tri_ones("lt")
        dy_blk = dy_ref[...]
        do2 = jnp.concatenate([jnp.where(lane < HEAD, dy_blk, 0.0), jnp.where(lane >= HEAD, dy_blk, 0.0)],
                              axis=0).astype(BF16)
        lt2 = jnp.concatenate([lt_ref[0], lt_ref[1]], axis=0)

        def step(kb, carry, diagonal):
            cp, cg, dq = carry
            off = pl.multiple_of(kb * SB_BK, SB_BK)
            kblk = k_ref[pl.ds(off, SB_BK), :].astype(BF16)
            vblk = v_ref[pl.ds(off, SB_BK), :].astype(BF16)
            z = lax.dot_general(q2, kblk, _NT, preferred_element_type=F32)
            lb = _log_sigmoid(z)
            lk = jnp.where(diff < 0, lb - z, 0.0) if diagonal else lb - z
            w2 = _dot2(lk, m_le)
            att = jnp.exp(lb + _rep(lt2 - cp) - w2[:, :SB_BK])
            if diagonal:
                att = jnp.where(diff < 0, att, 0.0)
            d_e = lax.dot_general(do2, vblk, _NT, preferred_element_type=F32) * att
            g2 = _dot2(d_e, m_lt)
            sig = jnp.exp(lb)
            dz = d_e * (1.0 - sig) - (_rep(cg) + g2[:, :SB_BK]) * sig
            dz = (jnp.where(diff < 0, dz, 0.0) if diagonal else dz).astype(BF16)
            dq = dq + lax.dot_general(dz, kblk, _NN, preferred_element_type=F32)
            dk_ref[pl.ds(off, SB_BK), :] += lax.dot_general(dz, q2, _TN, preferred_element_type=F32)
            dv_ref[pl.ds(off, SB_BK), :] += lax.dot_general(att.astype(BF16), do2, _TN, preferred_element_type=F32)
            return cp + w2[:, SB_BK:], cg + g2[:, SB_BK:], dq

        zero = jnp.zeros((2 * SB_BQ, LANES), F32)
        before = lax.fori_loop(0, i, lambda kb, carry: step(kb, carry, False), (zero, zero, zero))
        _, _, dq = step(i, before, True)
        dq_ref[...] = jnp.where(lane < HEAD, dq[:SB_BQ], dq[SB_BQ:]) * (HEAD ** -0.5)

    q_spec, k_spec, v_spec = _sb2_specs(t)
    blk = pl.BlockSpec((SB_BQ, LANES), lambda j, i: (i, j))
    col = pl.BlockSpec((t, LANES), lambda j, i: (0, j))
    return pl.pallas_call(
        body, name=name, grid=(4, t // SB_BQ),
        in_specs=[q_spec, k_spec, v_spec, blk, pl.BlockSpec((2, SB_BQ, LANES), lambda j, i: (j, i, 0))],
        out_specs=[blk, col, col],
        out_shape=[jax.ShapeDtypeStruct((t, 4 * LANES), F32)] * 3,
        compiler_params=_params(("parallel", "arbitrary")),
    )(proj, proj, proj, dy, lt)


SSD_HEADS = 16
SSD_PAIRS = 8


def _split3(x):
    a = x.astype(BF16)
    r = x - a.astype(F32)
    b = r.astype(BF16)
    return a, b, (r - b.astype(F32)).astype(BF16)


def _dot3(x, m, dn=_NN):
    return sum(lax.dot_general(p, m, dn, preferred_element_type=F32) for p in _split3(x))


def _mdot3(m, x):
    return sum(lax.dot_general(m, p, _NN, preferred_element_type=F32) for p in _split3(x))


def _ssd_common(dtr, dtb, alog, acsx_s, acst_s):
    lane = lax.broadcasted_iota(jnp.int32, (CHUNK, LANES), 1)
    lane1 = lax.broadcasted_iota(jnp.int32, (1, LANES), 1)
    arow = jnp.where(lane1 < SSD_HEADS, -jnp.exp(alog), 0.0)
    dt = jnp.where(lane < SSD_HEADS, _softplus(dtr + dtb), 0.0)
    da = dt * arow
    r = lax.broadcasted_iota(jnp.int32, (CHUNK, CHUNK), 0)
    c = lax.broadcasted_iota(jnp.int32, (CHUNK, CHUNK), 1)
    tril = (r >= c).astype(BF16)
    triu = (r <= c).astype(BF16)
    acs = _mdot3(tril, da)
    acst_s[...] = _dot3(da, triu, _TN)
    eh = lax.broadcasted_iota(jnp.int32, (LANES, 8 * LANES), 0)
    e = (eh == lax.broadcasted_iota(jnp.int32, (LANES, 8 * LANES), 1) // HEAD).astype(BF16)
    eh2 = lax.broadcasted_iota(jnp.int32, (LANES, 16 * LANES), 0)
    e2 = (eh2 == lax.broadcasted_iota(jnp.int32, (LANES, 16 * LANES), 1) // LANES).astype(BF16)
    acsx_s[...] = _dot3(acs, e)
    return dt, arow, _dot3(dt, e), _dot3(acs, e2), e, tril, triu


def _ssd_fwd(xc, proj, dtb, alog, dsk, *, name):
    t = xc.shape[0]
    nc = t // CHUNK

    def body(x_ref, b_ref, c_ref, dtr_ref, dtb_ref, alog_ref, dsk_ref, y_ref, hin_ref, acsx_s, acst_s, h_s):
        @pl.when(pl.program_id(0) == 0)
        def _():
            h_s[...] = jnp.zeros_like(h_s)

        dt, arow, dt_x, acs_b, e, tril, _ = _ssd_common(dtr_ref[...], dtb_ref[...], alog_ref[...], acsx_s, acst_s)
        dsk_x = _dot3(jnp.broadcast_to(dsk_ref[...], (CHUNK, LANES)), e)
        lane = lax.broadcasted_iota(jnp.int32, (CHUNK, LANES), 1)
        causal = (lax.broadcasted_iota(jnp.int32, (CHUNK, CHUNK), 0)
                  >= lax.broadcasted_iota(jnp.int32, (CHUNK, CHUNK), 1))
        for j in range(SSD_PAIRS):
            g = j // 4
            sl = slice(j * LANES, (j + 1) * LANES)
            if j % 4 == 0:
                bg = jnp.where(lane // HEAD == g, b_ref[...], 0.0)
                cg = jnp.where(lane // HEAD == g, c_ref[...], 0.0)
                cb = _dot_nt(cg, bg)
            x = x_ref[:, sl]
            a = acsx_s[:, sl]
            at = acsx_s[CHUNK - 1:CHUNK, sl]
            xdt = x * dt_x[:, sl]
            hin = h_s[j]
            hin_ref[0, j] = hin
            y = jnp.exp(a) * _dot_nn(cg, hin) + x * dsk_x[:, sl]
            h_s[j] = jnp.exp(at) * hin + _dot_tn(bg, xdt * jnp.exp(at - a))
            yd = []
            for hh in (0, 1):
                h = 2 * j + hh
                dec = jnp.exp(jnp.minimum(acs_b[:, h * LANES:(h + 1) * LANES] - acst_s[pl.ds(h, 1), :], 0.0))
                yd.append(_dot_nn(jnp.where(causal, cb * dec, 0.0), xdt))
            y_ref[:, sl] = y + jnp.where(lane < HEAD, yd[0], yd[1])

    one = pl.BlockSpec((1, LANES), lambda i: (0, 0))
    return pl.pallas_call(
        body, name=name, grid=(nc,),
        in_specs=[pl.BlockSpec((CHUNK, 8 * LANES), lambda i: (i, 0)),
                  pl.BlockSpec((CHUNK, LANES), lambda i: (i, 8)),
                  pl.BlockSpec((CHUNK, LANES), lambda i: (i, 9)),
                  pl.BlockSpec((CHUNK, LANES), lambda i: (i, C_DT // LANES)), one, one, one],
        out_specs=[pl.BlockSpec((CHUNK, 8 * LANES), lambda i: (i, 0)),
                   pl.BlockSpec((1, SSD_PAIRS, LANES, LANES), lambda i: (i, 0, 0, 0))],
        out_shape=[jax.ShapeDtypeStruct((t, 8 * LANES), F32),
                   jax.ShapeDtypeStruct((nc, SSD_PAIRS, LANES, LANES), F32)],
        scratch_shapes=[pltpu.VMEM((CHUNK, 8 * LANES), F32), pltpu.VMEM((LANES, CHUNK), F32),
                        pltpu.VMEM((SSD_PAIRS, LANES, LANES), F32)],
        compiler_params=_params(("arbitrary",)),
    )(xc, xc, xc, proj, dtb, alog, dsk)


def _ssd_bwd(xc, proj, dtb, alog, dsk, hin_all, dy, *, name):
    t = xc.shape[0]
    nc = t // CHUNK

    def body(x_ref, b_ref, c_ref, dtr_ref, dtb_ref, alog_ref, dsk_ref, hin_ref, dy_ref,
             dxc_ref, ddtr_ref, ddtb_ref, dalog_ref, ddsk_ref, acsx_s, acst_s, dh_s, dax_s, ddx_s):
        @pl.when(pl.program_id(0) == 0)
        def _():
            dh_s[...] = jnp.zeros_like(dh_s)
            ddtb_ref[...] = jnp.zeros_like(ddtb_ref)
            dalog_ref[...] = jnp.zeros_like(dalog_ref)
            ddsk_ref[...] = jnp.zeros_like(ddsk_ref)

        dtr = dtr_ref[...]
        dtb = dtb_ref[...]
        dt, arow, dt_x, acs_b, e, tril, triu = _ssd_common(dtr, dtb, alog_ref[...], acsx_s, acst_s)
        dsk_x = _dot3(jnp.broadcast_to(dsk_ref[...], (CHUNK, LANES)), e)
        lane = lax.broadcasted_iota(jnp.int32, (CHUNK, LANES), 1)
        rowi = lax.broadcasted_iota(jnp.int32, (CHUNK, LANES), 0)
        causal = (lax.broadcasted_iota(jnp.int32, (CHUNK, CHUNK), 0)
                  >= lax.broadcasted_iota(jnp.int32, (CHUNK, CHUNK), 1))
        dacs = jnp.zeros((CHUNK, LANES), F32)
        d_b = jnp.zeros((CHUNK, LANES), F32)
        d_c = jnp.zeros((CHUNK, LANES), F32)
        for j in range(SSD_PAIRS):
            g = j // 4
            sl = slice(j * LANES, (j + 1) * LANES)
            if j % 4 == 0:
                bg = jnp.where(lane // HEAD == g, b_ref[...], 0.0)
                cg = jnp.where(lane // HEAD == g, c_ref[...], 0.0)
                cb = _dot_nt(cg, bg)
                dcb = jnp.zeros((CHUNK, CHUNK), F32)
            x = x_ref[:, sl]
            d = dt_x[:, sl]
            a = acsx_s[:, sl]
            at = acsx_s[CHUNK - 1:CHUNK, sl]
            xdt = x * d
            hin = hin_ref[0, j]
            dhout = dh_s[j]
            dyp = dy_ref[:, sl]
            ea, eat, ed = jnp.exp(a), jnp.exp(at), jnp.exp(at - a)
            da_l = dyp * ea * _dot_nn(cg, hin)
            dm = dyp * ea
            d_c = d_c + _dot_nt(dm, hin)
            dh_s[j] = _dot_tn(cg, dm) + eat * dhout
            dat = jnp.sum(dhout * hin * eat, axis=0, keepdims=True)
            d_b = d_b + _dot_nt(xdt * ed, dhout)
            dw = _dot_nn(bg, dhout)
            dxdt = dw * ed
            ded = dw * xdt * ed
            dat = dat + jnp.sum(ded, axis=0, keepdims=True)
            da_l = da_l - ded
            for hh in (0, 1):
                h = 2 * j + hh
                dec = jnp.exp(jnp.minimum(acs_b[:, h * LANES:(h + 1) * LANES] - acst_s[pl.ds(h, 1), :], 0.0))
                gm = jnp.where(causal, cb * dec, 0.0)
                dyh = jnp.where(lane // HEAD == hh, dyp, 0.0)
                dg = _dot_nt(dyh, xdt)
                dxdt = dxdt + _dot_tn(gm, dyh)
                dcb = dcb + jnp.where(causal, dg * dec, 0.0)
                th = dg * gm
                oh = (lane == h).astype(BF16)
                dacs = dacs + _dot2(th, oh) - _dot2_tn(th, oh)
            if j % 4 == 3:
                d_c = d_c + _dot_nn(dcb, bg)
                d_b = d_b + _dot_tn(dcb, cg)
            dxc_ref[:, sl] = dyp * dsk_x[:, sl] + dxdt * d
            ddx_s[:, sl] = dxdt * x
            dax_s[:, sl] = da_l + jnp.where(rowi == CHUNK - 1, dat, 0.0)
            dskp = jnp.sum(dyp * x, axis=0, keepdims=True)
            ddsk_ref[...] += _dot2(jnp.broadcast_to(dskp, (8, LANES)), e[:, sl], _NT)
        dxc_ref[:, 8 * LANES:9 * LANES] = d_b
        dxc_ref[:, 9 * LANES:10 * LANES] = d_c
        dacs = dacs + _dot2(dax_s[...], e, _NT)
        ddt = _dot2(ddx_s[...], e, _NT)
        dda = _mdot3(triu, dacs)
        ddt = ddt + dda * arow
        dalog_ref[...] += jnp.sum(dda * dt, axis=0, keepdims=True) * arow
        ddtr = jnp.where(lane < SSD_HEADS, ddt * _sigmoid(dtr + dtb), 0.0)
        ddtr_ref[...] = ddtr
        ddtb_ref[...] += jnp.sum(ddtr, axis=0, keepdims=True)

    one = pl.BlockSpec((1, LANES), lambda i: (0, 0))
    rev = lambda c: (lambda i: (nc - 1 - i, c))
    return pl.pallas_call(
        body, name=name, grid=(nc,),
        in_specs=[pl.BlockSpec((CHUNK, 8 * LANES), rev(0)), pl.BlockSpec((CHUNK, LANES), rev(8)),
                  pl.BlockSpec((CHUNK, LANES), rev(9)), pl.BlockSpec((CHUNK, LANES), rev(C_DT // LANES)),
                  one, one, one,
                  pl.BlockSpec((1, SSD_PAIRS, LANES, LANES), lambda i: (nc - 1 - i, 0, 0, 0)),
                  pl.BlockSpec((CHUNK, 8 * LANES), rev(0))],
        out_specs=[pl.BlockSpec((CHUNK, XBC_COLS), rev(0)), pl.BlockSpec((CHUNK, LANES), rev(0)), one, one,
                   pl.BlockSpec((8, LANES), lambda i: (0, 0))],
        out_shape=[jax.ShapeDtypeStruct((t, XBC_COLS), F32), jax.ShapeDtypeStruct((t, LANES), F32)]
        + [jax.ShapeDtypeStruct((1, LANES), F32)] * 2 + [jax.ShapeDtypeStruct((8, LANES), F32)],
        scratch_shapes=[pltpu.VMEM((CHUNK, 8 * LANES), F32), pltpu.VMEM((LANES, CHUNK), F32),
                        pltpu.VMEM((SSD_PAIRS, LANES, LANES), F32),
                        pltpu.VMEM((CHUNK, 8 * LANES), F32), pltpu.VMEM((CHUNK, 8 * LANES), F32)],
        compiler_params=_params(("arbitrary",)),
    )(xc, xc, xc, proj, dtb, alog, dsk, hin_all, dy)


RW_LW = 128
RW_PAIRS = 4 * LANES // RW_LW
RW_BT = 16
RW_DECAY_ROW = 1
RW_BWD_PAIRS = 4


def _rw_consts():
    seg = _seg_matrix(RW_LW)
    ti = (lax.broadcasted_iota(jnp.int32, (HEAD, RW_LW), 0)
          == lax.broadcasted_iota(jnp.int32, (HEAD, RW_LW), 1) % HEAD)
    return seg, ti


def _col_tiles(rows, ti, seg):
    tib = ti.astype(BF16)
    n = len(rows)
    hi = [r.astype(BF16) for r in rows]
    w_lo = (rows[RW_DECAY_ROW] - hi[RW_DECAY_ROW].astype(F32)).astype(BF16)
    out = lax.dot_general(jnp.concatenate([tib * h for h in hi + [w_lo]], axis=0), seg, _NN, preferred_element_type=F32)
    tiles = [out[i * HEAD:(i + 1) * HEAD] for i in range(n)]
    tiles[RW_DECAY_ROW] = tiles[RW_DECAY_ROW] + out[n * HEAD:(n + 1) * HEAD]
    return tiles


def _col_tiles2(rows, ti, seg):
    tib = ti.astype(BF16)
    hi = [r.astype(BF16) for r in rows]
    lo = [(r - h.astype(F32)).astype(BF16) for r, h in zip(rows, hi)]
    out = (lax.dot_general(jnp.concatenate([tib * h for h in hi], axis=0), seg, _NN, preferred_element_type=F32)
           + lax.dot_general(jnp.concatenate([tib * l for l in lo], axis=0), seg, _NN, preferred_element_type=F32))
    return [out[i * HEAD:(i + 1) * HEAD] for i in range(len(rows))]


def _head_lane_sums(tiles, ti, seg):
    out = _dot2(jnp.concatenate(tiles, axis=0), seg)
    return [jnp.sum(jnp.where(ti, out[i * HEAD:(i + 1) * HEAD], 0.0), axis=0, keepdims=True) for i in range(len(tiles))]


def _rw_scan_fwd(mixed, w, k, n, b, *, name):
    t = w.shape[0]

    def body(r_ref, v_ref, w_ref, k_ref, n_ref, b_ref, y_ref, st_ref, s_s):
        @pl.when(pl.program_id(0) == 0)
        def _():
            s_s[...] = jnp.zeros_like(s_s)

        seg, ti = _rw_consts()

        def step(tt, state):
            row = pl.ds(tt, 1)
            new = []
            for p in range(RW_PAIRS):
                sl = pl.ds(p * RW_LW, RW_LW)
                s = state[p]
                ncol, wcol, bcol, kcol, rcol = _col_tiles(
                    [x[row, sl] for x in (n_ref, w_ref, b_ref, k_ref, r_ref)], ti, seg)
                sa = jnp.sum(s * ncol, axis=0, keepdims=True)
                s = s * wcol + bcol * sa + kcol * v_ref[row, sl]
                y_ref[row, sl] = jnp.sum(s * rcol, axis=0, keepdims=True)
                st_ref[tt, p] = s
                new.append(s)
            return tuple(new)

        out = tuple(s_s[p] for p in range(RW_PAIRS))
        for tt in range(RW_BT):
            out = step(tt, out)
        for p in range(RW_PAIRS):
            s_s[p] = out[p]

    blk = lambda c: pl.BlockSpec((RW_BT, 4 * LANES), functools.partial(lambda i, c: (i, c), c=c))
    return pl.pallas_call(
        body, name=name, grid=(t // RW_BT,),
        in_specs=[blk(0), blk(2), blk(0), blk(0), blk(0), blk(0)],
        out_specs=[blk(0), pl.BlockSpec((RW_BT, RW_PAIRS, HEAD, RW_LW), lambda i: (i, 0, 0, 0))],
        out_shape=[jax.ShapeDtypeStruct((t, 4 * LANES), F32),
                   jax.ShapeDtypeStruct((t, RW_PAIRS, HEAD, RW_LW), F32)],
        scratch_shapes=[pltpu.VMEM((RW_PAIRS, HEAD, RW_LW), F32)],
        compiler_params=_params(("arbitrary",)),
    )(mixed, mixed, w, k, n, b)


def _rw_scan_bwd(mixed, w, k, n, b, states, dy, dr0, dk0, dv0, *, name):
    t = w.shape[0]
    nb = t // RW_BT
    ppc = RW_BWD_PAIRS
    ng = RW_PAIRS // ppc

    def body(r_ref, v_ref, w_ref, k_ref, n_ref, b_ref, st_ref, prev_ref, dy_ref, dr0_ref, dk0_ref, dv0_ref,
             dr_ref, dw_ref, dk_ref, dv_ref, dn_ref, db_ref, ds_s):
        @pl.when(pl.program_id(1) == 0)
        def _():
            ds_s[...] = jnp.zeros_like(ds_s)

        seg, ti = _rw_consts()
        has_prev = (pl.program_id(1) < nb - 1).astype(F32)

        def step(it, carry):
            tt = RW_BT - 1 - it
            row = pl.ds(tt, 1)
            prev_t = max(tt - 1, 0)
            new_ds, new_s = [], []
            for p in range(ppc):
                sl = pl.ds(p * RW_LW, RW_LW)
                ds, s_t = carry[p], carry[ppc + p]
                s_p = st_ref[prev_t, p] if tt > 0 else prev_ref[0, p] * has_prev
                ncol, wcol, bcol, kcol, rcol = _col_tiles2(
                    [x[row, sl] for x in (n_ref, w_ref, b_ref, k_ref, r_ref)], ti, seg)
                vv, dyy = v_ref[row, sl], dy_ref[row, sl]
                sa = jnp.sum(s_p * ncol, axis=0, keepdims=True)
                ds = ds + rcol * dyy
                dsa = jnp.sum(ds * bcol, axis=0, keepdims=True)
                dv_ref[row, sl] = jnp.sum(ds * kcol, axis=0, keepdims=True) + dv0_ref[row, sl]
                dr, dw, db, dk, dn = _head_lane_sums([s_t * dyy, ds * s_p, ds * sa, ds * vv, s_p * dsa], ti, seg)
                dr_ref[row, sl] = dr + dr0_ref[row, sl]
                dw_ref[row, sl] = dw
                db_ref[row, sl] = db
                dk_ref[row, sl] = dk + dk0_ref[row, sl]
                dn_ref[row, sl] = dn
                new_ds.append(ds * wcol + ncol * dsa)
                new_s.append(s_p)
            return tuple(new_ds) + tuple(new_s)

        init = tuple(ds_s[p] for p in range(ppc)) + tuple(st_ref[RW_BT - 1, p] for p in range(ppc))
        out = init
        for it in range(RW_BT):
            out = step(it, out)
        for p in range(ppc):
            ds_s[p] = out[p]

    blk = lambda c: pl.BlockSpec((RW_BT, ppc * RW_LW), functools.partial(lambda g, i, c: (nb - 1 - i, c * ng + g), c=c))
    st_spec = pl.BlockSpec((RW_BT, ppc, HEAD, RW_LW), lambda g, i: (nb - 1 - i, g, 0, 0))
    prev_spec = pl.BlockSpec((1, ppc, HEAD, RW_LW), lambda g, i: (jnp.maximum((nb - 1 - i) * RW_BT - 1, 0), g, 0, 0))
    return pl.pallas_call(
        body, name=name, grid=(ng, nb),
        in_specs=[blk(0), blk(2), blk(0), blk(0), blk(0), blk(0), st_spec, prev_spec, blk(0), blk(0), blk(0), blk(0)],
        out_specs=[blk(0)] * 6,
        out_shape=[jax.ShapeDtypeStruct((t, 4 * LANES), F32)] * 6,
        scratch_shapes=[pltpu.VMEM((ppc, HEAD, RW_LW), F32)],
        compiler_params=_params(("parallel", "arbitrary")),
    )(mixed, mixed, w, k, n, b, states, states, dy, dr0, dk0, dv0)


RW_C = 64


def _p3(a, b, dn):
    ah, al = _split2(a)
    bh, bl = _split2(b)
    d = lambda x, y: lax.dot_general(x, y, dn, preferred_element_type=F32)
    return d(ah, bh) + d(ah, bl) + d(al, bh)


_BNN = (((2,), (1,)), ((0,), (0,)))
_BNT = (((2,), (2,)), ((0,), (0,)))
_BTN = (((1,), (1,)), ((0,), (0,)))


@jax.custom_vjp
def _pnn(a, b):
    return _p3(a, b, _BNN)


@jax.custom_vjp
def _pnt(a, b):
    return _p3(a, b, _BNT)


@jax.custom_vjp
def _ptn(a, b):
    return _p3(a, b, _BTN)


_pnn.defvjp(lambda a, b: (_p3(a, b, _BNN), (a, b)), lambda res, g: (_p3(g, res[1], _BNT), _p3(res[0], g, _BTN)))
_pnt.defvjp(lambda a, b: (_p3(a, b, _BNT), (a, b)), lambda res, g: (_p3(g, res[1], _BNN), _p3(g, res[0], _BTN)))
_ptn.defvjp(lambda a, b: (_p3(a, b, _BTN), (a, b)), lambda res, g: (_p3(res[1], g, _BNT), _p3(res[0], g, _BNN)))


def _rw_chunk_consts():
    c2 = 2 * RW_C
    row = lax.broadcasted_iota(jnp.int32, (c2, c2), 0)
    col = lax.broadcasted_iota(jnp.int32, (c2, c2), 1)
    same = (row // RW_C) == (col // RW_C)
    strict = (same & (row > col)).astype(F32)
    incl = (same & (row >= col)).astype(F32)
    eye = (row == col).astype(F32)
    tr = lax.broadcasted_iota(jnp.int32, (RW_C, RW_C), 0)
    tc = lax.broadcasted_iota(jnp.int32, (RW_C, RW_C), 1)
    tril = (tr >= tc).astype(F32)
    lane = lax.broadcasted_iota(jnp.int32, (1, LANES), 1)
    hm = [(lane // HEAD == h).astype(F32) for h in (0, 1)]
    return strict, incl, eye, tril, hm


def _rw_chunk(r, lw, k, v, n, b, s2, consts):
    strict, incl, eye, tril, hm = consts
    two = lambda x: jnp.concatenate([x * hm[0], x * hm[1]], axis=1)
    cum = _pnn(jnp.broadcast_to(tril, (4, RW_C, RW_C)), lw)
    grow, shrink = jnp.exp(-cum), jnp.exp(cum)
    n2, r2 = two(n * jnp.exp(cum - lw)), two(r * shrink)
    b2, k2, v2 = two(b * grow), two(k * grow), two(v)
    p = _pnt(n2, b2) * strict
    x2 = _pnt(n2, s2) + _pnn(_pnt(n2, k2) * strict, v2)
    t_inv, a = eye + p, p
    for _ in range(RW_C.bit_length() - 2):
        a = _pnn(a, a)
        t_inv = t_inv + _pnn(t_inv, a)
    u2 = _pnn(t_inv, x2)
    y2 = _pnt(r2, s2) + _pnn(_pnt(r2, b2) * incl, u2) + _pnn(_pnt(r2, k2) * incl, v2)
    s2_new = (s2 + _ptn(u2, b2) + _ptn(v2, k2)) * jnp.exp(jnp.sum(lw, axis=1, keepdims=True))
    return jnp.sum(y2.reshape(4, 2, RW_C, LANES), axis=1), s2_new


def _pairs(ref):
    return jnp.stack([ref[:, p * LANES:(p + 1) * LANES] for p in range(4)])


def _rw_chunk_fwd(mixed, lw, k, n, b, *, name):
    t = lw.shape[0]
    nc = t // RW_C

    def body(r_ref, v_ref, lw_ref, k_ref, n_ref, b_ref, y_ref, sin_ref, s_s):
        @pl.when(pl.program_id(0) == 0)
        def _():
            s_s[...] = jnp.zeros_like(s_s)

        s2 = s_s[...]
        sin_ref[0] = s2
        y, s2 = _rw_chunk(*[_pairs(x) for x in (r_ref, lw_ref, k_ref, v_ref, n_ref, b_ref)], s2, _rw_chunk_consts())
        for p in range(4):
            y_ref[:, p * LANES:(p + 1) * LANES] = y[p]
        s_s[...] = s2

    blk = lambda c: pl.BlockSpec((RW_C, 4 * LANES), functools.partial(lambda i, c: (i, c), c=c))
    return pl.pallas_call(
        body, name=name, grid=(nc,),
        in_specs=[blk(0), blk(2), blk(0), blk(0), blk(0), blk(0)],
        out_specs=[blk(0), pl.BlockSpec((1, 4, LANES, LANES), lambda i: (i, 0, 0, 0))],
        out_shape=[jax.ShapeDtypeStruct((t, 4 * LANES), F32), jax.ShapeDtypeStruct((nc, 4, LANES, LANES), F32)],
        scratch_shapes=[pltpu.VMEM((4, LANES, LANES), F32)],
        compiler_params=_params(("arbitrary",)),
    )(mixed, mixed, lw, k, n, b)


def _rw_chunk_bwd(mixed, lw, k, n, b, s_in, dy, dr0, dk0, dv0, *, name):
    t = lw.shape[0]
    nc = t // RW_C

    def body(r_ref, v_ref, lw_ref, k_ref, n_ref, b_ref, sin_ref, dy_ref, dr0_ref, dk0_ref, dv0_ref,
             dr_ref, dlw_ref, dk_ref, dv_ref, dn_ref, db_ref, ds_s):
        @pl.when(pl.program_id(0) == 0)
        def _():
            ds_s[...] = jnp.zeros_like(ds_s)

        consts = _rw_chunk_consts()
        args = [_pairs(x) for x in (r_ref, lw_ref, k_ref, v_ref, n_ref, b_ref)] + [sin_ref[0]]
        _, vjp = jax.vjp(lambda *a: _rw_chunk(*a, consts), *args)
        dr, dlw, dk, dv, dn, db, ds = vjp((_pairs(dy_ref), ds_s[...]))
        for p in range(4):
            sl = slice(p * LANES, (p + 1) * LANES)
            dr_ref[:, sl] = dr[p] + dr0_ref[:, sl]
            dlw_ref[:, sl] = dlw[p]
            dk_ref[:, sl] = dk[p] + dk0_ref[:, sl]
            dv_ref[:, sl] = dv[p] + dv0_ref[:, sl]
            dn_ref[:, sl] = dn[p]
            db_ref[:, sl] = db[p]
        ds_s[...] = ds

    blk = lambda c: pl.BlockSpec((RW_C, 4 * LANES), functools.partial(lambda i, c: (nc - 1 - i, c), c=c))
    return pl.pallas_call(
        body, name=name, grid=(nc,),
        in_specs=[blk(0), blk(2), blk(0), blk(0), blk(0), blk(0),
                  pl.BlockSpec((1, 4, LANES, LANES), lambda i: (nc - 1 - i, 0, 0, 0)), blk(0), blk(0), blk(0), blk(0)],
        out_specs=[blk(0)] * 6,
        out_shape=[jax.ShapeDtypeStruct((t, 4 * LANES), F32)] * 6,
        scratch_shapes=[pltpu.VMEM((4, LANES, LANES), F32)],
        compiler_params=_params(("arbitrary",)),
    )(mixed, mixed, lw, k, n, b, s_in, dy, dr0, dk0, dv0)


def _f_rms_res(x, g):
    return _f_rms(x, g)[0], x


def _final(x, g, target, *, bt, name):
    t, d = x.shape

    def body(x_ref, g_ref, t_ref, dx_ref, loss_ref, dg_ref):
        tgt = t_ref[...]

        def f(xv, gv):
            err = _f_rms(xv, gv)[0] - tgt
            return 0.5 * jnp.mean(err * err, axis=-1, keepdims=True)

        row_loss, vjp = jax.vjp(f, x_ref[...], g_ref[...])
        dx, dg = vjp(jnp.ones_like(row_loss))
        dx_ref[...] = dx

        @pl.when(pl.program_id(0) == 0)
        def _():
            loss_ref[...] = jnp.zeros_like(loss_ref)
            dg_ref[...] = jnp.zeros_like(dg_ref)

        loss_ref[...] += jnp.broadcast_to(jnp.sum(row_loss, axis=0, keepdims=True), (1, LANES))
        dg_ref[...] += dg

    blk = pl.BlockSpec((bt, d), lambda i: (i, 0))
    return pl.pallas_call(
        body, name=name, grid=(t // bt,),
        in_specs=[blk, pl.BlockSpec((1, d), lambda i: (0, 0)), blk],
        out_specs=[blk, pl.BlockSpec((1, LANES), lambda i: (0, 0)), pl.BlockSpec((1, d), lambda i: (0, 0))],
        out_shape=[jax.ShapeDtypeStruct((t, d), F32), jax.ShapeDtypeStruct((1, LANES), F32),
                   jax.ShapeDtypeStruct((1, d), F32)],
        compiler_params=_params(("arbitrary",)),
    )(x, g, target)


ADAMW_BLOCK_BYTES = 1 << 20


def _adamw(w, g, m, v, *, name):
    shape = w.shape
    c = shape[-1]
    shape3 = (1,) * (3 - len(shape)) + shape if len(shape) <= 3 else (-1,) + shape[-2:]
    args = [a.reshape(shape3) for a in (w, g, m, v)]
    lead, r, _ = args[0].shape
    br = r
    if r * c * 4 > ADAMW_BLOCK_BYTES:
        cands = [b for b in range(8, r, 8) if r % b == 0 and b * c * 4 <= ADAMW_BLOCK_BYTES]
        br = max(cands) if cands else r

    def body(w_ref, g_ref, m_ref, v_ref, d_ref, nm_ref, nv_ref):
        gv = g_ref[...]
        m_new = ADAM_B1 * m_ref[...] + (1.0 - ADAM_B1) * gv
        v_new = ADAM_B2 * v_ref[...] + (1.0 - ADAM_B2) * (gv * gv)
        m_hat = m_new / (1.0 - ADAM_B1 ** ADAM_STEP)
        v_hat = v_new / (1.0 - ADAM_B2 ** ADAM_STEP)
        d_ref[...] = -ADAM_LR * (m_hat / (jnp.sqrt(v_hat) + ADAM_EPS) + ADAM_WD * w_ref[...])
        nm_ref[...] = m_new
        nv_ref[...] = v_new

    blk = pl.BlockSpec((1, br, c), lambda l, i: (l, i, 0))
    outs = pl.pallas_call(
        body, name=name, grid=(lead, r // br), in_specs=[blk] * 4, out_specs=[blk] * 3,
        out_shape=[jax.ShapeDtypeStruct((lead, r, c), F32)] * 3,
        compiler_params=_params(("parallel", "parallel")),
    )(*args)
    return tuple(o.reshape(shape) for o in outs)


BT = 256
BC = 128


def _layer_rows(x, proj, s):
    s = {k: s.get(k) for k in ("y_sb_raw", "y_ssd_raw", "mixed", "ys", "k2", "p_sb", "p_ssd", "p_rw")}
    return dict(
        rms=[(x, D_MODEL, 0)],
        sb_gate=[(s["y_sb_raw"], 512, 0), (proj, 512, 3)],
        ssd_norm=[(s["y_ssd_raw"], 1024, 0), (proj, 1024, C_Z // 1024)],
        rw_pre=[(s["mixed"], 512, 1), (s["mixed"], LANES, 16)],
        rw_post=[(s["ys"], 512, 0), (s["mixed"], 512, 0), (s["k2"], 512, 0), (s["mixed"], 512, 2), (s["mixed"], 512, 3)],
        merge=[(s["p_sb"], 1024, 0), (s["p_ssd"], 1024, 0), (s["p_rw"], 1024, 0),
               (proj, 1024, 3), (proj, 1024, 4), (proj, 1024, 5)],
    )


def _layer_fwd(x, p, nm):
    s = {}
    (s["h"],) = _rowwise(_f_rms, [(x, D_MODEL, 0)], [p["norm_g"]], [D_MODEL], bt=BT, name=nm + "rms")
    proj = s["proj"] = _mm(s["h"], p["w_in"], name=nm + "proj")
    s["y_sb_raw"], s["lt"] = _sb2_fwd(proj, name=nm + "sb")
    s["xc"] = _colwise(_f_conv, proj, C_XBC, XBC_COLS, p["conv"], bc=BC, name=nm + "conv")
    s["y_ssd_raw"], s["hin"] = _ssd_fwd(s["xc"], proj, p["dt_bias"], p["a_log"], p["d_skip"], name=nm + "ssd")
    s["mixed"] = _colwise(_f_rw_mix, proj, C_RW, RW_COLS, [p["rw_mu"]], bc=BC, name=nm + "mix")
    s["w"], s["k2"], s["n"], s["b"] = _rowwise(_f_rw_pre, [(s["mixed"], 512, 1), (s["mixed"], LANES, 16)], p["rw_pre"],
                                               [512] * 4, bt=BT, name=nm + "rwpre")
    s["ys"], s["st"] = _rw_chunk_fwd(s["mixed"], s["w"], s["k2"], s["n"], s["b"], name=nm + "scan")
    rows = _layer_rows(x, proj, s)
    (s["y_sb"],) = _rowwise(_f_sb_gate, rows["sb_gate"], [], [512], bt=BT, name=nm + "sbgate")
    (s["y_ssd"],) = _rowwise(_f_ssd_norm, rows["ssd_norm"], [p["ssd_norm_g"]], [1024], bt=BT, name=nm + "ssdnorm")
    (s["y_rw"],) = _rowwise(_f_rw_post, rows["rw_post"], p["rw_post"], [512], bt=BT, name=nm + "rwpost")
    s["p_sb"] = _mm(s["y_sb"], p["w_out_sb"], name=nm + "osb")
    s["p_ssd"] = _mm(s["y_ssd"], p["w_out_ssd"], name=nm + "ossd")
    s["p_rw"] = _mm(s["y_rw"], p["w_out_rw"], name=nm + "orw")
    (s["merged"],) = _rowwise(_f_merge, _layer_rows(x, proj, s)["merge"], [], [1024], bt=BT, name=nm + "merge")
    return _mm(s["merged"], p["w_o"], add=x, name=nm + "wo"), s


def _layer_bwd(x, dx_out, p, s, nm):
    g = {}
    proj = s["proj"]
    rows = _layer_rows(x, proj, s)
    g["w_o"] = _mm(s["merged"], dx_out, ta=True, name=nm + "g_wo")
    d_merged = _mm(dx_out, p["w_o"], tb=True, name=nm + "d_merged")
    dp_sb, dp_ssd, dp_rw, d_gates = _rowwise_bwd(_f_merge, rows["merge"], [], [(d_merged, 1024, 0)], bt=BT,
                                                 name=nm + "merge_b", groups=[[0], [1], [2], [3, 4, 5]])
    g["w_out_sb"] = _mm(s["y_sb"], dp_sb, ta=True, name=nm + "g_osb")
    g["w_out_ssd"] = _mm(s["y_ssd"], dp_ssd, ta=True, name=nm + "g_ossd")
    g["w_out_rw"] = _mm(s["y_rw"], dp_rw, ta=True, name=nm + "g_orw")
    dy_sb = _mm(dp_sb, p["w_out_sb"], tb=True, name=nm + "d_ysb")
    dy_ssd = _mm(dp_ssd, p["w_out_ssd"], tb=True, name=nm + "d_yssd")
    dy_rw = _mm(dp_rw, p["w_out_rw"], tb=True, name=nm + "d_yrw")
    dy_sb_raw, d_sbgate = _rowwise_bwd(_f_sb_gate, rows["sb_gate"], [], [(dy_sb, 512, 0)], bt=BT, name=nm + "sbgate_b")
    dq, dk, dv = _sb2_bwd(proj, dy_sb_raw, s["lt"], name=nm + "sb_b")
    dy_ssd_raw, dz, g["ssd_norm_g"] = _rowwise_bwd(_f_ssd_norm, rows["ssd_norm"], [p["ssd_norm_g"]],
                                                   [(dy_ssd, 1024, 0)], bt=BT, name=nm + "ssdnorm_b")
    dxc, ddtr, g["dt_bias"], g["a_log"], g["d_skip"] = _ssd_bwd(
        s["xc"], proj, p["dt_bias"], p["a_log"], p["d_skip"], s["hin"], dy_ssd_raw, name=nm + "ssd_b")
    conv_out = _colwise_bwd(_f_conv, proj, C_XBC, XBC_COLS, p["conv"], dxc, bc=BC, name=nm + "conv_b")
    dxbc, g["conv"] = conv_out[0], conv_out[1:]
    dys, dr0, dk0, dv0, d_rwgate, g["rw_ln_g"], g["rw_ln_b"], g["rw_r_k"] = _rowwise_bwd(
        _f_rw_post, rows["rw_post"], p["rw_post"], [(dy_rw, 512, 0)], bt=BT, name=nm + "rwpost_b")
    dr, dw, dk2, dvv, dn, db = _rw_chunk_bwd(s["mixed"], s["w"], s["k2"], s["n"], s["b"], s["st"], dys, dr0, dk0, dv0,
                                            name=nm + "scan_b")
    pre_out = _rowwise_bwd(_f_rw_pre, rows["rw_pre"], p["rw_pre"],
                           [(dw, 512, 0), (dk2, 512, 0), (dn, 512, 0), (db, 512, 0)], bt=BT, name=nm + "rwpre_b")
    dkm, dlo, g["rw_pre"] = pre_out[0], pre_out[1], pre_out[2:]
    d_mixed = jnp.concatenate([dr, dkm, dvv, d_rwgate, dlo], axis=1)
    d_slab, g["rw_mu"] = _colwise_bwd(_f_rw_mix, proj, C_RW, RW_COLS, [p["rw_mu"]], d_mixed, bc=BC, name=nm + "mix_b")
    d_proj = jnp.concatenate([dq, dk, dv, d_sbgate, dz, d_gates, d_slab, ddtr, dxbc], axis=1)
    g["w_in"] = _mm(s["h"], d_proj, ta=True, name=nm + "g_win")
    dh = _mm(d_proj, p["w_in"], tb=True, tn=1024, tk=512, name=nm + "d_h")
    dx, g["norm_g"] = _rowwise_bwd(_f_rms_res, rows["rms"], [p["norm_g"]], [(dh, D_MODEL, 0), (dx_out, D_MODEL, 0)],
                                   bt=BT, name=nm + "rms_b")
    return dx, g


MESH = pl.DeviceIdType.MESH
N_DEV = 8
_ANY = pl.BlockSpec(memory_space=pl.ANY)
_CHIP_SEMS = [pltpu.SemaphoreType.DMA((3,)), pltpu.SemaphoreType.DMA((3,)), pltpu.SemaphoreType.DMA]


def _here():
    x, y, c = lax.axis_index("x"), lax.axis_index("y"), lax.axis_index("c")
    return x, y, c, [(1 - x, y), (x, 1 - y), (1 - x, 1 - y)]


def _chip_exchange(srcs, *, per_dest, name):
    n = len(srcs)

    def body(*refs):
        src_refs, out_refs = refs[:n], refs[n:2 * n]
        send_sems, recv_sems, local_sems = refs[2 * n:]
        x, y, c, chips = _here()
        me = 2 * x + y
        sends, owns = [], []
        for a, (src_ref, out_ref) in enumerate(zip(src_refs, out_refs)):
            pick = (lambda q, s=src_ref: s.at[q]) if per_dest else (lambda q, s=src_ref: s.at[c])
            owns.append(pltpu.make_async_copy(pick(me), out_ref.at[me], local_sems.at[a]))
            owns[-1].start()
            for j, (px, py) in enumerate(chips):
                sends.append(pltpu.make_async_remote_copy(
                    pick(2 * px + py), out_ref.at[me], send_sems.at[3 * a + j], recv_sems.at[3 * a + j],
                    device_id=(px, py, c), device_id_type=MESH))
                sends[-1].start()
        for a, (src_ref, out_ref) in enumerate(zip(src_refs, out_refs)):
            for j, (px, py) in enumerate(chips):
                pltpu.make_async_remote_copy(
                    src_ref.at[0], out_ref.at[2 * px + py], send_sems.at[3 * a + j], recv_sems.at[3 * a + j],
                    device_id=(px, py, c), device_id_type=MESH).wait_recv()
        for cp in sends:
            cp.wait_send()
        for cp in owns:
            cp.wait()

    return pl.pallas_call(
        body, name=name, in_specs=[_ANY] * n, out_specs=[_ANY] * n,
        out_shape=[jax.ShapeDtypeStruct((4,) + s.shape[1:], s.dtype) for s in srcs],
        scratch_shapes=[pltpu.SemaphoreType.DMA((3 * n,)), pltpu.SemaphoreType.DMA((3 * n,)),
                        pltpu.SemaphoreType.DMA((n,))],
    )(*srcs)


def _sibling_swap(srcs, *, other_slot, name):
    n = len(srcs)

    def body(*refs):
        src_refs, out_refs, send_sems, recv_sems = refs[:n], refs[n:2 * n], refs[2 * n], refs[2 * n + 1]
        x, y, c, _ = _here()
        copies = [pltpu.make_async_remote_copy(s.at[1 - c] if other_slot else s, o, send_sems.at[a], recv_sems.at[a],
                                               device_id=(x, y, 1 - c), device_id_type=MESH)
                  for a, (s, o) in enumerate(zip(src_refs, out_refs))]
        for cp in copies:
            cp.start()
        for cp in copies:
            cp.wait()

    return pl.pallas_call(
        body, name=name, in_specs=[_ANY] * n, out_specs=[_ANY] * n,
        out_shape=[jax.ShapeDtypeStruct(s.shape[1:] if other_slot else s.shape, s.dtype) for s in srcs],
        scratch_shapes=[pltpu.SemaphoreType.DMA((n,)), pltpu.SemaphoreType.DMA((n,))],
    )(*srcs)


def _allgather_small(v, *, reduce, name):
    r = v.shape[0]

    def body(v_ref, out_ref, *rest):
        send_sems, recv_sems, local_sem = rest[-3:]
        x, y, c, chips = _here()
        me, sibling = (x, y, c), (x, y, 1 - c)

        def slot(px, py, pc):
            return out_ref.at[4 * px + 2 * py + pc]

        def copy(k, block, to, src=None):
            return pltpu.make_async_remote_copy(
                src_ref=slot(*block) if src is None else src, dst_ref=slot(*block),
                send_sem=send_sems.at[k], recv_sem=recv_sems.at[k], device_id=to, device_id_type=MESH)

        mine = pltpu.make_async_copy(v_ref, slot(*me), local_sem)
        mine.start()
        first = [copy(0, me, sibling, src=v_ref)]
        first += [copy(1 + j, me, (*chip, c), src=v_ref) for j, chip in enumerate(chips)]
        for cp in first:
            cp.start()
        passed = [copy(4 + j, (*chip, c), sibling) for j, chip in enumerate(chips)]
        for j, chip in enumerate(chips):
            copy(1 + j, (*chip, c), me).wait_recv()
            passed[j].start()
        copy(0, sibling, me).wait_recv()
        for j, chip in enumerate(chips):
            copy(4 + j, (*chip, 1 - c), me).wait_recv()
        for cp in first + passed:
            cp.wait_send()
        mine.wait()
        if reduce:
            total = out_ref[0]
            for d in range(1, N_DEV):
                total = total + out_ref[d]
            rest[0][...] = total

    vm = pl.BlockSpec(memory_space=pltpu.VMEM)
    out_shape = [jax.ShapeDtypeStruct((N_DEV, r, LANES), F32)] + ([jax.ShapeDtypeStruct((r, LANES), F32)] if reduce else [])
    return pl.pallas_call(
        body, name=name, in_specs=[vm], out_specs=[vm] * len(out_shape), out_shape=out_shape,
        scratch_shapes=[pltpu.SemaphoreType.DMA((7,)), pltpu.SemaphoreType.DMA((7,)), pltpu.SemaphoreType.DMA],
        compiler_params=pltpu.CompilerParams(vmem_limit_bytes=VMEM_LIMIT),
    )(v)


REDUCE_BLOCK_BYTES = 2 << 20


def _reduce_rows(r, c):
    cands = [b for b in range(16, r + 1, 16) if r % b == 0 and b * c * 4 <= REDUCE_BLOCK_BYTES]
    return max(cands)


def _add_halves(mine2, other, c_idx, *, name):
    _, nq, r, c = mine2.shape
    br = _reduce_rows(r, c)

    def body(c_ref, a_ref, b_ref, o_ref):
        o_ref[...] = (a_ref[0] + b_ref[...]).astype(o_ref.dtype)

    blk = pl.BlockSpec((1, br, c), lambda q, i, c_ref: (q, i, 0))
    return pl.pallas_call(
        body, name=name,
        grid_spec=pltpu.PrefetchScalarGridSpec(
            num_scalar_prefetch=1, grid=(nq, r // br),
            in_specs=[pl.BlockSpec((1, 1, br, c), lambda q, i, c_ref: (c_ref[0], q, i, 0)), blk],
            out_specs=blk),
        out_shape=jax.ShapeDtypeStruct((nq, r, c), BF16),
        compiler_params=_params(("parallel", "parallel")),
    )(c_idx, mine2, other)


def _sum_chips(parts, *, name):
    _, r, c = parts.shape
    br = _reduce_rows(r, c)

    def body(p_ref, o_ref):
        total = p_ref[0].astype(F32)
        for q in range(1, 4):
            total = total + p_ref[q].astype(F32)
        o_ref[...] = total

    return pl.pallas_call(
        body, name=name, grid=(r // br,),
        in_specs=[pl.BlockSpec((4, br, c), lambda i: (0, i, 0))],
        out_specs=pl.BlockSpec((br, c), lambda i: (i, 0)),
        out_shape=jax.ShapeDtypeStruct((r, c), F32),
        compiler_params=_params(("parallel",)),
    )(parts)


BIG = ("w_in", "w_out_sb", "w_out_ssd", "w_out_rw", "w_o")
BIG_AXIS = {"w_in": 2, "w_out_sb": 2, "w_out_ssd": 1, "w_out_rw": 2, "w_o": 1}
SMALL_SHARDED = {"conv_w": 320, "rw_w_up": 128, "rw_a_up": 128}
SMALL = ("norm_g", "conv_w", "conv_b", "dt_bias", "a_log", "d_skip", "ssd_norm_g", "rw_mu", "rw_w0", "rw_w_up",
         "rw_a0", "rw_a_up", "rw_k_k", "rw_k_a", "rw_r_k", "rw_ln_g", "rw_ln_b", "final_g")


def _rows_of(a):
    flat = a.reshape(-1)
    pad = (-flat.shape[0]) % LANES
    return jnp.pad(flat, (0, pad)).reshape(-1, LANES)


def _pack_rows(arrays, multiple=8):
    rows = jnp.concatenate([_rows_of(a) for a in arrays], axis=0)
    pad = (-rows.shape[0]) % multiple
    return jnp.pad(rows, ((0, pad), (0, 0)))


def _unpack_rows(rows, shapes):
    out, off = [], 0
    for shp in shapes:
        n = 1
        for d in shp:
            n *= d
        nr = -(-n // LANES)
        out.append(rows[off:off + nr].reshape(-1)[:n].reshape(shp))
        off += nr
    return out


COL_MAP = ((0, 3072, 0), (3072, 4352, C_XBC), (4352, 4368, C_DT), (4368, 6544, C_RW), (6544, 9616, C_GATES))
SHARD_COLS = N_IN // 4


def _w_in_from_shards(shards):
    pieces = []
    for a, b, dst in sorted(COL_MAP, key=lambda m: m[2]):
        if pieces and dst > pieces[-1][0]:
            pieces.append((dst, jnp.zeros((shards[0].shape[0], dst - pieces[-1][0]), shards[0].dtype)))
        for q in range(4):
            lo, hi = max(a, q * SHARD_COLS), min(b, (q + 1) * SHARD_COLS)
            if lo < hi:
                pieces.append((dst + hi - a, shards[q][:, lo - q * SHARD_COLS:hi - q * SHARD_COLS]))
    return jnp.concatenate([p for _, p in pieces], axis=1)


def _w_in_shard(g, q):
    pieces = []
    for a, b, dst in COL_MAP:
        lo, hi = max(a, q * SHARD_COLS), min(b, (q + 1) * SHARD_COLS)
        if lo < hi:
            pieces.append(g[:, dst + lo - a:dst + hi - a])
    return jnp.concatenate(pieces, axis=1)


def _split_chips(a, axis):
    n = a.shape[axis] // 4
    return jnp.stack([lax.slice_in_dim(a, q * n, (q + 1) * n, axis=axis) for q in range(4)])


def _join_chips(a, axis):
    return jnp.concatenate([a[q] for q in range(4)], axis=axis)


def kernel(x, norm_g, w_in, conv_w, conv_b, dt_bias, a_log, d_skip, ssd_norm_g, rw_mu, rw_w0, rw_w_up, rw_a0, rw_a_up, rw_k_k, rw_k_a, rw_r_k, rw_ln_g, rw_ln_b, w_out_sb, w_out_ssd, w_out_rw, w_o, final_g, loss_target, m_norm_g, m_w_in, m_conv_w, m_conv_b, m_dt_bias, m_a_log, m_d_skip, m_ssd_norm_g, m_rw_mu, m_rw_w0, m_rw_w_up, m_rw_a0, m_rw_a_up, m_rw_k_k, m_rw_k_a, m_rw_r_k, m_rw_ln_g, m_rw_ln_b, m_w_out_sb, m_w_out_ssd, m_w_out_rw, m_w_o, m_final_g, v_norm_g, v_w_in, v_conv_w, v_conv_b, v_dt_bias, v_a_log, v_d_skip, v_ssd_norm_g, v_rw_mu, v_rw_w0, v_rw_w_up, v_rw_a0, v_rw_a_up, v_rw_k_k, v_rw_k_a, v_rw_r_k, v_rw_ln_g, v_rw_ln_b, v_w_out_sb, v_w_out_ssd, v_w_out_rw, v_w_o, v_final_g):
    names = ("norm_g", "w_in", "conv_w", "conv_b", "dt_bias", "a_log", "d_skip", "ssd_norm_g", "rw_mu", "rw_w0",
             "rw_w_up", "rw_a0", "rw_a_up", "rw_k_k", "rw_k_a", "rw_r_k", "rw_ln_g", "rw_ln_b", "w_out_sb",
             "w_out_ssd", "w_out_rw", "w_o", "final_g")
    w_loc = dict(zip(names, (norm_g, w_in, conv_w, conv_b, dt_bias, a_log, d_skip, ssd_norm_g, rw_mu, rw_w0, rw_w_up,
                             rw_a0, rw_a_up, rw_k_k, rw_k_a, rw_r_k, rw_ln_g, rw_ln_b, w_out_sb, w_out_ssd, w_out_rw,
                             w_o, final_g)))
    m_loc = dict(zip(names, (m_norm_g, m_w_in, m_conv_w, m_conv_b, m_dt_bias, m_a_log, m_d_skip, m_ssd_norm_g,
                             m_rw_mu, m_rw_w0, m_rw_w_up, m_rw_a0, m_rw_a_up, m_rw_k_k, m_rw_k_a, m_rw_r_k,
                             m_rw_ln_g, m_rw_ln_b, m_w_out_sb, m_w_out_ssd, m_w_out_rw, m_w_o, m_final_g)))
    v_loc = dict(zip(names, (v_norm_g, v_w_in, v_conv_w, v_conv_b, v_dt_bias, v_a_log, v_d_skip, v_ssd_norm_g,
                             v_rw_mu, v_rw_w0, v_rw_w_up, v_rw_a0, v_rw_a_up, v_rw_k_k, v_rw_k_a, v_rw_r_k,
                             v_rw_ln_g, v_rw_ln_b, v_w_out_sb, v_w_out_ssd, v_w_out_rw, v_w_o, v_final_g)))
    chip = 2 * lax.axis_index("x") + lax.axis_index("y")
    core = lax.axis_index("c")

    got_mine = _chip_exchange([w_loc[n].astype(BF16) for n in BIG], per_dest=False, name="gather_big")
    got_theirs = _sibling_swap(got_mine, other_slot=False, name="gather_join")
    full = {}
    for n, mine, theirs in zip(BIG, got_mine, got_theirs):
        layers = [jnp.where(core == l, mine, theirs) for l in range(DEPTH)]
        if n == "w_in":
            full[n] = jnp.stack([_w_in_from_shards([lay[q] for q in range(4)]) for lay in layers])
        else:
            full[n] = jnp.stack([jnp.concatenate([lay[q] for q in range(4)], axis=BIG_AXIS[n] - 1) for lay in layers])
    sm_names = tuple(SMALL_SHARDED)
    sm_shapes = [w_loc[n].shape for n in sm_names]
    (got_sm,) = _allgather_small(_pack_rows([w_loc[n] for n in sm_names]), reduce=False, name="gather_small")
    per_chip = [_unpack_rows(got_sm[4 * (q // 2) + 2 * (q % 2)], sm_shapes) for q in range(4)]
    for i, n in enumerate(sm_names):
        full[n] = jnp.concatenate([per_chip[q][i] for q in range(4)], axis=-1)

    def pad16(a):
        return jnp.zeros((1, LANES), F32).at[0, :SSD_HEADS].set(a)

    def layer_params(i):
        row = lambda n: w_loc[n][i].reshape(1, -1)
        cw = full["conv_w"][i]
        return dict(
            norm_g=row("norm_g"), w_in=full["w_in"][i], conv=[cw[k][None] for k in range(4)] + [row("conv_b")],
            dt_bias=pad16(dt_bias[i]), a_log=pad16(a_log[i]), d_skip=pad16(d_skip[i]),
            ssd_norm_g=row("ssd_norm_g"), rw_mu=row("rw_mu"),
            rw_pre=[row("rw_w0"), jnp.zeros((LANES, 512), F32).at[:HEAD].set(full["rw_w_up"][i]), row("rw_a0"),
                    jnp.zeros((LANES, 512), F32).at[HEAD:].set(full["rw_a_up"][i]), row("rw_k_k"), row("rw_k_a")],
            rw_post=[row("rw_ln_g"), row("rw_ln_b"), row("rw_r_k")],
            w_out_sb=full["w_out_sb"][i], w_out_ssd=full["w_out_ssd"][i], w_out_rw=full["w_out_rw"][i],
            w_o=full["w_o"][i])

    params = [layer_params(i) for i in range(DEPTH)]
    xs, saved = [x[0]], []
    for i in range(DEPTH):
        nxt, s = _layer_fwd(xs[-1], params[i], f"l{i}_")
        xs.append(nxt)
        saved.append(s)
    dx, loss_row, g_final = _final(xs[-1], final_g.reshape(1, -1), loss_target[0], bt=BT, name="final")
    grads = [None] * DEPTH
    for i in reversed(range(DEPTH)):
        dx, grads[i] = _layer_bwd(xs[i], dx, params[i], saved[i], f"l{i}_")

    def stacked(fn):
        return jnp.stack([fn(grads[i]) for i in range(DEPTH)])

    g_loc = {
        "norm_g": stacked(lambda g: g["norm_g"][0]),
        "w_in": stacked(lambda g: g["w_in"]),
        "conv_w": stacked(lambda g: jnp.concatenate(g["conv"][:4], axis=0)),
        "conv_b": stacked(lambda g: g["conv"][4][0]),
        "dt_bias": stacked(lambda g: g["dt_bias"][0, :SSD_HEADS]),
        "a_log": stacked(lambda g: g["a_log"][0, :SSD_HEADS]),
        "d_skip": stacked(lambda g: g["d_skip"][0, :SSD_HEADS]),
        "ssd_norm_g": stacked(lambda g: g["ssd_norm_g"][0]),
        "rw_mu": stacked(lambda g: g["rw_mu"][0]),
        "rw_w0": stacked(lambda g: g["rw_pre"][0][0]),
        "rw_w_up": stacked(lambda g: g["rw_pre"][1][:HEAD]),
        "rw_a0": stacked(lambda g: g["rw_pre"][2][0]),
        "rw_a_up": stacked(lambda g: g["rw_pre"][3][HEAD:]),
        "rw_k_k": stacked(lambda g: g["rw_pre"][4][0]),
        "rw_k_a": stacked(lambda g: g["rw_pre"][5][0]),
        "rw_r_k": stacked(lambda g: g["rw_r_k"].reshape(8, HEAD)),
        "rw_ln_g": stacked(lambda g: g["rw_ln_g"][0]),
        "rw_ln_b": stacked(lambda g: g["rw_ln_b"][0]),
        "w_out_sb": stacked(lambda g: g["w_out_sb"]),
        "w_out_ssd": stacked(lambda g: g["w_out_ssd"]),
        "w_out_rw": stacked(lambda g: g["w_out_rw"]),
        "w_o": stacked(lambda g: g["w_o"]),
        "final_g": g_final[0],
    }

    sends = [jnp.stack([jnp.stack([_w_in_shard(g_loc[n][l], q) for q in range(4)]) for l in range(DEPTH)])
             if n == "w_in" else jnp.swapaxes(_split_chips(g_loc[n], BIG_AXIS[n]), 0, 1) for n in BIG]
    others = _sibling_swap(sends, other_slot=True, name="reduce_sibling")
    c_idx = core.reshape(1).astype(jnp.int32)
    parts = [_add_halves(s, o, c_idx, name="reduce_add_" + n) for n, s, o in zip(BIG, sends, others)]
    parts = _chip_exchange(parts, per_dest=True, name="reduce_chips")
    mine = [_sum_chips(p, name="reduce_sum_" + n) for n, p in zip(BIG, parts)]
    theirs = _sibling_swap(mine, other_slot=False, name="reduce_join")
    g_out = {n: jnp.stack([jnp.where(core == 0, a, b), jnp.where(core == 0, b, a)])
             for n, a, b in zip(BIG, mine, theirs)}

    sm_all = SMALL + ("loss",)
    sm_full_shapes = [g_loc[n].shape for n in SMALL] + [(1,)]
    _, summed = _allgather_small(_pack_rows([g_loc[n] for n in SMALL] + [loss_row[0, :1]]), reduce=True, name="reduce_small")
    sm = dict(zip(sm_all, _unpack_rows(summed, sm_full_shapes)))
    for n in SMALL:
        g_out[n] = sm[n]
    for n, wd in SMALL_SHARDED.items():
        g_out[n] = lax.dynamic_slice_in_dim(sm[n], chip * wd, wd, axis=sm[n].ndim - 1)
    loss = sm["loss"][0]

    upd = {n: _adamw(w_loc[n], g_out[n], m_loc[n], v_loc[n], name="adamw_" + n) for n in names}
    return (loss, dx[None], *[g_out[n] for n in names], *[upd[n][0] for n in names],
            *[upd[n][1] for n in names], *[upd[n][2] for n in names])
```

```python
import functools

import jax
import jax.numpy as jnp
from jax import lax
from jax.experimental import pallas as pl
from jax.experimental.pallas import tpu as pltpu

F32 = jnp.float32
BF16 = jnp.bfloat16

D_MODEL = 1024
DEPTH = 2
HEAD = 64
LANES = 128
CHUNK = 128
RMS_EPS = 1e-6
GN_EPS = 64e-5
VMEM_LIMIT = 56 * 1024 * 1024

N_IN = 9616
N_PAD = 9728
C_SB, C_Z, C_GATES, C_RW, C_LO, C_DT, C_XBC = 0, 2048, 3072, 6144, 8192, 8320, 8448
RW_COLS = 2176
XBC_COLS = 1280

ADAM_LR, ADAM_B1, ADAM_B2, ADAM_EPS, ADAM_WD, ADAM_STEP = 0.001, 0.9, 0.999, 1e-08, 0.01, 10


def _params(sem=None):
    return pltpu.CompilerParams(dimension_semantics=sem, vmem_limit_bytes=VMEM_LIMIT)


@jax.custom_vjp
def _sigmoid(x):
    return 1.0 / (1.0 + jnp.exp(-x))


def _sigmoid_fwd(x):
    s = _sigmoid(x)
    return s, s


def _sigmoid_bwd(s, g):
    return (g * s * (1.0 - s),)


_sigmoid.defvjp(_sigmoid_fwd, _sigmoid_bwd)


@jax.custom_vjp
def _silu(x):
    return x * _sigmoid(x)


def _silu_fwd(x):
    s = _sigmoid(x)
    return x * s, (x, s)


def _silu_bwd(res, g):
    x, s = res
    return (g * (s + x * s * (1.0 - s)),)


_silu.defvjp(_silu_fwd, _silu_bwd)


@jax.custom_vjp
def _softplus(x):
    return jnp.maximum(x, 0.0) + jnp.log(1.0 + jnp.exp(-jnp.abs(x)))


def _softplus_fwd(x):
    return _softplus(x), x


def _softplus_bwd(x, g):
    return (g * _sigmoid(x),)


_softplus.defvjp(_softplus_fwd, _softplus_bwd)


def _dot(a, b, dims):
    return lax.dot_general(a.astype(BF16), b.astype(BF16), (dims, ((), ())), preferred_element_type=F32)


def _dot_nn(a, b):
    return _dot(a, b, ((1,), (0,)))


def _dot_nt(a, b):
    return _dot(a, b, ((1,), (1,)))


def _dot_tn(a, b):
    return _dot(a, b, ((0,), (0,)))


@jax.custom_vjp
def _bdot(a, b):
    return _dot_nn(a, b)


def _bdot_fwd(a, b):
    return _dot_nn(a, b), (a, b)


def _bdot_bwd(res, g):
    a, b = res
    return _dot_nt(g, b), _dot_tn(a, g)


_bdot.defvjp(_bdot_fwd, _bdot_bwd)


def _split2(x):
    hi = x.astype(BF16)
    lo = (x - hi.astype(F32)).astype(BF16)
    return hi, lo


_NT = (((1,), (1,)), ((), ()))
_NN = (((1,), (0,)), ((), ()))
_TN = (((0,), (0,)), ((), ()))


def _dot2(x, m, dn=_NN):
    hi, lo = _split2(x)
    return (lax.dot_general(hi, m, dn, preferred_element_type=F32)
            + lax.dot_general(lo, m, dn, preferred_element_type=F32))


def _dot2_tn(x, m):
    return _dot2(x, m, _TN)


def _seg_matrix(n):
    r = lax.broadcasted_iota(jnp.int32, (n, n), 0) // HEAD
    c = lax.broadcasted_iota(jnp.int32, (n, n), 1) // HEAD
    return (r == c).astype(BF16)


@jax.custom_vjp
def _segsum2(x, seg):
    return _dot2(x, seg)


def _segsum2_fwd(x, seg):
    return _dot2(x, seg), seg


def _segsum2_bwd(seg, g):
    return _dot2(g, seg), jnp.zeros_like(seg)


_segsum2.defvjp(_segsum2_fwd, _segsum2_bwd)


def _make_segsum(seg):
    return lambda x: _segsum2(x, seg)


def _shift_down_raw(x, k):
    row = lax.broadcasted_iota(jnp.int32, x.shape, 0)
    return jnp.where(row >= k, pltpu.roll(x, k, 0), 0.0)


def _shift_up_raw(x, k):
    t = x.shape[0]
    row = lax.broadcasted_iota(jnp.int32, x.shape, 0)
    return jnp.where(row < t - k, pltpu.roll(x, t - k, 0), 0.0)


@functools.partial(jax.custom_vjp, nondiff_argnums=(1,))
def _shift_down(x, k):
    return _shift_down_raw(x, k)


def _shift_down_fwd(x, k):
    return _shift_down_raw(x, k), None


def _shift_down_bwd(k, _, g):
    return (_shift_up_raw(g, k),)


_shift_down.defvjp(_shift_down_fwd, _shift_down_bwd)


def _mm(a, b, *, name, ta=False, tb=False, add=None, out_dtype=F32, tm=2048, tn=512, tk=None):
    m, k = (a.shape[1], a.shape[0]) if ta else a.shape
    n = b.shape[0] if tb else b.shape[1]
    tm, tn = min(tm, m), min(tn, n)
    tk = k if tk is None else tk
    nk = k // tk
    assert m % tm == 0 and n % tn == 0 and k % tk == 0
    dims = ((0 if ta else 1,), (1 if tb else 0,))

    def body(a_ref, b_ref, *refs):
        o_ref, acc_ref = refs[-2:]
        p = _dot(a_ref[...], b_ref[...], dims)

        def emit(total):
            if add is not None:
                total = total + refs[0][...]
            o_ref[...] = total.astype(o_ref.dtype)

        if nk == 1:
            emit(p)
        else:
            kk = pl.program_id(2)

            @pl.when(kk == 0)
            def _():
                acc_ref[...] = p

            @pl.when(kk > 0)
            def _():
                acc_ref[...] += p

            @pl.when(kk == nk - 1)
            def _():
                emit(acc_ref[...])

    a_spec = pl.BlockSpec((tk, tm), lambda i, j, kk: (kk, i)) if ta else pl.BlockSpec((tm, tk), lambda i, j, kk: (i, kk))
    b_spec = pl.BlockSpec((tn, tk), lambda i, j, kk: (j, kk)) if tb else pl.BlockSpec((tk, tn), lambda i, j, kk: (kk, j))
    o_spec = pl.BlockSpec((tm, tn), lambda i, j, kk: (i, j))
    return pl.pallas_call(
        body, name=name, grid=(m // tm, n // tn, nk),
        in_specs=[a_spec, b_spec] + ([o_spec] if add is not None else []), out_specs=o_spec,
        out_shape=jax.ShapeDtypeStruct((m, n), out_dtype),
        scratch_shapes=[pltpu.VMEM((tm, tn) if nk > 1 else (8, LANES), F32)],
        compiler_params=_params(("parallel", "parallel", "arbitrary")),
    )(a, b, *([add] if add is not None else []))


def _row_specs(rows, bt):
    return [pl.BlockSpec((bt, w), functools.partial(lambda i, c: (i, c), c=c)) for _, w, c in rows]


def _full_spec(p):
    return pl.BlockSpec(p.shape, functools.partial(lambda i, nd: (0,) * nd, nd=p.ndim))


def _rowwise(f, rows, pars, out_widths, *, bt, name, acc_widths=()):
    t = rows[0][0].shape[0]
    nr, npar, no, na = len(rows), len(pars), len(out_widths), len(acc_widths)

    def body(*refs):
        vals = [r[...] for r in refs[:nr + npar]]
        outs = f(*vals)
        for o_ref, o in zip(refs[nr + npar:nr + npar + no], outs[:no]):
            o_ref[...] = o.astype(o_ref.dtype)
        if na:
            first = pl.program_id(0) == 0
            for a_ref, a in zip(refs[nr + npar + no:], outs[no:]):
                @pl.when(first)
                def _():
                    a_ref[...] = jnp.zeros_like(a_ref)
                a_ref[...] += a

    return pl.pallas_call(
        body, name=name, grid=(t // bt,),
        in_specs=_row_specs(rows, bt) + [_full_spec(p) for p in pars],
        out_specs=[pl.BlockSpec((bt, w), lambda i: (i, 0)) for w in out_widths]
        + [pl.BlockSpec((1, w), lambda i: (0, 0)) for w in acc_widths],
        out_shape=[jax.ShapeDtypeStruct((t, w), F32) for w in out_widths]
        + [jax.ShapeDtypeStruct((1, w), F32) for w in acc_widths],
        compiler_params=_params(("arbitrary",)),
    )(*[r[0] for r in rows], *pars)


def _rowwise_bwd(f, rows, pars, douts, *, bt, name, groups=None):
    t = rows[0][0].shape[0]
    nr, npar, nd = len(rows), len(pars), len(douts)
    groups = [[i] for i in range(nr)] if groups is None else groups
    widths = [r[1] for r in rows]

    def body(*refs):
        vals = [r[...] for r in refs[:nr + npar]]
        cts = tuple(r[...] for r in refs[nr + npar:nr + npar + nd])
        _, vjp = jax.vjp(lambda *a: tuple(f(*a)), *vals)
        grads = vjp(cts)
        out_refs = refs[nr + npar + nd:]
        for g_ref, grp in zip(out_refs[:len(groups)], groups):
            off = 0
            for i in grp:
                g_ref[:, off:off + widths[i]] = grads[i]
                off += widths[i]
        first = pl.program_id(0) == 0
        for p_ref, g in zip(out_refs[len(groups):], grads[nr:]):
            @pl.when(first)
            def _():
                p_ref[...] = jnp.zeros_like(p_ref)
            p_ref[...] += g

    gw = [sum(widths[i] for i in grp) for grp in groups]
    return pl.pallas_call(
        body, name=name, grid=(t // bt,),
        in_specs=_row_specs(rows, bt) + [_full_spec(p) for p in pars] + _row_specs(douts, bt),
        out_specs=[pl.BlockSpec((bt, w), lambda i: (i, 0)) for w in gw] + [_full_spec(p) for p in pars],
        out_shape=[jax.ShapeDtypeStruct((t, w), F32) for w in gw] + [jax.ShapeDtypeStruct(p.shape, F32) for p in pars],
        compiler_params=_params(("arbitrary",)),
    )(*[r[0] for r in rows], *pars, *[d[0] for d in douts])


def _colwise(f, x, c0, ncols, pars, *, bc, name):
    t = x.shape[0]

    def body(x_ref, *refs):
        o_ref = refs[-1]
        o_ref[...] = f(x_ref[...], *[r[...] for r in refs[:-1]])

    return pl.pallas_call(
        body, name=name, grid=(ncols // bc,),
        in_specs=[pl.BlockSpec((t, bc), lambda j: (0, j + c0 // bc))]
        + [pl.BlockSpec((p.shape[0], bc), lambda j: (0, j)) for p in pars],
        out_specs=pl.BlockSpec((t, bc), lambda j: (0, j)),
        out_shape=jax.ShapeDtypeStruct((t, ncols), F32),
        compiler_params=_params(("parallel",)),
    )(x, *pars)


def _colwise_bwd(f, x, c0, ncols, pars, dout, *, bc, name):
    t = x.shape[0]
    npar = len(pars)

    def body(x_ref, *refs):
        vals = [x_ref[...]] + [r[...] for r in refs[:npar]]
        _, vjp = jax.vjp(f, *vals)
        grads = vjp(refs[npar][...])
        for g_ref, g in zip(refs[npar + 1:], grads):
            g_ref[...] = g

    return pl.pallas_call(
        body, name=name, grid=(ncols // bc,),
        in_specs=[pl.BlockSpec((t, bc), lambda j: (0, j + c0 // bc))]
        + [pl.BlockSpec((p.shape[0], bc), lambda j: (0, j)) for p in pars]
        + [pl.BlockSpec((t, bc), lambda j: (0, j))],
        out_specs=[pl.BlockSpec((t, bc), lambda j: (0, j))]
        + [pl.BlockSpec((p.shape[0], bc), lambda j: (0, j)) for p in pars],
        out_shape=[jax.ShapeDtypeStruct((t, ncols), F32)] + [jax.ShapeDtypeStruct(p.shape, F32) for p in pars],
        compiler_params=_params(("parallel",)),
    )(x, *pars, dout)


def _f_rms(x, g):
    return (x * lax.rsqrt(jnp.mean(x * x, axis=-1, keepdims=True) + RMS_EPS) * g,)


def _f_sb_gate(y, gate):
    return (y * _silu(gate),)


def _f_ssd_norm(y, z, g):
    u = y * _silu(z)
    return (u * lax.rsqrt(jnp.mean(u * u, axis=-1, keepdims=True) + RMS_EPS) * g,)


def _f_merge(p_sb, p_ssd, p_rw, g_sb, g_ssd, g_rw):
    return (_sigmoid(g_sb) * p_sb + _sigmoid(g_ssd) * p_ssd + _sigmoid(g_rw) * p_rw,)


def _f_rw_pre(k, lo, w0, w_up, a0, a_up, k_k, k_a):
    segsum = _make_segsum(_seg_matrix(k.shape[1]))
    lane = lax.broadcasted_iota(jnp.int32, lo.shape, 1)
    w_lo = jnp.where(lane < HEAD, jnp.tanh(lo), 0.0)
    a_lo = jnp.where(lane >= HEAD, lo, 0.0)
    w = -_softplus(-(w0 + _bdot(w_lo, w_up))) - 0.5
    log_decay = -jnp.exp(w)
    a = _sigmoid(a0 + _bdot(a_lo, a_up))
    kk = k * k_k
    kk = kk / jnp.maximum(jnp.sqrt(segsum(kk * kk)), 1e-12)
    return log_decay, k * (1.0 + (a - 1.0) * k_a), -kk, kk * a


def _f_rw_post(y, r, k2, v, gate, ln_g, ln_b, r_k):
    segsum = _make_segsum(_seg_matrix(y.shape[1]))
    yc = y - segsum(y) * (1.0 / HEAD)
    var = segsum(yc * yc) * (1.0 / HEAD)
    yn = yc * lax.rsqrt(var + GN_EPS) * ln_g + ln_b
    return ((yn + segsum(r * k2 * r_k) * v) * _silu(gate),)


def _f_rw_mix(slab, mu):
    return slab + (_shift_down(slab, 1) - slab) * mu


def _f_conv(x, w0, w1, w2, w3, b):
    acc = x * w3 + b
    for i, w in enumerate((w0, w1, w2)):
        acc = acc + _shift_down(x, 3 - i) * w
    return _silu(acc)


def _log_sigmoid(z):
    return jnp.minimum(z, 0.0) - jnp.log(1.0 + jnp.exp(-jnp.abs(z)))


def _prefix_matrix(kind):
    j = lax.broadcasted_iota(jnp.int32, (CHUNK, 2 * CHUNK), 0)
    s = lax.broadcasted_iota(jnp.int32, (CHUNK, 2 * CHUNK), 1)
    tri = {"gt": j > s, "le": j <= s, "lt": j < s}[kind]
    return (tri | (s >= CHUNK)).astype(BF16)


def _sb_specs(t):
    q = pl.BlockSpec((CHUNK, LANES), lambda j, i: (i, j))
    k = pl.BlockSpec((t, LANES), lambda j, i: (0, 4 + j))
    v = pl.BlockSpec((t, LANES), lambda j, i: (0, 8 + j))
    return q, k, v


def _sb_fwd(proj, *, name):
    t = proj.shape[0]
    scale = HEAD ** -0.5

    def body(q_ref, k_ref, v_ref, y_ref, lt_ref):
        i = pl.program_id(1)
        lane = lax.broadcasted_iota(jnp.int32, (CHUNK, LANES), 1)
        diff = (lax.broadcasted_iota(jnp.int32, (CHUNK, CHUNK), 1)
                - lax.broadcasted_iota(jnp.int32, (CHUNK, CHUNK), 0))
        m_f = _prefix_matrix("gt")
        q = q_ref[...] * scale
        qh = [jnp.where((lane // HEAD) == h, q, 0.0).astype(BF16) for h in (0, 1)]

        def step(it, carry):
            off = pl.multiple_of((i - it) * CHUNK, CHUNK)
            kblk = k_ref[pl.ds(off, CHUNK), :].astype(BF16)
            vblk = v_ref[pl.ds(off, CHUNK), :].astype(BF16)
            mask = diff < it * CHUNK
            new = []
            for h in (0, 1):
                c, acc = carry[2 * h], carry[2 * h + 1]
                z = lax.dot_general(qh[h], kblk, _NT, preferred_element_type=F32)
                lb = _log_sigmoid(z)
                w2 = _dot2(jnp.where(mask, lb - z, 0.0), m_f)
                att = jnp.where(mask, jnp.exp(lb + c + w2[:, :CHUNK]), 0.0)
                acc = acc + lax.dot_general(att.astype(BF16), vblk, _NN, preferred_element_type=F32)
                new += [c + w2[:, CHUNK:], acc]
            return tuple(new)

        zero = jnp.zeros((CHUNK, LANES), F32)
        c_a, acc_a, c_b, acc_b = lax.fori_loop(0, i + 1, step, (zero, zero, zero, zero))
        y_ref[...] = jnp.where(lane < HEAD, acc_a, acc_b)
        lt_ref[0] = c_a
        lt_ref[1] = c_b

    return pl.pallas_call(
        body, name=name, grid=(4, t // CHUNK),
        in_specs=list(_sb_specs(t)),
        out_specs=[pl.BlockSpec((CHUNK, LANES), lambda j, i: (i, j)),
                   pl.BlockSpec((2, CHUNK, LANES), lambda j, i: (j, i, 0))],
        out_shape=[jax.ShapeDtypeStruct((t, 4 * LANES), F32), jax.ShapeDtypeStruct((8, t, LANES), F32)],
        compiler_params=_params(("parallel", "arbitrary")),
    )(proj, proj, proj)


def _sb_bwd(proj, dy, lt, *, name):
    t = proj.shape[0]
    scale = HEAD ** -0.5

    def body(q_ref, k_ref, v_ref, dy_ref, lt_ref, dq_ref, dk_ref, dv_ref):
        i = pl.program_id(1)

        @pl.when(i == 0)
        def _():
            dk_ref[...] = jnp.zeros_like(dk_ref)
            dv_ref[...] = jnp.zeros_like(dv_ref)

        lane = lax.broadcasted_iota(jnp.int32, (CHUNK, LANES), 1)
        diff = (lax.broadcasted_iota(jnp.int32, (CHUNK, CHUNK), 1)
                - lax.broadcasted_iota(jnp.int32, (CHUNK, CHUNK), 0))
        m_le, m_lt = _prefix_matrix("le"), _prefix_matrix("lt")
        q = q_ref[...] * scale
        dy_blk = dy_ref[...]
        qh = [jnp.where((lane // HEAD) == h, q, 0.0).astype(BF16) for h in (0, 1)]
        doh = [jnp.where((lane // HEAD) == h, dy_blk, 0.0).astype(BF16) for h in (0, 1)]
        lth = [lt_ref[0], lt_ref[1]]

        def step(kb, carry):
            off = pl.multiple_of(kb * CHUNK, CHUNK)
            kblk = k_ref[pl.ds(off, CHUNK), :].astype(BF16)
            vblk = v_ref[pl.ds(off, CHUNK), :].astype(BF16)
            mask = diff < (i - kb) * CHUNK
            new = []
            dk_acc = jnp.zeros((CHUNK, LANES), F32)
            dv_acc = jnp.zeros((CHUNK, LANES), F32)
            for h in (0, 1):
                cp, cg, dq = carry[3 * h:3 * h + 3]
                z = lax.dot_general(qh[h], kblk, _NT, preferred_element_type=F32)
                lb = _log_sigmoid(z)
                w2 = _dot2(jnp.where(mask, lb - z, 0.0), m_le)
                att = jnp.where(mask, jnp.exp(lb + lth[h] - cp - w2[:, :CHUNK]), 0.0)
                d_att = lax.dot_general(doh[h], vblk, _NT, preferred_element_type=F32)
                d_e = d_att * att
                g2 = _dot2(d_e, m_lt)
                sig = jnp.exp(lb)
                dz = jnp.where(mask, d_e * (1.0 - sig) - (cg + g2[:, :CHUNK]) * sig, 0.0).astype(BF16)
                dq = dq + lax.dot_general(dz, kblk, _NN, preferred_element_type=F32)
                dk_acc = dk_acc + lax.dot_general(dz, qh[h], _TN, preferred_element_type=F32)
                dv_acc = dv_acc + lax.dot_general(att.astype(BF16), doh[h], _TN, preferred_element_type=F32)
                new += [cp + w2[:, CHUNK:], cg + g2[:, CHUNK:], dq]
            dk_ref[pl.ds(off, CHUNK), :] += dk_acc
            dv_ref[pl.ds(off, CHUNK), :] += dv_acc
            return tuple(new)

        zero = jnp.zeros((CHUNK, LANES), F32)
        out = lax.fori_loop(0, i + 1, step, (zero,) * 6)
        dq_ref[...] = jnp.where(lane < HEAD, out[2], out[5]) * scale

    q_spec, k_spec, v_spec = _sb_specs(t)
    blk = pl.BlockSpec((CHUNK, LANES), lambda j, i: (i, j))
    col = pl.BlockSpec((t, LANES), lambda j, i: (0, j))
    return pl.pallas_call(
        body, name=name, grid=(4, t // CHUNK),
        in_specs=[q_spec, k_spec, v_spec, blk, pl.BlockSpec((2, CHUNK, LANES), lambda j, i: (j, i, 0))],
        out_specs=[blk, col, col],
        out_shape=[jax.ShapeDtypeStruct((t, 4 * LANES), F32)] * 3,
        compiler_params=_params(("parallel", "arbitrary")),
    )(proj, proj, proj, dy, lt)


SB_BQ = 256
SB_BK = 256
assert SB_BQ == SB_BK


def _tri_ones(kind):
    j = lax.broadcasted_iota(jnp.int32, (SB_BK, SB_BK + LANES), 0)
    s = lax.broadcasted_iota(jnp.int32, (SB_BK, SB_BK + LANES), 1)
    tri = {"gt": j > s, "le": j <= s, "lt": j < s}[kind]
    return (tri | (s >= SB_BK)).astype(BF16)


def _sb_common(q_ref):
    lane = lax.broadcasted_iota(jnp.int32, (SB_BQ, LANES), 1)
    q = q_ref[...] * (HEAD ** -0.5)
    q2 = jnp.concatenate([jnp.where(lane < HEAD, q, 0.0), jnp.where(lane >= HEAD, q, 0.0)], axis=0).astype(BF16)
    diff = (lax.broadcasted_iota(jnp.int32, (2 * SB_BQ, SB_BK), 1)
            - (lax.broadcasted_iota(jnp.int32, (2 * SB_BQ, SB_BK), 0) & (SB_BQ - 1)))
    return lane, q2, diff


def _rep(x):
    return jnp.concatenate([x] * (SB_BK // LANES), axis=1)


def _sb2_specs(t):
    q = pl.BlockSpec((SB_BQ, LANES), lambda j, i: (i, j))
    k = pl.BlockSpec((t, LANES), lambda j, i: (0, 4 + j))
    v = pl.BlockSpec((t, LANES), lambda j, i: (0, 8 + j))
    return q, k, v


def _sb2_fwd(proj, *, name):
    t = proj.shape[0]

    def body(q_ref, k_ref, v_ref, y_ref, lt_ref):
        i = pl.program_id(1)
        lane, q2, diff = _sb_common(q_ref)
        m_f = _tri_ones("gt")

        def step(kb, carry, diagonal):
            c, acc = carry
            off = pl.multiple_of(kb * SB_BK, SB_BK)
            kblk = k_ref[pl.ds(off, SB_BK), :].astype(BF16)
            vblk = v_ref[pl.ds(off, SB_BK), :].astype(BF16)
            z = lax.dot_general(q2, kblk, _NT, preferred_element_type=F32)
            lb = _log_sigmoid(z)
            lk = jnp.where(diff < 0, lb - z, 0.0) if diagonal else lb - z
            w2 = _dot2(lk, m_f)
            att = jnp.exp(lb + _rep(c) + w2[:, :SB_BK])
            if diagonal:
                att = jnp.where(diff < 0, att, 0.0)
            acc = acc + lax.dot_general(att.astype(BF16), vblk, _NN, preferred_element_type=F32)
            return c + w2[:, SB_BK:], acc

        zero = jnp.zeros((2 * SB_BQ, LANES), F32)
        c, acc = lax.fori_loop(0, i, lambda it, carry: step(i - 1 - it, carry, False), step(i, (zero, zero), True))
        y_ref[...] = jnp.where(lane < HEAD, acc[:SB_BQ], acc[SB_BQ:])
        lt_ref[0] = c[:SB_BQ]
        lt_ref[1] = c[SB_BQ:]

    return pl.pallas_call(
        body, name=name, grid=(4, t // SB_BQ),
        in_specs=list(_sb2_specs(t)),
        out_specs=[pl.BlockSpec((SB_BQ, LANES), lambda j, i: (i, j)),
                   pl.BlockSpec((2, SB_BQ, LANES), lambda j, i: (j, i, 0))],
        out_shape=[jax.ShapeDtypeStruct((t, 4 * LANES), F32), jax.ShapeDtypeStruct((8, t, LANES), F32)],
        compiler_params=_params(("parallel", "arbitrary")),
    )(proj, proj, proj)


def _sb2_bwd(proj, dy, lt, *, name):
    t = proj.shape[0]

    def body(q_ref, k_ref, v_ref, dy_ref, lt_ref, dq_ref, dk_ref, dv_ref):
        i = pl.program_id(1)

        @pl.when(i == 0)
        def _():
            dk_ref[...] = jnp.zeros_like(dk_ref)
            dv_ref[...] = jnp.zeros_like(dv_ref)

        lane, q2, diff = _sb_common(q_ref)
        m_le, m_lt = _tri_ones("le"), _tri_ones("lt")
        dy_blk = dy_ref[...]
        do2 = jnp.concatenate([jnp.where(lane < HEAD, dy_blk, 0.0), jnp.where(lane >= HEAD, dy_blk, 0.0)],
                              axis=0).astype(BF16)
        lt2 = jnp.concatenate([lt_ref[0], lt_ref[1]], axis=0)

        def step(kb, carry, diagonal):
            cp, cg, dq = carry
            off = pl.multiple_of(kb * SB_BK, SB_BK)
            kblk = k_ref[pl.ds(off, SB_BK), :].astype(BF16)
            vblk = v_ref[pl.ds(off, SB_BK), :].astype(BF16)
            z = lax.dot_general(q2, kblk, _NT, preferred_element_type=F32)
            lb = _log_sigmoid(z)
            lk = jnp.where(diff < 0, lb - z, 0.0) if diagonal else lb - z
            w2 = _dot2(lk, m_le)
            att = jnp.exp(lb + _rep(lt2 - cp) - w2[:, :SB_BK])
            if diagonal:
                att = jnp.where(diff < 0, att, 0.0)
            d_e = lax.dot_general(do2, vblk, _NT, preferred_element_type=F32) * att
            g2 = _dot2(d_e, m_lt)
            sig = jnp.exp(lb)
            dz = d_e * (1.0 - sig) - (_rep(cg) + g2[:, :SB_BK]) * sig
            dz = (jnp.where(diff < 0, dz, 0.0) if diagonal else dz).astype(BF16)
            dq = dq + lax.dot_general(dz, kblk, _NN, preferred_element_type=F32)
            dk_ref[pl.ds(off, SB_BK), :] += lax.dot_general(dz, q2, _TN, preferred_element_type=F32)
            dv_ref[pl.ds(off, SB_BK), :] += lax.dot_general(att.astype(BF16), do2, _TN, preferred_element_type=F32)
            return cp + w2[:, SB_BK:], cg + g2[:, SB_BK:], dq

        zero = jnp.zeros((2 * SB_BQ, LANES), F32)
        before = lax.fori_loop(0, i, lambda kb, carry: step(kb, carry, False), (zero, zero, zero))
        _, _, dq = step(i, before, True)
        dq_ref[...] = jnp.where(lane < HEAD, dq[:SB_BQ], dq[SB_BQ:]) * (HEAD ** -0.5)

    q_spec, k_spec, v_spec = _sb2_specs(t)
    blk = pl.BlockSpec((SB_BQ, LANES), lambda j, i: (i, j))
    col = pl.BlockSpec((t, LANES), lambda j, i: (0, j))
    return pl.pallas_call(
        body, name=name, grid=(4, t // SB_BQ),
        in_specs=[q_spec, k_spec, v_spec, blk, pl.BlockSpec((2, SB_BQ, LANES), lambda j, i: (j, i, 0))],
        out_specs=[blk, col, col],
        out_shape=[jax.ShapeDtypeStruct((t, 4 * LANES), F32)] * 3,
        compiler_params=_params(("parallel", "arbitrary")),
    )(proj, proj, proj, dy, lt)


SSD_HEADS = 16
SSD_PAIRS = 8


def _split3(x):
    a = x.astype(BF16)
    r = x - a.astype(F32)
    b = r.astype(BF16)
    return a, b, (r - b.astype(F32)).astype(BF16)


def _dot3(x, m, dn=_NN):
    return sum(lax.dot_general(p, m, dn, preferred_element_type=F32) for p in _split3(x))


def _mdot3(m, x):
    return sum(lax.dot_general(m, p, _NN, preferred_element_type=F32) for p in _split3(x))


def _ssd_common(dtr, dtb, alog, acsx_s, acst_s):
    lane = lax.broadcasted_iota(jnp.int32, (CHUNK, LANES), 1)
    lane1 = lax.broadcasted_iota(jnp.int32, (1, LANES), 1)
    arow = jnp.where(lane1 < SSD_HEADS, -jnp.exp(alog), 0.0)
    dt = jnp.where(lane < SSD_HEADS, _softplus(dtr + dtb), 0.0)
    da = dt * arow
    r = lax.broadcasted_iota(jnp.int32, (CHUNK, CHUNK), 0)
    c = lax.broadcasted_iota(jnp.int32, (CHUNK, CHUNK), 1)
    tril = (r >= c).astype(BF16)
    triu = (r <= c).astype(BF16)
    acs = _mdot3(tril, da)
    acst_s[...] = _dot3(da, triu, _TN)
    eh = lax.broadcasted_iota(jnp.int32, (LANES, 8 * LANES), 0)
    e = (eh == lax.broadcasted_iota(jnp.int32, (LANES, 8 * LANES), 1) // HEAD).astype(BF16)
    eh2 = lax.broadcasted_iota(jnp.int32, (LANES, 16 * LANES), 0)
    e2 = (eh2 == lax.broadcasted_iota(jnp.int32, (LANES, 16 * LANES), 1) // LANES).astype(BF16)
    acsx_s[...] = _dot3(acs, e)
    return dt, arow, _dot3(dt, e), _dot3(acs, e2), e, tril, triu


def _ssd_fwd(xc, proj, dtb, alog, dsk, *, name):
    t = xc.shape[0]
    nc = t // CHUNK

    def body(x_ref, b_ref, c_ref, dtr_ref, dtb_ref, alog_ref, dsk_ref, y_ref, hin_ref, acsx_s, acst_s, h_s):
        @pl.when(pl.program_id(0) == 0)
        def _():
            h_s[...] = jnp.zeros_like(h_s)

        dt, arow, dt_x, acs_b, e, tril, _ = _ssd_common(dtr_ref[...], dtb_ref[...], alog_ref[...], acsx_s, acst_s)
        dsk_x = _dot3(jnp.broadcast_to(dsk_ref[...], (CHUNK, LANES)), e)
        lane = lax.broadcasted_iota(jnp.int32, (CHUNK, LANES), 1)
        causal = (lax.broadcasted_iota(jnp.int32, (CHUNK, CHUNK), 0)
                  >= lax.broadcasted_iota(jnp.int32, (CHUNK, CHUNK), 1))
        for j in range(SSD_PAIRS):
            g = j // 4
            sl = slice(j * LANES, (j + 1) * LANES)
            if j % 4 == 0:
                bg = jnp.where(lane // HEAD == g, b_ref[...], 0.0)
                cg = jnp.where(lane // HEAD == g, c_ref[...], 0.0)
                cb = _dot_nt(cg, bg)
            x = x_ref[:, sl]
            a = acsx_s[:, sl]
            at = acsx_s[CHUNK - 1:CHUNK, sl]
            xdt = x * dt_x[:, sl]
            hin = h_s[j]
            hin_ref[0, j] = hin
            y = jnp.exp(a) * _dot_nn(cg, hin) + x * dsk_x[:, sl]
            h_s[j] = jnp.exp(at) * hin + _dot_tn(bg, xdt * jnp.exp(at - a))
            yd = []
            for hh in (0, 1):
                h = 2 * j + hh
                dec = jnp.exp(jnp.minimum(acs_b[:, h * LANES:(h + 1) * LANES] - acst_s[pl.ds(h, 1), :], 0.0))
                yd.append(_dot_nn(jnp.where(causal, cb * dec, 0.0), xdt))
            y_ref[:, sl] = y + jnp.where(lane < HEAD, yd[0], yd[1])

    one = pl.BlockSpec((1, LANES), lambda i: (0, 0))
    return pl.pallas_call(
        body, name=name, grid=(nc,),
        in_specs=[pl.BlockSpec((CHUNK, 8 * LANES), lambda i: (i, 0)),
                  pl.BlockSpec((CHUNK, LANES), lambda i: (i, 8)),
                  pl.BlockSpec((CHUNK, LANES), lambda i: (i, 9)),
                  pl.BlockSpec((CHUNK, LANES), lambda i: (i, C_DT // LANES)), one, one, one],
        out_specs=[pl.BlockSpec((CHUNK, 8 * LANES), lambda i: (i, 0)),
                   pl.BlockSpec((1, SSD_PAIRS, LANES, LANES), lambda i: (i, 0, 0, 0))],
        out_shape=[jax.ShapeDtypeStruct((t, 8 * LANES), F32),
                   jax.ShapeDtypeStruct((nc, SSD_PAIRS, LANES, LANES), F32)],
        scratch_shapes=[pltpu.VMEM((CHUNK, 8 * LANES), F32), pltpu.VMEM((LANES, CHUNK), F32),
                        pltpu.VMEM((SSD_PAIRS, LANES, LANES), F32)],
        compiler_params=_params(("arbitrary",)),
    )(xc, xc, xc, proj, dtb, alog, dsk)


def _ssd_bwd(xc, proj, dtb, alog, dsk, hin_all, dy, *, name):
    t = xc.shape[0]
    nc = t // CHUNK

    def body(x_ref, b_ref, c_ref, dtr_ref, dtb_ref, alog_ref, dsk_ref, hin_ref, dy_ref,
             dxc_ref, ddtr_ref, ddtb_ref, dalog_ref, ddsk_ref, acsx_s, acst_s, dh_s, dax_s, ddx_s):
        @pl.when(pl.program_id(0) == 0)
        def _():
            dh_s[...] = jnp.zeros_like(dh_s)
            ddtb_ref[...] = jnp.zeros_like(ddtb_ref)
            dalog_ref[...] = jnp.zeros_like(dalog_ref)
            ddsk_ref[...] = jnp.zeros_like(ddsk_ref)

        dtr = dtr_ref[...]
        dtb = dtb_ref[...]
        dt, arow, dt_x, acs_b, e, tril, triu = _ssd_common(dtr, dtb, alog_ref[...], acsx_s, acst_s)
        dsk_x = _dot3(jnp.broadcast_to(dsk_ref[...], (CHUNK, LANES)), e)
        lane = lax.broadcasted_iota(jnp.int32, (CHUNK, LANES), 1)
        rowi = lax.broadcasted_iota(jnp.int32, (CHUNK, LANES), 0)
        causal = (lax.broadcasted_iota(jnp.int32, (CHUNK, CHUNK), 0)
                  >= lax.broadcasted_iota(jnp.int32, (CHUNK, CHUNK), 1))
        dacs = jnp.zeros((CHUNK, LANES), F32)
        d_b = jnp.zeros((CHUNK, LANES), F32)
        d_c = jnp.zeros((CHUNK, LANES), F32)
        for j in range(SSD_PAIRS):
            g = j // 4
            sl = slice(j * LANES, (j + 1) * LANES)
            if j % 4 == 0:
                bg = jnp.where(lane // HEAD == g, b_ref[...], 0.0)
                cg = jnp.where(lane // HEAD == g, c_ref[...], 0.0)
                cb = _dot_nt(cg, bg)
                dcb = jnp.zeros((CHUNK, CHUNK), F32)
            x = x_ref[:, sl]
            d = dt_x[:, sl]
            a = acsx_s[:, sl]
            at = acsx_s[CHUNK - 1:CHUNK, sl]
            xdt = x * d
            hin = hin_ref[0, j]
            dhout = dh_s[j]
            dyp = dy_ref[:, sl]
            ea, eat, ed = jnp.exp(a), jnp.exp(at), jnp.exp(at - a)
            da_l = dyp * ea * _dot_nn(cg, hin)
            dm = dyp * ea
            d_c = d_c + _dot_nt(dm, hin)
            dh_s[j] = _dot_tn(cg, dm) + eat * dhout
            dat = jnp.sum(dhout * hin * eat, axis=0, keepdims=True)
            d_b = d_b + _dot_nt(xdt * ed, dhout)
            dw = _dot_nn(bg, dhout)
            dxdt = dw * ed
            ded = dw * xdt * ed
            dat = dat + jnp.sum(ded, axis=0, keepdims=True)
            da_l = da_l - ded
            for hh in (0, 1):
                h = 2 * j + hh
                dec = jnp.exp(jnp.minimum(acs_b[:, h * LANES:(h + 1) * LANES] - acst_s[pl.ds(h, 1), :], 0.0))
                gm = jnp.where(causal, cb * dec, 0.0)
                dyh = jnp.where(lane // HEAD == hh, dyp, 0.0)
                dg = _dot_nt(dyh, xdt)
                dxdt = dxdt + _dot_tn(gm, dyh)
                dcb = dcb + jnp.where(causal, dg * dec, 0.0)
                th = dg * gm
                oh = (lane == h).astype(BF16)
                dacs = dacs + _dot2(th, oh) - _dot2_tn(th, oh)
            if j % 4 == 3:
                d_c = d_c + _dot_nn(dcb, bg)
                d_b = d_b + _dot_tn(dcb, cg)
            dxc_ref[:, sl] = dyp * dsk_x[:, sl] + dxdt * d
            ddx_s[:, sl] = dxdt * x
            dax_s[:, sl] = da_l + jnp.where(rowi == CHUNK - 1, dat, 0.0)
            dskp = jnp.sum(dyp * x, axis=0, keepdims=True)
            ddsk_ref[...] += _dot2(jnp.broadcast_to(dskp, (8, LANES)), e[:, sl], _NT)
        dxc_ref[:, 8 * LANES:9 * LANES] = d_b
        dxc_ref[:, 9 * LANES:10 * LANES] = d_c
        dacs = dacs + _dot2(dax_s[...], e, _NT)
        ddt = _dot2(ddx_s[...], e, _NT)
        dda = _mdot3(triu, dacs)
        ddt = ddt + dda * arow
        dalog_ref[...] += jnp.sum(dda * dt, axis=0, keepdims=True) * arow
        ddtr = jnp.where(lane < SSD_HEADS, ddt * _sigmoid(dtr + dtb), 0.0)
        ddtr_ref[...] = ddtr
        ddtb_ref[...] += jnp.sum(ddtr, axis=0, keepdims=True)

    one = pl.BlockSpec((1, LANES), lambda i: (0, 0))
    rev = lambda c: (lambda i: (nc - 1 - i, c))
    return pl.pallas_call(
        body, name=name, grid=(nc,),
        in_specs=[pl.BlockSpec((CHUNK, 8 * LANES), rev(0)), pl.BlockSpec((CHUNK, LANES), rev(8)),
                  pl.BlockSpec((CHUNK, LANES), rev(9)), pl.BlockSpec((CHUNK, LANES), rev(C_DT // LANES)),
                  one, one, one,
                  pl.BlockSpec((1, SSD_PAIRS, LANES, LANES), lambda i: (nc - 1 - i, 0, 0, 0)),
                  pl.BlockSpec((CHUNK, 8 * LANES), rev(0))],
        out_specs=[pl.BlockSpec((CHUNK, XBC_COLS), rev(0)), pl.BlockSpec((CHUNK, LANES), rev(0)), one, one,
                   pl.BlockSpec((8, LANES), lambda i: (0, 0))],
        out_shape=[jax.ShapeDtypeStruct((t, XBC_COLS), F32), jax.ShapeDtypeStruct((t, LANES), F32)]
        + [jax.ShapeDtypeStruct((1, LANES), F32)] * 2 + [jax.ShapeDtypeStruct((8, LANES), F32)],
        scratch_shapes=[pltpu.VMEM((CHUNK, 8 * LANES), F32), pltpu.VMEM((LANES, CHUNK), F32),
                        pltpu.VMEM((SSD_PAIRS, LANES, LANES), F32),
                        pltpu.VMEM((CHUNK, 8 * LANES), F32), pltpu.VMEM((CHUNK, 8 * LANES), F32)],
        compiler_params=_params(("arbitrary",)),
    )(xc, xc, xc, proj, dtb, alog, dsk, hin_all, dy)


RW_LW = 128
RW_PAIRS = 4 * LANES // RW_LW
RW_BT = 16
RW_DECAY_ROW = 1
RW_BWD_PAIRS = 4


def _rw_consts():
    seg = _seg_matrix(RW_LW)
    ti = (lax.broadcasted_iota(jnp.int32, (HEAD, RW_LW), 0)
          == lax.broadcasted_iota(jnp.int32, (HEAD, RW_LW), 1) % HEAD)
    return seg, ti


def _col_tiles(rows, ti, seg):
    tib = ti.astype(BF16)
    n = len(rows)
    hi = [r.astype(BF16) for r in rows]
    w_lo = (rows[RW_DECAY_ROW] - hi[RW_DECAY_ROW].astype(F32)).astype(BF16)
    out = lax.dot_general(jnp.concatenate([tib * h for h in hi + [w_lo]], axis=0), seg, _NN, preferred_element_type=F32)
    tiles = [out[i * HEAD:(i + 1) * HEAD] for i in range(n)]
    tiles[RW_DECAY_ROW] = tiles[RW_DECAY_ROW] + out[n * HEAD:(n + 1) * HEAD]
    return tiles


def _col_tiles2(rows, ti, seg):
    tib = ti.astype(BF16)
    hi = [r.astype(BF16) for r in rows]
    lo = [(r - h.astype(F32)).astype(BF16) for r, h in zip(rows, hi)]
    out = (lax.dot_general(jnp.concatenate([tib * h for h in hi], axis=0), seg, _NN, preferred_element_type=F32)
           + lax.dot_general(jnp.concatenate([tib * l for l in lo], axis=0), seg, _NN, preferred_element_type=F32))
    return [out[i * HEAD:(i + 1) * HEAD] for i in range(len(rows))]


def _head_lane_sums(tiles, ti, seg):
    out = _dot2(jnp.concatenate(tiles, axis=0), seg)
    return [jnp.sum(jnp.where(ti, out[i * HEAD:(i + 1) * HEAD], 0.0), axis=0, keepdims=True) for i in range(len(tiles))]


def _rw_scan_fwd(mixed, w, k, n, b, *, name):
    t = w.shape[0]

    def body(r_ref, v_ref, w_ref, k_ref, n_ref, b_ref, y_ref, st_ref, s_s):
        @pl.when(pl.program_id(0) == 0)
        def _():
            s_s[...] = jnp.zeros_like(s_s)

        seg, ti = _rw_consts()

        def step(tt, state):
            row = pl.ds(tt, 1)
            new = []
            for p in range(RW_PAIRS):
                sl = pl.ds(p * RW_LW, RW_LW)
                s = state[p]
                ncol, wcol, bcol, kcol, rcol = _col_tiles(
                    [x[row, sl] for x in (n_ref, w_ref, b_ref, k_ref, r_ref)], ti, seg)
                sa = jnp.sum(s * ncol, axis=0, keepdims=True)
                s = s * wcol + bcol * sa + kcol * v_ref[row, sl]
                y_ref[row, sl] = jnp.sum(s * rcol, axis=0, keepdims=True)
                st_ref[tt, p] = s
                new.append(s)
            return tuple(new)

        out = tuple(s_s[p] for p in range(RW_PAIRS))
        for tt in range(RW_BT):
            out = step(tt, out)
        for p in range(RW_PAIRS):
            s_s[p] = out[p]

    blk = lambda c: pl.BlockSpec((RW_BT, 4 * LANES), functools.partial(lambda i, c: (i, c), c=c))
    return pl.pallas_call(
        body, name=name, grid=(t // RW_BT,),
        in_specs=[blk(0), blk(2), blk(0), blk(0), blk(0), blk(0)],
        out_specs=[blk(0), pl.BlockSpec((RW_BT, RW_PAIRS, HEAD, RW_LW), lambda i: (i, 0, 0, 0))],
        out_shape=[jax.ShapeDtypeStruct((t, 4 * LANES), F32),
                   jax.ShapeDtypeStruct((t, RW_PAIRS, HEAD, RW_LW), F32)],
        scratch_shapes=[pltpu.VMEM((RW_PAIRS, HEAD, RW_LW), F32)],
        compiler_params=_params(("arbitrary",)),
    )(mixed, mixed, w, k, n, b)


def _rw_scan_bwd(mixed, w, k, n, b, states, dy, dr0, dk0, dv0, *, name):
    t = w.shape[0]
    nb = t // RW_BT
    ppc = RW_BWD_PAIRS
    ng = RW_PAIRS // ppc

    def body(r_ref, v_ref, w_ref, k_ref, n_ref, b_ref, st_ref, prev_ref, dy_ref, dr0_ref, dk0_ref, dv0_ref,
             dr_ref, dw_ref, dk_ref, dv_ref, dn_ref, db_ref, ds_s):
        @pl.when(pl.program_id(1) == 0)
        def _():
            ds_s[...] = jnp.zeros_like(ds_s)

        seg, ti = _rw_consts()
        has_prev = (pl.program_id(1) < nb - 1).astype(F32)

        def step(it, carry):
            tt = RW_BT - 1 - it
            row = pl.ds(tt, 1)
            prev_t = max(tt - 1, 0)
            new_ds, new_s = [], []
            for p in range(ppc):
                sl = pl.ds(p * RW_LW, RW_LW)
                ds, s_t = carry[p], carry[ppc + p]
                s_p = st_ref[prev_t, p] if tt > 0 else prev_ref[0, p] * has_prev
                ncol, wcol, bcol, kcol, rcol = _col_tiles2(
                    [x[row, sl] for x in (n_ref, w_ref, b_ref, k_ref, r_ref)], ti, seg)
                vv, dyy = v_ref[row, sl], dy_ref[row, sl]
                sa = jnp.sum(s_p * ncol, axis=0, keepdims=True)
                ds = ds + rcol * dyy
                dsa = jnp.sum(ds * bcol, axis=0, keepdims=True)
                dv_ref[row, sl] = jnp.sum(ds * kcol, axis=0, keepdims=True) + dv0_ref[row, sl]
                dr, dw, db, dk, dn = _head_lane_sums([s_t * dyy, ds * s_p, ds * sa, ds * vv, s_p * dsa], ti, seg)
                dr_ref[row, sl] = dr + dr0_ref[row, sl]
                dw_ref[row, sl] = dw
                db_ref[row, sl] = db
                dk_ref[row, sl] = dk + dk0_ref[row, sl]
                dn_ref[row, sl] = dn
                new_ds.append(ds * wcol + ncol * dsa)
                new_s.append(s_p)
            return tuple(new_ds) + tuple(new_s)

        init = tuple(ds_s[p] for p in range(ppc)) + tuple(st_ref[RW_BT - 1, p] for p in range(ppc))
        out = init
        for it in range(RW_BT):
            out = step(it, out)
        for p in range(ppc):
            ds_s[p] = out[p]

    blk = lambda c: pl.BlockSpec((RW_BT, ppc * RW_LW), functools.partial(lambda g, i, c: (nb - 1 - i, c * ng + g), c=c))
    st_spec = pl.BlockSpec((RW_BT, ppc, HEAD, RW_LW), lambda g, i: (nb - 1 - i, g, 0, 0))
    prev_spec = pl.BlockSpec((1, ppc, HEAD, RW_LW), lambda g, i: (jnp.maximum((nb - 1 - i) * RW_BT - 1, 0), g, 0, 0))
    return pl.pallas_call(
        body, name=name, grid=(ng, nb),
        in_specs=[blk(0), blk(2), blk(0), blk(0), blk(0), blk(0), st_spec, prev_spec, blk(0), blk(0), blk(0), blk(0)],
        out_specs=[blk(0)] * 6,
        out_shape=[jax.ShapeDtypeStruct((t, 4 * LANES), F32)] * 6,
        scratch_shapes=[pltpu.VMEM((ppc, HEAD, RW_LW), F32)],
        compiler_params=_params(("parallel", "arbitrary")),
    )(mixed, mixed, w, k, n, b, states, states, dy, dr0, dk0, dv0)


RW_C = 64


def _p3(a, b, dn):
    ah, al = _split2(a)
    bh, bl = _split2(b)
    d = lambda x, y: lax.dot_general(x, y, dn, preferred_element_type=F32)
    return d(ah, bh) + d(ah, bl) + d(al, bh)


_BNN = (((2,), (1,)), ((0,), (0,)))
_BNT = (((2,), (2,)), ((0,), (0,)))
_BTN = (((1,), (1,)), ((0,), (0,)))


@jax.custom_vjp
def _pnn(a, b):
    return _p3(a, b, _BNN)


@jax.custom_vjp
def _pnt(a, b):
    return _p3(a, b, _BNT)


@jax.custom_vjp
def _ptn(a, b):
    return _p3(a, b, _BTN)


_pnn.defvjp(lambda a, b: (_p3(a, b, _BNN), (a, b)), lambda res, g: (_p3(g, res[1], _BNT), _p3(res[0], g, _BTN)))
_pnt.defvjp(lambda a, b: (_p3(a, b, _BNT), (a, b)), lambda res, g: (_p3(g, res[1], _BNN), _p3(g, res[0], _BTN)))
_ptn.defvjp(lambda a, b: (_p3(a, b, _BTN), (a, b)), lambda res, g: (_p3(res[1], g, _BNT), _p3(res[0], g, _BNN)))


def _rw_chunk_consts():
    c2 = 2 * RW_C
    row = lax.broadcasted_iota(jnp.int32, (c2, c2), 0)
    col = lax.broadcasted_iota(jnp.int32, (c2, c2), 1)
    same = (row // RW_C) == (col // RW_C)
    strict = (same & (row > col)).astype(F32)
    incl = (same & (row >= col)).astype(F32)
    eye = (row == col).astype(F32)
    tr = lax.broadcasted_iota(jnp.int32, (RW_C, RW_C), 0)
    tc = lax.broadcasted_iota(jnp.int32, (RW_C, RW_C), 1)
    tril = (tr >= tc).astype(F32)
    lane = lax.broadcasted_iota(jnp.int32, (1, LANES), 1)
    hm = [(lane // HEAD == h).astype(F32) for h in (0, 1)]
    return strict, incl, eye, tril, hm


def _rw_chunk(r, lw, k, v, n, b, s2, consts):
    strict, incl, eye, tril, hm = consts
    two = lambda x: jnp.concatenate([x * hm[0], x * hm[1]], axis=1)
    cum = _pnn(jnp.broadcast_to(tril, (4, RW_C, RW_C)), lw)
    grow, shrink = jnp.exp(-cum), jnp.exp(cum)
    n2, r2 = two(n * jnp.exp(cum - lw)), two(r * shrink)
    b2, k2, v2 = two(b * grow), two(k * grow), two(v)
    p = _pnt(n2, b2) * strict
    x2 = _pnt(n2, s2) + _pnn(_pnt(n2, k2) * strict, v2)
    t_inv, a = eye + p, p
    for _ in range(RW_C.bit_length() - 2):
        a = _pnn(a, a)
        t_inv = t_inv + _pnn(t_inv, a)
    u2 = _pnn(t_inv, x2)
    y2 = _pnt(r2, s2) + _pnn(_pnt(r2, b2) * incl, u2) + _pnn(_pnt(r2, k2) * incl, v2)
    s2_new = (s2 + _ptn(u2, b2) + _ptn(v2, k2)) * jnp.exp(jnp.sum(lw, axis=1, keepdims=True))
    return jnp.sum(y2.reshape(4, 2, RW_C, LANES), axis=1), s2_new


def _pairs(ref):
    return jnp.stack([ref[:, p * LANES:(p + 1) * LANES] for p in range(4)])


def _rw_chunk_fwd(mixed, lw, k, n, b, *, name, side=None):
    t = lw.shape[0]
    nc = t // RW_C

    def body(r_ref, v_ref, lw_ref, k_ref, n_ref, b_ref, y_ref, sin_ref, s_s):
        @pl.when(pl.program_id(0) == 0)
        def _():
            s_s[...] = jnp.zeros_like(s_s)

        s2 = s_s[...]
        sin_ref[0] = s2
        y, s2 = _rw_chunk(*[_pairs(x) for x in (r_ref, lw_ref, k_ref, v_ref, n_ref, b_ref)], s2, _rw_chunk_consts())
        for p in range(4):
            y_ref[:, p * LANES:(p + 1) * LANES] = y[p]
        s_s[...] = s2

    blk = lambda c: pl.BlockSpec((RW_C, 4 * LANES), functools.partial(lambda i, c: (i, c), c=c))
    return _call_with_side(
        body, side, name=name, steps=nc,
        in_specs=[blk(0), blk(2), blk(0), blk(0), blk(0), blk(0)],
        out_specs=[blk(0), pl.BlockSpec((1, 4, LANES, LANES), lambda i: (i, 0, 0, 0))],
        out_shape=[jax.ShapeDtypeStruct((t, 4 * LANES), F32), jax.ShapeDtypeStruct((nc, 4, LANES, LANES), F32)],
        scratch_shapes=[pltpu.VMEM((4, LANES, LANES), F32)],
        operands=(mixed, mixed, lw, k, n, b))


def _call_with_side(body, side, *, name, steps, in_specs, out_specs, out_shape, scratch_shapes, operands):
    if side is None:
        return pl.pallas_call(body, name=name, grid=(steps,), in_specs=in_specs, out_specs=out_specs, out_shape=out_shape,
                              scratch_shapes=scratch_shapes, compiler_params=_params(("arbitrary",)))(*operands)
    srcs, per_dest = side
    ns, ni, no, nscr = len(srcs), len(in_specs), len(out_specs), len(scratch_shapes)

    def full_body(*refs):
        ins, side_in = refs[:ni], refs[ni:ni + ns]
        outs, side_out = refs[ni + ns:ni + ns + no], refs[ni + ns + no:ni + 2 * ns + no]
        scratch, sems = refs[ni + 2 * ns + no:ni + 2 * ns + no + nscr], refs[ni + 2 * ns + no + nscr:]

        @pl.when(pl.program_id(0) == 0)
        def _():
            _exchange(side_in, side_out, sems, per_dest, start=True, wait=False)

        body(*ins, *outs, *scratch)

        @pl.when(pl.program_id(0) == steps - 1)
        def _():
            _exchange(side_in, side_out, sems, per_dest, start=False, wait=True)

    res = pl.pallas_call(
        full_body, name=name, grid=(steps,), in_specs=list(in_specs) + [_ANY] * ns,
        out_specs=list(out_specs) + [_ANY] * ns, out_shape=list(out_shape) + _exchange_out_shapes(srcs),
        scratch_shapes=list(scratch_shapes) + _exchange_sems(ns), compiler_params=_params(("arbitrary",)),
    )(*operands, *srcs)
    return list(res[:no]) + [list(res[no:])]


def _rw_chunk_bwd(mixed, lw, k, n, b, s_in, dy, dr0, dk0, dv0, *, name, side=None):
    t = lw.shape[0]
    nc = t // RW_C

    def body(r_ref, v_ref, lw_ref, k_ref, n_ref, b_ref, sin_ref, dy_ref, dr0_ref, dk0_ref, dv0_ref,
             dr_ref, dlw_ref, dk_ref, dv_ref, dn_ref, db_ref, ds_s):
        @pl.when(pl.program_id(0) == 0)
        def _():
            ds_s[...] = jnp.zeros_like(ds_s)

        consts = _rw_chunk_consts()
        args = [_pairs(x) for x in (r_ref, lw_ref, k_ref, v_ref, n_ref, b_ref)] + [sin_ref[0]]
        _, vjp = jax.vjp(lambda *a: _rw_chunk(*a, consts), *args)
        dr, dlw, dk, dv, dn, db, ds = vjp((_pairs(dy_ref), ds_s[...]))
        for p in range(4):
            sl = slice(p * LANES, (p + 1) * LANES)
            dr_ref[:, sl] = dr[p] + dr0_ref[:, sl]
            dlw_ref[:, sl] = dlw[p]
            dk_ref[:, sl] = dk[p] + dk0_ref[:, sl]
            dv_ref[:, sl] = dv[p] + dv0_ref[:, sl]
            dn_ref[:, sl] = dn[p]
            db_ref[:, sl] = db[p]
        ds_s[...] = ds

    blk = lambda c: pl.BlockSpec((RW_C, 4 * LANES), functools.partial(lambda i, c: (nc - 1 - i, c), c=c))
    return _call_with_side(
        body, side, name=name, steps=nc,
        in_specs=[blk(0), blk(2), blk(0), blk(0), blk(0), blk(0),
                  pl.BlockSpec((1, 4, LANES, LANES), lambda i: (nc - 1 - i, 0, 0, 0)), blk(0), blk(0), blk(0), blk(0)],
        out_specs=[blk(0)] * 6,
        out_shape=[jax.ShapeDtypeStruct((t, 4 * LANES), F32)] * 6,
        scratch_shapes=[pltpu.VMEM((4, LANES, LANES), F32)],
        operands=(mixed, mixed, lw, k, n, b, s_in, dy, dr0, dk0, dv0))


def _f_rms_res(x, g):
    return _f_rms(x, g)[0], x


def _final(x, g, target, *, bt, name):
    t, d = x.shape

    def body(x_ref, g_ref, t_ref, dx_ref, loss_ref, dg_ref):
        tgt = t_ref[...]

        def f(xv, gv):
            err = _f_rms(xv, gv)[0] - tgt
            return 0.5 * jnp.mean(err * err, axis=-1, keepdims=True)

        row_loss, vjp = jax.vjp(f, x_ref[...], g_ref[...])
        dx, dg = vjp(jnp.ones_like(row_loss))
        dx_ref[...] = dx

        @pl.when(pl.program_id(0) == 0)
        def _():
            loss_ref[...] = jnp.zeros_like(loss_ref)
            dg_ref[...] = jnp.zeros_like(dg_ref)

        loss_ref[...] += jnp.broadcast_to(jnp.sum(row_loss, axis=0, keepdims=True), (1, LANES))
        dg_ref[...] += dg

    blk = pl.BlockSpec((bt, d), lambda i: (i, 0))
    return pl.pallas_call(
        body, name=name, grid=(t // bt,),
        in_specs=[blk, pl.BlockSpec((1, d), lambda i: (0, 0)), blk],
        out_specs=[blk, pl.BlockSpec((1, LANES), lambda i: (0, 0)), pl.BlockSpec((1, d), lambda i: (0, 0))],
        out_shape=[jax.ShapeDtypeStruct((t, d), F32), jax.ShapeDtypeStruct((1, LANES), F32),
                   jax.ShapeDtypeStruct((1, d), F32)],
        compiler_params=_params(("arbitrary",)),
    )(x, g, target)


ADAMW_BLOCK_BYTES = 1 << 20


def _adamw(w, g, m, v, *, name):
    shape = w.shape
    c = shape[-1]
    shape3 = (1,) * (3 - len(shape)) + shape if len(shape) <= 3 else (-1,) + shape[-2:]
    args = [a.reshape(shape3) for a in (w, g, m, v)]
    lead, r, _ = args[0].shape
    br = r
    if r * c * 4 > ADAMW_BLOCK_BYTES:
        cands = [b for b in range(8, r, 8) if r % b == 0 and b * c * 4 <= ADAMW_BLOCK_BYTES]
        br = max(cands) if cands else r

    def body(w_ref, g_ref, m_ref, v_ref, d_ref, nm_ref, nv_ref):
        gv = g_ref[...]
        m_new = ADAM_B1 * m_ref[...] + (1.0 - ADAM_B1) * gv
        v_new = ADAM_B2 * v_ref[...] + (1.0 - ADAM_B2) * (gv * gv)
        m_hat = m_new / (1.0 - ADAM_B1 ** ADAM_STEP)
        v_hat = v_new / (1.0 - ADAM_B2 ** ADAM_STEP)
        d_ref[...] = -ADAM_LR * (m_hat / (jnp.sqrt(v_hat) + ADAM_EPS) + ADAM_WD * w_ref[...])
        nm_ref[...] = m_new
        nv_ref[...] = v_new

    blk = pl.BlockSpec((1, br, c), lambda l, i: (l, i, 0))
    outs = pl.pallas_call(
        body, name=name, grid=(lead, r // br), in_specs=[blk] * 4, out_specs=[blk] * 3,
        out_shape=[jax.ShapeDtypeStruct((lead, r, c), F32)] * 3,
        compiler_params=_params(("parallel", "parallel")),
    )(*args)
    return tuple(o.reshape(shape) for o in outs)


BT = 256
BC = 128


def _layer_rows(x, proj, s):
    s = {k: s.get(k) for k in ("y_sb_raw", "y_ssd_raw", "mixed", "ys", "k2", "p_sb", "p_ssd", "p_rw")}
    return dict(
        rms=[(x, D_MODEL, 0)],
        sb_gate=[(s["y_sb_raw"], 512, 0), (proj, 512, 3)],
        ssd_norm=[(s["y_ssd_raw"], 1024, 0), (proj, 1024, C_Z // 1024)],
        rw_pre=[(s["mixed"], 512, 1), (s["mixed"], LANES, 16)],
        rw_post=[(s["ys"], 512, 0), (s["mixed"], 512, 0), (s["k2"], 512, 0), (s["mixed"], 512, 2), (s["mixed"], 512, 3)],
        merge=[(s["p_sb"], 1024, 0), (s["p_ssd"], 1024, 0), (s["p_rw"], 1024, 0),
               (proj, 1024, 3), (proj, 1024, 4), (proj, 1024, 5)],
    )


def _layer_fwd(x, p, nm, side=None):
    s = {}
    (s["h"],) = _rowwise(_f_rms, [(x, D_MODEL, 0)], [p["norm_g"]], [D_MODEL], bt=BT, name=nm + "rms")
    proj = s["proj"] = _mm(s["h"], p["w_in"], name=nm + "proj")
    s["y_sb_raw"], s["lt"] = _sb2_fwd(proj, name=nm + "sb")
    s["xc"] = _colwise(_f_conv, proj, C_XBC, XBC_COLS, p["conv"], bc=BC, name=nm + "conv")
    s["y_ssd_raw"], s["hin"] = _ssd_fwd(s["xc"], proj, p["dt_bias"], p["a_log"], p["d_skip"], name=nm + "ssd")
    s["mixed"] = _colwise(_f_rw_mix, proj, C_RW, RW_COLS, [p["rw_mu"]], bc=BC, name=nm + "mix")
    s["w"], s["k2"], s["n"], s["b"] = _rowwise(_f_rw_pre, [(s["mixed"], 512, 1), (s["mixed"], LANES, 16)], p["rw_pre"],
                                               [512] * 4, bt=BT, name=nm + "rwpre")
    s["ys"], s["st"], *exchanged = _rw_chunk_fwd(s["mixed"], s["w"], s["k2"], s["n"], s["b"], name=nm + "scan", side=side)
    rows = _layer_rows(x, proj, s)
    (s["y_sb"],) = _rowwise(_f_sb_gate, rows["sb_gate"], [], [512], bt=BT, name=nm + "sbgate")
    (s["y_ssd"],) = _rowwise(_f_ssd_norm, rows["ssd_norm"], [p["ssd_norm_g"]], [1024], bt=BT, name=nm + "ssdnorm")
    (s["y_rw"],) = _rowwise(_f_rw_post, rows["rw_post"], p["rw_post"], [512], bt=BT, name=nm + "rwpost")
    s["p_sb"] = _mm(s["y_sb"], p["w_out_sb"], name=nm + "osb")
    s["p_ssd"] = _mm(s["y_ssd"], p["w_out_ssd"], name=nm + "ossd")
    s["p_rw"] = _mm(s["y_rw"], p["w_out_rw"], name=nm + "orw")
    (s["merged"],) = _rowwise(_f_merge, _layer_rows(x, proj, s)["merge"], [], [1024], bt=BT, name=nm + "merge")
    return _mm(s["merged"], p["w_o"], add=x, name=nm + "wo"), s, (exchanged[0] if exchanged else None)


def _layer_bwd(x, dx_out, p, s, nm, side=None):
    g = {}
    proj = s["proj"]
    rows = _layer_rows(x, proj, s)
    g["w_o"] = _mm(s["merged"], dx_out, ta=True, name=nm + "g_wo")
    d_merged = _mm(dx_out, p["w_o"], tb=True, name=nm + "d_merged")
    dp_sb, dp_ssd, dp_rw, d_gates = _rowwise_bwd(_f_merge, rows["merge"], [], [(d_merged, 1024, 0)], bt=BT,
                                                 name=nm + "merge_b", groups=[[0], [1], [2], [3, 4, 5]])
    g["w_out_sb"] = _mm(s["y_sb"], dp_sb, ta=True, name=nm + "g_osb")
    g["w_out_ssd"] = _mm(s["y_ssd"], dp_ssd, ta=True, name=nm + "g_ossd")
    g["w_out_rw"] = _mm(s["y_rw"], dp_rw, ta=True, name=nm + "g_orw")
    dy_sb = _mm(dp_sb, p["w_out_sb"], tb=True, name=nm + "d_ysb")
    dy_ssd = _mm(dp_ssd, p["w_out_ssd"], tb=True, name=nm + "d_yssd")
    dy_rw = _mm(dp_rw, p["w_out_rw"], tb=True, name=nm + "d_yrw")
    dy_sb_raw, d_sbgate = _rowwise_bwd(_f_sb_gate, rows["sb_gate"], [], [(dy_sb, 512, 0)], bt=BT, name=nm + "sbgate_b")
    dq, dk, dv = _sb2_bwd(proj, dy_sb_raw, s["lt"], name=nm + "sb_b")
    dy_ssd_raw, dz, g["ssd_norm_g"] = _rowwise_bwd(_f_ssd_norm, rows["ssd_norm"], [p["ssd_norm_g"]],
                                                   [(dy_ssd, 1024, 0)], bt=BT, name=nm + "ssdnorm_b")
    dxc, ddtr, g["dt_bias"], g["a_log"], g["d_skip"] = _ssd_bwd(
        s["xc"], proj, p["dt_bias"], p["a_log"], p["d_skip"], s["hin"], dy_ssd_raw, name=nm + "ssd_b")
    conv_out = _colwise_bwd(_f_conv, proj, C_XBC, XBC_COLS, p["conv"], dxc, bc=BC, name=nm + "conv_b")
    dxbc, g["conv"] = conv_out[0], conv_out[1:]
    dys, dr0, dk0, dv0, d_rwgate, g["rw_ln_g"], g["rw_ln_b"], g["rw_r_k"] = _rowwise_bwd(
        _f_rw_post, rows["rw_post"], p["rw_post"], [(dy_rw, 512, 0)], bt=BT, name=nm + "rwpost_b")
    dr, dw, dk2, dvv, dn, db, *exchanged = _rw_chunk_bwd(s["mixed"], s["w"], s["k2"], s["n"], s["b"], s["st"], dys,
                                                         dr0, dk0, dv0, name=nm + "scan_b", side=side)
    pre_out = _rowwise_bwd(_f_rw_pre, rows["rw_pre"], p["rw_pre"],
                           [(dw, 512, 0), (dk2, 512, 0), (dn, 512, 0), (db, 512, 0)], bt=BT, name=nm + "rwpre_b")
    dkm, dlo, g["rw_pre"] = pre_out[0], pre_out[1], pre_out[2:]
    d_mixed = jnp.concatenate([dr, dkm, dvv, d_rwgate, dlo], axis=1)
    d_slab, g["rw_mu"] = _colwise_bwd(_f_rw_mix, proj, C_RW, RW_COLS, [p["rw_mu"]], d_mixed, bc=BC, name=nm + "mix_b")
    d_proj = jnp.concatenate([dq, dk, dv, d_sbgate, dz, d_gates, d_slab, ddtr, dxbc], axis=1)
    g["w_in"] = _mm(s["h"], d_proj, ta=True, name=nm + "g_win")
    dh = _mm(d_proj, p["w_in"], tb=True, tn=1024, tk=512, name=nm + "d_h")
    dx, g["norm_g"] = _rowwise_bwd(_f_rms_res, rows["rms"], [p["norm_g"]], [(dh, D_MODEL, 0), (dx_out, D_MODEL, 0)],
                                   bt=BT, name=nm + "rms_b")
    return dx, g, (exchanged[0] if exchanged else None)


MESH = pl.DeviceIdType.MESH
N_DEV = 8
_ANY = pl.BlockSpec(memory_space=pl.ANY)
_CHIP_SEMS = [pltpu.SemaphoreType.DMA((3,)), pltpu.SemaphoreType.DMA((3,)), pltpu.SemaphoreType.DMA]


def _here():
    x, y, c = lax.axis_index("x"), lax.axis_index("y"), lax.axis_index("c")
    return x, y, c, [(1 - x, y), (x, 1 - y), (1 - x, 1 - y)]


def _chip_exchange(srcs, *, per_dest, name):
    n = len(srcs)

    def body(*refs):
        _exchange(refs[:n], refs[n:2 * n], refs[2 * n:], per_dest, start=True, wait=True)

    return pl.pallas_call(
        body, name=name, in_specs=[_ANY] * n, out_specs=[_ANY] * n,
        out_shape=_exchange_out_shapes(srcs), scratch_shapes=_exchange_sems(n),
    )(*srcs)


def _exchange_out_shapes(srcs):
    return [jax.ShapeDtypeStruct((4,) + s.shape[1:], s.dtype) for s in srcs]


def _exchange_sems(n):
    return [pltpu.SemaphoreType.DMA((3 * n,)), pltpu.SemaphoreType.DMA((3 * n,)), pltpu.SemaphoreType.DMA((n,))]


def _exchange(src_refs, out_refs, sems, per_dest, *, start, wait):
    send_sems, recv_sems, local_sems = sems
    x, y, c, chips = _here()
    me = 2 * x + y
    owns, sends, recvs = [], [], []
    for a, (src_ref, out_ref) in enumerate(zip(src_refs, out_refs)):
        pick = (lambda q, s=src_ref: s.at[q]) if per_dest else (lambda q, s=src_ref: s.at[c])
        owns.append(pltpu.make_async_copy(pick(me), out_ref.at[me], local_sems.at[a]))
        for j, (px, py) in enumerate(chips):
            sends.append(pltpu.make_async_remote_copy(
                pick(2 * px + py), out_ref.at[me], send_sems.at[3 * a + j], recv_sems.at[3 * a + j],
                device_id=(px, py, c), device_id_type=MESH))
            recvs.append(pltpu.make_async_remote_copy(
                src_ref.at[0], out_ref.at[2 * px + py], send_sems.at[3 * a + j], recv_sems.at[3 * a + j],
                device_id=(px, py, c), device_id_type=MESH))
    if start:
        for cp in owns + sends:
            cp.start()
    if wait:
        for cp in recvs:
            cp.wait_recv()
        for cp in sends:
            cp.wait_send()
        for cp in owns:
            cp.wait()


def _sibling_swap(srcs, *, other_slot, name):
    n = len(srcs)

    def body(*refs):
        src_refs, out_refs, send_sems, recv_sems = refs[:n], refs[n:2 * n], refs[2 * n], refs[2 * n + 1]
        x, y, c, _ = _here()
        copies = [pltpu.make_async_remote_copy(s.at[1 - c] if other_slot else s, o, send_sems.at[a], recv_sems.at[a],
                                               device_id=(x, y, 1 - c), device_id_type=MESH)
                  for a, (s, o) in enumerate(zip(src_refs, out_refs))]
        for cp in copies:
            cp.start()
        for cp in copies:
            cp.wait()

    return pl.pallas_call(
        body, name=name, in_specs=[_ANY] * n, out_specs=[_ANY] * n,
        out_shape=[jax.ShapeDtypeStruct(s.shape[1:] if other_slot else s.shape, s.dtype) for s in srcs],
        scratch_shapes=[pltpu.SemaphoreType.DMA((n,)), pltpu.SemaphoreType.DMA((n,))],
    )(*srcs)


def _allgather_small(v, *, reduce, name):
    r = v.shape[0]

    def body(v_ref, out_ref, *rest):
        send_sems, recv_sems, local_sem = rest[-3:]
        x, y, c, chips = _here()
        me, sibling = (x, y, c), (x, y, 1 - c)

        def slot(px, py, pc):
            return out_ref.at[4 * px + 2 * py + pc]

        def copy(k, block, to, src=None):
            return pltpu.make_async_remote_copy(
                src_ref=slot(*block) if src is None else src, dst_ref=slot(*block),
                send_sem=send_sems.at[k], recv_sem=recv_sems.at[k], device_id=to, device_id_type=MESH)

        mine = pltpu.make_async_copy(v_ref, slot(*me), local_sem)
        mine.start()
        first = [copy(0, me, sibling, src=v_ref)]
        first += [copy(1 + j, me, (*chip, c), src=v_ref) for j, chip in enumerate(chips)]
        for cp in first:
            cp.start()
        passed = [copy(4 + j, (*chip, c), sibling) for j, chip in enumerate(chips)]
        for j, chip in enumerate(chips):
            copy(1 + j, (*chip, c), me).wait_recv()
            passed[j].start()
        copy(0, sibling, me).wait_recv()
        for j, chip in enumerate(chips):
            copy(4 + j, (*chip, 1 - c), me).wait_recv()
        for cp in first + passed:
            cp.wait_send()
        mine.wait()
        if reduce:
            total = out_ref[0]
            for d in range(1, N_DEV):
                total = total + out_ref[d]
            rest[0][...] = total

    vm = pl.BlockSpec(memory_space=pltpu.VMEM)
    out_shape = [jax.ShapeDtypeStruct((N_DEV, r, LANES), F32)] + ([jax.ShapeDtypeStruct((r, LANES), F32)] if reduce else [])
    return pl.pallas_call(
        body, name=name, in_specs=[vm], out_specs=[vm] * len(out_shape), out_shape=out_shape,
        scratch_shapes=[pltpu.SemaphoreType.DMA((7,)), pltpu.SemaphoreType.DMA((7,)), pltpu.SemaphoreType.DMA],
        compiler_params=pltpu.CompilerParams(vmem_limit_bytes=VMEM_LIMIT),
    )(v)


REDUCE_BLOCK_BYTES = 2 << 20


def _reduce_rows(r, c):
    cands = [b for b in range(16, r + 1, 16) if r % b == 0 and b * c * 4 <= REDUCE_BLOCK_BYTES]
    return max(cands)


def _add_halves(mine2, other, c_idx, *, name):
    _, nq, r, c = mine2.shape
    br = _reduce_rows(r, c)

    def body(c_ref, a_ref, b_ref, o_ref):
        o_ref[...] = (a_ref[0] + b_ref[...]).astype(o_ref.dtype)

    blk = pl.BlockSpec((1, br, c), lambda q, i, c_ref: (q, i, 0))
    return pl.pallas_call(
        body, name=name,
        grid_spec=pltpu.PrefetchScalarGridSpec(
            num_scalar_prefetch=1, grid=(nq, r // br),
            in_specs=[pl.BlockSpec((1, 1, br, c), lambda q, i, c_ref: (c_ref[0], q, i, 0)), blk],
            out_specs=blk),
        out_shape=jax.ShapeDtypeStruct((nq, r, c), BF16),
        compiler_params=_params(("parallel", "parallel")),
    )(c_idx, mine2, other)


def _sum_chips(parts, *, name):
    _, r, c = parts.shape
    br = _reduce_rows(r, c)

    def body(p_ref, o_ref):
        total = p_ref[0].astype(F32)
        for q in range(1, 4):
            total = total + p_ref[q].astype(F32)
        o_ref[...] = total

    return pl.pallas_call(
        body, name=name, grid=(r // br,),
        in_specs=[pl.BlockSpec((4, br, c), lambda i: (0, i, 0))],
        out_specs=pl.BlockSpec((br, c), lambda i: (i, 0)),
        out_shape=jax.ShapeDtypeStruct((r, c), F32),
        compiler_params=_params(("parallel",)),
    )(parts)


BIG = ("w_in", "w_out_sb", "w_out_ssd", "w_out_rw", "w_o")
BIG_AXIS = {"w_in": 2, "w_out_sb": 2, "w_out_ssd": 1, "w_out_rw": 2, "w_o": 1}
SMALL_SHARDED = {"conv_w": 320, "rw_w_up": 128, "rw_a_up": 128}
SMALL = ("norm_g", "conv_w", "conv_b", "dt_bias", "a_log", "d_skip", "ssd_norm_g", "rw_mu", "rw_w0", "rw_w_up",
         "rw_a0", "rw_a_up", "rw_k_k", "rw_k_a", "rw_r_k", "rw_ln_g", "rw_ln_b", "final_g")


def _rows_of(a):
    flat = a.reshape(-1)
    pad = (-flat.shape[0]) % LANES
    return jnp.pad(flat, (0, pad)).reshape(-1, LANES)


def _pack_rows(arrays, multiple=8):
    rows = jnp.concatenate([_rows_of(a) for a in arrays], axis=0)
    pad = (-rows.shape[0]) % multiple
    return jnp.pad(rows, ((0, pad), (0, 0)))


def _unpack_rows(rows, shapes):
    out, off = [], 0
    for shp in shapes:
        n = 1
        for d in shp:
            n *= d
        nr = -(-n // LANES)
        out.append(rows[off:off + nr].reshape(-1)[:n].reshape(shp))
        off += nr
    return out


COL_MAP = ((0, 3072, 0), (3072, 4352, C_XBC), (4352, 4368, C_DT), (4368, 6544, C_RW), (6544, 9616, C_GATES))
SHARD_COLS = N_IN // 4


def _w_in_from_shards(shards):
    pieces = []
    for a, b, dst in sorted(COL_MAP, key=lambda m: m[2]):
        if pieces and dst > pieces[-1][0]:
            pieces.append((dst, jnp.zeros((shards[0].shape[0], dst - pieces[-1][0]), shards[0].dtype)))
        for q in range(4):
            lo, hi = max(a, q * SHARD_COLS), min(b, (q + 1) * SHARD_COLS)
            if lo < hi:
                pieces.append((dst + hi - a, shards[q][:, lo - q * SHARD_COLS:hi - q * SHARD_COLS]))
    return jnp.concatenate([p for _, p in pieces], axis=1)


def _w_in_shard(g, q):
    pieces = []
    for a, b, dst in COL_MAP:
        lo, hi = max(a, q * SHARD_COLS), min(b, (q + 1) * SHARD_COLS)
        if lo < hi:
            pieces.append(g[:, dst + lo - a:dst + hi - a])
    return jnp.concatenate(pieces, axis=1)


def _row_halves(a):
    return a.reshape(2, a.shape[0] // 2, a.shape[1])


def _join_halves(core, mine, theirs):
    return jnp.where(core == 0, jnp.concatenate([mine, theirs], axis=-2), jnp.concatenate([theirs, mine], axis=-2))


def kernel(x, norm_g, w_in, conv_w, conv_b, dt_bias, a_log, d_skip, ssd_norm_g, rw_mu, rw_w0, rw_w_up, rw_a0, rw_a_up, rw_k_k, rw_k_a, rw_r_k, rw_ln_g, rw_ln_b, w_out_sb, w_out_ssd, w_out_rw, w_o, final_g, loss_target, m_norm_g, m_w_in, m_conv_w, m_conv_b, m_dt_bias, m_a_log, m_d_skip, m_ssd_norm_g, m_rw_mu, m_rw_w0, m_rw_w_up, m_rw_a0, m_rw_a_up, m_rw_k_k, m_rw_k_a, m_rw_r_k, m_rw_ln_g, m_rw_ln_b, m_w_out_sb, m_w_out_ssd, m_w_out_rw, m_w_o, m_final_g, v_norm_g, v_w_in, v_conv_w, v_conv_b, v_dt_bias, v_a_log, v_d_skip, v_ssd_norm_g, v_rw_mu, v_rw_w0, v_rw_w_up, v_rw_a0, v_rw_a_up, v_rw_k_k, v_rw_k_a, v_rw_r_k, v_rw_ln_g, v_rw_ln_b, v_w_out_sb, v_w_out_ssd, v_w_out_rw, v_w_o, v_final_g):
    names = ("norm_g", "w_in", "conv_w", "conv_b", "dt_bias", "a_log", "d_skip", "ssd_norm_g", "rw_mu", "rw_w0",
             "rw_w_up", "rw_a0", "rw_a_up", "rw_k_k", "rw_k_a", "rw_r_k", "rw_ln_g", "rw_ln_b", "w_out_sb",
             "w_out_ssd", "w_out_rw", "w_o", "final_g")
    w_loc = dict(zip(names, (norm_g, w_in, conv_w, conv_b, dt_bias, a_log, d_skip, ssd_norm_g, rw_mu, rw_w0, rw_w_up,
                             rw_a0, rw_a_up, rw_k_k, rw_k_a, rw_r_k, rw_ln_g, rw_ln_b, w_out_sb, w_out_ssd, w_out_rw,
                             w_o, final_g)))
    m_loc = dict(zip(names, (m_norm_g, m_w_in, m_conv_w, m_conv_b, m_dt_bias, m_a_log, m_d_skip, m_ssd_norm_g,
                             m_rw_mu, m_rw_w0, m_rw_w_up, m_rw_a0, m_rw_a_up, m_rw_k_k, m_rw_k_a, m_rw_r_k,
                             m_rw_ln_g, m_rw_ln_b, m_w_out_sb, m_w_out_ssd, m_w_out_rw, m_w_o, m_final_g)))
    v_loc = dict(zip(names, (v_norm_g, v_w_in, v_conv_w, v_conv_b, v_dt_bias, v_a_log, v_d_skip, v_ssd_norm_g,
                             v_rw_mu, v_rw_w0, v_rw_w_up, v_rw_a0, v_rw_a_up, v_rw_k_k, v_rw_k_a, v_rw_r_k,
                             v_rw_ln_g, v_rw_ln_b, v_w_out_sb, v_w_out_ssd, v_w_out_rw, v_w_o, v_final_g)))
    chip = 2 * lax.axis_index("x") + lax.axis_index("y")
    core = lax.axis_index("c")

    def gather_srcs(i):
        return [_row_halves(w_loc[n][i].astype(BF16)) for n in BIG]

    def gathered(mine, nm):
        theirs = _sibling_swap(mine, other_slot=False, name=nm)
        out = {}
        for n, a, b in zip(BIG, mine, theirs):
            shards = _join_halves(core, a, b)
            out[n] = (_w_in_from_shards([shards[q] for q in range(4)]) if n == "w_in"
                      else jnp.concatenate([shards[q] for q in range(4)], axis=BIG_AXIS[n] - 1))
        return out

    full = {}
    sm_names = tuple(SMALL_SHARDED)
    sm_shapes = [w_loc[n].shape for n in sm_names]
    (got_sm,) = _allgather_small(_pack_rows([w_loc[n] for n in sm_names]), reduce=False, name="gather_small")
    per_chip = [_unpack_rows(got_sm[4 * (q // 2) + 2 * (q % 2)], sm_shapes) for q in range(4)]
    for i, n in enumerate(sm_names):
        full[n] = jnp.concatenate([per_chip[q][i] for q in range(4)], axis=-1)

    def pad16(a):
        return jnp.zeros((1, LANES), F32).at[0, :SSD_HEADS].set(a)

    def layer_params(i, big):
        row = lambda n: w_loc[n][i].reshape(1, -1)
        cw = full["conv_w"][i]
        return dict(
            norm_g=row("norm_g"), w_in=big["w_in"], conv=[cw[k][None] for k in range(4)] + [row("conv_b")],
            dt_bias=pad16(dt_bias[i]), a_log=pad16(a_log[i]), d_skip=pad16(d_skip[i]),
            ssd_norm_g=row("ssd_norm_g"), rw_mu=row("rw_mu"),
            rw_pre=[row("rw_w0"), jnp.zeros((LANES, 512), F32).at[:HEAD].set(full["rw_w_up"][i]), row("rw_a0"),
                    jnp.zeros((LANES, 512), F32).at[HEAD:].set(full["rw_a_up"][i]), row("rw_k_k"), row("rw_k_a")],
            rw_post=[row("rw_ln_g"), row("rw_ln_b"), row("rw_r_k")],
            w_out_sb=big["w_out_sb"], w_out_ssd=big["w_out_ssd"], w_out_rw=big["w_out_rw"], w_o=big["w_o"])

    c_idx = core.reshape(1).astype(jnp.int32)

    def reduce_prepare(g, nm):
        sends = []
        for n in BIG:
            per_chip = ([_w_in_shard(g[n], q) for q in range(4)] if n == "w_in"
                        else jnp.split(g[n], 4, axis=BIG_AXIS[n] - 1))
            sends.append(jnp.stack([_row_halves(p) for p in per_chip], axis=1))
        others = _sibling_swap(sends, other_slot=True, name=nm + "sibling")
        return [_add_halves(s, o, c_idx, name=nm + "add_" + n) for n, s, o in zip(BIG, sends, others)]

    def reduce_finish(exchanged, nm):
        mine = [_sum_chips(p, name=nm + "sum_" + n) for n, p in zip(BIG, exchanged)]
        theirs = _sibling_swap(mine, other_slot=False, name=nm + "join")
        return {n: _join_halves(core, a, b) for n, a, b in zip(BIG, mine, theirs)}

    assert DEPTH == 2
    params, xs, saved, grads, totals = [None] * 2, [x[0], None, None], [None] * 2, [None] * 2, [None] * 2
    params[0] = layer_params(0, gathered(_chip_exchange(gather_srcs(0), per_dest=False, name="gather_l0"), "gather_l0_join"))
    xs[1], saved[0], got = _layer_fwd(xs[0], params[0], "l0_", side=(gather_srcs(1), False))
    params[1] = layer_params(1, gathered(got, "gather_l1_join"))
    xs[2], saved[1], _ = _layer_fwd(xs[1], params[1], "l1_")
    dx, loss_row, g_final = _final(xs[2], final_g.reshape(1, -1), loss_target[0], bt=BT, name="final")
    dx, grads[1], _ = _layer_bwd(xs[1], dx, params[1], saved[1], "l1_")
    dx, grads[0], got = _layer_bwd(xs[0], dx, params[0], saved[0], "l0_", side=(reduce_prepare(grads[1], "reduce_l1_"), True))
    totals[1] = reduce_finish(got, "reduce_l1_")
    totals[0] = reduce_finish(_chip_exchange(reduce_prepare(grads[0], "reduce_l0_"), per_dest=True, name="reduce_l0_chips"),
                              "reduce_l0_")

    def stacked(fn):
        return jnp.stack([fn(grads[i]) for i in range(DEPTH)])

    g_loc = {
        "norm_g": stacked(lambda g: g["norm_g"][0]),
        "conv_w": stacked(lambda g: jnp.concatenate(g["conv"][:4], axis=0)),
        "conv_b": stacked(lambda g: g["conv"][4][0]),
        "dt_bias": stacked(lambda g: g["dt_bias"][0, :SSD_HEADS]),
        "a_log": stacked(lambda g: g["a_log"][0, :SSD_HEADS]),
        "d_skip": stacked(lambda g: g["d_skip"][0, :SSD_HEADS]),
        "ssd_norm_g": stacked(lambda g: g["ssd_norm_g"][0]),
        "rw_mu": stacked(lambda g: g["rw_mu"][0]),
        "rw_w0": stacked(lambda g: g["rw_pre"][0][0]),
        "rw_w_up": stacked(lambda g: g["rw_pre"][1][:HEAD]),
        "rw_a0": stacked(lambda g: g["rw_pre"][2][0]),
        "rw_a_up": stacked(lambda g: g["rw_pre"][3][HEAD:]),
        "rw_k_k": stacked(lambda g: g["rw_pre"][4][0]),
        "rw_k_a": stacked(lambda g: g["rw_pre"][5][0]),
        "rw_r_k": stacked(lambda g: g["rw_r_k"].reshape(8, HEAD)),
        "rw_ln_g": stacked(lambda g: g["rw_ln_g"][0]),
        "rw_ln_b": stacked(lambda g: g["rw_ln_b"][0]),
        "final_g": g_final[0],
    }

    g_out = {n: jnp.stack([totals[0][n], totals[1][n]]) for n in BIG}

    sm_all = SMALL + ("loss",)
    sm_full_shapes = [g_loc[n].shape for n in SMALL] + [(1,)]
    _, summed = _allgather_small(_pack_rows([g_loc[n] for n in SMALL] + [loss_row[0, :1]]), reduce=True, name="reduce_small")
    sm = dict(zip(sm_all, _unpack_rows(summed, sm_full_shapes)))
    for n in SMALL:
        g_out[n] = sm[n]
    for n, wd in SMALL_SHARDED.items():
        g_out[n] = lax.dynamic_slice_in_dim(sm[n], chip * wd, wd, axis=sm[n].ndim - 1)
    loss = sm["loss"][0]

    upd = {n: _adamw(w_loc[n], g_out[n], m_loc[n], v_loc[n], name="adamw_" + n) for n in names}
    return (loss, dx[None], *[g_out[n] for n in names], *[upd[n][0] for n in names],
            *[upd[n][1] for n in names], *[upd[n][2] for n in names])
```

```python
import functools

import jax
import jax.numpy as jnp
from jax import lax
from jax.experimental import pallas as pl
from jax.experimental.pallas import tpu as pltpu

F32 = jnp.float32
BF16 = jnp.bfloat16

D_MODEL = 1024
DEPTH = 2
HEAD = 64
LANES = 128
CHUNK = 128
RMS_EPS = 1e-6
GN_EPS = 64e-5
VMEM_LIMIT = 56 * 1024 * 1024

N_IN = 9616
N_PAD = 9728
C_SB, C_Z, C_GATES, C_RW, C_LO, C_DT, C_XBC = 0, 2048, 3072, 6144, 8192, 8320, 8448
RW_COLS = 2176
XBC_COLS = 1280

ADAM_LR, ADAM_B1, ADAM_B2, ADAM_EPS, ADAM_WD, ADAM_STEP = 0.001, 0.9, 0.999, 1e-08, 0.01, 10


def _params(sem=None):
    return pltpu.CompilerParams(dimension_semantics=sem, vmem_limit_bytes=VMEM_LIMIT)


@jax.custom_vjp
def _sigmoid(x):
    return 1.0 / (1.0 + jnp.exp(-x))


def _sigmoid_fwd(x):
    s = _sigmoid(x)
    return s, s


def _sigmoid_bwd(s, g):
    return (g * s * (1.0 - s),)


_sigmoid.defvjp(_sigmoid_fwd, _sigmoid_bwd)


@jax.custom_vjp
def _silu(x):
    return x * _sigmoid(x)


def _silu_fwd(x):
    s = _sigmoid(x)
    return x * s, (x, s)


def _silu_bwd(res, g):
    x, s = res
    return (g * (s + x * s * (1.0 - s)),)


_silu.defvjp(_silu_fwd, _silu_bwd)


@jax.custom_vjp
def _softplus(x):
    return jnp.maximum(x, 0.0) + jnp.log(1.0 + jnp.exp(-jnp.abs(x)))


def _softplus_fwd(x):
    return _softplus(x), x


def _softplus_bwd(x, g):
    return (g * _sigmoid(x),)


_softplus.defvjp(_softplus_fwd, _softplus_bwd)


def _dot(a, b, dims):
    return lax.dot_general(a.astype(BF16), b.astype(BF16), (dims, ((), ())), preferred_element_type=F32)


def _dot_nn(a, b):
    return _dot(a, b, ((1,), (0,)))


def _dot_nt(a, b):
    return _dot(a, b, ((1,), (1,)))


def _dot_tn(a, b):
    return _dot(a, b, ((0,), (0,)))


@jax.custom_vjp
def _bdot(a, b):
    return _dot_nn(a, b)


def _bdot_fwd(a, b):
    return _dot_nn(a, b), (a, b)


def _bdot_bwd(res, g):
    a, b = res
    return _dot_nt(g, b), _dot_tn(a, g)


_bdot.defvjp(_bdot_fwd, _bdot_bwd)


def _split2(x):
    hi = x.astype(BF16)
    lo = (x - hi.astype(F32)).astype(BF16)
    return hi, lo


_NT = (((1,), (1,)), ((), ()))
_NN = (((1,), (0,)), ((), ()))
_TN = (((0,), (0,)), ((), ()))


def _dot2(x, m, dn=_NN):
    hi, lo = _split2(x)
    return (lax.dot_general(hi, m, dn, preferred_element_type=F32)
            + lax.dot_general(lo, m, dn, preferred_element_type=F32))


def _dot2_tn(x, m):
    return _dot2(x, m, _TN)


def _seg_matrix(n):
    r = lax.broadcasted_iota(jnp.int32, (n, n), 0) // HEAD
    c = lax.broadcasted_iota(jnp.int32, (n, n), 1) // HEAD
    return (r == c).astype(BF16)


@jax.custom_vjp
def _segsum2(x, seg):
    return _dot2(x, seg)


def _segsum2_fwd(x, seg):
    return _dot2(x, seg), seg


def _segsum2_bwd(seg, g):
    return _dot2(g, seg), jnp.zeros_like(seg)


_segsum2.defvjp(_segsum2_fwd, _segsum2_bwd)


def _make_segsum(seg):
    return lambda x: _segsum2(x, seg)


def _shift_down_raw(x, k):
    row = lax.broadcasted_iota(jnp.int32, x.shape, 0)
    return jnp.where(row >= k, pltpu.roll(x, k, 0), 0.0)


def _shift_up_raw(x, k):
    t = x.shape[0]
    row = lax.broadcasted_iota(jnp.int32, x.shape, 0)
    return jnp.where(row < t - k, pltpu.roll(x, t - k, 0), 0.0)


@functools.partial(jax.custom_vjp, nondiff_argnums=(1,))
def _shift_down(x, k):
    return _shift_down_raw(x, k)


def _shift_down_fwd(x, k):
    return _shift_down_raw(x, k), None


def _shift_down_bwd(k, _, g):
    return (_shift_up_raw(g, k),)


_shift_down.defvjp(_shift_down_fwd, _shift_down_bwd)


def _mm(a, b, *, name, ta=False, tb=False, add=None, out_dtype=F32, tm=2048, tn=512, tk=None):
    m, k = (a.shape[1], a.shape[0]) if ta else a.shape
    n = b.shape[0] if tb else b.shape[1]
    tm, tn = min(tm, m), min(tn, n)
    tk = k if tk is None else tk
    nk = k // tk
    assert m % tm == 0 and n % tn == 0 and k % tk == 0
    dims = ((0 if ta else 1,), (1 if tb else 0,))

    def body(a_ref, b_ref, *refs):
        o_ref, acc_ref = refs[-2:]
        p = _dot(a_ref[...], b_ref[...], dims)

        def emit(total):
            if add is not None:
                total = total + refs[0][...]
            o_ref[...] = total.astype(o_ref.dtype)

        if nk == 1:
            emit(p)
        else:
            kk = pl.program_id(2)

            @pl.when(kk == 0)
            def _():
                acc_ref[...] = p

            @pl.when(kk > 0)
            def _():
                acc_ref[...] += p

            @pl.when(kk == nk - 1)
            def _():
                emit(acc_ref[...])

    a_spec = pl.BlockSpec((tk, tm), lambda i, j, kk: (kk, i)) if ta else pl.BlockSpec((tm, tk), lambda i, j, kk: (i, kk))
    b_spec = pl.BlockSpec((tn, tk), lambda i, j, kk: (j, kk)) if tb else pl.BlockSpec((tk, tn), lambda i, j, kk: (kk, j))
    o_spec = pl.BlockSpec((tm, tn), lambda i, j, kk: (i, j))
    return pl.pallas_call(
        body, name=name, grid=(m // tm, n // tn, nk),
        in_specs=[a_spec, b_spec] + ([o_spec] if add is not None else []), out_specs=o_spec,
        out_shape=jax.ShapeDtypeStruct((m, n), out_dtype),
        scratch_shapes=[pltpu.VMEM((tm, tn) if nk > 1 else (8, LANES), F32)],
        compiler_params=_params(("parallel", "parallel", "arbitrary")),
    )(a, b, *([add] if add is not None else []))


def _row_specs(rows, bt):
    return [pl.BlockSpec((bt, w), functools.partial(lambda i, c: (i, c), c=c)) for _, w, c in rows]


def _full_spec(p):
    return pl.BlockSpec(p.shape, functools.partial(lambda i, nd: (0,) * nd, nd=p.ndim))


def _rowwise(f, rows, pars, out_widths, *, bt, name, acc_widths=()):
    t = rows[0][0].shape[0]
    nr, npar, no, na = len(rows), len(pars), len(out_widths), len(acc_widths)

    def body(*refs):
        vals = [r[...] for r in refs[:nr + npar]]
        outs = f(*vals)
        for o_ref, o in zip(refs[nr + npar:nr + npar + no], outs[:no]):
            o_ref[...] = o.astype(o_ref.dtype)
        if na:
            first = pl.program_id(0) == 0
            for a_ref, a in zip(refs[nr + npar + no:], outs[no:]):
                @pl.when(first)
                def _():
                    a_ref[...] = jnp.zeros_like(a_ref)
                a_ref[...] += a

    return pl.pallas_call(
        body, name=name, grid=(t // bt,),
        in_specs=_row_specs(rows, bt) + [_full_spec(p) for p in pars],
        out_specs=[pl.BlockSpec((bt, w), lambda i: (i, 0)) for w in out_widths]
        + [pl.BlockSpec((1, w), lambda i: (0, 0)) for w in acc_widths],
        out_shape=[jax.ShapeDtypeStruct((t, w), F32) for w in out_widths]
        + [jax.ShapeDtypeStruct((1, w), F32) for w in acc_widths],
        compiler_params=_params(("arbitrary",)),
    )(*[r[0] for r in rows], *pars)


def _rowwise_bwd(f, rows, pars, douts, *, bt, name, groups=None):
    t = rows[0][0].shape[0]
    nr, npar, nd = len(rows), len(pars), len(douts)
    groups = [[i] for i in range(nr)] if groups is None else groups
    widths = [r[1] for r in rows]

    def body(*refs):
        vals = [r[...] for r in refs[:nr + npar]]
        cts = tuple(r[...] for r in refs[nr + npar:nr + npar + nd])
        _, vjp = jax.vjp(lambda *a: tuple(f(*a)), *vals)
        grads = vjp(cts)
        out_refs = refs[nr + npar + nd:]
        for g_ref, grp in zip(out_refs[:len(groups)], groups):
            off = 0
            for i in grp:
                g_ref[:, off:off + widths[i]] = grads[i]
                off += widths[i]
        first = pl.program_id(0) == 0
        for p_ref, g in zip(out_refs[len(groups):], grads[nr:]):
            @pl.when(first)
            def _():
                p_ref[...] = jnp.zeros_like(p_ref)
            p_ref[...] += g

    gw = [sum(widths[i] for i in grp) for grp in groups]
    return pl.pallas_call(
        body, name=name, grid=(t // bt,),
        in_specs=_row_specs(rows, bt) + [_full_spec(p) for p in pars] + _row_specs(douts, bt),
        out_specs=[pl.BlockSpec((bt, w), lambda i: (i, 0)) for w in gw] + [_full_spec(p) for p in pars],
        out_shape=[jax.ShapeDtypeStruct((t, w), F32) for w in gw] + [jax.ShapeDtypeStruct(p.shape, F32) for p in pars],
        compiler_params=_params(("arbitrary",)),
    )(*[r[0] for r in rows], *pars, *[d[0] for d in douts])


def _colwise(f, x, c0, ncols, pars, *, bc, name):
    t = x.shape[0]

    def body(x_ref, *refs):
        o_ref = refs[-1]
        o_ref[...] = f(x_ref[...], *[r[...] for r in refs[:-1]])

    return pl.pallas_call(
        body, name=name, grid=(ncols // bc,),
        in_specs=[pl.BlockSpec((t, bc), lambda j: (0, j + c0 // bc))]
        + [pl.BlockSpec((p.shape[0], bc), lambda j: (0, j)) for p in pars],
        out_specs=pl.BlockSpec((t, bc), lambda j: (0, j)),
        out_shape=jax.ShapeDtypeStruct((t, ncols), F32),
        compiler_params=_params(("parallel",)),
    )(x, *pars)


def _colwise_bwd(f, x, c0, ncols, pars, dout, *, bc, name):
    t = x.shape[0]
    npar = len(pars)

    def body(x_ref, *refs):
        vals = [x_ref[...]] + [r[...] for r in refs[:npar]]
        _, vjp = jax.vjp(f, *vals)
        grads = vjp(refs[npar][...])
        for g_ref, g in zip(refs[npar + 1:], grads):
            g_ref[...] = g

    return pl.pallas_call(
        body, name=name, grid=(ncols // bc,),
        in_specs=[pl.BlockSpec((t, bc), lambda j: (0, j + c0 // bc))]
        + [pl.BlockSpec((p.shape[0], bc), lambda j: (0, j)) for p in pars]
        + [pl.BlockSpec((t, bc), lambda j: (0, j))],
        out_specs=[pl.BlockSpec((t, bc), lambda j: (0, j))]
        + [pl.BlockSpec((p.shape[0], bc), lambda j: (0, j)) for p in pars],
        out_shape=[jax.ShapeDtypeStruct((t, ncols), F32)] + [jax.ShapeDtypeStruct(p.shape, F32) for p in pars],
        compiler_params=_params(("parallel",)),
    )(x, *pars, dout)


def _f_rms(x, g):
    return (x * lax.rsqrt(jnp.mean(x * x, axis=-1, keepdims=True) + RMS_EPS) * g,)


def _f_sb_gate(y, gate):
    return (y * _silu(gate),)


def _f_ssd_norm(y, z, g):
    u = y * _silu(z)
    return (u * lax.rsqrt(jnp.mean(u * u, axis=-1, keepdims=True) + RMS_EPS) * g,)


def _f_merge(p_sb, p_ssd, p_rw, g_sb, g_ssd, g_rw):
    return (_sigmoid(g_sb) * p_sb + _sigmoid(g_ssd) * p_ssd + _sigmoid(g_rw) * p_rw,)


def _f_rw_pre(k, lo, w0, w_up, a0, a_up, k_k, k_a):
    segsum = _make_segsum(_seg_matrix(k.shape[1]))
    lane = lax.broadcasted_iota(jnp.int32, lo.shape, 1)
    w_lo = jnp.where(lane < HEAD, jnp.tanh(lo), 0.0)
    a_lo = jnp.where(lane >= HEAD, lo, 0.0)
    w = -_softplus(-(w0 + _bdot(w_lo, w_up))) - 0.5
    log_decay = -jnp.exp(w)
    a = _sigmoid(a0 + _bdot(a_lo, a_up))
    kk = k * k_k
    kk = kk / jnp.maximum(jnp.sqrt(segsum(kk * kk)), 1e-12)
    return log_decay, k * (1.0 + (a - 1.0) * k_a), -kk, kk * a


def _f_rw_post(y, r, k2, v, gate, ln_g, ln_b, r_k):
    segsum = _make_segsum(_seg_matrix(y.shape[1]))
    yc = y - segsum(y) * (1.0 / HEAD)
    var = segsum(yc * yc) * (1.0 / HEAD)
    yn = yc * lax.rsqrt(var + GN_EPS) * ln_g + ln_b
    return ((yn + segsum(r * k2 * r_k) * v) * _silu(gate),)


def _f_rw_mix(slab, mu):
    return slab + (_shift_down(slab, 1) - slab) * mu


def _f_conv(x, w0, w1, w2, w3, b):
    acc = x * w3 + b
    for i, w in enumerate((w0, w1, w2)):
        acc = acc + _shift_down(x, 3 - i) * w
    return _silu(acc)


def _log_sigmoid(z):
    return jnp.minimum(z, 0.0) - jnp.log(1.0 + jnp.exp(-jnp.abs(z)))


def _prefix_matrix(kind):
    j = lax.broadcasted_iota(jnp.int32, (CHUNK, 2 * CHUNK), 0)
    s = lax.broadcasted_iota(jnp.int32, (CHUNK, 2 * CHUNK), 1)
    tri = {"gt": j > s, "le": j <= s, "lt": j < s}[kind]
    return (tri | (s >= CHUNK)).astype(BF16)


def _sb_specs(t):
    q = pl.BlockSpec((CHUNK, LANES), lambda j, i: (i, j))
    k = pl.BlockSpec((t, LANES), lambda j, i: (0, 4 + j))
    v = pl.BlockSpec((t, LANES), lambda j, i: (0, 8 + j))
    return q, k, v


def _sb_fwd(proj, *, name):
    t = proj.shape[0]
    scale = HEAD ** -0.5

    def body(q_ref, k_ref, v_ref, y_ref, lt_ref):
        i = pl.program_id(1)
        lane = lax.broadcasted_iota(jnp.int32, (CHUNK, LANES), 1)
        diff = (lax.broadcasted_iota(jnp.int32, (CHUNK, CHUNK), 1)
                - lax.broadcasted_iota(jnp.int32, (CHUNK, CHUNK), 0))
        m_f = _prefix_matrix("gt")
        q = q_ref[...] * scale
        qh = [jnp.where((lane // HEAD) == h, q, 0.0).astype(BF16) for h in (0, 1)]

        def step(it, carry):
            off = pl.multiple_of((i - it) * CHUNK, CHUNK)
            kblk = k_ref[pl.ds(off, CHUNK), :].astype(BF16)
            vblk = v_ref[pl.ds(off, CHUNK), :].astype(BF16)
            mask = diff < it * CHUNK
            new = []
            for h in (0, 1):
                c, acc = carry[2 * h], carry[2 * h + 1]
                z = lax.dot_general(qh[h], kblk, _NT, preferred_element_type=F32)
                lb = _log_sigmoid(z)
                w2 = _dot2(jnp.where(mask, lb - z, 0.0), m_f)
                att = jnp.where(mask, jnp.exp(lb + c + w2[:, :CHUNK]), 0.0)
                acc = acc + lax.dot_general(att.astype(BF16), vblk, _NN, preferred_element_type=F32)
                new += [c + w2[:, CHUNK:], acc]
            return tuple(new)

        zero = jnp.zeros((CHUNK, LANES), F32)
        c_a, acc_a, c_b, acc_b = lax.fori_loop(0, i + 1, step, (zero, zero, zero, zero))
        y_ref[...] = jnp.where(lane < HEAD, acc_a, acc_b)
        lt_ref[0] = c_a
        lt_ref[1] = c_b

    return pl.pallas_call(
        body, name=name, grid=(4, t // CHUNK),
        in_specs=list(_sb_specs(t)),
        out_specs=[pl.BlockSpec((CHUNK, LANES), lambda j, i: (i, j)),
                   pl.BlockSpec((2, CHUNK, LANES), lambda j, i: (j, i, 0))],
        out_shape=[jax.ShapeDtypeStruct((t, 4 * LANES), F32), jax.ShapeDtypeStruct((8, t, LANES), F32)],
        compiler_params=_params(("parallel", "arbitrary")),
    )(proj, proj, proj)


def _sb_bwd(proj, dy, lt, *, name):
    t = proj.shape[0]
    scale = HEAD ** -0.5

    def body(q_ref, k_ref, v_ref, dy_ref, lt_ref, dq_ref, dk_ref, dv_ref):
        i = pl.program_id(1)

        @pl.when(i == 0)
        def _():
            dk_ref[...] = jnp.zeros_like(dk_ref)
            dv_ref[...] = jnp.zeros_like(dv_ref)

        lane = lax.broadcasted_iota(jnp.int32, (CHUNK, LANES), 1)
        diff = (lax.broadcasted_iota(jnp.int32, (CHUNK, CHUNK), 1)
                - lax.broadcasted_iota(jnp.int32, (CHUNK, CHUNK), 0))
        m_le, m_lt = _prefix_matrix("le"), _prefix_matrix("lt")
        q = q_ref[...] * scale
        dy_blk = dy_ref[...]
        qh = [jnp.where((lane // HEAD) == h, q, 0.0).astype(BF16) for h in (0, 1)]
        doh = [jnp.where((lane // HEAD) == h, dy_blk, 0.0).astype(BF16) for h in (0, 1)]
        lth = [lt_ref[0], lt_ref[1]]

        def step(kb, carry):
            off = pl.multiple_of(kb * CHUNK, CHUNK)
            kblk = k_ref[pl.ds(off, CHUNK), :].astype(BF16)
            vblk = v_ref[pl.ds(off, CHUNK), :].astype(BF16)
            mask = diff < (i - kb) * CHUNK
            new = []
            dk_acc = jnp.zeros((CHUNK, LANES), F32)
            dv_acc = jnp.zeros((CHUNK, LANES), F32)
            for h in (0, 1):
                cp, cg, dq = carry[3 * h:3 * h + 3]
                z = lax.dot_general(qh[h], kblk, _NT, preferred_element_type=F32)
                lb = _log_sigmoid(z)
                w2 = _dot2(jnp.where(mask, lb - z, 0.0), m_le)
                att = jnp.where(mask, jnp.exp(lb + lth[h] - cp - w2[:, :CHUNK]), 0.0)
                d_att = lax.dot_general(doh[h], vblk, _NT, preferred_element_type=F32)
                d_e = d_att * att
                g2 = _dot2(d_e, m_lt)
                sig = jnp.exp(lb)
                dz = jnp.where(mask, d_e * (1.0 - sig) - (cg + g2[:, :CHUNK]) * sig, 0.0).astype(BF16)
                dq = dq + lax.dot_general(dz, kblk, _NN, preferred_element_type=F32)
                dk_acc = dk_acc + lax.dot_general(dz, qh[h], _TN, preferred_element_type=F32)
                dv_acc = dv_acc + lax.dot_general(att.astype(BF16), doh[h], _TN, preferred_element_type=F32)
                new += [cp + w2[:, CHUNK:], cg + g2[:, CHUNK:], dq]
            dk_ref[pl.ds(off, CHUNK), :] += dk_acc
            dv_ref[pl.ds(off, CHUNK), :] += dv_acc
            return tuple(new)

        zero = jnp.zeros((CHUNK, LANES), F32)
        out = lax.fori_loop(0, i + 1, step, (zero,) * 6)
        dq_ref[...] = jnp.where(lane < HEAD, out[2], out[5]) * scale

    q_spec, k_spec, v_spec = _sb_specs(t)
    blk = pl.BlockSpec((CHUNK, LANES), lambda j, i: (i, j))
    col = pl.BlockSpec((t, LANES), lambda j, i: (0, j))
    return pl.pallas_call(
        body, name=name, grid=(4, t // CHUNK),
        in_specs=[q_spec, k_spec, v_spec, blk, pl.BlockSpec((2, CHUNK, LANES), lambda j, i: (j, i, 0))],
        out_specs=[blk, col, col],
        out_shape=[jax.ShapeDtypeStruct((t, 4 * LANES), F32)] * 3,
        compiler_params=_params(("parallel", "arbitrary")),
    )(proj, proj, proj, dy, lt)


SB_BQ = 256
SB_BK = 256
assert SB_BQ == SB_BK


def _tri_ones(kind):
    j = lax.broadcasted_iota(jnp.int32, (SB_BK, SB_BK + LANES), 0)
    s = lax.broadcasted_iota(jnp.int32, (SB_BK, SB_BK + LANES), 1)
    tri = {"gt": j > s, "le": j <= s, "lt": j < s}[kind]
    return (tri | (s >= SB_BK)).astype(BF16)


def _sb_common(q_ref):
    lane = lax.broadcasted_iota(jnp.int32, (SB_BQ, LANES), 1)
    q = q_ref[...] * (HEAD ** -0.5)
    q2 = jnp.concatenate([jnp.where(lane < HEAD, q, 0.0), jnp.where(lane >= HEAD, q, 0.0)], axis=0).astype(BF16)
    diff = (lax.broadcasted_iota(jnp.int32, (2 * SB_BQ, SB_BK), 1)
            - (lax.broadcasted_iota(jnp.int32, (2 * SB_BQ, SB_BK), 0) & (SB_BQ - 1)))
    return lane, q2, diff


def _rep(x):
    return jnp.concatenate([x] * (SB_BK // LANES), axis=1)


def _sb2_specs(t):
    q = pl.BlockSpec((SB_BQ, LANES), lambda j, i: (i, j))
    k = pl.BlockSpec((t, LANES), lambda j, i: (0, 4 + j))
    v = pl.BlockSpec((t, LANES), lambda j, i: (0, 8 + j))
    return q, k, v


def _sb2_fwd(proj, *, name):
    t = proj.shape[0]

    def body(q_ref, k_ref, v_ref, y_ref, lt_ref):
        i = pl.program_id(1)
        lane, q2, diff = _sb_common(q_ref)
        m_f = _tri_ones("gt")

        def step(kb, carry, diagonal):
            c, acc = carry
            off = pl.multiple_of(kb * SB_BK, SB_BK)
            kblk = k_ref[pl.ds(off, SB_BK), :].astype(BF16)
            vblk = v_ref[pl.ds(off, SB_BK), :].astype(BF16)
            z = lax.dot_general(q2, kblk, _NT, preferred_element_type=F32)
            lb = _log_sigmoid(z)
            lk = jnp.where(diff < 0, lb - z, 0.0) if diagonal else lb - z
            w2 = _dot2(lk, m_f)
            att = jnp.exp(lb + _rep(c) + w2[:, :SB_BK])
            if diagonal:
                att = jnp.where(diff < 0, att, 0.0)
            acc = acc + lax.dot_general(att.astype(BF16), vblk, _NN, preferred_element_type=F32)
            return c + w2[:, SB_BK:], acc

        zero = jnp.zeros((2 * SB_BQ, LANES), F32)
        c, acc = lax.fori_loop(0, i, lambda it, carry: step(i - 1 - it, carry, False), step(i, (zero, zero), True))
        y_ref[...] = jnp.where(lane < HEAD, acc[:SB_BQ], acc[SB_BQ:])
        lt_ref[0] = c[:SB_BQ]
        lt_ref[1] = c[SB_BQ:]

    return pl.pallas_call(
        body, name=name, grid=(4, t // SB_BQ),
        in_specs=list(_sb2_specs(t)),
        out_specs=[pl.BlockSpec((SB_BQ, LANES), lambda j, i: (i, j)),
                   pl.BlockSpec((2, SB_BQ, LANES), lambda j, i: (j, i, 0))],
        out_shape=[jax.ShapeDtypeStruct((t, 4 * LANES), F32), jax.ShapeDtypeStruct((8, t, LANES), F32)],
        compiler_params=_params(("parallel", "arbitrary")),
    )(proj, proj, proj)


def _sb2_bwd(proj, dy, lt, *, name):
    t = proj.shape[0]

    def body(q_ref, k_ref, v_ref, dy_ref, lt_ref, dq_ref, dk_ref, dv_ref):
        i = pl.program_id(1)

        @pl.when(i == 0)
        def _():
            dk_ref[...] = jnp.zeros_like(dk_ref)
            dv_ref[...] = jnp.zeros_like(dv_ref)

        lane, q2, diff = _sb_common(q_ref)
        m_le, m_lt = _tri_ones("le"), _tri_ones("lt")
        dy_blk = dy_ref[...]
        do2 = jnp.concatenate([jnp.where(lane < HEAD, dy_blk, 0.0), jnp.where(lane >= HEAD, dy_blk, 0.0)],
                              axis=0).astype(BF16)
        lt2 = jnp.concatenate([lt_ref[0], lt_ref[1]], axis=0)

        def step(kb, carry, diagonal):
            cp, cg, dq = carry
            off = pl.multiple_of(kb * SB_BK, SB_BK)
            kblk = k_ref[pl.ds(off, SB_BK), :].astype(BF16)
            vblk = v_ref[pl.ds(off, SB_BK), :].astype(BF16)
            z = lax.dot_general(q2, kblk, _NT, preferred_element_type=F32)
            lb = _log_sigmoid(z)
            lk = jnp.where(diff < 0, lb - z, 0.0) if diagonal else lb - z
            w2 = _dot2(lk, m_le)
            att = jnp.exp(lb + _rep(lt2 - cp) - w2[:, :SB_BK])
            if diagonal:
                att = jnp.where(diff < 0, att, 0.0)
            d_e = lax.dot_general(do2, vblk, _NT, preferred_element_type=F32) * att
            g2 = _dot2(d_e, m_lt)
            sig = jnp.exp(lb)
            dz = d_e * (1.0 - sig) - (_rep(cg) + g2[:, :SB_BK]) * sig
            dz = (jnp.where(diff < 0, dz, 0.0) if diagonal else dz).astype(BF16)
            dq = dq + lax.dot_general(dz, kblk, _NN, preferred_element_type=F32)
            dk_ref[pl.ds(off, SB_BK), :] += lax.dot_general(dz, q2, _TN, preferred_element_type=F32)
            dv_ref[pl.ds(off, SB_BK), :] += lax.dot_general(att.astype(BF16), do2, _TN, preferred_element_type=F32)
            return cp + w2[:, SB_BK:], cg + g2[:, SB_BK:], dq

        zero = jnp.zeros((2 * SB_BQ, LANES), F32)
        before = lax.fori_loop(0, i, lambda kb, carry: step(kb, carry, False), (zero, zero, zero))
        _, _, dq = step(i, before, True)
        dq_ref[...] = jnp.where(lane < HEAD, dq[:SB_BQ], dq[SB_BQ:]) * (HEAD ** -0.5)

    q_spec, k_spec, v_spec = _sb2_specs(t)
    blk = pl.BlockSpec((SB_BQ, LANES), lambda j, i: (i, j))
    col = pl.BlockSpec((t, LANES), lambda j, i: (0, j))
    return pl.pallas_call(
        body, name=name, grid=(4, t // SB_BQ),
        in_specs=[q_spec, k_spec, v_spec, blk, pl.BlockSpec((2, SB_BQ, LANES), lambda j, i: (j, i, 0))],
        out_specs=[blk, col, col],
        out_shape=[jax.ShapeDtypeStruct((t, 4 * LANES), F32)] * 3,
        compiler_params=_params(("parallel", "arbitrary")),
    )(proj, proj, proj, dy, lt)


SSD_HEADS = 16
SSD_PAIRS = 8


def _split3(x):
    a = x.astype(BF16)
    r = x - a.astype(F32)
    b = r.astype(BF16)
    return a, b, (r - b.astype(F32)).astype(BF16)


def _dot3(x, m, dn=_NN):
    return sum(lax.dot_general(p, m, dn, preferred_element_type=F32) for p in _split3(x))


def _mdot3(m, x):
    return sum(lax.dot_general(m, p, _NN, preferred_element_type=F32) for p in _split3(x))


def _ssd_common(dtr, dtb, alog, acsx_s, acst_s):
    lane = lax.broadcasted_iota(jnp.int32, (CHUNK, LANES), 1)
    lane1 = lax.broadcasted_iota(jnp.int32, (1, LANES), 1)
    arow = jnp.where(lane1 < SSD_HEADS, -jnp.exp(alog), 0.0)
    dt = jnp.where(lane < SSD_HEADS, _softplus(dtr + dtb), 0.0)
    da = dt * arow
    r = lax.broadcasted_iota(jnp.int32, (CHUNK, CHUNK), 0)
    c = lax.broadcasted_iota(jnp.int32, (CHUNK, CHUNK), 1)
    tril = (r >= c).astype(BF16)
    triu = (r <= c).astype(BF16)
    acs = _mdot3(tril, da)
    acst_s[...] = _dot3(da, triu, _TN)
    eh = lax.broadcasted_iota(jnp.int32, (LANES, 8 * LANES), 0)
    e = (eh == lax.broadcasted_iota(jnp.int32, (LANES, 8 * LANES), 1) // HEAD).astype(BF16)
    eh2 = lax.broadcasted_iota(jnp.int32, (LANES, 16 * LANES), 0)
    e2 = (eh2 == lax.broadcasted_iota(jnp.int32, (LANES, 16 * LANES), 1) // LANES).astype(BF16)
    acsx_s[...] = _dot3(acs, e)
    return dt, arow, _dot3(dt, e), _dot3(acs, e2), e, tril, triu


def _ssd_fwd(xc, proj, dtb, alog, dsk, *, name):
    t = xc.shape[0]
    nc = t // CHUNK

    def body(x_ref, b_ref, c_ref, dtr_ref, dtb_ref, alog_ref, dsk_ref, y_ref, hin_ref, acsx_s, acst_s, h_s):
        @pl.when(pl.program_id(0) == 0)
        def _():
            h_s[...] = jnp.zeros_like(h_s)

        dt, arow, dt_x, acs_b, e, tril, _ = _ssd_common(dtr_ref[...], dtb_ref[...], alog_ref[...], acsx_s, acst_s)
        dsk_x = _dot3(jnp.broadcast_to(dsk_ref[...], (CHUNK, LANES)), e)
        lane = lax.broadcasted_iota(jnp.int32, (CHUNK, LANES), 1)
        causal = (lax.broadcasted_iota(jnp.int32, (CHUNK, CHUNK), 0)
                  >= lax.broadcasted_iota(jnp.int32, (CHUNK, CHUNK), 1))
        for j in range(SSD_PAIRS):
            g = j // 4
            sl = slice(j * LANES, (j + 1) * LANES)
            if j % 4 == 0:
                bg = jnp.where(lane // HEAD == g, b_ref[...], 0.0)
                cg = jnp.where(lane // HEAD == g, c_ref[...], 0.0)
                cb = _dot_nt(cg, bg)
            x = x_ref[:, sl]
            a = acsx_s[:, sl]
            at = acsx_s[CHUNK - 1:CHUNK, sl]
            xdt = x * dt_x[:, sl]
            hin = h_s[j]
            hin_ref[0, j] = hin
            y = jnp.exp(a) * _dot_nn(cg, hin) + x * dsk_x[:, sl]
            h_s[j] = jnp.exp(at) * hin + _dot_tn(bg, xdt * jnp.exp(at - a))
            yd = []
            for hh in (0, 1):
                h = 2 * j + hh
                dec = jnp.exp(jnp.minimum(acs_b[:, h * LANES:(h + 1) * LANES] - acst_s[pl.ds(h, 1), :], 0.0))
                yd.append(_dot_nn(jnp.where(causal, cb * dec, 0.0), xdt))
            y_ref[:, sl] = y + jnp.where(lane < HEAD, yd[0], yd[1])

    one = pl.BlockSpec((1, LANES), lambda i: (0, 0))
    return pl.pallas_call(
        body, name=name, grid=(nc,),
        in_specs=[pl.BlockSpec((CHUNK, 8 * LANES), lambda i: (i, 0)),
                  pl.BlockSpec((CHUNK, LANES), lambda i: (i, 8)),
                  pl.BlockSpec((CHUNK, LANES), lambda i: (i, 9)),
                  pl.BlockSpec((CHUNK, LANES), lambda i: (i, C_DT // LANES)), one, one, one],
        out_specs=[pl.BlockSpec((CHUNK, 8 * LANES), lambda i: (i, 0)),
                   pl.BlockSpec((1, SSD_PAIRS, LANES, LANES), lambda i: (i, 0, 0, 0))],
        out_shape=[jax.ShapeDtypeStruct((t, 8 * LANES), F32),
                   jax.ShapeDtypeStruct((nc, SSD_PAIRS, LANES, LANES), F32)],
        scratch_shapes=[pltpu.VMEM((CHUNK, 8 * LANES), F32), pltpu.VMEM((LANES, CHUNK), F32),
                        pltpu.VMEM((SSD_PAIRS, LANES, LANES), F32)],
        compiler_params=_params(("arbitrary",)),
    )(xc, xc, xc, proj, dtb, alog, dsk)


def _ssd_bwd(xc, proj, dtb, alog, dsk, hin_all, dy, *, name):
    t = xc.shape[0]
    nc = t // CHUNK

    def body(x_ref, b_ref, c_ref, dtr_ref, dtb_ref, alog_ref, dsk_ref, hin_ref, dy_ref,
             dxc_ref, ddtr_ref, ddtb_ref, dalog_ref, ddsk_ref, acsx_s, acst_s, dh_s, dax_s, ddx_s):
        @pl.when(pl.program_id(0) == 0)
        def _():
            dh_s[...] = jnp.zeros_like(dh_s)
            ddtb_ref[...] = jnp.zeros_like(ddtb_ref)
            dalog_ref[...] = jnp.zeros_like(dalog_ref)
            ddsk_ref[...] = jnp.zeros_like(ddsk_ref)

        dtr = dtr_ref[...]
        dtb = dtb_ref[...]
        dt, arow, dt_x, acs_b, e, tril, triu = _ssd_common(dtr, dtb, alog_ref[...], acsx_s, acst_s)
        dsk_x = _dot3(jnp.broadcast_to(dsk_ref[...], (CHUNK, LANES)), e)
        lane = lax.broadcasted_iota(jnp.int32, (CHUNK, LANES), 1)
        rowi = lax.broadcasted_iota(jnp.int32, (CHUNK, LANES), 0)
        causal = (lax.broadcasted_iota(jnp.int32, (CHUNK, CHUNK), 0)
                  >= lax.broadcasted_iota(jnp.int32, (CHUNK, CHUNK), 1))
        dacs = jnp.zeros((CHUNK, LANES), F32)
        d_b = jnp.zeros((CHUNK, LANES), F32)
        d_c = jnp.zeros((CHUNK, LANES), F32)
        for j in range(SSD_PAIRS):
            g = j // 4
            sl = slice(j * LANES, (j + 1) * LANES)
            if j % 4 == 0:
                bg = jnp.where(lane // HEAD == g, b_ref[...], 0.0)
                cg = jnp.where(lane // HEAD == g, c_ref[...], 0.0)
                cb = _dot_nt(cg, bg)
                dcb = jnp.zeros((CHUNK, CHUNK), F32)
            x = x_ref[:, sl]
            d = dt_x[:, sl]
            a = acsx_s[:, sl]
            at = acsx_s[CHUNK - 1:CHUNK, sl]
            xdt = x * d
            hin = hin_ref[0, j]
            dhout = dh_s[j]
            dyp = dy_ref[:, sl]
            ea, eat, ed = jnp.exp(a), jnp.exp(at), jnp.exp(at - a)
            da_l = dyp * ea * _dot_nn(cg, hin)
            dm = dyp * ea
            d_c = d_c + _dot_nt(dm, hin)
            dh_s[j] = _dot_tn(cg, dm) + eat * dhout
            dat = jnp.sum(dhout * hin * eat, axis=0, keepdims=True)
            d_b = d_b + _dot_nt(xdt * ed, dhout)
            dw = _dot_nn(bg, dhout)
            dxdt = dw * ed
            ded = dw * xdt * ed
            dat = dat + jnp.sum(ded, axis=0, keepdims=True)
            da_l = da_l - ded
            for hh in (0, 1):
                h = 2 * j + hh
                dec = jnp.exp(jnp.minimum(acs_b[:, h * LANES:(h + 1) * LANES] - acst_s[pl.ds(h, 1), :], 0.0))
                gm = jnp.where(causal, cb * dec, 0.0)
                dyh = jnp.where(lane // HEAD == hh, dyp, 0.0)
                dg = _dot_nt(dyh, xdt)
                dxdt = dxdt + _dot_tn(gm, dyh)
                dcb = dcb + jnp.where(causal, dg * dec, 0.0)
                th = dg * gm
                oh = (lane == h).astype(BF16)
                dacs = dacs + _dot2(th, oh) - _dot2_tn(th, oh)
            if j % 4 == 3:
                d_c = d_c + _dot_nn(dcb, bg)
                d_b = d_b + _dot_tn(dcb, cg)
            dxc_ref[:, sl] = dyp * dsk_x[:, sl] + dxdt * d
            ddx_s[:, sl] = dxdt * x
            dax_s[:, sl] = da_l + jnp.where(rowi == CHUNK - 1, dat, 0.0)
            dskp = jnp.sum(dyp * x, axis=0, keepdims=True)
            ddsk_ref[...] += _dot2(jnp.broadcast_to(dskp, (8, LANES)), e[:, sl], _NT)
        dxc_ref[:, 8 * LANES:9 * LANES] = d_b
        dxc_ref[:, 9 * LANES:10 * LANES] = d_c
        dacs = dacs + _dot2(dax_s[...], e, _NT)
        ddt = _dot2(ddx_s[...], e, _NT)
        dda = _mdot3(triu, dacs)
        ddt = ddt + dda * arow
        dalog_ref[...] += jnp.sum(dda * dt, axis=0, keepdims=True) * arow
        ddtr = jnp.where(lane < SSD_HEADS, ddt * _sigmoid(dtr + dtb), 0.0)
        ddtr_ref[...] = ddtr
        ddtb_ref[...] += jnp.sum(ddtr, axis=0, keepdims=True)

    one = pl.BlockSpec((1, LANES), lambda i: (0, 0))
    rev = lambda c: (lambda i: (nc - 1 - i, c))
    return pl.pallas_call(
        body, name=name, grid=(nc,),
        in_specs=[pl.BlockSpec((CHUNK, 8 * LANES), rev(0)), pl.BlockSpec((CHUNK, LANES), rev(8)),
                  pl.BlockSpec((CHUNK, LANES), rev(9)), pl.BlockSpec((CHUNK, LANES), rev(C_DT // LANES)),
                  one, one, one,
                  pl.BlockSpec((1, SSD_PAIRS, LANES, LANES), lambda i: (nc - 1 - i, 0, 0, 0)),
                  pl.BlockSpec((CHUNK, 8 * LANES), rev(0))],
        out_specs=[pl.BlockSpec((CHUNK, XBC_COLS), rev(0)), pl.BlockSpec((CHUNK, LANES), rev(0)), one, one,
                   pl.BlockSpec((8, LANES), lambda i: (0, 0))],
        out_shape=[jax.ShapeDtypeStruct((t, XBC_COLS), F32), jax.ShapeDtypeStruct((t, LANES), F32)]
        + [jax.ShapeDtypeStruct((1, LANES), F32)] * 2 + [jax.ShapeDtypeStruct((8, LANES), F32)],
        scratch_shapes=[pltpu.VMEM((CHUNK, 8 * LANES), F32), pltpu.VMEM((LANES, CHUNK), F32),
                        pltpu.VMEM((SSD_PAIRS, LANES, LANES), F32),
                        pltpu.VMEM((CHUNK, 8 * LANES), F32), pltpu.VMEM((CHUNK, 8 * LANES), F32)],
        compiler_params=_params(("arbitrary",)),
    )(xc, xc, xc, proj, dtb, alog, dsk, hin_all, dy)


RW_LW = 128
RW_PAIRS = 4 * LANES // RW_LW
RW_BT = 16
RW_DECAY_ROW = 1
RW_BWD_PAIRS = 4


def _rw_consts():
    seg = _seg_matrix(RW_LW)
    ti = (lax.broadcasted_iota(jnp.int32, (HEAD, RW_LW), 0)
          == lax.broadcasted_iota(jnp.int32, (HEAD, RW_LW), 1) % HEAD)
    return seg, ti


def _col_tiles(rows, ti, seg):
    tib = ti.astype(BF16)
    n = len(rows)
    hi = [r.astype(BF16) for r in rows]
    w_lo = (rows[RW_DECAY_ROW] - hi[RW_DECAY_ROW].astype(F32)).astype(BF16)
    out = lax.dot_general(jnp.concatenate([tib * h for h in hi + [w_lo]], axis=0), seg, _NN, preferred_element_type=F32)
    tiles = [out[i * HEAD:(i + 1) * HEAD] for i in range(n)]
    tiles[RW_DECAY_ROW] = tiles[RW_DECAY_ROW] + out[n * HEAD:(n + 1) * HEAD]
    return tiles


def _col_tiles2(rows, ti, seg):
    tib = ti.astype(BF16)
    hi = [r.astype(BF16) for r in rows]
    lo = [(r - h.astype(F32)).astype(BF16) for r, h in zip(rows, hi)]
    out = (lax.dot_general(jnp.concatenate([tib * h for h in hi], axis=0), seg, _NN, preferred_element_type=F32)
           + lax.dot_general(jnp.concatenate([tib * l for l in lo], axis=0), seg, _NN, preferred_element_type=F32))
    return [out[i * HEAD:(i + 1) * HEAD] for i in range(len(rows))]


def _head_lane_sums(tiles, ti, seg):
    out = _dot2(jnp.concatenate(tiles, axis=0), seg)
    return [jnp.sum(jnp.where(ti, out[i * HEAD:(i + 1) * HEAD], 0.0), axis=0, keepdims=True) for i in range(len(tiles))]


def _rw_scan_fwd(mixed, w, k, n, b, *, name):
    t = w.shape[0]

    def body(r_ref, v_ref, w_ref, k_ref, n_ref, b_ref, y_ref, st_ref, s_s):
        @pl.when(pl.program_id(0) == 0)
        def _():
            s_s[...] = jnp.zeros_like(s_s)

        seg, ti = _rw_consts()

        def step(tt, state):
            row = pl.ds(tt, 1)
            new = []
            for p in range(RW_PAIRS):
                sl = pl.ds(p * RW_LW, RW_LW)
                s = state[p]
                ncol, wcol, bcol, kcol, rcol = _col_tiles(
                    [x[row, sl] for x in (n_ref, w_ref, b_ref, k_ref, r_ref)], ti, seg)
                sa = jnp.sum(s * ncol, axis=0, keepdims=True)
                s = s * wcol + bcol * sa + kcol * v_ref[row, sl]
                y_ref[row, sl] = jnp.sum(s * rcol, axis=0, keepdims=True)
                st_ref[tt, p] = s
                new.append(s)
            return tuple(new)

        out = tuple(s_s[p] for p in range(RW_PAIRS))
        for tt in range(RW_BT):
            out = step(tt, out)
        for p in range(RW_PAIRS):
            s_s[p] = out[p]

    blk = lambda c: pl.BlockSpec((RW_BT, 4 * LANES), functools.partial(lambda i, c: (i, c), c=c))
    return pl.pallas_call(
        body, name=name, grid=(t // RW_BT,),
        in_specs=[blk(0), blk(2), blk(0), blk(0), blk(0), blk(0)],
        out_specs=[blk(0), pl.BlockSpec((RW_BT, RW_PAIRS, HEAD, RW_LW), lambda i: (i, 0, 0, 0))],
        out_shape=[jax.ShapeDtypeStruct((t, 4 * LANES), F32),
                   jax.ShapeDtypeStruct((t, RW_PAIRS, HEAD, RW_LW), F32)],
        scratch_shapes=[pltpu.VMEM((RW_PAIRS, HEAD, RW_LW), F32)],
        compiler_params=_params(("arbitrary",)),
    )(mixed, mixed, w, k, n, b)


def _rw_scan_bwd(mixed, w, k, n, b, states, dy, dr0, dk0, dv0, *, name):
    t = w.shape[0]
    nb = t // RW_BT
    ppc = RW_BWD_PAIRS
    ng = RW_PAIRS // ppc

    def body(r_ref, v_ref, w_ref, k_ref, n_ref, b_ref, st_ref, prev_ref, dy_ref, dr0_ref, dk0_ref, dv0_ref,
             dr_ref, dw_ref, dk_ref, dv_ref, dn_ref, db_ref, ds_s):
        @pl.when(pl.program_id(1) == 0)
        def _():
            ds_s[...] = jnp.zeros_like(ds_s)

        seg, ti = _rw_consts()
        has_prev = (pl.program_id(1) < nb - 1).astype(F32)

        def step(it, carry):
            tt = RW_BT - 1 - it
            row = pl.ds(tt, 1)
            prev_t = max(tt - 1, 0)
            new_ds, new_s = [], []
            for p in range(ppc):
                sl = pl.ds(p * RW_LW, RW_LW)
                ds, s_t = carry[p], carry[ppc + p]
                s_p = st_ref[prev_t, p] if tt > 0 else prev_ref[0, p] * has_prev
                ncol, wcol, bcol, kcol, rcol = _col_tiles2(
                    [x[row, sl] for x in (n_ref, w_ref, b_ref, k_ref, r_ref)], ti, seg)
                vv, dyy = v_ref[row, sl], dy_ref[row, sl]
                sa = jnp.sum(s_p * ncol, axis=0, keepdims=True)
                ds = ds + rcol * dyy
                dsa = jnp.sum(ds * bcol, axis=0, keepdims=True)
                dv_ref[row, sl] = jnp.sum(ds * kcol, axis=0, keepdims=True) + dv0_ref[row, sl]
                dr, dw, db, dk, dn = _head_lane_sums([s_t * dyy, ds * s_p, ds * sa, ds * vv, s_p * dsa], ti, seg)
                dr_ref[row, sl] = dr + dr0_ref[row, sl]
                dw_ref[row, sl] = dw
                db_ref[row, sl] = db
                dk_ref[row, sl] = dk + dk0_ref[row, sl]
                dn_ref[row, sl] = dn
                new_ds.append(ds * wcol + ncol * dsa)
                new_s.append(s_p)
            return tuple(new_ds) + tuple(new_s)

        init = tuple(ds_s[p] for p in range(ppc)) + tuple(st_ref[RW_BT - 1, p] for p in range(ppc))
        out = init
        for it in range(RW_BT):
            out = step(it, out)
        for p in range(ppc):
            ds_s[p] = out[p]

    blk = lambda c: pl.BlockSpec((RW_BT, ppc * RW_LW), functools.partial(lambda g, i, c: (nb - 1 - i, c * ng + g), c=c))
    st_spec = pl.BlockSpec((RW_BT, ppc, HEAD, RW_LW), lambda g, i: (nb - 1 - i, g, 0, 0))
    prev_spec = pl.BlockSpec((1, ppc, HEAD, RW_LW), lambda g, i: (jnp.maximum((nb - 1 - i) * RW_BT - 1, 0), g, 0, 0))
    return pl.pallas_call(
        body, name=name, grid=(ng, nb),
        in_specs=[blk(0), blk(2), blk(0), blk(0), blk(0), blk(0), st_spec, prev_spec, blk(0), blk(0), blk(0), blk(0)],
        out_specs=[blk(0)] * 6,
        out_shape=[jax.ShapeDtypeStruct((t, 4 * LANES), F32)] * 6,
        scratch_shapes=[pltpu.VMEM((ppc, HEAD, RW_LW), F32)],
        compiler_params=_params(("parallel", "arbitrary")),
    )(mixed, mixed, w, k, n, b, states, states, dy, dr0, dk0, dv0)


RW_C = 64


def _p3(a, b, dn):
    ah, al = _split2(a)
    bh, bl = _split2(b)
    d = lambda x, y: lax.dot_general(x, y, dn, preferred_element_type=F32)
    return d(ah, bh) + d(ah, bl) + d(al, bh)


_BNN = (((2,), (1,)), ((0,), (0,)))
_BNT = (((2,), (2,)), ((0,), (0,)))
_BTN = (((1,), (1,)), ((0,), (0,)))


@jax.custom_vjp
def _pnn(a, b):
    return _p3(a, b, _BNN)


@jax.custom_vjp
def _pnt(a, b):
    return _p3(a, b, _BNT)


@jax.custom_vjp
def _ptn(a, b):
    return _p3(a, b, _BTN)


_pnn.defvjp(lambda a, b: (_p3(a, b, _BNN), (a, b)), lambda res, g: (_p3(g, res[1], _BNT), _p3(res[0], g, _BTN)))
_pnt.defvjp(lambda a, b: (_p3(a, b, _BNT), (a, b)), lambda res, g: (_p3(g, res[1], _BNN), _p3(g, res[0], _BTN)))
_ptn.defvjp(lambda a, b: (_p3(a, b, _BTN), (a, b)), lambda res, g: (_p3(res[1], g, _BNT), _p3(res[0], g, _BNN)))


def _rw_chunk_consts():
    c2 = 2 * RW_C
    row = lax.broadcasted_iota(jnp.int32, (c2, c2), 0)
    col = lax.broadcasted_iota(jnp.int32, (c2, c2), 1)
    same = (row // RW_C) == (col // RW_C)
    strict = (same & (row > col)).astype(F32)
    incl = (same & (row >= col)).astype(F32)
    eye = (row == col).astype(F32)
    tr = lax.broadcasted_iota(jnp.int32, (RW_C, RW_C), 0)
    tc = lax.broadcasted_iota(jnp.int32, (RW_C, RW_C), 1)
    tril = (tr >= tc).astype(F32)
    lane = lax.broadcasted_iota(jnp.int32, (1, LANES), 1)
    hm = [(lane // HEAD == h).astype(F32) for h in (0, 1)]
    return strict, incl, eye, tril, hm


def _rw_chunk(r, lw, k, v, n, b, s2, consts):
    strict, incl, eye, tril, hm = consts
    two = lambda x: jnp.concatenate([x * hm[0], x * hm[1]], axis=1)
    cum = _pnn(jnp.broadcast_to(tril, (4, RW_C, RW_C)), lw)
    grow, shrink = jnp.exp(-cum), jnp.exp(cum)
    n2, r2 = two(n * jnp.exp(cum - lw)), two(r * shrink)
    b2, k2, v2 = two(b * grow), two(k * grow), two(v)
    p = _pnt(n2, b2) * strict
    x2 = _pnt(n2, s2) + _pnn(_pnt(n2, k2) * strict, v2)
    t_inv, a = eye + p, p
    for _ in range(RW_C.bit_length() - 2):
        a = _pnn(a, a)
        t_inv = t_inv + _pnn(t_inv, a)
    u2 = _pnn(t_inv, x2)
    y2 = _pnt(r2, s2) + _pnn(_pnt(r2, b2) * incl, u2) + _pnn(_pnt(r2, k2) * incl, v2)
    s2_new = (s2 + _ptn(u2, b2) + _ptn(v2, k2)) * jnp.exp(jnp.sum(lw, axis=1, keepdims=True))
    return jnp.sum(y2.reshape(4, 2, RW_C, LANES), axis=1), s2_new


def _pairs(ref):
    return jnp.stack([ref[:, p * LANES:(p + 1) * LANES] for p in range(4)])


def _rw_chunk_fwd(mixed, lw, k, n, b, *, name, side=None):
    t = lw.shape[0]
    nc = t // RW_C

    def body(r_ref, v_ref, lw_ref, k_ref, n_ref, b_ref, y_ref, sin_ref, s_s):
        @pl.when(pl.program_id(0) == 0)
        def _():
            s_s[...] = jnp.zeros_like(s_s)

        s2 = s_s[...]
        sin_ref[0] = s2
        y, s2 = _rw_chunk(*[_pairs(x) for x in (r_ref, lw_ref, k_ref, v_ref, n_ref, b_ref)], s2, _rw_chunk_consts())
        for p in range(4):
            y_ref[:, p * LANES:(p + 1) * LANES] = y[p]
        s_s[...] = s2

    blk = lambda c: pl.BlockSpec((RW_C, 4 * LANES), functools.partial(lambda i, c: (i, c), c=c))
    return _call_with_side(
        body, side, name=name, steps=nc,
        in_specs=[blk(0), blk(2), blk(0), blk(0), blk(0), blk(0)],
        out_specs=[blk(0), pl.BlockSpec((1, 4, LANES, LANES), lambda i: (i, 0, 0, 0))],
        out_shape=[jax.ShapeDtypeStruct((t, 4 * LANES), F32), jax.ShapeDtypeStruct((nc, 4, LANES, LANES), F32)],
        scratch_shapes=[pltpu.VMEM((4, LANES, LANES), F32)],
        operands=(mixed, mixed, lw, k, n, b))


def _call_with_side(body, side, *, name, steps, in_specs, out_specs, out_shape, scratch_shapes, operands):
    if side is None:
        return pl.pallas_call(body, name=name, grid=(steps,), in_specs=in_specs, out_specs=out_specs, out_shape=out_shape,
                              scratch_shapes=scratch_shapes, compiler_params=_params(("arbitrary",)))(*operands)
    srcs, per_dest = side
    ns, ni, no, nscr = len(srcs), len(in_specs), len(out_specs), len(scratch_shapes)

    def full_body(*refs):
        ins, side_in = refs[:ni], refs[ni:ni + ns]
        outs, side_out = refs[ni + ns:ni + ns + no], refs[ni + ns + no:ni + 2 * ns + no]
        scratch, sems = refs[ni + 2 * ns + no:ni + 2 * ns + no + nscr], refs[ni + 2 * ns + no + nscr:]

        @pl.when(pl.program_id(0) == 0)
        def _():
            _exchange(side_in, side_out, sems, per_dest, start=True, wait=False)

        body(*ins, *outs, *scratch)

        @pl.when(pl.program_id(0) == steps - 1)
        def _():
            _exchange(side_in, side_out, sems, per_dest, start=False, wait=True)

    res = pl.pallas_call(
        full_body, name=name, grid=(steps,), in_specs=list(in_specs) + [_ANY] * ns,
        out_specs=list(out_specs) + [_ANY] * ns, out_shape=list(out_shape) + _exchange_out_shapes(srcs),
        scratch_shapes=list(scratch_shapes) + _exchange_sems(ns), compiler_params=_params(("arbitrary",)),
    )(*operands, *srcs)
    return list(res[:no]) + [list(res[no:])]


def _rw_chunk_bwd(mixed, lw, k, n, b, s_in, dy, dr0, dk0, dv0, *, name, side=None):
    t = lw.shape[0]
    nc = t // RW_C

    def body(r_ref, v_ref, lw_ref, k_ref, n_ref, b_ref, sin_ref, dy_ref, dr0_ref, dk0_ref, dv0_ref,
             dr_ref, dlw_ref, dk_ref, dv_ref, dn_ref, db_ref, ds_s):
        @pl.when(pl.program_id(0) == 0)
        def _():
            ds_s[...] = jnp.zeros_like(ds_s)

        consts = _rw_chunk_consts()
        args = [_pairs(x) for x in (r_ref, lw_ref, k_ref, v_ref, n_ref, b_ref)] + [sin_ref[0]]
        _, vjp = jax.vjp(lambda *a: _rw_chunk(*a, consts), *args)
        dr, dlw, dk, dv, dn, db, ds = vjp((_pairs(dy_ref), ds_s[...]))
        for p in range(4):
            sl = slice(p * LANES, (p + 1) * LANES)
            dr_ref[:, sl] = dr[p] + dr0_ref[:, sl]
            dlw_ref[:, sl] = dlw[p]
            dk_ref[:, sl] = dk[p] + dk0_ref[:, sl]
            dv_ref[:, sl] = dv[p] + dv0_ref[:, sl]
            dn_ref[:, sl] = dn[p]
            db_ref[:, sl] = db[p]
        ds_s[...] = ds

    blk = lambda c: pl.BlockSpec((RW_C, 4 * LANES), functools.partial(lambda i, c: (nc - 1 - i, c), c=c))
    return _call_with_side(
        body, side, name=name, steps=nc,
        in_specs=[blk(0), blk(2), blk(0), blk(0), blk(0), blk(0),
                  pl.BlockSpec((1, 4, LANES, LANES), lambda i: (nc - 1 - i, 0, 0, 0)), blk(0), blk(0), blk(0), blk(0)],
        out_specs=[blk(0)] * 6,
        out_shape=[jax.ShapeDtypeStruct((t, 4 * LANES), F32)] * 6,
        scratch_shapes=[pltpu.VMEM((4, LANES, LANES), F32)],
        operands=(mixed, mixed, lw, k, n, b, s_in, dy, dr0, dk0, dv0))


def _f_rms_res(x, g):
    return _f_rms(x, g)[0], x


def _final(x, g, target, *, bt, name):
    t, d = x.shape

    def body(x_ref, g_ref, t_ref, dx_ref, loss_ref, dg_ref):
        tgt = t_ref[...]

        def f(xv, gv):
            err = _f_rms(xv, gv)[0] - tgt
            return 0.5 * jnp.mean(err * err, axis=-1, keepdims=True)

        row_loss, vjp = jax.vjp(f, x_ref[...], g_ref[...])
        dx, dg = vjp(jnp.ones_like(row_loss))
        dx_ref[...] = dx

        @pl.when(pl.program_id(0) == 0)
        def _():
            loss_ref[...] = jnp.zeros_like(loss_ref)
            dg_ref[...] = jnp.zeros_like(dg_ref)

        loss_ref[...] += jnp.broadcast_to(jnp.sum(row_loss, axis=0, keepdims=True), (1, LANES))
        dg_ref[...] += dg

    blk = pl.BlockSpec((bt, d), lambda i: (i, 0))
    return pl.pallas_call(
        body, name=name, grid=(t // bt,),
        in_specs=[blk, pl.BlockSpec((1, d), lambda i: (0, 0)), blk],
        out_specs=[blk, pl.BlockSpec((1, LANES), lambda i: (0, 0)), pl.BlockSpec((1, d), lambda i: (0, 0))],
        out_shape=[jax.ShapeDtypeStruct((t, d), F32), jax.ShapeDtypeStruct((1, LANES), F32),
                   jax.ShapeDtypeStruct((1, d), F32)],
        compiler_params=_params(("arbitrary",)),
    )(x, g, target)


ADAMW_BLOCK_BYTES = 1 << 20


def _adamw(w, g, m, v, *, name, block=None):
    shape = w.shape
    if block is not None:
        return _adamw_blocks(w, g, m, v, block, name)
    c = shape[-1]
    shape3 = (1,) * (3 - len(shape)) + shape if len(shape) <= 3 else (-1,) + shape[-2:]
    args = [a.reshape(shape3) for a in (w, g, m, v)]
    lead, r, _ = args[0].shape
    br = r
    if r * c * 4 > ADAMW_BLOCK_BYTES:
        cands = [b for b in range(8, r, 8) if r % b == 0 and b * c * 4 <= ADAMW_BLOCK_BYTES]
        br = max(cands) if cands else r
    outs = _adamw_blocks(*args, (1, br, c), name)
    return tuple(o.reshape(shape) for o in outs)


def _adamw_blocks(w, g, m, v, block, name):
    shape = w.shape
    assert all(s % b == 0 for s, b in zip(shape, block))

    def body(w_ref, g_ref, m_ref, v_ref, d_ref, nm_ref, nv_ref):
        gv = g_ref[...]
        m_new = ADAM_B1 * m_ref[...] + (1.0 - ADAM_B1) * gv
        v_new = ADAM_B2 * v_ref[...] + (1.0 - ADAM_B2) * (gv * gv)
        m_hat = m_new / (1.0 - ADAM_B1 ** ADAM_STEP)
        v_hat = v_new / (1.0 - ADAM_B2 ** ADAM_STEP)
        d_ref[...] = -ADAM_LR * (m_hat / (jnp.sqrt(v_hat) + ADAM_EPS) + ADAM_WD * w_ref[...])
        nm_ref[...] = m_new
        nv_ref[...] = v_new

    blk = pl.BlockSpec(tuple(block), lambda *ids: ids)
    return pl.pallas_call(
        body, name=name, grid=tuple(s // b for s, b in zip(shape, block)), in_specs=[blk] * 4, out_specs=[blk] * 3,
        out_shape=[jax.ShapeDtypeStruct(shape, F32)] * 3,
        compiler_params=_params(("parallel",) * len(shape)),
    )(w, g, m, v)


BT = 256
BC = 128


def _layer_rows(x, proj, s):
    s = {k: s.get(k) for k in ("y_sb_raw", "y_ssd_raw", "mixed", "ys", "k2", "p_sb", "p_ssd", "p_rw")}
    return dict(
        rms=[(x, D_MODEL, 0)],
        sb_gate=[(s["y_sb_raw"], 512, 0), (proj, 512, 3)],
        ssd_norm=[(s["y_ssd_raw"], 1024, 0), (proj, 1024, C_Z // 1024)],
        rw_pre=[(s["mixed"], 512, 1), (s["mixed"], LANES, 16)],
        rw_post=[(s["ys"], 512, 0), (s["mixed"], 512, 0), (s["k2"], 512, 0), (s["mixed"], 512, 2), (s["mixed"], 512, 3)],
        merge=[(s["p_sb"], 1024, 0), (s["p_ssd"], 1024, 0), (s["p_rw"], 1024, 0),
               (proj, 1024, 3), (proj, 1024, 4), (proj, 1024, 5)],
    )


def _layer_fwd(x, p, nm, side=None):
    s = {}
    (s["h"],) = _rowwise(_f_rms, [(x, D_MODEL, 0)], [p["norm_g"]], [D_MODEL], bt=BT, name=nm + "rms")
    proj = s["proj"] = _mm(s["h"], p["w_in"], name=nm + "proj")
    s["y_sb_raw"], s["lt"] = _sb2_fwd(proj, name=nm + "sb")
    s["xc"] = _colwise(_f_conv, proj, C_XBC, XBC_COLS, p["conv"], bc=BC, name=nm + "conv")
    s["y_ssd_raw"], s["hin"] = _ssd_fwd(s["xc"], proj, p["dt_bias"], p["a_log"], p["d_skip"], name=nm + "ssd")
    s["mixed"] = _colwise(_f_rw_mix, proj, C_RW, RW_COLS, [p["rw_mu"]], bc=BC, name=nm + "mix")
    s["w"], s["k2"], s["n"], s["b"] = _rowwise(_f_rw_pre, [(s["mixed"], 512, 1), (s["mixed"], LANES, 16)], p["rw_pre"],
                                               [512] * 4, bt=BT, name=nm + "rwpre")
    s["ys"], s["st"], *exchanged = _rw_chunk_fwd(s["mixed"], s["w"], s["k2"], s["n"], s["b"], name=nm + "scan", side=side)
    rows = _layer_rows(x, proj, s)
    (s["y_sb"],) = _rowwise(_f_sb_gate, rows["sb_gate"], [], [512], bt=BT, name=nm + "sbgate")
    (s["y_ssd"],) = _rowwise(_f_ssd_norm, rows["ssd_norm"], [p["ssd_norm_g"]], [1024], bt=BT, name=nm + "ssdnorm")
    (s["y_rw"],) = _rowwise(_f_rw_post, rows["rw_post"], p["rw_post"], [512], bt=BT, name=nm + "rwpost")
    s["p_sb"] = _mm(s["y_sb"], p["w_out_sb"], name=nm + "osb")
    s["p_ssd"] = _mm(s["y_ssd"], p["w_out_ssd"], name=nm + "ossd")
    s["p_rw"] = _mm(s["y_rw"], p["w_out_rw"], name=nm + "orw")
    (s["merged"],) = _rowwise(_f_merge, _layer_rows(x, proj, s)["merge"], [], [1024], bt=BT, name=nm + "merge")
    return _mm(s["merged"], p["w_o"], add=x, name=nm + "wo"), s, (exchanged[0] if exchanged else None)


def _layer_bwd(x, dx_out, p, s, nm, side=None):
    g = {}
    proj = s["proj"]
    rows = _layer_rows(x, proj, s)
    g["w_o"] = _mm(s["merged"], dx_out, ta=True, name=nm + "g_wo")
    d_merged = _mm(dx_out, p["w_o"], tb=True, name=nm + "d_merged")
    dp_sb, dp_ssd, dp_rw, d_gates = _rowwise_bwd(_f_merge, rows["merge"], [], [(d_merged, 1024, 0)], bt=BT,
                                                 name=nm + "merge_b", groups=[[0], [1], [2], [3, 4, 5]])
    g["w_out_sb"] = _mm(s["y_sb"], dp_sb, ta=True, name=nm + "g_osb")
    g["w_out_ssd"] = _mm(s["y_ssd"], dp_ssd, ta=True, name=nm + "g_ossd")
    g["w_out_rw"] = _mm(s["y_rw"], dp_rw, ta=True, name=nm + "g_orw")
    dy_sb = _mm(dp_sb, p["w_out_sb"], tb=True, name=nm + "d_ysb")
    dy_ssd = _mm(dp_ssd, p["w_out_ssd"], tb=True, name=nm + "d_yssd")
    dy_rw = _mm(dp_rw, p["w_out_rw"], tb=True, name=nm + "d_yrw")
    dy_sb_raw, d_sbgate = _rowwise_bwd(_f_sb_gate, rows["sb_gate"], [], [(dy_sb, 512, 0)], bt=BT, name=nm + "sbgate_b")
    dq, dk, dv = _sb2_bwd(proj, dy_sb_raw, s["lt"], name=nm + "sb_b")
    dy_ssd_raw, dz, g["ssd_norm_g"] = _rowwise_bwd(_f_ssd_norm, rows["ssd_norm"], [p["ssd_norm_g"]],
                                                   [(dy_ssd, 1024, 0)], bt=BT, name=nm + "ssdnorm_b")
    dxc, ddtr, g["dt_bias"], g["a_log"], g["d_skip"] = _ssd_bwd(
        s["xc"], proj, p["dt_bias"], p["a_log"], p["d_skip"], s["hin"], dy_ssd_raw, name=nm + "ssd_b")
    conv_out = _colwise_bwd(_f_conv, proj, C_XBC, XBC_COLS, p["conv"], dxc, bc=BC, name=nm + "conv_b")
    dxbc, g["conv"] = conv_out[0], conv_out[1:]
    dys, dr0, dk0, dv0, d_rwgate, g["rw_ln_g"], g["rw_ln_b"], g["rw_r_k"] = _rowwise_bwd(
        _f_rw_post, rows["rw_post"], p["rw_post"], [(dy_rw, 512, 0)], bt=BT, name=nm + "rwpost_b")
    dr, dw, dk2, dvv, dn, db, *exchanged = _rw_chunk_bwd(s["mixed"], s["w"], s["k2"], s["n"], s["b"], s["st"], dys,
                                                         dr0, dk0, dv0, name=nm + "scan_b", side=side)
    pre_out = _rowwise_bwd(_f_rw_pre, rows["rw_pre"], p["rw_pre"],
                           [(dw, 512, 0), (dk2, 512, 0), (dn, 512, 0), (db, 512, 0)], bt=BT, name=nm + "rwpre_b")
    dkm, dlo, g["rw_pre"] = pre_out[0], pre_out[1], pre_out[2:]
    d_mixed = jnp.concatenate([dr, dkm, dvv, d_rwgate, dlo], axis=1)
    d_slab, g["rw_mu"] = _colwise_bwd(_f_rw_mix, proj, C_RW, RW_COLS, [p["rw_mu"]], d_mixed, bc=BC, name=nm + "mix_b")
    d_proj = jnp.concatenate([dq, dk, dv, d_sbgate, dz, d_gates, d_slab, ddtr, dxbc], axis=1)
    g["w_in"] = _mm(s["h"], d_proj, ta=True, name=nm + "g_win")
    dh = _mm(d_proj, p["w_in"], tb=True, tn=1024, tk=512, name=nm + "d_h")
    dx, g["norm_g"] = _rowwise_bwd(_f_rms_res, rows["rms"], [p["norm_g"]], [(dh, D_MODEL, 0), (dx_out, D_MODEL, 0)],
                                   bt=BT, name=nm + "rms_b")
    return dx, g, (exchanged[0] if exchanged else None)


MESH = pl.DeviceIdType.MESH
N_DEV = 8
_ANY = pl.BlockSpec(memory_space=pl.ANY)
_CHIP_SEMS = [pltpu.SemaphoreType.DMA((3,)), pltpu.SemaphoreType.DMA((3,)), pltpu.SemaphoreType.DMA]


def _here():
    x, y, c = lax.axis_index("x"), lax.axis_index("y"), lax.axis_index("c")
    return x, y, c, [(1 - x, y), (x, 1 - y), (1 - x, 1 - y)]


def _chip_exchange(srcs, *, per_dest, name):
    n = len(srcs)

    def body(*refs):
        _exchange(refs[:n], refs[n:2 * n], refs[2 * n:], per_dest, start=True, wait=True)

    return pl.pallas_call(
        body, name=name, in_specs=[_ANY] * n, out_specs=[_ANY] * n,
        out_shape=_exchange_out_shapes(srcs), scratch_shapes=_exchange_sems(n),
    )(*srcs)


def _exchange_out_shapes(srcs):
    return [jax.ShapeDtypeStruct((4,) + s.shape[1:], s.dtype) for s in srcs]


def _exchange_sems(n):
    return [pltpu.SemaphoreType.DMA((3 * n,)), pltpu.SemaphoreType.DMA((3 * n,)), pltpu.SemaphoreType.DMA((n,))]


def _exchange(src_refs, out_refs, sems, per_dest, *, start, wait):
    send_sems, recv_sems, local_sems = sems
    x, y, c, chips = _here()
    me = 2 * x + y
    owns, sends, recvs = [], [], []
    for a, (src_ref, out_ref) in enumerate(zip(src_refs, out_refs)):
        pick = (lambda q, s=src_ref: s.at[q]) if per_dest else (lambda q, s=src_ref: s.at[c])
        owns.append(pltpu.make_async_copy(pick(me), out_ref.at[me], local_sems.at[a]))
        for j, (px, py) in enumerate(chips):
            sends.append(pltpu.make_async_remote_copy(
                pick(2 * px + py), out_ref.at[me], send_sems.at[3 * a + j], recv_sems.at[3 * a + j],
                device_id=(px, py, c), device_id_type=MESH))
            recvs.append(pltpu.make_async_remote_copy(
                src_ref.at[0], out_ref.at[2 * px + py], send_sems.at[3 * a + j], recv_sems.at[3 * a + j],
                device_id=(px, py, c), device_id_type=MESH))
    if start:
        for cp in owns + sends:
            cp.start()
    if wait:
        for cp in recvs:
            cp.wait_recv()
        for cp in sends:
            cp.wait_send()
        for cp in owns:
            cp.wait()


def _sibling_swap(srcs, *, other_slot, name):
    n = len(srcs)

    def body(*refs):
        src_refs, out_refs, send_sems, recv_sems = refs[:n], refs[n:2 * n], refs[2 * n], refs[2 * n + 1]
        x, y, c, _ = _here()
        copies = [pltpu.make_async_remote_copy(s.at[1 - c] if other_slot else s, o, send_sems.at[a], recv_sems.at[a],
                                               device_id=(x, y, 1 - c), device_id_type=MESH)
                  for a, (s, o) in enumerate(zip(src_refs, out_refs))]
        for cp in copies:
            cp.start()
        for cp in copies:
            cp.wait()

    return pl.pallas_call(
        body, name=name, in_specs=[_ANY] * n, out_specs=[_ANY] * n,
        out_shape=[jax.ShapeDtypeStruct(s.shape[1:] if other_slot else s.shape, s.dtype) for s in srcs],
        scratch_shapes=[pltpu.SemaphoreType.DMA((n,)), pltpu.SemaphoreType.DMA((n,))],
    )(*srcs)


def _allgather_small(v, *, reduce, name):
    r = v.shape[0]

    def body(v_ref, out_ref, *rest):
        send_sems, recv_sems, local_sem = rest[-3:]
        x, y, c, chips = _here()
        me, sibling = (x, y, c), (x, y, 1 - c)

        def slot(px, py, pc):
            return out_ref.at[4 * px + 2 * py + pc]

        def copy(k, block, to, src=None):
            return pltpu.make_async_remote_copy(
                src_ref=slot(*block) if src is None else src, dst_ref=slot(*block),
                send_sem=send_sems.at[k], recv_sem=recv_sems.at[k], device_id=to, device_id_type=MESH)

        mine = pltpu.make_async_copy(v_ref, slot(*me), local_sem)
        mine.start()
        first = [copy(0, me, sibling, src=v_ref)]
        first += [copy(1 + j, me, (*chip, c), src=v_ref) for j, chip in enumerate(chips)]
        for cp in first:
            cp.start()
        passed = [copy(4 + j, (*chip, c), sibling) for j, chip in enumerate(chips)]
        for j, chip in enumerate(chips):
            copy(1 + j, (*chip, c), me).wait_recv()
            passed[j].start()
        copy(0, sibling, me).wait_recv()
        for j, chip in enumerate(chips):
            copy(4 + j, (*chip, 1 - c), me).wait_recv()
        for cp in first + passed:
            cp.wait_send()
        mine.wait()
        if reduce:
            total = out_ref[0]
            for d in range(1, N_DEV):
                total = total + out_ref[d]
            rest[0][...] = total

    vm = pl.BlockSpec(memory_space=pltpu.VMEM)
    out_shape = [jax.ShapeDtypeStruct((N_DEV, r, LANES), F32)] + ([jax.ShapeDtypeStruct((r, LANES), F32)] if reduce else [])
    return pl.pallas_call(
        body, name=name, in_specs=[vm], out_specs=[vm] * len(out_shape), out_shape=out_shape,
        scratch_shapes=[pltpu.SemaphoreType.DMA((7,)), pltpu.SemaphoreType.DMA((7,)), pltpu.SemaphoreType.DMA],
        compiler_params=pltpu.CompilerParams(vmem_limit_bytes=VMEM_LIMIT),
    )(v)


REDUCE_BLOCK_BYTES = 2 << 20


def _reduce_rows(r, c):
    cands = [b for b in range(16, r + 1, 16) if r % b == 0 and b * c * 4 <= REDUCE_BLOCK_BYTES]
    return max(cands)


def _add_halves(mine2, other, c_idx, *, name):
    _, nq, r, c = mine2.shape
    br = _reduce_rows(r, c)

    def body(c_ref, a_ref, b_ref, o_ref):
        o_ref[...] = (a_ref[0] + b_ref[...]).astype(o_ref.dtype)

    blk = pl.BlockSpec((1, br, c), lambda q, i, c_ref: (q, i, 0))
    return pl.pallas_call(
        body, name=name,
        grid_spec=pltpu.PrefetchScalarGridSpec(
            num_scalar_prefetch=1, grid=(nq, r // br),
            in_specs=[pl.BlockSpec((1, 1, br, c), lambda q, i, c_ref: (c_ref[0], q, i, 0)), blk],
            out_specs=blk),
        out_shape=jax.ShapeDtypeStruct((nq, r, c), BF16),
        compiler_params=_params(("parallel", "parallel")),
    )(c_idx, mine2, other)


def _sum_chips(parts, *, name):
    _, r, c = parts.shape
    br = _reduce_rows(r, c)

    def body(p_ref, o_ref):
        total = p_ref[0].astype(F32)
        for q in range(1, 4):
            total = total + p_ref[q].astype(F32)
        o_ref[...] = total

    return pl.pallas_call(
        body, name=name, grid=(r // br,),
        in_specs=[pl.BlockSpec((4, br, c), lambda i: (0, i, 0))],
        out_specs=pl.BlockSpec((br, c), lambda i: (i, 0)),
        out_shape=jax.ShapeDtypeStruct((r, c), F32),
        compiler_params=_params(("parallel",)),
    )(parts)


BIG = ("w_in", "w_out_sb", "w_out_ssd", "w_out_rw", "w_o")
BIG_AXIS = {"w_in": 2, "w_out_sb": 2, "w_out_ssd": 1, "w_out_rw": 2, "w_o": 1}
SMALL_SHARDED = {"conv_w": 320, "rw_w_up": 128, "rw_a_up": 128}
SMALL = ("norm_g", "conv_w", "conv_b", "dt_bias", "a_log", "d_skip", "ssd_norm_g", "rw_mu", "rw_w0", "rw_w_up",
         "rw_a0", "rw_a_up", "rw_k_k", "rw_k_a", "rw_r_k", "rw_ln_g", "rw_ln_b", "final_g")


def _rows_of(a):
    flat = a.reshape(-1)
    pad = (-flat.shape[0]) % LANES
    return jnp.pad(flat, (0, pad)).reshape(-1, LANES)


def _pack_rows(arrays, multiple=8):
    rows = jnp.concatenate([_rows_of(a) for a in arrays], axis=0)
    pad = (-rows.shape[0]) % multiple
    return jnp.pad(rows, ((0, pad), (0, 0)))


def _unpack_rows(rows, shapes):
    out, off = [], 0
    for shp in shapes:
        n = 1
        for d in shp:
            n *= d
        nr = -(-n // LANES)
        out.append(rows[off:off + nr].reshape(-1)[:n].reshape(shp))
        off += nr
    return out


COL_MAP = ((0, 3072, 0), (3072, 4352, C_XBC), (4352, 4368, C_DT), (4368, 6544, C_RW), (6544, 9616, C_GATES))
SHARD_COLS = N_IN // 4


def _w_in_from_shards(shards):
    pieces = []
    for a, b, dst in sorted(COL_MAP, key=lambda m: m[2]):
        if pieces and dst > pieces[-1][0]:
            pieces.append((dst, jnp.zeros((shards[0].shape[0], dst - pieces[-1][0]), shards[0].dtype)))
        for q in range(4):
            lo, hi = max(a, q * SHARD_COLS), min(b, (q + 1) * SHARD_COLS)
            if lo < hi:
                pieces.append((dst + hi - a, shards[q][:, lo - q * SHARD_COLS:hi - q * SHARD_COLS]))
    return jnp.concatenate([p for _, p in pieces], axis=1)


def _w_in_shard(g, q):
    pieces = []
    for a, b, dst in COL_MAP:
        lo, hi = max(a, q * SHARD_COLS), min(b, (q + 1) * SHARD_COLS)
        if lo < hi:
            pieces.append(g[:, dst + lo - a:dst + hi - a])
    return jnp.concatenate(pieces, axis=1)


def _row_halves(a):
    return a.reshape(2, a.shape[0] // 2, a.shape[1])


def _join_halves(core, mine, theirs):
    return jnp.where(core == 0, jnp.concatenate([mine, theirs], axis=-2), jnp.concatenate([theirs, mine], axis=-2))


def kernel(x, norm_g, w_in, conv_w, conv_b, dt_bias, a_log, d_skip, ssd_norm_g, rw_mu, rw_w0, rw_w_up, rw_a0, rw_a_up, rw_k_k, rw_k_a, rw_r_k, rw_ln_g, rw_ln_b, w_out_sb, w_out_ssd, w_out_rw, w_o, final_g, loss_target, m_norm_g, m_w_in, m_conv_w, m_conv_b, m_dt_bias, m_a_log, m_d_skip, m_ssd_norm_g, m_rw_mu, m_rw_w0, m_rw_w_up, m_rw_a0, m_rw_a_up, m_rw_k_k, m_rw_k_a, m_rw_r_k, m_rw_ln_g, m_rw_ln_b, m_w_out_sb, m_w_out_ssd, m_w_out_rw, m_w_o, m_final_g, v_norm_g, v_w_in, v_conv_w, v_conv_b, v_dt_bias, v_a_log, v_d_skip, v_ssd_norm_g, v_rw_mu, v_rw_w0, v_rw_w_up, v_rw_a0, v_rw_a_up, v_rw_k_k, v_rw_k_a, v_rw_r_k, v_rw_ln_g, v_rw_ln_b, v_w_out_sb, v_w_out_ssd, v_w_out_rw, v_w_o, v_final_g):
    names = ("norm_g", "w_in", "conv_w", "conv_b", "dt_bias", "a_log", "d_skip", "ssd_norm_g", "rw_mu", "rw_w0",
             "rw_w_up", "rw_a0", "rw_a_up", "rw_k_k", "rw_k_a", "rw_r_k", "rw_ln_g", "rw_ln_b", "w_out_sb",
             "w_out_ssd", "w_out_rw", "w_o", "final_g")
    w_loc = dict(zip(names, (norm_g, w_in, conv_w, conv_b, dt_bias, a_log, d_skip, ssd_norm_g, rw_mu, rw_w0, rw_w_up,
                             rw_a0, rw_a_up, rw_k_k, rw_k_a, rw_r_k, rw_ln_g, rw_ln_b, w_out_sb, w_out_ssd, w_out_rw,
                             w_o, final_g)))
    m_loc = dict(zip(names, (m_norm_g, m_w_in, m_conv_w, m_conv_b, m_dt_bias, m_a_log, m_d_skip, m_ssd_norm_g,
                             m_rw_mu, m_rw_w0, m_rw_w_up, m_rw_a0, m_rw_a_up, m_rw_k_k, m_rw_k_a, m_rw_r_k,
                             m_rw_ln_g, m_rw_ln_b, m_w_out_sb, m_w_out_ssd, m_w_out_rw, m_w_o, m_final_g)))
    v_loc = dict(zip(names, (v_norm_g, v_w_in, v_conv_w, v_conv_b, v_dt_bias, v_a_log, v_d_skip, v_ssd_norm_g,
                             v_rw_mu, v_rw_w0, v_rw_w_up, v_rw_a0, v_rw_a_up, v_rw_k_k, v_rw_k_a, v_rw_r_k,
                             v_rw_ln_g, v_rw_ln_b, v_w_out_sb, v_w_out_ssd, v_w_out_rw, v_w_o, v_final_g)))
    chip = 2 * lax.axis_index("x") + lax.axis_index("y")
    core = lax.axis_index("c")

    def gather_srcs(i):
        return [_row_halves(w_loc[n][i].astype(BF16)) for n in BIG]

    def gathered(mine, nm):
        theirs = _sibling_swap(mine, other_slot=False, name=nm)
        out = {}
        for n, a, b in zip(BIG, mine, theirs):
            shards = _join_halves(core, a, b)
            out[n] = (_w_in_from_shards([shards[q] for q in range(4)]) if n == "w_in"
                      else jnp.concatenate([shards[q] for q in range(4)], axis=BIG_AXIS[n] - 1))
        return out

    full = {}
    sm_names = tuple(SMALL_SHARDED)
    sm_shapes = [w_loc[n].shape for n in sm_names]
    (got_sm,) = _allgather_small(_pack_rows([w_loc[n] for n in sm_names]), reduce=False, name="gather_small")
    per_chip = [_unpack_rows(got_sm[4 * (q // 2) + 2 * (q % 2)], sm_shapes) for q in range(4)]
    for i, n in enumerate(sm_names):
        full[n] = jnp.concatenate([per_chip[q][i] for q in range(4)], axis=-1)

    def pad16(a):
        return jnp.zeros((1, LANES), F32).at[0, :SSD_HEADS].set(a)

    def layer_params(i, big):
        row = lambda n: w_loc[n][i].reshape(1, -1)
        cw = full["conv_w"][i]
        return dict(
            norm_g=row("norm_g"), w_in=big["w_in"], conv=[cw[k][None] for k in range(4)] + [row("conv_b")],
            dt_bias=pad16(dt_bias[i]), a_log=pad16(a_log[i]), d_skip=pad16(d_skip[i]),
            ssd_norm_g=row("ssd_norm_g"), rw_mu=row("rw_mu"),
            rw_pre=[row("rw_w0"), jnp.zeros((LANES, 512), F32).at[:HEAD].set(full["rw_w_up"][i]), row("rw_a0"),
                    jnp.zeros((LANES, 512), F32).at[HEAD:].set(full["rw_a_up"][i]), row("rw_k_k"), row("rw_k_a")],
            rw_post=[row("rw_ln_g"), row("rw_ln_b"), row("rw_r_k")],
            w_out_sb=big["w_out_sb"], w_out_ssd=big["w_out_ssd"], w_out_rw=big["w_out_rw"], w_o=big["w_o"])

    c_idx = core.reshape(1).astype(jnp.int32)

    def reduce_prepare(g, nm):
        sends = []
        for n in BIG:
            per_chip = ([_w_in_shard(g[n], q) for q in range(4)] if n == "w_in"
                        else jnp.split(g[n], 4, axis=BIG_AXIS[n] - 1))
            sends.append(jnp.stack([_row_halves(p) for p in per_chip], axis=1))
        others = _sibling_swap(sends, other_slot=True, name=nm + "sibling")
        return [_add_halves(s, o, c_idx, name=nm + "add_" + n) for n, s, o in zip(BIG, sends, others)]

    def reduce_finish(exchanged, nm):
        mine = [_sum_chips(p, name=nm + "sum_" + n) for n, p in zip(BIG, exchanged)]
        theirs = _sibling_swap(mine, other_slot=False, name=nm + "join")
        return {n: _join_halves(core, a, b) for n, a, b in zip(BIG, mine, theirs)}

    assert DEPTH == 2
    params, xs, saved, grads, totals = [None] * 2, [x[0], None, None], [None] * 2, [None] * 2, [None] * 2
    params[0] = layer_params(0, gathered(_chip_exchange(gather_srcs(0), per_dest=False, name="gather_l0"), "gather_l0_join"))
    xs[1], saved[0], got = _layer_fwd(xs[0], params[0], "l0_", side=(gather_srcs(1), False))
    params[1] = layer_params(1, gathered(got, "gather_l1_join"))
    xs[2], saved[1], _ = _layer_fwd(xs[1], params[1], "l1_")
    dx, loss_row, g_final = _final(xs[2], final_g.reshape(1, -1), loss_target[0], bt=BT, name="final")
    dx, grads[1], _ = _layer_bwd(xs[1], dx, params[1], saved[1], "l1_")
    dx, grads[0], got = _layer_bwd(xs[0], dx, params[0], saved[0], "l0_", side=(reduce_prepare(grads[1], "reduce_l1_"), True))
    totals[1] = reduce_finish(got, "reduce_l1_")
    totals[0] = reduce_finish(_chip_exchange(reduce_prepare(grads[0], "reduce_l0_"), per_dest=True, name="reduce_l0_chips"),
                              "reduce_l0_")

    def stacked(fn):
        return jnp.stack([fn(grads[i]) for i in range(DEPTH)])

    g_loc = {
        "norm_g": stacked(lambda g: g["norm_g"][0]),
        "conv_w": stacked(lambda g: jnp.concatenate(g["conv"][:4], axis=0)),
        "conv_b": stacked(lambda g: g["conv"][4][0]),
        "dt_bias": stacked(lambda g: g["dt_bias"][0, :SSD_HEADS]),
        "a_log": stacked(lambda g: g["a_log"][0, :SSD_HEADS]),
        "d_skip": stacked(lambda g: g["d_skip"][0, :SSD_HEADS]),
        "ssd_norm_g": stacked(lambda g: g["ssd_norm_g"][0]),
        "rw_mu": stacked(lambda g: g["rw_mu"][0]),
        "rw_w0": stacked(lambda g: g["rw_pre"][0][0]),
        "rw_w_up": stacked(lambda g: g["rw_pre"][1][:HEAD]),
        "rw_a0": stacked(lambda g: g["rw_pre"][2][0]),
        "rw_a_up": stacked(lambda g: g["rw_pre"][3][HEAD:]),
        "rw_k_k": stacked(lambda g: g["rw_pre"][4][0]),
        "rw_k_a": stacked(lambda g: g["rw_pre"][5][0]),
        "rw_r_k": stacked(lambda g: g["rw_r_k"].reshape(8, HEAD)),
        "rw_ln_g": stacked(lambda g: g["rw_ln_g"][0]),
        "rw_ln_b": stacked(lambda g: g["rw_ln_b"][0]),
        "final_g": g_final[0],
    }

    g_out = {n: jnp.stack([totals[0][n], totals[1][n]]) for n in BIG}

    sm_all = SMALL + ("loss",)
    sm_full_shapes = [g_loc[n].shape for n in SMALL] + [(1,)]
    _, summed = _allgather_small(_pack_rows([g_loc[n] for n in SMALL] + [loss_row[0, :1]]), reduce=True, name="reduce_small")
    sm = dict(zip(sm_all, _unpack_rows(summed, sm_full_shapes)))
    for n in SMALL:
        g_out[n] = sm[n]
    for n, wd in SMALL_SHARDED.items():
        g_out[n] = lax.dynamic_slice_in_dim(sm[n], chip * wd, wd, axis=sm[n].ndim - 1)
    loss = sm["loss"][0]

    upd = {n: _adamw(w_loc[n], g_out[n], m_loc[n], v_loc[n], name="adamw_" + n) for n in names if n != "w_in"}
    cols = SHARD_COLS // 4
    to_cols = lambda a: jnp.transpose(a, (2, 0, 1)).reshape(4, cols, DEPTH, D_MODEL)
    from_cols = lambda a: jnp.transpose(a.reshape(SHARD_COLS, DEPTH, D_MODEL), (1, 2, 0))
    g_cols = lax.optimization_barrier(to_cols(g_out["w_in"]))
    g_out["w_in"] = from_cols(g_cols)
    upd["w_in"] = tuple(from_cols(a) for a in _adamw(
        to_cols(w_loc["w_in"]), g_cols, to_cols(m_loc["w_in"]), to_cols(v_loc["w_in"]),
        name="adamw_w_in", block=(1, cols, DEPTH, D_MODEL // 2)))
    return (loss, dx[None], *[g_out[n] for n in names], *[upd[n][0] for n in names],
            *[upd[n][1] for n in names], *[upd[n][2] for n in names])
```

```python
import functools

import jax
import jax.numpy as jnp
from jax import lax
from jax.experimental import pallas as pl
from jax.experimental.pallas import tpu as pltpu

F32 = jnp.float32
BF16 = jnp.bfloat16

D_MODEL = 1024
DEPTH = 2
HEAD = 64
LANES = 128
CHUNK = 128
RMS_EPS = 1e-6
GN_EPS = 64e-5
VMEM_LIMIT = 56 * 1024 * 1024

N_IN = 9616
N_PAD = 9728
C_SB, C_Z, C_GATES, C_RW, C_LO, C_DT, C_XBC = 0, 2048, 3072, 6144, 8192, 8320, 8448
RW_COLS = 2176
XBC_COLS = 1280

ADAM_LR, ADAM_B1, ADAM_B2, ADAM_EPS, ADAM_WD, ADAM_STEP = 0.001, 0.9, 0.999, 1e-08, 0.01, 10


def _params(sem=None):
    return pltpu.CompilerParams(dimension_semantics=sem, vmem_limit_bytes=VMEM_LIMIT)


@jax.custom_vjp
def _sigmoid(x):
    return 1.0 / (1.0 + jnp.exp(-x))


def _sigmoid_fwd(x):
    s = _sigmoid(x)
    return s, s


def _sigmoid_bwd(s, g):
    return (g * s * (1.0 - s),)


_sigmoid.defvjp(_sigmoid_fwd, _sigmoid_bwd)


@jax.custom_vjp
def _silu(x):
    return x * _sigmoid(x)


def _silu_fwd(x):
    s = _sigmoid(x)
    return x * s, (x, s)


def _silu_bwd(res, g):
    x, s = res
    return (g * (s + x * s * (1.0 - s)),)


_silu.defvjp(_silu_fwd, _silu_bwd)


@jax.custom_vjp
def _softplus(x):
    return jnp.maximum(x, 0.0) + jnp.log(1.0 + jnp.exp(-jnp.abs(x)))


def _softplus_fwd(x):
    return _softplus(x), x


def _softplus_bwd(x, g):
    return (g * _sigmoid(x),)


_softplus.defvjp(_softplus_fwd, _softplus_bwd)


def _dot(a, b, dims):
    return lax.dot_general(a.astype(BF16), b.astype(BF16), (dims, ((), ())), preferred_element_type=F32)


def _dot_nn(a, b):
    return _dot(a, b, ((1,), (0,)))


def _dot_nt(a, b):
    return _dot(a, b, ((1,), (1,)))


def _dot_tn(a, b):
    return _dot(a, b, ((0,), (0,)))


@jax.custom_vjp
def _bdot(a, b):
    return _dot_nn(a, b)


def _bdot_fwd(a, b):
    return _dot_nn(a, b), (a, b)


def _bdot_bwd(res, g):
    a, b = res
    return _dot_nt(g, b), _dot_tn(a, g)


_bdot.defvjp(_bdot_fwd, _bdot_bwd)


def _split2(x):
    hi = x.astype(BF16)
    lo = (x - hi.astype(F32)).astype(BF16)
    return hi, lo


_NT = (((1,), (1,)), ((), ()))
_NN = (((1,), (0,)), ((), ()))
_TN = (((0,), (0,)), ((), ()))


def _dot2(x, m, dn=_NN):
    hi, lo = _split2(x)
    return (lax.dot_general(hi, m, dn, preferred_element_type=F32)
            + lax.dot_general(lo, m, dn, preferred_element_type=F32))


def _dot2_tn(x, m):
    return _dot2(x, m, _TN)


def _seg_matrix(n):
    r = lax.broadcasted_iota(jnp.int32, (n, n), 0) // HEAD
    c = lax.broadcasted_iota(jnp.int32, (n, n), 1) // HEAD
    return (r == c).astype(BF16)


@jax.custom_vjp
def _segsum2(x, seg):
    return _dot2(x, seg)


def _segsum2_fwd(x, seg):
    return _dot2(x, seg), seg


def _segsum2_bwd(seg, g):
    return _dot2(g, seg), jnp.zeros_like(seg)


_segsum2.defvjp(_segsum2_fwd, _segsum2_bwd)


def _make_segsum(seg):
    return lambda x: _segsum2(x, seg)


def _shift_down_raw(x, k):
    row = lax.broadcasted_iota(jnp.int32, x.shape, 0)
    return jnp.where(row >= k, pltpu.roll(x, k, 0), 0.0)


def _shift_up_raw(x, k):
    t = x.shape[0]
    row = lax.broadcasted_iota(jnp.int32, x.shape, 0)
    return jnp.where(row < t - k, pltpu.roll(x, t - k, 0), 0.0)


@functools.partial(jax.custom_vjp, nondiff_argnums=(1,))
def _shift_down(x, k):
    return _shift_down_raw(x, k)


def _shift_down_fwd(x, k):
    return _shift_down_raw(x, k), None


def _shift_down_bwd(k, _, g):
    return (_shift_up_raw(g, k),)


_shift_down.defvjp(_shift_down_fwd, _shift_down_bwd)


def _mm(a, b, *, name, ta=False, tb=False, add=None, out_dtype=F32, tm=2048, tn=512, tk=None, side=None):
    m, k = (a.shape[1], a.shape[0]) if ta else a.shape
    n = b.shape[0] if tb else b.shape[1]
    tm, tn = min(tm, m), min(tn, n)
    tk = k if tk is None else tk
    nk = k // tk
    assert m % tm == 0 and n % tn == 0 and k % tk == 0
    dims = ((0 if ta else 1,), (1 if tb else 0,))

    def body(a_ref, b_ref, *refs):
        o_ref, acc_ref = refs[-2:]
        p = _dot(a_ref[...], b_ref[...], dims)

        def emit(total):
            if add is not None:
                total = total + refs[0][...]
            o_ref[...] = total.astype(o_ref.dtype)

        if nk == 1:
            emit(p)
        else:
            kk = pl.program_id(2)

            @pl.when(kk == 0)
            def _():
                acc_ref[...] = p

            @pl.when(kk > 0)
            def _():
                acc_ref[...] += p

            @pl.when(kk == nk - 1)
            def _():
                emit(acc_ref[...])

    a_spec = pl.BlockSpec((tk, tm), lambda i, j, kk: (kk, i)) if ta else pl.BlockSpec((tm, tk), lambda i, j, kk: (i, kk))
    b_spec = pl.BlockSpec((tn, tk), lambda i, j, kk: (j, kk)) if tb else pl.BlockSpec((tk, tn), lambda i, j, kk: (kk, j))
    o_spec = pl.BlockSpec((tm, tn), lambda i, j, kk: (i, j))
    res = _call_with_side(
        body, side, name=name, grid=(m // tm, n // tn, nk), semantics=("parallel", "parallel", "arbitrary"),
        in_specs=[a_spec, b_spec] + ([o_spec] if add is not None else []), out_specs=[o_spec],
        out_shape=[jax.ShapeDtypeStruct((m, n), out_dtype)],
        scratch_shapes=[pltpu.VMEM((tm, tn) if nk > 1 else (8, LANES), F32)],
        operands=(a, b) + ((add,) if add is not None else ()))
    return res[0] if side is None else (res[0], res[1])


def _row_specs(rows, bt):
    return [pl.BlockSpec((bt, w), functools.partial(lambda i, c: (i, c), c=c)) for _, w, c in rows]


def _full_spec(p):
    return pl.BlockSpec(p.shape, functools.partial(lambda i, nd: (0,) * nd, nd=p.ndim))


def _rowwise(f, rows, pars, out_widths, *, bt, name, acc_widths=()):
    t = rows[0][0].shape[0]
    nr, npar, no, na = len(rows), len(pars), len(out_widths), len(acc_widths)

    def body(*refs):
        vals = [r[...] for r in refs[:nr + npar]]
        outs = f(*vals)
        for o_ref, o in zip(refs[nr + npar:nr + npar + no], outs[:no]):
            o_ref[...] = o.astype(o_ref.dtype)
        if na:
            first = pl.program_id(0) == 0
            for a_ref, a in zip(refs[nr + npar + no:], outs[no:]):
                @pl.when(first)
                def _():
                    a_ref[...] = jnp.zeros_like(a_ref)
                a_ref[...] += a

    return pl.pallas_call(
        body, name=name, grid=(t // bt,),
        in_specs=_row_specs(rows, bt) + [_full_spec(p) for p in pars],
        out_specs=[pl.BlockSpec((bt, w), lambda i: (i, 0)) for w in out_widths]
        + [pl.BlockSpec((1, w), lambda i: (0, 0)) for w in acc_widths],
        out_shape=[jax.ShapeDtypeStruct((t, w), F32) for w in out_widths]
        + [jax.ShapeDtypeStruct((1, w), F32) for w in acc_widths],
        compiler_params=_params(("arbitrary",)),
    )(*[r[0] for r in rows], *pars)


def _rowwise_bwd(f, rows, pars, douts, *, bt, name, groups=None):
    t = rows[0][0].shape[0]
    nr, npar, nd = len(rows), len(pars), len(douts)
    groups = [[i] for i in range(nr)] if groups is None else groups
    widths = [r[1] for r in rows]

    def body(*refs):
        vals = [r[...] for r in refs[:nr + npar]]
        cts = tuple(r[...] for r in refs[nr + npar:nr + npar + nd])
        _, vjp = jax.vjp(lambda *a: tuple(f(*a)), *vals)
        grads = vjp(cts)
        out_refs = refs[nr + npar + nd:]
        for g_ref, grp in zip(out_refs[:len(groups)], groups):
            off = 0
            for i in grp:
                g_ref[:, off:off + widths[i]] = grads[i]
                off += widths[i]
        first = pl.program_id(0) == 0
        for p_ref, g in zip(out_refs[len(groups):], grads[nr:]):
            @pl.when(first)
            def _():
                p_ref[...] = jnp.zeros_like(p_ref)
            p_ref[...] += g

    gw = [sum(widths[i] for i in grp) for grp in groups]
    return pl.pallas_call(
        body, name=name, grid=(t // bt,),
        in_specs=_row_specs(rows, bt) + [_full_spec(p) for p in pars] + _row_specs(douts, bt),
        out_specs=[pl.BlockSpec((bt, w), lambda i: (i, 0)) for w in gw] + [_full_spec(p) for p in pars],
        out_shape=[jax.ShapeDtypeStruct((t, w), F32) for w in gw] + [jax.ShapeDtypeStruct(p.shape, F32) for p in pars],
        compiler_params=_params(("arbitrary",)),
    )(*[r[0] for r in rows], *pars, *[d[0] for d in douts])


def _colwise(f, x, c0, ncols, pars, *, bc, name):
    t = x.shape[0]

    def body(x_ref, *refs):
        o_ref = refs[-1]
        o_ref[...] = f(x_ref[...], *[r[...] for r in refs[:-1]])

    return pl.pallas_call(
        body, name=name, grid=(ncols // bc,),
        in_specs=[pl.BlockSpec((t, bc), lambda j: (0, j + c0 // bc))]
        + [pl.BlockSpec((p.shape[0], bc), lambda j: (0, j)) for p in pars],
        out_specs=pl.BlockSpec((t, bc), lambda j: (0, j)),
        out_shape=jax.ShapeDtypeStruct((t, ncols), F32),
        compiler_params=_params(("parallel",)),
    )(x, *pars)


def _colwise_bwd(f, x, c0, ncols, pars, dout, *, bc, name):
    t = x.shape[0]
    npar = len(pars)

    def body(x_ref, *refs):
        vals = [x_ref[...]] + [r[...] for r in refs[:npar]]
        _, vjp = jax.vjp(f, *vals)
        grads = vjp(refs[npar][...])
        for g_ref, g in zip(refs[npar + 1:], grads):
            g_ref[...] = g

    return pl.pallas_call(
        body, name=name, grid=(ncols // bc,),
        in_specs=[pl.BlockSpec((t, bc), lambda j: (0, j + c0 // bc))]
        + [pl.BlockSpec((p.shape[0], bc), lambda j: (0, j)) for p in pars]
        + [pl.BlockSpec((t, bc), lambda j: (0, j))],
        out_specs=[pl.BlockSpec((t, bc), lambda j: (0, j))]
        + [pl.BlockSpec((p.shape[0], bc), lambda j: (0, j)) for p in pars],
        out_shape=[jax.ShapeDtypeStruct((t, ncols), F32)] + [jax.ShapeDtypeStruct(p.shape, F32) for p in pars],
        compiler_params=_params(("parallel",)),
    )(x, *pars, dout)


def _f_rms(x, g):
    return (x * lax.rsqrt(jnp.mean(x * x, axis=-1, keepdims=True) + RMS_EPS) * g,)


def _f_sb_gate(y, gate):
    return (y * _silu(gate),)


def _f_ssd_norm(y, z, g):
    u = y * _silu(z)
    return (u * lax.rsqrt(jnp.mean(u * u, axis=-1, keepdims=True) + RMS_EPS) * g,)


def _f_merge(p_sb, p_ssd, p_rw, g_sb, g_ssd, g_rw):
    return (_sigmoid(g_sb) * p_sb + _sigmoid(g_ssd) * p_ssd + _sigmoid(g_rw) * p_rw,)


def _f_rw_pre(k, lo, w0, w_up, a0, a_up, k_k, k_a):
    segsum = _make_segsum(_seg_matrix(k.shape[1]))
    lane = lax.broadcasted_iota(jnp.int32, lo.shape, 1)
    w_lo = jnp.where(lane < HEAD, jnp.tanh(lo), 0.0)
    a_lo = jnp.where(lane >= HEAD, lo, 0.0)
    w = -_softplus(-(w0 + _bdot(w_lo, w_up))) - 0.5
    log_decay = -jnp.exp(w)
    a = _sigmoid(a0 + _bdot(a_lo, a_up))
    kk = k * k_k
    kk = kk / jnp.maximum(jnp.sqrt(segsum(kk * kk)), 1e-12)
    return log_decay, k * (1.0 + (a - 1.0) * k_a), -kk, kk * a


def _f_rw_post(y, r, k2, v, gate, ln_g, ln_b, r_k):
    segsum = _make_segsum(_seg_matrix(y.shape[1]))
    yc = y - segsum(y) * (1.0 / HEAD)
    var = segsum(yc * yc) * (1.0 / HEAD)
    yn = yc * lax.rsqrt(var + GN_EPS) * ln_g + ln_b
    return ((yn + segsum(r * k2 * r_k) * v) * _silu(gate),)


def _f_rw_mix(slab, mu):
    return slab + (_shift_down(slab, 1) - slab) * mu


def _f_conv(x, w0, w1, w2, w3, b):
    acc = x * w3 + b
    for i, w in enumerate((w0, w1, w2)):
        acc = acc + _shift_down(x, 3 - i) * w
    return _silu(acc)


def _log_sigmoid(z):
    return jnp.minimum(z, 0.0) - jnp.log(1.0 + jnp.exp(-jnp.abs(z)))


def _prefix_matrix(kind):
    j = lax.broadcasted_iota(jnp.int32, (CHUNK, 2 * CHUNK), 0)
    s = lax.broadcasted_iota(jnp.int32, (CHUNK, 2 * CHUNK), 1)
    tri = {"gt": j > s, "le": j <= s, "lt": j < s}[kind]
    return (tri | (s >= CHUNK)).astype(BF16)


def _sb_specs(t):
    q = pl.BlockSpec((CHUNK, LANES), lambda j, i: (i, j))
    k = pl.BlockSpec((t, LANES), lambda j, i: (0, 4 + j))
    v = pl.BlockSpec((t, LANES), lambda j, i: (0, 8 + j))
    return q, k, v


def _sb_fwd(proj, *, name):
    t = proj.shape[0]
    scale = HEAD ** -0.5

    def body(q_ref, k_ref, v_ref, y_ref, lt_ref):
        i = pl.program_id(1)
        lane = lax.broadcasted_iota(jnp.int32, (CHUNK, LANES), 1)
        diff = (lax.broadcasted_iota(jnp.int32, (CHUNK, CHUNK), 1)
                - lax.broadcasted_iota(jnp.int32, (CHUNK, CHUNK), 0))
        m_f = _prefix_matrix("gt")
        q = q_ref[...] * scale
        qh = [jnp.where((lane // HEAD) == h, q, 0.0).astype(BF16) for h in (0, 1)]

        def step(it, carry):
            off = pl.multiple_of((i - it) * CHUNK, CHUNK)
            kblk = k_ref[pl.ds(off, CHUNK), :].astype(BF16)
            vblk = v_ref[pl.ds(off, CHUNK), :].astype(BF16)
            mask = diff < it * CHUNK
            new = []
            for h in (0, 1):
                c, acc = carry[2 * h], carry[2 * h + 1]
                z = lax.dot_general(qh[h], kblk, _NT, preferred_element_type=F32)
                lb = _log_sigmoid(z)
                w2 = _dot2(jnp.where(mask, lb - z, 0.0), m_f)
                att = jnp.where(mask, jnp.exp(lb + c + w2[:, :CHUNK]), 0.0)
                acc = acc + lax.dot_general(att.astype(BF16), vblk, _NN, preferred_element_type=F32)
                new += [c + w2[:, CHUNK:], acc]
            return tuple(new)

        zero = jnp.zeros((CHUNK, LANES), F32)
        c_a, acc_a, c_b, acc_b = lax.fori_loop(0, i + 1, step, (zero, zero, zero, zero))
        y_ref[...] = jnp.where(lane < HEAD, acc_a, acc_b)
        lt_ref[0] = c_a
        lt_ref[1] = c_b

    return pl.pallas_call(
        body, name=name, grid=(4, t // CHUNK),
        in_specs=list(_sb_specs(t)),
        out_specs=[pl.BlockSpec((CHUNK, LANES), lambda j, i: (i, j)),
                   pl.BlockSpec((2, CHUNK, LANES), lambda j, i: (j, i, 0))],
        out_shape=[jax.ShapeDtypeStruct((t, 4 * LANES), F32), jax.ShapeDtypeStruct((8, t, LANES), F32)],
        compiler_params=_params(("parallel", "arbitrary")),
    )(proj, proj, proj)


def _sb_bwd(proj, dy, lt, *, name):
    t = proj.shape[0]
    scale = HEAD ** -0.5

    def body(q_ref, k_ref, v_ref, dy_ref, lt_ref, dq_ref, dk_ref, dv_ref):
        i = pl.program_id(1)

        @pl.when(i == 0)
        def _():
            dk_ref[...] = jnp.zeros_like(dk_ref)
            dv_ref[...] = jnp.zeros_like(dv_ref)

        lane = lax.broadcasted_iota(jnp.int32, (CHUNK, LANES), 1)
        diff = (lax.broadcasted_iota(jnp.int32, (CHUNK, CHUNK), 1)
                - lax.broadcasted_iota(jnp.int32, (CHUNK, CHUNK), 0))
        m_le, m_lt = _prefix_matrix("le"), _prefix_matrix("lt")
        q = q_ref[...] * scale
        dy_blk = dy_ref[...]
        qh = [jnp.where((lane // HEAD) == h, q, 0.0).astype(BF16) for h in (0, 1)]
        doh = [jnp.where((lane // HEAD) == h, dy_blk, 0.0).astype(BF16) for h in (0, 1)]
        lth = [lt_ref[0], lt_ref[1]]

        def step(kb, carry):
            off = pl.multiple_of(kb * CHUNK, CHUNK)
            kblk = k_ref[pl.ds(off, CHUNK), :].astype(BF16)
            vblk = v_ref[pl.ds(off, CHUNK), :].astype(BF16)
            mask = diff < (i - kb) * CHUNK
            new = []
            dk_acc = jnp.zeros((CHUNK, LANES), F32)
            dv_acc = jnp.zeros((CHUNK, LANES), F32)
            for h in (0, 1):
                cp, cg, dq = carry[3 * h:3 * h + 3]
                z = lax.dot_general(qh[h], kblk, _NT, preferred_element_type=F32)
                lb = _log_sigmoid(z)
                w2 = _dot2(jnp.where(mask, lb - z, 0.0), m_le)
                att = jnp.where(mask, jnp.exp(lb + lth[h] - cp - w2[:, :CHUNK]), 0.0)
                d_att = lax.dot_general(doh[h], vblk, _NT, preferred_element_type=F32)
                d_e = d_att * att
                g2 = _dot2(d_e, m_lt)
                sig = jnp.exp(lb)
                dz = jnp.where(mask, d_e * (1.0 - sig) - (cg + g2[:, :CHUNK]) * sig, 0.0).astype(BF16)
                dq = dq + lax.dot_general(dz, kblk, _NN, preferred_element_type=F32)
                dk_acc = dk_acc + lax.dot_general(dz, qh[h], _TN, preferred_element_type=F32)
                dv_acc = dv_acc + lax.dot_general(att.astype(BF16), doh[h], _TN, preferred_element_type=F32)
                new += [cp + w2[:, CHUNK:], cg + g2[:, CHUNK:], dq]
            dk_ref[pl.ds(off, CHUNK), :] += dk_acc
            dv_ref[pl.ds(off, CHUNK), :] += dv_acc
            return tuple(new)

        zero = jnp.zeros((CHUNK, LANES), F32)
        out = lax.fori_loop(0, i + 1, step, (zero,) * 6)
        dq_ref[...] = jnp.where(lane < HEAD, out[2], out[5]) * scale

    q_spec, k_spec, v_spec = _sb_specs(t)
    blk = pl.BlockSpec((CHUNK, LANES), lambda j, i: (i, j))
    col = pl.BlockSpec((t, LANES), lambda j, i: (0, j))
    return pl.pallas_call(
        body, name=name, grid=(4, t // CHUNK),
        in_specs=[q_spec, k_spec, v_spec, blk, pl.BlockSpec((2, CHUNK, LANES), lambda j, i: (j, i, 0))],
        out_specs=[blk, col, col],
        out_shape=[jax.ShapeDtypeStruct((t, 4 * LANES), F32)] * 3,
        compiler_params=_params(("parallel", "arbitrary")),
    )(proj, proj, proj, dy, lt)


SB_BQ = 256
SB_BK = 256
assert SB_BQ == SB_BK


def _tri_ones(kind):
    j = lax.broadcasted_iota(jnp.int32, (SB_BK, SB_BK + LANES), 0)
    s = lax.broadcasted_iota(jnp.int32, (SB_BK, SB_BK + LANES), 1)
    tri = {"gt": j > s, "le": j <= s, "lt": j < s}[kind]
    return (tri | (s >= SB_BK)).astype(BF16)


def _sb_common(q_ref):
    lane = lax.broadcasted_iota(jnp.int32, (SB_BQ, LANES), 1)
    q = q_ref[...] * (HEAD ** -0.5)
    q2 = jnp.concatenate([jnp.where(lane < HEAD, q, 0.0), jnp.where(lane >= HEAD, q, 0.0)], axis=0).astype(BF16)
    diff = (lax.broadcasted_iota(jnp.int32, (2 * SB_BQ, SB_BK), 1)
            - (lax.broadcasted_iota(jnp.int32, (2 * SB_BQ, SB_BK), 0) & (SB_BQ - 1)))
    return lane, q2, diff


def _rep(x):
    return jnp.concatenate([x] * (SB_BK // LANES), axis=1)


def _sb2_specs(t):
    q = pl.BlockSpec((SB_BQ, LANES), lambda j, i: (i, j))
    k = pl.BlockSpec((t, LANES), lambda j, i: (0, 4 + j))
    v = pl.BlockSpec((t, LANES), lambda j, i: (0, 8 + j))
    return q, k, v


def _sb2_fwd(proj, *, name):
    t = proj.shape[0]

    def body(q_ref, k_ref, v_ref, y_ref, lt_ref):
        i = pl.program_id(1)
        lane, q2, diff = _sb_common(q_ref)
        m_f = _tri_ones("gt")

        def step(kb, carry, diagonal):
            c, acc = carry
            off = pl.multiple_of(kb * SB_BK, SB_BK)
            kblk = k_ref[pl.ds(off, SB_BK), :].astype(BF16)
            vblk = v_ref[pl.ds(off, SB_BK), :].astype(BF16)
            z = lax.dot_general(q2, kblk, _NT, preferred_element_type=F32)
            lb = _log_sigmoid(z)
            lk = jnp.where(diff < 0, lb - z, 0.0) if diagonal else lb - z
            w2 = _dot2(lk, m_f)
            att = jnp.exp(lb + _rep(c) + w2[:, :SB_BK])
            if diagonal:
                att = jnp.where(diff < 0, att, 0.0)
            acc = acc + lax.dot_general(att.astype(BF16), vblk, _NN, preferred_element_type=F32)
            return c + w2[:, SB_BK:], acc

        zero = jnp.zeros((2 * SB_BQ, LANES), F32)
        c, acc = lax.fori_loop(0, i, lambda it, carry: step(i - 1 - it, carry, False), step(i, (zero, zero), True))
        y_ref[...] = jnp.where(lane < HEAD, acc[:SB_BQ], acc[SB_BQ:])
        lt_ref[0] = c[:SB_BQ]
        lt_ref[1] = c[SB_BQ:]

    return pl.pallas_call(
        body, name=name, grid=(4, t // SB_BQ),
        in_specs=list(_sb2_specs(t)),
        out_specs=[pl.BlockSpec((SB_BQ, LANES), lambda j, i: (i, j)),
                   pl.BlockSpec((2, SB_BQ, LANES), lambda j, i: (j, i, 0))],
        out_shape=[jax.ShapeDtypeStruct((t, 4 * LANES), F32), jax.ShapeDtypeStruct((8, t, LANES), F32)],
        compiler_params=_params(("parallel", "arbitrary")),
    )(proj, proj, proj)


def _sb2_bwd(proj, dy, lt, *, name):
    t = proj.shape[0]

    def body(q_ref, k_ref, v_ref, dy_ref, lt_ref, dq_ref, dk_ref, dv_ref):
        i = pl.program_id(1)

        @pl.when(i == 0)
        def _():
            dk_ref[...] = jnp.zeros_like(dk_ref)
            dv_ref[...] = jnp.zeros_like(dv_ref)

        lane, q2, diff = _sb_common(q_ref)
        m_le, m_lt = _tri_ones("le"), _tri_ones("lt")
        dy_blk = dy_ref[...]
        do2 = jnp.concatenate([jnp.where(lane < HEAD, dy_blk, 0.0), jnp.where(lane >= HEAD, dy_blk, 0.0)],
                              axis=0).astype(BF16)
        lt2 = jnp.concatenate([lt_ref[0], lt_ref[1]], axis=0)

        def step(kb, carry, diagonal):
            cp, cg, dq = carry
            off = pl.multiple_of(kb * SB_BK, SB_BK)
            kblk = k_ref[pl.ds(off, SB_BK), :].astype(BF16)
            vblk = v_ref[pl.ds(off, SB_BK), :].astype(BF16)
            z = lax.dot_general(q2, kblk, _NT, preferred_element_type=F32)
            lb = _log_sigmoid(z)
            lk = jnp.where(diff < 0, lb - z, 0.0) if diagonal else lb - z
            w2 = _dot2(lk, m_le)
            att = jnp.exp(lb + _rep(lt2 - cp) - w2[:, :SB_BK])
            if diagonal:
                att = jnp.where(diff < 0, att, 0.0)
            d_e = lax.dot_general(do2, vblk, _NT, preferred_element_type=F32) * att
            g2 = _dot2(d_e, m_lt)
            sig = jnp.exp(lb)
            dz = d_e * (1.0 - sig) - (_rep(cg) + g2[:, :SB_BK]) * sig
            dz = (jnp.where(diff < 0, dz, 0.0) if diagonal else dz).astype(BF16)
            dq = dq + lax.dot_general(dz, kblk, _NN, preferred_element_type=F32)
            dk_ref[pl.ds(off, SB_BK), :] += lax.dot_general(dz, q2, _TN, preferred_element_type=F32)
            dv_ref[pl.ds(off, SB_BK), :] += lax.dot_general(att.astype(BF16), do2, _TN, preferred_element_type=F32)
            return cp + w2[:, SB_BK:], cg + g2[:, SB_BK:], dq

        zero = jnp.zeros((2 * SB_BQ, LANES), F32)
        before = lax.fori_loop(0, i, lambda kb, carry: step(kb, carry, False), (zero, zero, zero))
        _, _, dq = step(i, before, True)
        dq_ref[...] = jnp.where(lane < HEAD, dq[:SB_BQ], dq[SB_BQ:]) * (HEAD ** -0.5)

    q_spec, k_spec, v_spec = _sb2_specs(t)
    blk = pl.BlockSpec((SB_BQ, LANES), lambda j, i: (i, j))
    col = pl.BlockSpec((t, LANES), lambda j, i: (0, j))
    return pl.pallas_call(
        body, name=name, grid=(4, t // SB_BQ),
        in_specs=[q_spec, k_spec, v_spec, blk, pl.BlockSpec((2, SB_BQ, LANES), lambda j, i: (j, i, 0))],
        out_specs=[blk, col, col],
        out_shape=[jax.ShapeDtypeStruct((t, 4 * LANES), F32)] * 3,
        compiler_params=_params(("parallel", "arbitrary")),
    )(proj, proj, proj, dy, lt)


SSD_HEADS = 16
SSD_PAIRS = 8


def _split3(x):
    a = x.astype(BF16)
    r = x - a.astype(F32)
    b = r.astype(BF16)
    return a, b, (r - b.astype(F32)).astype(BF16)


def _dot3(x, m, dn=_NN):
    return sum(lax.dot_general(p, m, dn, preferred_element_type=F32) for p in _split3(x))


def _mdot3(m, x):
    return sum(lax.dot_general(m, p, _NN, preferred_element_type=F32) for p in _split3(x))


def _ssd_common(dtr, dtb, alog, acsx_s, acst_s):
    lane = lax.broadcasted_iota(jnp.int32, (CHUNK, LANES), 1)
    lane1 = lax.broadcasted_iota(jnp.int32, (1, LANES), 1)
    arow = jnp.where(lane1 < SSD_HEADS, -jnp.exp(alog), 0.0)
    dt = jnp.where(lane < SSD_HEADS, _softplus(dtr + dtb), 0.0)
    da = dt * arow
    r = lax.broadcasted_iota(jnp.int32, (CHUNK, CHUNK), 0)
    c = lax.broadcasted_iota(jnp.int32, (CHUNK, CHUNK), 1)
    tril = (r >= c).astype(BF16)
    triu = (r <= c).astype(BF16)
    acs = _mdot3(tril, da)
    acst_s[...] = _dot3(da, triu, _TN)
    eh = lax.broadcasted_iota(jnp.int32, (LANES, 8 * LANES), 0)
    e = (eh == lax.broadcasted_iota(jnp.int32, (LANES, 8 * LANES), 1) // HEAD).astype(BF16)
    eh2 = lax.broadcasted_iota(jnp.int32, (LANES, 16 * LANES), 0)
    e2 = (eh2 == lax.broadcasted_iota(jnp.int32, (LANES, 16 * LANES), 1) // LANES).astype(BF16)
    acsx_s[...] = _dot3(acs, e)
    return dt, arow, _dot3(dt, e), _dot3(acs, e2), e, tril, triu


def _ssd_fwd(xc, proj, dtb, alog, dsk, *, name):
    t = xc.shape[0]
    nc = t // CHUNK

    def body(x_ref, b_ref, c_ref, dtr_ref, dtb_ref, alog_ref, dsk_ref, y_ref, hin_ref, acsx_s, acst_s, h_s):
        @pl.when(pl.program_id(0) == 0)
        def _():
            h_s[...] = jnp.zeros_like(h_s)

        dt, arow, dt_x, acs_b, e, tril, _ = _ssd_common(dtr_ref[...], dtb_ref[...], alog_ref[...], acsx_s, acst_s)
        dsk_x = _dot3(jnp.broadcast_to(dsk_ref[...], (CHUNK, LANES)), e)
        lane = lax.broadcasted_iota(jnp.int32, (CHUNK, LANES), 1)
        causal = (lax.broadcasted_iota(jnp.int32, (CHUNK, CHUNK), 0)
                  >= lax.broadcasted_iota(jnp.int32, (CHUNK, CHUNK), 1))
        for j in range(SSD_PAIRS):
            g = j // 4
            sl = slice(j * LANES, (j + 1) * LANES)
            if j % 4 == 0:
                bg = jnp.where(lane // HEAD == g, b_ref[...], 0.0)
                cg = jnp.where(lane // HEAD == g, c_ref[...], 0.0)
                cb = _dot_nt(cg, bg)
            x = x_ref[:, sl]
            a = acsx_s[:, sl]
            at = acsx_s[CHUNK - 1:CHUNK, sl]
            xdt = x * dt_x[:, sl]
            hin = h_s[j]
            hin_ref[0, j] = hin
            y = jnp.exp(a) * _dot_nn(cg, hin) + x * dsk_x[:, sl]
            h_s[j] = jnp.exp(at) * hin + _dot_tn(bg, xdt * jnp.exp(at - a))
            yd = []
            for hh in (0, 1):
                h = 2 * j + hh
                dec = jnp.exp(jnp.minimum(acs_b[:, h * LANES:(h + 1) * LANES] - acst_s[pl.ds(h, 1), :], 0.0))
                yd.append(_dot_nn(jnp.where(causal, cb * dec, 0.0), xdt))
            y_ref[:, sl] = y + jnp.where(lane < HEAD, yd[0], yd[1])

    one = pl.BlockSpec((1, LANES), lambda i: (0, 0))
    return pl.pallas_call(
        body, name=name, grid=(nc,),
        in_specs=[pl.BlockSpec((CHUNK, 8 * LANES), lambda i: (i, 0)),
                  pl.BlockSpec((CHUNK, LANES), lambda i: (i, 8)),
                  pl.BlockSpec((CHUNK, LANES), lambda i: (i, 9)),
                  pl.BlockSpec((CHUNK, LANES), lambda i: (i, C_DT // LANES)), one, one, one],
        out_specs=[pl.BlockSpec((CHUNK, 8 * LANES), lambda i: (i, 0)),
                   pl.BlockSpec((1, SSD_PAIRS, LANES, LANES), lambda i: (i, 0, 0, 0))],
        out_shape=[jax.ShapeDtypeStruct((t, 8 * LANES), F32),
                   jax.ShapeDtypeStruct((nc, SSD_PAIRS, LANES, LANES), F32)],
        scratch_shapes=[pltpu.VMEM((CHUNK, 8 * LANES), F32), pltpu.VMEM((LANES, CHUNK), F32),
                        pltpu.VMEM((SSD_PAIRS, LANES, LANES), F32)],
        compiler_params=_params(("arbitrary",)),
    )(xc, xc, xc, proj, dtb, alog, dsk)


def _ssd_bwd(xc, proj, dtb, alog, dsk, hin_all, dy, *, name):
    t = xc.shape[0]
    nc = t // CHUNK

    def body(x_ref, b_ref, c_ref, dtr_ref, dtb_ref, alog_ref, dsk_ref, hin_ref, dy_ref,
             dxc_ref, ddtr_ref, ddtb_ref, dalog_ref, ddsk_ref, acsx_s, acst_s, dh_s, dax_s, ddx_s):
        @pl.when(pl.program_id(0) == 0)
        def _():
            dh_s[...] = jnp.zeros_like(dh_s)
            ddtb_ref[...] = jnp.zeros_like(ddtb_ref)
            dalog_ref[...] = jnp.zeros_like(dalog_ref)
            ddsk_ref[...] = jnp.zeros_like(ddsk_ref)

        dtr = dtr_ref[...]
        dtb = dtb_ref[...]
        dt, arow, dt_x, acs_b, e, tril, triu = _ssd_common(dtr, dtb, alog_ref[...], acsx_s, acst_s)
        dsk_x = _dot3(jnp.broadcast_to(dsk_ref[...], (CHUNK, LANES)), e)
        lane = lax.broadcasted_iota(jnp.int32, (CHUNK, LANES), 1)
        rowi = lax.broadcasted_iota(jnp.int32, (CHUNK, LANES), 0)
        causal = (lax.broadcasted_iota(jnp.int32, (CHUNK, CHUNK), 0)
                  >= lax.broadcasted_iota(jnp.int32, (CHUNK, CHUNK), 1))
        dacs = jnp.zeros((CHUNK, LANES), F32)
        d_b = jnp.zeros((CHUNK, LANES), F32)
        d_c = jnp.zeros((CHUNK, LANES), F32)
        for j in range(SSD_PAIRS):
            g = j // 4
            sl = slice(j * LANES, (j + 1) * LANES)
            if j % 4 == 0:
                bg = jnp.where(lane // HEAD == g, b_ref[...], 0.0)
                cg = jnp.where(lane // HEAD == g, c_ref[...], 0.0)
                cb = _dot_nt(cg, bg)
                dcb = jnp.zeros((CHUNK, CHUNK), F32)
            x = x_ref[:, sl]
            d = dt_x[:, sl]
            a = acsx_s[:, sl]
            at = acsx_s[CHUNK - 1:CHUNK, sl]
            xdt = x * d
            hin = hin_ref[0, j]
            dhout = dh_s[j]
            dyp = dy_ref[:, sl]
            ea, eat, ed = jnp.exp(a), jnp.exp(at), jnp.exp(at - a)
            da_l = dyp * ea * _dot_nn(cg, hin)
            dm = dyp * ea
            d_c = d_c + _dot_nt(dm, hin)
            dh_s[j] = _dot_tn(cg, dm) + eat * dhout
            dat = jnp.sum(dhout * hin * eat, axis=0, keepdims=True)
            d_b = d_b + _dot_nt(xdt * ed, dhout)
            dw = _dot_nn(bg, dhout)
            dxdt = dw * ed
            ded = dw * xdt * ed
            dat = dat + jnp.sum(ded, axis=0, keepdims=True)
            da_l = da_l - ded
            for hh in (0, 1):
                h = 2 * j + hh
                dec = jnp.exp(jnp.minimum(acs_b[:, h * LANES:(h + 1) * LANES] - acst_s[pl.ds(h, 1), :], 0.0))
                gm = jnp.where(causal, cb * dec, 0.0)
                dyh = jnp.where(lane // HEAD == hh, dyp, 0.0)
                dg = _dot_nt(dyh, xdt)
                dxdt = dxdt + _dot_tn(gm, dyh)
                dcb = dcb + jnp.where(causal, dg * dec, 0.0)
                th = dg * gm
                oh = (lane == h).astype(BF16)
                dacs = dacs + _dot2(th, oh) - _dot2_tn(th, oh)
            if j % 4 == 3:
                d_c = d_c + _dot_nn(dcb, bg)
                d_b = d_b + _dot_tn(dcb, cg)
            dxc_ref[:, sl] = dyp * dsk_x[:, sl] + dxdt * d
            ddx_s[:, sl] = dxdt * x
            dax_s[:, sl] = da_l + jnp.where(rowi == CHUNK - 1, dat, 0.0)
            dskp = jnp.sum(dyp * x, axis=0, keepdims=True)
            ddsk_ref[...] += _dot2(jnp.broadcast_to(dskp, (8, LANES)), e[:, sl], _NT)
        dxc_ref[:, 8 * LANES:9 * LANES] = d_b
        dxc_ref[:, 9 * LANES:10 * LANES] = d_c
        dacs = dacs + _dot2(dax_s[...], e, _NT)
        ddt = _dot2(ddx_s[...], e, _NT)
        dda = _mdot3(triu, dacs)
        ddt = ddt + dda * arow
        dalog_ref[...] += jnp.sum(dda * dt, axis=0, keepdims=True) * arow
        ddtr = jnp.where(lane < SSD_HEADS, ddt * _sigmoid(dtr + dtb), 0.0)
        ddtr_ref[...] = ddtr
        ddtb_ref[...] += jnp.sum(ddtr, axis=0, keepdims=True)

    one = pl.BlockSpec((1, LANES), lambda i: (0, 0))
    rev = lambda c: (lambda i: (nc - 1 - i, c))
    return pl.pallas_call(
        body, name=name, grid=(nc,),
        in_specs=[pl.BlockSpec((CHUNK, 8 * LANES), rev(0)), pl.BlockSpec((CHUNK, LANES), rev(8)),
                  pl.BlockSpec((CHUNK, LANES), rev(9)), pl.BlockSpec((CHUNK, LANES), rev(C_DT // LANES)),
                  one, one, one,
                  pl.BlockSpec((1, SSD_PAIRS, LANES, LANES), lambda i: (nc - 1 - i, 0, 0, 0)),
                  pl.BlockSpec((CHUNK, 8 * LANES), rev(0))],
        out_specs=[pl.BlockSpec((CHUNK, XBC_COLS), rev(0)), pl.BlockSpec((CHUNK, LANES), rev(0)), one, one,
                   pl.BlockSpec((8, LANES), lambda i: (0, 0))],
        out_shape=[jax.ShapeDtypeStruct((t, XBC_COLS), F32), jax.ShapeDtypeStruct((t, LANES), F32)]
        + [jax.ShapeDtypeStruct((1, LANES), F32)] * 2 + [jax.ShapeDtypeStruct((8, LANES), F32)],
        scratch_shapes=[pltpu.VMEM((CHUNK, 8 * LANES), F32), pltpu.VMEM((LANES, CHUNK), F32),
                        pltpu.VMEM((SSD_PAIRS, LANES, LANES), F32),
                        pltpu.VMEM((CHUNK, 8 * LANES), F32), pltpu.VMEM((CHUNK, 8 * LANES), F32)],
        compiler_params=_params(("arbitrary",)),
    )(xc, xc, xc, proj, dtb, alog, dsk, hin_all, dy)


RW_LW = 128
RW_PAIRS = 4 * LANES // RW_LW
RW_BT = 16
RW_DECAY_ROW = 1
RW_BWD_PAIRS = 4


def _rw_consts():
    seg = _seg_matrix(RW_LW)
    ti = (lax.broadcasted_iota(jnp.int32, (HEAD, RW_LW), 0)
          == lax.broadcasted_iota(jnp.int32, (HEAD, RW_LW), 1) % HEAD)
    return seg, ti


def _col_tiles(rows, ti, seg):
    tib = ti.astype(BF16)
    n = len(rows)
    hi = [r.astype(BF16) for r in rows]
    w_lo = (rows[RW_DECAY_ROW] - hi[RW_DECAY_ROW].astype(F32)).astype(BF16)
    out = lax.dot_general(jnp.concatenate([tib * h for h in hi + [w_lo]], axis=0), seg, _NN, preferred_element_type=F32)
    tiles = [out[i * HEAD:(i + 1) * HEAD] for i in range(n)]
    tiles[RW_DECAY_ROW] = tiles[RW_DECAY_ROW] + out[n * HEAD:(n + 1) * HEAD]
    return tiles


def _col_tiles2(rows, ti, seg):
    tib = ti.astype(BF16)
    hi = [r.astype(BF16) for r in rows]
    lo = [(r - h.astype(F32)).astype(BF16) for r, h in zip(rows, hi)]
    out = (lax.dot_general(jnp.concatenate([tib * h for h in hi], axis=0), seg, _NN, preferred_element_type=F32)
           + lax.dot_general(jnp.concatenate([tib * l for l in lo], axis=0), seg, _NN, preferred_element_type=F32))
    return [out[i * HEAD:(i + 1) * HEAD] for i in range(len(rows))]


def _head_lane_sums(tiles, ti, seg):
    out = _dot2(jnp.concatenate(tiles, axis=0), seg)
    return [jnp.sum(jnp.where(ti, out[i * HEAD:(i + 1) * HEAD], 0.0), axis=0, keepdims=True) for i in range(len(tiles))]


def _rw_scan_fwd(mixed, w, k, n, b, *, name):
    t = w.shape[0]

    def body(r_ref, v_ref, w_ref, k_ref, n_ref, b_ref, y_ref, st_ref, s_s):
        @pl.when(pl.program_id(0) == 0)
        def _():
            s_s[...] = jnp.zeros_like(s_s)

        seg, ti = _rw_consts()

        def step(tt, state):
            row = pl.ds(tt, 1)
            new = []
            for p in range(RW_PAIRS):
                sl = pl.ds(p * RW_LW, RW_LW)
                s = state[p]
                ncol, wcol, bcol, kcol, rcol = _col_tiles(
                    [x[row, sl] for x in (n_ref, w_ref, b_ref, k_ref, r_ref)], ti, seg)
                sa = jnp.sum(s * ncol, axis=0, keepdims=True)
                s = s * wcol + bcol * sa + kcol * v_ref[row, sl]
                y_ref[row, sl] = jnp.sum(s * rcol, axis=0, keepdims=True)
                st_ref[tt, p] = s
                new.append(s)
            return tuple(new)

        out = tuple(s_s[p] for p in range(RW_PAIRS))
        for tt in range(RW_BT):
            out = step(tt, out)
        for p in range(RW_PAIRS):
            s_s[p] = out[p]

    blk = lambda c: pl.BlockSpec((RW_BT, 4 * LANES), functools.partial(lambda i, c: (i, c), c=c))
    return pl.pallas_call(
        body, name=name, grid=(t // RW_BT,),
        in_specs=[blk(0), blk(2), blk(0), blk(0), blk(0), blk(0)],
        out_specs=[blk(0), pl.BlockSpec((RW_BT, RW_PAIRS, HEAD, RW_LW), lambda i: (i, 0, 0, 0))],
        out_shape=[jax.ShapeDtypeStruct((t, 4 * LANES), F32),
                   jax.ShapeDtypeStruct((t, RW_PAIRS, HEAD, RW_LW), F32)],
        scratch_shapes=[pltpu.VMEM((RW_PAIRS, HEAD, RW_LW), F32)],
        compiler_params=_params(("arbitrary",)),
    )(mixed, mixed, w, k, n, b)


def _rw_scan_bwd(mixed, w, k, n, b, states, dy, dr0, dk0, dv0, *, name):
    t = w.shape[0]
    nb = t // RW_BT
    ppc = RW_BWD_PAIRS
    ng = RW_PAIRS // ppc

    def body(r_ref, v_ref, w_ref, k_ref, n_ref, b_ref, st_ref, prev_ref, dy_ref, dr0_ref, dk0_ref, dv0_ref,
             dr_ref, dw_ref, dk_ref, dv_ref, dn_ref, db_ref, ds_s):
        @pl.when(pl.program_id(1) == 0)
        def _():
            ds_s[...] = jnp.zeros_like(ds_s)

        seg, ti = _rw_consts()
        has_prev = (pl.program_id(1) < nb - 1).astype(F32)

        def step(it, carry):
            tt = RW_BT - 1 - it
            row = pl.ds(tt, 1)
            prev_t = max(tt - 1, 0)
            new_ds, new_s = [], []
            for p in range(ppc):
                sl = pl.ds(p * RW_LW, RW_LW)
                ds, s_t = carry[p], carry[ppc + p]
                s_p = st_ref[prev_t, p] if tt > 0 else prev_ref[0, p] * has_prev
                ncol, wcol, bcol, kcol, rcol = _col_tiles2(
                    [x[row, sl] for x in (n_ref, w_ref, b_ref, k_ref, r_ref)], ti, seg)
                vv, dyy = v_ref[row, sl], dy_ref[row, sl]
                sa = jnp.sum(s_p * ncol, axis=0, keepdims=True)
                ds = ds + rcol * dyy
                dsa = jnp.sum(ds * bcol, axis=0, keepdims=True)
                dv_ref[row, sl] = jnp.sum(ds * kcol, axis=0, keepdims=True) + dv0_ref[row, sl]
                dr, dw, db, dk, dn = _head_lane_sums([s_t * dyy, ds * s_p, ds * sa, ds * vv, s_p * dsa], ti, seg)
                dr_ref[row, sl] = dr + dr0_ref[row, sl]
                dw_ref[row, sl] = dw
                db_ref[row, sl] = db
                dk_ref[row, sl] = dk + dk0_ref[row, sl]
                dn_ref[row, sl] = dn
                new_ds.append(ds * wcol + ncol * dsa)
                new_s.append(s_p)
            return tuple(new_ds) + tuple(new_s)

        init = tuple(ds_s[p] for p in range(ppc)) + tuple(st_ref[RW_BT - 1, p] for p in range(ppc))
        out = init
        for it in range(RW_BT):
            out = step(it, out)
        for p in range(ppc):
            ds_s[p] = out[p]

    blk = lambda c: pl.BlockSpec((RW_BT, ppc * RW_LW), functools.partial(lambda g, i, c: (nb - 1 - i, c * ng + g), c=c))
    st_spec = pl.BlockSpec((RW_BT, ppc, HEAD, RW_LW), lambda g, i: (nb - 1 - i, g, 0, 0))
    prev_spec = pl.BlockSpec((1, ppc, HEAD, RW_LW), lambda g, i: (jnp.maximum((nb - 1 - i) * RW_BT - 1, 0), g, 0, 0))
    return pl.pallas_call(
        body, name=name, grid=(ng, nb),
        in_specs=[blk(0), blk(2), blk(0), blk(0), blk(0), blk(0), st_spec, prev_spec, blk(0), blk(0), blk(0), blk(0)],
        out_specs=[blk(0)] * 6,
        out_shape=[jax.ShapeDtypeStruct((t, 4 * LANES), F32)] * 6,
        scratch_shapes=[pltpu.VMEM((ppc, HEAD, RW_LW), F32)],
        compiler_params=_params(("parallel", "arbitrary")),
    )(mixed, mixed, w, k, n, b, states, states, dy, dr0, dk0, dv0)


RW_C = 64


def _p3(a, b, dn):
    ah, al = _split2(a)
    bh, bl = _split2(b)
    d = lambda x, y: lax.dot_general(x, y, dn, preferred_element_type=F32)
    return d(ah, bh) + d(ah, bl) + d(al, bh)


_BNN = (((2,), (1,)), ((0,), (0,)))
_BNT = (((2,), (2,)), ((0,), (0,)))
_BTN = (((1,), (1,)), ((0,), (0,)))


@jax.custom_vjp
def _pnn(a, b):
    return _p3(a, b, _BNN)


@jax.custom_vjp
def _pnt(a, b):
    return _p3(a, b, _BNT)


@jax.custom_vjp
def _ptn(a, b):
    return _p3(a, b, _BTN)


_pnn.defvjp(lambda a, b: (_p3(a, b, _BNN), (a, b)), lambda res, g: (_p3(g, res[1], _BNT), _p3(res[0], g, _BTN)))
_pnt.defvjp(lambda a, b: (_p3(a, b, _BNT), (a, b)), lambda res, g: (_p3(g, res[1], _BNN), _p3(g, res[0], _BTN)))
_ptn.defvjp(lambda a, b: (_p3(a, b, _BTN), (a, b)), lambda res, g: (_p3(res[1], g, _BNT), _p3(res[0], g, _BNN)))


def _rw_chunk_consts():
    c2 = 2 * RW_C
    row = lax.broadcasted_iota(jnp.int32, (c2, c2), 0)
    col = lax.broadcasted_iota(jnp.int32, (c2, c2), 1)
    same = (row // RW_C) == (col // RW_C)
    strict = (same & (row > col)).astype(F32)
    incl = (same & (row >= col)).astype(F32)
    eye = (row == col).astype(F32)
    tr = lax.broadcasted_iota(jnp.int32, (RW_C, RW_C), 0)
    tc = lax.broadcasted_iota(jnp.int32, (RW_C, RW_C), 1)
    tril = (tr >= tc).astype(F32)
    lane = lax.broadcasted_iota(jnp.int32, (1, LANES), 1)
    hm = [(lane // HEAD == h).astype(F32) for h in (0, 1)]
    return strict, incl, eye, tril, hm


def _rw_chunk(r, lw, k, v, n, b, s2, consts):
    strict, incl, eye, tril, hm = consts
    two = lambda x: jnp.concatenate([x * hm[0], x * hm[1]], axis=1)
    cum = _pnn(jnp.broadcast_to(tril, (4, RW_C, RW_C)), lw)
    grow, shrink = jnp.exp(-cum), jnp.exp(cum)
    n2, r2 = two(n * jnp.exp(cum - lw)), two(r * shrink)
    b2, k2, v2 = two(b * grow), two(k * grow), two(v)
    p = _pnt(n2, b2) * strict
    x2 = _pnt(n2, s2) + _pnn(_pnt(n2, k2) * strict, v2)
    t_inv, a = eye + p, p
    for _ in range(RW_C.bit_length() - 2):
        a = _pnn(a, a)
        t_inv = t_inv + _pnn(t_inv, a)
    u2 = _pnn(t_inv, x2)
    y2 = _pnt(r2, s2) + _pnn(_pnt(r2, b2) * incl, u2) + _pnn(_pnt(r2, k2) * incl, v2)
    s2_new = (s2 + _ptn(u2, b2) + _ptn(v2, k2)) * jnp.exp(jnp.sum(lw, axis=1, keepdims=True))
    return jnp.sum(y2.reshape(4, 2, RW_C, LANES), axis=1), s2_new


def _pairs(ref):
    return jnp.stack([ref[:, p * LANES:(p + 1) * LANES] for p in range(4)])


def _rw_chunk_fwd(mixed, lw, k, n, b, *, name, side=None):
    t = lw.shape[0]
    nc = t // RW_C

    def body(r_ref, v_ref, lw_ref, k_ref, n_ref, b_ref, y_ref, sin_ref, s_s):
        @pl.when(pl.program_id(0) == 0)
        def _():
            s_s[...] = jnp.zeros_like(s_s)

        s2 = s_s[...]
        sin_ref[0] = s2
        y, s2 = _rw_chunk(*[_pairs(x) for x in (r_ref, lw_ref, k_ref, v_ref, n_ref, b_ref)], s2, _rw_chunk_consts())
        for p in range(4):
            y_ref[:, p * LANES:(p + 1) * LANES] = y[p]
        s_s[...] = s2

    blk = lambda c: pl.BlockSpec((RW_C, 4 * LANES), functools.partial(lambda i, c: (i, c), c=c))
    return _call_with_side(
        body, side, name=name, grid=(nc,), semantics=("arbitrary",),
        in_specs=[blk(0), blk(2), blk(0), blk(0), blk(0), blk(0)],
        out_specs=[blk(0), pl.BlockSpec((1, 4, LANES, LANES), lambda i: (i, 0, 0, 0))],
        out_shape=[jax.ShapeDtypeStruct((t, 4 * LANES), F32), jax.ShapeDtypeStruct((nc, 4, LANES, LANES), F32)],
        scratch_shapes=[pltpu.VMEM((4, LANES, LANES), F32)],
        operands=(mixed, mixed, lw, k, n, b))


def _call_with_side(body, side, *, name, grid, semantics, in_specs, out_specs, out_shape, scratch_shapes, operands):
    if side is None:
        return pl.pallas_call(body, name=name, grid=grid, in_specs=in_specs, out_specs=out_specs, out_shape=out_shape,
                              scratch_shapes=scratch_shapes, compiler_params=_params(semantics))(*operands)
    srcs, per_dest = side
    ns, ni, no, nscr = len(srcs), len(in_specs), len(out_specs), len(scratch_shapes)

    def full_body(*refs):
        ins, side_in = refs[:ni], refs[ni:ni + ns]
        outs, side_out = refs[ni + ns:ni + ns + no], refs[ni + ns + no:ni + 2 * ns + no]
        scratch, sems = refs[ni + 2 * ns + no:ni + 2 * ns + no + nscr], refs[ni + 2 * ns + no + nscr:]

        ids = [pl.program_id(a) for a in range(len(grid))]
        first = functools.reduce(jnp.logical_and, [i == 0 for i in ids])
        last = functools.reduce(jnp.logical_and, [i == n - 1 for i, n in zip(ids, grid)])

        @pl.when(first)
        def _():
            _exchange(side_in, side_out, sems, per_dest, start=True, wait=False)

        body(*ins, *outs, *scratch)

        @pl.when(last)
        def _():
            _exchange(side_in, side_out, sems, per_dest, start=False, wait=True)

    res = pl.pallas_call(
        full_body, name=name, grid=grid, in_specs=list(in_specs) + [_ANY] * ns,
        out_specs=list(out_specs) + [_ANY] * ns, out_shape=list(out_shape) + _exchange_out_shapes(srcs),
        scratch_shapes=list(scratch_shapes) + _exchange_sems(ns), compiler_params=_params(("arbitrary",) * len(grid)),
    )(*operands, *srcs)
    return list(res[:no]) + [list(res[no:])]


def _rw_chunk_bwd(mixed, lw, k, n, b, s_in, dy, dr0, dk0, dv0, *, name, side=None):
    t = lw.shape[0]
    nc = t // RW_C

    def body(r_ref, v_ref, lw_ref, k_ref, n_ref, b_ref, sin_ref, dy_ref, dr0_ref, dk0_ref, dv0_ref,
             dr_ref, dlw_ref, dk_ref, dv_ref, dn_ref, db_ref, ds_s):
        @pl.when(pl.program_id(0) == 0)
        def _():
            ds_s[...] = jnp.zeros_like(ds_s)

        consts = _rw_chunk_consts()
        args = [_pairs(x) for x in (r_ref, lw_ref, k_ref, v_ref, n_ref, b_ref)] + [sin_ref[0]]
        _, vjp = jax.vjp(lambda *a: _rw_chunk(*a, consts), *args)
        dr, dlw, dk, dv, dn, db, ds = vjp((_pairs(dy_ref), ds_s[...]))
        for p in range(4):
            sl = slice(p * LANES, (p + 1) * LANES)
            dr_ref[:, sl] = dr[p] + dr0_ref[:, sl]
            dlw_ref[:, sl] = dlw[p]
            dk_ref[:, sl] = dk[p] + dk0_ref[:, sl]
            dv_ref[:, sl] = dv[p] + dv0_ref[:, sl]
            dn_ref[:, sl] = dn[p]
            db_ref[:, sl] = db[p]
        ds_s[...] = ds

    blk = lambda c: pl.BlockSpec((RW_C, 4 * LANES), functools.partial(lambda i, c: (nc - 1 - i, c), c=c))
    return _call_with_side(
        body, side, name=name, grid=(nc,), semantics=("arbitrary",),
        in_specs=[blk(0), blk(2), blk(0), blk(0), blk(0), blk(0),
                  pl.BlockSpec((1, 4, LANES, LANES), lambda i: (nc - 1 - i, 0, 0, 0)), blk(0), blk(0), blk(0), blk(0)],
        out_specs=[blk(0)] * 6,
        out_shape=[jax.ShapeDtypeStruct((t, 4 * LANES), F32)] * 6,
        scratch_shapes=[pltpu.VMEM((4, LANES, LANES), F32)],
        operands=(mixed, mixed, lw, k, n, b, s_in, dy, dr0, dk0, dv0))


def _f_rms_res(x, g):
    return _f_rms(x, g)[0], x


def _final(x, g, target, *, bt, name):
    t, d = x.shape

    def body(x_ref, g_ref, t_ref, dx_ref, loss_ref, dg_ref):
        tgt = t_ref[...]

        def f(xv, gv):
            err = _f_rms(xv, gv)[0] - tgt
            return 0.5 * jnp.mean(err * err, axis=-1, keepdims=True)

        row_loss, vjp = jax.vjp(f, x_ref[...], g_ref[...])
        dx, dg = vjp(jnp.ones_like(row_loss))
        dx_ref[...] = dx

        @pl.when(pl.program_id(0) == 0)
        def _():
            loss_ref[...] = jnp.zeros_like(loss_ref)
            dg_ref[...] = jnp.zeros_like(dg_ref)

        loss_ref[...] += jnp.broadcast_to(jnp.sum(row_loss, axis=0, keepdims=True), (1, LANES))
        dg_ref[...] += dg

    blk = pl.BlockSpec((bt, d), lambda i: (i, 0))
    return pl.pallas_call(
        body, name=name, grid=(t // bt,),
        in_specs=[blk, pl.BlockSpec((1, d), lambda i: (0, 0)), blk],
        out_specs=[blk, pl.BlockSpec((1, LANES), lambda i: (0, 0)), pl.BlockSpec((1, d), lambda i: (0, 0))],
        out_shape=[jax.ShapeDtypeStruct((t, d), F32), jax.ShapeDtypeStruct((1, LANES), F32),
                   jax.ShapeDtypeStruct((1, d), F32)],
        compiler_params=_params(("arbitrary",)),
    )(x, g, target)


ADAMW_BLOCK_BYTES = 1 << 20


def _adamw(w, g, m, v, *, name, block=None):
    shape = w.shape
    if block is not None:
        return _adamw_blocks(w, g, m, v, block, name)
    c = shape[-1]
    shape3 = (1,) * (3 - len(shape)) + shape if len(shape) <= 3 else (-1,) + shape[-2:]
    args = [a.reshape(shape3) for a in (w, g, m, v)]
    lead, r, _ = args[0].shape
    br = r
    if r * c * 4 > ADAMW_BLOCK_BYTES:
        cands = [b for b in range(8, r, 8) if r % b == 0 and b * c * 4 <= ADAMW_BLOCK_BYTES]
        br = max(cands) if cands else r
    outs = _adamw_blocks(*args, (1, br, c), name)
    return tuple(o.reshape(shape) for o in outs)


def _adamw_blocks(w, g, m, v, block, name):
    shape = w.shape
    assert all(s % b == 0 for s, b in zip(shape, block))

    def body(w_ref, g_ref, m_ref, v_ref, d_ref, nm_ref, nv_ref):
        gv = g_ref[...]
        m_new = ADAM_B1 * m_ref[...] + (1.0 - ADAM_B1) * gv
        v_new = ADAM_B2 * v_ref[...] + (1.0 - ADAM_B2) * (gv * gv)
        m_hat = m_new / (1.0 - ADAM_B1 ** ADAM_STEP)
        v_hat = v_new / (1.0 - ADAM_B2 ** ADAM_STEP)
        d_ref[...] = -ADAM_LR * (m_hat / (jnp.sqrt(v_hat) + ADAM_EPS) + ADAM_WD * w_ref[...])
        nm_ref[...] = m_new
        nv_ref[...] = v_new

    blk = pl.BlockSpec(tuple(block), lambda *ids: ids)
    return pl.pallas_call(
        body, name=name, grid=tuple(s // b for s, b in zip(shape, block)), in_specs=[blk] * 4, out_specs=[blk] * 3,
        out_shape=[jax.ShapeDtypeStruct(shape, F32)] * 3,
        compiler_params=_params(("parallel",) * len(shape)),
    )(w, g, m, v)


BT = 256
BC = 128


def _layer_rows(x, proj, s):
    s = {k: s.get(k) for k in ("y_sb_raw", "y_ssd_raw", "mixed", "ys", "k2", "p_sb", "p_ssd", "p_rw")}
    return dict(
        rms=[(x, D_MODEL, 0)],
        sb_gate=[(s["y_sb_raw"], 512, 0), (proj, 512, 3)],
        ssd_norm=[(s["y_ssd_raw"], 1024, 0), (proj, 1024, C_Z // 1024)],
        rw_pre=[(s["mixed"], 512, 1), (s["mixed"], LANES, 16)],
        rw_post=[(s["ys"], 512, 0), (s["mixed"], 512, 0), (s["k2"], 512, 0), (s["mixed"], 512, 2), (s["mixed"], 512, 3)],
        merge=[(s["p_sb"], 1024, 0), (s["p_ssd"], 1024, 0), (s["p_rw"], 1024, 0),
               (proj, 1024, 3), (proj, 1024, 4), (proj, 1024, 5)],
    )


def _layer_fwd(x, p, nm, side=None):
    s = {}
    (s["h"],) = _rowwise(_f_rms, [(x, D_MODEL, 0)], [p["norm_g"]], [D_MODEL], bt=BT, name=nm + "rms")
    proj = s["proj"] = _mm(s["h"], p["w_in"], name=nm + "proj")
    s["y_sb_raw"], s["lt"] = _sb2_fwd(proj, name=nm + "sb")
    s["xc"] = _colwise(_f_conv, proj, C_XBC, XBC_COLS, p["conv"], bc=BC, name=nm + "conv")
    s["y_ssd_raw"], s["hin"] = _ssd_fwd(s["xc"], proj, p["dt_bias"], p["a_log"], p["d_skip"], name=nm + "ssd")
    s["mixed"] = _colwise(_f_rw_mix, proj, C_RW, RW_COLS, [p["rw_mu"]], bc=BC, name=nm + "mix")
    s["w"], s["k2"], s["n"], s["b"] = _rowwise(_f_rw_pre, [(s["mixed"], 512, 1), (s["mixed"], LANES, 16)], p["rw_pre"],
                                               [512] * 4, bt=BT, name=nm + "rwpre")
    s["ys"], s["st"], *exchanged = _rw_chunk_fwd(s["mixed"], s["w"], s["k2"], s["n"], s["b"], name=nm + "scan", side=side)
    rows = _layer_rows(x, proj, s)
    (s["y_sb"],) = _rowwise(_f_sb_gate, rows["sb_gate"], [], [512], bt=BT, name=nm + "sbgate")
    (s["y_ssd"],) = _rowwise(_f_ssd_norm, rows["ssd_norm"], [p["ssd_norm_g"]], [1024], bt=BT, name=nm + "ssdnorm")
    (s["y_rw"],) = _rowwise(_f_rw_post, rows["rw_post"], p["rw_post"], [512], bt=BT, name=nm + "rwpost")
    s["p_sb"] = _mm(s["y_sb"], p["w_out_sb"], name=nm + "osb")
    s["p_ssd"] = _mm(s["y_ssd"], p["w_out_ssd"], name=nm + "ossd")
    s["p_rw"] = _mm(s["y_rw"], p["w_out_rw"], name=nm + "orw")
    (s["merged"],) = _rowwise(_f_merge, _layer_rows(x, proj, s)["merge"], [], [1024], bt=BT, name=nm + "merge")
    return _mm(s["merged"], p["w_o"], add=x, name=nm + "wo"), s, (exchanged[0] if exchanged else None)


def _layer_bwd(x, dx_out, p, s, nm, side=None, side_late=None):
    g = {}
    proj = s["proj"]
    rows = _layer_rows(x, proj, s)
    g["w_o"] = _mm(s["merged"], dx_out, ta=True, name=nm + "g_wo")
    d_merged = _mm(dx_out, p["w_o"], tb=True, name=nm + "d_merged")
    dp_sb, dp_ssd, dp_rw, d_gates = _rowwise_bwd(_f_merge, rows["merge"], [], [(d_merged, 1024, 0)], bt=BT,
                                                 name=nm + "merge_b", groups=[[0], [1], [2], [3, 4, 5]])
    g["w_out_sb"] = _mm(s["y_sb"], dp_sb, ta=True, name=nm + "g_osb")
    g["w_out_ssd"] = _mm(s["y_ssd"], dp_ssd, ta=True, name=nm + "g_ossd")
    g["w_out_rw"] = _mm(s["y_rw"], dp_rw, ta=True, name=nm + "g_orw")
    dy_sb = _mm(dp_sb, p["w_out_sb"], tb=True, name=nm + "d_ysb")
    dy_ssd = _mm(dp_ssd, p["w_out_ssd"], tb=True, name=nm + "d_yssd")
    dy_rw = _mm(dp_rw, p["w_out_rw"], tb=True, name=nm + "d_yrw")
    dy_sb_raw, d_sbgate = _rowwise_bwd(_f_sb_gate, rows["sb_gate"], [], [(dy_sb, 512, 0)], bt=BT, name=nm + "sbgate_b")
    dq, dk, dv = _sb2_bwd(proj, dy_sb_raw, s["lt"], name=nm + "sb_b")
    dy_ssd_raw, dz, g["ssd_norm_g"] = _rowwise_bwd(_f_ssd_norm, rows["ssd_norm"], [p["ssd_norm_g"]],
                                                   [(dy_ssd, 1024, 0)], bt=BT, name=nm + "ssdnorm_b")
    dxc, ddtr, g["dt_bias"], g["a_log"], g["d_skip"] = _ssd_bwd(
        s["xc"], proj, p["dt_bias"], p["a_log"], p["d_skip"], s["hin"], dy_ssd_raw, name=nm + "ssd_b")
    conv_out = _colwise_bwd(_f_conv, proj, C_XBC, XBC_COLS, p["conv"], dxc, bc=BC, name=nm + "conv_b")
    dxbc, g["conv"] = conv_out[0], conv_out[1:]
    dys, dr0, dk0, dv0, d_rwgate, g["rw_ln_g"], g["rw_ln_b"], g["rw_r_k"] = _rowwise_bwd(
        _f_rw_post, rows["rw_post"], p["rw_post"], [(dy_rw, 512, 0)], bt=BT, name=nm + "rwpost_b")
    dr, dw, dk2, dvv, dn, db, *exchanged = _rw_chunk_bwd(s["mixed"], s["w"], s["k2"], s["n"], s["b"], s["st"], dys,
                                                         dr0, dk0, dv0, name=nm + "scan_b",
                                                         side=side(g) if side else None)
    pre_out = _rowwise_bwd(_f_rw_pre, rows["rw_pre"], p["rw_pre"],
                           [(dw, 512, 0), (dk2, 512, 0), (dn, 512, 0), (db, 512, 0)], bt=BT, name=nm + "rwpre_b")
    dkm, dlo, g["rw_pre"] = pre_out[0], pre_out[1], pre_out[2:]
    d_mixed = jnp.concatenate([dr, dkm, dvv, d_rwgate, dlo], axis=1)
    d_slab, g["rw_mu"] = _colwise_bwd(_f_rw_mix, proj, C_RW, RW_COLS, [p["rw_mu"]], d_mixed, bc=BC, name=nm + "mix_b")
    d_proj = jnp.concatenate([dq, dk, dv, d_sbgate, dz, d_gates, d_slab, ddtr, dxbc], axis=1)
    g["w_in"] = _mm(s["h"], d_proj, ta=True, name=nm + "g_win")
    dh = _mm(d_proj, p["w_in"], tb=True, tn=1024, tk=512, name=nm + "d_h", side=side_late(g) if side_late else None)
    dh, late = dh if side_late else (dh, None)
    dx, g["norm_g"] = _rowwise_bwd(_f_rms_res, rows["rms"], [p["norm_g"]], [(dh, D_MODEL, 0), (dx_out, D_MODEL, 0)],
                                   bt=BT, name=nm + "rms_b")
    return dx, g, (exchanged[0] if exchanged else None), late


MESH = pl.DeviceIdType.MESH
N_DEV = 8
_ANY = pl.BlockSpec(memory_space=pl.ANY)
_CHIP_SEMS = [pltpu.SemaphoreType.DMA((3,)), pltpu.SemaphoreType.DMA((3,)), pltpu.SemaphoreType.DMA]


def _here():
    x, y, c = lax.axis_index("x"), lax.axis_index("y"), lax.axis_index("c")
    return x, y, c, [(1 - x, y), (x, 1 - y), (1 - x, 1 - y)]


def _chip_exchange(srcs, *, per_dest, name):
    n = len(srcs)

    def body(*refs):
        _exchange(refs[:n], refs[n:2 * n], refs[2 * n:], per_dest, start=True, wait=True)

    return pl.pallas_call(
        body, name=name, in_specs=[_ANY] * n, out_specs=[_ANY] * n,
        out_shape=_exchange_out_shapes(srcs), scratch_shapes=_exchange_sems(n),
    )(*srcs)


def _exchange_out_shapes(srcs):
    return [jax.ShapeDtypeStruct((4,) + s.shape[1:], s.dtype) for s in srcs]


def _exchange_sems(n):
    return [pltpu.SemaphoreType.DMA((3 * n,)), pltpu.SemaphoreType.DMA((3 * n,)), pltpu.SemaphoreType.DMA((n,))]


def _exchange(src_refs, out_refs, sems, per_dest, *, start, wait):
    send_sems, recv_sems, local_sems = sems
    x, y, c, chips = _here()
    me = 2 * x + y
    owns, sends, recvs = [], [], []
    for a, (src_ref, out_ref) in enumerate(zip(src_refs, out_refs)):
        pick = (lambda q, s=src_ref: s.at[q]) if per_dest else (lambda q, s=src_ref: s.at[c])
        owns.append(pltpu.make_async_copy(pick(me), out_ref.at[me], local_sems.at[a]))
        for j, (px, py) in enumerate(chips):
            sends.append(pltpu.make_async_remote_copy(
                pick(2 * px + py), out_ref.at[me], send_sems.at[3 * a + j], recv_sems.at[3 * a + j],
                device_id=(px, py, c), device_id_type=MESH))
            recvs.append(pltpu.make_async_remote_copy(
                src_ref.at[0], out_ref.at[2 * px + py], send_sems.at[3 * a + j], recv_sems.at[3 * a + j],
                device_id=(px, py, c), device_id_type=MESH))
    if start:
        for cp in owns + sends:
            cp.start()
    if wait:
        for cp in recvs:
            cp.wait_recv()
        for cp in sends:
            cp.wait_send()
        for cp in owns:
            cp.wait()


def _sibling_swap(srcs, *, other_slot, name):
    n = len(srcs)

    def body(*refs):
        src_refs, out_refs, send_sems, recv_sems = refs[:n], refs[n:2 * n], refs[2 * n], refs[2 * n + 1]
        x, y, c, _ = _here()
        copies = [pltpu.make_async_remote_copy(s.at[1 - c] if other_slot else s, o, send_sems.at[a], recv_sems.at[a],
                                               device_id=(x, y, 1 - c), device_id_type=MESH)
                  for a, (s, o) in enumerate(zip(src_refs, out_refs))]
        for cp in copies:
            cp.start()
        for cp in copies:
            cp.wait()

    return pl.pallas_call(
        body, name=name, in_specs=[_ANY] * n, out_specs=[_ANY] * n,
        out_shape=[jax.ShapeDtypeStruct(s.shape[1:] if other_slot else s.shape, s.dtype) for s in srcs],
        scratch_shapes=[pltpu.SemaphoreType.DMA((n,)), pltpu.SemaphoreType.DMA((n,))],
    )(*srcs)


def _allgather_small(v, *, reduce, name):
    r = v.shape[0]

    def body(v_ref, out_ref, *rest):
        send_sems, recv_sems, local_sem = rest[-3:]
        x, y, c, chips = _here()
        me, sibling = (x, y, c), (x, y, 1 - c)

        def slot(px, py, pc):
            return out_ref.at[4 * px + 2 * py + pc]

        def copy(k, block, to, src=None):
            return pltpu.make_async_remote_copy(
                src_ref=slot(*block) if src is None else src, dst_ref=slot(*block),
                send_sem=send_sems.at[k], recv_sem=recv_sems.at[k], device_id=to, device_id_type=MESH)

        mine = pltpu.make_async_copy(v_ref, slot(*me), local_sem)
        mine.start()
        first = [copy(0, me, sibling, src=v_ref)]
        first += [copy(1 + j, me, (*chip, c), src=v_ref) for j, chip in enumerate(chips)]
        for cp in first:
            cp.start()
        passed = [copy(4 + j, (*chip, c), sibling) for j, chip in enumerate(chips)]
        for j, chip in enumerate(chips):
            copy(1 + j, (*chip, c), me).wait_recv()
            passed[j].start()
        copy(0, sibling, me).wait_recv()
        for j, chip in enumerate(chips):
            copy(4 + j, (*chip, 1 - c), me).wait_recv()
        for cp in first + passed:
            cp.wait_send()
        mine.wait()
        if reduce:
            total = out_ref[0]
            for d in range(1, N_DEV):
                total = total + out_ref[d]
            rest[0][...] = total

    vm = pl.BlockSpec(memory_space=pltpu.VMEM)
    out_shape = [jax.ShapeDtypeStruct((N_DEV, r, LANES), F32)] + ([jax.ShapeDtypeStruct((r, LANES), F32)] if reduce else [])
    return pl.pallas_call(
        body, name=name, in_specs=[vm], out_specs=[vm] * len(out_shape), out_shape=out_shape,
        scratch_shapes=[pltpu.SemaphoreType.DMA((7,)), pltpu.SemaphoreType.DMA((7,)), pltpu.SemaphoreType.DMA],
        compiler_params=pltpu.CompilerParams(vmem_limit_bytes=VMEM_LIMIT),
    )(v)


REDUCE_BLOCK_BYTES = 2 << 20


def _reduce_rows(r, c):
    cands = [b for b in range(16, r + 1, 16) if r % b == 0 and b * c * 4 <= REDUCE_BLOCK_BYTES]
    return max(cands)


def _add_halves(mine2, other, c_idx, *, name):
    _, nq, r, c = mine2.shape
    br = _reduce_rows(r, c)

    def body(c_ref, a_ref, b_ref, o_ref):
        o_ref[...] = (a_ref[0] + b_ref[...]).astype(o_ref.dtype)

    blk = pl.BlockSpec((1, br, c), lambda q, i, c_ref: (q, i, 0))
    return pl.pallas_call(
        body, name=name,
        grid_spec=pltpu.PrefetchScalarGridSpec(
            num_scalar_prefetch=1, grid=(nq, r // br),
            in_specs=[pl.BlockSpec((1, 1, br, c), lambda q, i, c_ref: (c_ref[0], q, i, 0)), blk],
            out_specs=blk),
        out_shape=jax.ShapeDtypeStruct((nq, r, c), BF16),
        compiler_params=_params(("parallel", "parallel")),
    )(c_idx, mine2, other)


def _sum_chips(parts, *, name):
    _, r, c = parts.shape
    br = _reduce_rows(r, c)

    def body(p_ref, o_ref):
        total = p_ref[0].astype(F32)
        for q in range(1, 4):
            total = total + p_ref[q].astype(F32)
        o_ref[...] = total

    return pl.pallas_call(
        body, name=name, grid=(r // br,),
        in_specs=[pl.BlockSpec((4, br, c), lambda i: (0, i, 0))],
        out_specs=pl.BlockSpec((br, c), lambda i: (i, 0)),
        out_shape=jax.ShapeDtypeStruct((r, c), F32),
        compiler_params=_params(("parallel",)),
    )(parts)


BIG = ("w_in", "w_out_sb", "w_out_ssd", "w_out_rw", "w_o")
BIG_AXIS = {"w_in": 2, "w_out_sb": 2, "w_out_ssd": 1, "w_out_rw": 2, "w_o": 1}
SMALL_SHARDED = {"conv_w": 320, "rw_w_up": 128, "rw_a_up": 128}
SMALL = ("norm_g", "conv_w", "conv_b", "dt_bias", "a_log", "d_skip", "ssd_norm_g", "rw_mu", "rw_w0", "rw_w_up",
         "rw_a0", "rw_a_up", "rw_k_k", "rw_k_a", "rw_r_k", "rw_ln_g", "rw_ln_b", "final_g")


def _rows_of(a):
    flat = a.reshape(-1)
    pad = (-flat.shape[0]) % LANES
    return jnp.pad(flat, (0, pad)).reshape(-1, LANES)


def _pack_rows(arrays, multiple=8):
    rows = jnp.concatenate([_rows_of(a) for a in arrays], axis=0)
    pad = (-rows.shape[0]) % multiple
    return jnp.pad(rows, ((0, pad), (0, 0)))


def _unpack_rows(rows, shapes):
    out, off = [], 0
    for shp in shapes:
        n = 1
        for d in shp:
            n *= d
        nr = -(-n // LANES)
        out.append(rows[off:off + nr].reshape(-1)[:n].reshape(shp))
        off += nr
    return out


COL_MAP = ((0, 3072, 0), (3072, 4352, C_XBC), (4352, 4368, C_DT), (4368, 6544, C_RW), (6544, 9616, C_GATES))
SHARD_COLS = N_IN // 4


def _w_in_from_shards(shards):
    pieces = []
    for a, b, dst in sorted(COL_MAP, key=lambda m: m[2]):
        if pieces and dst > pieces[-1][0]:
            pieces.append((dst, jnp.zeros((shards[0].shape[0], dst - pieces[-1][0]), shards[0].dtype)))
        for q in range(4):
            lo, hi = max(a, q * SHARD_COLS), min(b, (q + 1) * SHARD_COLS)
            if lo < hi:
                pieces.append((dst + hi - a, shards[q][:, lo - q * SHARD_COLS:hi - q * SHARD_COLS]))
    return jnp.concatenate([p for _, p in pieces], axis=1)


def _w_in_shard(g, q):
    pieces = []
    for a, b, dst in COL_MAP:
        lo, hi = max(a, q * SHARD_COLS), min(b, (q + 1) * SHARD_COLS)
        if lo < hi:
            pieces.append(g[:, dst + lo - a:dst + hi - a])
    return jnp.concatenate(pieces, axis=1)


def _row_halves(a):
    return a.reshape(2, a.shape[0] // 2, a.shape[1])


def _join_halves(core, mine, theirs):
    return jnp.where(core == 0, jnp.concatenate([mine, theirs], axis=-2), jnp.concatenate([theirs, mine], axis=-2))


def kernel(x, norm_g, w_in, conv_w, conv_b, dt_bias, a_log, d_skip, ssd_norm_g, rw_mu, rw_w0, rw_w_up, rw_a0, rw_a_up, rw_k_k, rw_k_a, rw_r_k, rw_ln_g, rw_ln_b, w_out_sb, w_out_ssd, w_out_rw, w_o, final_g, loss_target, m_norm_g, m_w_in, m_conv_w, m_conv_b, m_dt_bias, m_a_log, m_d_skip, m_ssd_norm_g, m_rw_mu, m_rw_w0, m_rw_w_up, m_rw_a0, m_rw_a_up, m_rw_k_k, m_rw_k_a, m_rw_r_k, m_rw_ln_g, m_rw_ln_b, m_w_out_sb, m_w_out_ssd, m_w_out_rw, m_w_o, m_final_g, v_norm_g, v_w_in, v_conv_w, v_conv_b, v_dt_bias, v_a_log, v_d_skip, v_ssd_norm_g, v_rw_mu, v_rw_w0, v_rw_w_up, v_rw_a0, v_rw_a_up, v_rw_k_k, v_rw_k_a, v_rw_r_k, v_rw_ln_g, v_rw_ln_b, v_w_out_sb, v_w_out_ssd, v_w_out_rw, v_w_o, v_final_g):
    names = ("norm_g", "w_in", "conv_w", "conv_b", "dt_bias", "a_log", "d_skip", "ssd_norm_g", "rw_mu", "rw_w0",
             "rw_w_up", "rw_a0", "rw_a_up", "rw_k_k", "rw_k_a", "rw_r_k", "rw_ln_g", "rw_ln_b", "w_out_sb",
             "w_out_ssd", "w_out_rw", "w_o", "final_g")
    w_loc = dict(zip(names, (norm_g, w_in, conv_w, conv_b, dt_bias, a_log, d_skip, ssd_norm_g, rw_mu, rw_w0, rw_w_up,
                             rw_a0, rw_a_up, rw_k_k, rw_k_a, rw_r_k, rw_ln_g, rw_ln_b, w_out_sb, w_out_ssd, w_out_rw,
                             w_o, final_g)))
    m_loc = dict(zip(names, (m_norm_g, m_w_in, m_conv_w, m_conv_b, m_dt_bias, m_a_log, m_d_skip, m_ssd_norm_g,
                             m_rw_mu, m_rw_w0, m_rw_w_up, m_rw_a0, m_rw_a_up, m_rw_k_k, m_rw_k_a, m_rw_r_k,
                             m_rw_ln_g, m_rw_ln_b, m_w_out_sb, m_w_out_ssd, m_w_out_rw, m_w_o, m_final_g)))
    v_loc = dict(zip(names, (v_norm_g, v_w_in, v_conv_w, v_conv_b, v_dt_bias, v_a_log, v_d_skip, v_ssd_norm_g,
                             v_rw_mu, v_rw_w0, v_rw_w_up, v_rw_a0, v_rw_a_up, v_rw_k_k, v_rw_k_a, v_rw_r_k,
                             v_rw_ln_g, v_rw_ln_b, v_w_out_sb, v_w_out_ssd, v_w_out_rw, v_w_o, v_final_g)))
    chip = 2 * lax.axis_index("x") + lax.axis_index("y")
    core = lax.axis_index("c")

    def gather_srcs(i):
        return [_row_halves(w_loc[n][i].astype(BF16)) for n in BIG]

    def gathered(mine, nm):
        theirs = _sibling_swap(mine, other_slot=False, name=nm)
        out = {}
        for n, a, b in zip(BIG, mine, theirs):
            shards = _join_halves(core, a, b)
            out[n] = (_w_in_from_shards([shards[q] for q in range(4)]) if n == "w_in"
                      else jnp.concatenate([shards[q] for q in range(4)], axis=BIG_AXIS[n] - 1))
        return out

    full = {}
    sm_names = tuple(SMALL_SHARDED)
    sm_shapes = [w_loc[n].shape for n in sm_names]
    (got_sm,) = _allgather_small(_pack_rows([w_loc[n] for n in sm_names]), reduce=False, name="gather_small")
    per_chip = [_unpack_rows(got_sm[4 * (q // 2) + 2 * (q % 2)], sm_shapes) for q in range(4)]
    for i, n in enumerate(sm_names):
        full[n] = jnp.concatenate([per_chip[q][i] for q in range(4)], axis=-1)

    def pad16(a):
        return jnp.zeros((1, LANES), F32).at[0, :SSD_HEADS].set(a)

    def layer_params(i, big):
        row = lambda n: w_loc[n][i].reshape(1, -1)
        cw = full["conv_w"][i]
        return dict(
            norm_g=row("norm_g"), w_in=big["w_in"], conv=[cw[k][None] for k in range(4)] + [row("conv_b")],
            dt_bias=pad16(dt_bias[i]), a_log=pad16(a_log[i]), d_skip=pad16(d_skip[i]),
            ssd_norm_g=row("ssd_norm_g"), rw_mu=row("rw_mu"),
            rw_pre=[row("rw_w0"), jnp.zeros((LANES, 512), F32).at[:HEAD].set(full["rw_w_up"][i]), row("rw_a0"),
                    jnp.zeros((LANES, 512), F32).at[HEAD:].set(full["rw_a_up"][i]), row("rw_k_k"), row("rw_k_a")],
            rw_post=[row("rw_ln_g"), row("rw_ln_b"), row("rw_r_k")],
            w_out_sb=big["w_out_sb"], w_out_ssd=big["w_out_ssd"], w_out_rw=big["w_out_rw"], w_o=big["w_o"])

    c_idx = core.reshape(1).astype(jnp.int32)

    def reduce_prepare(g, which, nm):
        sends = []
        for n in which:
            per_chip = ([_w_in_shard(g[n], q) for q in range(4)] if n == "w_in"
                        else jnp.split(g[n], 4, axis=BIG_AXIS[n] - 1))
            sends.append(jnp.stack([_row_halves(p) for p in per_chip], axis=1))
        others = _sibling_swap(sends, other_slot=True, name=nm + "sibling")
        return [_add_halves(s, o, c_idx, name=nm + "add_" + n) for n, s, o in zip(which, sends, others)]

    def reduce_finish(exchanged, which, nm):
        mine = [_sum_chips(p, name=nm + "sum_" + n) for n, p in zip(which, exchanged)]
        theirs = _sibling_swap(mine, other_slot=False, name=nm + "join")
        return {n: _join_halves(core, a, b) for n, a, b in zip(which, mine, theirs)}

    assert DEPTH == 2
    out_proj = BIG[1:]
    params, xs, saved, grads = [None] * 2, [x[0], None, None], [None] * 2, [None] * 2
    params[0] = layer_params(0, gathered(_chip_exchange(gather_srcs(0), per_dest=False, name="gather_l0"), "gather_l0_join"))
    xs[1], saved[0], got = _layer_fwd(xs[0], params[0], "l0_", side=(gather_srcs(1), False))
    params[1] = layer_params(1, gathered(got, "gather_l1_join"))
    xs[2], saved[1], _ = _layer_fwd(xs[1], params[1], "l1_")
    dx, loss_row, g_final = _final(xs[2], final_g.reshape(1, -1), loss_target[0], bt=BT, name="final")
    dx, grads[1], _, _ = _layer_bwd(xs[1], dx, params[1], saved[1], "l1_")
    dx, grads[0], got, got_late = _layer_bwd(
        xs[0], dx, params[0], saved[0], "l0_",
        side=lambda g: (reduce_prepare(grads[1], BIG, "reduce_l1_") + reduce_prepare(g, out_proj, "reduce_l0_out_"), True),
        side_late=lambda g: (reduce_prepare(g, ("w_in",), "reduce_l0_in_"), True))
    totals = [{**reduce_finish(got[len(BIG):], out_proj, "reduce_l0_out_"),
               **reduce_finish(got_late, ("w_in",), "reduce_l0_in_")},
              reduce_finish(got[:len(BIG)], BIG, "reduce_l1_")]

    def stacked(fn):
        return jnp.stack([fn(grads[i]) for i in range(DEPTH)])

    g_loc = {
        "norm_g": stacked(lambda g: g["norm_g"][0]),
        "conv_w": stacked(lambda g: jnp.concatenate(g["conv"][:4], axis=0)),
        "conv_b": stacked(lambda g: g["conv"][4][0]),
        "dt_bias": stacked(lambda g: g["dt_bias"][0, :SSD_HEADS]),
        "a_log": stacked(lambda g: g["a_log"][0, :SSD_HEADS]),
        "d_skip": stacked(lambda g: g["d_skip"][0, :SSD_HEADS]),
        "ssd_norm_g": stacked(lambda g: g["ssd_norm_g"][0]),
        "rw_mu": stacked(lambda g: g["rw_mu"][0]),
        "rw_w0": stacked(lambda g: g["rw_pre"][0][0]),
        "rw_w_up": stacked(lambda g: g["rw_pre"][1][:HEAD]),
        "rw_a0": stacked(lambda g: g["rw_pre"][2][0]),
        "rw_a_up": stacked(lambda g: g["rw_pre"][3][HEAD:]),
        "rw_k_k": stacked(lambda g: g["rw_pre"][4][0]),
        "rw_k_a": stacked(lambda g: g["rw_pre"][5][0]),
        "rw_r_k": stacked(lambda g: g["rw_r_k"].reshape(8, HEAD)),
        "rw_ln_g": stacked(lambda g: g["rw_ln_g"][0]),
        "rw_ln_b": stacked(lambda g: g["rw_ln_b"][0]),
        "final_g": g_final[0],
    }

    g_out = {n: jnp.stack([totals[0][n], totals[1][n]]) for n in BIG}

    sm_all = SMALL + ("loss",)
    sm_full_shapes = [g_loc[n].shape for n in SMALL] + [(1,)]
    _, summed = _allgather_small(_pack_rows([g_loc[n] for n in SMALL] + [loss_row[0, :1]]), reduce=True, name="reduce_small")
    sm = dict(zip(sm_all, _unpack_rows(summed, sm_full_shapes)))
    for n in SMALL:
        g_out[n] = sm[n]
    for n, wd in SMALL_SHARDED.items():
        g_out[n] = lax.dynamic_slice_in_dim(sm[n], chip * wd, wd, axis=sm[n].ndim - 1)
    loss = sm["loss"][0]

    upd = {n: _adamw(w_loc[n], g_out[n], m_loc[n], v_loc[n], name="adamw_" + n) for n in names if n != "w_in"}
    cols = SHARD_COLS // 4
    to_cols = lambda a: jnp.transpose(a, (2, 0, 1)).reshape(4, cols, DEPTH, D_MODEL)
    from_cols = lambda a: jnp.transpose(a.reshape(SHARD_COLS, DEPTH, D_MODEL), (1, 2, 0))
    g_cols = lax.optimization_barrier(to_cols(g_out["w_in"]))
    g_out["w_in"] = from_cols(g_cols)
    upd["w_in"] = tuple(from_cols(a) for a in _adamw(
        to_cols(w_loc["w_in"]), g_cols, to_cols(m_loc["w_in"]), to_cols(v_loc["w_in"]),
        name="adamw_w_in", block=(1, cols, DEPTH, D_MODEL // 2)))
    return (loss, dx[None], *[g_out[n] for n in names], *[upd[n][0] for n in names],
            *[upd[n][1] for n in names], *[upd[n][2] for n in names])
```

```python
import functools

import jax
import jax.numpy as jnp
from jax import lax
from jax.experimental import pallas as pl
from jax.experimental.pallas import tpu as pltpu

F32 = jnp.float32
BF16 = jnp.bfloat16

D_MODEL = 1024
DEPTH = 2
HEAD = 64
LANES = 128
CHUNK = 128
RMS_EPS = 1e-6
GN_EPS = 64e-5
VMEM_LIMIT = 56 * 1024 * 1024

N_IN = 9616
N_PAD = 9728
C_SB, C_Z, C_GATES, C_RW, C_LO, C_DT, C_XBC = 0, 2048, 3072, 6144, 8192, 8320, 8448
RW_COLS = 2176
XBC_COLS = 1280

ADAM_LR, ADAM_B1, ADAM_B2, ADAM_EPS, ADAM_WD, ADAM_STEP = 0.001, 0.9, 0.999, 1e-08, 0.01, 10


def _params(sem=None):
    return pltpu.CompilerParams(dimension_semantics=sem, vmem_limit_bytes=VMEM_LIMIT)


@jax.custom_vjp
def _sigmoid(x):
    return 1.0 / (1.0 + jnp.exp(-x))


def _sigmoid_fwd(x):
    s = _sigmoid(x)
    return s, s


def _sigmoid_bwd(s, g):
    return (g * s * (1.0 - s),)


_sigmoid.defvjp(_sigmoid_fwd, _sigmoid_bwd)


@jax.custom_vjp
def _silu(x):
    return x * _sigmoid(x)


def _silu_fwd(x):
    s = _sigmoid(x)
    return x * s, (x, s)


def _silu_bwd(res, g):
    x, s = res
    return (g * (s + x * s * (1.0 - s)),)


_silu.defvjp(_silu_fwd, _silu_bwd)


@jax.custom_vjp
def _softplus(x):
    return jnp.maximum(x, 0.0) + jnp.log(1.0 + jnp.exp(-jnp.abs(x)))


def _softplus_fwd(x):
    return _softplus(x), x


def _softplus_bwd(x, g):
    return (g * _sigmoid(x),)


_softplus.defvjp(_softplus_fwd, _softplus_bwd)


def _dot(a, b, dims):
    return lax.dot_general(a.astype(BF16), b.astype(BF16), (dims, ((), ())), preferred_element_type=F32)


def _dot_nn(a, b):
    return _dot(a, b, ((1,), (0,)))


def _dot_nt(a, b):
    return _dot(a, b, ((1,), (1,)))


def _dot_tn(a, b):
    return _dot(a, b, ((0,), (0,)))


@jax.custom_vjp
def _bdot(a, b):
    return _dot_nn(a, b)


def _bdot_fwd(a, b):
    return _dot_nn(a, b), (a, b)


def _bdot_bwd(res, g):
    a, b = res
    return _dot_nt(g, b), _dot_tn(a, g)


_bdot.defvjp(_bdot_fwd, _bdot_bwd)


def _split2(x):
    hi = x.astype(BF16)
    lo = (x - hi.astype(F32)).astype(BF16)
    return hi, lo


_NT = (((1,), (1,)), ((), ()))
_NN = (((1,), (0,)), ((), ()))
_TN = (((0,), (0,)), ((), ()))


def _dot2(x, m, dn=_NN):
    hi, lo = _split2(x)
    return (lax.dot_general(hi, m, dn, preferred_element_type=F32)
            + lax.dot_general(lo, m, dn, preferred_element_type=F32))


def _dot2_tn(x, m):
    return _dot2(x, m, _TN)


def _seg_matrix(n):
    r = lax.broadcasted_iota(jnp.int32, (n, n), 0) // HEAD
    c = lax.broadcasted_iota(jnp.int32, (n, n), 1) // HEAD
    return (r == c).astype(BF16)


@jax.custom_vjp
def _segsum2(x, seg):
    return _dot2(x, seg)


def _segsum2_fwd(x, seg):
    return _dot2(x, seg), seg


def _segsum2_bwd(seg, g):
    return _dot2(g, seg), jnp.zeros_like(seg)


_segsum2.defvjp(_segsum2_fwd, _segsum2_bwd)


def _make_segsum(seg):
    return lambda x: _segsum2(x, seg)


def _shift_down_raw(x, k):
    row = lax.broadcasted_iota(jnp.int32, x.shape, 0)
    return jnp.where(row >= k, pltpu.roll(x, k, 0), 0.0)


def _shift_up_raw(x, k):
    t = x.shape[0]
    row = lax.broadcasted_iota(jnp.int32, x.shape, 0)
    return jnp.where(row < t - k, pltpu.roll(x, t - k, 0), 0.0)


@functools.partial(jax.custom_vjp, nondiff_argnums=(1,))
def _shift_down(x, k):
    return _shift_down_raw(x, k)


def _shift_down_fwd(x, k):
    return _shift_down_raw(x, k), None


def _shift_down_bwd(k, _, g):
    return (_shift_up_raw(g, k),)


_shift_down.defvjp(_shift_down_fwd, _shift_down_bwd)


def _mm(a, b, *, name, ta=False, tb=False, add=None, out_dtype=F32, tm=2048, tn=512, tk=None, side=None):
    m, k = (a.shape[1], a.shape[0]) if ta else a.shape
    n = b.shape[0] if tb else b.shape[1]
    tm, tn = min(tm, m), min(tn, n)
    tk = k if tk is None else tk
    nk = k // tk
    assert m % tm == 0 and n % tn == 0 and k % tk == 0
    dims = ((0 if ta else 1,), (1 if tb else 0,))

    def body(a_ref, b_ref, *refs):
        o_ref, acc_ref = refs[-2:]
        p = _dot(a_ref[...], b_ref[...], dims)

        def emit(total):
            if add is not None:
                total = total + refs[0][...]
            o_ref[...] = total.astype(o_ref.dtype)

        if nk == 1:
            emit(p)
        else:
            kk = pl.program_id(2)

            @pl.when(kk == 0)
            def _():
                acc_ref[...] = p

            @pl.when(kk > 0)
            def _():
                acc_ref[...] += p

            @pl.when(kk == nk - 1)
            def _():
                emit(acc_ref[...])

    a_spec = pl.BlockSpec((tk, tm), lambda i, j, kk: (kk, i)) if ta else pl.BlockSpec((tm, tk), lambda i, j, kk: (i, kk))
    b_spec = pl.BlockSpec((tn, tk), lambda i, j, kk: (j, kk)) if tb else pl.BlockSpec((tk, tn), lambda i, j, kk: (kk, j))
    o_spec = pl.BlockSpec((tm, tn), lambda i, j, kk: (i, j))
    res = _call_with_side(
        body, side, name=name, grid=(m // tm, n // tn, nk), semantics=("parallel", "parallel", "arbitrary"),
        in_specs=[a_spec, b_spec] + ([o_spec] if add is not None else []), out_specs=[o_spec],
        out_shape=[jax.ShapeDtypeStruct((m, n), out_dtype)],
        scratch_shapes=[pltpu.VMEM((tm, tn) if nk > 1 else (8, LANES), F32)],
        operands=(a, b) + ((add,) if add is not None else ()))
    return res[0] if side is None else (res[0], res[1])


def _row_specs(rows, bt):
    return [pl.BlockSpec((bt, w), functools.partial(lambda i, c: (i, c), c=c)) for _, w, c in rows]


def _full_spec(p):
    return pl.BlockSpec(p.shape, functools.partial(lambda i, nd: (0,) * nd, nd=p.ndim))


def _rowwise(f, rows, pars, out_widths, *, bt, name, acc_widths=()):
    t = rows[0][0].shape[0]
    nr, npar, no, na = len(rows), len(pars), len(out_widths), len(acc_widths)

    def body(*refs):
        vals = [r[...] for r in refs[:nr + npar]]
        outs = f(*vals)
        for o_ref, o in zip(refs[nr + npar:nr + npar + no], outs[:no]):
            o_ref[...] = o.astype(o_ref.dtype)
        if na:
            first = pl.program_id(0) == 0
            for a_ref, a in zip(refs[nr + npar + no:], outs[no:]):
                @pl.when(first)
                def _():
                    a_ref[...] = jnp.zeros_like(a_ref)
                a_ref[...] += a

    return pl.pallas_call(
        body, name=name, grid=(t // bt,),
        in_specs=_row_specs(rows, bt) + [_full_spec(p) for p in pars],
        out_specs=[pl.BlockSpec((bt, w), lambda i: (i, 0)) for w in out_widths]
        + [pl.BlockSpec((1, w), lambda i: (0, 0)) for w in acc_widths],
        out_shape=[jax.ShapeDtypeStruct((t, w), F32) for w in out_widths]
        + [jax.ShapeDtypeStruct((1, w), F32) for w in acc_widths],
        compiler_params=_params(("arbitrary",)),
    )(*[r[0] for r in rows], *pars)


def _rowwise_bwd(f, rows, pars, douts, *, bt, name, groups=None):
    t = rows[0][0].shape[0]
    nr, npar, nd = len(rows), len(pars), len(douts)
    groups = [[i] for i in range(nr)] if groups is None else groups
    widths = [r[1] for r in rows]

    def body(*refs):
        vals = [r[...] for r in refs[:nr + npar]]
        cts = tuple(r[...] for r in refs[nr + npar:nr + npar + nd])
        _, vjp = jax.vjp(lambda *a: tuple(f(*a)), *vals)
        grads = vjp(cts)
        out_refs = refs[nr + npar + nd:]
        for g_ref, grp in zip(out_refs[:len(groups)], groups):
            off = 0
            for i in grp:
                g_ref[:, off:off + widths[i]] = grads[i]
                off += widths[i]
        first = pl.program_id(0) == 0
        for p_ref, g in zip(out_refs[len(groups):], grads[nr:]):
            @pl.when(first)
            def _():
                p_ref[...] = jnp.zeros_like(p_ref)
            p_ref[...] += g

    gw = [sum(widths[i] for i in grp) for grp in groups]
    return pl.pallas_call(
        body, name=name, grid=(t // bt,),
        in_specs=_row_specs(rows, bt) + [_full_spec(p) for p in pars] + _row_specs(douts, bt),
        out_specs=[pl.BlockSpec((bt, w), lambda i: (i, 0)) for w in gw] + [_full_spec(p) for p in pars],
        out_shape=[jax.ShapeDtypeStruct((t, w), F32) for w in gw] + [jax.ShapeDtypeStruct(p.shape, F32) for p in pars],
        compiler_params=_params(("arbitrary",)),
    )(*[r[0] for r in rows], *pars, *[d[0] for d in douts])


def _colwise(f, x, c0, ncols, pars, *, bc, name):
    t = x.shape[0]

    def body(x_ref, *refs):
        o_ref = refs[-1]
        o_ref[...] = f(x_ref[...], *[r[...] for r in refs[:-1]])

    return pl.pallas_call(
        body, name=name, grid=(ncols // bc,),
        in_specs=[pl.BlockSpec((t, bc), lambda j: (0, j + c0 // bc))]
        + [pl.BlockSpec((p.shape[0], bc), lambda j: (0, j)) for p in pars],
        out_specs=pl.BlockSpec((t, bc), lambda j: (0, j)),
        out_shape=jax.ShapeDtypeStruct((t, ncols), F32),
        compiler_params=_params(("parallel",)),
    )(x, *pars)


def _colwise_bwd(f, x, c0, ncols, pars, dout, *, bc, name):
    t = x.shape[0]
    npar = len(pars)

    def body(x_ref, *refs):
        vals = [x_ref[...]] + [r[...] for r in refs[:npar]]
        _, vjp = jax.vjp(f, *vals)
        grads = vjp(refs[npar][...])
        for g_ref, g in zip(refs[npar + 1:], grads):
            g_ref[...] = g

    return pl.pallas_call(
        body, name=name, grid=(ncols // bc,),
        in_specs=[pl.BlockSpec((t, bc), lambda j: (0, j + c0 // bc))]
        + [pl.BlockSpec((p.shape[0], bc), lambda j: (0, j)) for p in pars]
        + [pl.BlockSpec((t, bc), lambda j: (0, j))],
        out_specs=[pl.BlockSpec((t, bc), lambda j: (0, j))]
        + [pl.BlockSpec((p.shape[0], bc), lambda j: (0, j)) for p in pars],
        out_shape=[jax.ShapeDtypeStruct((t, ncols), F32)] + [jax.ShapeDtypeStruct(p.shape, F32) for p in pars],
        compiler_params=_params(("parallel",)),
    )(x, *pars, dout)


def _f_rms(x, g):
    return (x * lax.rsqrt(jnp.mean(x * x, axis=-1, keepdims=True) + RMS_EPS) * g,)


def _f_sb_gate(y, gate):
    return (y * _silu(gate),)


def _f_ssd_norm(y, z, g):
    u = y * _silu(z)
    return (u * lax.rsqrt(jnp.mean(u * u, axis=-1, keepdims=True) + RMS_EPS) * g,)


def _f_merge(p_sb, p_ssd, p_rw, g_sb, g_ssd, g_rw):
    return (_sigmoid(g_sb) * p_sb + _sigmoid(g_ssd) * p_ssd + _sigmoid(g_rw) * p_rw,)


def _f_rw_pre(k, lo, w0, w_up, a0, a_up, k_k, k_a):
    segsum = _make_segsum(_seg_matrix(k.shape[1]))
    lane = lax.broadcasted_iota(jnp.int32, lo.shape, 1)
    w_lo = jnp.where(lane < HEAD, jnp.tanh(lo), 0.0)
    a_lo = jnp.where(lane >= HEAD, lo, 0.0)
    w = -_softplus(-(w0 + _bdot(w_lo, w_up))) - 0.5
    log_decay = -jnp.exp(w)
    a = _sigmoid(a0 + _bdot(a_lo, a_up))
    kk = k * k_k
    kk = kk / jnp.maximum(jnp.sqrt(segsum(kk * kk)), 1e-12)
    return log_decay, k * (1.0 + (a - 1.0) * k_a), -kk, kk * a


def _f_rw_post(y, r, k2, v, gate, ln_g, ln_b, r_k):
    segsum = _make_segsum(_seg_matrix(y.shape[1]))
    yc = y - segsum(y) * (1.0 / HEAD)
    var = segsum(yc * yc) * (1.0 / HEAD)
    yn = yc * lax.rsqrt(var + GN_EPS) * ln_g + ln_b
    return ((yn + segsum(r * k2 * r_k) * v) * _silu(gate),)


def _f_rw_mix(slab, mu):
    return slab + (_shift_down(slab, 1) - slab) * mu


def _f_conv(x, w0, w1, w2, w3, b):
    acc = x * w3 + b
    for i, w in enumerate((w0, w1, w2)):
        acc = acc + _shift_down(x, 3 - i) * w
    return _silu(acc)


def _log_sigmoid(z):
    return jnp.minimum(z, 0.0) - jnp.log(1.0 + jnp.exp(-jnp.abs(z)))


SB_BQ = 256
SB_BK = 256
assert SB_BQ == SB_BK


def _tri_ones(kind):
    j = lax.broadcasted_iota(jnp.int32, (SB_BK, SB_BK + LANES), 0)
    s = lax.broadcasted_iota(jnp.int32, (SB_BK, SB_BK + LANES), 1)
    tri = {"gt": j > s, "le": j <= s, "lt": j < s}[kind]
    return (tri | (s >= SB_BK)).astype(BF16)


def _sb_common(q_ref):
    lane = lax.broadcasted_iota(jnp.int32, (SB_BQ, LANES), 1)
    q = q_ref[...] * (HEAD ** -0.5)
    q2 = jnp.concatenate([jnp.where(lane < HEAD, q, 0.0), jnp.where(lane >= HEAD, q, 0.0)], axis=0).astype(BF16)
    diff = (lax.broadcasted_iota(jnp.int32, (2 * SB_BQ, SB_BK), 1)
            - (lax.broadcasted_iota(jnp.int32, (2 * SB_BQ, SB_BK), 0) & (SB_BQ - 1)))
    return lane, q2, diff


def _rep(x):
    return jnp.concatenate([x] * (SB_BK // LANES), axis=1)


def _sb2_specs(t):
    q = pl.BlockSpec((SB_BQ, LANES), lambda j, i: (i, j))
    k = pl.BlockSpec((t, LANES), lambda j, i: (0, 4 + j))
    v = pl.BlockSpec((t, LANES), lambda j, i: (0, 8 + j))
    return q, k, v


def _sb2_fwd(proj, *, name):
    t = proj.shape[0]

    def body(q_ref, k_ref, v_ref, y_ref, lt_ref):
        i = pl.program_id(1)
        lane, q2, diff = _sb_common(q_ref)
        m_f = _tri_ones("gt")

        def step(kb, carry, diagonal):
            c, acc = carry
            off = pl.multiple_of(kb * SB_BK, SB_BK)
            kblk = k_ref[pl.ds(off, SB_BK), :].astype(BF16)
            vblk = v_ref[pl.ds(off, SB_BK), :].astype(BF16)
            z = lax.dot_general(q2, kblk, _NT, preferred_element_type=F32)
            lb = _log_sigmoid(z)
            lk = jnp.where(diff < 0, lb - z, 0.0) if diagonal else lb - z
            w2 = _dot2(lk, m_f)
            att = jnp.exp(lb + _rep(c) + w2[:, :SB_BK])
            if diagonal:
                att = jnp.where(diff < 0, att, 0.0)
            acc = acc + lax.dot_general(att.astype(BF16), vblk, _NN, preferred_element_type=F32)
            return c + w2[:, SB_BK:], acc

        zero = jnp.zeros((2 * SB_BQ, LANES), F32)
        c, acc = lax.fori_loop(0, i, lambda it, carry: step(i - 1 - it, carry, False), step(i, (zero, zero), True))
        y_ref[...] = jnp.where(lane < HEAD, acc[:SB_BQ], acc[SB_BQ:])
        lt_ref[0] = c[:SB_BQ]
        lt_ref[1] = c[SB_BQ:]

    return pl.pallas_call(
        body, name=name, grid=(4, t // SB_BQ),
        in_specs=list(_sb2_specs(t)),
        out_specs=[pl.BlockSpec((SB_BQ, LANES), lambda j, i: (i, j)),
                   pl.BlockSpec((2, SB_BQ, LANES), lambda j, i: (j, i, 0))],
        out_shape=[jax.ShapeDtypeStruct((t, 4 * LANES), F32), jax.ShapeDtypeStruct((8, t, LANES), F32)],
        compiler_params=_params(("parallel", "arbitrary")),
    )(proj, proj, proj)


def _sb2_bwd(proj, dy, lt, *, name):
    t = proj.shape[0]

    def body(q_ref, k_ref, v_ref, dy_ref, lt_ref, dq_ref, dk_ref, dv_ref):
        i = pl.program_id(1)

        @pl.when(i == 0)
        def _():
            dk_ref[...] = jnp.zeros_like(dk_ref)
            dv_ref[...] = jnp.zeros_like(dv_ref)

        lane, q2, diff = _sb_common(q_ref)
        m_le, m_lt = _tri_ones("le"), _tri_ones("lt")
        dy_blk = dy_ref[...]
        do2 = jnp.concatenate([jnp.where(lane < HEAD, dy_blk, 0.0), jnp.where(lane >= HEAD, dy_blk, 0.0)],
                              axis=0).astype(BF16)
        lt2 = jnp.concatenate([lt_ref[0], lt_ref[1]], axis=0)

        def step(kb, carry, diagonal):
            cp, cg, dq = carry
            off = pl.multiple_of(kb * SB_BK, SB_BK)
            kblk = k_ref[pl.ds(off, SB_BK), :].astype(BF16)
            vblk = v_ref[pl.ds(off, SB_BK), :].astype(BF16)
            z = lax.dot_general(q2, kblk, _NT, preferred_element_type=F32)
            lb = _log_sigmoid(z)
            lk = jnp.where(diff < 0, lb - z, 0.0) if diagonal else lb - z
            w2 = _dot2(lk, m_le)
            att = jnp.exp(lb + _rep(lt2 - cp) - w2[:, :SB_BK])
            if diagonal:
                att = jnp.where(diff < 0, att, 0.0)
            d_e = lax.dot_general(do2, vblk, _NT, preferred_element_type=F32) * att
            g2 = _dot2(d_e, m_lt)
            sig = jnp.exp(lb)
            dz = d_e * (1.0 - sig) - (_rep(cg) + g2[:, :SB_BK]) * sig
            dz = (jnp.where(diff < 0, dz, 0.0) if diagonal else dz).astype(BF16)
            dq = dq + lax.dot_general(dz, kblk, _NN, preferred_element_type=F32)
            dk_ref[pl.ds(off, SB_BK), :] += lax.dot_general(dz, q2, _TN, preferred_element_type=F32)
            dv_ref[pl.ds(off, SB_BK), :] += lax.dot_general(att.astype(BF16), do2, _TN, preferred_element_type=F32)
            return cp + w2[:, SB_BK:], cg + g2[:, SB_BK:], dq

        zero = jnp.zeros((2 * SB_BQ, LANES), F32)
        before = lax.fori_loop(0, i, lambda kb, carry: step(kb, carry, False), (zero, zero, zero))
        _, _, dq = step(i, before, True)
        dq_ref[...] = jnp.where(lane < HEAD, dq[:SB_BQ], dq[SB_BQ:]) * (HEAD ** -0.5)

    q_spec, k_spec, v_spec = _sb2_specs(t)
    blk = pl.BlockSpec((SB_BQ, LANES), lambda j, i: (i, j))
    col = pl.BlockSpec((t, LANES), lambda j, i: (0, j))
    return pl.pallas_call(
        body, name=name, grid=(4, t // SB_BQ),
        in_specs=[q_spec, k_spec, v_spec, blk, pl.BlockSpec((2, SB_BQ, LANES), lambda j, i: (j, i, 0))],
        out_specs=[blk, col, col],
        out_shape=[jax.ShapeDtypeStruct((t, 4 * LANES), F32)] * 3,
        compiler_params=_params(("parallel", "arbitrary")),
    )(proj, proj, proj, dy, lt)


SSD_HEADS = 16
SSD_PAIRS = 8


def _split3(x):
    a = x.astype(BF16)
    r = x - a.astype(F32)
    b = r.astype(BF16)
    return a, b, (r - b.astype(F32)).astype(BF16)


def _dot3(x, m, dn=_NN):
    return sum(lax.dot_general(p, m, dn, preferred_element_type=F32) for p in _split3(x))


def _mdot3(m, x):
    return sum(lax.dot_general(m, p, _NN, preferred_element_type=F32) for p in _split3(x))


def _ssd_common(dtr, dtb, alog, acsx_s, acst_s):
    lane = lax.broadcasted_iota(jnp.int32, (CHUNK, LANES), 1)
    lane1 = lax.broadcasted_iota(jnp.int32, (1, LANES), 1)
    arow = jnp.where(lane1 < SSD_HEADS, -jnp.exp(alog), 0.0)
    dt = jnp.where(lane < SSD_HEADS, _softplus(dtr + dtb), 0.0)
    da = dt * arow
    r = lax.broadcasted_iota(jnp.int32, (CHUNK, CHUNK), 0)
    c = lax.broadcasted_iota(jnp.int32, (CHUNK, CHUNK), 1)
    tril = (r >= c).astype(BF16)
    triu = (r <= c).astype(BF16)
    acs = _mdot3(tril, da)
    acst_s[...] = _dot3(da, triu, _TN)
    eh = lax.broadcasted_iota(jnp.int32, (LANES, 8 * LANES), 0)
    e = (eh == lax.broadcasted_iota(jnp.int32, (LANES, 8 * LANES), 1) // HEAD).astype(BF16)
    eh2 = lax.broadcasted_iota(jnp.int32, (LANES, 16 * LANES), 0)
    e2 = (eh2 == lax.broadcasted_iota(jnp.int32, (LANES, 16 * LANES), 1) // LANES).astype(BF16)
    acsx_s[...] = _dot3(acs, e)
    return dt, arow, _dot3(dt, e), _dot3(acs, e2), e, tril, triu


def _ssd_fwd(xc, proj, dtb, alog, dsk, *, name):
    t = xc.shape[0]
    nc = t // CHUNK

    def body(x_ref, b_ref, c_ref, dtr_ref, dtb_ref, alog_ref, dsk_ref, y_ref, hin_ref, acsx_s, acst_s, h_s):
        @pl.when(pl.program_id(0) == 0)
        def _():
            h_s[...] = jnp.zeros_like(h_s)

        dt, arow, dt_x, acs_b, e, tril, _ = _ssd_common(dtr_ref[...], dtb_ref[...], alog_ref[...], acsx_s, acst_s)
        dsk_x = _dot3(jnp.broadcast_to(dsk_ref[...], (CHUNK, LANES)), e)
        lane = lax.broadcasted_iota(jnp.int32, (CHUNK, LANES), 1)
        causal = (lax.broadcasted_iota(jnp.int32, (CHUNK, CHUNK), 0)
                  >= lax.broadcasted_iota(jnp.int32, (CHUNK, CHUNK), 1))
        for j in range(SSD_PAIRS):
            g = j // 4
            sl = slice(j * LANES, (j + 1) * LANES)
            if j % 4 == 0:
                bg = jnp.where(lane // HEAD == g, b_ref[...], 0.0)
                cg = jnp.where(lane // HEAD == g, c_ref[...], 0.0)
                cb = _dot_nt(cg, bg)
            x = x_ref[:, sl]
            a = acsx_s[:, sl]
            at = acsx_s[CHUNK - 1:CHUNK, sl]
            xdt = x * dt_x[:, sl]
            hin = h_s[j]
            hin_ref[0, j] = hin
            y = jnp.exp(a) * _dot_nn(cg, hin) + x * dsk_x[:, sl]
            h_s[j] = jnp.exp(at) * hin + _dot_tn(bg, xdt * jnp.exp(at - a))
            yd = []
            for hh in (0, 1):
                h = 2 * j + hh
                dec = jnp.exp(jnp.minimum(acs_b[:, h * LANES:(h + 1) * LANES] - acst_s[pl.ds(h, 1), :], 0.0))
                yd.append(_dot_nn(jnp.where(causal, cb * dec, 0.0), xdt))
            y_ref[:, sl] = y + jnp.where(lane < HEAD, yd[0], yd[1])

    one = pl.BlockSpec((1, LANES), lambda i: (0, 0))
    return pl.pallas_call(
        body, name=name, grid=(nc,),
        in_specs=[pl.BlockSpec((CHUNK, 8 * LANES), lambda i: (i, 0)),
                  pl.BlockSpec((CHUNK, LANES), lambda i: (i, 8)),
                  pl.BlockSpec((CHUNK, LANES), lambda i: (i, 9)),
                  pl.BlockSpec((CHUNK, LANES), lambda i: (i, C_DT // LANES)), one, one, one],
        out_specs=[pl.BlockSpec((CHUNK, 8 * LANES), lambda i: (i, 0)),
                   pl.BlockSpec((1, SSD_PAIRS, LANES, LANES), lambda i: (i, 0, 0, 0))],
        out_shape=[jax.ShapeDtypeStruct((t, 8 * LANES), F32),
                   jax.ShapeDtypeStruct((nc, SSD_PAIRS, LANES, LANES), F32)],
        scratch_shapes=[pltpu.VMEM((CHUNK, 8 * LANES), F32), pltpu.VMEM((LANES, CHUNK), F32),
                        pltpu.VMEM((SSD_PAIRS, LANES, LANES), F32)],
        compiler_params=_params(("arbitrary",)),
    )(xc, xc, xc, proj, dtb, alog, dsk)


def _ssd_bwd(xc, proj, dtb, alog, dsk, hin_all, dy, *, name):
    t = xc.shape[0]
    nc = t // CHUNK

    def body(x_ref, b_ref, c_ref, dtr_ref, dtb_ref, alog_ref, dsk_ref, hin_ref, dy_ref,
             dxc_ref, ddtr_ref, ddtb_ref, dalog_ref, ddsk_ref, acsx_s, acst_s, dh_s, dax_s, ddx_s):
        @pl.when(pl.program_id(0) == 0)
        def _():
            dh_s[...] = jnp.zeros_like(dh_s)
            ddtb_ref[...] = jnp.zeros_like(ddtb_ref)
            dalog_ref[...] = jnp.zeros_like(dalog_ref)
            ddsk_ref[...] = jnp.zeros_like(ddsk_ref)

        dtr = dtr_ref[...]
        dtb = dtb_ref[...]
        dt, arow, dt_x, acs_b, e, tril, triu = _ssd_common(dtr, dtb, alog_ref[...], acsx_s, acst_s)
        dsk_x = _dot3(jnp.broadcast_to(dsk_ref[...], (CHUNK, LANES)), e)
        lane = lax.broadcasted_iota(jnp.int32, (CHUNK, LANES), 1)
        rowi = lax.broadcasted_iota(jnp.int32, (CHUNK, LANES), 0)
        causal = (lax.broadcasted_iota(jnp.int32, (CHUNK, CHUNK), 0)
                  >= lax.broadcasted_iota(jnp.int32, (CHUNK, CHUNK), 1))
        dacs = jnp.zeros((CHUNK, LANES), F32)
        d_b = jnp.zeros((CHUNK, LANES), F32)
        d_c = jnp.zeros((CHUNK, LANES), F32)
        for j in range(SSD_PAIRS):
            g = j // 4
            sl = slice(j * LANES, (j + 1) * LANES)
            if j % 4 == 0:
                bg = jnp.where(lane // HEAD == g, b_ref[...], 0.0)
                cg = jnp.where(lane // HEAD == g, c_ref[...], 0.0)
                cb = _dot_nt(cg, bg)
                dcb = jnp.zeros((CHUNK, CHUNK), F32)
            x = x_ref[:, sl]
            d = dt_x[:, sl]
            a = acsx_s[:, sl]
            at = acsx_s[CHUNK - 1:CHUNK, sl]
            xdt = x * d
            hin = hin_ref[0, j]
            dhout = dh_s[j]
            dyp = dy_ref[:, sl]
            ea, eat, ed = jnp.exp(a), jnp.exp(at), jnp.exp(at - a)
            da_l = dyp * ea * _dot_nn(cg, hin)
            dm = dyp * ea
            d_c = d_c + _dot_nt(dm, hin)
            dh_s[j] = _dot_tn(cg, dm) + eat * dhout
            dat = jnp.sum(dhout * hin * eat, axis=0, keepdims=True)
            d_b = d_b + _dot_nt(xdt * ed, dhout)
            dw = _dot_nn(bg, dhout)
            dxdt = dw * ed
            ded = dw * xdt * ed
            dat = dat + jnp.sum(ded, axis=0, keepdims=True)
            da_l = da_l - ded
            for hh in (0, 1):
                h = 2 * j + hh
                dec = jnp.exp(jnp.minimum(acs_b[:, h * LANES:(h + 1) * LANES] - acst_s[pl.ds(h, 1), :], 0.0))
                gm = jnp.where(causal, cb * dec, 0.0)
                dyh = jnp.where(lane // HEAD == hh, dyp, 0.0)
                dg = _dot_nt(dyh, xdt)
                dxdt = dxdt + _dot_tn(gm, dyh)
                dcb = dcb + jnp.where(causal, dg * dec, 0.0)
                th = dg * gm
                oh = (lane == h).astype(BF16)
                dacs = dacs + _dot2(th, oh) - _dot2_tn(th, oh)
            if j % 4 == 3:
                d_c = d_c + _dot_nn(dcb, bg)
                d_b = d_b + _dot_tn(dcb, cg)
            dxc_ref[:, sl] = dyp * dsk_x[:, sl] + dxdt * d
            ddx_s[:, sl] = dxdt * x
            dax_s[:, sl] = da_l + jnp.where(rowi == CHUNK - 1, dat, 0.0)
            dskp = jnp.sum(dyp * x, axis=0, keepdims=True)
            ddsk_ref[...] += _dot2(jnp.broadcast_to(dskp, (8, LANES)), e[:, sl], _NT)
        dxc_ref[:, 8 * LANES:9 * LANES] = d_b
        dxc_ref[:, 9 * LANES:10 * LANES] = d_c
        dacs = dacs + _dot2(dax_s[...], e, _NT)
        ddt = _dot2(ddx_s[...], e, _NT)
        dda = _mdot3(triu, dacs)
        ddt = ddt + dda * arow
        dalog_ref[...] += jnp.sum(dda * dt, axis=0, keepdims=True) * arow
        ddtr = jnp.where(lane < SSD_HEADS, ddt * _sigmoid(dtr + dtb), 0.0)
        ddtr_ref[...] = ddtr
        ddtb_ref[...] += jnp.sum(ddtr, axis=0, keepdims=True)

    one = pl.BlockSpec((1, LANES), lambda i: (0, 0))
    rev = lambda c: (lambda i: (nc - 1 - i, c))
    return pl.pallas_call(
        body, name=name, grid=(nc,),
        in_specs=[pl.BlockSpec((CHUNK, 8 * LANES), rev(0)), pl.BlockSpec((CHUNK, LANES), rev(8)),
                  pl.BlockSpec((CHUNK, LANES), rev(9)), pl.BlockSpec((CHUNK, LANES), rev(C_DT // LANES)),
                  one, one, one,
                  pl.BlockSpec((1, SSD_PAIRS, LANES, LANES), lambda i: (nc - 1 - i, 0, 0, 0)),
                  pl.BlockSpec((CHUNK, 8 * LANES), rev(0))],
        out_specs=[pl.BlockSpec((CHUNK, XBC_COLS), rev(0)), pl.BlockSpec((CHUNK, LANES), rev(0)), one, one,
                   pl.BlockSpec((8, LANES), lambda i: (0, 0))],
        out_shape=[jax.ShapeDtypeStruct((t, XBC_COLS), F32), jax.ShapeDtypeStruct((t, LANES), F32)]
        + [jax.ShapeDtypeStruct((1, LANES), F32)] * 2 + [jax.ShapeDtypeStruct((8, LANES), F32)],
        scratch_shapes=[pltpu.VMEM((CHUNK, 8 * LANES), F32), pltpu.VMEM((LANES, CHUNK), F32),
                        pltpu.VMEM((SSD_PAIRS, LANES, LANES), F32),
                        pltpu.VMEM((CHUNK, 8 * LANES), F32), pltpu.VMEM((CHUNK, 8 * LANES), F32)],
        compiler_params=_params(("arbitrary",)),
    )(xc, xc, xc, proj, dtb, alog, dsk, hin_all, dy)


RW_C = 64


def _p3(a, b, dn):
    ah, al = _split2(a)
    bh, bl = _split2(b)
    d = lambda x, y: lax.dot_general(x, y, dn, preferred_element_type=F32)
    return d(ah, bh) + d(ah, bl) + d(al, bh)


_BNN = (((2,), (1,)), ((0,), (0,)))
_BNT = (((2,), (2,)), ((0,), (0,)))
_BTN = (((1,), (1,)), ((0,), (0,)))


@jax.custom_vjp
def _pnn(a, b):
    return _p3(a, b, _BNN)


@jax.custom_vjp
def _pnt(a, b):
    return _p3(a, b, _BNT)


@jax.custom_vjp
def _ptn(a, b):
    return _p3(a, b, _BTN)


_pnn.defvjp(lambda a, b: (_p3(a, b, _BNN), (a, b)), lambda res, g: (_p3(g, res[1], _BNT), _p3(res[0], g, _BTN)))
_pnt.defvjp(lambda a, b: (_p3(a, b, _BNT), (a, b)), lambda res, g: (_p3(g, res[1], _BNN), _p3(g, res[0], _BTN)))
_ptn.defvjp(lambda a, b: (_p3(a, b, _BTN), (a, b)), lambda res, g: (_p3(res[1], g, _BNT), _p3(res[0], g, _BNN)))


def _tri2(tril, x, dn):
    hi, lo = _split2(x)
    m = tril.astype(BF16)
    return (lax.dot_general(m, hi, dn, preferred_element_type=F32) + lax.dot_general(m, lo, dn, preferred_element_type=F32))


@jax.custom_vjp
def _cumsum_rows(tril, x):
    return _tri2(tril, x, _BNN)


_cumsum_rows.defvjp(lambda tril, x: (_tri2(tril, x, _BNN), tril),
                    lambda tril, g: (jnp.zeros_like(tril), _tri2(tril, g, _BTN)))


def _rw_chunk_consts():
    c2 = 2 * RW_C
    row = lax.broadcasted_iota(jnp.int32, (c2, c2), 0)
    col = lax.broadcasted_iota(jnp.int32, (c2, c2), 1)
    same = (row // RW_C) == (col // RW_C)
    strict = (same & (row > col)).astype(F32)
    incl = (same & (row >= col)).astype(F32)
    eye = (row == col).astype(F32)
    tr = lax.broadcasted_iota(jnp.int32, (RW_C, RW_C), 0)
    tc = lax.broadcasted_iota(jnp.int32, (RW_C, RW_C), 1)
    tril = (tr >= tc).astype(F32)
    lane = lax.broadcasted_iota(jnp.int32, (1, LANES), 1)
    hm = [(lane // HEAD == h).astype(F32) for h in (0, 1)]
    return strict, incl, eye, tril, hm


def _rw_chunk(r, lw, k, v, n, b, s2, consts):
    strict, incl, eye, tril, hm = consts
    two = lambda x: jnp.concatenate([x * hm[0], x * hm[1]], axis=1)
    cum = _cumsum_rows(jnp.broadcast_to(tril, (4, RW_C, RW_C)), lw)
    grow, shrink = jnp.exp(-cum), jnp.exp(cum)
    n2, r2 = two(n * jnp.exp(cum - lw)), two(r * shrink)
    b2, k2, v2 = two(b * grow), two(k * grow), two(v)
    p = _pnt(n2, b2) * strict
    x2 = _pnt(n2, s2) + _pnn(_pnt(n2, k2) * strict, v2)
    t_inv, a = eye + p, p
    for _ in range(RW_C.bit_length() - 2):
        a = _pnn(a, a)
        t_inv = t_inv + _pnn(t_inv, a)
    u2 = _pnn(t_inv, x2)
    y2 = _pnt(r2, s2) + _pnn(_pnt(r2, b2) * incl, u2) + _pnn(_pnt(r2, k2) * incl, v2)
    s2_new = (s2 + _ptn(u2, b2) + _ptn(v2, k2)) * jnp.exp(jnp.sum(lw, axis=1, keepdims=True))
    return jnp.sum(y2.reshape(4, 2, RW_C, LANES), axis=1), s2_new


def _pairs(ref):
    return jnp.stack([ref[:, p * LANES:(p + 1) * LANES] for p in range(4)])


def _rw_chunk_fwd(mixed, lw, k, n, b, *, name, side=None):
    t = lw.shape[0]
    nc = t // RW_C

    def body(r_ref, v_ref, lw_ref, k_ref, n_ref, b_ref, y_ref, sin_ref, s_s):
        @pl.when(pl.program_id(0) == 0)
        def _():
            s_s[...] = jnp.zeros_like(s_s)

        s2 = s_s[...]
        sin_ref[0] = s2
        y, s2 = _rw_chunk(*[_pairs(x) for x in (r_ref, lw_ref, k_ref, v_ref, n_ref, b_ref)], s2, _rw_chunk_consts())
        for p in range(4):
            y_ref[:, p * LANES:(p + 1) * LANES] = y[p]
        s_s[...] = s2

    blk = lambda c: pl.BlockSpec((RW_C, 4 * LANES), functools.partial(lambda i, c: (i, c), c=c))
    return _call_with_side(
        body, side, name=name, grid=(nc,), semantics=("arbitrary",),
        in_specs=[blk(0), blk(2), blk(0), blk(0), blk(0), blk(0)],
        out_specs=[blk(0), pl.BlockSpec((1, 4, LANES, LANES), lambda i: (i, 0, 0, 0))],
        out_shape=[jax.ShapeDtypeStruct((t, 4 * LANES), F32), jax.ShapeDtypeStruct((nc, 4, LANES, LANES), F32)],
        scratch_shapes=[pltpu.VMEM((4, LANES, LANES), F32)],
        operands=(mixed, mixed, lw, k, n, b))


def _call_with_side(body, side, *, name, grid, semantics, in_specs, out_specs, out_shape, scratch_shapes, operands):
    if side is None:
        return pl.pallas_call(body, name=name, grid=grid, in_specs=in_specs, out_specs=out_specs, out_shape=out_shape,
                              scratch_shapes=scratch_shapes, compiler_params=_params(semantics))(*operands)
    srcs, per_dest = side
    ns, ni, no, nscr = len(srcs), len(in_specs), len(out_specs), len(scratch_shapes)

    def full_body(*refs):
        ins, side_in = refs[:ni], refs[ni:ni + ns]
        outs, side_out = refs[ni + ns:ni + ns + no], refs[ni + ns + no:ni + 2 * ns + no]
        scratch, sems = refs[ni + 2 * ns + no:ni + 2 * ns + no + nscr], refs[ni + 2 * ns + no + nscr:]

        ids = [pl.program_id(a) for a in range(len(grid))]
        first = functools.reduce(jnp.logical_and, [i == 0 for i in ids])
        last = functools.reduce(jnp.logical_and, [i == n - 1 for i, n in zip(ids, grid)])

        @pl.when(first)
        def _():
            _exchange(side_in, side_out, sems, per_dest, start=True, wait=False)

        body(*ins, *outs, *scratch)

        @pl.when(last)
        def _():
            _exchange(side_in, side_out, sems, per_dest, start=False, wait=True)

    res = pl.pallas_call(
        full_body, name=name, grid=grid, in_specs=list(in_specs) + [_ANY] * ns,
        out_specs=list(out_specs) + [_ANY] * ns, out_shape=list(out_shape) + _exchange_out_shapes(srcs),
        scratch_shapes=list(scratch_shapes) + _exchange_sems(ns), compiler_params=_params(("arbitrary",) * len(grid)),
    )(*operands, *srcs)
    return list(res[:no]) + [list(res[no:])]


def _rw_chunk_bwd(mixed, lw, k, n, b, s_in, dy, dr0, dk0, dv0, *, name, side=None):
    t = lw.shape[0]
    nc = t // RW_C

    def body(r_ref, v_ref, lw_ref, k_ref, n_ref, b_ref, sin_ref, dy_ref, dr0_ref, dk0_ref, dv0_ref,
             dr_ref, dlw_ref, dk_ref, dv_ref, dn_ref, db_ref, ds_s):
        @pl.when(pl.program_id(0) == 0)
        def _():
            ds_s[...] = jnp.zeros_like(ds_s)

        consts = _rw_chunk_consts()
        args = [_pairs(x) for x in (r_ref, lw_ref, k_ref, v_ref, n_ref, b_ref)] + [sin_ref[0]]
        _, vjp = jax.vjp(lambda *a: _rw_chunk(*a, consts), *args)
        dr, dlw, dk, dv, dn, db, ds = vjp((_pairs(dy_ref), ds_s[...]))
        for p in range(4):
            sl = slice(p * LANES, (p + 1) * LANES)
            dr_ref[:, sl] = dr[p] + dr0_ref[:, sl]
            dlw_ref[:, sl] = dlw[p]
            dk_ref[:, sl] = dk[p] + dk0_ref[:, sl]
            dv_ref[:, sl] = dv[p] + dv0_ref[:, sl]
            dn_ref[:, sl] = dn[p]
            db_ref[:, sl] = db[p]
        ds_s[...] = ds

    blk = lambda c: pl.BlockSpec((RW_C, 4 * LANES), functools.partial(lambda i, c: (nc - 1 - i, c), c=c))
    return _call_with_side(
        body, side, name=name, grid=(nc,), semantics=("arbitrary",),
        in_specs=[blk(0), blk(2), blk(0), blk(0), blk(0), blk(0),
                  pl.BlockSpec((1, 4, LANES, LANES), lambda i: (nc - 1 - i, 0, 0, 0)), blk(0), blk(0), blk(0), blk(0)],
        out_specs=[blk(0)] * 6,
        out_shape=[jax.ShapeDtypeStruct((t, 4 * LANES), F32)] * 6,
        scratch_shapes=[pltpu.VMEM((4, LANES, LANES), F32)],
        operands=(mixed, mixed, lw, k, n, b, s_in, dy, dr0, dk0, dv0))


def _f_rms_res(x, g):
    return _f_rms(x, g)[0], x


def _final(x, g, target, *, bt, name):
    t, d = x.shape

    def body(x_ref, g_ref, t_ref, dx_ref, loss_ref, dg_ref):
        tgt = t_ref[...]

        def f(xv, gv):
            err = _f_rms(xv, gv)[0] - tgt
            return 0.5 * jnp.mean(err * err, axis=-1, keepdims=True)

        row_loss, vjp = jax.vjp(f, x_ref[...], g_ref[...])
        dx, dg = vjp(jnp.ones_like(row_loss))
        dx_ref[...] = dx

        @pl.when(pl.program_id(0) == 0)
        def _():
            loss_ref[...] = jnp.zeros_like(loss_ref)
            dg_ref[...] = jnp.zeros_like(dg_ref)

        loss_ref[...] += jnp.broadcast_to(jnp.sum(row_loss, axis=0, keepdims=True), (1, LANES))
        dg_ref[...] += dg

    blk = pl.BlockSpec((bt, d), lambda i: (i, 0))
    return pl.pallas_call(
        body, name=name, grid=(t // bt,),
        in_specs=[blk, pl.BlockSpec((1, d), lambda i: (0, 0)), blk],
        out_specs=[blk, pl.BlockSpec((1, LANES), lambda i: (0, 0)), pl.BlockSpec((1, d), lambda i: (0, 0))],
        out_shape=[jax.ShapeDtypeStruct((t, d), F32), jax.ShapeDtypeStruct((1, LANES), F32),
                   jax.ShapeDtypeStruct((1, d), F32)],
        compiler_params=_params(("arbitrary",)),
    )(x, g, target)


ADAMW_BLOCK_BYTES = 1 << 20


def _adamw(w, g, m, v, *, name, block=None):
    shape = w.shape
    if block is not None:
        return _adamw_blocks(w, g, m, v, block, name)
    c = shape[-1]
    shape3 = (1,) * (3 - len(shape)) + shape if len(shape) <= 3 else (-1,) + shape[-2:]
    args = [a.reshape(shape3) for a in (w, g, m, v)]
    lead, r, _ = args[0].shape
    br = r
    if r * c * 4 > ADAMW_BLOCK_BYTES:
        cands = [b for b in range(8, r, 8) if r % b == 0 and b * c * 4 <= ADAMW_BLOCK_BYTES]
        br = max(cands) if cands else r
    outs = _adamw_blocks(*args, (1, br, c), name)
    return tuple(o.reshape(shape) for o in outs)


def _adamw_blocks(w, g, m, v, block, name):
    shape = w.shape
    assert all(s % b == 0 for s, b in zip(shape, block))

    def body(w_ref, g_ref, m_ref, v_ref, d_ref, nm_ref, nv_ref):
        gv = g_ref[...]
        m_new = ADAM_B1 * m_ref[...] + (1.0 - ADAM_B1) * gv
        v_new = ADAM_B2 * v_ref[...] + (1.0 - ADAM_B2) * (gv * gv)
        m_hat = m_new / (1.0 - ADAM_B1 ** ADAM_STEP)
        v_hat = v_new / (1.0 - ADAM_B2 ** ADAM_STEP)
        d_ref[...] = -ADAM_LR * (m_hat / (jnp.sqrt(v_hat) + ADAM_EPS) + ADAM_WD * w_ref[...])
        nm_ref[...] = m_new
        nv_ref[...] = v_new

    blk = pl.BlockSpec(tuple(block), lambda *ids: ids)
    return pl.pallas_call(
        body, name=name, grid=tuple(s // b for s, b in zip(shape, block)), in_specs=[blk] * 4, out_specs=[blk] * 3,
        out_shape=[jax.ShapeDtypeStruct(shape, F32)] * 3,
        compiler_params=_params(("parallel",) * len(shape)),
    )(w, g, m, v)


BT = 256
BC = 128


def _layer_rows(x, proj, s):
    s = {k: s.get(k) for k in ("y_sb_raw", "y_ssd_raw", "mixed", "ys", "k2", "p_sb", "p_ssd", "p_rw")}
    return dict(
        rms=[(x, D_MODEL, 0)],
        sb_gate=[(s["y_sb_raw"], 512, 0), (proj, 512, 3)],
        ssd_norm=[(s["y_ssd_raw"], 1024, 0), (proj, 1024, C_Z // 1024)],
        rw_pre=[(s["mixed"], 512, 1), (s["mixed"], LANES, 16)],
        rw_post=[(s["ys"], 512, 0), (s["mixed"], 512, 0), (s["k2"], 512, 0), (s["mixed"], 512, 2), (s["mixed"], 512, 3)],
        merge=[(s["p_sb"], 1024, 0), (s["p_ssd"], 1024, 0), (s["p_rw"], 1024, 0),
               (proj, 1024, 3), (proj, 1024, 4), (proj, 1024, 5)],
    )


def _layer_fwd(x, p, nm, side=None):
    s = {}
    (s["h"],) = _rowwise(_f_rms, [(x, D_MODEL, 0)], [p["norm_g"]], [D_MODEL], bt=BT, name=nm + "rms")
    proj = s["proj"] = _mm(s["h"], p["w_in"], name=nm + "proj")
    s["y_sb_raw"], s["lt"] = _sb2_fwd(proj, name=nm + "sb")
    s["xc"] = _colwise(_f_conv, proj, C_XBC, XBC_COLS, p["conv"], bc=BC, name=nm + "conv")
    s["y_ssd_raw"], s["hin"] = _ssd_fwd(s["xc"], proj, p["dt_bias"], p["a_log"], p["d_skip"], name=nm + "ssd")
    s["mixed"] = _colwise(_f_rw_mix, proj, C_RW, RW_COLS, [p["rw_mu"]], bc=BC, name=nm + "mix")
    s["w"], s["k2"], s["n"], s["b"] = _rowwise(_f_rw_pre, [(s["mixed"], 512, 1), (s["mixed"], LANES, 16)], p["rw_pre"],
                                               [512] * 4, bt=BT, name=nm + "rwpre")
    s["ys"], s["st"], *exchanged = _rw_chunk_fwd(s["mixed"], s["w"], s["k2"], s["n"], s["b"], name=nm + "scan", side=side)
    rows = _layer_rows(x, proj, s)
    (s["y_sb"],) = _rowwise(_f_sb_gate, rows["sb_gate"], [], [512], bt=BT, name=nm + "sbgate")
    (s["y_ssd"],) = _rowwise(_f_ssd_norm, rows["ssd_norm"], [p["ssd_norm_g"]], [1024], bt=BT, name=nm + "ssdnorm")
    (s["y_rw"],) = _rowwise(_f_rw_post, rows["rw_post"], p["rw_post"], [512], bt=BT, name=nm + "rwpost")
    s["p_sb"] = _mm(s["y_sb"], p["w_out_sb"], name=nm + "osb")
    s["p_ssd"] = _mm(s["y_ssd"], p["w_out_ssd"], name=nm + "ossd")
    s["p_rw"] = _mm(s["y_rw"], p["w_out_rw"], name=nm + "orw")
    (s["merged"],) = _rowwise(_f_merge, _layer_rows(x, proj, s)["merge"], [], [1024], bt=BT, name=nm + "merge")
    return _mm(s["merged"], p["w_o"], add=x, name=nm + "wo"), s, (exchanged[0] if exchanged else None)


def _layer_bwd(x, dx_out, p, s, nm, side=None, side_late=None):
    g = {}
    proj = s["proj"]
    rows = _layer_rows(x, proj, s)
    g["w_o"] = _mm(s["merged"], dx_out, ta=True, name=nm + "g_wo")
    d_merged = _mm(dx_out, p["w_o"], tb=True, name=nm + "d_merged")
    dp_sb, dp_ssd, dp_rw, d_gates = _rowwise_bwd(_f_merge, rows["merge"], [], [(d_merged, 1024, 0)], bt=BT,
                                                 name=nm + "merge_b", groups=[[0], [1], [2], [3, 4, 5]])
    g["w_out_sb"] = _mm(s["y_sb"], dp_sb, ta=True, name=nm + "g_osb")
    g["w_out_ssd"] = _mm(s["y_ssd"], dp_ssd, ta=True, name=nm + "g_ossd")
    g["w_out_rw"] = _mm(s["y_rw"], dp_rw, ta=True, name=nm + "g_orw")
    dy_sb = _mm(dp_sb, p["w_out_sb"], tb=True, name=nm + "d_ysb")
    dy_ssd = _mm(dp_ssd, p["w_out_ssd"], tb=True, name=nm + "d_yssd")
    dy_rw = _mm(dp_rw, p["w_out_rw"], tb=True, name=nm + "d_yrw")
    dy_sb_raw, d_sbgate = _rowwise_bwd(_f_sb_gate, rows["sb_gate"], [], [(dy_sb, 512, 0)], bt=BT, name=nm + "sbgate_b")
    dq, dk, dv = _sb2_bwd(proj, dy_sb_raw, s["lt"], name=nm + "sb_b")
    dy_ssd_raw, dz, g["ssd_norm_g"] = _rowwise_bwd(_f_ssd_norm, rows["ssd_norm"], [p["ssd_norm_g"]],
                                                   [(dy_ssd, 1024, 0)], bt=BT, name=nm + "ssdnorm_b")
    dxc, ddtr, g["dt_bias"], g["a_log"], g["d_skip"] = _ssd_bwd(
        s["xc"], proj, p["dt_bias"], p["a_log"], p["d_skip"], s["hin"], dy_ssd_raw, name=nm + "ssd_b")
    conv_out = _colwise_bwd(_f_conv, proj, C_XBC, XBC_COLS, p["conv"], dxc, bc=BC, name=nm + "conv_b")
    dxbc, g["conv"] = conv_out[0], conv_out[1:]
    dys, dr0, dk0, dv0, d_rwgate, g["rw_ln_g"], g["rw_ln_b"], g["rw_r_k"] = _rowwise_bwd(
        _f_rw_post, rows["rw_post"], p["rw_post"], [(dy_rw, 512, 0)], bt=BT, name=nm + "rwpost_b")
    dr, dw, dk2, dvv, dn, db, *exchanged = _rw_chunk_bwd(s["mixed"], s["w"], s["k2"], s["n"], s["b"], s["st"], dys,
                                                         dr0, dk0, dv0, name=nm + "scan_b",
                                                         side=side(g) if side else None)
    pre_out = _rowwise_bwd(_f_rw_pre, rows["rw_pre"], p["rw_pre"],
                           [(dw, 512, 0), (dk2, 512, 0), (dn, 512, 0), (db, 512, 0)], bt=BT, name=nm + "rwpre_b")
    dkm, dlo, g["rw_pre"] = pre_out[0], pre_out[1], pre_out[2:]
    d_mixed = jnp.concatenate([dr, dkm, dvv, d_rwgate, dlo], axis=1)
    d_slab, g["rw_mu"] = _colwise_bwd(_f_rw_mix, proj, C_RW, RW_COLS, [p["rw_mu"]], d_mixed, bc=BC, name=nm + "mix_b")
    d_proj = jnp.concatenate([dq, dk, dv, d_sbgate, dz, d_gates, d_slab, ddtr, dxbc], axis=1)
    g["w_in"] = _mm(s["h"], d_proj, ta=True, name=nm + "g_win")
    dh = _mm(d_proj, p["w_in"], tb=True, tn=1024, tk=512, name=nm + "d_h", side=side_late(g) if side_late else None)
    dh, late = dh if side_late else (dh, None)
    dx, g["norm_g"] = _rowwise_bwd(_f_rms_res, rows["rms"], [p["norm_g"]], [(dh, D_MODEL, 0), (dx_out, D_MODEL, 0)],
                                   bt=BT, name=nm + "rms_b")
    return dx, g, (exchanged[0] if exchanged else None), late


MESH = pl.DeviceIdType.MESH
N_DEV = 8
_ANY = pl.BlockSpec(memory_space=pl.ANY)


def _here():
    x, y, c = lax.axis_index("x"), lax.axis_index("y"), lax.axis_index("c")
    return x, y, c, [(1 - x, y), (x, 1 - y), (1 - x, 1 - y)]


def _chip_exchange(srcs, *, per_dest, name):
    n = len(srcs)

    def body(*refs):
        _exchange(refs[:n], refs[n:2 * n], refs[2 * n:], per_dest, start=True, wait=True)

    return pl.pallas_call(
        body, name=name, in_specs=[_ANY] * n, out_specs=[_ANY] * n,
        out_shape=_exchange_out_shapes(srcs), scratch_shapes=_exchange_sems(n),
    )(*srcs)


def _exchange_out_shapes(srcs):
    return [jax.ShapeDtypeStruct((4,) + s.shape[1:], s.dtype) for s in srcs]


def _exchange_sems(n):
    return [pltpu.SemaphoreType.DMA((3 * n,)), pltpu.SemaphoreType.DMA((3 * n,)), pltpu.SemaphoreType.DMA((n,))]


def _exchange(src_refs, out_refs, sems, per_dest, *, start, wait):
    send_sems, recv_sems, local_sems = sems
    x, y, c, chips = _here()
    me = 2 * x + y
    owns, sends, recvs = [], [], []
    for a, (src_ref, out_ref) in enumerate(zip(src_refs, out_refs)):
        pick = (lambda q, s=src_ref: s.at[q]) if per_dest else (lambda q, s=src_ref: s.at[c])
        owns.append(pltpu.make_async_copy(pick(me), out_ref.at[me], local_sems.at[a]))
        for j, (px, py) in enumerate(chips):
            sends.append(pltpu.make_async_remote_copy(
                pick(2 * px + py), out_ref.at[me], send_sems.at[3 * a + j], recv_sems.at[3 * a + j],
                device_id=(px, py, c), device_id_type=MESH))
            recvs.append(pltpu.make_async_remote_copy(
                src_ref.at[0], out_ref.at[2 * px + py], send_sems.at[3 * a + j], recv_sems.at[3 * a + j],
                device_id=(px, py, c), device_id_type=MESH))
    if start:
        for cp in owns + sends:
            cp.start()
    if wait:
        for cp in recvs:
            cp.wait_recv()
        for cp in sends:
            cp.wait_send()
        for cp in owns:
            cp.wait()


def _sibling_swap(srcs, *, other_slot, name):
    n = len(srcs)

    def body(*refs):
        src_refs, out_refs, send_sems, recv_sems = refs[:n], refs[n:2 * n], refs[2 * n], refs[2 * n + 1]
        x, y, c, _ = _here()
        copies = [pltpu.make_async_remote_copy(s.at[1 - c] if other_slot else s, o, send_sems.at[a], recv_sems.at[a],
                                               device_id=(x, y, 1 - c), device_id_type=MESH)
                  for a, (s, o) in enumerate(zip(src_refs, out_refs))]
        for cp in copies:
            cp.start()
        for cp in copies:
            cp.wait()

    return pl.pallas_call(
        body, name=name, in_specs=[_ANY] * n, out_specs=[_ANY] * n,
        out_shape=[jax.ShapeDtypeStruct(s.shape[1:] if other_slot else s.shape, s.dtype) for s in srcs],
        scratch_shapes=[pltpu.SemaphoreType.DMA((n,)), pltpu.SemaphoreType.DMA((n,))],
    )(*srcs)


def _allgather_small(v, *, reduce, name):
    r = v.shape[0]

    def body(v_ref, out_ref, *rest):
        send_sems, recv_sems, local_sem = rest[-3:]
        x, y, c, chips = _here()
        me, sibling = (x, y, c), (x, y, 1 - c)

        def slot(px, py, pc):
            return out_ref.at[4 * px + 2 * py + pc]

        def copy(k, block, to, src=None):
            return pltpu.make_async_remote_copy(
                src_ref=slot(*block) if src is None else src, dst_ref=slot(*block),
                send_sem=send_sems.at[k], recv_sem=recv_sems.at[k], device_id=to, device_id_type=MESH)

        mine = pltpu.make_async_copy(v_ref, slot(*me), local_sem)
        mine.start()
        first = [copy(0, me, sibling, src=v_ref)]
        first += [copy(1 + j, me, (*chip, c), src=v_ref) for j, chip in enumerate(chips)]
        for cp in first:
            cp.start()
        passed = [copy(4 + j, (*chip, c), sibling) for j, chip in enumerate(chips)]
        for j, chip in enumerate(chips):
            copy(1 + j, (*chip, c), me).wait_recv()
            passed[j].start()
        copy(0, sibling, me).wait_recv()
        for j, chip in enumerate(chips):
            copy(4 + j, (*chip, 1 - c), me).wait_recv()
        for cp in first + passed:
            cp.wait_send()
        mine.wait()
        if reduce:
            total = out_ref[0]
            for d in range(1, N_DEV):
                total = total + out_ref[d]
            rest[0][...] = total

    vm = pl.BlockSpec(memory_space=pltpu.VMEM)
    out_shape = [jax.ShapeDtypeStruct((N_DEV, r, LANES), F32)] + ([jax.ShapeDtypeStruct((r, LANES), F32)] if reduce else [])
    return pl.pallas_call(
        body, name=name, in_specs=[vm], out_specs=[vm] * len(out_shape), out_shape=out_shape,
        scratch_shapes=[pltpu.SemaphoreType.DMA((7,)), pltpu.SemaphoreType.DMA((7,)), pltpu.SemaphoreType.DMA],
        compiler_params=pltpu.CompilerParams(vmem_limit_bytes=VMEM_LIMIT),
    )(v)


REDUCE_BLOCK_BYTES = 2 << 20


def _reduce_rows(r, c):
    cands = [b for b in range(16, r + 1, 16) if r % b == 0 and b * c * 4 <= REDUCE_BLOCK_BYTES]
    return max(cands)


def _add_halves(mine2, other, c_idx, *, name):
    _, nq, r, c = mine2.shape
    br = _reduce_rows(r, c)

    def body(c_ref, a_ref, b_ref, o_ref):
        o_ref[...] = (a_ref[0] + b_ref[...]).astype(o_ref.dtype)

    blk = pl.BlockSpec((1, br, c), lambda q, i, c_ref: (q, i, 0))
    return pl.pallas_call(
        body, name=name,
        grid_spec=pltpu.PrefetchScalarGridSpec(
            num_scalar_prefetch=1, grid=(nq, r // br),
            in_specs=[pl.BlockSpec((1, 1, br, c), lambda q, i, c_ref: (c_ref[0], q, i, 0)), blk],
            out_specs=blk),
        out_shape=jax.ShapeDtypeStruct((nq, r, c), BF16),
        compiler_params=_params(("parallel", "parallel")),
    )(c_idx, mine2, other)


def _sum_chips(parts, *, name):
    _, r, c = parts.shape
    br = _reduce_rows(r, c)

    def body(p_ref, o_ref):
        total = p_ref[0].astype(F32)
        for q in range(1, 4):
            total = total + p_ref[q].astype(F32)
        o_ref[...] = total

    return pl.pallas_call(
        body, name=name, grid=(r // br,),
        in_specs=[pl.BlockSpec((4, br, c), lambda i: (0, i, 0))],
        out_specs=pl.BlockSpec((br, c), lambda i: (i, 0)),
        out_shape=jax.ShapeDtypeStruct((r, c), F32),
        compiler_params=_params(("parallel",)),
    )(parts)


BIG = ("w_in", "w_out_sb", "w_out_ssd", "w_out_rw", "w_o")
BIG_AXIS = {"w_in": 2, "w_out_sb": 2, "w_out_ssd": 1, "w_out_rw": 2, "w_o": 1}
SMALL_SHARDED = {"conv_w": 320, "rw_w_up": 128, "rw_a_up": 128}
SMALL = ("norm_g", "conv_w", "conv_b", "dt_bias", "a_log", "d_skip", "ssd_norm_g", "rw_mu", "rw_w0", "rw_w_up",
         "rw_a0", "rw_a_up", "rw_k_k", "rw_k_a", "rw_r_k", "rw_ln_g", "rw_ln_b", "final_g")


def _rows_of(a):
    flat = a.reshape(-1)
    pad = (-flat.shape[0]) % LANES
    return jnp.pad(flat, (0, pad)).reshape(-1, LANES)


def _pack_rows(arrays, multiple=8):
    rows = jnp.concatenate([_rows_of(a) for a in arrays], axis=0)
    pad = (-rows.shape[0]) % multiple
    return jnp.pad(rows, ((0, pad), (0, 0)))


def _unpack_rows(rows, shapes):
    out, off = [], 0
    for shp in shapes:
        n = 1
        for d in shp:
            n *= d
        nr = -(-n // LANES)
        out.append(rows[off:off + nr].reshape(-1)[:n].reshape(shp))
        off += nr
    return out


COL_MAP = ((0, 3072, 0), (3072, 4352, C_XBC), (4352, 4368, C_DT), (4368, 6544, C_RW), (6544, 9616, C_GATES))
SHARD_COLS = N_IN // 4


def _w_in_from_shards(shards):
    pieces = []
    for a, b, dst in sorted(COL_MAP, key=lambda m: m[2]):
        if pieces and dst > pieces[-1][0]:
            pieces.append((dst, jnp.zeros((shards[0].shape[0], dst - pieces[-1][0]), shards[0].dtype)))
        for q in range(4):
            lo, hi = max(a, q * SHARD_COLS), min(b, (q + 1) * SHARD_COLS)
            if lo < hi:
                pieces.append((dst + hi - a, shards[q][:, lo - q * SHARD_COLS:hi - q * SHARD_COLS]))
    return jnp.concatenate([p for _, p in pieces], axis=1)


def _w_in_shard(g, q):
    pieces = []
    for a, b, dst in COL_MAP:
        lo, hi = max(a, q * SHARD_COLS), min(b, (q + 1) * SHARD_COLS)
        if lo < hi:
            pieces.append(g[:, dst + lo - a:dst + hi - a])
    return jnp.concatenate(pieces, axis=1)


def _row_halves(a):
    return a.reshape(2, a.shape[0] // 2, a.shape[1])


def _join_halves(core, mine, theirs):
    return jnp.where(core == 0, jnp.concatenate([mine, theirs], axis=-2), jnp.concatenate([theirs, mine], axis=-2))


def kernel(x, norm_g, w_in, conv_w, conv_b, dt_bias, a_log, d_skip, ssd_norm_g, rw_mu, rw_w0, rw_w_up, rw_a0, rw_a_up, rw_k_k, rw_k_a, rw_r_k, rw_ln_g, rw_ln_b, w_out_sb, w_out_ssd, w_out_rw, w_o, final_g, loss_target, m_norm_g, m_w_in, m_conv_w, m_conv_b, m_dt_bias, m_a_log, m_d_skip, m_ssd_norm_g, m_rw_mu, m_rw_w0, m_rw_w_up, m_rw_a0, m_rw_a_up, m_rw_k_k, m_rw_k_a, m_rw_r_k, m_rw_ln_g, m_rw_ln_b, m_w_out_sb, m_w_out_ssd, m_w_out_rw, m_w_o, m_final_g, v_norm_g, v_w_in, v_conv_w, v_conv_b, v_dt_bias, v_a_log, v_d_skip, v_ssd_norm_g, v_rw_mu, v_rw_w0, v_rw_w_up, v_rw_a0, v_rw_a_up, v_rw_k_k, v_rw_k_a, v_rw_r_k, v_rw_ln_g, v_rw_ln_b, v_w_out_sb, v_w_out_ssd, v_w_out_rw, v_w_o, v_final_g):
    names = ("norm_g", "w_in", "conv_w", "conv_b", "dt_bias", "a_log", "d_skip", "ssd_norm_g", "rw_mu", "rw_w0",
             "rw_w_up", "rw_a0", "rw_a_up", "rw_k_k", "rw_k_a", "rw_r_k", "rw_ln_g", "rw_ln_b", "w_out_sb",
             "w_out_ssd", "w_out_rw", "w_o", "final_g")
    w_loc = dict(zip(names, (norm_g, w_in, conv_w, conv_b, dt_bias, a_log, d_skip, ssd_norm_g, rw_mu, rw_w0, rw_w_up,
                             rw_a0, rw_a_up, rw_k_k, rw_k_a, rw_r_k, rw_ln_g, rw_ln_b, w_out_sb, w_out_ssd, w_out_rw,
                             w_o, final_g)))
    m_loc = dict(zip(names, (m_norm_g, m_w_in, m_conv_w, m_conv_b, m_dt_bias, m_a_log, m_d_skip, m_ssd_norm_g,
                             m_rw_mu, m_rw_w0, m_rw_w_up, m_rw_a0, m_rw_a_up, m_rw_k_k, m_rw_k_a, m_rw_r_k,
                             m_rw_ln_g, m_rw_ln_b, m_w_out_sb, m_w_out_ssd, m_w_out_rw, m_w_o, m_final_g)))
    v_loc = dict(zip(names, (v_norm_g, v_w_in, v_conv_w, v_conv_b, v_dt_bias, v_a_log, v_d_skip, v_ssd_norm_g,
                             v_rw_mu, v_rw_w0, v_rw_w_up, v_rw_a0, v_rw_a_up, v_rw_k_k, v_rw_k_a, v_rw_r_k,
                             v_rw_ln_g, v_rw_ln_b, v_w_out_sb, v_w_out_ssd, v_w_out_rw, v_w_o, v_final_g)))
    chip = 2 * lax.axis_index("x") + lax.axis_index("y")
    core = lax.axis_index("c")

    def gather_srcs(i):
        return [_row_halves(w_loc[n][i].astype(BF16)) for n in BIG]

    def gathered(mine, nm):
        theirs = _sibling_swap(mine, other_slot=False, name=nm)
        out = {}
        for n, a, b in zip(BIG, mine, theirs):
            shards = _join_halves(core, a, b)
            out[n] = (_w_in_from_shards([shards[q] for q in range(4)]) if n == "w_in"
                      else jnp.concatenate([shards[q] for q in range(4)], axis=BIG_AXIS[n] - 1))
        return out

    full = {}
    sm_names = tuple(SMALL_SHARDED)
    sm_shapes = [w_loc[n].shape for n in sm_names]
    (got_sm,) = _allgather_small(_pack_rows([w_loc[n] for n in sm_names]), reduce=False, name="gather_small")
    per_chip = [_unpack_rows(got_sm[4 * (q // 2) + 2 * (q % 2)], sm_shapes) for q in range(4)]
    for i, n in enumerate(sm_names):
        full[n] = jnp.concatenate([per_chip[q][i] for q in range(4)], axis=-1)

    def pad16(a):
        return jnp.zeros((1, LANES), F32).at[0, :SSD_HEADS].set(a)

    def layer_params(i, big):
        row = lambda n: w_loc[n][i].reshape(1, -1)
        cw = full["conv_w"][i]
        return dict(
            norm_g=row("norm_g"), w_in=big["w_in"], conv=[cw[k][None] for k in range(4)] + [row("conv_b")],
            dt_bias=pad16(dt_bias[i]), a_log=pad16(a_log[i]), d_skip=pad16(d_skip[i]),
            ssd_norm_g=row("ssd_norm_g"), rw_mu=row("rw_mu"),
            rw_pre=[row("rw_w0"), jnp.zeros((LANES, 512), F32).at[:HEAD].set(full["rw_w_up"][i]), row("rw_a0"),
                    jnp.zeros((LANES, 512), F32).at[HEAD:].set(full["rw_a_up"][i]), row("rw_k_k"), row("rw_k_a")],
            rw_post=[row("rw_ln_g"), row("rw_ln_b"), row("rw_r_k")],
            w_out_sb=big["w_out_sb"], w_out_ssd=big["w_out_ssd"], w_out_rw=big["w_out_rw"], w_o=big["w_o"])

    c_idx = core.reshape(1).astype(jnp.int32)

    def reduce_prepare(g, which, nm):
        sends = []
        for n in which:
            per_chip = ([_w_in_shard(g[n], q) for q in range(4)] if n == "w_in"
                        else jnp.split(g[n], 4, axis=BIG_AXIS[n] - 1))
            sends.append(jnp.stack([_row_halves(p) for p in per_chip], axis=1))
        others = _sibling_swap(sends, other_slot=True, name=nm + "sibling")
        return [_add_halves(s, o, c_idx, name=nm + "add_" + n) for n, s, o in zip(which, sends, others)]

    def reduce_finish(exchanged, which, nm):
        mine = [_sum_chips(p, name=nm + "sum_" + n) for n, p in zip(which, exchanged)]
        theirs = _sibling_swap(mine, other_slot=False, name=nm + "join")
        return {n: _join_halves(core, a, b) for n, a, b in zip(which, mine, theirs)}

    assert DEPTH == 2
    out_proj = BIG[1:]
    params, xs, saved, grads = [None] * 2, [x[0], None, None], [None] * 2, [None] * 2
    params[0] = layer_params(0, gathered(_chip_exchange(gather_srcs(0), per_dest=False, name="gather_l0"), "gather_l0_join"))
    xs[1], saved[0], got = _layer_fwd(xs[0], params[0], "l0_", side=(gather_srcs(1), False))
    params[1] = layer_params(1, gathered(got, "gather_l1_join"))
    xs[2], saved[1], _ = _layer_fwd(xs[1], params[1], "l1_")
    dx, loss_row, g_final = _final(xs[2], final_g.reshape(1, -1), loss_target[0], bt=BT, name="final")
    dx, grads[1], _, _ = _layer_bwd(xs[1], dx, params[1], saved[1], "l1_")
    dx, grads[0], got, got_late = _layer_bwd(
        xs[0], dx, params[0], saved[0], "l0_",
        side=lambda g: (reduce_prepare(grads[1], BIG, "reduce_l1_") + reduce_prepare(g, out_proj, "reduce_l0_out_"), True),
        side_late=lambda g: (reduce_prepare(g, ("w_in",), "reduce_l0_in_"), True))
    totals = [{**reduce_finish(got[len(BIG):], out_proj, "reduce_l0_out_"),
               **reduce_finish(got_late, ("w_in",), "reduce_l0_in_")},
              reduce_finish(got[:len(BIG)], BIG, "reduce_l1_")]

    def stacked(fn):
        return jnp.stack([fn(grads[i]) for i in range(DEPTH)])

    g_loc = {
        "norm_g": stacked(lambda g: g["norm_g"][0]),
        "conv_w": stacked(lambda g: jnp.concatenate(g["conv"][:4], axis=0)),
        "conv_b": stacked(lambda g: g["conv"][4][0]),
        "dt_bias": stacked(lambda g: g["dt_bias"][0, :SSD_HEADS]),
        "a_log": stacked(lambda g: g["a_log"][0, :SSD_HEADS]),
        "d_skip": stacked(lambda g: g["d_skip"][0, :SSD_HEADS]),
        "ssd_norm_g": stacked(lambda g: g["ssd_norm_g"][0]),
        "rw_mu": stacked(lambda g: g["rw_mu"][0]),
        "rw_w0": stacked(lambda g: g["rw_pre"][0][0]),
        "rw_w_up": stacked(lambda g: g["rw_pre"][1][:HEAD]),
        "rw_a0": stacked(lambda g: g["rw_pre"][2][0]),
        "rw_a_up": stacked(lambda g: g["rw_pre"][3][HEAD:]),
        "rw_k_k": stacked(lambda g: g["rw_pre"][4][0]),
        "rw_k_a": stacked(lambda g: g["rw_pre"][5][0]),
        "rw_r_k": stacked(lambda g: g["rw_r_k"].reshape(8, HEAD)),
        "rw_ln_g": stacked(lambda g: g["rw_ln_g"][0]),
        "rw_ln_b": stacked(lambda g: g["rw_ln_b"][0]),
        "final_g": g_final[0],
    }

    g_out = {n: jnp.stack([totals[0][n], totals[1][n]]) for n in BIG}

    sm_all = SMALL + ("loss",)
    sm_full_shapes = [g_loc[n].shape for n in SMALL] + [(1,)]
    _, summed = _allgather_small(_pack_rows([g_loc[n] for n in SMALL] + [loss_row[0, :1]]), reduce=True, name="reduce_small")
    sm = dict(zip(sm_all, _unpack_rows(summed, sm_full_shapes)))
    for n in SMALL:
        g_out[n] = sm[n]
    for n, wd in SMALL_SHARDED.items():
        g_out[n] = lax.dynamic_slice_in_dim(sm[n], chip * wd, wd, axis=sm[n].ndim - 1)
    loss = sm["loss"][0]

    upd = {n: _adamw(w_loc[n], g_out[n], m_loc[n], v_loc[n], name="adamw_" + n) for n in names if n != "w_in"}
    cols = SHARD_COLS // 4
    to_cols = lambda a: jnp.transpose(a, (2, 0, 1)).reshape(4, cols, DEPTH, D_MODEL)
    from_cols = lambda a: jnp.transpose(a.reshape(SHARD_COLS, DEPTH, D_MODEL), (1, 2, 0))
    g_cols = lax.optimization_barrier(to_cols(g_out["w_in"]))
    g_out["w_in"] = from_cols(g_cols)
    upd["w_in"] = tuple(from_cols(a) for a in _adamw(
        to_cols(w_loc["w_in"]), g_cols, to_cols(m_loc["w_in"]), to_cols(v_loc["w_in"]),
        name="adamw_w_in", block=(1, cols, DEPTH, D_MODEL // 2)))
    return (loss, dx[None], *[g_out[n] for n in names], *[upd[n][0] for n in names],
            *[upd[n][1] for n in names], *[upd[n][2] for n in names])
```

```python
import functools

import jax
import jax.numpy as jnp
from jax import lax
from jax.experimental import pallas as pl
from jax.experimental.pallas import tpu as pltpu

F32 = jnp.float32
BF16 = jnp.bfloat16

D_MODEL = 1024
DEPTH = 2
HEAD = 64
LANES = 128
CHUNK = 128
RMS_EPS = 1e-6
GN_EPS = 64e-5
VMEM_LIMIT = 56 * 1024 * 1024

N_IN = 9616
N_PAD = 9728
C_SB, C_Z, C_GATES, C_RW, C_LO, C_DT, C_XBC = 0, 2048, 3072, 6144, 8192, 8320, 8448
RW_COLS = 2176
XBC_COLS = 1280

ADAM_LR, ADAM_B1, ADAM_B2, ADAM_EPS, ADAM_WD, ADAM_STEP = 0.001, 0.9, 0.999, 1e-08, 0.01, 10


def _params(sem=None):
    return pltpu.CompilerParams(dimension_semantics=sem, vmem_limit_bytes=VMEM_LIMIT)


@jax.custom_vjp
def _sigmoid(x):
    return 1.0 / (1.0 + jnp.exp(-x))


def _sigmoid_fwd(x):
    s = _sigmoid(x)
    return s, s


def _sigmoid_bwd(s, g):
    return (g * s * (1.0 - s),)


_sigmoid.defvjp(_sigmoid_fwd, _sigmoid_bwd)


@jax.custom_vjp
def _silu(x):
    return x * _sigmoid(x)


def _silu_fwd(x):
    s = _sigmoid(x)
    return x * s, (x, s)


def _silu_bwd(res, g):
    x, s = res
    return (g * (s + x * s * (1.0 - s)),)


_silu.defvjp(_silu_fwd, _silu_bwd)


@jax.custom_vjp
def _softplus(x):
    return jnp.maximum(x, 0.0) + jnp.log(1.0 + jnp.exp(-jnp.abs(x)))


def _softplus_fwd(x):
    return _softplus(x), x


def _softplus_bwd(x, g):
    return (g * _sigmoid(x),)


_softplus.defvjp(_softplus_fwd, _softplus_bwd)


def _dot(a, b, dims):
    return lax.dot_general(a.astype(BF16), b.astype(BF16), (dims, ((), ())), preferred_element_type=F32)


def _dot_nn(a, b):
    return _dot(a, b, ((1,), (0,)))


def _dot_nt(a, b):
    return _dot(a, b, ((1,), (1,)))


def _dot_tn(a, b):
    return _dot(a, b, ((0,), (0,)))


@jax.custom_vjp
def _bdot(a, b):
    return _dot_nn(a, b)


def _bdot_fwd(a, b):
    return _dot_nn(a, b), (a, b)


def _bdot_bwd(res, g):
    a, b = res
    return _dot_nt(g, b), _dot_tn(a, g)


_bdot.defvjp(_bdot_fwd, _bdot_bwd)


def _split2(x):
    hi = x.astype(BF16)
    lo = (x - hi.astype(F32)).astype(BF16)
    return hi, lo


_NT = (((1,), (1,)), ((), ()))
_NN = (((1,), (0,)), ((), ()))
_TN = (((0,), (0,)), ((), ()))


def _dot2(x, m, dn=_NN):
    hi, lo = _split2(x)
    return (lax.dot_general(hi, m, dn, preferred_element_type=F32)
            + lax.dot_general(lo, m, dn, preferred_element_type=F32))


def _dot2_tn(x, m):
    return _dot2(x, m, _TN)


def _seg_matrix(n):
    r = lax.broadcasted_iota(jnp.int32, (n, n), 0) // HEAD
    c = lax.broadcasted_iota(jnp.int32, (n, n), 1) // HEAD
    return (r == c).astype(BF16)


@jax.custom_vjp
def _segsum2(x, seg):
    return _dot2(x, seg)


def _segsum2_fwd(x, seg):
    return _dot2(x, seg), seg


def _segsum2_bwd(seg, g):
    return _dot2(g, seg), jnp.zeros_like(seg)


_segsum2.defvjp(_segsum2_fwd, _segsum2_bwd)


def _make_segsum(seg):
    return lambda x: _segsum2(x, seg)


def _shift_down_raw(x, k):
    row = lax.broadcasted_iota(jnp.int32, x.shape, 0)
    return jnp.where(row >= k, pltpu.roll(x, k, 0), 0.0)


def _shift_up_raw(x, k):
    t = x.shape[0]
    row = lax.broadcasted_iota(jnp.int32, x.shape, 0)
    return jnp.where(row < t - k, pltpu.roll(x, t - k, 0), 0.0)


@functools.partial(jax.custom_vjp, nondiff_argnums=(1,))
def _shift_down(x, k):
    return _shift_down_raw(x, k)


def _shift_down_fwd(x, k):
    return _shift_down_raw(x, k), None


def _shift_down_bwd(k, _, g):
    return (_shift_up_raw(g, k),)


_shift_down.defvjp(_shift_down_fwd, _shift_down_bwd)


def _mm(a, b, *, name, ta=False, tb=False, add=None, out_dtype=F32, tm=2048, tn=512, tk=None, side=None):
    m, k = (a.shape[1], a.shape[0]) if ta else a.shape
    n = b.shape[0] if tb else b.shape[1]
    tm, tn = min(tm, m), min(tn, n)
    tk = k if tk is None else tk
    nk = k // tk
    assert m % tm == 0 and n % tn == 0 and k % tk == 0
    dims = ((0 if ta else 1,), (1 if tb else 0,))

    def body(a_ref, b_ref, *refs):
        o_ref, acc_ref = refs[-2:]
        p = _dot(a_ref[...], b_ref[...], dims)

        def emit(total):
            if add is not None:
                total = total + refs[0][...]
            o_ref[...] = total.astype(o_ref.dtype)

        if nk == 1:
            emit(p)
        else:
            kk = pl.program_id(2)

            @pl.when(kk == 0)
            def _():
                acc_ref[...] = p

            @pl.when(kk > 0)
            def _():
                acc_ref[...] += p

            @pl.when(kk == nk - 1)
            def _():
                emit(acc_ref[...])

    a_spec = pl.BlockSpec((tk, tm), lambda i, j, kk: (kk, i)) if ta else pl.BlockSpec((tm, tk), lambda i, j, kk: (i, kk))
    b_spec = pl.BlockSpec((tn, tk), lambda i, j, kk: (j, kk)) if tb else pl.BlockSpec((tk, tn), lambda i, j, kk: (kk, j))
    o_spec = pl.BlockSpec((tm, tn), lambda i, j, kk: (i, j))
    res = _call_with_side(
        body, side, name=name, grid=(m // tm, n // tn, nk), semantics=("parallel", "parallel", "arbitrary"),
        in_specs=[a_spec, b_spec] + ([o_spec] if add is not None else []), out_specs=[o_spec],
        out_shape=[jax.ShapeDtypeStruct((m, n), out_dtype)],
        scratch_shapes=[pltpu.VMEM((tm, tn) if nk > 1 else (8, LANES), F32)],
        operands=(a, b) + ((add,) if add is not None else ()))
    return res[0] if side is None else (res[0], res[1])


def _row_specs(rows, bt):
    return [pl.BlockSpec((bt, w), functools.partial(lambda i, c: (i, c), c=c)) for _, w, c in rows]


def _full_spec(p):
    return pl.BlockSpec(p.shape, functools.partial(lambda i, nd: (0,) * nd, nd=p.ndim))


def _rowwise(f, rows, pars, out_widths, *, bt, name, acc_widths=()):
    t = rows[0][0].shape[0]
    nr, npar, no, na = len(rows), len(pars), len(out_widths), len(acc_widths)

    def body(*refs):
        vals = [r[...] for r in refs[:nr + npar]]
        outs = f(*vals)
        for o_ref, o in zip(refs[nr + npar:nr + npar + no], outs[:no]):
            o_ref[...] = o.astype(o_ref.dtype)
        if na:
            first = pl.program_id(0) == 0
            for a_ref, a in zip(refs[nr + npar + no:], outs[no:]):
                @pl.when(first)
                def _():
                    a_ref[...] = jnp.zeros_like(a_ref)
                a_ref[...] += a

    return pl.pallas_call(
        body, name=name, grid=(t // bt,),
        in_specs=_row_specs(rows, bt) + [_full_spec(p) for p in pars],
        out_specs=[pl.BlockSpec((bt, w), lambda i: (i, 0)) for w in out_widths]
        + [pl.BlockSpec((1, w), lambda i: (0, 0)) for w in acc_widths],
        out_shape=[jax.ShapeDtypeStruct((t, w), F32) for w in out_widths]
        + [jax.ShapeDtypeStruct((1, w), F32) for w in acc_widths],
        compiler_params=_params(("arbitrary",)),
    )(*[r[0] for r in rows], *pars)


def _rowwise_bwd(f, rows, pars, douts, *, bt, name, groups=None):
    t = rows[0][0].shape[0]
    nr, npar, nd = len(rows), len(pars), len(douts)
    groups = [[i] for i in range(nr)] if groups is None else groups
    widths = [r[1] for r in rows]

    def body(*refs):
        vals = [r[...] for r in refs[:nr + npar]]
        cts = tuple(r[...] for r in refs[nr + npar:nr + npar + nd])
        _, vjp = jax.vjp(lambda *a: tuple(f(*a)), *vals)
        grads = vjp(cts)
        out_refs = refs[nr + npar + nd:]
        for g_ref, grp in zip(out_refs[:len(groups)], groups):
            off = 0
            for i in grp:
                g_ref[:, off:off + widths[i]] = grads[i]
                off += widths[i]
        first = pl.program_id(0) == 0
        for p_ref, g in zip(out_refs[len(groups):], grads[nr:]):
            @pl.when(first)
            def _():
                p_ref[...] = jnp.zeros_like(p_ref)
            p_ref[...] += g

    gw = [sum(widths[i] for i in grp) for grp in groups]
    return pl.pallas_call(
        body, name=name, grid=(t // bt,),
        in_specs=_row_specs(rows, bt) + [_full_spec(p) for p in pars] + _row_specs(douts, bt),
        out_specs=[pl.BlockSpec((bt, w), lambda i: (i, 0)) for w in gw] + [_full_spec(p) for p in pars],
        out_shape=[jax.ShapeDtypeStruct((t, w), F32) for w in gw] + [jax.ShapeDtypeStruct(p.shape, F32) for p in pars],
        compiler_params=_params(("arbitrary",)),
    )(*[r[0] for r in rows], *pars, *[d[0] for d in douts])


def _colwise(f, x, c0, ncols, pars, *, bc, name):
    t = x.shape[0]

    def body(x_ref, *refs):
        o_ref = refs[-1]
        o_ref[...] = f(x_ref[...], *[r[...] for r in refs[:-1]])

    return pl.pallas_call(
        body, name=name, grid=(ncols // bc,),
        in_specs=[pl.BlockSpec((t, bc), lambda j: (0, j + c0 // bc))]
        + [pl.BlockSpec((p.shape[0], bc), lambda j: (0, j)) for p in pars],
        out_specs=pl.BlockSpec((t, bc), lambda j: (0, j)),
        out_shape=jax.ShapeDtypeStruct((t, ncols), F32),
        compiler_params=_params(("parallel",)),
    )(x, *pars)


def _colwise_bwd(f, x, c0, ncols, pars, dout, *, bc, name):
    t = x.shape[0]
    npar = len(pars)

    def body(x_ref, *refs):
        vals = [x_ref[...]] + [r[...] for r in refs[:npar]]
        _, vjp = jax.vjp(f, *vals)
        grads = vjp(refs[npar][...])
        for g_ref, g in zip(refs[npar + 1:], grads):
            g_ref[...] = g

    return pl.pallas_call(
        body, name=name, grid=(ncols // bc,),
        in_specs=[pl.BlockSpec((t, bc), lambda j: (0, j + c0 // bc))]
        + [pl.BlockSpec((p.shape[0], bc), lambda j: (0, j)) for p in pars]
        + [pl.BlockSpec((t, bc), lambda j: (0, j))],
        out_specs=[pl.BlockSpec((t, bc), lambda j: (0, j))]
        + [pl.BlockSpec((p.shape[0], bc), lambda j: (0, j)) for p in pars],
        out_shape=[jax.ShapeDtypeStruct((t, ncols), F32)] + [jax.ShapeDtypeStruct(p.shape, F32) for p in pars],
        compiler_params=_params(("parallel",)),
    )(x, *pars, dout)


def _f_rms(x, g):
    return (x * lax.rsqrt(jnp.mean(x * x, axis=-1, keepdims=True) + RMS_EPS) * g,)


def _f_sb_gate(y, gate):
    return (y * _silu(gate),)


def _f_ssd_norm(y, z, g):
    u = y * _silu(z)
    return (u * lax.rsqrt(jnp.mean(u * u, axis=-1, keepdims=True) + RMS_EPS) * g,)


def _f_merge(p_sb, p_ssd, p_rw, g_sb, g_ssd, g_rw):
    return (_sigmoid(g_sb) * p_sb + _sigmoid(g_ssd) * p_ssd + _sigmoid(g_rw) * p_rw,)


def _f_rw_pre(k, lo, w0, w_up, a0, a_up, k_k, k_a):
    segsum = _make_segsum(_seg_matrix(k.shape[1]))
    lane = lax.broadcasted_iota(jnp.int32, lo.shape, 1)
    w_lo = jnp.where(lane < HEAD, jnp.tanh(lo), 0.0)
    a_lo = jnp.where(lane >= HEAD, lo, 0.0)
    w = -_softplus(-(w0 + _bdot(w_lo, w_up))) - 0.5
    log_decay = -jnp.exp(w)
    a = _sigmoid(a0 + _bdot(a_lo, a_up))
    kk = k * k_k
    kk = kk / jnp.maximum(jnp.sqrt(segsum(kk * kk)), 1e-12)
    return log_decay, k * (1.0 + (a - 1.0) * k_a), -kk, kk * a


def _f_rw_post(y, r, k2, v, gate, ln_g, ln_b, r_k):
    segsum = _make_segsum(_seg_matrix(y.shape[1]))
    yc = y - segsum(y) * (1.0 / HEAD)
    var = segsum(yc * yc) * (1.0 / HEAD)
    yn = yc * lax.rsqrt(var + GN_EPS) * ln_g + ln_b
    return ((yn + segsum(r * k2 * r_k) * v) * _silu(gate),)


def _f_rw_mix(slab, mu):
    return slab + (_shift_down(slab, 1) - slab) * mu


def _f_conv(x, w0, w1, w2, w3, b):
    acc = x * w3 + b
    for i, w in enumerate((w0, w1, w2)):
        acc = acc + _shift_down(x, 3 - i) * w
    return _silu(acc)


def _log_sigmoid(z):
    return jnp.minimum(z, 0.0) - jnp.log(1.0 + jnp.exp(-jnp.abs(z)))


SB_BQ = 256
SB_BK = 256
assert SB_BQ == SB_BK


def _tri_ones(kind):
    j = lax.broadcasted_iota(jnp.int32, (SB_BK, SB_BK + LANES), 0)
    s = lax.broadcasted_iota(jnp.int32, (SB_BK, SB_BK + LANES), 1)
    tri = {"gt": j > s, "le": j <= s, "lt": j < s}[kind]
    return (tri | (s >= SB_BK)).astype(BF16)


def _sb_common(q_ref):
    lane = lax.broadcasted_iota(jnp.int32, (SB_BQ, LANES), 1)
    q = q_ref[...] * (HEAD ** -0.5)
    q2 = jnp.concatenate([jnp.where(lane < HEAD, q, 0.0), jnp.where(lane >= HEAD, q, 0.0)], axis=0).astype(BF16)
    diff = (lax.broadcasted_iota(jnp.int32, (2 * SB_BQ, SB_BK), 1)
            - (lax.broadcasted_iota(jnp.int32, (2 * SB_BQ, SB_BK), 0) & (SB_BQ - 1)))
    return lane, q2, diff


def _rep(x):
    return jnp.concatenate([x] * (SB_BK // LANES), axis=1)


def _sb2_specs(t):
    q = pl.BlockSpec((SB_BQ, LANES), lambda j, i: (i, j))
    k = pl.BlockSpec((t, LANES), lambda j, i: (0, 4 + j))
    v = pl.BlockSpec((t, LANES), lambda j, i: (0, 8 + j))
    return q, k, v


def _sb2_fwd(proj, *, name):
    t = proj.shape[0]

    def body(q_ref, k_ref, v_ref, y_ref, lt_ref):
        i = pl.program_id(1)
        lane, q2, diff = _sb_common(q_ref)
        m_f = _tri_ones("gt")

        def step(kb, carry, diagonal):
            c, acc = carry
            off = pl.multiple_of(kb * SB_BK, SB_BK)
            kblk = k_ref[pl.ds(off, SB_BK), :].astype(BF16)
            vblk = v_ref[pl.ds(off, SB_BK), :].astype(BF16)
            z = lax.dot_general(q2, kblk, _NT, preferred_element_type=F32)
            lb = _log_sigmoid(z)
            lk = jnp.where(diff < 0, lb - z, 0.0) if diagonal else lb - z
            w2 = _dot2(lk, m_f)
            att = jnp.exp(lb + _rep(c) + w2[:, :SB_BK])
            if diagonal:
                att = jnp.where(diff < 0, att, 0.0)
            acc = acc + lax.dot_general(att.astype(BF16), vblk, _NN, preferred_element_type=F32)
            return c + w2[:, SB_BK:], acc

        zero = jnp.zeros((2 * SB_BQ, LANES), F32)
        c, acc = lax.fori_loop(0, i, lambda it, carry: step(i - 1 - it, carry, False), step(i, (zero, zero), True))
        y_ref[...] = jnp.where(lane < HEAD, acc[:SB_BQ], acc[SB_BQ:])
        lt_ref[0] = c[:SB_BQ]
        lt_ref[1] = c[SB_BQ:]

    return pl.pallas_call(
        body, name=name, grid=(4, t // SB_BQ),
        in_specs=list(_sb2_specs(t)),
        out_specs=[pl.BlockSpec((SB_BQ, LANES), lambda j, i: (i, j)),
                   pl.BlockSpec((2, SB_BQ, LANES), lambda j, i: (j, i, 0))],
        out_shape=[jax.ShapeDtypeStruct((t, 4 * LANES), F32), jax.ShapeDtypeStruct((8, t, LANES), F32)],
        compiler_params=_params(("parallel", "arbitrary")),
    )(proj, proj, proj)


def _sb2_bwd(proj, dy, lt, *, name):
    t = proj.shape[0]

    def body(q_ref, k_ref, v_ref, dy_ref, lt_ref, dq_ref, dk_ref, dv_ref):
        i = pl.program_id(1)

        @pl.when(i == 0)
        def _():
            dk_ref[...] = jnp.zeros_like(dk_ref)
            dv_ref[...] = jnp.zeros_like(dv_ref)

        lane, q2, diff = _sb_common(q_ref)
        m_le, m_lt = _tri_ones("le"), _tri_ones("lt")
        dy_blk = dy_ref[...]
        do2 = jnp.concatenate([jnp.where(lane < HEAD, dy_blk, 0.0), jnp.where(lane >= HEAD, dy_blk, 0.0)],
                              axis=0).astype(BF16)
        lt2 = jnp.concatenate([lt_ref[0], lt_ref[1]], axis=0)

        def step(kb, carry, diagonal):
            cp, cg, dq = carry
            off = pl.multiple_of(kb * SB_BK, SB_BK)
            kblk = k_ref[pl.ds(off, SB_BK), :].astype(BF16)
            vblk = v_ref[pl.ds(off, SB_BK), :].astype(BF16)
            z = lax.dot_general(q2, kblk, _NT, preferred_element_type=F32)
            lb = _log_sigmoid(z)
            lk = jnp.where(diff < 0, lb - z, 0.0) if diagonal else lb - z
            w2 = _dot2(lk, m_le)
            att = jnp.exp(lb + _rep(lt2 - cp) - w2[:, :SB_BK])
            if diagonal:
                att = jnp.where(diff < 0, att, 0.0)
            d_e = lax.dot_general(do2, vblk, _NT, preferred_element_type=F32) * att
            g2 = _dot2(d_e, m_lt)
            sig = jnp.exp(lb)
            dz = d_e * (1.0 - sig) - (_rep(cg) + g2[:, :SB_BK]) * sig
            dz = (jnp.where(diff < 0, dz, 0.0) if diagonal else dz).astype(BF16)
            dq = dq + lax.dot_general(dz, kblk, _NN, preferred_element_type=F32)
            dk_ref[pl.ds(off, SB_BK), :] += lax.dot_general(dz, q2, _TN, preferred_element_type=F32)
            dv_ref[pl.ds(off, SB_BK), :] += lax.dot_general(att.astype(BF16), do2, _TN, preferred_element_type=F32)
            return cp + w2[:, SB_BK:], cg + g2[:, SB_BK:], dq

        zero = jnp.zeros((2 * SB_BQ, LANES), F32)
        before = lax.fori_loop(0, i, lambda kb, carry: step(kb, carry, False), (zero, zero, zero))
        _, _, dq = step(i, before, True)
        dq_ref[...] = jnp.where(lane < HEAD, dq[:SB_BQ], dq[SB_BQ:]) * (HEAD ** -0.5)

    q_spec, k_spec, v_spec = _sb2_specs(t)
    blk = pl.BlockSpec((SB_BQ, LANES), lambda j, i: (i, j))
    col = pl.BlockSpec((t, LANES), lambda j, i: (0, j))
    return pl.pallas_call(
        body, name=name, grid=(4, t // SB_BQ),
        in_specs=[q_spec, k_spec, v_spec, blk, pl.BlockSpec((2, SB_BQ, LANES), lambda j, i: (j, i, 0))],
        out_specs=[blk, col, col],
        out_shape=[jax.ShapeDtypeStruct((t, 4 * LANES), F32)] * 3,
        compiler_params=_params(("parallel", "arbitrary")),
    )(proj, proj, proj, dy, lt)


SSD_HEADS = 16
SSD_PAIRS = 8


def _split3(x):
    a = x.astype(BF16)
    r = x - a.astype(F32)
    b = r.astype(BF16)
    return a, b, (r - b.astype(F32)).astype(BF16)


def _dot3(x, m, dn=_NN):
    return sum(lax.dot_general(p, m, dn, preferred_element_type=F32) for p in _split3(x))


def _mdot3(m, x):
    return sum(lax.dot_general(m, p, _NN, preferred_element_type=F32) for p in _split3(x))


def _ssd_common(dtr, dtb, alog, acsx_s, acst_s):
    lane = lax.broadcasted_iota(jnp.int32, (CHUNK, LANES), 1)
    lane1 = lax.broadcasted_iota(jnp.int32, (1, LANES), 1)
    arow = jnp.where(lane1 < SSD_HEADS, -jnp.exp(alog), 0.0)
    dt = jnp.where(lane < SSD_HEADS, _softplus(dtr + dtb), 0.0)
    da = dt * arow
    r = lax.broadcasted_iota(jnp.int32, (CHUNK, CHUNK), 0)
    c = lax.broadcasted_iota(jnp.int32, (CHUNK, CHUNK), 1)
    tril = (r >= c).astype(BF16)
    triu = (r <= c).astype(BF16)
    acs = _mdot3(tril, da)
    acst_s[...] = _dot3(da, triu, _TN)
    eh = lax.broadcasted_iota(jnp.int32, (LANES, 8 * LANES), 0)
    e = (eh == lax.broadcasted_iota(jnp.int32, (LANES, 8 * LANES), 1) // HEAD).astype(BF16)
    eh2 = lax.broadcasted_iota(jnp.int32, (LANES, 16 * LANES), 0)
    e2 = (eh2 == lax.broadcasted_iota(jnp.int32, (LANES, 16 * LANES), 1) // LANES).astype(BF16)
    acsx_s[...] = _dot3(acs, e)
    return dt, arow, _dot3(dt, e), _dot3(acs, e2), e, tril, triu


def _ssd_fwd(xc, proj, dtb, alog, dsk, *, name):
    t = xc.shape[0]
    nc = t // CHUNK

    def body(x_ref, b_ref, c_ref, dtr_ref, dtb_ref, alog_ref, dsk_ref, y_ref, hin_ref, acsx_s, acst_s, h_s):
        @pl.when(pl.program_id(0) == 0)
        def _():
            h_s[...] = jnp.zeros_like(h_s)

        dt, arow, dt_x, acs_b, e, tril, _ = _ssd_common(dtr_ref[...], dtb_ref[...], alog_ref[...], acsx_s, acst_s)
        dsk_x = _dot3(jnp.broadcast_to(dsk_ref[...], (CHUNK, LANES)), e)
        lane = lax.broadcasted_iota(jnp.int32, (CHUNK, LANES), 1)
        causal = (lax.broadcasted_iota(jnp.int32, (CHUNK, CHUNK), 0)
                  >= lax.broadcasted_iota(jnp.int32, (CHUNK, CHUNK), 1))
        for j in range(SSD_PAIRS):
            g = j // 4
            sl = slice(j * LANES, (j + 1) * LANES)
            if j % 4 == 0:
                bg = jnp.where(lane // HEAD == g, b_ref[...], 0.0)
                cg = jnp.where(lane // HEAD == g, c_ref[...], 0.0)
                cb = _dot_nt(cg, bg)
            x = x_ref[:, sl]
            a = acsx_s[:, sl]
            at = acsx_s[CHUNK - 1:CHUNK, sl]
            xdt = x * dt_x[:, sl]
            hin = h_s[j]
            hin_ref[0, j] = hin
            y = jnp.exp(a) * _dot_nn(cg, hin) + x * dsk_x[:, sl]
            h_s[j] = jnp.exp(at) * hin + _dot_tn(bg, xdt * jnp.exp(at - a))
            yd = []
            for hh in (0, 1):
                h = 2 * j + hh
                dec = jnp.exp(jnp.minimum(acs_b[:, h * LANES:(h + 1) * LANES] - acst_s[pl.ds(h, 1), :], 0.0))
                yd.append(_dot_nn(jnp.where(causal, cb * dec, 0.0), xdt))
            y_ref[:, sl] = y + jnp.where(lane < HEAD, yd[0], yd[1])

    one = pl.BlockSpec((1, LANES), lambda i: (0, 0))
    return pl.pallas_call(
        body, name=name, grid=(nc,),
        in_specs=[pl.BlockSpec((CHUNK, 8 * LANES), lambda i: (i, 0)),
                  pl.BlockSpec((CHUNK, LANES), lambda i: (i, 8)),
                  pl.BlockSpec((CHUNK, LANES), lambda i: (i, 9)),
                  pl.BlockSpec((CHUNK, LANES), lambda i: (i, C_DT // LANES)), one, one, one],
        out_specs=[pl.BlockSpec((CHUNK, 8 * LANES), lambda i: (i, 0)),
                   pl.BlockSpec((1, SSD_PAIRS, LANES, LANES), lambda i: (i, 0, 0, 0))],
        out_shape=[jax.ShapeDtypeStruct((t, 8 * LANES), F32),
                   jax.ShapeDtypeStruct((nc, SSD_PAIRS, LANES, LANES), F32)],
        scratch_shapes=[pltpu.VMEM((CHUNK, 8 * LANES), F32), pltpu.VMEM((LANES, CHUNK), F32),
                        pltpu.VMEM((SSD_PAIRS, LANES, LANES), F32)],
        compiler_params=_params(("arbitrary",)),
    )(xc, xc, xc, proj, dtb, alog, dsk)


def _ssd_bwd(xc, proj, dtb, alog, dsk, hin_all, dy, *, name):
    t = xc.shape[0]
    nc = t // CHUNK

    def body(x_ref, b_ref, c_ref, dtr_ref, dtb_ref, alog_ref, dsk_ref, hin_ref, dy_ref,
             dxc_ref, ddtr_ref, ddtb_ref, dalog_ref, ddsk_ref, acsx_s, acst_s, dh_s, dax_s, ddx_s):
        @pl.when(pl.program_id(0) == 0)
        def _():
            dh_s[...] = jnp.zeros_like(dh_s)
            ddtb_ref[...] = jnp.zeros_like(ddtb_ref)
            dalog_ref[...] = jnp.zeros_like(dalog_ref)
            ddsk_ref[...] = jnp.zeros_like(ddsk_ref)

        dtr = dtr_ref[...]
        dtb = dtb_ref[...]
        dt, arow, dt_x, acs_b, e, tril, triu = _ssd_common(dtr, dtb, alog_ref[...], acsx_s, acst_s)
        dsk_x = _dot3(jnp.broadcast_to(dsk_ref[...], (CHUNK, LANES)), e)
        lane = lax.broadcasted_iota(jnp.int32, (CHUNK, LANES), 1)
        rowi = lax.broadcasted_iota(jnp.int32, (CHUNK, LANES), 0)
        causal = (lax.broadcasted_iota(jnp.int32, (CHUNK, CHUNK), 0)
                  >= lax.broadcasted_iota(jnp.int32, (CHUNK, CHUNK), 1))
        dacs = jnp.zeros((CHUNK, LANES), F32)
        d_b = jnp.zeros((CHUNK, LANES), F32)
        d_c = jnp.zeros((CHUNK, LANES), F32)
        for j in range(SSD_PAIRS):
            g = j // 4
            sl = slice(j * LANES, (j + 1) * LANES)
            if j % 4 == 0:
                bg = jnp.where(lane // HEAD == g, b_ref[...], 0.0)
                cg = jnp.where(lane // HEAD == g, c_ref[...], 0.0)
                cb = _dot_nt(cg, bg)
                dcb = jnp.zeros((CHUNK, CHUNK), F32)
            x = x_ref[:, sl]
            d = dt_x[:, sl]
            a = acsx_s[:, sl]
            at = acsx_s[CHUNK - 1:CHUNK, sl]
            xdt = x * d
            hin = hin_ref[0, j]
            dhout = dh_s[j]
            dyp = dy_ref[:, sl]
            ea, eat, ed = jnp.exp(a), jnp.exp(at), jnp.exp(at - a)
            da_l = dyp * ea * _dot_nn(cg, hin)
            dm = dyp * ea
            d_c = d_c + _dot_nt(dm, hin)
            dh_s[j] = _dot_tn(cg, dm) + eat * dhout
            dat = jnp.sum(dhout * hin * eat, axis=0, keepdims=True)
            d_b = d_b + _dot_nt(xdt * ed, dhout)
            dw = _dot_nn(bg, dhout)
            dxdt = dw * ed
            ded = dw * xdt * ed
            dat = dat + jnp.sum(ded, axis=0, keepdims=True)
            da_l = da_l - ded
            for hh in (0, 1):
                h = 2 * j + hh
                dec = jnp.exp(jnp.minimum(acs_b[:, h * LANES:(h + 1) * LANES] - acst_s[pl.ds(h, 1), :], 0.0))
                gm = jnp.where(causal, cb * dec, 0.0)
                dyh = jnp.where(lane // HEAD == hh, dyp, 0.0)
                dg = _dot_nt(dyh, xdt)
                dxdt = dxdt + _dot_tn(gm, dyh)
                dcb = dcb + jnp.where(causal, dg * dec, 0.0)
                th = dg * gm
                oh = (lane == h).astype(BF16)
                dacs = dacs + _dot2(th, oh) - _dot2_tn(th, oh)
            if j % 4 == 3:
                d_c = d_c + _dot_nn(dcb, bg)
                d_b = d_b + _dot_tn(dcb, cg)
            dxc_ref[:, sl] = dyp * dsk_x[:, sl] + dxdt * d
            ddx_s[:, sl] = dxdt * x
            dax_s[:, sl] = da_l + jnp.where(rowi == CHUNK - 1, dat, 0.0)
            dskp = jnp.sum(dyp * x, axis=0, keepdims=True)
            ddsk_ref[...] += _dot2(jnp.broadcast_to(dskp, (8, LANES)), e[:, sl], _NT)
        dxc_ref[:, 8 * LANES:9 * LANES] = d_b
        dxc_ref[:, 9 * LANES:10 * LANES] = d_c
        dacs = dacs + _dot2(dax_s[...], e, _NT)
        ddt = _dot2(ddx_s[...], e, _NT)
        dda = _mdot3(triu, dacs)
        ddt = ddt + dda * arow
        dalog_ref[...] += jnp.sum(dda * dt, axis=0, keepdims=True) * arow
        ddtr = jnp.where(lane < SSD_HEADS, ddt * _sigmoid(dtr + dtb), 0.0)
        ddtr_ref[...] = ddtr
        ddtb_ref[...] += jnp.sum(ddtr, axis=0, keepdims=True)

    one = pl.BlockSpec((1, LANES), lambda i: (0, 0))
    rev = lambda c: (lambda i: (nc - 1 - i, c))
    return pl.pallas_call(
        body, name=name, grid=(nc,),
        in_specs=[pl.BlockSpec((CHUNK, 8 * LANES), rev(0)), pl.BlockSpec((CHUNK, LANES), rev(8)),
                  pl.BlockSpec((CHUNK, LANES), rev(9)), pl.BlockSpec((CHUNK, LANES), rev(C_DT // LANES)),
                  one, one, one,
                  pl.BlockSpec((1, SSD_PAIRS, LANES, LANES), lambda i: (nc - 1 - i, 0, 0, 0)),
                  pl.BlockSpec((CHUNK, 8 * LANES), rev(0))],
        out_specs=[pl.BlockSpec((CHUNK, XBC_COLS), rev(0)), pl.BlockSpec((CHUNK, LANES), rev(0)), one, one,
                   pl.BlockSpec((8, LANES), lambda i: (0, 0))],
        out_shape=[jax.ShapeDtypeStruct((t, XBC_COLS), F32), jax.ShapeDtypeStruct((t, LANES), F32)]
        + [jax.ShapeDtypeStruct((1, LANES), F32)] * 2 + [jax.ShapeDtypeStruct((8, LANES), F32)],
        scratch_shapes=[pltpu.VMEM((CHUNK, 8 * LANES), F32), pltpu.VMEM((LANES, CHUNK), F32),
                        pltpu.VMEM((SSD_PAIRS, LANES, LANES), F32),
                        pltpu.VMEM((CHUNK, 8 * LANES), F32), pltpu.VMEM((CHUNK, 8 * LANES), F32)],
        compiler_params=_params(("arbitrary",)),
    )(xc, xc, xc, proj, dtb, alog, dsk, hin_all, dy)


RW_C = 64


def _p3(a, b, dn):
    ah, al = _split2(a)
    bh, bl = _split2(b)
    d = lambda x, y: lax.dot_general(x, y, dn, preferred_element_type=F32)
    return d(ah, bh) + d(ah, bl) + d(al, bh)


_BNN = (((2,), (1,)), ((0,), (0,)))
_BNT = (((2,), (2,)), ((0,), (0,)))
_BTN = (((1,), (1,)), ((0,), (0,)))


@jax.custom_vjp
def _pnn(a, b):
    return _p3(a, b, _BNN)


@jax.custom_vjp
def _pnt(a, b):
    return _p3(a, b, _BNT)


@jax.custom_vjp
def _ptn(a, b):
    return _p3(a, b, _BTN)


_pnn.defvjp(lambda a, b: (_p3(a, b, _BNN), (a, b)), lambda res, g: (_p3(g, res[1], _BNT), _p3(res[0], g, _BTN)))
_pnt.defvjp(lambda a, b: (_p3(a, b, _BNT), (a, b)), lambda res, g: (_p3(g, res[1], _BNN), _p3(g, res[0], _BTN)))
_ptn.defvjp(lambda a, b: (_p3(a, b, _BTN), (a, b)), lambda res, g: (_p3(res[1], g, _BNT), _p3(res[0], g, _BNN)))


def _tri2(tril, x, dn):
    hi, lo = _split2(x)
    m = tril.astype(BF16)
    return (lax.dot_general(m, hi, dn, preferred_element_type=F32) + lax.dot_general(m, lo, dn, preferred_element_type=F32))


@jax.custom_vjp
def _cumsum_rows(tril, x):
    return _tri2(tril, x, _BNN)


_cumsum_rows.defvjp(lambda tril, x: (_tri2(tril, x, _BNN), tril),
                    lambda tril, g: (jnp.zeros_like(tril), _tri2(tril, g, _BTN)))


def _rw_chunk_consts():
    c2 = 2 * RW_C
    row = lax.broadcasted_iota(jnp.int32, (c2, c2), 0)
    col = lax.broadcasted_iota(jnp.int32, (c2, c2), 1)
    same = (row // RW_C) == (col // RW_C)
    strict = (same & (row > col)).astype(F32)
    incl = (same & (row >= col)).astype(F32)
    eye = (row == col).astype(F32)
    tr = lax.broadcasted_iota(jnp.int32, (RW_C, RW_C), 0)
    tc = lax.broadcasted_iota(jnp.int32, (RW_C, RW_C), 1)
    tril = (tr >= tc).astype(F32)
    lane = lax.broadcasted_iota(jnp.int32, (1, LANES), 1)
    hm = [(lane // HEAD == h).astype(F32) for h in (0, 1)]
    return strict, incl, eye, tril, hm


def _rw_chunk(r, lw, k, v, n, b, s2, consts):
    strict, incl, eye, tril, hm = consts
    two = lambda x: jnp.concatenate([x * hm[0], x * hm[1]], axis=1)
    cum = _cumsum_rows(jnp.broadcast_to(tril, (4, RW_C, RW_C)), lw)
    grow, shrink = jnp.exp(-cum), jnp.exp(cum)
    n2, r2 = two(n * jnp.exp(cum - lw)), two(r * shrink)
    b2, k2, v2 = two(b * grow), two(k * grow), two(v)
    p = _pnt(n2, b2) * strict
    x2 = _pnt(n2, s2) + _pnn(_pnt(n2, k2) * strict, v2)
    t_inv, a = eye + p, p
    for _ in range(RW_C.bit_length() - 2):
        a = _pnn(a, a)
        t_inv = t_inv + _pnn(t_inv, a)
    u2 = _pnn(t_inv, x2)
    y2 = _pnt(r2, s2) + _pnn(_pnt(r2, b2) * incl, u2) + _pnn(_pnt(r2, k2) * incl, v2)
    s2_new = (s2 + _ptn(u2, b2) + _ptn(v2, k2)) * jnp.exp(jnp.sum(lw, axis=1, keepdims=True))
    return jnp.sum(y2.reshape(4, 2, RW_C, LANES), axis=1), s2_new


def _pairs(ref):
    return jnp.stack([ref[:, p * LANES:(p + 1) * LANES] for p in range(4)])


def _rw_chunk_fwd(mixed, lw, k, n, b, *, name, side=None):
    t = lw.shape[0]
    nc = t // RW_C

    def body(r_ref, v_ref, lw_ref, k_ref, n_ref, b_ref, y_ref, sin_ref, s_s):
        @pl.when(pl.program_id(0) == 0)
        def _():
            s_s[...] = jnp.zeros_like(s_s)

        s2 = s_s[...]
        sin_ref[0] = s2
        y, s2 = _rw_chunk(*[_pairs(x) for x in (r_ref, lw_ref, k_ref, v_ref, n_ref, b_ref)], s2, _rw_chunk_consts())
        for p in range(4):
            y_ref[:, p * LANES:(p + 1) * LANES] = y[p]
        s_s[...] = s2

    blk = lambda c: pl.BlockSpec((RW_C, 4 * LANES), functools.partial(lambda i, c: (i, c), c=c))
    return _call_with_side(
        body, side, name=name, grid=(nc,), semantics=("arbitrary",),
        in_specs=[blk(0), blk(2), blk(0), blk(0), blk(0), blk(0)],
        out_specs=[blk(0), pl.BlockSpec((1, 4, LANES, LANES), lambda i: (i, 0, 0, 0))],
        out_shape=[jax.ShapeDtypeStruct((t, 4 * LANES), F32), jax.ShapeDtypeStruct((nc, 4, LANES, LANES), F32)],
        scratch_shapes=[pltpu.VMEM((4, LANES, LANES), F32)],
        operands=(mixed, mixed, lw, k, n, b))


def _call_with_side(body, side, *, name, grid, semantics, in_specs, out_specs, out_shape, scratch_shapes, operands):
    if side is None:
        return pl.pallas_call(body, name=name, grid=grid, in_specs=in_specs, out_specs=out_specs, out_shape=out_shape,
                              scratch_shapes=scratch_shapes, compiler_params=_params(semantics))(*operands)
    srcs, per_dest = side
    ns, ni, no, nscr = len(srcs), len(in_specs), len(out_specs), len(scratch_shapes)

    def full_body(*refs):
        ins, side_in = refs[:ni], refs[ni:ni + ns]
        outs, side_out = refs[ni + ns:ni + ns + no], refs[ni + ns + no:ni + 2 * ns + no]
        scratch, sems = refs[ni + 2 * ns + no:ni + 2 * ns + no + nscr], refs[ni + 2 * ns + no + nscr:]

        ids = [pl.program_id(a) for a in range(len(grid))]
        first = functools.reduce(jnp.logical_and, [i == 0 for i in ids])
        last = functools.reduce(jnp.logical_and, [i == n - 1 for i, n in zip(ids, grid)])

        @pl.when(first)
        def _():
            _exchange(side_in, side_out, sems, per_dest, start=True, wait=False)

        body(*ins, *outs, *scratch)

        @pl.when(last)
        def _():
            _exchange(side_in, side_out, sems, per_dest, start=False, wait=True)

    res = pl.pallas_call(
        full_body, name=name, grid=grid, in_specs=list(in_specs) + [_ANY] * ns,
        out_specs=list(out_specs) + [_ANY] * ns, out_shape=list(out_shape) + _exchange_out_shapes(srcs),
        scratch_shapes=list(scratch_shapes) + _exchange_sems(ns), compiler_params=_params(("arbitrary",) * len(grid)),
    )(*operands, *srcs)
    return list(res[:no]) + [list(res[no:])]


def _rw_chunk_bwd(mixed, lw, k, n, b, s_in, dy, dr0, dk0, dv0, *, name, side=None):
    t = lw.shape[0]
    nc = t // RW_C

    def body(r_ref, v_ref, lw_ref, k_ref, n_ref, b_ref, sin_ref, dy_ref, dr0_ref, dk0_ref, dv0_ref,
             dr_ref, dlw_ref, dk_ref, dv_ref, dn_ref, db_ref, ds_s):
        @pl.when(pl.program_id(0) == 0)
        def _():
            ds_s[...] = jnp.zeros_like(ds_s)

        consts = _rw_chunk_consts()
        args = [_pairs(x) for x in (r_ref, lw_ref, k_ref, v_ref, n_ref, b_ref)] + [sin_ref[0]]
        _, vjp = jax.vjp(lambda *a: _rw_chunk(*a, consts), *args)
        dr, dlw, dk, dv, dn, db, ds = vjp((_pairs(dy_ref), ds_s[...]))
        for p in range(4):
            sl = slice(p * LANES, (p + 1) * LANES)
            dr_ref[:, sl] = dr[p] + dr0_ref[:, sl]
            dlw_ref[:, sl] = dlw[p]
            dk_ref[:, sl] = dk[p] + dk0_ref[:, sl]
            dv_ref[:, sl] = dv[p] + dv0_ref[:, sl]
            dn_ref[:, sl] = dn[p]
            db_ref[:, sl] = db[p]
        ds_s[...] = ds

    blk = lambda c: pl.BlockSpec((RW_C, 4 * LANES), functools.partial(lambda i, c: (nc - 1 - i, c), c=c))
    return _call_with_side(
        body, side, name=name, grid=(nc,), semantics=("arbitrary",),
        in_specs=[blk(0), blk(2), blk(0), blk(0), blk(0), blk(0),
                  pl.BlockSpec((1, 4, LANES, LANES), lambda i: (nc - 1 - i, 0, 0, 0)), blk(0), blk(0), blk(0), blk(0)],
        out_specs=[blk(0)] * 6,
        out_shape=[jax.ShapeDtypeStruct((t, 4 * LANES), F32)] * 6,
        scratch_shapes=[pltpu.VMEM((4, LANES, LANES), F32)],
        operands=(mixed, mixed, lw, k, n, b, s_in, dy, dr0, dk0, dv0))


def _f_rms_res(x, g):
    return _f_rms(x, g)[0], x


def _final(x, g, target, *, bt, name):
    t, d = x.shape

    def body(x_ref, g_ref, t_ref, dx_ref, loss_ref, dg_ref):
        tgt = t_ref[...]

        def f(xv, gv):
            err = _f_rms(xv, gv)[0] - tgt
            return 0.5 * jnp.mean(err * err, axis=-1, keepdims=True)

        row_loss, vjp = jax.vjp(f, x_ref[...], g_ref[...])
        dx, dg = vjp(jnp.ones_like(row_loss))
        dx_ref[...] = dx

        @pl.when(pl.program_id(0) == 0)
        def _():
            loss_ref[...] = jnp.zeros_like(loss_ref)
            dg_ref[...] = jnp.zeros_like(dg_ref)

        loss_ref[...] += jnp.broadcast_to(jnp.sum(row_loss, axis=0, keepdims=True), (1, LANES))
        dg_ref[...] += dg

    blk = pl.BlockSpec((bt, d), lambda i: (i, 0))
    return pl.pallas_call(
        body, name=name, grid=(t // bt,),
        in_specs=[blk, pl.BlockSpec((1, d), lambda i: (0, 0)), blk],
        out_specs=[blk, pl.BlockSpec((1, LANES), lambda i: (0, 0)), pl.BlockSpec((1, d), lambda i: (0, 0))],
        out_shape=[jax.ShapeDtypeStruct((t, d), F32), jax.ShapeDtypeStruct((1, LANES), F32),
                   jax.ShapeDtypeStruct((1, d), F32)],
        compiler_params=_params(("arbitrary",)),
    )(x, g, target)


ADAMW_BLOCK_BYTES = 1 << 20


def _adamw(w, g, m, v, *, name, block=None):
    shape = w.shape
    if block is not None:
        return _adamw_blocks(w, g, m, v, block, name)
    c = shape[-1]
    shape3 = (1,) * (3 - len(shape)) + shape if len(shape) <= 3 else (-1,) + shape[-2:]
    args = [a.reshape(shape3) for a in (w, g, m, v)]
    lead, r, _ = args[0].shape
    br = r
    if r * c * 4 > ADAMW_BLOCK_BYTES:
        cands = [b for b in range(8, r, 8) if r % b == 0 and b * c * 4 <= ADAMW_BLOCK_BYTES]
        br = max(cands) if cands else r
    outs = _adamw_blocks(*args, (1, br, c), name)
    return tuple(o.reshape(shape) for o in outs)


def _adamw_blocks(w, g, m, v, block, name):
    shape = w.shape
    assert all(s % b == 0 for s, b in zip(shape, block))

    def body(w_ref, g_ref, m_ref, v_ref, d_ref, nm_ref, nv_ref):
        gv = g_ref[...]
        m_new = ADAM_B1 * m_ref[...] + (1.0 - ADAM_B1) * gv
        v_new = ADAM_B2 * v_ref[...] + (1.0 - ADAM_B2) * (gv * gv)
        m_hat = m_new / (1.0 - ADAM_B1 ** ADAM_STEP)
        v_hat = v_new / (1.0 - ADAM_B2 ** ADAM_STEP)
        d_ref[...] = -ADAM_LR * (m_hat / (jnp.sqrt(v_hat) + ADAM_EPS) + ADAM_WD * w_ref[...])
        nm_ref[...] = m_new
        nv_ref[...] = v_new

    blk = pl.BlockSpec(tuple(block), lambda *ids: ids)
    return pl.pallas_call(
        body, name=name, grid=tuple(s // b for s, b in zip(shape, block)), in_specs=[blk] * 4, out_specs=[blk] * 3,
        out_shape=[jax.ShapeDtypeStruct(shape, F32)] * 3,
        compiler_params=_params(("parallel",) * len(shape)),
    )(w, g, m, v)


BT = 256
BC = 128


def _layer_rows(x, proj, s):
    s = {k: s.get(k) for k in ("y_sb_raw", "y_ssd_raw", "mixed", "ys", "k2", "p_sb", "p_ssd", "p_rw")}
    return dict(
        rms=[(x, D_MODEL, 0)],
        sb_gate=[(s["y_sb_raw"], 512, 0), (proj, 512, 3)],
        ssd_norm=[(s["y_ssd_raw"], 1024, 0), (proj, 1024, C_Z // 1024)],
        rw_pre=[(s["mixed"], 512, 1), (s["mixed"], LANES, 16)],
        rw_post=[(s["ys"], 512, 0), (s["mixed"], 512, 0), (s["k2"], 512, 0), (s["mixed"], 512, 2), (s["mixed"], 512, 3)],
        merge=[(s["p_sb"], 1024, 0), (s["p_ssd"], 1024, 0), (s["p_rw"], 1024, 0),
               (proj, 1024, 3), (proj, 1024, 4), (proj, 1024, 5)],
    )


def _layer_fwd(x, p, nm, side=None):
    s = {}
    (s["h"],) = _rowwise(_f_rms, [(x, D_MODEL, 0)], [p["norm_g"]], [D_MODEL], bt=BT, name=nm + "rms")
    proj = s["proj"] = _mm(s["h"], p["w_in"], name=nm + "proj")
    s["y_sb_raw"], s["lt"] = _sb2_fwd(proj, name=nm + "sb")
    s["xc"] = _colwise(_f_conv, proj, C_XBC, XBC_COLS, p["conv"], bc=BC, name=nm + "conv")
    s["y_ssd_raw"], s["hin"] = _ssd_fwd(s["xc"], proj, p["dt_bias"], p["a_log"], p["d_skip"], name=nm + "ssd")
    s["mixed"] = _colwise(_f_rw_mix, proj, C_RW, RW_COLS, [p["rw_mu"]], bc=BC, name=nm + "mix")
    s["w"], s["k2"], s["n"], s["b"] = _rowwise(_f_rw_pre, [(s["mixed"], 512, 1), (s["mixed"], LANES, 16)], p["rw_pre"],
                                               [512] * 4, bt=BT, name=nm + "rwpre")
    s["ys"], s["st"], *exchanged = _rw_chunk_fwd(s["mixed"], s["w"], s["k2"], s["n"], s["b"], name=nm + "scan", side=side)
    rows = _layer_rows(x, proj, s)
    (s["y_sb"],) = _rowwise(_f_sb_gate, rows["sb_gate"], [], [512], bt=BT, name=nm + "sbgate")
    (s["y_ssd"],) = _rowwise(_f_ssd_norm, rows["ssd_norm"], [p["ssd_norm_g"]], [1024], bt=BT, name=nm + "ssdnorm")
    (s["y_rw"],) = _rowwise(_f_rw_post, rows["rw_post"], p["rw_post"], [512], bt=BT, name=nm + "rwpost")
    s["p_sb"] = _mm(s["y_sb"], p["w_out_sb"], name=nm + "osb")
    s["p_ssd"] = _mm(s["y_ssd"], p["w_out_ssd"], name=nm + "ossd")
    s["p_rw"] = _mm(s["y_rw"], p["w_out_rw"], name=nm + "orw")
    (s["merged"],) = _rowwise(_f_merge, _layer_rows(x, proj, s)["merge"], [], [1024], bt=BT, name=nm + "merge")
    return _mm(s["merged"], p["w_o"], add=x, name=nm + "wo"), s, (exchanged[0] if exchanged else None)


def _layer_bwd(x, dx_out, p, s, nm, side=None, side_late=None):
    g = {}
    proj = s["proj"]
    rows = _layer_rows(x, proj, s)
    g["w_o"] = _mm(s["merged"], dx_out, ta=True, name=nm + "g_wo")
    d_merged = _mm(dx_out, p["w_o"], tb=True, name=nm + "d_merged")
    dp_sb, dp_ssd, dp_rw, d_gates = _rowwise_bwd(_f_merge, rows["merge"], [], [(d_merged, 1024, 0)], bt=BT,
                                                 name=nm + "merge_b", groups=[[0], [1], [2], [3, 4, 5]])
    g["w_out_sb"] = _mm(s["y_sb"], dp_sb, ta=True, name=nm + "g_osb")
    g["w_out_ssd"] = _mm(s["y_ssd"], dp_ssd, ta=True, name=nm + "g_ossd")
    g["w_out_rw"] = _mm(s["y_rw"], dp_rw, ta=True, name=nm + "g_orw")
    dy_sb = _mm(dp_sb, p["w_out_sb"], tb=True, name=nm + "d_ysb")
    dy_ssd = _mm(dp_ssd, p["w_out_ssd"], tb=True, name=nm + "d_yssd")
    dy_rw = _mm(dp_rw, p["w_out_rw"], tb=True, name=nm + "d_yrw")
    dy_sb_raw, d_sbgate = _rowwise_bwd(_f_sb_gate, rows["sb_gate"], [], [(dy_sb, 512, 0)], bt=BT, name=nm + "sbgate_b")
    dq, dk, dv = _sb2_bwd(proj, dy_sb_raw, s["lt"], name=nm + "sb_b")
    dy_ssd_raw, dz, g["ssd_norm_g"] = _rowwise_bwd(_f_ssd_norm, rows["ssd_norm"], [p["ssd_norm_g"]],
                                                   [(dy_ssd, 1024, 0)], bt=BT, name=nm + "ssdnorm_b")
    dxc, ddtr, g["dt_bias"], g["a_log"], g["d_skip"] = _ssd_bwd(
        s["xc"], proj, p["dt_bias"], p["a_log"], p["d_skip"], s["hin"], dy_ssd_raw, name=nm + "ssd_b")
    conv_out = _colwise_bwd(_f_conv, proj, C_XBC, XBC_COLS, p["conv"], dxc, bc=BC, name=nm + "conv_b")
    dxbc, g["conv"] = conv_out[0], conv_out[1:]
    dys, dr0, dk0, dv0, d_rwgate, g["rw_ln_g"], g["rw_ln_b"], g["rw_r_k"] = _rowwise_bwd(
        _f_rw_post, rows["rw_post"], p["rw_post"], [(dy_rw, 512, 0)], bt=BT, name=nm + "rwpost_b")
    dr, dw, dk2, dvv, dn, db, *exchanged = _rw_chunk_bwd(s["mixed"], s["w"], s["k2"], s["n"], s["b"], s["st"], dys,
                                                         dr0, dk0, dv0, name=nm + "scan_b",
                                                         side=side(g) if side else None)
    pre_out = _rowwise_bwd(_f_rw_pre, rows["rw_pre"], p["rw_pre"],
                           [(dw, 512, 0), (dk2, 512, 0), (dn, 512, 0), (db, 512, 0)], bt=BT, name=nm + "rwpre_b")
    dkm, dlo, g["rw_pre"] = pre_out[0], pre_out[1], pre_out[2:]
    d_mixed = jnp.concatenate([dr, dkm, dvv, d_rwgate, dlo], axis=1)
    d_slab, g["rw_mu"] = _colwise_bwd(_f_rw_mix, proj, C_RW, RW_COLS, [p["rw_mu"]], d_mixed, bc=BC, name=nm + "mix_b")
    d_proj = jnp.concatenate([dq, dk, dv, d_sbgate, dz, d_gates, d_slab, ddtr, dxbc], axis=1)
    g["w_in"] = _mm(s["h"], d_proj, ta=True, name=nm + "g_win")
    dh = _mm(d_proj, p["w_in"], tb=True, tn=1024, tk=512, name=nm + "d_h", side=side_late(g) if side_late else None)
    dh, late = dh if side_late else (dh, None)
    dx, g["norm_g"] = _rowwise_bwd(_f_rms_res, rows["rms"], [p["norm_g"]], [(dh, D_MODEL, 0), (dx_out, D_MODEL, 0)],
                                   bt=BT, name=nm + "rms_b")
    return dx, g, (exchanged[0] if exchanged else None), late


MESH = pl.DeviceIdType.MESH
N_DEV = 8
_ANY = pl.BlockSpec(memory_space=pl.ANY)


def _here():
    x, y, c = lax.axis_index("x"), lax.axis_index("y"), lax.axis_index("c")
    return x, y, c, [(1 - x, y), (x, 1 - y), (1 - x, 1 - y)]


def _chip_exchange(srcs, *, per_dest, name):
    n = len(srcs)

    def body(*refs):
        _exchange(refs[:n], refs[n:2 * n], refs[2 * n:], per_dest, start=True, wait=True)

    return pl.pallas_call(
        body, name=name, in_specs=[_ANY] * n, out_specs=[_ANY] * n,
        out_shape=_exchange_out_shapes(srcs), scratch_shapes=_exchange_sems(n),
    )(*srcs)


def _exchange_out_shapes(srcs):
    return [jax.ShapeDtypeStruct((4,) + s.shape[1:], s.dtype) for s in srcs]


def _exchange_sems(n):
    return [pltpu.SemaphoreType.DMA((3 * n,)), pltpu.SemaphoreType.DMA((3 * n,)), pltpu.SemaphoreType.DMA((n,))]


def _exchange(src_refs, out_refs, sems, per_dest, *, start, wait):
    send_sems, recv_sems, local_sems = sems
    x, y, c, chips = _here()
    me = 2 * x + y
    owns, sends, recvs = [], [], []
    for a, (src_ref, out_ref) in enumerate(zip(src_refs, out_refs)):
        pick = (lambda q, s=src_ref: s.at[q]) if per_dest else (lambda q, s=src_ref: s.at[c])
        owns.append(pltpu.make_async_copy(pick(me), out_ref.at[me], local_sems.at[a]))
        for j, (px, py) in enumerate(chips):
            sends.append(pltpu.make_async_remote_copy(
                pick(2 * px + py), out_ref.at[me], send_sems.at[3 * a + j], recv_sems.at[3 * a + j],
                device_id=(px, py, c), device_id_type=MESH))
            recvs.append(pltpu.make_async_remote_copy(
                src_ref.at[0], out_ref.at[2 * px + py], send_sems.at[3 * a + j], recv_sems.at[3 * a + j],
                device_id=(px, py, c), device_id_type=MESH))
    if start:
        for cp in owns + sends:
            cp.start()
    if wait:
        for cp in recvs:
            cp.wait_recv()
        for cp in sends:
            cp.wait_send()
        for cp in owns:
            cp.wait()


def _sibling_swap(srcs, *, other_slot, name):
    n = len(srcs)

    def body(*refs):
        src_refs, out_refs, send_sems, recv_sems = refs[:n], refs[n:2 * n], refs[2 * n], refs[2 * n + 1]
        x, y, c, _ = _here()
        copies = [pltpu.make_async_remote_copy(s.at[1 - c] if other_slot else s, o, send_sems.at[a], recv_sems.at[a],
                                               device_id=(x, y, 1 - c), device_id_type=MESH)
                  for a, (s, o) in enumerate(zip(src_refs, out_refs))]
        for cp in copies:
            cp.start()
        for cp in copies:
            cp.wait()

    return pl.pallas_call(
        body, name=name, in_specs=[_ANY] * n, out_specs=[_ANY] * n,
        out_shape=[jax.ShapeDtypeStruct(s.shape[1:] if other_slot else s.shape, s.dtype) for s in srcs],
        scratch_shapes=[pltpu.SemaphoreType.DMA((n,)), pltpu.SemaphoreType.DMA((n,))],
    )(*srcs)


def _allgather_small(v, *, reduce, name):
    r = v.shape[0]

    def body(v_ref, out_ref, *rest):
        send_sems, recv_sems, local_sem = rest[-3:]
        x, y, c, chips = _here()
        me, sibling = (x, y, c), (x, y, 1 - c)

        def slot(px, py, pc):
            return out_ref.at[4 * px + 2 * py + pc]

        def copy(k, block, to, src=None):
            return pltpu.make_async_remote_copy(
                src_ref=slot(*block) if src is None else src, dst_ref=slot(*block),
                send_sem=send_sems.at[k], recv_sem=recv_sems.at[k], device_id=to, device_id_type=MESH)

        mine = pltpu.make_async_copy(v_ref, slot(*me), local_sem)
        mine.start()
        first = [copy(0, me, sibling, src=v_ref)]
        first += [copy(1 + j, me, (*chip, c), src=v_ref) for j, chip in enumerate(chips)]
        for cp in first:
            cp.start()
        passed = [copy(4 + j, (*chip, c), sibling) for j, chip in enumerate(chips)]
        for j, chip in enumerate(chips):
            copy(1 + j, (*chip, c), me).wait_recv()
            passed[j].start()
        copy(0, sibling, me).wait_recv()
        for j, chip in enumerate(chips):
            copy(4 + j, (*chip, 1 - c), me).wait_recv()
        for cp in first + passed:
            cp.wait_send()
        mine.wait()
        if reduce:
            total = out_ref[0]
            for d in range(1, N_DEV):
                total = total + out_ref[d]
            rest[0][...] = total

    vm = pl.BlockSpec(memory_space=pltpu.VMEM)
    out_shape = [jax.ShapeDtypeStruct((N_DEV, r, LANES), F32)] + ([jax.ShapeDtypeStruct((r, LANES), F32)] if reduce else [])
    return pl.pallas_call(
        body, name=name, in_specs=[vm], out_specs=[vm] * len(out_shape), out_shape=out_shape,
        scratch_shapes=[pltpu.SemaphoreType.DMA((7,)), pltpu.SemaphoreType.DMA((7,)), pltpu.SemaphoreType.DMA],
        compiler_params=pltpu.CompilerParams(vmem_limit_bytes=VMEM_LIMIT),
    )(v)


REDUCE_BLOCK_BYTES = 2 << 20


def _reduce_rows(r, c):
    cands = [b for b in range(16, r + 1, 16) if r % b == 0 and b * c * 4 <= REDUCE_BLOCK_BYTES]
    return max(cands)


def _add_halves(mine2, other, c_idx, *, name):
    _, nq, r, c = mine2.shape
    br = _reduce_rows(r, c)

    def body(c_ref, a_ref, b_ref, o_ref):
        o_ref[...] = (a_ref[0] + b_ref[...]).astype(o_ref.dtype)

    blk = pl.BlockSpec((1, br, c), lambda q, i, c_ref: (q, i, 0))
    return pl.pallas_call(
        body, name=name,
        grid_spec=pltpu.PrefetchScalarGridSpec(
            num_scalar_prefetch=1, grid=(nq, r // br),
            in_specs=[pl.BlockSpec((1, 1, br, c), lambda q, i, c_ref: (c_ref[0], q, i, 0)), blk],
            out_specs=blk),
        out_shape=jax.ShapeDtypeStruct((nq, r, c), BF16),
        compiler_params=_params(("parallel", "parallel")),
    )(c_idx, mine2, other)


def _sum_chips(parts, *, name):
    _, r, c = parts.shape
    br = _reduce_rows(r, c)

    def body(p_ref, o_ref):
        total = p_ref[0].astype(F32)
        for q in range(1, 4):
            total = total + p_ref[q].astype(F32)
        o_ref[...] = total

    return pl.pallas_call(
        body, name=name, grid=(r // br,),
        in_specs=[pl.BlockSpec((4, br, c), lambda i: (0, i, 0))],
        out_specs=pl.BlockSpec((br, c), lambda i: (i, 0)),
        out_shape=jax.ShapeDtypeStruct((r, c), F32),
        compiler_params=_params(("parallel",)),
    )(parts)


BIG = ("w_in", "w_out_sb", "w_out_ssd", "w_out_rw", "w_o")
BIG_AXIS = {"w_in": 2, "w_out_sb": 2, "w_out_ssd": 1, "w_out_rw": 2, "w_o": 1}
SMALL_SHARDED = {"conv_w": 320, "rw_w_up": 128, "rw_a_up": 128}
SMALL = ("norm_g", "conv_w", "conv_b", "dt_bias", "a_log", "d_skip", "ssd_norm_g", "rw_mu", "rw_w0", "rw_w_up",
         "rw_a0", "rw_a_up", "rw_k_k", "rw_k_a", "rw_r_k", "rw_ln_g", "rw_ln_b", "final_g")


def _rows_of(a):
    flat = a.reshape(-1)
    pad = (-flat.shape[0]) % LANES
    return jnp.pad(flat, (0, pad)).reshape(-1, LANES)


def _pack_rows(arrays, multiple=8):
    rows = jnp.concatenate([_rows_of(a) for a in arrays], axis=0)
    pad = (-rows.shape[0]) % multiple
    return jnp.pad(rows, ((0, pad), (0, 0)))


def _unpack_rows(rows, shapes):
    out, off = [], 0
    for shp in shapes:
        n = 1
        for d in shp:
            n *= d
        nr = -(-n // LANES)
        out.append(rows[off:off + nr].reshape(-1)[:n].reshape(shp))
        off += nr
    return out


COL_MAP = ((0, 3072, 0), (3072, 4352, C_XBC), (4352, 4368, C_DT), (4368, 6544, C_RW), (6544, 9616, C_GATES))
SHARD_COLS = N_IN // 4


def _w_in_from_shards(shards):
    pieces = []
    for a, b, dst in sorted(COL_MAP, key=lambda m: m[2]):
        if pieces and dst > pieces[-1][0]:
            pieces.append((dst, jnp.zeros((shards[0].shape[0], dst - pieces[-1][0]), shards[0].dtype)))
        for q in range(4):
            lo, hi = max(a, q * SHARD_COLS), min(b, (q + 1) * SHARD_COLS)
            if lo < hi:
                pieces.append((dst + hi - a, shards[q][:, lo - q * SHARD_COLS:hi - q * SHARD_COLS]))
    return jnp.concatenate([p for _, p in pieces], axis=1)


def _w_in_shard(g, q):
    pieces = []
    for a, b, dst in COL_MAP:
        lo, hi = max(a, q * SHARD_COLS), min(b, (q + 1) * SHARD_COLS)
        if lo < hi:
            pieces.append(g[:, dst + lo - a:dst + hi - a])
    return jnp.concatenate(pieces, axis=1)


def _row_halves(a):
    return a.reshape(2, a.shape[0] // 2, a.shape[1])


def _join_halves(core, mine, theirs):
    return jnp.where(core == 0, jnp.concatenate([mine, theirs], axis=-2), jnp.concatenate([theirs, mine], axis=-2))


def kernel(x, norm_g, w_in, conv_w, conv_b, dt_bias, a_log, d_skip, ssd_norm_g, rw_mu, rw_w0, rw_w_up, rw_a0, rw_a_up, rw_k_k, rw_k_a, rw_r_k, rw_ln_g, rw_ln_b, w_out_sb, w_out_ssd, w_out_rw, w_o, final_g, loss_target, m_norm_g, m_w_in, m_conv_w, m_conv_b, m_dt_bias, m_a_log, m_d_skip, m_ssd_norm_g, m_rw_mu, m_rw_w0, m_rw_w_up, m_rw_a0, m_rw_a_up, m_rw_k_k, m_rw_k_a, m_rw_r_k, m_rw_ln_g, m_rw_ln_b, m_w_out_sb, m_w_out_ssd, m_w_out_rw, m_w_o, m_final_g, v_norm_g, v_w_in, v_conv_w, v_conv_b, v_dt_bias, v_a_log, v_d_skip, v_ssd_norm_g, v_rw_mu, v_rw_w0, v_rw_w_up, v_rw_a0, v_rw_a_up, v_rw_k_k, v_rw_k_a, v_rw_r_k, v_rw_ln_g, v_rw_ln_b, v_w_out_sb, v_w_out_ssd, v_w_out_rw, v_w_o, v_final_g):
    names = ("norm_g", "w_in", "conv_w", "conv_b", "dt_bias", "a_log", "d_skip", "ssd_norm_g", "rw_mu", "rw_w0",
             "rw_w_up", "rw_a0", "rw_a_up", "rw_k_k", "rw_k_a", "rw_r_k", "rw_ln_g", "rw_ln_b", "w_out_sb",
             "w_out_ssd", "w_out_rw", "w_o", "final_g")
    w_loc = dict(zip(names, (norm_g, w_in, conv_w, conv_b, dt_bias, a_log, d_skip, ssd_norm_g, rw_mu, rw_w0, rw_w_up,
                             rw_a0, rw_a_up, rw_k_k, rw_k_a, rw_r_k, rw_ln_g, rw_ln_b, w_out_sb, w_out_ssd, w_out_rw,
                             w_o, final_g)))
    m_loc = dict(zip(names, (m_norm_g, m_w_in, m_conv_w, m_conv_b, m_dt_bias, m_a_log, m_d_skip, m_ssd_norm_g,
                             m_rw_mu, m_rw_w0, m_rw_w_up, m_rw_a0, m_rw_a_up, m_rw_k_k, m_rw_k_a, m_rw_r_k,
                             m_rw_ln_g, m_rw_ln_b, m_w_out_sb, m_w_out_ssd, m_w_out_rw, m_w_o, m_final_g)))
    v_loc = dict(zip(names, (v_norm_g, v_w_in, v_conv_w, v_conv_b, v_dt_bias, v_a_log, v_d_skip, v_ssd_norm_g,
                             v_rw_mu, v_rw_w0, v_rw_w_up, v_rw_a0, v_rw_a_up, v_rw_k_k, v_rw_k_a, v_rw_r_k,
                             v_rw_ln_g, v_rw_ln_b, v_w_out_sb, v_w_out_ssd, v_w_out_rw, v_w_o, v_final_g)))
    chip = 2 * lax.axis_index("x") + lax.axis_index("y")
    core = lax.axis_index("c")

    def gather_srcs(i):
        return [_row_halves(w_loc[n][i].astype(BF16)) for n in BIG]

    def gathered(mine, nm):
        theirs = _sibling_swap(mine, other_slot=False, name=nm)
        out = {}
        for n, a, b in zip(BIG, mine, theirs):
            shards = _join_halves(core, a, b)
            out[n] = (_w_in_from_shards([shards[q] for q in range(4)]) if n == "w_in"
                      else jnp.concatenate([shards[q] for q in range(4)], axis=BIG_AXIS[n] - 1))
        return out

    full = {}
    sm_names = tuple(SMALL_SHARDED)
    sm_shapes = [w_loc[n].shape for n in sm_names]
    (got_sm,) = _allgather_small(_pack_rows([w_loc[n] for n in sm_names]), reduce=False, name="gather_small")
    per_chip = [_unpack_rows(got_sm[4 * (q // 2) + 2 * (q % 2)], sm_shapes) for q in range(4)]
    for i, n in enumerate(sm_names):
        full[n] = jnp.concatenate([per_chip[q][i] for q in range(4)], axis=-1)

    def pad16(a):
        return jnp.zeros((1, LANES), F32).at[0, :SSD_HEADS].set(a)

    def layer_params(i, big):
        row = lambda n: w_loc[n][i].reshape(1, -1)
        cw = full["conv_w"][i]
        return dict(
            norm_g=row("norm_g"), w_in=big["w_in"], conv=[cw[k][None] for k in range(4)] + [row("conv_b")],
            dt_bias=pad16(dt_bias[i]), a_log=pad16(a_log[i]), d_skip=pad16(d_skip[i]),
            ssd_norm_g=row("ssd_norm_g"), rw_mu=row("rw_mu"),
            rw_pre=[row("rw_w0"), jnp.zeros((LANES, 512), F32).at[:HEAD].set(full["rw_w_up"][i]), row("rw_a0"),
                    jnp.zeros((LANES, 512), F32).at[HEAD:].set(full["rw_a_up"][i]), row("rw_k_k"), row("rw_k_a")],
            rw_post=[row("rw_ln_g"), row("rw_ln_b"), row("rw_r_k")],
            w_out_sb=big["w_out_sb"], w_out_ssd=big["w_out_ssd"], w_out_rw=big["w_out_rw"], w_o=big["w_o"])

    c_idx = core.reshape(1).astype(jnp.int32)

    def reduce_prepare(items, nm):
        sends = []
        for g, n, _ in items:
            per_chip = ([_w_in_shard(g[n], q) for q in range(4)] if n == "w_in"
                        else jnp.split(g[n], 4, axis=BIG_AXIS[n] - 1))
            sends.append(jnp.stack([_row_halves(p) for p in per_chip], axis=1))
        others = _sibling_swap(sends, other_slot=True, name=nm + "sibling")
        return [_add_halves(s, o, c_idx, name=nm + "add_" + lab) for (_, _, lab), s, o in zip(items, sends, others)]

    def reduce_finish(exchanged, labels, nm):
        mine = [_sum_chips(p, name=nm + "sum_" + lab) for lab, p in zip(labels, exchanged)]
        theirs = _sibling_swap(mine, other_slot=False, name=nm + "join")
        return {lab: _join_halves(core, a, b) for lab, a, b in zip(labels, mine, theirs)}

    assert DEPTH == 2
    out_proj = BIG[1:]
    params, xs, saved, grads = [None] * 2, [x[0], None, None], [None] * 2, [None] * 2
    params[0] = layer_params(0, gathered(_chip_exchange(gather_srcs(0), per_dest=False, name="gather_l0"), "gather_l0_join"))
    xs[1], saved[0], got = _layer_fwd(xs[0], params[0], "l0_", side=(gather_srcs(1), False))
    params[1] = layer_params(1, gathered(got, "gather_l1_join"))
    xs[2], saved[1], _ = _layer_fwd(xs[1], params[1], "l1_")
    dx, loss_row, g_final = _final(xs[2], final_g.reshape(1, -1), loss_target[0], bt=BT, name="final")
    dx, grads[1], _, _ = _layer_bwd(xs[1], dx, params[1], saved[1], "l1_")
    early = lambda g: [(grads[1], n, "l1_" + n) for n in BIG] + [(g, n, "l0_" + n) for n in out_proj]
    dx, grads[0], got, got_late = _layer_bwd(
        xs[0], dx, params[0], saved[0], "l0_",
        side=lambda g: (reduce_prepare(early(g), "reduce_early_"), True),
        side_late=lambda g: (reduce_prepare([(g, "w_in", "l0_w_in")], "reduce_late_"), True))
    total = reduce_finish(got + got_late, [lab for _, _, lab in early(None)] + ["l0_w_in"], "reduce_")
    totals = [{n: total[f"l{i}_" + n] for n in BIG} for i in range(DEPTH)]

    def stacked(fn):
        return jnp.stack([fn(grads[i]) for i in range(DEPTH)])

    g_loc = {
        "norm_g": stacked(lambda g: g["norm_g"][0]),
        "conv_w": stacked(lambda g: jnp.concatenate(g["conv"][:4], axis=0)),
        "conv_b": stacked(lambda g: g["conv"][4][0]),
        "dt_bias": stacked(lambda g: g["dt_bias"][0, :SSD_HEADS]),
        "a_log": stacked(lambda g: g["a_log"][0, :SSD_HEADS]),
        "d_skip": stacked(lambda g: g["d_skip"][0, :SSD_HEADS]),
        "ssd_norm_g": stacked(lambda g: g["ssd_norm_g"][0]),
        "rw_mu": stacked(lambda g: g["rw_mu"][0]),
        "rw_w0": stacked(lambda g: g["rw_pre"][0][0]),
        "rw_w_up": stacked(lambda g: g["rw_pre"][1][:HEAD]),
        "rw_a0": stacked(lambda g: g["rw_pre"][2][0]),
        "rw_a_up": stacked(lambda g: g["rw_pre"][3][HEAD:]),
        "rw_k_k": stacked(lambda g: g["rw_pre"][4][0]),
        "rw_k_a": stacked(lambda g: g["rw_pre"][5][0]),
        "rw_r_k": stacked(lambda g: g["rw_r_k"].reshape(8, HEAD)),
        "rw_ln_g": stacked(lambda g: g["rw_ln_g"][0]),
        "rw_ln_b": stacked(lambda g: g["rw_ln_b"][0]),
        "final_g": g_final[0],
    }

    g_out = {n: jnp.stack([totals[0][n], totals[1][n]]) for n in BIG}

    sm_all = SMALL + ("loss",)
    sm_full_shapes = [g_loc[n].shape for n in SMALL] + [(1,)]
    _, summed = _allgather_small(_pack_rows([g_loc[n] for n in SMALL] + [loss_row[0, :1]]), reduce=True, name="reduce_small")
    sm = dict(zip(sm_all, _unpack_rows(summed, sm_full_shapes)))
    for n in SMALL:
        g_out[n] = sm[n]
    for n, wd in SMALL_SHARDED.items():
        g_out[n] = lax.dynamic_slice_in_dim(sm[n], chip * wd, wd, axis=sm[n].ndim - 1)
    loss = sm["loss"][0]

    upd = {n: _adamw(w_loc[n], g_out[n], m_loc[n], v_loc[n], name="adamw_" + n) for n in names if n != "w_in"}
    cols = SHARD_COLS // 4
    to_cols = lambda a: jnp.transpose(a, (2, 0, 1)).reshape(4, cols, DEPTH, D_MODEL)
    from_cols = lambda a: jnp.transpose(a.reshape(SHARD_COLS, DEPTH, D_MODEL), (1, 2, 0))
    g_cols = lax.optimization_barrier(to_cols(g_out["w_in"]))
    g_out["w_in"] = from_cols(g_cols)
    upd["w_in"] = tuple(from_cols(a) for a in _adamw(
        to_cols(w_loc["w_in"]), g_cols, to_cols(m_loc["w_in"]), to_cols(v_loc["w_in"]),
        name="adamw_w_in", block=(1, cols, DEPTH, D_MODEL // 2)))
    return (loss, dx[None], *[g_out[n] for n in names], *[upd[n][0] for n in names],
            *[upd[n][1] for n in names], *[upd[n][2] for n in names])
```

```python
import functools

import jax
import jax.numpy as jnp
from jax import lax
from jax.experimental import pallas as pl
from jax.experimental.pallas import tpu as pltpu

F32 = jnp.float32
BF16 = jnp.bfloat16

D_MODEL = 1024
DEPTH = 2
HEAD = 64
LANES = 128
CHUNK = 128
RMS_EPS = 1e-6
GN_EPS = 64e-5
VMEM_LIMIT = 56 * 1024 * 1024

N_IN = 9616
N_PAD = 9728
C_SB, C_Z, C_GATES, C_RW, C_LO, C_DT, C_XBC = 0, 2048, 3072, 6144, 8192, 8320, 8448
RW_COLS = 2176
XBC_COLS = 1280

ADAM_LR, ADAM_B1, ADAM_B2, ADAM_EPS, ADAM_WD, ADAM_STEP = 0.001, 0.9, 0.999, 1e-08, 0.01, 10


def _params(sem=None):
    return pltpu.CompilerParams(dimension_semantics=sem, vmem_limit_bytes=VMEM_LIMIT)


@jax.custom_vjp
def _sigmoid(x):
    return 1.0 / (1.0 + jnp.exp(-x))


def _sigmoid_fwd(x):
    s = _sigmoid(x)
    return s, s


def _sigmoid_bwd(s, g):
    return (g * s * (1.0 - s),)


_sigmoid.defvjp(_sigmoid_fwd, _sigmoid_bwd)


@jax.custom_vjp
def _silu(x):
    return x * _sigmoid(x)


def _silu_fwd(x):
    s = _sigmoid(x)
    return x * s, (x, s)


def _silu_bwd(res, g):
    x, s = res
    return (g * (s + x * s * (1.0 - s)),)


_silu.defvjp(_silu_fwd, _silu_bwd)


@jax.custom_vjp
def _softplus(x):
    return jnp.maximum(x, 0.0) + jnp.log(1.0 + jnp.exp(-jnp.abs(x)))


def _softplus_fwd(x):
    return _softplus(x), x


def _softplus_bwd(x, g):
    return (g * _sigmoid(x),)


_softplus.defvjp(_softplus_fwd, _softplus_bwd)


def _dot(a, b, dims):
    return lax.dot_general(a.astype(BF16), b.astype(BF16), (dims, ((), ())), preferred_element_type=F32)


def _dot_nn(a, b):
    return _dot(a, b, ((1,), (0,)))


def _dot_nt(a, b):
    return _dot(a, b, ((1,), (1,)))


def _dot_tn(a, b):
    return _dot(a, b, ((0,), (0,)))


@jax.custom_vjp
def _bdot(a, b):
    return _dot_nn(a, b)


def _bdot_fwd(a, b):
    return _dot_nn(a, b), (a, b)


def _bdot_bwd(res, g):
    a, b = res
    return _dot_nt(g, b), _dot_tn(a, g)


_bdot.defvjp(_bdot_fwd, _bdot_bwd)


def _split2(x):
    hi = x.astype(BF16)
    lo = (x - hi.astype(F32)).astype(BF16)
    return hi, lo


_NT = (((1,), (1,)), ((), ()))
_NN = (((1,), (0,)), ((), ()))
_TN = (((0,), (0,)), ((), ()))


def _dot2(x, m, dn=_NN):
    hi, lo = _split2(x)
    return (lax.dot_general(hi, m, dn, preferred_element_type=F32)
            + lax.dot_general(lo, m, dn, preferred_element_type=F32))


def _seg_matrix(n):
    r = lax.broadcasted_iota(jnp.int32, (n, n), 0) // HEAD
    c = lax.broadcasted_iota(jnp.int32, (n, n), 1) // HEAD
    return (r == c).astype(BF16)


@jax.custom_vjp
def _segsum2(x, seg):
    return _dot2(x, seg)


def _segsum2_fwd(x, seg):
    return _dot2(x, seg), seg


def _segsum2_bwd(seg, g):
    return _dot2(g, seg), jnp.zeros_like(seg)


_segsum2.defvjp(_segsum2_fwd, _segsum2_bwd)


def _make_segsum(seg):
    return lambda x: _segsum2(x, seg)


def _shift_down_raw(x, k):
    row = lax.broadcasted_iota(jnp.int32, x.shape, 0)
    return jnp.where(row >= k, pltpu.roll(x, k, 0), 0.0)


def _shift_up_raw(x, k):
    t = x.shape[0]
    row = lax.broadcasted_iota(jnp.int32, x.shape, 0)
    return jnp.where(row < t - k, pltpu.roll(x, t - k, 0), 0.0)


@functools.partial(jax.custom_vjp, nondiff_argnums=(1,))
def _shift_down(x, k):
    return _shift_down_raw(x, k)


def _shift_down_fwd(x, k):
    return _shift_down_raw(x, k), None


def _shift_down_bwd(k, _, g):
    return (_shift_up_raw(g, k),)


_shift_down.defvjp(_shift_down_fwd, _shift_down_bwd)


def _mm(a, b, *, name, ta=False, tb=False, add=None, out_dtype=F32, tm=2048, tn=512, tk=None, side=None):
    m, k = (a.shape[1], a.shape[0]) if ta else a.shape
    n = b.shape[0] if tb else b.shape[1]
    tm, tn = min(tm, m), min(tn, n)
    tk = k if tk is None else tk
    nk = k // tk
    assert m % tm == 0 and n % tn == 0 and k % tk == 0
    dims = ((0 if ta else 1,), (1 if tb else 0,))

    def body(a_ref, b_ref, *refs):
        o_ref, acc_ref = refs[-2:]
        p = _dot(a_ref[...], b_ref[...], dims)

        def emit(total):
            if add is not None:
                total = total + refs[0][...]
            o_ref[...] = total.astype(o_ref.dtype)

        if nk == 1:
            emit(p)
        else:
            kk = pl.program_id(2)

            @pl.when(kk == 0)
            def _():
                acc_ref[...] = p

            @pl.when(kk > 0)
            def _():
                acc_ref[...] += p

            @pl.when(kk == nk - 1)
            def _():
                emit(acc_ref[...])

    a_spec = pl.BlockSpec((tk, tm), lambda i, j, kk: (kk, i)) if ta else pl.BlockSpec((tm, tk), lambda i, j, kk: (i, kk))
    b_spec = pl.BlockSpec((tn, tk), lambda i, j, kk: (j, kk)) if tb else pl.BlockSpec((tk, tn), lambda i, j, kk: (kk, j))
    o_spec = pl.BlockSpec((tm, tn), lambda i, j, kk: (i, j))
    res = _call_with_side(
        body, side, name=name, grid=(m // tm, n // tn, nk), semantics=("parallel", "parallel", "arbitrary"),
        in_specs=[a_spec, b_spec] + ([o_spec] if add is not None else []), out_specs=[o_spec],
        out_shape=[jax.ShapeDtypeStruct((m, n), out_dtype)],
        scratch_shapes=[pltpu.VMEM((tm, tn) if nk > 1 else (8, LANES), F32)],
        operands=(a, b) + ((add,) if add is not None else ()))
    return res[0] if side is None else (res[0], res[1])


def _row_specs(rows, bt):
    return [pl.BlockSpec((bt, w), functools.partial(lambda i, c: (i, c), c=c)) for _, w, c in rows]


def _full_spec(p):
    return pl.BlockSpec(p.shape, functools.partial(lambda i, nd: (0,) * nd, nd=p.ndim))


def _rowwise(f, rows, pars, out_widths, *, bt, name, acc_widths=()):
    t = rows[0][0].shape[0]
    nr, npar, no, na = len(rows), len(pars), len(out_widths), len(acc_widths)

    def body(*refs):
        vals = [r[...] for r in refs[:nr + npar]]
        outs = f(*vals)
        for o_ref, o in zip(refs[nr + npar:nr + npar + no], outs[:no]):
            o_ref[...] = o.astype(o_ref.dtype)
        if na:
            first = pl.program_id(0) == 0
            for a_ref, a in zip(refs[nr + npar + no:], outs[no:]):
                @pl.when(first)
                def _():
                    a_ref[...] = jnp.zeros_like(a_ref)
                a_ref[...] += a

    return pl.pallas_call(
        body, name=name, grid=(t // bt,),
        in_specs=_row_specs(rows, bt) + [_full_spec(p) for p in pars],
        out_specs=[pl.BlockSpec((bt, w), lambda i: (i, 0)) for w in out_widths]
        + [pl.BlockSpec((1, w), lambda i: (0, 0)) for w in acc_widths],
        out_shape=[jax.ShapeDtypeStruct((t, w), F32) for w in out_widths]
        + [jax.ShapeDtypeStruct((1, w), F32) for w in acc_widths],
        compiler_params=_params(("arbitrary",)),
    )(*[r[0] for r in rows], *pars)


def _rowwise_bwd(f, rows, pars, douts, *, bt, name, groups=None):
    t = rows[0][0].shape[0]
    nr, npar, nd = len(rows), len(pars), len(douts)
    groups = [[i] for i in range(nr)] if groups is None else groups
    widths = [r[1] for r in rows]

    def body(*refs):
        vals = [r[...] for r in refs[:nr + npar]]
        cts = tuple(r[...] for r in refs[nr + npar:nr + npar + nd])
        _, vjp = jax.vjp(lambda *a: tuple(f(*a)), *vals)
        grads = vjp(cts)
        out_refs = refs[nr + npar + nd:]
        for g_ref, grp in zip(out_refs[:len(groups)], groups):
            off = 0
            for i in grp:
                g_ref[:, off:off + widths[i]] = grads[i]
                off += widths[i]
        first = pl.program_id(0) == 0
        for p_ref, g in zip(out_refs[len(groups):], grads[nr:]):
            @pl.when(first)
            def _():
                p_ref[...] = jnp.zeros_like(p_ref)
            p_ref[...] += g

    gw = [sum(widths[i] for i in grp) for grp in groups]
    return pl.pallas_call(
        body, name=name, grid=(t // bt,),
        in_specs=_row_specs(rows, bt) + [_full_spec(p) for p in pars] + _row_specs(douts, bt),
        out_specs=[pl.BlockSpec((bt, w), lambda i: (i, 0)) for w in gw] + [_full_spec(p) for p in pars],
        out_shape=[jax.ShapeDtypeStruct((t, w), F32) for w in gw] + [jax.ShapeDtypeStruct(p.shape, F32) for p in pars],
        compiler_params=_params(("arbitrary",)),
    )(*[r[0] for r in rows], *pars, *[d[0] for d in douts])


def _colwise(f, x, c0, ncols, pars, *, bc, name):
    t = x.shape[0]

    def body(x_ref, *refs):
        o_ref = refs[-1]
        o_ref[...] = f(x_ref[...], *[r[...] for r in refs[:-1]])

    return pl.pallas_call(
        body, name=name, grid=(ncols // bc,),
        in_specs=[pl.BlockSpec((t, bc), lambda j: (0, j + c0 // bc))]
        + [pl.BlockSpec((p.shape[0], bc), lambda j: (0, j)) for p in pars],
        out_specs=pl.BlockSpec((t, bc), lambda j: (0, j)),
        out_shape=jax.ShapeDtypeStruct((t, ncols), F32),
        compiler_params=_params(("parallel",)),
    )(x, *pars)


def _colwise_bwd(f, x, c0, ncols, pars, dout, *, bc, name):
    t = x.shape[0]
    npar = len(pars)

    def body(x_ref, *refs):
        vals = [x_ref[...]] + [r[...] for r in refs[:npar]]
        _, vjp = jax.vjp(f, *vals)
        grads = vjp(refs[npar][...])
        for g_ref, g in zip(refs[npar + 1:], grads):
            g_ref[...] = g

    return pl.pallas_call(
        body, name=name, grid=(ncols // bc,),
        in_specs=[pl.BlockSpec((t, bc), lambda j: (0, j + c0 // bc))]
        + [pl.BlockSpec((p.shape[0], bc), lambda j: (0, j)) for p in pars]
        + [pl.BlockSpec((t, bc), lambda j: (0, j))],
        out_specs=[pl.BlockSpec((t, bc), lambda j: (0, j))]
        + [pl.BlockSpec((p.shape[0], bc), lambda j: (0, j)) for p in pars],
        out_shape=[jax.ShapeDtypeStruct((t, ncols), F32)] + [jax.ShapeDtypeStruct(p.shape, F32) for p in pars],
        compiler_params=_params(("parallel",)),
    )(x, *pars, dout)


def _f_rms(x, g):
    return (x * lax.rsqrt(jnp.mean(x * x, axis=-1, keepdims=True) + RMS_EPS) * g,)


def _f_sb_gate(y, gate):
    return (y * _silu(gate),)


def _f_ssd_norm(y, z, g):
    u = y * _silu(z)
    return (u * lax.rsqrt(jnp.mean(u * u, axis=-1, keepdims=True) + RMS_EPS) * g,)


def _f_merge(p_sb, p_ssd, p_rw, g_sb, g_ssd, g_rw):
    return (_sigmoid(g_sb) * p_sb + _sigmoid(g_ssd) * p_ssd + _sigmoid(g_rw) * p_rw,)


def _f_rw_pre(k, lo, w0, w_up, a0, a_up, k_k, k_a):
    segsum = _make_segsum(_seg_matrix(k.shape[1]))
    lane = lax.broadcasted_iota(jnp.int32, lo.shape, 1)
    w_lo = jnp.where(lane < HEAD, jnp.tanh(lo), 0.0)
    a_lo = jnp.where(lane >= HEAD, lo, 0.0)
    w = -_softplus(-(w0 + _bdot(w_lo, w_up))) - 0.5
    log_decay = -jnp.exp(w)
    a = _sigmoid(a0 + _bdot(a_lo, a_up))
    kk = k * k_k
    kk = kk / jnp.maximum(jnp.sqrt(segsum(kk * kk)), 1e-12)
    return log_decay, k * (1.0 + (a - 1.0) * k_a), -kk, kk * a


def _f_rw_post(y, r, k2, v, gate, ln_g, ln_b, r_k):
    segsum = _make_segsum(_seg_matrix(y.shape[1]))
    yc = y - segsum(y) * (1.0 / HEAD)
    var = segsum(yc * yc) * (1.0 / HEAD)
    yn = yc * lax.rsqrt(var + GN_EPS) * ln_g + ln_b
    return ((yn + segsum(r * k2 * r_k) * v) * _silu(gate),)


def _f_rw_mix(slab, mu):
    return slab + (_shift_down(slab, 1) - slab) * mu


def _f_conv(x, w0, w1, w2, w3, b):
    acc = x * w3 + b
    for i, w in enumerate((w0, w1, w2)):
        acc = acc + _shift_down(x, 3 - i) * w
    return _silu(acc)


def _log_sigmoid(z):
    return jnp.minimum(z, 0.0) - jnp.log(1.0 + jnp.exp(-jnp.abs(z)))


SB_BQ = 256
SB_BK = 256
assert SB_BQ == SB_BK


def _tri_ones(kind):
    j = lax.broadcasted_iota(jnp.int32, (SB_BK, SB_BK + LANES), 0)
    s = lax.broadcasted_iota(jnp.int32, (SB_BK, SB_BK + LANES), 1)
    tri = {"gt": j > s, "le": j <= s, "lt": j < s}[kind]
    return (tri | (s >= SB_BK)).astype(BF16)


def _sb_common(q_ref):
    lane = lax.broadcasted_iota(jnp.int32, (SB_BQ, LANES), 1)
    q = q_ref[...] * (HEAD ** -0.5)
    q2 = jnp.concatenate([jnp.where(lane < HEAD, q, 0.0), jnp.where(lane >= HEAD, q, 0.0)], axis=0).astype(BF16)
    diff = (lax.broadcasted_iota(jnp.int32, (2 * SB_BQ, SB_BK), 1)
            - (lax.broadcasted_iota(jnp.int32, (2 * SB_BQ, SB_BK), 0) & (SB_BQ - 1)))
    return lane, q2, diff


def _rep(x):
    return jnp.concatenate([x] * (SB_BK // LANES), axis=1)


def _sb2_specs(t):
    q = pl.BlockSpec((SB_BQ, LANES), lambda j, i: (i, j))
    k = pl.BlockSpec((t, LANES), lambda j, i: (0, 4 + j))
    v = pl.BlockSpec((t, LANES), lambda j, i: (0, 8 + j))
    return q, k, v


def _sb2_fwd(proj, *, name):
    t = proj.shape[0]

    def body(q_ref, k_ref, v_ref, y_ref, lt_ref):
        i = pl.program_id(1)
        lane, q2, diff = _sb_common(q_ref)
        m_f = _tri_ones("gt")

        def step(kb, carry, diagonal):
            c, acc = carry
            off = pl.multiple_of(kb * SB_BK, SB_BK)
            kblk = k_ref[pl.ds(off, SB_BK), :].astype(BF16)
            vblk = v_ref[pl.ds(off, SB_BK), :].astype(BF16)
            z = lax.dot_general(q2, kblk, _NT, preferred_element_type=F32)
            lb = _log_sigmoid(z)
            lk = jnp.where(diff < 0, lb - z, 0.0) if diagonal else lb - z
            w2 = _dot2(lk, m_f)
            att = jnp.exp(lb + _rep(c) + w2[:, :SB_BK])
            if diagonal:
                att = jnp.where(diff < 0, att, 0.0)
            acc = acc + lax.dot_general(att.astype(BF16), vblk, _NN, preferred_element_type=F32)
            return c + w2[:, SB_BK:], acc

        zero = jnp.zeros((2 * SB_BQ, LANES), F32)
        c, acc = lax.fori_loop(0, i, lambda it, carry: step(i - 1 - it, carry, False), step(i, (zero, zero), True))
        y_ref[...] = jnp.where(lane < HEAD, acc[:SB_BQ], acc[SB_BQ:])
        lt_ref[0] = c[:SB_BQ]
        lt_ref[1] = c[SB_BQ:]

    return pl.pallas_call(
        body, name=name, grid=(4, t // SB_BQ),
        in_specs=list(_sb2_specs(t)),
        out_specs=[pl.BlockSpec((SB_BQ, LANES), lambda j, i: (i, j)),
                   pl.BlockSpec((2, SB_BQ, LANES), lambda j, i: (j, i, 0))],
        out_shape=[jax.ShapeDtypeStruct((t, 4 * LANES), F32), jax.ShapeDtypeStruct((8, t, LANES), F32)],
        compiler_params=_params(("parallel", "arbitrary")),
    )(proj, proj, proj)


def _sb2_bwd(proj, dy, lt, *, name):
    t = proj.shape[0]

    def body(q_ref, k_ref, v_ref, dy_ref, lt_ref, dq_ref, dk_ref, dv_ref):
        i = pl.program_id(1)

        @pl.when(i == 0)
        def _():
            dk_ref[...] = jnp.zeros_like(dk_ref)
            dv_ref[...] = jnp.zeros_like(dv_ref)

        lane, q2, diff = _sb_common(q_ref)
        m_le, m_lt = _tri_ones("le"), _tri_ones("lt")
        dy_blk = dy_ref[...]
        do2 = jnp.concatenate([jnp.where(lane < HEAD, dy_blk, 0.0), jnp.where(lane >= HEAD, dy_blk, 0.0)],
                              axis=0).astype(BF16)
        lt2 = jnp.concatenate([lt_ref[0], lt_ref[1]], axis=0)

        def step(kb, carry, diagonal):
            cp, cg, dq = carry
            off = pl.multiple_of(kb * SB_BK, SB_BK)
            kblk = k_ref[pl.ds(off, SB_BK), :].astype(BF16)
            vblk = v_ref[pl.ds(off, SB_BK), :].astype(BF16)
            z = lax.dot_general(q2, kblk, _NT, preferred_element_type=F32)
            lb = _log_sigmoid(z)
            lk = jnp.where(diff < 0, lb - z, 0.0) if diagonal else lb - z
            w2 = _dot2(lk, m_le)
            att = jnp.exp(lb + _rep(lt2 - cp) - w2[:, :SB_BK])
            if diagonal:
                att = jnp.where(diff < 0, att, 0.0)
            d_e = lax.dot_general(do2, vblk, _NT, preferred_element_type=F32) * att
            g2 = _dot2(d_e, m_lt)
            sig = jnp.exp(lb)
            dz = d_e * (1.0 - sig) - (_rep(cg) + g2[:, :SB_BK]) * sig
            dz = (jnp.where(diff < 0, dz, 0.0) if diagonal else dz).astype(BF16)
            dq = dq + lax.dot_general(dz, kblk, _NN, preferred_element_type=F32)
            dk_ref[pl.ds(off, SB_BK), :] += lax.dot_general(dz, q2, _TN, preferred_element_type=F32)
            dv_ref[pl.ds(off, SB_BK), :] += lax.dot_general(att.astype(BF16), do2, _TN, preferred_element_type=F32)
            return cp + w2[:, SB_BK:], cg + g2[:, SB_BK:], dq

        zero = jnp.zeros((2 * SB_BQ, LANES), F32)
        before = lax.fori_loop(0, i, lambda kb, carry: step(kb, carry, False), (zero, zero, zero))
        _, _, dq = step(i, before, True)
        dq_ref[...] = jnp.where(lane < HEAD, dq[:SB_BQ], dq[SB_BQ:]) * (HEAD ** -0.5)

    q_spec, k_spec, v_spec = _sb2_specs(t)
    blk = pl.BlockSpec((SB_BQ, LANES), lambda j, i: (i, j))
    col = pl.BlockSpec((t, LANES), lambda j, i: (0, j))
    return pl.pallas_call(
        body, name=name, grid=(4, t // SB_BQ),
        in_specs=[q_spec, k_spec, v_spec, blk, pl.BlockSpec((2, SB_BQ, LANES), lambda j, i: (j, i, 0))],
        out_specs=[blk, col, col],
        out_shape=[jax.ShapeDtypeStruct((t, 4 * LANES), F32)] * 3,
        compiler_params=_params(("parallel", "arbitrary")),
    )(proj, proj, proj, dy, lt)


SSD_HEADS = 16
SSD_PAIRS = 8


def _split3(x):
    a = x.astype(BF16)
    r = x - a.astype(F32)
    b = r.astype(BF16)
    return a, b, (r - b.astype(F32)).astype(BF16)


def _dot3(x, m, dn=_NN):
    return sum(lax.dot_general(p, m, dn, preferred_element_type=F32) for p in _split3(x))


def _mdot3(m, x):
    return sum(lax.dot_general(m, p, _NN, preferred_element_type=F32) for p in _split3(x))


def _ssd_common(dtr, dtb, alog, acsx_s, acst_s):
    lane = lax.broadcasted_iota(jnp.int32, (CHUNK, LANES), 1)
    lane1 = lax.broadcasted_iota(jnp.int32, (1, LANES), 1)
    arow = jnp.where(lane1 < SSD_HEADS, -jnp.exp(alog), 0.0)
    dt = jnp.where(lane < SSD_HEADS, _softplus(dtr + dtb), 0.0)
    da = dt * arow
    r = lax.broadcasted_iota(jnp.int32, (CHUNK, CHUNK), 0)
    c = lax.broadcasted_iota(jnp.int32, (CHUNK, CHUNK), 1)
    tril = (r >= c).astype(BF16)
    triu = (r <= c).astype(BF16)
    acs = _mdot3(tril, da)
    acst_s[...] = _dot3(da, triu, _TN)
    eh = lax.broadcasted_iota(jnp.int32, (LANES, 8 * LANES), 0)
    e = (eh == lax.broadcasted_iota(jnp.int32, (LANES, 8 * LANES), 1) // HEAD).astype(BF16)
    eh2 = lax.broadcasted_iota(jnp.int32, (LANES, 16 * LANES), 0)
    e2 = (eh2 == lax.broadcasted_iota(jnp.int32, (LANES, 16 * LANES), 1) // LANES).astype(BF16)
    acsx_s[...] = _dot3(acs, e)
    return dt, arow, _dot3(dt, e), _dot3(acs, e2), e, tril, triu


def _ssd_fwd(xc, proj, dtb, alog, dsk, *, name):
    t = xc.shape[0]
    nc = t // CHUNK

    def body(x_ref, b_ref, c_ref, dtr_ref, dtb_ref, alog_ref, dsk_ref, y_ref, hin_ref, acsx_s, acst_s, h_s):
        @pl.when(pl.program_id(0) == 0)
        def _():
            h_s[...] = jnp.zeros_like(h_s)

        dt, arow, dt_x, acs_b, e, tril, _ = _ssd_common(dtr_ref[...], dtb_ref[...], alog_ref[...], acsx_s, acst_s)
        dsk_x = _dot3(jnp.broadcast_to(dsk_ref[...], (CHUNK, LANES)), e)
        lane = lax.broadcasted_iota(jnp.int32, (CHUNK, LANES), 1)
        causal = (lax.broadcasted_iota(jnp.int32, (CHUNK, CHUNK), 0)
                  >= lax.broadcasted_iota(jnp.int32, (CHUNK, CHUNK), 1))
        for j in range(SSD_PAIRS):
            g = j // 4
            sl = slice(j * LANES, (j + 1) * LANES)
            if j % 4 == 0:
                bg = jnp.where(lane // HEAD == g, b_ref[...], 0.0)
                cg = jnp.where(lane // HEAD == g, c_ref[...], 0.0)
                cb = _dot_nt(cg, bg)
            x = x_ref[:, sl]
            a = acsx_s[:, sl]
            at = acsx_s[CHUNK - 1:CHUNK, sl]
            xdt = x * dt_x[:, sl]
            hin = h_s[j]
            hin_ref[0, j] = hin
            y = jnp.exp(a) * _dot_nn(cg, hin) + x * dsk_x[:, sl]
            h_s[j] = jnp.exp(at) * hin + _dot_tn(bg, xdt * jnp.exp(at - a))
            yd = []
            for hh in (0, 1):
                h = 2 * j + hh
                dec = jnp.exp(jnp.minimum(acs_b[:, h * LANES:(h + 1) * LANES] - acst_s[pl.ds(h, 1), :], 0.0))
                yd.append(_dot_nn(jnp.where(causal, cb * dec, 0.0), xdt))
            y_ref[:, sl] = y + jnp.where(lane < HEAD, yd[0], yd[1])

    one = pl.BlockSpec((1, LANES), lambda i: (0, 0))
    return pl.pallas_call(
        body, name=name, grid=(nc,),
        in_specs=[pl.BlockSpec((CHUNK, 8 * LANES), lambda i: (i, 0)),
                  pl.BlockSpec((CHUNK, LANES), lambda i: (i, 8)),
                  pl.BlockSpec((CHUNK, LANES), lambda i: (i, 9)),
                  pl.BlockSpec((CHUNK, LANES), lambda i: (i, C_DT // LANES)), one, one, one],
        out_specs=[pl.BlockSpec((CHUNK, 8 * LANES), lambda i: (i, 0)),
                   pl.BlockSpec((1, SSD_PAIRS, LANES, LANES), lambda i: (i, 0, 0, 0))],
        out_shape=[jax.ShapeDtypeStruct((t, 8 * LANES), F32),
                   jax.ShapeDtypeStruct((nc, SSD_PAIRS, LANES, LANES), F32)],
        scratch_shapes=[pltpu.VMEM((CHUNK, 8 * LANES), F32), pltpu.VMEM((LANES, CHUNK), F32),
                        pltpu.VMEM((SSD_PAIRS, LANES, LANES), F32)],
        compiler_params=_params(("arbitrary",)),
    )(xc, xc, xc, proj, dtb, alog, dsk)


def _ssd_bwd(xc, proj, dtb, alog, dsk, hin_all, dy, *, name):
    t = xc.shape[0]
    nc = t // CHUNK

    def body(x_ref, b_ref, c_ref, dtr_ref, dtb_ref, alog_ref, dsk_ref, hin_ref, dy_ref,
             dxc_ref, ddtr_ref, ddtb_ref, dalog_ref, ddsk_ref, acsx_s, acst_s, dh_s, dax_s, ddx_s):
        @pl.when(pl.program_id(0) == 0)
        def _():
            dh_s[...] = jnp.zeros_like(dh_s)
            ddtb_ref[...] = jnp.zeros_like(ddtb_ref)
            dalog_ref[...] = jnp.zeros_like(dalog_ref)
            ddsk_ref[...] = jnp.zeros_like(ddsk_ref)

        dtr = dtr_ref[...]
        dtb = dtb_ref[...]
        dt, arow, dt_x, acs_b, e, tril, triu = _ssd_common(dtr, dtb, alog_ref[...], acsx_s, acst_s)
        dsk_x = _dot3(jnp.broadcast_to(dsk_ref[...], (CHUNK, LANES)), e)
        lane = lax.broadcasted_iota(jnp.int32, (CHUNK, LANES), 1)
        rowi = lax.broadcasted_iota(jnp.int32, (CHUNK, LANES), 0)
        causal = (lax.broadcasted_iota(jnp.int32, (CHUNK, CHUNK), 0)
                  >= lax.broadcasted_iota(jnp.int32, (CHUNK, CHUNK), 1))
        acs_rows = jnp.zeros((CHUNK, LANES), F32)
        acs_cols = jnp.zeros((LANES, CHUNK), F32)
        d_b = jnp.zeros((CHUNK, LANES), F32)
        d_c = jnp.zeros((CHUNK, LANES), F32)
        for j in range(SSD_PAIRS):
            g = j // 4
            sl = slice(j * LANES, (j + 1) * LANES)
            if j % 4 == 0:
                bg = jnp.where(lane // HEAD == g, b_ref[...], 0.0)
                cg = jnp.where(lane // HEAD == g, c_ref[...], 0.0)
                cb = _dot_nt(cg, bg)
                dcb = jnp.zeros((CHUNK, CHUNK), F32)
            x = x_ref[:, sl]
            d = dt_x[:, sl]
            a = acsx_s[:, sl]
            at = acsx_s[CHUNK - 1:CHUNK, sl]
            xdt = x * d
            hin = hin_ref[0, j]
            dhout = dh_s[j]
            dyp = dy_ref[:, sl]
            ea, eat, ed = jnp.exp(a), jnp.exp(at), jnp.exp(at - a)
            da_l = dyp * ea * _dot_nn(cg, hin)
            dm = dyp * ea
            d_c = d_c + _dot_nt(dm, hin)
            dh_s[j] = _dot_tn(cg, dm) + eat * dhout
            dat = jnp.sum(dhout * hin * eat, axis=0, keepdims=True)
            d_b = d_b + _dot_nt(xdt * ed, dhout)
            dw = _dot_nn(bg, dhout)
            dxdt = dw * ed
            ded = dw * xdt * ed
            dat = dat + jnp.sum(ded, axis=0, keepdims=True)
            da_l = da_l - ded
            for hh in (0, 1):
                h = 2 * j + hh
                dec = jnp.exp(jnp.minimum(acs_b[:, h * LANES:(h + 1) * LANES] - acst_s[pl.ds(h, 1), :], 0.0))
                gm = jnp.where(causal, cb * dec, 0.0)
                dyh = jnp.where(lane // HEAD == hh, dyp, 0.0)
                dg = _dot_nt(dyh, xdt)
                dxdt = dxdt + _dot_tn(gm, dyh)
                dcb = dcb + jnp.where(causal, dg * dec, 0.0)
                th = dg * gm
                acs_rows = acs_rows + jnp.where(lane == h, jnp.sum(th, axis=1, keepdims=True), 0.0)
                acs_cols = acs_cols + jnp.where(rowi == h, jnp.sum(th, axis=0, keepdims=True), 0.0)
            if j % 4 == 3:
                d_c = d_c + _dot_nn(dcb, bg)
                d_b = d_b + _dot_tn(dcb, cg)
            dxc_ref[:, sl] = dyp * dsk_x[:, sl] + dxdt * d
            ddx_s[:, sl] = dxdt * x
            dax_s[:, sl] = da_l + jnp.where(rowi == CHUNK - 1, dat, 0.0)
            dskp = jnp.sum(dyp * x, axis=0, keepdims=True)
            ddsk_ref[...] += _dot2(jnp.broadcast_to(dskp, (8, LANES)), e[:, sl], _NT)
        dxc_ref[:, 8 * LANES:9 * LANES] = d_b
        dxc_ref[:, 9 * LANES:10 * LANES] = d_c
        dacs = acs_rows - acs_cols.T + _dot2(dax_s[...], e, _NT)
        ddt = _dot2(ddx_s[...], e, _NT)
        dda = _mdot3(triu, dacs)
        ddt = ddt + dda * arow
        dalog_ref[...] += jnp.sum(dda * dt, axis=0, keepdims=True) * arow
        ddtr = jnp.where(lane < SSD_HEADS, ddt * _sigmoid(dtr + dtb), 0.0)
        ddtr_ref[...] = ddtr
        ddtb_ref[...] += jnp.sum(ddtr, axis=0, keepdims=True)

    one = pl.BlockSpec((1, LANES), lambda i: (0, 0))
    rev = lambda c: (lambda i: (nc - 1 - i, c))
    return pl.pallas_call(
        body, name=name, grid=(nc,),
        in_specs=[pl.BlockSpec((CHUNK, 8 * LANES), rev(0)), pl.BlockSpec((CHUNK, LANES), rev(8)),
                  pl.BlockSpec((CHUNK, LANES), rev(9)), pl.BlockSpec((CHUNK, LANES), rev(C_DT // LANES)),
                  one, one, one,
                  pl.BlockSpec((1, SSD_PAIRS, LANES, LANES), lambda i: (nc - 1 - i, 0, 0, 0)),
                  pl.BlockSpec((CHUNK, 8 * LANES), rev(0))],
        out_specs=[pl.BlockSpec((CHUNK, XBC_COLS), rev(0)), pl.BlockSpec((CHUNK, LANES), rev(0)), one, one,
                   pl.BlockSpec((8, LANES), lambda i: (0, 0))],
        out_shape=[jax.ShapeDtypeStruct((t, XBC_COLS), F32), jax.ShapeDtypeStruct((t, LANES), F32)]
        + [jax.ShapeDtypeStruct((1, LANES), F32)] * 2 + [jax.ShapeDtypeStruct((8, LANES), F32)],
        scratch_shapes=[pltpu.VMEM((CHUNK, 8 * LANES), F32), pltpu.VMEM((LANES, CHUNK), F32),
                        pltpu.VMEM((SSD_PAIRS, LANES, LANES), F32),
                        pltpu.VMEM((CHUNK, 8 * LANES), F32), pltpu.VMEM((CHUNK, 8 * LANES), F32)],
        compiler_params=_params(("arbitrary",)),
    )(xc, xc, xc, proj, dtb, alog, dsk, hin_all, dy)


RW_C = 64


def _p3(a, b, dn):
    ah, al = _split2(a)
    bh, bl = _split2(b)
    d = lambda x, y: lax.dot_general(x, y, dn, preferred_element_type=F32)
    return d(ah, bh) + d(ah, bl) + d(al, bh)


_BNN = (((2,), (1,)), ((0,), (0,)))
_BNT = (((2,), (2,)), ((0,), (0,)))
_BTN = (((1,), (1,)), ((0,), (0,)))


@jax.custom_vjp
def _pnn(a, b):
    return _p3(a, b, _BNN)


@jax.custom_vjp
def _pnt(a, b):
    return _p3(a, b, _BNT)


@jax.custom_vjp
def _ptn(a, b):
    return _p3(a, b, _BTN)


_pnn.defvjp(lambda a, b: (_p3(a, b, _BNN), (a, b)), lambda res, g: (_p3(g, res[1], _BNT), _p3(res[0], g, _BTN)))
_pnt.defvjp(lambda a, b: (_p3(a, b, _BNT), (a, b)), lambda res, g: (_p3(g, res[1], _BNN), _p3(g, res[0], _BTN)))
_ptn.defvjp(lambda a, b: (_p3(a, b, _BTN), (a, b)), lambda res, g: (_p3(res[1], g, _BNT), _p3(res[0], g, _BNN)))


def _tri2(tril, x, dn):
    hi, lo = _split2(x)
    m = tril.astype(BF16)
    return (lax.dot_general(m, hi, dn, preferred_element_type=F32) + lax.dot_general(m, lo, dn, preferred_element_type=F32))


@jax.custom_vjp
def _cumsum_rows(tril, x):
    return _tri2(tril, x, _BNN)


_cumsum_rows.defvjp(lambda tril, x: (_tri2(tril, x, _BNN), tril),
                    lambda tril, g: (jnp.zeros_like(tril), _tri2(tril, g, _BTN)))


def _rw_chunk_consts():
    c2 = 2 * RW_C
    row = lax.broadcasted_iota(jnp.int32, (c2, c2), 0)
    col = lax.broadcasted_iota(jnp.int32, (c2, c2), 1)
    same = (row // RW_C) == (col // RW_C)
    strict = (same & (row > col)).astype(F32)
    incl = (same & (row >= col)).astype(F32)
    eye = (row == col).astype(F32)
    tr = lax.broadcasted_iota(jnp.int32, (RW_C, RW_C), 0)
    tc = lax.broadcasted_iota(jnp.int32, (RW_C, RW_C), 1)
    tril = (tr >= tc).astype(F32)
    lane = lax.broadcasted_iota(jnp.int32, (1, LANES), 1)
    hm = [(lane // HEAD == h).astype(F32) for h in (0, 1)]
    return strict, incl, eye, tril, hm


def _rw_chunk(r, lw, k, v, n, b, s2, consts):
    strict, incl, eye, tril, hm = consts
    two = lambda x: jnp.concatenate([x * hm[0], x * hm[1]], axis=1)
    cum = _cumsum_rows(jnp.broadcast_to(tril, (4, RW_C, RW_C)), lw)
    grow, shrink = jnp.exp(-cum), jnp.exp(cum)
    n2, r2 = two(n * jnp.exp(cum - lw)), two(r * shrink)
    b2, k2, v2 = two(b * grow), two(k * grow), two(v)
    p = _pnt(n2, b2) * strict
    x2 = _pnt(n2, s2) + _pnn(_pnt(n2, k2) * strict, v2)
    t_inv, a = eye + p, p
    for _ in range(RW_C.bit_length() - 2):
        a = _pnn(a, a)
        t_inv = t_inv + _pnn(t_inv, a)
    u2 = _pnn(t_inv, x2)
    y2 = _pnt(r2, s2) + _pnn(_pnt(r2, b2) * incl, u2) + _pnn(_pnt(r2, k2) * incl, v2)
    s2_new = (s2 + _ptn(u2, b2) + _ptn(v2, k2)) * jnp.exp(jnp.sum(lw, axis=1, keepdims=True))
    return jnp.sum(y2.reshape(4, 2, RW_C, LANES), axis=1), s2_new


def _pairs(ref):
    return jnp.stack([ref[:, p * LANES:(p + 1) * LANES] for p in range(4)])


def _rw_chunk_fwd(mixed, lw, k, n, b, *, name, side=None):
    t = lw.shape[0]
    nc = t // RW_C

    def body(r_ref, v_ref, lw_ref, k_ref, n_ref, b_ref, y_ref, sin_ref, s_s):
        @pl.when(pl.program_id(0) == 0)
        def _():
            s_s[...] = jnp.zeros_like(s_s)

        s2 = s_s[...]
        sin_ref[0] = s2
        y, s2 = _rw_chunk(*[_pairs(x) for x in (r_ref, lw_ref, k_ref, v_ref, n_ref, b_ref)], s2, _rw_chunk_consts())
        for p in range(4):
            y_ref[:, p * LANES:(p + 1) * LANES] = y[p]
        s_s[...] = s2

    blk = lambda c: pl.BlockSpec((RW_C, 4 * LANES), functools.partial(lambda i, c: (i, c), c=c))
    return _call_with_side(
        body, side, name=name, grid=(nc,), semantics=("arbitrary",),
        in_specs=[blk(0), blk(2), blk(0), blk(0), blk(0), blk(0)],
        out_specs=[blk(0), pl.BlockSpec((1, 4, LANES, LANES), lambda i: (i, 0, 0, 0))],
        out_shape=[jax.ShapeDtypeStruct((t, 4 * LANES), F32), jax.ShapeDtypeStruct((nc, 4, LANES, LANES), F32)],
        scratch_shapes=[pltpu.VMEM((4, LANES, LANES), F32)],
        operands=(mixed, mixed, lw, k, n, b))


def _call_with_side(body, side, *, name, grid, semantics, in_specs, out_specs, out_shape, scratch_shapes, operands):
    if side is None:
        return pl.pallas_call(body, name=name, grid=grid, in_specs=in_specs, out_specs=out_specs, out_shape=out_shape,
                              scratch_shapes=scratch_shapes, compiler_params=_params(semantics))(*operands)
    srcs, per_dest = side
    ns, ni, no, nscr = len(srcs), len(in_specs), len(out_specs), len(scratch_shapes)

    def full_body(*refs):
        ins, side_in = refs[:ni], refs[ni:ni + ns]
        outs, side_out = refs[ni + ns:ni + ns + no], refs[ni + ns + no:ni + 2 * ns + no]
        scratch, sems = refs[ni + 2 * ns + no:ni + 2 * ns + no + nscr], refs[ni + 2 * ns + no + nscr:]

        ids = [pl.program_id(a) for a in range(len(grid))]
        first = functools.reduce(jnp.logical_and, [i == 0 for i in ids])
        last = functools.reduce(jnp.logical_and, [i == n - 1 for i, n in zip(ids, grid)])

        @pl.when(first)
        def _():
            _exchange(side_in, side_out, sems, per_dest, start=True, wait=False)

        body(*ins, *outs, *scratch)

        @pl.when(last)
        def _():
            _exchange(side_in, side_out, sems, per_dest, start=False, wait=True)

    res = pl.pallas_call(
        full_body, name=name, grid=grid, in_specs=list(in_specs) + [_ANY] * ns,
        out_specs=list(out_specs) + [_ANY] * ns, out_shape=list(out_shape) + _exchange_out_shapes(srcs),
        scratch_shapes=list(scratch_shapes) + _exchange_sems(ns), compiler_params=_params(("arbitrary",) * len(grid)),
    )(*operands, *srcs)
    return list(res[:no]) + [list(res[no:])]


def _rw_chunk_bwd(mixed, lw, k, n, b, s_in, dy, dr0, dk0, dv0, *, name, side=None):
    t = lw.shape[0]
    nc = t // RW_C

    def body(r_ref, v_ref, lw_ref, k_ref, n_ref, b_ref, sin_ref, dy_ref, dr0_ref, dk0_ref, dv0_ref,
             dr_ref, dlw_ref, dk_ref, dv_ref, dn_ref, db_ref, ds_s):
        @pl.when(pl.program_id(0) == 0)
        def _():
            ds_s[...] = jnp.zeros_like(ds_s)

        consts = _rw_chunk_consts()
        args = [_pairs(x) for x in (r_ref, lw_ref, k_ref, v_ref, n_ref, b_ref)] + [sin_ref[0]]
        _, vjp = jax.vjp(lambda *a: _rw_chunk(*a, consts), *args)
        dr, dlw, dk, dv, dn, db, ds = vjp((_pairs(dy_ref), ds_s[...]))
        for p in range(4):
            sl = slice(p * LANES, (p + 1) * LANES)
            dr_ref[:, sl] = dr[p] + dr0_ref[:, sl]
            dlw_ref[:, sl] = dlw[p]
            dk_ref[:, sl] = dk[p] + dk0_ref[:, sl]
            dv_ref[:, sl] = dv[p] + dv0_ref[:, sl]
            dn_ref[:, sl] = dn[p]
            db_ref[:, sl] = db[p]
        ds_s[...] = ds

    blk = lambda c: pl.BlockSpec((RW_C, 4 * LANES), functools.partial(lambda i, c: (nc - 1 - i, c), c=c))
    return _call_with_side(
        body, side, name=name, grid=(nc,), semantics=("arbitrary",),
        in_specs=[blk(0), blk(2), blk(0), blk(0), blk(0), blk(0),
                  pl.BlockSpec((1, 4, LANES, LANES), lambda i: (nc - 1 - i, 0, 0, 0)), blk(0), blk(0), blk(0), blk(0)],
        out_specs=[blk(0)] * 6,
        out_shape=[jax.ShapeDtypeStruct((t, 4 * LANES), F32)] * 6,
        scratch_shapes=[pltpu.VMEM((4, LANES, LANES), F32)],
        operands=(mixed, mixed, lw, k, n, b, s_in, dy, dr0, dk0, dv0))


def _f_rms_res(x, g):
    return _f_rms(x, g)[0], x


def _final(x, g, target, *, bt, name):
    t, d = x.shape

    def body(x_ref, g_ref, t_ref, dx_ref, loss_ref, dg_ref):
        tgt = t_ref[...]

        def f(xv, gv):
            err = _f_rms(xv, gv)[0] - tgt
            return 0.5 * jnp.mean(err * err, axis=-1, keepdims=True)

        row_loss, vjp = jax.vjp(f, x_ref[...], g_ref[...])
        dx, dg = vjp(jnp.ones_like(row_loss))
        dx_ref[...] = dx

        @pl.when(pl.program_id(0) == 0)
        def _():
            loss_ref[...] = jnp.zeros_like(loss_ref)
            dg_ref[...] = jnp.zeros_like(dg_ref)

        loss_ref[...] += jnp.broadcast_to(jnp.sum(row_loss, axis=0, keepdims=True), (1, LANES))
        dg_ref[...] += dg

    blk = pl.BlockSpec((bt, d), lambda i: (i, 0))
    return pl.pallas_call(
        body, name=name, grid=(t // bt,),
        in_specs=[blk, pl.BlockSpec((1, d), lambda i: (0, 0)), blk],
        out_specs=[blk, pl.BlockSpec((1, LANES), lambda i: (0, 0)), pl.BlockSpec((1, d), lambda i: (0, 0))],
        out_shape=[jax.ShapeDtypeStruct((t, d), F32), jax.ShapeDtypeStruct((1, LANES), F32),
                   jax.ShapeDtypeStruct((1, d), F32)],
        compiler_params=_params(("arbitrary",)),
    )(x, g, target)


ADAMW_BLOCK_BYTES = 1 << 20


def _adamw(w, g, m, v, *, name, block=None):
    shape = w.shape
    if block is not None:
        return _adamw_blocks(w, g, m, v, block, name)
    c = shape[-1]
    shape3 = (1,) * (3 - len(shape)) + shape if len(shape) <= 3 else (-1,) + shape[-2:]
    args = [a.reshape(shape3) for a in (w, g, m, v)]
    lead, r, _ = args[0].shape
    br = r
    if r * c * 4 > ADAMW_BLOCK_BYTES:
        cands = [b for b in range(8, r, 8) if r % b == 0 and b * c * 4 <= ADAMW_BLOCK_BYTES]
        br = max(cands) if cands else r
    outs = _adamw_blocks(*args, (1, br, c), name)
    return tuple(o.reshape(shape) for o in outs)


def _adamw_blocks(w, g, m, v, block, name):
    shape = w.shape
    assert all(s % b == 0 for s, b in zip(shape, block))

    def body(w_ref, g_ref, m_ref, v_ref, d_ref, nm_ref, nv_ref):
        gv = g_ref[...]
        m_new = ADAM_B1 * m_ref[...] + (1.0 - ADAM_B1) * gv
        v_new = ADAM_B2 * v_ref[...] + (1.0 - ADAM_B2) * (gv * gv)
        m_hat = m_new / (1.0 - ADAM_B1 ** ADAM_STEP)
        v_hat = v_new / (1.0 - ADAM_B2 ** ADAM_STEP)
        d_ref[...] = -ADAM_LR * (m_hat / (jnp.sqrt(v_hat) + ADAM_EPS) + ADAM_WD * w_ref[...])
        nm_ref[...] = m_new
        nv_ref[...] = v_new

    blk = pl.BlockSpec(tuple(block), lambda *ids: ids)
    return pl.pallas_call(
        body, name=name, grid=tuple(s // b for s, b in zip(shape, block)), in_specs=[blk] * 4, out_specs=[blk] * 3,
        out_shape=[jax.ShapeDtypeStruct(shape, F32)] * 3,
        compiler_params=_params(("parallel",) * len(shape)),
    )(w, g, m, v)


BT = 256
BC = 128


def _layer_rows(x, proj, s):
    s = {k: s.get(k) for k in ("y_sb_raw", "y_ssd_raw", "mixed", "ys", "k2", "p_sb", "p_ssd", "p_rw")}
    return dict(
        rms=[(x, D_MODEL, 0)],
        sb_gate=[(s["y_sb_raw"], 512, 0), (proj, 512, 3)],
        ssd_norm=[(s["y_ssd_raw"], 1024, 0), (proj, 1024, C_Z // 1024)],
        rw_pre=[(s["mixed"], 512, 1), (s["mixed"], LANES, 16)],
        rw_post=[(s["ys"], 512, 0), (s["mixed"], 512, 0), (s["k2"], 512, 0), (s["mixed"], 512, 2), (s["mixed"], 512, 3)],
        merge=[(s["p_sb"], 1024, 0), (s["p_ssd"], 1024, 0), (s["p_rw"], 1024, 0),
               (proj, 1024, 3), (proj, 1024, 4), (proj, 1024, 5)],
    )


def _layer_fwd(x, p, nm, side=None):
    s = {}
    (s["h"],) = _rowwise(_f_rms, [(x, D_MODEL, 0)], [p["norm_g"]], [D_MODEL], bt=BT, name=nm + "rms")
    proj = s["proj"] = _mm(s["h"], p["w_in"], name=nm + "proj")
    s["y_sb_raw"], s["lt"] = _sb2_fwd(proj, name=nm + "sb")
    s["xc"] = _colwise(_f_conv, proj, C_XBC, XBC_COLS, p["conv"], bc=BC, name=nm + "conv")
    s["y_ssd_raw"], s["hin"] = _ssd_fwd(s["xc"], proj, p["dt_bias"], p["a_log"], p["d_skip"], name=nm + "ssd")
    s["mixed"] = _colwise(_f_rw_mix, proj, C_RW, RW_COLS, [p["rw_mu"]], bc=BC, name=nm + "mix")
    s["w"], s["k2"], s["n"], s["b"] = _rowwise(_f_rw_pre, [(s["mixed"], 512, 1), (s["mixed"], LANES, 16)], p["rw_pre"],
                                               [512] * 4, bt=BT, name=nm + "rwpre")
    s["ys"], s["st"], *exchanged = _rw_chunk_fwd(s["mixed"], s["w"], s["k2"], s["n"], s["b"], name=nm + "scan", side=side)
    rows = _layer_rows(x, proj, s)
    (s["y_sb"],) = _rowwise(_f_sb_gate, rows["sb_gate"], [], [512], bt=BT, name=nm + "sbgate")
    (s["y_ssd"],) = _rowwise(_f_ssd_norm, rows["ssd_norm"], [p["ssd_norm_g"]], [1024], bt=BT, name=nm + "ssdnorm")
    (s["y_rw"],) = _rowwise(_f_rw_post, rows["rw_post"], p["rw_post"], [512], bt=BT, name=nm + "rwpost")
    s["p_sb"] = _mm(s["y_sb"], p["w_out_sb"], name=nm + "osb")
    s["p_ssd"] = _mm(s["y_ssd"], p["w_out_ssd"], name=nm + "ossd")
    s["p_rw"] = _mm(s["y_rw"], p["w_out_rw"], name=nm + "orw")
    (s["merged"],) = _rowwise(_f_merge, _layer_rows(x, proj, s)["merge"], [], [1024], bt=BT, name=nm + "merge")
    return _mm(s["merged"], p["w_o"], add=x, name=nm + "wo"), s, (exchanged[0] if exchanged else None)


def _layer_bwd(x, dx_out, p, s, nm, side=None, side_late=None):
    g = {}
    proj = s["proj"]
    rows = _layer_rows(x, proj, s)
    g["w_o"] = _mm(s["merged"], dx_out, ta=True, name=nm + "g_wo")
    d_merged = _mm(dx_out, p["w_o"], tb=True, name=nm + "d_merged")
    dp_sb, dp_ssd, dp_rw, d_gates = _rowwise_bwd(_f_merge, rows["merge"], [], [(d_merged, 1024, 0)], bt=BT,
                                                 name=nm + "merge_b", groups=[[0], [1], [2], [3, 4, 5]])
    g["w_out_sb"] = _mm(s["y_sb"], dp_sb, ta=True, name=nm + "g_osb")
    g["w_out_ssd"] = _mm(s["y_ssd"], dp_ssd, ta=True, name=nm + "g_ossd")
    g["w_out_rw"] = _mm(s["y_rw"], dp_rw, ta=True, name=nm + "g_orw")
    dy_sb = _mm(dp_sb, p["w_out_sb"], tb=True, name=nm + "d_ysb")
    dy_ssd = _mm(dp_ssd, p["w_out_ssd"], tb=True, name=nm + "d_yssd")
    dy_rw = _mm(dp_rw, p["w_out_rw"], tb=True, name=nm + "d_yrw")
    dy_sb_raw, d_sbgate = _rowwise_bwd(_f_sb_gate, rows["sb_gate"], [], [(dy_sb, 512, 0)], bt=BT, name=nm + "sbgate_b")
    dq, dk, dv = _sb2_bwd(proj, dy_sb_raw, s["lt"], name=nm + "sb_b")
    dy_ssd_raw, dz, g["ssd_norm_g"] = _rowwise_bwd(_f_ssd_norm, rows["ssd_norm"], [p["ssd_norm_g"]],
                                                   [(dy_ssd, 1024, 0)], bt=BT, name=nm + "ssdnorm_b")
    dxc, ddtr, g["dt_bias"], g["a_log"], g["d_skip"] = _ssd_bwd(
        s["xc"], proj, p["dt_bias"], p["a_log"], p["d_skip"], s["hin"], dy_ssd_raw, name=nm + "ssd_b")
    conv_out = _colwise_bwd(_f_conv, proj, C_XBC, XBC_COLS, p["conv"], dxc, bc=BC, name=nm + "conv_b")
    dxbc, g["conv"] = conv_out[0], conv_out[1:]
    dys, dr0, dk0, dv0, d_rwgate, g["rw_ln_g"], g["rw_ln_b"], g["rw_r_k"] = _rowwise_bwd(
        _f_rw_post, rows["rw_post"], p["rw_post"], [(dy_rw, 512, 0)], bt=BT, name=nm + "rwpost_b")
    dr, dw, dk2, dvv, dn, db, *exchanged = _rw_chunk_bwd(s["mixed"], s["w"], s["k2"], s["n"], s["b"], s["st"], dys,
                                                         dr0, dk0, dv0, name=nm + "scan_b",
                                                         side=side(g) if side else None)
    pre_out = _rowwise_bwd(_f_rw_pre, rows["rw_pre"], p["rw_pre"],
                           [(dw, 512, 0), (dk2, 512, 0), (dn, 512, 0), (db, 512, 0)], bt=BT, name=nm + "rwpre_b")
    dkm, dlo, g["rw_pre"] = pre_out[0], pre_out[1], pre_out[2:]
    d_mixed = jnp.concatenate([dr, dkm, dvv, d_rwgate, dlo], axis=1)
    d_slab, g["rw_mu"] = _colwise_bwd(_f_rw_mix, proj, C_RW, RW_COLS, [p["rw_mu"]], d_mixed, bc=BC, name=nm + "mix_b")
    d_proj = jnp.concatenate([dq, dk, dv, d_sbgate, dz, d_gates, d_slab, ddtr, dxbc], axis=1)
    g["w_in"] = _mm(s["h"], d_proj, ta=True, name=nm + "g_win")
    dh = _mm(d_proj, p["w_in"], tb=True, tn=1024, tk=512, name=nm + "d_h", side=side_late(g) if side_late else None)
    dh, late = dh if side_late else (dh, None)
    dx, g["norm_g"] = _rowwise_bwd(_f_rms_res, rows["rms"], [p["norm_g"]], [(dh, D_MODEL, 0), (dx_out, D_MODEL, 0)],
                                   bt=BT, name=nm + "rms_b")
    return dx, g, (exchanged[0] if exchanged else None), late


MESH = pl.DeviceIdType.MESH
N_DEV = 8
_ANY = pl.BlockSpec(memory_space=pl.ANY)


def _here():
    x, y, c = lax.axis_index("x"), lax.axis_index("y"), lax.axis_index("c")
    return x, y, c, [(1 - x, y), (x, 1 - y), (1 - x, 1 - y)]


def _chip_exchange(srcs, *, per_dest, name):
    n = len(srcs)

    def body(*refs):
        _exchange(refs[:n], refs[n:2 * n], refs[2 * n:], per_dest, start=True, wait=True)

    return pl.pallas_call(
        body, name=name, in_specs=[_ANY] * n, out_specs=[_ANY] * n,
        out_shape=_exchange_out_shapes(srcs), scratch_shapes=_exchange_sems(n),
    )(*srcs)


def _exchange_out_shapes(srcs):
    return [jax.ShapeDtypeStruct((4,) + s.shape[1:], s.dtype) for s in srcs]


def _exchange_sems(n):
    return [pltpu.SemaphoreType.DMA((3 * n,)), pltpu.SemaphoreType.DMA((3 * n,)), pltpu.SemaphoreType.DMA((n,))]


def _exchange(src_refs, out_refs, sems, per_dest, *, start, wait):
    send_sems, recv_sems, local_sems = sems
    x, y, c, chips = _here()
    me = 2 * x + y
    owns, sends, recvs = [], [], []
    for a, (src_ref, out_ref) in enumerate(zip(src_refs, out_refs)):
        pick = (lambda q, s=src_ref: s.at[q]) if per_dest else (lambda q, s=src_ref: s.at[c])
        owns.append(pltpu.make_async_copy(pick(me), out_ref.at[me], local_sems.at[a]))
        for j, (px, py) in enumerate(chips):
            sends.append(pltpu.make_async_remote_copy(
                pick(2 * px + py), out_ref.at[me], send_sems.at[3 * a + j], recv_sems.at[3 * a + j],
                device_id=(px, py, c), device_id_type=MESH))
            recvs.append(pltpu.make_async_remote_copy(
                src_ref.at[0], out_ref.at[2 * px + py], send_sems.at[3 * a + j], recv_sems.at[3 * a + j],
                device_id=(px, py, c), device_id_type=MESH))
    if start:
        for cp in owns + sends:
            cp.start()
    if wait:
        for cp in recvs:
            cp.wait_recv()
        for cp in sends:
            cp.wait_send()
        for cp in owns:
            cp.wait()


def _sibling_swap(srcs, *, other_slot, name):
    n = len(srcs)

    def body(*refs):
        src_refs, out_refs, send_sems, recv_sems = refs[:n], refs[n:2 * n], refs[2 * n], refs[2 * n + 1]
        x, y, c, _ = _here()
        copies = [pltpu.make_async_remote_copy(s.at[1 - c] if other_slot else s, o, send_sems.at[a], recv_sems.at[a],
                                               device_id=(x, y, 1 - c), device_id_type=MESH)
                  for a, (s, o) in enumerate(zip(src_refs, out_refs))]
        for cp in copies:
            cp.start()
        for cp in copies:
            cp.wait()

    return pl.pallas_call(
        body, name=name, in_specs=[_ANY] * n, out_specs=[_ANY] * n,
        out_shape=[jax.ShapeDtypeStruct(s.shape[1:] if other_slot else s.shape, s.dtype) for s in srcs],
        scratch_shapes=[pltpu.SemaphoreType.DMA((n,)), pltpu.SemaphoreType.DMA((n,))],
    )(*srcs)


def _allgather_small(v, *, reduce, name):
    r = v.shape[0]

    def body(v_ref, out_ref, *rest):
        send_sems, recv_sems, local_sem = rest[-3:]
        x, y, c, chips = _here()
        me, sibling = (x, y, c), (x, y, 1 - c)

        def slot(px, py, pc):
            return out_ref.at[4 * px + 2 * py + pc]

        def copy(k, block, to, src=None):
            return pltpu.make_async_remote_copy(
                src_ref=slot(*block) if src is None else src, dst_ref=slot(*block),
                send_sem=send_sems.at[k], recv_sem=recv_sems.at[k], device_id=to, device_id_type=MESH)

        mine = pltpu.make_async_copy(v_ref, slot(*me), local_sem)
        mine.start()
        first = [copy(0, me, sibling, src=v_ref)]
        first += [copy(1 + j, me, (*chip, c), src=v_ref) for j, chip in enumerate(chips)]
        for cp in first:
            cp.start()
        passed = [copy(4 + j, (*chip, c), sibling) for j, chip in enumerate(chips)]
        for j, chip in enumerate(chips):
            copy(1 + j, (*chip, c), me).wait_recv()
            passed[j].start()
        copy(0, sibling, me).wait_recv()
        for j, chip in enumerate(chips):
            copy(4 + j, (*chip, 1 - c), me).wait_recv()
        for cp in first + passed:
            cp.wait_send()
        mine.wait()
        if reduce:
            total = out_ref[0]
            for d in range(1, N_DEV):
                total = total + out_ref[d]
            rest[0][...] = total

    vm = pl.BlockSpec(memory_space=pltpu.VMEM)
    out_shape = [jax.ShapeDtypeStruct((N_DEV, r, LANES), F32)] + ([jax.ShapeDtypeStruct((r, LANES), F32)] if reduce else [])
    return pl.pallas_call(
        body, name=name, in_specs=[vm], out_specs=[vm] * len(out_shape), out_shape=out_shape,
        scratch_shapes=[pltpu.SemaphoreType.DMA((7,)), pltpu.SemaphoreType.DMA((7,)), pltpu.SemaphoreType.DMA],
        compiler_params=pltpu.CompilerParams(vmem_limit_bytes=VMEM_LIMIT),
    )(v)


REDUCE_BLOCK_BYTES = 2 << 20


def _reduce_rows(r, c):
    cands = [b for b in range(16, r + 1, 16) if r % b == 0 and b * c * 4 <= REDUCE_BLOCK_BYTES]
    return max(cands)


def _add_halves(mine2, other, c_idx, *, name):
    _, nq, r, c = mine2.shape
    br = _reduce_rows(r, c)

    def body(c_ref, a_ref, b_ref, o_ref):
        o_ref[...] = (a_ref[0] + b_ref[...]).astype(o_ref.dtype)

    blk = pl.BlockSpec((1, br, c), lambda q, i, c_ref: (q, i, 0))
    return pl.pallas_call(
        body, name=name,
        grid_spec=pltpu.PrefetchScalarGridSpec(
            num_scalar_prefetch=1, grid=(nq, r // br),
            in_specs=[pl.BlockSpec((1, 1, br, c), lambda q, i, c_ref: (c_ref[0], q, i, 0)), blk],
            out_specs=blk),
        out_shape=jax.ShapeDtypeStruct((nq, r, c), BF16),
        compiler_params=_params(("parallel", "parallel")),
    )(c_idx, mine2, other)


def _sum_chips(parts, *, name):
    _, r, c = parts.shape
    br = _reduce_rows(r, c)

    def body(p_ref, o_ref):
        total = p_ref[0].astype(F32)
        for q in range(1, 4):
            total = total + p_ref[q].astype(F32)
        o_ref[...] = total

    return pl.pallas_call(
        body, name=name, grid=(r // br,),
        in_specs=[pl.BlockSpec((4, br, c), lambda i: (0, i, 0))],
        out_specs=pl.BlockSpec((br, c), lambda i: (i, 0)),
        out_shape=jax.ShapeDtypeStruct((r, c), F32),
        compiler_params=_params(("parallel",)),
    )(parts)


BIG = ("w_in", "w_out_sb", "w_out_ssd", "w_out_rw", "w_o")
BIG_AXIS = {"w_in": 2, "w_out_sb": 2, "w_out_ssd": 1, "w_out_rw": 2, "w_o": 1}
SMALL_SHARDED = {"conv_w": 320, "rw_w_up": 128, "rw_a_up": 128}
SMALL = ("norm_g", "conv_w", "conv_b", "dt_bias", "a_log", "d_skip", "ssd_norm_g", "rw_mu", "rw_w0", "rw_w_up",
         "rw_a0", "rw_a_up", "rw_k_k", "rw_k_a", "rw_r_k", "rw_ln_g", "rw_ln_b", "final_g")


def _rows_of(a):
    flat = a.reshape(-1)
    pad = (-flat.shape[0]) % LANES
    return jnp.pad(flat, (0, pad)).reshape(-1, LANES)


def _pack_rows(arrays, multiple=8):
    rows = jnp.concatenate([_rows_of(a) for a in arrays], axis=0)
    pad = (-rows.shape[0]) % multiple
    return jnp.pad(rows, ((0, pad), (0, 0)))


def _unpack_rows(rows, shapes):
    out, off = [], 0
    for shp in shapes:
        n = 1
        for d in shp:
            n *= d
        nr = -(-n // LANES)
        out.append(rows[off:off + nr].reshape(-1)[:n].reshape(shp))
        off += nr
    return out


COL_MAP = ((0, 3072, 0), (3072, 4352, C_XBC), (4352, 4368, C_DT), (4368, 6544, C_RW), (6544, 9616, C_GATES))
SHARD_COLS = N_IN // 4


def _w_in_from_shards(shards):
    pieces = []
    for a, b, dst in sorted(COL_MAP, key=lambda m: m[2]):
        if pieces and dst > pieces[-1][0]:
            pieces.append((dst, jnp.zeros((shards[0].shape[0], dst - pieces[-1][0]), shards[0].dtype)))
        for q in range(4):
            lo, hi = max(a, q * SHARD_COLS), min(b, (q + 1) * SHARD_COLS)
            if lo < hi:
                pieces.append((dst + hi - a, shards[q][:, lo - q * SHARD_COLS:hi - q * SHARD_COLS]))
    return jnp.concatenate([p for _, p in pieces], axis=1)


def _w_in_shard(g, q):
    pieces = []
    for a, b, dst in COL_MAP:
        lo, hi = max(a, q * SHARD_COLS), min(b, (q + 1) * SHARD_COLS)
        if lo < hi:
            pieces.append(g[:, dst + lo - a:dst + hi - a])
    return jnp.concatenate(pieces, axis=1)


def _row_halves(a):
    return a.reshape(2, a.shape[0] // 2, a.shape[1])


def _join_halves(core, mine, theirs):
    return jnp.where(core == 0, jnp.concatenate([mine, theirs], axis=-2), jnp.concatenate([theirs, mine], axis=-2))


def kernel(x, norm_g, w_in, conv_w, conv_b, dt_bias, a_log, d_skip, ssd_norm_g, rw_mu, rw_w0, rw_w_up, rw_a0, rw_a_up, rw_k_k, rw_k_a, rw_r_k, rw_ln_g, rw_ln_b, w_out_sb, w_out_ssd, w_out_rw, w_o, final_g, loss_target, m_norm_g, m_w_in, m_conv_w, m_conv_b, m_dt_bias, m_a_log, m_d_skip, m_ssd_norm_g, m_rw_mu, m_rw_w0, m_rw_w_up, m_rw_a0, m_rw_a_up, m_rw_k_k, m_rw_k_a, m_rw_r_k, m_rw_ln_g, m_rw_ln_b, m_w_out_sb, m_w_out_ssd, m_w_out_rw, m_w_o, m_final_g, v_norm_g, v_w_in, v_conv_w, v_conv_b, v_dt_bias, v_a_log, v_d_skip, v_ssd_norm_g, v_rw_mu, v_rw_w0, v_rw_w_up, v_rw_a0, v_rw_a_up, v_rw_k_k, v_rw_k_a, v_rw_r_k, v_rw_ln_g, v_rw_ln_b, v_w_out_sb, v_w_out_ssd, v_w_out_rw, v_w_o, v_final_g):
    names = ("norm_g", "w_in", "conv_w", "conv_b", "dt_bias", "a_log", "d_skip", "ssd_norm_g", "rw_mu", "rw_w0",
             "rw_w_up", "rw_a0", "rw_a_up", "rw_k_k", "rw_k_a", "rw_r_k", "rw_ln_g", "rw_ln_b", "w_out_sb",
             "w_out_ssd", "w_out_rw", "w_o", "final_g")
    w_loc = dict(zip(names, (norm_g, w_in, conv_w, conv_b, dt_bias, a_log, d_skip, ssd_norm_g, rw_mu, rw_w0, rw_w_up,
                             rw_a0, rw_a_up, rw_k_k, rw_k_a, rw_r_k, rw_ln_g, rw_ln_b, w_out_sb, w_out_ssd, w_out_rw,
                             w_o, final_g)))
    m_loc = dict(zip(names, (m_norm_g, m_w_in, m_conv_w, m_conv_b, m_dt_bias, m_a_log, m_d_skip, m_ssd_norm_g,
                             m_rw_mu, m_rw_w0, m_rw_w_up, m_rw_a0, m_rw_a_up, m_rw_k_k, m_rw_k_a, m_rw_r_k,
                             m_rw_ln_g, m_rw_ln_b, m_w_out_sb, m_w_out_ssd, m_w_out_rw, m_w_o, m_final_g)))
    v_loc = dict(zip(names, (v_norm_g, v_w_in, v_conv_w, v_conv_b, v_dt_bias, v_a_log, v_d_skip, v_ssd_norm_g,
                             v_rw_mu, v_rw_w0, v_rw_w_up, v_rw_a0, v_rw_a_up, v_rw_k_k, v_rw_k_a, v_rw_r_k,
                             v_rw_ln_g, v_rw_ln_b, v_w_out_sb, v_w_out_ssd, v_w_out_rw, v_w_o, v_final_g)))
    chip = 2 * lax.axis_index("x") + lax.axis_index("y")
    core = lax.axis_index("c")

    def gather_srcs(i):
        return [_row_halves(w_loc[n][i].astype(BF16)) for n in BIG]

    def gathered(mine, nm):
        theirs = _sibling_swap(mine, other_slot=False, name=nm)
        out = {}
        for n, a, b in zip(BIG, mine, theirs):
            shards = _join_halves(core, a, b)
            out[n] = (_w_in_from_shards([shards[q] for q in range(4)]) if n == "w_in"
                      else jnp.concatenate([shards[q] for q in range(4)], axis=BIG_AXIS[n] - 1))
        return out

    full = {}
    sm_names = tuple(SMALL_SHARDED)
    sm_shapes = [w_loc[n].shape for n in sm_names]
    (got_sm,) = _allgather_small(_pack_rows([w_loc[n] for n in sm_names]), reduce=False, name="gather_small")
    per_chip = [_unpack_rows(got_sm[4 * (q // 2) + 2 * (q % 2)], sm_shapes) for q in range(4)]
    for i, n in enumerate(sm_names):
        full[n] = jnp.concatenate([per_chip[q][i] for q in range(4)], axis=-1)

    def pad16(a):
        return jnp.zeros((1, LANES), F32).at[0, :SSD_HEADS].set(a)

    def layer_params(i, big):
        row = lambda n: w_loc[n][i].reshape(1, -1)
        cw = full["conv_w"][i]
        return dict(
            norm_g=row("norm_g"), w_in=big["w_in"], conv=[cw[k][None] for k in range(4)] + [row("conv_b")],
            dt_bias=pad16(dt_bias[i]), a_log=pad16(a_log[i]), d_skip=pad16(d_skip[i]),
            ssd_norm_g=row("ssd_norm_g"), rw_mu=row("rw_mu"),
            rw_pre=[row("rw_w0"), jnp.zeros((LANES, 512), F32).at[:HEAD].set(full["rw_w_up"][i]), row("rw_a0"),
                    jnp.zeros((LANES, 512), F32).at[HEAD:].set(full["rw_a_up"][i]), row("rw_k_k"), row("rw_k_a")],
            rw_post=[row("rw_ln_g"), row("rw_ln_b"), row("rw_r_k")],
            w_out_sb=big["w_out_sb"], w_out_ssd=big["w_out_ssd"], w_out_rw=big["w_out_rw"], w_o=big["w_o"])

    c_idx = core.reshape(1).astype(jnp.int32)

    def reduce_prepare(items, nm):
        sends = []
        for g, n, _ in items:
            per_chip = ([_w_in_shard(g[n], q) for q in range(4)] if n == "w_in"
                        else jnp.split(g[n], 4, axis=BIG_AXIS[n] - 1))
            sends.append(jnp.stack([_row_halves(p) for p in per_chip], axis=1))
        others = _sibling_swap(sends, other_slot=True, name=nm + "sibling")
        return [_add_halves(s, o, c_idx, name=nm + "add_" + lab) for (_, _, lab), s, o in zip(items, sends, others)]

    def reduce_finish(exchanged, labels, nm):
        mine = [_sum_chips(p, name=nm + "sum_" + lab) for lab, p in zip(labels, exchanged)]
        theirs = _sibling_swap(mine, other_slot=False, name=nm + "join")
        return {lab: _join_halves(core, a, b) for lab, a, b in zip(labels, mine, theirs)}

    assert DEPTH == 2
    out_proj = BIG[1:]
    params, xs, saved, grads = [None] * 2, [x[0], None, None], [None] * 2, [None] * 2
    params[0] = layer_params(0, gathered(_chip_exchange(gather_srcs(0), per_dest=False, name="gather_l0"), "gather_l0_join"))
    xs[1], saved[0], got = _layer_fwd(xs[0], params[0], "l0_", side=(gather_srcs(1), False))
    params[1] = layer_params(1, gathered(got, "gather_l1_join"))
    xs[2], saved[1], _ = _layer_fwd(xs[1], params[1], "l1_")
    dx, loss_row, g_final = _final(xs[2], final_g.reshape(1, -1), loss_target[0], bt=BT, name="final")
    dx, grads[1], _, _ = _layer_bwd(xs[1], dx, params[1], saved[1], "l1_")
    early = lambda g: [(grads[1], n, "l1_" + n) for n in BIG] + [(g, n, "l0_" + n) for n in out_proj]
    dx, grads[0], got, got_late = _layer_bwd(
        xs[0], dx, params[0], saved[0], "l0_",
        side=lambda g: (reduce_prepare(early(g), "reduce_early_"), True),
        side_late=lambda g: (reduce_prepare([(g, "w_in", "l0_w_in")], "reduce_late_"), True))
    total = reduce_finish(got + got_late, [lab for _, _, lab in early(None)] + ["l0_w_in"], "reduce_")
    totals = [{n: total[f"l{i}_" + n] for n in BIG} for i in range(DEPTH)]

    def stacked(fn):
        return jnp.stack([fn(grads[i]) for i in range(DEPTH)])

    g_loc = {
        "norm_g": stacked(lambda g: g["norm_g"][0]),
        "conv_w": stacked(lambda g: jnp.concatenate(g["conv"][:4], axis=0)),
        "conv_b": stacked(lambda g: g["conv"][4][0]),
        "dt_bias": stacked(lambda g: g["dt_bias"][0, :SSD_HEADS]),
        "a_log": stacked(lambda g: g["a_log"][0, :SSD_HEADS]),
        "d_skip": stacked(lambda g: g["d_skip"][0, :SSD_HEADS]),
        "ssd_norm_g": stacked(lambda g: g["ssd_norm_g"][0]),
        "rw_mu": stacked(lambda g: g["rw_mu"][0]),
        "rw_w0": stacked(lambda g: g["rw_pre"][0][0]),
        "rw_w_up": stacked(lambda g: g["rw_pre"][1][:HEAD]),
        "rw_a0": stacked(lambda g: g["rw_pre"][2][0]),
        "rw_a_up": stacked(lambda g: g["rw_pre"][3][HEAD:]),
        "rw_k_k": stacked(lambda g: g["rw_pre"][4][0]),
        "rw_k_a": stacked(lambda g: g["rw_pre"][5][0]),
        "rw_r_k": stacked(lambda g: g["rw_r_k"].reshape(8, HEAD)),
        "rw_ln_g": stacked(lambda g: g["rw_ln_g"][0]),
        "rw_ln_b": stacked(lambda g: g["rw_ln_b"][0]),
        "final_g": g_final[0],
    }

    g_out = {n: jnp.stack([totals[0][n], totals[1][n]]) for n in BIG}

    sm_all = SMALL + ("loss",)
    sm_full_shapes = [g_loc[n].shape for n in SMALL] + [(1,)]
    _, summed = _allgather_small(_pack_rows([g_loc[n] for n in SMALL] + [loss_row[0, :1]]), reduce=True, name="reduce_small")
    sm = dict(zip(sm_all, _unpack_rows(summed, sm_full_shapes)))
    for n in SMALL:
        g_out[n] = sm[n]
    for n, wd in SMALL_SHARDED.items():
        g_out[n] = lax.dynamic_slice_in_dim(sm[n], chip * wd, wd, axis=sm[n].ndim - 1)
    loss = sm["loss"][0]

    upd = {n: _adamw(w_loc[n], g_out[n], m_loc[n], v_loc[n], name="adamw_" + n) for n in names if n != "w_in"}
    cols = SHARD_COLS // 4
    to_cols = lambda a: jnp.transpose(a, (2, 0, 1)).reshape(4, cols, DEPTH, D_MODEL)
    from_cols = lambda a: jnp.transpose(a.reshape(SHARD_COLS, DEPTH, D_MODEL), (1, 2, 0))
    g_cols = lax.optimization_barrier(to_cols(g_out["w_in"]))
    g_out["w_in"] = from_cols(g_cols)
    upd["w_in"] = tuple(from_cols(a) for a in _adamw(
        to_cols(w_loc["w_in"]), g_cols, to_cols(m_loc["w_in"]), to_cols(v_loc["w_in"]),
        name="adamw_w_in", block=(1, cols, DEPTH, D_MODEL // 2)))
    return (loss, dx[None], *[g_out[n] for n in names], *[upd[n][0] for n in names],
            *[upd[n][1] for n in names], *[upd[n][2] for n in names])
```

```python
import functools

import jax
import jax.numpy as jnp
from jax import lax
from jax.experimental import pallas as pl
from jax.experimental.pallas import tpu as pltpu

F32 = jnp.float32
BF16 = jnp.bfloat16

D_MODEL = 1024
DEPTH = 2
HEAD = 64
LANES = 128
CHUNK = 128
RMS_EPS = 1e-6
GN_EPS = 64e-5
VMEM_LIMIT = 56 * 1024 * 1024

N_IN = 9616
N_PAD = 9728
C_SB, C_Z, C_GATES, C_RW, C_LO, C_DT, C_XBC = 0, 2048, 3072, 6144, 8192, 8320, 8448
RW_COLS = 2176
XBC_COLS = 1280

ADAM_LR, ADAM_B1, ADAM_B2, ADAM_EPS, ADAM_WD, ADAM_STEP = 0.001, 0.9, 0.999, 1e-08, 0.01, 10


def _params(sem=None):
    return pltpu.CompilerParams(dimension_semantics=sem, vmem_limit_bytes=VMEM_LIMIT)


@jax.custom_vjp
def _sigmoid(x):
    return 1.0 / (1.0 + jnp.exp(-x))


def _sigmoid_fwd(x):
    s = _sigmoid(x)
    return s, s


def _sigmoid_bwd(s, g):
    return (g * s * (1.0 - s),)


_sigmoid.defvjp(_sigmoid_fwd, _sigmoid_bwd)


@jax.custom_vjp
def _silu(x):
    return x * _sigmoid(x)


def _silu_fwd(x):
    s = _sigmoid(x)
    return x * s, (x, s)


def _silu_bwd(res, g):
    x, s = res
    return (g * (s + x * s * (1.0 - s)),)


_silu.defvjp(_silu_fwd, _silu_bwd)


@jax.custom_vjp
def _softplus(x):
    return jnp.maximum(x, 0.0) + jnp.log(1.0 + jnp.exp(-jnp.abs(x)))


def _softplus_fwd(x):
    return _softplus(x), x


def _softplus_bwd(x, g):
    return (g * _sigmoid(x),)


_softplus.defvjp(_softplus_fwd, _softplus_bwd)


def _dot(a, b, dims):
    return lax.dot_general(a.astype(BF16), b.astype(BF16), (dims, ((), ())), preferred_element_type=F32)


def _dot_nn(a, b):
    return _dot(a, b, ((1,), (0,)))


def _dot_nt(a, b):
    return _dot(a, b, ((1,), (1,)))


def _dot_tn(a, b):
    return _dot(a, b, ((0,), (0,)))


@jax.custom_vjp
def _bdot(a, b):
    return _dot_nn(a, b)


def _bdot_fwd(a, b):
    return _dot_nn(a, b), (a, b)


def _bdot_bwd(res, g):
    a, b = res
    return _dot_nt(g, b), _dot_tn(a, g)


_bdot.defvjp(_bdot_fwd, _bdot_bwd)


def _split2(x):
    hi = x.astype(BF16)
    lo = (x - hi.astype(F32)).astype(BF16)
    return hi, lo


_NT = (((1,), (1,)), ((), ()))
_NN = (((1,), (0,)), ((), ()))
_TN = (((0,), (0,)), ((), ()))


def _dot2(x, m, dn=_NN):
    hi, lo = _split2(x)
    return (lax.dot_general(hi, m, dn, preferred_element_type=F32)
            + lax.dot_general(lo, m, dn, preferred_element_type=F32))


def _seg_matrix(n):
    r = lax.broadcasted_iota(jnp.int32, (n, n), 0) // HEAD
    c = lax.broadcasted_iota(jnp.int32, (n, n), 1) // HEAD
    return (r == c).astype(BF16)


@jax.custom_vjp
def _segsum2(x, seg):
    return _dot2(x, seg)


def _segsum2_fwd(x, seg):
    return _dot2(x, seg), seg


def _segsum2_bwd(seg, g):
    return _dot2(g, seg), jnp.zeros_like(seg)


_segsum2.defvjp(_segsum2_fwd, _segsum2_bwd)


def _make_segsum(seg):
    return lambda x: _segsum2(x, seg)


def _shift_down_raw(x, k):
    row = lax.broadcasted_iota(jnp.int32, x.shape, 0)
    return jnp.where(row >= k, pltpu.roll(x, k, 0), 0.0)


def _shift_up_raw(x, k):
    t = x.shape[0]
    row = lax.broadcasted_iota(jnp.int32, x.shape, 0)
    return jnp.where(row < t - k, pltpu.roll(x, t - k, 0), 0.0)


@functools.partial(jax.custom_vjp, nondiff_argnums=(1,))
def _shift_down(x, k):
    return _shift_down_raw(x, k)


def _shift_down_fwd(x, k):
    return _shift_down_raw(x, k), None


def _shift_down_bwd(k, _, g):
    return (_shift_up_raw(g, k),)


_shift_down.defvjp(_shift_down_fwd, _shift_down_bwd)


def _mm(a, b, *, name, ta=False, tb=False, add=None, out_dtype=F32, tm=2048, tn=512, tk=None, side=None):
    m, k = (a.shape[1], a.shape[0]) if ta else a.shape
    n = b.shape[0] if tb else b.shape[1]
    tm, tn = min(tm, m), min(tn, n)
    tk = k if tk is None else tk
    nk = k // tk
    assert m % tm == 0 and n % tn == 0 and k % tk == 0
    dims = ((0 if ta else 1,), (1 if tb else 0,))

    def body(a_ref, b_ref, *refs):
        o_ref, acc_ref = refs[-2:]
        p = _dot(a_ref[...], b_ref[...], dims)

        def emit(total):
            if add is not None:
                total = total + refs[0][...]
            o_ref[...] = total.astype(o_ref.dtype)

        if nk == 1:
            emit(p)
        else:
            kk = pl.program_id(2)

            @pl.when(kk == 0)
            def _():
                acc_ref[...] = p

            @pl.when(kk > 0)
            def _():
                acc_ref[...] += p

            @pl.when(kk == nk - 1)
            def _():
                emit(acc_ref[...])

    a_spec = pl.BlockSpec((tk, tm), lambda i, j, kk: (kk, i)) if ta else pl.BlockSpec((tm, tk), lambda i, j, kk: (i, kk))
    b_spec = pl.BlockSpec((tn, tk), lambda i, j, kk: (j, kk)) if tb else pl.BlockSpec((tk, tn), lambda i, j, kk: (kk, j))
    o_spec = pl.BlockSpec((tm, tn), lambda i, j, kk: (i, j))
    res = _call_with_side(
        body, side, name=name, grid=(m // tm, n // tn, nk), semantics=("parallel", "parallel", "arbitrary"),
        in_specs=[a_spec, b_spec] + ([o_spec] if add is not None else []), out_specs=[o_spec],
        out_shape=[jax.ShapeDtypeStruct((m, n), out_dtype)],
        scratch_shapes=[pltpu.VMEM((tm, tn) if nk > 1 else (8, LANES), F32)],
        operands=(a, b) + ((add,) if add is not None else ()))
    return res[0] if side is None else (res[0], res[1])


def _row_specs(rows, bt):
    return [pl.BlockSpec((bt, w), functools.partial(lambda i, c: (i, c), c=c)) for _, w, c in rows]


def _full_spec(p):
    return pl.BlockSpec(p.shape, functools.partial(lambda i, nd: (0,) * nd, nd=p.ndim))


def _rowwise(f, rows, pars, out_widths, *, bt, name, acc_widths=()):
    t = rows[0][0].shape[0]
    nr, npar, no, na = len(rows), len(pars), len(out_widths), len(acc_widths)

    def body(*refs):
        vals = [r[...] for r in refs[:nr + npar]]
        outs = f(*vals)
        for o_ref, o in zip(refs[nr + npar:nr + npar + no], outs[:no]):
            o_ref[...] = o.astype(o_ref.dtype)
        if na:
            first = pl.program_id(0) == 0
            for a_ref, a in zip(refs[nr + npar + no:], outs[no:]):
                @pl.when(first)
                def _():
                    a_ref[...] = jnp.zeros_like(a_ref)
                a_ref[...] += a

    return pl.pallas_call(
        body, name=name, grid=(t // bt,),
        in_specs=_row_specs(rows, bt) + [_full_spec(p) for p in pars],
        out_specs=[pl.BlockSpec((bt, w), lambda i: (i, 0)) for w in out_widths]
        + [pl.BlockSpec((1, w), lambda i: (0, 0)) for w in acc_widths],
        out_shape=[jax.ShapeDtypeStruct((t, w), F32) for w in out_widths]
        + [jax.ShapeDtypeStruct((1, w), F32) for w in acc_widths],
        compiler_params=_params(("arbitrary",)),
    )(*[r[0] for r in rows], *pars)


def _rowwise_bwd(f, rows, pars, douts, *, bt, name, groups=None):
    t = rows[0][0].shape[0]
    nr, npar, nd = len(rows), len(pars), len(douts)
    groups = [[i] for i in range(nr)] if groups is None else groups
    widths = [r[1] for r in rows]

    def body(*refs):
        vals = [r[...] for r in refs[:nr + npar]]
        cts = tuple(r[...] for r in refs[nr + npar:nr + npar + nd])
        _, vjp = jax.vjp(lambda *a: tuple(f(*a)), *vals)
        grads = vjp(cts)
        out_refs = refs[nr + npar + nd:]
        for g_ref, grp in zip(out_refs[:len(groups)], groups):
            off = 0
            for i in grp:
                g_ref[:, off:off + widths[i]] = grads[i]
                off += widths[i]
        first = pl.program_id(0) == 0
        for p_ref, g in zip(out_refs[len(groups):], grads[nr:]):
            @pl.when(first)
            def _():
                p_ref[...] = jnp.zeros_like(p_ref)
            p_ref[...] += g

    gw = [sum(widths[i] for i in grp) for grp in groups]
    return pl.pallas_call(
        body, name=name, grid=(t // bt,),
        in_specs=_row_specs(rows, bt) + [_full_spec(p) for p in pars] + _row_specs(douts, bt),
        out_specs=[pl.BlockSpec((bt, w), lambda i: (i, 0)) for w in gw] + [_full_spec(p) for p in pars],
        out_shape=[jax.ShapeDtypeStruct((t, w), F32) for w in gw] + [jax.ShapeDtypeStruct(p.shape, F32) for p in pars],
        compiler_params=_params(("arbitrary",)),
    )(*[r[0] for r in rows], *pars, *[d[0] for d in douts])


def _colwise(f, x, c0, ncols, pars, *, bc, name):
    t = x.shape[0]

    def body(x_ref, *refs):
        o_ref = refs[-1]
        o_ref[...] = f(x_ref[...], *[r[...] for r in refs[:-1]])

    return pl.pallas_call(
        body, name=name, grid=(ncols // bc,),
        in_specs=[pl.BlockSpec((t, bc), lambda j: (0, j + c0 // bc))]
        + [pl.BlockSpec((p.shape[0], bc), lambda j: (0, j)) for p in pars],
        out_specs=pl.BlockSpec((t, bc), lambda j: (0, j)),
        out_shape=jax.ShapeDtypeStruct((t, ncols), F32),
        compiler_params=_params(("parallel",)),
    )(x, *pars)


def _colwise_bwd(f, x, c0, ncols, pars, dout, *, bc, name):
    t = x.shape[0]
    npar = len(pars)

    def body(x_ref, *refs):
        vals = [x_ref[...]] + [r[...] for r in refs[:npar]]
        _, vjp = jax.vjp(f, *vals)
        grads = vjp(refs[npar][...])
        for g_ref, g in zip(refs[npar + 1:], grads):
            g_ref[...] = g

    return pl.pallas_call(
        body, name=name, grid=(ncols // bc,),
        in_specs=[pl.BlockSpec((t, bc), lambda j: (0, j + c0 // bc))]
        + [pl.BlockSpec((p.shape[0], bc), lambda j: (0, j)) for p in pars]
        + [pl.BlockSpec((t, bc), lambda j: (0, j))],
        out_specs=[pl.BlockSpec((t, bc), lambda j: (0, j))]
        + [pl.BlockSpec((p.shape[0], bc), lambda j: (0, j)) for p in pars],
        out_shape=[jax.ShapeDtypeStruct((t, ncols), F32)] + [jax.ShapeDtypeStruct(p.shape, F32) for p in pars],
        compiler_params=_params(("parallel",)),
    )(x, *pars, dout)


def _f_rms(x, g):
    return (x * lax.rsqrt(jnp.mean(x * x, axis=-1, keepdims=True) + RMS_EPS) * g,)


def _f_sb_gate(y, gate):
    return (y * _silu(gate),)


def _f_ssd_norm(y, z, g):
    u = y * _silu(z)
    return (u * lax.rsqrt(jnp.mean(u * u, axis=-1, keepdims=True) + RMS_EPS) * g,)


def _f_merge(p_sb, p_ssd, p_rw, g_sb, g_ssd, g_rw):
    return (_sigmoid(g_sb) * p_sb + _sigmoid(g_ssd) * p_ssd + _sigmoid(g_rw) * p_rw,)


def _f_rw_pre(k, lo, w0, w_up, a0, a_up, k_k, k_a):
    segsum = _make_segsum(_seg_matrix(k.shape[1]))
    lane = lax.broadcasted_iota(jnp.int32, lo.shape, 1)
    w_lo = jnp.where(lane < HEAD, jnp.tanh(lo), 0.0)
    a_lo = jnp.where(lane >= HEAD, lo, 0.0)
    w = -_softplus(-(w0 + _bdot(w_lo, w_up))) - 0.5
    log_decay = -jnp.exp(w)
    a = _sigmoid(a0 + _bdot(a_lo, a_up))
    kk = k * k_k
    kk = kk / jnp.maximum(jnp.sqrt(segsum(kk * kk)), 1e-12)
    return log_decay, k * (1.0 + (a - 1.0) * k_a), -kk, kk * a


def _f_rw_post(y, r, k2, v, gate, ln_g, ln_b, r_k):
    segsum = _make_segsum(_seg_matrix(y.shape[1]))
    yc = y - segsum(y) * (1.0 / HEAD)
    var = segsum(yc * yc) * (1.0 / HEAD)
    yn = yc * lax.rsqrt(var + GN_EPS) * ln_g + ln_b
    return ((yn + segsum(r * k2 * r_k) * v) * _silu(gate),)


def _f_rw_mix(slab, mu):
    return slab + (_shift_down(slab, 1) - slab) * mu


def _f_conv(x, w0, w1, w2, w3, b):
    acc = x * w3 + b
    for i, w in enumerate((w0, w1, w2)):
        acc = acc + _shift_down(x, 3 - i) * w
    return _silu(acc)


def _log_sigmoid(z):
    return jnp.minimum(z, 0.0) - jnp.log(1.0 + jnp.exp(-jnp.abs(z)))


SB_BQ = 256
SB_BK = 256
assert SB_BQ == SB_BK


def _tri_ones(kind):
    j = lax.broadcasted_iota(jnp.int32, (SB_BK, SB_BK + LANES), 0)
    s = lax.broadcasted_iota(jnp.int32, (SB_BK, SB_BK + LANES), 1)
    tri = {"gt": j > s, "le": j <= s, "lt": j < s}[kind]
    return (tri | (s >= SB_BK)).astype(BF16)


def _sb_common(q_ref):
    lane = lax.broadcasted_iota(jnp.int32, (SB_BQ, LANES), 1)
    q = q_ref[...] * (HEAD ** -0.5)
    q2 = jnp.concatenate([jnp.where(lane < HEAD, q, 0.0), jnp.where(lane >= HEAD, q, 0.0)], axis=0).astype(BF16)
    diff = (lax.broadcasted_iota(jnp.int32, (2 * SB_BQ, SB_BK), 1)
            - (lax.broadcasted_iota(jnp.int32, (2 * SB_BQ, SB_BK), 0) & (SB_BQ - 1)))
    return lane, q2, diff


def _rep(x):
    return jnp.concatenate([x] * (SB_BK // LANES), axis=1)


def _sb2_specs(t):
    q = pl.BlockSpec((SB_BQ, LANES), lambda j, i: (i, j))
    k = pl.BlockSpec((t, LANES), lambda j, i: (0, 4 + j))
    v = pl.BlockSpec((t, LANES), lambda j, i: (0, 8 + j))
    return q, k, v


def _sb2_fwd(proj, *, name):
    t = proj.shape[0]

    def body(q_ref, k_ref, v_ref, y_ref, lt_ref):
        i = pl.program_id(1)
        lane, q2, diff = _sb_common(q_ref)
        m_f = _tri_ones("gt")

        def step(kb, carry, diagonal):
            c, acc = carry
            off = pl.multiple_of(kb * SB_BK, SB_BK)
            kblk = k_ref[pl.ds(off, SB_BK), :].astype(BF16)
            vblk = v_ref[pl.ds(off, SB_BK), :].astype(BF16)
            z = lax.dot_general(q2, kblk, _NT, preferred_element_type=F32)
            lb = _log_sigmoid(z)
            lk = jnp.where(diff < 0, lb - z, 0.0) if diagonal else lb - z
            w2 = _dot2(lk, m_f)
            att = jnp.exp(lb + _rep(c) + w2[:, :SB_BK])
            if diagonal:
                att = jnp.where(diff < 0, att, 0.0)
            acc = acc + lax.dot_general(att.astype(BF16), vblk, _NN, preferred_element_type=F32)
            return c + w2[:, SB_BK:], acc

        zero = jnp.zeros((2 * SB_BQ, LANES), F32)
        c, acc = lax.fori_loop(0, i, lambda it, carry: step(i - 1 - it, carry, False), step(i, (zero, zero), True))
        y_ref[...] = jnp.where(lane < HEAD, acc[:SB_BQ], acc[SB_BQ:])
        lt_ref[0] = c[:SB_BQ]
        lt_ref[1] = c[SB_BQ:]

    return pl.pallas_call(
        body, name=name, grid=(4, t // SB_BQ),
        in_specs=list(_sb2_specs(t)),
        out_specs=[pl.BlockSpec((SB_BQ, LANES), lambda j, i: (i, j)),
                   pl.BlockSpec((2, SB_BQ, LANES), lambda j, i: (j, i, 0))],
        out_shape=[jax.ShapeDtypeStruct((t, 4 * LANES), F32), jax.ShapeDtypeStruct((8, t, LANES), F32)],
        compiler_params=_params(("parallel", "arbitrary")),
    )(proj, proj, proj)


def _sb2_bwd(proj, dy, lt, *, name):
    t = proj.shape[0]

    def body(q_ref, k_ref, v_ref, dy_ref, lt_ref, dq_ref, dk_ref, dv_ref):
        i = pl.program_id(1)

        @pl.when(i == 0)
        def _():
            dk_ref[...] = jnp.zeros_like(dk_ref)
            dv_ref[...] = jnp.zeros_like(dv_ref)

        lane, q2, diff = _sb_common(q_ref)
        m_le, m_lt = _tri_ones("le"), _tri_ones("lt")
        dy_blk = dy_ref[...]
        do2 = jnp.concatenate([jnp.where(lane < HEAD, dy_blk, 0.0), jnp.where(lane >= HEAD, dy_blk, 0.0)],
                              axis=0).astype(BF16)
        lt2 = jnp.concatenate([lt_ref[0], lt_ref[1]], axis=0)

        def step(kb, carry, diagonal):
            cp, cg, dq = carry
            off = pl.multiple_of(kb * SB_BK, SB_BK)
            kblk = k_ref[pl.ds(off, SB_BK), :].astype(BF16)
            vblk = v_ref[pl.ds(off, SB_BK), :].astype(BF16)
            z = lax.dot_general(q2, kblk, _NT, preferred_element_type=F32)
            lb = _log_sigmoid(z)
            lk = jnp.where(diff < 0, lb - z, 0.0) if diagonal else lb - z
            w2 = _dot2(lk, m_le)
            att = jnp.exp(lb + _rep(lt2 - cp) - w2[:, :SB_BK])
            if diagonal:
                att = jnp.where(diff < 0, att, 0.0)
            d_e = lax.dot_general(do2, vblk, _NT, preferred_element_type=F32) * att
            g2 = _dot2(d_e, m_lt)
            sig = jnp.exp(lb)
            dz = d_e * (1.0 - sig) - (_rep(cg) + g2[:, :SB_BK]) * sig
            dz = (jnp.where(diff < 0, dz, 0.0) if diagonal else dz).astype(BF16)
            dq = dq + lax.dot_general(dz, kblk, _NN, preferred_element_type=F32)
            dk_ref[pl.ds(off, SB_BK), :] += lax.dot_general(dz, q2, _TN, preferred_element_type=F32)
            dv_ref[pl.ds(off, SB_BK), :] += lax.dot_general(att.astype(BF16), do2, _TN, preferred_element_type=F32)
            return cp + w2[:, SB_BK:], cg + g2[:, SB_BK:], dq

        zero = jnp.zeros((2 * SB_BQ, LANES), F32)
        before = lax.fori_loop(0, i, lambda kb, carry: step(kb, carry, False), (zero, zero, zero))
        _, _, dq = step(i, before, True)
        dq_ref[...] = jnp.where(lane < HEAD, dq[:SB_BQ], dq[SB_BQ:]) * (HEAD ** -0.5)

    q_spec, k_spec, v_spec = _sb2_specs(t)
    blk = pl.BlockSpec((SB_BQ, LANES), lambda j, i: (i, j))
    col = pl.BlockSpec((t, LANES), lambda j, i: (0, j))
    return pl.pallas_call(
        body, name=name, grid=(4, t // SB_BQ),
        in_specs=[q_spec, k_spec, v_spec, blk, pl.BlockSpec((2, SB_BQ, LANES), lambda j, i: (j, i, 0))],
        out_specs=[blk, col, col],
        out_shape=[jax.ShapeDtypeStruct((t, 4 * LANES), F32)] * 3,
        compiler_params=_params(("parallel", "arbitrary")),
    )(proj, proj, proj, dy, lt)


SSD_HEADS = 16
SSD_PAIRS = 8


def _split3(x):
    a = x.astype(BF16)
    r = x - a.astype(F32)
    b = r.astype(BF16)
    return a, b, (r - b.astype(F32)).astype(BF16)


def _dot3(x, m, dn=_NN):
    return sum(lax.dot_general(p, m, dn, preferred_element_type=F32) for p in _split3(x))


def _mdot3(m, x):
    return sum(lax.dot_general(m, p, _NN, preferred_element_type=F32) for p in _split3(x))


def _ssd_common(dtr, dtb, alog, acsx_s, acst_s):
    lane = lax.broadcasted_iota(jnp.int32, (CHUNK, LANES), 1)
    lane1 = lax.broadcasted_iota(jnp.int32, (1, LANES), 1)
    arow = jnp.where(lane1 < SSD_HEADS, -jnp.exp(alog), 0.0)
    dt = jnp.where(lane < SSD_HEADS, _softplus(dtr + dtb), 0.0)
    da = dt * arow
    r = lax.broadcasted_iota(jnp.int32, (CHUNK, CHUNK), 0)
    c = lax.broadcasted_iota(jnp.int32, (CHUNK, CHUNK), 1)
    tril = (r >= c).astype(BF16)
    triu = (r <= c).astype(BF16)
    acs = _mdot3(tril, da)
    acst_s[...] = _dot3(da, triu, _TN)
    eh = lax.broadcasted_iota(jnp.int32, (LANES, 8 * LANES), 0)
    e = (eh == lax.broadcasted_iota(jnp.int32, (LANES, 8 * LANES), 1) // HEAD).astype(BF16)
    eh2 = lax.broadcasted_iota(jnp.int32, (LANES, 16 * LANES), 0)
    e2 = (eh2 == lax.broadcasted_iota(jnp.int32, (LANES, 16 * LANES), 1) // LANES).astype(BF16)
    acsx_s[...] = _dot3(acs, e)
    return dt, arow, _dot3(dt, e), _dot3(acs, e2), e, tril, triu


def _ssd_fwd(xc, proj, dtb, alog, dsk, *, name):
    t = xc.shape[0]
    nc = t // CHUNK

    def body(x_ref, b_ref, c_ref, dtr_ref, dtb_ref, alog_ref, dsk_ref, y_ref, hin_ref, acsx_s, acst_s, h_s):
        @pl.when(pl.program_id(0) == 0)
        def _():
            h_s[...] = jnp.zeros_like(h_s)

        dt, arow, dt_x, acs_b, e, tril, _ = _ssd_common(dtr_ref[...], dtb_ref[...], alog_ref[...], acsx_s, acst_s)
        dsk_x = _dot3(jnp.broadcast_to(dsk_ref[...], (CHUNK, LANES)), e)
        lane = lax.broadcasted_iota(jnp.int32, (CHUNK, LANES), 1)
        causal = (lax.broadcasted_iota(jnp.int32, (CHUNK, CHUNK), 0)
                  >= lax.broadcasted_iota(jnp.int32, (CHUNK, CHUNK), 1))
        for j in range(SSD_PAIRS):
            g = j // 4
            sl = slice(j * LANES, (j + 1) * LANES)
            if j % 4 == 0:
                bg = jnp.where(lane // HEAD == g, b_ref[...], 0.0)
                cg = jnp.where(lane // HEAD == g, c_ref[...], 0.0)
                cb = _dot_nt(cg, bg)
            x = x_ref[:, sl]
            a = acsx_s[:, sl]
            at = acsx_s[CHUNK - 1:CHUNK, sl]
            xdt = x * dt_x[:, sl]
            hin = h_s[j]
            hin_ref[0, j] = hin
            y = jnp.exp(a) * _dot_nn(cg, hin) + x * dsk_x[:, sl]
            h_s[j] = jnp.exp(at) * hin + _dot_tn(bg, xdt * jnp.exp(at - a))
            yd = []
            for hh in (0, 1):
                h = 2 * j + hh
                dec = jnp.exp(jnp.minimum(acs_b[:, h * LANES:(h + 1) * LANES] - acst_s[pl.ds(h, 1), :], 0.0))
                yd.append(_dot_nn(jnp.where(causal, cb * dec, 0.0), xdt))
            y_ref[:, sl] = y + jnp.where(lane < HEAD, yd[0], yd[1])

    one = pl.BlockSpec((1, LANES), lambda i: (0, 0))
    return pl.pallas_call(
        body, name=name, grid=(nc,),
        in_specs=[pl.BlockSpec((CHUNK, 8 * LANES), lambda i: (i, 0)),
                  pl.BlockSpec((CHUNK, LANES), lambda i: (i, 8)),
                  pl.BlockSpec((CHUNK, LANES), lambda i: (i, 9)),
                  pl.BlockSpec((CHUNK, LANES), lambda i: (i, C_DT // LANES)), one, one, one],
        out_specs=[pl.BlockSpec((CHUNK, 8 * LANES), lambda i: (i, 0)),
                   pl.BlockSpec((1, SSD_PAIRS, LANES, LANES), lambda i: (i, 0, 0, 0))],
        out_shape=[jax.ShapeDtypeStruct((t, 8 * LANES), F32),
                   jax.ShapeDtypeStruct((nc, SSD_PAIRS, LANES, LANES), F32)],
        scratch_shapes=[pltpu.VMEM((CHUNK, 8 * LANES), F32), pltpu.VMEM((LANES, CHUNK), F32),
                        pltpu.VMEM((SSD_PAIRS, LANES, LANES), F32)],
        compiler_params=_params(("arbitrary",)),
    )(xc, xc, xc, proj, dtb, alog, dsk)


def _ssd_bwd(xc, proj, dtb, alog, dsk, hin_all, dy, *, name):
    t = xc.shape[0]
    nc = t // CHUNK

    def body(x_ref, b_ref, c_ref, dtr_ref, dtb_ref, alog_ref, dsk_ref, hin_ref, dy_ref,
             dxc_ref, ddtr_ref, ddtb_ref, dalog_ref, ddsk_ref, acsx_s, acst_s, dh_s, dax_s, ddx_s):
        @pl.when(pl.program_id(0) == 0)
        def _():
            dh_s[...] = jnp.zeros_like(dh_s)
            ddtb_ref[...] = jnp.zeros_like(ddtb_ref)
            dalog_ref[...] = jnp.zeros_like(dalog_ref)
            ddsk_ref[...] = jnp.zeros_like(ddsk_ref)

        dtr = dtr_ref[...]
        dtb = dtb_ref[...]
        dt, arow, dt_x, acs_b, e, tril, triu = _ssd_common(dtr, dtb, alog_ref[...], acsx_s, acst_s)
        dsk_x = _dot3(jnp.broadcast_to(dsk_ref[...], (CHUNK, LANES)), e)
        lane = lax.broadcasted_iota(jnp.int32, (CHUNK, LANES), 1)
        rowi = lax.broadcasted_iota(jnp.int32, (CHUNK, LANES), 0)
        causal = (lax.broadcasted_iota(jnp.int32, (CHUNK, CHUNK), 0)
                  >= lax.broadcasted_iota(jnp.int32, (CHUNK, CHUNK), 1))
        acs_rows = jnp.zeros((CHUNK, LANES), F32)
        acs_cols = jnp.zeros((LANES, CHUNK), F32)
        d_b = jnp.zeros((CHUNK, LANES), F32)
        d_c = jnp.zeros((CHUNK, LANES), F32)
        for j in range(SSD_PAIRS):
            g = j // 4
            sl = slice(j * LANES, (j + 1) * LANES)
            if j % 4 == 0:
                bg = jnp.where(lane // HEAD == g, b_ref[...], 0.0)
                cg = jnp.where(lane // HEAD == g, c_ref[...], 0.0)
                cb = _dot_nt(cg, bg)
                dcb = jnp.zeros((CHUNK, CHUNK), F32)
            x = x_ref[:, sl]
            d = dt_x[:, sl]
            a = acsx_s[:, sl]
            at = acsx_s[CHUNK - 1:CHUNK, sl]
            xdt = x * d
            hin = hin_ref[0, j]
            dhout = dh_s[j]
            dyp = dy_ref[:, sl]
            ea, eat, ed = jnp.exp(a), jnp.exp(at), jnp.exp(at - a)
            da_l = dyp * ea * _dot_nn(cg, hin)
            dm = dyp * ea
            d_c = d_c + _dot_nt(dm, hin)
            dh_s[j] = _dot_tn(cg, dm) + eat * dhout
            dat = jnp.sum(dhout * hin * eat, axis=0, keepdims=True)
            d_b = d_b + _dot_nt(xdt * ed, dhout)
            dw = _dot_nn(bg, dhout)
            dxdt = dw * ed
            ded = dw * xdt * ed
            dat = dat + jnp.sum(ded, axis=0, keepdims=True)
            da_l = da_l - ded
            for hh in (0, 1):
                h = 2 * j + hh
                dec = jnp.exp(jnp.minimum(acs_b[:, h * LANES:(h + 1) * LANES] - acst_s[pl.ds(h, 1), :], 0.0))
                gm = jnp.where(causal, cb * dec, 0.0)
                dyh = jnp.where(lane // HEAD == hh, dyp, 0.0)
                dg = _dot_nt(dyh, xdt)
                dxdt = dxdt + _dot_tn(gm, dyh)
                dcb = dcb + jnp.where(causal, dg * dec, 0.0)
                th = dg * gm
                acs_rows = acs_rows + jnp.where(lane == h, jnp.sum(th, axis=1, keepdims=True), 0.0)
                acs_cols = acs_cols + jnp.where(rowi == h, jnp.sum(th, axis=0, keepdims=True), 0.0)
            if j % 4 == 3:
                d_c = d_c + _dot_nn(dcb, bg)
                d_b = d_b + _dot_tn(dcb, cg)
            dxc_ref[:, sl] = dyp * dsk_x[:, sl] + dxdt * d
            ddx_s[:, sl] = dxdt * x
            dax_s[:, sl] = da_l + jnp.where(rowi == CHUNK - 1, dat, 0.0)
            dskp = jnp.sum(dyp * x, axis=0, keepdims=True)
            ddsk_ref[...] += _dot2(jnp.broadcast_to(dskp, (8, LANES)), e[:, sl], _NT)
        dxc_ref[:, 8 * LANES:9 * LANES] = d_b
        dxc_ref[:, 9 * LANES:10 * LANES] = d_c
        dacs = acs_rows - acs_cols.T + _dot2(dax_s[...], e, _NT)
        ddt = _dot2(ddx_s[...], e, _NT)
        dda = _mdot3(triu, dacs)
        ddt = ddt + dda * arow
        dalog_ref[...] += jnp.sum(dda * dt, axis=0, keepdims=True) * arow
        ddtr = jnp.where(lane < SSD_HEADS, ddt * _sigmoid(dtr + dtb), 0.0)
        ddtr_ref[...] = ddtr
        ddtb_ref[...] += jnp.sum(ddtr, axis=0, keepdims=True)

    one = pl.BlockSpec((1, LANES), lambda i: (0, 0))
    rev = lambda c: (lambda i: (nc - 1 - i, c))
    return pl.pallas_call(
        body, name=name, grid=(nc,),
        in_specs=[pl.BlockSpec((CHUNK, 8 * LANES), rev(0)), pl.BlockSpec((CHUNK, LANES), rev(8)),
                  pl.BlockSpec((CHUNK, LANES), rev(9)), pl.BlockSpec((CHUNK, LANES), rev(C_DT // LANES)),
                  one, one, one,
                  pl.BlockSpec((1, SSD_PAIRS, LANES, LANES), lambda i: (nc - 1 - i, 0, 0, 0)),
                  pl.BlockSpec((CHUNK, 8 * LANES), rev(0))],
        out_specs=[pl.BlockSpec((CHUNK, XBC_COLS), rev(0)), pl.BlockSpec((CHUNK, LANES), rev(0)), one, one,
                   pl.BlockSpec((8, LANES), lambda i: (0, 0))],
        out_shape=[jax.ShapeDtypeStruct((t, XBC_COLS), F32), jax.ShapeDtypeStruct((t, LANES), F32)]
        + [jax.ShapeDtypeStruct((1, LANES), F32)] * 2 + [jax.ShapeDtypeStruct((8, LANES), F32)],
        scratch_shapes=[pltpu.VMEM((CHUNK, 8 * LANES), F32), pltpu.VMEM((LANES, CHUNK), F32),
                        pltpu.VMEM((SSD_PAIRS, LANES, LANES), F32),
                        pltpu.VMEM((CHUNK, 8 * LANES), F32), pltpu.VMEM((CHUNK, 8 * LANES), F32)],
        compiler_params=_params(("arbitrary",)),
    )(xc, xc, xc, proj, dtb, alog, dsk, hin_all, dy)


RW_C = 64


def _p3(a, b, dn):
    ah, al = _split2(a)
    bh, bl = _split2(b)
    d = lambda x, y: lax.dot_general(x, y, dn, preferred_element_type=F32)
    return d(ah, bh) + d(ah, bl) + d(al, bh)


_BNN = (((2,), (1,)), ((0,), (0,)))
_BNT = (((2,), (2,)), ((0,), (0,)))
_BTN = (((1,), (1,)), ((0,), (0,)))


@jax.custom_vjp
def _pnn(a, b):
    return _p3(a, b, _BNN)


@jax.custom_vjp
def _pnt(a, b):
    return _p3(a, b, _BNT)


@jax.custom_vjp
def _ptn(a, b):
    return _p3(a, b, _BTN)


_pnn.defvjp(lambda a, b: (_p3(a, b, _BNN), (a, b)), lambda res, g: (_p3(g, res[1], _BNT), _p3(res[0], g, _BTN)))
_pnt.defvjp(lambda a, b: (_p3(a, b, _BNT), (a, b)), lambda res, g: (_p3(g, res[1], _BNN), _p3(g, res[0], _BTN)))
_ptn.defvjp(lambda a, b: (_p3(a, b, _BTN), (a, b)), lambda res, g: (_p3(res[1], g, _BNT), _p3(res[0], g, _BNN)))


def _tri2(tril, x, dn):
    hi, lo = _split2(x)
    m = tril.astype(BF16)
    return (lax.dot_general(m, hi, dn, preferred_element_type=F32) + lax.dot_general(m, lo, dn, preferred_element_type=F32))


@jax.custom_vjp
def _cumsum_rows(tril, x):
    return _tri2(tril, x, _BNN)


_cumsum_rows.defvjp(lambda tril, x: (_tri2(tril, x, _BNN), tril),
                    lambda tril, g: (jnp.zeros_like(tril), _tri2(tril, g, _BTN)))


def _rw_chunk_consts():
    c2 = 2 * RW_C
    row = lax.broadcasted_iota(jnp.int32, (c2, c2), 0)
    col = lax.broadcasted_iota(jnp.int32, (c2, c2), 1)
    same = (row // RW_C) == (col // RW_C)
    strict = (same & (row > col)).astype(F32)
    incl = (same & (row >= col)).astype(F32)
    eye = (row == col).astype(F32)
    tr = lax.broadcasted_iota(jnp.int32, (RW_C, RW_C), 0)
    tc = lax.broadcasted_iota(jnp.int32, (RW_C, RW_C), 1)
    tril = (tr >= tc).astype(F32)
    lane = lax.broadcasted_iota(jnp.int32, (1, LANES), 1)
    hm = [(lane // HEAD == h).astype(F32) for h in (0, 1)]
    return strict, incl, eye, tril, hm


def _rw_chunk(r, lw, k, v, n, b, s2, consts):
    strict, incl, eye, tril, hm = consts
    two = lambda x: jnp.concatenate([x * hm[0], x * hm[1]], axis=1)
    cum = _cumsum_rows(jnp.broadcast_to(tril, (4, RW_C, RW_C)), lw)
    grow, shrink = jnp.exp(-cum), jnp.exp(cum)
    n2, r2 = two(n * jnp.exp(cum - lw)), two(r * shrink)
    b2, k2, v2 = two(b * grow), two(k * grow), two(v)
    p = _pnt(n2, b2) * strict
    x2 = _pnt(n2, s2) + _pnn(_pnt(n2, k2) * strict, v2)
    t_inv, a = eye + p, p
    for _ in range(RW_C.bit_length() - 2):
        a = _pnn(a, a)
        t_inv = t_inv + _pnn(t_inv, a)
    u2 = _pnn(t_inv, x2)
    y2 = _pnt(r2, s2) + _pnn(_pnt(r2, b2) * incl, u2) + _pnn(_pnt(r2, k2) * incl, v2)
    s2_new = (s2 + _ptn(u2, b2) + _ptn(v2, k2)) * jnp.exp(jnp.sum(lw, axis=1, keepdims=True))
    return jnp.sum(y2.reshape(4, 2, RW_C, LANES), axis=1), s2_new


def _pairs(ref):
    return jnp.stack([ref[:, p * LANES:(p + 1) * LANES] for p in range(4)])


def _rw_chunk_fwd(mixed, lw, k, n, b, *, name, side=None):
    t = lw.shape[0]
    nc = t // RW_C

    def body(r_ref, v_ref, lw_ref, k_ref, n_ref, b_ref, y_ref, sin_ref, s_s):
        @pl.when(pl.program_id(0) == 0)
        def _():
            s_s[...] = jnp.zeros_like(s_s)

        s2 = s_s[...]
        sin_ref[0] = s2
        y, s2 = _rw_chunk(*[_pairs(x) for x in (r_ref, lw_ref, k_ref, v_ref, n_ref, b_ref)], s2, _rw_chunk_consts())
        for p in range(4):
            y_ref[:, p * LANES:(p + 1) * LANES] = y[p]
        s_s[...] = s2

    blk = lambda c: pl.BlockSpec((RW_C, 4 * LANES), functools.partial(lambda i, c: (i, c), c=c))
    return _call_with_side(
        body, side, name=name, grid=(nc,), semantics=("arbitrary",),
        in_specs=[blk(0), blk(2), blk(0), blk(0), blk(0), blk(0)],
        out_specs=[blk(0), pl.BlockSpec((1, 4, LANES, LANES), lambda i: (i, 0, 0, 0))],
        out_shape=[jax.ShapeDtypeStruct((t, 4 * LANES), F32), jax.ShapeDtypeStruct((nc, 4, LANES, LANES), F32)],
        scratch_shapes=[pltpu.VMEM((4, LANES, LANES), F32)],
        operands=(mixed, mixed, lw, k, n, b))


def _call_with_side(body, side, *, name, grid, semantics, in_specs, out_specs, out_shape, scratch_shapes, operands):
    if side is None:
        return pl.pallas_call(body, name=name, grid=grid, in_specs=in_specs, out_specs=out_specs, out_shape=out_shape,
                              scratch_shapes=scratch_shapes, compiler_params=_params(semantics))(*operands)
    srcs, per_dest = side
    ns, ni, no, nscr = len(srcs), len(in_specs), len(out_specs), len(scratch_shapes)

    def full_body(*refs):
        ins, side_in = refs[:ni], refs[ni:ni + ns]
        outs, side_out = refs[ni + ns:ni + ns + no], refs[ni + ns + no:ni + 2 * ns + no]
        scratch, sems = refs[ni + 2 * ns + no:ni + 2 * ns + no + nscr], refs[ni + 2 * ns + no + nscr:]

        ids = [pl.program_id(a) for a in range(len(grid))]
        first = functools.reduce(jnp.logical_and, [i == 0 for i in ids])
        last = functools.reduce(jnp.logical_and, [i == n - 1 for i, n in zip(ids, grid)])

        @pl.when(first)
        def _():
            _exchange(side_in, side_out, sems, per_dest, start=True, wait=False)

        body(*ins, *outs, *scratch)

        @pl.when(last)
        def _():
            _exchange(side_in, side_out, sems, per_dest, start=False, wait=True)

    res = pl.pallas_call(
        full_body, name=name, grid=grid, in_specs=list(in_specs) + [_ANY] * ns,
        out_specs=list(out_specs) + [_ANY] * ns, out_shape=list(out_shape) + _exchange_out_shapes(srcs),
        scratch_shapes=list(scratch_shapes) + _exchange_sems(ns), compiler_params=_params(("arbitrary",) * len(grid)),
    )(*operands, *srcs)
    return list(res[:no]) + [list(res[no:])]


def _rw_chunk_bwd(mixed, lw, k, n, b, s_in, dy, dr0, dk0, dv0, *, name, side=None):
    t = lw.shape[0]
    nc = t // RW_C

    def body(r_ref, v_ref, lw_ref, k_ref, n_ref, b_ref, sin_ref, dy_ref, dr0_ref, dk0_ref, dv0_ref,
             dr_ref, dlw_ref, dk_ref, dv_ref, dn_ref, db_ref, ds_s):
        @pl.when(pl.program_id(0) == 0)
        def _():
            ds_s[...] = jnp.zeros_like(ds_s)

        consts = _rw_chunk_consts()
        args = [_pairs(x) for x in (r_ref, lw_ref, k_ref, v_ref, n_ref, b_ref)] + [sin_ref[0]]
        _, vjp = jax.vjp(lambda *a: _rw_chunk(*a, consts), *args)
        dr, dlw, dk, dv, dn, db, ds = vjp((_pairs(dy_ref), ds_s[...]))
        for p in range(4):
            sl = slice(p * LANES, (p + 1) * LANES)
            dr_ref[:, sl] = dr[p] + dr0_ref[:, sl]
            dlw_ref[:, sl] = dlw[p]
            dk_ref[:, sl] = dk[p] + dk0_ref[:, sl]
            dv_ref[:, sl] = dv[p] + dv0_ref[:, sl]
            dn_ref[:, sl] = dn[p]
            db_ref[:, sl] = db[p]
        ds_s[...] = ds

    blk = lambda c: pl.BlockSpec((RW_C, 4 * LANES), functools.partial(lambda i, c: (nc - 1 - i, c), c=c))
    return _call_with_side(
        body, side, name=name, grid=(nc,), semantics=("arbitrary",),
        in_specs=[blk(0), blk(2), blk(0), blk(0), blk(0), blk(0),
                  pl.BlockSpec((1, 4, LANES, LANES), lambda i: (nc - 1 - i, 0, 0, 0)), blk(0), blk(0), blk(0), blk(0)],
        out_specs=[blk(0)] * 6,
        out_shape=[jax.ShapeDtypeStruct((t, 4 * LANES), F32)] * 6,
        scratch_shapes=[pltpu.VMEM((4, LANES, LANES), F32)],
        operands=(mixed, mixed, lw, k, n, b, s_in, dy, dr0, dk0, dv0))


def _f_rms_res(x, g):
    return _f_rms(x, g)[0], x


def _final(x, g, target, *, bt, name):
    t, d = x.shape

    def body(x_ref, g_ref, t_ref, dx_ref, loss_ref, dg_ref):
        tgt = t_ref[...]

        def f(xv, gv):
            err = _f_rms(xv, gv)[0] - tgt
            return 0.5 * jnp.mean(err * err, axis=-1, keepdims=True)

        row_loss, vjp = jax.vjp(f, x_ref[...], g_ref[...])
        dx, dg = vjp(jnp.ones_like(row_loss))
        dx_ref[...] = dx

        @pl.when(pl.program_id(0) == 0)
        def _():
            loss_ref[...] = jnp.zeros_like(loss_ref)
            dg_ref[...] = jnp.zeros_like(dg_ref)

        loss_ref[...] += jnp.broadcast_to(jnp.sum(row_loss, axis=0, keepdims=True), (1, LANES))
        dg_ref[...] += dg

    blk = pl.BlockSpec((bt, d), lambda i: (i, 0))
    return pl.pallas_call(
        body, name=name, grid=(t // bt,),
        in_specs=[blk, pl.BlockSpec((1, d), lambda i: (0, 0)), blk],
        out_specs=[blk, pl.BlockSpec((1, LANES), lambda i: (0, 0)), pl.BlockSpec((1, d), lambda i: (0, 0))],
        out_shape=[jax.ShapeDtypeStruct((t, d), F32), jax.ShapeDtypeStruct((1, LANES), F32),
                   jax.ShapeDtypeStruct((1, d), F32)],
        compiler_params=_params(("arbitrary",)),
    )(x, g, target)


ADAMW_BLOCK_BYTES = 1 << 20


def _adamw(w, g, m, v, *, name, block=None):
    shape = w.shape
    if block is not None:
        return _adamw_blocks(w, g, m, v, block, name)
    c = shape[-1]
    shape3 = (1,) * (3 - len(shape)) + shape if len(shape) <= 3 else (-1,) + shape[-2:]
    args = [a.reshape(shape3) for a in (w, g, m, v)]
    lead, r, _ = args[0].shape
    br = r
    if r * c * 4 > ADAMW_BLOCK_BYTES:
        cands = [b for b in range(8, r, 8) if r % b == 0 and b * c * 4 <= ADAMW_BLOCK_BYTES]
        br = max(cands) if cands else r
    outs = _adamw_blocks(*args, (1, br, c), name)
    return tuple(o.reshape(shape) for o in outs)


def _adamw_blocks(w, g, m, v, block, name):
    shape = w.shape
    assert all(s % b == 0 for s, b in zip(shape, block))

    def body(w_ref, g_ref, m_ref, v_ref, d_ref, nm_ref, nv_ref):
        gv = g_ref[...]
        m_new = ADAM_B1 * m_ref[...] + (1.0 - ADAM_B1) * gv
        v_new = ADAM_B2 * v_ref[...] + (1.0 - ADAM_B2) * (gv * gv)
        m_hat = m_new / (1.0 - ADAM_B1 ** ADAM_STEP)
        v_hat = v_new / (1.0 - ADAM_B2 ** ADAM_STEP)
        d_ref[...] = -ADAM_LR * (m_hat / (jnp.sqrt(v_hat) + ADAM_EPS) + ADAM_WD * w_ref[...])
        nm_ref[...] = m_new
        nv_ref[...] = v_new

    blk = pl.BlockSpec(tuple(block), lambda *ids: ids)
    return pl.pallas_call(
        body, name=name, grid=tuple(s // b for s, b in zip(shape, block)), in_specs=[blk] * 4, out_specs=[blk] * 3,
        out_shape=[jax.ShapeDtypeStruct(shape, F32)] * 3,
        compiler_params=_params(("parallel",) * len(shape)),
    )(w, g, m, v)


BT = 256
BC = 128


def _layer_rows(x, proj, s):
    s = {k: s.get(k) for k in ("y_sb_raw", "y_ssd_raw", "mixed", "ys", "k2", "p_sb", "p_ssd", "p_rw")}
    return dict(
        rms=[(x, D_MODEL, 0)],
        sb_gate=[(s["y_sb_raw"], 512, 0), (proj, 512, 3)],
        ssd_norm=[(s["y_ssd_raw"], 1024, 0), (proj, 1024, C_Z // 1024)],
        rw_pre=[(s["mixed"], 512, 1), (s["mixed"], LANES, 16)],
        rw_post=[(s["ys"], 512, 0), (s["mixed"], 512, 0), (s["k2"], 512, 0), (s["mixed"], 512, 2), (s["mixed"], 512, 3)],
        merge=[(s["p_sb"], 1024, 0), (s["p_ssd"], 1024, 0), (s["p_rw"], 1024, 0),
               (proj, 1024, 3), (proj, 1024, 4), (proj, 1024, 5)],
    )


def _layer_fwd(x, p, nm, side=None):
    s = {}
    (s["h"],) = _rowwise(_f_rms, [(x, D_MODEL, 0)], [p["norm_g"]], [D_MODEL], bt=BT, name=nm + "rms")
    proj = s["proj"] = _mm(s["h"], p["w_in"], name=nm + "proj")
    s["y_sb_raw"], s["lt"] = _sb2_fwd(proj, name=nm + "sb")
    s["xc"] = _colwise(_f_conv, proj, C_XBC, XBC_COLS, p["conv"], bc=BC, name=nm + "conv")
    s["y_ssd_raw"], s["hin"] = _ssd_fwd(s["xc"], proj, p["dt_bias"], p["a_log"], p["d_skip"], name=nm + "ssd")
    s["mixed"] = _colwise(_f_rw_mix, proj, C_RW, RW_COLS, [p["rw_mu"]], bc=BC, name=nm + "mix")
    s["w"], s["k2"], s["n"], s["b"] = _rowwise(_f_rw_pre, [(s["mixed"], 512, 1), (s["mixed"], LANES, 16)], p["rw_pre"],
                                               [512] * 4, bt=BT, name=nm + "rwpre")
    s["ys"], s["st"], *exchanged = _rw_chunk_fwd(s["mixed"], s["w"], s["k2"], s["n"], s["b"], name=nm + "scan", side=side)
    rows = _layer_rows(x, proj, s)
    (s["y_sb"],) = _rowwise(_f_sb_gate, rows["sb_gate"], [], [512], bt=BT, name=nm + "sbgate")
    (s["y_ssd"],) = _rowwise(_f_ssd_norm, rows["ssd_norm"], [p["ssd_norm_g"]], [1024], bt=BT, name=nm + "ssdnorm")
    (s["y_rw"],) = _rowwise(_f_rw_post, rows["rw_post"], p["rw_post"], [512], bt=BT, name=nm + "rwpost")
    s["p_sb"] = _mm(s["y_sb"], p["w_out_sb"], name=nm + "osb")
    s["p_ssd"] = _mm(s["y_ssd"], p["w_out_ssd"], name=nm + "ossd")
    s["p_rw"] = _mm(s["y_rw"], p["w_out_rw"], name=nm + "orw")
    (s["merged"],) = _rowwise(_f_merge, _layer_rows(x, proj, s)["merge"], [], [1024], bt=BT, name=nm + "merge")
    return _mm(s["merged"], p["w_o"], add=x, name=nm + "wo"), s, (exchanged[0] if exchanged else None)


def _layer_bwd(x, dx_out, p, s, nm, side=None, side_late=None):
    g = {}
    proj = s["proj"]
    rows = _layer_rows(x, proj, s)
    g["w_o"] = _mm(s["merged"], dx_out, ta=True, name=nm + "g_wo")
    d_merged = _mm(dx_out, p["w_o"], tb=True, name=nm + "d_merged")
    dp_sb, dp_ssd, dp_rw, d_gates = _rowwise_bwd(_f_merge, rows["merge"], [], [(d_merged, 1024, 0)], bt=BT,
                                                 name=nm + "merge_b", groups=[[0], [1], [2], [3, 4, 5]])
    g["w_out_sb"] = _mm(s["y_sb"], dp_sb, ta=True, name=nm + "g_osb")
    g["w_out_ssd"] = _mm(s["y_ssd"], dp_ssd, ta=True, name=nm + "g_ossd")
    g["w_out_rw"] = _mm(s["y_rw"], dp_rw, ta=True, name=nm + "g_orw")
    dy_sb = _mm(dp_sb, p["w_out_sb"], tb=True, name=nm + "d_ysb")
    dy_ssd = _mm(dp_ssd, p["w_out_ssd"], tb=True, name=nm + "d_yssd")
    dy_rw = _mm(dp_rw, p["w_out_rw"], tb=True, name=nm + "d_yrw")
    dy_sb_raw, d_sbgate = _rowwise_bwd(_f_sb_gate, rows["sb_gate"], [], [(dy_sb, 512, 0)], bt=BT, name=nm + "sbgate_b")
    dq, dk, dv = _sb2_bwd(proj, dy_sb_raw, s["lt"], name=nm + "sb_b")
    dy_ssd_raw, dz, g["ssd_norm_g"] = _rowwise_bwd(_f_ssd_norm, rows["ssd_norm"], [p["ssd_norm_g"]],
                                                   [(dy_ssd, 1024, 0)], bt=BT, name=nm + "ssdnorm_b")
    dxc, ddtr, g["dt_bias"], g["a_log"], g["d_skip"] = _ssd_bwd(
        s["xc"], proj, p["dt_bias"], p["a_log"], p["d_skip"], s["hin"], dy_ssd_raw, name=nm + "ssd_b")
    conv_out = _colwise_bwd(_f_conv, proj, C_XBC, XBC_COLS, p["conv"], dxc, bc=BC, name=nm + "conv_b")
    dxbc, g["conv"] = conv_out[0], conv_out[1:]
    dys, dr0, dk0, dv0, d_rwgate, g["rw_ln_g"], g["rw_ln_b"], g["rw_r_k"] = _rowwise_bwd(
        _f_rw_post, rows["rw_post"], p["rw_post"], [(dy_rw, 512, 0)], bt=BT, name=nm + "rwpost_b")
    dr, dw, dk2, dvv, dn, db, *exchanged = _rw_chunk_bwd(s["mixed"], s["w"], s["k2"], s["n"], s["b"], s["st"], dys,
                                                         dr0, dk0, dv0, name=nm + "scan_b",
                                                         side=side(g) if side else None)
    pre_out = _rowwise_bwd(_f_rw_pre, rows["rw_pre"], p["rw_pre"],
                           [(dw, 512, 0), (dk2, 512, 0), (dn, 512, 0), (db, 512, 0)], bt=BT, name=nm + "rwpre_b")
    dkm, dlo, g["rw_pre"] = pre_out[0], pre_out[1], pre_out[2:]
    d_mixed = jnp.concatenate([dr, dkm, dvv, d_rwgate, dlo], axis=1)
    d_slab, g["rw_mu"] = _colwise_bwd(_f_rw_mix, proj, C_RW, RW_COLS, [p["rw_mu"]], d_mixed, bc=BC, name=nm + "mix_b")
    d_proj = jnp.concatenate([dq, dk, dv, d_sbgate, dz, d_gates, d_slab, ddtr, dxbc], axis=1)
    g["w_in"] = _mm(s["h"], d_proj, ta=True, name=nm + "g_win")
    dh = _mm(d_proj, p["w_in"], tb=True, tn=1024, tk=512, name=nm + "d_h", side=side_late(g) if side_late else None)
    dh, late = dh if side_late else (dh, None)
    dx, g["norm_g"] = _rowwise_bwd(_f_rms_res, rows["rms"], [p["norm_g"]], [(dh, D_MODEL, 0), (dx_out, D_MODEL, 0)],
                                   bt=BT, name=nm + "rms_b")
    return dx, g, (exchanged[0] if exchanged else None), late


MESH = pl.DeviceIdType.MESH
N_DEV = 8
_ANY = pl.BlockSpec(memory_space=pl.ANY)


def _here():
    x, y, c = lax.axis_index("x"), lax.axis_index("y"), lax.axis_index("c")
    return x, y, c, [(1 - x, y), (x, 1 - y), (1 - x, 1 - y)]


def _chip_exchange(srcs, *, per_dest, name):
    n = len(srcs)

    def body(*refs):
        _exchange(refs[:n], refs[n:2 * n], refs[2 * n:], per_dest, start=True, wait=True)

    return pl.pallas_call(
        body, name=name, in_specs=[_ANY] * n, out_specs=[_ANY] * n,
        out_shape=_exchange_out_shapes(srcs), scratch_shapes=_exchange_sems(n),
    )(*srcs)


def _exchange_out_shapes(srcs):
    return [jax.ShapeDtypeStruct((4,) + s.shape[-2:], s.dtype) for s in srcs]


def _exchange_sems(n):
    return [pltpu.SemaphoreType.DMA((3 * n,)), pltpu.SemaphoreType.DMA((3 * n,)), pltpu.SemaphoreType.DMA((n,))]


def _exchange(src_refs, out_refs, sems, per_dest, *, start, wait):
    send_sems, recv_sems, local_sems = sems
    x, y, c, chips = _here()
    me = 2 * x + y
    owns, sends, recvs = [], [], []
    for a, (src_ref, out_ref) in enumerate(zip(src_refs, out_refs)):
        if per_dest is True:
            pick = lambda q, s=src_ref: s.at[q]
        elif per_dest is False:
            pick = lambda q, s=src_ref: s.at[c]
        else:
            pick = lambda q, s=src_ref: s.at[per_dest].at[c]
        any_block = src_ref.at[0] if len(src_ref.shape) == 3 else src_ref.at[0].at[0]
        owns.append(pltpu.make_async_copy(pick(me), out_ref.at[me], local_sems.at[a]))
        for j, (px, py) in enumerate(chips):
            sends.append(pltpu.make_async_remote_copy(
                pick(2 * px + py), out_ref.at[me], send_sems.at[3 * a + j], recv_sems.at[3 * a + j],
                device_id=(px, py, c), device_id_type=MESH))
            recvs.append(pltpu.make_async_remote_copy(
                any_block, out_ref.at[2 * px + py], send_sems.at[3 * a + j], recv_sems.at[3 * a + j],
                device_id=(px, py, c), device_id_type=MESH))
    if start:
        for cp in owns + sends:
            cp.start()
    if wait:
        for cp in recvs:
            cp.wait_recv()
        for cp in sends:
            cp.wait_send()
        for cp in owns:
            cp.wait()


def _sibling_swap(srcs, *, other_slot, name):
    n = len(srcs)

    def body(*refs):
        src_refs, out_refs, send_sems, recv_sems = refs[:n], refs[n:2 * n], refs[2 * n], refs[2 * n + 1]
        x, y, c, _ = _here()
        copies = [pltpu.make_async_remote_copy(s.at[1 - c] if other_slot else s, o, send_sems.at[a], recv_sems.at[a],
                                               device_id=(x, y, 1 - c), device_id_type=MESH)
                  for a, (s, o) in enumerate(zip(src_refs, out_refs))]
        for cp in copies:
            cp.start()
        for cp in copies:
            cp.wait()

    return pl.pallas_call(
        body, name=name, in_specs=[_ANY] * n, out_specs=[_ANY] * n,
        out_shape=[jax.ShapeDtypeStruct(s.shape[1:] if other_slot else s.shape, s.dtype) for s in srcs],
        scratch_shapes=[pltpu.SemaphoreType.DMA((n,)), pltpu.SemaphoreType.DMA((n,))],
    )(*srcs)


def _allgather_small(v, *, reduce, name):
    r = v.shape[0]

    def body(v_ref, out_ref, *rest):
        send_sems, recv_sems, local_sem = rest[-3:]
        x, y, c, chips = _here()
        me, sibling = (x, y, c), (x, y, 1 - c)

        def slot(px, py, pc):
            return out_ref.at[4 * px + 2 * py + pc]

        def copy(k, block, to, src=None):
            return pltpu.make_async_remote_copy(
                src_ref=slot(*block) if src is None else src, dst_ref=slot(*block),
                send_sem=send_sems.at[k], recv_sem=recv_sems.at[k], device_id=to, device_id_type=MESH)

        mine = pltpu.make_async_copy(v_ref, slot(*me), local_sem)
        mine.start()
        first = [copy(0, me, sibling, src=v_ref)]
        first += [copy(1 + j, me, (*chip, c), src=v_ref) for j, chip in enumerate(chips)]
        for cp in first:
            cp.start()
        passed = [copy(4 + j, (*chip, c), sibling) for j, chip in enumerate(chips)]
        for j, chip in enumerate(chips):
            copy(1 + j, (*chip, c), me).wait_recv()
            passed[j].start()
        copy(0, sibling, me).wait_recv()
        for j, chip in enumerate(chips):
            copy(4 + j, (*chip, 1 - c), me).wait_recv()
        for cp in first + passed:
            cp.wait_send()
        mine.wait()
        if reduce:
            total = out_ref[0]
            for d in range(1, N_DEV):
                total = total + out_ref[d]
            rest[0][...] = total

    vm = pl.BlockSpec(memory_space=pltpu.VMEM)
    out_shape = [jax.ShapeDtypeStruct((N_DEV, r, LANES), F32)] + ([jax.ShapeDtypeStruct((r, LANES), F32)] if reduce else [])
    return pl.pallas_call(
        body, name=name, in_specs=[vm], out_specs=[vm] * len(out_shape), out_shape=out_shape,
        scratch_shapes=[pltpu.SemaphoreType.DMA((7,)), pltpu.SemaphoreType.DMA((7,)), pltpu.SemaphoreType.DMA],
        compiler_params=pltpu.CompilerParams(vmem_limit_bytes=VMEM_LIMIT),
    )(v)


REDUCE_BLOCK_BYTES = 2 << 20


def _reduce_rows(r, c):
    cands = [b for b in range(16, r + 1, 16) if r % b == 0 and b * c * 4 <= REDUCE_BLOCK_BYTES]
    return max(cands)


def _add_halves(mine2, other, c_idx, *, name):
    _, nq, r, c = mine2.shape
    br = _reduce_rows(r, c)

    def body(c_ref, a_ref, b_ref, o_ref):
        o_ref[...] = (a_ref[0] + b_ref[...]).astype(o_ref.dtype)

    blk = pl.BlockSpec((1, br, c), lambda q, i, c_ref: (q, i, 0))
    return pl.pallas_call(
        body, name=name,
        grid_spec=pltpu.PrefetchScalarGridSpec(
            num_scalar_prefetch=1, grid=(nq, r // br),
            in_specs=[pl.BlockSpec((1, 1, br, c), lambda q, i, c_ref: (c_ref[0], q, i, 0)), blk],
            out_specs=blk),
        out_shape=jax.ShapeDtypeStruct((nq, r, c), BF16),
        compiler_params=_params(("parallel", "parallel")),
    )(c_idx, mine2, other)


def _sum_chips(parts, *, name):
    _, r, c = parts.shape
    br = _reduce_rows(r, c)

    def body(p_ref, o_ref):
        total = p_ref[0].astype(F32)
        for q in range(1, 4):
            total = total + p_ref[q].astype(F32)
        o_ref[...] = total

    return pl.pallas_call(
        body, name=name, grid=(r // br,),
        in_specs=[pl.BlockSpec((4, br, c), lambda i: (0, i, 0))],
        out_specs=pl.BlockSpec((br, c), lambda i: (i, 0)),
        out_shape=jax.ShapeDtypeStruct((r, c), F32),
        compiler_params=_params(("parallel",)),
    )(parts)


BIG = ("w_in", "w_out_sb", "w_out_ssd", "w_out_rw", "w_o")
BIG_AXIS = {"w_in": 2, "w_out_sb": 2, "w_out_ssd": 1, "w_out_rw": 2, "w_o": 1}
SMALL_SHARDED = {"conv_w": 320, "rw_w_up": 128, "rw_a_up": 128}
SMALL = ("norm_g", "conv_w", "conv_b", "dt_bias", "a_log", "d_skip", "ssd_norm_g", "rw_mu", "rw_w0", "rw_w_up",
         "rw_a0", "rw_a_up", "rw_k_k", "rw_k_a", "rw_r_k", "rw_ln_g", "rw_ln_b", "final_g")


def _rows_of(a):
    flat = a.reshape(-1)
    pad = (-flat.shape[0]) % LANES
    return jnp.pad(flat, (0, pad)).reshape(-1, LANES)


def _pack_rows(arrays, multiple=8):
    rows = jnp.concatenate([_rows_of(a) for a in arrays], axis=0)
    pad = (-rows.shape[0]) % multiple
    return jnp.pad(rows, ((0, pad), (0, 0)))


def _unpack_rows(rows, shapes):
    out, off = [], 0
    for shp in shapes:
        n = 1
        for d in shp:
            n *= d
        nr = -(-n // LANES)
        out.append(rows[off:off + nr].reshape(-1)[:n].reshape(shp))
        off += nr
    return out


COL_MAP = ((0, 3072, 0), (3072, 4352, C_XBC), (4352, 4368, C_DT), (4368, 6544, C_RW), (6544, 9616, C_GATES))
SHARD_COLS = N_IN // 4


def _w_in_from_shards(shards):
    pieces = []
    for a, b, dst in sorted(COL_MAP, key=lambda m: m[2]):
        if pieces and dst > pieces[-1][0]:
            pieces.append((dst, jnp.zeros((shards[0].shape[0], dst - pieces[-1][0]), shards[0].dtype)))
        for q in range(4):
            lo, hi = max(a, q * SHARD_COLS), min(b, (q + 1) * SHARD_COLS)
            if lo < hi:
                pieces.append((dst + hi - a, shards[q][:, lo - q * SHARD_COLS:hi - q * SHARD_COLS]))
    return jnp.concatenate([p for _, p in pieces], axis=1)


def _w_in_shard(g, q):
    pieces = []
    for a, b, dst in COL_MAP:
        lo, hi = max(a, q * SHARD_COLS), min(b, (q + 1) * SHARD_COLS)
        if lo < hi:
            pieces.append(g[:, dst + lo - a:dst + hi - a])
    return jnp.concatenate(pieces, axis=1)


def _row_halves(a):
    return a.reshape(2, a.shape[0] // 2, a.shape[1])


def _join_halves(core, mine, theirs):
    return jnp.where(core == 0, jnp.concatenate([mine, theirs], axis=-2), jnp.concatenate([theirs, mine], axis=-2))


def kernel(x, norm_g, w_in, conv_w, conv_b, dt_bias, a_log, d_skip, ssd_norm_g, rw_mu, rw_w0, rw_w_up, rw_a0, rw_a_up, rw_k_k, rw_k_a, rw_r_k, rw_ln_g, rw_ln_b, w_out_sb, w_out_ssd, w_out_rw, w_o, final_g, loss_target, m_norm_g, m_w_in, m_conv_w, m_conv_b, m_dt_bias, m_a_log, m_d_skip, m_ssd_norm_g, m_rw_mu, m_rw_w0, m_rw_w_up, m_rw_a0, m_rw_a_up, m_rw_k_k, m_rw_k_a, m_rw_r_k, m_rw_ln_g, m_rw_ln_b, m_w_out_sb, m_w_out_ssd, m_w_out_rw, m_w_o, m_final_g, v_norm_g, v_w_in, v_conv_w, v_conv_b, v_dt_bias, v_a_log, v_d_skip, v_ssd_norm_g, v_rw_mu, v_rw_w0, v_rw_w_up, v_rw_a0, v_rw_a_up, v_rw_k_k, v_rw_k_a, v_rw_r_k, v_rw_ln_g, v_rw_ln_b, v_w_out_sb, v_w_out_ssd, v_w_out_rw, v_w_o, v_final_g):
    names = ("norm_g", "w_in", "conv_w", "conv_b", "dt_bias", "a_log", "d_skip", "ssd_norm_g", "rw_mu", "rw_w0",
             "rw_w_up", "rw_a0", "rw_a_up", "rw_k_k", "rw_k_a", "rw_r_k", "rw_ln_g", "rw_ln_b", "w_out_sb",
             "w_out_ssd", "w_out_rw", "w_o", "final_g")
    w_loc = dict(zip(names, (norm_g, w_in, conv_w, conv_b, dt_bias, a_log, d_skip, ssd_norm_g, rw_mu, rw_w0, rw_w_up,
                             rw_a0, rw_a_up, rw_k_k, rw_k_a, rw_r_k, rw_ln_g, rw_ln_b, w_out_sb, w_out_ssd, w_out_rw,
                             w_o, final_g)))
    m_loc = dict(zip(names, (m_norm_g, m_w_in, m_conv_w, m_conv_b, m_dt_bias, m_a_log, m_d_skip, m_ssd_norm_g,
                             m_rw_mu, m_rw_w0, m_rw_w_up, m_rw_a0, m_rw_a_up, m_rw_k_k, m_rw_k_a, m_rw_r_k,
                             m_rw_ln_g, m_rw_ln_b, m_w_out_sb, m_w_out_ssd, m_w_out_rw, m_w_o, m_final_g)))
    v_loc = dict(zip(names, (v_norm_g, v_w_in, v_conv_w, v_conv_b, v_dt_bias, v_a_log, v_d_skip, v_ssd_norm_g,
                             v_rw_mu, v_rw_w0, v_rw_w_up, v_rw_a0, v_rw_a_up, v_rw_k_k, v_rw_k_a, v_rw_r_k,
                             v_rw_ln_g, v_rw_ln_b, v_w_out_sb, v_w_out_ssd, v_w_out_rw, v_w_o, v_final_g)))
    chip = 2 * lax.axis_index("x") + lax.axis_index("y")
    core = lax.axis_index("c")

    as_sent = [w_loc[n].astype(BF16).reshape(DEPTH, 2, w_loc[n].shape[1] // 2, w_loc[n].shape[2]) for n in BIG]

    def gathered(mine, nm):
        theirs = _sibling_swap(mine, other_slot=False, name=nm)
        out = {}
        for n, a, b in zip(BIG, mine, theirs):
            shards = _join_halves(core, a, b)
            out[n] = (_w_in_from_shards([shards[q] for q in range(4)]) if n == "w_in"
                      else jnp.concatenate([shards[q] for q in range(4)], axis=BIG_AXIS[n] - 1))
        return out

    full = {}
    sm_names = tuple(SMALL_SHARDED)
    sm_shapes = [w_loc[n].shape for n in sm_names]
    (got_sm,) = _allgather_small(_pack_rows([w_loc[n] for n in sm_names]), reduce=False, name="gather_small")
    per_chip = [_unpack_rows(got_sm[4 * (q // 2) + 2 * (q % 2)], sm_shapes) for q in range(4)]
    for i, n in enumerate(sm_names):
        full[n] = jnp.concatenate([per_chip[q][i] for q in range(4)], axis=-1)

    def pad16(a):
        return jnp.zeros((1, LANES), F32).at[0, :SSD_HEADS].set(a)

    def layer_params(i, big):
        row = lambda n: w_loc[n][i].reshape(1, -1)
        cw = full["conv_w"][i]
        return dict(
            norm_g=row("norm_g"), w_in=big["w_in"], conv=[cw[k][None] for k in range(4)] + [row("conv_b")],
            dt_bias=pad16(dt_bias[i]), a_log=pad16(a_log[i]), d_skip=pad16(d_skip[i]),
            ssd_norm_g=row("ssd_norm_g"), rw_mu=row("rw_mu"),
            rw_pre=[row("rw_w0"), jnp.zeros((LANES, 512), F32).at[:HEAD].set(full["rw_w_up"][i]), row("rw_a0"),
                    jnp.zeros((LANES, 512), F32).at[HEAD:].set(full["rw_a_up"][i]), row("rw_k_k"), row("rw_k_a")],
            rw_post=[row("rw_ln_g"), row("rw_ln_b"), row("rw_r_k")],
            w_out_sb=big["w_out_sb"], w_out_ssd=big["w_out_ssd"], w_out_rw=big["w_out_rw"], w_o=big["w_o"])

    c_idx = core.reshape(1).astype(jnp.int32)

    def reduce_prepare(items, nm):
        sends = []
        for g, n, _ in items:
            per_chip = ([_w_in_shard(g[n], q) for q in range(4)] if n == "w_in"
                        else jnp.split(g[n], 4, axis=BIG_AXIS[n] - 1))
            sends.append(jnp.stack([_row_halves(p) for p in per_chip], axis=1))
        others = _sibling_swap(sends, other_slot=True, name=nm + "sibling")
        return [_add_halves(s, o, c_idx, name=nm + "add_" + lab) for (_, _, lab), s, o in zip(items, sends, others)]

    def reduce_finish(exchanged, labels, nm):
        mine = [_sum_chips(p, name=nm + "sum_" + lab) for lab, p in zip(labels, exchanged)]
        theirs = _sibling_swap(mine, other_slot=False, name=nm + "join")
        return {lab: _join_halves(core, a, b) for lab, a, b in zip(labels, mine, theirs)}

    assert DEPTH == 2
    out_proj = BIG[1:]
    params, xs, saved, grads = [None] * 2, [x[0], None, None], [None] * 2, [None] * 2
    params[0] = layer_params(0, gathered(_chip_exchange(as_sent, per_dest=0, name="gather_l0"), "gather_l0_join"))
    xs[1], saved[0], got = _layer_fwd(xs[0], params[0], "l0_", side=(as_sent, 1))
    params[1] = layer_params(1, gathered(got, "gather_l1_join"))
    xs[2], saved[1], _ = _layer_fwd(xs[1], params[1], "l1_")
    dx, loss_row, g_final = _final(xs[2], final_g.reshape(1, -1), loss_target[0], bt=BT, name="final")
    dx, grads[1], _, _ = _layer_bwd(xs[1], dx, params[1], saved[1], "l1_")
    early = lambda g: [(grads[1], n, "l1_" + n) for n in BIG] + [(g, n, "l0_" + n) for n in out_proj]
    dx, grads[0], got, got_late = _layer_bwd(
        xs[0], dx, params[0], saved[0], "l0_",
        side=lambda g: (reduce_prepare(early(g), "reduce_early_"), True),
        side_late=lambda g: (reduce_prepare([(g, "w_in", "l0_w_in")], "reduce_late_"), True))
    total = reduce_finish(got + got_late, [lab for _, _, lab in early(None)] + ["l0_w_in"], "reduce_")
    totals = [{n: total[f"l{i}_" + n] for n in BIG} for i in range(DEPTH)]

    def stacked(fn):
        return jnp.stack([fn(grads[i]) for i in range(DEPTH)])

    g_loc = {
        "norm_g": stacked(lambda g: g["norm_g"][0]),
        "conv_w": stacked(lambda g: jnp.concatenate(g["conv"][:4], axis=0)),
        "conv_b": stacked(lambda g: g["conv"][4][0]),
        "dt_bias": stacked(lambda g: g["dt_bias"][0, :SSD_HEADS]),
        "a_log": stacked(lambda g: g["a_log"][0, :SSD_HEADS]),
        "d_skip": stacked(lambda g: g["d_skip"][0, :SSD_HEADS]),
        "ssd_norm_g": stacked(lambda g: g["ssd_norm_g"][0]),
        "rw_mu": stacked(lambda g: g["rw_mu"][0]),
        "rw_w0": stacked(lambda g: g["rw_pre"][0][0]),
        "rw_w_up": stacked(lambda g: g["rw_pre"][1][:HEAD]),
        "rw_a0": stacked(lambda g: g["rw_pre"][2][0]),
        "rw_a_up": stacked(lambda g: g["rw_pre"][3][HEAD:]),
        "rw_k_k": stacked(lambda g: g["rw_pre"][4][0]),
        "rw_k_a": stacked(lambda g: g["rw_pre"][5][0]),
        "rw_r_k": stacked(lambda g: g["rw_r_k"].reshape(8, HEAD)),
        "rw_ln_g": stacked(lambda g: g["rw_ln_g"][0]),
        "rw_ln_b": stacked(lambda g: g["rw_ln_b"][0]),
        "final_g": g_final[0],
    }

    g_out = {n: jnp.stack([totals[0][n], totals[1][n]]) for n in BIG}

    sm_all = SMALL + ("loss",)
    sm_full_shapes = [g_loc[n].shape for n in SMALL] + [(1,)]
    _, summed = _allgather_small(_pack_rows([g_loc[n] for n in SMALL] + [loss_row[0, :1]]), reduce=True, name="reduce_small")
    sm = dict(zip(sm_all, _unpack_rows(summed, sm_full_shapes)))
    for n in SMALL:
        g_out[n] = sm[n]
    for n, wd in SMALL_SHARDED.items():
        g_out[n] = lax.dynamic_slice_in_dim(sm[n], chip * wd, wd, axis=sm[n].ndim - 1)
    loss = sm["loss"][0]

    upd = {n: _adamw(w_loc[n], g_out[n], m_loc[n], v_loc[n], name="adamw_" + n) for n in names if n != "w_in"}
    cols = SHARD_COLS // 4
    to_cols = lambda a: jnp.transpose(a, (2, 0, 1)).reshape(4, cols, DEPTH, D_MODEL)
    from_cols = lambda a: jnp.transpose(a.reshape(SHARD_COLS, DEPTH, D_MODEL), (1, 2, 0))
    g_cols = lax.optimization_barrier(to_cols(g_out["w_in"]))
    g_out["w_in"] = from_cols(g_cols)
    upd["w_in"] = tuple(from_cols(a) for a in _adamw(
        to_cols(w_loc["w_in"]), g_cols, to_cols(m_loc["w_in"]), to_cols(v_loc["w_in"]),
        name="adamw_w_in", block=(1, cols, DEPTH, D_MODEL // 2)))
    return (loss, dx[None], *[g_out[n] for n in names], *[upd[n][0] for n in names],
            *[upd[n][1] for n in names], *[upd[n][2] for n in names])
```

```python
import functools

import jax
import jax.numpy as jnp
from jax import lax
from jax.experimental import pallas as pl
from jax.experimental.pallas import tpu as pltpu

F32 = jnp.float32
BF16 = jnp.bfloat16

D_MODEL = 1024
DEPTH = 2
HEAD = 64
LANES = 128
CHUNK = 128
RMS_EPS = 1e-6
GN_EPS = 64e-5
VMEM_LIMIT = 56 * 1024 * 1024

N_IN = 9616
N_PAD = 9728
C_SB, C_Z, C_GATES, C_RW, C_LO, C_DT, C_XBC = 0, 2048, 3072, 6144, 8192, 8320, 8448
RW_COLS = 2176
XBC_COLS = 1280

ADAM_LR, ADAM_B1, ADAM_B2, ADAM_EPS, ADAM_WD, ADAM_STEP = 0.001, 0.9, 0.999, 1e-08, 0.01, 10


def _params(sem=None):
    return pltpu.CompilerParams(dimension_semantics=sem, vmem_limit_bytes=VMEM_LIMIT)


@jax.custom_vjp
def _sigmoid(x):
    return 1.0 / (1.0 + jnp.exp(-x))


def _sigmoid_fwd(x):
    s = _sigmoid(x)
    return s, s


def _sigmoid_bwd(s, g):
    return (g * s * (1.0 - s),)


_sigmoid.defvjp(_sigmoid_fwd, _sigmoid_bwd)


@jax.custom_vjp
def _silu(x):
    return x * _sigmoid(x)


def _silu_fwd(x):
    s = _sigmoid(x)
    return x * s, (x, s)


def _silu_bwd(res, g):
    x, s = res
    return (g * (s + x * s * (1.0 - s)),)


_silu.defvjp(_silu_fwd, _silu_bwd)


@jax.custom_vjp
def _softplus(x):
    return jnp.maximum(x, 0.0) + jnp.log(1.0 + jnp.exp(-jnp.abs(x)))


def _softplus_fwd(x):
    return _softplus(x), x


def _softplus_bwd(x, g):
    return (g * _sigmoid(x),)


_softplus.defvjp(_softplus_fwd, _softplus_bwd)


def _dot(a, b, dims):
    return lax.dot_general(a.astype(BF16), b.astype(BF16), (dims, ((), ())), preferred_element_type=F32)


def _dot_nn(a, b):
    return _dot(a, b, ((1,), (0,)))


def _dot_nt(a, b):
    return _dot(a, b, ((1,), (1,)))


def _dot_tn(a, b):
    return _dot(a, b, ((0,), (0,)))


@jax.custom_vjp
def _bdot(a, b):
    return _dot_nn(a, b)


def _bdot_fwd(a, b):
    return _dot_nn(a, b), (a, b)


def _bdot_bwd(res, g):
    a, b = res
    return _dot_nt(g, b), _dot_tn(a, g)


_bdot.defvjp(_bdot_fwd, _bdot_bwd)


def _split2(x):
    hi = x.astype(BF16)
    lo = (x - hi.astype(F32)).astype(BF16)
    return hi, lo


_NT = (((1,), (1,)), ((), ()))
_NN = (((1,), (0,)), ((), ()))
_TN = (((0,), (0,)), ((), ()))


def _dot2(x, m, dn=_NN):
    hi, lo = _split2(x)
    return (lax.dot_general(hi, m, dn, preferred_element_type=F32)
            + lax.dot_general(lo, m, dn, preferred_element_type=F32))


def _seg_matrix(n):
    r = lax.broadcasted_iota(jnp.int32, (n, n), 0) // HEAD
    c = lax.broadcasted_iota(jnp.int32, (n, n), 1) // HEAD
    return (r == c).astype(BF16)


@jax.custom_vjp
def _segsum2(x, seg):
    return _dot2(x, seg)


def _segsum2_fwd(x, seg):
    return _dot2(x, seg), seg


def _segsum2_bwd(seg, g):
    return _dot2(g, seg), jnp.zeros_like(seg)


_segsum2.defvjp(_segsum2_fwd, _segsum2_bwd)


def _make_segsum(seg):
    return lambda x: _segsum2(x, seg)


def _shift_down_raw(x, k):
    row = lax.broadcasted_iota(jnp.int32, x.shape, 0)
    return jnp.where(row >= k, pltpu.roll(x, k, 0), 0.0)


def _shift_up_raw(x, k):
    t = x.shape[0]
    row = lax.broadcasted_iota(jnp.int32, x.shape, 0)
    return jnp.where(row < t - k, pltpu.roll(x, t - k, 0), 0.0)


@functools.partial(jax.custom_vjp, nondiff_argnums=(1,))
def _shift_down(x, k):
    return _shift_down_raw(x, k)


def _shift_down_fwd(x, k):
    return _shift_down_raw(x, k), None


def _shift_down_bwd(k, _, g):
    return (_shift_up_raw(g, k),)


_shift_down.defvjp(_shift_down_fwd, _shift_down_bwd)


def _mm(a, b, *, name, ta=False, tb=False, add=None, out_dtype=F32, tm=2048, tn=512, tk=None, side=None):
    m, k = (a.shape[1], a.shape[0]) if ta else a.shape
    n = b.shape[0] if tb else b.shape[1]
    tm, tn = min(tm, m), min(tn, n)
    tk = k if tk is None else tk
    nk = k // tk
    assert m % tm == 0 and n % tn == 0 and k % tk == 0
    dims = ((0 if ta else 1,), (1 if tb else 0,))

    def body(a_ref, b_ref, *refs):
        o_ref, acc_ref = refs[-2:]
        p = _dot(a_ref[...], b_ref[...], dims)

        def emit(total):
            if add is not None:
                total = total + refs[0][...]
            o_ref[...] = total.astype(o_ref.dtype)

        if nk == 1:
            emit(p)
        else:
            kk = pl.program_id(2)

            @pl.when(kk == 0)
            def _():
                acc_ref[...] = p

            @pl.when(kk > 0)
            def _():
                acc_ref[...] += p

            @pl.when(kk == nk - 1)
            def _():
                emit(acc_ref[...])

    a_spec = pl.BlockSpec((tk, tm), lambda i, j, kk: (kk, i)) if ta else pl.BlockSpec((tm, tk), lambda i, j, kk: (i, kk))
    b_spec = pl.BlockSpec((tn, tk), lambda i, j, kk: (j, kk)) if tb else pl.BlockSpec((tk, tn), lambda i, j, kk: (kk, j))
    o_spec = pl.BlockSpec((tm, tn), lambda i, j, kk: (i, j))
    res = _call_with_side(
        body, side, name=name, grid=(m // tm, n // tn, nk), semantics=("parallel", "parallel", "arbitrary"),
        in_specs=[a_spec, b_spec] + ([o_spec] if add is not None else []), out_specs=[o_spec],
        out_shape=[jax.ShapeDtypeStruct((m, n), out_dtype)],
        scratch_shapes=[pltpu.VMEM((tm, tn) if nk > 1 else (8, LANES), F32)],
        operands=(a, b) + ((add,) if add is not None else ()))
    return res[0] if side is None else (res[0], res[1])


def _row_specs(rows, bt):
    return [pl.BlockSpec((bt, w), functools.partial(lambda i, c: (i, c), c=c)) for _, w, c in rows]


def _full_spec(p):
    return pl.BlockSpec(p.shape, functools.partial(lambda i, nd: (0,) * nd, nd=p.ndim))


def _rowwise(f, rows, pars, out_widths, *, bt, name, acc_widths=()):
    t = rows[0][0].shape[0]
    nr, npar, no, na = len(rows), len(pars), len(out_widths), len(acc_widths)

    def body(*refs):
        vals = [r[...] for r in refs[:nr + npar]]
        outs = f(*vals)
        for o_ref, o in zip(refs[nr + npar:nr + npar + no], outs[:no]):
            o_ref[...] = o.astype(o_ref.dtype)
        if na:
            first = pl.program_id(0) == 0
            for a_ref, a in zip(refs[nr + npar + no:], outs[no:]):
                @pl.when(first)
                def _():
                    a_ref[...] = jnp.zeros_like(a_ref)
                a_ref[...] += a

    return pl.pallas_call(
        body, name=name, grid=(t // bt,),
        in_specs=_row_specs(rows, bt) + [_full_spec(p) for p in pars],
        out_specs=[pl.BlockSpec((bt, w), lambda i: (i, 0)) for w in out_widths]
        + [pl.BlockSpec((1, w), lambda i: (0, 0)) for w in acc_widths],
        out_shape=[jax.ShapeDtypeStruct((t, w), F32) for w in out_widths]
        + [jax.ShapeDtypeStruct((1, w), F32) for w in acc_widths],
        compiler_params=_params(("arbitrary",)),
    )(*[r[0] for r in rows], *pars)


def _rowwise_bwd(f, rows, pars, douts, *, bt, name, groups=None):
    t = rows[0][0].shape[0]
    nr, npar, nd = len(rows), len(pars), len(douts)
    groups = [[i] for i in range(nr)] if groups is None else groups
    widths = [r[1] for r in rows]

    def body(*refs):
        vals = [r[...] for r in refs[:nr + npar]]
        cts = tuple(r[...] for r in refs[nr + npar:nr + npar + nd])
        _, vjp = jax.vjp(lambda *a: tuple(f(*a)), *vals)
        grads = vjp(cts)
        out_refs = refs[nr + npar + nd:]
        for g_ref, grp in zip(out_refs[:len(groups)], groups):
            off = 0
            for i in grp:
                g_ref[:, off:off + widths[i]] = grads[i]
                off += widths[i]
        first = pl.program_id(0) == 0
        for p_ref, g in zip(out_refs[len(groups):], grads[nr:]):
            @pl.when(first)
            def _():
                p_ref[...] = jnp.zeros_like(p_ref)
            p_ref[...] += g

    gw = [sum(widths[i] for i in grp) for grp in groups]
    return pl.pallas_call(
        body, name=name, grid=(t // bt,),
        in_specs=_row_specs(rows, bt) + [_full_spec(p) for p in pars] + _row_specs(douts, bt),
        out_specs=[pl.BlockSpec((bt, w), lambda i: (i, 0)) for w in gw] + [_full_spec(p) for p in pars],
        out_shape=[jax.ShapeDtypeStruct((t, w), F32) for w in gw] + [jax.ShapeDtypeStruct(p.shape, F32) for p in pars],
        compiler_params=_params(("arbitrary",)),
    )(*[r[0] for r in rows], *pars, *[d[0] for d in douts])


def _colwise(f, x, c0, ncols, pars, *, bc, name):
    t = x.shape[0]

    def body(x_ref, *refs):
        o_ref = refs[-1]
        o_ref[...] = f(x_ref[...], *[r[...] for r in refs[:-1]])

    return pl.pallas_call(
        body, name=name, grid=(ncols // bc,),
        in_specs=[pl.BlockSpec((t, bc), lambda j: (0, j + c0 // bc))]
        + [pl.BlockSpec((p.shape[0], bc), lambda j: (0, j)) for p in pars],
        out_specs=pl.BlockSpec((t, bc), lambda j: (0, j)),
        out_shape=jax.ShapeDtypeStruct((t, ncols), F32),
        compiler_params=_params(("parallel",)),
    )(x, *pars)


def _colwise_bwd(f, x, c0, ncols, pars, dout, *, bc, name):
    t = x.shape[0]
    npar = len(pars)

    def body(x_ref, *refs):
        vals = [x_ref[...]] + [r[...] for r in refs[:npar]]
        _, vjp = jax.vjp(f, *vals)
        grads = vjp(refs[npar][...])
        for g_ref, g in zip(refs[npar + 1:], grads):
            g_ref[...] = g

    return pl.pallas_call(
        body, name=name, grid=(ncols // bc,),
        in_specs=[pl.BlockSpec((t, bc), lambda j: (0, j + c0 // bc))]
        + [pl.BlockSpec((p.shape[0], bc), lambda j: (0, j)) for p in pars]
        + [pl.BlockSpec((t, bc), lambda j: (0, j))],
        out_specs=[pl.BlockSpec((t, bc), lambda j: (0, j))]
        + [pl.BlockSpec((p.shape[0], bc), lambda j: (0, j)) for p in pars],
        out_shape=[jax.ShapeDtypeStruct((t, ncols), F32)] + [jax.ShapeDtypeStruct(p.shape, F32) for p in pars],
        compiler_params=_params(("parallel",)),
    )(x, *pars, dout)


def _f_rms(x, g):
    return (x * lax.rsqrt(jnp.mean(x * x, axis=-1, keepdims=True) + RMS_EPS) * g,)


def _f_sb_gate(y, gate):
    return (y * _silu(gate),)


def _f_ssd_norm(y, z, g):
    u = y * _silu(z)
    return (u * lax.rsqrt(jnp.mean(u * u, axis=-1, keepdims=True) + RMS_EPS) * g,)


def _f_merge(p_sb, p_ssd, p_rw, g_sb, g_ssd, g_rw):
    return (_sigmoid(g_sb) * p_sb + _sigmoid(g_ssd) * p_ssd + _sigmoid(g_rw) * p_rw,)


def _f_rw_pre(k, lo, w0, w_up, a0, a_up, k_k, k_a):
    segsum = _make_segsum(_seg_matrix(k.shape[1]))
    lane = lax.broadcasted_iota(jnp.int32, lo.shape, 1)
    w_lo = jnp.where(lane < HEAD, jnp.tanh(lo), 0.0)
    a_lo = jnp.where(lane >= HEAD, lo, 0.0)
    w = -_softplus(-(w0 + _bdot(w_lo, w_up))) - 0.5
    log_decay = -jnp.exp(w)
    a = _sigmoid(a0 + _bdot(a_lo, a_up))
    kk = k * k_k
    kk = kk / jnp.maximum(jnp.sqrt(segsum(kk * kk)), 1e-12)
    return log_decay, k * (1.0 + (a - 1.0) * k_a), -kk, kk * a


def _f_rw_post(y, r, k2, v, gate, ln_g, ln_b, r_k):
    segsum = _make_segsum(_seg_matrix(y.shape[1]))
    yc = y - segsum(y) * (1.0 / HEAD)
    var = segsum(yc * yc) * (1.0 / HEAD)
    yn = yc * lax.rsqrt(var + GN_EPS) * ln_g + ln_b
    return ((yn + segsum(r * k2 * r_k) * v) * _silu(gate),)


def _f_rw_mix(slab, mu):
    return slab + (_shift_down(slab, 1) - slab) * mu


def _f_conv(x, w0, w1, w2, w3, b):
    acc = x * w3 + b
    for i, w in enumerate((w0, w1, w2)):
        acc = acc + _shift_down(x, 3 - i) * w
    return _silu(acc)


def _log_sigmoid(z):
    return jnp.minimum(z, 0.0) - jnp.log(1.0 + jnp.exp(-jnp.abs(z)))


SB_BQ = 256
SB_BK = 256
assert SB_BQ == SB_BK


def _tri_ones(kind):
    j = lax.broadcasted_iota(jnp.int32, (SB_BK, SB_BK + LANES), 0)
    s = lax.broadcasted_iota(jnp.int32, (SB_BK, SB_BK + LANES), 1)
    tri = {"gt": j > s, "le": j <= s, "lt": j < s}[kind]
    return (tri | (s >= SB_BK)).astype(BF16)


def _sb_common(q_ref):
    lane = lax.broadcasted_iota(jnp.int32, (SB_BQ, LANES), 1)
    q = q_ref[...] * (HEAD ** -0.5)
    q2 = jnp.concatenate([jnp.where(lane < HEAD, q, 0.0), jnp.where(lane >= HEAD, q, 0.0)], axis=0).astype(BF16)
    diff = (lax.broadcasted_iota(jnp.int32, (2 * SB_BQ, SB_BK), 1)
            - (lax.broadcasted_iota(jnp.int32, (2 * SB_BQ, SB_BK), 0) & (SB_BQ - 1)))
    return lane, q2, diff


def _rep(x):
    return jnp.concatenate([x] * (SB_BK // LANES), axis=1)


def _sb2_specs(t):
    q = pl.BlockSpec((SB_BQ, LANES), lambda j, i: (i, j))
    k = pl.BlockSpec((t, LANES), lambda j, i: (0, 4 + j))
    v = pl.BlockSpec((t, LANES), lambda j, i: (0, 8 + j))
    return q, k, v


def _sb2_fwd(proj, *, name):
    t = proj.shape[0]

    def body(q_ref, k_ref, v_ref, y_ref, lt_ref):
        i = pl.program_id(1)
        lane, q2, diff = _sb_common(q_ref)
        m_f = _tri_ones("gt")

        def step(kb, carry, diagonal):
            c, acc = carry
            off = pl.multiple_of(kb * SB_BK, SB_BK)
            kblk = k_ref[pl.ds(off, SB_BK), :].astype(BF16)
            vblk = v_ref[pl.ds(off, SB_BK), :].astype(BF16)
            z = lax.dot_general(q2, kblk, _NT, preferred_element_type=F32)
            lb = _log_sigmoid(z)
            lk = jnp.where(diff < 0, lb - z, 0.0) if diagonal else lb - z
            w2 = _dot2(lk, m_f)
            att = jnp.exp(lb + _rep(c) + w2[:, :SB_BK])
            if diagonal:
                att = jnp.where(diff < 0, att, 0.0)
            acc = acc + lax.dot_general(att.astype(BF16), vblk, _NN, preferred_element_type=F32)
            return c + w2[:, SB_BK:], acc

        zero = jnp.zeros((2 * SB_BQ, LANES), F32)
        c, acc = lax.fori_loop(0, i, lambda it, carry: step(i - 1 - it, carry, False), step(i, (zero, zero), True))
        y_ref[...] = jnp.where(lane < HEAD, acc[:SB_BQ], acc[SB_BQ:])
        lt_ref[0] = c[:SB_BQ]
        lt_ref[1] = c[SB_BQ:]

    return pl.pallas_call(
        body, name=name, grid=(4, t // SB_BQ),
        in_specs=list(_sb2_specs(t)),
        out_specs=[pl.BlockSpec((SB_BQ, LANES), lambda j, i: (i, j)),
                   pl.BlockSpec((2, SB_BQ, LANES), lambda j, i: (j, i, 0))],
        out_shape=[jax.ShapeDtypeStruct((t, 4 * LANES), F32), jax.ShapeDtypeStruct((8, t, LANES), F32)],
        compiler_params=_params(("parallel", "arbitrary")),
    )(proj, proj, proj)


def _sb2_bwd(proj, dy, lt, *, name):
    t = proj.shape[0]

    def body(q_ref, k_ref, v_ref, dy_ref, lt_ref, dq_ref, dk_ref, dv_ref):
        i = pl.program_id(1)

        @pl.when(i == 0)
        def _():
            dk_ref[...] = jnp.zeros_like(dk_ref)
            dv_ref[...] = jnp.zeros_like(dv_ref)

        lane, q2, diff = _sb_common(q_ref)
        m_le, m_lt = _tri_ones("le"), _tri_ones("lt")
        dy_blk = dy_ref[...]
        do2 = jnp.concatenate([jnp.where(lane < HEAD, dy_blk, 0.0), jnp.where(lane >= HEAD, dy_blk, 0.0)],
                              axis=0).astype(BF16)
        lt2 = jnp.concatenate([lt_ref[0], lt_ref[1]], axis=0)

        def step(kb, carry, diagonal):
            cp, cg, dq = carry
            off = pl.multiple_of(kb * SB_BK, SB_BK)
            kblk = k_ref[pl.ds(off, SB_BK), :].astype(BF16)
            vblk = v_ref[pl.ds(off, SB_BK), :].astype(BF16)
            z = lax.dot_general(q2, kblk, _NT, preferred_element_type=F32)
            lb = _log_sigmoid(z)
            lk = jnp.where(diff < 0, lb - z, 0.0) if diagonal else lb - z
            w2 = _dot2(lk, m_le)
            att = jnp.exp(lb + _rep(lt2 - cp) - w2[:, :SB_BK])
            if diagonal:
                att = jnp.where(diff < 0, att, 0.0)
            d_e = lax.dot_general(do2, vblk, _NT, preferred_element_type=F32) * att
            g2 = _dot2(d_e, m_lt)
            sig = jnp.exp(lb)
            dz = d_e * (1.0 - sig) - (_rep(cg) + g2[:, :SB_BK]) * sig
            dz = (jnp.where(diff < 0, dz, 0.0) if diagonal else dz).astype(BF16)
            dq = dq + lax.dot_general(dz, kblk, _NN, preferred_element_type=F32)
            dk_ref[pl.ds(off, SB_BK), :] += lax.dot_general(dz, q2, _TN, preferred_element_type=F32)
            dv_ref[pl.ds(off, SB_BK), :] += lax.dot_general(att.astype(BF16), do2, _TN, preferred_element_type=F32)
            return cp + w2[:, SB_BK:], cg + g2[:, SB_BK:], dq

        zero = jnp.zeros((2 * SB_BQ, LANES), F32)
        before = lax.fori_loop(0, i, lambda kb, carry: step(kb, carry, False), (zero, zero, zero))
        _, _, dq = step(i, before, True)
        dq_ref[...] = jnp.where(lane < HEAD, dq[:SB_BQ], dq[SB_BQ:]) * (HEAD ** -0.5)

    q_spec, k_spec, v_spec = _sb2_specs(t)
    blk = pl.BlockSpec((SB_BQ, LANES), lambda j, i: (i, j))
    col = pl.BlockSpec((t, LANES), lambda j, i: (0, j))
    return pl.pallas_call(
        body, name=name, grid=(4, t // SB_BQ),
        in_specs=[q_spec, k_spec, v_spec, blk, pl.BlockSpec((2, SB_BQ, LANES), lambda j, i: (j, i, 0))],
        out_specs=[blk, col, col],
        out_shape=[jax.ShapeDtypeStruct((t, 4 * LANES), F32)] * 3,
        compiler_params=_params(("parallel", "arbitrary")),
    )(proj, proj, proj, dy, lt)


SSD_HEADS = 16
SSD_PAIRS = 8


def _split3(x):
    a = x.astype(BF16)
    r = x - a.astype(F32)
    b = r.astype(BF16)
    return a, b, (r - b.astype(F32)).astype(BF16)


def _dot3(x, m, dn=_NN):
    return sum(lax.dot_general(p, m, dn, preferred_element_type=F32) for p in _split3(x))


def _mdot3(m, x):
    return sum(lax.dot_general(m, p, _NN, preferred_element_type=F32) for p in _split3(x))


def _ssd_common(dtr, dtb, alog, acsx_s, acst_s):
    lane = lax.broadcasted_iota(jnp.int32, (CHUNK, LANES), 1)
    lane1 = lax.broadcasted_iota(jnp.int32, (1, LANES), 1)
    arow = jnp.where(lane1 < SSD_HEADS, -jnp.exp(alog), 0.0)
    dt = jnp.where(lane < SSD_HEADS, _softplus(dtr + dtb), 0.0)
    da = dt * arow
    r = lax.broadcasted_iota(jnp.int32, (CHUNK, CHUNK), 0)
    c = lax.broadcasted_iota(jnp.int32, (CHUNK, CHUNK), 1)
    tril = (r >= c).astype(BF16)
    triu = (r <= c).astype(BF16)
    acs = _mdot3(tril, da)
    acst_s[...] = _dot3(da, triu, _TN)
    eh = lax.broadcasted_iota(jnp.int32, (LANES, 8 * LANES), 0)
    e = (eh == lax.broadcasted_iota(jnp.int32, (LANES, 8 * LANES), 1) // HEAD).astype(BF16)
    eh2 = lax.broadcasted_iota(jnp.int32, (LANES, 16 * LANES), 0)
    e2 = (eh2 == lax.broadcasted_iota(jnp.int32, (LANES, 16 * LANES), 1) // LANES).astype(BF16)
    acsx_s[...] = _dot3(acs, e)
    return dt, arow, _dot3(dt, e), _dot3(acs, e2), e, tril, triu


def _ssd_fwd(xc, proj, dtb, alog, dsk, *, name):
    t = xc.shape[0]
    nc = t // CHUNK

    def body(x_ref, b_ref, c_ref, dtr_ref, dtb_ref, alog_ref, dsk_ref, y_ref, hin_ref, acsx_s, acst_s, h_s):
        @pl.when(pl.program_id(0) == 0)
        def _():
            h_s[...] = jnp.zeros_like(h_s)

        dt, arow, dt_x, acs_b, e, tril, _ = _ssd_common(dtr_ref[...], dtb_ref[...], alog_ref[...], acsx_s, acst_s)
        dsk_x = _dot3(jnp.broadcast_to(dsk_ref[...], (CHUNK, LANES)), e)
        lane = lax.broadcasted_iota(jnp.int32, (CHUNK, LANES), 1)
        causal = (lax.broadcasted_iota(jnp.int32, (CHUNK, CHUNK), 0)
                  >= lax.broadcasted_iota(jnp.int32, (CHUNK, CHUNK), 1))
        for j in range(SSD_PAIRS):
            g = j // 4
            sl = slice(j * LANES, (j + 1) * LANES)
            if j % 4 == 0:
                bg = jnp.where(lane // HEAD == g, b_ref[...], 0.0)
                cg = jnp.where(lane // HEAD == g, c_ref[...], 0.0)
                cb = _dot_nt(cg, bg)
            x = x_ref[:, sl]
            a = acsx_s[:, sl]
            at = acsx_s[CHUNK - 1:CHUNK, sl]
            xdt = x * dt_x[:, sl]
            hin = h_s[j]
            hin_ref[0, j] = hin
            y = jnp.exp(a) * _dot_nn(cg, hin) + x * dsk_x[:, sl]
            h_s[j] = jnp.exp(at) * hin + _dot_tn(bg, xdt * jnp.exp(at - a))
            yd = []
            for hh in (0, 1):
                h = 2 * j + hh
                dec = jnp.exp(jnp.minimum(acs_b[:, h * LANES:(h + 1) * LANES] - acst_s[pl.ds(h, 1), :], 0.0))
                yd.append(_dot_nn(jnp.where(causal, cb * dec, 0.0), xdt))
            y_ref[:, sl] = y + jnp.where(lane < HEAD, yd[0], yd[1])

    one = pl.BlockSpec((1, LANES), lambda i: (0, 0))
    return pl.pallas_call(
        body, name=name, grid=(nc,),
        in_specs=[pl.BlockSpec((CHUNK, 8 * LANES), lambda i: (i, 0)),
                  pl.BlockSpec((CHUNK, LANES), lambda i: (i, 8)),
                  pl.BlockSpec((CHUNK, LANES), lambda i: (i, 9)),
                  pl.BlockSpec((CHUNK, LANES), lambda i: (i, C_DT // LANES)), one, one, one],
        out_specs=[pl.BlockSpec((CHUNK, 8 * LANES), lambda i: (i, 0)),
                   pl.BlockSpec((1, SSD_PAIRS, LANES, LANES), lambda i: (i, 0, 0, 0))],
        out_shape=[jax.ShapeDtypeStruct((t, 8 * LANES), F32),
                   jax.ShapeDtypeStruct((nc, SSD_PAIRS, LANES, LANES), F32)],
        scratch_shapes=[pltpu.VMEM((CHUNK, 8 * LANES), F32), pltpu.VMEM((LANES, CHUNK), F32),
                        pltpu.VMEM((SSD_PAIRS, LANES, LANES), F32)],
        compiler_params=_params(("arbitrary",)),
    )(xc, xc, xc, proj, dtb, alog, dsk)


def _ssd_bwd(xc, proj, dtb, alog, dsk, hin_all, dy, *, name):
    t = xc.shape[0]
    nc = t // CHUNK

    def body(x_ref, b_ref, c_ref, dtr_ref, dtb_ref, alog_ref, dsk_ref, hin_ref, dy_ref,
             dxc_ref, ddtr_ref, ddtb_ref, dalog_ref, ddsk_ref, acsx_s, acst_s, dh_s, dax_s, ddx_s):
        @pl.when(pl.program_id(0) == 0)
        def _():
            dh_s[...] = jnp.zeros_like(dh_s)
            ddtb_ref[...] = jnp.zeros_like(ddtb_ref)
            dalog_ref[...] = jnp.zeros_like(dalog_ref)
            ddsk_ref[...] = jnp.zeros_like(ddsk_ref)

        dtr = dtr_ref[...]
        dtb = dtb_ref[...]
        dt, arow, dt_x, acs_b, e, tril, triu = _ssd_common(dtr, dtb, alog_ref[...], acsx_s, acst_s)
        dsk_x = _dot3(jnp.broadcast_to(dsk_ref[...], (CHUNK, LANES)), e)
        lane = lax.broadcasted_iota(jnp.int32, (CHUNK, LANES), 1)
        rowi = lax.broadcasted_iota(jnp.int32, (CHUNK, LANES), 0)
        causal = (lax.broadcasted_iota(jnp.int32, (CHUNK, CHUNK), 0)
                  >= lax.broadcasted_iota(jnp.int32, (CHUNK, CHUNK), 1))
        acs_rows = jnp.zeros((CHUNK, LANES), F32)
        acs_cols = jnp.zeros((LANES, CHUNK), F32)
        d_b = jnp.zeros((CHUNK, LANES), F32)
        d_c = jnp.zeros((CHUNK, LANES), F32)
        for j in range(SSD_PAIRS):
            g = j // 4
            sl = slice(j * LANES, (j + 1) * LANES)
            if j % 4 == 0:
                bg = jnp.where(lane // HEAD == g, b_ref[...], 0.0)
                cg = jnp.where(lane // HEAD == g, c_ref[...], 0.0)
                cb = _dot_nt(cg, bg)
                dcb = jnp.zeros((CHUNK, CHUNK), F32)
            x = x_ref[:, sl]
            d = dt_x[:, sl]
            a = acsx_s[:, sl]
            at = acsx_s[CHUNK - 1:CHUNK, sl]
            xdt = x * d
            hin = hin_ref[0, j]
            dhout = dh_s[j]
            dyp = dy_ref[:, sl]
            ea, eat, ed = jnp.exp(a), jnp.exp(at), jnp.exp(at - a)
            da_l = dyp * ea * _dot_nn(cg, hin)
            dm = dyp * ea
            d_c = d_c + _dot_nt(dm, hin)
            dh_s[j] = _dot_tn(cg, dm) + eat * dhout
            dat = jnp.sum(dhout * hin * eat, axis=0, keepdims=True)
            d_b = d_b + _dot_nt(xdt * ed, dhout)
            dw = _dot_nn(bg, dhout)
            dxdt = dw * ed
            ded = dw * xdt * ed
            dat = dat + jnp.sum(ded, axis=0, keepdims=True)
            da_l = da_l - ded
            for hh in (0, 1):
                h = 2 * j + hh
                dec = jnp.exp(jnp.minimum(acs_b[:, h * LANES:(h + 1) * LANES] - acst_s[pl.ds(h, 1), :], 0.0))
                gm = jnp.where(causal, cb * dec, 0.0)
                dyh = jnp.where(lane // HEAD == hh, dyp, 0.0)
                dg = _dot_nt(dyh, xdt)
                dxdt = dxdt + _dot_tn(gm, dyh)
                dcb = dcb + jnp.where(causal, dg * dec, 0.0)
                th = dg * gm
                acs_rows = acs_rows + jnp.where(lane == h, jnp.sum(th, axis=1, keepdims=True), 0.0)
                acs_cols = acs_cols + jnp.where(rowi == h, jnp.sum(th, axis=0, keepdims=True), 0.0)
            if j % 4 == 3:
                d_c = d_c + _dot_nn(dcb, bg)
                d_b = d_b + _dot_tn(dcb, cg)
            dxc_ref[:, sl] = dyp * dsk_x[:, sl] + dxdt * d
            ddx_s[:, sl] = dxdt * x
            dax_s[:, sl] = da_l + jnp.where(rowi == CHUNK - 1, dat, 0.0)
            dskp = jnp.sum(dyp * x, axis=0, keepdims=True)
            ddsk_ref[...] += _dot2(jnp.broadcast_to(dskp, (8, LANES)), e[:, sl], _NT)
        dxc_ref[:, 8 * LANES:9 * LANES] = d_b
        dxc_ref[:, 9 * LANES:10 * LANES] = d_c
        dacs = acs_rows - acs_cols.T + _dot2(dax_s[...], e, _NT)
        ddt = _dot2(ddx_s[...], e, _NT)
        dda = _mdot3(triu, dacs)
        ddt = ddt + dda * arow
        dalog_ref[...] += jnp.sum(dda * dt, axis=0, keepdims=True) * arow
        ddtr = jnp.where(lane < SSD_HEADS, ddt * _sigmoid(dtr + dtb), 0.0)
        ddtr_ref[...] = ddtr
        ddtb_ref[...] += jnp.sum(ddtr, axis=0, keepdims=True)

    one = pl.BlockSpec((1, LANES), lambda i: (0, 0))
    rev = lambda c: (lambda i: (nc - 1 - i, c))
    return pl.pallas_call(
        body, name=name, grid=(nc,),
        in_specs=[pl.BlockSpec((CHUNK, 8 * LANES), rev(0)), pl.BlockSpec((CHUNK, LANES), rev(8)),
                  pl.BlockSpec((CHUNK, LANES), rev(9)), pl.BlockSpec((CHUNK, LANES), rev(C_DT // LANES)),
                  one, one, one,
                  pl.BlockSpec((1, SSD_PAIRS, LANES, LANES), lambda i: (nc - 1 - i, 0, 0, 0)),
                  pl.BlockSpec((CHUNK, 8 * LANES), rev(0))],
        out_specs=[pl.BlockSpec((CHUNK, XBC_COLS), rev(0)), pl.BlockSpec((CHUNK, LANES), rev(0)), one, one,
                   pl.BlockSpec((8, LANES), lambda i: (0, 0))],
        out_shape=[jax.ShapeDtypeStruct((t, XBC_COLS), F32), jax.ShapeDtypeStruct((t, LANES), F32)]
        + [jax.ShapeDtypeStruct((1, LANES), F32)] * 2 + [jax.ShapeDtypeStruct((8, LANES), F32)],
        scratch_shapes=[pltpu.VMEM((CHUNK, 8 * LANES), F32), pltpu.VMEM((LANES, CHUNK), F32),
                        pltpu.VMEM((SSD_PAIRS, LANES, LANES), F32),
                        pltpu.VMEM((CHUNK, 8 * LANES), F32), pltpu.VMEM((CHUNK, 8 * LANES), F32)],
        compiler_params=_params(("arbitrary",)),
    )(xc, xc, xc, proj, dtb, alog, dsk, hin_all, dy)


RW_C = 64


def _p3(a, b, dn):
    ah, al = _split2(a)
    bh, bl = _split2(b)
    d = lambda x, y: lax.dot_general(x, y, dn, preferred_element_type=F32)
    return d(ah, bh) + d(ah, bl) + d(al, bh)


_BNN = (((2,), (1,)), ((0,), (0,)))
_BNT = (((2,), (2,)), ((0,), (0,)))
_BTN = (((1,), (1,)), ((0,), (0,)))


@jax.custom_vjp
def _pnn(a, b):
    return _p3(a, b, _BNN)


@jax.custom_vjp
def _pnt(a, b):
    return _p3(a, b, _BNT)


@jax.custom_vjp
def _ptn(a, b):
    return _p3(a, b, _BTN)


_pnn.defvjp(lambda a, b: (_p3(a, b, _BNN), (a, b)), lambda res, g: (_p3(g, res[1], _BNT), _p3(res[0], g, _BTN)))
_pnt.defvjp(lambda a, b: (_p3(a, b, _BNT), (a, b)), lambda res, g: (_p3(g, res[1], _BNN), _p3(g, res[0], _BTN)))
_ptn.defvjp(lambda a, b: (_p3(a, b, _BTN), (a, b)), lambda res, g: (_p3(res[1], g, _BNT), _p3(res[0], g, _BNN)))


def _tri2(tril, x, dn):
    hi, lo = _split2(x)
    m = tril.astype(BF16)
    return (lax.dot_general(m, hi, dn, preferred_element_type=F32) + lax.dot_general(m, lo, dn, preferred_element_type=F32))


@jax.custom_vjp
def _cumsum_rows(tril, x):
    return _tri2(tril, x, _BNN)


_cumsum_rows.defvjp(lambda tril, x: (_tri2(tril, x, _BNN), tril),
                    lambda tril, g: (jnp.zeros_like(tril), _tri2(tril, g, _BTN)))


def _rw_chunk_consts():
    c2 = 2 * RW_C
    row = lax.broadcasted_iota(jnp.int32, (c2, c2), 0)
    col = lax.broadcasted_iota(jnp.int32, (c2, c2), 1)
    same = (row // RW_C) == (col // RW_C)
    strict = (same & (row > col)).astype(F32)
    incl = (same & (row >= col)).astype(F32)
    eye = (row == col).astype(F32)
    tr = lax.broadcasted_iota(jnp.int32, (RW_C, RW_C), 0)
    tc = lax.broadcasted_iota(jnp.int32, (RW_C, RW_C), 1)
    tril = (tr >= tc).astype(F32)
    lane = lax.broadcasted_iota(jnp.int32, (1, LANES), 1)
    hm = [(lane // HEAD == h).astype(F32) for h in (0, 1)]
    return strict, incl, eye, tril, hm


def _rw_chunk(r, lw, k, v, n, b, s2, consts):
    strict, incl, eye, tril, hm = consts
    two = lambda x: jnp.concatenate([x * hm[0], x * hm[1]], axis=1)
    cum = _cumsum_rows(jnp.broadcast_to(tril, (4, RW_C, RW_C)), lw)
    grow, shrink = jnp.exp(-cum), jnp.exp(cum)
    n2, r2 = two(n * jnp.exp(cum - lw)), two(r * shrink)
    b2, k2, v2 = two(b * grow), two(k * grow), two(v)
    p = _pnt(n2, b2) * strict
    x2 = _pnt(n2, s2) + _pnn(_pnt(n2, k2) * strict, v2)
    t_inv, a = eye + p, p
    for _ in range(RW_C.bit_length() - 2):
        a = _pnn(a, a)
        t_inv = t_inv + _pnn(t_inv, a)
    u2 = _pnn(t_inv, x2)
    y2 = _pnt(r2, s2) + _pnn(_pnt(r2, b2) * incl, u2) + _pnn(_pnt(r2, k2) * incl, v2)
    s2_new = (s2 + _ptn(u2, b2) + _ptn(v2, k2)) * jnp.exp(jnp.sum(lw, axis=1, keepdims=True))
    return jnp.sum(y2.reshape(4, 2, RW_C, LANES), axis=1), s2_new


def _pairs(ref):
    return jnp.stack([ref[:, p * LANES:(p + 1) * LANES] for p in range(4)])


def _rw_chunk_fwd(mixed, lw, k, n, b, *, name, side=None):
    t = lw.shape[0]
    nc = t // RW_C

    def body(r_ref, v_ref, lw_ref, k_ref, n_ref, b_ref, y_ref, sin_ref, s_s):
        @pl.when(pl.program_id(0) == 0)
        def _():
            s_s[...] = jnp.zeros_like(s_s)

        s2 = s_s[...]
        sin_ref[0] = s2
        y, s2 = _rw_chunk(*[_pairs(x) for x in (r_ref, lw_ref, k_ref, v_ref, n_ref, b_ref)], s2, _rw_chunk_consts())
        for p in range(4):
            y_ref[:, p * LANES:(p + 1) * LANES] = y[p]
        s_s[...] = s2

    blk = lambda c: pl.BlockSpec((RW_C, 4 * LANES), functools.partial(lambda i, c: (i, c), c=c))
    return _call_with_side(
        body, side, name=name, grid=(nc,), semantics=("arbitrary",),
        in_specs=[blk(0), blk(2), blk(0), blk(0), blk(0), blk(0)],
        out_specs=[blk(0), pl.BlockSpec((1, 4, LANES, LANES), lambda i: (i, 0, 0, 0))],
        out_shape=[jax.ShapeDtypeStruct((t, 4 * LANES), F32), jax.ShapeDtypeStruct((nc, 4, LANES, LANES), F32)],
        scratch_shapes=[pltpu.VMEM((4, LANES, LANES), F32)],
        operands=(mixed, mixed, lw, k, n, b))


def _call_with_side(body, side, *, name, grid, semantics, in_specs, out_specs, out_shape, scratch_shapes, operands):
    if side is None:
        return pl.pallas_call(body, name=name, grid=grid, in_specs=in_specs, out_specs=out_specs, out_shape=out_shape,
                              scratch_shapes=scratch_shapes, compiler_params=_params(semantics))(*operands)
    srcs, per_dest = side
    ns, ni, no, nscr = len(srcs), len(in_specs), len(out_specs), len(scratch_shapes)

    def full_body(*refs):
        ins, side_in = refs[:ni], refs[ni:ni + ns]
        outs, side_out = refs[ni + ns:ni + ns + no], refs[ni + ns + no:ni + 2 * ns + no]
        scratch, sems = refs[ni + 2 * ns + no:ni + 2 * ns + no + nscr], refs[ni + 2 * ns + no + nscr:]

        ids = [pl.program_id(a) for a in range(len(grid))]
        first = functools.reduce(jnp.logical_and, [i == 0 for i in ids])
        last = functools.reduce(jnp.logical_and, [i == n - 1 for i, n in zip(ids, grid)])

        @pl.when(first)
        def _():
            _exchange(side_in, side_out, sems, per_dest, start=True, wait=False)

        body(*ins, *outs, *scratch)

        @pl.when(last)
        def _():
            _exchange(side_in, side_out, sems, per_dest, start=False, wait=True)

    res = pl.pallas_call(
        full_body, name=name, grid=grid, in_specs=list(in_specs) + [_ANY] * ns,
        out_specs=list(out_specs) + [_ANY] * ns, out_shape=list(out_shape) + _exchange_out_shapes(srcs, per_dest),
        scratch_shapes=list(scratch_shapes) + _exchange_sems(ns), compiler_params=_params(("arbitrary",) * len(grid)),
    )(*operands, *srcs)
    return list(res[:no]) + [list(res[no:])]


def _rw_chunk_bwd(mixed, lw, k, n, b, s_in, dy, dr0, dk0, dv0, *, name, side=None):
    t = lw.shape[0]
    nc = t // RW_C

    def body(r_ref, v_ref, lw_ref, k_ref, n_ref, b_ref, sin_ref, dy_ref, dr0_ref, dk0_ref, dv0_ref,
             dr_ref, dlw_ref, dk_ref, dv_ref, dn_ref, db_ref, ds_s):
        @pl.when(pl.program_id(0) == 0)
        def _():
            ds_s[...] = jnp.zeros_like(ds_s)

        consts = _rw_chunk_consts()
        args = [_pairs(x) for x in (r_ref, lw_ref, k_ref, v_ref, n_ref, b_ref)] + [sin_ref[0]]
        _, vjp = jax.vjp(lambda *a: _rw_chunk(*a, consts), *args)
        dr, dlw, dk, dv, dn, db, ds = vjp((_pairs(dy_ref), ds_s[...]))
        for p in range(4):
            sl = slice(p * LANES, (p + 1) * LANES)
            dr_ref[:, sl] = dr[p] + dr0_ref[:, sl]
            dlw_ref[:, sl] = dlw[p]
            dk_ref[:, sl] = dk[p] + dk0_ref[:, sl]
            dv_ref[:, sl] = dv[p] + dv0_ref[:, sl]
            dn_ref[:, sl] = dn[p]
            db_ref[:, sl] = db[p]
        ds_s[...] = ds

    blk = lambda c: pl.BlockSpec((RW_C, 4 * LANES), functools.partial(lambda i, c: (nc - 1 - i, c), c=c))
    return _call_with_side(
        body, side, name=name, grid=(nc,), semantics=("arbitrary",),
        in_specs=[blk(0), blk(2), blk(0), blk(0), blk(0), blk(0),
                  pl.BlockSpec((1, 4, LANES, LANES), lambda i: (nc - 1 - i, 0, 0, 0)), blk(0), blk(0), blk(0), blk(0)],
        out_specs=[blk(0)] * 6,
        out_shape=[jax.ShapeDtypeStruct((t, 4 * LANES), F32)] * 6,
        scratch_shapes=[pltpu.VMEM((4, LANES, LANES), F32)],
        operands=(mixed, mixed, lw, k, n, b, s_in, dy, dr0, dk0, dv0))


def _f_rms_res(x, g):
    return _f_rms(x, g)[0], x


def _final(x, g, target, *, bt, name):
    t, d = x.shape

    def body(x_ref, g_ref, t_ref, dx_ref, loss_ref, dg_ref):
        tgt = t_ref[...]

        def f(xv, gv):
            err = _f_rms(xv, gv)[0] - tgt
            return 0.5 * jnp.mean(err * err, axis=-1, keepdims=True)

        row_loss, vjp = jax.vjp(f, x_ref[...], g_ref[...])
        dx, dg = vjp(jnp.ones_like(row_loss))
        dx_ref[...] = dx

        @pl.when(pl.program_id(0) == 0)
        def _():
            loss_ref[...] = jnp.zeros_like(loss_ref)
            dg_ref[...] = jnp.zeros_like(dg_ref)

        loss_ref[...] += jnp.broadcast_to(jnp.sum(row_loss, axis=0, keepdims=True), (1, LANES))
        dg_ref[...] += dg

    blk = pl.BlockSpec((bt, d), lambda i: (i, 0))
    return pl.pallas_call(
        body, name=name, grid=(t // bt,),
        in_specs=[blk, pl.BlockSpec((1, d), lambda i: (0, 0)), blk],
        out_specs=[blk, pl.BlockSpec((1, LANES), lambda i: (0, 0)), pl.BlockSpec((1, d), lambda i: (0, 0))],
        out_shape=[jax.ShapeDtypeStruct((t, d), F32), jax.ShapeDtypeStruct((1, LANES), F32),
                   jax.ShapeDtypeStruct((1, d), F32)],
        compiler_params=_params(("arbitrary",)),
    )(x, g, target)


ADAMW_BLOCK_BYTES = 1 << 20


def _adamw(w, g, m, v, *, name, block=None):
    shape = w.shape
    if block is not None:
        return _adamw_blocks(w, g, m, v, block, name)
    c = shape[-1]
    shape3 = (1,) * (3 - len(shape)) + shape if len(shape) <= 3 else (-1,) + shape[-2:]
    args = [a.reshape(shape3) for a in (w, g, m, v)]
    lead, r, _ = args[0].shape
    br = r
    if r * c * 4 > ADAMW_BLOCK_BYTES:
        cands = [b for b in range(8, r, 8) if r % b == 0 and b * c * 4 <= ADAMW_BLOCK_BYTES]
        br = max(cands) if cands else r
    outs = _adamw_blocks(*args, (1, br, c), name)
    return tuple(o.reshape(shape) for o in outs)


def _adamw_blocks(w, g, m, v, block, name):
    shape = w.shape
    assert all(s % b == 0 for s, b in zip(shape, block))

    def body(w_ref, g_ref, m_ref, v_ref, d_ref, nm_ref, nv_ref):
        gv = g_ref[...]
        m_new = ADAM_B1 * m_ref[...] + (1.0 - ADAM_B1) * gv
        v_new = ADAM_B2 * v_ref[...] + (1.0 - ADAM_B2) * (gv * gv)
        m_hat = m_new / (1.0 - ADAM_B1 ** ADAM_STEP)
        v_hat = v_new / (1.0 - ADAM_B2 ** ADAM_STEP)
        d_ref[...] = -ADAM_LR * (m_hat / (jnp.sqrt(v_hat) + ADAM_EPS) + ADAM_WD * w_ref[...])
        nm_ref[...] = m_new
        nv_ref[...] = v_new

    blk = pl.BlockSpec(tuple(block), lambda *ids: ids)
    return pl.pallas_call(
        body, name=name, grid=tuple(s // b for s, b in zip(shape, block)), in_specs=[blk] * 4, out_specs=[blk] * 3,
        out_shape=[jax.ShapeDtypeStruct(shape, F32)] * 3,
        compiler_params=_params(("parallel",) * len(shape)),
    )(w, g, m, v)


BT = 256
BC = 128


def _layer_rows(x, proj, s):
    s = {k: s.get(k) for k in ("y_sb_raw", "y_ssd_raw", "mixed", "ys", "k2", "p_sb", "p_ssd", "p_rw")}
    return dict(
        rms=[(x, D_MODEL, 0)],
        sb_gate=[(s["y_sb_raw"], 512, 0), (proj, 512, 3)],
        ssd_norm=[(s["y_ssd_raw"], 1024, 0), (proj, 1024, C_Z // 1024)],
        rw_pre=[(s["mixed"], 512, 1), (s["mixed"], LANES, 16)],
        rw_post=[(s["ys"], 512, 0), (s["mixed"], 512, 0), (s["k2"], 512, 0), (s["mixed"], 512, 2), (s["mixed"], 512, 3)],
        merge=[(s["p_sb"], 1024, 0), (s["p_ssd"], 1024, 0), (s["p_rw"], 1024, 0),
               (proj, 1024, 3), (proj, 1024, 4), (proj, 1024, 5)],
    )


def _layer_fwd(x, p, nm, side=None):
    s = {}
    (s["h"],) = _rowwise(_f_rms, [(x, D_MODEL, 0)], [p["norm_g"]], [D_MODEL], bt=BT, name=nm + "rms")
    proj = s["proj"] = _mm(s["h"], p["w_in"], name=nm + "proj")
    s["y_sb_raw"], s["lt"] = _sb2_fwd(proj, name=nm + "sb")
    s["xc"] = _colwise(_f_conv, proj, C_XBC, XBC_COLS, p["conv"], bc=BC, name=nm + "conv")
    s["y_ssd_raw"], s["hin"] = _ssd_fwd(s["xc"], proj, p["dt_bias"], p["a_log"], p["d_skip"], name=nm + "ssd")
    s["mixed"] = _colwise(_f_rw_mix, proj, C_RW, RW_COLS, [p["rw_mu"]], bc=BC, name=nm + "mix")
    s["w"], s["k2"], s["n"], s["b"] = _rowwise(_f_rw_pre, [(s["mixed"], 512, 1), (s["mixed"], LANES, 16)], p["rw_pre"],
                                               [512] * 4, bt=BT, name=nm + "rwpre")
    s["ys"], s["st"], *exchanged = _rw_chunk_fwd(s["mixed"], s["w"], s["k2"], s["n"], s["b"], name=nm + "scan", side=side)
    rows = _layer_rows(x, proj, s)
    (s["y_sb"],) = _rowwise(_f_sb_gate, rows["sb_gate"], [], [512], bt=BT, name=nm + "sbgate")
    (s["y_ssd"],) = _rowwise(_f_ssd_norm, rows["ssd_norm"], [p["ssd_norm_g"]], [1024], bt=BT, name=nm + "ssdnorm")
    (s["y_rw"],) = _rowwise(_f_rw_post, rows["rw_post"], p["rw_post"], [512], bt=BT, name=nm + "rwpost")
    s["p_sb"] = _mm(s["y_sb"], p["w_out_sb"], name=nm + "osb")
    s["p_ssd"] = _mm(s["y_ssd"], p["w_out_ssd"], name=nm + "ossd")
    s["p_rw"] = _mm(s["y_rw"], p["w_out_rw"], name=nm + "orw")
    (s["merged"],) = _rowwise(_f_merge, _layer_rows(x, proj, s)["merge"], [], [1024], bt=BT, name=nm + "merge")
    return _mm(s["merged"], p["w_o"], add=x, name=nm + "wo"), s, (exchanged[0] if exchanged else None)


def _layer_bwd(x, dx_out, p, s, nm, side=None, side_late=None):
    g = {}
    proj = s["proj"]
    rows = _layer_rows(x, proj, s)
    g["w_o"] = _mm(s["merged"], dx_out, ta=True, name=nm + "g_wo")
    d_merged = _mm(dx_out, p["w_o"], tb=True, name=nm + "d_merged")
    dp_sb, dp_ssd, dp_rw, d_gates = _rowwise_bwd(_f_merge, rows["merge"], [], [(d_merged, 1024, 0)], bt=BT,
                                                 name=nm + "merge_b", groups=[[0], [1], [2], [3, 4, 5]])
    g["w_out_sb"] = _mm(s["y_sb"], dp_sb, ta=True, name=nm + "g_osb")
    g["w_out_ssd"] = _mm(s["y_ssd"], dp_ssd, ta=True, name=nm + "g_ossd")
    g["w_out_rw"] = _mm(s["y_rw"], dp_rw, ta=True, name=nm + "g_orw")
    dy_sb = _mm(dp_sb, p["w_out_sb"], tb=True, name=nm + "d_ysb")
    dy_ssd = _mm(dp_ssd, p["w_out_ssd"], tb=True, name=nm + "d_yssd")
    dy_rw = _mm(dp_rw, p["w_out_rw"], tb=True, name=nm + "d_yrw")
    dy_sb_raw, d_sbgate = _rowwise_bwd(_f_sb_gate, rows["sb_gate"], [], [(dy_sb, 512, 0)], bt=BT, name=nm + "sbgate_b")
    dq, dk, dv = _sb2_bwd(proj, dy_sb_raw, s["lt"], name=nm + "sb_b")
    dy_ssd_raw, dz, g["ssd_norm_g"] = _rowwise_bwd(_f_ssd_norm, rows["ssd_norm"], [p["ssd_norm_g"]],
                                                   [(dy_ssd, 1024, 0)], bt=BT, name=nm + "ssdnorm_b")
    dxc, ddtr, g["dt_bias"], g["a_log"], g["d_skip"] = _ssd_bwd(
        s["xc"], proj, p["dt_bias"], p["a_log"], p["d_skip"], s["hin"], dy_ssd_raw, name=nm + "ssd_b")
    conv_out = _colwise_bwd(_f_conv, proj, C_XBC, XBC_COLS, p["conv"], dxc, bc=BC, name=nm + "conv_b")
    dxbc, g["conv"] = conv_out[0], conv_out[1:]
    dys, dr0, dk0, dv0, d_rwgate, g["rw_ln_g"], g["rw_ln_b"], g["rw_r_k"] = _rowwise_bwd(
        _f_rw_post, rows["rw_post"], p["rw_post"], [(dy_rw, 512, 0)], bt=BT, name=nm + "rwpost_b")
    dr, dw, dk2, dvv, dn, db, *exchanged = _rw_chunk_bwd(s["mixed"], s["w"], s["k2"], s["n"], s["b"], s["st"], dys,
                                                         dr0, dk0, dv0, name=nm + "scan_b",
                                                         side=side(g) if side else None)
    pre_out = _rowwise_bwd(_f_rw_pre, rows["rw_pre"], p["rw_pre"],
                           [(dw, 512, 0), (dk2, 512, 0), (dn, 512, 0), (db, 512, 0)], bt=BT, name=nm + "rwpre_b")
    dkm, dlo, g["rw_pre"] = pre_out[0], pre_out[1], pre_out[2:]
    d_mixed = jnp.concatenate([dr, dkm, dvv, d_rwgate, dlo], axis=1)
    d_slab, g["rw_mu"] = _colwise_bwd(_f_rw_mix, proj, C_RW, RW_COLS, [p["rw_mu"]], d_mixed, bc=BC, name=nm + "mix_b")
    d_proj = jnp.concatenate([dq, dk, dv, d_sbgate, dz, d_gates, d_slab, ddtr, dxbc], axis=1)
    g["w_in"] = _mm(s["h"], d_proj, ta=True, name=nm + "g_win")
    dh = _mm(d_proj, p["w_in"], tb=True, tn=1024, tk=512, name=nm + "d_h", side=side_late(g) if side_late else None)
    dh, late = dh if side_late else (dh, None)
    dx, g["norm_g"] = _rowwise_bwd(_f_rms_res, rows["rms"], [p["norm_g"]], [(dh, D_MODEL, 0), (dx_out, D_MODEL, 0)],
                                   bt=BT, name=nm + "rms_b")
    return dx, g, (exchanged[0] if exchanged else None), late


MESH = pl.DeviceIdType.MESH
N_DEV = 8
_ANY = pl.BlockSpec(memory_space=pl.ANY)


def _here():
    x, y, c = lax.axis_index("x"), lax.axis_index("y"), lax.axis_index("c")
    return x, y, c, [(1 - x, y), (x, 1 - y), (1 - x, 1 - y)]


def _chip_exchange(srcs, *, per_dest, name):
    n = len(srcs)

    def body(*refs):
        _exchange(refs[:n], refs[n:2 * n], refs[2 * n:], per_dest, start=True, wait=True)

    return pl.pallas_call(
        body, name=name, in_specs=[_ANY] * n, out_specs=[_ANY] * n,
        out_shape=_exchange_out_shapes(srcs, per_dest), scratch_shapes=_exchange_sems(n),
    )(*srcs)


def _by_layer(per_dest):
    return per_dest is not True and per_dest is not False


def _exchange_out_shapes(srcs, per_dest):
    lead = (4, 2) if _by_layer(per_dest) else (4,)
    return [jax.ShapeDtypeStruct(lead + s.shape[-2:], s.dtype) for s in srcs]


def _exchange_sems(n):
    return [pltpu.SemaphoreType.DMA((3 * n,)), pltpu.SemaphoreType.DMA((3 * n,)), pltpu.SemaphoreType.DMA((2 * n,))]


def _exchange(src_refs, out_refs, sems, per_dest, *, start, wait):
    send_sems, recv_sems, local_sems = sems
    x, y, c, chips = _here()
    me = 2 * x + y
    owns, sends, recvs = [], [], []
    for a, (src_ref, out_ref) in enumerate(zip(src_refs, out_refs)):
        if per_dest is True:
            pick = lambda q, s=src_ref: s.at[q]
        elif per_dest is False:
            pick = lambda q, s=src_ref: s.at[c]
        else:
            pick = lambda q, s=src_ref: s.at[per_dest].at[c]
        any_block = src_ref.at[0] if len(src_ref.shape) == 3 else src_ref.at[0].at[0]
        if _by_layer(per_dest):
            slot = lambda q, o=out_ref: o.at[q].at[c]
            owns.append(pltpu.make_async_copy(src_ref.at[per_dest].at[1 - c], out_ref.at[me].at[1 - c],
                                              local_sems.at[2 * a + 1]))
        else:
            slot = lambda q, o=out_ref: o.at[q]
        owns.append(pltpu.make_async_copy(pick(me), slot(me), local_sems.at[2 * a]))
        for j, (px, py) in enumerate(chips):
            sends.append(pltpu.make_async_remote_copy(
                pick(2 * px + py), slot(me), send_sems.at[3 * a + j], recv_sems.at[3 * a + j],
                device_id=(px, py, c), device_id_type=MESH))
            recvs.append(pltpu.make_async_remote_copy(
                any_block, slot(2 * px + py), send_sems.at[3 * a + j], recv_sems.at[3 * a + j],
                device_id=(px, py, c), device_id_type=MESH))
    if start:
        for cp in owns + sends:
            cp.start()
    if wait:
        for cp in recvs:
            cp.wait_recv()
        for cp in sends:
            cp.wait_send()
        for cp in owns:
            cp.wait()


def _sibling_fill(bufs, *, name):
    n = len(bufs)

    def body(*refs):
        in_refs, out_refs, send_sems, recv_sems = refs[:n], refs[n:2 * n], refs[2 * n], refs[2 * n + 1]
        x, y, c, chips = _here()
        copies = []
        for a, (src, dst) in enumerate(zip(in_refs, out_refs)):
            for j, (px, py) in enumerate(chips):
                q = 2 * px + py
                copies.append(pltpu.make_async_remote_copy(
                    src.at[q].at[c], dst.at[q].at[c], send_sems.at[3 * a + j], recv_sems.at[3 * a + j],
                    device_id=(x, y, 1 - c), device_id_type=MESH))
        for cp in copies:
            cp.start()
        for a, (src, dst) in enumerate(zip(in_refs, out_refs)):
            for j, (px, py) in enumerate(chips):
                q = 2 * px + py
                pltpu.make_async_remote_copy(
                    src.at[q].at[c], dst.at[q].at[1 - c], send_sems.at[3 * a + j], recv_sems.at[3 * a + j],
                    device_id=(x, y, 1 - c), device_id_type=MESH).wait_recv()
        for cp in copies:
            cp.wait_send()

    return pl.pallas_call(
        body, name=name, in_specs=[_ANY] * n, out_specs=[_ANY] * n,
        out_shape=[jax.ShapeDtypeStruct(b.shape, b.dtype) for b in bufs],
        input_output_aliases={a: a for a in range(n)},
        scratch_shapes=[pltpu.SemaphoreType.DMA((3 * n,)), pltpu.SemaphoreType.DMA((3 * n,))],
    )(*bufs)


def _sibling_swap(srcs, *, other_slot, name):
    n = len(srcs)

    def body(*refs):
        src_refs, out_refs, send_sems, recv_sems = refs[:n], refs[n:2 * n], refs[2 * n], refs[2 * n + 1]
        x, y, c, _ = _here()
        copies = [pltpu.make_async_remote_copy(s.at[1 - c] if other_slot else s, o, send_sems.at[a], recv_sems.at[a],
                                               device_id=(x, y, 1 - c), device_id_type=MESH)
                  for a, (s, o) in enumerate(zip(src_refs, out_refs))]
        for cp in copies:
            cp.start()
        for cp in copies:
            cp.wait()

    return pl.pallas_call(
        body, name=name, in_specs=[_ANY] * n, out_specs=[_ANY] * n,
        out_shape=[jax.ShapeDtypeStruct(s.shape[1:] if other_slot else s.shape, s.dtype) for s in srcs],
        scratch_shapes=[pltpu.SemaphoreType.DMA((n,)), pltpu.SemaphoreType.DMA((n,))],
    )(*srcs)


def _allgather_small(v, *, reduce, name):
    r = v.shape[0]

    def body(v_ref, out_ref, *rest):
        send_sems, recv_sems, local_sem = rest[-3:]
        x, y, c, chips = _here()
        me, sibling = (x, y, c), (x, y, 1 - c)

        def slot(px, py, pc):
            return out_ref.at[4 * px + 2 * py + pc]

        def copy(k, block, to, src=None):
            return pltpu.make_async_remote_copy(
                src_ref=slot(*block) if src is None else src, dst_ref=slot(*block),
                send_sem=send_sems.at[k], recv_sem=recv_sems.at[k], device_id=to, device_id_type=MESH)

        mine = pltpu.make_async_copy(v_ref, slot(*me), local_sem)
        mine.start()
        first = [copy(0, me, sibling, src=v_ref)]
        first += [copy(1 + j, me, (*chip, c), src=v_ref) for j, chip in enumerate(chips)]
        for cp in first:
            cp.start()
        passed = [copy(4 + j, (*chip, c), sibling) for j, chip in enumerate(chips)]
        for j, chip in enumerate(chips):
            copy(1 + j, (*chip, c), me).wait_recv()
            passed[j].start()
        copy(0, sibling, me).wait_recv()
        for j, chip in enumerate(chips):
            copy(4 + j, (*chip, 1 - c), me).wait_recv()
        for cp in first + passed:
            cp.wait_send()
        mine.wait()
        if reduce:
            total = out_ref[0]
            for d in range(1, N_DEV):
                total = total + out_ref[d]
            rest[0][...] = total

    vm = pl.BlockSpec(memory_space=pltpu.VMEM)
    out_shape = [jax.ShapeDtypeStruct((N_DEV, r, LANES), F32)] + ([jax.ShapeDtypeStruct((r, LANES), F32)] if reduce else [])
    return pl.pallas_call(
        body, name=name, in_specs=[vm], out_specs=[vm] * len(out_shape), out_shape=out_shape,
        scratch_shapes=[pltpu.SemaphoreType.DMA((7,)), pltpu.SemaphoreType.DMA((7,)), pltpu.SemaphoreType.DMA],
        compiler_params=pltpu.CompilerParams(vmem_limit_bytes=VMEM_LIMIT),
    )(v)


REDUCE_BLOCK_BYTES = 2 << 20


def _reduce_rows(r, c):
    cands = [b for b in range(16, r + 1, 16) if r % b == 0 and b * c * 4 <= REDUCE_BLOCK_BYTES]
    return max(cands)


def _add_halves(mine2, other, c_idx, *, name):
    _, nq, r, c = mine2.shape
    br = _reduce_rows(r, c)

    def body(c_ref, a_ref, b_ref, o_ref):
        o_ref[...] = (a_ref[0] + b_ref[...]).astype(o_ref.dtype)

    blk = pl.BlockSpec((1, br, c), lambda q, i, c_ref: (q, i, 0))
    return pl.pallas_call(
        body, name=name,
        grid_spec=pltpu.PrefetchScalarGridSpec(
            num_scalar_prefetch=1, grid=(nq, r // br),
            in_specs=[pl.BlockSpec((1, 1, br, c), lambda q, i, c_ref: (c_ref[0], q, i, 0)), blk],
            out_specs=blk),
        out_shape=jax.ShapeDtypeStruct((nq, r, c), BF16),
        compiler_params=_params(("parallel", "parallel")),
    )(c_idx, mine2, other)


def _sum_chips(parts, *, name):
    _, r, c = parts.shape
    br = _reduce_rows(r, c)

    def body(p_ref, o_ref):
        total = p_ref[0].astype(F32)
        for q in range(1, 4):
            total = total + p_ref[q].astype(F32)
        o_ref[...] = total

    return pl.pallas_call(
        body, name=name, grid=(r // br,),
        in_specs=[pl.BlockSpec((4, br, c), lambda i: (0, i, 0))],
        out_specs=pl.BlockSpec((br, c), lambda i: (i, 0)),
        out_shape=jax.ShapeDtypeStruct((r, c), F32),
        compiler_params=_params(("parallel",)),
    )(parts)


BIG = ("w_in", "w_out_sb", "w_out_ssd", "w_out_rw", "w_o")
BIG_AXIS = {"w_in": 2, "w_out_sb": 2, "w_out_ssd": 1, "w_out_rw": 2, "w_o": 1}
SMALL_SHARDED = {"conv_w": 320, "rw_w_up": 128, "rw_a_up": 128}
SMALL = ("norm_g", "conv_w", "conv_b", "dt_bias", "a_log", "d_skip", "ssd_norm_g", "rw_mu", "rw_w0", "rw_w_up",
         "rw_a0", "rw_a_up", "rw_k_k", "rw_k_a", "rw_r_k", "rw_ln_g", "rw_ln_b", "final_g")


def _rows_of(a):
    flat = a.reshape(-1)
    pad = (-flat.shape[0]) % LANES
    return jnp.pad(flat, (0, pad)).reshape(-1, LANES)


def _pack_rows(arrays, multiple=8):
    rows = jnp.concatenate([_rows_of(a) for a in arrays], axis=0)
    pad = (-rows.shape[0]) % multiple
    return jnp.pad(rows, ((0, pad), (0, 0)))


def _unpack_rows(rows, shapes):
    out, off = [], 0
    for shp in shapes:
        n = 1
        for d in shp:
            n *= d
        nr = -(-n // LANES)
        out.append(rows[off:off + nr].reshape(-1)[:n].reshape(shp))
        off += nr
    return out


COL_MAP = ((0, 3072, 0), (3072, 4352, C_XBC), (4352, 4368, C_DT), (4368, 6544, C_RW), (6544, 9616, C_GATES))
SHARD_COLS = N_IN // 4


def _w_in_from_shards(shards):
    pieces = []
    for a, b, dst in sorted(COL_MAP, key=lambda m: m[2]):
        if pieces and dst > pieces[-1][0]:
            pieces.append((dst, jnp.zeros((shards[0].shape[0], dst - pieces[-1][0]), shards[0].dtype)))
        for q in range(4):
            lo, hi = max(a, q * SHARD_COLS), min(b, (q + 1) * SHARD_COLS)
            if lo < hi:
                pieces.append((dst + hi - a, shards[q][:, lo - q * SHARD_COLS:hi - q * SHARD_COLS]))
    return jnp.concatenate([p for _, p in pieces], axis=1)


def _w_in_shard(g, q):
    pieces = []
    for a, b, dst in COL_MAP:
        lo, hi = max(a, q * SHARD_COLS), min(b, (q + 1) * SHARD_COLS)
        if lo < hi:
            pieces.append(g[:, dst + lo - a:dst + hi - a])
    return jnp.concatenate(pieces, axis=1)


def _row_halves(a):
    return a.reshape(2, a.shape[0] // 2, a.shape[1])


def _join_halves(core, mine, theirs):
    return jnp.where(core == 0, jnp.concatenate([mine, theirs], axis=-2), jnp.concatenate([theirs, mine], axis=-2))


def kernel(x, norm_g, w_in, conv_w, conv_b, dt_bias, a_log, d_skip, ssd_norm_g, rw_mu, rw_w0, rw_w_up, rw_a0, rw_a_up, rw_k_k, rw_k_a, rw_r_k, rw_ln_g, rw_ln_b, w_out_sb, w_out_ssd, w_out_rw, w_o, final_g, loss_target, m_norm_g, m_w_in, m_conv_w, m_conv_b, m_dt_bias, m_a_log, m_d_skip, m_ssd_norm_g, m_rw_mu, m_rw_w0, m_rw_w_up, m_rw_a0, m_rw_a_up, m_rw_k_k, m_rw_k_a, m_rw_r_k, m_rw_ln_g, m_rw_ln_b, m_w_out_sb, m_w_out_ssd, m_w_out_rw, m_w_o, m_final_g, v_norm_g, v_w_in, v_conv_w, v_conv_b, v_dt_bias, v_a_log, v_d_skip, v_ssd_norm_g, v_rw_mu, v_rw_w0, v_rw_w_up, v_rw_a0, v_rw_a_up, v_rw_k_k, v_rw_k_a, v_rw_r_k, v_rw_ln_g, v_rw_ln_b, v_w_out_sb, v_w_out_ssd, v_w_out_rw, v_w_o, v_final_g):
    names = ("norm_g", "w_in", "conv_w", "conv_b", "dt_bias", "a_log", "d_skip", "ssd_norm_g", "rw_mu", "rw_w0",
             "rw_w_up", "rw_a0", "rw_a_up", "rw_k_k", "rw_k_a", "rw_r_k", "rw_ln_g", "rw_ln_b", "w_out_sb",
             "w_out_ssd", "w_out_rw", "w_o", "final_g")
    w_loc = dict(zip(names, (norm_g, w_in, conv_w, conv_b, dt_bias, a_log, d_skip, ssd_norm_g, rw_mu, rw_w0, rw_w_up,
                             rw_a0, rw_a_up, rw_k_k, rw_k_a, rw_r_k, rw_ln_g, rw_ln_b, w_out_sb, w_out_ssd, w_out_rw,
                             w_o, final_g)))
    m_loc = dict(zip(names, (m_norm_g, m_w_in, m_conv_w, m_conv_b, m_dt_bias, m_a_log, m_d_skip, m_ssd_norm_g,
                             m_rw_mu, m_rw_w0, m_rw_w_up, m_rw_a0, m_rw_a_up, m_rw_k_k, m_rw_k_a, m_rw_r_k,
                             m_rw_ln_g, m_rw_ln_b, m_w_out_sb, m_w_out_ssd, m_w_out_rw, m_w_o, m_final_g)))
    v_loc = dict(zip(names, (v_norm_g, v_w_in, v_conv_w, v_conv_b, v_dt_bias, v_a_log, v_d_skip, v_ssd_norm_g,
                             v_rw_mu, v_rw_w0, v_rw_w_up, v_rw_a0, v_rw_a_up, v_rw_k_k, v_rw_k_a, v_rw_r_k,
                             v_rw_ln_g, v_rw_ln_b, v_w_out_sb, v_w_out_ssd, v_w_out_rw, v_w_o, v_final_g)))
    chip = 2 * lax.axis_index("x") + lax.axis_index("y")
    core = lax.axis_index("c")

    as_sent = [w_loc[n].astype(BF16).reshape(DEPTH, 2, w_loc[n].shape[1] // 2, w_loc[n].shape[2]) for n in BIG]

    def gathered(mine, nm):
        out = {}
        for n, buf in zip(BIG, _sibling_fill(mine, name=nm)):
            shards = buf.reshape(4, 2 * buf.shape[2], buf.shape[3])
            out[n] = (_w_in_from_shards([shards[q] for q in range(4)]) if n == "w_in"
                      else jnp.concatenate([shards[q] for q in range(4)], axis=BIG_AXIS[n] - 1))
        return out

    full = {}
    sm_names = tuple(SMALL_SHARDED)
    sm_shapes = [w_loc[n].shape for n in sm_names]
    (got_sm,) = _allgather_small(_pack_rows([w_loc[n] for n in sm_names]), reduce=False, name="gather_small")
    per_chip = [_unpack_rows(got_sm[4 * (q // 2) + 2 * (q % 2)], sm_shapes) for q in range(4)]
    for i, n in enumerate(sm_names):
        full[n] = jnp.concatenate([per_chip[q][i] for q in range(4)], axis=-1)

    def pad16(a):
        return jnp.zeros((1, LANES), F32).at[0, :SSD_HEADS].set(a)

    def layer_params(i, big):
        row = lambda n: w_loc[n][i].reshape(1, -1)
        cw = full["conv_w"][i]
        return dict(
            norm_g=row("norm_g"), w_in=big["w_in"], conv=[cw[k][None] for k in range(4)] + [row("conv_b")],
            dt_bias=pad16(dt_bias[i]), a_log=pad16(a_log[i]), d_skip=pad16(d_skip[i]),
            ssd_norm_g=row("ssd_norm_g"), rw_mu=row("rw_mu"),
            rw_pre=[row("rw_w0"), jnp.zeros((LANES, 512), F32).at[:HEAD].set(full["rw_w_up"][i]), row("rw_a0"),
                    jnp.zeros((LANES, 512), F32).at[HEAD:].set(full["rw_a_up"][i]), row("rw_k_k"), row("rw_k_a")],
            rw_post=[row("rw_ln_g"), row("rw_ln_b"), row("rw_r_k")],
            w_out_sb=big["w_out_sb"], w_out_ssd=big["w_out_ssd"], w_out_rw=big["w_out_rw"], w_o=big["w_o"])

    c_idx = core.reshape(1).astype(jnp.int32)

    def reduce_prepare(items, nm):
        sends = []
        for g, n, _ in items:
            per_chip = ([_w_in_shard(g[n], q) for q in range(4)] if n == "w_in"
                        else jnp.split(g[n], 4, axis=BIG_AXIS[n] - 1))
            sends.append(jnp.stack([_row_halves(p) for p in per_chip], axis=1))
        others = _sibling_swap(sends, other_slot=True, name=nm + "sibling")
        return [_add_halves(s, o, c_idx, name=nm + "add_" + lab) for (_, _, lab), s, o in zip(items, sends, others)]

    def reduce_finish(exchanged, labels, nm):
        mine = [_sum_chips(p, name=nm + "sum_" + lab) for lab, p in zip(labels, exchanged)]
        theirs = _sibling_swap(mine, other_slot=False, name=nm + "join")
        return {lab: _join_halves(core, a, b) for lab, a, b in zip(labels, mine, theirs)}

    assert DEPTH == 2
    out_proj = BIG[1:]
    params, xs, saved, grads = [None] * 2, [x[0], None, None], [None] * 2, [None] * 2
    params[0] = layer_params(0, gathered(_chip_exchange(as_sent, per_dest=0, name="gather_l0"), "gather_l0_join"))
    xs[1], saved[0], got = _layer_fwd(xs[0], params[0], "l0_", side=(as_sent, 1))
    params[1] = layer_params(1, gathered(got, "gather_l1_join"))
    xs[2], saved[1], _ = _layer_fwd(xs[1], params[1], "l1_")
    dx, loss_row, g_final = _final(xs[2], final_g.reshape(1, -1), loss_target[0], bt=BT, name="final")
    dx, grads[1], _, _ = _layer_bwd(xs[1], dx, params[1], saved[1], "l1_")
    early = lambda g: [(grads[1], n, "l1_" + n) for n in BIG] + [(g, n, "l0_" + n) for n in out_proj]
    dx, grads[0], got, got_late = _layer_bwd(
        xs[0], dx, params[0], saved[0], "l0_",
        side=lambda g: (reduce_prepare(early(g), "reduce_early_"), True),
        side_late=lambda g: (reduce_prepare([(g, "w_in", "l0_w_in")], "reduce_late_"), True))
    total = reduce_finish(got + got_late, [lab for _, _, lab in early(None)] + ["l0_w_in"], "reduce_")
    totals = [{n: total[f"l{i}_" + n] for n in BIG} for i in range(DEPTH)]

    def stacked(fn):
        return jnp.stack([fn(grads[i]) for i in range(DEPTH)])

    g_loc = {
        "norm_g": stacked(lambda g: g["norm_g"][0]),
        "conv_w": stacked(lambda g: jnp.concatenate(g["conv"][:4], axis=0)),
        "conv_b": stacked(lambda g: g["conv"][4][0]),
        "dt_bias": stacked(lambda g: g["dt_bias"][0, :SSD_HEADS]),
        "a_log": stacked(lambda g: g["a_log"][0, :SSD_HEADS]),
        "d_skip": stacked(lambda g: g["d_skip"][0, :SSD_HEADS]),
        "ssd_norm_g": stacked(lambda g: g["ssd_norm_g"][0]),
        "rw_mu": stacked(lambda g: g["rw_mu"][0]),
        "rw_w0": stacked(lambda g: g["rw_pre"][0][0]),
        "rw_w_up": stacked(lambda g: g["rw_pre"][1][:HEAD]),
        "rw_a0": stacked(lambda g: g["rw_pre"][2][0]),
        "rw_a_up": stacked(lambda g: g["rw_pre"][3][HEAD:]),
        "rw_k_k": stacked(lambda g: g["rw_pre"][4][0]),
        "rw_k_a": stacked(lambda g: g["rw_pre"][5][0]),
        "rw_r_k": stacked(lambda g: g["rw_r_k"].reshape(8, HEAD)),
        "rw_ln_g": stacked(lambda g: g["rw_ln_g"][0]),
        "rw_ln_b": stacked(lambda g: g["rw_ln_b"][0]),
        "final_g": g_final[0],
    }

    g_out = {n: jnp.stack([totals[0][n], totals[1][n]]) for n in BIG}

    sm_all = SMALL + ("loss",)
    sm_full_shapes = [g_loc[n].shape for n in SMALL] + [(1,)]
    _, summed = _allgather_small(_pack_rows([g_loc[n] for n in SMALL] + [loss_row[0, :1]]), reduce=True, name="reduce_small")
    sm = dict(zip(sm_all, _unpack_rows(summed, sm_full_shapes)))
    for n in SMALL:
        g_out[n] = sm[n]
    for n, wd in SMALL_SHARDED.items():
        g_out[n] = lax.dynamic_slice_in_dim(sm[n], chip * wd, wd, axis=sm[n].ndim - 1)
    loss = sm["loss"][0]

    upd = {n: _adamw(w_loc[n], g_out[n], m_loc[n], v_loc[n], name="adamw_" + n) for n in names if n != "w_in"}
    cols = SHARD_COLS // 4
    to_cols = lambda a: jnp.transpose(a, (2, 0, 1)).reshape(4, cols, DEPTH, D_MODEL)
    from_cols = lambda a: jnp.transpose(a.reshape(SHARD_COLS, DEPTH, D_MODEL), (1, 2, 0))
    g_cols = lax.optimization_barrier(to_cols(g_out["w_in"]))
    g_out["w_in"] = from_cols(g_cols)
    upd["w_in"] = tuple(from_cols(a) for a in _adamw(
        to_cols(w_loc["w_in"]), g_cols, to_cols(m_loc["w_in"]), to_cols(v_loc["w_in"]),
        name="adamw_w_in", block=(1, cols, DEPTH, D_MODEL // 2)))
    return (loss, dx[None], *[g_out[n] for n in names], *[upd[n][0] for n in names],
            *[upd[n][1] for n in names], *[upd[n][2] for n in names])
```

```python
import functools

import jax
import jax.numpy as jnp
from jax import lax
from jax.experimental import pallas as pl
from jax.experimental.pallas import tpu as pltpu

F32 = jnp.float32
BF16 = jnp.bfloat16

D_MODEL = 1024
DEPTH = 2
HEAD = 64
LANES = 128
CHUNK = 128
RMS_EPS = 1e-6
GN_EPS = 64e-5
VMEM_LIMIT = 56 * 1024 * 1024

N_IN = 9616
N_PAD = 9728
C_SB, C_Z, C_GATES, C_RW, C_LO, C_DT, C_XBC = 0, 2048, 3072, 6144, 8192, 8320, 8448
RW_COLS = 2176
XBC_COLS = 1280

ADAM_LR, ADAM_B1, ADAM_B2, ADAM_EPS, ADAM_WD, ADAM_STEP = 0.001, 0.9, 0.999, 1e-08, 0.01, 10


def _params(sem=None):
    return pltpu.CompilerParams(dimension_semantics=sem, vmem_limit_bytes=VMEM_LIMIT)


@jax.custom_vjp
def _sigmoid(x):
    return 1.0 / (1.0 + jnp.exp(-x))


def _sigmoid_fwd(x):
    s = _sigmoid(x)
    return s, s


def _sigmoid_bwd(s, g):
    return (g * s * (1.0 - s),)


_sigmoid.defvjp(_sigmoid_fwd, _sigmoid_bwd)


@jax.custom_vjp
def _silu(x):
    return x * _sigmoid(x)


def _silu_fwd(x):
    s = _sigmoid(x)
    return x * s, (x, s)


def _silu_bwd(res, g):
    x, s = res
    return (g * (s + x * s * (1.0 - s)),)


_silu.defvjp(_silu_fwd, _silu_bwd)


@jax.custom_vjp
def _softplus(x):
    return jnp.maximum(x, 0.0) + jnp.log(1.0 + jnp.exp(-jnp.abs(x)))


def _softplus_fwd(x):
    return _softplus(x), x


def _softplus_bwd(x, g):
    return (g * _sigmoid(x),)


_softplus.defvjp(_softplus_fwd, _softplus_bwd)


def _dot(a, b, dims):
    return lax.dot_general(a.astype(BF16), b.astype(BF16), (dims, ((), ())), preferred_element_type=F32)


def _dot_nn(a, b):
    return _dot(a, b, ((1,), (0,)))


def _dot_nt(a, b):
    return _dot(a, b, ((1,), (1,)))


def _dot_tn(a, b):
    return _dot(a, b, ((0,), (0,)))


@jax.custom_vjp
def _bdot(a, b):
    return _dot_nn(a, b)


def _bdot_fwd(a, b):
    return _dot_nn(a, b), (a, b)


def _bdot_bwd(res, g):
    a, b = res
    return _dot_nt(g, b), _dot_tn(a, g)


_bdot.defvjp(_bdot_fwd, _bdot_bwd)


def _split2(x):
    hi = x.astype(BF16)
    lo = (x - hi.astype(F32)).astype(BF16)
    return hi, lo


_NT = (((1,), (1,)), ((), ()))
_NN = (((1,), (0,)), ((), ()))
_TN = (((0,), (0,)), ((), ()))


def _dot2(x, m, dn=_NN):
    hi, lo = _split2(x)
    return (lax.dot_general(hi, m, dn, preferred_element_type=F32)
            + lax.dot_general(lo, m, dn, preferred_element_type=F32))


def _seg_matrix(n):
    r = lax.broadcasted_iota(jnp.int32, (n, n), 0) // HEAD
    c = lax.broadcasted_iota(jnp.int32, (n, n), 1) // HEAD
    return (r == c).astype(BF16)


@jax.custom_vjp
def _segsum2(x, seg):
    return _dot2(x, seg)


def _segsum2_fwd(x, seg):
    return _dot2(x, seg), seg


def _segsum2_bwd(seg, g):
    return _dot2(g, seg), jnp.zeros_like(seg)


_segsum2.defvjp(_segsum2_fwd, _segsum2_bwd)


def _make_segsum(seg):
    return lambda x: _segsum2(x, seg)


def _shift_down_raw(x, k):
    row = lax.broadcasted_iota(jnp.int32, x.shape, 0)
    return jnp.where(row >= k, pltpu.roll(x, k, 0), 0.0)


def _shift_up_raw(x, k):
    t = x.shape[0]
    row = lax.broadcasted_iota(jnp.int32, x.shape, 0)
    return jnp.where(row < t - k, pltpu.roll(x, t - k, 0), 0.0)


@functools.partial(jax.custom_vjp, nondiff_argnums=(1,))
def _shift_down(x, k):
    return _shift_down_raw(x, k)


def _shift_down_fwd(x, k):
    return _shift_down_raw(x, k), None


def _shift_down_bwd(k, _, g):
    return (_shift_up_raw(g, k),)


_shift_down.defvjp(_shift_down_fwd, _shift_down_bwd)


def _mm(a, b, *, name, ta=False, tb=False, add=None, out_dtype=F32, tm=2048, tn=512, tk=None, side=None):
    m, k = (a.shape[1], a.shape[0]) if ta else a.shape
    n = b.shape[0] if tb else b.shape[1]
    tm, tn = min(tm, m), min(tn, n)
    tk = k if tk is None else tk
    nk = k // tk
    assert m % tm == 0 and n % tn == 0 and k % tk == 0
    dims = ((0 if ta else 1,), (1 if tb else 0,))

    def body(a_ref, b_ref, *refs):
        o_ref, acc_ref = refs[-2:]
        p = _dot(a_ref[...], b_ref[...], dims)

        def emit(total):
            if add is not None:
                total = total + refs[0][...]
            o_ref[...] = total.astype(o_ref.dtype)

        if nk == 1:
            emit(p)
        else:
            kk = pl.program_id(2)

            @pl.when(kk == 0)
            def _():
                acc_ref[...] = p

            @pl.when(kk > 0)
            def _():
                acc_ref[...] += p

            @pl.when(kk == nk - 1)
            def _():
                emit(acc_ref[...])

    a_spec = pl.BlockSpec((tk, tm), lambda i, j, kk: (kk, i)) if ta else pl.BlockSpec((tm, tk), lambda i, j, kk: (i, kk))
    b_spec = pl.BlockSpec((tn, tk), lambda i, j, kk: (j, kk)) if tb else pl.BlockSpec((tk, tn), lambda i, j, kk: (kk, j))
    o_spec = pl.BlockSpec((tm, tn), lambda i, j, kk: (i, j))
    res = _call_with_side(
        body, side, name=name, grid=(m // tm, n // tn, nk), semantics=("parallel", "parallel", "arbitrary"),
        in_specs=[a_spec, b_spec] + ([o_spec] if add is not None else []), out_specs=[o_spec],
        out_shape=[jax.ShapeDtypeStruct((m, n), out_dtype)],
        scratch_shapes=[pltpu.VMEM((tm, tn) if nk > 1 else (8, LANES), F32)],
        operands=(a, b) + ((add,) if add is not None else ()))
    return res[0] if side is None else (res[0], res[1])


def _row_specs(rows, bt):
    return [pl.BlockSpec((bt, w), functools.partial(lambda i, c: (i, c), c=c)) for _, w, c in rows]


def _full_spec(p):
    return pl.BlockSpec(p.shape, functools.partial(lambda i, nd: (0,) * nd, nd=p.ndim))


def _rowwise(f, rows, pars, out_widths, *, bt, name, acc_widths=()):
    t = rows[0][0].shape[0]
    nr, npar, no, na = len(rows), len(pars), len(out_widths), len(acc_widths)

    def body(*refs):
        vals = [r[...] for r in refs[:nr + npar]]
        outs = f(*vals)
        for o_ref, o in zip(refs[nr + npar:nr + npar + no], outs[:no]):
            o_ref[...] = o.astype(o_ref.dtype)
        if na:
            first = pl.program_id(0) == 0
            for a_ref, a in zip(refs[nr + npar + no:], outs[no:]):
                @pl.when(first)
                def _():
                    a_ref[...] = jnp.zeros_like(a_ref)
                a_ref[...] += a

    return pl.pallas_call(
        body, name=name, grid=(t // bt,),
        in_specs=_row_specs(rows, bt) + [_full_spec(p) for p in pars],
        out_specs=[pl.BlockSpec((bt, w), lambda i: (i, 0)) for w in out_widths]
        + [pl.BlockSpec((1, w), lambda i: (0, 0)) for w in acc_widths],
        out_shape=[jax.ShapeDtypeStruct((t, w), F32) for w in out_widths]
        + [jax.ShapeDtypeStruct((1, w), F32) for w in acc_widths],
        compiler_params=_params(("arbitrary",)),
    )(*[r[0] for r in rows], *pars)


def _rowwise_bwd(f, rows, pars, douts, *, bt, name, groups=None):
    t = rows[0][0].shape[0]
    nr, npar, nd = len(rows), len(pars), len(douts)
    groups = [[i] for i in range(nr)] if groups is None else groups
    widths = [r[1] for r in rows]

    def body(*refs):
        vals = [r[...] for r in refs[:nr + npar]]
        cts = tuple(r[...] for r in refs[nr + npar:nr + npar + nd])
        _, vjp = jax.vjp(lambda *a: tuple(f(*a)), *vals)
        grads = vjp(cts)
        out_refs = refs[nr + npar + nd:]
        for g_ref, grp in zip(out_refs[:len(groups)], groups):
            off = 0
            for i in grp:
                g_ref[:, off:off + widths[i]] = grads[i]
                off += widths[i]
        first = pl.program_id(0) == 0
        for p_ref, g in zip(out_refs[len(groups):], grads[nr:]):
            @pl.when(first)
            def _():
                p_ref[...] = jnp.zeros_like(p_ref)
            p_ref[...] += g

    gw = [sum(widths[i] for i in grp) for grp in groups]
    return pl.pallas_call(
        body, name=name, grid=(t // bt,),
        in_specs=_row_specs(rows, bt) + [_full_spec(p) for p in pars] + _row_specs(douts, bt),
        out_specs=[pl.BlockSpec((bt, w), lambda i: (i, 0)) for w in gw] + [_full_spec(p) for p in pars],
        out_shape=[jax.ShapeDtypeStruct((t, w), F32) for w in gw] + [jax.ShapeDtypeStruct(p.shape, F32) for p in pars],
        compiler_params=_params(("arbitrary",)),
    )(*[r[0] for r in rows], *pars, *[d[0] for d in douts])


def _colwise(f, x, c0, ncols, pars, *, bc, name):
    t = x.shape[0]

    def body(x_ref, *refs):
        o_ref = refs[-1]
        o_ref[...] = f(x_ref[...], *[r[...] for r in refs[:-1]])

    return pl.pallas_call(
        body, name=name, grid=(ncols // bc,),
        in_specs=[pl.BlockSpec((t, bc), lambda j: (0, j + c0 // bc))]
        + [pl.BlockSpec((p.shape[0], bc), lambda j: (0, j)) for p in pars],
        out_specs=pl.BlockSpec((t, bc), lambda j: (0, j)),
        out_shape=jax.ShapeDtypeStruct((t, ncols), F32),
        compiler_params=_params(("parallel",)),
    )(x, *pars)


def _colwise_bwd(f, x, c0, ncols, pars, dout, *, bc, name):
    t = x.shape[0]
    npar = len(pars)

    def body(x_ref, *refs):
        vals = [x_ref[...]] + [r[...] for r in refs[:npar]]
        _, vjp = jax.vjp(f, *vals)
        grads = vjp(refs[npar][...])
        for g_ref, g in zip(refs[npar + 1:], grads):
            g_ref[...] = g

    return pl.pallas_call(
        body, name=name, grid=(ncols // bc,),
        in_specs=[pl.BlockSpec((t, bc), lambda j: (0, j + c0 // bc))]
        + [pl.BlockSpec((p.shape[0], bc), lambda j: (0, j)) for p in pars]
        + [pl.BlockSpec((t, bc), lambda j: (0, j))],
        out_specs=[pl.BlockSpec((t, bc), lambda j: (0, j))]
        + [pl.BlockSpec((p.shape[0], bc), lambda j: (0, j)) for p in pars],
        out_shape=[jax.ShapeDtypeStruct((t, ncols), F32)] + [jax.ShapeDtypeStruct(p.shape, F32) for p in pars],
        compiler_params=_params(("parallel",)),
    )(x, *pars, dout)


def _f_rms(x, g):
    return (x * lax.rsqrt(jnp.mean(x * x, axis=-1, keepdims=True) + RMS_EPS) * g,)


def _f_sb_gate(y, gate):
    return (y * _silu(gate),)


def _f_ssd_norm(y, z, g):
    u = y * _silu(z)
    return (u * lax.rsqrt(jnp.mean(u * u, axis=-1, keepdims=True) + RMS_EPS) * g,)


def _f_merge(p_sb, p_ssd, p_rw, g_sb, g_ssd, g_rw):
    return (_sigmoid(g_sb) * p_sb + _sigmoid(g_ssd) * p_ssd + _sigmoid(g_rw) * p_rw,)


def _f_rw_pre(k, lo, w0, w_up, a0, a_up, k_k, k_a):
    segsum = _make_segsum(_seg_matrix(k.shape[1]))
    lane = lax.broadcasted_iota(jnp.int32, lo.shape, 1)
    w_lo = jnp.where(lane < HEAD, jnp.tanh(lo), 0.0)
    a_lo = jnp.where(lane >= HEAD, lo, 0.0)
    w = -_softplus(-(w0 + _bdot(w_lo, w_up))) - 0.5
    log_decay = -jnp.exp(w)
    a = _sigmoid(a0 + _bdot(a_lo, a_up))
    kk = k * k_k
    kk = kk / jnp.maximum(jnp.sqrt(segsum(kk * kk)), 1e-12)
    return log_decay, k * (1.0 + (a - 1.0) * k_a), -kk, kk * a


def _f_rw_post(y, r, k2, v, gate, ln_g, ln_b, r_k):
    segsum = _make_segsum(_seg_matrix(y.shape[1]))
    yc = y - segsum(y) * (1.0 / HEAD)
    var = segsum(yc * yc) * (1.0 / HEAD)
    yn = yc * lax.rsqrt(var + GN_EPS) * ln_g + ln_b
    return ((yn + segsum(r * k2 * r_k) * v) * _silu(gate),)


def _f_rw_mix(slab, mu):
    return slab + (_shift_down(slab, 1) - slab) * mu


def _f_conv(x, w0, w1, w2, w3, b):
    acc = x * w3 + b
    for i, w in enumerate((w0, w1, w2)):
        acc = acc + _shift_down(x, 3 - i) * w
    return _silu(acc)


def _log_sigmoid(z):
    return jnp.minimum(z, 0.0) - jnp.log(1.0 + jnp.exp(-jnp.abs(z)))


SB_BQ = 256
SB_BK = 256
assert SB_BQ == SB_BK


def _tri_ones(kind):
    j = lax.broadcasted_iota(jnp.int32, (SB_BK, SB_BK + LANES), 0)
    s = lax.broadcasted_iota(jnp.int32, (SB_BK, SB_BK + LANES), 1)
    tri = {"gt": j > s, "le": j <= s, "lt": j < s}[kind]
    return (tri | (s >= SB_BK)).astype(BF16)


def _sb_common(q_ref):
    lane = lax.broadcasted_iota(jnp.int32, (SB_BQ, LANES), 1)
    q = q_ref[...] * (HEAD ** -0.5)
    q2 = jnp.concatenate([jnp.where(lane < HEAD, q, 0.0), jnp.where(lane >= HEAD, q, 0.0)], axis=0).astype(BF16)
    diff = (lax.broadcasted_iota(jnp.int32, (2 * SB_BQ, SB_BK), 1)
            - (lax.broadcasted_iota(jnp.int32, (2 * SB_BQ, SB_BK), 0) & (SB_BQ - 1)))
    return lane, q2, diff


def _rep(x):
    return jnp.concatenate([x] * (SB_BK // LANES), axis=1)


def _sb2_specs(t):
    q = pl.BlockSpec((SB_BQ, LANES), lambda j, i: (i, j))
    k = pl.BlockSpec((t, LANES), lambda j, i: (0, 4 + j))
    v = pl.BlockSpec((t, LANES), lambda j, i: (0, 8 + j))
    return q, k, v


def _sb2_fwd(proj, *, name):
    t = proj.shape[0]

    def body(q_ref, k_ref, v_ref, y_ref, lt_ref):
        i = pl.program_id(1)
        lane, q2, diff = _sb_common(q_ref)
        m_f = _tri_ones("gt")

        def step(kb, carry, diagonal):
            c, acc = carry
            off = pl.multiple_of(kb * SB_BK, SB_BK)
            kblk = k_ref[pl.ds(off, SB_BK), :].astype(BF16)
            vblk = v_ref[pl.ds(off, SB_BK), :].astype(BF16)
            z = lax.dot_general(q2, kblk, _NT, preferred_element_type=F32)
            lb = _log_sigmoid(z)
            lk = jnp.where(diff < 0, lb - z, 0.0) if diagonal else lb - z
            w2 = _dot2(lk, m_f)
            att = jnp.exp(lb + _rep(c) + w2[:, :SB_BK])
            if diagonal:
                att = jnp.where(diff < 0, att, 0.0)
            acc = acc + lax.dot_general(att.astype(BF16), vblk, _NN, preferred_element_type=F32)
            return c + w2[:, SB_BK:], acc

        zero = jnp.zeros((2 * SB_BQ, LANES), F32)
        c, acc = lax.fori_loop(0, i, lambda it, carry: step(i - 1 - it, carry, False), step(i, (zero, zero), True))
        y_ref[...] = jnp.where(lane < HEAD, acc[:SB_BQ], acc[SB_BQ:])
        lt_ref[0] = c[:SB_BQ]
        lt_ref[1] = c[SB_BQ:]

    return pl.pallas_call(
        body, name=name, grid=(4, t // SB_BQ),
        in_specs=list(_sb2_specs(t)),
        out_specs=[pl.BlockSpec((SB_BQ, LANES), lambda j, i: (i, j)),
                   pl.BlockSpec((2, SB_BQ, LANES), lambda j, i: (j, i, 0))],
        out_shape=[jax.ShapeDtypeStruct((t, 4 * LANES), F32), jax.ShapeDtypeStruct((8, t, LANES), F32)],
        compiler_params=_params(("parallel", "arbitrary")),
    )(proj, proj, proj)


def _sb2_bwd(proj, dy, lt, *, name):
    t = proj.shape[0]

    def body(q_ref, k_ref, v_ref, dy_ref, lt_ref, dq_ref, dk_ref, dv_ref):
        i = pl.program_id(1)

        @pl.when(i == 0)
        def _():
            dk_ref[...] = jnp.zeros_like(dk_ref)
            dv_ref[...] = jnp.zeros_like(dv_ref)

        lane, q2, diff = _sb_common(q_ref)
        m_le, m_lt = _tri_ones("le"), _tri_ones("lt")
        dy_blk = dy_ref[...]
        do2 = jnp.concatenate([jnp.where(lane < HEAD, dy_blk, 0.0), jnp.where(lane >= HEAD, dy_blk, 0.0)],
                              axis=0).astype(BF16)
        lt2 = jnp.concatenate([lt_ref[0], lt_ref[1]], axis=0)

        def step(kb, carry, diagonal):
            cp, cg, dq = carry
            off = pl.multiple_of(kb * SB_BK, SB_BK)
            kblk = k_ref[pl.ds(off, SB_BK), :].astype(BF16)
            vblk = v_ref[pl.ds(off, SB_BK), :].astype(BF16)
            z = lax.dot_general(q2, kblk, _NT, preferred_element_type=F32)
            lb = _log_sigmoid(z)
            lk = jnp.where(diff < 0, lb - z, 0.0) if diagonal else lb - z
            w2 = _dot2(lk, m_le)
            att = jnp.exp(lb + _rep(lt2 - cp) - w2[:, :SB_BK])
            if diagonal:
                att = jnp.where(diff < 0, att, 0.0)
            d_e = lax.dot_general(do2, vblk, _NT, preferred_element_type=F32) * att
            g2 = _dot2(d_e, m_lt)
            sig = jnp.exp(lb)
            dz = d_e * (1.0 - sig) - (_rep(cg) + g2[:, :SB_BK]) * sig
            dz = (jnp.where(diff < 0, dz, 0.0) if diagonal else dz).astype(BF16)
            dq = dq + lax.dot_general(dz, kblk, _NN, preferred_element_type=F32)
            dk_ref[pl.ds(off, SB_BK), :] += lax.dot_general(dz, q2, _TN, preferred_element_type=F32)
            dv_ref[pl.ds(off, SB_BK), :] += lax.dot_general(att.astype(BF16), do2, _TN, preferred_element_type=F32)
            return cp + w2[:, SB_BK:], cg + g2[:, SB_BK:], dq

        zero = jnp.zeros((2 * SB_BQ, LANES), F32)
        before = lax.fori_loop(0, i, lambda kb, carry: step(kb, carry, False), (zero, zero, zero))
        _, _, dq = step(i, before, True)
        dq_ref[...] = jnp.where(lane < HEAD, dq[:SB_BQ], dq[SB_BQ:]) * (HEAD ** -0.5)

    q_spec, k_spec, v_spec = _sb2_specs(t)
    blk = pl.BlockSpec((SB_BQ, LANES), lambda j, i: (i, j))
    col = pl.BlockSpec((t, LANES), lambda j, i: (0, j))
    return pl.pallas_call(
        body, name=name, grid=(4, t // SB_BQ),
        in_specs=[q_spec, k_spec, v_spec, blk, pl.BlockSpec((2, SB_BQ, LANES), lambda j, i: (j, i, 0))],
        out_specs=[blk, col, col],
        out_shape=[jax.ShapeDtypeStruct((t, 4 * LANES), F32)] * 3,
        compiler_params=_params(("parallel", "arbitrary")),
    )(proj, proj, proj, dy, lt)


SSD_HEADS = 16
SSD_PAIRS = 8


def _split3(x):
    a = x.astype(BF16)
    r = x - a.astype(F32)
    b = r.astype(BF16)
    return a, b, (r - b.astype(F32)).astype(BF16)


def _dot3(x, m, dn=_NN):
    return sum(lax.dot_general(p, m, dn, preferred_element_type=F32) for p in _split3(x))


def _mdot3(m, x):
    return sum(lax.dot_general(m, p, _NN, preferred_element_type=F32) for p in _split3(x))


def _ssd_common(dtr, dtb, alog, acsx_s, acst_s):
    lane = lax.broadcasted_iota(jnp.int32, (CHUNK, LANES), 1)
    lane1 = lax.broadcasted_iota(jnp.int32, (1, LANES), 1)
    arow = jnp.where(lane1 < SSD_HEADS, -jnp.exp(alog), 0.0)
    dt = jnp.where(lane < SSD_HEADS, _softplus(dtr + dtb), 0.0)
    da = dt * arow
    r = lax.broadcasted_iota(jnp.int32, (CHUNK, CHUNK), 0)
    c = lax.broadcasted_iota(jnp.int32, (CHUNK, CHUNK), 1)
    tril = (r >= c).astype(BF16)
    triu = (r <= c).astype(BF16)
    acs = _mdot3(tril, da)
    acst_s[...] = _dot3(da, triu, _TN)
    eh = lax.broadcasted_iota(jnp.int32, (LANES, 8 * LANES), 0)
    e = (eh == lax.broadcasted_iota(jnp.int32, (LANES, 8 * LANES), 1) // HEAD).astype(BF16)
    eh2 = lax.broadcasted_iota(jnp.int32, (LANES, 16 * LANES), 0)
    e2 = (eh2 == lax.broadcasted_iota(jnp.int32, (LANES, 16 * LANES), 1) // LANES).astype(BF16)
    acsx_s[...] = _dot3(acs, e)
    return dt, arow, _dot3(dt, e), _dot3(acs, e2), e, tril, triu


def _ssd_fwd(xc, proj, dtb, alog, dsk, *, name):
    t = xc.shape[0]
    nc = t // CHUNK

    def body(x_ref, b_ref, c_ref, dtr_ref, dtb_ref, alog_ref, dsk_ref, y_ref, hin_ref, acsx_s, acst_s, h_s):
        @pl.when(pl.program_id(0) == 0)
        def _():
            h_s[...] = jnp.zeros_like(h_s)

        dt, arow, dt_x, acs_b, e, tril, _ = _ssd_common(dtr_ref[...], dtb_ref[...], alog_ref[...], acsx_s, acst_s)
        dsk_x = _dot3(jnp.broadcast_to(dsk_ref[...], (CHUNK, LANES)), e)
        lane = lax.broadcasted_iota(jnp.int32, (CHUNK, LANES), 1)
        causal = (lax.broadcasted_iota(jnp.int32, (CHUNK, CHUNK), 0)
                  >= lax.broadcasted_iota(jnp.int32, (CHUNK, CHUNK), 1))
        for j in range(SSD_PAIRS):
            g = j // 4
            sl = slice(j * LANES, (j + 1) * LANES)
            if j % 4 == 0:
                bg = jnp.where(lane // HEAD == g, b_ref[...], 0.0)
                cg = jnp.where(lane // HEAD == g, c_ref[...], 0.0)
                cb = _dot_nt(cg, bg)
            x = x_ref[:, sl]
            a = acsx_s[:, sl]
            at = acsx_s[CHUNK - 1:CHUNK, sl]
            xdt = x * dt_x[:, sl]
            hin = h_s[j]
            hin_ref[0, j] = hin
            y = jnp.exp(a) * _dot_nn(cg, hin) + x * dsk_x[:, sl]
            h_s[j] = jnp.exp(at) * hin + _dot_tn(bg, xdt * jnp.exp(at - a))
            yd = []
            for hh in (0, 1):
                h = 2 * j + hh
                dec = jnp.exp(jnp.minimum(acs_b[:, h * LANES:(h + 1) * LANES] - acst_s[pl.ds(h, 1), :], 0.0))
                yd.append(_dot_nn(jnp.where(causal, cb * dec, 0.0), xdt))
            y_ref[:, sl] = y + jnp.where(lane < HEAD, yd[0], yd[1])

    one = pl.BlockSpec((1, LANES), lambda i: (0, 0))
    return pl.pallas_call(
        body, name=name, grid=(nc,),
        in_specs=[pl.BlockSpec((CHUNK, 8 * LANES), lambda i: (i, 0)),
                  pl.BlockSpec((CHUNK, LANES), lambda i: (i, 8)),
                  pl.BlockSpec((CHUNK, LANES), lambda i: (i, 9)),
                  pl.BlockSpec((CHUNK, LANES), lambda i: (i, C_DT // LANES)), one, one, one],
        out_specs=[pl.BlockSpec((CHUNK, 8 * LANES), lambda i: (i, 0)),
                   pl.BlockSpec((1, SSD_PAIRS, LANES, LANES), lambda i: (i, 0, 0, 0))],
        out_shape=[jax.ShapeDtypeStruct((t, 8 * LANES), F32),
                   jax.ShapeDtypeStruct((nc, SSD_PAIRS, LANES, LANES), F32)],
        scratch_shapes=[pltpu.VMEM((CHUNK, 8 * LANES), F32), pltpu.VMEM((LANES, CHUNK), F32),
                        pltpu.VMEM((SSD_PAIRS, LANES, LANES), F32)],
        compiler_params=_params(("arbitrary",)),
    )(xc, xc, xc, proj, dtb, alog, dsk)


def _ssd_bwd(xc, proj, dtb, alog, dsk, hin_all, dy, *, name):
    t = xc.shape[0]
    nc = t // CHUNK

    def body(x_ref, b_ref, c_ref, dtr_ref, dtb_ref, alog_ref, dsk_ref, hin_ref, dy_ref,
             dxc_ref, ddtr_ref, ddtb_ref, dalog_ref, ddsk_ref, acsx_s, acst_s, dh_s, dax_s, ddx_s):
        @pl.when(pl.program_id(0) == 0)
        def _():
            dh_s[...] = jnp.zeros_like(dh_s)
            ddtb_ref[...] = jnp.zeros_like(ddtb_ref)
            dalog_ref[...] = jnp.zeros_like(dalog_ref)
            ddsk_ref[...] = jnp.zeros_like(ddsk_ref)

        dtr = dtr_ref[...]
        dtb = dtb_ref[...]
        dt, arow, dt_x, acs_b, e, tril, triu = _ssd_common(dtr, dtb, alog_ref[...], acsx_s, acst_s)
        dsk_x = _dot3(jnp.broadcast_to(dsk_ref[...], (CHUNK, LANES)), e)
        lane = lax.broadcasted_iota(jnp.int32, (CHUNK, LANES), 1)
        rowi = lax.broadcasted_iota(jnp.int32, (CHUNK, LANES), 0)
        causal = (lax.broadcasted_iota(jnp.int32, (CHUNK, CHUNK), 0)
                  >= lax.broadcasted_iota(jnp.int32, (CHUNK, CHUNK), 1))
        acs_rows = jnp.zeros((CHUNK, LANES), F32)
        acs_cols = jnp.zeros((LANES, CHUNK), F32)
        d_b = jnp.zeros((CHUNK, LANES), F32)
        d_c = jnp.zeros((CHUNK, LANES), F32)
        for j in range(SSD_PAIRS):
            g = j // 4
            sl = slice(j * LANES, (j + 1) * LANES)
            if j % 4 == 0:
                bg = jnp.where(lane // HEAD == g, b_ref[...], 0.0)
                cg = jnp.where(lane // HEAD == g, c_ref[...], 0.0)
                cb = _dot_nt(cg, bg)
                dcb = jnp.zeros((CHUNK, CHUNK), F32)
            x = x_ref[:, sl]
            d = dt_x[:, sl]
            a = acsx_s[:, sl]
            at = acsx_s[CHUNK - 1:CHUNK, sl]
            xdt = x * d
            hin = hin_ref[0, j]
            dhout = dh_s[j]
            dyp = dy_ref[:, sl]
            ea, eat, ed = jnp.exp(a), jnp.exp(at), jnp.exp(at - a)
            da_l = dyp * ea * _dot_nn(cg, hin)
            dm = dyp * ea
            d_c = d_c + _dot_nt(dm, hin)
            dh_s[j] = _dot_tn(cg, dm) + eat * dhout
            dat = jnp.sum(dhout * hin * eat, axis=0, keepdims=True)
            d_b = d_b + _dot_nt(xdt * ed, dhout)
            dw = _dot_nn(bg, dhout)
            dxdt = dw * ed
            ded = dw * xdt * ed
            dat = dat + jnp.sum(ded, axis=0, keepdims=True)
            da_l = da_l - ded
            for hh in (0, 1):
                h = 2 * j + hh
                dec = jnp.exp(jnp.minimum(acs_b[:, h * LANES:(h + 1) * LANES] - acst_s[pl.ds(h, 1), :], 0.0))
                gm = jnp.where(causal, cb * dec, 0.0)
                dyh = jnp.where(lane // HEAD == hh, dyp, 0.0)
                dg = _dot_nt(dyh, xdt)
                dxdt = dxdt + _dot_tn(gm, dyh)
                dcb = dcb + jnp.where(causal, dg * dec, 0.0)
                th = dg * gm
                acs_rows = acs_rows + jnp.where(lane == h, jnp.sum(th, axis=1, keepdims=True), 0.0)
                acs_cols = acs_cols + jnp.where(rowi == h, jnp.sum(th, axis=0, keepdims=True), 0.0)
            if j % 4 == 3:
                d_c = d_c + _dot_nn(dcb, bg)
                d_b = d_b + _dot_tn(dcb, cg)
            dxc_ref[:, sl] = dyp * dsk_x[:, sl] + dxdt * d
            ddx_s[:, sl] = dxdt * x
            dax_s[:, sl] = da_l + jnp.where(rowi == CHUNK - 1, dat, 0.0)
            dskp = jnp.sum(dyp * x, axis=0, keepdims=True)
            ddsk_ref[...] += _dot2(jnp.broadcast_to(dskp, (8, LANES)), e[:, sl], _NT)
        dxc_ref[:, 8 * LANES:9 * LANES] = d_b
        dxc_ref[:, 9 * LANES:10 * LANES] = d_c
        dacs = acs_rows - acs_cols.T + _dot2(dax_s[...], e, _NT)
        ddt = _dot2(ddx_s[...], e, _NT)
        dda = _mdot3(triu, dacs)
        ddt = ddt + dda * arow
        dalog_ref[...] += jnp.sum(dda * dt, axis=0, keepdims=True) * arow
        ddtr = jnp.where(lane < SSD_HEADS, ddt * _sigmoid(dtr + dtb), 0.0)
        ddtr_ref[...] = ddtr
        ddtb_ref[...] += jnp.sum(ddtr, axis=0, keepdims=True)

    one = pl.BlockSpec((1, LANES), lambda i: (0, 0))
    rev = lambda c: (lambda i: (nc - 1 - i, c))
    return pl.pallas_call(
        body, name=name, grid=(nc,),
        in_specs=[pl.BlockSpec((CHUNK, 8 * LANES), rev(0)), pl.BlockSpec((CHUNK, LANES), rev(8)),
                  pl.BlockSpec((CHUNK, LANES), rev(9)), pl.BlockSpec((CHUNK, LANES), rev(C_DT // LANES)),
                  one, one, one,
                  pl.BlockSpec((1, SSD_PAIRS, LANES, LANES), lambda i: (nc - 1 - i, 0, 0, 0)),
                  pl.BlockSpec((CHUNK, 8 * LANES), rev(0))],
        out_specs=[pl.BlockSpec((CHUNK, XBC_COLS), rev(0)), pl.BlockSpec((CHUNK, LANES), rev(0)), one, one,
                   pl.BlockSpec((8, LANES), lambda i: (0, 0))],
        out_shape=[jax.ShapeDtypeStruct((t, XBC_COLS), F32), jax.ShapeDtypeStruct((t, LANES), F32)]
        + [jax.ShapeDtypeStruct((1, LANES), F32)] * 2 + [jax.ShapeDtypeStruct((8, LANES), F32)],
        scratch_shapes=[pltpu.VMEM((CHUNK, 8 * LANES), F32), pltpu.VMEM((LANES, CHUNK), F32),
                        pltpu.VMEM((SSD_PAIRS, LANES, LANES), F32),
                        pltpu.VMEM((CHUNK, 8 * LANES), F32), pltpu.VMEM((CHUNK, 8 * LANES), F32)],
        compiler_params=_params(("arbitrary",)),
    )(xc, xc, xc, proj, dtb, alog, dsk, hin_all, dy)


RW_C = 64


def _p3(a, b, dn):
    ah, al = _split2(a)
    bh, bl = _split2(b)
    d = lambda x, y: lax.dot_general(x, y, dn, preferred_element_type=F32)
    return d(ah, bh) + d(ah, bl) + d(al, bh)


_BNN = (((2,), (1,)), ((0,), (0,)))
_BNT = (((2,), (2,)), ((0,), (0,)))
_BTN = (((1,), (1,)), ((0,), (0,)))


@jax.custom_vjp
def _pnn(a, b):
    return _p3(a, b, _BNN)


@jax.custom_vjp
def _pnt(a, b):
    return _p3(a, b, _BNT)


@jax.custom_vjp
def _ptn(a, b):
    return _p3(a, b, _BTN)


_pnn.defvjp(lambda a, b: (_p3(a, b, _BNN), (a, b)), lambda res, g: (_p3(g, res[1], _BNT), _p3(res[0], g, _BTN)))
_pnt.defvjp(lambda a, b: (_p3(a, b, _BNT), (a, b)), lambda res, g: (_p3(g, res[1], _BNN), _p3(g, res[0], _BTN)))
_ptn.defvjp(lambda a, b: (_p3(a, b, _BTN), (a, b)), lambda res, g: (_p3(res[1], g, _BNT), _p3(res[0], g, _BNN)))


def _tri2(tril, x, dn):
    hi, lo = _split2(x)
    m = tril.astype(BF16)
    return (lax.dot_general(m, hi, dn, preferred_element_type=F32) + lax.dot_general(m, lo, dn, preferred_element_type=F32))


@jax.custom_vjp
def _cumsum_rows(tril, x):
    return _tri2(tril, x, _BNN)


_cumsum_rows.defvjp(lambda tril, x: (_tri2(tril, x, _BNN), tril),
                    lambda tril, g: (jnp.zeros_like(tril), _tri2(tril, g, _BTN)))


def _rw_chunk_consts():
    c2 = 2 * RW_C
    row = lax.broadcasted_iota(jnp.int32, (c2, c2), 0)
    col = lax.broadcasted_iota(jnp.int32, (c2, c2), 1)
    same = (row // RW_C) == (col // RW_C)
    strict = (same & (row > col)).astype(F32)
    incl = (same & (row >= col)).astype(F32)
    eye = (row == col).astype(F32)
    tr = lax.broadcasted_iota(jnp.int32, (RW_C, RW_C), 0)
    tc = lax.broadcasted_iota(jnp.int32, (RW_C, RW_C), 1)
    tril = (tr >= tc).astype(F32)
    lane = lax.broadcasted_iota(jnp.int32, (1, LANES), 1)
    hm = [(lane // HEAD == h).astype(F32) for h in (0, 1)]
    return strict, incl, eye, tril, hm


def _rw_chunk(r, lw, k, v, n, b, s2, consts):
    strict, incl, eye, tril, hm = consts
    two = lambda x: jnp.concatenate([x * hm[0], x * hm[1]], axis=1)
    cum = _cumsum_rows(jnp.broadcast_to(tril, (4, RW_C, RW_C)), lw)
    grow, shrink = jnp.exp(-cum), jnp.exp(cum)
    n2, r2 = two(n * jnp.exp(cum - lw)), two(r * shrink)
    b2, k2, v2 = two(b * grow), two(k * grow), two(v)
    p = _pnt(n2, b2) * strict
    x2 = _pnt(n2, s2) + _pnn(_pnt(n2, k2) * strict, v2)
    t_inv, a = eye + p, p
    for _ in range(RW_C.bit_length() - 2):
        a = _pnn(a, a)
        t_inv = t_inv + _pnn(t_inv, a)
    u2 = _pnn(t_inv, x2)
    y2 = _pnt(r2, s2) + _pnn(_pnt(r2, b2) * incl, u2) + _pnn(_pnt(r2, k2) * incl, v2)
    s2_new = (s2 + _ptn(u2, b2) + _ptn(v2, k2)) * jnp.exp(jnp.sum(lw, axis=1, keepdims=True))
    return jnp.sum(y2.reshape(4, 2, RW_C, LANES), axis=1), s2_new


def _pairs(ref):
    return jnp.stack([ref[:, p * LANES:(p + 1) * LANES] for p in range(4)])


def _rw_chunk_fwd(mixed, lw, k, n, b, *, name, side=None):
    t = lw.shape[0]
    nc = t // RW_C

    def body(r_ref, v_ref, lw_ref, k_ref, n_ref, b_ref, y_ref, sin_ref, s_s):
        @pl.when(pl.program_id(0) == 0)
        def _():
            s_s[...] = jnp.zeros_like(s_s)

        s2 = s_s[...]
        sin_ref[0] = s2
        y, s2 = _rw_chunk(*[_pairs(x) for x in (r_ref, lw_ref, k_ref, v_ref, n_ref, b_ref)], s2, _rw_chunk_consts())
        for p in range(4):
            y_ref[:, p * LANES:(p + 1) * LANES] = y[p]
        s_s[...] = s2

    blk = lambda c: pl.BlockSpec((RW_C, 4 * LANES), functools.partial(lambda i, c: (i, c), c=c))
    return _call_with_side(
        body, side, name=name, grid=(nc,), semantics=("arbitrary",),
        in_specs=[blk(0), blk(2), blk(0), blk(0), blk(0), blk(0)],
        out_specs=[blk(0), pl.BlockSpec((1, 4, LANES, LANES), lambda i: (i, 0, 0, 0))],
        out_shape=[jax.ShapeDtypeStruct((t, 4 * LANES), F32), jax.ShapeDtypeStruct((nc, 4, LANES, LANES), F32)],
        scratch_shapes=[pltpu.VMEM((4, LANES, LANES), F32)],
        operands=(mixed, mixed, lw, k, n, b))


def _call_with_side(body, side, *, name, grid, semantics, in_specs, out_specs, out_shape, scratch_shapes, operands):
    if side is None:
        return pl.pallas_call(body, name=name, grid=grid, in_specs=in_specs, out_specs=out_specs, out_shape=out_shape,
                              scratch_shapes=scratch_shapes, compiler_params=_params(semantics))(*operands)
    srcs, per_dest = side
    ns, ni, no, nscr = len(srcs), len(in_specs), len(out_specs), len(scratch_shapes)

    def full_body(*refs):
        ins, side_in = refs[:ni], refs[ni:ni + ns]
        outs, side_out = refs[ni + ns:ni + ns + no], refs[ni + ns + no:ni + 2 * ns + no]
        scratch, sems = refs[ni + 2 * ns + no:ni + 2 * ns + no + nscr], refs[ni + 2 * ns + no + nscr:]

        ids = [pl.program_id(a) for a in range(len(grid))]
        first = functools.reduce(jnp.logical_and, [i == 0 for i in ids])
        last = functools.reduce(jnp.logical_and, [i == n - 1 for i, n in zip(ids, grid)])

        @pl.when(first)
        def _():
            _exchange(side_in, side_out, sems, per_dest, start=True, wait=False)

        body(*ins, *outs, *scratch)

        @pl.when(last)
        def _():
            _exchange(side_in, side_out, sems, per_dest, start=False, wait=True)

    res = pl.pallas_call(
        full_body, name=name, grid=grid, in_specs=list(in_specs) + [_ANY] * ns,
        out_specs=list(out_specs) + [_ANY] * ns, out_shape=list(out_shape) + _exchange_out_shapes(srcs, per_dest),
        scratch_shapes=list(scratch_shapes) + _exchange_sems(ns), compiler_params=_params(("arbitrary",) * len(grid)),
    )(*operands, *srcs)
    return list(res[:no]) + [list(res[no:])]


def _rw_chunk_bwd(mixed, lw, k, n, b, s_in, dy, dr0, dk0, dv0, *, name, side=None):
    t = lw.shape[0]
    nc = t // RW_C

    def body(r_ref, v_ref, lw_ref, k_ref, n_ref, b_ref, sin_ref, dy_ref, dr0_ref, dk0_ref, dv0_ref,
             dr_ref, dlw_ref, dk_ref, dv_ref, dn_ref, db_ref, ds_s):
        @pl.when(pl.program_id(0) == 0)
        def _():
            ds_s[...] = jnp.zeros_like(ds_s)

        consts = _rw_chunk_consts()
        args = [_pairs(x) for x in (r_ref, lw_ref, k_ref, v_ref, n_ref, b_ref)] + [sin_ref[0]]
        _, vjp = jax.vjp(lambda *a: _rw_chunk(*a, consts), *args)
        dr, dlw, dk, dv, dn, db, ds = vjp((_pairs(dy_ref), ds_s[...]))
        for p in range(4):
            sl = slice(p * LANES, (p + 1) * LANES)
            dr_ref[:, sl] = dr[p] + dr0_ref[:, sl]
            dlw_ref[:, sl] = dlw[p]
            dk_ref[:, sl] = dk[p] + dk0_ref[:, sl]
            dv_ref[:, sl] = dv[p] + dv0_ref[:, sl]
            dn_ref[:, sl] = dn[p]
            db_ref[:, sl] = db[p]
        ds_s[...] = ds

    blk = lambda c: pl.BlockSpec((RW_C, 4 * LANES), functools.partial(lambda i, c: (nc - 1 - i, c), c=c))
    return _call_with_side(
        body, side, name=name, grid=(nc,), semantics=("arbitrary",),
        in_specs=[blk(0), blk(2), blk(0), blk(0), blk(0), blk(0),
                  pl.BlockSpec((1, 4, LANES, LANES), lambda i: (nc - 1 - i, 0, 0, 0)), blk(0), blk(0), blk(0), blk(0)],
        out_specs=[blk(0)] * 6,
        out_shape=[jax.ShapeDtypeStruct((t, 4 * LANES), F32)] * 6,
        scratch_shapes=[pltpu.VMEM((4, LANES, LANES), F32)],
        operands=(mixed, mixed, lw, k, n, b, s_in, dy, dr0, dk0, dv0))


def _f_rms_res(x, g):
    return _f_rms(x, g)[0], x


def _final(x, g, target, *, bt, name):
    t, d = x.shape

    def body(x_ref, g_ref, t_ref, dx_ref, loss_ref, dg_ref):
        tgt = t_ref[...]

        def f(xv, gv):
            err = _f_rms(xv, gv)[0] - tgt
            return 0.5 * jnp.mean(err * err, axis=-1, keepdims=True)

        row_loss, vjp = jax.vjp(f, x_ref[...], g_ref[...])
        dx, dg = vjp(jnp.ones_like(row_loss))
        dx_ref[...] = dx

        @pl.when(pl.program_id(0) == 0)
        def _():
            loss_ref[...] = jnp.zeros_like(loss_ref)
            dg_ref[...] = jnp.zeros_like(dg_ref)

        loss_ref[...] += jnp.broadcast_to(jnp.sum(row_loss, axis=0, keepdims=True), (1, LANES))
        dg_ref[...] += dg

    blk = pl.BlockSpec((bt, d), lambda i: (i, 0))
    return pl.pallas_call(
        body, name=name, grid=(t // bt,),
        in_specs=[blk, pl.BlockSpec((1, d), lambda i: (0, 0)), blk],
        out_specs=[blk, pl.BlockSpec((1, LANES), lambda i: (0, 0)), pl.BlockSpec((1, d), lambda i: (0, 0))],
        out_shape=[jax.ShapeDtypeStruct((t, d), F32), jax.ShapeDtypeStruct((1, LANES), F32),
                   jax.ShapeDtypeStruct((1, d), F32)],
        compiler_params=_params(("arbitrary",)),
    )(x, g, target)


ADAMW_BLOCK_BYTES = 1 << 20


def _adamw(w, g, m, v, *, name, block=None):
    shape = w.shape
    if block is not None:
        return _adamw_blocks(w, g, m, v, block, name)
    c = shape[-1]
    shape3 = (1,) * (3 - len(shape)) + shape if len(shape) <= 3 else (-1,) + shape[-2:]
    args = [a.reshape(shape3) for a in (w, g, m, v)]
    lead, r, _ = args[0].shape
    br = r
    if r * c * 4 > ADAMW_BLOCK_BYTES:
        cands = [b for b in range(8, r, 8) if r % b == 0 and b * c * 4 <= ADAMW_BLOCK_BYTES]
        br = max(cands) if cands else r
    outs = _adamw_blocks(*args, (1, br, c), name)
    return tuple(o.reshape(shape) for o in outs)


def _adamw_blocks(w, g, m, v, block, name):
    shape = w.shape
    assert all(s % b == 0 for s, b in zip(shape, block))

    def body(w_ref, g_ref, m_ref, v_ref, d_ref, nm_ref, nv_ref):
        gv = g_ref[...]
        m_new = ADAM_B1 * m_ref[...] + (1.0 - ADAM_B1) * gv
        v_new = ADAM_B2 * v_ref[...] + (1.0 - ADAM_B2) * (gv * gv)
        m_hat = m_new / (1.0 - ADAM_B1 ** ADAM_STEP)
        v_hat = v_new / (1.0 - ADAM_B2 ** ADAM_STEP)
        d_ref[...] = -ADAM_LR * (m_hat / (jnp.sqrt(v_hat) + ADAM_EPS) + ADAM_WD * w_ref[...])
        nm_ref[...] = m_new
        nv_ref[...] = v_new

    blk = pl.BlockSpec(tuple(block), lambda *ids: ids)
    return pl.pallas_call(
        body, name=name, grid=tuple(s // b for s, b in zip(shape, block)), in_specs=[blk] * 4, out_specs=[blk] * 3,
        out_shape=[jax.ShapeDtypeStruct(shape, F32)] * 3,
        compiler_params=_params(("parallel",) * len(shape)),
    )(w, g, m, v)


BT = 256
BC = 128


def _layer_rows(x, proj, s):
    s = {k: s.get(k) for k in ("y_sb_raw", "y_ssd_raw", "mixed", "ys", "k2", "p_sb", "p_ssd", "p_rw")}
    return dict(
        rms=[(x, D_MODEL, 0)],
        sb_gate=[(s["y_sb_raw"], 512, 0), (proj, 512, 3)],
        ssd_norm=[(s["y_ssd_raw"], 1024, 0), (proj, 1024, C_Z // 1024)],
        rw_pre=[(s["mixed"], 512, 1), (s["mixed"], LANES, 16)],
        rw_post=[(s["ys"], 512, 0), (s["mixed"], 512, 0), (s["k2"], 512, 0), (s["mixed"], 512, 2), (s["mixed"], 512, 3)],
        merge=[(s["p_sb"], 1024, 0), (s["p_ssd"], 1024, 0), (s["p_rw"], 1024, 0),
               (proj, 1024, 3), (proj, 1024, 4), (proj, 1024, 5)],
    )


def _layer_fwd(x, p, nm, side=None):
    s = {}
    (s["h"],) = _rowwise(_f_rms, [(x, D_MODEL, 0)], [p["norm_g"]], [D_MODEL], bt=BT, name=nm + "rms")
    proj = s["proj"] = _mm(s["h"], p["w_in"], name=nm + "proj")
    s["y_sb_raw"], s["lt"] = _sb2_fwd(proj, name=nm + "sb")
    s["xc"] = _colwise(_f_conv, proj, C_XBC, XBC_COLS, p["conv"], bc=BC, name=nm + "conv")
    s["y_ssd_raw"], s["hin"] = _ssd_fwd(s["xc"], proj, p["dt_bias"], p["a_log"], p["d_skip"], name=nm + "ssd")
    s["mixed"] = _colwise(_f_rw_mix, proj, C_RW, RW_COLS, [p["rw_mu"]], bc=BC, name=nm + "mix")
    s["w"], s["k2"], s["n"], s["b"] = _rowwise(_f_rw_pre, [(s["mixed"], 512, 1), (s["mixed"], LANES, 16)], p["rw_pre"],
                                               [512] * 4, bt=BT, name=nm + "rwpre")
    s["ys"], s["st"], *exchanged = _rw_chunk_fwd(s["mixed"], s["w"], s["k2"], s["n"], s["b"], name=nm + "scan", side=side)
    rows = _layer_rows(x, proj, s)
    (s["y_sb"],) = _rowwise(_f_sb_gate, rows["sb_gate"], [], [512], bt=BT, name=nm + "sbgate")
    (s["y_ssd"],) = _rowwise(_f_ssd_norm, rows["ssd_norm"], [p["ssd_norm_g"]], [1024], bt=BT, name=nm + "ssdnorm")
    (s["y_rw"],) = _rowwise(_f_rw_post, rows["rw_post"], p["rw_post"], [512], bt=BT, name=nm + "rwpost")
    s["p_sb"] = _mm(s["y_sb"], p["w_out_sb"], name=nm + "osb")
    s["p_ssd"] = _mm(s["y_ssd"], p["w_out_ssd"], name=nm + "ossd")
    s["p_rw"] = _mm(s["y_rw"], p["w_out_rw"], name=nm + "orw")
    (s["merged"],) = _rowwise(_f_merge, _layer_rows(x, proj, s)["merge"], [], [1024], bt=BT, name=nm + "merge")
    return _mm(s["merged"], p["w_o"], add=x, name=nm + "wo"), s, (exchanged[0] if exchanged else None)


def _layer_bwd(x, dx_out, p, s, nm, side=None, side_late=None):
    g = {}
    proj = s["proj"]
    rows = _layer_rows(x, proj, s)
    g["w_o"] = _mm(s["merged"], dx_out, ta=True, name=nm + "g_wo")
    d_merged = _mm(dx_out, p["w_o"], tb=True, name=nm + "d_merged")
    dp_sb, dp_ssd, dp_rw, d_gates = _rowwise_bwd(_f_merge, rows["merge"], [], [(d_merged, 1024, 0)], bt=BT,
                                                 name=nm + "merge_b", groups=[[0], [1], [2], [3, 4, 5]])
    g["w_out_sb"] = _mm(s["y_sb"], dp_sb, ta=True, name=nm + "g_osb")
    g["w_out_ssd"] = _mm(s["y_ssd"], dp_ssd, ta=True, name=nm + "g_ossd")
    g["w_out_rw"] = _mm(s["y_rw"], dp_rw, ta=True, name=nm + "g_orw")
    dy_sb = _mm(dp_sb, p["w_out_sb"], tb=True, name=nm + "d_ysb")
    dy_ssd = _mm(dp_ssd, p["w_out_ssd"], tb=True, name=nm + "d_yssd")
    dy_rw = _mm(dp_rw, p["w_out_rw"], tb=True, name=nm + "d_yrw")
    dy_sb_raw, d_sbgate = _rowwise_bwd(_f_sb_gate, rows["sb_gate"], [], [(dy_sb, 512, 0)], bt=BT, name=nm + "sbgate_b")
    dq, dk, dv = _sb2_bwd(proj, dy_sb_raw, s["lt"], name=nm + "sb_b")
    dy_ssd_raw, dz, g["ssd_norm_g"] = _rowwise_bwd(_f_ssd_norm, rows["ssd_norm"], [p["ssd_norm_g"]],
                                                   [(dy_ssd, 1024, 0)], bt=BT, name=nm + "ssdnorm_b")
    dxc, ddtr, g["dt_bias"], g["a_log"], g["d_skip"] = _ssd_bwd(
        s["xc"], proj, p["dt_bias"], p["a_log"], p["d_skip"], s["hin"], dy_ssd_raw, name=nm + "ssd_b")
    conv_out = _colwise_bwd(_f_conv, proj, C_XBC, XBC_COLS, p["conv"], dxc, bc=BC, name=nm + "conv_b")
    dxbc, g["conv"] = conv_out[0], conv_out[1:]
    dys, dr0, dk0, dv0, d_rwgate, g["rw_ln_g"], g["rw_ln_b"], g["rw_r_k"] = _rowwise_bwd(
        _f_rw_post, rows["rw_post"], p["rw_post"], [(dy_rw, 512, 0)], bt=BT, name=nm + "rwpost_b")
    dr, dw, dk2, dvv, dn, db, *exchanged = _rw_chunk_bwd(s["mixed"], s["w"], s["k2"], s["n"], s["b"], s["st"], dys,
                                                         dr0, dk0, dv0, name=nm + "scan_b",
                                                         side=side(g) if side else None)
    pre_out = _rowwise_bwd(_f_rw_pre, rows["rw_pre"], p["rw_pre"],
                           [(dw, 512, 0), (dk2, 512, 0), (dn, 512, 0), (db, 512, 0)], bt=BT, name=nm + "rwpre_b")
    dkm, dlo, g["rw_pre"] = pre_out[0], pre_out[1], pre_out[2:]
    d_mixed = jnp.concatenate([dr, dkm, dvv, d_rwgate, dlo], axis=1)
    d_slab, g["rw_mu"] = _colwise_bwd(_f_rw_mix, proj, C_RW, RW_COLS, [p["rw_mu"]], d_mixed, bc=BC, name=nm + "mix_b")
    d_proj = jnp.concatenate([dq, dk, dv, d_sbgate, dz, d_gates, d_slab, ddtr, dxbc], axis=1)
    g["w_in"] = _mm(s["h"], d_proj, ta=True, name=nm + "g_win")
    dh = _mm(d_proj, p["w_in"], tb=True, tn=1024, tk=512, name=nm + "d_h", side=side_late(g) if side_late else None)
    dh, late = dh if side_late else (dh, None)
    dx, g["norm_g"] = _rowwise_bwd(_f_rms_res, rows["rms"], [p["norm_g"]], [(dh, D_MODEL, 0), (dx_out, D_MODEL, 0)],
                                   bt=BT, name=nm + "rms_b")
    return dx, g, (exchanged[0] if exchanged else None), late


MESH = pl.DeviceIdType.MESH
N_DEV = 8
_ANY = pl.BlockSpec(memory_space=pl.ANY)


def _here():
    x, y, c = lax.axis_index("x"), lax.axis_index("y"), lax.axis_index("c")
    return x, y, c, [(1 - x, y), (x, 1 - y), (1 - x, 1 - y)]


def _chip_exchange(srcs, *, per_dest, name):
    n = len(srcs)

    def body(*refs):
        _exchange(refs[:n], refs[n:2 * n], refs[2 * n:], per_dest, start=True, wait=True)

    return pl.pallas_call(
        body, name=name, in_specs=[_ANY] * n, out_specs=[_ANY] * n,
        out_shape=_exchange_out_shapes(srcs, per_dest), scratch_shapes=_exchange_sems(n),
    )(*srcs)


def _by_layer(per_dest):
    return per_dest is not True and per_dest is not False


def _exchange_out_shapes(srcs, per_dest):
    lead = (4, 2) if _by_layer(per_dest) else (4,)
    return [jax.ShapeDtypeStruct(lead + s.shape[-2:], s.dtype) for s in srcs]


def _exchange_sems(n):
    return [pltpu.SemaphoreType.DMA((3 * n,)), pltpu.SemaphoreType.DMA((3 * n,)), pltpu.SemaphoreType.DMA((n,))]


def _exchange(src_refs, out_refs, sems, per_dest, *, start, wait):
    send_sems, recv_sems, local_sems = sems
    x, y, c, chips = _here()
    me = 2 * x + y
    owns, sends, recvs = [], [], []
    for a, (src_ref, out_ref) in enumerate(zip(src_refs, out_refs)):
        if per_dest is True:
            pick = lambda q, s=src_ref: s.at[q]
        elif per_dest is False:
            pick = lambda q, s=src_ref: s.at[c]
        else:
            pick = lambda q, s=src_ref: s.at[per_dest].at[c]
        any_block = src_ref.at[0] if len(src_ref.shape) == 3 else src_ref.at[0].at[0]
        if _by_layer(per_dest):
            slot = lambda q, o=out_ref: o.at[q].at[c]
        else:
            slot = lambda q, o=out_ref: o.at[q]
        owns.append(pltpu.make_async_copy(pick(me), slot(me), local_sems.at[a]))
        for j, (px, py) in enumerate(chips):
            sends.append(pltpu.make_async_remote_copy(
                pick(2 * px + py), slot(me), send_sems.at[3 * a + j], recv_sems.at[3 * a + j],
                device_id=(px, py, c), device_id_type=MESH))
            recvs.append(pltpu.make_async_remote_copy(
                any_block, slot(2 * px + py), send_sems.at[3 * a + j], recv_sems.at[3 * a + j],
                device_id=(px, py, c), device_id_type=MESH))
    if start:
        for cp in owns + sends:
            cp.start()
    if wait:
        for cp in recvs:
            cp.wait_recv()
        for cp in sends:
            cp.wait_send()
        for cp in owns:
            cp.wait()


def _sibling_fill(bufs, *, name):
    n = len(bufs)

    def body(*refs):
        in_refs, out_refs, send_sems, recv_sems = refs[:n], refs[n:2 * n], refs[2 * n], refs[2 * n + 1]
        x, y, c, _ = _here()
        copies = []
        for a, (src, dst) in enumerate(zip(in_refs, out_refs)):
            for q in range(4):
                copies.append(pltpu.make_async_remote_copy(
                    src.at[q].at[c], dst.at[q].at[c], send_sems.at[4 * a + q], recv_sems.at[4 * a + q],
                    device_id=(x, y, 1 - c), device_id_type=MESH))
        for cp in copies:
            cp.start()
        for a, (src, dst) in enumerate(zip(in_refs, out_refs)):
            for q in range(4):
                pltpu.make_async_remote_copy(
                    src.at[q].at[c], dst.at[q].at[1 - c], send_sems.at[4 * a + q], recv_sems.at[4 * a + q],
                    device_id=(x, y, 1 - c), device_id_type=MESH).wait_recv()
        for cp in copies:
            cp.wait_send()

    return pl.pallas_call(
        body, name=name, in_specs=[_ANY] * n, out_specs=[_ANY] * n,
        out_shape=[jax.ShapeDtypeStruct(b.shape, b.dtype) for b in bufs],
        input_output_aliases={a: a for a in range(n)},
        scratch_shapes=[pltpu.SemaphoreType.DMA((4 * n,)), pltpu.SemaphoreType.DMA((4 * n,))],
    )(*bufs)


def _sibling_swap(srcs, *, other_slot, name):
    n = len(srcs)

    def body(*refs):
        src_refs, out_refs, send_sems, recv_sems = refs[:n], refs[n:2 * n], refs[2 * n], refs[2 * n + 1]
        x, y, c, _ = _here()
        copies = [pltpu.make_async_remote_copy(s.at[1 - c] if other_slot else s, o, send_sems.at[a], recv_sems.at[a],
                                               device_id=(x, y, 1 - c), device_id_type=MESH)
                  for a, (s, o) in enumerate(zip(src_refs, out_refs))]
        for cp in copies:
            cp.start()
        for cp in copies:
            cp.wait()

    return pl.pallas_call(
        body, name=name, in_specs=[_ANY] * n, out_specs=[_ANY] * n,
        out_shape=[jax.ShapeDtypeStruct(s.shape[1:] if other_slot else s.shape, s.dtype) for s in srcs],
        scratch_shapes=[pltpu.SemaphoreType.DMA((n,)), pltpu.SemaphoreType.DMA((n,))],
    )(*srcs)


def _allgather_small(v, *, reduce, name):
    r = v.shape[0]

    def body(v_ref, out_ref, *rest):
        send_sems, recv_sems, local_sem = rest[-3:]
        x, y, c, chips = _here()
        me, sibling = (x, y, c), (x, y, 1 - c)

        def slot(px, py, pc):
            return out_ref.at[4 * px + 2 * py + pc]

        def copy(k, block, to, src=None):
            return pltpu.make_async_remote_copy(
                src_ref=slot(*block) if src is None else src, dst_ref=slot(*block),
                send_sem=send_sems.at[k], recv_sem=recv_sems.at[k], device_id=to, device_id_type=MESH)

        mine = pltpu.make_async_copy(v_ref, slot(*me), local_sem)
        mine.start()
        first = [copy(0, me, sibling, src=v_ref)]
        first += [copy(1 + j, me, (*chip, c), src=v_ref) for j, chip in enumerate(chips)]
        for cp in first:
            cp.start()
        passed = [copy(4 + j, (*chip, c), sibling) for j, chip in enumerate(chips)]
        for j, chip in enumerate(chips):
            copy(1 + j, (*chip, c), me).wait_recv()
            passed[j].start()
        copy(0, sibling, me).wait_recv()
        for j, chip in enumerate(chips):
            copy(4 + j, (*chip, 1 - c), me).wait_recv()
        for cp in first + passed:
            cp.wait_send()
        mine.wait()
        if reduce:
            total = out_ref[0]
            for d in range(1, N_DEV):
                total = total + out_ref[d]
            rest[0][...] = total

    vm = pl.BlockSpec(memory_space=pltpu.VMEM)
    out_shape = [jax.ShapeDtypeStruct((N_DEV, r, LANES), F32)] + ([jax.ShapeDtypeStruct((r, LANES), F32)] if reduce else [])
    return pl.pallas_call(
        body, name=name, in_specs=[vm], out_specs=[vm] * len(out_shape), out_shape=out_shape,
        scratch_shapes=[pltpu.SemaphoreType.DMA((7,)), pltpu.SemaphoreType.DMA((7,)), pltpu.SemaphoreType.DMA],
        compiler_params=pltpu.CompilerParams(vmem_limit_bytes=VMEM_LIMIT),
    )(v)


REDUCE_BLOCK_BYTES = 2 << 20


def _reduce_rows(r, c):
    cands = [b for b in range(16, r + 1, 16) if r % b == 0 and b * c * 4 <= REDUCE_BLOCK_BYTES]
    return max(cands)


def _add_halves(mine2, other, c_idx, *, name):
    _, nq, r, c = mine2.shape
    br = _reduce_rows(r, c)

    def body(c_ref, a_ref, b_ref, o_ref):
        o_ref[...] = (a_ref[0] + b_ref[...]).astype(o_ref.dtype)

    blk = pl.BlockSpec((1, br, c), lambda q, i, c_ref: (q, i, 0))
    return pl.pallas_call(
        body, name=name,
        grid_spec=pltpu.PrefetchScalarGridSpec(
            num_scalar_prefetch=1, grid=(nq, r // br),
            in_specs=[pl.BlockSpec((1, 1, br, c), lambda q, i, c_ref: (c_ref[0], q, i, 0)), blk],
            out_specs=blk),
        out_shape=jax.ShapeDtypeStruct((nq, r, c), BF16),
        compiler_params=_params(("parallel", "parallel")),
    )(c_idx, mine2, other)


def _sum_chips(parts, *, name):
    _, r, c = parts.shape
    br = _reduce_rows(r, c)

    def body(p_ref, o_ref):
        total = p_ref[0].astype(F32)
        for q in range(1, 4):
            total = total + p_ref[q].astype(F32)
        o_ref[...] = total

    return pl.pallas_call(
        body, name=name, grid=(r // br,),
        in_specs=[pl.BlockSpec((4, br, c), lambda i: (0, i, 0))],
        out_specs=pl.BlockSpec((br, c), lambda i: (i, 0)),
        out_shape=jax.ShapeDtypeStruct((r, c), F32),
        compiler_params=_params(("parallel",)),
    )(parts)


BIG = ("w_in", "w_out_sb", "w_out_ssd", "w_out_rw", "w_o")
BIG_AXIS = {"w_in": 2, "w_out_sb": 2, "w_out_ssd": 1, "w_out_rw": 2, "w_o": 1}
SMALL_SHARDED = {"conv_w": 320, "rw_w_up": 128, "rw_a_up": 128}
SMALL = ("norm_g", "conv_w", "conv_b", "dt_bias", "a_log", "d_skip", "ssd_norm_g", "rw_mu", "rw_w0", "rw_w_up",
         "rw_a0", "rw_a_up", "rw_k_k", "rw_k_a", "rw_r_k", "rw_ln_g", "rw_ln_b", "final_g")


def _rows_of(a):
    flat = a.reshape(-1)
    pad = (-flat.shape[0]) % LANES
    return jnp.pad(flat, (0, pad)).reshape(-1, LANES)


def _pack_rows(arrays, multiple=8):
    rows = jnp.concatenate([_rows_of(a) for a in arrays], axis=0)
    pad = (-rows.shape[0]) % multiple
    return jnp.pad(rows, ((0, pad), (0, 0)))


def _unpack_rows(rows, shapes):
    out, off = [], 0
    for shp in shapes:
        n = 1
        for d in shp:
            n *= d
        nr = -(-n // LANES)
        out.append(rows[off:off + nr].reshape(-1)[:n].reshape(shp))
        off += nr
    return out


COL_MAP = ((0, 3072, 0), (3072, 4352, C_XBC), (4352, 4368, C_DT), (4368, 6544, C_RW), (6544, 9616, C_GATES))
SHARD_COLS = N_IN // 4


def _w_in_from_shards(shards):
    pieces = []
    for a, b, dst in sorted(COL_MAP, key=lambda m: m[2]):
        if pieces and dst > pieces[-1][0]:
            pieces.append((dst, jnp.zeros((shards[0].shape[0], dst - pieces[-1][0]), shards[0].dtype)))
        for q in range(4):
            lo, hi = max(a, q * SHARD_COLS), min(b, (q + 1) * SHARD_COLS)
            if lo < hi:
                pieces.append((dst + hi - a, shards[q][:, lo - q * SHARD_COLS:hi - q * SHARD_COLS]))
    return jnp.concatenate([p for _, p in pieces], axis=1)


def _w_in_shard(g, q):
    pieces = []
    for a, b, dst in COL_MAP:
        lo, hi = max(a, q * SHARD_COLS), min(b, (q + 1) * SHARD_COLS)
        if lo < hi:
            pieces.append(g[:, dst + lo - a:dst + hi - a])
    return jnp.concatenate(pieces, axis=1)


def _row_halves(a):
    return a.reshape(2, a.shape[0] // 2, a.shape[1])


def _join_halves(core, mine, theirs):
    return jnp.where(core == 0, jnp.concatenate([mine, theirs], axis=-2), jnp.concatenate([theirs, mine], axis=-2))


def kernel(x, norm_g, w_in, conv_w, conv_b, dt_bias, a_log, d_skip, ssd_norm_g, rw_mu, rw_w0, rw_w_up, rw_a0, rw_a_up, rw_k_k, rw_k_a, rw_r_k, rw_ln_g, rw_ln_b, w_out_sb, w_out_ssd, w_out_rw, w_o, final_g, loss_target, m_norm_g, m_w_in, m_conv_w, m_conv_b, m_dt_bias, m_a_log, m_d_skip, m_ssd_norm_g, m_rw_mu, m_rw_w0, m_rw_w_up, m_rw_a0, m_rw_a_up, m_rw_k_k, m_rw_k_a, m_rw_r_k, m_rw_ln_g, m_rw_ln_b, m_w_out_sb, m_w_out_ssd, m_w_out_rw, m_w_o, m_final_g, v_norm_g, v_w_in, v_conv_w, v_conv_b, v_dt_bias, v_a_log, v_d_skip, v_ssd_norm_g, v_rw_mu, v_rw_w0, v_rw_w_up, v_rw_a0, v_rw_a_up, v_rw_k_k, v_rw_k_a, v_rw_r_k, v_rw_ln_g, v_rw_ln_b, v_w_out_sb, v_w_out_ssd, v_w_out_rw, v_w_o, v_final_g):
    names = ("norm_g", "w_in", "conv_w", "conv_b", "dt_bias", "a_log", "d_skip", "ssd_norm_g", "rw_mu", "rw_w0",
             "rw_w_up", "rw_a0", "rw_a_up", "rw_k_k", "rw_k_a", "rw_r_k", "rw_ln_g", "rw_ln_b", "w_out_sb",
             "w_out_ssd", "w_out_rw", "w_o", "final_g")
    w_loc = dict(zip(names, (norm_g, w_in, conv_w, conv_b, dt_bias, a_log, d_skip, ssd_norm_g, rw_mu, rw_w0, rw_w_up,
                             rw_a0, rw_a_up, rw_k_k, rw_k_a, rw_r_k, rw_ln_g, rw_ln_b, w_out_sb, w_out_ssd, w_out_rw,
                             w_o, final_g)))
    m_loc = dict(zip(names, (m_norm_g, m_w_in, m_conv_w, m_conv_b, m_dt_bias, m_a_log, m_d_skip, m_ssd_norm_g,
                             m_rw_mu, m_rw_w0, m_rw_w_up, m_rw_a0, m_rw_a_up, m_rw_k_k, m_rw_k_a, m_rw_r_k,
                             m_rw_ln_g, m_rw_ln_b, m_w_out_sb, m_w_out_ssd, m_w_out_rw, m_w_o, m_final_g)))
    v_loc = dict(zip(names, (v_norm_g, v_w_in, v_conv_w, v_conv_b, v_dt_bias, v_a_log, v_d_skip, v_ssd_norm_g,
                             v_rw_mu, v_rw_w0, v_rw_w_up, v_rw_a0, v_rw_a_up, v_rw_k_k, v_rw_k_a, v_rw_r_k,
                             v_rw_ln_g, v_rw_ln_b, v_w_out_sb, v_w_out_ssd, v_w_out_rw, v_w_o, v_final_g)))
    chip = 2 * lax.axis_index("x") + lax.axis_index("y")
    core = lax.axis_index("c")

    as_sent = [w_loc[n].astype(BF16).reshape(DEPTH, 2, w_loc[n].shape[1] // 2, w_loc[n].shape[2]) for n in BIG]

    def gathered(mine, nm):
        out = {}
        for n, buf in zip(BIG, _sibling_fill(mine, name=nm)):
            shards = buf.reshape(4, 2 * buf.shape[2], buf.shape[3])
            out[n] = (_w_in_from_shards([shards[q] for q in range(4)]) if n == "w_in"
                      else jnp.concatenate([shards[q] for q in range(4)], axis=BIG_AXIS[n] - 1))
        return out

    full = {}
    sm_names = tuple(SMALL_SHARDED)
    sm_shapes = [w_loc[n].shape for n in sm_names]
    (got_sm,) = _allgather_small(_pack_rows([w_loc[n] for n in sm_names]), reduce=False, name="gather_small")
    per_chip = [_unpack_rows(got_sm[4 * (q // 2) + 2 * (q % 2)], sm_shapes) for q in range(4)]
    for i, n in enumerate(sm_names):
        full[n] = jnp.concatenate([per_chip[q][i] for q in range(4)], axis=-1)

    def pad16(a):
        return jnp.zeros((1, LANES), F32).at[0, :SSD_HEADS].set(a)

    def layer_params(i, big):
        row = lambda n: w_loc[n][i].reshape(1, -1)
        cw = full["conv_w"][i]
        return dict(
            norm_g=row("norm_g"), w_in=big["w_in"], conv=[cw[k][None] for k in range(4)] + [row("conv_b")],
            dt_bias=pad16(dt_bias[i]), a_log=pad16(a_log[i]), d_skip=pad16(d_skip[i]),
            ssd_norm_g=row("ssd_norm_g"), rw_mu=row("rw_mu"),
            rw_pre=[row("rw_w0"), jnp.zeros((LANES, 512), F32).at[:HEAD].set(full["rw_w_up"][i]), row("rw_a0"),
                    jnp.zeros((LANES, 512), F32).at[HEAD:].set(full["rw_a_up"][i]), row("rw_k_k"), row("rw_k_a")],
            rw_post=[row("rw_ln_g"), row("rw_ln_b"), row("rw_r_k")],
            w_out_sb=big["w_out_sb"], w_out_ssd=big["w_out_ssd"], w_out_rw=big["w_out_rw"], w_o=big["w_o"])

    c_idx = core.reshape(1).astype(jnp.int32)

    def reduce_prepare(items, nm):
        sends = []
        for g, n, _ in items:
            per_chip = ([_w_in_shard(g[n], q) for q in range(4)] if n == "w_in"
                        else jnp.split(g[n], 4, axis=BIG_AXIS[n] - 1))
            sends.append(jnp.stack([_row_halves(p) for p in per_chip], axis=1))
        others = _sibling_swap(sends, other_slot=True, name=nm + "sibling")
        return [_add_halves(s, o, c_idx, name=nm + "add_" + lab) for (_, _, lab), s, o in zip(items, sends, others)]

    def reduce_finish(exchanged, labels, nm):
        mine = [_sum_chips(p, name=nm + "sum_" + lab) for lab, p in zip(labels, exchanged)]
        theirs = _sibling_swap(mine, other_slot=False, name=nm + "join")
        return {lab: _join_halves(core, a, b) for lab, a, b in zip(labels, mine, theirs)}

    assert DEPTH == 2
    out_proj = BIG[1:]
    params, xs, saved, grads = [None] * 2, [x[0], None, None], [None] * 2, [None] * 2
    params[0] = layer_params(0, gathered(_chip_exchange(as_sent, per_dest=0, name="gather_l0"), "gather_l0_join"))
    xs[1], saved[0], got = _layer_fwd(xs[0], params[0], "l0_", side=(as_sent, 1))
    params[1] = layer_params(1, gathered(got, "gather_l1_join"))
    xs[2], saved[1], _ = _layer_fwd(xs[1], params[1], "l1_")
    dx, loss_row, g_final = _final(xs[2], final_g.reshape(1, -1), loss_target[0], bt=BT, name="final")
    dx, grads[1], _, _ = _layer_bwd(xs[1], dx, params[1], saved[1], "l1_")
    early = lambda g: [(grads[1], n, "l1_" + n) for n in BIG] + [(g, n, "l0_" + n) for n in out_proj]
    dx, grads[0], got, got_late = _layer_bwd(
        xs[0], dx, params[0], saved[0], "l0_",
        side=lambda g: (reduce_prepare(early(g), "reduce_early_"), True),
        side_late=lambda g: (reduce_prepare([(g, "w_in", "l0_w_in")], "reduce_late_"), True))
    total = reduce_finish(got + got_late, [lab for _, _, lab in early(None)] + ["l0_w_in"], "reduce_")
    totals = [{n: total[f"l{i}_" + n] for n in BIG} for i in range(DEPTH)]

    def stacked(fn):
        return jnp.stack([fn(grads[i]) for i in range(DEPTH)])

    g_loc = {
        "norm_g": stacked(lambda g: g["norm_g"][0]),
        "conv_w": stacked(lambda g: jnp.concatenate(g["conv"][:4], axis=0)),
        "conv_b": stacked(lambda g: g["conv"][4][0]),
        "dt_bias": stacked(lambda g: g["dt_bias"][0, :SSD_HEADS]),
        "a_log": stacked(lambda g: g["a_log"][0, :SSD_HEADS]),
        "d_skip": stacked(lambda g: g["d_skip"][0, :SSD_HEADS]),
        "ssd_norm_g": stacked(lambda g: g["ssd_norm_g"][0]),
        "rw_mu": stacked(lambda g: g["rw_mu"][0]),
        "rw_w0": stacked(lambda g: g["rw_pre"][0][0]),
        "rw_w_up": stacked(lambda g: g["rw_pre"][1][:HEAD]),
        "rw_a0": stacked(lambda g: g["rw_pre"][2][0]),
        "rw_a_up": stacked(lambda g: g["rw_pre"][3][HEAD:]),
        "rw_k_k": stacked(lambda g: g["rw_pre"][4][0]),
        "rw_k_a": stacked(lambda g: g["rw_pre"][5][0]),
        "rw_r_k": stacked(lambda g: g["rw_r_k"].reshape(8, HEAD)),
        "rw_ln_g": stacked(lambda g: g["rw_ln_g"][0]),
        "rw_ln_b": stacked(lambda g: g["rw_ln_b"][0]),
        "final_g": g_final[0],
    }

    g_out = {n: jnp.stack([totals[0][n], totals[1][n]]) for n in BIG}

    sm_all = SMALL + ("loss",)
    sm_full_shapes = [g_loc[n].shape for n in SMALL] + [(1,)]
    _, summed = _allgather_small(_pack_rows([g_loc[n] for n in SMALL] + [loss_row[0, :1]]), reduce=True, name="reduce_small")
    sm = dict(zip(sm_all, _unpack_rows(summed, sm_full_shapes)))
    for n in SMALL:
        g_out[n] = sm[n]
    for n, wd in SMALL_SHARDED.items():
        g_out[n] = lax.dynamic_slice_in_dim(sm[n], chip * wd, wd, axis=sm[n].ndim - 1)
    loss = sm["loss"][0]

    upd = {n: _adamw(w_loc[n], g_out[n], m_loc[n], v_loc[n], name="adamw_" + n) for n in names if n != "w_in"}
    cols = SHARD_COLS // 4
    to_cols = lambda a: jnp.transpose(a, (2, 0, 1)).reshape(4, cols, DEPTH, D_MODEL)
    from_cols = lambda a: jnp.transpose(a.reshape(SHARD_COLS, DEPTH, D_MODEL), (1, 2, 0))
    g_cols = lax.optimization_barrier(to_cols(g_out["w_in"]))
    g_out["w_in"] = from_cols(g_cols)
    upd["w_in"] = tuple(from_cols(a) for a in _adamw(
        to_cols(w_loc["w_in"]), g_cols, to_cols(m_loc["w_in"]), to_cols(v_loc["w_in"]),
        name="adamw_w_in", block=(1, cols, DEPTH, D_MODEL // 2)))
    return (loss, dx[None], *[g_out[n] for n in names], *[upd[n][0] for n in names],
            *[upd[n][1] for n in names], *[upd[n][2] for n in names])
```

```python
import functools

import jax
import jax.numpy as jnp
from jax import lax
from jax.experimental import pallas as pl
from jax.experimental.pallas import tpu as pltpu

F32 = jnp.float32
BF16 = jnp.bfloat16

D_MODEL = 1024
DEPTH = 2
HEAD = 64
LANES = 128
CHUNK = 128
RMS_EPS = 1e-6
GN_EPS = 64e-5
VMEM_LIMIT = 56 * 1024 * 1024

N_IN = 9616
N_PAD = 9728
C_SB, C_Z, C_GATES, C_RW, C_LO, C_DT, C_XBC = 0, 2048, 3072, 6144, 8192, 8320, 8448
RW_COLS = 2176
XBC_COLS = 1280

ADAM_LR, ADAM_B1, ADAM_B2, ADAM_EPS, ADAM_WD, ADAM_STEP = 0.001, 0.9, 0.999, 1e-08, 0.01, 10


def _params(sem=None):
    return pltpu.CompilerParams(dimension_semantics=sem, vmem_limit_bytes=VMEM_LIMIT)


@jax.custom_vjp
def _sigmoid(x):
    return 1.0 / (1.0 + jnp.exp(-x))


def _sigmoid_fwd(x):
    s = _sigmoid(x)
    return s, s


def _sigmoid_bwd(s, g):
    return (g * s * (1.0 - s),)


_sigmoid.defvjp(_sigmoid_fwd, _sigmoid_bwd)


@jax.custom_vjp
def _silu(x):
    return x * _sigmoid(x)


def _silu_fwd(x):
    s = _sigmoid(x)
    return x * s, (x, s)


def _silu_bwd(res, g):
    x, s = res
    return (g * (s + x * s * (1.0 - s)),)


_silu.defvjp(_silu_fwd, _silu_bwd)


@jax.custom_vjp
def _softplus(x):
    return jnp.maximum(x, 0.0) + jnp.log(1.0 + jnp.exp(-jnp.abs(x)))


def _softplus_fwd(x):
    return _softplus(x), x


def _softplus_bwd(x, g):
    return (g * _sigmoid(x),)


_softplus.defvjp(_softplus_fwd, _softplus_bwd)


def _dot(a, b, dims):
    return lax.dot_general(a.astype(BF16), b.astype(BF16), (dims, ((), ())), preferred_element_type=F32)


def _dot_nn(a, b):
    return _dot(a, b, ((1,), (0,)))


def _dot_nt(a, b):
    return _dot(a, b, ((1,), (1,)))


def _dot_tn(a, b):
    return _dot(a, b, ((0,), (0,)))


@jax.custom_vjp
def _bdot(a, b):
    return _dot_nn(a, b)


def _bdot_fwd(a, b):
    return _dot_nn(a, b), (a, b)


def _bdot_bwd(res, g):
    a, b = res
    return _dot_nt(g, b), _dot_tn(a, g)


_bdot.defvjp(_bdot_fwd, _bdot_bwd)


def _split2(x):
    hi = x.astype(BF16)
    lo = (x - hi.astype(F32)).astype(BF16)
    return hi, lo


_NT = (((1,), (1,)), ((), ()))
_NN = (((1,), (0,)), ((), ()))
_TN = (((0,), (0,)), ((), ()))


def _dot2(x, m, dn=_NN):
    hi, lo = _split2(x)
    return (lax.dot_general(hi, m, dn, preferred_element_type=F32)
            + lax.dot_general(lo, m, dn, preferred_element_type=F32))


def _seg_matrix(n):
    r = lax.broadcasted_iota(jnp.int32, (n, n), 0) // HEAD
    c = lax.broadcasted_iota(jnp.int32, (n, n), 1) // HEAD
    return (r == c).astype(BF16)


@jax.custom_vjp
def _segsum2(x, seg):
    return _dot2(x, seg)


def _segsum2_fwd(x, seg):
    return _dot2(x, seg), seg


def _segsum2_bwd(seg, g):
    return _dot2(g, seg), jnp.zeros_like(seg)


_segsum2.defvjp(_segsum2_fwd, _segsum2_bwd)


def _make_segsum(seg):
    return lambda x: _segsum2(x, seg)


def _shift_down_raw(x, k):
    row = lax.broadcasted_iota(jnp.int32, x.shape, 0)
    return jnp.where(row >= k, pltpu.roll(x, k, 0), 0.0)


def _shift_up_raw(x, k):
    t = x.shape[0]
    row = lax.broadcasted_iota(jnp.int32, x.shape, 0)
    return jnp.where(row < t - k, pltpu.roll(x, t - k, 0), 0.0)


@functools.partial(jax.custom_vjp, nondiff_argnums=(1,))
def _shift_down(x, k):
    return _shift_down_raw(x, k)


def _shift_down_fwd(x, k):
    return _shift_down_raw(x, k), None


def _shift_down_bwd(k, _, g):
    return (_shift_up_raw(g, k),)


_shift_down.defvjp(_shift_down_fwd, _shift_down_bwd)


def _mm(a, b, *, name, ta=False, tb=False, add=None, out_dtype=F32, tm=2048, tn=512, tk=None, side=None):
    m, k = (a.shape[1], a.shape[0]) if ta else a.shape
    n = b.shape[0] if tb else b.shape[1]
    tm, tn = min(tm, m), min(tn, n)
    tk = k if tk is None else tk
    nk = k // tk
    assert m % tm == 0 and n % tn == 0 and k % tk == 0
    dims = ((0 if ta else 1,), (1 if tb else 0,))

    def body(a_ref, b_ref, *refs):
        o_ref, acc_ref = refs[-2:]
        p = _dot(a_ref[...], b_ref[...], dims)

        def emit(total):
            if add is not None:
                total = total + refs[0][...]
            o_ref[...] = total.astype(o_ref.dtype)

        if nk == 1:
            emit(p)
        else:
            kk = pl.program_id(2)

            @pl.when(kk == 0)
            def _():
                acc_ref[...] = p

            @pl.when(kk > 0)
            def _():
                acc_ref[...] += p

            @pl.when(kk == nk - 1)
            def _():
                emit(acc_ref[...])

    a_spec = pl.BlockSpec((tk, tm), lambda i, j, kk: (kk, i)) if ta else pl.BlockSpec((tm, tk), lambda i, j, kk: (i, kk))
    b_spec = pl.BlockSpec((tn, tk), lambda i, j, kk: (j, kk)) if tb else pl.BlockSpec((tk, tn), lambda i, j, kk: (kk, j))
    o_spec = pl.BlockSpec((tm, tn), lambda i, j, kk: (i, j))
    res = _call_with_side(
        body, side, name=name, grid=(m // tm, n // tn, nk), semantics=("parallel", "parallel", "arbitrary"),
        in_specs=[a_spec, b_spec] + ([o_spec] if add is not None else []), out_specs=[o_spec],
        out_shape=[jax.ShapeDtypeStruct((m, n), out_dtype)],
        scratch_shapes=[pltpu.VMEM((tm, tn) if nk > 1 else (8, LANES), F32)],
        operands=(a, b) + ((add,) if add is not None else ()))
    return res[0] if side is None else (res[0], res[1])


def _row_specs(rows, bt):
    return [pl.BlockSpec((bt, w), functools.partial(lambda i, c: (i, c), c=c)) for _, w, c in rows]


def _full_spec(p):
    return pl.BlockSpec(p.shape, functools.partial(lambda i, nd: (0,) * nd, nd=p.ndim))


def _rowwise(f, rows, pars, out_widths, *, bt, name, acc_widths=()):
    t = rows[0][0].shape[0]
    nr, npar, no, na = len(rows), len(pars), len(out_widths), len(acc_widths)

    def body(*refs):
        vals = [r[...] for r in refs[:nr + npar]]
        outs = f(*vals)
        for o_ref, o in zip(refs[nr + npar:nr + npar + no], outs[:no]):
            o_ref[...] = o.astype(o_ref.dtype)
        if na:
            first = pl.program_id(0) == 0
            for a_ref, a in zip(refs[nr + npar + no:], outs[no:]):
                @pl.when(first)
                def _():
                    a_ref[...] = jnp.zeros_like(a_ref)
                a_ref[...] += a

    return pl.pallas_call(
        body, name=name, grid=(t // bt,),
        in_specs=_row_specs(rows, bt) + [_full_spec(p) for p in pars],
        out_specs=[pl.BlockSpec((bt, w), lambda i: (i, 0)) for w in out_widths]
        + [pl.BlockSpec((1, w), lambda i: (0, 0)) for w in acc_widths],
        out_shape=[jax.ShapeDtypeStruct((t, w), F32) for w in out_widths]
        + [jax.ShapeDtypeStruct((1, w), F32) for w in acc_widths],
        compiler_params=_params(("arbitrary",)),
    )(*[r[0] for r in rows], *pars)


def _rowwise_bwd(f, rows, pars, douts, *, bt, name, groups=None):
    t = rows[0][0].shape[0]
    nr, npar, nd = len(rows), len(pars), len(douts)
    groups = [[i] for i in range(nr)] if groups is None else groups
    widths = [r[1] for r in rows]

    def body(*refs):
        vals = [r[...] for r in refs[:nr + npar]]
        cts = tuple(r[...] for r in refs[nr + npar:nr + npar + nd])
        _, vjp = jax.vjp(lambda *a: tuple(f(*a)), *vals)
        grads = vjp(cts)
        out_refs = refs[nr + npar + nd:]
        for g_ref, grp in zip(out_refs[:len(groups)], groups):
            off = 0
            for i in grp:
                g_ref[:, off:off + widths[i]] = grads[i]
                off += widths[i]
        first = pl.program_id(0) == 0
        for p_ref, g in zip(out_refs[len(groups):], grads[nr:]):
            @pl.when(first)
            def _():
                p_ref[...] = jnp.zeros_like(p_ref)
            p_ref[...] += g

    gw = [sum(widths[i] for i in grp) for grp in groups]
    return pl.pallas_call(
        body, name=name, grid=(t // bt,),
        in_specs=_row_specs(rows, bt) + [_full_spec(p) for p in pars] + _row_specs(douts, bt),
        out_specs=[pl.BlockSpec((bt, w), lambda i: (i, 0)) for w in gw] + [_full_spec(p) for p in pars],
        out_shape=[jax.ShapeDtypeStruct((t, w), F32) for w in gw] + [jax.ShapeDtypeStruct(p.shape, F32) for p in pars],
        compiler_params=_params(("arbitrary",)),
    )(*[r[0] for r in rows], *pars, *[d[0] for d in douts])


def _colwise(f, x, c0, ncols, pars, *, bc, name):
    t = x.shape[0]

    def body(x_ref, *refs):
        o_ref = refs[-1]
        o_ref[...] = f(x_ref[...], *[r[...] for r in refs[:-1]])

    return pl.pallas_call(
        body, name=name, grid=(ncols // bc,),
        in_specs=[pl.BlockSpec((t, bc), lambda j: (0, j + c0 // bc))]
        + [pl.BlockSpec((p.shape[0], bc), lambda j: (0, j)) for p in pars],
        out_specs=pl.BlockSpec((t, bc), lambda j: (0, j)),
        out_shape=jax.ShapeDtypeStruct((t, ncols), F32),
        compiler_params=_params(("parallel",)),
    )(x, *pars)


def _colwise_bwd(f, x, c0, ncols, pars, dout, *, bc, name):
    t = x.shape[0]
    npar = len(pars)

    def body(x_ref, *refs):
        vals = [x_ref[...]] + [r[...] for r in refs[:npar]]
        _, vjp = jax.vjp(f, *vals)
        grads = vjp(refs[npar][...])
        for g_ref, g in zip(refs[npar + 1:], grads):
            g_ref[...] = g

    return pl.pallas_call(
        body, name=name, grid=(ncols // bc,),
        in_specs=[pl.BlockSpec((t, bc), lambda j: (0, j + c0 // bc))]
        + [pl.BlockSpec((p.shape[0], bc), lambda j: (0, j)) for p in pars]
        + [pl.BlockSpec((t, bc), lambda j: (0, j))],
        out_specs=[pl.BlockSpec((t, bc), lambda j: (0, j))]
        + [pl.BlockSpec((p.shape[0], bc), lambda j: (0, j)) for p in pars],
        out_shape=[jax.ShapeDtypeStruct((t, ncols), F32)] + [jax.ShapeDtypeStruct(p.shape, F32) for p in pars],
        compiler_params=_params(("parallel",)),
    )(x, *pars, dout)


def _f_rms(x, g):
    return (x * lax.rsqrt(jnp.mean(x * x, axis=-1, keepdims=True) + RMS_EPS) * g,)


def _f_sb_gate(y, gate):
    return (y * _silu(gate),)


def _f_ssd_norm(y, z, g):
    u = y * _silu(z)
    return (u * lax.rsqrt(jnp.mean(u * u, axis=-1, keepdims=True) + RMS_EPS) * g,)


def _f_merge(p_sb, p_ssd, p_rw, g_sb, g_ssd, g_rw):
    return (_sigmoid(g_sb) * p_sb + _sigmoid(g_ssd) * p_ssd + _sigmoid(g_rw) * p_rw,)


def _f_rw_pre(k, lo, w0, w_up, a0, a_up, k_k, k_a):
    segsum = _make_segsum(_seg_matrix(k.shape[1]))
    lane = lax.broadcasted_iota(jnp.int32, lo.shape, 1)
    w_lo = jnp.where(lane < HEAD, jnp.tanh(lo), 0.0)
    a_lo = jnp.where(lane >= HEAD, lo, 0.0)
    w = -_softplus(-(w0 + _bdot(w_lo, w_up))) - 0.5
    log_decay = -jnp.exp(w)
    a = _sigmoid(a0 + _bdot(a_lo, a_up))
    kk = k * k_k
    kk = kk / jnp.maximum(jnp.sqrt(segsum(kk * kk)), 1e-12)
    return log_decay, k * (1.0 + (a - 1.0) * k_a), -kk, kk * a


def _f_rw_post(y, r, k2, v, gate, ln_g, ln_b, r_k):
    segsum = _make_segsum(_seg_matrix(y.shape[1]))
    yc = y - segsum(y) * (1.0 / HEAD)
    var = segsum(yc * yc) * (1.0 / HEAD)
    yn = yc * lax.rsqrt(var + GN_EPS) * ln_g + ln_b
    return ((yn + segsum(r * k2 * r_k) * v) * _silu(gate),)


def _f_rw_mix(slab, mu):
    return slab + (_shift_down(slab, 1) - slab) * mu


def _f_conv(x, w0, w1, w2, w3, b):
    acc = x * w3 + b
    for i, w in enumerate((w0, w1, w2)):
        acc = acc + _shift_down(x, 3 - i) * w
    return _silu(acc)


def _log_sigmoid(z):
    return jnp.minimum(z, 0.0) - jnp.log(1.0 + jnp.exp(-jnp.abs(z)))


SB_BQ = 256
SB_BK = 256
assert SB_BQ == SB_BK


def _tri_ones(kind):
    j = lax.broadcasted_iota(jnp.int32, (SB_BK, SB_BK + LANES), 0)
    s = lax.broadcasted_iota(jnp.int32, (SB_BK, SB_BK + LANES), 1)
    tri = {"gt": j > s, "le": j <= s, "lt": j < s}[kind]
    return (tri | (s >= SB_BK)).astype(BF16)


def _sb_common(q_ref):
    lane = lax.broadcasted_iota(jnp.int32, (SB_BQ, LANES), 1)
    q = q_ref[...] * (HEAD ** -0.5)
    q2 = jnp.concatenate([jnp.where(lane < HEAD, q, 0.0), jnp.where(lane >= HEAD, q, 0.0)], axis=0).astype(BF16)
    diff = (lax.broadcasted_iota(jnp.int32, (2 * SB_BQ, SB_BK), 1)
            - (lax.broadcasted_iota(jnp.int32, (2 * SB_BQ, SB_BK), 0) & (SB_BQ - 1)))
    return lane, q2, diff


def _rep(x):
    return jnp.concatenate([x] * (SB_BK // LANES), axis=1)


def _sb2_specs(t):
    q = pl.BlockSpec((SB_BQ, LANES), lambda j, i: (i, j))
    k = pl.BlockSpec((t, LANES), lambda j, i: (0, 4 + j))
    v = pl.BlockSpec((t, LANES), lambda j, i: (0, 8 + j))
    return q, k, v


def _sb2_fwd(proj, *, name):
    t = proj.shape[0]

    def body(q_ref, k_ref, v_ref, y_ref, lt_ref):
        i = pl.program_id(1)
        lane, q2, diff = _sb_common(q_ref)
        m_f = _tri_ones("gt")

        def step(kb, carry, diagonal):
            c, acc = carry
            off = pl.multiple_of(kb * SB_BK, SB_BK)
            kblk = k_ref[pl.ds(off, SB_BK), :].astype(BF16)
            vblk = v_ref[pl.ds(off, SB_BK), :].astype(BF16)
            z = lax.dot_general(q2, kblk, _NT, preferred_element_type=F32)
            lb = _log_sigmoid(z)
            lk = jnp.where(diff < 0, lb - z, 0.0) if diagonal else lb - z
            w2 = _dot2(lk, m_f)
            att = jnp.exp(lb + _rep(c) + w2[:, :SB_BK])
            if diagonal:
                att = jnp.where(diff < 0, att, 0.0)
            acc = acc + lax.dot_general(att.astype(BF16), vblk, _NN, preferred_element_type=F32)
            return c + w2[:, SB_BK:], acc

        zero = jnp.zeros((2 * SB_BQ, LANES), F32)
        c, acc = lax.fori_loop(0, i, lambda it, carry: step(i - 1 - it, carry, False), step(i, (zero, zero), True))
        y_ref[...] = jnp.where(lane < HEAD, acc[:SB_BQ], acc[SB_BQ:])
        lt_ref[0] = c[:SB_BQ]
        lt_ref[1] = c[SB_BQ:]

    return pl.pallas_call(
        body, name=name, grid=(4, t // SB_BQ),
        in_specs=list(_sb2_specs(t)),
        out_specs=[pl.BlockSpec((SB_BQ, LANES), lambda j, i: (i, j)),
                   pl.BlockSpec((2, SB_BQ, LANES), lambda j, i: (j, i, 0))],
        out_shape=[jax.ShapeDtypeStruct((t, 4 * LANES), F32), jax.ShapeDtypeStruct((8, t, LANES), F32)],
        compiler_params=_params(("parallel", "arbitrary")),
    )(proj, proj, proj)


def _sb2_bwd(proj, dy, lt, *, name):
    t = proj.shape[0]

    def body(q_ref, k_ref, v_ref, dy_ref, lt_ref, dq_ref, dk_ref, dv_ref):
        i = pl.program_id(1)

        @pl.when(i == 0)
        def _():
            dk_ref[...] = jnp.zeros_like(dk_ref)
            dv_ref[...] = jnp.zeros_like(dv_ref)

        lane, q2, diff = _sb_common(q_ref)
        m_le, m_lt = _tri_ones("le"), _tri_ones("lt")
        dy_blk = dy_ref[...]
        do2 = jnp.concatenate([jnp.where(lane < HEAD, dy_blk, 0.0), jnp.where(lane >= HEAD, dy_blk, 0.0)],
                              axis=0).astype(BF16)
        lt2 = jnp.concatenate([lt_ref[0], lt_ref[1]], axis=0)

        def step(kb, carry, diagonal):
            cp, cg, dq = carry
            off = pl.multiple_of(kb * SB_BK, SB_BK)
            kblk = k_ref[pl.ds(off, SB_BK), :].astype(BF16)
            vblk = v_ref[pl.ds(off, SB_BK), :].astype(BF16)
            z = lax.dot_general(q2, kblk, _NT, preferred_element_type=F32)
            lb = _log_sigmoid(z)
            lk = jnp.where(diff < 0, lb - z, 0.0) if diagonal else lb - z
            w2 = _dot2(lk, m_le)
            att = jnp.exp(lb + _rep(lt2 - cp) - w2[:, :SB_BK])
            if diagonal:
                att = jnp.where(diff < 0, att, 0.0)
            d_e = lax.dot_general(do2, vblk, _NT, preferred_element_type=F32) * att
            g2 = _dot2(d_e, m_lt)
            sig = jnp.exp(lb)
            dz = d_e * (1.0 - sig) - (_rep(cg) + g2[:, :SB_BK]) * sig
            dz = (jnp.where(diff < 0, dz, 0.0) if diagonal else dz).astype(BF16)
            dq = dq + lax.dot_general(dz, kblk, _NN, preferred_element_type=F32)
            dk_ref[pl.ds(off, SB_BK), :] += lax.dot_general(dz, q2, _TN, preferred_element_type=F32)
            dv_ref[pl.ds(off, SB_BK), :] += lax.dot_general(att.astype(BF16), do2, _TN, preferred_element_type=F32)
            return cp + w2[:, SB_BK:], cg + g2[:, SB_BK:], dq

        zero = jnp.zeros((2 * SB_BQ, LANES), F32)
        before = lax.fori_loop(0, i, lambda kb, carry: step(kb, carry, False), (zero, zero, zero))
        _, _, dq = step(i, before, True)
        dq_ref[...] = jnp.where(lane < HEAD, dq[:SB_BQ], dq[SB_BQ:]) * (HEAD ** -0.5)

    q_spec, k_spec, v_spec = _sb2_specs(t)
    blk = pl.BlockSpec((SB_BQ, LANES), lambda j, i: (i, j))
    col = pl.BlockSpec((t, LANES), lambda j, i: (0, j))
    return pl.pallas_call(
        body, name=name, grid=(4, t // SB_BQ),
        in_specs=[q_spec, k_spec, v_spec, blk, pl.BlockSpec((2, SB_BQ, LANES), lambda j, i: (j, i, 0))],
        out_specs=[blk, col, col],
        out_shape=[jax.ShapeDtypeStruct((t, 4 * LANES), F32)] * 3,
        compiler_params=_params(("parallel", "arbitrary")),
    )(proj, proj, proj, dy, lt)


SSD_HEADS = 16
SSD_PAIRS = 8


def _split3(x):
    a = x.astype(BF16)
    r = x - a.astype(F32)
    b = r.astype(BF16)
    return a, b, (r - b.astype(F32)).astype(BF16)


def _dot3(x, m, dn=_NN):
    return sum(lax.dot_general(p, m, dn, preferred_element_type=F32) for p in _split3(x))


def _mdot3(m, x):
    return sum(lax.dot_general(m, p, _NN, preferred_element_type=F32) for p in _split3(x))


def _ssd_common(dtr, dtb, alog, acsx_s, acst_s):
    lane = lax.broadcasted_iota(jnp.int32, (CHUNK, LANES), 1)
    lane1 = lax.broadcasted_iota(jnp.int32, (1, LANES), 1)
    arow = jnp.where(lane1 < SSD_HEADS, -jnp.exp(alog), 0.0)
    dt = jnp.where(lane < SSD_HEADS, _softplus(dtr + dtb), 0.0)
    da = dt * arow
    r = lax.broadcasted_iota(jnp.int32, (CHUNK, CHUNK), 0)
    c = lax.broadcasted_iota(jnp.int32, (CHUNK, CHUNK), 1)
    tril = (r >= c).astype(BF16)
    triu = (r <= c).astype(BF16)
    acs = _mdot3(tril, da)
    acst_s[...] = _dot3(da, triu, _TN)
    eh = lax.broadcasted_iota(jnp.int32, (LANES, 8 * LANES), 0)
    e = (eh == lax.broadcasted_iota(jnp.int32, (LANES, 8 * LANES), 1) // HEAD).astype(BF16)
    eh2 = lax.broadcasted_iota(jnp.int32, (LANES, 16 * LANES), 0)
    e2 = (eh2 == lax.broadcasted_iota(jnp.int32, (LANES, 16 * LANES), 1) // LANES).astype(BF16)
    acsx_s[...] = _dot3(acs, e)
    return dt, arow, _dot3(dt, e), _dot3(acs, e2), e, tril, triu


def _ssd_fwd(xc, proj, dtb, alog, dsk, *, name):
    t = xc.shape[0]
    nc = t // CHUNK

    def body(x_ref, b_ref, c_ref, dtr_ref, dtb_ref, alog_ref, dsk_ref, y_ref, hin_ref, acsx_s, acst_s, h_s):
        @pl.when(pl.program_id(0) == 0)
        def _():
            h_s[...] = jnp.zeros_like(h_s)

        dt, arow, dt_x, acs_b, e, tril, _ = _ssd_common(dtr_ref[...], dtb_ref[...], alog_ref[...], acsx_s, acst_s)
        dsk_x = _dot3(jnp.broadcast_to(dsk_ref[...], (CHUNK, LANES)), e)
        lane = lax.broadcasted_iota(jnp.int32, (CHUNK, LANES), 1)
        causal = (lax.broadcasted_iota(jnp.int32, (CHUNK, CHUNK), 0)
                  >= lax.broadcasted_iota(jnp.int32, (CHUNK, CHUNK), 1))
        for j in range(SSD_PAIRS):
            g = j // 4
            sl = slice(j * LANES, (j + 1) * LANES)
            if j % 4 == 0:
                bg = jnp.where(lane // HEAD == g, b_ref[...], 0.0)
                cg = jnp.where(lane // HEAD == g, c_ref[...], 0.0)
                cb = _dot_nt(cg, bg)
            x = x_ref[:, sl]
            a = acsx_s[:, sl]
            at = acsx_s[CHUNK - 1:CHUNK, sl]
            xdt = x * dt_x[:, sl]
            hin = h_s[j]
            hin_ref[0, j] = hin
            y = jnp.exp(a) * _dot_nn(cg, hin) + x * dsk_x[:, sl]
            h_s[j] = jnp.exp(at) * hin + _dot_tn(bg, xdt * jnp.exp(at - a))
            yd = []
            for hh in (0, 1):
                h = 2 * j + hh
                dec = jnp.exp(jnp.minimum(acs_b[:, h * LANES:(h + 1) * LANES] - acst_s[pl.ds(h, 1), :], 0.0))
                yd.append(_dot_nn(jnp.where(causal, cb * dec, 0.0), xdt))
            y_ref[:, sl] = y + jnp.where(lane < HEAD, yd[0], yd[1])

    one = pl.BlockSpec((1, LANES), lambda i: (0, 0))
    return pl.pallas_call(
        body, name=name, grid=(nc,),
        in_specs=[pl.BlockSpec((CHUNK, 8 * LANES), lambda i: (i, 0)),
                  pl.BlockSpec((CHUNK, LANES), lambda i: (i, 8)),
                  pl.BlockSpec((CHUNK, LANES), lambda i: (i, 9)),
                  pl.BlockSpec((CHUNK, LANES), lambda i: (i, C_DT // LANES)), one, one, one],
        out_specs=[pl.BlockSpec((CHUNK, 8 * LANES), lambda i: (i, 0)),
                   pl.BlockSpec((1, SSD_PAIRS, LANES, LANES), lambda i: (i, 0, 0, 0))],
        out_shape=[jax.ShapeDtypeStruct((t, 8 * LANES), F32),
                   jax.ShapeDtypeStruct((nc, SSD_PAIRS, LANES, LANES), F32)],
        scratch_shapes=[pltpu.VMEM((CHUNK, 8 * LANES), F32), pltpu.VMEM((LANES, CHUNK), F32),
                        pltpu.VMEM((SSD_PAIRS, LANES, LANES), F32)],
        compiler_params=_params(("arbitrary",)),
    )(xc, xc, xc, proj, dtb, alog, dsk)


def _ssd_bwd(xc, proj, dtb, alog, dsk, hin_all, dy, *, name):
    t = xc.shape[0]
    nc = t // CHUNK

    def body(x_ref, b_ref, c_ref, dtr_ref, dtb_ref, alog_ref, dsk_ref, hin_ref, dy_ref,
             dxc_ref, ddtr_ref, ddtb_ref, dalog_ref, ddsk_ref, acsx_s, acst_s, dh_s, dax_s, ddx_s):
        @pl.when(pl.program_id(0) == 0)
        def _():
            dh_s[...] = jnp.zeros_like(dh_s)
            ddtb_ref[...] = jnp.zeros_like(ddtb_ref)
            dalog_ref[...] = jnp.zeros_like(dalog_ref)
            ddsk_ref[...] = jnp.zeros_like(ddsk_ref)

        dtr = dtr_ref[...]
        dtb = dtb_ref[...]
        dt, arow, dt_x, acs_b, e, tril, triu = _ssd_common(dtr, dtb, alog_ref[...], acsx_s, acst_s)
        dsk_x = _dot3(jnp.broadcast_to(dsk_ref[...], (CHUNK, LANES)), e)
        lane = lax.broadcasted_iota(jnp.int32, (CHUNK, LANES), 1)
        rowi = lax.broadcasted_iota(jnp.int32, (CHUNK, LANES), 0)
        causal = (lax.broadcasted_iota(jnp.int32, (CHUNK, CHUNK), 0)
                  >= lax.broadcasted_iota(jnp.int32, (CHUNK, CHUNK), 1))
        acs_rows = jnp.zeros((CHUNK, LANES), F32)
        acs_cols = jnp.zeros((LANES, CHUNK), F32)
        d_b = jnp.zeros((CHUNK, LANES), F32)
        d_c = jnp.zeros((CHUNK, LANES), F32)
        for j in range(SSD_PAIRS):
            g = j // 4
            sl = slice(j * LANES, (j + 1) * LANES)
            if j % 4 == 0:
                bg = jnp.where(lane // HEAD == g, b_ref[...], 0.0)
                cg = jnp.where(lane // HEAD == g, c_ref[...], 0.0)
                cb = _dot_nt(cg, bg)
                dcb = jnp.zeros((CHUNK, CHUNK), F32)
            x = x_ref[:, sl]
            d = dt_x[:, sl]
            a = acsx_s[:, sl]
            at = acsx_s[CHUNK - 1:CHUNK, sl]
            xdt = x * d
            hin = hin_ref[0, j]
            dhout = dh_s[j]
            dyp = dy_ref[:, sl]
            ea, eat, ed = jnp.exp(a), jnp.exp(at), jnp.exp(at - a)
            da_l = dyp * ea * _dot_nn(cg, hin)
            dm = dyp * ea
            d_c = d_c + _dot_nt(dm, hin)
            dh_s[j] = _dot_tn(cg, dm) + eat * dhout
            dat = jnp.sum(dhout * hin * eat, axis=0, keepdims=True)
            d_b = d_b + _dot_nt(xdt * ed, dhout)
            dw = _dot_nn(bg, dhout)
            dxdt = dw * ed
            ded = dw * xdt * ed
            dat = dat + jnp.sum(ded, axis=0, keepdims=True)
            da_l = da_l - ded
            for hh in (0, 1):
                h = 2 * j + hh
                dec = jnp.exp(jnp.minimum(acs_b[:, h * LANES:(h + 1) * LANES] - acst_s[pl.ds(h, 1), :], 0.0))
                gm = jnp.where(causal, cb * dec, 0.0)
                dyh = jnp.where(lane // HEAD == hh, dyp, 0.0)
                dg = _dot_nt(dyh, xdt)
                dxdt = dxdt + _dot_tn(gm, dyh)
                dcb = dcb + jnp.where(causal, dg * dec, 0.0)
                th = dg * gm
                acs_rows = acs_rows + jnp.where(lane == h, jnp.sum(th, axis=1, keepdims=True), 0.0)
                acs_cols = acs_cols + jnp.where(rowi == h, jnp.sum(th, axis=0, keepdims=True), 0.0)
            if j % 4 == 3:
                d_c = d_c + _dot_nn(dcb, bg)
                d_b = d_b + _dot_tn(dcb, cg)
            dxc_ref[:, sl] = dyp * dsk_x[:, sl] + dxdt * d
            ddx_s[:, sl] = dxdt * x
            dax_s[:, sl] = da_l + jnp.where(rowi == CHUNK - 1, dat, 0.0)
            dskp = jnp.sum(dyp * x, axis=0, keepdims=True)
            ddsk_ref[...] += _dot2(jnp.broadcast_to(dskp, (8, LANES)), e[:, sl], _NT)
        dxc_ref[:, 8 * LANES:9 * LANES] = d_b
        dxc_ref[:, 9 * LANES:10 * LANES] = d_c
        dacs = acs_rows - acs_cols.T + _dot2(dax_s[...], e, _NT)
        ddt = _dot2(ddx_s[...], e, _NT)
        dda = _mdot3(triu, dacs)
        ddt = ddt + dda * arow
        dalog_ref[...] += jnp.sum(dda * dt, axis=0, keepdims=True) * arow
        ddtr = jnp.where(lane < SSD_HEADS, ddt * _sigmoid(dtr + dtb), 0.0)
        ddtr_ref[...] = ddtr
        ddtb_ref[...] += jnp.sum(ddtr, axis=0, keepdims=True)

    one = pl.BlockSpec((1, LANES), lambda i: (0, 0))
    rev = lambda c: (lambda i: (nc - 1 - i, c))
    return pl.pallas_call(
        body, name=name, grid=(nc,),
        in_specs=[pl.BlockSpec((CHUNK, 8 * LANES), rev(0)), pl.BlockSpec((CHUNK, LANES), rev(8)),
                  pl.BlockSpec((CHUNK, LANES), rev(9)), pl.BlockSpec((CHUNK, LANES), rev(C_DT // LANES)),
                  one, one, one,
                  pl.BlockSpec((1, SSD_PAIRS, LANES, LANES), lambda i: (nc - 1 - i, 0, 0, 0)),
                  pl.BlockSpec((CHUNK, 8 * LANES), rev(0))],
        out_specs=[pl.BlockSpec((CHUNK, XBC_COLS), rev(0)), pl.BlockSpec((CHUNK, LANES), rev(0)), one, one,
                   pl.BlockSpec((8, LANES), lambda i: (0, 0))],
        out_shape=[jax.ShapeDtypeStruct((t, XBC_COLS), F32), jax.ShapeDtypeStruct((t, LANES), F32)]
        + [jax.ShapeDtypeStruct((1, LANES), F32)] * 2 + [jax.ShapeDtypeStruct((8, LANES), F32)],
        scratch_shapes=[pltpu.VMEM((CHUNK, 8 * LANES), F32), pltpu.VMEM((LANES, CHUNK), F32),
                        pltpu.VMEM((SSD_PAIRS, LANES, LANES), F32),
                        pltpu.VMEM((CHUNK, 8 * LANES), F32), pltpu.VMEM((CHUNK, 8 * LANES), F32)],
        compiler_params=_params(("arbitrary",)),
    )(xc, xc, xc, proj, dtb, alog, dsk, hin_all, dy)


RW_C = 64


def _p3(a, b, dn):
    ah, al = _split2(a)
    bh, bl = _split2(b)
    d = lambda x, y: lax.dot_general(x, y, dn, preferred_element_type=F32)
    return d(ah, bh) + d(ah, bl) + d(al, bh)


_BNN = (((2,), (1,)), ((0,), (0,)))
_BNT = (((2,), (2,)), ((0,), (0,)))
_BTN = (((1,), (1,)), ((0,), (0,)))


@jax.custom_vjp
def _pnn(a, b):
    return _p3(a, b, _BNN)


@jax.custom_vjp
def _pnt(a, b):
    return _p3(a, b, _BNT)


@jax.custom_vjp
def _ptn(a, b):
    return _p3(a, b, _BTN)


_pnn.defvjp(lambda a, b: (_p3(a, b, _BNN), (a, b)), lambda res, g: (_p3(g, res[1], _BNT), _p3(res[0], g, _BTN)))
_pnt.defvjp(lambda a, b: (_p3(a, b, _BNT), (a, b)), lambda res, g: (_p3(g, res[1], _BNN), _p3(g, res[0], _BTN)))
_ptn.defvjp(lambda a, b: (_p3(a, b, _BTN), (a, b)), lambda res, g: (_p3(res[1], g, _BNT), _p3(res[0], g, _BNN)))


def _tri2(tril, x, dn):
    hi, lo = _split2(x)
    m = tril.astype(BF16)
    return (lax.dot_general(m, hi, dn, preferred_element_type=F32) + lax.dot_general(m, lo, dn, preferred_element_type=F32))


@jax.custom_vjp
def _cumsum_rows(tril, x):
    return _tri2(tril, x, _BNN)


_cumsum_rows.defvjp(lambda tril, x: (_tri2(tril, x, _BNN), tril),
                    lambda tril, g: (jnp.zeros_like(tril), _tri2(tril, g, _BTN)))


def _rw_chunk_consts():
    c2 = 2 * RW_C
    row = lax.broadcasted_iota(jnp.int32, (c2, c2), 0)
    col = lax.broadcasted_iota(jnp.int32, (c2, c2), 1)
    same = (row // RW_C) == (col // RW_C)
    strict = (same & (row > col)).astype(F32)
    incl = (same & (row >= col)).astype(F32)
    eye = (row == col).astype(F32)
    tr = lax.broadcasted_iota(jnp.int32, (RW_C, RW_C), 0)
    tc = lax.broadcasted_iota(jnp.int32, (RW_C, RW_C), 1)
    tril = (tr >= tc).astype(F32)
    lane = lax.broadcasted_iota(jnp.int32, (1, LANES), 1)
    hm = [(lane // HEAD == h).astype(F32) for h in (0, 1)]
    return strict, incl, eye, tril, hm


def _rw_chunk(r, lw, k, v, n, b, s2, consts):
    strict, incl, eye, tril, hm = consts
    two = lambda x: jnp.concatenate([x * hm[0], x * hm[1]], axis=1)
    cum = _cumsum_rows(jnp.broadcast_to(tril, (4, RW_C, RW_C)), lw)
    grow, shrink = jnp.exp(-cum), jnp.exp(cum)
    n2, r2 = two(n * jnp.exp(cum - lw)), two(r * shrink)
    b2, k2, v2 = two(b * grow), two(k * grow), two(v)
    p = _pnt(n2, b2) * strict
    x2 = _pnt(n2, s2) + _pnn(_pnt(n2, k2) * strict, v2)
    t_inv, a = eye + p, p
    for _ in range(RW_C.bit_length() - 2):
        a = _pnn(a, a)
        t_inv = t_inv + _pnn(t_inv, a)
    u2 = _pnn(t_inv, x2)
    y2 = _pnt(r2, s2) + _pnn(_pnt(r2, b2) * incl, u2) + _pnn(_pnt(r2, k2) * incl, v2)
    s2_new = (s2 + _ptn(u2, b2) + _ptn(v2, k2)) * jnp.exp(jnp.sum(lw, axis=1, keepdims=True))
    return jnp.sum(y2.reshape(4, 2, RW_C, LANES), axis=1), s2_new


def _pairs(ref):
    return jnp.stack([ref[:, p * LANES:(p + 1) * LANES] for p in range(4)])


def _rw_chunk_fwd(mixed, lw, k, n, b, *, name, side=None):
    t = lw.shape[0]
    nc = t // RW_C

    def body(r_ref, v_ref, lw_ref, k_ref, n_ref, b_ref, y_ref, sin_ref, s_s):
        @pl.when(pl.program_id(0) == 0)
        def _():
            s_s[...] = jnp.zeros_like(s_s)

        s2 = s_s[...]
        sin_ref[0] = s2
        y, s2 = _rw_chunk(*[_pairs(x) for x in (r_ref, lw_ref, k_ref, v_ref, n_ref, b_ref)], s2, _rw_chunk_consts())
        for p in range(4):
            y_ref[:, p * LANES:(p + 1) * LANES] = y[p]
        s_s[...] = s2

    blk = lambda c: pl.BlockSpec((RW_C, 4 * LANES), functools.partial(lambda i, c: (i, c), c=c))
    return _call_with_side(
        body, side, name=name, grid=(nc,), semantics=("arbitrary",),
        in_specs=[blk(0), blk(2), blk(0), blk(0), blk(0), blk(0)],
        out_specs=[blk(0), pl.BlockSpec((1, 4, LANES, LANES), lambda i: (i, 0, 0, 0))],
        out_shape=[jax.ShapeDtypeStruct((t, 4 * LANES), F32), jax.ShapeDtypeStruct((nc, 4, LANES, LANES), F32)],
        scratch_shapes=[pltpu.VMEM((4, LANES, LANES), F32)],
        operands=(mixed, mixed, lw, k, n, b))


def _call_with_side(body, side, *, name, grid, semantics, in_specs, out_specs, out_shape, scratch_shapes, operands):
    if side is None:
        return pl.pallas_call(body, name=name, grid=grid, in_specs=in_specs, out_specs=out_specs, out_shape=out_shape,
                              scratch_shapes=scratch_shapes, compiler_params=_params(semantics))(*operands)
    srcs, per_dest = side
    ns, ni, no, nscr = len(srcs), len(in_specs), len(out_specs), len(scratch_shapes)

    def full_body(*refs):
        ins, side_in = refs[:ni], refs[ni:ni + ns]
        outs, side_out = refs[ni + ns:ni + ns + no], refs[ni + ns + no:ni + 2 * ns + no]
        scratch, sems = refs[ni + 2 * ns + no:ni + 2 * ns + no + nscr], refs[ni + 2 * ns + no + nscr:]

        ids = [pl.program_id(a) for a in range(len(grid))]
        first = functools.reduce(jnp.logical_and, [i == 0 for i in ids])
        last = functools.reduce(jnp.logical_and, [i == n - 1 for i, n in zip(ids, grid)])

        @pl.when(first)
        def _():
            _exchange(side_in, side_out, sems, per_dest, start=True, wait=False)

        body(*ins, *outs, *scratch)

        @pl.when(last)
        def _():
            _exchange(side_in, side_out, sems, per_dest, start=False, wait=True)

    res = pl.pallas_call(
        full_body, name=name, grid=grid, in_specs=list(in_specs) + [_ANY] * ns,
        out_specs=list(out_specs) + [_ANY] * ns, out_shape=list(out_shape) + _exchange_out_shapes(srcs, per_dest),
        scratch_shapes=list(scratch_shapes) + _exchange_sems(ns), compiler_params=_params(("arbitrary",) * len(grid)),
    )(*operands, *srcs)
    return list(res[:no]) + [list(res[no:])]


def _rw_chunk_bwd(mixed, lw, k, n, b, s_in, dy, dr0, dk0, dv0, *, name, side=None):
    t = lw.shape[0]
    nc = t // RW_C

    def body(r_ref, v_ref, lw_ref, k_ref, n_ref, b_ref, sin_ref, dy_ref, dr0_ref, dk0_ref, dv0_ref,
             dr_ref, dlw_ref, dk_ref, dv_ref, dn_ref, db_ref, ds_s):
        @pl.when(pl.program_id(0) == 0)
        def _():
            ds_s[...] = jnp.zeros_like(ds_s)

        consts = _rw_chunk_consts()
        args = [_pairs(x) for x in (r_ref, lw_ref, k_ref, v_ref, n_ref, b_ref)] + [sin_ref[0]]
        _, vjp = jax.vjp(lambda *a: _rw_chunk(*a, consts), *args)
        dr, dlw, dk, dv, dn, db, ds = vjp((_pairs(dy_ref), ds_s[...]))
        for p in range(4):
            sl = slice(p * LANES, (p + 1) * LANES)
            dr_ref[:, sl] = dr[p] + dr0_ref[:, sl]
            dlw_ref[:, sl] = dlw[p]
            dk_ref[:, sl] = dk[p] + dk0_ref[:, sl]
            dv_ref[:, sl] = dv[p] + dv0_ref[:, sl]
            dn_ref[:, sl] = dn[p]
            db_ref[:, sl] = db[p]
        ds_s[...] = ds

    blk = lambda c: pl.BlockSpec((RW_C, 4 * LANES), functools.partial(lambda i, c: (nc - 1 - i, c), c=c))
    return _call_with_side(
        body, side, name=name, grid=(nc,), semantics=("arbitrary",),
        in_specs=[blk(0), blk(2), blk(0), blk(0), blk(0), blk(0),
                  pl.BlockSpec((1, 4, LANES, LANES), lambda i: (nc - 1 - i, 0, 0, 0)), blk(0), blk(0), blk(0), blk(0)],
        out_specs=[blk(0)] * 6,
        out_shape=[jax.ShapeDtypeStruct((t, 4 * LANES), F32)] * 6,
        scratch_shapes=[pltpu.VMEM((4, LANES, LANES), F32)],
        operands=(mixed, mixed, lw, k, n, b, s_in, dy, dr0, dk0, dv0))


def _f_rms_res(x, g):
    return _f_rms(x, g)[0], x


def _final(x, g, target, *, bt, name):
    t, d = x.shape

    def body(x_ref, g_ref, t_ref, dx_ref, loss_ref, dg_ref):
        tgt = t_ref[...]

        def f(xv, gv):
            err = _f_rms(xv, gv)[0] - tgt
            return 0.5 * jnp.mean(err * err, axis=-1, keepdims=True)

        row_loss, vjp = jax.vjp(f, x_ref[...], g_ref[...])
        dx, dg = vjp(jnp.ones_like(row_loss))
        dx_ref[...] = dx

        @pl.when(pl.program_id(0) == 0)
        def _():
            loss_ref[...] = jnp.zeros_like(loss_ref)
            dg_ref[...] = jnp.zeros_like(dg_ref)

        loss_ref[...] += jnp.broadcast_to(jnp.sum(row_loss, axis=0, keepdims=True), (1, LANES))
        dg_ref[...] += dg

    blk = pl.BlockSpec((bt, d), lambda i: (i, 0))
    return pl.pallas_call(
        body, name=name, grid=(t // bt,),
        in_specs=[blk, pl.BlockSpec((1, d), lambda i: (0, 0)), blk],
        out_specs=[blk, pl.BlockSpec((1, LANES), lambda i: (0, 0)), pl.BlockSpec((1, d), lambda i: (0, 0))],
        out_shape=[jax.ShapeDtypeStruct((t, d), F32), jax.ShapeDtypeStruct((1, LANES), F32),
                   jax.ShapeDtypeStruct((1, d), F32)],
        compiler_params=_params(("arbitrary",)),
    )(x, g, target)


ADAMW_BLOCK_BYTES = 1 << 20


def _adamw(w, g, m, v, *, name, block=None):
    shape = w.shape
    if block is not None:
        return _adamw_blocks(w, g, m, v, block, name)
    c = shape[-1]
    shape3 = (1,) * (3 - len(shape)) + shape if len(shape) <= 3 else (-1,) + shape[-2:]
    args = [a.reshape(shape3) for a in (w, g, m, v)]
    lead, r, _ = args[0].shape
    br = r
    if r * c * 4 > ADAMW_BLOCK_BYTES:
        cands = [b for b in range(8, r, 8) if r % b == 0 and b * c * 4 <= ADAMW_BLOCK_BYTES]
        br = max(cands) if cands else r
    outs = _adamw_blocks(*args, (1, br, c), name)
    return tuple(o.reshape(shape) for o in outs)


def _adamw_blocks(w, g, m, v, block, name):
    shape = w.shape
    assert all(s % b == 0 for s, b in zip(shape, block))

    def body(w_ref, g_ref, m_ref, v_ref, d_ref, nm_ref, nv_ref):
        gv = g_ref[...]
        m_new = ADAM_B1 * m_ref[...] + (1.0 - ADAM_B1) * gv
        v_new = ADAM_B2 * v_ref[...] + (1.0 - ADAM_B2) * (gv * gv)
        m_hat = m_new / (1.0 - ADAM_B1 ** ADAM_STEP)
        v_hat = v_new / (1.0 - ADAM_B2 ** ADAM_STEP)
        d_ref[...] = -ADAM_LR * (m_hat / (jnp.sqrt(v_hat) + ADAM_EPS) + ADAM_WD * w_ref[...])
        nm_ref[...] = m_new
        nv_ref[...] = v_new

    blk = pl.BlockSpec(tuple(block), lambda *ids: ids)
    return pl.pallas_call(
        body, name=name, grid=tuple(s // b for s, b in zip(shape, block)), in_specs=[blk] * 4, out_specs=[blk] * 3,
        out_shape=[jax.ShapeDtypeStruct(shape, F32)] * 3,
        compiler_params=_params(("parallel",) * len(shape)),
    )(w, g, m, v)


BT = 256
BC = 128


def _layer_rows(x, proj, s):
    s = {k: s.get(k) for k in ("y_sb_raw", "y_ssd_raw", "mixed", "ys", "k2", "p_sb", "p_ssd", "p_rw")}
    return dict(
        rms=[(x, D_MODEL, 0)],
        sb_gate=[(s["y_sb_raw"], 512, 0), (proj, 512, 3)],
        ssd_norm=[(s["y_ssd_raw"], 1024, 0), (proj, 1024, C_Z // 1024)],
        rw_pre=[(s["mixed"], 512, 1), (s["mixed"], LANES, 16)],
        rw_post=[(s["ys"], 512, 0), (s["mixed"], 512, 0), (s["k2"], 512, 0), (s["mixed"], 512, 2), (s["mixed"], 512, 3)],
        merge=[(s["p_sb"], 1024, 0), (s["p_ssd"], 1024, 0), (s["p_rw"], 1024, 0),
               (proj, 1024, 3), (proj, 1024, 4), (proj, 1024, 5)],
    )


def _layer_fwd(x, p, nm, side=None):
    s = {}
    (s["h"],) = _rowwise(_f_rms, [(x, D_MODEL, 0)], [p["norm_g"]], [D_MODEL], bt=BT, name=nm + "rms")
    proj = s["proj"] = _mm(s["h"], p["w_in"], name=nm + "proj")
    s["y_sb_raw"], s["lt"] = _sb2_fwd(proj, name=nm + "sb")
    s["xc"] = _colwise(_f_conv, proj, C_XBC, XBC_COLS, p["conv"], bc=BC, name=nm + "conv")
    s["y_ssd_raw"], s["hin"] = _ssd_fwd(s["xc"], proj, p["dt_bias"], p["a_log"], p["d_skip"], name=nm + "ssd")
    s["mixed"] = _colwise(_f_rw_mix, proj, C_RW, RW_COLS, [p["rw_mu"]], bc=BC, name=nm + "mix")
    s["w"], s["k2"], s["n"], s["b"] = _rowwise(_f_rw_pre, [(s["mixed"], 512, 1), (s["mixed"], LANES, 16)], p["rw_pre"],
                                               [512] * 4, bt=BT, name=nm + "rwpre")
    s["ys"], s["st"], *exchanged = _rw_chunk_fwd(s["mixed"], s["w"], s["k2"], s["n"], s["b"], name=nm + "scan", side=side)
    rows = _layer_rows(x, proj, s)
    (s["y_sb"],) = _rowwise(_f_sb_gate, rows["sb_gate"], [], [512], bt=BT, name=nm + "sbgate")
    (s["y_ssd"],) = _rowwise(_f_ssd_norm, rows["ssd_norm"], [p["ssd_norm_g"]], [1024], bt=BT, name=nm + "ssdnorm")
    (s["y_rw"],) = _rowwise(_f_rw_post, rows["rw_post"], p["rw_post"], [512], bt=BT, name=nm + "rwpost")
    s["p_sb"] = _mm(s["y_sb"], p["w_out_sb"], name=nm + "osb")
    s["p_ssd"] = _mm(s["y_ssd"], p["w_out_ssd"], name=nm + "ossd")
    s["p_rw"] = _mm(s["y_rw"], p["w_out_rw"], name=nm + "orw")
    (s["merged"],) = _rowwise(_f_merge, _layer_rows(x, proj, s)["merge"], [], [1024], bt=BT, name=nm + "merge")
    return _mm(s["merged"], p["w_o"], add=x, name=nm + "wo"), s, (exchanged[0] if exchanged else None)


def _layer_bwd(x, dx_out, p, s, nm, side=None, side_late=None):
    g = {}
    proj = s["proj"]
    rows = _layer_rows(x, proj, s)
    g["w_o"] = _mm(s["merged"], dx_out, ta=True, name=nm + "g_wo")
    d_merged = _mm(dx_out, p["w_o"], tb=True, name=nm + "d_merged")
    dp_sb, dp_ssd, dp_rw, d_gates = _rowwise_bwd(_f_merge, rows["merge"], [], [(d_merged, 1024, 0)], bt=BT,
                                                 name=nm + "merge_b", groups=[[0], [1], [2], [3, 4, 5]])
    g["w_out_sb"] = _mm(s["y_sb"], dp_sb, ta=True, name=nm + "g_osb")
    g["w_out_ssd"] = _mm(s["y_ssd"], dp_ssd, ta=True, name=nm + "g_ossd")
    g["w_out_rw"] = _mm(s["y_rw"], dp_rw, ta=True, name=nm + "g_orw")
    dy_sb = _mm(dp_sb, p["w_out_sb"], tb=True, name=nm + "d_ysb")
    dy_ssd = _mm(dp_ssd, p["w_out_ssd"], tb=True, name=nm + "d_yssd")
    dy_rw = _mm(dp_rw, p["w_out_rw"], tb=True, name=nm + "d_yrw")
    dy_sb_raw, d_sbgate = _rowwise_bwd(_f_sb_gate, rows["sb_gate"], [], [(dy_sb, 512, 0)], bt=BT, name=nm + "sbgate_b")
    dq, dk, dv = _sb2_bwd(proj, dy_sb_raw, s["lt"], name=nm + "sb_b")
    dy_ssd_raw, dz, g["ssd_norm_g"] = _rowwise_bwd(_f_ssd_norm, rows["ssd_norm"], [p["ssd_norm_g"]],
                                                   [(dy_ssd, 1024, 0)], bt=BT, name=nm + "ssdnorm_b")
    dxc, ddtr, g["dt_bias"], g["a_log"], g["d_skip"] = _ssd_bwd(
        s["xc"], proj, p["dt_bias"], p["a_log"], p["d_skip"], s["hin"], dy_ssd_raw, name=nm + "ssd_b")
    conv_out = _colwise_bwd(_f_conv, proj, C_XBC, XBC_COLS, p["conv"], dxc, bc=BC, name=nm + "conv_b")
    dxbc, g["conv"] = conv_out[0], conv_out[1:]
    dys, dr0, dk0, dv0, d_rwgate, g["rw_ln_g"], g["rw_ln_b"], g["rw_r_k"] = _rowwise_bwd(
        _f_rw_post, rows["rw_post"], p["rw_post"], [(dy_rw, 512, 0)], bt=BT, name=nm + "rwpost_b")
    dr, dw, dk2, dvv, dn, db, *exchanged = _rw_chunk_bwd(s["mixed"], s["w"], s["k2"], s["n"], s["b"], s["st"], dys,
                                                         dr0, dk0, dv0, name=nm + "scan_b",
                                                         side=side(g) if side else None)
    pre_out = _rowwise_bwd(_f_rw_pre, rows["rw_pre"], p["rw_pre"],
                           [(dw, 512, 0), (dk2, 512, 0), (dn, 512, 0), (db, 512, 0)], bt=BT, name=nm + "rwpre_b")
    dkm, dlo, g["rw_pre"] = pre_out[0], pre_out[1], pre_out[2:]
    d_mixed = jnp.concatenate([dr, dkm, dvv, d_rwgate, dlo], axis=1)
    d_slab, g["rw_mu"] = _colwise_bwd(_f_rw_mix, proj, C_RW, RW_COLS, [p["rw_mu"]], d_mixed, bc=BC, name=nm + "mix_b")
    d_proj = jnp.concatenate([dq, dk, dv, d_sbgate, dz, d_gates, d_slab, ddtr, dxbc], axis=1)
    g["w_in"] = _mm(s["h"], d_proj, ta=True, name=nm + "g_win")
    dh = _mm(d_proj, p["w_in"], tb=True, tn=1024, tk=512, name=nm + "d_h", side=side_late(g) if side_late else None)
    dh, late = dh if side_late else (dh, None)
    dx, g["norm_g"] = _rowwise_bwd(_f_rms_res, rows["rms"], [p["norm_g"]], [(dh, D_MODEL, 0), (dx_out, D_MODEL, 0)],
                                   bt=BT, name=nm + "rms_b")
    return dx, g, (exchanged[0] if exchanged else None), late


MESH = pl.DeviceIdType.MESH
N_DEV = 8
_ANY = pl.BlockSpec(memory_space=pl.ANY)


def _here():
    x, y, c = lax.axis_index("x"), lax.axis_index("y"), lax.axis_index("c")
    return x, y, c, [(1 - x, y), (x, 1 - y), (1 - x, 1 - y)]


def _chip_exchange(srcs, *, per_dest, name):
    n = len(srcs)

    def body(*refs):
        _exchange(refs[:n], refs[n:2 * n], refs[2 * n:], per_dest, start=True, wait=True)

    return pl.pallas_call(
        body, name=name, in_specs=[_ANY] * n, out_specs=[_ANY] * n,
        out_shape=_exchange_out_shapes(srcs, per_dest), scratch_shapes=_exchange_sems(n),
    )(*srcs)


def _by_layer(per_dest):
    return per_dest is not True and per_dest is not False


def _exchange_out_shapes(srcs, per_dest):
    lead = (4, 2) if _by_layer(per_dest) else (4,)
    return [jax.ShapeDtypeStruct(lead + s.shape[-2:], s.dtype) for s in srcs]


def _exchange_sems(n):
    return [pltpu.SemaphoreType.DMA((3 * n,)), pltpu.SemaphoreType.DMA((3 * n,)), pltpu.SemaphoreType.DMA((n,))]


def _exchange(src_refs, out_refs, sems, per_dest, *, start, wait):
    send_sems, recv_sems, local_sems = sems
    x, y, c, chips = _here()
    me = 2 * x + y
    owns, sends, recvs = [], [], []
    for a, (src_ref, out_ref) in enumerate(zip(src_refs, out_refs)):
        if per_dest is True:
            pick = lambda q, s=src_ref: s.at[q]
        elif per_dest is False:
            pick = lambda q, s=src_ref: s.at[c]
        else:
            pick = lambda q, s=src_ref: s.at[per_dest].at[c]
        any_block = src_ref.at[0] if len(src_ref.shape) == 3 else src_ref.at[0].at[0]
        if _by_layer(per_dest):
            slot = lambda q, o=out_ref: o.at[q].at[c]
        else:
            slot = lambda q, o=out_ref: o.at[q]
        owns.append(pltpu.make_async_copy(pick(me), slot(me), local_sems.at[a]))
        for j, (px, py) in enumerate(chips):
            sends.append(pltpu.make_async_remote_copy(
                pick(2 * px + py), slot(me), send_sems.at[3 * a + j], recv_sems.at[3 * a + j],
                device_id=(px, py, c), device_id_type=MESH))
            recvs.append(pltpu.make_async_remote_copy(
                any_block, slot(2 * px + py), send_sems.at[3 * a + j], recv_sems.at[3 * a + j],
                device_id=(px, py, c), device_id_type=MESH))
    if start:
        for cp in owns + sends:
            cp.start()
    if wait:
        for cp in recvs:
            cp.wait_recv()
        for cp in sends:
            cp.wait_send()
        for cp in owns:
            cp.wait()


def _sibling_fill(bufs, *, name):
    n = len(bufs)

    def body(*refs):
        in_refs, out_refs, send_sems, recv_sems = refs[:n], refs[n:2 * n], refs[2 * n], refs[2 * n + 1]
        x, y, c, _ = _here()
        copies = []
        for a, (src, dst) in enumerate(zip(in_refs, out_refs)):
            for q in range(4):
                copies.append(pltpu.make_async_remote_copy(
                    src.at[q].at[c], dst.at[q].at[c], send_sems.at[4 * a + q], recv_sems.at[4 * a + q],
                    device_id=(x, y, 1 - c), device_id_type=MESH))
        for cp in copies:
            cp.start()
        for a, (src, dst) in enumerate(zip(in_refs, out_refs)):
            for q in range(4):
                pltpu.make_async_remote_copy(
                    src.at[q].at[c], dst.at[q].at[1 - c], send_sems.at[4 * a + q], recv_sems.at[4 * a + q],
                    device_id=(x, y, 1 - c), device_id_type=MESH).wait_recv()
        for cp in copies:
            cp.wait_send()

    return pl.pallas_call(
        body, name=name, in_specs=[_ANY] * n, out_specs=[_ANY] * n,
        out_shape=[jax.ShapeDtypeStruct(b.shape, b.dtype) for b in bufs],
        input_output_aliases={a: a for a in range(n)},
        scratch_shapes=[pltpu.SemaphoreType.DMA((4 * n,)), pltpu.SemaphoreType.DMA((4 * n,))],
    )(*bufs)


def _sibling_swap(srcs, *, other_slot, name):
    n = len(srcs)

    def body(*refs):
        src_refs, out_refs, send_sems, recv_sems = refs[:n], refs[n:2 * n], refs[2 * n], refs[2 * n + 1]
        x, y, c, _ = _here()
        copies = [pltpu.make_async_remote_copy(s.at[1 - c] if other_slot else s, o, send_sems.at[a], recv_sems.at[a],
                                               device_id=(x, y, 1 - c), device_id_type=MESH)
                  for a, (s, o) in enumerate(zip(src_refs, out_refs))]
        for cp in copies:
            cp.start()
        for cp in copies:
            cp.wait()

    return pl.pallas_call(
        body, name=name, in_specs=[_ANY] * n, out_specs=[_ANY] * n,
        out_shape=[jax.ShapeDtypeStruct(s.shape[1:] if other_slot else s.shape, s.dtype) for s in srcs],
        scratch_shapes=[pltpu.SemaphoreType.DMA((n,)), pltpu.SemaphoreType.DMA((n,))],
    )(*srcs)


def _allgather_small(v, *, reduce, name):
    r = v.shape[0]

    def body(v_ref, out_ref, *rest):
        send_sems, recv_sems, local_sem = rest[-3:]
        x, y, c, chips = _here()
        me, sibling = (x, y, c), (x, y, 1 - c)

        def slot(px, py, pc):
            return out_ref.at[4 * px + 2 * py + pc]

        def copy(k, block, to, src=None):
            return pltpu.make_async_remote_copy(
                src_ref=slot(*block) if src is None else src, dst_ref=slot(*block),
                send_sem=send_sems.at[k], recv_sem=recv_sems.at[k], device_id=to, device_id_type=MESH)

        mine = pltpu.make_async_copy(v_ref, slot(*me), local_sem)
        mine.start()
        first = [copy(0, me, sibling, src=v_ref)]
        first += [copy(1 + j, me, (*chip, c), src=v_ref) for j, chip in enumerate(chips)]
        for cp in first:
            cp.start()
        passed = [copy(4 + j, (*chip, c), sibling) for j, chip in enumerate(chips)]
        for j, chip in enumerate(chips):
            copy(1 + j, (*chip, c), me).wait_recv()
            passed[j].start()
        copy(0, sibling, me).wait_recv()
        for j, chip in enumerate(chips):
            copy(4 + j, (*chip, 1 - c), me).wait_recv()
        for cp in first + passed:
            cp.wait_send()
        mine.wait()
        if reduce:
            total = out_ref[0]
            for d in range(1, N_DEV):
                total = total + out_ref[d]
            rest[0][...] = total

    vm = pl.BlockSpec(memory_space=pltpu.VMEM)
    out_shape = [jax.ShapeDtypeStruct((N_DEV, r, LANES), F32)] + ([jax.ShapeDtypeStruct((r, LANES), F32)] if reduce else [])
    return pl.pallas_call(
        body, name=name, in_specs=[vm], out_specs=[vm] * len(out_shape), out_shape=out_shape,
        scratch_shapes=[pltpu.SemaphoreType.DMA((7,)), pltpu.SemaphoreType.DMA((7,)), pltpu.SemaphoreType.DMA],
        compiler_params=pltpu.CompilerParams(vmem_limit_bytes=VMEM_LIMIT),
    )(v)


REDUCE_BLOCK_BYTES = 2 << 20


def _reduce_rows(r, c):
    cands = [b for b in range(16, r + 1, 16) if r % b == 0 and b * c * 4 <= REDUCE_BLOCK_BYTES]
    return max(cands)


def _add_halves(mine2, other, c_idx, *, name):
    _, nq, r, c = mine2.shape
    br = _reduce_rows(r, c)

    def body(c_ref, a_ref, b_ref, o_ref):
        o_ref[...] = (a_ref[0] + b_ref[...]).astype(o_ref.dtype)

    blk = pl.BlockSpec((1, br, c), lambda q, i, c_ref: (q, i, 0))
    return pl.pallas_call(
        body, name=name,
        grid_spec=pltpu.PrefetchScalarGridSpec(
            num_scalar_prefetch=1, grid=(nq, r // br),
            in_specs=[pl.BlockSpec((1, 1, br, c), lambda q, i, c_ref: (c_ref[0], q, i, 0)), blk],
            out_specs=blk),
        out_shape=jax.ShapeDtypeStruct((nq, r, c), BF16),
        compiler_params=_params(("parallel", "parallel")),
    )(c_idx, mine2, other)


def _sum_chips(parts, c_idx, *, name):
    _, r, c = parts.shape
    br = _reduce_rows(r, c)

    def body(c_ref, p_ref, o_ref):
        total = p_ref[0].astype(F32)
        for q in range(1, 4):
            total = total + p_ref[q].astype(F32)
        o_ref[0] = total

    return pl.pallas_call(
        body, name=name,
        grid_spec=pltpu.PrefetchScalarGridSpec(
            num_scalar_prefetch=1, grid=(r // br,),
            in_specs=[pl.BlockSpec((4, br, c), lambda i, c_ref: (0, i, 0))],
            out_specs=pl.BlockSpec((1, br, c), lambda i, c_ref: (c_ref[0], i, 0))),
        out_shape=jax.ShapeDtypeStruct((2, r, c), F32),
        compiler_params=_params(("parallel",)),
    )(c_idx, parts)


def _sibling_fill_halves(bufs, *, name):
    n = len(bufs)

    def body(*refs):
        in_refs, out_refs, send_sems, recv_sems = refs[:n], refs[n:2 * n], refs[2 * n], refs[2 * n + 1]
        x, y, c, _ = _here()
        copies = [pltpu.make_async_remote_copy(src.at[c], dst.at[c], send_sems.at[a], recv_sems.at[a],
                                               device_id=(x, y, 1 - c), device_id_type=MESH)
                  for a, (src, dst) in enumerate(zip(in_refs, out_refs))]
        for cp in copies:
            cp.start()
        for a, (src, dst) in enumerate(zip(in_refs, out_refs)):
            pltpu.make_async_remote_copy(src.at[c], dst.at[1 - c], send_sems.at[a], recv_sems.at[a],
                                         device_id=(x, y, 1 - c), device_id_type=MESH).wait_recv()
        for cp in copies:
            cp.wait_send()

    return pl.pallas_call(
        body, name=name, in_specs=[_ANY] * n, out_specs=[_ANY] * n,
        out_shape=[jax.ShapeDtypeStruct(b.shape, b.dtype) for b in bufs],
        input_output_aliases={a: a for a in range(n)},
        scratch_shapes=[pltpu.SemaphoreType.DMA((n,)), pltpu.SemaphoreType.DMA((n,))],
    )(*bufs)


BIG = ("w_in", "w_out_sb", "w_out_ssd", "w_out_rw", "w_o")
BIG_AXIS = {"w_in": 2, "w_out_sb": 2, "w_out_ssd": 1, "w_out_rw": 2, "w_o": 1}
SMALL_SHARDED = {"conv_w": 320, "rw_w_up": 128, "rw_a_up": 128}
SMALL = ("norm_g", "conv_w", "conv_b", "dt_bias", "a_log", "d_skip", "ssd_norm_g", "rw_mu", "rw_w0", "rw_w_up",
         "rw_a0", "rw_a_up", "rw_k_k", "rw_k_a", "rw_r_k", "rw_ln_g", "rw_ln_b", "final_g")


def _rows_of(a):
    flat = a.reshape(-1)
    pad = (-flat.shape[0]) % LANES
    return jnp.pad(flat, (0, pad)).reshape(-1, LANES)


def _pack_rows(arrays, multiple=8):
    rows = jnp.concatenate([_rows_of(a) for a in arrays], axis=0)
    pad = (-rows.shape[0]) % multiple
    return jnp.pad(rows, ((0, pad), (0, 0)))


def _unpack_rows(rows, shapes):
    out, off = [], 0
    for shp in shapes:
        n = 1
        for d in shp:
            n *= d
        nr = -(-n // LANES)
        out.append(rows[off:off + nr].reshape(-1)[:n].reshape(shp))
        off += nr
    return out


COL_MAP = ((0, 3072, 0), (3072, 4352, C_XBC), (4352, 4368, C_DT), (4368, 6544, C_RW), (6544, 9616, C_GATES))
SHARD_COLS = N_IN // 4


def _w_in_from_shards(shards):
    pieces = []
    for a, b, dst in sorted(COL_MAP, key=lambda m: m[2]):
        if pieces and dst > pieces[-1][0]:
            pieces.append((dst, jnp.zeros((shards[0].shape[0], dst - pieces[-1][0]), shards[0].dtype)))
        for q in range(4):
            lo, hi = max(a, q * SHARD_COLS), min(b, (q + 1) * SHARD_COLS)
            if lo < hi:
                pieces.append((dst + hi - a, shards[q][:, lo - q * SHARD_COLS:hi - q * SHARD_COLS]))
    return jnp.concatenate([p for _, p in pieces], axis=1)


def _w_in_shard(g, q):
    pieces = []
    for a, b, dst in COL_MAP:
        lo, hi = max(a, q * SHARD_COLS), min(b, (q + 1) * SHARD_COLS)
        if lo < hi:
            pieces.append(g[:, dst + lo - a:dst + hi - a])
    return jnp.concatenate(pieces, axis=1)


def _row_halves(a):
    return a.reshape(2, a.shape[0] // 2, a.shape[1])


def kernel(x, norm_g, w_in, conv_w, conv_b, dt_bias, a_log, d_skip, ssd_norm_g, rw_mu, rw_w0, rw_w_up, rw_a0, rw_a_up, rw_k_k, rw_k_a, rw_r_k, rw_ln_g, rw_ln_b, w_out_sb, w_out_ssd, w_out_rw, w_o, final_g, loss_target, m_norm_g, m_w_in, m_conv_w, m_conv_b, m_dt_bias, m_a_log, m_d_skip, m_ssd_norm_g, m_rw_mu, m_rw_w0, m_rw_w_up, m_rw_a0, m_rw_a_up, m_rw_k_k, m_rw_k_a, m_rw_r_k, m_rw_ln_g, m_rw_ln_b, m_w_out_sb, m_w_out_ssd, m_w_out_rw, m_w_o, m_final_g, v_norm_g, v_w_in, v_conv_w, v_conv_b, v_dt_bias, v_a_log, v_d_skip, v_ssd_norm_g, v_rw_mu, v_rw_w0, v_rw_w_up, v_rw_a0, v_rw_a_up, v_rw_k_k, v_rw_k_a, v_rw_r_k, v_rw_ln_g, v_rw_ln_b, v_w_out_sb, v_w_out_ssd, v_w_out_rw, v_w_o, v_final_g):
    names = ("norm_g", "w_in", "conv_w", "conv_b", "dt_bias", "a_log", "d_skip", "ssd_norm_g", "rw_mu", "rw_w0",
             "rw_w_up", "rw_a0", "rw_a_up", "rw_k_k", "rw_k_a", "rw_r_k", "rw_ln_g", "rw_ln_b", "w_out_sb",
             "w_out_ssd", "w_out_rw", "w_o", "final_g")
    w_loc = dict(zip(names, (norm_g, w_in, conv_w, conv_b, dt_bias, a_log, d_skip, ssd_norm_g, rw_mu, rw_w0, rw_w_up,
                             rw_a0, rw_a_up, rw_k_k, rw_k_a, rw_r_k, rw_ln_g, rw_ln_b, w_out_sb, w_out_ssd, w_out_rw,
                             w_o, final_g)))
    m_loc = dict(zip(names, (m_norm_g, m_w_in, m_conv_w, m_conv_b, m_dt_bias, m_a_log, m_d_skip, m_ssd_norm_g,
                             m_rw_mu, m_rw_w0, m_rw_w_up, m_rw_a0, m_rw_a_up, m_rw_k_k, m_rw_k_a, m_rw_r_k,
                             m_rw_ln_g, m_rw_ln_b, m_w_out_sb, m_w_out_ssd, m_w_out_rw, m_w_o, m_final_g)))
    v_loc = dict(zip(names, (v_norm_g, v_w_in, v_conv_w, v_conv_b, v_dt_bias, v_a_log, v_d_skip, v_ssd_norm_g,
                             v_rw_mu, v_rw_w0, v_rw_w_up, v_rw_a0, v_rw_a_up, v_rw_k_k, v_rw_k_a, v_rw_r_k,
                             v_rw_ln_g, v_rw_ln_b, v_w_out_sb, v_w_out_ssd, v_w_out_rw, v_w_o, v_final_g)))
    chip = 2 * lax.axis_index("x") + lax.axis_index("y")
    core = lax.axis_index("c")

    as_sent = [w_loc[n].astype(BF16).reshape(DEPTH, 2, w_loc[n].shape[1] // 2, w_loc[n].shape[2]) for n in BIG]

    def gathered(mine, nm):
        out = {}
        for n, buf in zip(BIG, _sibling_fill(mine, name=nm)):
            shards = buf.reshape(4, 2 * buf.shape[2], buf.shape[3])
            out[n] = (_w_in_from_shards([shards[q] for q in range(4)]) if n == "w_in"
                      else jnp.concatenate([shards[q] for q in range(4)], axis=BIG_AXIS[n] - 1))
        return out

    full = {}
    sm_names = tuple(SMALL_SHARDED)
    sm_shapes = [w_loc[n].shape for n in sm_names]
    (got_sm,) = _allgather_small(_pack_rows([w_loc[n] for n in sm_names]), reduce=False, name="gather_small")
    per_chip = [_unpack_rows(got_sm[4 * (q // 2) + 2 * (q % 2)], sm_shapes) for q in range(4)]
    for i, n in enumerate(sm_names):
        full[n] = jnp.concatenate([per_chip[q][i] for q in range(4)], axis=-1)

    def pad16(a):
        return jnp.zeros((1, LANES), F32).at[0, :SSD_HEADS].set(a)

    def layer_params(i, big):
        row = lambda n: w_loc[n][i].reshape(1, -1)
        cw = full["conv_w"][i]
        return dict(
            norm_g=row("norm_g"), w_in=big["w_in"], conv=[cw[k][None] for k in range(4)] + [row("conv_b")],
            dt_bias=pad16(dt_bias[i]), a_log=pad16(a_log[i]), d_skip=pad16(d_skip[i]),
            ssd_norm_g=row("ssd_norm_g"), rw_mu=row("rw_mu"),
            rw_pre=[row("rw_w0"), jnp.zeros((LANES, 512), F32).at[:HEAD].set(full["rw_w_up"][i]), row("rw_a0"),
                    jnp.zeros((LANES, 512), F32).at[HEAD:].set(full["rw_a_up"][i]), row("rw_k_k"), row("rw_k_a")],
            rw_post=[row("rw_ln_g"), row("rw_ln_b"), row("rw_r_k")],
            w_out_sb=big["w_out_sb"], w_out_ssd=big["w_out_ssd"], w_out_rw=big["w_out_rw"], w_o=big["w_o"])

    c_idx = core.reshape(1).astype(jnp.int32)

    def reduce_prepare(items, nm):
        sends = []
        for g, n, _ in items:
            per_chip = ([_w_in_shard(g[n], q) for q in range(4)] if n == "w_in"
                        else jnp.split(g[n], 4, axis=BIG_AXIS[n] - 1))
            sends.append(jnp.stack([_row_halves(p) for p in per_chip], axis=1))
        others = _sibling_swap(sends, other_slot=True, name=nm + "sibling")
        return [_add_halves(s, o, c_idx, name=nm + "add_" + lab) for (_, _, lab), s, o in zip(items, sends, others)]

    def reduce_finish(exchanged, labels, nm):
        halves = [_sum_chips(p, c_idx, name=nm + "sum_" + lab) for lab, p in zip(labels, exchanged)]
        whole = _sibling_fill_halves(halves, name=nm + "join")
        return {lab: b.reshape(2 * b.shape[1], b.shape[2]) for lab, b in zip(labels, whole)}

    assert DEPTH == 2
    out_proj = BIG[1:]
    params, xs, saved, grads = [None] * 2, [x[0], None, None], [None] * 2, [None] * 2
    params[0] = layer_params(0, gathered(_chip_exchange(as_sent, per_dest=0, name="gather_l0"), "gather_l0_join"))
    xs[1], saved[0], got = _layer_fwd(xs[0], params[0], "l0_", side=(as_sent, 1))
    params[1] = layer_params(1, gathered(got, "gather_l1_join"))
    xs[2], saved[1], _ = _layer_fwd(xs[1], params[1], "l1_")
    dx, loss_row, g_final = _final(xs[2], final_g.reshape(1, -1), loss_target[0], bt=BT, name="final")
    dx, grads[1], _, _ = _layer_bwd(xs[1], dx, params[1], saved[1], "l1_")
    early = lambda g: [(grads[1], n, "l1_" + n) for n in BIG] + [(g, n, "l0_" + n) for n in out_proj]
    dx, grads[0], got, got_late = _layer_bwd(
        xs[0], dx, params[0], saved[0], "l0_",
        side=lambda g: (reduce_prepare(early(g), "reduce_early_"), True),
        side_late=lambda g: (reduce_prepare([(g, "w_in", "l0_w_in")], "reduce_late_"), True))
    total = reduce_finish(got + got_late, [lab for _, _, lab in early(None)] + ["l0_w_in"], "reduce_")
    totals = [{n: total[f"l{i}_" + n] for n in BIG} for i in range(DEPTH)]

    def stacked(fn):
        return jnp.stack([fn(grads[i]) for i in range(DEPTH)])

    g_loc = {
        "norm_g": stacked(lambda g: g["norm_g"][0]),
        "conv_w": stacked(lambda g: jnp.concatenate(g["conv"][:4], axis=0)),
        "conv_b": stacked(lambda g: g["conv"][4][0]),
        "dt_bias": stacked(lambda g: g["dt_bias"][0, :SSD_HEADS]),
        "a_log": stacked(lambda g: g["a_log"][0, :SSD_HEADS]),
        "d_skip": stacked(lambda g: g["d_skip"][0, :SSD_HEADS]),
        "ssd_norm_g": stacked(lambda g: g["ssd_norm_g"][0]),
        "rw_mu": stacked(lambda g: g["rw_mu"][0]),
        "rw_w0": stacked(lambda g: g["rw_pre"][0][0]),
        "rw_w_up": stacked(lambda g: g["rw_pre"][1][:HEAD]),
        "rw_a0": stacked(lambda g: g["rw_pre"][2][0]),
        "rw_a_up": stacked(lambda g: g["rw_pre"][3][HEAD:]),
        "rw_k_k": stacked(lambda g: g["rw_pre"][4][0]),
        "rw_k_a": stacked(lambda g: g["rw_pre"][5][0]),
        "rw_r_k": stacked(lambda g: g["rw_r_k"].reshape(8, HEAD)),
        "rw_ln_g": stacked(lambda g: g["rw_ln_g"][0]),
        "rw_ln_b": stacked(lambda g: g["rw_ln_b"][0]),
        "final_g": g_final[0],
    }

    g_out = {n: jnp.stack([totals[0][n], totals[1][n]]) for n in BIG}

    sm_all = SMALL + ("loss",)
    sm_full_shapes = [g_loc[n].shape for n in SMALL] + [(1,)]
    _, summed = _allgather_small(_pack_rows([g_loc[n] for n in SMALL] + [loss_row[0, :1]]), reduce=True, name="reduce_small")
    sm = dict(zip(sm_all, _unpack_rows(summed, sm_full_shapes)))
    for n in SMALL:
        g_out[n] = sm[n]
    for n, wd in SMALL_SHARDED.items():
        g_out[n] = lax.dynamic_slice_in_dim(sm[n], chip * wd, wd, axis=sm[n].ndim - 1)
    loss = sm["loss"][0]

    upd = {n: _adamw(w_loc[n], g_out[n], m_loc[n], v_loc[n], name="adamw_" + n) for n in names if n != "w_in"}
    cols = SHARD_COLS // 4
    to_cols = lambda a: jnp.transpose(a, (2, 0, 1)).reshape(4, cols, DEPTH, D_MODEL)
    from_cols = lambda a: jnp.transpose(a.reshape(SHARD_COLS, DEPTH, D_MODEL), (1, 2, 0))
    g_cols = lax.optimization_barrier(to_cols(g_out["w_in"]))
    g_out["w_in"] = from_cols(g_cols)
    upd["w_in"] = tuple(from_cols(a) for a in _adamw(
        to_cols(w_loc["w_in"]), g_cols, to_cols(m_loc["w_in"]), to_cols(v_loc["w_in"]),
        name="adamw_w_in", block=(1, cols, DEPTH, D_MODEL // 2)))
    return (loss, dx[None], *[g_out[n] for n in names], *[upd[n][0] for n in names],
            *[upd[n][1] for n in names], *[upd[n][2] for n in names])
```

```python
import functools

import jax
import jax.numpy as jnp
from jax import lax
from jax.experimental import pallas as pl
from jax.experimental.pallas import tpu as pltpu

F32 = jnp.float32
BF16 = jnp.bfloat16

D_MODEL = 1024
DEPTH = 2
HEAD = 64
LANES = 128
CHUNK = 128
RMS_EPS = 1e-6
GN_EPS = 64e-5
VMEM_LIMIT = 56 * 1024 * 1024

N_IN = 9616
N_PAD = 9728
C_SB, C_Z, C_GATES, C_RW, C_LO, C_DT, C_XBC = 0, 2048, 3072, 6144, 8192, 8320, 8448
RW_COLS = 2176
XBC_COLS = 1280

ADAM_LR, ADAM_B1, ADAM_B2, ADAM_EPS, ADAM_WD, ADAM_STEP = 0.001, 0.9, 0.999, 1e-08, 0.01, 10


def _params(sem=None):
    return pltpu.CompilerParams(dimension_semantics=sem, vmem_limit_bytes=VMEM_LIMIT)


@jax.custom_vjp
def _sigmoid(x):
    return 1.0 / (1.0 + jnp.exp(-x))


def _sigmoid_fwd(x):
    s = _sigmoid(x)
    return s, s


def _sigmoid_bwd(s, g):
    return (g * s * (1.0 - s),)


_sigmoid.defvjp(_sigmoid_fwd, _sigmoid_bwd)


@jax.custom_vjp
def _silu(x):
    return x * _sigmoid(x)


def _silu_fwd(x):
    s = _sigmoid(x)
    return x * s, (x, s)


def _silu_bwd(res, g):
    x, s = res
    return (g * (s + x * s * (1.0 - s)),)


_silu.defvjp(_silu_fwd, _silu_bwd)


@jax.custom_vjp
def _softplus(x):
    return jnp.maximum(x, 0.0) + jnp.log(1.0 + jnp.exp(-jnp.abs(x)))


def _softplus_fwd(x):
    return _softplus(x), x


def _softplus_bwd(x, g):
    return (g * _sigmoid(x),)


_softplus.defvjp(_softplus_fwd, _softplus_bwd)


def _dot(a, b, dims):
    return lax.dot_general(a.astype(BF16), b.astype(BF16), (dims, ((), ())), preferred_element_type=F32)


def _dot_nn(a, b):
    return _dot(a, b, ((1,), (0,)))


def _dot_nt(a, b):
    return _dot(a, b, ((1,), (1,)))


def _dot_tn(a, b):
    return _dot(a, b, ((0,), (0,)))


@jax.custom_vjp
def _bdot(a, b):
    return _dot_nn(a, b)


def _bdot_fwd(a, b):
    return _dot_nn(a, b), (a, b)


def _bdot_bwd(res, g):
    a, b = res
    return _dot_nt(g, b), _dot_tn(a, g)


_bdot.defvjp(_bdot_fwd, _bdot_bwd)


def _split2(x):
    hi = x.astype(BF16)
    lo = (x - hi.astype(F32)).astype(BF16)
    return hi, lo


_NT = (((1,), (1,)), ((), ()))
_NN = (((1,), (0,)), ((), ()))
_TN = (((0,), (0,)), ((), ()))


def _dot2(x, m, dn=_NN):
    hi, lo = _split2(x)
    return (lax.dot_general(hi, m, dn, preferred_element_type=F32)
            + lax.dot_general(lo, m, dn, preferred_element_type=F32))


def _seg_matrix(n):
    r = lax.broadcasted_iota(jnp.int32, (n, n), 0) // HEAD
    c = lax.broadcasted_iota(jnp.int32, (n, n), 1) // HEAD
    return (r == c).astype(BF16)


@jax.custom_vjp
def _segsum2(x, seg):
    return _dot2(x, seg)


def _segsum2_fwd(x, seg):
    return _dot2(x, seg), seg


def _segsum2_bwd(seg, g):
    return _dot2(g, seg), jnp.zeros_like(seg)


_segsum2.defvjp(_segsum2_fwd, _segsum2_bwd)


def _make_segsum(seg):
    return lambda x: _segsum2(x, seg)


def _shift_down_raw(x, k):
    row = lax.broadcasted_iota(jnp.int32, x.shape, 0)
    return jnp.where(row >= k, pltpu.roll(x, k, 0), 0.0)


def _shift_up_raw(x, k):
    t = x.shape[0]
    row = lax.broadcasted_iota(jnp.int32, x.shape, 0)
    return jnp.where(row < t - k, pltpu.roll(x, t - k, 0), 0.0)


@functools.partial(jax.custom_vjp, nondiff_argnums=(1,))
def _shift_down(x, k):
    return _shift_down_raw(x, k)


def _shift_down_fwd(x, k):
    return _shift_down_raw(x, k), None


def _shift_down_bwd(k, _, g):
    return (_shift_up_raw(g, k),)


_shift_down.defvjp(_shift_down_fwd, _shift_down_bwd)


def _mm(a, b, *, name, ta=False, tb=False, add=None, out_dtype=F32, tm=2048, tn=512, tk=None, side=None):
    m, k = (a.shape[1], a.shape[0]) if ta else a.shape
    n = b.shape[0] if tb else b.shape[1]
    tm, tn = min(tm, m), min(tn, n)
    tk = k if tk is None else tk
    nk = k // tk
    assert m % tm == 0 and n % tn == 0 and k % tk == 0
    dims = ((0 if ta else 1,), (1 if tb else 0,))

    def body(a_ref, b_ref, *refs):
        o_ref, acc_ref = refs[-2:]
        p = _dot(a_ref[...], b_ref[...], dims)

        def emit(total):
            if add is not None:
                total = total + refs[0][...]
            o_ref[...] = total.astype(o_ref.dtype)

        if nk == 1:
            emit(p)
        else:
            kk = pl.program_id(2)

            @pl.when(kk == 0)
            def _():
                acc_ref[...] = p

            @pl.when(kk > 0)
            def _():
                acc_ref[...] += p

            @pl.when(kk == nk - 1)
            def _():
                emit(acc_ref[...])

    a_spec = pl.BlockSpec((tk, tm), lambda i, j, kk: (kk, i)) if ta else pl.BlockSpec((tm, tk), lambda i, j, kk: (i, kk))
    b_spec = pl.BlockSpec((tn, tk), lambda i, j, kk: (j, kk)) if tb else pl.BlockSpec((tk, tn), lambda i, j, kk: (kk, j))
    o_spec = pl.BlockSpec((tm, tn), lambda i, j, kk: (i, j))
    res = _call_with_side(
        body, side, name=name, grid=(m // tm, n // tn, nk), semantics=("parallel", "parallel", "arbitrary"),
        in_specs=[a_spec, b_spec] + ([o_spec] if add is not None else []), out_specs=[o_spec],
        out_shape=[jax.ShapeDtypeStruct((m, n), out_dtype)],
        scratch_shapes=[pltpu.VMEM((tm, tn) if nk > 1 else (8, LANES), F32)],
        operands=(a, b) + ((add,) if add is not None else ()))
    return res[0] if side is None else (res[0], res[1])


PROJ_BWD_COLS = 256


def _proj_bwd(h, d_proj, w, *, name):
    t, k = h.shape
    n = d_proj.shape[1]
    bn = PROJ_BWD_COLS
    assert n % bn == 0

    def body(h_ref, d_ref, w_ref, gw_ref, dh_ref):
        d_blk = d_ref[...].astype(BF16)
        gw_ref[...] = lax.dot_general(h_ref[...].astype(BF16), d_blk, _TN, preferred_element_type=F32)
        part = lax.dot_general(d_blk, w_ref[...].astype(BF16), _NT, preferred_element_type=F32)

        @pl.when(pl.program_id(0) == 0)
        def _():
            dh_ref[...] = part

        @pl.when(pl.program_id(0) > 0)
        def _():
            dh_ref[...] += part

    return pl.pallas_call(
        body, name=name, grid=(n // bn,),
        in_specs=[pl.BlockSpec((t, k), lambda j: (0, 0)), pl.BlockSpec((t, bn), lambda j: (0, j)),
                  pl.BlockSpec((k, bn), lambda j: (0, j))],
        out_specs=[pl.BlockSpec((k, bn), lambda j: (0, j)), pl.BlockSpec((t, k), lambda j: (0, 0))],
        out_shape=[jax.ShapeDtypeStruct((k, n), F32), jax.ShapeDtypeStruct((t, k), F32)],
        compiler_params=_params(("arbitrary",)),
    )(h, d_proj, w)


def _row_specs(rows, bt):
    return [pl.BlockSpec((bt, w), functools.partial(lambda i, c: (i, c), c=c)) for _, w, c in rows]


def _full_spec(p):
    return pl.BlockSpec(p.shape, functools.partial(lambda i, nd: (0,) * nd, nd=p.ndim))


def _rowwise(f, rows, pars, out_widths, *, bt, name, acc_widths=()):
    t = rows[0][0].shape[0]
    nr, npar, no, na = len(rows), len(pars), len(out_widths), len(acc_widths)

    def body(*refs):
        vals = [r[...] for r in refs[:nr + npar]]
        outs = f(*vals)
        for o_ref, o in zip(refs[nr + npar:nr + npar + no], outs[:no]):
            o_ref[...] = o.astype(o_ref.dtype)
        if na:
            first = pl.program_id(0) == 0
            for a_ref, a in zip(refs[nr + npar + no:], outs[no:]):
                @pl.when(first)
                def _():
                    a_ref[...] = jnp.zeros_like(a_ref)
                a_ref[...] += a

    return pl.pallas_call(
        body, name=name, grid=(t // bt,),
        in_specs=_row_specs(rows, bt) + [_full_spec(p) for p in pars],
        out_specs=[pl.BlockSpec((bt, w), lambda i: (i, 0)) for w in out_widths]
        + [pl.BlockSpec((1, w), lambda i: (0, 0)) for w in acc_widths],
        out_shape=[jax.ShapeDtypeStruct((t, w), F32) for w in out_widths]
        + [jax.ShapeDtypeStruct((1, w), F32) for w in acc_widths],
        compiler_params=_params(("arbitrary",)),
    )(*[r[0] for r in rows], *pars)


def _rowwise_bwd(f, rows, pars, douts, *, bt, name, groups=None):
    t = rows[0][0].shape[0]
    nr, npar, nd = len(rows), len(pars), len(douts)
    groups = [[i] for i in range(nr)] if groups is None else groups
    widths = [r[1] for r in rows]

    def body(*refs):
        vals = [r[...] for r in refs[:nr + npar]]
        cts = tuple(r[...] for r in refs[nr + npar:nr + npar + nd])
        _, vjp = jax.vjp(lambda *a: tuple(f(*a)), *vals)
        grads = vjp(cts)
        out_refs = refs[nr + npar + nd:]
        for g_ref, grp in zip(out_refs[:len(groups)], groups):
            off = 0
            for i in grp:
                g_ref[:, off:off + widths[i]] = grads[i]
                off += widths[i]
        first = pl.program_id(0) == 0
        for p_ref, g in zip(out_refs[len(groups):], grads[nr:]):
            @pl.when(first)
            def _():
                p_ref[...] = jnp.zeros_like(p_ref)
            p_ref[...] += g

    gw = [sum(widths[i] for i in grp) for grp in groups]
    return pl.pallas_call(
        body, name=name, grid=(t // bt,),
        in_specs=_row_specs(rows, bt) + [_full_spec(p) for p in pars] + _row_specs(douts, bt),
        out_specs=[pl.BlockSpec((bt, w), lambda i: (i, 0)) for w in gw] + [_full_spec(p) for p in pars],
        out_shape=[jax.ShapeDtypeStruct((t, w), F32) for w in gw] + [jax.ShapeDtypeStruct(p.shape, F32) for p in pars],
        compiler_params=_params(("arbitrary",)),
    )(*[r[0] for r in rows], *pars, *[d[0] for d in douts])


def _colwise(f, x, c0, ncols, pars, *, bc, name):
    t = x.shape[0]

    def body(x_ref, *refs):
        o_ref = refs[-1]
        o_ref[...] = f(x_ref[...], *[r[...] for r in refs[:-1]])

    return pl.pallas_call(
        body, name=name, grid=(ncols // bc,),
        in_specs=[pl.BlockSpec((t, bc), lambda j: (0, j + c0 // bc))]
        + [pl.BlockSpec((p.shape[0], bc), lambda j: (0, j)) for p in pars],
        out_specs=pl.BlockSpec((t, bc), lambda j: (0, j)),
        out_shape=jax.ShapeDtypeStruct((t, ncols), F32),
        compiler_params=_params(("parallel",)),
    )(x, *pars)


def _colwise_bwd(f, x, c0, ncols, pars, dout, *, bc, name):
    t = x.shape[0]
    npar = len(pars)

    def body(x_ref, *refs):
        vals = [x_ref[...]] + [r[...] for r in refs[:npar]]
        _, vjp = jax.vjp(f, *vals)
        grads = vjp(refs[npar][...])
        for g_ref, g in zip(refs[npar + 1:], grads):
            g_ref[...] = g

    return pl.pallas_call(
        body, name=name, grid=(ncols // bc,),
        in_specs=[pl.BlockSpec((t, bc), lambda j: (0, j + c0 // bc))]
        + [pl.BlockSpec((p.shape[0], bc), lambda j: (0, j)) for p in pars]
        + [pl.BlockSpec((t, bc), lambda j: (0, j))],
        out_specs=[pl.BlockSpec((t, bc), lambda j: (0, j))]
        + [pl.BlockSpec((p.shape[0], bc), lambda j: (0, j)) for p in pars],
        out_shape=[jax.ShapeDtypeStruct((t, ncols), F32)] + [jax.ShapeDtypeStruct(p.shape, F32) for p in pars],
        compiler_params=_params(("parallel",)),
    )(x, *pars, dout)


def _f_rms(x, g):
    return (x * lax.rsqrt(jnp.mean(x * x, axis=-1, keepdims=True) + RMS_EPS) * g,)


def _f_sb_gate(y, gate):
    return (y * _silu(gate),)


def _f_ssd_norm(y, z, g):
    u = y * _silu(z)
    return (u * lax.rsqrt(jnp.mean(u * u, axis=-1, keepdims=True) + RMS_EPS) * g,)


def _f_merge(p_sb, p_ssd, p_rw, g_sb, g_ssd, g_rw):
    return (_sigmoid(g_sb) * p_sb + _sigmoid(g_ssd) * p_ssd + _sigmoid(g_rw) * p_rw,)


def _f_rw_pre(k, lo, w0, w_up, a0, a_up, k_k, k_a):
    segsum = _make_segsum(_seg_matrix(k.shape[1]))
    lane = lax.broadcasted_iota(jnp.int32, lo.shape, 1)
    w_lo = jnp.where(lane < HEAD, jnp.tanh(lo), 0.0)
    a_lo = jnp.where(lane >= HEAD, lo, 0.0)
    w = -_softplus(-(w0 + _bdot(w_lo, w_up))) - 0.5
    log_decay = -jnp.exp(w)
    a = _sigmoid(a0 + _bdot(a_lo, a_up))
    kk = k * k_k
    kk = kk / jnp.maximum(jnp.sqrt(segsum(kk * kk)), 1e-12)
    return log_decay, k * (1.0 + (a - 1.0) * k_a), -kk, kk * a


def _f_rw_post(y, r, k2, v, gate, ln_g, ln_b, r_k):
    segsum = _make_segsum(_seg_matrix(y.shape[1]))
    yc = y - segsum(y) * (1.0 / HEAD)
    var = segsum(yc * yc) * (1.0 / HEAD)
    yn = yc * lax.rsqrt(var + GN_EPS) * ln_g + ln_b
    return ((yn + segsum(r * k2 * r_k) * v) * _silu(gate),)


def _f_rw_mix(slab, mu):
    return slab + (_shift_down(slab, 1) - slab) * mu


def _f_conv(x, w0, w1, w2, w3, b):
    acc = x * w3 + b
    for i, w in enumerate((w0, w1, w2)):
        acc = acc + _shift_down(x, 3 - i) * w
    return _silu(acc)


def _log_sigmoid(z):
    return jnp.minimum(z, 0.0) - jnp.log(1.0 + jnp.exp(-jnp.abs(z)))


SB_BQ = 256
SB_BK = 256
assert SB_BQ == SB_BK


def _tri_ones(kind):
    j = lax.broadcasted_iota(jnp.int32, (SB_BK, SB_BK + LANES), 0)
    s = lax.broadcasted_iota(jnp.int32, (SB_BK, SB_BK + LANES), 1)
    tri = {"gt": j > s, "le": j <= s, "lt": j < s}[kind]
    return (tri | (s >= SB_BK)).astype(BF16)


def _sb_common(q_ref):
    lane = lax.broadcasted_iota(jnp.int32, (SB_BQ, LANES), 1)
    q = q_ref[...] * (HEAD ** -0.5)
    q2 = jnp.concatenate([jnp.where(lane < HEAD, q, 0.0), jnp.where(lane >= HEAD, q, 0.0)], axis=0).astype(BF16)
    diff = (lax.broadcasted_iota(jnp.int32, (2 * SB_BQ, SB_BK), 1)
            - (lax.broadcasted_iota(jnp.int32, (2 * SB_BQ, SB_BK), 0) & (SB_BQ - 1)))
    return lane, q2, diff


def _rep(x):
    return jnp.concatenate([x] * (SB_BK // LANES), axis=1)


def _sb2_specs(t):
    q = pl.BlockSpec((SB_BQ, LANES), lambda j, i: (i, j))
    k = pl.BlockSpec((t, LANES), lambda j, i: (0, 4 + j))
    v = pl.BlockSpec((t, LANES), lambda j, i: (0, 8 + j))
    return q, k, v


def _sb2_fwd(proj, *, name):
    t = proj.shape[0]

    def body(q_ref, k_ref, v_ref, y_ref, lt_ref):
        i = pl.program_id(1)
        lane, q2, diff = _sb_common(q_ref)
        m_f = _tri_ones("gt")

        def step(kb, carry, diagonal):
            c, acc = carry
            off = pl.multiple_of(kb * SB_BK, SB_BK)
            kblk = k_ref[pl.ds(off, SB_BK), :].astype(BF16)
            vblk = v_ref[pl.ds(off, SB_BK), :].astype(BF16)
            z = lax.dot_general(q2, kblk, _NT, preferred_element_type=F32)
            lb = _log_sigmoid(z)
            lk = jnp.where(diff < 0, lb - z, 0.0) if diagonal else lb - z
            w2 = _dot2(lk, m_f)
            att = jnp.exp(lb + _rep(c) + w2[:, :SB_BK])
            if diagonal:
                att = jnp.where(diff < 0, att, 0.0)
            acc = acc + lax.dot_general(att.astype(BF16), vblk, _NN, preferred_element_type=F32)
            return c + w2[:, SB_BK:], acc

        zero = jnp.zeros((2 * SB_BQ, LANES), F32)
        c, acc = lax.fori_loop(0, i, lambda it, carry: step(i - 1 - it, carry, False), step(i, (zero, zero), True))
        y_ref[...] = jnp.where(lane < HEAD, acc[:SB_BQ], acc[SB_BQ:])
        lt_ref[0] = c[:SB_BQ]
        lt_ref[1] = c[SB_BQ:]

    return pl.pallas_call(
        body, name=name, grid=(4, t // SB_BQ),
        in_specs=list(_sb2_specs(t)),
        out_specs=[pl.BlockSpec((SB_BQ, LANES), lambda j, i: (i, j)),
                   pl.BlockSpec((2, SB_BQ, LANES), lambda j, i: (j, i, 0))],
        out_shape=[jax.ShapeDtypeStruct((t, 4 * LANES), F32), jax.ShapeDtypeStruct((8, t, LANES), F32)],
        compiler_params=_params(("parallel", "arbitrary")),
    )(proj, proj, proj)


def _sb2_bwd(proj, dy, lt, *, name):
    t = proj.shape[0]

    def body(q_ref, k_ref, v_ref, dy_ref, lt_ref, dq_ref, dk_ref, dv_ref):
        i = pl.program_id(1)

        @pl.when(i == 0)
        def _():
            dk_ref[...] = jnp.zeros_like(dk_ref)
            dv_ref[...] = jnp.zeros_like(dv_ref)

        lane, q2, diff = _sb_common(q_ref)
        m_le, m_lt = _tri_ones("le"), _tri_ones("lt")
        dy_blk = dy_ref[...]
        do2 = jnp.concatenate([jnp.where(lane < HEAD, dy_blk, 0.0), jnp.where(lane >= HEAD, dy_blk, 0.0)],
                              axis=0).astype(BF16)
        lt2 = jnp.concatenate([lt_ref[0], lt_ref[1]], axis=0)

        def step(kb, carry, diagonal):
            cp, cg, dq = carry
            off = pl.multiple_of(kb * SB_BK, SB_BK)
            kblk = k_ref[pl.ds(off, SB_BK), :].astype(BF16)
            vblk = v_ref[pl.ds(off, SB_BK), :].astype(BF16)
            z = lax.dot_general(q2, kblk, _NT, preferred_element_type=F32)
            lb = _log_sigmoid(z)
            lk = jnp.where(diff < 0, lb - z, 0.0) if diagonal else lb - z
            w2 = _dot2(lk, m_le)
            att = jnp.exp(lb + _rep(lt2 - cp) - w2[:, :SB_BK])
            if diagonal:
                att = jnp.where(diff < 0, att, 0.0)
            d_e = lax.dot_general(do2, vblk, _NT, preferred_element_type=F32) * att
            g2 = _dot2(d_e, m_lt)
            sig = jnp.exp(lb)
            dz = d_e * (1.0 - sig) - (_rep(cg) + g2[:, :SB_BK]) * sig
            dz = (jnp.where(diff < 0, dz, 0.0) if diagonal else dz).astype(BF16)
            dq = dq + lax.dot_general(dz, kblk, _NN, preferred_element_type=F32)
            dk_ref[pl.ds(off, SB_BK), :] += lax.dot_general(dz, q2, _TN, preferred_element_type=F32)
            dv_ref[pl.ds(off, SB_BK), :] += lax.dot_general(att.astype(BF16), do2, _TN, preferred_element_type=F32)
            return cp + w2[:, SB_BK:], cg + g2[:, SB_BK:], dq

        zero = jnp.zeros((2 * SB_BQ, LANES), F32)
        before = lax.fori_loop(0, i, lambda kb, carry: step(kb, carry, False), (zero, zero, zero))
        _, _, dq = step(i, before, True)
        dq_ref[...] = jnp.where(lane < HEAD, dq[:SB_BQ], dq[SB_BQ:]) * (HEAD ** -0.5)

    q_spec, k_spec, v_spec = _sb2_specs(t)
    blk = pl.BlockSpec((SB_BQ, LANES), lambda j, i: (i, j))
    col = pl.BlockSpec((t, LANES), lambda j, i: (0, j))
    return pl.pallas_call(
        body, name=name, grid=(4, t // SB_BQ),
        in_specs=[q_spec, k_spec, v_spec, blk, pl.BlockSpec((2, SB_BQ, LANES), lambda j, i: (j, i, 0))],
        out_specs=[blk, col, col],
        out_shape=[jax.ShapeDtypeStruct((t, 4 * LANES), F32)] * 3,
        compiler_params=_params(("parallel", "arbitrary")),
    )(proj, proj, proj, dy, lt)


SSD_HEADS = 16
SSD_PAIRS = 8


def _split3(x):
    a = x.astype(BF16)
    r = x - a.astype(F32)
    b = r.astype(BF16)
    return a, b, (r - b.astype(F32)).astype(BF16)


def _dot3(x, m, dn=_NN):
    return sum(lax.dot_general(p, m, dn, preferred_element_type=F32) for p in _split3(x))


def _mdot3(m, x):
    return sum(lax.dot_general(m, p, _NN, preferred_element_type=F32) for p in _split3(x))


def _ssd_common(dtr, dtb, alog, acsx_s, acst_s):
    lane = lax.broadcasted_iota(jnp.int32, (CHUNK, LANES), 1)
    lane1 = lax.broadcasted_iota(jnp.int32, (1, LANES), 1)
    arow = jnp.where(lane1 < SSD_HEADS, -jnp.exp(alog), 0.0)
    dt = jnp.where(lane < SSD_HEADS, _softplus(dtr + dtb), 0.0)
    da = dt * arow
    r = lax.broadcasted_iota(jnp.int32, (CHUNK, CHUNK), 0)
    c = lax.broadcasted_iota(jnp.int32, (CHUNK, CHUNK), 1)
    tril = (r >= c).astype(BF16)
    triu = (r <= c).astype(BF16)
    acs = _mdot3(tril, da)
    acst_s[...] = _dot3(da, triu, _TN)
    eh = lax.broadcasted_iota(jnp.int32, (LANES, 8 * LANES), 0)
    e = (eh == lax.broadcasted_iota(jnp.int32, (LANES, 8 * LANES), 1) // HEAD).astype(BF16)
    eh2 = lax.broadcasted_iota(jnp.int32, (LANES, 16 * LANES), 0)
    e2 = (eh2 == lax.broadcasted_iota(jnp.int32, (LANES, 16 * LANES), 1) // LANES).astype(BF16)
    acsx_s[...] = _dot3(acs, e)
    return dt, arow, _dot3(dt, e), _dot3(acs, e2), e, tril, triu


def _ssd_fwd(xc, proj, dtb, alog, dsk, *, name):
    t = xc.shape[0]
    nc = t // CHUNK

    def body(x_ref, b_ref, c_ref, dtr_ref, dtb_ref, alog_ref, dsk_ref, y_ref, hin_ref, acsx_s, acst_s, h_s):
        @pl.when(pl.program_id(0) == 0)
        def _():
            h_s[...] = jnp.zeros_like(h_s)

        dt, arow, dt_x, acs_b, e, tril, _ = _ssd_common(dtr_ref[...], dtb_ref[...], alog_ref[...], acsx_s, acst_s)
        dsk_x = _dot3(jnp.broadcast_to(dsk_ref[...], (CHUNK, LANES)), e)
        lane = lax.broadcasted_iota(jnp.int32, (CHUNK, LANES), 1)
        causal = (lax.broadcasted_iota(jnp.int32, (CHUNK, CHUNK), 0)
                  >= lax.broadcasted_iota(jnp.int32, (CHUNK, CHUNK), 1))
        for j in range(SSD_PAIRS):
            g = j // 4
            sl = slice(j * LANES, (j + 1) * LANES)
            if j % 4 == 0:
                bg = jnp.where(lane // HEAD == g, b_ref[...], 0.0)
                cg = jnp.where(lane // HEAD == g, c_ref[...], 0.0)
                cb = _dot_nt(cg, bg)
            x = x_ref[:, sl]
            a = acsx_s[:, sl]
            at = acsx_s[CHUNK - 1:CHUNK, sl]
            xdt = x * dt_x[:, sl]
            hin = h_s[j]
            hin_ref[0, j] = hin
            y = jnp.exp(a) * _dot_nn(cg, hin) + x * dsk_x[:, sl]
            h_s[j] = jnp.exp(at) * hin + _dot_tn(bg, xdt * jnp.exp(at - a))
            yd = []
            for hh in (0, 1):
                h = 2 * j + hh
                dec = jnp.exp(jnp.minimum(acs_b[:, h * LANES:(h + 1) * LANES] - acst_s[pl.ds(h, 1), :], 0.0))
                yd.append(_dot_nn(jnp.where(causal, cb * dec, 0.0), xdt))
            y_ref[:, sl] = y + jnp.where(lane < HEAD, yd[0], yd[1])

    one = pl.BlockSpec((1, LANES), lambda i: (0, 0))
    return pl.pallas_call(
        body, name=name, grid=(nc,),
        in_specs=[pl.BlockSpec((CHUNK, 8 * LANES), lambda i: (i, 0)),
                  pl.BlockSpec((CHUNK, LANES), lambda i: (i, 8)),
                  pl.BlockSpec((CHUNK, LANES), lambda i: (i, 9)),
                  pl.BlockSpec((CHUNK, LANES), lambda i: (i, C_DT // LANES)), one, one, one],
        out_specs=[pl.BlockSpec((CHUNK, 8 * LANES), lambda i: (i, 0)),
                   pl.BlockSpec((1, SSD_PAIRS, LANES, LANES), lambda i: (i, 0, 0, 0))],
        out_shape=[jax.ShapeDtypeStruct((t, 8 * LANES), F32),
                   jax.ShapeDtypeStruct((nc, SSD_PAIRS, LANES, LANES), F32)],
        scratch_shapes=[pltpu.VMEM((CHUNK, 8 * LANES), F32), pltpu.VMEM((LANES, CHUNK), F32),
                        pltpu.VMEM((SSD_PAIRS, LANES, LANES), F32)],
        compiler_params=_params(("arbitrary",)),
    )(xc, xc, xc, proj, dtb, alog, dsk)


def _ssd_bwd(xc, proj, dtb, alog, dsk, hin_all, dy, *, name):
    t = xc.shape[0]
    nc = t // CHUNK

    def body(x_ref, b_ref, c_ref, dtr_ref, dtb_ref, alog_ref, dsk_ref, hin_ref, dy_ref,
             dxc_ref, ddtr_ref, ddtb_ref, dalog_ref, ddsk_ref, acsx_s, acst_s, dh_s, dax_s, ddx_s):
        @pl.when(pl.program_id(0) == 0)
        def _():
            dh_s[...] = jnp.zeros_like(dh_s)
            ddtb_ref[...] = jnp.zeros_like(ddtb_ref)
            dalog_ref[...] = jnp.zeros_like(dalog_ref)
            ddsk_ref[...] = jnp.zeros_like(ddsk_ref)

        dtr = dtr_ref[...]
        dtb = dtb_ref[...]
        dt, arow, dt_x, acs_b, e, tril, triu = _ssd_common(dtr, dtb, alog_ref[...], acsx_s, acst_s)
        dsk_x = _dot3(jnp.broadcast_to(dsk_ref[...], (CHUNK, LANES)), e)
        lane = lax.broadcasted_iota(jnp.int32, (CHUNK, LANES), 1)
        rowi = lax.broadcasted_iota(jnp.int32, (CHUNK, LANES), 0)
        causal = (lax.broadcasted_iota(jnp.int32, (CHUNK, CHUNK), 0)
                  >= lax.broadcasted_iota(jnp.int32, (CHUNK, CHUNK), 1))
        acs_rows = jnp.zeros((CHUNK, LANES), F32)
        acs_cols = jnp.zeros((LANES, CHUNK), F32)
        d_b = jnp.zeros((CHUNK, LANES), F32)
        d_c = jnp.zeros((CHUNK, LANES), F32)
        for j in range(SSD_PAIRS):
            g = j // 4
            sl = slice(j * LANES, (j + 1) * LANES)
            if j % 4 == 0:
                bg = jnp.where(lane // HEAD == g, b_ref[...], 0.0)
                cg = jnp.where(lane // HEAD == g, c_ref[...], 0.0)
                cb = _dot_nt(cg, bg)
                dcb = jnp.zeros((CHUNK, CHUNK), F32)
            x = x_ref[:, sl]
            d = dt_x[:, sl]
            a = acsx_s[:, sl]
            at = acsx_s[CHUNK - 1:CHUNK, sl]
            xdt = x * d
            hin = hin_ref[0, j]
            dhout = dh_s[j]
            dyp = dy_ref[:, sl]
            ea, eat, ed = jnp.exp(a), jnp.exp(at), jnp.exp(at - a)
            da_l = dyp * ea * _dot_nn(cg, hin)
            dm = dyp * ea
            d_c = d_c + _dot_nt(dm, hin)
            dh_s[j] = _dot_tn(cg, dm) + eat * dhout
            dat = jnp.sum(dhout * hin * eat, axis=0, keepdims=True)
            d_b = d_b + _dot_nt(xdt * ed, dhout)
            dw = _dot_nn(bg, dhout)
            dxdt = dw * ed
            ded = dw * xdt * ed
            dat = dat + jnp.sum(ded, axis=0, keepdims=True)
            da_l = da_l - ded
            for hh in (0, 1):
                h = 2 * j + hh
                dec = jnp.exp(jnp.minimum(acs_b[:, h * LANES:(h + 1) * LANES] - acst_s[pl.ds(h, 1), :], 0.0))
                gm = jnp.where(causal, cb * dec, 0.0)
                dyh = jnp.where(lane // HEAD == hh, dyp, 0.0)
                dg = _dot_nt(dyh, xdt)
                dxdt = dxdt + _dot_tn(gm, dyh)
                dcb = dcb + jnp.where(causal, dg * dec, 0.0)
                th = dg * gm
                acs_rows = acs_rows + jnp.where(lane == h, jnp.sum(th, axis=1, keepdims=True), 0.0)
                acs_cols = acs_cols + jnp.where(rowi == h, jnp.sum(th, axis=0, keepdims=True), 0.0)
            if j % 4 == 3:
                d_c = d_c + _dot_nn(dcb, bg)
                d_b = d_b + _dot_tn(dcb, cg)
            dxc_ref[:, sl] = dyp * dsk_x[:, sl] + dxdt * d
            ddx_s[:, sl] = dxdt * x
            dax_s[:, sl] = da_l + jnp.where(rowi == CHUNK - 1, dat, 0.0)
            dskp = jnp.sum(dyp * x, axis=0, keepdims=True)
            ddsk_ref[...] += _dot2(jnp.broadcast_to(dskp, (8, LANES)), e[:, sl], _NT)
        dxc_ref[:, 8 * LANES:9 * LANES] = d_b
        dxc_ref[:, 9 * LANES:10 * LANES] = d_c
        dacs = acs_rows - acs_cols.T + _dot2(dax_s[...], e, _NT)
        ddt = _dot2(ddx_s[...], e, _NT)
        dda = _mdot3(triu, dacs)
        ddt = ddt + dda * arow
        dalog_ref[...] += jnp.sum(dda * dt, axis=0, keepdims=True) * arow
        ddtr = jnp.where(lane < SSD_HEADS, ddt * _sigmoid(dtr + dtb), 0.0)
        ddtr_ref[...] = ddtr
        ddtb_ref[...] += jnp.sum(ddtr, axis=0, keepdims=True)

    one = pl.BlockSpec((1, LANES), lambda i: (0, 0))
    rev = lambda c: (lambda i: (nc - 1 - i, c))
    return pl.pallas_call(
        body, name=name, grid=(nc,),
        in_specs=[pl.BlockSpec((CHUNK, 8 * LANES), rev(0)), pl.BlockSpec((CHUNK, LANES), rev(8)),
                  pl.BlockSpec((CHUNK, LANES), rev(9)), pl.BlockSpec((CHUNK, LANES), rev(C_DT // LANES)),
                  one, one, one,
                  pl.BlockSpec((1, SSD_PAIRS, LANES, LANES), lambda i: (nc - 1 - i, 0, 0, 0)),
                  pl.BlockSpec((CHUNK, 8 * LANES), rev(0))],
        out_specs=[pl.BlockSpec((CHUNK, XBC_COLS), rev(0)), pl.BlockSpec((CHUNK, LANES), rev(0)), one, one,
                   pl.BlockSpec((8, LANES), lambda i: (0, 0))],
        out_shape=[jax.ShapeDtypeStruct((t, XBC_COLS), F32), jax.ShapeDtypeStruct((t, LANES), F32)]
        + [jax.ShapeDtypeStruct((1, LANES), F32)] * 2 + [jax.ShapeDtypeStruct((8, LANES), F32)],
        scratch_shapes=[pltpu.VMEM((CHUNK, 8 * LANES), F32), pltpu.VMEM((LANES, CHUNK), F32),
                        pltpu.VMEM((SSD_PAIRS, LANES, LANES), F32),
                        pltpu.VMEM((CHUNK, 8 * LANES), F32), pltpu.VMEM((CHUNK, 8 * LANES), F32)],
        compiler_params=_params(("arbitrary",)),
    )(xc, xc, xc, proj, dtb, alog, dsk, hin_all, dy)


RW_C = 64


def _p3(a, b, dn):
    ah, al = _split2(a)
    bh, bl = _split2(b)
    d = lambda x, y: lax.dot_general(x, y, dn, preferred_element_type=F32)
    return d(ah, bh) + d(ah, bl) + d(al, bh)


_BNN = (((2,), (1,)), ((0,), (0,)))
_BNT = (((2,), (2,)), ((0,), (0,)))
_BTN = (((1,), (1,)), ((0,), (0,)))


@jax.custom_vjp
def _pnn(a, b):
    return _p3(a, b, _BNN)


@jax.custom_vjp
def _pnt(a, b):
    return _p3(a, b, _BNT)


@jax.custom_vjp
def _ptn(a, b):
    return _p3(a, b, _BTN)


_pnn.defvjp(lambda a, b: (_p3(a, b, _BNN), (a, b)), lambda res, g: (_p3(g, res[1], _BNT), _p3(res[0], g, _BTN)))
_pnt.defvjp(lambda a, b: (_p3(a, b, _BNT), (a, b)), lambda res, g: (_p3(g, res[1], _BNN), _p3(g, res[0], _BTN)))
_ptn.defvjp(lambda a, b: (_p3(a, b, _BTN), (a, b)), lambda res, g: (_p3(res[1], g, _BNT), _p3(res[0], g, _BNN)))


def _tri2(tril, x, dn):
    hi, lo = _split2(x)
    m = tril.astype(BF16)
    return (lax.dot_general(m, hi, dn, preferred_element_type=F32) + lax.dot_general(m, lo, dn, preferred_element_type=F32))


@jax.custom_vjp
def _cumsum_rows(tril, x):
    return _tri2(tril, x, _BNN)


_cumsum_rows.defvjp(lambda tril, x: (_tri2(tril, x, _BNN), tril),
                    lambda tril, g: (jnp.zeros_like(tril), _tri2(tril, g, _BTN)))


def _rw_chunk_consts():
    c2 = 2 * RW_C
    row = lax.broadcasted_iota(jnp.int32, (c2, c2), 0)
    col = lax.broadcasted_iota(jnp.int32, (c2, c2), 1)
    same = (row // RW_C) == (col // RW_C)
    strict = (same & (row > col)).astype(F32)
    incl = (same & (row >= col)).astype(F32)
    eye = (row == col).astype(F32)
    tr = lax.broadcasted_iota(jnp.int32, (RW_C, RW_C), 0)
    tc = lax.broadcasted_iota(jnp.int32, (RW_C, RW_C), 1)
    tril = (tr >= tc).astype(F32)
    lane = lax.broadcasted_iota(jnp.int32, (1, LANES), 1)
    hm = [(lane // HEAD == h).astype(F32) for h in (0, 1)]
    return strict, incl, eye, tril, hm


def _rw_chunk(r, lw, k, v, n, b, s2, consts):
    strict, incl, eye, tril, hm = consts
    two = lambda x: jnp.concatenate([x * hm[0], x * hm[1]], axis=1)
    cum = _cumsum_rows(jnp.broadcast_to(tril, (4, RW_C, RW_C)), lw)
    grow, shrink = jnp.exp(-cum), jnp.exp(cum)
    n2, r2 = two(n * jnp.exp(cum - lw)), two(r * shrink)
    b2, k2, v2 = two(b * grow), two(k * grow), two(v)
    p = _pnt(n2, b2) * strict
    x2 = _pnt(n2, s2) + _pnn(_pnt(n2, k2) * strict, v2)
    t_inv, a = eye + p, p
    for _ in range(RW_C.bit_length() - 2):
        a = _pnn(a, a)
        t_inv = t_inv + _pnn(t_inv, a)
    u2 = _pnn(t_inv, x2)
    y2 = _pnt(r2, s2) + _pnn(_pnt(r2, b2) * incl, u2) + _pnn(_pnt(r2, k2) * incl, v2)
    s2_new = (s2 + _ptn(u2, b2) + _ptn(v2, k2)) * jnp.exp(jnp.sum(lw, axis=1, keepdims=True))
    return jnp.sum(y2.reshape(4, 2, RW_C, LANES), axis=1), s2_new


def _pairs(ref):
    return jnp.stack([ref[:, p * LANES:(p + 1) * LANES] for p in range(4)])


def _rw_chunk_fwd(mixed, lw, k, n, b, *, name, side=None):
    t = lw.shape[0]
    nc = t // RW_C

    def body(r_ref, v_ref, lw_ref, k_ref, n_ref, b_ref, y_ref, sin_ref, s_s):
        @pl.when(pl.program_id(0) == 0)
        def _():
            s_s[...] = jnp.zeros_like(s_s)

        s2 = s_s[...]
        sin_ref[0] = s2
        y, s2 = _rw_chunk(*[_pairs(x) for x in (r_ref, lw_ref, k_ref, v_ref, n_ref, b_ref)], s2, _rw_chunk_consts())
        for p in range(4):
            y_ref[:, p * LANES:(p + 1) * LANES] = y[p]
        s_s[...] = s2

    blk = lambda c: pl.BlockSpec((RW_C, 4 * LANES), functools.partial(lambda i, c: (i, c), c=c))
    return _call_with_side(
        body, side, name=name, grid=(nc,), semantics=("arbitrary",),
        in_specs=[blk(0), blk(2), blk(0), blk(0), blk(0), blk(0)],
        out_specs=[blk(0), pl.BlockSpec((1, 4, LANES, LANES), lambda i: (i, 0, 0, 0))],
        out_shape=[jax.ShapeDtypeStruct((t, 4 * LANES), F32), jax.ShapeDtypeStruct((nc, 4, LANES, LANES), F32)],
        scratch_shapes=[pltpu.VMEM((4, LANES, LANES), F32)],
        operands=(mixed, mixed, lw, k, n, b))


def _call_with_side(body, side, *, name, grid, semantics, in_specs, out_specs, out_shape, scratch_shapes, operands):
    if side is None:
        return pl.pallas_call(body, name=name, grid=grid, in_specs=in_specs, out_specs=out_specs, out_shape=out_shape,
                              scratch_shapes=scratch_shapes, compiler_params=_params(semantics))(*operands)
    srcs, per_dest = side
    ns, ni, no, nscr = len(srcs), len(in_specs), len(out_specs), len(scratch_shapes)

    def full_body(*refs):
        ins, side_in = refs[:ni], refs[ni:ni + ns]
        outs, side_out = refs[ni + ns:ni + ns + no], refs[ni + ns + no:ni + 2 * ns + no]
        scratch, sems = refs[ni + 2 * ns + no:ni + 2 * ns + no + nscr], refs[ni + 2 * ns + no + nscr:]

        ids = [pl.program_id(a) for a in range(len(grid))]
        first = functools.reduce(jnp.logical_and, [i == 0 for i in ids])
        last = functools.reduce(jnp.logical_and, [i == n - 1 for i, n in zip(ids, grid)])

        @pl.when(first)
        def _():
            _exchange(side_in, side_out, sems, per_dest, start=True, wait=False)

        body(*ins, *outs, *scratch)

        @pl.when(last)
        def _():
            _exchange(side_in, side_out, sems, per_dest, start=False, wait=True)

    res = pl.pallas_call(
        full_body, name=name, grid=grid, in_specs=list(in_specs) + [_ANY] * ns,
        out_specs=list(out_specs) + [_ANY] * ns, out_shape=list(out_shape) + _exchange_out_shapes(srcs, per_dest),
        scratch_shapes=list(scratch_shapes) + _exchange_sems(ns), compiler_params=_params(("arbitrary",) * len(grid)),
    )(*operands, *srcs)
    return list(res[:no]) + [list(res[no:])]


def _rw_chunk_bwd(mixed, lw, k, n, b, s_in, dy, dr0, dk0, dv0, *, name, side=None):
    t = lw.shape[0]
    nc = t // RW_C

    def body(r_ref, v_ref, lw_ref, k_ref, n_ref, b_ref, sin_ref, dy_ref, dr0_ref, dk0_ref, dv0_ref,
             dr_ref, dlw_ref, dk_ref, dv_ref, dn_ref, db_ref, ds_s):
        @pl.when(pl.program_id(0) == 0)
        def _():
            ds_s[...] = jnp.zeros_like(ds_s)

        consts = _rw_chunk_consts()
        args = [_pairs(x) for x in (r_ref, lw_ref, k_ref, v_ref, n_ref, b_ref)] + [sin_ref[0]]
        _, vjp = jax.vjp(lambda *a: _rw_chunk(*a, consts), *args)
        dr, dlw, dk, dv, dn, db, ds = vjp((_pairs(dy_ref), ds_s[...]))
        for p in range(4):
            sl = slice(p * LANES, (p + 1) * LANES)
            dr_ref[:, sl] = dr[p] + dr0_ref[:, sl]
            dlw_ref[:, sl] = dlw[p]
            dk_ref[:, sl] = dk[p] + dk0_ref[:, sl]
            dv_ref[:, sl] = dv[p] + dv0_ref[:, sl]
            dn_ref[:, sl] = dn[p]
            db_ref[:, sl] = db[p]
        ds_s[...] = ds

    blk = lambda c: pl.BlockSpec((RW_C, 4 * LANES), functools.partial(lambda i, c: (nc - 1 - i, c), c=c))
    return _call_with_side(
        body, side, name=name, grid=(nc,), semantics=("arbitrary",),
        in_specs=[blk(0), blk(2), blk(0), blk(0), blk(0), blk(0),
                  pl.BlockSpec((1, 4, LANES, LANES), lambda i: (nc - 1 - i, 0, 0, 0)), blk(0), blk(0), blk(0), blk(0)],
        out_specs=[blk(0)] * 6,
        out_shape=[jax.ShapeDtypeStruct((t, 4 * LANES), F32)] * 6,
        scratch_shapes=[pltpu.VMEM((4, LANES, LANES), F32)],
        operands=(mixed, mixed, lw, k, n, b, s_in, dy, dr0, dk0, dv0))


def _f_rms_res(x, g):
    return _f_rms(x, g)[0], x


def _final(x, g, target, *, bt, name):
    t, d = x.shape

    def body(x_ref, g_ref, t_ref, dx_ref, loss_ref, dg_ref):
        tgt = t_ref[...]

        def f(xv, gv):
            err = _f_rms(xv, gv)[0] - tgt
            return 0.5 * jnp.mean(err * err, axis=-1, keepdims=True)

        row_loss, vjp = jax.vjp(f, x_ref[...], g_ref[...])
        dx, dg = vjp(jnp.ones_like(row_loss))
        dx_ref[...] = dx

        @pl.when(pl.program_id(0) == 0)
        def _():
            loss_ref[...] = jnp.zeros_like(loss_ref)
            dg_ref[...] = jnp.zeros_like(dg_ref)

        loss_ref[...] += jnp.broadcast_to(jnp.sum(row_loss, axis=0, keepdims=True), (1, LANES))
        dg_ref[...] += dg

    blk = pl.BlockSpec((bt, d), lambda i: (i, 0))
    return pl.pallas_call(
        body, name=name, grid=(t // bt,),
        in_specs=[blk, pl.BlockSpec((1, d), lambda i: (0, 0)), blk],
        out_specs=[blk, pl.BlockSpec((1, LANES), lambda i: (0, 0)), pl.BlockSpec((1, d), lambda i: (0, 0))],
        out_shape=[jax.ShapeDtypeStruct((t, d), F32), jax.ShapeDtypeStruct((1, LANES), F32),
                   jax.ShapeDtypeStruct((1, d), F32)],
        compiler_params=_params(("arbitrary",)),
    )(x, g, target)


ADAMW_BLOCK_BYTES = 1 << 20


def _adamw(w, g, m, v, *, name, block=None):
    shape = w.shape
    if block is not None:
        return _adamw_blocks(w, g, m, v, block, name)
    c = shape[-1]
    shape3 = (1,) * (3 - len(shape)) + shape if len(shape) <= 3 else (-1,) + shape[-2:]
    args = [a.reshape(shape3) for a in (w, g, m, v)]
    lead, r, _ = args[0].shape
    br = r
    if r * c * 4 > ADAMW_BLOCK_BYTES:
        cands = [b for b in range(8, r, 8) if r % b == 0 and b * c * 4 <= ADAMW_BLOCK_BYTES]
        br = max(cands) if cands else r
    outs = _adamw_blocks(*args, (1, br, c), name)
    return tuple(o.reshape(shape) for o in outs)


def _adamw_blocks(w, g, m, v, block, name):
    shape = w.shape
    assert all(s % b == 0 for s, b in zip(shape, block))

    def body(w_ref, g_ref, m_ref, v_ref, d_ref, nm_ref, nv_ref):
        gv = g_ref[...]
        m_new = ADAM_B1 * m_ref[...] + (1.0 - ADAM_B1) * gv
        v_new = ADAM_B2 * v_ref[...] + (1.0 - ADAM_B2) * (gv * gv)
        m_hat = m_new / (1.0 - ADAM_B1 ** ADAM_STEP)
        v_hat = v_new / (1.0 - ADAM_B2 ** ADAM_STEP)
        d_ref[...] = -ADAM_LR * (m_hat / (jnp.sqrt(v_hat) + ADAM_EPS) + ADAM_WD * w_ref[...])
        nm_ref[...] = m_new
        nv_ref[...] = v_new

    blk = pl.BlockSpec(tuple(block), lambda *ids: ids)
    return pl.pallas_call(
        body, name=name, grid=tuple(s // b for s, b in zip(shape, block)), in_specs=[blk] * 4, out_specs=[blk] * 3,
        out_shape=[jax.ShapeDtypeStruct(shape, F32)] * 3,
        compiler_params=_params(("parallel",) * len(shape)),
    )(w, g, m, v)


BT = 256
BC = 128


def _layer_rows(x, proj, s):
    s = {k: s.get(k) for k in ("y_sb_raw", "y_ssd_raw", "mixed", "ys", "k2", "p_sb", "p_ssd", "p_rw")}
    return dict(
        rms=[(x, D_MODEL, 0)],
        sb_gate=[(s["y_sb_raw"], 512, 0), (proj, 512, 3)],
        ssd_norm=[(s["y_ssd_raw"], 1024, 0), (proj, 1024, C_Z // 1024)],
        rw_pre=[(s["mixed"], 512, 1), (s["mixed"], LANES, 16)],
        rw_post=[(s["ys"], 512, 0), (s["mixed"], 512, 0), (s["k2"], 512, 0), (s["mixed"], 512, 2), (s["mixed"], 512, 3)],
        merge=[(s["p_sb"], 1024, 0), (s["p_ssd"], 1024, 0), (s["p_rw"], 1024, 0),
               (proj, 1024, 3), (proj, 1024, 4), (proj, 1024, 5)],
    )


def _layer_fwd(x, p, nm, side=None):
    s = {}
    (s["h"],) = _rowwise(_f_rms, [(x, D_MODEL, 0)], [p["norm_g"]], [D_MODEL], bt=BT, name=nm + "rms")
    proj = s["proj"] = _mm(s["h"], p["w_in"], name=nm + "proj")
    s["y_sb_raw"], s["lt"] = _sb2_fwd(proj, name=nm + "sb")
    s["xc"] = _colwise(_f_conv, proj, C_XBC, XBC_COLS, p["conv"], bc=BC, name=nm + "conv")
    s["y_ssd_raw"], s["hin"] = _ssd_fwd(s["xc"], proj, p["dt_bias"], p["a_log"], p["d_skip"], name=nm + "ssd")
    s["mixed"] = _colwise(_f_rw_mix, proj, C_RW, RW_COLS, [p["rw_mu"]], bc=BC, name=nm + "mix")
    s["w"], s["k2"], s["n"], s["b"] = _rowwise(_f_rw_pre, [(s["mixed"], 512, 1), (s["mixed"], LANES, 16)], p["rw_pre"],
                                               [512] * 4, bt=BT, name=nm + "rwpre")
    s["ys"], s["st"], *exchanged = _rw_chunk_fwd(s["mixed"], s["w"], s["k2"], s["n"], s["b"], name=nm + "scan", side=side)
    rows = _layer_rows(x, proj, s)
    (s["y_sb"],) = _rowwise(_f_sb_gate, rows["sb_gate"], [], [512], bt=BT, name=nm + "sbgate")
    (s["y_ssd"],) = _rowwise(_f_ssd_norm, rows["ssd_norm"], [p["ssd_norm_g"]], [1024], bt=BT, name=nm + "ssdnorm")
    (s["y_rw"],) = _rowwise(_f_rw_post, rows["rw_post"], p["rw_post"], [512], bt=BT, name=nm + "rwpost")
    s["p_sb"] = _mm(s["y_sb"], p["w_out_sb"], name=nm + "osb")
    s["p_ssd"] = _mm(s["y_ssd"], p["w_out_ssd"], name=nm + "ossd")
    s["p_rw"] = _mm(s["y_rw"], p["w_out_rw"], name=nm + "orw")
    (s["merged"],) = _rowwise(_f_merge, _layer_rows(x, proj, s)["merge"], [], [1024], bt=BT, name=nm + "merge")
    return _mm(s["merged"], p["w_o"], add=x, name=nm + "wo"), s, (exchanged[0] if exchanged else None)


def _layer_bwd(x, dx_out, p, s, nm, side=None, side_late=None):
    g = {}
    proj = s["proj"]
    rows = _layer_rows(x, proj, s)
    g["w_o"] = _mm(s["merged"], dx_out, ta=True, name=nm + "g_wo")
    d_merged = _mm(dx_out, p["w_o"], tb=True, name=nm + "d_merged")
    dp_sb, dp_ssd, dp_rw, d_gates = _rowwise_bwd(_f_merge, rows["merge"], [], [(d_merged, 1024, 0)], bt=BT,
                                                 name=nm + "merge_b", groups=[[0], [1], [2], [3, 4, 5]])
    g["w_out_sb"] = _mm(s["y_sb"], dp_sb, ta=True, name=nm + "g_osb")
    g["w_out_ssd"] = _mm(s["y_ssd"], dp_ssd, ta=True, name=nm + "g_ossd")
    g["w_out_rw"] = _mm(s["y_rw"], dp_rw, ta=True, name=nm + "g_orw")
    dy_sb = _mm(dp_sb, p["w_out_sb"], tb=True, name=nm + "d_ysb")
    dy_ssd = _mm(dp_ssd, p["w_out_ssd"], tb=True, name=nm + "d_yssd")
    dy_rw = _mm(dp_rw, p["w_out_rw"], tb=True, name=nm + "d_yrw")
    dy_sb_raw, d_sbgate = _rowwise_bwd(_f_sb_gate, rows["sb_gate"], [], [(dy_sb, 512, 0)], bt=BT, name=nm + "sbgate_b")
    dq, dk, dv = _sb2_bwd(proj, dy_sb_raw, s["lt"], name=nm + "sb_b")
    dy_ssd_raw, dz, g["ssd_norm_g"] = _rowwise_bwd(_f_ssd_norm, rows["ssd_norm"], [p["ssd_norm_g"]],
                                                   [(dy_ssd, 1024, 0)], bt=BT, name=nm + "ssdnorm_b")
    dxc, ddtr, g["dt_bias"], g["a_log"], g["d_skip"] = _ssd_bwd(
        s["xc"], proj, p["dt_bias"], p["a_log"], p["d_skip"], s["hin"], dy_ssd_raw, name=nm + "ssd_b")
    conv_out = _colwise_bwd(_f_conv, proj, C_XBC, XBC_COLS, p["conv"], dxc, bc=BC, name=nm + "conv_b")
    dxbc, g["conv"] = conv_out[0], conv_out[1:]
    dys, dr0, dk0, dv0, d_rwgate, g["rw_ln_g"], g["rw_ln_b"], g["rw_r_k"] = _rowwise_bwd(
        _f_rw_post, rows["rw_post"], p["rw_post"], [(dy_rw, 512, 0)], bt=BT, name=nm + "rwpost_b")
    dr, dw, dk2, dvv, dn, db, *exchanged = _rw_chunk_bwd(s["mixed"], s["w"], s["k2"], s["n"], s["b"], s["st"], dys,
                                                         dr0, dk0, dv0, name=nm + "scan_b",
                                                         side=side(g) if side else None)
    pre_out = _rowwise_bwd(_f_rw_pre, rows["rw_pre"], p["rw_pre"],
                           [(dw, 512, 0), (dk2, 512, 0), (dn, 512, 0), (db, 512, 0)], bt=BT, name=nm + "rwpre_b")
    dkm, dlo, g["rw_pre"] = pre_out[0], pre_out[1], pre_out[2:]
    d_mixed = jnp.concatenate([dr, dkm, dvv, d_rwgate, dlo], axis=1)
    d_slab, g["rw_mu"] = _colwise_bwd(_f_rw_mix, proj, C_RW, RW_COLS, [p["rw_mu"]], d_mixed, bc=BC, name=nm + "mix_b")
    d_proj = jnp.concatenate([dq, dk, dv, d_sbgate, dz, d_gates, d_slab, ddtr, dxbc], axis=1)
    if side_late:
        g["w_in"] = _mm(s["h"], d_proj, ta=True, name=nm + "g_win")
        dh, late = _mm(d_proj, p["w_in"], tb=True, tn=1024, tk=512, name=nm + "d_h", side=side_late(g))
    else:
        g["w_in"], dh, late = *_proj_bwd(s["h"], d_proj, p["w_in"], name=nm + "proj_b"), None
    dx, g["norm_g"] = _rowwise_bwd(_f_rms_res, rows["rms"], [p["norm_g"]], [(dh, D_MODEL, 0), (dx_out, D_MODEL, 0)],
                                   bt=BT, name=nm + "rms_b")
    return dx, g, (exchanged[0] if exchanged else None), late


MESH = pl.DeviceIdType.MESH
N_DEV = 8
_ANY = pl.BlockSpec(memory_space=pl.ANY)


def _here():
    x, y, c = lax.axis_index("x"), lax.axis_index("y"), lax.axis_index("c")
    return x, y, c, [(1 - x, y), (x, 1 - y), (1 - x, 1 - y)]


def _chip_exchange(srcs, *, per_dest, name):
    n = len(srcs)

    def body(*refs):
        _exchange(refs[:n], refs[n:2 * n], refs[2 * n:], per_dest, start=True, wait=True)

    return pl.pallas_call(
        body, name=name, in_specs=[_ANY] * n, out_specs=[_ANY] * n,
        out_shape=_exchange_out_shapes(srcs, per_dest), scratch_shapes=_exchange_sems(n),
    )(*srcs)


def _by_layer(per_dest):
    return per_dest is not True and per_dest is not False


def _exchange_out_shapes(srcs, per_dest):
    lead = (4, 2) if _by_layer(per_dest) else (4,)
    return [jax.ShapeDtypeStruct(lead + s.shape[-2:], s.dtype) for s in srcs]


def _exchange_sems(n):
    return [pltpu.SemaphoreType.DMA((3 * n,)), pltpu.SemaphoreType.DMA((3 * n,)), pltpu.SemaphoreType.DMA((n,))]


def _exchange(src_refs, out_refs, sems, per_dest, *, start, wait):
    send_sems, recv_sems, local_sems = sems
    x, y, c, chips = _here()
    me = 2 * x + y
    owns, sends, recvs = [], [], []
    for a, (src_ref, out_ref) in enumerate(zip(src_refs, out_refs)):
        if per_dest is True:
            pick = lambda q, s=src_ref: s.at[q]
        elif per_dest is False:
            pick = lambda q, s=src_ref: s.at[c]
        else:
            pick = lambda q, s=src_ref: s.at[per_dest].at[c]
        any_block = src_ref.at[0] if len(src_ref.shape) == 3 else src_ref.at[0].at[0]
        if _by_layer(per_dest):
            slot = lambda q, o=out_ref: o.at[q].at[c]
        else:
            slot = lambda q, o=out_ref: o.at[q]
        owns.append(pltpu.make_async_copy(pick(me), slot(me), local_sems.at[a]))
        for j, (px, py) in enumerate(chips):
            sends.append(pltpu.make_async_remote_copy(
                pick(2 * px + py), slot(me), send_sems.at[3 * a + j], recv_sems.at[3 * a + j],
                device_id=(px, py, c), device_id_type=MESH))
            recvs.append(pltpu.make_async_remote_copy(
                any_block, slot(2 * px + py), send_sems.at[3 * a + j], recv_sems.at[3 * a + j],
                device_id=(px, py, c), device_id_type=MESH))
    if start:
        for cp in owns + sends:
            cp.start()
    if wait:
        for cp in recvs:
            cp.wait_recv()
        for cp in sends:
            cp.wait_send()
        for cp in owns:
            cp.wait()


def _sibling_fill(bufs, *, name):
    n = len(bufs)

    def body(*refs):
        in_refs, out_refs, send_sems, recv_sems = refs[:n], refs[n:2 * n], refs[2 * n], refs[2 * n + 1]
        x, y, c, _ = _here()
        copies = []
        for a, (src, dst) in enumerate(zip(in_refs, out_refs)):
            for q in range(4):
                copies.append(pltpu.make_async_remote_copy(
                    src.at[q].at[c], dst.at[q].at[c], send_sems.at[4 * a + q], recv_sems.at[4 * a + q],
                    device_id=(x, y, 1 - c), device_id_type=MESH))
        for cp in copies:
            cp.start()
        for a, (src, dst) in enumerate(zip(in_refs, out_refs)):
            for q in range(4):
                pltpu.make_async_remote_copy(
                    src.at[q].at[c], dst.at[q].at[1 - c], send_sems.at[4 * a + q], recv_sems.at[4 * a + q],
                    device_id=(x, y, 1 - c), device_id_type=MESH).wait_recv()
        for cp in copies:
            cp.wait_send()

    return pl.pallas_call(
        body, name=name, in_specs=[_ANY] * n, out_specs=[_ANY] * n,
        out_shape=[jax.ShapeDtypeStruct(b.shape, b.dtype) for b in bufs],
        input_output_aliases={a: a for a in range(n)},
        scratch_shapes=[pltpu.SemaphoreType.DMA((4 * n,)), pltpu.SemaphoreType.DMA((4 * n,))],
    )(*bufs)


def _sibling_swap(srcs, *, other_slot, name):
    n = len(srcs)

    def body(*refs):
        src_refs, out_refs, send_sems, recv_sems = refs[:n], refs[n:2 * n], refs[2 * n], refs[2 * n + 1]
        x, y, c, _ = _here()
        copies = [pltpu.make_async_remote_copy(s.at[1 - c] if other_slot else s, o, send_sems.at[a], recv_sems.at[a],
                                               device_id=(x, y, 1 - c), device_id_type=MESH)
                  for a, (s, o) in enumerate(zip(src_refs, out_refs))]
        for cp in copies:
            cp.start()
        for cp in copies:
            cp.wait()

    return pl.pallas_call(
        body, name=name, in_specs=[_ANY] * n, out_specs=[_ANY] * n,
        out_shape=[jax.ShapeDtypeStruct(s.shape[1:] if other_slot else s.shape, s.dtype) for s in srcs],
        scratch_shapes=[pltpu.SemaphoreType.DMA((n,)), pltpu.SemaphoreType.DMA((n,))],
    )(*srcs)


def _allgather_small(v, *, reduce, name):
    r = v.shape[0]

    def body(v_ref, out_ref, *rest):
        send_sems, recv_sems, local_sem = rest[-3:]
        x, y, c, chips = _here()
        me, sibling = (x, y, c), (x, y, 1 - c)

        def slot(px, py, pc):
            return out_ref.at[4 * px + 2 * py + pc]

        def copy(k, block, to, src=None):
            return pltpu.make_async_remote_copy(
                src_ref=slot(*block) if src is None else src, dst_ref=slot(*block),
                send_sem=send_sems.at[k], recv_sem=recv_sems.at[k], device_id=to, device_id_type=MESH)

        mine = pltpu.make_async_copy(v_ref, slot(*me), local_sem)
        mine.start()
        first = [copy(0, me, sibling, src=v_ref)]
        first += [copy(1 + j, me, (*chip, c), src=v_ref) for j, chip in enumerate(chips)]
        for cp in first:
            cp.start()
        passed = [copy(4 + j, (*chip, c), sibling) for j, chip in enumerate(chips)]
        for j, chip in enumerate(chips):
            copy(1 + j, (*chip, c), me).wait_recv()
            passed[j].start()
        copy(0, sibling, me).wait_recv()
        for j, chip in enumerate(chips):
            copy(4 + j, (*chip, 1 - c), me).wait_recv()
        for cp in first + passed:
            cp.wait_send()
        mine.wait()
        if reduce:
            total = out_ref[0]
            for d in range(1, N_DEV):
                total = total + out_ref[d]
            rest[0][...] = total

    vm = pl.BlockSpec(memory_space=pltpu.VMEM)
    out_shape = [jax.ShapeDtypeStruct((N_DEV, r, LANES), F32)] + ([jax.ShapeDtypeStruct((r, LANES), F32)] if reduce else [])
    return pl.pallas_call(
        body, name=name, in_specs=[vm], out_specs=[vm] * len(out_shape), out_shape=out_shape,
        scratch_shapes=[pltpu.SemaphoreType.DMA((7,)), pltpu.SemaphoreType.DMA((7,)), pltpu.SemaphoreType.DMA],
        compiler_params=pltpu.CompilerParams(vmem_limit_bytes=VMEM_LIMIT),
    )(v)


REDUCE_BLOCK_BYTES = 2 << 20


def _reduce_rows(r, c):
    cands = [b for b in range(16, r + 1, 16) if r % b == 0 and b * c * 4 <= REDUCE_BLOCK_BYTES]
    return max(cands)


def _add_halves(mine2, other, c_idx, *, name):
    _, nq, r, c = mine2.shape
    br = _reduce_rows(r, c)

    def body(c_ref, a_ref, b_ref, o_ref):
        o_ref[...] = (a_ref[0] + b_ref[...]).astype(o_ref.dtype)

    blk = pl.BlockSpec((1, br, c), lambda q, i, c_ref: (q, i, 0))
    return pl.pallas_call(
        body, name=name,
        grid_spec=pltpu.PrefetchScalarGridSpec(
            num_scalar_prefetch=1, grid=(nq, r // br),
            in_specs=[pl.BlockSpec((1, 1, br, c), lambda q, i, c_ref: (c_ref[0], q, i, 0)), blk],
            out_specs=blk),
        out_shape=jax.ShapeDtypeStruct((nq, r, c), BF16),
        compiler_params=_params(("parallel", "parallel")),
    )(c_idx, mine2, other)


def _sum_chips(parts, c_idx, *, name):
    _, r, c = parts.shape
    br = _reduce_rows(r, c)

    def body(c_ref, p_ref, o_ref):
        total = p_ref[0].astype(F32)
        for q in range(1, 4):
            total = total + p_ref[q].astype(F32)
        o_ref[0] = total

    return pl.pallas_call(
        body, name=name,
        grid_spec=pltpu.PrefetchScalarGridSpec(
            num_scalar_prefetch=1, grid=(r // br,),
            in_specs=[pl.BlockSpec((4, br, c), lambda i, c_ref: (0, i, 0))],
            out_specs=pl.BlockSpec((1, br, c), lambda i, c_ref: (c_ref[0], i, 0))),
        out_shape=jax.ShapeDtypeStruct((2, r, c), F32),
        compiler_params=_params(("parallel",)),
    )(c_idx, parts)


def _sibling_fill_halves(bufs, *, name):
    n = len(bufs)

    def body(*refs):
        in_refs, out_refs, send_sems, recv_sems = refs[:n], refs[n:2 * n], refs[2 * n], refs[2 * n + 1]
        x, y, c, _ = _here()
        copies = [pltpu.make_async_remote_copy(src.at[c], dst.at[c], send_sems.at[a], recv_sems.at[a],
                                               device_id=(x, y, 1 - c), device_id_type=MESH)
                  for a, (src, dst) in enumerate(zip(in_refs, out_refs))]
        for cp in copies:
            cp.start()
        for a, (src, dst) in enumerate(zip(in_refs, out_refs)):
            pltpu.make_async_remote_copy(src.at[c], dst.at[1 - c], send_sems.at[a], recv_sems.at[a],
                                         device_id=(x, y, 1 - c), device_id_type=MESH).wait_recv()
        for cp in copies:
            cp.wait_send()

    return pl.pallas_call(
        body, name=name, in_specs=[_ANY] * n, out_specs=[_ANY] * n,
        out_shape=[jax.ShapeDtypeStruct(b.shape, b.dtype) for b in bufs],
        input_output_aliases={a: a for a in range(n)},
        scratch_shapes=[pltpu.SemaphoreType.DMA((n,)), pltpu.SemaphoreType.DMA((n,))],
    )(*bufs)


BIG = ("w_in", "w_out_sb", "w_out_ssd", "w_out_rw", "w_o")
BIG_AXIS = {"w_in": 2, "w_out_sb": 2, "w_out_ssd": 1, "w_out_rw": 2, "w_o": 1}
SMALL_SHARDED = {"conv_w": 320, "rw_w_up": 128, "rw_a_up": 128}
SMALL = ("norm_g", "conv_w", "conv_b", "dt_bias", "a_log", "d_skip", "ssd_norm_g", "rw_mu", "rw_w0", "rw_w_up",
         "rw_a0", "rw_a_up", "rw_k_k", "rw_k_a", "rw_r_k", "rw_ln_g", "rw_ln_b", "final_g")


def _rows_of(a):
    flat = a.reshape(-1)
    pad = (-flat.shape[0]) % LANES
    return jnp.pad(flat, (0, pad)).reshape(-1, LANES)


def _pack_rows(arrays, multiple=8):
    rows = jnp.concatenate([_rows_of(a) for a in arrays], axis=0)
    pad = (-rows.shape[0]) % multiple
    return jnp.pad(rows, ((0, pad), (0, 0)))


def _unpack_rows(rows, shapes):
    out, off = [], 0
    for shp in shapes:
        n = 1
        for d in shp:
            n *= d
        nr = -(-n // LANES)
        out.append(rows[off:off + nr].reshape(-1)[:n].reshape(shp))
        off += nr
    return out


COL_MAP = ((0, 3072, 0), (3072, 4352, C_XBC), (4352, 4368, C_DT), (4368, 6544, C_RW), (6544, 9616, C_GATES))
SHARD_COLS = N_IN // 4


def _w_in_from_shards(shards):
    pieces = []
    for a, b, dst in sorted(COL_MAP, key=lambda m: m[2]):
        if pieces and dst > pieces[-1][0]:
            pieces.append((dst, jnp.zeros((shards[0].shape[0], dst - pieces[-1][0]), shards[0].dtype)))
        for q in range(4):
            lo, hi = max(a, q * SHARD_COLS), min(b, (q + 1) * SHARD_COLS)
            if lo < hi:
                pieces.append((dst + hi - a, shards[q][:, lo - q * SHARD_COLS:hi - q * SHARD_COLS]))
    return jnp.concatenate([p for _, p in pieces], axis=1)


def _w_in_shard(g, q):
    pieces = []
    for a, b, dst in COL_MAP:
        lo, hi = max(a, q * SHARD_COLS), min(b, (q + 1) * SHARD_COLS)
        if lo < hi:
            pieces.append(g[:, dst + lo - a:dst + hi - a])
    return jnp.concatenate(pieces, axis=1)


def _row_halves(a):
    return a.reshape(2, a.shape[0] // 2, a.shape[1])


def kernel(x, norm_g, w_in, conv_w, conv_b, dt_bias, a_log, d_skip, ssd_norm_g, rw_mu, rw_w0, rw_w_up, rw_a0, rw_a_up, rw_k_k, rw_k_a, rw_r_k, rw_ln_g, rw_ln_b, w_out_sb, w_out_ssd, w_out_rw, w_o, final_g, loss_target, m_norm_g, m_w_in, m_conv_w, m_conv_b, m_dt_bias, m_a_log, m_d_skip, m_ssd_norm_g, m_rw_mu, m_rw_w0, m_rw_w_up, m_rw_a0, m_rw_a_up, m_rw_k_k, m_rw_k_a, m_rw_r_k, m_rw_ln_g, m_rw_ln_b, m_w_out_sb, m_w_out_ssd, m_w_out_rw, m_w_o, m_final_g, v_norm_g, v_w_in, v_conv_w, v_conv_b, v_dt_bias, v_a_log, v_d_skip, v_ssd_norm_g, v_rw_mu, v_rw_w0, v_rw_w_up, v_rw_a0, v_rw_a_up, v_rw_k_k, v_rw_k_a, v_rw_r_k, v_rw_ln_g, v_rw_ln_b, v_w_out_sb, v_w_out_ssd, v_w_out_rw, v_w_o, v_final_g):
    names = ("norm_g", "w_in", "conv_w", "conv_b", "dt_bias", "a_log", "d_skip", "ssd_norm_g", "rw_mu", "rw_w0",
             "rw_w_up", "rw_a0", "rw_a_up", "rw_k_k", "rw_k_a", "rw_r_k", "rw_ln_g", "rw_ln_b", "w_out_sb",
             "w_out_ssd", "w_out_rw", "w_o", "final_g")
    w_loc = dict(zip(names, (norm_g, w_in, conv_w, conv_b, dt_bias, a_log, d_skip, ssd_norm_g, rw_mu, rw_w0, rw_w_up,
                             rw_a0, rw_a_up, rw_k_k, rw_k_a, rw_r_k, rw_ln_g, rw_ln_b, w_out_sb, w_out_ssd, w_out_rw,
                             w_o, final_g)))
    m_loc = dict(zip(names, (m_norm_g, m_w_in, m_conv_w, m_conv_b, m_dt_bias, m_a_log, m_d_skip, m_ssd_norm_g,
                             m_rw_mu, m_rw_w0, m_rw_w_up, m_rw_a0, m_rw_a_up, m_rw_k_k, m_rw_k_a, m_rw_r_k,
                             m_rw_ln_g, m_rw_ln_b, m_w_out_sb, m_w_out_ssd, m_w_out_rw, m_w_o, m_final_g)))
    v_loc = dict(zip(names, (v_norm_g, v_w_in, v_conv_w, v_conv_b, v_dt_bias, v_a_log, v_d_skip, v_ssd_norm_g,
                             v_rw_mu, v_rw_w0, v_rw_w_up, v_rw_a0, v_rw_a_up, v_rw_k_k, v_rw_k_a, v_rw_r_k,
                             v_rw_ln_g, v_rw_ln_b, v_w_out_sb, v_w_out_ssd, v_w_out_rw, v_w_o, v_final_g)))
    chip = 2 * lax.axis_index("x") + lax.axis_index("y")
    core = lax.axis_index("c")

    as_sent = [w_loc[n].astype(BF16).reshape(DEPTH, 2, w_loc[n].shape[1] // 2, w_loc[n].shape[2]) for n in BIG]

    def gathered(mine, nm):
        out = {}
        for n, buf in zip(BIG, _sibling_fill(mine, name=nm)):
            shards = buf.reshape(4, 2 * buf.shape[2], buf.shape[3])
            out[n] = (_w_in_from_shards([shards[q] for q in range(4)]) if n == "w_in"
                      else jnp.concatenate([shards[q] for q in range(4)], axis=BIG_AXIS[n] - 1))
        return out

    full = {}
    sm_names = tuple(SMALL_SHARDED)
    sm_shapes = [w_loc[n].shape for n in sm_names]
    (got_sm,) = _allgather_small(_pack_rows([w_loc[n] for n in sm_names]), reduce=False, name="gather_small")
    per_chip = [_unpack_rows(got_sm[4 * (q // 2) + 2 * (q % 2)], sm_shapes) for q in range(4)]
    for i, n in enumerate(sm_names):
        full[n] = jnp.concatenate([per_chip[q][i] for q in range(4)], axis=-1)

    def pad16(a):
        return jnp.zeros((1, LANES), F32).at[0, :SSD_HEADS].set(a)

    def layer_params(i, big):
        row = lambda n: w_loc[n][i].reshape(1, -1)
        cw = full["conv_w"][i]
        return dict(
            norm_g=row("norm_g"), w_in=big["w_in"], conv=[cw[k][None] for k in range(4)] + [row("conv_b")],
            dt_bias=pad16(dt_bias[i]), a_log=pad16(a_log[i]), d_skip=pad16(d_skip[i]),
            ssd_norm_g=row("ssd_norm_g"), rw_mu=row("rw_mu"),
            rw_pre=[row("rw_w0"), jnp.zeros((LANES, 512), F32).at[:HEAD].set(full["rw_w_up"][i]), row("rw_a0"),
                    jnp.zeros((LANES, 512), F32).at[HEAD:].set(full["rw_a_up"][i]), row("rw_k_k"), row("rw_k_a")],
            rw_post=[row("rw_ln_g"), row("rw_ln_b"), row("rw_r_k")],
            w_out_sb=big["w_out_sb"], w_out_ssd=big["w_out_ssd"], w_out_rw=big["w_out_rw"], w_o=big["w_o"])

    c_idx = core.reshape(1).astype(jnp.int32)

    def reduce_prepare(items, nm):
        sends = []
        for g, n, _ in items:
            per_chip = ([_w_in_shard(g[n], q) for q in range(4)] if n == "w_in"
                        else jnp.split(g[n], 4, axis=BIG_AXIS[n] - 1))
            sends.append(jnp.stack([_row_halves(p) for p in per_chip], axis=1))
        others = _sibling_swap(sends, other_slot=True, name=nm + "sibling")
        return [_add_halves(s, o, c_idx, name=nm + "add_" + lab) for (_, _, lab), s, o in zip(items, sends, others)]

    def reduce_finish(exchanged, labels, nm):
        halves = [_sum_chips(p, c_idx, name=nm + "sum_" + lab) for lab, p in zip(labels, exchanged)]
        whole = _sibling_fill_halves(halves, name=nm + "join")
        return {lab: b.reshape(2 * b.shape[1], b.shape[2]) for lab, b in zip(labels, whole)}

    assert DEPTH == 2
    out_proj = BIG[1:]
    params, xs, saved, grads = [None] * 2, [x[0], None, None], [None] * 2, [None] * 2
    params[0] = layer_params(0, gathered(_chip_exchange(as_sent, per_dest=0, name="gather_l0"), "gather_l0_join"))
    xs[1], saved[0], got = _layer_fwd(xs[0], params[0], "l0_", side=(as_sent, 1))
    params[1] = layer_params(1, gathered(got, "gather_l1_join"))
    xs[2], saved[1], _ = _layer_fwd(xs[1], params[1], "l1_")
    dx, loss_row, g_final = _final(xs[2], final_g.reshape(1, -1), loss_target[0], bt=BT, name="final")
    dx, grads[1], _, _ = _layer_bwd(xs[1], dx, params[1], saved[1], "l1_")
    early = lambda g: [(grads[1], n, "l1_" + n) for n in BIG] + [(g, n, "l0_" + n) for n in out_proj]
    dx, grads[0], got, got_late = _layer_bwd(
        xs[0], dx, params[0], saved[0], "l0_",
        side=lambda g: (reduce_prepare(early(g), "reduce_early_"), True),
        side_late=lambda g: (reduce_prepare([(g, "w_in", "l0_w_in")], "reduce_late_"), True))
    total = reduce_finish(got + got_late, [lab for _, _, lab in early(None)] + ["l0_w_in"], "reduce_")
    totals = [{n: total[f"l{i}_" + n] for n in BIG} for i in range(DEPTH)]

    def stacked(fn):
        return jnp.stack([fn(grads[i]) for i in range(DEPTH)])

    g_loc = {
        "norm_g": stacked(lambda g: g["norm_g"][0]),
        "conv_w": stacked(lambda g: jnp.concatenate(g["conv"][:4], axis=0)),
        "conv_b": stacked(lambda g: g["conv"][4][0]),
        "dt_bias": stacked(lambda g: g["dt_bias"][0, :SSD_HEADS]),
        "a_log": stacked(lambda g: g["a_log"][0, :SSD_HEADS]),
        "d_skip": stacked(lambda g: g["d_skip"][0, :SSD_HEADS]),
        "ssd_norm_g": stacked(lambda g: g["ssd_norm_g"][0]),
        "rw_mu": stacked(lambda g: g["rw_mu"][0]),
        "rw_w0": stacked(lambda g: g["rw_pre"][0][0]),
        "rw_w_up": stacked(lambda g: g["rw_pre"][1][:HEAD]),
        "rw_a0": stacked(lambda g: g["rw_pre"][2][0]),
        "rw_a_up": stacked(lambda g: g["rw_pre"][3][HEAD:]),
        "rw_k_k": stacked(lambda g: g["rw_pre"][4][0]),
        "rw_k_a": stacked(lambda g: g["rw_pre"][5][0]),
        "rw_r_k": stacked(lambda g: g["rw_r_k"].reshape(8, HEAD)),
        "rw_ln_g": stacked(lambda g: g["rw_ln_g"][0]),
        "rw_ln_b": stacked(lambda g: g["rw_ln_b"][0]),
        "final_g": g_final[0],
    }

    g_out = {n: jnp.stack([totals[0][n], totals[1][n]]) for n in BIG}

    sm_all = SMALL + ("loss",)
    sm_full_shapes = [g_loc[n].shape for n in SMALL] + [(1,)]
    _, summed = _allgather_small(_pack_rows([g_loc[n] for n in SMALL] + [loss_row[0, :1]]), reduce=True, name="reduce_small")
    sm = dict(zip(sm_all, _unpack_rows(summed, sm_full_shapes)))
    for n in SMALL:
        g_out[n] = sm[n]
    for n, wd in SMALL_SHARDED.items():
        g_out[n] = lax.dynamic_slice_in_dim(sm[n], chip * wd, wd, axis=sm[n].ndim - 1)
    loss = sm["loss"][0]

    upd = {n: _adamw(w_loc[n], g_out[n], m_loc[n], v_loc[n], name="adamw_" + n) for n in names if n != "w_in"}
    cols = SHARD_COLS // 4
    to_cols = lambda a: jnp.transpose(a, (2, 0, 1)).reshape(4, cols, DEPTH, D_MODEL)
    from_cols = lambda a: jnp.transpose(a.reshape(SHARD_COLS, DEPTH, D_MODEL), (1, 2, 0))
    g_cols = lax.optimization_barrier(to_cols(g_out["w_in"]))
    g_out["w_in"] = from_cols(g_cols)
    upd["w_in"] = tuple(from_cols(a) for a in _adamw(
        to_cols(w_loc["w_in"]), g_cols, to_cols(m_loc["w_in"]), to_cols(v_loc["w_in"]),
        name="adamw_w_in", block=(1, cols, DEPTH, D_MODEL // 2)))
    return (loss, dx[None], *[g_out[n] for n in names], *[upd[n][0] for n in names],
            *[upd[n][1] for n in names], *[upd[n][2] for n in names])
```

```python
import functools

import jax
import jax.numpy as jnp
from jax import lax
from jax.experimental import pallas as pl
from jax.experimental.pallas import tpu as pltpu

F32 = jnp.float32
BF16 = jnp.bfloat16

D_MODEL = 1024
DEPTH = 2
HEAD = 64
LANES = 128
CHUNK = 128
RMS_EPS = 1e-6
GN_EPS = 64e-5
VMEM_LIMIT = 56 * 1024 * 1024

N_IN = 9616
N_PAD = 9728
C_SB, C_Z, C_GATES, C_RW, C_LO, C_DT, C_XBC = 0, 2048, 3072, 6144, 8192, 8320, 8448
RW_COLS = 2176
XBC_COLS = 1280

ADAM_LR, ADAM_B1, ADAM_B2, ADAM_EPS, ADAM_WD, ADAM_STEP = 0.001, 0.9, 0.999, 1e-08, 0.01, 10


def _params(sem=None):
    return pltpu.CompilerParams(dimension_semantics=sem, vmem_limit_bytes=VMEM_LIMIT)


@jax.custom_vjp
def _sigmoid(x):
    return 1.0 / (1.0 + jnp.exp(-x))


def _sigmoid_fwd(x):
    s = _sigmoid(x)
    return s, s


def _sigmoid_bwd(s, g):
    return (g * s * (1.0 - s),)


_sigmoid.defvjp(_sigmoid_fwd, _sigmoid_bwd)


@jax.custom_vjp
def _silu(x):
    return x * _sigmoid(x)


def _silu_fwd(x):
    s = _sigmoid(x)
    return x * s, (x, s)


def _silu_bwd(res, g):
    x, s = res
    return (g * (s + x * s * (1.0 - s)),)


_silu.defvjp(_silu_fwd, _silu_bwd)


@jax.custom_vjp
def _softplus(x):
    return jnp.maximum(x, 0.0) + jnp.log(1.0 + jnp.exp(-jnp.abs(x)))


def _softplus_fwd(x):
    return _softplus(x), x


def _softplus_bwd(x, g):
    return (g * _sigmoid(x),)


_softplus.defvjp(_softplus_fwd, _softplus_bwd)


def _dot(a, b, dims):
    return lax.dot_general(a.astype(BF16), b.astype(BF16), (dims, ((), ())), preferred_element_type=F32)


def _dot_nn(a, b):
    return _dot(a, b, ((1,), (0,)))


def _dot_nt(a, b):
    return _dot(a, b, ((1,), (1,)))


def _dot_tn(a, b):
    return _dot(a, b, ((0,), (0,)))


@jax.custom_vjp
def _bdot(a, b):
    return _dot_nn(a, b)


def _bdot_fwd(a, b):
    return _dot_nn(a, b), (a, b)


def _bdot_bwd(res, g):
    a, b = res
    return _dot_nt(g, b), _dot_tn(a, g)


_bdot.defvjp(_bdot_fwd, _bdot_bwd)


def _split2(x):
    hi = x.astype(BF16)
    lo = (x - hi.astype(F32)).astype(BF16)
    return hi, lo


_NT = (((1,), (1,)), ((), ()))
_NN = (((1,), (0,)), ((), ()))
_TN = (((0,), (0,)), ((), ()))


def _dot2(x, m, dn=_NN):
    hi, lo = _split2(x)
    return (lax.dot_general(hi, m, dn, preferred_element_type=F32)
            + lax.dot_general(lo, m, dn, preferred_element_type=F32))


def _seg_matrix(n):
    r = lax.broadcasted_iota(jnp.int32, (n, n), 0) // HEAD
    c = lax.broadcasted_iota(jnp.int32, (n, n), 1) // HEAD
    return (r == c).astype(BF16)


@jax.custom_vjp
def _segsum2(x, seg):
    return _dot2(x, seg)


def _segsum2_fwd(x, seg):
    return _dot2(x, seg), seg


def _segsum2_bwd(seg, g):
    return _dot2(g, seg), jnp.zeros_like(seg)


_segsum2.defvjp(_segsum2_fwd, _segsum2_bwd)


def _make_segsum(seg):
    return lambda x: _segsum2(x, seg)


def _shift_down_raw(x, k):
    row = lax.broadcasted_iota(jnp.int32, x.shape, 0)
    return jnp.where(row >= k, pltpu.roll(x, k, 0), 0.0)


def _shift_up_raw(x, k):
    t = x.shape[0]
    row = lax.broadcasted_iota(jnp.int32, x.shape, 0)
    return jnp.where(row < t - k, pltpu.roll(x, t - k, 0), 0.0)


@functools.partial(jax.custom_vjp, nondiff_argnums=(1,))
def _shift_down(x, k):
    return _shift_down_raw(x, k)


def _shift_down_fwd(x, k):
    return _shift_down_raw(x, k), None


def _shift_down_bwd(k, _, g):
    return (_shift_up_raw(g, k),)


_shift_down.defvjp(_shift_down_fwd, _shift_down_bwd)


def _mm(a, b, *, name, ta=False, tb=False, add=None, out_dtype=F32, tm=2048, tn=512, tk=None, side=None):
    m, k = (a.shape[1], a.shape[0]) if ta else a.shape
    n = b.shape[0] if tb else b.shape[1]
    tm, tn = min(tm, m), min(tn, n)
    tk = k if tk is None else tk
    nk = k // tk
    assert m % tm == 0 and n % tn == 0 and k % tk == 0
    dims = ((0 if ta else 1,), (1 if tb else 0,))

    def body(a_ref, b_ref, *refs):
        o_ref, acc_ref = refs[-2:]
        p = _dot(a_ref[...], b_ref[...], dims)

        def emit(total):
            if add is not None:
                total = total + refs[0][...]
            o_ref[...] = total.astype(o_ref.dtype)

        if nk == 1:
            emit(p)
        else:
            kk = pl.program_id(2)

            @pl.when(kk == 0)
            def _():
                acc_ref[...] = p

            @pl.when(kk > 0)
            def _():
                acc_ref[...] += p

            @pl.when(kk == nk - 1)
            def _():
                emit(acc_ref[...])

    a_spec = pl.BlockSpec((tk, tm), lambda i, j, kk: (kk, i)) if ta else pl.BlockSpec((tm, tk), lambda i, j, kk: (i, kk))
    b_spec = pl.BlockSpec((tn, tk), lambda i, j, kk: (j, kk)) if tb else pl.BlockSpec((tk, tn), lambda i, j, kk: (kk, j))
    o_spec = pl.BlockSpec((tm, tn), lambda i, j, kk: (i, j))
    res = _call_with_side(
        body, side, name=name, grid=(m // tm, n // tn, nk), semantics=("parallel", "parallel", "arbitrary"),
        in_specs=[a_spec, b_spec] + ([o_spec] if add is not None else []), out_specs=[o_spec],
        out_shape=[jax.ShapeDtypeStruct((m, n), out_dtype)],
        scratch_shapes=[pltpu.VMEM((tm, tn) if nk > 1 else (8, LANES), F32)],
        operands=(a, b) + ((add,) if add is not None else ()))
    return res[0] if side is None else (res[0], res[1])


def _row_specs(rows, bt):
    return [pl.BlockSpec((bt, w), functools.partial(lambda i, c: (i, c), c=c)) for _, w, c in rows]


def _full_spec(p):
    return pl.BlockSpec(p.shape, functools.partial(lambda i, nd: (0,) * nd, nd=p.ndim))


def _rowwise(f, rows, pars, out_widths, *, bt, name, acc_widths=()):
    t = rows[0][0].shape[0]
    nr, npar, no, na = len(rows), len(pars), len(out_widths), len(acc_widths)

    def body(*refs):
        vals = [r[...] for r in refs[:nr + npar]]
        outs = f(*vals)
        for o_ref, o in zip(refs[nr + npar:nr + npar + no], outs[:no]):
            o_ref[...] = o.astype(o_ref.dtype)
        if na:
            first = pl.program_id(0) == 0
            for a_ref, a in zip(refs[nr + npar + no:], outs[no:]):
                @pl.when(first)
                def _():
                    a_ref[...] = jnp.zeros_like(a_ref)
                a_ref[...] += a

    return pl.pallas_call(
        body, name=name, grid=(t // bt,),
        in_specs=_row_specs(rows, bt) + [_full_spec(p) for p in pars],
        out_specs=[pl.BlockSpec((bt, w), lambda i: (i, 0)) for w in out_widths]
        + [pl.BlockSpec((1, w), lambda i: (0, 0)) for w in acc_widths],
        out_shape=[jax.ShapeDtypeStruct((t, w), F32) for w in out_widths]
        + [jax.ShapeDtypeStruct((1, w), F32) for w in acc_widths],
        compiler_params=_params(("arbitrary",)),
    )(*[r[0] for r in rows], *pars)


def _rowwise_bwd(f, rows, pars, douts, *, bt, name, groups=None):
    t = rows[0][0].shape[0]
    nr, npar, nd = len(rows), len(pars), len(douts)
    groups = [[i] for i in range(nr)] if groups is None else groups
    widths = [r[1] for r in rows]

    def body(*refs):
        vals = [r[...] for r in refs[:nr + npar]]
        cts = tuple(r[...] for r in refs[nr + npar:nr + npar + nd])
        _, vjp = jax.vjp(lambda *a: tuple(f(*a)), *vals)
        grads = vjp(cts)
        out_refs = refs[nr + npar + nd:]
        for g_ref, grp in zip(out_refs[:len(groups)], groups):
            off = 0
            for i in grp:
                g_ref[:, off:off + widths[i]] = grads[i]
                off += widths[i]
        first = pl.program_id(0) == 0
        for p_ref, g in zip(out_refs[len(groups):], grads[nr:]):
            @pl.when(first)
            def _():
                p_ref[...] = jnp.zeros_like(p_ref)
            p_ref[...] += g

    gw = [sum(widths[i] for i in grp) for grp in groups]
    return pl.pallas_call(
        body, name=name, grid=(t // bt,),
        in_specs=_row_specs(rows, bt) + [_full_spec(p) for p in pars] + _row_specs(douts, bt),
        out_specs=[pl.BlockSpec((bt, w), lambda i: (i, 0)) for w in gw] + [_full_spec(p) for p in pars],
        out_shape=[jax.ShapeDtypeStruct((t, w), F32) for w in gw] + [jax.ShapeDtypeStruct(p.shape, F32) for p in pars],
        compiler_params=_params(("arbitrary",)),
    )(*[r[0] for r in rows], *pars, *[d[0] for d in douts])


def _colwise(f, x, c0, ncols, pars, *, bc, name):
    t = x.shape[0]

    def body(x_ref, *refs):
        o_ref = refs[-1]
        o_ref[...] = f(x_ref[...], *[r[...] for r in refs[:-1]])

    return pl.pallas_call(
        body, name=name, grid=(ncols // bc,),
        in_specs=[pl.BlockSpec((t, bc), lambda j: (0, j + c0 // bc))]
        + [pl.BlockSpec((p.shape[0], bc), lambda j: (0, j)) for p in pars],
        out_specs=pl.BlockSpec((t, bc), lambda j: (0, j)),
        out_shape=jax.ShapeDtypeStruct((t, ncols), F32),
        compiler_params=_params(("parallel",)),
    )(x, *pars)


def _colwise_bwd(f, x, c0, ncols, pars, dout, *, bc, name):
    t = x.shape[0]
    npar = len(pars)

    def body(x_ref, *refs):
        vals = [x_ref[...]] + [r[...] for r in refs[:npar]]
        _, vjp = jax.vjp(f, *vals)
        grads = vjp(refs[npar][...])
        for g_ref, g in zip(refs[npar + 1:], grads):
            g_ref[...] = g

    return pl.pallas_call(
        body, name=name, grid=(ncols // bc,),
        in_specs=[pl.BlockSpec((t, bc), lambda j: (0, j + c0 // bc))]
        + [pl.BlockSpec((p.shape[0], bc), lambda j: (0, j)) for p in pars]
        + [pl.BlockSpec((t, bc), lambda j: (0, j))],
        out_specs=[pl.BlockSpec((t, bc), lambda j: (0, j))]
        + [pl.BlockSpec((p.shape[0], bc), lambda j: (0, j)) for p in pars],
        out_shape=[jax.ShapeDtypeStruct((t, ncols), F32)] + [jax.ShapeDtypeStruct(p.shape, F32) for p in pars],
        compiler_params=_params(("parallel",)),
    )(x, *pars, dout)


def _f_rms(x, g):
    return (x * lax.rsqrt(jnp.mean(x * x, axis=-1, keepdims=True) + RMS_EPS) * g,)


def _f_sb_gate(y, gate):
    return (y * _silu(gate),)


def _f_ssd_norm(y, z, g):
    u = y * _silu(z)
    return (u * lax.rsqrt(jnp.mean(u * u, axis=-1, keepdims=True) + RMS_EPS) * g,)


def _f_merge(p_sb, p_ssd, p_rw, g_sb, g_ssd, g_rw):
    return (_sigmoid(g_sb) * p_sb + _sigmoid(g_ssd) * p_ssd + _sigmoid(g_rw) * p_rw,)


def _f_rw_pre(k, lo, w0, w_up, a0, a_up, k_k, k_a):
    segsum = _make_segsum(_seg_matrix(k.shape[1]))
    lane = lax.broadcasted_iota(jnp.int32, lo.shape, 1)
    w_lo = jnp.where(lane < HEAD, jnp.tanh(lo), 0.0)
    a_lo = jnp.where(lane >= HEAD, lo, 0.0)
    w = -_softplus(-(w0 + _bdot(w_lo, w_up))) - 0.5
    log_decay = -jnp.exp(w)
    a = _sigmoid(a0 + _bdot(a_lo, a_up))
    kk = k * k_k
    kk = kk / jnp.maximum(jnp.sqrt(segsum(kk * kk)), 1e-12)
    return log_decay, k * (1.0 + (a - 1.0) * k_a), -kk, kk * a


def _f_rw_post(y, r, k2, v, gate, ln_g, ln_b, r_k):
    segsum = _make_segsum(_seg_matrix(y.shape[1]))
    yc = y - segsum(y) * (1.0 / HEAD)
    var = segsum(yc * yc) * (1.0 / HEAD)
    yn = yc * lax.rsqrt(var + GN_EPS) * ln_g + ln_b
    return ((yn + segsum(r * k2 * r_k) * v) * _silu(gate),)


def _f_rw_mix(slab, mu):
    return slab + (_shift_down(slab, 1) - slab) * mu


def _f_conv(x, w0, w1, w2, w3, b):
    acc = x * w3 + b
    for i, w in enumerate((w0, w1, w2)):
        acc = acc + _shift_down(x, 3 - i) * w
    return _silu(acc)


def _log_sigmoid(z):
    return jnp.minimum(z, 0.0) - jnp.log(1.0 + jnp.exp(-jnp.abs(z)))


SB_BQ = 256
SB_BK = 256
assert SB_BQ == SB_BK


def _tri_ones(kind):
    j = lax.broadcasted_iota(jnp.int32, (SB_BK, SB_BK + LANES), 0)
    s = lax.broadcasted_iota(jnp.int32, (SB_BK, SB_BK + LANES), 1)
    tri = {"gt": j > s, "le": j <= s, "lt": j < s}[kind]
    return (tri | (s >= SB_BK)).astype(BF16)


def _sb_common(q_ref):
    lane = lax.broadcasted_iota(jnp.int32, (SB_BQ, LANES), 1)
    q = q_ref[...] * (HEAD ** -0.5)
    q2 = jnp.concatenate([jnp.where(lane < HEAD, q, 0.0), jnp.where(lane >= HEAD, q, 0.0)], axis=0).astype(BF16)
    diff = (lax.broadcasted_iota(jnp.int32, (2 * SB_BQ, SB_BK), 1)
            - (lax.broadcasted_iota(jnp.int32, (2 * SB_BQ, SB_BK), 0) & (SB_BQ - 1)))
    return lane, q2, diff


def _rep(x):
    return jnp.concatenate([x] * (SB_BK // LANES), axis=1)


def _sb2_specs(t):
    q = pl.BlockSpec((SB_BQ, LANES), lambda j, i: (i, j))
    k = pl.BlockSpec((t, LANES), lambda j, i: (0, 4 + j))
    v = pl.BlockSpec((t, LANES), lambda j, i: (0, 8 + j))
    return q, k, v


def _sb2_fwd(proj, *, name):
    t = proj.shape[0]

    def body(q_ref, k_ref, v_ref, y_ref, lt_ref):
        i = pl.program_id(1)
        lane, q2, diff = _sb_common(q_ref)
        m_f = _tri_ones("gt")

        def step(kb, carry, diagonal):
            c, acc = carry
            off = pl.multiple_of(kb * SB_BK, SB_BK)
            kblk = k_ref[pl.ds(off, SB_BK), :].astype(BF16)
            vblk = v_ref[pl.ds(off, SB_BK), :].astype(BF16)
            z = lax.dot_general(q2, kblk, _NT, preferred_element_type=F32)
            lb = _log_sigmoid(z)
            lk = jnp.where(diff < 0, lb - z, 0.0) if diagonal else lb - z
            w2 = _dot2(lk, m_f)
            att = jnp.exp(lb + _rep(c) + w2[:, :SB_BK])
            if diagonal:
                att = jnp.where(diff < 0, att, 0.0)
            acc = acc + lax.dot_general(att.astype(BF16), vblk, _NN, preferred_element_type=F32)
            return c + w2[:, SB_BK:], acc

        zero = jnp.zeros((2 * SB_BQ, LANES), F32)
        c, acc = lax.fori_loop(0, i, lambda it, carry: step(i - 1 - it, carry, False), step(i, (zero, zero), True))
        y_ref[...] = jnp.where(lane < HEAD, acc[:SB_BQ], acc[SB_BQ:])
        lt_ref[0] = c[:SB_BQ]
        lt_ref[1] = c[SB_BQ:]

    return pl.pallas_call(
        body, name=name, grid=(4, t // SB_BQ),
        in_specs=list(_sb2_specs(t)),
        out_specs=[pl.BlockSpec((SB_BQ, LANES), lambda j, i: (i, j)),
                   pl.BlockSpec((2, SB_BQ, LANES), lambda j, i: (j, i, 0))],
        out_shape=[jax.ShapeDtypeStruct((t, 4 * LANES), F32), jax.ShapeDtypeStruct((8, t, LANES), F32)],
        compiler_params=_params(("parallel", "arbitrary")),
    )(proj, proj, proj)


def _sb2_bwd(proj, dy, lt, *, name):
    t = proj.shape[0]

    def body(q_ref, k_ref, v_ref, dy_ref, lt_ref, dq_ref, dk_ref, dv_ref):
        i = pl.program_id(1)

        @pl.when(i == 0)
        def _():
            dk_ref[...] = jnp.zeros_like(dk_ref)
            dv_ref[...] = jnp.zeros_like(dv_ref)

        lane, q2, diff = _sb_common(q_ref)
        m_le, m_lt = _tri_ones("le"), _tri_ones("lt")
        dy_blk = dy_ref[...]
        do2 = jnp.concatenate([jnp.where(lane < HEAD, dy_blk, 0.0), jnp.where(lane >= HEAD, dy_blk, 0.0)],
                              axis=0).astype(BF16)
        lt2 = jnp.concatenate([lt_ref[0], lt_ref[1]], axis=0)

        def step(kb, carry, diagonal):
            cp, cg, dq = carry
            off = pl.multiple_of(kb * SB_BK, SB_BK)
            kblk = k_ref[pl.ds(off, SB_BK), :].astype(BF16)
            vblk = v_ref[pl.ds(off, SB_BK), :].astype(BF16)
            z = lax.dot_general(q2, kblk, _NT, preferred_element_type=F32)
            lb = _log_sigmoid(z)
            lk = jnp.where(diff < 0, lb - z, 0.0) if diagonal else lb - z
            w2 = _dot2(lk, m_le)
            att = jnp.exp(lb + _rep(lt2 - cp) - w2[:, :SB_BK])
            if diagonal:
                att = jnp.where(diff < 0, att, 0.0)
            d_e = lax.dot_general(do2, vblk, _NT, preferred_element_type=F32) * att
            g2 = _dot2(d_e, m_lt)
            sig = jnp.exp(lb)
            dz = d_e * (1.0 - sig) - (_rep(cg) + g2[:, :SB_BK]) * sig
            dz = (jnp.where(diff < 0, dz, 0.0) if diagonal else dz).astype(BF16)
            dq = dq + lax.dot_general(dz, kblk, _NN, preferred_element_type=F32)
            dk_ref[pl.ds(off, SB_BK), :] += lax.dot_general(dz, q2, _TN, preferred_element_type=F32)
            dv_ref[pl.ds(off, SB_BK), :] += lax.dot_general(att.astype(BF16), do2, _TN, preferred_element_type=F32)
            return cp + w2[:, SB_BK:], cg + g2[:, SB_BK:], dq

        zero = jnp.zeros((2 * SB_BQ, LANES), F32)
        before = lax.fori_loop(0, i, lambda kb, carry: step(kb, carry, False), (zero, zero, zero))
        _, _, dq = step(i, before, True)
        dq_ref[...] = jnp.where(lane < HEAD, dq[:SB_BQ], dq[SB_BQ:]) * (HEAD ** -0.5)

    q_spec, k_spec, v_spec = _sb2_specs(t)
    blk = pl.BlockSpec((SB_BQ, LANES), lambda j, i: (i, j))
    col = pl.BlockSpec((t, LANES), lambda j, i: (0, j))
    return pl.pallas_call(
        body, name=name, grid=(4, t // SB_BQ),
        in_specs=[q_spec, k_spec, v_spec, blk, pl.BlockSpec((2, SB_BQ, LANES), lambda j, i: (j, i, 0))],
        out_specs=[blk, col, col],
        out_shape=[jax.ShapeDtypeStruct((t, 4 * LANES), F32)] * 3,
        compiler_params=_params(("parallel", "arbitrary")),
    )(proj, proj, proj, dy, lt)


SSD_HEADS = 16
SSD_PAIRS = 8


def _split3(x):
    a = x.astype(BF16)
    r = x - a.astype(F32)
    b = r.astype(BF16)
    return a, b, (r - b.astype(F32)).astype(BF16)


def _dot3(x, m, dn=_NN):
    return sum(lax.dot_general(p, m, dn, preferred_element_type=F32) for p in _split3(x))


def _mdot3(m, x):
    return sum(lax.dot_general(m, p, _NN, preferred_element_type=F32) for p in _split3(x))


def _ssd_common(dtr, dtb, alog, acsx_s, acst_s):
    lane = lax.broadcasted_iota(jnp.int32, (CHUNK, LANES), 1)
    lane1 = lax.broadcasted_iota(jnp.int32, (1, LANES), 1)
    arow = jnp.where(lane1 < SSD_HEADS, -jnp.exp(alog), 0.0)
    dt = jnp.where(lane < SSD_HEADS, _softplus(dtr + dtb), 0.0)
    da = dt * arow
    r = lax.broadcasted_iota(jnp.int32, (CHUNK, CHUNK), 0)
    c = lax.broadcasted_iota(jnp.int32, (CHUNK, CHUNK), 1)
    tril = (r >= c).astype(BF16)
    triu = (r <= c).astype(BF16)
    acs = _mdot3(tril, da)
    acst_s[...] = _dot3(da, triu, _TN)
    eh = lax.broadcasted_iota(jnp.int32, (LANES, 8 * LANES), 0)
    e = (eh == lax.broadcasted_iota(jnp.int32, (LANES, 8 * LANES), 1) // HEAD).astype(BF16)
    eh2 = lax.broadcasted_iota(jnp.int32, (LANES, 16 * LANES), 0)
    e2 = (eh2 == lax.broadcasted_iota(jnp.int32, (LANES, 16 * LANES), 1) // LANES).astype(BF16)
    acsx_s[...] = _dot3(acs, e)
    return dt, arow, _dot3(dt, e), _dot3(acs, e2), e, tril, triu


def _ssd_fwd(xc, proj, dtb, alog, dsk, *, name):
    t = xc.shape[0]
    nc = t // CHUNK

    def body(x_ref, b_ref, c_ref, dtr_ref, dtb_ref, alog_ref, dsk_ref, y_ref, hin_ref, acsx_s, acst_s, h_s):
        @pl.when(pl.program_id(0) == 0)
        def _():
            h_s[...] = jnp.zeros_like(h_s)

        dt, arow, dt_x, acs_b, e, tril, _ = _ssd_common(dtr_ref[...], dtb_ref[...], alog_ref[...], acsx_s, acst_s)
        dsk_x = _dot3(jnp.broadcast_to(dsk_ref[...], (CHUNK, LANES)), e)
        lane = lax.broadcasted_iota(jnp.int32, (CHUNK, LANES), 1)
        causal = (lax.broadcasted_iota(jnp.int32, (CHUNK, CHUNK), 0)
                  >= lax.broadcasted_iota(jnp.int32, (CHUNK, CHUNK), 1))
        for j in range(SSD_PAIRS):
            g = j // 4
            sl = slice(j * LANES, (j + 1) * LANES)
            if j % 4 == 0:
                bg = jnp.where(lane // HEAD == g, b_ref[...], 0.0)
                cg = jnp.where(lane // HEAD == g, c_ref[...], 0.0)
                cb = _dot_nt(cg, bg)
            x = x_ref[:, sl]
            a = acsx_s[:, sl]
            at = acsx_s[CHUNK - 1:CHUNK, sl]
            xdt = x * dt_x[:, sl]
            hin = h_s[j]
            hin_ref[0, j] = hin
            y = jnp.exp(a) * _dot_nn(cg, hin) + x * dsk_x[:, sl]
            h_s[j] = jnp.exp(at) * hin + _dot_tn(bg, xdt * jnp.exp(at - a))
            yd = []
            for hh in (0, 1):
                h = 2 * j + hh
                dec = jnp.exp(jnp.minimum(acs_b[:, h * LANES:(h + 1) * LANES] - acst_s[pl.ds(h, 1), :], 0.0))
                yd.append(_dot_nn(jnp.where(causal, cb * dec, 0.0), xdt))
            y_ref[:, sl] = y + jnp.where(lane < HEAD, yd[0], yd[1])

    one = pl.BlockSpec((1, LANES), lambda i: (0, 0))
    return pl.pallas_call(
        body, name=name, grid=(nc,),
        in_specs=[pl.BlockSpec((CHUNK, 8 * LANES), lambda i: (i, 0)),
                  pl.BlockSpec((CHUNK, LANES), lambda i: (i, 8)),
                  pl.BlockSpec((CHUNK, LANES), lambda i: (i, 9)),
                  pl.BlockSpec((CHUNK, LANES), lambda i: (i, C_DT // LANES)), one, one, one],
        out_specs=[pl.BlockSpec((CHUNK, 8 * LANES), lambda i: (i, 0)),
                   pl.BlockSpec((1, SSD_PAIRS, LANES, LANES), lambda i: (i, 0, 0, 0))],
        out_shape=[jax.ShapeDtypeStruct((t, 8 * LANES), F32),
                   jax.ShapeDtypeStruct((nc, SSD_PAIRS, LANES, LANES), F32)],
        scratch_shapes=[pltpu.VMEM((CHUNK, 8 * LANES), F32), pltpu.VMEM((LANES, CHUNK), F32),
                        pltpu.VMEM((SSD_PAIRS, LANES, LANES), F32)],
        compiler_params=_params(("arbitrary",)),
    )(xc, xc, xc, proj, dtb, alog, dsk)


def _ssd_bwd(xc, proj, dtb, alog, dsk, hin_all, dy, *, name):
    t = xc.shape[0]
    nc = t // CHUNK

    def body(x_ref, b_ref, c_ref, dtr_ref, dtb_ref, alog_ref, dsk_ref, hin_ref, dy_ref,
             dxc_ref, ddtr_ref, ddtb_ref, dalog_ref, ddsk_ref, acsx_s, acst_s, dh_s, dax_s, ddx_s):
        @pl.when(pl.program_id(0) == 0)
        def _():
            dh_s[...] = jnp.zeros_like(dh_s)
            ddtb_ref[...] = jnp.zeros_like(ddtb_ref)
            dalog_ref[...] = jnp.zeros_like(dalog_ref)
            ddsk_ref[...] = jnp.zeros_like(ddsk_ref)

        dtr = dtr_ref[...]
        dtb = dtb_ref[...]
        dt, arow, dt_x, acs_b, e, tril, triu = _ssd_common(dtr, dtb, alog_ref[...], acsx_s, acst_s)
        dsk_x = _dot3(jnp.broadcast_to(dsk_ref[...], (CHUNK, LANES)), e)
        lane = lax.broadcasted_iota(jnp.int32, (CHUNK, LANES), 1)
        rowi = lax.broadcasted_iota(jnp.int32, (CHUNK, LANES), 0)
        causal = (lax.broadcasted_iota(jnp.int32, (CHUNK, CHUNK), 0)
                  >= lax.broadcasted_iota(jnp.int32, (CHUNK, CHUNK), 1))
        acs_rows = jnp.zeros((CHUNK, LANES), F32)
        acs_cols = jnp.zeros((LANES, CHUNK), F32)
        d_b = jnp.zeros((CHUNK, LANES), F32)
        d_c = jnp.zeros((CHUNK, LANES), F32)
        for j in range(SSD_PAIRS):
            g = j // 4
            sl = slice(j * LANES, (j + 1) * LANES)
            if j % 4 == 0:
                bg = jnp.where(lane // HEAD == g, b_ref[...], 0.0)
                cg = jnp.where(lane // HEAD == g, c_ref[...], 0.0)
                cb = _dot_nt(cg, bg)
                dcb = jnp.zeros((CHUNK, CHUNK), F32)
            x = x_ref[:, sl]
            d = dt_x[:, sl]
            a = acsx_s[:, sl]
            at = acsx_s[CHUNK - 1:CHUNK, sl]
            xdt = x * d
            hin = hin_ref[0, j]
            dhout = dh_s[j]
            dyp = dy_ref[:, sl]
            ea, eat, ed = jnp.exp(a), jnp.exp(at), jnp.exp(at - a)
            da_l = dyp * ea * _dot_nn(cg, hin)
            dm = dyp * ea
            d_c = d_c + _dot_nt(dm, hin)
            dh_s[j] = _dot_tn(cg, dm) + eat * dhout
            dat = jnp.sum(dhout * hin * eat, axis=0, keepdims=True)
            d_b = d_b + _dot_nt(xdt * ed, dhout)
            dw = _dot_nn(bg, dhout)
            dxdt = dw * ed
            ded = dw * xdt * ed
            dat = dat + jnp.sum(ded, axis=0, keepdims=True)
            da_l = da_l - ded
            for hh in (0, 1):
                h = 2 * j + hh
                dec = jnp.exp(jnp.minimum(acs_b[:, h * LANES:(h + 1) * LANES] - acst_s[pl.ds(h, 1), :], 0.0))
                gm = jnp.where(causal, cb * dec, 0.0)
                dyh = jnp.where(lane // HEAD == hh, dyp, 0.0)
                dg = _dot_nt(dyh, xdt)
                dxdt = dxdt + _dot_tn(gm, dyh)
                dcb = dcb + jnp.where(causal, dg * dec, 0.0)
                th = dg * gm
                acs_rows = acs_rows + jnp.where(lane == h, jnp.sum(th, axis=1, keepdims=True), 0.0)
                acs_cols = acs_cols + jnp.where(rowi == h, jnp.sum(th, axis=0, keepdims=True), 0.0)
            if j % 4 == 3:
                d_c = d_c + _dot_nn(dcb, bg)
                d_b = d_b + _dot_tn(dcb, cg)
            dxc_ref[:, sl] = dyp * dsk_x[:, sl] + dxdt * d
            ddx_s[:, sl] = dxdt * x
            dax_s[:, sl] = da_l + jnp.where(rowi == CHUNK - 1, dat, 0.0)
            dskp = jnp.sum(dyp * x, axis=0, keepdims=True)
            ddsk_ref[...] += _dot2(jnp.broadcast_to(dskp, (8, LANES)), e[:, sl], _NT)
        dxc_ref[:, 8 * LANES:9 * LANES] = d_b
        dxc_ref[:, 9 * LANES:10 * LANES] = d_c
        dacs = acs_rows - acs_cols.T + _dot2(dax_s[...], e, _NT)
        ddt = _dot2(ddx_s[...], e, _NT)
        dda = _mdot3(triu, dacs)
        ddt = ddt + dda * arow
        dalog_ref[...] += jnp.sum(dda * dt, axis=0, keepdims=True) * arow
        ddtr = jnp.where(lane < SSD_HEADS, ddt * _sigmoid(dtr + dtb), 0.0)
        ddtr_ref[...] = ddtr
        ddtb_ref[...] += jnp.sum(ddtr, axis=0, keepdims=True)

    one = pl.BlockSpec((1, LANES), lambda i: (0, 0))
    rev = lambda c: (lambda i: (nc - 1 - i, c))
    return pl.pallas_call(
        body, name=name, grid=(nc,),
        in_specs=[pl.BlockSpec((CHUNK, 8 * LANES), rev(0)), pl.BlockSpec((CHUNK, LANES), rev(8)),
                  pl.BlockSpec((CHUNK, LANES), rev(9)), pl.BlockSpec((CHUNK, LANES), rev(C_DT // LANES)),
                  one, one, one,
                  pl.BlockSpec((1, SSD_PAIRS, LANES, LANES), lambda i: (nc - 1 - i, 0, 0, 0)),
                  pl.BlockSpec((CHUNK, 8 * LANES), rev(0))],
        out_specs=[pl.BlockSpec((CHUNK, XBC_COLS), rev(0)), pl.BlockSpec((CHUNK, LANES), rev(0)), one, one,
                   pl.BlockSpec((8, LANES), lambda i: (0, 0))],
        out_shape=[jax.ShapeDtypeStruct((t, XBC_COLS), F32), jax.ShapeDtypeStruct((t, LANES), F32)]
        + [jax.ShapeDtypeStruct((1, LANES), F32)] * 2 + [jax.ShapeDtypeStruct((8, LANES), F32)],
        scratch_shapes=[pltpu.VMEM((CHUNK, 8 * LANES), F32), pltpu.VMEM((LANES, CHUNK), F32),
                        pltpu.VMEM((SSD_PAIRS, LANES, LANES), F32),
                        pltpu.VMEM((CHUNK, 8 * LANES), F32), pltpu.VMEM((CHUNK, 8 * LANES), F32)],
        compiler_params=_params(("arbitrary",)),
    )(xc, xc, xc, proj, dtb, alog, dsk, hin_all, dy)


RW_C = 64


def _p3(a, b, dn):
    ah, al = _split2(a)
    bh, bl = _split2(b)
    d = lambda x, y: lax.dot_general(x, y, dn, preferred_element_type=F32)
    return d(ah, bh) + d(ah, bl) + d(al, bh)


_BNN = (((2,), (1,)), ((0,), (0,)))
_BNT = (((2,), (2,)), ((0,), (0,)))
_BTN = (((1,), (1,)), ((0,), (0,)))


@jax.custom_vjp
def _pnn(a, b):
    return _p3(a, b, _BNN)


@jax.custom_vjp
def _pnt(a, b):
    return _p3(a, b, _BNT)


@jax.custom_vjp
def _ptn(a, b):
    return _p3(a, b, _BTN)


_pnn.defvjp(lambda a, b: (_p3(a, b, _BNN), (a, b)), lambda res, g: (_p3(g, res[1], _BNT), _p3(res[0], g, _BTN)))
_pnt.defvjp(lambda a, b: (_p3(a, b, _BNT), (a, b)), lambda res, g: (_p3(g, res[1], _BNN), _p3(g, res[0], _BTN)))
_ptn.defvjp(lambda a, b: (_p3(a, b, _BTN), (a, b)), lambda res, g: (_p3(res[1], g, _BNT), _p3(res[0], g, _BNN)))


def _tri2(tril, x, dn):
    hi, lo = _split2(x)
    m = tril.astype(BF16)
    return (lax.dot_general(m, hi, dn, preferred_element_type=F32) + lax.dot_general(m, lo, dn, preferred_element_type=F32))


@jax.custom_vjp
def _cumsum_rows(tril, x):
    return _tri2(tril, x, _BNN)


_cumsum_rows.defvjp(lambda tril, x: (_tri2(tril, x, _BNN), tril),
                    lambda tril, g: (jnp.zeros_like(tril), _tri2(tril, g, _BTN)))


def _rw_chunk_consts():
    c2 = 2 * RW_C
    row = lax.broadcasted_iota(jnp.int32, (c2, c2), 0)
    col = lax.broadcasted_iota(jnp.int32, (c2, c2), 1)
    same = (row // RW_C) == (col // RW_C)
    strict = (same & (row > col)).astype(F32)
    incl = (same & (row >= col)).astype(F32)
    eye = (row == col).astype(F32)
    tr = lax.broadcasted_iota(jnp.int32, (RW_C, RW_C), 0)
    tc = lax.broadcasted_iota(jnp.int32, (RW_C, RW_C), 1)
    tril = (tr >= tc).astype(F32)
    lane = lax.broadcasted_iota(jnp.int32, (1, LANES), 1)
    hm = [(lane // HEAD == h).astype(F32) for h in (0, 1)]
    return strict, incl, eye, tril, hm


def _rw_chunk(r, lw, k, v, n, b, s2, consts):
    strict, incl, eye, tril, hm = consts
    two = lambda x: jnp.concatenate([x * hm[0], x * hm[1]], axis=1)
    cum = _cumsum_rows(jnp.broadcast_to(tril, (4, RW_C, RW_C)), lw)
    grow, shrink = jnp.exp(-cum), jnp.exp(cum)
    n2, r2 = two(n * jnp.exp(cum - lw)), two(r * shrink)
    b2, k2, v2 = two(b * grow), two(k * grow), two(v)
    p = _pnt(n2, b2) * strict
    x2 = _pnt(n2, s2) + _pnn(_pnt(n2, k2) * strict, v2)
    t_inv, a = eye + p, p
    for _ in range(RW_C.bit_length() - 2):
        a = _pnn(a, a)
        t_inv = t_inv + _pnn(t_inv, a)
    u2 = _pnn(t_inv, x2)
    y2 = _pnt(r2, s2) + _pnn(_pnt(r2, b2) * incl, u2) + _pnn(_pnt(r2, k2) * incl, v2)
    s2_new = (s2 + _ptn(u2, b2) + _ptn(v2, k2)) * jnp.exp(jnp.sum(lw, axis=1, keepdims=True))
    return jnp.sum(y2.reshape(4, 2, RW_C, LANES), axis=1), s2_new


def _pairs(ref):
    return jnp.stack([ref[:, p * LANES:(p + 1) * LANES] for p in range(4)])


def _rw_chunk_fwd(mixed, lw, k, n, b, *, name, side=None):
    t = lw.shape[0]
    nc = t // RW_C

    def body(r_ref, v_ref, lw_ref, k_ref, n_ref, b_ref, y_ref, sin_ref, s_s):
        @pl.when(pl.program_id(0) == 0)
        def _():
            s_s[...] = jnp.zeros_like(s_s)

        s2 = s_s[...]
        sin_ref[0] = s2
        y, s2 = _rw_chunk(*[_pairs(x) for x in (r_ref, lw_ref, k_ref, v_ref, n_ref, b_ref)], s2, _rw_chunk_consts())
        for p in range(4):
            y_ref[:, p * LANES:(p + 1) * LANES] = y[p]
        s_s[...] = s2

    blk = lambda c: pl.BlockSpec((RW_C, 4 * LANES), functools.partial(lambda i, c: (i, c), c=c))
    return _call_with_side(
        body, side, name=name, grid=(nc,), semantics=("arbitrary",),
        in_specs=[blk(0), blk(2), blk(0), blk(0), blk(0), blk(0)],
        out_specs=[blk(0), pl.BlockSpec((1, 4, LANES, LANES), lambda i: (i, 0, 0, 0))],
        out_shape=[jax.ShapeDtypeStruct((t, 4 * LANES), F32), jax.ShapeDtypeStruct((nc, 4, LANES, LANES), F32)],
        scratch_shapes=[pltpu.VMEM((4, LANES, LANES), F32)],
        operands=(mixed, mixed, lw, k, n, b))


def _call_with_side(body, side, *, name, grid, semantics, in_specs, out_specs, out_shape, scratch_shapes, operands):
    if side is None:
        return pl.pallas_call(body, name=name, grid=grid, in_specs=in_specs, out_specs=out_specs, out_shape=out_shape,
                              scratch_shapes=scratch_shapes, compiler_params=_params(semantics))(*operands)
    srcs, per_dest = side
    ns, ni, no, nscr = len(srcs), len(in_specs), len(out_specs), len(scratch_shapes)

    def full_body(*refs):
        ins, side_in = refs[:ni], refs[ni:ni + ns]
        outs, side_out = refs[ni + ns:ni + ns + no], refs[ni + ns + no:ni + 2 * ns + no]
        scratch, sems = refs[ni + 2 * ns + no:ni + 2 * ns + no + nscr], refs[ni + 2 * ns + no + nscr:]

        ids = [pl.program_id(a) for a in range(len(grid))]
        first = functools.reduce(jnp.logical_and, [i == 0 for i in ids])
        last = functools.reduce(jnp.logical_and, [i == n - 1 for i, n in zip(ids, grid)])

        @pl.when(first)
        def _():
            _exchange(side_in, side_out, sems, per_dest, start=True, wait=False)

        body(*ins, *outs, *scratch)

        @pl.when(last)
        def _():
            _exchange(side_in, side_out, sems, per_dest, start=False, wait=True)

    res = pl.pallas_call(
        full_body, name=name, grid=grid, in_specs=list(in_specs) + [_ANY] * ns,
        out_specs=list(out_specs) + [_ANY] * ns, out_shape=list(out_shape) + _exchange_out_shapes(srcs, per_dest),
        scratch_shapes=list(scratch_shapes) + _exchange_sems(ns), compiler_params=_params(("arbitrary",) * len(grid)),
    )(*operands, *srcs)
    return list(res[:no]) + [list(res[no:])]


def _rw_chunk_bwd(mixed, lw, k, n, b, s_in, dy, dr0, dk0, dv0, *, name, side=None):
    t = lw.shape[0]
    nc = t // RW_C

    def body(r_ref, v_ref, lw_ref, k_ref, n_ref, b_ref, sin_ref, dy_ref, dr0_ref, dk0_ref, dv0_ref,
             dr_ref, dlw_ref, dk_ref, dv_ref, dn_ref, db_ref, ds_s):
        @pl.when(pl.program_id(0) == 0)
        def _():
            ds_s[...] = jnp.zeros_like(ds_s)

        consts = _rw_chunk_consts()
        args = [_pairs(x) for x in (r_ref, lw_ref, k_ref, v_ref, n_ref, b_ref)] + [sin_ref[0]]
        _, vjp = jax.vjp(lambda *a: _rw_chunk(*a, consts), *args)
        dr, dlw, dk, dv, dn, db, ds = vjp((_pairs(dy_ref), ds_s[...]))
        for p in range(4):
            sl = slice(p * LANES, (p + 1) * LANES)
            dr_ref[:, sl] = dr[p] + dr0_ref[:, sl]
            dlw_ref[:, sl] = dlw[p]
            dk_ref[:, sl] = dk[p] + dk0_ref[:, sl]
            dv_ref[:, sl] = dv[p] + dv0_ref[:, sl]
            dn_ref[:, sl] = dn[p]
            db_ref[:, sl] = db[p]
        ds_s[...] = ds

    blk = lambda c: pl.BlockSpec((RW_C, 4 * LANES), functools.partial(lambda i, c: (nc - 1 - i, c), c=c))
    return _call_with_side(
        body, side, name=name, grid=(nc,), semantics=("arbitrary",),
        in_specs=[blk(0), blk(2), blk(0), blk(0), blk(0), blk(0),
                  pl.BlockSpec((1, 4, LANES, LANES), lambda i: (nc - 1 - i, 0, 0, 0)), blk(0), blk(0), blk(0), blk(0)],
        out_specs=[blk(0)] * 6,
        out_shape=[jax.ShapeDtypeStruct((t, 4 * LANES), F32)] * 6,
        scratch_shapes=[pltpu.VMEM((4, LANES, LANES), F32)],
        operands=(mixed, mixed, lw, k, n, b, s_in, dy, dr0, dk0, dv0))


def _f_rms_res(x, g):
    return _f_rms(x, g)[0], x


def _final(x, g, target, *, bt, name):
    t, d = x.shape

    def body(x_ref, g_ref, t_ref, dx_ref, loss_ref, dg_ref):
        tgt = t_ref[...]

        def f(xv, gv):
            err = _f_rms(xv, gv)[0] - tgt
            return 0.5 * jnp.mean(err * err, axis=-1, keepdims=True)

        row_loss, vjp = jax.vjp(f, x_ref[...], g_ref[...])
        dx, dg = vjp(jnp.ones_like(row_loss))
        dx_ref[...] = dx

        @pl.when(pl.program_id(0) == 0)
        def _():
            loss_ref[...] = jnp.zeros_like(loss_ref)
            dg_ref[...] = jnp.zeros_like(dg_ref)

        loss_ref[...] += jnp.broadcast_to(jnp.sum(row_loss, axis=0, keepdims=True), (1, LANES))
        dg_ref[...] += dg

    blk = pl.BlockSpec((bt, d), lambda i: (i, 0))
    return pl.pallas_call(
        body, name=name, grid=(t // bt,),
        in_specs=[blk, pl.BlockSpec((1, d), lambda i: (0, 0)), blk],
        out_specs=[blk, pl.BlockSpec((1, LANES), lambda i: (0, 0)), pl.BlockSpec((1, d), lambda i: (0, 0))],
        out_shape=[jax.ShapeDtypeStruct((t, d), F32), jax.ShapeDtypeStruct((1, LANES), F32),
                   jax.ShapeDtypeStruct((1, d), F32)],
        compiler_params=_params(("arbitrary",)),
    )(x, g, target)


ADAMW_BLOCK_BYTES = 1 << 20


def _adamw(w, g, m, v, *, name, block=None):
    shape = w.shape
    if block is not None:
        return _adamw_blocks(w, g, m, v, block, name)
    c = shape[-1]
    shape3 = (1,) * (3 - len(shape)) + shape if len(shape) <= 3 else (-1,) + shape[-2:]
    args = [a.reshape(shape3) for a in (w, g, m, v)]
    lead, r, _ = args[0].shape
    br = r
    if r * c * 4 > ADAMW_BLOCK_BYTES:
        cands = [b for b in range(8, r, 8) if r % b == 0 and b * c * 4 <= ADAMW_BLOCK_BYTES]
        br = max(cands) if cands else r
    outs = _adamw_blocks(*args, (1, br, c), name)
    return tuple(o.reshape(shape) for o in outs)


def _adamw_blocks(w, g, m, v, block, name):
    shape = w.shape
    assert all(s % b == 0 for s, b in zip(shape, block))

    def body(w_ref, g_ref, m_ref, v_ref, d_ref, nm_ref, nv_ref):
        gv = g_ref[...]
        m_new = ADAM_B1 * m_ref[...] + (1.0 - ADAM_B1) * gv
        v_new = ADAM_B2 * v_ref[...] + (1.0 - ADAM_B2) * (gv * gv)
        m_hat = m_new / (1.0 - ADAM_B1 ** ADAM_STEP)
        v_hat = v_new / (1.0 - ADAM_B2 ** ADAM_STEP)
        d_ref[...] = -ADAM_LR * (m_hat / (jnp.sqrt(v_hat) + ADAM_EPS) + ADAM_WD * w_ref[...])
        nm_ref[...] = m_new
        nv_ref[...] = v_new

    blk = pl.BlockSpec(tuple(block), lambda *ids: ids)
    return pl.pallas_call(
        body, name=name, grid=tuple(s // b for s, b in zip(shape, block)), in_specs=[blk] * 4, out_specs=[blk] * 3,
        out_shape=[jax.ShapeDtypeStruct(shape, F32)] * 3,
        compiler_params=_params(("parallel",) * len(shape)),
    )(w, g, m, v)


BT = 256
BC = 128


def _layer_rows(x, proj, s):
    s = {k: s.get(k) for k in ("y_sb_raw", "y_ssd_raw", "mixed", "ys", "k2", "p_sb", "p_ssd", "p_rw")}
    return dict(
        rms=[(x, D_MODEL, 0)],
        sb_gate=[(s["y_sb_raw"], 512, 0), (proj, 512, 3)],
        ssd_norm=[(s["y_ssd_raw"], 1024, 0), (proj, 1024, C_Z // 1024)],
        rw_pre=[(s["mixed"], 512, 1), (s["mixed"], LANES, 16)],
        rw_post=[(s["ys"], 512, 0), (s["mixed"], 512, 0), (s["k2"], 512, 0), (s["mixed"], 512, 2), (s["mixed"], 512, 3)],
        merge=[(s["p_sb"], 1024, 0), (s["p_ssd"], 1024, 0), (s["p_rw"], 1024, 0),
               (proj, 1024, 3), (proj, 1024, 4), (proj, 1024, 5)],
    )


def _layer_fwd(x, p, nm, side=None):
    s = {}
    (s["h"],) = _rowwise(_f_rms, [(x, D_MODEL, 0)], [p["norm_g"]], [D_MODEL], bt=BT, name=nm + "rms")
    proj = s["proj"] = _mm(s["h"], p["w_in"], name=nm + "proj")
    s["y_sb_raw"], s["lt"] = _sb2_fwd(proj, name=nm + "sb")
    s["xc"] = _colwise(_f_conv, proj, C_XBC, XBC_COLS, p["conv"], bc=BC, name=nm + "conv")
    s["y_ssd_raw"], s["hin"] = _ssd_fwd(s["xc"], proj, p["dt_bias"], p["a_log"], p["d_skip"], name=nm + "ssd")
    s["mixed"] = _colwise(_f_rw_mix, proj, C_RW, RW_COLS, [p["rw_mu"]], bc=BC, name=nm + "mix")
    s["w"], s["k2"], s["n"], s["b"] = _rowwise(_f_rw_pre, [(s["mixed"], 512, 1), (s["mixed"], LANES, 16)], p["rw_pre"],
                                               [512] * 4, bt=BT, name=nm + "rwpre")
    s["ys"], s["st"], *exchanged = _rw_chunk_fwd(s["mixed"], s["w"], s["k2"], s["n"], s["b"], name=nm + "scan", side=side)
    rows = _layer_rows(x, proj, s)
    (s["y_sb"],) = _rowwise(_f_sb_gate, rows["sb_gate"], [], [512], bt=BT, name=nm + "sbgate")
    (s["y_ssd"],) = _rowwise(_f_ssd_norm, rows["ssd_norm"], [p["ssd_norm_g"]], [1024], bt=BT, name=nm + "ssdnorm")
    (s["y_rw"],) = _rowwise(_f_rw_post, rows["rw_post"], p["rw_post"], [512], bt=BT, name=nm + "rwpost")
    s["p_sb"] = _mm(s["y_sb"], p["w_out_sb"], name=nm + "osb")
    s["p_ssd"] = _mm(s["y_ssd"], p["w_out_ssd"], name=nm + "ossd")
    s["p_rw"] = _mm(s["y_rw"], p["w_out_rw"], name=nm + "orw")
    (s["merged"],) = _rowwise(_f_merge, _layer_rows(x, proj, s)["merge"], [], [1024], bt=BT, name=nm + "merge")
    return _mm(s["merged"], p["w_o"], add=x, name=nm + "wo"), s, (exchanged[0] if exchanged else None)


def _layer_bwd(x, dx_out, p, s, nm, side=None, side_late=None):
    g = {}
    proj = s["proj"]
    rows = _layer_rows(x, proj, s)
    g["w_o"] = _mm(s["merged"], dx_out, ta=True, name=nm + "g_wo")
    d_merged = _mm(dx_out, p["w_o"], tb=True, name=nm + "d_merged")
    dp_sb, dp_ssd, dp_rw, d_gates = _rowwise_bwd(_f_merge, rows["merge"], [], [(d_merged, 1024, 0)], bt=BT,
                                                 name=nm + "merge_b", groups=[[0], [1], [2], [3, 4, 5]])
    g["w_out_sb"] = _mm(s["y_sb"], dp_sb, ta=True, name=nm + "g_osb")
    g["w_out_ssd"] = _mm(s["y_ssd"], dp_ssd, ta=True, name=nm + "g_ossd")
    g["w_out_rw"] = _mm(s["y_rw"], dp_rw, ta=True, name=nm + "g_orw")
    dy_sb = _mm(dp_sb, p["w_out_sb"], tb=True, name=nm + "d_ysb")
    dy_ssd = _mm(dp_ssd, p["w_out_ssd"], tb=True, name=nm + "d_yssd")
    dy_rw = _mm(dp_rw, p["w_out_rw"], tb=True, name=nm + "d_yrw")
    dy_sb_raw, d_sbgate = _rowwise_bwd(_f_sb_gate, rows["sb_gate"], [], [(dy_sb, 512, 0)], bt=BT, name=nm + "sbgate_b")
    dq, dk, dv = _sb2_bwd(proj, dy_sb_raw, s["lt"], name=nm + "sb_b")
    dy_ssd_raw, dz, g["ssd_norm_g"] = _rowwise_bwd(_f_ssd_norm, rows["ssd_norm"], [p["ssd_norm_g"]],
                                                   [(dy_ssd, 1024, 0)], bt=BT, name=nm + "ssdnorm_b")
    dxc, ddtr, g["dt_bias"], g["a_log"], g["d_skip"] = _ssd_bwd(
        s["xc"], proj, p["dt_bias"], p["a_log"], p["d_skip"], s["hin"], dy_ssd_raw, name=nm + "ssd_b")
    conv_out = _colwise_bwd(_f_conv, proj, C_XBC, XBC_COLS, p["conv"], dxc, bc=BC, name=nm + "conv_b")
    dxbc, g["conv"] = conv_out[0], conv_out[1:]
    dys, dr0, dk0, dv0, d_rwgate, g["rw_ln_g"], g["rw_ln_b"], g["rw_r_k"] = _rowwise_bwd(
        _f_rw_post, rows["rw_post"], p["rw_post"], [(dy_rw, 512, 0)], bt=BT, name=nm + "rwpost_b")
    dr, dw, dk2, dvv, dn, db, *exchanged = _rw_chunk_bwd(s["mixed"], s["w"], s["k2"], s["n"], s["b"], s["st"], dys,
                                                         dr0, dk0, dv0, name=nm + "scan_b",
                                                         side=side(g) if side else None)
    pre_out = _rowwise_bwd(_f_rw_pre, rows["rw_pre"], p["rw_pre"],
                           [(dw, 512, 0), (dk2, 512, 0), (dn, 512, 0), (db, 512, 0)], bt=BT, name=nm + "rwpre_b")
    dkm, dlo, g["rw_pre"] = pre_out[0], pre_out[1], pre_out[2:]
    d_mixed = jnp.concatenate([dr, dkm, dvv, d_rwgate, dlo], axis=1)
    d_slab, g["rw_mu"] = _colwise_bwd(_f_rw_mix, proj, C_RW, RW_COLS, [p["rw_mu"]], d_mixed, bc=BC, name=nm + "mix_b")
    d_proj = jnp.concatenate([dq, dk, dv, d_sbgate, dz, d_gates, d_slab, ddtr, dxbc], axis=1)
    g["w_in"] = _mm(s["h"], d_proj, ta=True, name=nm + "g_win")
    dh = _mm(d_proj, p["w_in"], tb=True, tn=1024, tk=512, name=nm + "d_h", side=side_late(g) if side_late else None)
    dh, late = dh if side_late else (dh, None)
    dx, g["norm_g"] = _rowwise_bwd(_f_rms_res, rows["rms"], [p["norm_g"]], [(dh, D_MODEL, 0), (dx_out, D_MODEL, 0)],
                                   bt=BT, name=nm + "rms_b")
    return dx, g, (exchanged[0] if exchanged else None), late


MESH = pl.DeviceIdType.MESH
N_DEV = 8
_ANY = pl.BlockSpec(memory_space=pl.ANY)


def _here():
    x, y, c = lax.axis_index("x"), lax.axis_index("y"), lax.axis_index("c")
    return x, y, c, [(1 - x, y), (x, 1 - y), (1 - x, 1 - y)]


def _chip_exchange(srcs, *, per_dest, name):
    n = len(srcs)

    def body(*refs):
        _exchange(refs[:n], refs[n:2 * n], refs[2 * n:], per_dest, start=True, wait=True)

    return pl.pallas_call(
        body, name=name, in_specs=[_ANY] * n, out_specs=[_ANY] * n,
        out_shape=_exchange_out_shapes(srcs, per_dest), scratch_shapes=_exchange_sems(n),
    )(*srcs)


def _by_layer(per_dest):
    return per_dest is not True and per_dest is not False


def _exchange_out_shapes(srcs, per_dest):
    lead = (4, 2) if _by_layer(per_dest) else (4,)
    return [jax.ShapeDtypeStruct(lead + s.shape[-2:], s.dtype) for s in srcs]


def _exchange_sems(n):
    return [pltpu.SemaphoreType.DMA((3 * n,)), pltpu.SemaphoreType.DMA((3 * n,)), pltpu.SemaphoreType.DMA((n,))]


def _exchange(src_refs, out_refs, sems, per_dest, *, start, wait):
    send_sems, recv_sems, local_sems = sems
    x, y, c, chips = _here()
    me = 2 * x + y
    owns, sends, recvs = [], [], []
    for a, (src_ref, out_ref) in enumerate(zip(src_refs, out_refs)):
        if per_dest is True:
            pick = lambda q, s=src_ref: s.at[q]
        elif per_dest is False:
            pick = lambda q, s=src_ref: s.at[c]
        else:
            pick = lambda q, s=src_ref: s.at[per_dest].at[c]
        any_block = src_ref.at[0] if len(src_ref.shape) == 3 else src_ref.at[0].at[0]
        if _by_layer(per_dest):
            slot = lambda q, o=out_ref: o.at[q].at[c]
        else:
            slot = lambda q, o=out_ref: o.at[q]
        owns.append(pltpu.make_async_copy(pick(me), slot(me), local_sems.at[a]))
        for j, (px, py) in enumerate(chips):
            sends.append(pltpu.make_async_remote_copy(
                pick(2 * px + py), slot(me), send_sems.at[3 * a + j], recv_sems.at[3 * a + j],
                device_id=(px, py, c), device_id_type=MESH))
            recvs.append(pltpu.make_async_remote_copy(
                any_block, slot(2 * px + py), send_sems.at[3 * a + j], recv_sems.at[3 * a + j],
                device_id=(px, py, c), device_id_type=MESH))
    if start:
        for cp in owns + sends:
            cp.start()
    if wait:
        for cp in recvs:
            cp.wait_recv()
        for cp in sends:
            cp.wait_send()
        for cp in owns:
            cp.wait()


def _sibling_fill(bufs, *, name):
    n = len(bufs)

    def body(*refs):
        in_refs, out_refs, send_sems, recv_sems = refs[:n], refs[n:2 * n], refs[2 * n], refs[2 * n + 1]
        x, y, c, _ = _here()
        copies = []
        for a, (src, dst) in enumerate(zip(in_refs, out_refs)):
            for q in range(4):
                copies.append(pltpu.make_async_remote_copy(
                    src.at[q].at[c], dst.at[q].at[c], send_sems.at[4 * a + q], recv_sems.at[4 * a + q],
                    device_id=(x, y, 1 - c), device_id_type=MESH))
        for cp in copies:
            cp.start()
        for a, (src, dst) in enumerate(zip(in_refs, out_refs)):
            for q in range(4):
                pltpu.make_async_remote_copy(
                    src.at[q].at[c], dst.at[q].at[1 - c], send_sems.at[4 * a + q], recv_sems.at[4 * a + q],
                    device_id=(x, y, 1 - c), device_id_type=MESH).wait_recv()
        for cp in copies:
            cp.wait_send()

    return pl.pallas_call(
        body, name=name, in_specs=[_ANY] * n, out_specs=[_ANY] * n,
        out_shape=[jax.ShapeDtypeStruct(b.shape, b.dtype) for b in bufs],
        input_output_aliases={a: a for a in range(n)},
        scratch_shapes=[pltpu.SemaphoreType.DMA((4 * n,)), pltpu.SemaphoreType.DMA((4 * n,))],
    )(*bufs)


def _sibling_swap(srcs, *, other_slot, name):
    n = len(srcs)

    def body(*refs):
        src_refs, out_refs, send_sems, recv_sems = refs[:n], refs[n:2 * n], refs[2 * n], refs[2 * n + 1]
        x, y, c, _ = _here()
        copies = [pltpu.make_async_remote_copy(s.at[1 - c] if other_slot else s, o, send_sems.at[a], recv_sems.at[a],
                                               device_id=(x, y, 1 - c), device_id_type=MESH)
                  for a, (s, o) in enumerate(zip(src_refs, out_refs))]
        for cp in copies:
            cp.start()
        for cp in copies:
            cp.wait()

    return pl.pallas_call(
        body, name=name, in_specs=[_ANY] * n, out_specs=[_ANY] * n,
        out_shape=[jax.ShapeDtypeStruct(s.shape[1:] if other_slot else s.shape, s.dtype) for s in srcs],
        scratch_shapes=[pltpu.SemaphoreType.DMA((n,)), pltpu.SemaphoreType.DMA((n,))],
    )(*srcs)


def _allgather_small(v, *, reduce, name):
    r = v.shape[0]

    def body(v_ref, out_ref, *rest):
        send_sems, recv_sems, local_sem = rest[-3:]
        x, y, c, chips = _here()
        me, sibling = (x, y, c), (x, y, 1 - c)

        def slot(px, py, pc):
            return out_ref.at[4 * px + 2 * py + pc]

        def copy(k, block, to, src=None):
            return pltpu.make_async_remote_copy(
                src_ref=slot(*block) if src is None else src, dst_ref=slot(*block),
                send_sem=send_sems.at[k], recv_sem=recv_sems.at[k], device_id=to, device_id_type=MESH)

        mine = pltpu.make_async_copy(v_ref, slot(*me), local_sem)
        mine.start()
        first = [copy(0, me, sibling, src=v_ref)]
        first += [copy(1 + j, me, (*chip, c), src=v_ref) for j, chip in enumerate(chips)]
        for cp in first:
            cp.start()
        passed = [copy(4 + j, (*chip, c), sibling) for j, chip in enumerate(chips)]
        for j, chip in enumerate(chips):
            copy(1 + j, (*chip, c), me).wait_recv()
            passed[j].start()
        copy(0, sibling, me).wait_recv()
        for j, chip in enumerate(chips):
            copy(4 + j, (*chip, 1 - c), me).wait_recv()
        for cp in first + passed:
            cp.wait_send()
        mine.wait()
        if reduce:
            total = out_ref[0]
            for d in range(1, N_DEV):
                total = total + out_ref[d]
            rest[0][...] = total

    vm = pl.BlockSpec(memory_space=pltpu.VMEM)
    out_shape = [jax.ShapeDtypeStruct((N_DEV, r, LANES), F32)] + ([jax.ShapeDtypeStruct((r, LANES), F32)] if reduce else [])
    return pl.pallas_call(
        body, name=name, in_specs=[vm], out_specs=[vm] * len(out_shape), out_shape=out_shape,
        scratch_shapes=[pltpu.SemaphoreType.DMA((7,)), pltpu.SemaphoreType.DMA((7,)), pltpu.SemaphoreType.DMA],
        compiler_params=pltpu.CompilerParams(vmem_limit_bytes=VMEM_LIMIT),
    )(v)


REDUCE_BLOCK_BYTES = 2 << 20


def _reduce_rows(r, c):
    cands = [b for b in range(16, r + 1, 16) if r % b == 0 and b * c * 4 <= REDUCE_BLOCK_BYTES]
    return max(cands)


def _add_halves(mine2, other, c_idx, *, name):
    _, nq, r, c = mine2.shape
    br = _reduce_rows(r, c)

    def body(c_ref, a_ref, b_ref, o_ref):
        o_ref[...] = (a_ref[0] + b_ref[...]).astype(o_ref.dtype)

    blk = pl.BlockSpec((1, br, c), lambda q, i, c_ref: (q, i, 0))
    return pl.pallas_call(
        body, name=name,
        grid_spec=pltpu.PrefetchScalarGridSpec(
            num_scalar_prefetch=1, grid=(nq, r // br),
            in_specs=[pl.BlockSpec((1, 1, br, c), lambda q, i, c_ref: (c_ref[0], q, i, 0)), blk],
            out_specs=blk),
        out_shape=jax.ShapeDtypeStruct((nq, r, c), BF16),
        compiler_params=_params(("parallel", "parallel")),
    )(c_idx, mine2, other)


def _sum_chips(parts, c_idx, *, name):
    _, r, c = parts.shape
    br = _reduce_rows(r, c)

    def body(c_ref, p_ref, o_ref):
        total = p_ref[0].astype(F32)
        for q in range(1, 4):
            total = total + p_ref[q].astype(F32)
        o_ref[0] = total

    return pl.pallas_call(
        body, name=name,
        grid_spec=pltpu.PrefetchScalarGridSpec(
            num_scalar_prefetch=1, grid=(r // br,),
            in_specs=[pl.BlockSpec((4, br, c), lambda i, c_ref: (0, i, 0))],
            out_specs=pl.BlockSpec((1, br, c), lambda i, c_ref: (c_ref[0], i, 0))),
        out_shape=jax.ShapeDtypeStruct((2, r, c), F32),
        compiler_params=_params(("parallel",)),
    )(c_idx, parts)


def _sibling_fill_halves(bufs, *, name):
    n = len(bufs)

    def body(*refs):
        in_refs, out_refs, send_sems, recv_sems = refs[:n], refs[n:2 * n], refs[2 * n], refs[2 * n + 1]
        x, y, c, _ = _here()
        copies = [pltpu.make_async_remote_copy(src.at[c], dst.at[c], send_sems.at[a], recv_sems.at[a],
                                               device_id=(x, y, 1 - c), device_id_type=MESH)
                  for a, (src, dst) in enumerate(zip(in_refs, out_refs))]
        for cp in copies:
            cp.start()
        for a, (src, dst) in enumerate(zip(in_refs, out_refs)):
            pltpu.make_async_remote_copy(src.at[c], dst.at[1 - c], send_sems.at[a], recv_sems.at[a],
                                         device_id=(x, y, 1 - c), device_id_type=MESH).wait_recv()
        for cp in copies:
            cp.wait_send()

    return pl.pallas_call(
        body, name=name, in_specs=[_ANY] * n, out_specs=[_ANY] * n,
        out_shape=[jax.ShapeDtypeStruct(b.shape, b.dtype) for b in bufs],
        input_output_aliases={a: a for a in range(n)},
        scratch_shapes=[pltpu.SemaphoreType.DMA((n,)), pltpu.SemaphoreType.DMA((n,))],
    )(*bufs)


BIG = ("w_in", "w_out_sb", "w_out_ssd", "w_out_rw", "w_o")
BIG_AXIS = {"w_in": 2, "w_out_sb": 2, "w_out_ssd": 1, "w_out_rw": 2, "w_o": 1}
SMALL_SHARDED = {"conv_w": 320, "rw_w_up": 128, "rw_a_up": 128}
SMALL = ("norm_g", "conv_w", "conv_b", "dt_bias", "a_log", "d_skip", "ssd_norm_g", "rw_mu", "rw_w0", "rw_w_up",
         "rw_a0", "rw_a_up", "rw_k_k", "rw_k_a", "rw_r_k", "rw_ln_g", "rw_ln_b", "final_g")


def _rows_of(a):
    flat = a.reshape(-1)
    pad = (-flat.shape[0]) % LANES
    return jnp.pad(flat, (0, pad)).reshape(-1, LANES)


def _pack_rows(arrays, multiple=8):
    rows = jnp.concatenate([_rows_of(a) for a in arrays], axis=0)
    pad = (-rows.shape[0]) % multiple
    return jnp.pad(rows, ((0, pad), (0, 0)))


def _unpack_rows(rows, shapes):
    out, off = [], 0
    for shp in shapes:
        n = 1
        for d in shp:
            n *= d
        nr = -(-n // LANES)
        out.append(rows[off:off + nr].reshape(-1)[:n].reshape(shp))
        off += nr
    return out


COL_MAP = ((0, 3072, 0), (3072, 4352, C_XBC), (4352, 4368, C_DT), (4368, 6544, C_RW), (6544, 9616, C_GATES))
SHARD_COLS = N_IN // 4


def _w_in_from_shards(shards):
    pieces = []
    for a, b, dst in sorted(COL_MAP, key=lambda m: m[2]):
        if pieces and dst > pieces[-1][0]:
            pieces.append((dst, jnp.zeros((shards[0].shape[0], dst - pieces[-1][0]), shards[0].dtype)))
        for q in range(4):
            lo, hi = max(a, q * SHARD_COLS), min(b, (q + 1) * SHARD_COLS)
            if lo < hi:
                pieces.append((dst + hi - a, shards[q][:, lo - q * SHARD_COLS:hi - q * SHARD_COLS]))
    return jnp.concatenate([p for _, p in pieces], axis=1)


def _w_in_shard(g, q):
    pieces = []
    for a, b, dst in COL_MAP:
        lo, hi = max(a, q * SHARD_COLS), min(b, (q + 1) * SHARD_COLS)
        if lo < hi:
            pieces.append(g[:, dst + lo - a:dst + hi - a])
    return jnp.concatenate(pieces, axis=1)


def _row_halves(a):
    return a.reshape(2, a.shape[0] // 2, a.shape[1])


def kernel(x, norm_g, w_in, conv_w, conv_b, dt_bias, a_log, d_skip, ssd_norm_g, rw_mu, rw_w0, rw_w_up, rw_a0, rw_a_up, rw_k_k, rw_k_a, rw_r_k, rw_ln_g, rw_ln_b, w_out_sb, w_out_ssd, w_out_rw, w_o, final_g, loss_target, m_norm_g, m_w_in, m_conv_w, m_conv_b, m_dt_bias, m_a_log, m_d_skip, m_ssd_norm_g, m_rw_mu, m_rw_w0, m_rw_w_up, m_rw_a0, m_rw_a_up, m_rw_k_k, m_rw_k_a, m_rw_r_k, m_rw_ln_g, m_rw_ln_b, m_w_out_sb, m_w_out_ssd, m_w_out_rw, m_w_o, m_final_g, v_norm_g, v_w_in, v_conv_w, v_conv_b, v_dt_bias, v_a_log, v_d_skip, v_ssd_norm_g, v_rw_mu, v_rw_w0, v_rw_w_up, v_rw_a0, v_rw_a_up, v_rw_k_k, v_rw_k_a, v_rw_r_k, v_rw_ln_g, v_rw_ln_b, v_w_out_sb, v_w_out_ssd, v_w_out_rw, v_w_o, v_final_g):
    names = ("norm_g", "w_in", "conv_w", "conv_b", "dt_bias", "a_log", "d_skip", "ssd_norm_g", "rw_mu", "rw_w0",
             "rw_w_up", "rw_a0", "rw_a_up", "rw_k_k", "rw_k_a", "rw_r_k", "rw_ln_g", "rw_ln_b", "w_out_sb",
             "w_out_ssd", "w_out_rw", "w_o", "final_g")
    w_loc = dict(zip(names, (norm_g, w_in, conv_w, conv_b, dt_bias, a_log, d_skip, ssd_norm_g, rw_mu, rw_w0, rw_w_up,
                             rw_a0, rw_a_up, rw_k_k, rw_k_a, rw_r_k, rw_ln_g, rw_ln_b, w_out_sb, w_out_ssd, w_out_rw,
                             w_o, final_g)))
    m_loc = dict(zip(names, (m_norm_g, m_w_in, m_conv_w, m_conv_b, m_dt_bias, m_a_log, m_d_skip, m_ssd_norm_g,
                             m_rw_mu, m_rw_w0, m_rw_w_up, m_rw_a0, m_rw_a_up, m_rw_k_k, m_rw_k_a, m_rw_r_k,
                             m_rw_ln_g, m_rw_ln_b, m_w_out_sb, m_w_out_ssd, m_w_out_rw, m_w_o, m_final_g)))
    v_loc = dict(zip(names, (v_norm_g, v_w_in, v_conv_w, v_conv_b, v_dt_bias, v_a_log, v_d_skip, v_ssd_norm_g,
                             v_rw_mu, v_rw_w0, v_rw_w_up, v_rw_a0, v_rw_a_up, v_rw_k_k, v_rw_k_a, v_rw_r_k,
                             v_rw_ln_g, v_rw_ln_b, v_w_out_sb, v_w_out_ssd, v_w_out_rw, v_w_o, v_final_g)))
    chip = 2 * lax.axis_index("x") + lax.axis_index("y")
    core = lax.axis_index("c")

    as_sent = [w_loc[n].astype(BF16).reshape(DEPTH, 2, w_loc[n].shape[1] // 2, w_loc[n].shape[2]) for n in BIG]

    def gathered(mine, nm):
        out = {}
        for n, buf in zip(BIG, _sibling_fill(mine, name=nm)):
            shards = buf.reshape(4, 2 * buf.shape[2], buf.shape[3])
            out[n] = (_w_in_from_shards([shards[q] for q in range(4)]) if n == "w_in"
                      else jnp.concatenate([shards[q] for q in range(4)], axis=BIG_AXIS[n] - 1))
        return out

    full = {}
    sm_names = tuple(SMALL_SHARDED)
    sm_shapes = [w_loc[n].shape for n in sm_names]
    (got_sm,) = _allgather_small(_pack_rows([w_loc[n] for n in sm_names]), reduce=False, name="gather_small")
    per_chip = [_unpack_rows(got_sm[4 * (q // 2) + 2 * (q % 2)], sm_shapes) for q in range(4)]
    for i, n in enumerate(sm_names):
        full[n] = jnp.concatenate([per_chip[q][i] for q in range(4)], axis=-1)

    def pad16(a):
        return jnp.zeros((1, LANES), F32).at[0, :SSD_HEADS].set(a)

    def layer_params(i, big):
        row = lambda n: w_loc[n][i].reshape(1, -1)
        cw = full["conv_w"][i]
        return dict(
            norm_g=row("norm_g"), w_in=big["w_in"], conv=[cw[k][None] for k in range(4)] + [row("conv_b")],
            dt_bias=pad16(dt_bias[i]), a_log=pad16(a_log[i]), d_skip=pad16(d_skip[i]),
            ssd_norm_g=row("ssd_norm_g"), rw_mu=row("rw_mu"),
            rw_pre=[row("rw_w0"), jnp.zeros((LANES, 512), F32).at[:HEAD].set(full["rw_w_up"][i]), row("rw_a0"),
                    jnp.zeros((LANES, 512), F32).at[HEAD:].set(full["rw_a_up"][i]), row("rw_k_k"), row("rw_k_a")],
            rw_post=[row("rw_ln_g"), row("rw_ln_b"), row("rw_r_k")],
            w_out_sb=big["w_out_sb"], w_out_ssd=big["w_out_ssd"], w_out_rw=big["w_out_rw"], w_o=big["w_o"])

    c_idx = core.reshape(1).astype(jnp.int32)

    def reduce_prepare(items, nm):
        sends = []
        for g, n, _ in items:
            per_chip = ([_w_in_shard(g[n], q) for q in range(4)] if n == "w_in"
                        else jnp.split(g[n], 4, axis=BIG_AXIS[n] - 1))
            sends.append(jnp.stack([_row_halves(p) for p in per_chip], axis=1))
        others = _sibling_swap([s.astype(BF16) for s in sends], other_slot=True, name=nm + "sibling")
        return [_add_halves(s, o, c_idx, name=nm + "add_" + lab) for (_, _, lab), s, o in zip(items, sends, others)]

    def reduce_finish(exchanged, labels, nm):
        halves = [_sum_chips(p, c_idx, name=nm + "sum_" + lab) for lab, p in zip(labels, exchanged)]
        whole = _sibling_fill_halves(halves, name=nm + "join")
        return {lab: b.reshape(2 * b.shape[1], b.shape[2]) for lab, b in zip(labels, whole)}

    assert DEPTH == 2
    out_proj = BIG[1:]
    params, xs, saved, grads = [None] * 2, [x[0], None, None], [None] * 2, [None] * 2
    params[0] = layer_params(0, gathered(_chip_exchange(as_sent, per_dest=0, name="gather_l0"), "gather_l0_join"))
    xs[1], saved[0], got = _layer_fwd(xs[0], params[0], "l0_", side=(as_sent, 1))
    params[1] = layer_params(1, gathered(got, "gather_l1_join"))
    xs[2], saved[1], _ = _layer_fwd(xs[1], params[1], "l1_")
    dx, loss_row, g_final = _final(xs[2], final_g.reshape(1, -1), loss_target[0], bt=BT, name="final")
    dx, grads[1], _, _ = _layer_bwd(xs[1], dx, params[1], saved[1], "l1_")
    early = lambda g: [(grads[1], n, "l1_" + n) for n in BIG] + [(g, n, "l0_" + n) for n in out_proj]
    dx, grads[0], got, got_late = _layer_bwd(
        xs[0], dx, params[0], saved[0], "l0_",
        side=lambda g: (reduce_prepare(early(g), "reduce_early_"), True),
        side_late=lambda g: (reduce_prepare([(g, "w_in", "l0_w_in")], "reduce_late_"), True))
    total = reduce_finish(got + got_late, [lab for _, _, lab in early(None)] + ["l0_w_in"], "reduce_")
    totals = [{n: total[f"l{i}_" + n] for n in BIG} for i in range(DEPTH)]

    def stacked(fn):
        return jnp.stack([fn(grads[i]) for i in range(DEPTH)])

    g_loc = {
        "norm_g": stacked(lambda g: g["norm_g"][0]),
        "conv_w": stacked(lambda g: jnp.concatenate(g["conv"][:4], axis=0)),
        "conv_b": stacked(lambda g: g["conv"][4][0]),
        "dt_bias": stacked(lambda g: g["dt_bias"][0, :SSD_HEADS]),
        "a_log": stacked(lambda g: g["a_log"][0, :SSD_HEADS]),
        "d_skip": stacked(lambda g: g["d_skip"][0, :SSD_HEADS]),
        "ssd_norm_g": stacked(lambda g: g["ssd_norm_g"][0]),
        "rw_mu": stacked(lambda g: g["rw_mu"][0]),
        "rw_w0": stacked(lambda g: g["rw_pre"][0][0]),
        "rw_w_up": stacked(lambda g: g["rw_pre"][1][:HEAD]),
        "rw_a0": stacked(lambda g: g["rw_pre"][2][0]),
        "rw_a_up": stacked(lambda g: g["rw_pre"][3][HEAD:]),
        "rw_k_k": stacked(lambda g: g["rw_pre"][4][0]),
        "rw_k_a": stacked(lambda g: g["rw_pre"][5][0]),
        "rw_r_k": stacked(lambda g: g["rw_r_k"].reshape(8, HEAD)),
        "rw_ln_g": stacked(lambda g: g["rw_ln_g"][0]),
        "rw_ln_b": stacked(lambda g: g["rw_ln_b"][0]),
        "final_g": g_final[0],
    }

    g_out = {n: jnp.stack([totals[0][n], totals[1][n]]) for n in BIG}

    sm_all = SMALL + ("loss",)
    sm_full_shapes = [g_loc[n].shape for n in SMALL] + [(1,)]
    _, summed = _allgather_small(_pack_rows([g_loc[n] for n in SMALL] + [loss_row[0, :1]]), reduce=True, name="reduce_small")
    sm = dict(zip(sm_all, _unpack_rows(summed, sm_full_shapes)))
    for n in SMALL:
        g_out[n] = sm[n]
    for n, wd in SMALL_SHARDED.items():
        g_out[n] = lax.dynamic_slice_in_dim(sm[n], chip * wd, wd, axis=sm[n].ndim - 1)
    loss = sm["loss"][0]

    upd = {n: _adamw(w_loc[n], g_out[n], m_loc[n], v_loc[n], name="adamw_" + n) for n in names if n != "w_in"}
    cols = SHARD_COLS // 4
    to_cols = lambda a: jnp.transpose(a, (2, 0, 1)).reshape(4, cols, DEPTH, D_MODEL)
    from_cols = lambda a: jnp.transpose(a.reshape(SHARD_COLS, DEPTH, D_MODEL), (1, 2, 0))
    g_cols = lax.optimization_barrier(to_cols(g_out["w_in"]))
    g_out["w_in"] = from_cols(g_cols)
    upd["w_in"] = tuple(from_cols(a) for a in _adamw(
        to_cols(w_loc["w_in"]), g_cols, to_cols(m_loc["w_in"]), to_cols(v_loc["w_in"]),
        name="adamw_w_in", block=(1, cols, DEPTH, D_MODEL // 2)))
    return (loss, dx[None], *[g_out[n] for n in names], *[upd[n][0] for n in names],
            *[upd[n][1] for n in names], *[upd[n][2] for n in names])
```
